```python
import jax, jax.numpy as jnp
from jax import lax
import numpy as np

D_MODEL = 1024
BATCH = 8
SEQ = 8192
DEPTH = 1

N_META = 16
HEAD_DIM = 64
ATTN_HEADS = (D_MODEL // 2) // HEAD_DIM
KV_HEADS = 2
GQA_GROUP = ATTN_HEADS // KV_HEADS
ATTN_WIDTH = ATTN_HEADS * HEAD_DIM
KV_WIDTH = KV_HEADS * HEAD_DIM
LRU_WIDTH = D_MODEL // 2
LRU_BLOCKS = 8
LRU_BLOCK = LRU_WIDTH // LRU_BLOCKS
LRU_C = 8.0
CONV_WIDTH = 4
WINDOW = 128
BLOCK = 128
PAD = BLOCK - N_META
MIX_WIDTH = ATTN_WIDTH + LRU_WIDTH
IN_WIDTH = ATTN_WIDTH + 2 * KV_WIDTH + 2 * LRU_WIDTH
D_FF = 4 * D_MODEL
EPS = 1e-6
NEG = -1e30

kernel_name = "hymba_griffin_swa_sink_hybrid"


def rmsnorm(x, g):
    xf = x.astype(jnp.float32)
    y = xf * lax.rsqrt(jnp.mean(xf * xf, axis=-1, keepdims=True) + EPS)
    return (y * g.astype(jnp.float32)).astype(x.dtype)


def causal_depthwise_conv(x, w, b):
    c = x.shape[-1]
    y = lax.conv_general_dilated(
        x, w[:, None, :].astype(x.dtype), window_strides=(1,),
        padding=[(CONV_WIDTH - 1, 0)],
        dimension_numbers=("NWC", "WIO", "NWC"), feature_group_count=c)
    return y + b.astype(x.dtype)


def rg_lru(x, w_a, b_a, w_x, b_x, lam):
    bsz, t, _ = x.shape
    xb = x.reshape(bsz, t, LRU_BLOCKS, LRU_BLOCK)
    gate_r = jnp.einsum("btnc,ncd->btnd", xb, w_a).reshape(bsz, t, LRU_WIDTH) + b_a
    gate_i = jnp.einsum("btnc,ncd->btnd", xb, w_x).reshape(bsz, t, LRU_WIDTH) + b_x
    r = jax.nn.sigmoid(gate_r.astype(jnp.float32))
    i = jax.nn.sigmoid(gate_i.astype(jnp.float32))
    log_a = -LRU_C * r * jax.nn.softplus(-lam.astype(jnp.float32))
    a = jnp.exp(log_a)
    mult = jnp.sqrt(-jnp.expm1(2.0 * log_a))
    u = mult * (i * x.astype(jnp.float32))

    def combine(left, right):
        a_l, b_l = left
        a_r, b_r = right
        return a_l * a_r, a_r * b_l + b_r

    _, h = lax.associative_scan(combine, (a, u), axis=1)
    return h


def sliding_window_attention_with_sinks(q, k, v, sinks):
    bsz, t, _, _ = q.shape
    pad_cfg = ((0, 0), (PAD, 0), (0, 0), (0, 0))
    qp, kp, vp = jnp.pad(q, pad_cfg), jnp.pad(k, pad_cfg), jnp.pad(v, pad_cfg)
    tp = t + PAD
    nb = tp // BLOCK
    qb = qp.reshape(bsz, nb, BLOCK, KV_HEADS, GQA_GROUP, HEAD_DIM)
    kb = kp.reshape(bsz, nb, BLOCK, KV_HEADS, HEAD_DIM)
    vb = vp.reshape(bsz, nb, BLOCK, KV_HEADS, HEAD_DIM)
    blk_pad = ((0, 0), (1, 0), (0, 0), (0, 0), (0, 0))
    kk = jnp.concatenate([jnp.pad(kb, blk_pad)[:, :-1], kb], axis=2)
    vv = jnp.concatenate([jnp.pad(vb, blk_pad)[:, :-1], vb], axis=2)
    scale = HEAD_DIM ** -0.5
    s = jnp.einsum("bnqkgd,bnskd->bnkgqs", qb, kk,
                   preferred_element_type=jnp.float32) * scale
    blk = jnp.arange(nb)[:, None] * BLOCK
    qpos = blk + jnp.arange(BLOCK)[None, :]
    kpos = blk - BLOCK + jnp.arange(2 * BLOCK)[None, :]
    diff = qpos[:, :, None] - kpos[:, None, :]
    valid = (diff >= 0) & (diff < WINDOW) & (kpos[:, None, :] >= PAD)
    s = jnp.where(valid[None, :, None, None], s, NEG)
    sink = sinks.astype(jnp.float32).reshape(1, 1, KV_HEADS, GQA_GROUP, 1, 1)
    m = jnp.maximum(jnp.max(s, axis=-1, keepdims=True), sink)
    p = jnp.exp(s - m)
    denom = jnp.sum(p, axis=-1, keepdims=True) + jnp.exp(sink - m)
    o = jnp.einsum("bnkgqs,bnskd->bnqkgd", (p / denom).astype(v.dtype), vv)
    return o.reshape(bsz, tp, ATTN_WIDTH)[:, PAD:]


def _fwd_setup_inputs(seed: int = 0) -> dict:
    key = jax.random.key(seed)
    ks = jax.random.split(key, 20)
    f32 = jnp.float32

    def nrm(k, shape, scale):
        return jax.random.normal(k, shape, f32) * scale

    u = jax.random.uniform(ks[10], (DEPTH, LRU_WIDTH), f32, 0.9, 0.999)
    a0 = u ** (1.0 / LRU_C)
    lru_lambda = jnp.log(a0) - jnp.log1p(-a0)
    return {
        "x": nrm(ks[0], (BATCH, SEQ, D_MODEL), 1.0),
        "meta_tokens": nrm(ks[1], (N_META, D_MODEL), 1.0),
        "g_pre_mix": 1.0 + nrm(ks[2], (DEPTH, D_MODEL), 0.05),
        "w_in": nrm(ks[3], (DEPTH, D_MODEL, IN_WIDTH), D_MODEL ** -0.5),
        "conv_w": nrm(ks[4], (DEPTH, CONV_WIDTH, LRU_WIDTH), CONV_WIDTH ** -0.5),
        "conv_b": nrm(ks[5], (DEPTH, LRU_WIDTH), 0.01),
        "w_a": nrm(ks[6], (DEPTH, LRU_BLOCKS, LRU_BLOCK, LRU_BLOCK), LRU_BLOCK ** -0.5),
        "b_a": nrm(ks[7], (DEPTH, LRU_WIDTH), 0.01),
        "w_x": nrm(ks[8], (DEPTH, LRU_BLOCKS, LRU_BLOCK, LRU_BLOCK), LRU_BLOCK ** -0.5),
        "b_x": nrm(ks[9], (DEPTH, LRU_WIDTH), 0.01),
        "lru_lambda": lru_lambda,
        "attn_sinks": nrm(ks[11], (DEPTH, ATTN_HEADS), 0.5),
        "w_out": nrm(ks[12], (DEPTH, MIX_WIDTH, D_MODEL), MIX_WIDTH ** -0.5),
        "g_post_mix": 1.0 + nrm(ks[13], (DEPTH, D_MODEL), 0.05),
        "g_pre_ffn": 1.0 + nrm(ks[14], (DEPTH, D_MODEL), 0.05),
        "w_ff1": nrm(ks[15], (DEPTH, D_MODEL, D_FF), D_MODEL ** -0.5),
        "w_ff2": nrm(ks[16], (DEPTH, D_FF, D_MODEL), D_FF ** -0.5),
        "g_post_ffn": 1.0 + nrm(ks[17], (DEPTH, D_MODEL), 0.05),
    }


def _fwd_reference(x, meta_tokens, g_pre_mix, w_in, conv_w, conv_b, w_a, b_a, w_x, b_x,
              lru_lambda, attn_sinks, w_out, g_post_mix, g_pre_ffn, w_ff1, w_ff2,
              g_post_ffn):
    bsz = x.shape[0]
    meta = jnp.broadcast_to(meta_tokens[None].astype(x.dtype), (bsz, N_META, D_MODEL))
    h = jnp.concatenate([meta, x], axis=1)
    t = h.shape[1]
    split_at = [ATTN_WIDTH, ATTN_WIDTH + KV_WIDTH, ATTN_WIDTH + 2 * KV_WIDTH,
                ATTN_WIDTH + 2 * KV_WIDTH + LRU_WIDTH]
    for l in range(DEPTH):
        u = rmsnorm(h, g_pre_mix[l])
        z = u @ w_in[l]
        q, k, v, xr, yr = jnp.split(z, split_at, axis=-1)
        attn = sliding_window_attention_with_sinks(
            q.reshape(bsz, t, ATTN_HEADS, HEAD_DIM),
            k.reshape(bsz, t, KV_HEADS, HEAD_DIM),
            v.reshape(bsz, t, KV_HEADS, HEAD_DIM),
            attn_sinks[l])
        xr = causal_depthwise_conv(xr, conv_w[l], conv_b[l])
        hr = rg_lru(xr, w_a[l], b_a[l], w_x[l], b_x[l], lru_lambda[l])
        rec = (jax.nn.gelu(yr.astype(jnp.float32)) * hr).astype(h.dtype)
        mix = jnp.concatenate([attn.astype(h.dtype), rec], axis=-1) @ w_out[l]
        h = h + rmsnorm(mix, g_post_mix[l])
        u = rmsnorm(h, g_pre_ffn[l])
        f = jnp.square(jax.nn.relu(u @ w_ff1[l])) @ w_ff2[l]
        h = h + rmsnorm(f, g_post_ffn[l])
    return h[:, N_META:]


import jax as _jax
import jax.numpy as _jnp

TWIN_FORMAT = 'train_step'
FWD_PARAMS = ['x', 'meta_tokens', 'g_pre_mix', 'w_in', 'conv_w', 'conv_b', 'w_a', 'b_a', 'w_x', 'b_x', 'lru_lambda', 'attn_sinks', 'w_out', 'g_post_mix', 'g_pre_ffn', 'w_ff1', 'w_ff2', 'g_post_ffn']
TWIN_WEIGHTS = ['meta_tokens', 'g_pre_mix', 'w_in', 'conv_w', 'conv_b', 'w_a', 'b_a', 'w_x', 'b_x', 'lru_lambda', 'attn_sinks', 'w_out', 'g_post_mix', 'g_pre_ffn', 'w_ff1', 'w_ff2', 'g_post_ffn']
TWIN_DIFF_INPUT = 'x'
TWIN_INPUTS = ['x', 'meta_tokens', 'g_pre_mix', 'w_in', 'conv_w', 'conv_b', 'w_a', 'b_a', 'w_x', 'b_x', 'lru_lambda', 'attn_sinks', 'w_out', 'g_post_mix', 'g_pre_ffn', 'w_ff1', 'w_ff2', 'g_post_ffn', 'loss_target', 'm_meta_tokens', 'm_g_pre_mix', 'm_w_in', 'm_conv_w', 'm_conv_b', 'm_w_a', 'm_b_a', 'm_w_x', 'm_b_x', 'm_lru_lambda', 'm_attn_sinks', 'm_w_out', 'm_g_post_mix', 'm_g_pre_ffn', 'm_w_ff1', 'm_w_ff2', 'm_g_post_ffn', 'v_meta_tokens', 'v_g_pre_mix', 'v_w_in', 'v_conv_w', 'v_conv_b', 'v_w_a', 'v_b_a', 'v_w_x', 'v_b_x', 'v_lru_lambda', 'v_attn_sinks', 'v_w_out', 'v_g_post_mix', 'v_g_pre_ffn', 'v_w_ff1', 'v_w_ff2', 'v_g_post_ffn']
TWIN_OUTPUTS = ['loss', 'grad_x', 'grad_meta_tokens', 'grad_g_pre_mix', 'grad_w_in', 'grad_conv_w', 'grad_conv_b', 'grad_w_a', 'grad_b_a', 'grad_w_x', 'grad_b_x', 'grad_lru_lambda', 'grad_attn_sinks', 'grad_w_out', 'grad_g_post_mix', 'grad_g_pre_ffn', 'grad_w_ff1', 'grad_w_ff2', 'grad_g_post_ffn', 'delta_meta_tokens', 'delta_g_pre_mix', 'delta_w_in', 'delta_conv_w', 'delta_conv_b', 'delta_w_a', 'delta_b_a', 'delta_w_x', 'delta_b_x', 'delta_lru_lambda', 'delta_attn_sinks', 'delta_w_out', 'delta_g_post_mix', 'delta_g_pre_ffn', 'delta_w_ff1', 'delta_w_ff2', 'delta_g_post_ffn', 'new_m_meta_tokens', 'new_m_g_pre_mix', 'new_m_w_in', 'new_m_conv_w', 'new_m_conv_b', 'new_m_w_a', 'new_m_b_a', 'new_m_w_x', 'new_m_b_x', 'new_m_lru_lambda', 'new_m_attn_sinks', 'new_m_w_out', 'new_m_g_post_mix', 'new_m_g_pre_ffn', 'new_m_w_ff1', 'new_m_w_ff2', 'new_m_g_post_ffn', 'new_v_meta_tokens', 'new_v_g_pre_mix', 'new_v_w_in', 'new_v_conv_w', 'new_v_conv_b', 'new_v_w_a', 'new_v_b_a', 'new_v_w_x', 'new_v_b_x', 'new_v_lru_lambda', 'new_v_attn_sinks', 'new_v_w_out', 'new_v_g_post_mix', 'new_v_g_pre_ffn', 'new_v_w_ff1', 'new_v_w_ff2', 'new_v_g_post_ffn']
TWIN_LEAF_KINDS = {'loss': 'loss', 'grad_x': 'grad_x', 'grad_meta_tokens': 'grad_w', 'grad_g_pre_mix': 'grad_w', 'grad_w_in': 'grad_w', 'grad_conv_w': 'grad_w', 'grad_conv_b': 'grad_w', 'grad_w_a': 'grad_w', 'grad_b_a': 'grad_w', 'grad_w_x': 'grad_w', 'grad_b_x': 'grad_w', 'grad_lru_lambda': 'grad_w', 'grad_attn_sinks': 'grad_w', 'grad_w_out': 'grad_w', 'grad_g_post_mix': 'grad_w', 'grad_g_pre_ffn': 'grad_w', 'grad_w_ff1': 'grad_w', 'grad_w_ff2': 'grad_w', 'grad_g_post_ffn': 'grad_w', 'delta_meta_tokens': 'delta_w', 'delta_g_pre_mix': 'delta_w', 'delta_w_in': 'delta_w', 'delta_conv_w': 'delta_w', 'delta_conv_b': 'delta_w', 'delta_w_a': 'delta_w', 'delta_b_a': 'delta_w', 'delta_w_x': 'delta_w', 'delta_b_x': 'delta_w', 'delta_lru_lambda': 'delta_w', 'delta_attn_sinks': 'delta_w', 'delta_w_out': 'delta_w', 'delta_g_post_mix': 'delta_w', 'delta_g_pre_ffn': 'delta_w', 'delta_w_ff1': 'delta_w', 'delta_w_ff2': 'delta_w', 'delta_g_post_ffn': 'delta_w', 'new_m_meta_tokens': 'new_m', 'new_m_g_pre_mix': 'new_m', 'new_m_w_in': 'new_m', 'new_m_conv_w': 'new_m', 'new_m_conv_b': 'new_m', 'new_m_w_a': 'new_m', 'new_m_b_a': 'new_m', 'new_m_w_x': 'new_m', 'new_m_b_x': 'new_m', 'new_m_lru_lambda': 'new_m', 'new_m_attn_sinks': 'new_m', 'new_m_w_out': 'new_m', 'new_m_g_post_mix': 'new_m', 'new_m_g_pre_ffn': 'new_m', 'new_m_w_ff1': 'new_m', 'new_m_w_ff2': 'new_m', 'new_m_g_post_ffn': 'new_m', 'new_v_meta_tokens': 'new_v', 'new_v_g_pre_mix': 'new_v', 'new_v_w_in': 'new_v', 'new_v_conv_w': 'new_v', 'new_v_conv_b': 'new_v', 'new_v_w_a': 'new_v', 'new_v_b_a': 'new_v', 'new_v_w_x': 'new_v', 'new_v_b_x': 'new_v', 'new_v_lru_lambda': 'new_v', 'new_v_attn_sinks': 'new_v', 'new_v_w_out': 'new_v', 'new_v_g_post_mix': 'new_v', 'new_v_g_pre_ffn': 'new_v', 'new_v_w_ff1': 'new_v', 'new_v_w_ff2': 'new_v', 'new_v_g_post_ffn': 'new_v'}


def _forward(args):
    return _fwd_reference(*[args[k] for k in FWD_PARAMS])


def _output_shape():
    def fwd():
        inp = _fwd_setup_inputs(0)
        return _fwd_reference(*[inp[k] for k in FWD_PARAMS])
    out = _jax.eval_shape(fwd)
    return out.shape, out.dtype

N_MICROBATCH = 1
ADAM_LR = 0.001
ADAM_B1 = 0.9
ADAM_B2 = 0.999
ADAM_EPS = 1e-08
ADAM_WD = 0.01
ADAM_STEP = 10
PER_EXAMPLE_BATCH_AXIS = {'x': 0, 'loss_target': 0}
SHARED_INPUTS = []
_WEIGHT_DTYPES = {'meta_tokens': _jnp.float32, 'g_pre_mix': _jnp.float32, 'w_in': _jnp.float32, 'conv_w': _jnp.float32, 'conv_b': _jnp.float32, 'w_a': _jnp.float32, 'b_a': _jnp.float32, 'w_x': _jnp.float32, 'b_x': _jnp.float32, 'lru_lambda': _jnp.float32, 'attn_sinks': _jnp.float32, 'w_out': _jnp.float32, 'g_post_mix': _jnp.float32, 'g_pre_ffn': _jnp.float32, 'w_ff1': _jnp.float32, 'w_ff2': _jnp.float32, 'g_post_ffn': _jnp.float32}
MOMENT_SCALE = {'meta_tokens': 1.430605e-01, 'g_pre_mix': 1.344561e+00, 'w_in': 9.098162e-01, 'conv_w': 8.219465e+00, 'conv_b': 1.064817e+02, 'w_a': 3.490966e+00, 'b_a': 2.070705e+00, 'w_x': 6.435255e+00, 'b_x': 2.091139e+00, 'lru_lambda': 2.997304e+00, 'attn_sinks': 1.367995e-01, 'w_out': 7.889409e+00, 'g_post_mix': 6.522209e+01, 'g_pre_ffn': 3.383274e+00, 'w_ff1': 1.711455e+00, 'w_ff2': 7.376575e+00, 'g_post_ffn': 6.683739e+01}


def _to_microbatches(a, axis):
    t = _jnp.moveaxis(a, axis, 0)
    t = t.reshape((N_MICROBATCH, t.shape[0] // N_MICROBATCH) + t.shape[1:])
    return _jnp.moveaxis(t, 1, axis + 1)


def setup_inputs(seed: int = 0) -> dict:
    inp = _fwd_setup_inputs(seed)
    key = _jax.random.fold_in(_jax.random.key(seed), 7919)
    shape, _ = _output_shape()
    out = dict(inp)
    out["loss_target"] = _jax.random.normal(_jax.random.fold_in(key, 0), shape, _jnp.float32)
    for i, name in enumerate(TWIN_WEIGHTS):
        w = inp[name].astype(_jnp.float32)
        if MOMENT_SCALE is None:
            s = _jnp.sqrt(_jnp.mean(_jnp.square(w)) + 1e-30)
        else:
            s = MOMENT_SCALE[name]
        km, kv = _jax.random.split(_jax.random.fold_in(key, i + 1))
        out[name] = w
        out["m_" + name] = s * _jax.random.normal(km, w.shape, _jnp.float32)
        out["v_" + name] = (s * s) * _jax.random.uniform(kv, w.shape, _jnp.float32, 0.5, 1.5)
    if N_MICROBATCH > 1:
        for name, axis in PER_EXAMPLE_BATCH_AXIS.items():
            out[name] = _to_microbatches(out[name], axis)
    return {'x': out['x'], 'meta_tokens': out['meta_tokens'], 'g_pre_mix': out['g_pre_mix'], 'w_in': out['w_in'], 'conv_w': out['conv_w'], 'conv_b': out['conv_b'], 'w_a': out['w_a'], 'b_a': out['b_a'], 'w_x': out['w_x'], 'b_x': out['b_x'], 'lru_lambda': out['lru_lambda'], 'attn_sinks': out['attn_sinks'], 'w_out': out['w_out'], 'g_post_mix': out['g_post_mix'], 'g_pre_ffn': out['g_pre_ffn'], 'w_ff1': out['w_ff1'], 'w_ff2': out['w_ff2'], 'g_post_ffn': out['g_post_ffn'], 'loss_target': out['loss_target'], 'm_meta_tokens': out['m_meta_tokens'], 'm_g_pre_mix': out['m_g_pre_mix'], 'm_w_in': out['m_w_in'], 'm_conv_w': out['m_conv_w'], 'm_conv_b': out['m_conv_b'], 'm_w_a': out['m_w_a'], 'm_b_a': out['m_b_a'], 'm_w_x': out['m_w_x'], 'm_b_x': out['m_b_x'], 'm_lru_lambda': out['m_lru_lambda'], 'm_attn_sinks': out['m_attn_sinks'], 'm_w_out': out['m_w_out'], 'm_g_post_mix': out['m_g_post_mix'], 'm_g_pre_ffn': out['m_g_pre_ffn'], 'm_w_ff1': out['m_w_ff1'], 'm_w_ff2': out['m_w_ff2'], 'm_g_post_ffn': out['m_g_post_ffn'], 'v_meta_tokens': out['v_meta_tokens'], 'v_g_pre_mix': out['v_g_pre_mix'], 'v_w_in': out['v_w_in'], 'v_conv_w': out['v_conv_w'], 'v_conv_b': out['v_conv_b'], 'v_w_a': out['v_w_a'], 'v_b_a': out['v_b_a'], 'v_w_x': out['v_w_x'], 'v_b_x': out['v_b_x'], 'v_lru_lambda': out['v_lru_lambda'], 'v_attn_sinks': out['v_attn_sinks'], 'v_w_out': out['v_w_out'], 'v_g_post_mix': out['v_g_post_mix'], 'v_g_pre_ffn': out['v_g_pre_ffn'], 'v_w_ff1': out['v_w_ff1'], 'v_w_ff2': out['v_w_ff2'], 'v_g_post_ffn': out['v_g_post_ffn']}


def _loss(weights, diff, rest, loss_target):
    with _jax.named_scope("forward"):
        args = {**rest, TWIN_DIFF_INPUT: diff, **{k: w.astype(_WEIGHT_DTYPES[k]) for k, w in weights.items()}}
        y = _forward(args)
    with _jax.named_scope("loss_head"):
        err = _jnp.square(y.astype(_jnp.float32) - loss_target)
        return 0.5 * _jnp.sum(_jnp.mean(err, axis=-1)) if err.ndim else 0.5 * err


def _adamw(w, g, m, v):
    m = ADAM_B1 * m + (1.0 - ADAM_B1) * g
    v = ADAM_B2 * v + (1.0 - ADAM_B2) * _jnp.square(g)
    m_hat = m / (1.0 - ADAM_B1 ** ADAM_STEP)
    v_hat = v / (1.0 - ADAM_B2 ** ADAM_STEP)
    delta = -ADAM_LR * (m_hat / (_jnp.sqrt(v_hat) + ADAM_EPS) + ADAM_WD * w)
    return delta, m, v


def reference(x, meta_tokens, g_pre_mix, w_in, conv_w, conv_b, w_a, b_a, w_x, b_x, lru_lambda, attn_sinks, w_out, g_post_mix, g_pre_ffn, w_ff1, w_ff2, g_post_ffn, loss_target, m_meta_tokens, m_g_pre_mix, m_w_in, m_conv_w, m_conv_b, m_w_a, m_b_a, m_w_x, m_b_x, m_lru_lambda, m_attn_sinks, m_w_out, m_g_post_mix, m_g_pre_ffn, m_w_ff1, m_w_ff2, m_g_post_ffn, v_meta_tokens, v_g_pre_mix, v_w_in, v_conv_w, v_conv_b, v_w_a, v_b_a, v_w_x, v_b_x, v_lru_lambda, v_attn_sinks, v_w_out, v_g_post_mix, v_g_pre_ffn, v_w_ff1, v_w_ff2, v_g_post_ffn):
    given = dict(x=x, meta_tokens=meta_tokens, g_pre_mix=g_pre_mix, w_in=w_in, conv_w=conv_w, conv_b=conv_b, w_a=w_a, b_a=b_a, w_x=w_x, b_x=b_x, lru_lambda=lru_lambda, attn_sinks=attn_sinks, w_out=w_out, g_post_mix=g_post_mix, g_pre_ffn=g_pre_ffn, w_ff1=w_ff1, w_ff2=w_ff2, g_post_ffn=g_post_ffn, loss_target=loss_target, m_meta_tokens=m_meta_tokens, m_g_pre_mix=m_g_pre_mix, m_w_in=m_w_in, m_conv_w=m_conv_w, m_conv_b=m_conv_b, m_w_a=m_w_a, m_b_a=m_b_a, m_w_x=m_w_x, m_b_x=m_b_x, m_lru_lambda=m_lru_lambda, m_attn_sinks=m_attn_sinks, m_w_out=m_w_out, m_g_post_mix=m_g_post_mix, m_g_pre_ffn=m_g_pre_ffn, m_w_ff1=m_w_ff1, m_w_ff2=m_w_ff2, m_g_post_ffn=m_g_post_ffn, v_meta_tokens=v_meta_tokens, v_g_pre_mix=v_g_pre_mix, v_w_in=v_w_in, v_conv_w=v_conv_w, v_conv_b=v_conv_b, v_w_a=v_w_a, v_b_a=v_b_a, v_w_x=v_w_x, v_b_x=v_b_x, v_lru_lambda=v_lru_lambda, v_attn_sinks=v_attn_sinks, v_w_out=v_w_out, v_g_post_mix=v_g_post_mix, v_g_pre_ffn=v_g_pre_ffn, v_w_ff1=v_w_ff1, v_w_ff2=v_w_ff2, v_g_post_ffn=v_g_post_ffn)
    weights = {n: given[n] for n in TWIN_WEIGHTS}
    shared = {n: given[n] for n in SHARED_INPUTS}
    per_example = {n: given[n] for n in ['x']}
    grad_fn = _jax.value_and_grad(_loss, argnums=(0, 1))

    def one_microbatch(ex, loss_target):
        ex = dict(ex)
        diff = ex.pop(TWIN_DIFF_INPUT)
        return grad_fn(weights, diff, {**shared, **ex}, loss_target)

    if N_MICROBATCH == 1:
        loss, (grad_w, grad_x) = one_microbatch(per_example, given["loss_target"])
    else:
        def body(carry, xs):
            loss_sum, grad_sum = carry
            l_k, (gw_k, gx_k) = one_microbatch(xs[0], xs[1])
            with _jax.named_scope("update"):
                return (loss_sum + l_k, _jax.tree.map(_jnp.add, grad_sum, gw_k)), gx_k

        init = (_jnp.zeros((), _jnp.float32), _jax.tree.map(_jnp.zeros_like, weights))
        (loss, grad_w), grad_x = _jax.lax.scan(body, init, (per_example, given["loss_target"]))
    with _jax.named_scope("update"):
        delta_w, new_m, new_v = {}, {}, {}
        for n in TWIN_WEIGHTS:
            delta_w[n], new_m[n], new_v[n] = _adamw(weights[n], grad_w[n], given["m_" + n], given["v_" + n])
    return (loss, grad_x, *[grad_w[n] for n in TWIN_WEIGHTS], *[delta_w[n] for n in TWIN_WEIGHTS],
            *[new_m[n] for n in TWIN_WEIGHTS], *[new_v[n] for n in TWIN_WEIGHTS])
```

```python
import functools

import jax
import jax.numpy as jnp
from jax import lax
from jax.experimental import pallas as pl
from jax.experimental.pallas import tpu as pltpu

F32 = jnp.float32
BF16 = jnp.bfloat16

D_MODEL = 1024
N_META = 16
BLOCK = 128
PAD_ROWS = BLOCK - N_META
HEAD_DIM = 64
ATTN_HEADS = 8
GQA_GROUP = 4
ATTN_WIDTH = 512
KV_WIDTH = 128
QKV_WIDTH = ATTN_WIDTH + 2 * KV_WIDTH
LRU_WIDTH = 512
LRU_BLOCKS = 8
LRU_BLOCK = 64
LRU_C = 8.0
IN_WIDTH = 1792
D_FF = 4096
N_CHIPS = 4
FF_CHUNK = D_FF // N_CHIPS
EPS = 1e-6
NEG = -1e30

ADAM_LR = 0.001
ADAM_B1 = 0.9
ADAM_B2 = 0.999
ADAM_EPS = 1e-08
ADAM_WD = 0.01
ADAM_STEP = 10

VMEM_LIMIT_V7X = 56 * 1024 * 1024
MESH = pl.DeviceIdType.MESH

NT = (((1,), (1,)), ((), ()))
TN = (((0,), (0,)), ((), ()))


def _row_tile(tp):
    return 640 if tp % 640 == 0 else BLOCK


def _params(*sem):
    return pltpu.CompilerParams(dimension_semantics=sem, vmem_limit_bytes=VMEM_LIMIT_V7X)


def _dot(a, b):
    return jnp.dot(a, b, preferred_element_type=F32)


def _dot_nt(a, b):
    return lax.dot_general(a, b, NT, preferred_element_type=F32)


def _dot_tn(a, b):
    return lax.dot_general(a, b, TN, preferred_element_type=F32)


def _rms(x):
    rs = lax.rsqrt(jnp.mean(x * x, axis=-1, keepdims=True) + EPS)
    return x * rs, rs


def _rms_bwd(xhat, rs, g, dy):
    dyg = dy * g
    dx = rs * (dyg - xhat * jnp.mean(dyg * xhat, axis=-1, keepdims=True))
    dg = jnp.sum(dy * xhat, axis=0, keepdims=True)
    return dx, dg


def _gelu(x):
    k = 0.7978845608028654
    t = jnp.tanh(k * (x + 0.044715 * x * x * x))
    return 0.5 * x * (1.0 + t), t


def _gelu_grad(x, t):
    k = 0.7978845608028654
    return 0.5 * (1.0 + t) + 0.5 * x * (1.0 - t * t) * k * (1.0 + 3 * 0.044715 * x * x)


def _sigmoid(x):
    return 1.0 / (1.0 + jnp.exp(-x))


def _neg_expm1(x):
    series = x * (1.0 + x * 0.5 * (1.0 + x * (1.0 / 3.0) * (1.0 + x * 0.25 * (1.0 + x * 0.2))))
    return -jnp.where(jnp.abs(x) < 0.05, series, jnp.exp(x) - 1.0)


def _softplus(x):
    return jnp.maximum(x, 0.0) + jnp.log1p(jnp.exp(-jnp.abs(x)))


def _inproj_fwd(h0, g, w_in):
    tp = h0.shape[0]
    tr = _row_tile(tp)

    def body(h_ref, g_ref, w_ref, u_ref, qkv_ref, xr_ref, yr_ref):
        xhat, _ = _rms(h_ref[...])
        u = (xhat * g_ref[...]).astype(BF16)
        u_ref[...] = u
        z = _dot(u, w_ref[...])
        qkv_ref[...] = z[:, :QKV_WIDTH].astype(BF16)
        xr_ref[...] = z[:, QKV_WIDTH:QKV_WIDTH + LRU_WIDTH]
        yr_ref[...] = z[:, QKV_WIDTH + LRU_WIDTH:]

    row = lambda w: pl.BlockSpec((tr, w), lambda i: (i, 0))
    full = lambda a: pl.BlockSpec(a.shape, lambda i: (0,) * a.ndim)
    return pl.pallas_call(
        body, name="inproj_fwd", grid=(tp // tr,),
        in_specs=[row(D_MODEL), full(g), full(w_in)],
        out_specs=[row(D_MODEL), row(QKV_WIDTH), row(LRU_WIDTH), row(LRU_WIDTH)],
        out_shape=[jax.ShapeDtypeStruct((tp, D_MODEL), BF16), jax.ShapeDtypeStruct((tp, QKV_WIDTH), BF16),
                   jax.ShapeDtypeStruct((tp, LRU_WIDTH), F32), jax.ShapeDtypeStruct((tp, LRU_WIDTH), F32)],
        compiler_params=_params("parallel"),
    )(h0, g, w_in)


def _attn_valid(n):
    i = lax.broadcasted_iota(jnp.int32, (BLOCK, 2 * BLOCK), 0)
    j = lax.broadcasted_iota(jnp.int32, (BLOCK, 2 * BLOCK), 1)
    diff = i + BLOCK - j
    kpos = (n - 1) * BLOCK + j
    return (diff >= 0) & (diff < BLOCK) & (kpos >= PAD_ROWS)


def _attn_probs(q_h, k2, valid, sink):
    s = _dot_nt(q_h, k2) * (HEAD_DIM ** -0.5)
    s = jnp.where(valid, s, NEG)
    m = jnp.maximum(jnp.max(s, axis=-1, keepdims=True), sink)
    p = jnp.exp(s - m)
    es = jnp.exp(sink - m)
    inv = 1.0 / (jnp.sum(p, axis=-1, keepdims=True) + es)
    return p * inv, es * inv


def _kv_specs(nb):
    prev = lambda col: pl.BlockSpec((BLOCK, KV_WIDTH), lambda n: (jnp.maximum(jnp.minimum(n, nb - 1) - 1, 0), col))
    cur = lambda col: pl.BlockSpec((BLOCK, KV_WIDTH), lambda n: (jnp.minimum(n, nb - 1), col))
    return [prev(4), cur(4), prev(5), cur(5)]


def _attn_fwd(qkv, sinks):
    tp = qkv.shape[0]
    nb = tp // BLOCK

    def body(s_ref, q_ref, kp_ref, kc_ref, vp_ref, vc_ref, o_ref):
        n = pl.program_id(0)
        valid = _attn_valid(n)
        q = q_ref[...]
        k2 = jnp.concatenate([kp_ref[...], kc_ref[...]], axis=0)
        v2 = jnp.concatenate([vp_ref[...], vc_ref[...]], axis=0)
        outs = []
        for h in range(ATTN_HEADS):
            kv = h // GQA_GROUP
            p, _ = _attn_probs(q[:, h * HEAD_DIM:(h + 1) * HEAD_DIM], k2[:, kv * HEAD_DIM:(kv + 1) * HEAD_DIM],
                               valid, s_ref[h])
            outs.append(_dot(p.astype(BF16), v2[:, kv * HEAD_DIM:(kv + 1) * HEAD_DIM]))
        o_ref[...] = jnp.concatenate(outs, axis=1).astype(BF16)

    return pl.pallas_call(
        body, name="attn_fwd", grid=(nb,),
        in_specs=[pl.BlockSpec(memory_space=pltpu.SMEM),
                  pl.BlockSpec((BLOCK, ATTN_WIDTH), lambda n: (n, 0))] + _kv_specs(nb),
        out_specs=pl.BlockSpec((BLOCK, ATTN_WIDTH), lambda n: (n, 0)),
        out_shape=jax.ShapeDtypeStruct((tp, ATTN_WIDTH), BF16),
        compiler_params=_params("parallel"),
    )(sinks, qkv, qkv, qkv, qkv, qkv)


def _conv_taps(x, halo):
    ext = jnp.concatenate([halo, x], axis=0)
    return [ext[8:] if k == 3 else pltpu.roll(ext, 3 - k, 0)[8:] for k in range(4)]


def _lru_gates(xc, wa, ba, wx, bx, sp):
    xb = xc.astype(BF16)
    r = _sigmoid(_dot(xb, wa) + ba)
    ig = _sigmoid(_dot(xb, wx) + bx)
    log_a = (-LRU_C * sp) * r
    a = jnp.exp(log_a)
    mult = jnp.sqrt(_neg_expm1(2.0 * log_a))
    return xb, r, ig, a, mult


def _scan_fwd(a, b):
    rows = lax.broadcasted_iota(jnp.int32, a.shape, 0)
    d = 1
    while d < a.shape[0]:
        keep = rows >= d
        b = jnp.where(keep, a * pltpu.roll(b, d, 0) + b, b)
        a = jnp.where(keep, a * pltpu.roll(a, d, 0), a)
        d *= 2
    return a, b


def _scan_rev(c, b):
    n = c.shape[0]
    rows = lax.broadcasted_iota(jnp.int32, c.shape, 0)
    d = 1
    while d < n:
        keep = rows < n - d
        b = jnp.where(keep, b + c * pltpu.roll(b, n - d, 0), b)
        c = jnp.where(keep, c * pltpu.roll(c, n - d, 0), c)
        d *= 2
    return c, b


def _lru_fwd(xr, yr, conv_w, conv_b, wa, ba, wx, bx, lam):
    tp = xr.shape[0]
    nb = tp // BLOCK

    def body(xr_ref, yr_ref, cw_ref, cb_ref, wa_ref, ba_ref, wx_ref, bx_ref, lam_ref, hr_ref, rec_ref, halo, hprev):
        n = pl.program_id(0)

        @pl.when(n == 0)
        def _():
            halo[...] = jnp.zeros_like(halo)
            hprev[...] = jnp.zeros_like(hprev)

        x = xr_ref[...]
        taps = _conv_taps(x, halo[...])
        halo[...] = x[BLOCK - 8:]
        cw = cw_ref[...]
        xc = cb_ref[...] + sum(cw[k:k + 1] * taps[k] for k in range(4))
        sp = _softplus(-lam_ref[...])
        _, _, ig, a, mult = _lru_gates(xc, wa_ref[...], ba_ref[...], wx_ref[...], bx_ref[...], sp)
        rows = n * BLOCK + lax.broadcasted_iota(jnp.int32, xc.shape, 0)
        u = jnp.where(rows >= PAD_ROWS, mult * (ig * xc), 0.0)
        acum, hloc = _scan_fwd(a, u)
        h = acum * hprev[0:1] + hloc
        hprev[0:1] = h[BLOCK - 1:]
        hr_ref[...] = h
        gl, _ = _gelu(yr_ref[...])
        rec_ref[...] = (gl * h).astype(BF16)

    blk = pl.BlockSpec((BLOCK, LRU_WIDTH), lambda n: (n, 0))
    full = lambda a: pl.BlockSpec(a.shape, lambda n: (0,) * a.ndim)
    small = [conv_w, conv_b, wa, ba, wx, bx, lam]
    return pl.pallas_call(
        body, name="lru_fwd", grid=(nb,),
        in_specs=[blk, blk] + [full(a) for a in small],
        out_specs=[blk, blk],
        out_shape=[jax.ShapeDtypeStruct((tp, LRU_WIDTH), F32), jax.ShapeDtypeStruct((tp, LRU_WIDTH), BF16)],
        scratch_shapes=[pltpu.VMEM((8, LRU_WIDTH), F32), pltpu.VMEM((8, LRU_WIDTH), F32)],
        compiler_params=_params("arbitrary"),
    )(xr, yr, *small)


def _outproj_fwd(attn, rec, w_out, h0, g_post_mix, g_pre_ffn):
    tp = h0.shape[0]
    tr = _row_tile(tp)

    def body(a_ref, r_ref, w_ref, h_ref, gm_ref, gf_ref, mix_ref, h1_ref, u1_ref):
        mix = _dot(a_ref[...], w_ref[:ATTN_WIDTH]) + _dot(r_ref[...], w_ref[ATTN_WIDTH:])
        mix_ref[...] = mix
        mhat, _ = _rms(mix)
        h1 = h_ref[...] + mhat * gm_ref[...]
        h1_ref[...] = h1
        hhat, _ = _rms(h1)
        u1_ref[...] = (hhat * gf_ref[...]).astype(BF16)

    row = lambda w: pl.BlockSpec((tr, w), lambda i: (i, 0))
    full = lambda a: pl.BlockSpec(a.shape, lambda i: (0,) * a.ndim)
    return pl.pallas_call(
        body, name="outproj_fwd", grid=(tp // tr,),
        in_specs=[row(ATTN_WIDTH), row(LRU_WIDTH), full(w_out), row(D_MODEL), full(g_post_mix), full(g_pre_ffn)],
        out_specs=[row(D_MODEL), row(D_MODEL), row(D_MODEL)],
        out_shape=[jax.ShapeDtypeStruct((tp, D_MODEL), F32), jax.ShapeDtypeStruct((tp, D_MODEL), F32),
                   jax.ShapeDtypeStruct((tp, D_MODEL), BF16)],
        compiler_params=_params("parallel"),
    )(attn, rec, w_out, h0, g_post_mix, g_pre_ffn)


def _ffn_fwd(u1, w1, w2, h1, tgt, g_post_ffn):
    tp = h1.shape[0]
    tr = _row_tile(tp)

    def body(u_ref, w1_ref, w2_ref, h1_ref, t_ref, g_ref, r1_ref, dy_ref, df2_ref, loss_ref, dg_ref, acc):
        i, c = pl.program_id(0), pl.program_id(1)

        @pl.when((i == 0) & (c == 0))
        def _():
            loss_ref[...] = jnp.zeros_like(loss_ref)
            dg_ref[...] = jnp.zeros_like(dg_ref)

        r = jnp.maximum(_dot(u_ref[...], w1_ref[0]), 0.0)
        r1_ref[...] = r.astype(BF16)
        part = _dot((r * r).astype(BF16), w2_ref[0])

        @pl.when(c == 0)
        def _():
            acc[...] = part

        @pl.when(c > 0)
        def _():
            acc[...] += part

        @pl.when(c == N_CHIPS - 1)
        def _():
            g = g_ref[...]
            fhat, rs = _rms(acc[...])
            h2 = h1_ref[...] + fhat * g
            rows = i * tr + lax.broadcasted_iota(jnp.int32, h2.shape, 0)
            err = jnp.where(rows >= BLOCK, h2 - t_ref[...], 0.0)
            dy = err * (1.0 / D_MODEL)
            dy_ref[...] = dy
            loss_ref[...] += (0.5 / D_MODEL) * jnp.sum(err * err)
            df2, dg = _rms_bwd(fhat, rs, g, dy)
            df2_ref[...] = df2.astype(BF16)
            dg_ref[...] += dg

    row = pl.BlockSpec((tr, D_MODEL), lambda i, c: (i, 0))
    full = lambda a: pl.BlockSpec(a.shape, lambda i, c: (0,) * a.ndim)
    return pl.pallas_call(
        body, name="ffn_fwd", grid=(tp // tr, N_CHIPS),
        in_specs=[row, pl.BlockSpec((1, D_MODEL, FF_CHUNK), lambda i, c: (c, 0, 0)),
                  pl.BlockSpec((1, FF_CHUNK, D_MODEL), lambda i, c: (c, 0, 0)), row, row, full(g_post_ffn)],
        out_specs=[pl.BlockSpec((tr, FF_CHUNK), lambda i, c: (i, c)), row, row,
                   pl.BlockSpec((1, 1), lambda i, c: (0, 0)), pl.BlockSpec((1, D_MODEL), lambda i, c: (0, 0))],
        out_shape=[jax.ShapeDtypeStruct((tp, D_FF), BF16), jax.ShapeDtypeStruct((tp, D_MODEL), F32),
                   jax.ShapeDtypeStruct((tp, D_MODEL), BF16), jax.ShapeDtypeStruct((1, 1), F32),
                   jax.ShapeDtypeStruct((1, D_MODEL), F32)],
        scratch_shapes=[pltpu.VMEM((tr, D_MODEL), F32)],
        compiler_params=_params("arbitrary", "arbitrary"),
    )(u1, w1, w2, h1, tgt, g_post_ffn)


def _ffn_bwd_data(df2, r1, w1, w2, dy, h1, mix, g_pre_ffn, g_post_mix):
    tp = h1.shape[0]
    tr = _row_tile(tp)

    def body(df2_ref, r1_ref, w1_ref, w2_ref, dy_ref, h1_ref, mix_ref, gf_ref, gm_ref,
             da_ref, dh1_ref, dmix_ref, dgf_ref, dgm_ref, acc):
        i, c = pl.program_id(0), pl.program_id(1)

        @pl.when((i == 0) & (c == 0))
        def _():
            dgf_ref[...] = jnp.zeros_like(dgf_ref)
            dgm_ref[...] = jnp.zeros_like(dgm_ref)

        df = _dot_nt(df2_ref[...], w2_ref[0])
        da = (df * (2.0 * r1_ref[...].astype(F32))).astype(BF16)
        da_ref[...] = da
        part = _dot_nt(da, w1_ref[0])

        @pl.when(c == 0)
        def _():
            acc[...] = part

        @pl.when(c > 0)
        def _():
            acc[...] += part

        @pl.when(c == N_CHIPS - 1)
        def _():
            hhat, rs = _rms(h1_ref[...])
            dx, dgf = _rms_bwd(hhat, rs, gf_ref[...], acc[...])
            dh1 = dy_ref[...] + dx
            dh1_ref[...] = dh1
            dgf_ref[...] += dgf
            mhat, rsm = _rms(mix_ref[...])
            dmix, dgm = _rms_bwd(mhat, rsm, gm_ref[...], dh1)
            dmix_ref[...] = dmix.astype(BF16)
            dgm_ref[...] += dgm

    row = pl.BlockSpec((tr, D_MODEL), lambda i, c: (i, 0))
    chunk = pl.BlockSpec((tr, FF_CHUNK), lambda i, c: (i, c))
    gain = pl.BlockSpec((1, D_MODEL), lambda i, c: (0, 0))
    return pl.pallas_call(
        body, name="ffn_bwd_data", grid=(tp // tr, N_CHIPS),
        in_specs=[row, chunk, pl.BlockSpec((1, D_MODEL, FF_CHUNK), lambda i, c: (c, 0, 0)),
                  pl.BlockSpec((1, FF_CHUNK, D_MODEL), lambda i, c: (c, 0, 0)), row, row, row, gain, gain],
        out_specs=[chunk, row, row, gain, gain],
        out_shape=[jax.ShapeDtypeStruct((tp, D_FF), BF16), jax.ShapeDtypeStruct((tp, D_MODEL), F32),
                   jax.ShapeDtypeStruct((tp, D_MODEL), BF16), jax.ShapeDtypeStruct((1, D_MODEL), F32),
                   jax.ShapeDtypeStruct((1, D_MODEL), F32)],
        scratch_shapes=[pltpu.VMEM((tr, D_MODEL), F32)],
        compiler_params=_params("arbitrary", "arbitrary"),
    )(df2, r1, w1, w2, dy, h1, mix, g_pre_ffn, g_post_mix)


def _ffn_bwd_weights(u1, da1, r1, df2):
    tp = u1.shape[0]
    tr = _row_tile(tp)

    def body(u_ref, da_ref, r1_ref, df2_ref, dw1_ref, dw2_ref):
        i = pl.program_id(1)
        r = r1_ref[...].astype(F32)
        p1 = _dot_tn(u_ref[...], da_ref[...])
        p2 = _dot_tn((r * r).astype(BF16), df2_ref[...])

        @pl.when(i == 0)
        def _():
            dw1_ref[0] = p1
            dw2_ref[0] = p2

        @pl.when(i > 0)
        def _():
            dw1_ref[0] += p1
            dw2_ref[0] += p2

    row = pl.BlockSpec((tr, D_MODEL), lambda c, i: (i, 0))
    chunk = pl.BlockSpec((tr, FF_CHUNK), lambda c, i: (i, c))
    return pl.pallas_call(
        body, name="ffn_bwd_weights", grid=(N_CHIPS, tp // tr),
        in_specs=[row, chunk, chunk, row],
        out_specs=[pl.BlockSpec((1, D_MODEL, FF_CHUNK), lambda c, i: (c, 0, 0)),
                   pl.BlockSpec((1, FF_CHUNK, D_MODEL), lambda c, i: (c, 0, 0))],
        out_shape=[jax.ShapeDtypeStruct((N_CHIPS, D_MODEL, FF_CHUNK), F32),
                   jax.ShapeDtypeStruct((N_CHIPS, FF_CHUNK, D_MODEL), F32)],
        compiler_params=_params("parallel", "arbitrary"),
    )(u1, da1, r1, df2)


def _outproj_bwd(dmix, w_out, attn, rec):
    tp = dmix.shape[0]
    tr = _row_tile(tp)

    def body(dm_ref, w_ref, a_ref, r_ref, da_ref, dr_ref, dw_ref):
        i = pl.program_id(0)
        dm = dm_ref[...]
        dcat = _dot_nt(dm, w_ref[...])
        da_ref[...] = dcat[:, :ATTN_WIDTH].astype(BF16)
        dr_ref[...] = dcat[:, ATTN_WIDTH:]
        pa = _dot_tn(a_ref[...], dm)
        pr = _dot_tn(r_ref[...], dm)

        @pl.when(i == 0)
        def _():
            dw_ref[:ATTN_WIDTH] = pa
            dw_ref[ATTN_WIDTH:] = pr

        @pl.when(i > 0)
        def _():
            dw_ref[:ATTN_WIDTH] += pa
            dw_ref[ATTN_WIDTH:] += pr

    row = lambda w: pl.BlockSpec((tr, w), lambda i: (i, 0))
    full = pl.BlockSpec((D_MODEL, D_MODEL), lambda i: (0, 0))
    return pl.pallas_call(
        body, name="outproj_bwd", grid=(tp // tr,),
        in_specs=[row(D_MODEL), full, row(ATTN_WIDTH), row(LRU_WIDTH)],
        out_specs=[row(ATTN_WIDTH), row(LRU_WIDTH), full],
        out_shape=[jax.ShapeDtypeStruct((tp, ATTN_WIDTH), BF16), jax.ShapeDtypeStruct((tp, LRU_WIDTH), F32),
                   jax.ShapeDtypeStruct((D_MODEL, D_MODEL), F32)],
        compiler_params=_params("arbitrary"),
    )(dmix, w_out, attn, rec)


N_VEC_ROWS = 8


def _lru_bwd(xr, yr, hr, drec, conv_w, conv_b, wa, ba, wx, bx, lam):
    tp = xr.shape[0]
    nb = tp // BLOCK

    def body(xr_ref, xh_ref, yr_ref, hr_ref, hp_ref, dr_ref, cw_ref, cb_ref, wa_ref, ba_ref, wx_ref, bx_ref, lam_ref,
             dxr_ref, dyr_ref, dwa_ref, dwx_ref, vec_ref, g_next, a_next, dxc_next, dsp):
        s = pl.program_id(0)
        n = nb - 1 - s

        @pl.when(s == 0)
        def _():
            g_next[...] = jnp.zeros_like(g_next)
            a_next[...] = jnp.zeros_like(a_next)
            dxc_next[...] = jnp.zeros_like(dxc_next)
            dsp[...] = jnp.zeros_like(dsp)
            dwa_ref[...] = jnp.zeros_like(dwa_ref)
            dwx_ref[...] = jnp.zeros_like(dwx_ref)
            vec_ref[...] = jnp.zeros_like(vec_ref)

        first = n == 0
        x = xr_ref[...]
        taps = _conv_taps(x, jnp.where(first, 0.0, xh_ref[...]))
        cw = cw_ref[...]
        xc = cb_ref[...] + sum(cw[k:k + 1] * taps[k] for k in range(4))
        lam_v = lam_ref[...]
        sp = _softplus(-lam_v)
        wa_m, wx_m = wa_ref[...], wx_ref[...]
        xb, r, ig, a, mult = _lru_gates(xc, wa_m, ba_ref[...], wx_m, bx_ref[...], sp)

        yr_v = yr_ref[...]
        gl, th = _gelu(yr_v)
        h = hr_ref[...]
        drec = dr_ref[...]
        dyr_ref[...] = (drec * h * _gelu_grad(yr_v, th)).astype(BF16)
        dh_direct = drec * gl

        rows = lax.broadcasted_iota(jnp.int32, a.shape, 0)
        a_up = jnp.where(rows == BLOCK - 1, a_next[0:1], pltpu.roll(a, BLOCK - 1, 0))
        cprod, gloc = _scan_rev(a_up, dh_direct)
        g = gloc + cprod * g_next[0:1]
        g_next[0:1] = g[0:1]
        a_next[0:1] = a[0:1]

        real = (n * BLOCK + rows) >= PAD_ROWS
        h_prev = jnp.where(rows == 0, jnp.where(first, 0.0, hp_ref[7:8]), pltpu.roll(h, 1, 0))
        du = jnp.where(real, g, 0.0)
        da = g * h_prev
        dmult = du * (ig * xc)
        dig = du * (mult * xc)
        dxc = du * (mult * ig)
        dlog_a = jnp.where(real, da * a - dmult * (a * a / mult), 0.0)
        dgr = (dlog_a * (-LRU_C * sp)) * (r * (1.0 - r))
        dgi = dig * (ig * (1.0 - ig))
        dsp[0:1] += jnp.sum(dlog_a * (-LRU_C * r), axis=0, keepdims=True)
        dgr_b, dgi_b = dgr.astype(BF16), dgi.astype(BF16)
        dxc = dxc + _dot_nt(dgr_b, wa_m) + _dot_nt(dgi_b, wx_m)
        dwa_ref[...] += _dot_tn(xb, dgr_b)
        dwx_ref[...] += _dot_tn(xb, dgi_b)

        ext = jnp.concatenate([dxc, dxc_next[...]], axis=0)
        up = [ext[:BLOCK] if j == 0 else pltpu.roll(ext, BLOCK + 8 - j, 0)[:BLOCK] for j in range(4)]
        dxr_ref[...] = sum(cw[k:k + 1] * up[3 - k] for k in range(4)).astype(BF16)
        dxc_next[...] = dxc[:8]

        col = lambda v: jnp.sum(v, axis=0, keepdims=True)
        for k in range(4):
            vec_ref[k:k + 1] += col(dxc * taps[k])
        vec_ref[4:5] += col(dxc)
        vec_ref[5:6] += col(dgr)
        vec_ref[6:7] += col(dgi)

        @pl.when(s == nb - 1)
        def _():
            vec_ref[7:8] = dsp[0:1] * (-_sigmoid(-lam_v))

    blk = pl.BlockSpec((BLOCK, LRU_WIDTH), lambda s: (nb - 1 - s, 0))
    rows_before = pl.BlockSpec((8, LRU_WIDTH), lambda s: (jnp.maximum((nb - 1 - s) * (BLOCK // 8) - 1, 0), 0))
    full = lambda a: pl.BlockSpec(a.shape, lambda s: (0,) * a.ndim)
    small = [conv_w, conv_b, wa, ba, wx, bx, lam]
    sq = pl.BlockSpec((LRU_WIDTH, LRU_WIDTH), lambda s: (0, 0))
    return pl.pallas_call(
        body, name="lru_bwd", grid=(nb,),
        in_specs=[blk, rows_before, blk, blk, rows_before, blk] + [full(a) for a in small],
        out_specs=[blk, blk, sq, sq, pl.BlockSpec((N_VEC_ROWS, LRU_WIDTH), lambda s: (0, 0))],
        out_shape=[jax.ShapeDtypeStruct((tp, LRU_WIDTH), BF16), jax.ShapeDtypeStruct((tp, LRU_WIDTH), BF16),
                   jax.ShapeDtypeStruct((LRU_WIDTH, LRU_WIDTH), F32), jax.ShapeDtypeStruct((LRU_WIDTH, LRU_WIDTH), F32),
                   jax.ShapeDtypeStruct((N_VEC_ROWS, LRU_WIDTH), F32)],
        scratch_shapes=[pltpu.VMEM((8, LRU_WIDTH), F32)] * 4,
        compiler_params=_params("arbitrary"),
    )(xr, xr, yr, hr, hr, drec, *small)


def _attn_bwd(qkv, dattn, sinks):
    tp = qkv.shape[0]
    nb = tp // BLOCK

    def body(s_ref, q_ref, kp_ref, kc_ref, vp_ref, vc_ref, do_ref, dq_ref, dkv_ref, ds_ref, carry):
        n = pl.program_id(0)

        @pl.when(n == 0)
        def _():
            carry[...] = jnp.zeros_like(carry)
            ds_ref[...] = jnp.zeros_like(ds_ref)

        @pl.when(n == nb)
        def _():
            dkv_ref[...] = carry[...].astype(BF16)

        @pl.when(n < nb)
        def _():
            valid = _attn_valid(n)
            q = q_ref[...]
            do = do_ref[...]
            k2 = jnp.concatenate([kp_ref[...], kc_ref[...]], axis=0)
            v2 = jnp.concatenate([vp_ref[...], vc_ref[...]], axis=0)
            dqs, dk2, dv2 = [], [], []
            lane = lax.broadcasted_iota(jnp.int32, (1, ATTN_HEADS), 1)
            dsink = jnp.zeros((1, ATTN_HEADS), F32)
            for kv in range(ATTN_HEADS // GQA_GROUP):
                k_h = k2[:, kv * HEAD_DIM:(kv + 1) * HEAD_DIM]
                v_h = v2[:, kv * HEAD_DIM:(kv + 1) * HEAD_DIM]
                dk_acc = jnp.zeros((2 * BLOCK, HEAD_DIM), F32)
                dv_acc = jnp.zeros((2 * BLOCK, HEAD_DIM), F32)
                for gq in range(GQA_GROUP):
                    h = kv * GQA_GROUP + gq
                    q_h = q[:, h * HEAD_DIM:(h + 1) * HEAD_DIM]
                    do_h = do[:, h * HEAD_DIM:(h + 1) * HEAD_DIM]
                    p, ps = _attn_probs(q_h, k_h, valid, s_ref[h])
                    dp = _dot_nt(do_h, v_h)
                    delta = jnp.sum(p * dp, axis=-1, keepdims=True)
                    dsc = ((p * (dp - delta)) * (HEAD_DIM ** -0.5)).astype(BF16)
                    dsink = dsink + jnp.where(lane == h, -jnp.sum(ps * delta), 0.0)
                    dqs.append(_dot(dsc, k_h))
                    dk_acc = dk_acc + _dot_tn(dsc, q_h)
                    dv_acc = dv_acc + _dot_tn(p.astype(BF16), do_h)
                dk2.append(dk_acc)
                dv2.append(dv_acc)
            dq_ref[...] = jnp.concatenate(dqs, axis=1).astype(BF16)
            dkv = jnp.concatenate(dk2 + dv2, axis=1)
            dkv_ref[...] = (carry[...] + dkv[:BLOCK]).astype(BF16)
            carry[...] = dkv[BLOCK:]
            ds_ref[...] += dsink

    cur = lambda w: pl.BlockSpec((BLOCK, w), lambda n: (jnp.minimum(n, nb - 1), 0))
    return pl.pallas_call(
        body, name="attn_bwd", grid=(nb + 1,),
        in_specs=[pl.BlockSpec(memory_space=pltpu.SMEM), cur(ATTN_WIDTH)] + _kv_specs(nb) + [cur(ATTN_WIDTH)],
        out_specs=[cur(ATTN_WIDTH), pl.BlockSpec((BLOCK, 2 * KV_WIDTH), lambda n: (jnp.maximum(n - 1, 0), 0)),
                   pl.BlockSpec((1, ATTN_HEADS), lambda n: (0, 0))],
        out_shape=[jax.ShapeDtypeStruct((tp, ATTN_WIDTH), BF16), jax.ShapeDtypeStruct((tp, 2 * KV_WIDTH), BF16),
                   jax.ShapeDtypeStruct((1, ATTN_HEADS), F32)],
        scratch_shapes=[pltpu.VMEM((BLOCK, 2 * KV_WIDTH), F32)],
        compiler_params=_params("arbitrary"),
    )(sinks, qkv, qkv, qkv, qkv, qkv, dattn)


def _inproj_bwd(dq, dkv, dxr, dyr, w_in, u0, h0, dh1, g):
    tp = h0.shape[0]
    tr = _row_tile(tp)

    def body(dq_ref, dkv_ref, dxr_ref, dyr_ref, w_ref, u_ref, h_ref, dh1_ref, g_ref, dh0_ref, dw_ref, dg_ref):
        i = pl.program_id(0)
        dz = jnp.concatenate([dq_ref[...], dkv_ref[...], dxr_ref[...], dyr_ref[...]], axis=1)
        du = _dot_nt(dz, w_ref[...])
        hhat, rs = _rms(h_ref[...])
        dx, dg = _rms_bwd(hhat, rs, g_ref[...], du)
        dh0_ref[...] = dh1_ref[...] + dx
        pw = _dot_tn(u_ref[...], dz)

        @pl.when(i == 0)
        def _():
            dw_ref[...] = pw
            dg_ref[...] = dg

        @pl.when(i > 0)
        def _():
            dw_ref[...] += pw
            dg_ref[...] += dg

    row = lambda w: pl.BlockSpec((tr, w), lambda i: (i, 0))
    full = lambda shape: pl.BlockSpec(shape, lambda i: (0,) * len(shape))
    return pl.pallas_call(
        body, name="inproj_bwd", grid=(tp // tr,),
        in_specs=[row(ATTN_WIDTH), row(2 * KV_WIDTH), row(LRU_WIDTH), row(LRU_WIDTH), full(w_in.shape),
                  row(D_MODEL), row(D_MODEL), row(D_MODEL), full(g.shape)],
        out_specs=[row(D_MODEL), full((D_MODEL, IN_WIDTH)), full((1, D_MODEL))],
        out_shape=[jax.ShapeDtypeStruct((tp, D_MODEL), F32), jax.ShapeDtypeStruct((D_MODEL, IN_WIDTH), F32),
                   jax.ShapeDtypeStruct((1, D_MODEL), F32)],
        compiler_params=_params("arbitrary"),
    )(dq, dkv, dxr, dyr, w_in, u0, h0, dh1, g)


def _dense_block_diag(w):
    eye = jnp.eye(LRU_BLOCKS, dtype=w.dtype)
    return (w[:, :, None, :] * eye[:, None, :, None]).reshape(LRU_WIDTH, LRU_WIDTH)


def _diag_blocks(dense):
    d4 = dense.reshape(LRU_BLOCKS, LRU_BLOCK, LRU_BLOCKS, LRU_BLOCK)
    return jnp.stack([d4[n, :, n, :] for n in range(LRU_BLOCKS)])


def _local_step(h0, tgt, g_pre_mix, w_in, conv_w, conv_b, w_a, b_a, w_x, b_x, lam, sinks, w_out, g_post_mix,
                g_pre_ffn, w1, w2, g_post_ffn):
    wa = _dense_block_diag(w_a).astype(BF16)
    wx = _dense_block_diag(w_x).astype(BF16)
    sinks1 = sinks.reshape(ATTN_HEADS)

    u0, qkv, xr, yr = _inproj_fwd(h0, g_pre_mix, w_in)
    attn = _attn_fwd(qkv, sinks1)
    hr, rec = _lru_fwd(xr, yr, conv_w, conv_b, wa, b_a, wx, b_x, lam)
    mix, h1, u1 = _outproj_fwd(attn, rec, w_out, h0, g_post_mix, g_pre_ffn)
    r1, dy, df2, loss, dg_post_ffn = _ffn_fwd(u1, w1, w2, h1, tgt, g_post_ffn)

    da1, dh1, dmix, dg_pre_ffn, dg_post_mix = _ffn_bwd_data(df2, r1, w1, w2, dy, h1, mix, g_pre_ffn, g_post_mix)
    dw1, dw2 = _ffn_bwd_weights(u1, da1, r1, df2)
    dattn, drec, dw_out = _outproj_bwd(dmix, w_out, attn, rec)
    dxr, dyr, dwa, dwx, vec = _lru_bwd(xr, yr, hr, drec, conv_w, conv_b, wa, b_a, wx, b_x, lam)
    dq, dkv, dsinks = _attn_bwd(qkv, dattn, sinks1)
    dh0, dw_in, dg_pre_mix = _inproj_bwd(dq, dkv, dxr, dyr, w_in, u0, h0, dh1, g_pre_mix)

    grads = dict(
        g_pre_mix=dg_pre_mix, w_in=dw_in, conv_w=vec[0:4], conv_b=vec[4:5], w_a=_diag_blocks(dwa), b_a=vec[5:6],
        w_x=_diag_blocks(dwx), b_x=vec[6:7], lru_lambda=vec[7:8], attn_sinks=dsinks, w_out=dw_out,
        g_post_mix=dg_post_mix, g_pre_ffn=dg_pre_ffn, w_ff1=dw1, w_ff2=dw2, g_post_ffn=dg_post_ffn)
    return loss, dh0, grads


HBM = pl.BlockSpec(memory_space=pltpu.HBM)


def _mesh_pos():
    return lax.axis_index("x"), lax.axis_index("y"), lax.axis_index("c")


def _other_chips(x, y):
    return [(1 - x, y), (x, 1 - y), (1 - x, 1 - y)]


def _remote(src, dst, send_sem, recv_sem, to):
    return pltpu.make_async_remote_copy(src_ref=src, dst_ref=dst, send_sem=send_sem, recv_sem=recv_sem,
                                        device_id=to, device_id_type=MESH)


def _gather_weights(shards, tiny):
    nbig = len(shards)

    def body(*refs):
        srcs, tiny_src = refs[:nbig], refs[nbig]
        outs, tiny_out = refs[nbig + 1:2 * nbig + 1], refs[2 * nbig + 1]
        local_sems, ici_send, ici_recv, d2d_send, d2d_recv, tiny_send, tiny_recv = refs[2 * nbig + 2:]
        x, y, c = _mesh_pos()
        me = 2 * x + y
        chips = _other_chips(x, y)
        sibling = (x, y, 1 - c)
        started = []
        for w, (src, out) in enumerate(zip(srcs, outs)):
            lc = pltpu.make_async_copy(src, out.at[me], local_sems.at[w])
            lc.start()
            started.append(lc)
        lt = pltpu.make_async_copy(tiny_src, tiny_out.at[me], local_sems.at[nbig])
        lt.start()
        sends = []
        for w, (src, out) in enumerate(zip(srcs, outs)):
            hr = src.shape[0] // 2
            for j, chip in enumerate(chips):
                k = 3 * w + j
                cp = _remote(src.at[pl.ds(c * hr, hr)], out.at[me, pl.ds(c * hr, hr)],
                             ici_send.at[k], ici_recv.at[k], (*chip, c))
                cp.start()
                sends.append(cp)
        for j, chip in enumerate(chips):
            cp = _remote(tiny_src, tiny_out.at[me], tiny_send.at[j], tiny_recv.at[j], (*chip, c))
            cp.start()
            sends.append(cp)
        for w, (src, out) in enumerate(zip(srcs, outs)):
            hr = src.shape[0] // 2
            for j, (px, py) in enumerate(chips):
                k = 3 * w + j
                landed = out.at[2 * px + py, pl.ds(c * hr, hr)]
                _remote(landed, landed, ici_send.at[k], ici_recv.at[k], sibling).wait_recv()
                cp = _remote(landed, landed, d2d_send.at[k], d2d_recv.at[k], sibling)
                cp.start()
                sends.append(cp)
        for w, (src, out) in enumerate(zip(srcs, outs)):
            hr = src.shape[0] // 2
            for j, (px, py) in enumerate(chips):
                k = 3 * w + j
                other = out.at[2 * px + py, pl.ds((1 - c) * hr, hr)]
                _remote(other, other, d2d_send.at[k], d2d_recv.at[k], sibling).wait_recv()
        for j, (px, py) in enumerate(chips):
            blk = tiny_out.at[2 * px + py]
            _remote(blk, blk, tiny_send.at[j], tiny_recv.at[j], sibling).wait_recv()
        for cp in sends:
            cp.wait_send()
        for lc in started:
            lc.wait()
        lt.wait()

    out_shape = [jax.ShapeDtypeStruct((N_CHIPS,) + s.shape, s.dtype) for s in shards]
    out_shape.append(jax.ShapeDtypeStruct((N_CHIPS,) + tiny.shape, tiny.dtype))
    n = 3 * nbig
    return pl.pallas_call(
        body, name="gather_weights", out_shape=out_shape,
        in_specs=[HBM] * (nbig + 1), out_specs=[HBM] * (nbig + 1),
        scratch_shapes=[pltpu.SemaphoreType.DMA((nbig + 1,)), pltpu.SemaphoreType.DMA((n,)),
                        pltpu.SemaphoreType.DMA((n,)), pltpu.SemaphoreType.DMA((n,)), pltpu.SemaphoreType.DMA((n,)),
                        pltpu.SemaphoreType.DMA((3,)), pltpu.SemaphoreType.DMA((3,))],
    )(*shards, tiny)


N_DEV = 8


def _gather_small(block):
    m_per, n = block.shape

    def body(x_ref, out_ref, send_sems, recv_sems, local_sem):
        x, y, c = _mesh_pos()
        me, sibling = (x, y, c), (x, y, 1 - c)
        chips = _other_chips(x, y)

        def rows(px, py, pc):
            return out_ref.at[pl.ds((4 * px + 2 * py + pc) * m_per, m_per), :]

        def copy(k, block_of, to, src=None):
            return _remote(rows(*block_of) if src is None else src, rows(*block_of),
                           send_sems.at[k], recv_sems.at[k], to)

        mine = pltpu.make_async_copy(x_ref, rows(*me), local_sem)
        mine.start()
        first = [copy(0, me, sibling, src=x_ref)]
        first += [copy(1 + j, me, (*chip, c), src=x_ref) for j, chip in enumerate(chips)]
        for cp in first:
            cp.start()
        passed = [copy(4 + j, (*chip, c), sibling) for j, chip in enumerate(chips)]
        for j, chip in enumerate(chips):
            copy(1 + j, (*chip, c), me).wait_recv()
            passed[j].start()
        copy(0, sibling, me).wait_recv()
        for j, chip in enumerate(chips):
            copy(4 + j, (*chip, 1 - c), me).wait_recv()
        for cp in first + passed:
            cp.wait_send()
        mine.wait()

    return pl.pallas_call(
        body, name="gather_small", out_shape=jax.ShapeDtypeStruct((N_DEV * m_per, n), block.dtype),
        in_specs=[pl.BlockSpec(memory_space=pltpu.VMEM)], out_specs=pl.BlockSpec(memory_space=pltpu.VMEM),
        scratch_shapes=[pltpu.SemaphoreType.DMA((7,)), pltpu.SemaphoreType.DMA((7,)), pltpu.SemaphoreType.DMA],
        compiler_params=pltpu.CompilerParams(vmem_limit_bytes=VMEM_LIMIT_V7X),
    )(block)


def _sibling_exchange(parts):
    def body(*refs):
        n = len(parts)
        srcs, outs, send_sems, recv_sems = refs[:n], refs[n:2 * n], refs[2 * n], refs[2 * n + 1]
        x, y, c = _mesh_pos()
        sibling = (x, y, 1 - c)
        cps = []
        for w, (src, out) in enumerate(zip(srcs, outs)):
            hr = src.shape[1] // 2
            cp = _remote(src.at[:, pl.ds((1 - c) * hr, hr)], out, send_sems.at[w], recv_sems.at[w], sibling)
            cp.start()
            cps.append(cp)
        for cp in cps:
            cp.wait()

    n = len(parts)
    return pl.pallas_call(
        body, name="sibling_exchange",
        out_shape=[jax.ShapeDtypeStruct((p.shape[0], p.shape[1] // 2, p.shape[2]), p.dtype) for p in parts],
        in_specs=[HBM] * n, out_specs=[HBM] * n,
        scratch_shapes=[pltpu.SemaphoreType.DMA((n,)), pltpu.SemaphoreType.DMA((n,))],
    )(*parts)


def _chip_presum(part, from_sibling, c):
    _, hr, cols = from_sibling.shape
    tr = 256 if hr % 256 == 0 else hr
    steps = hr // tr

    def body(c_ref, a_ref, b_ref, o_ref):
        o_ref[...] = (a_ref[...] + b_ref[...]).astype(BF16)

    return pl.pallas_call(
        body, name="chip_presum",
        grid_spec=pltpu.PrefetchScalarGridSpec(
            num_scalar_prefetch=1, grid=(N_CHIPS, steps),
            in_specs=[pl.BlockSpec((1, tr, cols), lambda j, i, c_ref: (j, c_ref[0] * steps + i, 0)),
                      pl.BlockSpec((1, tr, cols), lambda j, i, c_ref: (j, i, 0))],
            out_specs=pl.BlockSpec((1, tr, cols), lambda j, i, c_ref: (j, i, 0))),
        out_shape=jax.ShapeDtypeStruct(from_sibling.shape, BF16),
        compiler_params=_params("parallel", "parallel"),
    )(c, part, from_sibling)


def _scatter_partials(cparts):
    nw = len(cparts)

    def body(*refs):
        srcs, outs = refs[:nw], refs[nw:2 * nw]
        local_sems, own_send, own_recv, ici_send, ici_recv, d2d_send, d2d_recv = refs[2 * nw:]
        x, y, c = _mesh_pos()
        me = 2 * x + y
        chips = _other_chips(x, y)
        sibling = (x, y, 1 - c)
        locals_, sends = [], []
        for w, (src, out) in enumerate(zip(srcs, outs)):
            hr = src.shape[1]
            mine = out.at[me, pl.ds(c * hr, hr)]
            lc = pltpu.make_async_copy(src.at[me], mine, local_sems.at[w])
            lc.start()
            locals_.append(lc)
            cp = _remote(src.at[me], mine, own_send.at[w], own_recv.at[w], sibling)
            cp.start()
            sends.append(cp)
            for j, (px, py) in enumerate(chips):
                k = 3 * w + j
                cp = _remote(src.at[2 * px + py], mine, ici_send.at[k], ici_recv.at[k], (px, py, c))
                cp.start()
                sends.append(cp)
        for w, (src, out) in enumerate(zip(srcs, outs)):
            hr = src.shape[1]
            for j, (px, py) in enumerate(chips):
                k = 3 * w + j
                landed = out.at[2 * px + py, pl.ds(c * hr, hr)]
                _remote(landed, landed, ici_send.at[k], ici_recv.at[k], sibling).wait_recv()
                cp = _remote(landed, landed, d2d_send.at[k], d2d_recv.at[k], sibling)
                cp.start()
                sends.append(cp)
        for w, (src, out) in enumerate(zip(srcs, outs)):
            hr = src.shape[1]
            other = out.at[me, pl.ds((1 - c) * hr, hr)]
            _remote(other, other, own_send.at[w], own_recv.at[w], sibling).wait_recv()
            for j, (px, py) in enumerate(chips):
                k = 3 * w + j
                other = out.at[2 * px + py, pl.ds((1 - c) * hr, hr)]
                _remote(other, other, d2d_send.at[k], d2d_recv.at[k], sibling).wait_recv()
        for cp in sends:
            cp.wait_send()
        for lc in locals_:
            lc.wait()

    n = 3 * nw
    dma = pltpu.SemaphoreType.DMA
    return pl.pallas_call(
        body, name="scatter_partials",
        out_shape=[jax.ShapeDtypeStruct((N_CHIPS, 2 * p.shape[1], p.shape[2]), p.dtype) for p in cparts],
        in_specs=[HBM] * nw, out_specs=[HBM] * nw,
        scratch_shapes=[dma((nw,)), dma((nw,)), dma((nw,)), dma((n,)), dma((n,)), dma((n,)), dma((n,))],
    )(*cparts)


def _adamw(w, g, m, v):
    m = ADAM_B1 * m + (1.0 - ADAM_B1) * g
    v = ADAM_B2 * v + (1.0 - ADAM_B2) * (g * g)
    m_hat = m / (1.0 - ADAM_B1 ** ADAM_STEP)
    v_hat = v / (1.0 - ADAM_B2 ** ADAM_STEP)
    delta = -ADAM_LR * (m_hat / (jnp.sqrt(v_hat) + ADAM_EPS) + ADAM_WD * w)
    return delta, m, v


def _adamw_big(partials, w, m, v):
    rows, cols = w.shape
    tr = 256

    def body(p_ref, w_ref, m_ref, v_ref, g_ref, d_ref, m2_ref, v2_ref):
        g = ((p_ref[0].astype(F32) + p_ref[1].astype(F32)) + p_ref[2].astype(F32)) + p_ref[3].astype(F32)
        g_ref[...] = g
        d_ref[...], m2_ref[...], v2_ref[...] = _adamw(w_ref[...], g, m_ref[...], v_ref[...])

    blk = pl.BlockSpec((tr, cols), lambda i: (i, 0))
    return pl.pallas_call(
        body, name="adamw_big", grid=(rows // tr,),
        in_specs=[pl.BlockSpec((N_CHIPS, tr, cols), lambda i: (0, i, 0)), blk, blk, blk],
        out_specs=[blk] * 4, out_shape=[jax.ShapeDtypeStruct((rows, cols), F32)] * 4,
        compiler_params=_params("parallel"),
    )(partials, w, m, v)


def _sum_devices(gathered, rows):
    cols = gathered.shape[1]

    def body(g_ref, o_ref):
        acc = g_ref[0:rows]
        for d in range(1, N_DEV):
            acc = acc + g_ref[d * rows:(d + 1) * rows]
        o_ref[...] = acc

    return pl.pallas_call(
        body, name="sum_devices", out_shape=jax.ShapeDtypeStruct((rows, cols), F32),
        in_specs=[pl.BlockSpec(memory_space=pltpu.VMEM)], out_specs=pl.BlockSpec(memory_space=pltpu.VMEM),
        compiler_params=pltpu.CompilerParams(vmem_limit_bytes=VMEM_LIMIT_V7X),
    )(gathered)


def _adamw_small(quads):
    n = len(quads)

    def body(*refs):
        ins, outs = refs[:4 * n], refs[4 * n:]
        for t in range(n):
            w, g, m, v = (r[...] for r in ins[4 * t:4 * t + 4])
            outs[3 * t][...], outs[3 * t + 1][...], outs[3 * t + 2][...] = _adamw(w, g, m, v)

    flat = [a for q in quads for a in q]
    vm = pl.BlockSpec(memory_space=pltpu.VMEM)
    res = pl.pallas_call(
        body, name="adamw_small",
        out_shape=[jax.ShapeDtypeStruct(q[0].shape, F32) for q in quads for _ in range(3)],
        in_specs=[vm] * (4 * n), out_specs=[vm] * (3 * n),
    )(*flat)
    return [tuple(res[3 * t:3 * t + 3]) for t in range(n)]


SMALL_PACK_ROWS = 96
_WEIGHTS = ['meta_tokens', 'g_pre_mix', 'w_in', 'conv_w', 'conv_b', 'w_a', 'b_a', 'w_x', 'b_x', 'lru_lambda',
            'attn_sinks', 'w_out', 'g_post_mix', 'g_pre_ffn', 'w_ff1', 'w_ff2', 'g_post_ffn']
_BIG = ['w_in', 'w_out', 'w_ff1', 'w_ff2']


def _pack_small(dmeta, g):
    z = lambda r, c: jnp.zeros((r, c), F32)
    rows = [
        dmeta,
        g['g_pre_mix'], g['g_post_mix'], g['g_pre_ffn'], g['g_post_ffn'],
        jnp.concatenate([g['conv_w'], z(4, 512)], axis=1),
        jnp.concatenate([g['conv_b'], g['b_a']], axis=1),
        jnp.concatenate([g['b_x'], g['lru_lambda']], axis=1),
        jnp.concatenate([g['attn_sinks'], z(1, D_MODEL - ATTN_HEADS)], axis=1),
        z(5, D_MODEL),
        g['w_a'].reshape(32, D_MODEL), g['w_x'].reshape(32, D_MODEL),
    ]
    return jnp.concatenate(rows, axis=0)


def _unpack_small(s, chip):
    return dict(
        meta_tokens=lax.dynamic_slice(s[0:16], (0, chip * 256), (16, 256)),
        g_pre_mix=s[16:17], g_post_mix=s[17:18], g_pre_ffn=s[18:19], g_post_ffn=s[19:20],
        conv_w=lax.dynamic_slice(s[20:24], (0, chip * 128), (4, 128)).reshape(1, 4, 128),
        conv_b=s[24:25, :512], b_a=s[24:25, 512:], b_x=s[25:26, :512], lru_lambda=s[25:26, 512:],
        attn_sinks=s[26:27, :ATTN_HEADS],
        w_a=s[32:64].reshape(1, LRU_BLOCKS, LRU_BLOCK, LRU_BLOCK),
        w_x=s[64:96].reshape(1, LRU_BLOCKS, LRU_BLOCK, LRU_BLOCK))


def _as2d(a):
    if a.ndim == 2:
        return a
    return a.reshape(-1, a.shape[-1])


def kernel(x, meta_tokens, g_pre_mix, w_in, conv_w, conv_b, w_a, b_a, w_x, b_x, lru_lambda, attn_sinks, w_out, g_post_mix, g_pre_ffn, w_ff1, w_ff2, g_post_ffn, loss_target, m_meta_tokens, m_g_pre_mix, m_w_in, m_conv_w, m_conv_b, m_w_a, m_b_a, m_w_x, m_b_x, m_lru_lambda, m_attn_sinks, m_w_out, m_g_post_mix, m_g_pre_ffn, m_w_ff1, m_w_ff2, m_g_post_ffn, v_meta_tokens, v_g_pre_mix, v_w_in, v_conv_w, v_conv_b, v_w_a, v_b_a, v_w_x, v_b_x, v_lru_lambda, v_attn_sinks, v_w_out, v_g_post_mix, v_g_pre_ffn, v_w_ff1, v_w_ff2, v_g_post_ffn):
    weights = dict(meta_tokens=meta_tokens, g_pre_mix=g_pre_mix, w_in=w_in, conv_w=conv_w, conv_b=conv_b, w_a=w_a,
                   b_a=b_a, w_x=w_x, b_x=b_x, lru_lambda=lru_lambda, attn_sinks=attn_sinks, w_out=w_out,
                   g_post_mix=g_post_mix, g_pre_ffn=g_pre_ffn, w_ff1=w_ff1, w_ff2=w_ff2, g_post_ffn=g_post_ffn)
    mom1 = dict(zip(_WEIGHTS, [m_meta_tokens, m_g_pre_mix, m_w_in, m_conv_w, m_conv_b, m_w_a, m_b_a, m_w_x, m_b_x,
                               m_lru_lambda, m_attn_sinks, m_w_out, m_g_post_mix, m_g_pre_ffn, m_w_ff1, m_w_ff2,
                               m_g_post_ffn]))
    mom2 = dict(zip(_WEIGHTS, [v_meta_tokens, v_g_pre_mix, v_w_in, v_conv_w, v_conv_b, v_w_a, v_b_a, v_w_x, v_b_x,
                               v_lru_lambda, v_attn_sinks, v_w_out, v_g_post_mix, v_g_pre_ffn, v_w_ff1, v_w_ff2,
                               v_g_post_ffn]))
    xi, yi, ci = _mesh_pos()
    chip = 2 * xi + yi

    tiny = jnp.concatenate([meta_tokens, jnp.pad(conv_w[0], ((0, 4), (0, 128)))], axis=0)
    g_in, g_out, g_f1, g_f2, g_tiny = _gather_weights(
        [w_in[0].astype(BF16), w_out[0].astype(BF16), w_ff1[0].astype(BF16), w_ff2[0].astype(BF16)], tiny)
    w_in_full = jnp.concatenate([g_in[j] for j in range(N_CHIPS)], axis=1)
    w_out_full = g_out.reshape(D_MODEL, D_MODEL)
    meta_full = jnp.concatenate([g_tiny[j, :N_META] for j in range(N_CHIPS)], axis=1)
    conv_w_full = jnp.concatenate([g_tiny[j, N_META:N_META + 4, :128] for j in range(N_CHIPS)], axis=1)

    h0 = jnp.concatenate([jnp.zeros((PAD_ROWS, D_MODEL), F32), meta_full, x[0]], axis=0)
    tgt = jnp.concatenate([jnp.zeros((BLOCK, D_MODEL), F32), loss_target[0]], axis=0)
    loss, dh0, grads = _local_step(h0, tgt, g_pre_mix, w_in_full, conv_w_full, conv_b, w_a[0], b_a, w_x[0], b_x,
                                   lru_lambda, attn_sinks, w_out_full, g_post_mix, g_pre_ffn, g_f1, g_f2, g_post_ffn)
    loss = lax.psum(loss[0, 0], ("x", "y", "c"))
    grad_x = dh0[BLOCK:][None]

    gathered = _gather_small(_pack_small(dh0[PAD_ROWS:BLOCK], grads))
    small = _unpack_small(_sum_devices(gathered, SMALL_PACK_ROWS), chip)

    dw_in = grads['w_in']
    parts = [jnp.stack([dw_in[:, j * 448:(j + 1) * 448] for j in range(N_CHIPS)]),
             grads['w_out'].reshape(N_CHIPS, D_MODEL // N_CHIPS, D_MODEL), grads['w_ff1'], grads['w_ff2']]
    from_sibling = _sibling_exchange(parts)
    c_arr = jnp.reshape(ci, (1,)).astype(jnp.int32)
    cparts = [_chip_presum(p, r, c_arr) for p, r in zip(parts, from_sibling)]
    chip_partials = _scatter_partials(cparts)

    g_out_d, delta, new_m, new_v = {}, {}, {}, {}
    for name, part in zip(_BIG, chip_partials):
        shp = weights[name].shape
        res = _adamw_big(part, weights[name][0], mom1[name][0], mom2[name][0])
        g_out_d[name], delta[name], new_m[name], new_v[name] = (r.reshape(shp) for r in res)
    small_names = [n for n in _WEIGHTS if n not in _BIG]
    quads = [(_as2d(weights[n]), _as2d(small[n]), _as2d(mom1[n]), _as2d(mom2[n])) for n in small_names]
    for name, (d, m2, v2) in zip(small_names, _adamw_small(quads)):
        shp = weights[name].shape
        g_out_d[name] = small[name].reshape(shp)
        delta[name], new_m[name], new_v[name] = d.reshape(shp), m2.reshape(shp), v2.reshape(shp)

    return (loss, grad_x, *[g_out_d[n] for n in _WEIGHTS], *[delta[n] for n in _WEIGHTS],
            *[new_m[n] for n in _WEIGHTS], *[new_v[n] for n in _WEIGHTS])
```

```python
import numpy as np
import jax
import jax.numpy as jnp
from jax import lax
from jax.experimental import pallas as pl
from jax.experimental.pallas import tpu as pltpu

F32 = jnp.float32
BF16 = jnp.bfloat16

D_MODEL = 1024
N_META = 16
BLOCK = 128
PAD_ROWS = BLOCK - N_META
HEAD_DIM = 64
ATTN_HEADS = 8
GQA_GROUP = 4
ATTN_WIDTH = 512
KV_WIDTH = 128
QKV_WIDTH = ATTN_WIDTH + 2 * KV_WIDTH
LRU_WIDTH = 512
LRU_BLOCKS = 8
LRU_BLOCK = 64
LRU_C = 8.0
IN_WIDTH = 1792
D_FF = 4096
N_CHIPS = 4
FF_CHUNK = D_FF // N_CHIPS
EPS = 1e-6
NEG = -1e30

ADAM_LR = 0.001
ADAM_B1 = 0.9
ADAM_B2 = 0.999
ADAM_EPS = 1e-08
ADAM_WD = 0.01
ADAM_STEP = 10

VMEM_LIMIT_V7X = 56 * 1024 * 1024
MESH = pl.DeviceIdType.MESH

NT = (((1,), (1,)), ((), ()))
TN = (((0,), (0,)), ((), ()))


def _row_tile(tp):
    return 640 if tp % 640 == 0 else BLOCK


def _params(*sem):
    return pltpu.CompilerParams(dimension_semantics=sem, vmem_limit_bytes=VMEM_LIMIT_V7X)


def _dot(a, b):
    return jnp.dot(a, b, preferred_element_type=F32)


def _dot_nt(a, b):
    return lax.dot_general(a, b, NT, preferred_element_type=F32)


def _dot_tn(a, b):
    return lax.dot_general(a, b, TN, preferred_element_type=F32)


def _rms(x):
    rs = lax.rsqrt(jnp.mean(x * x, axis=-1, keepdims=True) + EPS)
    return x * rs, rs


def _rms_bwd(xhat, rs, g, dy):
    dyg = dy * g
    dx = rs * (dyg - xhat * jnp.mean(dyg * xhat, axis=-1, keepdims=True))
    dg = jnp.sum(dy * xhat, axis=0, keepdims=True)
    return dx, dg


def _gelu(x):
    k = 0.7978845608028654
    t = jnp.tanh(k * (x + 0.044715 * x * x * x))
    return 0.5 * x * (1.0 + t), t


def _gelu_grad(x, t):
    k = 0.7978845608028654
    return 0.5 * (1.0 + t) + 0.5 * x * (1.0 - t * t) * k * (1.0 + 3 * 0.044715 * x * x)


def _sigmoid(x):
    return 1.0 / (1.0 + jnp.exp(-x))


def _neg_expm1(x):
    series = x * (1.0 + x * 0.5 * (1.0 + x * (1.0 / 3.0) * (1.0 + x * 0.25 * (1.0 + x * 0.2))))
    return -jnp.where(jnp.abs(x) < 0.05, series, jnp.exp(x) - 1.0)


def _softplus(x):
    return jnp.maximum(x, 0.0) + jnp.log1p(jnp.exp(-jnp.abs(x)))


def _seq_specs(tr):
    qb = tr // BLOCK
    return [pl.BlockSpec((BLOCK, D_MODEL), lambda i, *_, s=s: (jnp.maximum(i * qb + s - 1, 0), 0)) for s in range(qb)]


def _seq_tile(head, pieces, i):
    first = jnp.where(i == 0, head, pieces[0][...])
    return jnp.concatenate([first] + [p[...] for p in pieces[1:]], axis=0)


def _inproj_fwd(head, x, g, w_in):
    tp = BLOCK + x.shape[0]
    tr = _row_tile(tp)
    qb = tr // BLOCK

    def body(*refs):
        head_ref, pieces = refs[0], refs[1:1 + qb]
        g_ref, w_ref, u_ref, qkv_ref, xr_ref, yr_ref = refs[1 + qb:]
        xhat, _ = _rms(_seq_tile(head_ref[...], pieces, pl.program_id(0)))
        u = (xhat * g_ref[...]).astype(BF16)
        u_ref[...] = u
        z = _dot(u, w_ref[...])
        qkv_ref[...] = z[:, :QKV_WIDTH].astype(BF16)
        xr_ref[...] = z[:, QKV_WIDTH:QKV_WIDTH + LRU_WIDTH]
        yr_ref[...] = z[:, QKV_WIDTH + LRU_WIDTH:]

    row = lambda w: pl.BlockSpec((tr, w), lambda i: (i, 0))
    full = lambda a: pl.BlockSpec(a.shape, lambda i: (0,) * a.ndim)
    return pl.pallas_call(
        body, name="inproj_fwd", grid=(tp // tr,),
        in_specs=[full(head)] + _seq_specs(tr) + [full(g), full(w_in)],
        out_specs=[row(D_MODEL), row(QKV_WIDTH), row(LRU_WIDTH), row(LRU_WIDTH)],
        out_shape=[jax.ShapeDtypeStruct((tp, D_MODEL), BF16), jax.ShapeDtypeStruct((tp, QKV_WIDTH), BF16),
                   jax.ShapeDtypeStruct((tp, LRU_WIDTH), F32), jax.ShapeDtypeStruct((tp, LRU_WIDTH), F32)],
        compiler_params=_params("parallel"),
    )(head, *([x] * qb), g, w_in)


GROUP_ROWS = GQA_GROUP * BLOCK


def _attn_bias():
    j = np.arange(2 * BLOCK)[:, None]
    i = np.arange(BLOCK)[None, :]
    band = (j - i >= 1) & (j - i <= BLOCK)
    out = []
    for n in range(3):
        ok = band & ((n - 1) * BLOCK + j >= PAD_ROWS) if n < 2 else band
        out.append(np.tile(np.where(ok, 0.0, NEG).astype(np.float32), (1, GQA_GROUP)))
    return jnp.asarray(np.stack(out))


def _stack_heads(a, g):
    heads = range(GQA_GROUP * g, GQA_GROUP * (g + 1))
    return jnp.concatenate([a[:, h * HEAD_DIM:(h + 1) * HEAD_DIM] for h in heads], axis=0)


def _unstack_heads(groups):
    return jnp.concatenate([p[h * BLOCK:(h + 1) * BLOCK] for p in groups for h in range(GQA_GROUP)], axis=1)


def _attn_probs_t(k_g, qg, bias, sink_row):
    st = _dot_nt(k_g, qg) + bias
    m = jnp.maximum(jnp.max(st, axis=0, keepdims=True), sink_row)
    p = jnp.exp(st - m)
    es = jnp.exp(sink_row - m)
    inv = 1.0 / (jnp.sum(p, axis=0, keepdims=True) + es)
    return p * inv, es * inv


def _attn_consts(sinks):
    return jnp.repeat(sinks.reshape(ATTN_HEADS), BLOCK).reshape(ATTN_HEADS // GQA_GROUP, GROUP_ROWS), _attn_bias()


_SINK_SPEC = pl.BlockSpec((ATTN_HEADS // GQA_GROUP, GROUP_ROWS), lambda n: (0, 0))
_BIAS_SPEC = pl.BlockSpec((3, 2 * BLOCK, GROUP_ROWS), lambda n: (0, 0, 0))
_QSCALE = HEAD_DIM ** -0.5


def _kv_specs(tr):
    qb = tr // BLOCK
    prev = lambda col: pl.BlockSpec((BLOCK, KV_WIDTH), lambda t: (jnp.maximum(t * qb - 1, 0), col))
    cur = lambda col: pl.BlockSpec((tr, KV_WIDTH), lambda t: (t, col))
    return [prev(4), cur(4), prev(5), cur(5)]


def _block_bias(b_ref, t, qb, i):
    return b_ref[2] if i >= 2 else b_ref[jnp.minimum(t * qb + i, 2)]


def _attn_fwd(qkv, sinks):
    tp = qkv.shape[0]
    tr = _row_tile(tp)
    qb = tr // BLOCK
    sink_rows, bias = _attn_consts(sinks)

    def body(s_ref, b_ref, q_ref, kp_ref, kc_ref, vp_ref, vc_ref, o_ref):
        t = pl.program_id(0)
        k_all = jnp.concatenate([kp_ref[...], kc_ref[...]], axis=0)
        v_all = jnp.concatenate([vp_ref[...], vc_ref[...]], axis=0)
        for i in range(qb):
            rows = slice(i * BLOCK, (i + 1) * BLOCK)
            q = q_ref[rows]
            k2, v2 = k_all[i * BLOCK:(i + 2) * BLOCK], v_all[i * BLOCK:(i + 2) * BLOCK]
            bias_n = _block_bias(b_ref, t, qb, i)
            outs = []
            for g in range(ATTN_HEADS // GQA_GROUP):
                cols = slice(g * HEAD_DIM, (g + 1) * HEAD_DIM)
                qg = _stack_heads(q, g) * jnp.asarray(_QSCALE, BF16)
                p, _ = _attn_probs_t(k2[:, cols], qg, bias_n, s_ref[g:g + 1])
                outs.append(_dot_tn(p.astype(BF16), v2[:, cols]))
            o_ref[rows] = _unstack_heads(outs).astype(BF16)

    return pl.pallas_call(
        body, name="attn_fwd", grid=(tp // tr,),
        in_specs=[_SINK_SPEC, _BIAS_SPEC, pl.BlockSpec((tr, ATTN_WIDTH), lambda t: (t, 0))] + _kv_specs(tr),
        out_specs=pl.BlockSpec((tr, ATTN_WIDTH), lambda t: (t, 0)),
        out_shape=jax.ShapeDtypeStruct((tp, ATTN_WIDTH), BF16),
        compiler_params=_params("parallel"),
    )(sink_rows, bias, qkv, qkv, qkv, qkv, qkv)


def _conv_taps(x, halo):
    ext = jnp.concatenate([halo, x], axis=0)
    return [ext[8:] if k == 3 else pltpu.roll(ext, 3 - k, 0)[8:] for k in range(4)]


def _lru_gates(xc, wa, ba, wx, bx, sp):
    xb = xc.astype(BF16)
    r = _sigmoid(_dot(xb, wa) + ba)
    ig = _sigmoid(_dot(xb, wx) + bx)
    log_a = (-LRU_C * sp) * r
    a = jnp.exp(log_a)
    mult = jnp.sqrt(_neg_expm1(2.0 * log_a))
    return xb, r, ig, a, mult


def _scan_fwd(a, b):
    rows = lax.broadcasted_iota(jnp.int32, a.shape, 0)
    d = 1
    while d < a.shape[0]:
        keep = rows >= d
        b = jnp.where(keep, a * pltpu.roll(b, d, 0) + b, b)
        a = jnp.where(keep, a * pltpu.roll(a, d, 0), a)
        d *= 2
    return a, b


def _scan_rev(c, b):
    n = c.shape[0]
    rows = lax.broadcasted_iota(jnp.int32, c.shape, 0)
    d = 1
    while d < n:
        keep = rows < n - d
        b = jnp.where(keep, b + c * pltpu.roll(b, n - d, 0), b)
        c = jnp.where(keep, c * pltpu.roll(c, n - d, 0), c)
        d *= 2
    return c, b


def _lru_fwd(xr, yr, conv_w, conv_b, wa, ba, wx, bx, lam):
    tp = xr.shape[0]
    nb = tp // BLOCK

    def body(xr_ref, yr_ref, cw_ref, cb_ref, wa_ref, ba_ref, wx_ref, bx_ref, lam_ref, hr_ref, rec_ref, halo, hprev):
        n = pl.program_id(0)

        @pl.when(n == 0)
        def _():
            halo[...] = jnp.zeros_like(halo)
            hprev[...] = jnp.zeros_like(hprev)

        x = xr_ref[...]
        taps = _conv_taps(x, halo[...])
        halo[...] = x[BLOCK - 8:]
        cw = cw_ref[...]
        xc = cb_ref[...] + sum(cw[k:k + 1] * taps[k] for k in range(4))
        sp = _softplus(-lam_ref[...])
        _, _, ig, a, mult = _lru_gates(xc, wa_ref[...], ba_ref[...], wx_ref[...], bx_ref[...], sp)
        rows = n * BLOCK + lax.broadcasted_iota(jnp.int32, xc.shape, 0)
        u = jnp.where(rows >= PAD_ROWS, mult * (ig * xc), 0.0)
        acum, hloc = _scan_fwd(a, u)
        h = acum * hprev[0:1] + hloc
        hprev[0:1] = h[BLOCK - 1:]
        hr_ref[...] = h
        gl, _ = _gelu(yr_ref[...])
        rec_ref[...] = (gl * h).astype(BF16)

    blk = pl.BlockSpec((BLOCK, LRU_WIDTH), lambda n: (n, 0))
    full = lambda a: pl.BlockSpec(a.shape, lambda n: (0,) * a.ndim)
    small = [conv_w, conv_b, wa, ba, wx, bx, lam]
    return pl.pallas_call(
        body, name="lru_fwd", grid=(nb,),
        in_specs=[blk, blk] + [full(a) for a in small],
        out_specs=[blk, blk],
        out_shape=[jax.ShapeDtypeStruct((tp, LRU_WIDTH), F32), jax.ShapeDtypeStruct((tp, LRU_WIDTH), BF16)],
        scratch_shapes=[pltpu.VMEM((8, LRU_WIDTH), F32), pltpu.VMEM((8, LRU_WIDTH), F32)],
        compiler_params=_params("arbitrary"),
    )(xr, yr, *small)


def _outproj_fwd(attn, rec, w_out, head, x, g_post_mix, g_pre_ffn):
    tp = attn.shape[0]
    tr = _row_tile(tp)
    qb = tr // BLOCK

    def body(*refs):
        a_ref, r_ref, w_ref, head_ref = refs[:4]
        pieces = refs[4:4 + qb]
        gm_ref, gf_ref, mix_ref, h1_ref, u1_ref = refs[4 + qb:]
        mix = _dot(a_ref[...], w_ref[:ATTN_WIDTH]) + _dot(r_ref[...], w_ref[ATTN_WIDTH:])
        mix_ref[...] = mix
        mhat, _ = _rms(mix)
        h1 = _seq_tile(head_ref[...], pieces, pl.program_id(0)) + mhat * gm_ref[...]
        h1_ref[...] = h1
        hhat, _ = _rms(h1)
        u1_ref[...] = (hhat * gf_ref[...]).astype(BF16)

    row = lambda w: pl.BlockSpec((tr, w), lambda i: (i, 0))
    full = lambda a: pl.BlockSpec(a.shape, lambda i: (0,) * a.ndim)
    return pl.pallas_call(
        body, name="outproj_fwd", grid=(tp // tr,),
        in_specs=[row(ATTN_WIDTH), row(LRU_WIDTH), full(w_out), full(head)] + _seq_specs(tr)
        + [full(g_post_mix), full(g_pre_ffn)],
        out_specs=[row(D_MODEL), row(D_MODEL), row(D_MODEL)],
        out_shape=[jax.ShapeDtypeStruct((tp, D_MODEL), F32), jax.ShapeDtypeStruct((tp, D_MODEL), F32),
                   jax.ShapeDtypeStruct((tp, D_MODEL), BF16)],
        compiler_params=_params("parallel"),
    )(attn, rec, w_out, head, *([x] * qb), g_post_mix, g_pre_ffn)


def _ffn_fwd(u1, w1, w2, h1, tgt, g_post_ffn):
    tp = h1.shape[0]
    tr = _row_tile(tp)
    qb = tr // BLOCK

    def body(*refs):
        u_ref, w1_ref, w2_ref, h1_ref = refs[:4]
        t_pieces = refs[4:4 + qb]
        g_ref, r1_ref, dy_ref, df2_ref, loss_ref, dg_ref, acc = refs[4 + qb:]
        i, c = pl.program_id(0), pl.program_id(1)

        @pl.when((i == 0) & (c == 0))
        def _():
            loss_ref[...] = jnp.zeros_like(loss_ref)
            dg_ref[...] = jnp.zeros_like(dg_ref)

        r = jnp.maximum(_dot(u_ref[...], w1_ref[0]), 0.0)
        r1_ref[...] = r.astype(BF16)
        part = _dot((r * r).astype(BF16), w2_ref[0])

        @pl.when(c == 0)
        def _():
            acc[...] = part

        @pl.when(c > 0)
        def _():
            acc[...] += part

        @pl.when(c == N_CHIPS - 1)
        def _():
            g = g_ref[...]
            fhat, rs = _rms(acc[...])
            h2 = h1_ref[...] + fhat * g
            rows = i * tr + lax.broadcasted_iota(jnp.int32, h2.shape, 0)
            tgt_tile = jnp.concatenate([p[...] for p in t_pieces], axis=0)
            err = jnp.where(rows >= BLOCK, h2 - tgt_tile, 0.0)
            dy = err * (1.0 / D_MODEL)
            dy_ref[...] = dy
            loss_ref[...] += (0.5 / D_MODEL) * jnp.sum(err * err)
            df2, dg = _rms_bwd(fhat, rs, g, dy)
            df2_ref[...] = df2.astype(BF16)
            dg_ref[...] += dg

    row = pl.BlockSpec((tr, D_MODEL), lambda i, c: (i, 0))
    full = lambda a: pl.BlockSpec(a.shape, lambda i, c: (0,) * a.ndim)
    return pl.pallas_call(
        body, name="ffn_fwd", grid=(tp // tr, N_CHIPS),
        in_specs=[row, pl.BlockSpec((1, D_MODEL, FF_CHUNK), lambda i, c: (c, 0, 0)),
                  pl.BlockSpec((1, FF_CHUNK, D_MODEL), lambda i, c: (c, 0, 0)), row] + _seq_specs(tr)
        + [full(g_post_ffn)],
        out_specs=[pl.BlockSpec((tr, FF_CHUNK), lambda i, c: (i, c)), row, row,
                   pl.BlockSpec((1, 1), lambda i, c: (0, 0)), pl.BlockSpec((1, D_MODEL), lambda i, c: (0, 0))],
        out_shape=[jax.ShapeDtypeStruct((tp, D_FF), BF16), jax.ShapeDtypeStruct((tp, D_MODEL), F32),
                   jax.ShapeDtypeStruct((tp, D_MODEL), BF16), jax.ShapeDtypeStruct((1, 1), F32),
                   jax.ShapeDtypeStruct((1, D_MODEL), F32)],
        scratch_shapes=[pltpu.VMEM((tr, D_MODEL), F32)],
        compiler_params=_params("arbitrary", "arbitrary"),
    )(u1, w1, w2, h1, *([tgt] * qb), g_post_ffn)


def _ffn_bwd_data(df2, r1, w1, w2, dy, h1, mix, g_pre_ffn, g_post_mix):
    tp = h1.shape[0]
    tr = _row_tile(tp)

    def body(df2_ref, r1_ref, w1_ref, w2_ref, dy_ref, h1_ref, mix_ref, gf_ref, gm_ref,
             da_ref, dh1_ref, dmix_ref, dgf_ref, dgm_ref, acc):
        i, c = pl.program_id(0), pl.program_id(1)

        @pl.when((i == 0) & (c == 0))
        def _():
            dgf_ref[...] = jnp.zeros_like(dgf_ref)
            dgm_ref[...] = jnp.zeros_like(dgm_ref)

        df = _dot_nt(df2_ref[...], w2_ref[0])
        da = (df * (2.0 * r1_ref[...].astype(F32))).astype(BF16)
        da_ref[...] = da
        part = _dot_nt(da, w1_ref[0])

        @pl.when(c == 0)
        def _():
            acc[...] = part

        @pl.when(c > 0)
        def _():
            acc[...] += part

        @pl.when(c == N_CHIPS - 1)
        def _():
            hhat, rs = _rms(h1_ref[...])
            dx, dgf = _rms_bwd(hhat, rs, gf_ref[...], acc[...])
            dh1 = dy_ref[...] + dx
            dh1_ref[...] = dh1
            dgf_ref[...] += dgf
            mhat, rsm = _rms(mix_ref[...])
            dmix, dgm = _rms_bwd(mhat, rsm, gm_ref[...], dh1)
            dmix_ref[...] = dmix.astype(BF16)
            dgm_ref[...] += dgm

    row = pl.BlockSpec((tr, D_MODEL), lambda i, c: (i, 0))
    chunk = pl.BlockSpec((tr, FF_CHUNK), lambda i, c: (i, c))
    gain = pl.BlockSpec((1, D_MODEL), lambda i, c: (0, 0))
    return pl.pallas_call(
        body, name="ffn_bwd_data", grid=(tp // tr, N_CHIPS),
        in_specs=[row, chunk, pl.BlockSpec((1, D_MODEL, FF_CHUNK), lambda i, c: (c, 0, 0)),
                  pl.BlockSpec((1, FF_CHUNK, D_MODEL), lambda i, c: (c, 0, 0)), row, row, row, gain, gain],
        out_specs=[chunk, row, row, gain, gain],
        out_shape=[jax.ShapeDtypeStruct((tp, D_FF), BF16), jax.ShapeDtypeStruct((tp, D_MODEL), F32),
                   jax.ShapeDtypeStruct((tp, D_MODEL), BF16), jax.ShapeDtypeStruct((1, D_MODEL), F32),
                   jax.ShapeDtypeStruct((1, D_MODEL), F32)],
        scratch_shapes=[pltpu.VMEM((tr, D_MODEL), F32)],
        compiler_params=_params("arbitrary", "arbitrary"),
    )(df2, r1, w1, w2, dy, h1, mix, g_pre_ffn, g_post_mix)


def _ffn_bwd_weights(u1, da1, r1, df2):
    tp = u1.shape[0]
    tr = _row_tile(tp)

    def body(u_ref, da_ref, r1_ref, df2_ref, dw1_ref, dw2_ref):
        i = pl.program_id(1)
        r = r1_ref[...].astype(F32)
        p1 = _dot_tn(u_ref[...], da_ref[...])
        p2 = _dot_tn((r * r).astype(BF16), df2_ref[...])

        @pl.when(i == 0)
        def _():
            dw1_ref[0] = p1
            dw2_ref[0] = p2

        @pl.when(i > 0)
        def _():
            dw1_ref[0] += p1
            dw2_ref[0] += p2

    row = pl.BlockSpec((tr, D_MODEL), lambda c, i: (i, 0))
    chunk = pl.BlockSpec((tr, FF_CHUNK), lambda c, i: (i, c))
    return pl.pallas_call(
        body, name="ffn_bwd_weights", grid=(N_CHIPS, tp // tr),
        in_specs=[row, chunk, chunk, row],
        out_specs=[pl.BlockSpec((1, D_MODEL, FF_CHUNK), lambda c, i: (c, 0, 0)),
                   pl.BlockSpec((1, FF_CHUNK, D_MODEL), lambda c, i: (c, 0, 0))],
        out_shape=[jax.ShapeDtypeStruct((N_CHIPS, D_MODEL, FF_CHUNK), F32),
                   jax.ShapeDtypeStruct((N_CHIPS, FF_CHUNK, D_MODEL), F32)],
        compiler_params=_params("parallel", "arbitrary"),
    )(u1, da1, r1, df2)


def _outproj_bwd(dmix, w_out, attn, rec):
    tp = dmix.shape[0]
    tr = _row_tile(tp)

    def body(dm_ref, w_ref, a_ref, r_ref, da_ref, dr_ref, dw_ref):
        i = pl.program_id(0)
        dm = dm_ref[...]
        dcat = _dot_nt(dm, w_ref[...])
        da_ref[...] = dcat[:, :ATTN_WIDTH].astype(BF16)
        dr_ref[...] = dcat[:, ATTN_WIDTH:]
        pa = _dot_tn(a_ref[...], dm)
        pr = _dot_tn(r_ref[...], dm)

        @pl.when(i == 0)
        def _():
            dw_ref[:ATTN_WIDTH] = pa
            dw_ref[ATTN_WIDTH:] = pr

        @pl.when(i > 0)
        def _():
            dw_ref[:ATTN_WIDTH] += pa
            dw_ref[ATTN_WIDTH:] += pr

    row = lambda w: pl.BlockSpec((tr, w), lambda i: (i, 0))
    full = pl.BlockSpec((D_MODEL, D_MODEL), lambda i: (0, 0))
    return pl.pallas_call(
        body, name="outproj_bwd", grid=(tp // tr,),
        in_specs=[row(D_MODEL), full, row(ATTN_WIDTH), row(LRU_WIDTH)],
        out_specs=[row(ATTN_WIDTH), row(LRU_WIDTH), full],
        out_shape=[jax.ShapeDtypeStruct((tp, ATTN_WIDTH), BF16), jax.ShapeDtypeStruct((tp, LRU_WIDTH), F32),
                   jax.ShapeDtypeStruct((D_MODEL, D_MODEL), F32)],
        compiler_params=_params("arbitrary"),
    )(dmix, w_out, attn, rec)


N_VEC_ROWS = 8


def _lru_bwd(xr, yr, hr, drec, conv_w, conv_b, wa, ba, wx, bx, lam):
    tp = xr.shape[0]
    nb = tp // BLOCK

    def body(xr_ref, xh_ref, yr_ref, hr_ref, hp_ref, dr_ref, cw_ref, cb_ref, wa_ref, ba_ref, wx_ref, bx_ref, lam_ref,
             dxr_ref, dyr_ref, dwa_ref, dwx_ref, vec_ref, g_next, a_next, dxc_next, dsp):
        s = pl.program_id(0)
        n = nb - 1 - s

        @pl.when(s == 0)
        def _():
            g_next[...] = jnp.zeros_like(g_next)
            a_next[...] = jnp.zeros_like(a_next)
            dxc_next[...] = jnp.zeros_like(dxc_next)
            dsp[...] = jnp.zeros_like(dsp)
            dwa_ref[...] = jnp.zeros_like(dwa_ref)
            dwx_ref[...] = jnp.zeros_like(dwx_ref)
            vec_ref[...] = jnp.zeros_like(vec_ref)

        first = n == 0
        x = xr_ref[...]
        taps = _conv_taps(x, jnp.where(first, 0.0, xh_ref[...]))
        cw = cw_ref[...]
        xc = cb_ref[...] + sum(cw[k:k + 1] * taps[k] for k in range(4))
        lam_v = lam_ref[...]
        sp = _softplus(-lam_v)
        wa_m, wx_m = wa_ref[...], wx_ref[...]
        xb, r, ig, a, mult = _lru_gates(xc, wa_m, ba_ref[...], wx_m, bx_ref[...], sp)

        yr_v = yr_ref[...]
        gl, th = _gelu(yr_v)
        h = hr_ref[...]
        drec = dr_ref[...]
        dyr_ref[...] = (drec * h * _gelu_grad(yr_v, th)).astype(BF16)
        dh_direct = drec * gl

        rows = lax.broadcasted_iota(jnp.int32, a.shape, 0)
        a_up = jnp.where(rows == BLOCK - 1, a_next[0:1], pltpu.roll(a, BLOCK - 1, 0))
        cprod, gloc = _scan_rev(a_up, dh_direct)
        g = gloc + cprod * g_next[0:1]
        g_next[0:1] = g[0:1]
        a_next[0:1] = a[0:1]

        real = (n * BLOCK + rows) >= PAD_ROWS
        h_prev = jnp.where(rows == 0, jnp.where(first, 0.0, hp_ref[7:8]), pltpu.roll(h, 1, 0))
        du = jnp.where(real, g, 0.0)
        da = g * h_prev
        dmult = du * (ig * xc)
        dig = du * (mult * xc)
        dxc = du * (mult * ig)
        dlog_a = jnp.where(real, da * a - dmult * (a * a / mult), 0.0)
        dgr = (dlog_a * (-LRU_C * sp)) * (r * (1.0 - r))
        dgi = dig * (ig * (1.0 - ig))
        dsp[0:1] += jnp.sum(dlog_a * (-LRU_C * r), axis=0, keepdims=True)
        dgr_b, dgi_b = dgr.astype(BF16), dgi.astype(BF16)
        dxc = dxc + _dot_nt(dgr_b, wa_m) + _dot_nt(dgi_b, wx_m)
        dwa_ref[...] += _dot_tn(xb, dgr_b)
        dwx_ref[...] += _dot_tn(xb, dgi_b)

        ext = jnp.concatenate([dxc, dxc_next[...]], axis=0)
        up = [ext[:BLOCK] if j == 0 else pltpu.roll(ext, BLOCK + 8 - j, 0)[:BLOCK] for j in range(4)]
        dxr_ref[...] = sum(cw[k:k + 1] * up[3 - k] for k in range(4)).astype(BF16)
        dxc_next[...] = dxc[:8]

        col = lambda v: jnp.sum(v, axis=0, keepdims=True)
        for k in range(4):
            vec_ref[k:k + 1] += col(dxc * taps[k])
        vec_ref[4:5] += col(dxc)
        vec_ref[5:6] += col(dgr)
        vec_ref[6:7] += col(dgi)

        @pl.when(s == nb - 1)
        def _():
            vec_ref[7:8] = dsp[0:1] * (-_sigmoid(-lam_v))

    blk = pl.BlockSpec((BLOCK, LRU_WIDTH), lambda s: (nb - 1 - s, 0))
    rows_before = pl.BlockSpec((8, LRU_WIDTH), lambda s: (jnp.maximum((nb - 1 - s) * (BLOCK // 8) - 1, 0), 0))
    full = lambda a: pl.BlockSpec(a.shape, lambda s: (0,) * a.ndim)
    small = [conv_w, conv_b, wa, ba, wx, bx, lam]
    sq = pl.BlockSpec((LRU_WIDTH, LRU_WIDTH), lambda s: (0, 0))
    return pl.pallas_call(
        body, name="lru_bwd", grid=(nb,),
        in_specs=[blk, rows_before, blk, blk, rows_before, blk] + [full(a) for a in small],
        out_specs=[blk, blk, sq, sq, pl.BlockSpec((N_VEC_ROWS, LRU_WIDTH), lambda s: (0, 0))],
        out_shape=[jax.ShapeDtypeStruct((tp, LRU_WIDTH), BF16), jax.ShapeDtypeStruct((tp, LRU_WIDTH), BF16),
                   jax.ShapeDtypeStruct((LRU_WIDTH, LRU_WIDTH), F32), jax.ShapeDtypeStruct((LRU_WIDTH, LRU_WIDTH), F32),
                   jax.ShapeDtypeStruct((N_VEC_ROWS, LRU_WIDTH), F32)],
        scratch_shapes=[pltpu.VMEM((8, LRU_WIDTH), F32)] * 4,
        compiler_params=_params("arbitrary"),
    )(xr, xr, yr, hr, hr, drec, *small)


def _attn_bwd(qkv, dattn, sinks):
    tp = qkv.shape[0]
    tr = _row_tile(tp)
    qb, nt = tr // BLOCK, tp // tr
    n_groups = ATTN_HEADS // GQA_GROUP
    sink_rows, bias = _attn_consts(sinks)

    def body(s_ref, b_ref, q_ref, kp_ref, kc_ref, vp_ref, vc_ref, do_ref, dq_ref, dkv_ref, ex_ref, ds_ref, dsink):
        t = pl.program_id(0)

        @pl.when(t == 0)
        def _():
            dsink[...] = jnp.zeros_like(dsink)

        k_all = jnp.concatenate([kp_ref[...], kc_ref[...]], axis=0)
        v_all = jnp.concatenate([vp_ref[...], vc_ref[...]], axis=0)
        tail = None
        for i in range(qb):
            rows = slice(i * BLOCK, (i + 1) * BLOCK)
            q, do = q_ref[rows], do_ref[rows]
            k2, v2 = k_all[i * BLOCK:(i + 2) * BLOCK], v_all[i * BLOCK:(i + 2) * BLOCK]
            bias_n = _block_bias(b_ref, t, qb, i)
            dqs, dks, dvs = [], [], []
            for g in range(n_groups):
                cols = slice(g * HEAD_DIM, (g + 1) * HEAD_DIM)
                k_g, v_g = k2[:, cols], v2[:, cols]
                qg = _stack_heads(q, g) * jnp.asarray(_QSCALE, BF16)
                dog = _stack_heads(do, g)
                p, ps = _attn_probs_t(k_g, qg, bias_n, s_ref[g:g + 1])
                dpt = _dot_nt(v_g, dog)
                delta = jnp.sum(p * dpt, axis=0, keepdims=True)
                dst = (p * (dpt - delta)).astype(BF16)
                dqs.append(_dot_tn(dst, k_g) * _QSCALE)
                dks.append(_dot(dst, qg))
                dvs.append(_dot(p.astype(BF16), dog))
                dsink[g:g + 1] -= ps * delta
            dq_ref[rows] = _unstack_heads(dqs).astype(BF16)
            dkv = jnp.concatenate(dks + dvs, axis=1)
            if i == 0:
                ex_ref[0] = dkv[:BLOCK]
            else:
                dkv_ref[(i - 1) * BLOCK:i * BLOCK] = (tail + dkv[:BLOCK]).astype(BF16)
            tail = dkv[BLOCK:]
        dkv_ref[(qb - 1) * BLOCK:] = tail.astype(BF16)

        @pl.when(t == nt - 1)
        def _():
            lane = lax.broadcasted_iota(jnp.int32, (1, ATTN_HEADS), 1)
            acc = jnp.zeros((1, ATTN_HEADS), F32)
            for h in range(ATTN_HEADS):
                g, hh = divmod(h, GQA_GROUP)
                acc = acc + jnp.where(lane == h, jnp.sum(dsink[g:g + 1, hh * BLOCK:(hh + 1) * BLOCK]), 0.0)
            ds_ref[...] = acc

    cur = lambda w: pl.BlockSpec((tr, w), lambda t: (t, 0))
    return pl.pallas_call(
        body, name="attn_bwd", grid=(nt,),
        in_specs=[_SINK_SPEC, _BIAS_SPEC, cur(ATTN_WIDTH)] + _kv_specs(tr) + [cur(ATTN_WIDTH)],
        out_specs=[cur(ATTN_WIDTH), cur(2 * KV_WIDTH), pl.BlockSpec((1, BLOCK, 2 * KV_WIDTH), lambda t: (t, 0, 0)),
                   pl.BlockSpec((1, ATTN_HEADS), lambda t: (0, 0))],
        out_shape=[jax.ShapeDtypeStruct((tp, ATTN_WIDTH), BF16), jax.ShapeDtypeStruct((tp, 2 * KV_WIDTH), BF16),
                   jax.ShapeDtypeStruct((nt, BLOCK, 2 * KV_WIDTH), F32), jax.ShapeDtypeStruct((1, ATTN_HEADS), F32)],
        scratch_shapes=[pltpu.VMEM((n_groups, GROUP_ROWS), F32)],
        compiler_params=_params("arbitrary"),
    )(sink_rows, bias, qkv, qkv, qkv, qkv, qkv, dattn)


def _inproj_bwd(dq, dkv, dkv_extra, dxr, dyr, w_in, u0, head, x, dh1, g):
    tp = dq.shape[0]
    tr = _row_tile(tp)
    nt, qb = tp // tr, tr // BLOCK

    def body(*refs):
        dq_ref, dkv_ref, ex_ref, dxr_ref, dyr_ref, w_ref, u_ref, head_ref = refs[:8]
        pieces = refs[8:8 + qb]
        dh1_ref, g_ref, gx_ref, dhead_ref, dw_ref, dg_ref, buf, sems = refs[8 + qb:]
        i = pl.program_id(0)
        slot = i % 2

        def out_copy(step, at):
            return pltpu.make_async_copy(buf.at[at], gx_ref.at[pl.ds(step * tr - BLOCK, tr)], sems.at[at])

        extra = jnp.where(i < nt - 1, ex_ref[0], 0.0)
        last = (dkv_ref[tr - BLOCK:].astype(F32) + extra).astype(BF16)
        dkv = last if tr == BLOCK else jnp.concatenate([dkv_ref[:tr - BLOCK], last], axis=0)
        dz = jnp.concatenate([dq_ref[...], dkv, dxr_ref[...], dyr_ref[...]], axis=1)
        du = _dot_nt(dz, w_ref[...])
        hhat, rs = _rms(_seq_tile(head_ref[...], pieces, i))
        dx, dg = _rms_bwd(hhat, rs, g_ref[...], du)
        dh0 = dh1_ref[...] + dx

        @pl.when(i >= 3)
        def _():
            out_copy(i - 2, slot).wait()

        buf[slot] = dh0

        @pl.when(i == 0)
        def _():
            dhead_ref[...] = dh0[:BLOCK]
            if tr > BLOCK:
                first = pltpu.make_async_copy(buf.at[0, pl.ds(BLOCK, tr - BLOCK)], gx_ref.at[pl.ds(0, tr - BLOCK)],
                                              sems.at[0])
                first.start()
                first.wait()

        @pl.when(i >= 1)
        def _():
            out_copy(i, slot).start()

        @pl.when(i == nt - 1)
        def _():
            if nt >= 3:
                out_copy(nt - 2, (nt - 2) % 2).wait()
            if nt >= 2:
                out_copy(nt - 1, (nt - 1) % 2).wait()

        pw = _dot_tn(u_ref[...], dz)

        @pl.when(i == 0)
        def _():
            dw_ref[...] = pw
            dg_ref[...] = dg

        @pl.when(i > 0)
        def _():
            dw_ref[...] += pw
            dg_ref[...] += dg

    row = lambda w: pl.BlockSpec((tr, w), lambda i: (i, 0))
    full = lambda shape: pl.BlockSpec(shape, lambda i: (0,) * len(shape))
    return pl.pallas_call(
        body, name="inproj_bwd", grid=(tp // tr,),
        in_specs=[row(ATTN_WIDTH), row(2 * KV_WIDTH),
                  pl.BlockSpec((1, BLOCK, 2 * KV_WIDTH), lambda i: (jnp.minimum(i + 1, nt - 1), 0, 0)),
                  row(LRU_WIDTH), row(LRU_WIDTH), full(w_in.shape), row(D_MODEL), full(head.shape)]
        + _seq_specs(tr) + [row(D_MODEL), full(g.shape)],
        out_specs=[pl.BlockSpec(memory_space=pl.ANY), full((BLOCK, D_MODEL)), full((D_MODEL, IN_WIDTH)),
                   full((1, D_MODEL))],
        out_shape=[jax.ShapeDtypeStruct(x.shape, F32), jax.ShapeDtypeStruct((BLOCK, D_MODEL), F32),
                   jax.ShapeDtypeStruct((D_MODEL, IN_WIDTH), F32), jax.ShapeDtypeStruct((1, D_MODEL), F32)],
        scratch_shapes=[pltpu.VMEM((2, tr, D_MODEL), F32), pltpu.SemaphoreType.DMA((2,))],
        compiler_params=_params("arbitrary"),
    )(dq, dkv, dkv_extra, dxr, dyr, w_in, u0, head, *([x] * qb), dh1, g)


def _dense_block_diag(w):
    eye = jnp.eye(LRU_BLOCKS, dtype=w.dtype)
    return (w[:, :, None, :] * eye[:, None, :, None]).reshape(LRU_WIDTH, LRU_WIDTH)


def _diag_blocks(dense):
    d4 = dense.reshape(LRU_BLOCKS, LRU_BLOCK, LRU_BLOCKS, LRU_BLOCK)
    return jnp.stack([d4[n, :, n, :] for n in range(LRU_BLOCKS)])


def _local_step(head, x, tgt, g_pre_mix, w_in, conv_w, conv_b, w_a, b_a, w_x, b_x, lam, sinks, w_out, g_post_mix,
                g_pre_ffn, w1, w2, g_post_ffn):
    wa = _dense_block_diag(w_a).astype(BF16)
    wx = _dense_block_diag(w_x).astype(BF16)

    u0, qkv, xr, yr = _inproj_fwd(head, x, g_pre_mix, w_in)
    attn = _attn_fwd(qkv, sinks)
    hr, rec = _lru_fwd(xr, yr, conv_w, conv_b, wa, b_a, wx, b_x, lam)
    mix, h1, u1 = _outproj_fwd(attn, rec, w_out, head, x, g_post_mix, g_pre_ffn)
    r1, dy, df2, loss, dg_post_ffn = _ffn_fwd(u1, w1, w2, h1, tgt, g_post_ffn)

    da1, dh1, dmix, dg_pre_ffn, dg_post_mix = _ffn_bwd_data(df2, r1, w1, w2, dy, h1, mix, g_pre_ffn, g_post_mix)
    dw1, dw2 = _ffn_bwd_weights(u1, da1, r1, df2)
    dattn, drec, dw_out = _outproj_bwd(dmix, w_out, attn, rec)
    dxr, dyr, dwa, dwx, vec = _lru_bwd(xr, yr, hr, drec, conv_w, conv_b, wa, b_a, wx, b_x, lam)
    dq, dkv, dkv_extra, dsinks = _attn_bwd(qkv, dattn, sinks)
    dx, dhead, dw_in, dg_pre_mix = _inproj_bwd(dq, dkv, dkv_extra, dxr, dyr, w_in, u0, head, x, dh1, g_pre_mix)

    grads = dict(
        g_pre_mix=dg_pre_mix, w_in=dw_in, conv_w=vec[0:4], conv_b=vec[4:5], w_a=_diag_blocks(dwa), b_a=vec[5:6],
        w_x=_diag_blocks(dwx), b_x=vec[6:7], lru_lambda=vec[7:8], attn_sinks=dsinks, w_out=dw_out,
        g_post_mix=dg_post_mix, g_pre_ffn=dg_pre_ffn, w_ff1=dw1, w_ff2=dw2, g_post_ffn=dg_post_ffn)
    return loss, dx, dhead, grads


HBM = pl.BlockSpec(memory_space=pltpu.HBM)


def _mesh_pos():
    return lax.axis_index("x"), lax.axis_index("y"), lax.axis_index("c")


def _other_chips(x, y):
    return [(1 - x, y), (x, 1 - y), (1 - x, 1 - y)]


def _remote(src, dst, send_sem, recv_sem, to):
    return pltpu.make_async_remote_copy(src_ref=src, dst_ref=dst, send_sem=send_sem, recv_sem=recv_sem,
                                        device_id=to, device_id_type=MESH)


def _gather_weights(shards, tiny):
    nbig = len(shards)

    def body(*refs):
        srcs, tiny_src = refs[:nbig], refs[nbig]
        outs, tiny_out = refs[nbig + 1:2 * nbig + 1], refs[2 * nbig + 1]
        local_sems, ici_send, ici_recv, d2d_send, d2d_recv, tiny_send, tiny_recv = refs[2 * nbig + 2:]
        x, y, c = _mesh_pos()
        me = 2 * x + y
        chips = _other_chips(x, y)
        sibling = (x, y, 1 - c)
        started = []
        for w, (src, out) in enumerate(zip(srcs, outs)):
            lc = pltpu.make_async_copy(src, out.at[me], local_sems.at[w])
            lc.start()
            started.append(lc)
        lt = pltpu.make_async_copy(tiny_src, tiny_out.at[me], local_sems.at[nbig])
        lt.start()
        sends = []
        for w, (src, out) in enumerate(zip(srcs, outs)):
            hr = src.shape[0] // 2
            for j, chip in enumerate(chips):
                k = 3 * w + j
                cp = _remote(src.at[pl.ds(c * hr, hr)], out.at[me, pl.ds(c * hr, hr)],
                             ici_send.at[k], ici_recv.at[k], (*chip, c))
                cp.start()
                sends.append(cp)
        for j, chip in enumerate(chips):
            cp = _remote(tiny_src, tiny_out.at[me], tiny_send.at[j], tiny_recv.at[j], (*chip, c))
            cp.start()
            sends.append(cp)
        for w, (src, out) in enumerate(zip(srcs, outs)):
            hr = src.shape[0] // 2
            for j, (px, py) in enumerate(chips):
                k = 3 * w + j
                landed = out.at[2 * px + py, pl.ds(c * hr, hr)]
                _remote(landed, landed, ici_send.at[k], ici_recv.at[k], sibling).wait_recv()
                cp = _remote(landed, landed, d2d_send.at[k], d2d_recv.at[k], sibling)
                cp.start()
                sends.append(cp)
        for w, (src, out) in enumerate(zip(srcs, outs)):
            hr = src.shape[0] // 2
            for j, (px, py) in enumerate(chips):
                k = 3 * w + j
                other = out.at[2 * px + py, pl.ds((1 - c) * hr, hr)]
                _remote(other, other, d2d_send.at[k], d2d_recv.at[k], sibling).wait_recv()
        for j, (px, py) in enumerate(chips):
            blk = tiny_out.at[2 * px + py]
            _remote(blk, blk, tiny_send.at[j], tiny_recv.at[j], sibling).wait_recv()
        for cp in sends:
            cp.wait_send()
        for lc in started:
            lc.wait()
        lt.wait()

    out_shape = [jax.ShapeDtypeStruct((N_CHIPS,) + s.shape, s.dtype) for s in shards]
    out_shape.append(jax.ShapeDtypeStruct((N_CHIPS,) + tiny.shape, tiny.dtype))
    n = 3 * nbig
    return pl.pallas_call(
        body, name="gather_weights", out_shape=out_shape,
        in_specs=[HBM] * (nbig + 1), out_specs=[HBM] * (nbig + 1),
        scratch_shapes=[pltpu.SemaphoreType.DMA((nbig + 1,)), pltpu.SemaphoreType.DMA((n,)),
                        pltpu.SemaphoreType.DMA((n,)), pltpu.SemaphoreType.DMA((n,)), pltpu.SemaphoreType.DMA((n,)),
                        pltpu.SemaphoreType.DMA((3,)), pltpu.SemaphoreType.DMA((3,))],
    )(*shards, tiny)


N_DEV = 8


def _gather_small(block):
    m_per, n = block.shape

    def body(x_ref, out_ref, send_sems, recv_sems, local_sem):
        x, y, c = _mesh_pos()
        me, sibling = (x, y, c), (x, y, 1 - c)
        chips = _other_chips(x, y)

        def rows(px, py, pc):
            return out_ref.at[pl.ds((4 * px + 2 * py + pc) * m_per, m_per), :]

        def copy(k, block_of, to, src=None):
            return _remote(rows(*block_of) if src is None else src, rows(*block_of),
                           send_sems.at[k], recv_sems.at[k], to)

        mine = pltpu.make_async_copy(x_ref, rows(*me), local_sem)
        mine.start()
        first = [copy(0, me, sibling, src=x_ref)]
        first += [copy(1 + j, me, (*chip, c), src=x_ref) for j, chip in enumerate(chips)]
        for cp in first:
            cp.start()
        passed = [copy(4 + j, (*chip, c), sibling) for j, chip in enumerate(chips)]
        for j, chip in enumerate(chips):
            copy(1 + j, (*chip, c), me).wait_recv()
            passed[j].start()
        copy(0, sibling, me).wait_recv()
        for j, chip in enumerate(chips):
            copy(4 + j, (*chip, 1 - c), me).wait_recv()
        for cp in first + passed:
            cp.wait_send()
        mine.wait()

    return pl.pallas_call(
        body, name="gather_small", out_shape=jax.ShapeDtypeStruct((N_DEV * m_per, n), block.dtype),
        in_specs=[pl.BlockSpec(memory_space=pltpu.VMEM)], out_specs=pl.BlockSpec(memory_space=pltpu.VMEM),
        scratch_shapes=[pltpu.SemaphoreType.DMA((7,)), pltpu.SemaphoreType.DMA((7,)), pltpu.SemaphoreType.DMA],
        compiler_params=pltpu.CompilerParams(vmem_limit_bytes=VMEM_LIMIT_V7X),
    )(block)


def _sibling_exchange(parts):
    def body(*refs):
        n = len(parts)
        srcs, outs, send_sems, recv_sems = refs[:n], refs[n:2 * n], refs[2 * n], refs[2 * n + 1]
        x, y, c = _mesh_pos()
        sibling = (x, y, 1 - c)
        cps = []
        for w, (src, out) in enumerate(zip(srcs, outs)):
            hr = src.shape[1] // 2
            cp = _remote(src.at[:, pl.ds((1 - c) * hr, hr)], out, send_sems.at[w], recv_sems.at[w], sibling)
            cp.start()
            cps.append(cp)
        for cp in cps:
            cp.wait()

    n = len(parts)
    return pl.pallas_call(
        body, name="sibling_exchange",
        out_shape=[jax.ShapeDtypeStruct((p.shape[0], p.shape[1] // 2, p.shape[2]), p.dtype) for p in parts],
        in_specs=[HBM] * n, out_specs=[HBM] * n,
        scratch_shapes=[pltpu.SemaphoreType.DMA((n,)), pltpu.SemaphoreType.DMA((n,))],
    )(*parts)


def _chip_presum(part, from_sibling, c):
    _, hr, cols = from_sibling.shape
    tr = 256 if hr % 256 == 0 else hr
    steps = hr // tr

    def body(c_ref, a_ref, b_ref, o_ref):
        o_ref[...] = (a_ref[...] + b_ref[...]).astype(BF16)

    return pl.pallas_call(
        body, name="chip_presum",
        grid_spec=pltpu.PrefetchScalarGridSpec(
            num_scalar_prefetch=1, grid=(N_CHIPS, steps),
            in_specs=[pl.BlockSpec((1, tr, cols), lambda j, i, c_ref: (j, c_ref[0] * steps + i, 0)),
                      pl.BlockSpec((1, tr, cols), lambda j, i, c_ref: (j, i, 0))],
            out_specs=pl.BlockSpec((1, tr, cols), lambda j, i, c_ref: (j, i, 0))),
        out_shape=jax.ShapeDtypeStruct(from_sibling.shape, BF16),
        compiler_params=_params("parallel", "parallel"),
    )(c, part, from_sibling)


def _scatter_partials(cparts):
    nw = len(cparts)

    def body(*refs):
        srcs, outs = refs[:nw], refs[nw:2 * nw]
        local_sems, own_send, own_recv, ici_send, ici_recv, d2d_send, d2d_recv = refs[2 * nw:]
        x, y, c = _mesh_pos()
        me = 2 * x + y
        chips = _other_chips(x, y)
        sibling = (x, y, 1 - c)
        locals_, sends = [], []
        for w, (src, out) in enumerate(zip(srcs, outs)):
            hr = src.shape[1]
            mine = out.at[me, pl.ds(c * hr, hr)]
            lc = pltpu.make_async_copy(src.at[me], mine, local_sems.at[w])
            lc.start()
            locals_.append(lc)
            cp = _remote(src.at[me], mine, own_send.at[w], own_recv.at[w], sibling)
            cp.start()
            sends.append(cp)
            for j, (px, py) in enumerate(chips):
                k = 3 * w + j
                cp = _remote(src.at[2 * px + py], mine, ici_send.at[k], ici_recv.at[k], (px, py, c))
                cp.start()
                sends.append(cp)
        for w, (src, out) in enumerate(zip(srcs, outs)):
            hr = src.shape[1]
            for j, (px, py) in enumerate(chips):
                k = 3 * w + j
                landed = out.at[2 * px + py, pl.ds(c * hr, hr)]
                _remote(landed, landed, ici_send.at[k], ici_recv.at[k], sibling).wait_recv()
                cp = _remote(landed, landed, d2d_send.at[k], d2d_recv.at[k], sibling)
                cp.start()
                sends.append(cp)
        for w, (src, out) in enumerate(zip(srcs, outs)):
            hr = src.shape[1]
            other = out.at[me, pl.ds((1 - c) * hr, hr)]
            _remote(other, other, own_send.at[w], own_recv.at[w], sibling).wait_recv()
            for j, (px, py) in enumerate(chips):
                k = 3 * w + j
                other = out.at[2 * px + py, pl.ds((1 - c) * hr, hr)]
                _remote(other, other, d2d_send.at[k], d2d_recv.at[k], sibling).wait_recv()
        for cp in sends:
            cp.wait_send()
        for lc in locals_:
            lc.wait()

    n = 3 * nw
    dma = pltpu.SemaphoreType.DMA
    return pl.pallas_call(
        body, name="scatter_partials",
        out_shape=[jax.ShapeDtypeStruct((N_CHIPS, 2 * p.shape[1], p.shape[2]), p.dtype) for p in cparts],
        in_specs=[HBM] * nw, out_specs=[HBM] * nw,
        scratch_shapes=[dma((nw,)), dma((nw,)), dma((nw,)), dma((n,)), dma((n,)), dma((n,)), dma((n,))],
    )(*cparts)


def _adamw(w, g, m, v):
    m = ADAM_B1 * m + (1.0 - ADAM_B1) * g
    v = ADAM_B2 * v + (1.0 - ADAM_B2) * (g * g)
    m_hat = m / (1.0 - ADAM_B1 ** ADAM_STEP)
    v_hat = v / (1.0 - ADAM_B2 ** ADAM_STEP)
    delta = -ADAM_LR * (m_hat / (jnp.sqrt(v_hat) + ADAM_EPS) + ADAM_WD * w)
    return delta, m, v


def _adamw_big(partials, w, m, v):
    rows, cols = w.shape
    tr = 256

    def body(p_ref, w_ref, m_ref, v_ref, g_ref, d_ref, m2_ref, v2_ref):
        g = ((p_ref[0].astype(F32) + p_ref[1].astype(F32)) + p_ref[2].astype(F32)) + p_ref[3].astype(F32)
        g_ref[...] = g
        d_ref[...], m2_ref[...], v2_ref[...] = _adamw(w_ref[...], g, m_ref[...], v_ref[...])

    blk = pl.BlockSpec((tr, cols), lambda i: (i, 0))
    return pl.pallas_call(
        body, name="adamw_big", grid=(rows // tr,),
        in_specs=[pl.BlockSpec((N_CHIPS, tr, cols), lambda i: (0, i, 0)), blk, blk, blk],
        out_specs=[blk] * 4, out_shape=[jax.ShapeDtypeStruct((rows, cols), F32)] * 4,
        compiler_params=_params("parallel"),
    )(partials, w, m, v)


def _sum_devices(gathered, rows):
    cols = gathered.shape[1]

    def body(g_ref, o_ref):
        acc = g_ref[0:rows]
        for d in range(1, N_DEV):
            acc = acc + g_ref[d * rows:(d + 1) * rows]
        o_ref[...] = acc

    return pl.pallas_call(
        body, name="sum_devices", out_shape=jax.ShapeDtypeStruct((rows, cols), F32),
        in_specs=[pl.BlockSpec(memory_space=pltpu.VMEM)], out_specs=pl.BlockSpec(memory_space=pltpu.VMEM),
        compiler_params=pltpu.CompilerParams(vmem_limit_bytes=VMEM_LIMIT_V7X),
    )(gathered)


def _adamw_small(quads):
    n = len(quads)

    def body(*refs):
        ins, outs = refs[:4 * n], refs[4 * n:]
        for t in range(n):
            w, g, m, v = (r[...] for r in ins[4 * t:4 * t + 4])
            outs[3 * t][...], outs[3 * t + 1][...], outs[3 * t + 2][...] = _adamw(w, g, m, v)

    flat = [a for q in quads for a in q]
    vm = pl.BlockSpec(memory_space=pltpu.VMEM)
    res = pl.pallas_call(
        body, name="adamw_small",
        out_shape=[jax.ShapeDtypeStruct(q[0].shape, F32) for q in quads for _ in range(3)],
        in_specs=[vm] * (4 * n), out_specs=[vm] * (3 * n),
    )(*flat)
    return [tuple(res[3 * t:3 * t + 3]) for t in range(n)]


SMALL_PACK_ROWS = 96
_WEIGHTS = ['meta_tokens', 'g_pre_mix', 'w_in', 'conv_w', 'conv_b', 'w_a', 'b_a', 'w_x', 'b_x', 'lru_lambda',
            'attn_sinks', 'w_out', 'g_post_mix', 'g_pre_ffn', 'w_ff1', 'w_ff2', 'g_post_ffn']
_BIG = ['w_in', 'w_out', 'w_ff1', 'w_ff2']


def _pack_small(dmeta, g):
    z = lambda r, c: jnp.zeros((r, c), F32)
    rows = [
        dmeta,
        g['g_pre_mix'], g['g_post_mix'], g['g_pre_ffn'], g['g_post_ffn'],
        jnp.concatenate([g['conv_w'], z(4, 512)], axis=1),
        jnp.concatenate([g['conv_b'], g['b_a']], axis=1),
        jnp.concatenate([g['b_x'], g['lru_lambda']], axis=1),
        jnp.concatenate([g['attn_sinks'], z(1, D_MODEL - ATTN_HEADS)], axis=1),
        z(5, D_MODEL),
        g['w_a'].reshape(32, D_MODEL), g['w_x'].reshape(32, D_MODEL),
    ]
    return jnp.concatenate(rows, axis=0)


def _unpack_small(s, chip):
    return dict(
        meta_tokens=lax.dynamic_slice(s[0:16], (0, chip * 256), (16, 256)),
        g_pre_mix=s[16:17], g_post_mix=s[17:18], g_pre_ffn=s[18:19], g_post_ffn=s[19:20],
        conv_w=lax.dynamic_slice(s[20:24], (0, chip * 128), (4, 128)).reshape(1, 4, 128),
        conv_b=s[24:25, :512], b_a=s[24:25, 512:], b_x=s[25:26, :512], lru_lambda=s[25:26, 512:],
        attn_sinks=s[26:27, :ATTN_HEADS],
        w_a=s[32:64].reshape(1, LRU_BLOCKS, LRU_BLOCK, LRU_BLOCK),
        w_x=s[64:96].reshape(1, LRU_BLOCKS, LRU_BLOCK, LRU_BLOCK))


def _as2d(a):
    if a.ndim == 2:
        return a
    return a.reshape(-1, a.shape[-1])


def kernel(x, meta_tokens, g_pre_mix, w_in, conv_w, conv_b, w_a, b_a, w_x, b_x, lru_lambda, attn_sinks, w_out, g_post_mix, g_pre_ffn, w_ff1, w_ff2, g_post_ffn, loss_target, m_meta_tokens, m_g_pre_mix, m_w_in, m_conv_w, m_conv_b, m_w_a, m_b_a, m_w_x, m_b_x, m_lru_lambda, m_attn_sinks, m_w_out, m_g_post_mix, m_g_pre_ffn, m_w_ff1, m_w_ff2, m_g_post_ffn, v_meta_tokens, v_g_pre_mix, v_w_in, v_conv_w, v_conv_b, v_w_a, v_b_a, v_w_x, v_b_x, v_lru_lambda, v_attn_sinks, v_w_out, v_g_post_mix, v_g_pre_ffn, v_w_ff1, v_w_ff2, v_g_post_ffn):
    weights = dict(meta_tokens=meta_tokens, g_pre_mix=g_pre_mix, w_in=w_in, conv_w=conv_w, conv_b=conv_b, w_a=w_a,
                   b_a=b_a, w_x=w_x, b_x=b_x, lru_lambda=lru_lambda, attn_sinks=attn_sinks, w_out=w_out,
                   g_post_mix=g_post_mix, g_pre_ffn=g_pre_ffn, w_ff1=w_ff1, w_ff2=w_ff2, g_post_ffn=g_post_ffn)
    mom1 = dict(zip(_WEIGHTS, [m_meta_tokens, m_g_pre_mix, m_w_in, m_conv_w, m_conv_b, m_w_a, m_b_a, m_w_x, m_b_x,
                               m_lru_lambda, m_attn_sinks, m_w_out, m_g_post_mix, m_g_pre_ffn, m_w_ff1, m_w_ff2,
                               m_g_post_ffn]))
    mom2 = dict(zip(_WEIGHTS, [v_meta_tokens, v_g_pre_mix, v_w_in, v_conv_w, v_conv_b, v_w_a, v_b_a, v_w_x, v_b_x,
                               v_lru_lambda, v_attn_sinks, v_w_out, v_g_post_mix, v_g_pre_ffn, v_w_ff1, v_w_ff2,
                               v_g_post_ffn]))
    xi, yi, ci = _mesh_pos()
    chip = 2 * xi + yi

    tiny = jnp.concatenate([meta_tokens, jnp.pad(conv_w[0], ((0, 4), (0, 128)))], axis=0)
    g_in, g_out, g_f1, g_f2, g_tiny = _gather_weights(
        [w_in[0].astype(BF16), w_out[0].astype(BF16), w_ff1[0].astype(BF16), w_ff2[0].astype(BF16)], tiny)
    w_in_full = jnp.concatenate([g_in[j] for j in range(N_CHIPS)], axis=1)
    w_out_full = g_out.reshape(D_MODEL, D_MODEL)
    meta_full = jnp.concatenate([g_tiny[j, :N_META] for j in range(N_CHIPS)], axis=1)
    conv_w_full = jnp.concatenate([g_tiny[j, N_META:N_META + 4, :128] for j in range(N_CHIPS)], axis=1)

    head = jnp.concatenate([jnp.zeros((PAD_ROWS, D_MODEL), F32), meta_full], axis=0)
    loss, dx, dhead, grads = _local_step(head, x[0], loss_target[0], g_pre_mix, w_in_full, conv_w_full, conv_b, w_a[0], b_a, w_x[0], b_x,
                                   lru_lambda, attn_sinks, w_out_full, g_post_mix, g_pre_ffn, g_f1, g_f2, g_post_ffn)
    loss = lax.psum(loss[0, 0], ("x", "y", "c"))
    grad_x = dx[None]

    gathered = _gather_small(_pack_small(dhead[PAD_ROWS:], grads))
    small = _unpack_small(_sum_devices(gathered, SMALL_PACK_ROWS), chip)

    dw_in = grads['w_in']
    parts = [jnp.stack([dw_in[:, j * 448:(j + 1) * 448] for j in range(N_CHIPS)]),
             grads['w_out'].reshape(N_CHIPS, D_MODEL // N_CHIPS, D_MODEL), grads['w_ff1'], grads['w_ff2']]
    from_sibling = _sibling_exchange(parts)
    c_arr = jnp.reshape(ci, (1,)).astype(jnp.int32)
    cparts = [_chip_presum(p, r, c_arr) for p, r in zip(parts, from_sibling)]
    chip_partials = _scatter_partials(cparts)

    g_out_d, delta, new_m, new_v = {}, {}, {}, {}
    for name, part in zip(_BIG, chip_partials):
        shp = weights[name].shape
        res = _adamw_big(part, weights[name][0], mom1[name][0], mom2[name][0])
        g_out_d[name], delta[name], new_m[name], new_v[name] = (r.reshape(shp) for r in res)
    small_names = [n for n in _WEIGHTS if n not in _BIG]
    quads = [(_as2d(weights[n]), _as2d(small[n]), _as2d(mom1[n]), _as2d(mom2[n])) for n in small_names]
    for name, (d, m2, v2) in zip(small_names, _adamw_small(quads)):
        shp = weights[name].shape
        g_out_d[name] = small[name].reshape(shp)
        delta[name], new_m[name], new_v[name] = d.reshape(shp), m2.reshape(shp), v2.reshape(shp)

    return (loss, grad_x, *[g_out_d[n] for n in _WEIGHTS], *[delta[n] for n in _WEIGHTS],
            *[new_m[n] for n in _WEIGHTS], *[new_v[n] for n in _WEIGHTS])
```

```python
import numpy as np
import jax
import jax.numpy as jnp
from jax import lax
from jax.experimental import pallas as pl
from jax.experimental.pallas import tpu as pltpu

F32 = jnp.float32
BF16 = jnp.bfloat16

D_MODEL = 1024
N_META = 16
BLOCK = 128
PAD_ROWS = BLOCK - N_META
HEAD_DIM = 64
ATTN_HEADS = 8
GQA_GROUP = 4
ATTN_WIDTH = 512
KV_WIDTH = 128
QKV_WIDTH = ATTN_WIDTH + 2 * KV_WIDTH
LRU_WIDTH = 512
LRU_BLOCKS = 8
LRU_BLOCK = 64
LRU_C = 8.0
IN_WIDTH = 1792
D_FF = 4096
N_CHIPS = 4
FF_CHUNK = D_FF // N_CHIPS
EPS = 1e-6
NEG = -1e30

ADAM_LR = 0.001
ADAM_B1 = 0.9
ADAM_B2 = 0.999
ADAM_EPS = 1e-08
ADAM_WD = 0.01
ADAM_STEP = 10

VMEM_LIMIT_V7X = 56 * 1024 * 1024
MESH = pl.DeviceIdType.MESH

NT = (((1,), (1,)), ((), ()))
TN = (((0,), (0,)), ((), ()))


def _row_tile(tp):
    return 640 if tp % 640 == 0 else BLOCK


def _params(*sem):
    return pltpu.CompilerParams(dimension_semantics=sem, vmem_limit_bytes=VMEM_LIMIT_V7X)


def _dot(a, b):
    return jnp.dot(a, b, preferred_element_type=F32)


def _dot_nt(a, b):
    return lax.dot_general(a, b, NT, preferred_element_type=F32)


def _dot_tn(a, b):
    return lax.dot_general(a, b, TN, preferred_element_type=F32)


def _rms(x):
    rs = lax.rsqrt(jnp.mean(x * x, axis=-1, keepdims=True) + EPS)
    return x * rs, rs


def _rms_bwd(xhat, rs, g, dy):
    dyg = dy * g
    dx = rs * (dyg - xhat * jnp.mean(dyg * xhat, axis=-1, keepdims=True))
    dg = jnp.sum(dy * xhat, axis=0, keepdims=True)
    return dx, dg


def _gelu(x):
    k = 0.7978845608028654
    t = jnp.tanh(k * (x + 0.044715 * x * x * x))
    return 0.5 * x * (1.0 + t), t


def _gelu_grad(x, t):
    k = 0.7978845608028654
    return 0.5 * (1.0 + t) + 0.5 * x * (1.0 - t * t) * k * (1.0 + 3 * 0.044715 * x * x)


def _sigmoid(x):
    return 1.0 / (1.0 + jnp.exp(-x))


def _neg_expm1(x):
    series = x * (1.0 + x * 0.5 * (1.0 + x * (1.0 / 3.0) * (1.0 + x * 0.25 * (1.0 + x * 0.2))))
    return -jnp.where(jnp.abs(x) < 0.05, series, jnp.exp(x) - 1.0)


def _softplus(x):
    return jnp.maximum(x, 0.0) + jnp.log1p(jnp.exp(-jnp.abs(x)))


def _seq_specs(tr):
    qb = tr // BLOCK
    return [pl.BlockSpec((BLOCK, D_MODEL), lambda i, *_, s=s: (jnp.maximum(i * qb + s - 1, 0), 0)) for s in range(qb)]


def _seq_tile(head, pieces, i):
    first = jnp.where(i == 0, head, pieces[0][...])
    return jnp.concatenate([first] + [p[...] for p in pieces[1:]], axis=0)


def _inproj_fwd(head, x, g, w_in, token):
    tp = BLOCK + x.shape[0]
    tr = _row_tile(tp)
    qb = tr // BLOCK

    def body(*refs):
        head_ref, pieces = refs[0], refs[1:1 + qb]
        g_ref, w_ref, _, u_ref, qkv_ref, xr_ref, yr_ref = refs[1 + qb:]
        xhat, _ = _rms(_seq_tile(head_ref[...], pieces, pl.program_id(0)))
        u = (xhat * g_ref[...]).astype(BF16)
        u_ref[...] = u
        z = _dot(u, w_ref[...])
        qkv_ref[...] = z[:, :QKV_WIDTH].astype(BF16)
        xr_ref[...] = z[:, QKV_WIDTH:QKV_WIDTH + LRU_WIDTH]
        yr_ref[...] = z[:, QKV_WIDTH + LRU_WIDTH:]

    row = lambda w: pl.BlockSpec((tr, w), lambda i: (i, 0))
    full = lambda a: pl.BlockSpec(a.shape, lambda i: (0,) * a.ndim)
    return pl.pallas_call(
        body, name="inproj_fwd", grid=(tp // tr,),
        in_specs=[full(head)] + _seq_specs(tr) + [full(g), full(w_in), full(token)],
        out_specs=[row(D_MODEL), row(QKV_WIDTH), row(LRU_WIDTH), row(LRU_WIDTH)],
        out_shape=[jax.ShapeDtypeStruct((tp, D_MODEL), BF16), jax.ShapeDtypeStruct((tp, QKV_WIDTH), BF16),
                   jax.ShapeDtypeStruct((tp, LRU_WIDTH), F32), jax.ShapeDtypeStruct((tp, LRU_WIDTH), F32)],
        compiler_params=_params("parallel"),
    )(head, *([x] * qb), g, w_in, token)


GROUP_ROWS = GQA_GROUP * BLOCK


def _attn_bias():
    j = np.arange(2 * BLOCK)[:, None]
    i = np.arange(BLOCK)[None, :]
    band = (j - i >= 1) & (j - i <= BLOCK)
    out = []
    for n in range(3):
        ok = band & ((n - 1) * BLOCK + j >= PAD_ROWS) if n < 2 else band
        out.append(np.tile(np.where(ok, 0.0, NEG).astype(np.float32), (1, GQA_GROUP)))
    return jnp.asarray(np.stack(out))


def _stack_heads(a, g):
    heads = range(GQA_GROUP * g, GQA_GROUP * (g + 1))
    return jnp.concatenate([a[:, h * HEAD_DIM:(h + 1) * HEAD_DIM] for h in heads], axis=0)


def _unstack_heads(groups):
    return jnp.concatenate([p[h * BLOCK:(h + 1) * BLOCK] for p in groups for h in range(GQA_GROUP)], axis=1)


def _attn_probs_t(k_g, qg, bias, sink_row):
    st = _dot_nt(k_g, qg) + bias
    m = jnp.maximum(jnp.max(st, axis=0, keepdims=True), sink_row)
    p = jnp.exp(st - m)
    es = jnp.exp(sink_row - m)
    inv = 1.0 / (jnp.sum(p, axis=0, keepdims=True) + es)
    return p * inv, es * inv


def _attn_consts(sinks):
    return jnp.repeat(sinks.reshape(ATTN_HEADS), BLOCK).reshape(ATTN_HEADS // GQA_GROUP, GROUP_ROWS), _attn_bias()


_SINK_SPEC = pl.BlockSpec((ATTN_HEADS // GQA_GROUP, GROUP_ROWS), lambda n: (0, 0))
_BIAS_SPEC = pl.BlockSpec((3, 2 * BLOCK, GROUP_ROWS), lambda n: (0, 0, 0))
_QSCALE = HEAD_DIM ** -0.5


def _kv_specs(tr):
    qb = tr // BLOCK
    prev = lambda col: pl.BlockSpec((BLOCK, KV_WIDTH), lambda t: (jnp.maximum(t * qb - 1, 0), col))
    cur = lambda col: pl.BlockSpec((tr, KV_WIDTH), lambda t: (t, col))
    return [prev(4), cur(4), prev(5), cur(5)]


def _block_bias(b_ref, t, qb, i):
    return b_ref[2] if i >= 2 else b_ref[jnp.minimum(t * qb + i, 2)]


def _attn_fwd(qkv, sinks):
    tp = qkv.shape[0]
    tr = _row_tile(tp)
    qb = tr // BLOCK
    sink_rows, bias = _attn_consts(sinks)

    def body(s_ref, b_ref, q_ref, kp_ref, kc_ref, vp_ref, vc_ref, o_ref):
        t = pl.program_id(0)
        k_all = jnp.concatenate([kp_ref[...], kc_ref[...]], axis=0)
        v_all = jnp.concatenate([vp_ref[...], vc_ref[...]], axis=0)
        for i in range(qb):
            rows = slice(i * BLOCK, (i + 1) * BLOCK)
            q = q_ref[rows]
            k2, v2 = k_all[i * BLOCK:(i + 2) * BLOCK], v_all[i * BLOCK:(i + 2) * BLOCK]
            bias_n = _block_bias(b_ref, t, qb, i)
            outs = []
            for g in range(ATTN_HEADS // GQA_GROUP):
                cols = slice(g * HEAD_DIM, (g + 1) * HEAD_DIM)
                qg = _stack_heads(q, g) * jnp.asarray(_QSCALE, BF16)
                p, _ = _attn_probs_t(k2[:, cols], qg, bias_n, s_ref[g:g + 1])
                outs.append(_dot_tn(p.astype(BF16), v2[:, cols]))
            o_ref[rows] = _unstack_heads(outs).astype(BF16)

    return pl.pallas_call(
        body, name="attn_fwd", grid=(tp // tr,),
        in_specs=[_SINK_SPEC, _BIAS_SPEC, pl.BlockSpec((tr, ATTN_WIDTH), lambda t: (t, 0))] + _kv_specs(tr),
        out_specs=pl.BlockSpec((tr, ATTN_WIDTH), lambda t: (t, 0)),
        out_shape=jax.ShapeDtypeStruct((tp, ATTN_WIDTH), BF16),
        compiler_params=_params("parallel"),
    )(sink_rows, bias, qkv, qkv, qkv, qkv, qkv)


def _conv_taps(x, halo):
    ext = jnp.concatenate([halo, x], axis=0)
    return [ext[8:] if k == 3 else pltpu.roll(ext, 3 - k, 0)[8:] for k in range(4)]


def _lru_gates(xc, wa, ba, wx, bx, sp):
    xb = xc.astype(BF16)
    r = _sigmoid(_dot(xb, wa) + ba)
    ig = _sigmoid(_dot(xb, wx) + bx)
    log_a = (-LRU_C * sp) * r
    a = jnp.exp(log_a)
    mult = jnp.sqrt(_neg_expm1(2.0 * log_a))
    return xb, r, ig, a, mult


def _scan_fwd(a, b):
    rows = lax.broadcasted_iota(jnp.int32, a.shape, 0)
    d = 1
    while d < a.shape[0]:
        keep = rows >= d
        b = jnp.where(keep, a * pltpu.roll(b, d, 0) + b, b)
        a = jnp.where(keep, a * pltpu.roll(a, d, 0), a)
        d *= 2
    return a, b


def _scan_rev(c, b):
    n = c.shape[0]
    rows = lax.broadcasted_iota(jnp.int32, c.shape, 0)
    d = 1
    while d < n:
        keep = rows < n - d
        b = jnp.where(keep, b + c * pltpu.roll(b, n - d, 0), b)
        c = jnp.where(keep, c * pltpu.roll(c, n - d, 0), c)
        d *= 2
    return c, b


def _lru_fwd(xr, yr, conv_w, conv_b, wa, ba, wx, bx, lam):
    tp = xr.shape[0]
    nb = tp // BLOCK

    def body(xr_ref, yr_ref, cw_ref, cb_ref, wa_ref, ba_ref, wx_ref, bx_ref, lam_ref, hr_ref, rec_ref, halo, hprev):
        n = pl.program_id(0)

        @pl.when(n == 0)
        def _():
            halo[...] = jnp.zeros_like(halo)
            hprev[...] = jnp.zeros_like(hprev)

        x = xr_ref[...]
        taps = _conv_taps(x, halo[...])
        halo[...] = x[BLOCK - 8:]
        cw = cw_ref[...]
        xc = cb_ref[...] + sum(cw[k:k + 1] * taps[k] for k in range(4))
        sp = _softplus(-lam_ref[...])
        _, _, ig, a, mult = _lru_gates(xc, wa_ref[...], ba_ref[...], wx_ref[...], bx_ref[...], sp)
        rows = n * BLOCK + lax.broadcasted_iota(jnp.int32, xc.shape, 0)
        u = jnp.where(rows >= PAD_ROWS, mult * (ig * xc), 0.0)
        acum, hloc = _scan_fwd(a, u)
        h = acum * hprev[0:1] + hloc
        hprev[0:1] = h[BLOCK - 1:]
        hr_ref[...] = h
        gl, _ = _gelu(yr_ref[...])
        rec_ref[...] = (gl * h).astype(BF16)

    blk = pl.BlockSpec((BLOCK, LRU_WIDTH), lambda n: (n, 0))
    full = lambda a: pl.BlockSpec(a.shape, lambda n: (0,) * a.ndim)
    small = [conv_w, conv_b, wa, ba, wx, bx, lam]
    return pl.pallas_call(
        body, name="lru_fwd", grid=(nb,),
        in_specs=[blk, blk] + [full(a) for a in small],
        out_specs=[blk, blk],
        out_shape=[jax.ShapeDtypeStruct((tp, LRU_WIDTH), F32), jax.ShapeDtypeStruct((tp, LRU_WIDTH), BF16)],
        scratch_shapes=[pltpu.VMEM((8, LRU_WIDTH), F32), pltpu.VMEM((8, LRU_WIDTH), F32)],
        compiler_params=_params("arbitrary"),
    )(xr, yr, *small)


def _outproj_fwd(attn, rec, w_out, head, x, g_post_mix, g_pre_ffn):
    tp = attn.shape[0]
    tr = _row_tile(tp)
    qb = tr // BLOCK

    def body(*refs):
        a_ref, r_ref, w_ref, head_ref = refs[:4]
        pieces = refs[4:4 + qb]
        gm_ref, gf_ref, mix_ref, h1_ref, u1_ref = refs[4 + qb:]
        mix = _dot(a_ref[...], w_ref[:ATTN_WIDTH]) + _dot(r_ref[...], w_ref[ATTN_WIDTH:])
        mix_ref[...] = mix
        mhat, _ = _rms(mix)
        h1 = _seq_tile(head_ref[...], pieces, pl.program_id(0)) + mhat * gm_ref[...]
        h1_ref[...] = h1
        hhat, _ = _rms(h1)
        u1_ref[...] = (hhat * gf_ref[...]).astype(BF16)

    row = lambda w: pl.BlockSpec((tr, w), lambda i: (i, 0))
    full = lambda a: pl.BlockSpec(a.shape, lambda i: (0,) * a.ndim)
    return pl.pallas_call(
        body, name="outproj_fwd", grid=(tp // tr,),
        in_specs=[row(ATTN_WIDTH), row(LRU_WIDTH), full(w_out), full(head)] + _seq_specs(tr)
        + [full(g_post_mix), full(g_pre_ffn)],
        out_specs=[row(D_MODEL), row(D_MODEL), row(D_MODEL)],
        out_shape=[jax.ShapeDtypeStruct((tp, D_MODEL), F32), jax.ShapeDtypeStruct((tp, D_MODEL), F32),
                   jax.ShapeDtypeStruct((tp, D_MODEL), BF16)],
        compiler_params=_params("parallel"),
    )(attn, rec, w_out, head, *([x] * qb), g_post_mix, g_pre_ffn)


def _ffn_fwd(u1, w1, w2, h1, tgt, g_post_ffn):
    tp = h1.shape[0]
    tr = _row_tile(tp)
    qb = tr // BLOCK

    def body(*refs):
        u_ref, w1_ref, w2_ref, h1_ref = refs[:4]
        t_pieces = refs[4:4 + qb]
        g_ref, r1_ref, dy_ref, df2_ref, loss_ref, dg_ref, acc = refs[4 + qb:]
        i, c = pl.program_id(0), pl.program_id(1)

        @pl.when((i == 0) & (c == 0))
        def _():
            loss_ref[...] = jnp.zeros_like(loss_ref)
            dg_ref[...] = jnp.zeros_like(dg_ref)

        r = jnp.maximum(_dot(u_ref[...], w1_ref[0]), 0.0)
        r1_ref[...] = r.astype(BF16)
        part = _dot((r * r).astype(BF16), w2_ref[0])

        @pl.when(c == 0)
        def _():
            acc[...] = part

        @pl.when(c > 0)
        def _():
            acc[...] += part

        @pl.when(c == N_CHIPS - 1)
        def _():
            g = g_ref[...]
            fhat, rs = _rms(acc[...])
            h2 = h1_ref[...] + fhat * g
            rows = i * tr + lax.broadcasted_iota(jnp.int32, h2.shape, 0)
            tgt_tile = jnp.concatenate([p[...] for p in t_pieces], axis=0)
            err = jnp.where(rows >= BLOCK, h2 - tgt_tile, 0.0)
            dy = err * (1.0 / D_MODEL)
            dy_ref[...] = dy
            loss_ref[...] += (0.5 / D_MODEL) * jnp.sum(err * err)
            df2, dg = _rms_bwd(fhat, rs, g, dy)
            df2_ref[...] = df2.astype(BF16)
            dg_ref[...] += dg

    row = pl.BlockSpec((tr, D_MODEL), lambda i, c: (i, 0))
    full = lambda a: pl.BlockSpec(a.shape, lambda i, c: (0,) * a.ndim)
    return pl.pallas_call(
        body, name="ffn_fwd", grid=(tp // tr, N_CHIPS),
        in_specs=[row, pl.BlockSpec((1, D_MODEL, FF_CHUNK), lambda i, c: (c, 0, 0)),
                  pl.BlockSpec((1, FF_CHUNK, D_MODEL), lambda i, c: (c, 0, 0)), row] + _seq_specs(tr)
        + [full(g_post_ffn)],
        out_specs=[pl.BlockSpec((tr, FF_CHUNK), lambda i, c: (i, c)), row, row,
                   pl.BlockSpec((1, 1), lambda i, c: (0, 0)), pl.BlockSpec((1, D_MODEL), lambda i, c: (0, 0))],
        out_shape=[jax.ShapeDtypeStruct((tp, D_FF), BF16), jax.ShapeDtypeStruct((tp, D_MODEL), F32),
                   jax.ShapeDtypeStruct((tp, D_MODEL), BF16), jax.ShapeDtypeStruct((1, 1), F32),
                   jax.ShapeDtypeStruct((1, D_MODEL), F32)],
        scratch_shapes=[pltpu.VMEM((tr, D_MODEL), F32)],
        compiler_params=_params("arbitrary", "arbitrary"),
    )(u1, w1, w2, h1, *([tgt] * qb), g_post_ffn)


def _ffn_bwd_data(df2, r1, w1, w2, dy, h1, mix, g_pre_ffn, g_post_mix):
    tp = h1.shape[0]
    tr = _row_tile(tp)

    def body(df2_ref, r1_ref, w1_ref, w2_ref, dy_ref, h1_ref, mix_ref, gf_ref, gm_ref,
             da_ref, dh1_ref, dmix_ref, dgf_ref, dgm_ref, acc):
        i, c = pl.program_id(0), pl.program_id(1)

        @pl.when((i == 0) & (c == 0))
        def _():
            dgf_ref[...] = jnp.zeros_like(dgf_ref)
            dgm_ref[...] = jnp.zeros_like(dgm_ref)

        df = _dot_nt(df2_ref[...], w2_ref[0])
        da = (df * (2.0 * r1_ref[...].astype(F32))).astype(BF16)
        da_ref[...] = da
        part = _dot_nt(da, w1_ref[0])

        @pl.when(c == 0)
        def _():
            acc[...] = part

        @pl.when(c > 0)
        def _():
            acc[...] += part

        @pl.when(c == N_CHIPS - 1)
        def _():
            hhat, rs = _rms(h1_ref[...])
            dx, dgf = _rms_bwd(hhat, rs, gf_ref[...], acc[...])
            dh1 = dy_ref[...] + dx
            dh1_ref[...] = dh1
            dgf_ref[...] += dgf
            mhat, rsm = _rms(mix_ref[...])
            dmix, dgm = _rms_bwd(mhat, rsm, gm_ref[...], dh1)
            dmix_ref[...] = dmix.astype(BF16)
            dgm_ref[...] += dgm

    row = pl.BlockSpec((tr, D_MODEL), lambda i, c: (i, 0))
    chunk = pl.BlockSpec((tr, FF_CHUNK), lambda i, c: (i, c))
    gain = pl.BlockSpec((1, D_MODEL), lambda i, c: (0, 0))
    return pl.pallas_call(
        body, name="ffn_bwd_data", grid=(tp // tr, N_CHIPS),
        in_specs=[row, chunk, pl.BlockSpec((1, D_MODEL, FF_CHUNK), lambda i, c: (c, 0, 0)),
                  pl.BlockSpec((1, FF_CHUNK, D_MODEL), lambda i, c: (c, 0, 0)), row, row, row, gain, gain],
        out_specs=[chunk, row, row, gain, gain],
        out_shape=[jax.ShapeDtypeStruct((tp, D_FF), BF16), jax.ShapeDtypeStruct((tp, D_MODEL), F32),
                   jax.ShapeDtypeStruct((tp, D_MODEL), BF16), jax.ShapeDtypeStruct((1, D_MODEL), F32),
                   jax.ShapeDtypeStruct((1, D_MODEL), F32)],
        scratch_shapes=[pltpu.VMEM((tr, D_MODEL), F32)],
        compiler_params=_params("arbitrary", "arbitrary"),
    )(df2, r1, w1, w2, dy, h1, mix, g_pre_ffn, g_post_mix)


def _ffn_bwd_weights(u1, da1, r1, df2):
    tp = u1.shape[0]
    tr = _row_tile(tp)

    def body(u_ref, da_ref, r1_ref, df2_ref, dw1_ref, dw2_ref):
        i = pl.program_id(1)
        r = r1_ref[...].astype(F32)
        p1 = _dot_tn(u_ref[...], da_ref[...])
        p2 = _dot_tn((r * r).astype(BF16), df2_ref[...])

        @pl.when(i == 0)
        def _():
            dw1_ref[0] = p1
            dw2_ref[0] = p2

        @pl.when(i > 0)
        def _():
            dw1_ref[0] += p1
            dw2_ref[0] += p2

    row = pl.BlockSpec((tr, D_MODEL), lambda c, i: (i, 0))
    chunk = pl.BlockSpec((tr, FF_CHUNK), lambda c, i: (i, c))
    return pl.pallas_call(
        body, name="ffn_bwd_weights", grid=(N_CHIPS, tp // tr),
        in_specs=[row, chunk, chunk, row],
        out_specs=[pl.BlockSpec((1, D_MODEL, FF_CHUNK), lambda c, i: (c, 0, 0)),
                   pl.BlockSpec((1, FF_CHUNK, D_MODEL), lambda c, i: (c, 0, 0))],
        out_shape=[jax.ShapeDtypeStruct((N_CHIPS, D_MODEL, FF_CHUNK), F32),
                   jax.ShapeDtypeStruct((N_CHIPS, FF_CHUNK, D_MODEL), F32)],
        compiler_params=_params("parallel", "arbitrary"),
    )(u1, da1, r1, df2)


def _outproj_bwd(dmix, w_out, attn, rec, token):
    tp = dmix.shape[0]
    tr = _row_tile(tp)

    def body(dm_ref, w_ref, a_ref, r_ref, _, da_ref, dr_ref, dw_ref):
        i = pl.program_id(0)
        dm = dm_ref[...]
        dcat = _dot_nt(dm, w_ref[...])
        da_ref[...] = dcat[:, :ATTN_WIDTH].astype(BF16)
        dr_ref[...] = dcat[:, ATTN_WIDTH:]
        pa = _dot_tn(a_ref[...], dm)
        pr = _dot_tn(r_ref[...], dm)

        @pl.when(i == 0)
        def _():
            dw_ref[:ATTN_WIDTH] = pa
            dw_ref[ATTN_WIDTH:] = pr

        @pl.when(i > 0)
        def _():
            dw_ref[:ATTN_WIDTH] += pa
            dw_ref[ATTN_WIDTH:] += pr

    row = lambda w: pl.BlockSpec((tr, w), lambda i: (i, 0))
    full = pl.BlockSpec((D_MODEL, D_MODEL), lambda i: (0, 0))
    return pl.pallas_call(
        body, name="outproj_bwd", grid=(tp // tr,),
        in_specs=[row(D_MODEL), full, row(ATTN_WIDTH), row(LRU_WIDTH), pl.BlockSpec(token.shape, lambda i: (0, 0))],
        out_specs=[row(ATTN_WIDTH), row(LRU_WIDTH), full],
        out_shape=[jax.ShapeDtypeStruct((tp, ATTN_WIDTH), BF16), jax.ShapeDtypeStruct((tp, LRU_WIDTH), F32),
                   jax.ShapeDtypeStruct((D_MODEL, D_MODEL), F32)],
        compiler_params=_params("arbitrary"),
    )(dmix, w_out, attn, rec, token)


N_VEC_ROWS = 8


def _lru_bwd(xr, yr, hr, drec, conv_w, conv_b, wa, ba, wx, bx, lam):
    tp = xr.shape[0]
    nb = tp // BLOCK

    def body(xr_ref, xh_ref, yr_ref, hr_ref, hp_ref, dr_ref, cw_ref, cb_ref, wa_ref, ba_ref, wx_ref, bx_ref, lam_ref,
             dxr_ref, dyr_ref, dwa_ref, dwx_ref, vec_ref, g_next, a_next, dxc_next, dsp):
        s = pl.program_id(0)
        n = nb - 1 - s

        @pl.when(s == 0)
        def _():
            g_next[...] = jnp.zeros_like(g_next)
            a_next[...] = jnp.zeros_like(a_next)
            dxc_next[...] = jnp.zeros_like(dxc_next)
            dsp[...] = jnp.zeros_like(dsp)
            dwa_ref[...] = jnp.zeros_like(dwa_ref)
            dwx_ref[...] = jnp.zeros_like(dwx_ref)
            vec_ref[...] = jnp.zeros_like(vec_ref)

        first = n == 0
        x = xr_ref[...]
        taps = _conv_taps(x, jnp.where(first, 0.0, xh_ref[...]))
        cw = cw_ref[...]
        xc = cb_ref[...] + sum(cw[k:k + 1] * taps[k] for k in range(4))
        lam_v = lam_ref[...]
        sp = _softplus(-lam_v)
        wa_m, wx_m = wa_ref[...], wx_ref[...]
        xb, r, ig, a, mult = _lru_gates(xc, wa_m, ba_ref[...], wx_m, bx_ref[...], sp)

        yr_v = yr_ref[...]
        gl, th = _gelu(yr_v)
        h = hr_ref[...]
        drec = dr_ref[...]
        dyr_ref[...] = (drec * h * _gelu_grad(yr_v, th)).astype(BF16)
        dh_direct = drec * gl

        rows = lax.broadcasted_iota(jnp.int32, a.shape, 0)
        a_up = jnp.where(rows == BLOCK - 1, a_next[0:1], pltpu.roll(a, BLOCK - 1, 0))
        cprod, gloc = _scan_rev(a_up, dh_direct)
        g = gloc + cprod * g_next[0:1]
        g_next[0:1] = g[0:1]
        a_next[0:1] = a[0:1]

        real = (n * BLOCK + rows) >= PAD_ROWS
        h_prev = jnp.where(rows == 0, jnp.where(first, 0.0, hp_ref[7:8]), pltpu.roll(h, 1, 0))
        du = jnp.where(real, g, 0.0)
        da = g * h_prev
        dmult = du * (ig * xc)
        dig = du * (mult * xc)
        dxc = du * (mult * ig)
        dlog_a = jnp.where(real, da * a - dmult * (a * a / mult), 0.0)
        dgr = (dlog_a * (-LRU_C * sp)) * (r * (1.0 - r))
        dgi = dig * (ig * (1.0 - ig))
        dsp[0:1] += jnp.sum(dlog_a * (-LRU_C * r), axis=0, keepdims=True)
        dgr_b, dgi_b = dgr.astype(BF16), dgi.astype(BF16)
        dxc = dxc + _dot_nt(dgr_b, wa_m) + _dot_nt(dgi_b, wx_m)
        dwa_ref[...] += _dot_tn(xb, dgr_b)
        dwx_ref[...] += _dot_tn(xb, dgi_b)

        ext = jnp.concatenate([dxc, dxc_next[...]], axis=0)
        up = [ext[:BLOCK] if j == 0 else pltpu.roll(ext, BLOCK + 8 - j, 0)[:BLOCK] for j in range(4)]
        dxr_ref[...] = sum(cw[k:k + 1] * up[3 - k] for k in range(4)).astype(BF16)
        dxc_next[...] = dxc[:8]

        col = lambda v: jnp.sum(v, axis=0, keepdims=True)
        for k in range(4):
            vec_ref[k:k + 1] += col(dxc * taps[k])
        vec_ref[4:5] += col(dxc)
        vec_ref[5:6] += col(dgr)
        vec_ref[6:7] += col(dgi)

        @pl.when(s == nb - 1)
        def _():
            vec_ref[7:8] = dsp[0:1] * (-_sigmoid(-lam_v))

    blk = pl.BlockSpec((BLOCK, LRU_WIDTH), lambda s: (nb - 1 - s, 0))
    rows_before = pl.BlockSpec((8, LRU_WIDTH), lambda s: (jnp.maximum((nb - 1 - s) * (BLOCK // 8) - 1, 0), 0))
    full = lambda a: pl.BlockSpec(a.shape, lambda s: (0,) * a.ndim)
    small = [conv_w, conv_b, wa, ba, wx, bx, lam]
    sq = pl.BlockSpec((LRU_WIDTH, LRU_WIDTH), lambda s: (0, 0))
    return pl.pallas_call(
        body, name="lru_bwd", grid=(nb,),
        in_specs=[blk, rows_before, blk, blk, rows_before, blk] + [full(a) for a in small],
        out_specs=[blk, blk, sq, sq, pl.BlockSpec((N_VEC_ROWS, LRU_WIDTH), lambda s: (0, 0))],
        out_shape=[jax.ShapeDtypeStruct((tp, LRU_WIDTH), BF16), jax.ShapeDtypeStruct((tp, LRU_WIDTH), BF16),
                   jax.ShapeDtypeStruct((LRU_WIDTH, LRU_WIDTH), F32), jax.ShapeDtypeStruct((LRU_WIDTH, LRU_WIDTH), F32),
                   jax.ShapeDtypeStruct((N_VEC_ROWS, LRU_WIDTH), F32)],
        scratch_shapes=[pltpu.VMEM((8, LRU_WIDTH), F32)] * 4,
        compiler_params=_params("arbitrary"),
    )(xr, xr, yr, hr, hr, drec, *small)


def _attn_bwd(qkv, dattn, sinks):
    tp = qkv.shape[0]
    tr = _row_tile(tp)
    qb, nt = tr // BLOCK, tp // tr
    n_groups = ATTN_HEADS // GQA_GROUP
    sink_rows, bias = _attn_consts(sinks)

    def body(s_ref, b_ref, q_ref, kp_ref, kc_ref, vp_ref, vc_ref, do_ref, dq_ref, dkv_ref, ex_ref, ds_ref, dsink):
        t = pl.program_id(0)

        @pl.when(t == 0)
        def _():
            dsink[...] = jnp.zeros_like(dsink)

        k_all = jnp.concatenate([kp_ref[...], kc_ref[...]], axis=0)
        v_all = jnp.concatenate([vp_ref[...], vc_ref[...]], axis=0)
        tail = None
        for i in range(qb):
            rows = slice(i * BLOCK, (i + 1) * BLOCK)
            q, do = q_ref[rows], do_ref[rows]
            k2, v2 = k_all[i * BLOCK:(i + 2) * BLOCK], v_all[i * BLOCK:(i + 2) * BLOCK]
            bias_n = _block_bias(b_ref, t, qb, i)
            dqs, dks, dvs = [], [], []
            for g in range(n_groups):
                cols = slice(g * HEAD_DIM, (g + 1) * HEAD_DIM)
                k_g, v_g = k2[:, cols], v2[:, cols]
                qg = _stack_heads(q, g) * jnp.asarray(_QSCALE, BF16)
                dog = _stack_heads(do, g)
                p, ps = _attn_probs_t(k_g, qg, bias_n, s_ref[g:g + 1])
                dpt = _dot_nt(v_g, dog)
                delta = jnp.sum(p * dpt, axis=0, keepdims=True)
                dst = (p * (dpt - delta)).astype(BF16)
                dqs.append(_dot_tn(dst, k_g) * _QSCALE)
                dks.append(_dot(dst, qg))
                dvs.append(_dot(p.astype(BF16), dog))
                dsink[g:g + 1] -= ps * delta
            dq_ref[rows] = _unstack_heads(dqs).astype(BF16)
            dkv = jnp.concatenate(dks + dvs, axis=1)
            if i == 0:
                ex_ref[0] = dkv[:BLOCK]
            else:
                dkv_ref[(i - 1) * BLOCK:i * BLOCK] = (tail + dkv[:BLOCK]).astype(BF16)
            tail = dkv[BLOCK:]
        dkv_ref[(qb - 1) * BLOCK:] = tail.astype(BF16)

        @pl.when(t == nt - 1)
        def _():
            lane = lax.broadcasted_iota(jnp.int32, (1, ATTN_HEADS), 1)
            acc = jnp.zeros((1, ATTN_HEADS), F32)
            for h in range(ATTN_HEADS):
                g, hh = divmod(h, GQA_GROUP)
                acc = acc + jnp.where(lane == h, jnp.sum(dsink[g:g + 1, hh * BLOCK:(hh + 1) * BLOCK]), 0.0)
            ds_ref[...] = acc

    cur = lambda w: pl.BlockSpec((tr, w), lambda t: (t, 0))
    return pl.pallas_call(
        body, name="attn_bwd", grid=(nt,),
        in_specs=[_SINK_SPEC, _BIAS_SPEC, cur(ATTN_WIDTH)] + _kv_specs(tr) + [cur(ATTN_WIDTH)],
        out_specs=[cur(ATTN_WIDTH), cur(2 * KV_WIDTH), pl.BlockSpec((1, BLOCK, 2 * KV_WIDTH), lambda t: (t, 0, 0)),
                   pl.BlockSpec((1, ATTN_HEADS), lambda t: (0, 0))],
        out_shape=[jax.ShapeDtypeStruct((tp, ATTN_WIDTH), BF16), jax.ShapeDtypeStruct((tp, 2 * KV_WIDTH), BF16),
                   jax.ShapeDtypeStruct((nt, BLOCK, 2 * KV_WIDTH), F32), jax.ShapeDtypeStruct((1, ATTN_HEADS), F32)],
        scratch_shapes=[pltpu.VMEM((n_groups, GROUP_ROWS), F32)],
        compiler_params=_params("arbitrary"),
    )(sink_rows, bias, qkv, qkv, qkv, qkv, qkv, dattn)


def _inproj_bwd(dq, dkv, dkv_extra, dxr, dyr, w_in, u0, head, x, dh1, g):
    tp = dq.shape[0]
    tr = _row_tile(tp)
    nt, qb = tp // tr, tr // BLOCK

    def body(*refs):
        dq_ref, dkv_ref, ex_ref, dxr_ref, dyr_ref, w_ref, u_ref, head_ref = refs[:8]
        pieces = refs[8:8 + qb]
        dh1_ref, g_ref, gx_ref, dhead_ref, dw_ref, dg_ref, buf, sems = refs[8 + qb:]
        i = pl.program_id(0)
        slot = i % 2

        def out_copy(step, at):
            return pltpu.make_async_copy(buf.at[at], gx_ref.at[pl.ds(step * tr - BLOCK, tr)], sems.at[at])

        extra = jnp.where(i < nt - 1, ex_ref[0], 0.0)
        last = (dkv_ref[tr - BLOCK:].astype(F32) + extra).astype(BF16)
        dkv = last if tr == BLOCK else jnp.concatenate([dkv_ref[:tr - BLOCK], last], axis=0)
        dz = jnp.concatenate([dq_ref[...], dkv, dxr_ref[...], dyr_ref[...]], axis=1)
        du = _dot_nt(dz, w_ref[...])
        hhat, rs = _rms(_seq_tile(head_ref[...], pieces, i))
        dx, dg = _rms_bwd(hhat, rs, g_ref[...], du)
        dh0 = dh1_ref[...] + dx

        @pl.when(i >= 3)
        def _():
            out_copy(i - 2, slot).wait()

        buf[slot] = dh0

        @pl.when(i == 0)
        def _():
            dhead_ref[...] = dh0[:BLOCK]
            if tr > BLOCK:
                first = pltpu.make_async_copy(buf.at[0, pl.ds(BLOCK, tr - BLOCK)], gx_ref.at[pl.ds(0, tr - BLOCK)],
                                              sems.at[0])
                first.start()
                first.wait()

        @pl.when(i >= 1)
        def _():
            out_copy(i, slot).start()

        @pl.when(i == nt - 1)
        def _():
            if nt >= 3:
                out_copy(nt - 2, (nt - 2) % 2).wait()
            if nt >= 2:
                out_copy(nt - 1, (nt - 1) % 2).wait()

        pw = _dot_tn(u_ref[...], dz)

        @pl.when(i == 0)
        def _():
            dw_ref[...] = pw
            dg_ref[...] = dg

        @pl.when(i > 0)
        def _():
            dw_ref[...] += pw
            dg_ref[...] += dg

    row = lambda w: pl.BlockSpec((tr, w), lambda i: (i, 0))
    full = lambda shape: pl.BlockSpec(shape, lambda i: (0,) * len(shape))
    return pl.pallas_call(
        body, name="inproj_bwd", grid=(tp // tr,),
        in_specs=[row(ATTN_WIDTH), row(2 * KV_WIDTH),
                  pl.BlockSpec((1, BLOCK, 2 * KV_WIDTH), lambda i: (jnp.minimum(i + 1, nt - 1), 0, 0)),
                  row(LRU_WIDTH), row(LRU_WIDTH), full(w_in.shape), row(D_MODEL), full(head.shape)]
        + _seq_specs(tr) + [row(D_MODEL), full(g.shape)],
        out_specs=[pl.BlockSpec(memory_space=pl.ANY), full((BLOCK, D_MODEL)), full((D_MODEL, IN_WIDTH)),
                   full((1, D_MODEL))],
        out_shape=[jax.ShapeDtypeStruct(x.shape, F32), jax.ShapeDtypeStruct((BLOCK, D_MODEL), F32),
                   jax.ShapeDtypeStruct((D_MODEL, IN_WIDTH), F32), jax.ShapeDtypeStruct((1, D_MODEL), F32)],
        scratch_shapes=[pltpu.VMEM((2, tr, D_MODEL), F32), pltpu.SemaphoreType.DMA((2,))],
        compiler_params=_params("arbitrary"),
    )(dq, dkv, dkv_extra, dxr, dyr, w_in, u0, head, *([x] * qb), dh1, g)


def _dense_block_diag(w):
    eye = jnp.eye(LRU_BLOCKS, dtype=w.dtype)
    return (w[:, :, None, :] * eye[:, None, :, None]).reshape(LRU_WIDTH, LRU_WIDTH)


def _diag_blocks(dense):
    d4 = dense.reshape(LRU_BLOCKS, LRU_BLOCK, LRU_BLOCKS, LRU_BLOCK)
    return jnp.stack([d4[n, :, n, :] for n in range(LRU_BLOCKS)])


def _local_step(head, x, tgt, g_pre_mix, w_in, conv_w, conv_b, w_a, b_a, w_x, b_x, lam, sinks, g_post_mix,
                g_pre_ffn, g_post_ffn, late_weights, on_ffn_grads, token):
    wa = _dense_block_diag(w_a).astype(BF16)
    wx = _dense_block_diag(w_x).astype(BF16)

    u0, qkv, xr, yr = _inproj_fwd(head, x, g_pre_mix, w_in, token)
    attn = _attn_fwd(qkv, sinks)
    hr, rec = _lru_fwd(xr, yr, conv_w, conv_b, wa, b_a, wx, b_x, lam)
    w_out, w1, w2 = late_weights([attn, rec])
    mix, h1, u1 = _outproj_fwd(attn, rec, w_out, head, x, g_post_mix, g_pre_ffn)
    r1, dy, df2, loss, dg_post_ffn = _ffn_fwd(u1, w1, w2, h1, tgt, g_post_ffn)

    da1, dh1, dmix, dg_pre_ffn, dg_post_mix = _ffn_bwd_data(df2, r1, w1, w2, dy, h1, mix, g_pre_ffn, g_post_mix)
    dw1, dw2 = _ffn_bwd_weights(u1, da1, r1, df2)
    token2 = on_ffn_grads(dw1, dw2)
    dattn, drec, dw_out = _outproj_bwd(dmix, w_out, attn, rec, token2)
    dxr, dyr, dwa, dwx, vec = _lru_bwd(xr, yr, hr, drec, conv_w, conv_b, wa, b_a, wx, b_x, lam)
    dq, dkv, dkv_extra, dsinks = _attn_bwd(qkv, dattn, sinks)
    dx, dhead, dw_in, dg_pre_mix = _inproj_bwd(dq, dkv, dkv_extra, dxr, dyr, w_in, u0, head, x, dh1, g_pre_mix)

    grads = dict(
        g_pre_mix=dg_pre_mix, w_in=dw_in, conv_w=vec[0:4], conv_b=vec[4:5], w_a=_diag_blocks(dwa), b_a=vec[5:6],
        w_x=_diag_blocks(dwx), b_x=vec[6:7], lru_lambda=vec[7:8], attn_sinks=dsinks, w_out=dw_out,
        g_post_mix=dg_post_mix, g_pre_ffn=dg_pre_ffn, w_ff1=dw1, w_ff2=dw2, g_post_ffn=dg_post_ffn)
    return loss, dx, dhead, grads


HBM = pl.BlockSpec(memory_space=pltpu.HBM)


def _mesh_pos():
    return lax.axis_index("x"), lax.axis_index("y"), lax.axis_index("c")


def _other_chips(x, y):
    return [(1 - x, y), (x, 1 - y), (1 - x, 1 - y)]


def _remote(src, dst, send_sem, recv_sem, to):
    return pltpu.make_async_remote_copy(src_ref=src, dst_ref=dst, send_sem=send_sem, recv_sem=recv_sem,
                                        device_id=to, device_id_type=MESH)


def _gather_weights(shards, tiny):
    nbig = len(shards)

    def body(*refs):
        srcs, tiny_src = refs[:nbig], refs[nbig]
        outs, tiny_out = refs[nbig + 1:2 * nbig + 1], refs[2 * nbig + 1]
        local_sems, ici_send, ici_recv, d2d_send, d2d_recv, tiny_send, tiny_recv = refs[2 * nbig + 2:]
        x, y, c = _mesh_pos()
        me = 2 * x + y
        chips = _other_chips(x, y)
        sibling = (x, y, 1 - c)
        started = []
        for w, (src, out) in enumerate(zip(srcs, outs)):
            lc = pltpu.make_async_copy(src, out.at[me], local_sems.at[w])
            lc.start()
            started.append(lc)
        lt = pltpu.make_async_copy(tiny_src, tiny_out.at[me], local_sems.at[nbig])
        lt.start()
        sends = []
        for w, (src, out) in enumerate(zip(srcs, outs)):
            hr = src.shape[0] // 2
            for j, chip in enumerate(chips):
                k = 3 * w + j
                cp = _remote(src.at[pl.ds(c * hr, hr)], out.at[me, pl.ds(c * hr, hr)],
                             ici_send.at[k], ici_recv.at[k], (*chip, c))
                cp.start()
                sends.append(cp)
        for j, chip in enumerate(chips):
            cp = _remote(tiny_src, tiny_out.at[me], tiny_send.at[j], tiny_recv.at[j], (*chip, c))
            cp.start()
            sends.append(cp)
        for w, (src, out) in enumerate(zip(srcs, outs)):
            hr = src.shape[0] // 2
            for j, (px, py) in enumerate(chips):
                k = 3 * w + j
                landed = out.at[2 * px + py, pl.ds(c * hr, hr)]
                _remote(landed, landed, ici_send.at[k], ici_recv.at[k], sibling).wait_recv()
                cp = _remote(landed, landed, d2d_send.at[k], d2d_recv.at[k], sibling)
                cp.start()
                sends.append(cp)
        for w, (src, out) in enumerate(zip(srcs, outs)):
            hr = src.shape[0] // 2
            for j, (px, py) in enumerate(chips):
                k = 3 * w + j
                other = out.at[2 * px + py, pl.ds((1 - c) * hr, hr)]
                _remote(other, other, d2d_send.at[k], d2d_recv.at[k], sibling).wait_recv()
        for j, (px, py) in enumerate(chips):
            blk = tiny_out.at[2 * px + py]
            _remote(blk, blk, tiny_send.at[j], tiny_recv.at[j], sibling).wait_recv()
        for cp in sends:
            cp.wait_send()
        for lc in started:
            lc.wait()
        lt.wait()

    out_shape = [jax.ShapeDtypeStruct((N_CHIPS,) + s.shape, s.dtype) for s in shards]
    out_shape.append(jax.ShapeDtypeStruct((N_CHIPS,) + tiny.shape, tiny.dtype))
    n = 3 * nbig
    return pl.pallas_call(
        body, name="gather_weights", out_shape=out_shape,
        in_specs=[HBM] * (nbig + 1), out_specs=[HBM] * (nbig + 1),
        scratch_shapes=[pltpu.SemaphoreType.DMA((nbig + 1,)), pltpu.SemaphoreType.DMA((n,)),
                        pltpu.SemaphoreType.DMA((n,)), pltpu.SemaphoreType.DMA((n,)), pltpu.SemaphoreType.DMA((n,)),
                        pltpu.SemaphoreType.DMA((3,)), pltpu.SemaphoreType.DMA((3,))],
    )(*shards, tiny)


N_DEV = 8


def _gather_small(block):
    m_per, n = block.shape

    def body(x_ref, out_ref, send_sems, recv_sems, local_sem):
        x, y, c = _mesh_pos()
        me, sibling = (x, y, c), (x, y, 1 - c)
        chips = _other_chips(x, y)

        def rows(px, py, pc):
            return out_ref.at[pl.ds((4 * px + 2 * py + pc) * m_per, m_per), :]

        def copy(k, block_of, to, src=None):
            return _remote(rows(*block_of) if src is None else src, rows(*block_of),
                           send_sems.at[k], recv_sems.at[k], to)

        mine = pltpu.make_async_copy(x_ref, rows(*me), local_sem)
        mine.start()
        first = [copy(0, me, sibling, src=x_ref)]
        first += [copy(1 + j, me, (*chip, c), src=x_ref) for j, chip in enumerate(chips)]
        for cp in first:
            cp.start()
        passed = [copy(4 + j, (*chip, c), sibling) for j, chip in enumerate(chips)]
        for j, chip in enumerate(chips):
            copy(1 + j, (*chip, c), me).wait_recv()
            passed[j].start()
        copy(0, sibling, me).wait_recv()
        for j, chip in enumerate(chips):
            copy(4 + j, (*chip, 1 - c), me).wait_recv()
        for cp in first + passed:
            cp.wait_send()
        mine.wait()

    return pl.pallas_call(
        body, name="gather_small", out_shape=jax.ShapeDtypeStruct((N_DEV * m_per, n), block.dtype),
        in_specs=[pl.BlockSpec(memory_space=pltpu.VMEM)], out_specs=pl.BlockSpec(memory_space=pltpu.VMEM),
        scratch_shapes=[pltpu.SemaphoreType.DMA((7,)), pltpu.SemaphoreType.DMA((7,)), pltpu.SemaphoreType.DMA],
        compiler_params=pltpu.CompilerParams(vmem_limit_bytes=VMEM_LIMIT_V7X),
    )(block)


def _sibling_exchange(parts):
    def body(*refs):
        n = len(parts)
        srcs, outs, send_sems, recv_sems = refs[:n], refs[n:2 * n], refs[2 * n], refs[2 * n + 1]
        x, y, c = _mesh_pos()
        sibling = (x, y, 1 - c)
        cps = []
        for w, (src, out) in enumerate(zip(srcs, outs)):
            hr = src.shape[1] // 2
            cp = _remote(src.at[:, pl.ds((1 - c) * hr, hr)], out, send_sems.at[w], recv_sems.at[w], sibling)
            cp.start()
            cps.append(cp)
        for cp in cps:
            cp.wait()

    n = len(parts)
    return pl.pallas_call(
        body, name="sibling_exchange",
        out_shape=[jax.ShapeDtypeStruct((p.shape[0], p.shape[1] // 2, p.shape[2]), p.dtype) for p in parts],
        in_specs=[HBM] * n, out_specs=[HBM] * n,
        scratch_shapes=[pltpu.SemaphoreType.DMA((n,)), pltpu.SemaphoreType.DMA((n,))],
    )(*parts)


def _chip_presum(part, from_sibling, c):
    _, hr, cols = from_sibling.shape
    tr = 256 if hr % 256 == 0 else hr
    steps = hr // tr

    def body(c_ref, a_ref, b_ref, o_ref):
        o_ref[...] = (a_ref[...] + b_ref[...]).astype(BF16)

    return pl.pallas_call(
        body, name="chip_presum",
        grid_spec=pltpu.PrefetchScalarGridSpec(
            num_scalar_prefetch=1, grid=(N_CHIPS, steps),
            in_specs=[pl.BlockSpec((1, tr, cols), lambda j, i, c_ref: (j, c_ref[0] * steps + i, 0)),
                      pl.BlockSpec((1, tr, cols), lambda j, i, c_ref: (j, i, 0))],
            out_specs=pl.BlockSpec((1, tr, cols), lambda j, i, c_ref: (j, i, 0))),
        out_shape=jax.ShapeDtypeStruct(from_sibling.shape, BF16),
        compiler_params=_params("parallel", "parallel"),
    )(c, part, from_sibling)


def _scatter_partials(cparts, done_cparts=(), done_lands=()):
    n_new = len(cparts)
    nw = n_new + len(done_cparts)

    def body(*refs):
        srcs = refs[:nw]
        outs = refs[nw + len(done_lands):2 * nw + len(done_lands)]
        local_sems, own_send, own_recv, ici_send, ici_recv, d2d_send, d2d_recv = refs[2 * nw + len(done_lands):]
        x, y, c = _mesh_pos()
        me = 2 * x + y
        chips = _other_chips(x, y)
        sibling = (x, y, 1 - c)
        locals_, sends = [], []
        for w, (src, out) in enumerate(zip(srcs, outs)):
            hr = src.shape[1]
            mine = out.at[me, pl.ds(c * hr, hr)]
            lc = pltpu.make_async_copy(src.at[me], mine, local_sems.at[w])
            lc.start()
            locals_.append(lc)
            cp = _remote(src.at[me], mine, own_send.at[w], own_recv.at[w], sibling)
            cp.start()
            sends.append(cp)
            for j, (px, py) in enumerate(chips):
                if w >= n_new:
                    break
                k = 3 * w + j
                cp = _remote(src.at[2 * px + py], mine, ici_send.at[k], ici_recv.at[k], (px, py, c))
                cp.start()
                sends.append(cp)
        for w, (src, out) in enumerate(zip(srcs, outs)):
            hr = src.shape[1]
            for j, (px, py) in enumerate(chips):
                k = 3 * w + j
                landed = out.at[2 * px + py, pl.ds(c * hr, hr)]
                if w < n_new:
                    _remote(landed, landed, ici_send.at[k], ici_recv.at[k], sibling).wait_recv()
                cp = _remote(landed, landed, d2d_send.at[k], d2d_recv.at[k], sibling)
                cp.start()
                sends.append(cp)
        for w, (src, out) in enumerate(zip(srcs, outs)):
            hr = src.shape[1]
            other = out.at[me, pl.ds((1 - c) * hr, hr)]
            _remote(other, other, own_send.at[w], own_recv.at[w], sibling).wait_recv()
            for j, (px, py) in enumerate(chips):
                k = 3 * w + j
                other = out.at[2 * px + py, pl.ds((1 - c) * hr, hr)]
                _remote(other, other, d2d_send.at[k], d2d_recv.at[k], sibling).wait_recv()
        for cp in sends:
            cp.wait_send()
        for lc in locals_:
            lc.wait()

    n = 3 * nw
    dma = pltpu.SemaphoreType.DMA
    every = list(cparts) + list(done_cparts)
    return pl.pallas_call(
        body, name="scatter_partials",
        out_shape=[jax.ShapeDtypeStruct((N_CHIPS, 2 * p.shape[1], p.shape[2]), p.dtype) for p in every],
        in_specs=[HBM] * (nw + len(done_lands)), out_specs=[HBM] * nw,
        input_output_aliases={nw + d: n_new + d for d in range(len(done_lands))},
        scratch_shapes=[dma((nw,)), dma((nw,)), dma((nw,)), dma((n,)), dma((n,)), dma((n,)), dma((n,))],
    )(*every, *done_lands)


SEM = pl.BlockSpec(memory_space=pltpu.SEMAPHORE)
SPLIT_COPY = pltpu.CompilerParams(has_side_effects=pltpu.SideEffectType.DATAFLOW_SIDE_EFFECTING)


def _hbm(a):
    return pltpu.with_memory_space_constraint(a, pltpu.HBM)


def _gather_copies(srcs, lands, send_sems, recv_sems):
    x, y, c = _mesh_pos()
    me = 2 * x + y
    sends, recvs = [], []
    for w, (src, land) in enumerate(zip(srcs, lands)):
        hr = src.shape[0] // 2
        for j, (px, py) in enumerate(_other_chips(x, y)):
            k = 3 * w + j
            sends.append(_remote(src.at[pl.ds(c * hr, hr)], land.at[me, pl.ds(c * hr, hr)],
                                 send_sems.at[k], recv_sems.at[k], (px, py, c)))
            got = land.at[2 * px + py, pl.ds(c * hr, hr)]
            recvs.append(_remote(got, got, send_sems.at[k], recv_sems.at[k], (px, py, c)))
    return sends, recvs


def _scatter_copies(srcs, lands, send_sems, recv_sems):
    x, y, c = _mesh_pos()
    me = 2 * x + y
    sends, recvs = [], []
    for w, (src, land) in enumerate(zip(srcs, lands)):
        hr = src.shape[1]
        for j, (px, py) in enumerate(_other_chips(x, y)):
            k = 3 * w + j
            sends.append(_remote(src.at[2 * px + py], land.at[me, pl.ds(c * hr, hr)],
                                 send_sems.at[k], recv_sems.at[k], (px, py, c)))
            got = land.at[2 * px + py, pl.ds(c * hr, hr)]
            recvs.append(_remote(got, got, send_sems.at[k], recv_sems.at[k], (px, py, c)))
    return sends, recvs


def _split_start(name, copies_of, srcs, land_shapes):
    n = len(srcs)
    k = 3 * n

    def body(*refs):
        src_refs, land_refs = refs[:n], refs[n:2 * n]
        send_sems, recv_sems = refs[2 * n], refs[2 * n + 1]
        token = refs[-1]
        sends, _ = copies_of(src_refs, land_refs, send_sems, recv_sems)
        for cp in sends:
            cp.start()
        token[...] = jnp.zeros_like(token)

    lands = [_hbm(lax.empty(s.shape, s.dtype)) for s in land_shapes]
    dma = pltpu.SemaphoreType.DMA
    res = pl.pallas_call(
        body, name=name,
        out_shape=(dma((k,)), dma((k,)), *[pltpu.HBM(s.shape, s.dtype) for s in srcs],
                   *[pltpu.HBM(s.shape, s.dtype) for s in land_shapes], jax.ShapeDtypeStruct((8, 128), F32)),
        in_specs=[HBM] * (2 * n),
        out_specs=(SEM, SEM, *([HBM] * (2 * n)), pl.BlockSpec(memory_space=pltpu.VMEM)),
        input_output_aliases={i: 2 + i for i in range(2 * n)},
        compiler_params=SPLIT_COPY,
    )(*[_hbm(s) for s in srcs], *lands)
    return res[0], res[1], list(res[2:2 + n]), list(res[2 + n:2 + 2 * n]), res[-1]


def _split_wait(name, copies_of, send_sems, recv_sems, srcs, lands, after):
    n = len(srcs)

    def body(*refs):
        src_refs, land_refs = refs[:n], refs[n:2 * n]
        sends, recvs = copies_of(src_refs, land_refs, refs[2 * n], refs[2 * n + 1])
        for cp in sends:
            cp.wait_send()
        for cp in recvs:
            cp.wait_recv()

    res = pl.pallas_call(
        body, name=name,
        out_shape=tuple(pltpu.HBM(s.shape, s.dtype) for s in list(srcs) + list(lands)),
        in_specs=[HBM] * (2 * n) + [SEM, SEM] + [pl.BlockSpec(memory_space=pl.ANY)] * len(after),
        out_specs=tuple([HBM] * (2 * n)),
        input_output_aliases={i: i for i in range(2 * n)},
        compiler_params=SPLIT_COPY,
    )(*srcs, *lands, send_sems, recv_sems, *after)
    return list(res[:n]), list(res[n:])


def _gather_finish(shards, lands):
    n = len(shards)

    def body(*refs):
        srcs, outs = refs[:n], refs[2 * n:3 * n]
        local_sems, d2d_send, d2d_recv = refs[3 * n:]
        x, y, c = _mesh_pos()
        me = 2 * x + y
        chips = _other_chips(x, y)
        sibling = (x, y, 1 - c)
        locals_, sends = [], []
        for w, (src, out) in enumerate(zip(srcs, outs)):
            lc = pltpu.make_async_copy(src, out.at[me], local_sems.at[w])
            lc.start()
            locals_.append(lc)
            hr = src.shape[0] // 2
            for j, (px, py) in enumerate(chips):
                landed = out.at[2 * px + py, pl.ds(c * hr, hr)]
                cp = _remote(landed, landed, d2d_send.at[3 * w + j], d2d_recv.at[3 * w + j], sibling)
                cp.start()
                sends.append(cp)
        for w, (src, out) in enumerate(zip(srcs, outs)):
            hr = src.shape[0] // 2
            for j, (px, py) in enumerate(chips):
                other = out.at[2 * px + py, pl.ds((1 - c) * hr, hr)]
                _remote(other, other, d2d_send.at[3 * w + j], d2d_recv.at[3 * w + j], sibling).wait_recv()
        for cp in sends:
            cp.wait_send()
        for lc in locals_:
            lc.wait()

    dma = pltpu.SemaphoreType.DMA
    return pl.pallas_call(
        body, name="gather_finish",
        out_shape=[jax.ShapeDtypeStruct(l.shape, l.dtype) for l in lands],
        in_specs=[HBM] * (2 * n), out_specs=[HBM] * n,
        input_output_aliases={n + i: i for i in range(n)},
        scratch_shapes=[dma((n,)), dma((3 * n,)), dma((3 * n,))],
    )(*shards, *lands)


def _adamw(w, g, m, v):
    m = ADAM_B1 * m + (1.0 - ADAM_B1) * g
    v = ADAM_B2 * v + (1.0 - ADAM_B2) * (g * g)
    m_hat = m / (1.0 - ADAM_B1 ** ADAM_STEP)
    v_hat = v / (1.0 - ADAM_B2 ** ADAM_STEP)
    delta = -ADAM_LR * (m_hat / (jnp.sqrt(v_hat) + ADAM_EPS) + ADAM_WD * w)
    return delta, m, v


def _adamw_big(partials, w, m, v):
    rows, cols = w.shape
    tr = 256

    def body(p_ref, w_ref, m_ref, v_ref, g_ref, d_ref, m2_ref, v2_ref):
        g = ((p_ref[0].astype(F32) + p_ref[1].astype(F32)) + p_ref[2].astype(F32)) + p_ref[3].astype(F32)
        g_ref[...] = g
        d_ref[...], m2_ref[...], v2_ref[...] = _adamw(w_ref[...], g, m_ref[...], v_ref[...])

    blk = pl.BlockSpec((tr, cols), lambda i: (i, 0))
    return pl.pallas_call(
        body, name="adamw_big", grid=(rows // tr,),
        in_specs=[pl.BlockSpec((N_CHIPS, tr, cols), lambda i: (0, i, 0)), blk, blk, blk],
        out_specs=[blk] * 4, out_shape=[jax.ShapeDtypeStruct((rows, cols), F32)] * 4,
        compiler_params=_params("parallel"),
    )(partials, w, m, v)


def _sum_devices(gathered, rows):
    cols = gathered.shape[1]

    def body(g_ref, o_ref):
        acc = g_ref[0:rows]
        for d in range(1, N_DEV):
            acc = acc + g_ref[d * rows:(d + 1) * rows]
        o_ref[...] = acc

    return pl.pallas_call(
        body, name="sum_devices", out_shape=jax.ShapeDtypeStruct((rows, cols), F32),
        in_specs=[pl.BlockSpec(memory_space=pltpu.VMEM)], out_specs=pl.BlockSpec(memory_space=pltpu.VMEM),
        compiler_params=pltpu.CompilerParams(vmem_limit_bytes=VMEM_LIMIT_V7X),
    )(gathered)


def _adamw_small(quads):
    n = len(quads)

    def body(*refs):
        ins, outs = refs[:4 * n], refs[4 * n:]
        for t in range(n):
            w, g, m, v = (r[...] for r in ins[4 * t:4 * t + 4])
            outs[3 * t][...], outs[3 * t + 1][...], outs[3 * t + 2][...] = _adamw(w, g, m, v)

    flat = [a for q in quads for a in q]
    vm = pl.BlockSpec(memory_space=pltpu.VMEM)
    res = pl.pallas_call(
        body, name="adamw_small",
        out_shape=[jax.ShapeDtypeStruct(q[0].shape, F32) for q in quads for _ in range(3)],
        in_specs=[vm] * (4 * n), out_specs=[vm] * (3 * n),
    )(*flat)
    return [tuple(res[3 * t:3 * t + 3]) for t in range(n)]


SMALL_PACK_ROWS = 96
_WEIGHTS = ['meta_tokens', 'g_pre_mix', 'w_in', 'conv_w', 'conv_b', 'w_a', 'b_a', 'w_x', 'b_x', 'lru_lambda',
            'attn_sinks', 'w_out', 'g_post_mix', 'g_pre_ffn', 'w_ff1', 'w_ff2', 'g_post_ffn']
_BIG = ['w_in', 'w_out', 'w_ff1', 'w_ff2']


def _pack_small(dmeta, g):
    z = lambda r, c: jnp.zeros((r, c), F32)
    rows = [
        dmeta,
        g['g_pre_mix'], g['g_post_mix'], g['g_pre_ffn'], g['g_post_ffn'],
        jnp.concatenate([g['conv_w'], z(4, 512)], axis=1),
        jnp.concatenate([g['conv_b'], g['b_a']], axis=1),
        jnp.concatenate([g['b_x'], g['lru_lambda']], axis=1),
        jnp.concatenate([g['attn_sinks'], z(1, D_MODEL - ATTN_HEADS)], axis=1),
        z(5, D_MODEL),
        g['w_a'].reshape(32, D_MODEL), g['w_x'].reshape(32, D_MODEL),
    ]
    return jnp.concatenate(rows, axis=0)


def _unpack_small(s, chip):
    return dict(
        meta_tokens=lax.dynamic_slice(s[0:16], (0, chip * 256), (16, 256)),
        g_pre_mix=s[16:17], g_post_mix=s[17:18], g_pre_ffn=s[18:19], g_post_ffn=s[19:20],
        conv_w=lax.dynamic_slice(s[20:24], (0, chip * 128), (4, 128)).reshape(1, 4, 128),
        conv_b=s[24:25, :512], b_a=s[24:25, 512:], b_x=s[25:26, :512], lru_lambda=s[25:26, 512:],
        attn_sinks=s[26:27, :ATTN_HEADS],
        w_a=s[32:64].reshape(1, LRU_BLOCKS, LRU_BLOCK, LRU_BLOCK),
        w_x=s[64:96].reshape(1, LRU_BLOCKS, LRU_BLOCK, LRU_BLOCK))


def _as2d(a):
    if a.ndim == 2:
        return a
    return a.reshape(-1, a.shape[-1])


def kernel(x, meta_tokens, g_pre_mix, w_in, conv_w, conv_b, w_a, b_a, w_x, b_x, lru_lambda, attn_sinks, w_out, g_post_mix, g_pre_ffn, w_ff1, w_ff2, g_post_ffn, loss_target, m_meta_tokens, m_g_pre_mix, m_w_in, m_conv_w, m_conv_b, m_w_a, m_b_a, m_w_x, m_b_x, m_lru_lambda, m_attn_sinks, m_w_out, m_g_post_mix, m_g_pre_ffn, m_w_ff1, m_w_ff2, m_g_post_ffn, v_meta_tokens, v_g_pre_mix, v_w_in, v_conv_w, v_conv_b, v_w_a, v_b_a, v_w_x, v_b_x, v_lru_lambda, v_attn_sinks, v_w_out, v_g_post_mix, v_g_pre_ffn, v_w_ff1, v_w_ff2, v_g_post_ffn):
    weights = dict(meta_tokens=meta_tokens, g_pre_mix=g_pre_mix, w_in=w_in, conv_w=conv_w, conv_b=conv_b, w_a=w_a,
                   b_a=b_a, w_x=w_x, b_x=b_x, lru_lambda=lru_lambda, attn_sinks=attn_sinks, w_out=w_out,
                   g_post_mix=g_post_mix, g_pre_ffn=g_pre_ffn, w_ff1=w_ff1, w_ff2=w_ff2, g_post_ffn=g_post_ffn)
    mom1 = dict(zip(_WEIGHTS, [m_meta_tokens, m_g_pre_mix, m_w_in, m_conv_w, m_conv_b, m_w_a, m_b_a, m_w_x, m_b_x,
                               m_lru_lambda, m_attn_sinks, m_w_out, m_g_post_mix, m_g_pre_ffn, m_w_ff1, m_w_ff2,
                               m_g_post_ffn]))
    mom2 = dict(zip(_WEIGHTS, [v_meta_tokens, v_g_pre_mix, v_w_in, v_conv_w, v_conv_b, v_w_a, v_b_a, v_w_x, v_b_x,
                               v_lru_lambda, v_attn_sinks, v_w_out, v_g_post_mix, v_g_pre_ffn, v_w_ff1, v_w_ff2,
                               v_g_post_ffn]))
    xi, yi, ci = _mesh_pos()
    chip = 2 * xi + yi

    tiny = jnp.concatenate([meta_tokens, jnp.pad(conv_w[0], ((0, 4), (0, 128)))], axis=0)
    shards = [w_in[0].astype(BF16), w_out[0].astype(BF16), w_ff1[0].astype(BF16), w_ff2[0].astype(BF16)]
    g_in, g_tiny = _gather_weights(shards[:1], tiny)
    w_in_full = jnp.concatenate([g_in[j] for j in range(N_CHIPS)], axis=1)
    meta_full = jnp.concatenate([g_tiny[j, :N_META] for j in range(N_CHIPS)], axis=1)
    conv_w_full = jnp.concatenate([g_tiny[j, N_META:N_META + 4, :128] for j in range(N_CHIPS)], axis=1)
    late = shards[1:]
    g_send, g_recv, late_thru, late_lands, token = _split_start(
        "gather_late_start", _gather_copies, late,
        [jax.ShapeDtypeStruct((N_CHIPS,) + s.shape, s.dtype) for s in late])

    def late_weights(after):
        own, lands = _split_wait("gather_late_wait", _gather_copies, g_send, g_recv, late_thru, late_lands, after)
        g_out, g_f1, g_f2 = _gather_finish(own, lands)
        return g_out.reshape(D_MODEL, D_MODEL), g_f1, g_f2

    c_arr = jnp.reshape(ci, (1,)).astype(jnp.int32)
    ffn = {}

    def chip_sums(parts):
        return [_chip_presum(p, r, c_arr) for p, r in zip(parts, _sibling_exchange(parts))]

    def on_ffn_grads(dw1, dw2):
        ffn['cparts'] = chip_sums([dw1, dw2])
        ffn['send'], ffn['recv'], ffn['thru'], ffn['lands'], token2 = _split_start(
            "scatter_ffn_start", _scatter_copies, ffn['cparts'],
            [jax.ShapeDtypeStruct((N_CHIPS, 2 * p.shape[1], p.shape[2]), p.dtype) for p in ffn['cparts']])
        return token2

    head = jnp.concatenate([jnp.zeros((PAD_ROWS, D_MODEL), F32), meta_full], axis=0)
    loss, dx, dhead, grads = _local_step(head, x[0], loss_target[0], g_pre_mix, w_in_full, conv_w_full, conv_b, w_a[0],
                                         b_a, w_x[0], b_x, lru_lambda, attn_sinks, g_post_mix, g_pre_ffn, g_post_ffn,
                                         late_weights, on_ffn_grads, token)
    loss = lax.psum(loss[0, 0], ("x", "y", "c"))
    grad_x = dx[None]

    gathered = _gather_small(_pack_small(dhead[PAD_ROWS:], grads))
    small = _unpack_small(_sum_devices(gathered, SMALL_PACK_ROWS), chip)

    dw_in = grads['w_in']
    cparts = chip_sums([jnp.stack([dw_in[:, j * 448:(j + 1) * 448] for j in range(N_CHIPS)]),
                        grads['w_out'].reshape(N_CHIPS, D_MODEL // N_CHIPS, D_MODEL)])
    ffn_cparts, ffn_lands = _split_wait("scatter_ffn_wait", _scatter_copies, ffn['send'], ffn['recv'], ffn['thru'],
                                        ffn['lands'], cparts)
    chip_partials = _scatter_partials(cparts, ffn_cparts, ffn_lands)

    g_out_d, delta, new_m, new_v = {}, {}, {}, {}
    for name, part in zip(_BIG, chip_partials):
        shp = weights[name].shape
        res = _adamw_big(part, weights[name][0], mom1[name][0], mom2[name][0])
        g_out_d[name], delta[name], new_m[name], new_v[name] = (r.reshape(shp) for r in res)
    small_names = [n for n in _WEIGHTS if n not in _BIG]
    quads = [(_as2d(weights[n]), _as2d(small[n]), _as2d(mom1[n]), _as2d(mom2[n])) for n in small_names]
    for name, (d, m2, v2) in zip(small_names, _adamw_small(quads)):
        shp = weights[name].shape
        g_out_d[name] = small[name].reshape(shp)
        delta[name], new_m[name], new_v[name] = d.reshape(shp), m2.reshape(shp), v2.reshape(shp)

    return (loss, grad_x, *[g_out_d[n] for n in _WEIGHTS], *[delta[n] for n in _WEIGHTS],
            *[new_m[n] for n in _WEIGHTS], *[new_v[n] for n in _WEIGHTS])
```

```python
import numpy as np
import jax
import jax.numpy as jnp
from jax import lax
from jax.experimental import pallas as pl
from jax.experimental.pallas import tpu as pltpu

F32 = jnp.float32
BF16 = jnp.bfloat16

D_MODEL = 1024
N_META = 16
BLOCK = 128
PAD_ROWS = BLOCK - N_META
HEAD_DIM = 64
ATTN_HEADS = 8
GQA_GROUP = 4
ATTN_WIDTH = 512
KV_WIDTH = 128
QKV_WIDTH = ATTN_WIDTH + 2 * KV_WIDTH
LRU_WIDTH = 512
LRU_BLOCKS = 8
LRU_BLOCK = 64
LRU_C = 8.0
IN_WIDTH = 1792
D_FF = 4096
N_CHIPS = 4
FF_CHUNK = D_FF // N_CHIPS
EPS = 1e-6
NEG = -1e30

ADAM_LR = 0.001
ADAM_B1 = 0.9
ADAM_B2 = 0.999
ADAM_EPS = 1e-08
ADAM_WD = 0.01
ADAM_STEP = 10

VMEM_LIMIT_V7X = 56 * 1024 * 1024
MESH = pl.DeviceIdType.MESH

NT = (((1,), (1,)), ((), ()))
TN = (((0,), (0,)), ((), ()))


def _row_tile(tp):
    return 640 if tp % 640 == 0 else BLOCK


def _params(*sem):
    return pltpu.CompilerParams(dimension_semantics=sem, vmem_limit_bytes=VMEM_LIMIT_V7X)


def _dot(a, b):
    return jnp.dot(a, b, preferred_element_type=F32)


def _dot_nt(a, b):
    return lax.dot_general(a, b, NT, preferred_element_type=F32)


def _dot_tn(a, b):
    return lax.dot_general(a, b, TN, preferred_element_type=F32)


def _rms(x):
    rs = lax.rsqrt(jnp.mean(x * x, axis=-1, keepdims=True) + EPS)
    return x * rs, rs


def _rms_bwd(xhat, rs, g, dy):
    dyg = dy * g
    dx = rs * (dyg - xhat * jnp.mean(dyg * xhat, axis=-1, keepdims=True))
    dg = jnp.sum(dy * xhat, axis=0, keepdims=True)
    return dx, dg


def _gelu(x):
    k = 0.7978845608028654
    t = jnp.tanh(k * (x + 0.044715 * x * x * x))
    return 0.5 * x * (1.0 + t), t


def _gelu_grad(x, t):
    k = 0.7978845608028654
    return 0.5 * (1.0 + t) + 0.5 * x * (1.0 - t * t) * k * (1.0 + 3 * 0.044715 * x * x)


def _sigmoid(x):
    return 1.0 / (1.0 + jnp.exp(-x))


def _neg_expm1(x):
    series = x * (1.0 + x * 0.5 * (1.0 + x * (1.0 / 3.0) * (1.0 + x * 0.25 * (1.0 + x * 0.2))))
    return -jnp.where(jnp.abs(x) < 0.05, series, jnp.exp(x) - 1.0)


def _softplus(x):
    return jnp.maximum(x, 0.0) + jnp.log1p(jnp.exp(-jnp.abs(x)))


def _seq_specs(tr):
    qb = tr // BLOCK
    return [pl.BlockSpec((BLOCK, D_MODEL), lambda i, *_, s=s: (jnp.maximum(i * qb + s - 1, 0), 0)) for s in range(qb)]


def _seq_tile(head, pieces, i):
    first = jnp.where(i == 0, head, pieces[0][...])
    return jnp.concatenate([first] + [p[...] for p in pieces[1:]], axis=0)


def _inproj_fwd(head, x, g, w_in, token):
    tp = BLOCK + x.shape[0]
    tr = _row_tile(tp)
    qb = tr // BLOCK

    def body(*refs):
        head_ref, pieces = refs[0], refs[1:1 + qb]
        g_ref, w_ref, _, u_ref, qkv_ref, xr_ref, yr_ref = refs[1 + qb:]
        xhat, _ = _rms(_seq_tile(head_ref[...], pieces, pl.program_id(0)))
        u = (xhat * g_ref[...]).astype(BF16)
        u_ref[...] = u
        z = _dot(u, w_ref[...])
        qkv_ref[...] = z[:, :QKV_WIDTH].astype(BF16)
        xr_ref[...] = z[:, QKV_WIDTH:QKV_WIDTH + LRU_WIDTH]
        yr_ref[...] = z[:, QKV_WIDTH + LRU_WIDTH:]

    row = lambda w: pl.BlockSpec((tr, w), lambda i: (i, 0))
    full = lambda a: pl.BlockSpec(a.shape, lambda i: (0,) * a.ndim)
    return pl.pallas_call(
        body, name="inproj_fwd", grid=(tp // tr,),
        in_specs=[full(head)] + _seq_specs(tr) + [full(g), full(w_in), full(token)],
        out_specs=[row(D_MODEL), row(QKV_WIDTH), row(LRU_WIDTH), row(LRU_WIDTH)],
        out_shape=[jax.ShapeDtypeStruct((tp, D_MODEL), BF16), jax.ShapeDtypeStruct((tp, QKV_WIDTH), BF16),
                   jax.ShapeDtypeStruct((tp, LRU_WIDTH), F32), jax.ShapeDtypeStruct((tp, LRU_WIDTH), F32)],
        compiler_params=_params("parallel"),
    )(head, *([x] * qb), g, w_in, token)


GROUP_ROWS = GQA_GROUP * BLOCK


def _attn_bias():
    j = np.arange(2 * BLOCK)[:, None]
    i = np.arange(BLOCK)[None, :]
    band = (j - i >= 1) & (j - i <= BLOCK)
    out = []
    for n in range(3):
        ok = band & ((n - 1) * BLOCK + j >= PAD_ROWS) if n < 2 else band
        out.append(np.tile(np.where(ok, 0.0, NEG).astype(np.float32), (1, GQA_GROUP)))
    return jnp.asarray(np.stack(out))


def _stack_heads(a, g):
    heads = range(GQA_GROUP * g, GQA_GROUP * (g + 1))
    return jnp.concatenate([a[:, h * HEAD_DIM:(h + 1) * HEAD_DIM] for h in heads], axis=0)


def _unstack_heads(groups):
    return jnp.concatenate([p[h * BLOCK:(h + 1) * BLOCK] for p in groups for h in range(GQA_GROUP)], axis=1)


def _attn_probs_t(k_g, qg, bias, sink_row):
    st = _dot_nt(k_g, qg) + bias
    m = jnp.maximum(jnp.max(st, axis=0, keepdims=True), sink_row)
    p = jnp.exp(st - m)
    es = jnp.exp(sink_row - m)
    inv = 1.0 / (jnp.sum(p, axis=0, keepdims=True) + es)
    return p * inv, es * inv


def _attn_consts(sinks):
    return jnp.repeat(sinks.reshape(ATTN_HEADS), BLOCK).reshape(ATTN_HEADS // GQA_GROUP, GROUP_ROWS), _attn_bias()


_SINK_SPEC = pl.BlockSpec((ATTN_HEADS // GQA_GROUP, GROUP_ROWS), lambda n: (0, 0))
_BIAS_SPEC = pl.BlockSpec((3, 2 * BLOCK, GROUP_ROWS), lambda n: (0, 0, 0))
_QSCALE = HEAD_DIM ** -0.5


def _kv_specs(tr):
    qb = tr // BLOCK
    prev = lambda col: pl.BlockSpec((BLOCK, KV_WIDTH), lambda t: (jnp.maximum(t * qb - 1, 0), col))
    cur = lambda col: pl.BlockSpec((tr, KV_WIDTH), lambda t: (t, col))
    return [prev(4), cur(4), prev(5), cur(5)]


def _block_bias(b_ref, t, qb, i):
    return b_ref[2] if i >= 2 else b_ref[jnp.minimum(t * qb + i, 2)]


def _attn_fwd(qkv, sinks):
    tp = qkv.shape[0]
    tr = _row_tile(tp)
    qb = tr // BLOCK
    sink_rows, bias = _attn_consts(sinks)

    def body(s_ref, b_ref, q_ref, kp_ref, kc_ref, vp_ref, vc_ref, o_ref):
        t = pl.program_id(0)
        k_all = jnp.concatenate([kp_ref[...], kc_ref[...]], axis=0)
        v_all = jnp.concatenate([vp_ref[...], vc_ref[...]], axis=0)
        for i in range(qb):
            rows = slice(i * BLOCK, (i + 1) * BLOCK)
            q = q_ref[rows]
            k2, v2 = k_all[i * BLOCK:(i + 2) * BLOCK], v_all[i * BLOCK:(i + 2) * BLOCK]
            bias_n = _block_bias(b_ref, t, qb, i)
            outs = []
            for g in range(ATTN_HEADS // GQA_GROUP):
                cols = slice(g * HEAD_DIM, (g + 1) * HEAD_DIM)
                qg = _stack_heads(q, g) * jnp.asarray(_QSCALE, BF16)
                p, _ = _attn_probs_t(k2[:, cols], qg, bias_n, s_ref[g:g + 1])
                outs.append(_dot_tn(p.astype(BF16), v2[:, cols]))
            o_ref[rows] = _unstack_heads(outs).astype(BF16)

    return pl.pallas_call(
        body, name="attn_fwd", grid=(tp // tr,),
        in_specs=[_SINK_SPEC, _BIAS_SPEC, pl.BlockSpec((tr, ATTN_WIDTH), lambda t: (t, 0))] + _kv_specs(tr),
        out_specs=pl.BlockSpec((tr, ATTN_WIDTH), lambda t: (t, 0)),
        out_shape=jax.ShapeDtypeStruct((tp, ATTN_WIDTH), BF16),
        compiler_params=_params("parallel"),
    )(sink_rows, bias, qkv, qkv, qkv, qkv, qkv)


def _conv_taps(x, halo):
    ext = jnp.concatenate([halo, x], axis=0)
    return [ext[8:] if k == 3 else pltpu.roll(ext, 3 - k, 0)[8:] for k in range(4)]


def _lru_gates(xc, wa, ba, wx, bx, sp):
    xb = xc.astype(BF16)
    r = _sigmoid(_dot(xb, wa) + ba)
    ig = _sigmoid(_dot(xb, wx) + bx)
    log_a = (-LRU_C * sp) * r
    a = jnp.exp(log_a)
    mult = jnp.sqrt(_neg_expm1(2.0 * log_a))
    return xb, r, ig, a, mult


def _scan_fwd(a, b):
    rows = lax.broadcasted_iota(jnp.int32, a.shape, 0)
    d = 1
    while d < a.shape[0]:
        keep = rows >= d
        b = jnp.where(keep, a * pltpu.roll(b, d, 0) + b, b)
        a = jnp.where(keep, a * pltpu.roll(a, d, 0), a)
        d *= 2
    return a, b


def _scan_rev(c, b):
    n = c.shape[0]
    rows = lax.broadcasted_iota(jnp.int32, c.shape, 0)
    d = 1
    while d < n:
        keep = rows < n - d
        b = jnp.where(keep, b + c * pltpu.roll(b, n - d, 0), b)
        c = jnp.where(keep, c * pltpu.roll(c, n - d, 0), c)
        d *= 2
    return c, b


def _lru_fwd(xr, yr, conv_w, conv_b, wa, ba, wx, bx, lam):
    tp = xr.shape[0]
    nb = tp // BLOCK

    def body(xr_ref, yr_ref, cw_ref, cb_ref, wa_ref, ba_ref, wx_ref, bx_ref, lam_ref, hr_ref, rec_ref, halo, hprev):
        n = pl.program_id(0)

        @pl.when(n == 0)
        def _():
            halo[...] = jnp.zeros_like(halo)
            hprev[...] = jnp.zeros_like(hprev)

        x = xr_ref[...]
        taps = _conv_taps(x, halo[...])
        halo[...] = x[BLOCK - 8:]
        cw = cw_ref[...]
        xc = cb_ref[...] + sum(cw[k:k + 1] * taps[k] for k in range(4))
        sp = _softplus(-lam_ref[...])
        _, _, ig, a, mult = _lru_gates(xc, wa_ref[...], ba_ref[...], wx_ref[...], bx_ref[...], sp)
        rows = n * BLOCK + lax.broadcasted_iota(jnp.int32, xc.shape, 0)
        u = jnp.where(rows >= PAD_ROWS, mult * (ig * xc), 0.0)
        acum, hloc = _scan_fwd(a, u)
        h = acum * hprev[0:1] + hloc
        hprev[0:1] = h[BLOCK - 1:]
        hr_ref[...] = h
        gl, _ = _gelu(yr_ref[...])
        rec_ref[...] = (gl * h).astype(BF16)

    blk = pl.BlockSpec((BLOCK, LRU_WIDTH), lambda n: (n, 0))
    full = lambda a: pl.BlockSpec(a.shape, lambda n: (0,) * a.ndim)
    small = [conv_w, conv_b, wa, ba, wx, bx, lam]
    return pl.pallas_call(
        body, name="lru_fwd", grid=(nb,),
        in_specs=[blk, blk] + [full(a) for a in small],
        out_specs=[blk, blk],
        out_shape=[jax.ShapeDtypeStruct((tp, LRU_WIDTH), F32), jax.ShapeDtypeStruct((tp, LRU_WIDTH), BF16)],
        scratch_shapes=[pltpu.VMEM((8, LRU_WIDTH), F32), pltpu.VMEM((8, LRU_WIDTH), F32)],
        compiler_params=_params("arbitrary"),
    )(xr, yr, *small)


def _outproj_fwd(attn, rec, w_out, head, x, g_post_mix, g_pre_ffn):
    tp = attn.shape[0]
    tr = _row_tile(tp)
    qb = tr // BLOCK

    def body(*refs):
        a_ref, r_ref, w_ref, head_ref = refs[:4]
        pieces = refs[4:4 + qb]
        gm_ref, gf_ref, mix_ref, h1_ref, u1_ref = refs[4 + qb:]
        mix = _dot(a_ref[...], w_ref[:ATTN_WIDTH]) + _dot(r_ref[...], w_ref[ATTN_WIDTH:])
        mix_ref[...] = mix
        mhat, _ = _rms(mix)
        h1 = _seq_tile(head_ref[...], pieces, pl.program_id(0)) + mhat * gm_ref[...]
        h1_ref[...] = h1
        hhat, _ = _rms(h1)
        u1_ref[...] = (hhat * gf_ref[...]).astype(BF16)

    row = lambda w: pl.BlockSpec((tr, w), lambda i: (i, 0))
    full = lambda a: pl.BlockSpec(a.shape, lambda i: (0,) * a.ndim)
    return pl.pallas_call(
        body, name="outproj_fwd", grid=(tp // tr,),
        in_specs=[row(ATTN_WIDTH), row(LRU_WIDTH), full(w_out), full(head)] + _seq_specs(tr)
        + [full(g_post_mix), full(g_pre_ffn)],
        out_specs=[row(D_MODEL), row(D_MODEL), row(D_MODEL)],
        out_shape=[jax.ShapeDtypeStruct((tp, D_MODEL), F32), jax.ShapeDtypeStruct((tp, D_MODEL), F32),
                   jax.ShapeDtypeStruct((tp, D_MODEL), BF16)],
        compiler_params=_params("parallel"),
    )(attn, rec, w_out, head, *([x] * qb), g_post_mix, g_pre_ffn)


def _ffn_fwd(u1, w1, w2, h1, tgt, g_post_ffn):
    tp = h1.shape[0]
    tr = _row_tile(tp)
    qb = tr // BLOCK

    def body(*refs):
        u_ref, w1_ref, w2_ref, h1_ref = refs[:4]
        t_pieces = refs[4:4 + qb]
        g_ref, r1_ref, dy_ref, df2_ref, loss_ref, dg_ref, acc = refs[4 + qb:]
        i, c = pl.program_id(0), pl.program_id(1)

        @pl.when((i == 0) & (c == 0))
        def _():
            loss_ref[...] = jnp.zeros_like(loss_ref)
            dg_ref[...] = jnp.zeros_like(dg_ref)

        r = jnp.maximum(_dot(u_ref[...], w1_ref[0]), 0.0)
        r1_ref[...] = r.astype(BF16)
        part = _dot((r * r).astype(BF16), w2_ref[0])

        @pl.when(c == 0)
        def _():
            acc[...] = part

        @pl.when(c > 0)
        def _():
            acc[...] += part

        @pl.when(c == N_CHIPS - 1)
        def _():
            g = g_ref[...]
            fhat, rs = _rms(acc[...])
            h2 = h1_ref[...] + fhat * g
            rows = i * tr + lax.broadcasted_iota(jnp.int32, h2.shape, 0)
            tgt_tile = jnp.concatenate([p[...] for p in t_pieces], axis=0)
            err = jnp.where(rows >= BLOCK, h2 - tgt_tile, 0.0)
            dy = err * (1.0 / D_MODEL)
            dy_ref[...] = dy
            loss_ref[...] += (0.5 / D_MODEL) * jnp.sum(err * err)
            df2, dg = _rms_bwd(fhat, rs, g, dy)
            df2_ref[...] = df2.astype(BF16)
            dg_ref[...] += dg

    row = pl.BlockSpec((tr, D_MODEL), lambda i, c: (i, 0))
    full = lambda a: pl.BlockSpec(a.shape, lambda i, c: (0,) * a.ndim)
    return pl.pallas_call(
        body, name="ffn_fwd", grid=(tp // tr, N_CHIPS),
        in_specs=[row, pl.BlockSpec((1, D_MODEL, FF_CHUNK), lambda i, c: (c, 0, 0)),
                  pl.BlockSpec((1, FF_CHUNK, D_MODEL), lambda i, c: (c, 0, 0)), row] + _seq_specs(tr)
        + [full(g_post_ffn)],
        out_specs=[pl.BlockSpec((tr, FF_CHUNK), lambda i, c: (i, c)), row, row,
                   pl.BlockSpec((1, 1), lambda i, c: (0, 0)), pl.BlockSpec((1, D_MODEL), lambda i, c: (0, 0))],
        out_shape=[jax.ShapeDtypeStruct((tp, D_FF), BF16), jax.ShapeDtypeStruct((tp, D_MODEL), F32),
                   jax.ShapeDtypeStruct((tp, D_MODEL), BF16), jax.ShapeDtypeStruct((1, 1), F32),
                   jax.ShapeDtypeStruct((1, D_MODEL), F32)],
        scratch_shapes=[pltpu.VMEM((tr, D_MODEL), F32)],
        compiler_params=_params("arbitrary", "arbitrary"),
    )(u1, w1, w2, h1, *([tgt] * qb), g_post_ffn)


def _ffn_bwd_data(df2, r1, w1, w2, dy, h1, mix, g_pre_ffn, g_post_mix):
    tp = h1.shape[0]
    tr = _row_tile(tp)

    def body(df2_ref, r1_ref, w1_ref, w2_ref, dy_ref, h1_ref, mix_ref, gf_ref, gm_ref,
             da_ref, dh1_ref, dmix_ref, dgf_ref, dgm_ref, acc):
        i, c = pl.program_id(0), pl.program_id(1)

        @pl.when((i == 0) & (c == 0))
        def _():
            dgf_ref[...] = jnp.zeros_like(dgf_ref)
            dgm_ref[...] = jnp.zeros_like(dgm_ref)

        df = _dot_nt(df2_ref[...], w2_ref[0])
        da = (df * (2.0 * r1_ref[...].astype(F32))).astype(BF16)
        da_ref[...] = da
        part = _dot_nt(da, w1_ref[0])

        @pl.when(c == 0)
        def _():
            acc[...] = part

        @pl.when(c > 0)
        def _():
            acc[...] += part

        @pl.when(c == N_CHIPS - 1)
        def _():
            hhat, rs = _rms(h1_ref[...])
            dx, dgf = _rms_bwd(hhat, rs, gf_ref[...], acc[...])
            dh1 = dy_ref[...] + dx
            dh1_ref[...] = dh1
            dgf_ref[...] += dgf
            mhat, rsm = _rms(mix_ref[...])
            dmix, dgm = _rms_bwd(mhat, rsm, gm_ref[...], dh1)
            dmix_ref[...] = dmix.astype(BF16)
            dgm_ref[...] += dgm

    row = pl.BlockSpec((tr, D_MODEL), lambda i, c: (i, 0))
    chunk = pl.BlockSpec((tr, FF_CHUNK), lambda i, c: (i, c))
    gain = pl.BlockSpec((1, D_MODEL), lambda i, c: (0, 0))
    return pl.pallas_call(
        body, name="ffn_bwd_data", grid=(tp // tr, N_CHIPS),
        in_specs=[row, chunk, pl.BlockSpec((1, D_MODEL, FF_CHUNK), lambda i, c: (c, 0, 0)),
                  pl.BlockSpec((1, FF_CHUNK, D_MODEL), lambda i, c: (c, 0, 0)), row, row, row, gain, gain],
        out_specs=[chunk, row, row, gain, gain],
        out_shape=[jax.ShapeDtypeStruct((tp, D_FF), BF16), jax.ShapeDtypeStruct((tp, D_MODEL), F32),
                   jax.ShapeDtypeStruct((tp, D_MODEL), BF16), jax.ShapeDtypeStruct((1, D_MODEL), F32),
                   jax.ShapeDtypeStruct((1, D_MODEL), F32)],
        scratch_shapes=[pltpu.VMEM((tr, D_MODEL), F32)],
        compiler_params=_params("arbitrary", "arbitrary"),
    )(df2, r1, w1, w2, dy, h1, mix, g_pre_ffn, g_post_mix)


def _ffn_bwd_weights(u1, da1, r1, df2):
    tp = u1.shape[0]
    tr = _row_tile(tp)

    def body(u_ref, da_ref, r1_ref, df2_ref, dw1_ref, dw2_ref):
        i = pl.program_id(1)
        r = r1_ref[...].astype(F32)
        p1 = _dot_tn(u_ref[...], da_ref[...])
        p2 = _dot_tn((r * r).astype(BF16), df2_ref[...])

        @pl.when(i == 0)
        def _():
            dw1_ref[0] = p1
            dw2_ref[0] = p2

        @pl.when(i > 0)
        def _():
            dw1_ref[0] += p1
            dw2_ref[0] += p2

    row = pl.BlockSpec((tr, D_MODEL), lambda c, i: (i, 0))
    chunk = pl.BlockSpec((tr, FF_CHUNK), lambda c, i: (i, c))
    return pl.pallas_call(
        body, name="ffn_bwd_weights", grid=(N_CHIPS, tp // tr),
        in_specs=[row, chunk, chunk, row],
        out_specs=[pl.BlockSpec((1, D_MODEL, FF_CHUNK), lambda c, i: (c, 0, 0)),
                   pl.BlockSpec((1, FF_CHUNK, D_MODEL), lambda c, i: (c, 0, 0))],
        out_shape=[jax.ShapeDtypeStruct((N_CHIPS, D_MODEL, FF_CHUNK), F32),
                   jax.ShapeDtypeStruct((N_CHIPS, FF_CHUNK, D_MODEL), F32)],
        compiler_params=_params("parallel", "arbitrary"),
    )(u1, da1, r1, df2)


def _outproj_bwd(dmix, w_out, attn, rec, token):
    tp = dmix.shape[0]
    tr = _row_tile(tp)

    def body(dm_ref, w_ref, a_ref, r_ref, _, da_ref, dr_ref, dw_ref):
        i = pl.program_id(0)
        dm = dm_ref[...]
        dcat = _dot_nt(dm, w_ref[...])
        da_ref[...] = dcat[:, :ATTN_WIDTH].astype(BF16)
        dr_ref[...] = dcat[:, ATTN_WIDTH:]
        pa = _dot_tn(a_ref[...], dm)
        pr = _dot_tn(r_ref[...], dm)

        @pl.when(i == 0)
        def _():
            dw_ref[:ATTN_WIDTH] = pa
            dw_ref[ATTN_WIDTH:] = pr

        @pl.when(i > 0)
        def _():
            dw_ref[:ATTN_WIDTH] += pa
            dw_ref[ATTN_WIDTH:] += pr

    row = lambda w: pl.BlockSpec((tr, w), lambda i: (i, 0))
    full = pl.BlockSpec((D_MODEL, D_MODEL), lambda i: (0, 0))
    return pl.pallas_call(
        body, name="outproj_bwd", grid=(tp // tr,),
        in_specs=[row(D_MODEL), full, row(ATTN_WIDTH), row(LRU_WIDTH), pl.BlockSpec(token.shape, lambda i: (0, 0))],
        out_specs=[row(ATTN_WIDTH), row(LRU_WIDTH), full],
        out_shape=[jax.ShapeDtypeStruct((tp, ATTN_WIDTH), BF16), jax.ShapeDtypeStruct((tp, LRU_WIDTH), F32),
                   jax.ShapeDtypeStruct((D_MODEL, D_MODEL), F32)],
        compiler_params=_params("arbitrary"),
    )(dmix, w_out, attn, rec, token)


N_VEC_ROWS = 8


def _lru_bwd(xr, yr, hr, drec, conv_w, conv_b, wa, ba, wx, bx, lam):
    tp = xr.shape[0]
    nb = tp // BLOCK

    def body(xr_ref, xh_ref, yr_ref, hr_ref, hp_ref, dr_ref, cw_ref, cb_ref, wa_ref, ba_ref, wx_ref, bx_ref, lam_ref,
             dxr_ref, dyr_ref, dwa_ref, dwx_ref, vec_ref, g_next, a_next, dxc_next, dsp):
        s = pl.program_id(0)
        n = nb - 1 - s

        @pl.when(s == 0)
        def _():
            g_next[...] = jnp.zeros_like(g_next)
            a_next[...] = jnp.zeros_like(a_next)
            dxc_next[...] = jnp.zeros_like(dxc_next)
            dsp[...] = jnp.zeros_like(dsp)
            dwa_ref[...] = jnp.zeros_like(dwa_ref)
            dwx_ref[...] = jnp.zeros_like(dwx_ref)
            vec_ref[...] = jnp.zeros_like(vec_ref)

        first = n == 0
        x = xr_ref[...]
        taps = _conv_taps(x, jnp.where(first, 0.0, xh_ref[...]))
        cw = cw_ref[...]
        xc = cb_ref[...] + sum(cw[k:k + 1] * taps[k] for k in range(4))
        lam_v = lam_ref[...]
        sp = _softplus(-lam_v)
        wa_m, wx_m = wa_ref[...], wx_ref[...]
        xb, r, ig, a, mult = _lru_gates(xc, wa_m, ba_ref[...], wx_m, bx_ref[...], sp)

        yr_v = yr_ref[...]
        gl, th = _gelu(yr_v)
        h = hr_ref[...]
        drec = dr_ref[...]
        dyr_ref[...] = (drec * h * _gelu_grad(yr_v, th)).astype(BF16)
        dh_direct = drec * gl

        rows = lax.broadcasted_iota(jnp.int32, a.shape, 0)
        a_up = jnp.where(rows == BLOCK - 1, a_next[0:1], pltpu.roll(a, BLOCK - 1, 0))
        cprod, gloc = _scan_rev(a_up, dh_direct)
        g = gloc + cprod * g_next[0:1]
        g_next[0:1] = g[0:1]
        a_next[0:1] = a[0:1]

        real = (n * BLOCK + rows) >= PAD_ROWS
        h_prev = jnp.where(rows == 0, jnp.where(first, 0.0, hp_ref[7:8]), pltpu.roll(h, 1, 0))
        du = jnp.where(real, g, 0.0)
        da = g * h_prev
        dmult = du * (ig * xc)
        dig = du * (mult * xc)
        dxc = du * (mult * ig)
        dlog_a = jnp.where(real, da * a - dmult * (a * a / mult), 0.0)
        dgr = (dlog_a * (-LRU_C * sp)) * (r * (1.0 - r))
        dgi = dig * (ig * (1.0 - ig))
        dsp[0:1] += jnp.sum(dlog_a * (-LRU_C * r), axis=0, keepdims=True)
        dgr_b, dgi_b = dgr.astype(BF16), dgi.astype(BF16)
        dxc = dxc + _dot_nt(dgr_b, wa_m) + _dot_nt(dgi_b, wx_m)
        dwa_ref[...] += _dot_tn(xb, dgr_b)
        dwx_ref[...] += _dot_tn(xb, dgi_b)

        ext = jnp.concatenate([dxc, dxc_next[...]], axis=0)
        up = [ext[:BLOCK] if j == 0 else pltpu.roll(ext, BLOCK + 8 - j, 0)[:BLOCK] for j in range(4)]
        dxr_ref[...] = sum(cw[k:k + 1] * up[3 - k] for k in range(4)).astype(BF16)
        dxc_next[...] = dxc[:8]

        col = lambda v: jnp.sum(v, axis=0, keepdims=True)
        for k in range(4):
            vec_ref[k:k + 1] += col(dxc * taps[k])
        vec_ref[4:5] += col(dxc)
        vec_ref[5:6] += col(dgr)
        vec_ref[6:7] += col(dgi)

        @pl.when(s == nb - 1)
        def _():
            vec_ref[7:8] = dsp[0:1] * (-_sigmoid(-lam_v))

    blk = pl.BlockSpec((BLOCK, LRU_WIDTH), lambda s: (nb - 1 - s, 0))
    rows_before = pl.BlockSpec((8, LRU_WIDTH), lambda s: (jnp.maximum((nb - 1 - s) * (BLOCK // 8) - 1, 0), 0))
    full = lambda a: pl.BlockSpec(a.shape, lambda s: (0,) * a.ndim)
    small = [conv_w, conv_b, wa, ba, wx, bx, lam]
    sq = pl.BlockSpec((LRU_WIDTH, LRU_WIDTH), lambda s: (0, 0))
    return pl.pallas_call(
        body, name="lru_bwd", grid=(nb,),
        in_specs=[blk, rows_before, blk, blk, rows_before, blk] + [full(a) for a in small],
        out_specs=[blk, blk, sq, sq, pl.BlockSpec((N_VEC_ROWS, LRU_WIDTH), lambda s: (0, 0))],
        out_shape=[jax.ShapeDtypeStruct((tp, LRU_WIDTH), BF16), jax.ShapeDtypeStruct((tp, LRU_WIDTH), BF16),
                   jax.ShapeDtypeStruct((LRU_WIDTH, LRU_WIDTH), F32), jax.ShapeDtypeStruct((LRU_WIDTH, LRU_WIDTH), F32),
                   jax.ShapeDtypeStruct((N_VEC_ROWS, LRU_WIDTH), F32)],
        scratch_shapes=[pltpu.VMEM((8, LRU_WIDTH), F32)] * 4,
        compiler_params=_params("arbitrary"),
    )(xr, xr, yr, hr, hr, drec, *small)


def _attn_bwd(qkv, dattn, sinks):
    tp = qkv.shape[0]
    tr = _row_tile(tp)
    qb, nt = tr // BLOCK, tp // tr
    n_groups = ATTN_HEADS // GQA_GROUP
    sink_rows, bias = _attn_consts(sinks)

    def body(s_ref, b_ref, q_ref, kp_ref, kc_ref, vp_ref, vc_ref, do_ref, dq_ref, dkv_ref, ex_ref, ds_ref, dsink):
        t = pl.program_id(0)

        @pl.when(t == 0)
        def _():
            dsink[...] = jnp.zeros_like(dsink)

        k_all = jnp.concatenate([kp_ref[...], kc_ref[...]], axis=0)
        v_all = jnp.concatenate([vp_ref[...], vc_ref[...]], axis=0)
        tail = None
        for i in range(qb):
            rows = slice(i * BLOCK, (i + 1) * BLOCK)
            q, do = q_ref[rows], do_ref[rows]
            k2, v2 = k_all[i * BLOCK:(i + 2) * BLOCK], v_all[i * BLOCK:(i + 2) * BLOCK]
            bias_n = _block_bias(b_ref, t, qb, i)
            dqs, dks, dvs = [], [], []
            for g in range(n_groups):
                cols = slice(g * HEAD_DIM, (g + 1) * HEAD_DIM)
                k_g, v_g = k2[:, cols], v2[:, cols]
                qg = _stack_heads(q, g) * jnp.asarray(_QSCALE, BF16)
                dog = _stack_heads(do, g)
                p, ps = _attn_probs_t(k_g, qg, bias_n, s_ref[g:g + 1])
                dpt = _dot_nt(v_g, dog)
                delta = jnp.sum(p * dpt, axis=0, keepdims=True)
                dst = (p * (dpt - delta)).astype(BF16)
                dqs.append(_dot_tn(dst, k_g) * _QSCALE)
                dks.append(_dot(dst, qg))
                dvs.append(_dot(p.astype(BF16), dog))
                dsink[g:g + 1] -= ps * delta
            dq_ref[rows] = _unstack_heads(dqs).astype(BF16)
            dkv = jnp.concatenate(dks + dvs, axis=1)
            if i == 0:
                ex_ref[0] = dkv[:BLOCK]
            else:
                dkv_ref[(i - 1) * BLOCK:i * BLOCK] = (tail + dkv[:BLOCK]).astype(BF16)
            tail = dkv[BLOCK:]
        dkv_ref[(qb - 1) * BLOCK:] = tail.astype(BF16)

        @pl.when(t == nt - 1)
        def _():
            lane = lax.broadcasted_iota(jnp.int32, (1, ATTN_HEADS), 1)
            acc = jnp.zeros((1, ATTN_HEADS), F32)
            for h in range(ATTN_HEADS):
                g, hh = divmod(h, GQA_GROUP)
                acc = acc + jnp.where(lane == h, jnp.sum(dsink[g:g + 1, hh * BLOCK:(hh + 1) * BLOCK]), 0.0)
            ds_ref[...] = acc

    cur = lambda w: pl.BlockSpec((tr, w), lambda t: (t, 0))
    return pl.pallas_call(
        body, name="attn_bwd", grid=(nt,),
        in_specs=[_SINK_SPEC, _BIAS_SPEC, cur(ATTN_WIDTH)] + _kv_specs(tr) + [cur(ATTN_WIDTH)],
        out_specs=[cur(ATTN_WIDTH), cur(2 * KV_WIDTH), pl.BlockSpec((1, BLOCK, 2 * KV_WIDTH), lambda t: (t, 0, 0)),
                   pl.BlockSpec((1, ATTN_HEADS), lambda t: (0, 0))],
        out_shape=[jax.ShapeDtypeStruct((tp, ATTN_WIDTH), BF16), jax.ShapeDtypeStruct((tp, 2 * KV_WIDTH), BF16),
                   jax.ShapeDtypeStruct((nt, BLOCK, 2 * KV_WIDTH), F32), jax.ShapeDtypeStruct((1, ATTN_HEADS), F32)],
        scratch_shapes=[pltpu.VMEM((n_groups, GROUP_ROWS), F32)],
        compiler_params=_params("arbitrary"),
    )(sink_rows, bias, qkv, qkv, qkv, qkv, qkv, dattn)


def _inproj_bwd(dq, dkv, dkv_extra, dxr, dyr, w_in, u0, head, x, dh1, g):
    tp = dq.shape[0]
    tr = _row_tile(tp)
    nt, qb = tp // tr, tr // BLOCK

    def body(*refs):
        dq_ref, dkv_ref, ex_ref, dxr_ref, dyr_ref, w_ref, u_ref, head_ref = refs[:8]
        pieces = refs[8:8 + qb]
        dh1_ref, g_ref, gx_ref, dhead_ref, dw_ref, dg_ref, buf, sems = refs[8 + qb:]
        i = pl.program_id(0)
        slot = i % 2

        def out_copy(step, at):
            return pltpu.make_async_copy(buf.at[at], gx_ref.at[pl.ds(step * tr - BLOCK, tr)], sems.at[at])

        extra = jnp.where(i < nt - 1, ex_ref[0], 0.0)
        last = (dkv_ref[tr - BLOCK:].astype(F32) + extra).astype(BF16)
        dkv = last if tr == BLOCK else jnp.concatenate([dkv_ref[:tr - BLOCK], last], axis=0)
        dz = jnp.concatenate([dq_ref[...], dkv, dxr_ref[...], dyr_ref[...]], axis=1)
        du = _dot_nt(dz, w_ref[...])
        hhat, rs = _rms(_seq_tile(head_ref[...], pieces, i))
        dx, dg = _rms_bwd(hhat, rs, g_ref[...], du)
        dh0 = dh1_ref[...] + dx

        @pl.when(i >= 3)
        def _():
            out_copy(i - 2, slot).wait()

        buf[slot] = dh0

        @pl.when(i == 0)
        def _():
            dhead_ref[...] = dh0[:BLOCK]
            if tr > BLOCK:
                first = pltpu.make_async_copy(buf.at[0, pl.ds(BLOCK, tr - BLOCK)], gx_ref.at[pl.ds(0, tr - BLOCK)],
                                              sems.at[0])
                first.start()
                first.wait()

        @pl.when(i >= 1)
        def _():
            out_copy(i, slot).start()

        @pl.when(i == nt - 1)
        def _():
            if nt >= 3:
                out_copy(nt - 2, (nt - 2) % 2).wait()
            if nt >= 2:
                out_copy(nt - 1, (nt - 1) % 2).wait()

        pw = _dot_tn(u_ref[...], dz)

        @pl.when(i == 0)
        def _():
            dw_ref[...] = pw
            dg_ref[...] = dg

        @pl.when(i > 0)
        def _():
            dw_ref[...] += pw
            dg_ref[...] += dg

    row = lambda w: pl.BlockSpec((tr, w), lambda i: (i, 0))
    full = lambda shape: pl.BlockSpec(shape, lambda i: (0,) * len(shape))
    return pl.pallas_call(
        body, name="inproj_bwd", grid=(tp // tr,),
        in_specs=[row(ATTN_WIDTH), row(2 * KV_WIDTH),
                  pl.BlockSpec((1, BLOCK, 2 * KV_WIDTH), lambda i: (jnp.minimum(i + 1, nt - 1), 0, 0)),
                  row(LRU_WIDTH), row(LRU_WIDTH), full(w_in.shape), row(D_MODEL), full(head.shape)]
        + _seq_specs(tr) + [row(D_MODEL), full(g.shape)],
        out_specs=[pl.BlockSpec(memory_space=pl.ANY), full((BLOCK, D_MODEL)), full((D_MODEL, IN_WIDTH)),
                   full((1, D_MODEL))],
        out_shape=[jax.ShapeDtypeStruct(x.shape, F32), jax.ShapeDtypeStruct((BLOCK, D_MODEL), F32),
                   jax.ShapeDtypeStruct((D_MODEL, IN_WIDTH), F32), jax.ShapeDtypeStruct((1, D_MODEL), F32)],
        scratch_shapes=[pltpu.VMEM((2, tr, D_MODEL), F32), pltpu.SemaphoreType.DMA((2,))],
        compiler_params=_params("arbitrary"),
    )(dq, dkv, dkv_extra, dxr, dyr, w_in, u0, head, *([x] * qb), dh1, g)


def _dense_block_diag(w):
    eye = jnp.eye(LRU_BLOCKS, dtype=w.dtype)
    return (w[:, :, None, :] * eye[:, None, :, None]).reshape(LRU_WIDTH, LRU_WIDTH)


def _diag_blocks(dense):
    d4 = dense.reshape(LRU_BLOCKS, LRU_BLOCK, LRU_BLOCKS, LRU_BLOCK)
    return jnp.stack([d4[n, :, n, :] for n in range(LRU_BLOCKS)])


def _local_step(head, x, tgt, g_pre_mix, w_in, conv_w, conv_b, w_a, b_a, w_x, b_x, lam, sinks, g_post_mix,
                g_pre_ffn, g_post_ffn, late_weights, on_ffn_grads, token):
    wa = _dense_block_diag(w_a).astype(BF16)
    wx = _dense_block_diag(w_x).astype(BF16)

    u0, qkv, xr, yr = _inproj_fwd(head, x, g_pre_mix, w_in, token)
    attn = _attn_fwd(qkv, sinks)
    hr, rec = _lru_fwd(xr, yr, conv_w, conv_b, wa, b_a, wx, b_x, lam)
    w_out, w1, w2 = late_weights([attn, rec])
    mix, h1, u1 = _outproj_fwd(attn, rec, w_out, head, x, g_post_mix, g_pre_ffn)
    r1, dy, df2, loss, dg_post_ffn = _ffn_fwd(u1, w1, w2, h1, tgt, g_post_ffn)

    da1, dh1, dmix, dg_pre_ffn, dg_post_mix = _ffn_bwd_data(df2, r1, w1, w2, dy, h1, mix, g_pre_ffn, g_post_mix)
    dw1, dw2 = _ffn_bwd_weights(u1, da1, r1, df2)
    token2 = on_ffn_grads(dw1, dw2)
    dattn, drec, dw_out = _outproj_bwd(dmix, w_out, attn, rec, token2)
    dxr, dyr, dwa, dwx, vec = _lru_bwd(xr, yr, hr, drec, conv_w, conv_b, wa, b_a, wx, b_x, lam)
    dq, dkv, dkv_extra, dsinks = _attn_bwd(qkv, dattn, sinks)
    dx, dhead, dw_in, dg_pre_mix = _inproj_bwd(dq, dkv, dkv_extra, dxr, dyr, w_in, u0, head, x, dh1, g_pre_mix)

    grads = dict(
        g_pre_mix=dg_pre_mix, w_in=dw_in, conv_w=vec[0:4], conv_b=vec[4:5], w_a=_diag_blocks(dwa), b_a=vec[5:6],
        w_x=_diag_blocks(dwx), b_x=vec[6:7], lru_lambda=vec[7:8], attn_sinks=dsinks, w_out=dw_out,
        g_post_mix=dg_post_mix, g_pre_ffn=dg_pre_ffn, w_ff1=dw1, w_ff2=dw2, g_post_ffn=dg_post_ffn)
    return loss, dx, dhead, grads


HBM = pl.BlockSpec(memory_space=pltpu.HBM)


def _mesh_pos():
    return lax.axis_index("x"), lax.axis_index("y"), lax.axis_index("c")


def _other_chips(x, y):
    return [(1 - x, y), (x, 1 - y), (1 - x, 1 - y)]


def _remote(src, dst, send_sem, recv_sem, to):
    return pltpu.make_async_remote_copy(src_ref=src, dst_ref=dst, send_sem=send_sem, recv_sem=recv_sem,
                                        device_id=to, device_id_type=MESH)


def _gather_weights(shards, lands, tiny, tiny_land):
    nbig = len(shards)

    def body(*refs):
        srcs, tiny_src = refs[:nbig], refs[nbig]
        outs, tiny_out = refs[2 * nbig + 2:3 * nbig + 2], refs[3 * nbig + 2]
        ici_send, ici_recv, d2d_send, d2d_recv, tiny_send, tiny_recv = refs[3 * nbig + 3:]
        x, y, c = _mesh_pos()
        me = 2 * x + y
        chips = _other_chips(x, y)
        sibling = (x, y, 1 - c)
        sends = []
        for w, (src, out) in enumerate(zip(srcs, outs)):
            hr = src.shape[0] // 2
            for j, chip in enumerate(chips):
                k = 3 * w + j
                cp = _remote(src.at[pl.ds(c * hr, hr)], out.at[me, pl.ds(c * hr, hr)],
                             ici_send.at[k], ici_recv.at[k], (*chip, c))
                cp.start()
                sends.append(cp)
        for j, chip in enumerate(chips):
            cp = _remote(tiny_src, tiny_out.at[me], tiny_send.at[j], tiny_recv.at[j], (*chip, c))
            cp.start()
            sends.append(cp)
        for w, (src, out) in enumerate(zip(srcs, outs)):
            hr = src.shape[0] // 2
            for j, (px, py) in enumerate(chips):
                k = 3 * w + j
                landed = out.at[2 * px + py, pl.ds(c * hr, hr)]
                _remote(landed, landed, ici_send.at[k], ici_recv.at[k], sibling).wait_recv()
                cp = _remote(landed, landed, d2d_send.at[k], d2d_recv.at[k], sibling)
                cp.start()
                sends.append(cp)
        for w, (src, out) in enumerate(zip(srcs, outs)):
            hr = src.shape[0] // 2
            for j, (px, py) in enumerate(chips):
                k = 3 * w + j
                other = out.at[2 * px + py, pl.ds((1 - c) * hr, hr)]
                _remote(other, other, d2d_send.at[k], d2d_recv.at[k], sibling).wait_recv()
        for j, (px, py) in enumerate(chips):
            blk = tiny_out.at[2 * px + py]
            _remote(blk, blk, tiny_send.at[j], tiny_recv.at[j], sibling).wait_recv()
        for cp in sends:
            cp.wait_send()

    out_shape = [jax.ShapeDtypeStruct(l.shape, l.dtype) for l in list(lands) + [tiny_land]]
    n = 3 * nbig
    return pl.pallas_call(
        body, name="gather_weights", out_shape=out_shape,
        in_specs=[HBM] * (2 * nbig + 2), out_specs=[HBM] * (nbig + 1),
        input_output_aliases={nbig + 1 + i: i for i in range(nbig + 1)},
        scratch_shapes=[pltpu.SemaphoreType.DMA((n,)),
                        pltpu.SemaphoreType.DMA((n,)), pltpu.SemaphoreType.DMA((n,)), pltpu.SemaphoreType.DMA((n,)),
                        pltpu.SemaphoreType.DMA((3,)), pltpu.SemaphoreType.DMA((3,))],
    )(*shards, tiny, *lands, tiny_land)


def _prep_shard(w, me):
    rows, cols = w.shape
    tr = 256 if rows % 256 == 0 else rows

    def body(me_ref, w_ref, s_ref, l_ref):
        b = w_ref[...].astype(BF16)
        s_ref[...] = b
        l_ref[0] = b

    return pl.pallas_call(
        body, name="prep_shard",
        grid_spec=pltpu.PrefetchScalarGridSpec(
            num_scalar_prefetch=1, grid=(rows // tr,),
            in_specs=[pl.BlockSpec((tr, cols), lambda i, me_ref: (i, 0))],
            out_specs=[pl.BlockSpec((tr, cols), lambda i, me_ref: (i, 0)),
                       pl.BlockSpec((1, tr, cols), lambda i, me_ref: (me_ref[0], i, 0))]),
        out_shape=[jax.ShapeDtypeStruct((rows, cols), BF16), jax.ShapeDtypeStruct((N_CHIPS, rows, cols), BF16)],
        compiler_params=_params("parallel"),
    )(me, w)


def _prep_tiny(tiny, me):
    def body(me_ref, t_ref, l_ref):
        l_ref[0] = t_ref[...]

    return pl.pallas_call(
        body, name="prep_tiny",
        grid_spec=pltpu.PrefetchScalarGridSpec(
            num_scalar_prefetch=1, grid=(1,),
            in_specs=[pl.BlockSpec(tiny.shape, lambda i, me_ref: (0, 0))],
            out_specs=pl.BlockSpec((1,) + tiny.shape, lambda i, me_ref: (me_ref[0], 0, 0))),
        out_shape=jax.ShapeDtypeStruct((N_CHIPS,) + tiny.shape, tiny.dtype),
    )(me, tiny)


N_DEV = 8


def _gather_small(block):
    m_per, n = block.shape

    def body(x_ref, out_ref, send_sems, recv_sems, local_sem):
        x, y, c = _mesh_pos()
        me, sibling = (x, y, c), (x, y, 1 - c)
        chips = _other_chips(x, y)

        def rows(px, py, pc):
            return out_ref.at[pl.ds((4 * px + 2 * py + pc) * m_per, m_per), :]

        def copy(k, block_of, to, src=None):
            return _remote(rows(*block_of) if src is None else src, rows(*block_of),
                           send_sems.at[k], recv_sems.at[k], to)

        mine = pltpu.make_async_copy(x_ref, rows(*me), local_sem)
        mine.start()
        first = [copy(0, me, sibling, src=x_ref)]
        first += [copy(1 + j, me, (*chip, c), src=x_ref) for j, chip in enumerate(chips)]
        for cp in first:
            cp.start()
        passed = [copy(4 + j, (*chip, c), sibling) for j, chip in enumerate(chips)]
        for j, chip in enumerate(chips):
            copy(1 + j, (*chip, c), me).wait_recv()
            passed[j].start()
        copy(0, sibling, me).wait_recv()
        for j, chip in enumerate(chips):
            copy(4 + j, (*chip, 1 - c), me).wait_recv()
        for cp in first + passed:
            cp.wait_send()
        mine.wait()

    return pl.pallas_call(
        body, name="gather_small", out_shape=jax.ShapeDtypeStruct((N_DEV * m_per, n), block.dtype),
        in_specs=[pl.BlockSpec(memory_space=pltpu.VMEM)], out_specs=pl.BlockSpec(memory_space=pltpu.VMEM),
        scratch_shapes=[pltpu.SemaphoreType.DMA((7,)), pltpu.SemaphoreType.DMA((7,)), pltpu.SemaphoreType.DMA],
        compiler_params=pltpu.CompilerParams(vmem_limit_bytes=VMEM_LIMIT_V7X),
    )(block)


def _sibling_exchange(parts):
    def body(*refs):
        n = len(parts)
        srcs, outs, send_sems, recv_sems = refs[:n], refs[n:2 * n], refs[2 * n], refs[2 * n + 1]
        x, y, c = _mesh_pos()
        sibling = (x, y, 1 - c)
        cps = []
        for w, (src, out) in enumerate(zip(srcs, outs)):
            hr = src.shape[1] // 2
            cp = _remote(src.at[:, pl.ds((1 - c) * hr, hr)], out, send_sems.at[w], recv_sems.at[w], sibling)
            cp.start()
            cps.append(cp)
        for cp in cps:
            cp.wait()

    n = len(parts)
    return pl.pallas_call(
        body, name="sibling_exchange",
        out_shape=[jax.ShapeDtypeStruct((p.shape[0], p.shape[1] // 2, p.shape[2]), p.dtype) for p in parts],
        in_specs=[HBM] * n, out_specs=[HBM] * n,
        scratch_shapes=[pltpu.SemaphoreType.DMA((n,)), pltpu.SemaphoreType.DMA((n,))],
    )(*parts)


def _chip_presum(part, from_sibling, pos):
    _, hr, cols = from_sibling.shape
    tr = 256 if hr % 256 == 0 else hr
    steps = hr // tr

    def body(pos_ref, a_ref, b_ref, o_ref, land_ref):
        s = (a_ref[...] + b_ref[...]).astype(BF16)
        o_ref[...] = s

        @pl.when(pl.program_id(1) == pos_ref[1])
        def _():
            land_ref[...] = s

    return pl.pallas_call(
        body, name="chip_presum",
        grid_spec=pltpu.PrefetchScalarGridSpec(
            num_scalar_prefetch=1, grid=(steps, N_CHIPS),
            in_specs=[pl.BlockSpec((1, tr, cols), lambda i, j, p: (j, p[0] * steps + i, 0)),
                      pl.BlockSpec((1, tr, cols), lambda i, j, p: (j, i, 0))],
            out_specs=[pl.BlockSpec((1, tr, cols), lambda i, j, p: (j, i, 0)),
                       pl.BlockSpec((1, tr, cols), lambda i, j, p: (p[1], p[0] * steps + i, 0))]),
        out_shape=[jax.ShapeDtypeStruct(from_sibling.shape, BF16),
                   jax.ShapeDtypeStruct((N_CHIPS, 2 * hr, cols), BF16)],
        compiler_params=_params("arbitrary", "arbitrary"),
    )(pos, part, from_sibling)


def _scatter_partials(cparts, lands, done_cparts=(), done_lands=()):
    n_new = len(cparts)
    nw = n_new + len(done_cparts)

    def body(*refs):
        srcs = refs[:nw]
        outs = refs[2 * nw:3 * nw]
        own_send, own_recv, ici_send, ici_recv, d2d_send, d2d_recv = refs[3 * nw:]
        x, y, c = _mesh_pos()
        me = 2 * x + y
        chips = _other_chips(x, y)
        sibling = (x, y, 1 - c)
        sends = []
        for w in list(range(n_new, nw)) + list(range(n_new)):
            src, out = srcs[w], outs[w]
            hr = src.shape[1]
            mine = out.at[me, pl.ds(c * hr, hr)]
            cp = _remote(src.at[me], mine, own_send.at[w], own_recv.at[w], sibling)
            cp.start()
            sends.append(cp)
            for j, (px, py) in enumerate(chips):
                if w >= n_new:
                    break
                k = 3 * w + j
                cp = _remote(src.at[2 * px + py], mine, ici_send.at[k], ici_recv.at[k], (px, py, c))
                cp.start()
                sends.append(cp)
        for w in list(range(n_new, nw)) + list(range(n_new)):
            src, out = srcs[w], outs[w]
            hr = src.shape[1]
            for j, (px, py) in enumerate(chips):
                k = 3 * w + j
                landed = out.at[2 * px + py, pl.ds(c * hr, hr)]
                if w < n_new:
                    _remote(landed, landed, ici_send.at[k], ici_recv.at[k], sibling).wait_recv()
                cp = _remote(landed, landed, d2d_send.at[k], d2d_recv.at[k], sibling)
                cp.start()
                sends.append(cp)
        for w, (src, out) in enumerate(zip(srcs, outs)):
            hr = src.shape[1]
            other = out.at[me, pl.ds((1 - c) * hr, hr)]
            _remote(other, other, own_send.at[w], own_recv.at[w], sibling).wait_recv()
            for j, (px, py) in enumerate(chips):
                k = 3 * w + j
                other = out.at[2 * px + py, pl.ds((1 - c) * hr, hr)]
                _remote(other, other, d2d_send.at[k], d2d_recv.at[k], sibling).wait_recv()
        for cp in sends:
            cp.wait_send()

    n = 3 * nw
    dma = pltpu.SemaphoreType.DMA
    every = list(cparts) + list(done_cparts)
    every_lands = list(lands) + list(done_lands)
    return pl.pallas_call(
        body, name="scatter_partials",
        out_shape=[jax.ShapeDtypeStruct(l.shape, l.dtype) for l in every_lands],
        in_specs=[HBM] * (2 * nw), out_specs=[HBM] * nw,
        input_output_aliases={nw + i: i for i in range(nw)},
        scratch_shapes=[dma((nw,)), dma((nw,)), dma((n,)), dma((n,)), dma((n,)), dma((n,))],
    )(*every, *every_lands)


SEM = pl.BlockSpec(memory_space=pltpu.SEMAPHORE)
SPLIT_COPY = pltpu.CompilerParams(has_side_effects=pltpu.SideEffectType.DATAFLOW_SIDE_EFFECTING)


def _hbm(a):
    return pltpu.with_memory_space_constraint(a, pltpu.HBM)


def _gather_copies(srcs, lands, send_sems, recv_sems):
    x, y, c = _mesh_pos()
    me = 2 * x + y
    sends, recvs = [], []
    for w, (src, land) in enumerate(zip(srcs, lands)):
        hr = src.shape[0] // 2
        for j, (px, py) in enumerate(_other_chips(x, y)):
            k = 3 * w + j
            sends.append(_remote(src.at[pl.ds(c * hr, hr)], land.at[me, pl.ds(c * hr, hr)],
                                 send_sems.at[k], recv_sems.at[k], (px, py, c)))
            got = land.at[2 * px + py, pl.ds(c * hr, hr)]
            recvs.append(_remote(got, got, send_sems.at[k], recv_sems.at[k], (px, py, c)))
    return sends, recvs


def _scatter_copies(srcs, lands, send_sems, recv_sems):
    x, y, c = _mesh_pos()
    me = 2 * x + y
    sends, recvs = [], []
    for w, (src, land) in enumerate(zip(srcs, lands)):
        hr = src.shape[1]
        for j, (px, py) in enumerate(_other_chips(x, y)):
            k = 3 * w + j
            sends.append(_remote(src.at[2 * px + py], land.at[me, pl.ds(c * hr, hr)],
                                 send_sems.at[k], recv_sems.at[k], (px, py, c)))
            got = land.at[2 * px + py, pl.ds(c * hr, hr)]
            recvs.append(_remote(got, got, send_sems.at[k], recv_sems.at[k], (px, py, c)))
    return sends, recvs


def _split_start(name, copies_of, srcs, land_shapes):
    n = len(srcs)
    k = 3 * n

    def body(*refs):
        src_refs, land_refs = refs[:n], refs[n:2 * n]
        send_sems, recv_sems = refs[2 * n], refs[2 * n + 1]
        token = refs[-1]
        sends, _ = copies_of(src_refs, land_refs, send_sems, recv_sems)
        for cp in sends:
            cp.start()
        token[...] = jnp.zeros_like(token)

    lands = [_hbm(s) for s in land_shapes]
    dma = pltpu.SemaphoreType.DMA
    res = pl.pallas_call(
        body, name=name,
        out_shape=(dma((k,)), dma((k,)), *[pltpu.HBM(s.shape, s.dtype) for s in srcs],
                   *[pltpu.HBM(s.shape, s.dtype) for s in land_shapes], jax.ShapeDtypeStruct((8, 128), F32)),
        in_specs=[HBM] * (2 * n),
        out_specs=(SEM, SEM, *([HBM] * (2 * n)), pl.BlockSpec(memory_space=pltpu.VMEM)),
        input_output_aliases={i: 2 + i for i in range(2 * n)},
        compiler_params=SPLIT_COPY,
    )(*[_hbm(s) for s in srcs], *lands)
    return res[0], res[1], list(res[2:2 + n]), list(res[2 + n:2 + 2 * n]), res[-1]


def _split_wait(name, copies_of, send_sems, recv_sems, srcs, lands, after):
    n = len(srcs)

    def body(*refs):
        src_refs, land_refs = refs[:n], refs[n:2 * n]
        sends, recvs = copies_of(src_refs, land_refs, refs[2 * n], refs[2 * n + 1])
        for cp in sends:
            cp.wait_send()
        for cp in recvs:
            cp.wait_recv()

    res = pl.pallas_call(
        body, name=name,
        out_shape=tuple(pltpu.HBM(s.shape, s.dtype) for s in list(srcs) + list(lands)),
        in_specs=[HBM] * (2 * n) + [SEM, SEM] + [pl.BlockSpec(memory_space=pl.ANY)] * len(after),
        out_specs=tuple([HBM] * (2 * n)),
        input_output_aliases={i: i for i in range(2 * n)},
        compiler_params=SPLIT_COPY,
    )(*srcs, *lands, send_sems, recv_sems, *after)
    return list(res[:n]), list(res[n:])


def _gather_finish(lands):
    n = len(lands)

    def body(*refs):
        outs = refs[n:2 * n]
        d2d_send, d2d_recv = refs[2 * n:]
        x, y, c = _mesh_pos()
        chips = _other_chips(x, y)
        sibling = (x, y, 1 - c)
        sends = []
        for w, out in enumerate(outs):
            hr = out.shape[1] // 2
            for j, (px, py) in enumerate(chips):
                landed = out.at[2 * px + py, pl.ds(c * hr, hr)]
                cp = _remote(landed, landed, d2d_send.at[3 * w + j], d2d_recv.at[3 * w + j], sibling)
                cp.start()
                sends.append(cp)
        for w, out in enumerate(outs):
            hr = out.shape[1] // 2
            for j, (px, py) in enumerate(chips):
                other = out.at[2 * px + py, pl.ds((1 - c) * hr, hr)]
                _remote(other, other, d2d_send.at[3 * w + j], d2d_recv.at[3 * w + j], sibling).wait_recv()
        for cp in sends:
            cp.wait_send()

    dma = pltpu.SemaphoreType.DMA
    return pl.pallas_call(
        body, name="gather_finish",
        out_shape=[jax.ShapeDtypeStruct(l.shape, l.dtype) for l in lands],
        in_specs=[HBM] * n, out_specs=[HBM] * n,
        input_output_aliases={i: i for i in range(n)},
        scratch_shapes=[dma((3 * n,)), dma((3 * n,))],
    )(*lands)


def _adamw(w, g, m, v):
    m = ADAM_B1 * m + (1.0 - ADAM_B1) * g
    v = ADAM_B2 * v + (1.0 - ADAM_B2) * (g * g)
    m_hat = m / (1.0 - ADAM_B1 ** ADAM_STEP)
    v_hat = v / (1.0 - ADAM_B2 ** ADAM_STEP)
    delta = -ADAM_LR * (m_hat / (jnp.sqrt(v_hat) + ADAM_EPS) + ADAM_WD * w)
    return delta, m, v


def _adamw_big(partials, w, m, v):
    rows, cols = w.shape
    tr = 256

    def body(p_ref, w_ref, m_ref, v_ref, g_ref, d_ref, m2_ref, v2_ref):
        g = ((p_ref[0].astype(F32) + p_ref[1].astype(F32)) + p_ref[2].astype(F32)) + p_ref[3].astype(F32)
        g_ref[...] = g
        d_ref[...], m2_ref[...], v2_ref[...] = _adamw(w_ref[...], g, m_ref[...], v_ref[...])

    blk = pl.BlockSpec((tr, cols), lambda i: (i, 0))
    return pl.pallas_call(
        body, name="adamw_big", grid=(rows // tr,),
        in_specs=[pl.BlockSpec((N_CHIPS, tr, cols), lambda i: (0, i, 0)), blk, blk, blk],
        out_specs=[blk] * 4, out_shape=[jax.ShapeDtypeStruct((rows, cols), F32)] * 4,
        compiler_params=_params("parallel"),
    )(partials, w, m, v)


def _sum_devices(gathered, rows):
    cols = gathered.shape[1]

    def body(g_ref, o_ref):
        acc = g_ref[0:rows]
        for d in range(1, N_DEV):
            acc = acc + g_ref[d * rows:(d + 1) * rows]
        o_ref[...] = acc

    return pl.pallas_call(
        body, name="sum_devices", out_shape=jax.ShapeDtypeStruct((rows, cols), F32),
        in_specs=[pl.BlockSpec(memory_space=pltpu.VMEM)], out_specs=pl.BlockSpec(memory_space=pltpu.VMEM),
        compiler_params=pltpu.CompilerParams(vmem_limit_bytes=VMEM_LIMIT_V7X),
    )(gathered)


def _adamw_small(quads):
    n = len(quads)

    def body(*refs):
        ins, outs = refs[:4 * n], refs[4 * n:]
        for t in range(n):
            w, g, m, v = (r[...] for r in ins[4 * t:4 * t + 4])
            outs[3 * t][...], outs[3 * t + 1][...], outs[3 * t + 2][...] = _adamw(w, g, m, v)

    flat = [a for q in quads for a in q]
    vm = pl.BlockSpec(memory_space=pltpu.VMEM)
    res = pl.pallas_call(
        body, name="adamw_small",
        out_shape=[jax.ShapeDtypeStruct(q[0].shape, F32) for q in quads for _ in range(3)],
        in_specs=[vm] * (4 * n), out_specs=[vm] * (3 * n),
    )(*flat)
    return [tuple(res[3 * t:3 * t + 3]) for t in range(n)]


SMALL_PACK_ROWS = 96
_WEIGHTS = ['meta_tokens', 'g_pre_mix', 'w_in', 'conv_w', 'conv_b', 'w_a', 'b_a', 'w_x', 'b_x', 'lru_lambda',
            'attn_sinks', 'w_out', 'g_post_mix', 'g_pre_ffn', 'w_ff1', 'w_ff2', 'g_post_ffn']
_BIG = ['w_in', 'w_out', 'w_ff1', 'w_ff2']


def _pack_small(dmeta, g):
    z = lambda r, c: jnp.zeros((r, c), F32)
    rows = [
        dmeta,
        g['g_pre_mix'], g['g_post_mix'], g['g_pre_ffn'], g['g_post_ffn'],
        jnp.concatenate([g['conv_w'], z(4, 512)], axis=1),
        jnp.concatenate([g['conv_b'], g['b_a']], axis=1),
        jnp.concatenate([g['b_x'], g['lru_lambda']], axis=1),
        jnp.concatenate([g['attn_sinks'], z(1, D_MODEL - ATTN_HEADS)], axis=1),
        z(5, D_MODEL),
        g['w_a'].reshape(32, D_MODEL), g['w_x'].reshape(32, D_MODEL),
    ]
    return jnp.concatenate(rows, axis=0)


def _unpack_small(s, chip):
    return dict(
        meta_tokens=lax.dynamic_slice(s[0:16], (0, chip * 256), (16, 256)),
        g_pre_mix=s[16:17], g_post_mix=s[17:18], g_pre_ffn=s[18:19], g_post_ffn=s[19:20],
        conv_w=lax.dynamic_slice(s[20:24], (0, chip * 128), (4, 128)).reshape(1, 4, 128),
        conv_b=s[24:25, :512], b_a=s[24:25, 512:], b_x=s[25:26, :512], lru_lambda=s[25:26, 512:],
        attn_sinks=s[26:27, :ATTN_HEADS],
        w_a=s[32:64].reshape(1, LRU_BLOCKS, LRU_BLOCK, LRU_BLOCK),
        w_x=s[64:96].reshape(1, LRU_BLOCKS, LRU_BLOCK, LRU_BLOCK))


def _as2d(a):
    if a.ndim == 2:
        return a
    return a.reshape(-1, a.shape[-1])


def kernel(x, meta_tokens, g_pre_mix, w_in, conv_w, conv_b, w_a, b_a, w_x, b_x, lru_lambda, attn_sinks, w_out, g_post_mix, g_pre_ffn, w_ff1, w_ff2, g_post_ffn, loss_target, m_meta_tokens, m_g_pre_mix, m_w_in, m_conv_w, m_conv_b, m_w_a, m_b_a, m_w_x, m_b_x, m_lru_lambda, m_attn_sinks, m_w_out, m_g_post_mix, m_g_pre_ffn, m_w_ff1, m_w_ff2, m_g_post_ffn, v_meta_tokens, v_g_pre_mix, v_w_in, v_conv_w, v_conv_b, v_w_a, v_b_a, v_w_x, v_b_x, v_lru_lambda, v_attn_sinks, v_w_out, v_g_post_mix, v_g_pre_ffn, v_w_ff1, v_w_ff2, v_g_post_ffn):
    weights = dict(meta_tokens=meta_tokens, g_pre_mix=g_pre_mix, w_in=w_in, conv_w=conv_w, conv_b=conv_b, w_a=w_a,
                   b_a=b_a, w_x=w_x, b_x=b_x, lru_lambda=lru_lambda, attn_sinks=attn_sinks, w_out=w_out,
                   g_post_mix=g_post_mix, g_pre_ffn=g_pre_ffn, w_ff1=w_ff1, w_ff2=w_ff2, g_post_ffn=g_post_ffn)
    mom1 = dict(zip(_WEIGHTS, [m_meta_tokens, m_g_pre_mix, m_w_in, m_conv_w, m_conv_b, m_w_a, m_b_a, m_w_x, m_b_x,
                               m_lru_lambda, m_attn_sinks, m_w_out, m_g_post_mix, m_g_pre_ffn, m_w_ff1, m_w_ff2,
                               m_g_post_ffn]))
    mom2 = dict(zip(_WEIGHTS, [v_meta_tokens, v_g_pre_mix, v_w_in, v_conv_w, v_conv_b, v_w_a, v_b_a, v_w_x, v_b_x,
                               v_lru_lambda, v_attn_sinks, v_w_out, v_g_post_mix, v_g_pre_ffn, v_w_ff1, v_w_ff2,
                               v_g_post_ffn]))
    xi, yi, ci = _mesh_pos()
    chip = 2 * xi + yi

    tiny = jnp.concatenate([meta_tokens, jnp.pad(conv_w[0], ((0, 4), (0, 128)))], axis=0)
    chip_arr = jnp.reshape(chip, (1,)).astype(jnp.int32)
    shards, lands = zip(*[_prep_shard(w[0], chip_arr) for w in (w_in, w_out, w_ff1, w_ff2)])
    g_in, g_tiny = _gather_weights(shards[:1], lands[:1], tiny, _prep_tiny(tiny, chip_arr))
    w_in_full = jnp.concatenate([g_in[j] for j in range(N_CHIPS)], axis=1)
    meta_full = jnp.concatenate([g_tiny[j, :N_META] for j in range(N_CHIPS)], axis=1)
    conv_w_full = jnp.concatenate([g_tiny[j, N_META:N_META + 4, :128] for j in range(N_CHIPS)], axis=1)
    g_send, g_recv, late_thru, late_lands, token = _split_start(
        "gather_late_start", _gather_copies, shards[1:], lands[1:])

    def late_weights(after):
        _, landed = _split_wait("gather_late_wait", _gather_copies, g_send, g_recv, late_thru, late_lands, after)
        g_out, g_f1, g_f2 = _gather_finish(landed)
        return g_out.reshape(D_MODEL, D_MODEL), g_f1, g_f2

    pos = jnp.stack([ci, chip]).astype(jnp.int32)
    ffn = {}

    def chip_sums(parts):
        return zip(*[_chip_presum(p, r, pos) for p, r in zip(parts, _sibling_exchange(parts))])

    def on_ffn_grads(dw1, dw2):
        cparts_ffn, lands_ffn = chip_sums([dw1, dw2])
        ffn['send'], ffn['recv'], ffn['thru'], ffn['lands'], token2 = _split_start(
            "scatter_ffn_start", _scatter_copies, cparts_ffn, lands_ffn)
        return token2

    head = jnp.concatenate([jnp.zeros((PAD_ROWS, D_MODEL), F32), meta_full], axis=0)
    loss, dx, dhead, grads = _local_step(head, x[0], loss_target[0], g_pre_mix, w_in_full, conv_w_full, conv_b, w_a[0],
                                         b_a, w_x[0], b_x, lru_lambda, attn_sinks, g_post_mix, g_pre_ffn, g_post_ffn,
                                         late_weights, on_ffn_grads, token)
    loss = lax.psum(loss[0, 0], ("x", "y", "c"))
    grad_x = dx[None]

    gathered = _gather_small(_pack_small(dhead[PAD_ROWS:], grads))
    small = _unpack_small(_sum_devices(gathered, SMALL_PACK_ROWS), chip)

    dw_in = grads['w_in']
    cparts, own_lands = chip_sums([jnp.stack([dw_in[:, j * 448:(j + 1) * 448] for j in range(N_CHIPS)]),
                                   grads['w_out'].reshape(N_CHIPS, D_MODEL // N_CHIPS, D_MODEL)])
    ffn_cparts, ffn_lands = _split_wait("scatter_ffn_wait", _scatter_copies, ffn['send'], ffn['recv'], ffn['thru'],
                                        ffn['lands'], cparts)
    chip_partials = _scatter_partials(cparts, own_lands, ffn_cparts, ffn_lands)

    g_out_d, delta, new_m, new_v = {}, {}, {}, {}
    for name, part in zip(_BIG, chip_partials):
        shp = weights[name].shape
        res = _adamw_big(part, weights[name][0], mom1[name][0], mom2[name][0])
        g_out_d[name], delta[name], new_m[name], new_v[name] = (r.reshape(shp) for r in res)
    small_names = [n for n in _WEIGHTS if n not in _BIG]
    quads = [(_as2d(weights[n]), _as2d(small[n]), _as2d(mom1[n]), _as2d(mom2[n])) for n in small_names]
    for name, (d, m2, v2) in zip(small_names, _adamw_small(quads)):
        shp = weights[name].shape
        g_out_d[name] = small[name].reshape(shp)
        delta[name], new_m[name], new_v[name] = d.reshape(shp), m2.reshape(shp), v2.reshape(shp)

    return (loss, grad_x, *[g_out_d[n] for n in _WEIGHTS], *[delta[n] for n in _WEIGHTS],
            *[new_m[n] for n in _WEIGHTS], *[new_v[n] for n in _WEIGHTS])
```

```python
import numpy as np
import jax
import jax.numpy as jnp
from jax import lax
from jax.experimental import pallas as pl
from jax.experimental.pallas import tpu as pltpu

F32 = jnp.float32
BF16 = jnp.bfloat16

D_MODEL = 1024
N_META = 16
BLOCK = 128
PAD_ROWS = BLOCK - N_META
HEAD_DIM = 64
ATTN_HEADS = 8
GQA_GROUP = 4
ATTN_WIDTH = 512
KV_WIDTH = 128
QKV_WIDTH = ATTN_WIDTH + 2 * KV_WIDTH
LRU_WIDTH = 512
LRU_BLOCKS = 8
LRU_BLOCK = 64
LRU_C = 8.0
IN_WIDTH = 1792
D_FF = 4096
N_CHIPS = 4
FF_CHUNK = D_FF // N_CHIPS
EPS = 1e-6
NEG = -1e30

ADAM_LR = 0.001
ADAM_B1 = 0.9
ADAM_B2 = 0.999
ADAM_EPS = 1e-08
ADAM_WD = 0.01
ADAM_STEP = 10

VMEM_LIMIT_V7X = 56 * 1024 * 1024
MESH = pl.DeviceIdType.MESH

NT = (((1,), (1,)), ((), ()))
TN = (((0,), (0,)), ((), ()))


def _row_tile(tp):
    return 640 if tp % 640 == 0 else BLOCK


def _wgrad_row_tile(tp):
    return 1664 if tp % 1664 == 0 else _row_tile(tp)


def _params(*sem):
    return pltpu.CompilerParams(dimension_semantics=sem, vmem_limit_bytes=VMEM_LIMIT_V7X)


def _dot(a, b):
    return jnp.dot(a, b, preferred_element_type=F32)


def _dot_nt(a, b):
    return lax.dot_general(a, b, NT, preferred_element_type=F32)


def _dot_tn(a, b):
    return lax.dot_general(a, b, TN, preferred_element_type=F32)


def _rms(x):
    rs = lax.rsqrt(jnp.mean(x * x, axis=-1, keepdims=True) + EPS)
    return x * rs, rs


def _rms_bwd(xhat, rs, g, dy):
    dyg = dy * g
    dx = rs * (dyg - xhat * jnp.mean(dyg * xhat, axis=-1, keepdims=True))
    dg = jnp.sum(dy * xhat, axis=0, keepdims=True)
    return dx, dg


def _gelu(x):
    k = 0.7978845608028654
    t = jnp.tanh(k * (x + 0.044715 * x * x * x))
    return 0.5 * x * (1.0 + t), t


def _gelu_grad(x, t):
    k = 0.7978845608028654
    return 0.5 * (1.0 + t) + 0.5 * x * (1.0 - t * t) * k * (1.0 + 3 * 0.044715 * x * x)


def _sigmoid(x):
    return 0.5 * jnp.tanh(0.5 * x) + 0.5


def _neg_expm1(x):
    series = x * (1.0 + x * 0.5 * (1.0 + x * (1.0 / 3.0) * (1.0 + x * 0.25 * (1.0 + x * 0.2))))
    return -jnp.where(jnp.abs(x) < 0.05, series, jnp.exp(x) - 1.0)


def _softplus(x):
    return jnp.maximum(x, 0.0) + jnp.log1p(jnp.exp(-jnp.abs(x)))


def _seq_specs(tr):
    qb = tr // BLOCK
    return [pl.BlockSpec((BLOCK, D_MODEL), lambda i, *_, s=s: (jnp.maximum(i * qb + s - 1, 0), 0)) for s in range(qb)]


def _seq_tile(head, pieces, i):
    first = jnp.where(i == 0, head, pieces[0][...])
    return jnp.concatenate([first] + [p[...] for p in pieces[1:]], axis=0)


def _inproj_fwd(head, x, g, w_in, token):
    tp = BLOCK + x.shape[0]
    tr = _row_tile(tp)
    qb = tr // BLOCK

    def body(*refs):
        head_ref, pieces = refs[0], refs[1:1 + qb]
        g_ref, w_ref, _, u_ref, qkv_ref, xr_ref, yr_ref = refs[1 + qb:]
        xhat, _ = _rms(_seq_tile(head_ref[...], pieces, pl.program_id(0)))
        u = (xhat * g_ref[...]).astype(BF16)
        u_ref[...] = u
        z = _dot(u, w_ref[...])
        qkv_ref[...] = z[:, :QKV_WIDTH].astype(BF16)
        xr_ref[...] = z[:, QKV_WIDTH:QKV_WIDTH + LRU_WIDTH]
        yr_ref[...] = z[:, QKV_WIDTH + LRU_WIDTH:]

    row = lambda w: pl.BlockSpec((tr, w), lambda i: (i, 0))
    full = lambda a: pl.BlockSpec(a.shape, lambda i: (0,) * a.ndim)
    return pl.pallas_call(
        body, name="inproj_fwd", grid=(tp // tr,),
        in_specs=[full(head)] + _seq_specs(tr) + [full(g), full(w_in), full(token)],
        out_specs=[row(D_MODEL), row(QKV_WIDTH), row(LRU_WIDTH), row(LRU_WIDTH)],
        out_shape=[jax.ShapeDtypeStruct((tp, D_MODEL), BF16), jax.ShapeDtypeStruct((tp, QKV_WIDTH), BF16),
                   jax.ShapeDtypeStruct((tp, LRU_WIDTH), F32), jax.ShapeDtypeStruct((tp, LRU_WIDTH), F32)],
        compiler_params=_params("parallel"),
    )(head, *([x] * qb), g, w_in, token)


GROUP_ROWS = GQA_GROUP * BLOCK


def _attn_bias():
    j = np.arange(2 * BLOCK)[:, None]
    i = np.arange(BLOCK)[None, :]
    band = (j - i >= 1) & (j - i <= BLOCK)
    out = []
    for n in range(3):
        ok = band & ((n - 1) * BLOCK + j >= PAD_ROWS) if n < 2 else band
        out.append(np.tile(np.where(ok, 0.0, NEG).astype(np.float32), (1, GQA_GROUP)))
    return jnp.asarray(np.stack(out))


def _stack_heads(a, g):
    heads = range(GQA_GROUP * g, GQA_GROUP * (g + 1))
    return jnp.concatenate([a[:, h * HEAD_DIM:(h + 1) * HEAD_DIM] for h in heads], axis=0)


def _unstack_heads(groups):
    return jnp.concatenate([p[h * BLOCK:(h + 1) * BLOCK] for p in groups for h in range(GQA_GROUP)], axis=1)


def _attn_probs_t(k_g, qg, bias, sink_row):
    st = _dot_nt(k_g, qg) + bias
    m = jnp.maximum(jnp.max(st, axis=0, keepdims=True), sink_row)
    p = jnp.exp(st - m)
    es = jnp.exp(sink_row - m)
    inv = 1.0 / (jnp.sum(p, axis=0, keepdims=True) + es)
    return p * inv, es * inv


def _attn_consts(sinks):
    return jnp.repeat(sinks.reshape(ATTN_HEADS), BLOCK).reshape(ATTN_HEADS // GQA_GROUP, GROUP_ROWS), _attn_bias()


_SINK_SPEC = pl.BlockSpec((ATTN_HEADS // GQA_GROUP, GROUP_ROWS), lambda n: (0, 0))
_BIAS_SPEC = pl.BlockSpec((3, 2 * BLOCK, GROUP_ROWS), lambda n: (0, 0, 0))
_QSCALE = HEAD_DIM ** -0.5


def _kv_specs(tr):
    qb = tr // BLOCK
    prev = lambda col: pl.BlockSpec((BLOCK, KV_WIDTH), lambda t: (jnp.maximum(t * qb - 1, 0), col))
    cur = lambda col: pl.BlockSpec((tr, KV_WIDTH), lambda t: (t, col))
    return [prev(4), cur(4), prev(5), cur(5)]


def _block_bias(b_ref, t, qb, i):
    return b_ref[2] if i >= 2 else b_ref[jnp.minimum(t * qb + i, 2)]


def _attn_fwd(qkv, sinks):
    tp = qkv.shape[0]
    tr = _row_tile(tp)
    qb = tr // BLOCK
    sink_rows, bias = _attn_consts(sinks)

    def body(s_ref, b_ref, q_ref, kp_ref, kc_ref, vp_ref, vc_ref, o_ref):
        t = pl.program_id(0)
        k_all = jnp.concatenate([kp_ref[...], kc_ref[...]], axis=0)
        v_all = jnp.concatenate([vp_ref[...], vc_ref[...]], axis=0)
        for i in range(qb):
            rows = slice(i * BLOCK, (i + 1) * BLOCK)
            q = q_ref[rows]
            k2, v2 = k_all[i * BLOCK:(i + 2) * BLOCK], v_all[i * BLOCK:(i + 2) * BLOCK]
            bias_n = _block_bias(b_ref, t, qb, i)
            outs = []
            for g in range(ATTN_HEADS // GQA_GROUP):
                cols = slice(g * HEAD_DIM, (g + 1) * HEAD_DIM)
                qg = _stack_heads(q, g) * jnp.asarray(_QSCALE, BF16)
                p, _ = _attn_probs_t(k2[:, cols], qg, bias_n, s_ref[g:g + 1])
                outs.append(_dot_tn(p.astype(BF16), v2[:, cols]))
            o_ref[rows] = _unstack_heads(outs).astype(BF16)

    return pl.pallas_call(
        body, name="attn_fwd", grid=(tp // tr,),
        in_specs=[_SINK_SPEC, _BIAS_SPEC, pl.BlockSpec((tr, ATTN_WIDTH), lambda t: (t, 0))] + _kv_specs(tr),
        out_specs=pl.BlockSpec((tr, ATTN_WIDTH), lambda t: (t, 0)),
        out_shape=jax.ShapeDtypeStruct((tp, ATTN_WIDTH), BF16),
        compiler_params=_params("parallel"),
    )(sink_rows, bias, qkv, qkv, qkv, qkv, qkv)


def _conv_taps(x, halo):
    ext = jnp.concatenate([halo, x], axis=0)
    return [ext[8:] if k == 3 else pltpu.roll(ext, 3 - k, 0)[8:] for k in range(4)]


def _lru_gates(xc, wa, ba, wx, bx, sp):
    xb = xc.astype(BF16)
    r = _sigmoid(_dot(xb, wa) + ba)
    ig = _sigmoid(_dot(xb, wx) + bx)
    log_a = (-LRU_C * sp) * r
    a = jnp.exp(log_a)
    mult = jnp.sqrt(_neg_expm1(2.0 * log_a))
    return xb, r, ig, a, mult


SUBLANES = 8


def _scan_fwd(a, b, h_in):
    n, width = a.shape
    a, b = (v.reshape(n // SUBLANES, SUBLANES, width) for v in (a, b))
    in_group = lax.broadcasted_iota(jnp.int32, a.shape, 1)
    for d in (1, 2, 4):
        keep = in_group >= d
        b = jnp.where(keep, a * pltpu.roll(b, d, 1) + b, b)
        a = jnp.where(keep, a * pltpu.roll(a, d, 1), a)
    a, b = a.reshape(n, width), b.reshape(n, width)
    out, carry = [], h_in
    for g in range(0, n, SUBLANES):
        h = a[g:g + SUBLANES] * carry + b[g:g + SUBLANES]
        out.append(h)
        carry = h[SUBLANES - 1:]
    return jnp.concatenate(out, axis=0)


def _scan_rev(c, b, g_in):
    n, width = c.shape
    c, b = (v.reshape(n // SUBLANES, SUBLANES, width) for v in (c, b))
    in_group = lax.broadcasted_iota(jnp.int32, c.shape, 1)
    for d in (1, 2, 4):
        keep = in_group < SUBLANES - d
        b = jnp.where(keep, b + c * pltpu.roll(b, SUBLANES - d, 1), b)
        c = jnp.where(keep, c * pltpu.roll(c, SUBLANES - d, 1), c)
    c, b = c.reshape(n, width), b.reshape(n, width)
    out, carry = [], g_in
    for g in range(n - SUBLANES, -1, -SUBLANES):
        r = b[g:g + SUBLANES] + c[g:g + SUBLANES] * carry
        out.append(r)
        carry = r[:1]
    return jnp.concatenate(out[::-1], axis=0)


def _lru_fwd(xr, yr, conv_w, conv_b, wa, ba, wx, bx, lam):
    tp = xr.shape[0]
    tr = _row_tile(tp)
    qb = tr // BLOCK

    def body(xr_ref, yr_ref, cw_ref, cb_ref, wa_ref, ba_ref, wx_ref, bx_ref, lam_ref, hr_ref, rec_ref, halo, hprev):
        t = pl.program_id(0)

        @pl.when(t == 0)
        def _():
            halo[...] = jnp.zeros_like(halo)
            hprev[...] = jnp.zeros_like(hprev)

        cw, cb = cw_ref[...], cb_ref[...]
        wa_m, ba_v, wx_m, bx_v = wa_ref[...], ba_ref[...], wx_ref[...], bx_ref[...]
        sp = _softplus(-lam_ref[...])
        before, h_last = halo[...], hprev[0:1]
        for i in range(qb):
            rows = slice(i * BLOCK, (i + 1) * BLOCK)
            x = xr_ref[rows]
            taps = _conv_taps(x, before)
            before = x[BLOCK - 8:]
            xc = cb + sum(cw[k:k + 1] * taps[k] for k in range(4))
            _, _, ig, a, mult = _lru_gates(xc, wa_m, ba_v, wx_m, bx_v, sp)
            u = mult * (ig * xc)
            if i == 0:
                pos = t * tr + lax.broadcasted_iota(jnp.int32, xc.shape, 0)
                u = jnp.where(pos >= PAD_ROWS, u, 0.0)
            h = _scan_fwd(a, u, h_last)
            h_last = h[BLOCK - 1:]
            hr_ref[rows] = h
            gl, _ = _gelu(yr_ref[rows])
            rec_ref[rows] = (gl * h).astype(BF16)
        halo[...] = before
        hprev[0:1] = h_last

    blk = pl.BlockSpec((tr, LRU_WIDTH), lambda t: (t, 0))
    full = lambda a: pl.BlockSpec(a.shape, lambda t: (0,) * a.ndim)
    small = [conv_w, conv_b, wa, ba, wx, bx, lam]
    return pl.pallas_call(
        body, name="lru_fwd", grid=(tp // tr,),
        in_specs=[blk, blk] + [full(a) for a in small],
        out_specs=[blk, blk],
        out_shape=[jax.ShapeDtypeStruct((tp, LRU_WIDTH), F32), jax.ShapeDtypeStruct((tp, LRU_WIDTH), BF16)],
        scratch_shapes=[pltpu.VMEM((8, LRU_WIDTH), F32), pltpu.VMEM((8, LRU_WIDTH), F32)],
        compiler_params=_params("arbitrary"),
    )(xr, yr, *small)


def _outproj_fwd(attn, rec, w_out, head, x, g_post_mix, g_pre_ffn):
    tp = attn.shape[0]
    tr = _row_tile(tp)
    qb = tr // BLOCK

    def body(*refs):
        a_ref, r_ref, w_ref, head_ref = refs[:4]
        pieces = refs[4:4 + qb]
        gm_ref, gf_ref, mix_ref, h1_ref, u1_ref = refs[4 + qb:]
        mix = _dot(a_ref[...], w_ref[:ATTN_WIDTH]) + _dot(r_ref[...], w_ref[ATTN_WIDTH:])
        mix_ref[...] = mix
        mhat, _ = _rms(mix)
        h1 = _seq_tile(head_ref[...], pieces, pl.program_id(0)) + mhat * gm_ref[...]
        h1_ref[...] = h1
        hhat, _ = _rms(h1)
        u1_ref[...] = (hhat * gf_ref[...]).astype(BF16)

    row = lambda w: pl.BlockSpec((tr, w), lambda i: (i, 0))
    full = lambda a: pl.BlockSpec(a.shape, lambda i: (0,) * a.ndim)
    return pl.pallas_call(
        body, name="outproj_fwd", grid=(tp // tr,),
        in_specs=[row(ATTN_WIDTH), row(LRU_WIDTH), full(w_out), full(head)] + _seq_specs(tr)
        + [full(g_post_mix), full(g_pre_ffn)],
        out_specs=[row(D_MODEL), row(D_MODEL), row(D_MODEL)],
        out_shape=[jax.ShapeDtypeStruct((tp, D_MODEL), F32), jax.ShapeDtypeStruct((tp, D_MODEL), F32),
                   jax.ShapeDtypeStruct((tp, D_MODEL), BF16)],
        compiler_params=_params("parallel"),
    )(attn, rec, w_out, head, *([x] * qb), g_post_mix, g_pre_ffn)


def _ffn_fwd(u1, w1, w2, h1, tgt, g_post_ffn):
    tp = h1.shape[0]
    tr = _row_tile(tp)
    qb = tr // BLOCK

    def body(*refs):
        u_ref, w1_ref, w2_ref, h1_ref = refs[:4]
        t_pieces = refs[4:4 + qb]
        g_ref, r1_ref, dy_ref, df2_ref, loss_ref, dg_ref, acc = refs[4 + qb:]
        i, c = pl.program_id(0), pl.program_id(1)

        @pl.when((i == 0) & (c == 0))
        def _():
            loss_ref[...] = jnp.zeros_like(loss_ref)
            dg_ref[...] = jnp.zeros_like(dg_ref)

        r = jnp.maximum(_dot(u_ref[...], w1_ref[0]), 0.0)
        r1_ref[...] = r.astype(BF16)
        part = _dot((r * r).astype(BF16), w2_ref[0])

        @pl.when(c == 0)
        def _():
            acc[...] = part

        @pl.when(c > 0)
        def _():
            acc[...] += part

        @pl.when(c == N_CHIPS - 1)
        def _():
            g = g_ref[...]
            fhat, rs = _rms(acc[...])
            h2 = h1_ref[...] + fhat * g
            rows = i * tr + lax.broadcasted_iota(jnp.int32, h2.shape, 0)
            tgt_tile = jnp.concatenate([p[...] for p in t_pieces], axis=0)
            err = jnp.where(rows >= BLOCK, h2 - tgt_tile, 0.0)
            dy = err * (1.0 / D_MODEL)
            dy_ref[...] = dy
            loss_ref[...] += (0.5 / D_MODEL) * jnp.sum(err * err)
            df2, dg = _rms_bwd(fhat, rs, g, dy)
            df2_ref[...] = df2.astype(BF16)
            dg_ref[...] += dg

    row = pl.BlockSpec((tr, D_MODEL), lambda i, c: (i, 0))
    full = lambda a: pl.BlockSpec(a.shape, lambda i, c: (0,) * a.ndim)
    return pl.pallas_call(
        body, name="ffn_fwd", grid=(tp // tr, N_CHIPS),
        in_specs=[row, pl.BlockSpec((1, D_MODEL, FF_CHUNK), lambda i, c: (c, 0, 0)),
                  pl.BlockSpec((1, FF_CHUNK, D_MODEL), lambda i, c: (c, 0, 0)), row] + _seq_specs(tr)
        + [full(g_post_ffn)],
        out_specs=[pl.BlockSpec((tr, FF_CHUNK), lambda i, c: (i, c)), row, row,
                   pl.BlockSpec((1, 1), lambda i, c: (0, 0)), pl.BlockSpec((1, D_MODEL), lambda i, c: (0, 0))],
        out_shape=[jax.ShapeDtypeStruct((tp, D_FF), BF16), jax.ShapeDtypeStruct((tp, D_MODEL), F32),
                   jax.ShapeDtypeStruct((tp, D_MODEL), BF16), jax.ShapeDtypeStruct((1, 1), F32),
                   jax.ShapeDtypeStruct((1, D_MODEL), F32)],
        scratch_shapes=[pltpu.VMEM((tr, D_MODEL), F32)],
        compiler_params=_params("arbitrary", "arbitrary"),
    )(u1, w1, w2, h1, *([tgt] * qb), g_post_ffn)


def _ffn_bwd_data(df2, r1, w1, w2, dy, h1, mix, g_pre_ffn, g_post_mix):
    tp = h1.shape[0]
    tr = _row_tile(tp)

    def body(df2_ref, r1_ref, w1_ref, w2_ref, dy_ref, h1_ref, mix_ref, gf_ref, gm_ref,
             da_ref, dh1_ref, dmix_ref, dgf_ref, dgm_ref, acc):
        i, c = pl.program_id(0), pl.program_id(1)

        @pl.when((i == 0) & (c == 0))
        def _():
            dgf_ref[...] = jnp.zeros_like(dgf_ref)
            dgm_ref[...] = jnp.zeros_like(dgm_ref)

        df = _dot_nt(df2_ref[...], w2_ref[0])
        da = (df * (2.0 * r1_ref[...].astype(F32))).astype(BF16)
        da_ref[...] = da
        part = _dot_nt(da, w1_ref[0])

        @pl.when(c == 0)
        def _():
            acc[...] = part

        @pl.when(c > 0)
        def _():
            acc[...] += part

        @pl.when(c == N_CHIPS - 1)
        def _():
            hhat, rs = _rms(h1_ref[...])
            dx, dgf = _rms_bwd(hhat, rs, gf_ref[...], acc[...])
            dh1 = dy_ref[...] + dx
            dh1_ref[...] = dh1
            dgf_ref[...] += dgf
            mhat, rsm = _rms(mix_ref[...])
            dmix, dgm = _rms_bwd(mhat, rsm, gm_ref[...], dh1)
            dmix_ref[...] = dmix.astype(BF16)
            dgm_ref[...] += dgm

    row = pl.BlockSpec((tr, D_MODEL), lambda i, c: (i, 0))
    chunk = pl.BlockSpec((tr, FF_CHUNK), lambda i, c: (i, c))
    gain = pl.BlockSpec((1, D_MODEL), lambda i, c: (0, 0))
    return pl.pallas_call(
        body, name="ffn_bwd_data", grid=(tp // tr, N_CHIPS),
        in_specs=[row, chunk, pl.BlockSpec((1, D_MODEL, FF_CHUNK), lambda i, c: (c, 0, 0)),
                  pl.BlockSpec((1, FF_CHUNK, D_MODEL), lambda i, c: (c, 0, 0)), row, row, row, gain, gain],
        out_specs=[chunk, row, row, gain, gain],
        out_shape=[jax.ShapeDtypeStruct((tp, D_FF), BF16), jax.ShapeDtypeStruct((tp, D_MODEL), F32),
                   jax.ShapeDtypeStruct((tp, D_MODEL), BF16), jax.ShapeDtypeStruct((1, D_MODEL), F32),
                   jax.ShapeDtypeStruct((1, D_MODEL), F32)],
        scratch_shapes=[pltpu.VMEM((tr, D_MODEL), F32)],
        compiler_params=_params("arbitrary", "arbitrary"),
    )(df2, r1, w1, w2, dy, h1, mix, g_pre_ffn, g_post_mix)


def _ffn_bwd_weights(u1, da1, r1, df2):
    tp = u1.shape[0]
    tr = _wgrad_row_tile(tp)

    def body(u_ref, da_ref, r1_ref, df2_ref, dw1_ref, dw2_ref):
        i = pl.program_id(1)
        r = r1_ref[...].astype(F32)
        p1 = _dot_tn(u_ref[...], da_ref[...])
        p2 = _dot_tn((r * r).astype(BF16), df2_ref[...])

        @pl.when(i == 0)
        def _():
            dw1_ref[0] = p1
            dw2_ref[0] = p2

        @pl.when(i > 0)
        def _():
            dw1_ref[0] += p1
            dw2_ref[0] += p2

    row = pl.BlockSpec((tr, D_MODEL), lambda c, i: (i, 0))
    chunk = pl.BlockSpec((tr, FF_CHUNK), lambda c, i: (i, c))
    return pl.pallas_call(
        body, name="ffn_bwd_weights", grid=(N_CHIPS, tp // tr),
        in_specs=[row, chunk, chunk, row],
        out_specs=[pl.BlockSpec((1, D_MODEL, FF_CHUNK), lambda c, i: (c, 0, 0)),
                   pl.BlockSpec((1, FF_CHUNK, D_MODEL), lambda c, i: (c, 0, 0))],
        out_shape=[jax.ShapeDtypeStruct((N_CHIPS, D_MODEL, FF_CHUNK), F32),
                   jax.ShapeDtypeStruct((N_CHIPS, FF_CHUNK, D_MODEL), F32)],
        compiler_params=_params("parallel", "arbitrary"),
    )(u1, da1, r1, df2)


def _outproj_bwd(dmix, w_out, attn, rec, token):
    tp = dmix.shape[0]
    tr = _wgrad_row_tile(tp)

    def body(dm_ref, w_ref, a_ref, r_ref, _, da_ref, dr_ref, dw_ref):
        i = pl.program_id(0)
        dm = dm_ref[...]
        dcat = _dot_nt(dm, w_ref[...])
        da_ref[...] = dcat[:, :ATTN_WIDTH].astype(BF16)
        dr_ref[...] = dcat[:, ATTN_WIDTH:]
        pa = _dot_tn(a_ref[...], dm)
        pr = _dot_tn(r_ref[...], dm)

        @pl.when(i == 0)
        def _():
            dw_ref[:ATTN_WIDTH] = pa
            dw_ref[ATTN_WIDTH:] = pr

        @pl.when(i > 0)
        def _():
            dw_ref[:ATTN_WIDTH] += pa
            dw_ref[ATTN_WIDTH:] += pr

    row = lambda w: pl.BlockSpec((tr, w), lambda i: (i, 0))
    full = pl.BlockSpec((D_MODEL, D_MODEL), lambda i: (0, 0))
    return pl.pallas_call(
        body, name="outproj_bwd", grid=(tp // tr,),
        in_specs=[row(D_MODEL), full, row(ATTN_WIDTH), row(LRU_WIDTH), pl.BlockSpec(token.shape, lambda i: (0, 0))],
        out_specs=[row(ATTN_WIDTH), row(LRU_WIDTH), full],
        out_shape=[jax.ShapeDtypeStruct((tp, ATTN_WIDTH), BF16), jax.ShapeDtypeStruct((tp, LRU_WIDTH), F32),
                   jax.ShapeDtypeStruct((D_MODEL, D_MODEL), F32)],
        compiler_params=_params("arbitrary"),
    )(dmix, w_out, attn, rec, token)


N_VEC_ROWS = 8


def _lru_bwd(xr, yr, hr, drec, conv_w, conv_b, wa, ba, wx, bx, lam):
    tp = xr.shape[0]
    tr = _row_tile(tp)
    qb, nt = tr // BLOCK, tp // tr

    def body(xr_ref, xh_ref, yr_ref, hr_ref, hp_ref, dr_ref, cw_ref, cb_ref, wa_ref, ba_ref, wx_ref, bx_ref, lam_ref,
             dxr_ref, dyr_ref, dwa_ref, dwx_ref, vec_ref, g_next, a_next, dxc_next, dsp):
        s = pl.program_id(0)
        t = nt - 1 - s

        @pl.when(s == 0)
        def _():
            g_next[...] = jnp.zeros_like(g_next)
            a_next[...] = jnp.zeros_like(a_next)
            dxc_next[...] = jnp.zeros_like(dxc_next)
            dsp[...] = jnp.zeros_like(dsp)
            dwa_ref[...] = jnp.zeros_like(dwa_ref)
            dwx_ref[...] = jnp.zeros_like(dwx_ref)
            vec_ref[...] = jnp.zeros_like(vec_ref)

        first_tile = t == 0
        cw, cb = cw_ref[...], cb_ref[...]
        lam_v = lam_ref[...]
        sp = _softplus(-lam_v)
        wa_m, ba_v, wx_m, bx_v = wa_ref[...], ba_ref[...], wx_ref[...], bx_ref[...]
        rows = lax.broadcasted_iota(jnp.int32, (BLOCK, LRU_WIDTH), 0)
        col = lambda v: jnp.sum(v, axis=0, keepdims=True)

        g_after, a_after, dxc_after = g_next[0:1], a_next[0:1], dxc_next[...]
        xbs, dgrs, dgis = [], [], []
        vec = [jnp.zeros((1, LRU_WIDTH), F32) for _ in range(N_VEC_ROWS)]
        for i in reversed(range(qb)):
            blk = slice(i * BLOCK, (i + 1) * BLOCK)
            if i == 0:
                x_before = jnp.where(first_tile, 0.0, xh_ref[...])
                h_before = jnp.where(first_tile, 0.0, hp_ref[7:8])
            else:
                x_before = xr_ref[i * BLOCK - 8:i * BLOCK]
                h_before = hr_ref[i * BLOCK - 1:i * BLOCK]
            taps = _conv_taps(xr_ref[blk], x_before)
            xc = cb + sum(cw[k:k + 1] * taps[k] for k in range(4))
            xb, r, ig, a, mult = _lru_gates(xc, wa_m, ba_v, wx_m, bx_v, sp)

            yr_v = yr_ref[blk]
            gl, th = _gelu(yr_v)
            h = hr_ref[blk]
            drec = dr_ref[blk]
            dyr_ref[blk] = (drec * h * _gelu_grad(yr_v, th)).astype(BF16)

            a_up = jnp.where(rows == BLOCK - 1, a_after, pltpu.roll(a, BLOCK - 1, 0))
            g = _scan_rev(a_up, drec * gl, g_after)
            g_after, a_after = g[0:1], a[0:1]

            h_prev = jnp.where(rows == 0, h_before, pltpu.roll(h, 1, 0))
            du, da = g, g * h_prev
            if i == 0:
                real = (t * tr + rows) >= PAD_ROWS
                du, da = jnp.where(real, du, 0.0), jnp.where(real, da, 0.0)
            dmult = du * (ig * xc)
            dig = du * (mult * xc)
            dxc = du * (mult * ig)
            dlog_a = da * a - dmult * (a * a / mult)
            if i == 0:
                dlog_a = jnp.where(real, dlog_a, 0.0)
            dgr = (dlog_a * (-LRU_C * sp)) * (r * (1.0 - r))
            dgi = dig * (ig * (1.0 - ig))
            dgr_b, dgi_b = dgr.astype(BF16), dgi.astype(BF16)
            dxc = dxc + _dot_nt(dgr_b, wa_m) + _dot_nt(dgi_b, wx_m)
            xbs.append(xb)
            dgrs.append(dgr_b)
            dgis.append(dgi_b)

            ext = jnp.concatenate([dxc, dxc_after], axis=0)
            up = [ext[:BLOCK] if j == 0 else pltpu.roll(ext, BLOCK + 8 - j, 0)[:BLOCK] for j in range(4)]
            dxr_ref[blk] = sum(cw[k:k + 1] * up[3 - k] for k in range(4)).astype(BF16)
            dxc_after = dxc[:8]

            for k in range(4):
                vec[k] = vec[k] + col(dxc * taps[k])
            vec[4] = vec[4] + col(dxc)
            vec[5] = vec[5] + col(dgr)
            vec[6] = vec[6] + col(dgi)
            vec[7] = vec[7] + col(dlog_a * (-LRU_C * r))

        g_next[0:1], a_next[0:1], dxc_next[...] = g_after, a_after, dxc_after
        xb_all = jnp.concatenate(xbs, axis=0)
        dwa_ref[...] += _dot_tn(xb_all, jnp.concatenate(dgrs, axis=0))
        dwx_ref[...] += _dot_tn(xb_all, jnp.concatenate(dgis, axis=0))
        for k in range(7):
            vec_ref[k:k + 1] += vec[k]
        dsp[0:1] += vec[7]

        @pl.when(s == nt - 1)
        def _():
            vec_ref[7:8] = dsp[0:1] * (-_sigmoid(-lam_v))

    blk_spec = pl.BlockSpec((tr, LRU_WIDTH), lambda s: (nt - 1 - s, 0))
    rows_before = pl.BlockSpec((8, LRU_WIDTH), lambda s: (jnp.maximum((nt - 1 - s) * (tr // 8) - 1, 0), 0))
    full = lambda a: pl.BlockSpec(a.shape, lambda s: (0,) * a.ndim)
    small = [conv_w, conv_b, wa, ba, wx, bx, lam]
    sq = pl.BlockSpec((LRU_WIDTH, LRU_WIDTH), lambda s: (0, 0))
    return pl.pallas_call(
        body, name="lru_bwd", grid=(nt,),
        in_specs=[blk_spec, rows_before, blk_spec, blk_spec, rows_before, blk_spec] + [full(a) for a in small],
        out_specs=[blk_spec, blk_spec, sq, sq, pl.BlockSpec((N_VEC_ROWS, LRU_WIDTH), lambda s: (0, 0))],
        out_shape=[jax.ShapeDtypeStruct((tp, LRU_WIDTH), BF16), jax.ShapeDtypeStruct((tp, LRU_WIDTH), BF16),
                   jax.ShapeDtypeStruct((LRU_WIDTH, LRU_WIDTH), F32), jax.ShapeDtypeStruct((LRU_WIDTH, LRU_WIDTH), F32),
                   jax.ShapeDtypeStruct((N_VEC_ROWS, LRU_WIDTH), F32)],
        scratch_shapes=[pltpu.VMEM((8, LRU_WIDTH), F32)] * 4,
        compiler_params=_params("arbitrary"),
    )(xr, xr, yr, hr, hr, drec, *small)


def _attn_bwd(qkv, dattn, sinks):
    tp = qkv.shape[0]
    tr = _row_tile(tp)
    qb, nt = tr // BLOCK, tp // tr
    n_groups = ATTN_HEADS // GQA_GROUP
    sink_rows, bias = _attn_consts(sinks)

    def body(s_ref, b_ref, q_ref, kp_ref, kc_ref, vp_ref, vc_ref, do_ref, dq_ref, dkv_ref, ex_ref, ds_ref, dsink):
        t = pl.program_id(0)

        @pl.when(t == 0)
        def _():
            dsink[...] = jnp.zeros_like(dsink)

        k_all = jnp.concatenate([kp_ref[...], kc_ref[...]], axis=0)
        v_all = jnp.concatenate([vp_ref[...], vc_ref[...]], axis=0)
        tail = None
        for i in range(qb):
            rows = slice(i * BLOCK, (i + 1) * BLOCK)
            q, do = q_ref[rows], do_ref[rows]
            k2, v2 = k_all[i * BLOCK:(i + 2) * BLOCK], v_all[i * BLOCK:(i + 2) * BLOCK]
            bias_n = _block_bias(b_ref, t, qb, i)
            dqs, dks, dvs = [], [], []
            for g in range(n_groups):
                cols = slice(g * HEAD_DIM, (g + 1) * HEAD_DIM)
                k_g, v_g = k2[:, cols], v2[:, cols]
                qg = _stack_heads(q, g) * jnp.asarray(_QSCALE, BF16)
                dog = _stack_heads(do, g)
                p, ps = _attn_probs_t(k_g, qg, bias_n, s_ref[g:g + 1])
                dpt = _dot_nt(v_g, dog)
                delta = jnp.sum(p * dpt, axis=0, keepdims=True)
                dst = (p * (dpt - delta)).astype(BF16)
                dqs.append(_dot_tn(dst, k_g) * _QSCALE)
                dks.append(_dot(dst, qg))
                dvs.append(_dot(p.astype(BF16), dog))
                dsink[g:g + 1] -= ps * delta
            dq_ref[rows] = _unstack_heads(dqs).astype(BF16)
            dkv = jnp.concatenate(dks + dvs, axis=1)
            if i == 0:
                ex_ref[0] = dkv[:BLOCK]
            else:
                dkv_ref[(i - 1) * BLOCK:i * BLOCK] = (tail + dkv[:BLOCK]).astype(BF16)
            tail = dkv[BLOCK:]
        dkv_ref[(qb - 1) * BLOCK:] = tail.astype(BF16)

        @pl.when(t == nt - 1)
        def _():
            lane = lax.broadcasted_iota(jnp.int32, (1, ATTN_HEADS), 1)
            acc = jnp.zeros((1, ATTN_HEADS), F32)
            for h in range(ATTN_HEADS):
                g, hh = divmod(h, GQA_GROUP)
                acc = acc + jnp.where(lane == h, jnp.sum(dsink[g:g + 1, hh * BLOCK:(hh + 1) * BLOCK]), 0.0)
            ds_ref[...] = acc

    cur = lambda w: pl.BlockSpec((tr, w), lambda t: (t, 0))
    return pl.pallas_call(
        body, name="attn_bwd", grid=(nt,),
        in_specs=[_SINK_SPEC, _BIAS_SPEC, cur(ATTN_WIDTH)] + _kv_specs(tr) + [cur(ATTN_WIDTH)],
        out_specs=[cur(ATTN_WIDTH), cur(2 * KV_WIDTH), pl.BlockSpec((1, BLOCK, 2 * KV_WIDTH), lambda t: (t, 0, 0)),
                   pl.BlockSpec((1, ATTN_HEADS), lambda t: (0, 0))],
        out_shape=[jax.ShapeDtypeStruct((tp, ATTN_WIDTH), BF16), jax.ShapeDtypeStruct((tp, 2 * KV_WIDTH), BF16),
                   jax.ShapeDtypeStruct((nt, BLOCK, 2 * KV_WIDTH), F32), jax.ShapeDtypeStruct((1, ATTN_HEADS), F32)],
        scratch_shapes=[pltpu.VMEM((n_groups, GROUP_ROWS), F32)],
        compiler_params=_params("arbitrary"),
    )(sink_rows, bias, qkv, qkv, qkv, qkv, qkv, dattn)


def _inproj_bwd(dq, dkv, dkv_extra, dxr, dyr, w_in, u0, head, x, dh1, g):
    tp = dq.shape[0]
    tr = _row_tile(tp)
    nt, qb = tp // tr, tr // BLOCK

    def body(*refs):
        dq_ref, dkv_ref, ex_ref, dxr_ref, dyr_ref, w_ref, u_ref, head_ref = refs[:8]
        pieces = refs[8:8 + qb]
        dh1_ref, g_ref, gx_ref, dhead_ref, dw_ref, dg_ref, buf, sems = refs[8 + qb:]
        i = pl.program_id(0)
        slot = i % 2

        def out_copy(step, at):
            return pltpu.make_async_copy(buf.at[at], gx_ref.at[pl.ds(step * tr - BLOCK, tr)], sems.at[at])

        extra = jnp.where(i < nt - 1, ex_ref[0], 0.0)
        last = (dkv_ref[tr - BLOCK:].astype(F32) + extra).astype(BF16)
        dkv = last if tr == BLOCK else jnp.concatenate([dkv_ref[:tr - BLOCK], last], axis=0)
        dz = jnp.concatenate([dq_ref[...], dkv, dxr_ref[...], dyr_ref[...]], axis=1)
        du = _dot_nt(dz, w_ref[...])
        hhat, rs = _rms(_seq_tile(head_ref[...], pieces, i))
        dx, dg = _rms_bwd(hhat, rs, g_ref[...], du)
        dh0 = dh1_ref[...] + dx

        @pl.when(i >= 3)
        def _():
            out_copy(i - 2, slot).wait()

        buf[slot] = dh0

        @pl.when(i == 0)
        def _():
            dhead_ref[...] = dh0[:BLOCK]
            if tr > BLOCK:
                first = pltpu.make_async_copy(buf.at[0, pl.ds(BLOCK, tr - BLOCK)], gx_ref.at[pl.ds(0, tr - BLOCK)],
                                              sems.at[0])
                first.start()
                first.wait()

        @pl.when(i >= 1)
        def _():
            out_copy(i, slot).start()

        @pl.when(i == nt - 1)
        def _():
            if nt >= 3:
                out_copy(nt - 2, (nt - 2) % 2).wait()
            if nt >= 2:
                out_copy(nt - 1, (nt - 1) % 2).wait()

        pw = _dot_tn(u_ref[...], dz)

        @pl.when(i == 0)
        def _():
            dw_ref[...] = pw
            dg_ref[...] = dg

        @pl.when(i > 0)
        def _():
            dw_ref[...] += pw
            dg_ref[...] += dg

    row = lambda w: pl.BlockSpec((tr, w), lambda i: (i, 0))
    full = lambda shape: pl.BlockSpec(shape, lambda i: (0,) * len(shape))
    return pl.pallas_call(
        body, name="inproj_bwd", grid=(tp // tr,),
        in_specs=[row(ATTN_WIDTH), row(2 * KV_WIDTH),
                  pl.BlockSpec((1, BLOCK, 2 * KV_WIDTH), lambda i: (jnp.minimum(i + 1, nt - 1), 0, 0)),
                  row(LRU_WIDTH), row(LRU_WIDTH), full(w_in.shape), row(D_MODEL), full(head.shape)]
        + _seq_specs(tr) + [row(D_MODEL), full(g.shape)],
        out_specs=[pl.BlockSpec(memory_space=pl.ANY), full((BLOCK, D_MODEL)), full((D_MODEL, IN_WIDTH)),
                   full((1, D_MODEL))],
        out_shape=[jax.ShapeDtypeStruct(x.shape, F32), jax.ShapeDtypeStruct((BLOCK, D_MODEL), F32),
                   jax.ShapeDtypeStruct((D_MODEL, IN_WIDTH), F32), jax.ShapeDtypeStruct((1, D_MODEL), F32)],
        scratch_shapes=[pltpu.VMEM((2, tr, D_MODEL), F32), pltpu.SemaphoreType.DMA((2,))],
        compiler_params=_params("arbitrary"),
    )(dq, dkv, dkv_extra, dxr, dyr, w_in, u0, head, *([x] * qb), dh1, g)


def _dense_block_diag(w):
    eye = jnp.eye(LRU_BLOCKS, dtype=w.dtype)
    return (w[:, :, None, :] * eye[:, None, :, None]).reshape(LRU_WIDTH, LRU_WIDTH)


def _diag_blocks(dense):
    d4 = dense.reshape(LRU_BLOCKS, LRU_BLOCK, LRU_BLOCKS, LRU_BLOCK)
    return jnp.stack([d4[n, :, n, :] for n in range(LRU_BLOCKS)])


def _local_step(head, x, tgt, g_pre_mix, w_in, conv_w, conv_b, w_a, b_a, w_x, b_x, lam, sinks, g_post_mix,
                g_pre_ffn, g_post_ffn, late_weights, on_ffn_grads, token):
    wa = _dense_block_diag(w_a).astype(BF16)
    wx = _dense_block_diag(w_x).astype(BF16)

    u0, qkv, xr, yr = _inproj_fwd(head, x, g_pre_mix, w_in, token)
    attn = _attn_fwd(qkv, sinks)
    hr, rec = _lru_fwd(xr, yr, conv_w, conv_b, wa, b_a, wx, b_x, lam)
    w_out, w1, w2 = late_weights([attn, rec])
    mix, h1, u1 = _outproj_fwd(attn, rec, w_out, head, x, g_post_mix, g_pre_ffn)
    r1, dy, df2, loss, dg_post_ffn = _ffn_fwd(u1, w1, w2, h1, tgt, g_post_ffn)

    da1, dh1, dmix, dg_pre_ffn, dg_post_mix = _ffn_bwd_data(df2, r1, w1, w2, dy, h1, mix, g_pre_ffn, g_post_mix)
    dw1, dw2 = _ffn_bwd_weights(u1, da1, r1, df2)
    token2 = on_ffn_grads(dw1, dw2)
    dattn, drec, dw_out = _outproj_bwd(dmix, w_out, attn, rec, token2)
    dxr, dyr, dwa, dwx, vec = _lru_bwd(xr, yr, hr, drec, conv_w, conv_b, wa, b_a, wx, b_x, lam)
    dq, dkv, dkv_extra, dsinks = _attn_bwd(qkv, dattn, sinks)
    dx, dhead, dw_in, dg_pre_mix = _inproj_bwd(dq, dkv, dkv_extra, dxr, dyr, w_in, u0, head, x, dh1, g_pre_mix)

    grads = dict(
        g_pre_mix=dg_pre_mix, w_in=dw_in, conv_w=vec[0:4], conv_b=vec[4:5], w_a=_diag_blocks(dwa), b_a=vec[5:6],
        w_x=_diag_blocks(dwx), b_x=vec[6:7], lru_lambda=vec[7:8], attn_sinks=dsinks, w_out=dw_out,
        g_post_mix=dg_post_mix, g_pre_ffn=dg_pre_ffn, w_ff1=dw1, w_ff2=dw2, g_post_ffn=dg_post_ffn)
    return loss, dx, dhead, grads


HBM = pl.BlockSpec(memory_space=pltpu.HBM)


def _mesh_pos():
    return lax.axis_index("x"), lax.axis_index("y"), lax.axis_index("c")


def _other_chips(x, y):
    return [(1 - x, y), (x, 1 - y), (1 - x, 1 - y)]


def _remote(src, dst, send_sem, recv_sem, to):
    return pltpu.make_async_remote_copy(src_ref=src, dst_ref=dst, send_sem=send_sem, recv_sem=recv_sem,
                                        device_id=to, device_id_type=MESH)


def _gather_weights(shards, lands, tiny, tiny_land):
    nbig = len(shards)

    def body(*refs):
        srcs, tiny_src = refs[:nbig], refs[nbig]
        outs, tiny_out = refs[2 * nbig + 2:3 * nbig + 2], refs[3 * nbig + 2]
        ici_send, ici_recv, d2d_send, d2d_recv, tiny_send, tiny_recv = refs[3 * nbig + 3:]
        x, y, c = _mesh_pos()
        me = 2 * x + y
        chips = _other_chips(x, y)
        sibling = (x, y, 1 - c)
        sends = []
        for w, (src, out) in enumerate(zip(srcs, outs)):
            hr = src.shape[0] // 2
            for j, chip in enumerate(chips):
                k = 3 * w + j
                cp = _remote(src.at[pl.ds(c * hr, hr)], out.at[me, pl.ds(c * hr, hr)],
                             ici_send.at[k], ici_recv.at[k], (*chip, c))
                cp.start()
                sends.append(cp)
        for j, chip in enumerate(chips):
            cp = _remote(tiny_src, tiny_out.at[me], tiny_send.at[j], tiny_recv.at[j], (*chip, c))
            cp.start()
            sends.append(cp)
        for w, (src, out) in enumerate(zip(srcs, outs)):
            hr = src.shape[0] // 2
            for j, (px, py) in enumerate(chips):
                k = 3 * w + j
                landed = out.at[2 * px + py, pl.ds(c * hr, hr)]
                _remote(landed, landed, ici_send.at[k], ici_recv.at[k], sibling).wait_recv()
                cp = _remote(landed, landed, d2d_send.at[k], d2d_recv.at[k], sibling)
                cp.start()
                sends.append(cp)
        for w, (src, out) in enumerate(zip(srcs, outs)):
            hr = src.shape[0] // 2
            for j, (px, py) in enumerate(chips):
                k = 3 * w + j
                other = out.at[2 * px + py, pl.ds((1 - c) * hr, hr)]
                _remote(other, other, d2d_send.at[k], d2d_recv.at[k], sibling).wait_recv()
        for j, (px, py) in enumerate(chips):
            blk = tiny_out.at[2 * px + py]
            _remote(blk, blk, tiny_send.at[j], tiny_recv.at[j], sibling).wait_recv()
        for cp in sends:
            cp.wait_send()

    out_shape = [jax.ShapeDtypeStruct(l.shape, l.dtype) for l in list(lands) + [tiny_land]]
    n = 3 * nbig
    return pl.pallas_call(
        body, name="gather_weights", out_shape=out_shape,
        in_specs=[HBM] * (2 * nbig + 2), out_specs=[HBM] * (nbig + 1),
        input_output_aliases={nbig + 1 + i: i for i in range(nbig + 1)},
        scratch_shapes=[pltpu.SemaphoreType.DMA((n,)),
                        pltpu.SemaphoreType.DMA((n,)), pltpu.SemaphoreType.DMA((n,)), pltpu.SemaphoreType.DMA((n,)),
                        pltpu.SemaphoreType.DMA((3,)), pltpu.SemaphoreType.DMA((3,))],
    )(*shards, tiny, *lands, tiny_land)


def _prep_shard(w, me):
    rows, cols = w.shape
    tr = 256 if rows % 256 == 0 else rows

    def body(me_ref, w_ref, s_ref, l_ref):
        b = w_ref[...].astype(BF16)
        s_ref[...] = b
        l_ref[0] = b

    return pl.pallas_call(
        body, name="prep_shard",
        grid_spec=pltpu.PrefetchScalarGridSpec(
            num_scalar_prefetch=1, grid=(rows // tr,),
            in_specs=[pl.BlockSpec((tr, cols), lambda i, me_ref: (i, 0))],
            out_specs=[pl.BlockSpec((tr, cols), lambda i, me_ref: (i, 0)),
                       pl.BlockSpec((1, tr, cols), lambda i, me_ref: (me_ref[0], i, 0))]),
        out_shape=[jax.ShapeDtypeStruct((rows, cols), BF16), jax.ShapeDtypeStruct((N_CHIPS, rows, cols), BF16)],
        compiler_params=_params("parallel"),
    )(me, w)


def _prep_tiny(tiny, me):
    def body(me_ref, t_ref, l_ref):
        l_ref[0] = t_ref[...]

    return pl.pallas_call(
        body, name="prep_tiny",
        grid_spec=pltpu.PrefetchScalarGridSpec(
            num_scalar_prefetch=1, grid=(1,),
            in_specs=[pl.BlockSpec(tiny.shape, lambda i, me_ref: (0, 0))],
            out_specs=pl.BlockSpec((1,) + tiny.shape, lambda i, me_ref: (me_ref[0], 0, 0))),
        out_shape=jax.ShapeDtypeStruct((N_CHIPS,) + tiny.shape, tiny.dtype),
    )(me, tiny)


N_DEV = 8


def _gather_small(block):
    m_per, n = block.shape

    def body(x_ref, out_ref, send_sems, recv_sems, local_sem):
        x, y, c = _mesh_pos()
        me, sibling = (x, y, c), (x, y, 1 - c)
        chips = _other_chips(x, y)

        def rows(px, py, pc):
            return out_ref.at[pl.ds((4 * px + 2 * py + pc) * m_per, m_per), :]

        def copy(k, block_of, to, src=None):
            return _remote(rows(*block_of) if src is None else src, rows(*block_of),
                           send_sems.at[k], recv_sems.at[k], to)

        mine = pltpu.make_async_copy(x_ref, rows(*me), local_sem)
        mine.start()
        first = [copy(0, me, sibling, src=x_ref)]
        first += [copy(1 + j, me, (*chip, c), src=x_ref) for j, chip in enumerate(chips)]
        for cp in first:
            cp.start()
        passed = [copy(4 + j, (*chip, c), sibling) for j, chip in enumerate(chips)]
        for j, chip in enumerate(chips):
            copy(1 + j, (*chip, c), me).wait_recv()
            passed[j].start()
        copy(0, sibling, me).wait_recv()
        for j, chip in enumerate(chips):
            copy(4 + j, (*chip, 1 - c), me).wait_recv()
        for cp in first + passed:
            cp.wait_send()
        mine.wait()

    return pl.pallas_call(
        body, name="gather_small", out_shape=jax.ShapeDtypeStruct((N_DEV * m_per, n), block.dtype),
        in_specs=[pl.BlockSpec(memory_space=pltpu.VMEM)], out_specs=pl.BlockSpec(memory_space=pltpu.VMEM),
        scratch_shapes=[pltpu.SemaphoreType.DMA((7,)), pltpu.SemaphoreType.DMA((7,)), pltpu.SemaphoreType.DMA],
        compiler_params=pltpu.CompilerParams(vmem_limit_bytes=VMEM_LIMIT_V7X),
    )(block)


def _sibling_exchange(parts):
    def body(*refs):
        n = len(parts)
        srcs, outs, send_sems, recv_sems = refs[:n], refs[n:2 * n], refs[2 * n], refs[2 * n + 1]
        x, y, c = _mesh_pos()
        sibling = (x, y, 1 - c)
        cps = []
        for w, (src, out) in enumerate(zip(srcs, outs)):
            hr = src.shape[1] // 2
            cp = _remote(src.at[:, pl.ds((1 - c) * hr, hr)], out, send_sems.at[w], recv_sems.at[w], sibling)
            cp.start()
            cps.append(cp)
        for cp in cps:
            cp.wait()

    n = len(parts)
    return pl.pallas_call(
        body, name="sibling_exchange",
        out_shape=[jax.ShapeDtypeStruct((p.shape[0], p.shape[1] // 2, p.shape[2]), p.dtype) for p in parts],
        in_specs=[HBM] * n, out_specs=[HBM] * n,
        scratch_shapes=[pltpu.SemaphoreType.DMA((n,)), pltpu.SemaphoreType.DMA((n,))],
    )(*parts)


def _chip_presum(part, from_sibling, pos):
    _, hr, cols = from_sibling.shape
    tr = 256 if hr % 256 == 0 else hr
    steps = hr // tr

    def body(pos_ref, a_ref, b_ref, o_ref, land_ref):
        s = (a_ref[...] + b_ref[...]).astype(BF16)
        o_ref[...] = s

        @pl.when(pl.program_id(1) == pos_ref[1])
        def _():
            land_ref[...] = s

    return pl.pallas_call(
        body, name="chip_presum",
        grid_spec=pltpu.PrefetchScalarGridSpec(
            num_scalar_prefetch=1, grid=(steps, N_CHIPS),
            in_specs=[pl.BlockSpec((1, tr, cols), lambda i, j, p: (j, p[0] * steps + i, 0)),
                      pl.BlockSpec((1, tr, cols), lambda i, j, p: (j, i, 0))],
            out_specs=[pl.BlockSpec((1, tr, cols), lambda i, j, p: (j, i, 0)),
                       pl.BlockSpec((1, tr, cols), lambda i, j, p: (p[1], p[0] * steps + i, 0))]),
        out_shape=[jax.ShapeDtypeStruct(from_sibling.shape, BF16),
                   jax.ShapeDtypeStruct((N_CHIPS, 2 * hr, cols), BF16)],
        compiler_params=_params("arbitrary", "arbitrary"),
    )(pos, part, from_sibling)


def _scatter_partials(cparts, lands, done_cparts=(), done_lands=()):
    n_new = len(cparts)
    nw = n_new + len(done_cparts)

    def body(*refs):
        srcs = refs[:nw]
        outs = refs[2 * nw:3 * nw]
        own_send, own_recv, ici_send, ici_recv, d2d_send, d2d_recv = refs[3 * nw:]
        x, y, c = _mesh_pos()
        me = 2 * x + y
        chips = _other_chips(x, y)
        sibling = (x, y, 1 - c)
        sends = []
        for w in list(range(n_new, nw)) + list(range(n_new)):
            src, out = srcs[w], outs[w]
            hr = src.shape[1]
            mine = out.at[me, pl.ds(c * hr, hr)]
            cp = _remote(src.at[me], mine, own_send.at[w], own_recv.at[w], sibling)
            cp.start()
            sends.append(cp)
            for j, (px, py) in enumerate(chips):
                if w >= n_new:
                    break
                k = 3 * w + j
                cp = _remote(src.at[2 * px + py], mine, ici_send.at[k], ici_recv.at[k], (px, py, c))
                cp.start()
                sends.append(cp)
        for w in list(range(n_new, nw)) + list(range(n_new)):
            src, out = srcs[w], outs[w]
            hr = src.shape[1]
            for j, (px, py) in enumerate(chips):
                k = 3 * w + j
                landed = out.at[2 * px + py, pl.ds(c * hr, hr)]
                if w < n_new:
                    _remote(landed, landed, ici_send.at[k], ici_recv.at[k], sibling).wait_recv()
                cp = _remote(landed, landed, d2d_send.at[k], d2d_recv.at[k], sibling)
                cp.start()
                sends.append(cp)
        for w, (src, out) in enumerate(zip(srcs, outs)):
            hr = src.shape[1]
            other = out.at[me, pl.ds((1 - c) * hr, hr)]
            _remote(other, other, own_send.at[w], own_recv.at[w], sibling).wait_recv()
            for j, (px, py) in enumerate(chips):
                k = 3 * w + j
                other = out.at[2 * px + py, pl.ds((1 - c) * hr, hr)]
                _remote(other, other, d2d_send.at[k], d2d_recv.at[k], sibling).wait_recv()
        for cp in sends:
            cp.wait_send()

    n = 3 * nw
    dma = pltpu.SemaphoreType.DMA
    every = list(cparts) + list(done_cparts)
    every_lands = list(lands) + list(done_lands)
    return pl.pallas_call(
        body, name="scatter_partials",
        out_shape=[jax.ShapeDtypeStruct(l.shape, l.dtype) for l in every_lands],
        in_specs=[HBM] * (2 * nw), out_specs=[HBM] * nw,
        input_output_aliases={nw + i: i for i in range(nw)},
        scratch_shapes=[dma((nw,)), dma((nw,)), dma((n,)), dma((n,)), dma((n,)), dma((n,))],
    )(*every, *every_lands)


SEM = pl.BlockSpec(memory_space=pltpu.SEMAPHORE)
SPLIT_COPY = pltpu.CompilerParams(has_side_effects=pltpu.SideEffectType.DATAFLOW_SIDE_EFFECTING)


def _hbm(a):
    return pltpu.with_memory_space_constraint(a, pltpu.HBM)


def _gather_copies(srcs, lands, send_sems, recv_sems):
    x, y, c = _mesh_pos()
    me = 2 * x + y
    sends, recvs = [], []
    for w, (src, land) in enumerate(zip(srcs, lands)):
        hr = src.shape[0] // 2
        for j, (px, py) in enumerate(_other_chips(x, y)):
            k = 3 * w + j
            sends.append(_remote(src.at[pl.ds(c * hr, hr)], land.at[me, pl.ds(c * hr, hr)],
                                 send_sems.at[k], recv_sems.at[k], (px, py, c)))
            got = land.at[2 * px + py, pl.ds(c * hr, hr)]
            recvs.append(_remote(got, got, send_sems.at[k], recv_sems.at[k], (px, py, c)))
    return sends, recvs


def _scatter_copies(srcs, lands, send_sems, recv_sems):
    x, y, c = _mesh_pos()
    me = 2 * x + y
    sends, recvs = [], []
    for w, (src, land) in enumerate(zip(srcs, lands)):
        hr = src.shape[1]
        for j, (px, py) in enumerate(_other_chips(x, y)):
            k = 3 * w + j
            sends.append(_remote(src.at[2 * px + py], land.at[me, pl.ds(c * hr, hr)],
                                 send_sems.at[k], recv_sems.at[k], (px, py, c)))
            got = land.at[2 * px + py, pl.ds(c * hr, hr)]
            recvs.append(_remote(got, got, send_sems.at[k], recv_sems.at[k], (px, py, c)))
    return sends, recvs


def _split_start(name, copies_of, srcs, land_shapes):
    n = len(srcs)
    k = 3 * n

    def body(*refs):
        src_refs, land_refs = refs[:n], refs[n:2 * n]
        send_sems, recv_sems = refs[2 * n], refs[2 * n + 1]
        token = refs[-1]
        sends, _ = copies_of(src_refs, land_refs, send_sems, recv_sems)
        for cp in sends:
            cp.start()
        token[...] = jnp.zeros_like(token)

    lands = [_hbm(s) for s in land_shapes]
    dma = pltpu.SemaphoreType.DMA
    res = pl.pallas_call(
        body, name=name,
        out_shape=(dma((k,)), dma((k,)), *[pltpu.HBM(s.shape, s.dtype) for s in srcs],
                   *[pltpu.HBM(s.shape, s.dtype) for s in land_shapes], jax.ShapeDtypeStruct((8, 128), F32)),
        in_specs=[HBM] * (2 * n),
        out_specs=(SEM, SEM, *([HBM] * (2 * n)), pl.BlockSpec(memory_space=pltpu.VMEM)),
        input_output_aliases={i: 2 + i for i in range(2 * n)},
        compiler_params=SPLIT_COPY,
    )(*[_hbm(s) for s in srcs], *lands)
    return res[0], res[1], list(res[2:2 + n]), list(res[2 + n:2 + 2 * n]), res[-1]


def _split_wait(name, copies_of, send_sems, recv_sems, srcs, lands, after):
    n = len(srcs)

    def body(*refs):
        src_refs, land_refs = refs[:n], refs[n:2 * n]
        sends, recvs = copies_of(src_refs, land_refs, refs[2 * n], refs[2 * n + 1])
        for cp in sends:
            cp.wait_send()
        for cp in recvs:
            cp.wait_recv()

    res = pl.pallas_call(
        body, name=name,
        out_shape=tuple(pltpu.HBM(s.shape, s.dtype) for s in list(srcs) + list(lands)),
        in_specs=[HBM] * (2 * n) + [SEM, SEM] + [pl.BlockSpec(memory_space=pl.ANY)] * len(after),
        out_specs=tuple([HBM] * (2 * n)),
        input_output_aliases={i: i for i in range(2 * n)},
        compiler_params=SPLIT_COPY,
    )(*srcs, *lands, send_sems, recv_sems, *after)
    return list(res[:n]), list(res[n:])


def _gather_finish(lands):
    n = len(lands)

    def body(*refs):
        outs = refs[n:2 * n]
        d2d_send, d2d_recv = refs[2 * n:]
        x, y, c = _mesh_pos()
        chips = _other_chips(x, y)
        sibling = (x, y, 1 - c)
        sends = []
        for w, out in enumerate(outs):
            hr = out.shape[1] // 2
            for j, (px, py) in enumerate(chips):
                landed = out.at[2 * px + py, pl.ds(c * hr, hr)]
                cp = _remote(landed, landed, d2d_send.at[3 * w + j], d2d_recv.at[3 * w + j], sibling)
                cp.start()
                sends.append(cp)
        for w, out in enumerate(outs):
            hr = out.shape[1] // 2
            for j, (px, py) in enumerate(chips):
                other = out.at[2 * px + py, pl.ds((1 - c) * hr, hr)]
                _remote(other, other, d2d_send.at[3 * w + j], d2d_recv.at[3 * w + j], sibling).wait_recv()
        for cp in sends:
            cp.wait_send()

    dma = pltpu.SemaphoreType.DMA
    return pl.pallas_call(
        body, name="gather_finish",
        out_shape=[jax.ShapeDtypeStruct(l.shape, l.dtype) for l in lands],
        in_specs=[HBM] * n, out_specs=[HBM] * n,
        input_output_aliases={i: i for i in range(n)},
        scratch_shapes=[dma((3 * n,)), dma((3 * n,))],
    )(*lands)


def _adamw(w, g, m, v):
    m = ADAM_B1 * m + (1.0 - ADAM_B1) * g
    v = ADAM_B2 * v + (1.0 - ADAM_B2) * (g * g)
    m_hat = m / (1.0 - ADAM_B1 ** ADAM_STEP)
    v_hat = v / (1.0 - ADAM_B2 ** ADAM_STEP)
    delta = -ADAM_LR * (m_hat / (jnp.sqrt(v_hat) + ADAM_EPS) + ADAM_WD * w)
    return delta, m, v


def _adamw_big(partials, w, m, v):
    rows, cols = w.shape
    tr = 256

    def body(p_ref, w_ref, m_ref, v_ref, g_ref, d_ref, m2_ref, v2_ref):
        g = ((p_ref[0].astype(F32) + p_ref[1].astype(F32)) + p_ref[2].astype(F32)) + p_ref[3].astype(F32)
        g_ref[...] = g
        d_ref[...], m2_ref[...], v2_ref[...] = _adamw(w_ref[...], g, m_ref[...], v_ref[...])

    blk = pl.BlockSpec((tr, cols), lambda i: (i, 0))
    return pl.pallas_call(
        body, name="adamw_big", grid=(rows // tr,),
        in_specs=[pl.BlockSpec((N_CHIPS, tr, cols), lambda i: (0, i, 0)), blk, blk, blk],
        out_specs=[blk] * 4, out_shape=[jax.ShapeDtypeStruct((rows, cols), F32)] * 4,
        compiler_params=_params("parallel"),
    )(partials, w, m, v)


def _sum_devices(gathered, rows):
    cols = gathered.shape[1]

    def body(g_ref, o_ref):
        acc = g_ref[0:rows]
        for d in range(1, N_DEV):
            acc = acc + g_ref[d * rows:(d + 1) * rows]
        o_ref[...] = acc

    return pl.pallas_call(
        body, name="sum_devices", out_shape=jax.ShapeDtypeStruct((rows, cols), F32),
        in_specs=[pl.BlockSpec(memory_space=pltpu.VMEM)], out_specs=pl.BlockSpec(memory_space=pltpu.VMEM),
        compiler_params=pltpu.CompilerParams(vmem_limit_bytes=VMEM_LIMIT_V7X),
    )(gathered)


def _adamw_small(quads):
    n = len(quads)

    def body(*refs):
        ins, outs = refs[:4 * n], refs[4 * n:]
        for t in range(n):
            w, g, m, v = (r[...] for r in ins[4 * t:4 * t + 4])
            outs[3 * t][...], outs[3 * t + 1][...], outs[3 * t + 2][...] = _adamw(w, g, m, v)

    flat = [a for q in quads for a in q]
    vm = pl.BlockSpec(memory_space=pltpu.VMEM)
    res = pl.pallas_call(
        body, name="adamw_small",
        out_shape=[jax.ShapeDtypeStruct(q[0].shape, F32) for q in quads for _ in range(3)],
        in_specs=[vm] * (4 * n), out_specs=[vm] * (3 * n),
    )(*flat)
    return [tuple(res[3 * t:3 * t + 3]) for t in range(n)]


SMALL_PACK_ROWS = 96
_WEIGHTS = ['meta_tokens', 'g_pre_mix', 'w_in', 'conv_w', 'conv_b', 'w_a', 'b_a', 'w_x', 'b_x', 'lru_lambda',
            'attn_sinks', 'w_out', 'g_post_mix', 'g_pre_ffn', 'w_ff1', 'w_ff2', 'g_post_ffn']
_BIG = ['w_in', 'w_out', 'w_ff1', 'w_ff2']


def _pack_small(dmeta, g):
    z = lambda r, c: jnp.zeros((r, c), F32)
    rows = [
        dmeta,
        g['g_pre_mix'], g['g_post_mix'], g['g_pre_ffn'], g['g_post_ffn'],
        jnp.concatenate([g['conv_w'], z(4, 512)], axis=1),
        jnp.concatenate([g['conv_b'], g['b_a']], axis=1),
        jnp.concatenate([g['b_x'], g['lru_lambda']], axis=1),
        jnp.concatenate([g['attn_sinks'], z(1, D_MODEL - ATTN_HEADS)], axis=1),
        z(5, D_MODEL),
        g['w_a'].reshape(32, D_MODEL), g['w_x'].reshape(32, D_MODEL),
    ]
    return jnp.concatenate(rows, axis=0)


def _unpack_small(s, chip):
    return dict(
        meta_tokens=lax.dynamic_slice(s[0:16], (0, chip * 256), (16, 256)),
        g_pre_mix=s[16:17], g_post_mix=s[17:18], g_pre_ffn=s[18:19], g_post_ffn=s[19:20],
        conv_w=lax.dynamic_slice(s[20:24], (0, chip * 128), (4, 128)).reshape(1, 4, 128),
        conv_b=s[24:25, :512], b_a=s[24:25, 512:], b_x=s[25:26, :512], lru_lambda=s[25:26, 512:],
        attn_sinks=s[26:27, :ATTN_HEADS],
        w_a=s[32:64].reshape(1, LRU_BLOCKS, LRU_BLOCK, LRU_BLOCK),
        w_x=s[64:96].reshape(1, LRU_BLOCKS, LRU_BLOCK, LRU_BLOCK))


def _as2d(a):
    if a.ndim == 2:
        return a
    return a.reshape(-1, a.shape[-1])


def kernel(x, meta_tokens, g_pre_mix, w_in, conv_w, conv_b, w_a, b_a, w_x, b_x, lru_lambda, attn_sinks, w_out, g_post_mix, g_pre_ffn, w_ff1, w_ff2, g_post_ffn, loss_target, m_meta_tokens, m_g_pre_mix, m_w_in, m_conv_w, m_conv_b, m_w_a, m_b_a, m_w_x, m_b_x, m_lru_lambda, m_attn_sinks, m_w_out, m_g_post_mix, m_g_pre_ffn, m_w_ff1, m_w_ff2, m_g_post_ffn, v_meta_tokens, v_g_pre_mix, v_w_in, v_conv_w, v_conv_b, v_w_a, v_b_a, v_w_x, v_b_x, v_lru_lambda, v_attn_sinks, v_w_out, v_g_post_mix, v_g_pre_ffn, v_w_ff1, v_w_ff2, v_g_post_ffn):
    weights = dict(meta_tokens=meta_tokens, g_pre_mix=g_pre_mix, w_in=w_in, conv_w=conv_w, conv_b=conv_b, w_a=w_a,
                   b_a=b_a, w_x=w_x, b_x=b_x, lru_lambda=lru_lambda, attn_sinks=attn_sinks, w_out=w_out,
                   g_post_mix=g_post_mix, g_pre_ffn=g_pre_ffn, w_ff1=w_ff1, w_ff2=w_ff2, g_post_ffn=g_post_ffn)
    mom1 = dict(zip(_WEIGHTS, [m_meta_tokens, m_g_pre_mix, m_w_in, m_conv_w, m_conv_b, m_w_a, m_b_a, m_w_x, m_b_x,
                               m_lru_lambda, m_attn_sinks, m_w_out, m_g_post_mix, m_g_pre_ffn, m_w_ff1, m_w_ff2,
                               m_g_post_ffn]))
    mom2 = dict(zip(_WEIGHTS, [v_meta_tokens, v_g_pre_mix, v_w_in, v_conv_w, v_conv_b, v_w_a, v_b_a, v_w_x, v_b_x,
                               v_lru_lambda, v_attn_sinks, v_w_out, v_g_post_mix, v_g_pre_ffn, v_w_ff1, v_w_ff2,
                               v_g_post_ffn]))
    xi, yi, ci = _mesh_pos()
    chip = 2 * xi + yi

    tiny = jnp.concatenate([meta_tokens, jnp.pad(conv_w[0], ((0, 4), (0, 128)))], axis=0)
    chip_arr = jnp.reshape(chip, (1,)).astype(jnp.int32)
    shards, lands = zip(*[_prep_shard(w[0], chip_arr) for w in (w_in, w_out, w_ff1, w_ff2)])
    g_in, g_tiny = _gather_weights(shards[:1], lands[:1], tiny, _prep_tiny(tiny, chip_arr))
    w_in_full = jnp.concatenate([g_in[j] for j in range(N_CHIPS)], axis=1)
    meta_full = jnp.concatenate([g_tiny[j, :N_META] for j in range(N_CHIPS)], axis=1)
    conv_w_full = jnp.concatenate([g_tiny[j, N_META:N_META + 4, :128] for j in range(N_CHIPS)], axis=1)
    g_send, g_recv, late_thru, late_lands, token = _split_start(
        "gather_late_start", _gather_copies, shards[1:], lands[1:])

    def late_weights(after):
        _, landed = _split_wait("gather_late_wait", _gather_copies, g_send, g_recv, late_thru, late_lands, after)
        g_out, g_f1, g_f2 = _gather_finish(landed)
        return g_out.reshape(D_MODEL, D_MODEL), g_f1, g_f2

    pos = jnp.stack([ci, chip]).astype(jnp.int32)
    ffn = {}

    def chip_sums(parts):
        return zip(*[_chip_presum(p, r, pos) for p, r in zip(parts, _sibling_exchange(parts))])

    def on_ffn_grads(dw1, dw2):
        cparts_ffn, lands_ffn = chip_sums([dw1, dw2])
        ffn['send'], ffn['recv'], ffn['thru'], ffn['lands'], token2 = _split_start(
            "scatter_ffn_start", _scatter_copies, cparts_ffn, lands_ffn)
        return token2

    head = jnp.concatenate([jnp.zeros((PAD_ROWS, D_MODEL), F32), meta_full], axis=0)
    loss, dx, dhead, grads = _local_step(head, x[0], loss_target[0], g_pre_mix, w_in_full, conv_w_full, conv_b, w_a[0],
                                         b_a, w_x[0], b_x, lru_lambda, attn_sinks, g_post_mix, g_pre_ffn, g_post_ffn,
                                         late_weights, on_ffn_grads, token)
    loss = lax.psum(loss[0, 0], ("x", "y", "c"))
    grad_x = dx[None]

    gathered = _gather_small(_pack_small(dhead[PAD_ROWS:], grads))
    small = _unpack_small(_sum_devices(gathered, SMALL_PACK_ROWS), chip)

    dw_in = grads['w_in']
    cparts, own_lands = chip_sums([jnp.stack([dw_in[:, j * 448:(j + 1) * 448] for j in range(N_CHIPS)]),
                                   grads['w_out'].reshape(N_CHIPS, D_MODEL // N_CHIPS, D_MODEL)])
    ffn_cparts, ffn_lands = _split_wait("scatter_ffn_wait", _scatter_copies, ffn['send'], ffn['recv'], ffn['thru'],
                                        ffn['lands'], cparts)
    chip_partials = _scatter_partials(cparts, own_lands, ffn_cparts, ffn_lands)

    g_out_d, delta, new_m, new_v = {}, {}, {}, {}
    for name, part in zip(_BIG, chip_partials):
        shp = weights[name].shape
        res = _adamw_big(part, weights[name][0], mom1[name][0], mom2[name][0])
        g_out_d[name], delta[name], new_m[name], new_v[name] = (r.reshape(shp) for r in res)
    small_names = [n for n in _WEIGHTS if n not in _BIG]
    quads = [(_as2d(weights[n]), _as2d(small[n]), _as2d(mom1[n]), _as2d(mom2[n])) for n in small_names]
    for name, (d, m2, v2) in zip(small_names, _adamw_small(quads)):
        shp = weights[name].shape
        g_out_d[name] = small[name].reshape(shp)
        delta[name], new_m[name], new_v[name] = d.reshape(shp), m2.reshape(shp), v2.reshape(shp)

    return (loss, grad_x, *[g_out_d[n] for n in _WEIGHTS], *[delta[n] for n in _WEIGHTS],
            *[new_m[n] for n in _WEIGHTS], *[new_v[n] for n in _WEIGHTS])
```

```python
import numpy as np
import jax
import jax.numpy as jnp
from jax import lax
from jax.experimental import pallas as pl
from jax.experimental.pallas import tpu as pltpu

F32 = jnp.float32
BF16 = jnp.bfloat16

D_MODEL = 1024
N_META = 16
BLOCK = 128
PAD_ROWS = BLOCK - N_META
HEAD_DIM = 64
ATTN_HEADS = 8
GQA_GROUP = 4
ATTN_WIDTH = 512
KV_WIDTH = 128
QKV_WIDTH = ATTN_WIDTH + 2 * KV_WIDTH
LRU_WIDTH = 512
LRU_BLOCKS = 8
LRU_BLOCK = 64
LRU_C = 8.0
IN_WIDTH = 1792
D_FF = 4096
N_CHIPS = 4
FF_CHUNK = D_FF // N_CHIPS
EPS = 1e-6
NEG = -1e30

ADAM_LR = 0.001
ADAM_B1 = 0.9
ADAM_B2 = 0.999
ADAM_EPS = 1e-08
ADAM_WD = 0.01
ADAM_STEP = 10

VMEM_LIMIT_V7X = 56 * 1024 * 1024
MESH = pl.DeviceIdType.MESH

NT = (((1,), (1,)), ((), ()))
TN = (((0,), (0,)), ((), ()))


def _row_tile(tp):
    return 640 if tp % 640 == 0 else BLOCK


def _wgrad_row_tile(tp):
    return 1664 if tp % 1664 == 0 else _row_tile(tp)


def _params(*sem):
    return pltpu.CompilerParams(dimension_semantics=sem, vmem_limit_bytes=VMEM_LIMIT_V7X)


def _dot(a, b):
    return jnp.dot(a, b, preferred_element_type=F32)


def _dot_nt(a, b):
    return lax.dot_general(a, b, NT, preferred_element_type=F32)


def _dot_tn(a, b):
    return lax.dot_general(a, b, TN, preferred_element_type=F32)


def _rms(x):
    rs = lax.rsqrt(jnp.mean(x * x, axis=-1, keepdims=True) + EPS)
    return x * rs, rs


def _rms_bwd(xhat, rs, g, dy):
    dyg = dy * g
    dx = rs * (dyg - xhat * jnp.mean(dyg * xhat, axis=-1, keepdims=True))
    dg = jnp.sum(dy * xhat, axis=0, keepdims=True)
    return dx, dg


def _gelu(x):
    k = 0.7978845608028654
    t = jnp.tanh(k * (x + 0.044715 * x * x * x))
    return 0.5 * x * (1.0 + t), t


def _gelu_grad(x, t):
    k = 0.7978845608028654
    return 0.5 * (1.0 + t) + 0.5 * x * (1.0 - t * t) * k * (1.0 + 3 * 0.044715 * x * x)


def _sigmoid(x):
    return 0.5 * jnp.tanh(0.5 * x) + 0.5


def _neg_expm1(x):
    series = x * (1.0 + x * 0.5 * (1.0 + x * (1.0 / 3.0) * (1.0 + x * 0.25 * (1.0 + x * 0.2))))
    return -jnp.where(jnp.abs(x) < 0.05, series, jnp.exp(x) - 1.0)


def _softplus(x):
    return jnp.maximum(x, 0.0) + jnp.log1p(jnp.exp(-jnp.abs(x)))


def _seq_specs(tr):
    qb = tr // BLOCK
    return [pl.BlockSpec((BLOCK, D_MODEL), lambda i, *_, s=s: (jnp.maximum(i * qb + s - 1, 0), 0)) for s in range(qb)]


def _seq_tile(head, pieces, i):
    first = jnp.where(i == 0, head, pieces[0][...])
    return jnp.concatenate([first] + [p[...] for p in pieces[1:]], axis=0)


def _inproj_fwd(head, x, g, w_in, token):
    tp = BLOCK + x.shape[0]
    tr = _row_tile(tp)
    qb = tr // BLOCK

    def body(*refs):
        head_ref, pieces = refs[0], refs[1:1 + qb]
        g_ref, w_ref, _, u_ref, qkv_ref, xr_ref, yr_ref = refs[1 + qb:]
        xhat, _ = _rms(_seq_tile(head_ref[...], pieces, pl.program_id(0)))
        u = (xhat * g_ref[...]).astype(BF16)
        u_ref[...] = u
        z = _dot(u, w_ref[...])
        qkv_ref[...] = z[:, :QKV_WIDTH].astype(BF16)
        xr_ref[...] = z[:, QKV_WIDTH:QKV_WIDTH + LRU_WIDTH]
        yr_ref[...] = z[:, QKV_WIDTH + LRU_WIDTH:]

    row = lambda w: pl.BlockSpec((tr, w), lambda i: (i, 0))
    full = lambda a: pl.BlockSpec(a.shape, lambda i: (0,) * a.ndim)
    return pl.pallas_call(
        body, name="inproj_fwd", grid=(tp // tr,),
        in_specs=[full(head)] + _seq_specs(tr) + [full(g), full(w_in), full(token)],
        out_specs=[row(D_MODEL), row(QKV_WIDTH), row(LRU_WIDTH), row(LRU_WIDTH)],
        out_shape=[jax.ShapeDtypeStruct((tp, D_MODEL), BF16), jax.ShapeDtypeStruct((tp, QKV_WIDTH), BF16),
                   jax.ShapeDtypeStruct((tp, LRU_WIDTH), F32), jax.ShapeDtypeStruct((tp, LRU_WIDTH), F32)],
        compiler_params=_params("parallel"),
    )(head, *([x] * qb), g, w_in, token)


GROUP_ROWS = GQA_GROUP * BLOCK


def _attn_bias():
    j = np.arange(2 * BLOCK)[:, None]
    i = np.arange(BLOCK)[None, :]
    band = (j - i >= 1) & (j - i <= BLOCK)
    out = []
    for n in range(3):
        ok = band & ((n - 1) * BLOCK + j >= PAD_ROWS) if n < 2 else band
        out.append(np.tile(np.where(ok, 0.0, NEG).astype(np.float32), (1, GQA_GROUP)))
    return jnp.asarray(np.stack(out))


def _stack_heads(a, g):
    heads = range(GQA_GROUP * g, GQA_GROUP * (g + 1))
    return jnp.concatenate([a[:, h * HEAD_DIM:(h + 1) * HEAD_DIM] for h in heads], axis=0)


def _unstack_heads(groups):
    return jnp.concatenate([p[h * BLOCK:(h + 1) * BLOCK] for p in groups for h in range(GQA_GROUP)], axis=1)


def _attn_probs_t(k_g, qg, bias, sink_row):
    st = _dot_nt(k_g, qg) + bias
    m = jnp.maximum(jnp.max(st, axis=0, keepdims=True), sink_row)
    p = jnp.exp(st - m)
    es = jnp.exp(sink_row - m)
    inv = 1.0 / (jnp.sum(p, axis=0, keepdims=True) + es)
    return p * inv, es * inv


def _attn_consts(sinks):
    return jnp.repeat(sinks.reshape(ATTN_HEADS), BLOCK).reshape(ATTN_HEADS // GQA_GROUP, GROUP_ROWS), _attn_bias()


_SINK_SPEC = pl.BlockSpec((ATTN_HEADS // GQA_GROUP, GROUP_ROWS), lambda n: (0, 0))
_BIAS_SPEC = pl.BlockSpec((3, 2 * BLOCK, GROUP_ROWS), lambda n: (0, 0, 0))
_QSCALE = HEAD_DIM ** -0.5


def _kv_specs(tr):
    qb = tr // BLOCK
    prev = lambda col: pl.BlockSpec((BLOCK, KV_WIDTH), lambda t: (jnp.maximum(t * qb - 1, 0), col))
    cur = lambda col: pl.BlockSpec((tr, KV_WIDTH), lambda t: (t, col))
    return [prev(4), cur(4), prev(5), cur(5)]


def _block_bias(b_ref, t, qb, i):
    return b_ref[2] if i >= 2 else b_ref[jnp.minimum(t * qb + i, 2)]


def _attn_fwd(qkv, sinks):
    tp = qkv.shape[0]
    tr = _row_tile(tp)
    qb = tr // BLOCK
    sink_rows, bias = _attn_consts(sinks)

    def body(s_ref, b_ref, q_ref, kp_ref, kc_ref, vp_ref, vc_ref, o_ref):
        t = pl.program_id(0)
        k_all = jnp.concatenate([kp_ref[...], kc_ref[...]], axis=0)
        v_all = jnp.concatenate([vp_ref[...], vc_ref[...]], axis=0)
        for i in range(qb):
            rows = slice(i * BLOCK, (i + 1) * BLOCK)
            q = q_ref[rows]
            k2, v2 = k_all[i * BLOCK:(i + 2) * BLOCK], v_all[i * BLOCK:(i + 2) * BLOCK]
            bias_n = _block_bias(b_ref, t, qb, i)
            outs = []
            for g in range(ATTN_HEADS // GQA_GROUP):
                cols = slice(g * HEAD_DIM, (g + 1) * HEAD_DIM)
                qg = _stack_heads(q, g) * jnp.asarray(_QSCALE, BF16)
                p, _ = _attn_probs_t(k2[:, cols], qg, bias_n, s_ref[g:g + 1])
                outs.append(_dot_tn(p.astype(BF16), v2[:, cols]))
            o_ref[rows] = _unstack_heads(outs).astype(BF16)

    return pl.pallas_call(
        body, name="attn_fwd", grid=(tp // tr,),
        in_specs=[_SINK_SPEC, _BIAS_SPEC, pl.BlockSpec((tr, ATTN_WIDTH), lambda t: (t, 0))] + _kv_specs(tr),
        out_specs=pl.BlockSpec((tr, ATTN_WIDTH), lambda t: (t, 0)),
        out_shape=jax.ShapeDtypeStruct((tp, ATTN_WIDTH), BF16),
        compiler_params=_params("parallel"),
    )(sink_rows, bias, qkv, qkv, qkv, qkv, qkv)


def _conv_taps(x, halo):
    ext = jnp.concatenate([halo, x], axis=0)
    return [ext[8:] if k == 3 else pltpu.roll(ext, 3 - k, 0)[8:] for k in range(4)]


def _lru_gates(xc, wa, ba, wx, bx, sp):
    xb = xc.astype(BF16)
    r = _sigmoid(_dot(xb, wa) + ba)
    ig = _sigmoid(_dot(xb, wx) + bx)
    log_a = (-LRU_C * sp) * r
    a = jnp.exp(log_a)
    mult = jnp.sqrt(_neg_expm1(2.0 * log_a))
    return xb, r, ig, a, mult


SUBLANES = 8


def _scan_fwd(a, b, h_in):
    n, width = a.shape
    a, b = (v.reshape(n // SUBLANES, SUBLANES, width) for v in (a, b))
    in_group = lax.broadcasted_iota(jnp.int32, a.shape, 1)
    for d in (1, 2, 4):
        keep = in_group >= d
        b = jnp.where(keep, a * pltpu.roll(b, d, 1) + b, b)
        a = jnp.where(keep, a * pltpu.roll(a, d, 1), a)
    a, b = a.reshape(n, width), b.reshape(n, width)
    out, carry = [], h_in
    for g in range(0, n, SUBLANES):
        h = a[g:g + SUBLANES] * carry + b[g:g + SUBLANES]
        out.append(h)
        carry = h[SUBLANES - 1:]
    return jnp.concatenate(out, axis=0)


def _scan_rev(c, b, g_in):
    n, width = c.shape
    c, b = (v.reshape(n // SUBLANES, SUBLANES, width) for v in (c, b))
    in_group = lax.broadcasted_iota(jnp.int32, c.shape, 1)
    for d in (1, 2, 4):
        keep = in_group < SUBLANES - d
        b = jnp.where(keep, b + c * pltpu.roll(b, SUBLANES - d, 1), b)
        c = jnp.where(keep, c * pltpu.roll(c, SUBLANES - d, 1), c)
    c, b = c.reshape(n, width), b.reshape(n, width)
    out, carry = [], g_in
    for g in range(n - SUBLANES, -1, -SUBLANES):
        r = b[g:g + SUBLANES] + c[g:g + SUBLANES] * carry
        out.append(r)
        carry = r[:1]
    return jnp.concatenate(out[::-1], axis=0)


def _lru_fwd(xr, yr, conv_w, conv_b, wa, ba, wx, bx, lam):
    tp = xr.shape[0]
    tr = _row_tile(tp)
    qb = tr // BLOCK

    def body(xr_ref, yr_ref, cw_ref, cb_ref, wa_ref, ba_ref, wx_ref, bx_ref, lam_ref, hr_ref, rec_ref, halo, hprev):
        t = pl.program_id(0)

        @pl.when(t == 0)
        def _():
            halo[...] = jnp.zeros_like(halo)
            hprev[...] = jnp.zeros_like(hprev)

        cw, cb = cw_ref[...], cb_ref[...]
        wa_m, ba_v, wx_m, bx_v = wa_ref[...], ba_ref[...], wx_ref[...], bx_ref[...]
        sp = _softplus(-lam_ref[...])
        before, h_last = halo[...], hprev[0:1]
        for i in range(qb):
            rows = slice(i * BLOCK, (i + 1) * BLOCK)
            x = xr_ref[rows]
            taps = _conv_taps(x, before)
            before = x[BLOCK - 8:]
            xc = cb + sum(cw[k:k + 1] * taps[k] for k in range(4))
            _, _, ig, a, mult = _lru_gates(xc, wa_m, ba_v, wx_m, bx_v, sp)
            u = mult * (ig * xc)
            if i == 0:
                pos = t * tr + lax.broadcasted_iota(jnp.int32, xc.shape, 0)
                u = jnp.where(pos >= PAD_ROWS, u, 0.0)
            h = _scan_fwd(a, u, h_last)
            h_last = h[BLOCK - 1:]
            hr_ref[rows] = h
            gl, _ = _gelu(yr_ref[rows])
            rec_ref[rows] = (gl * h).astype(BF16)
        halo[...] = before
        hprev[0:1] = h_last

    blk = pl.BlockSpec((tr, LRU_WIDTH), lambda t: (t, 0))
    full = lambda a: pl.BlockSpec(a.shape, lambda t: (0,) * a.ndim)
    small = [conv_w, conv_b, wa, ba, wx, bx, lam]
    return pl.pallas_call(
        body, name="lru_fwd", grid=(tp // tr,),
        in_specs=[blk, blk] + [full(a) for a in small],
        out_specs=[blk, blk],
        out_shape=[jax.ShapeDtypeStruct((tp, LRU_WIDTH), F32), jax.ShapeDtypeStruct((tp, LRU_WIDTH), BF16)],
        scratch_shapes=[pltpu.VMEM((8, LRU_WIDTH), F32), pltpu.VMEM((8, LRU_WIDTH), F32)],
        compiler_params=_params("arbitrary"),
    )(xr, yr, *small)


def _outproj_fwd(attn, rec, w_out, head, x, g_post_mix, g_pre_ffn):
    tp = attn.shape[0]
    tr = _row_tile(tp)
    qb = tr // BLOCK

    def body(*refs):
        a_ref, r_ref, w_ref, head_ref = refs[:4]
        pieces = refs[4:4 + qb]
        gm_ref, gf_ref, mix_ref, h1_ref, u1_ref = refs[4 + qb:]
        mix = _dot(a_ref[...], w_ref[:ATTN_WIDTH]) + _dot(r_ref[...], w_ref[ATTN_WIDTH:])
        mix_ref[...] = mix
        mhat, _ = _rms(mix)
        h1 = _seq_tile(head_ref[...], pieces, pl.program_id(0)) + mhat * gm_ref[...]
        h1_ref[...] = h1
        hhat, _ = _rms(h1)
        u1_ref[...] = (hhat * gf_ref[...]).astype(BF16)

    row = lambda w: pl.BlockSpec((tr, w), lambda i: (i, 0))
    full = lambda a: pl.BlockSpec(a.shape, lambda i: (0,) * a.ndim)
    return pl.pallas_call(
        body, name="outproj_fwd", grid=(tp // tr,),
        in_specs=[row(ATTN_WIDTH), row(LRU_WIDTH), full(w_out), full(head)] + _seq_specs(tr)
        + [full(g_post_mix), full(g_pre_ffn)],
        out_specs=[row(D_MODEL), row(D_MODEL), row(D_MODEL)],
        out_shape=[jax.ShapeDtypeStruct((tp, D_MODEL), F32), jax.ShapeDtypeStruct((tp, D_MODEL), F32),
                   jax.ShapeDtypeStruct((tp, D_MODEL), BF16)],
        compiler_params=_params("parallel"),
    )(attn, rec, w_out, head, *([x] * qb), g_post_mix, g_pre_ffn)


def _ffn_fwd(u1, w1, w2, h1, tgt, g_post_ffn):
    tp = h1.shape[0]
    tr = _row_tile(tp)
    qb = tr // BLOCK

    def body(*refs):
        u_ref, w1_ref, w2_ref, h1_ref = refs[:4]
        t_pieces = refs[4:4 + qb]
        g_ref, r1_ref, dy_ref, df2_ref, loss_ref, dg_ref, acc = refs[4 + qb:]
        i, c = pl.program_id(0), pl.program_id(1)

        @pl.when((i == 0) & (c == 0))
        def _():
            loss_ref[...] = jnp.zeros_like(loss_ref)
            dg_ref[...] = jnp.zeros_like(dg_ref)

        r = jnp.maximum(_dot(u_ref[...], w1_ref[0]), 0.0)
        r1_ref[...] = r.astype(BF16)
        part = _dot((r * r).astype(BF16), w2_ref[0])

        @pl.when(c == 0)
        def _():
            acc[...] = part

        @pl.when(c > 0)
        def _():
            acc[...] += part

        @pl.when(c == N_CHIPS - 1)
        def _():
            g = g_ref[...]
            fhat, rs = _rms(acc[...])
            h2 = h1_ref[...] + fhat * g
            rows = i * tr + lax.broadcasted_iota(jnp.int32, h2.shape, 0)
            tgt_tile = jnp.concatenate([p[...] for p in t_pieces], axis=0)
            err = jnp.where(rows >= BLOCK, h2 - tgt_tile, 0.0)
            dy = err * (1.0 / D_MODEL)
            dy_ref[...] = dy
            loss_ref[...] += (0.5 / D_MODEL) * jnp.sum(err * err)
            df2, dg = _rms_bwd(fhat, rs, g, dy)
            df2_ref[...] = df2.astype(BF16)
            dg_ref[...] += dg

    row = pl.BlockSpec((tr, D_MODEL), lambda i, c: (i, 0))
    full = lambda a: pl.BlockSpec(a.shape, lambda i, c: (0,) * a.ndim)
    return pl.pallas_call(
        body, name="ffn_fwd", grid=(tp // tr, N_CHIPS),
        in_specs=[row, pl.BlockSpec((1, D_MODEL, FF_CHUNK), lambda i, c: (c, 0, 0)),
                  pl.BlockSpec((1, FF_CHUNK, D_MODEL), lambda i, c: (c, 0, 0)), row] + _seq_specs(tr)
        + [full(g_post_ffn)],
        out_specs=[pl.BlockSpec((tr, FF_CHUNK), lambda i, c: (i, c)), row, row,
                   pl.BlockSpec((1, 1), lambda i, c: (0, 0)), pl.BlockSpec((1, D_MODEL), lambda i, c: (0, 0))],
        out_shape=[jax.ShapeDtypeStruct((tp, D_FF), BF16), jax.ShapeDtypeStruct((tp, D_MODEL), F32),
                   jax.ShapeDtypeStruct((tp, D_MODEL), BF16), jax.ShapeDtypeStruct((1, 1), F32),
                   jax.ShapeDtypeStruct((1, D_MODEL), F32)],
        scratch_shapes=[pltpu.VMEM((tr, D_MODEL), F32)],
        compiler_params=_params("arbitrary", "arbitrary"),
    )(u1, w1, w2, h1, *([tgt] * qb), g_post_ffn)


def _ffn_bwd_data(df2, r1, w1, w2, dy, h1, mix, g_pre_ffn, g_post_mix):
    tp = h1.shape[0]
    tr = _row_tile(tp)

    def body(df2_ref, r1_ref, w1_ref, w2_ref, dy_ref, h1_ref, mix_ref, gf_ref, gm_ref,
             da_ref, dh1_ref, dmix_ref, dgf_ref, dgm_ref, acc):
        i, c = pl.program_id(0), pl.program_id(1)

        @pl.when((i == 0) & (c == 0))
        def _():
            dgf_ref[...] = jnp.zeros_like(dgf_ref)
            dgm_ref[...] = jnp.zeros_like(dgm_ref)

        df = _dot_nt(df2_ref[...], w2_ref[0])
        da = (df * (2.0 * r1_ref[...].astype(F32))).astype(BF16)
        da_ref[...] = da
        part = _dot_nt(da, w1_ref[0])

        @pl.when(c == 0)
        def _():
            acc[...] = part

        @pl.when(c > 0)
        def _():
            acc[...] += part

        @pl.when(c == N_CHIPS - 1)
        def _():
            hhat, rs = _rms(h1_ref[...])
            dx, dgf = _rms_bwd(hhat, rs, gf_ref[...], acc[...])
            dh1 = dy_ref[...] + dx
            dh1_ref[...] = dh1
            dgf_ref[...] += dgf
            mhat, rsm = _rms(mix_ref[...])
            dmix, dgm = _rms_bwd(mhat, rsm, gm_ref[...], dh1)
            dmix_ref[...] = dmix.astype(BF16)
            dgm_ref[...] += dgm

    row = pl.BlockSpec((tr, D_MODEL), lambda i, c: (i, 0))
    chunk = pl.BlockSpec((tr, FF_CHUNK), lambda i, c: (i, c))
    gain = pl.BlockSpec((1, D_MODEL), lambda i, c: (0, 0))
    return pl.pallas_call(
        body, name="ffn_bwd_data", grid=(tp // tr, N_CHIPS),
        in_specs=[row, chunk, pl.BlockSpec((1, D_MODEL, FF_CHUNK), lambda i, c: (c, 0, 0)),
                  pl.BlockSpec((1, FF_CHUNK, D_MODEL), lambda i, c: (c, 0, 0)), row, row, row, gain, gain],
        out_specs=[chunk, row, row, gain, gain],
        out_shape=[jax.ShapeDtypeStruct((tp, D_FF), BF16), jax.ShapeDtypeStruct((tp, D_MODEL), F32),
                   jax.ShapeDtypeStruct((tp, D_MODEL), BF16), jax.ShapeDtypeStruct((1, D_MODEL), F32),
                   jax.ShapeDtypeStruct((1, D_MODEL), F32)],
        scratch_shapes=[pltpu.VMEM((tr, D_MODEL), F32)],
        compiler_params=_params("arbitrary", "arbitrary"),
    )(df2, r1, w1, w2, dy, h1, mix, g_pre_ffn, g_post_mix)


def _ffn_bwd_weights(u1, da1, r1, df2):
    tp = u1.shape[0]
    tr = _wgrad_row_tile(tp)

    def body(u_ref, da_ref, r1_ref, df2_ref, dw1_ref, dw2_ref):
        i = pl.program_id(1)
        r = r1_ref[...].astype(F32)
        p1 = _dot_tn(u_ref[...], da_ref[...])
        p2 = _dot_tn((r * r).astype(BF16), df2_ref[...])

        @pl.when(i == 0)
        def _():
            dw1_ref[0] = p1
            dw2_ref[0] = p2

        @pl.when(i > 0)
        def _():
            dw1_ref[0] += p1
            dw2_ref[0] += p2

    row = pl.BlockSpec((tr, D_MODEL), lambda c, i: (i, 0))
    chunk = pl.BlockSpec((tr, FF_CHUNK), lambda c, i: (i, c))
    return pl.pallas_call(
        body, name="ffn_bwd_weights", grid=(N_CHIPS, tp // tr),
        in_specs=[row, chunk, chunk, row],
        out_specs=[pl.BlockSpec((1, D_MODEL, FF_CHUNK), lambda c, i: (c, 0, 0)),
                   pl.BlockSpec((1, FF_CHUNK, D_MODEL), lambda c, i: (c, 0, 0))],
        out_shape=[jax.ShapeDtypeStruct((N_CHIPS, D_MODEL, FF_CHUNK), F32),
                   jax.ShapeDtypeStruct((N_CHIPS, FF_CHUNK, D_MODEL), F32)],
        compiler_params=_params("parallel", "arbitrary"),
    )(u1, da1, r1, df2)


def _outproj_bwd(dmix, w_out, attn, rec, token):
    tp = dmix.shape[0]
    tr = _wgrad_row_tile(tp)

    def body(dm_ref, w_ref, a_ref, r_ref, _, da_ref, dr_ref, dw_ref):
        i = pl.program_id(0)
        dm = dm_ref[...]
        dcat = _dot_nt(dm, w_ref[...])
        da_ref[...] = dcat[:, :ATTN_WIDTH].astype(BF16)
        dr_ref[...] = dcat[:, ATTN_WIDTH:]
        pa = _dot_tn(a_ref[...], dm)
        pr = _dot_tn(r_ref[...], dm)

        @pl.when(i == 0)
        def _():
            dw_ref[:ATTN_WIDTH] = pa
            dw_ref[ATTN_WIDTH:] = pr

        @pl.when(i > 0)
        def _():
            dw_ref[:ATTN_WIDTH] += pa
            dw_ref[ATTN_WIDTH:] += pr

    row = lambda w: pl.BlockSpec((tr, w), lambda i: (i, 0))
    full = pl.BlockSpec((D_MODEL, D_MODEL), lambda i: (0, 0))
    return pl.pallas_call(
        body, name="outproj_bwd", grid=(tp // tr,),
        in_specs=[row(D_MODEL), full, row(ATTN_WIDTH), row(LRU_WIDTH), pl.BlockSpec(token.shape, lambda i: (0, 0))],
        out_specs=[row(ATTN_WIDTH), row(LRU_WIDTH), full],
        out_shape=[jax.ShapeDtypeStruct((tp, ATTN_WIDTH), BF16), jax.ShapeDtypeStruct((tp, LRU_WIDTH), F32),
                   jax.ShapeDtypeStruct((D_MODEL, D_MODEL), F32)],
        compiler_params=_params("arbitrary"),
    )(dmix, w_out, attn, rec, token)


N_VEC_ROWS = 8


def _lru_bwd(xr, yr, hr, drec, conv_w, conv_b, wa, ba, wx, bx, lam, token):
    tp = xr.shape[0]
    tr = _row_tile(tp)
    qb, nt = tr // BLOCK, tp // tr

    def body(xr_ref, xh_ref, yr_ref, hr_ref, hp_ref, dr_ref, cw_ref, cb_ref, wa_ref, ba_ref, wx_ref, bx_ref, lam_ref, _,
             dxr_ref, dyr_ref, dwa_ref, dwx_ref, vec_ref, g_next, a_next, dxc_next, dsp):
        s = pl.program_id(0)
        t = nt - 1 - s

        @pl.when(s == 0)
        def _():
            g_next[...] = jnp.zeros_like(g_next)
            a_next[...] = jnp.zeros_like(a_next)
            dxc_next[...] = jnp.zeros_like(dxc_next)
            dsp[...] = jnp.zeros_like(dsp)
            dwa_ref[...] = jnp.zeros_like(dwa_ref)
            dwx_ref[...] = jnp.zeros_like(dwx_ref)
            vec_ref[...] = jnp.zeros_like(vec_ref)

        first_tile = t == 0
        cw, cb = cw_ref[...], cb_ref[...]
        lam_v = lam_ref[...]
        sp = _softplus(-lam_v)
        wa_m, ba_v, wx_m, bx_v = wa_ref[...], ba_ref[...], wx_ref[...], bx_ref[...]
        rows = lax.broadcasted_iota(jnp.int32, (BLOCK, LRU_WIDTH), 0)
        col = lambda v: jnp.sum(v, axis=0, keepdims=True)

        g_after, a_after, dxc_after = g_next[0:1], a_next[0:1], dxc_next[...]
        xbs, dgrs, dgis = [], [], []
        vec = [jnp.zeros((1, LRU_WIDTH), F32) for _ in range(N_VEC_ROWS)]
        for i in reversed(range(qb)):
            blk = slice(i * BLOCK, (i + 1) * BLOCK)
            if i == 0:
                x_before = jnp.where(first_tile, 0.0, xh_ref[...])
                h_before = jnp.where(first_tile, 0.0, hp_ref[7:8])
            else:
                x_before = xr_ref[i * BLOCK - 8:i * BLOCK]
                h_before = hr_ref[i * BLOCK - 1:i * BLOCK]
            taps = _conv_taps(xr_ref[blk], x_before)
            xc = cb + sum(cw[k:k + 1] * taps[k] for k in range(4))
            xb, r, ig, a, mult = _lru_gates(xc, wa_m, ba_v, wx_m, bx_v, sp)

            yr_v = yr_ref[blk]
            gl, th = _gelu(yr_v)
            h = hr_ref[blk]
            drec = dr_ref[blk]
            dyr_ref[blk] = (drec * h * _gelu_grad(yr_v, th)).astype(BF16)

            a_up = jnp.where(rows == BLOCK - 1, a_after, pltpu.roll(a, BLOCK - 1, 0))
            g = _scan_rev(a_up, drec * gl, g_after)
            g_after, a_after = g[0:1], a[0:1]

            h_prev = jnp.where(rows == 0, h_before, pltpu.roll(h, 1, 0))
            du, da = g, g * h_prev
            if i == 0:
                real = (t * tr + rows) >= PAD_ROWS
                du, da = jnp.where(real, du, 0.0), jnp.where(real, da, 0.0)
            dmult = du * (ig * xc)
            dig = du * (mult * xc)
            dxc = du * (mult * ig)
            dlog_a = da * a - dmult * (a * a / mult)
            if i == 0:
                dlog_a = jnp.where(real, dlog_a, 0.0)
            dgr = (dlog_a * (-LRU_C * sp)) * (r * (1.0 - r))
            dgi = dig * (ig * (1.0 - ig))
            dgr_b, dgi_b = dgr.astype(BF16), dgi.astype(BF16)
            dxc = dxc + _dot_nt(dgr_b, wa_m) + _dot_nt(dgi_b, wx_m)
            xbs.append(xb)
            dgrs.append(dgr_b)
            dgis.append(dgi_b)

            ext = jnp.concatenate([dxc, dxc_after], axis=0)
            up = [ext[:BLOCK] if j == 0 else pltpu.roll(ext, BLOCK + 8 - j, 0)[:BLOCK] for j in range(4)]
            dxr_ref[blk] = sum(cw[k:k + 1] * up[3 - k] for k in range(4)).astype(BF16)
            dxc_after = dxc[:8]

            for k in range(4):
                vec[k] = vec[k] + col(dxc * taps[k])
            vec[4] = vec[4] + col(dxc)
            vec[5] = vec[5] + col(dgr)
            vec[6] = vec[6] + col(dgi)
            vec[7] = vec[7] + col(dlog_a * (-LRU_C * r))

        g_next[0:1], a_next[0:1], dxc_next[...] = g_after, a_after, dxc_after
        xb_all = jnp.concatenate(xbs, axis=0)
        dwa_ref[...] += _dot_tn(xb_all, jnp.concatenate(dgrs, axis=0))
        dwx_ref[...] += _dot_tn(xb_all, jnp.concatenate(dgis, axis=0))
        for k in range(7):
            vec_ref[k:k + 1] += vec[k]
        dsp[0:1] += vec[7]

        @pl.when(s == nt - 1)
        def _():
            vec_ref[7:8] = dsp[0:1] * (-_sigmoid(-lam_v))

    blk_spec = pl.BlockSpec((tr, LRU_WIDTH), lambda s: (nt - 1 - s, 0))
    rows_before = pl.BlockSpec((8, LRU_WIDTH), lambda s: (jnp.maximum((nt - 1 - s) * (tr // 8) - 1, 0), 0))
    full = lambda a: pl.BlockSpec(a.shape, lambda s: (0,) * a.ndim)
    small = [conv_w, conv_b, wa, ba, wx, bx, lam, token]
    sq = pl.BlockSpec((LRU_WIDTH, LRU_WIDTH), lambda s: (0, 0))
    return pl.pallas_call(
        body, name="lru_bwd", grid=(nt,),
        in_specs=[blk_spec, rows_before, blk_spec, blk_spec, rows_before, blk_spec] + [full(a) for a in small],
        out_specs=[blk_spec, blk_spec, sq, sq, pl.BlockSpec((N_VEC_ROWS, LRU_WIDTH), lambda s: (0, 0))],
        out_shape=[jax.ShapeDtypeStruct((tp, LRU_WIDTH), BF16), jax.ShapeDtypeStruct((tp, LRU_WIDTH), BF16),
                   jax.ShapeDtypeStruct((LRU_WIDTH, LRU_WIDTH), F32), jax.ShapeDtypeStruct((LRU_WIDTH, LRU_WIDTH), F32),
                   jax.ShapeDtypeStruct((N_VEC_ROWS, LRU_WIDTH), F32)],
        scratch_shapes=[pltpu.VMEM((8, LRU_WIDTH), F32)] * 4,
        compiler_params=_params("arbitrary"),
    )(xr, xr, yr, hr, hr, drec, *small)


def _attn_bwd(qkv, dattn, sinks):
    tp = qkv.shape[0]
    tr = _row_tile(tp)
    qb, nt = tr // BLOCK, tp // tr
    n_groups = ATTN_HEADS // GQA_GROUP
    sink_rows, bias = _attn_consts(sinks)

    def body(s_ref, b_ref, q_ref, kp_ref, kc_ref, vp_ref, vc_ref, do_ref, dq_ref, dkv_ref, ex_ref, ds_ref, dsink):
        t = pl.program_id(0)

        @pl.when(t == 0)
        def _():
            dsink[...] = jnp.zeros_like(dsink)

        k_all = jnp.concatenate([kp_ref[...], kc_ref[...]], axis=0)
        v_all = jnp.concatenate([vp_ref[...], vc_ref[...]], axis=0)
        tail = None
        for i in range(qb):
            rows = slice(i * BLOCK, (i + 1) * BLOCK)
            q, do = q_ref[rows], do_ref[rows]
            k2, v2 = k_all[i * BLOCK:(i + 2) * BLOCK], v_all[i * BLOCK:(i + 2) * BLOCK]
            bias_n = _block_bias(b_ref, t, qb, i)
            dqs, dks, dvs = [], [], []
            for g in range(n_groups):
                cols = slice(g * HEAD_DIM, (g + 1) * HEAD_DIM)
                k_g, v_g = k2[:, cols], v2[:, cols]
                qg = _stack_heads(q, g) * jnp.asarray(_QSCALE, BF16)
                dog = _stack_heads(do, g)
                p, ps = _attn_probs_t(k_g, qg, bias_n, s_ref[g:g + 1])
                dpt = _dot_nt(v_g, dog)
                delta = jnp.sum(p * dpt, axis=0, keepdims=True)
                dst = (p * (dpt - delta)).astype(BF16)
                dqs.append(_dot_tn(dst, k_g) * _QSCALE)
                dks.append(_dot(dst, qg))
                dvs.append(_dot(p.astype(BF16), dog))
                dsink[g:g + 1] -= ps * delta
            dq_ref[rows] = _unstack_heads(dqs).astype(BF16)
            dkv = jnp.concatenate(dks + dvs, axis=1)
            if i == 0:
                ex_ref[0] = dkv[:BLOCK]
            else:
                dkv_ref[(i - 1) * BLOCK:i * BLOCK] = (tail + dkv[:BLOCK]).astype(BF16)
            tail = dkv[BLOCK:]
        dkv_ref[(qb - 1) * BLOCK:] = tail.astype(BF16)

        @pl.when(t == nt - 1)
        def _():
            lane = lax.broadcasted_iota(jnp.int32, (1, ATTN_HEADS), 1)
            acc = jnp.zeros((1, ATTN_HEADS), F32)
            for h in range(ATTN_HEADS):
                g, hh = divmod(h, GQA_GROUP)
                acc = acc + jnp.where(lane == h, jnp.sum(dsink[g:g + 1, hh * BLOCK:(hh + 1) * BLOCK]), 0.0)
            ds_ref[...] = acc

    cur = lambda w: pl.BlockSpec((tr, w), lambda t: (t, 0))
    return pl.pallas_call(
        body, name="attn_bwd", grid=(nt,),
        in_specs=[_SINK_SPEC, _BIAS_SPEC, cur(ATTN_WIDTH)] + _kv_specs(tr) + [cur(ATTN_WIDTH)],
        out_specs=[cur(ATTN_WIDTH), cur(2 * KV_WIDTH), pl.BlockSpec((1, BLOCK, 2 * KV_WIDTH), lambda t: (t, 0, 0)),
                   pl.BlockSpec((1, ATTN_HEADS), lambda t: (0, 0))],
        out_shape=[jax.ShapeDtypeStruct((tp, ATTN_WIDTH), BF16), jax.ShapeDtypeStruct((tp, 2 * KV_WIDTH), BF16),
                   jax.ShapeDtypeStruct((nt, BLOCK, 2 * KV_WIDTH), F32), jax.ShapeDtypeStruct((1, ATTN_HEADS), F32)],
        scratch_shapes=[pltpu.VMEM((n_groups, GROUP_ROWS), F32)],
        compiler_params=_params("arbitrary"),
    )(sink_rows, bias, qkv, qkv, qkv, qkv, qkv, dattn)


def _inproj_bwd(dq, dkv, dkv_extra, dxr, dyr, w_in, u0, head, x, dh1, g):
    tp = dq.shape[0]
    tr = _row_tile(tp)
    nt, qb = tp // tr, tr // BLOCK

    def body(*refs):
        dq_ref, dkv_ref, ex_ref, dxr_ref, dyr_ref, w_ref, u_ref, head_ref = refs[:8]
        pieces = refs[8:8 + qb]
        dh1_ref, g_ref, gx_ref, dhead_ref, dw_ref, dg_ref, buf, sems = refs[8 + qb:]
        i = pl.program_id(0)
        slot = i % 2

        def out_copy(step, at):
            return pltpu.make_async_copy(buf.at[at], gx_ref.at[pl.ds(step * tr - BLOCK, tr)], sems.at[at])

        extra = jnp.where(i < nt - 1, ex_ref[0], 0.0)
        last = (dkv_ref[tr - BLOCK:].astype(F32) + extra).astype(BF16)
        dkv = last if tr == BLOCK else jnp.concatenate([dkv_ref[:tr - BLOCK], last], axis=0)
        dz = jnp.concatenate([dq_ref[...], dkv, dxr_ref[...], dyr_ref[...]], axis=1)
        du = _dot_nt(dz, w_ref[...])
        hhat, rs = _rms(_seq_tile(head_ref[...], pieces, i))
        dx, dg = _rms_bwd(hhat, rs, g_ref[...], du)
        dh0 = dh1_ref[...] + dx

        @pl.when(i >= 3)
        def _():
            out_copy(i - 2, slot).wait()

        buf[slot] = dh0

        @pl.when(i == 0)
        def _():
            dhead_ref[...] = dh0[:BLOCK]
            if tr > BLOCK:
                first = pltpu.make_async_copy(buf.at[0, pl.ds(BLOCK, tr - BLOCK)], gx_ref.at[pl.ds(0, tr - BLOCK)],
                                              sems.at[0])
                first.start()
                first.wait()

        @pl.when(i >= 1)
        def _():
            out_copy(i, slot).start()

        @pl.when(i == nt - 1)
        def _():
            if nt >= 3:
                out_copy(nt - 2, (nt - 2) % 2).wait()
            if nt >= 2:
                out_copy(nt - 1, (nt - 1) % 2).wait()

        pw = _dot_tn(u_ref[...], dz)

        @pl.when(i == 0)
        def _():
            dw_ref[...] = pw
            dg_ref[...] = dg

        @pl.when(i > 0)
        def _():
            dw_ref[...] += pw
            dg_ref[...] += dg

    row = lambda w: pl.BlockSpec((tr, w), lambda i: (i, 0))
    full = lambda shape: pl.BlockSpec(shape, lambda i: (0,) * len(shape))
    return pl.pallas_call(
        body, name="inproj_bwd", grid=(tp // tr,),
        in_specs=[row(ATTN_WIDTH), row(2 * KV_WIDTH),
                  pl.BlockSpec((1, BLOCK, 2 * KV_WIDTH), lambda i: (jnp.minimum(i + 1, nt - 1), 0, 0)),
                  row(LRU_WIDTH), row(LRU_WIDTH), full(w_in.shape), row(D_MODEL), full(head.shape)]
        + _seq_specs(tr) + [row(D_MODEL), full(g.shape)],
        out_specs=[pl.BlockSpec(memory_space=pl.ANY), full((BLOCK, D_MODEL)), full((D_MODEL, IN_WIDTH)),
                   full((1, D_MODEL))],
        out_shape=[jax.ShapeDtypeStruct(x.shape, F32), jax.ShapeDtypeStruct((BLOCK, D_MODEL), F32),
                   jax.ShapeDtypeStruct((D_MODEL, IN_WIDTH), F32), jax.ShapeDtypeStruct((1, D_MODEL), F32)],
        scratch_shapes=[pltpu.VMEM((2, tr, D_MODEL), F32), pltpu.SemaphoreType.DMA((2,))],
        compiler_params=_params("arbitrary"),
    )(dq, dkv, dkv_extra, dxr, dyr, w_in, u0, head, *([x] * qb), dh1, g)


def _dense_block_diag(w):
    eye = jnp.eye(LRU_BLOCKS, dtype=w.dtype)
    return (w[:, :, None, :] * eye[:, None, :, None]).reshape(LRU_WIDTH, LRU_WIDTH)


def _diag_blocks(dense):
    d4 = dense.reshape(LRU_BLOCKS, LRU_BLOCK, LRU_BLOCKS, LRU_BLOCK)
    return jnp.stack([d4[n, :, n, :] for n in range(LRU_BLOCKS)])


def _local_step(head, x, tgt, g_pre_mix, w_in, conv_w, conv_b, w_a, b_a, w_x, b_x, lam, sinks, g_post_mix,
                g_pre_ffn, g_post_ffn, late_weights, on_ffn_grads, on_outproj_bwd, token):
    wa = _dense_block_diag(w_a).astype(BF16)
    wx = _dense_block_diag(w_x).astype(BF16)

    u0, qkv, xr, yr = _inproj_fwd(head, x, g_pre_mix, w_in, token)
    attn = _attn_fwd(qkv, sinks)
    hr, rec = _lru_fwd(xr, yr, conv_w, conv_b, wa, b_a, wx, b_x, lam)
    w_out, w1, w2 = late_weights([attn, rec])
    mix, h1, u1 = _outproj_fwd(attn, rec, w_out, head, x, g_post_mix, g_pre_ffn)
    r1, dy, df2, loss, dg_post_ffn = _ffn_fwd(u1, w1, w2, h1, tgt, g_post_ffn)

    da1, dh1, dmix, dg_pre_ffn, dg_post_mix = _ffn_bwd_data(df2, r1, w1, w2, dy, h1, mix, g_pre_ffn, g_post_mix)
    dw1, dw2 = _ffn_bwd_weights(u1, da1, r1, df2)
    token2 = on_ffn_grads(dw1, dw2)
    dattn, drec, dw_out = _outproj_bwd(dmix, w_out, attn, rec, token2)
    token3 = on_outproj_bwd(dattn)
    dxr, dyr, dwa, dwx, vec = _lru_bwd(xr, yr, hr, drec, conv_w, conv_b, wa, b_a, wx, b_x, lam, token3)
    dq, dkv, dkv_extra, dsinks = _attn_bwd(qkv, dattn, sinks)
    dx, dhead, dw_in, dg_pre_mix = _inproj_bwd(dq, dkv, dkv_extra, dxr, dyr, w_in, u0, head, x, dh1, g_pre_mix)

    grads = dict(
        g_pre_mix=dg_pre_mix, w_in=dw_in, conv_w=vec[0:4], conv_b=vec[4:5], w_a=_diag_blocks(dwa), b_a=vec[5:6],
        w_x=_diag_blocks(dwx), b_x=vec[6:7], lru_lambda=vec[7:8], attn_sinks=dsinks, w_out=dw_out,
        g_post_mix=dg_post_mix, g_pre_ffn=dg_pre_ffn, w_ff1=dw1, w_ff2=dw2, g_post_ffn=dg_post_ffn)
    return loss, dx, dhead, grads


HBM = pl.BlockSpec(memory_space=pltpu.HBM)


def _mesh_pos():
    return lax.axis_index("x"), lax.axis_index("y"), lax.axis_index("c")


def _other_chips(x, y):
    return [(1 - x, y), (x, 1 - y), (1 - x, 1 - y)]


def _remote(src, dst, send_sem, recv_sem, to):
    return pltpu.make_async_remote_copy(src_ref=src, dst_ref=dst, send_sem=send_sem, recv_sem=recv_sem,
                                        device_id=to, device_id_type=MESH)


def _gather_weights(shards, lands, tiny, tiny_land):
    nbig = len(shards)

    def body(*refs):
        srcs, tiny_src = refs[:nbig], refs[nbig]
        outs, tiny_out = refs[2 * nbig + 2:3 * nbig + 2], refs[3 * nbig + 2]
        ici_send, ici_recv, d2d_send, d2d_recv, tiny_send, tiny_recv = refs[3 * nbig + 3:]
        x, y, c = _mesh_pos()
        me = 2 * x + y
        chips = _other_chips(x, y)
        sibling = (x, y, 1 - c)
        sends = []
        for w, (src, out) in enumerate(zip(srcs, outs)):
            hr = src.shape[0] // 2
            for j, chip in enumerate(chips):
                k = 3 * w + j
                cp = _remote(src.at[pl.ds(c * hr, hr)], out.at[me, pl.ds(c * hr, hr)],
                             ici_send.at[k], ici_recv.at[k], (*chip, c))
                cp.start()
                sends.append(cp)
        for j, chip in enumerate(chips):
            cp = _remote(tiny_src, tiny_out.at[me], tiny_send.at[j], tiny_recv.at[j], (*chip, c))
            cp.start()
            sends.append(cp)
        for w, (src, out) in enumerate(zip(srcs, outs)):
            hr = src.shape[0] // 2
            for j, (px, py) in enumerate(chips):
                k = 3 * w + j
                landed = out.at[2 * px + py, pl.ds(c * hr, hr)]
                _remote(landed, landed, ici_send.at[k], ici_recv.at[k], sibling).wait_recv()
                cp = _remote(landed, landed, d2d_send.at[k], d2d_recv.at[k], sibling)
                cp.start()
                sends.append(cp)
        for w, (src, out) in enumerate(zip(srcs, outs)):
            hr = src.shape[0] // 2
            for j, (px, py) in enumerate(chips):
                k = 3 * w + j
                other = out.at[2 * px + py, pl.ds((1 - c) * hr, hr)]
                _remote(other, other, d2d_send.at[k], d2d_recv.at[k], sibling).wait_recv()
        for j, (px, py) in enumerate(chips):
            blk = tiny_out.at[2 * px + py]
            _remote(blk, blk, tiny_send.at[j], tiny_recv.at[j], sibling).wait_recv()
        for cp in sends:
            cp.wait_send()

    out_shape = [jax.ShapeDtypeStruct(l.shape, l.dtype) for l in list(lands) + [tiny_land]]
    n = 3 * nbig
    return pl.pallas_call(
        body, name="gather_weights", out_shape=out_shape,
        in_specs=[HBM] * (2 * nbig + 2), out_specs=[HBM] * (nbig + 1),
        input_output_aliases={nbig + 1 + i: i for i in range(nbig + 1)},
        scratch_shapes=[pltpu.SemaphoreType.DMA((n,)),
                        pltpu.SemaphoreType.DMA((n,)), pltpu.SemaphoreType.DMA((n,)), pltpu.SemaphoreType.DMA((n,)),
                        pltpu.SemaphoreType.DMA((3,)), pltpu.SemaphoreType.DMA((3,))],
    )(*shards, tiny, *lands, tiny_land)


def _prep_shard(w, me):
    rows, cols = w.shape
    tr = 256 if rows % 256 == 0 else rows

    def body(me_ref, w_ref, s_ref, l_ref):
        b = w_ref[...].astype(BF16)
        s_ref[...] = b
        l_ref[0] = b

    return pl.pallas_call(
        body, name="prep_shard",
        grid_spec=pltpu.PrefetchScalarGridSpec(
            num_scalar_prefetch=1, grid=(rows // tr,),
            in_specs=[pl.BlockSpec((tr, cols), lambda i, me_ref: (i, 0))],
            out_specs=[pl.BlockSpec((tr, cols), lambda i, me_ref: (i, 0)),
                       pl.BlockSpec((1, tr, cols), lambda i, me_ref: (me_ref[0], i, 0))]),
        out_shape=[jax.ShapeDtypeStruct((rows, cols), BF16), jax.ShapeDtypeStruct((N_CHIPS, rows, cols), BF16)],
        compiler_params=_params("parallel"),
    )(me, w)


def _prep_tiny(tiny, me, slots=N_CHIPS):
    def body(me_ref, t_ref, l_ref):
        l_ref[0] = t_ref[...]

    return pl.pallas_call(
        body, name="prep_tiny",
        grid_spec=pltpu.PrefetchScalarGridSpec(
            num_scalar_prefetch=1, grid=(1,),
            in_specs=[pl.BlockSpec(tiny.shape, lambda i, me_ref: (0, 0))],
            out_specs=pl.BlockSpec((1,) + tiny.shape, lambda i, me_ref: (me_ref[0], 0, 0))),
        out_shape=jax.ShapeDtypeStruct((slots,) + tiny.shape, tiny.dtype),
    )(me, tiny)


N_DEV = 8


def _sibling_exchange(parts, token):
    def body(*refs):
        n = len(parts)
        srcs, outs, send_sems, recv_sems = refs[:n], refs[n + 1:2 * n + 1], refs[2 * n + 1], refs[2 * n + 2]
        x, y, c = _mesh_pos()
        sibling = (x, y, 1 - c)
        cps = []
        for w, (src, out) in enumerate(zip(srcs, outs)):
            hr = src.shape[1] // 2
            cp = _remote(src.at[:, pl.ds((1 - c) * hr, hr)], out, send_sems.at[w], recv_sems.at[w], sibling)
            cp.start()
            cps.append(cp)
        for cp in cps:
            cp.wait()

    n = len(parts)
    return pl.pallas_call(
        body, name="sibling_exchange",
        out_shape=[jax.ShapeDtypeStruct((p.shape[0], p.shape[1] // 2, p.shape[2]), p.dtype) for p in parts],
        in_specs=[HBM] * n + [pl.BlockSpec(memory_space=pl.ANY)], out_specs=[HBM] * n,
        scratch_shapes=[pltpu.SemaphoreType.DMA((n,)), pltpu.SemaphoreType.DMA((n,))],
    )(*parts, token)


def _chip_presum(part, from_sibling, pos):
    _, hr, cols = from_sibling.shape
    tr = 256 if hr % 256 == 0 else hr
    steps = hr // tr

    def body(pos_ref, a_ref, b_ref, o_ref, land_ref):
        s = (a_ref[...] + b_ref[...]).astype(BF16)
        o_ref[...] = s

        @pl.when(pl.program_id(1) == pos_ref[1])
        def _():
            land_ref[...] = s

    return pl.pallas_call(
        body, name="chip_presum",
        grid_spec=pltpu.PrefetchScalarGridSpec(
            num_scalar_prefetch=1, grid=(steps, N_CHIPS),
            in_specs=[pl.BlockSpec((1, tr, cols), lambda i, j, p: (j, p[0] * steps + i, 0)),
                      pl.BlockSpec((1, tr, cols), lambda i, j, p: (j, i, 0))],
            out_specs=[pl.BlockSpec((1, tr, cols), lambda i, j, p: (j, i, 0)),
                       pl.BlockSpec((1, tr, cols), lambda i, j, p: (p[1], p[0] * steps + i, 0))]),
        out_shape=[jax.ShapeDtypeStruct(from_sibling.shape, BF16),
                   jax.ShapeDtypeStruct((N_CHIPS, 2 * hr, cols), BF16)],
        compiler_params=_params("arbitrary", "arbitrary"),
    )(pos, part, from_sibling)


def _scatter_partials(cparts, lands, done_cparts=(), done_lands=()):
    n_new = len(cparts)
    nw = n_new + len(done_cparts)

    def body(*refs):
        srcs = refs[:nw]
        outs = refs[2 * nw:3 * nw]
        own_send, own_recv, ici_send, ici_recv, d2d_send, d2d_recv = refs[3 * nw:]
        x, y, c = _mesh_pos()
        me = 2 * x + y
        chips = _other_chips(x, y)
        sibling = (x, y, 1 - c)
        sends = []
        for w in list(range(n_new, nw)) + list(range(n_new)):
            src, out = srcs[w], outs[w]
            hr = src.shape[1]
            mine = out.at[me, pl.ds(c * hr, hr)]
            cp = _remote(src.at[me], mine, own_send.at[w], own_recv.at[w], sibling)
            cp.start()
            sends.append(cp)
            for j, (px, py) in enumerate(chips):
                if w >= n_new:
                    break
                k = 3 * w + j
                cp = _remote(src.at[2 * px + py], mine, ici_send.at[k], ici_recv.at[k], (px, py, c))
                cp.start()
                sends.append(cp)
        for w in list(range(n_new, nw)) + list(range(n_new)):
            src, out = srcs[w], outs[w]
            hr = src.shape[1]
            for j, (px, py) in enumerate(chips):
                k = 3 * w + j
                landed = out.at[2 * px + py, pl.ds(c * hr, hr)]
                if w < n_new:
                    _remote(landed, landed, ici_send.at[k], ici_recv.at[k], sibling).wait_recv()
                cp = _remote(landed, landed, d2d_send.at[k], d2d_recv.at[k], sibling)
                cp.start()
                sends.append(cp)
        for w, (src, out) in enumerate(zip(srcs, outs)):
            hr = src.shape[1]
            other = out.at[me, pl.ds((1 - c) * hr, hr)]
            _remote(other, other, own_send.at[w], own_recv.at[w], sibling).wait_recv()
            for j, (px, py) in enumerate(chips):
                k = 3 * w + j
                other = out.at[2 * px + py, pl.ds((1 - c) * hr, hr)]
                _remote(other, other, d2d_send.at[k], d2d_recv.at[k], sibling).wait_recv()
        for cp in sends:
            cp.wait_send()

    n = 3 * nw
    dma = pltpu.SemaphoreType.DMA
    every = list(cparts) + list(done_cparts)
    every_lands = list(lands) + list(done_lands)
    return pl.pallas_call(
        body, name="scatter_partials",
        out_shape=[jax.ShapeDtypeStruct(l.shape, l.dtype) for l in every_lands],
        in_specs=[HBM] * (2 * nw), out_specs=[HBM] * nw,
        input_output_aliases={nw + i: i for i in range(nw)},
        scratch_shapes=[dma((nw,)), dma((nw,)), dma((n,)), dma((n,)), dma((n,)), dma((n,))],
    )(*every, *every_lands)


SEM = pl.BlockSpec(memory_space=pltpu.SEMAPHORE)
SPLIT_COPY = pltpu.CompilerParams(has_side_effects=pltpu.SideEffectType.DATAFLOW_SIDE_EFFECTING)


def _hbm(a):
    return pltpu.with_memory_space_constraint(a, pltpu.HBM)


def _gather_copies(srcs, lands, send_sems, recv_sems):
    x, y, c = _mesh_pos()
    me = 2 * x + y
    sends, recvs = [], []
    for w, (src, land) in enumerate(zip(srcs, lands)):
        hr = src.shape[0] // 2
        for j, (px, py) in enumerate(_other_chips(x, y)):
            k = 3 * w + j
            sends.append(_remote(src.at[pl.ds(c * hr, hr)], land.at[me, pl.ds(c * hr, hr)],
                                 send_sems.at[k], recv_sems.at[k], (px, py, c)))
            got = land.at[2 * px + py, pl.ds(c * hr, hr)]
            recvs.append(_remote(got, got, send_sems.at[k], recv_sems.at[k], (px, py, c)))
    return sends, recvs


def _scatter_copies(srcs, lands, send_sems, recv_sems):
    x, y, c = _mesh_pos()
    me = 2 * x + y
    sends, recvs = [], []
    for w, (src, land) in enumerate(zip(srcs, lands)):
        hr = src.shape[1]
        for j, (px, py) in enumerate(_other_chips(x, y)):
            k = 3 * w + j
            sends.append(_remote(src.at[2 * px + py], land.at[me, pl.ds(c * hr, hr)],
                                 send_sems.at[k], recv_sems.at[k], (px, py, c)))
            got = land.at[2 * px + py, pl.ds(c * hr, hr)]
            recvs.append(_remote(got, got, send_sems.at[k], recv_sems.at[k], (px, py, c)))
    return sends, recvs


def _sibling_copies(srcs, lands, send_sems, recv_sems):
    x, y, c = _mesh_pos()
    sibling = (x, y, 1 - c)
    sends, recvs = [], []
    for w, (src, land) in enumerate(zip(srcs, lands)):
        hr = src.shape[1] // 2
        sends.append(_remote(src.at[:, pl.ds((1 - c) * hr, hr)], land, send_sems.at[w], recv_sems.at[w], sibling))
        recvs.append(_remote(land, land, send_sems.at[w], recv_sems.at[w], sibling))
    return sends, recvs


def _all_peers_copies(srcs, lands, send_sems, recv_sems):
    x, y, c = _mesh_pos()
    (src,), (land,) = srcs, lands
    flip = lambda v, bit: 1 - v if bit else v
    sends, recvs = [], []
    for k in range(N_DEV - 1):
        px, py, pc = flip(x, (k + 1) & 4), flip(y, (k + 1) & 2), flip(c, (k + 1) & 1)
        sends.append(_remote(src, land.at[4 * x + 2 * y + c], send_sems.at[k], recv_sems.at[k], (px, py, pc)))
        got = land.at[4 * px + 2 * py + pc]
        recvs.append(_remote(got, got, send_sems.at[k], recv_sems.at[k], (px, py, pc)))
    return sends, recvs


def _split_start(name, copies_of, srcs, land_shapes, n_copies=None):
    n = len(srcs)
    k = 3 * n if n_copies is None else n_copies

    def body(*refs):
        src_refs, land_refs = refs[:n], refs[n:2 * n]
        send_sems, recv_sems = refs[2 * n], refs[2 * n + 1]
        token = refs[-1]
        sends, _ = copies_of(src_refs, land_refs, send_sems, recv_sems)
        for cp in sends:
            cp.start()
        token[...] = jnp.zeros_like(token)

    lands = [_hbm(s) for s in land_shapes]
    dma = pltpu.SemaphoreType.DMA
    res = pl.pallas_call(
        body, name=name,
        out_shape=(dma((k,)), dma((k,)), *[pltpu.HBM(s.shape, s.dtype) for s in srcs],
                   *[pltpu.HBM(s.shape, s.dtype) for s in land_shapes], jax.ShapeDtypeStruct((8, 128), F32)),
        in_specs=[HBM] * (2 * n),
        out_specs=(SEM, SEM, *([HBM] * (2 * n)), pl.BlockSpec(memory_space=pltpu.VMEM)),
        input_output_aliases={i: 2 + i for i in range(2 * n)},
        compiler_params=SPLIT_COPY,
    )(*[_hbm(s) for s in srcs], *lands)
    return res[0], res[1], list(res[2:2 + n]), list(res[2 + n:2 + 2 * n]), res[-1]


def _split_wait(name, copies_of, send_sems, recv_sems, srcs, lands, after):
    n = len(srcs)

    def body(*refs):
        src_refs, land_refs = refs[:n], refs[n:2 * n]
        sends, recvs = copies_of(src_refs, land_refs, refs[2 * n], refs[2 * n + 1])
        for cp in sends:
            cp.wait_send()
        for cp in recvs:
            cp.wait_recv()

    res = pl.pallas_call(
        body, name=name,
        out_shape=tuple(pltpu.HBM(s.shape, s.dtype) for s in list(srcs) + list(lands)),
        in_specs=[HBM] * (2 * n) + [SEM, SEM] + [pl.BlockSpec(memory_space=pl.ANY)] * len(after),
        out_specs=tuple([HBM] * (2 * n)),
        input_output_aliases={i: i for i in range(2 * n)},
        compiler_params=SPLIT_COPY,
    )(*srcs, *lands, send_sems, recv_sems, *after)
    return list(res[:n]), list(res[n:])


def _gather_finish(lands):
    n = len(lands)

    def body(*refs):
        outs = refs[n:2 * n]
        d2d_send, d2d_recv = refs[2 * n:]
        x, y, c = _mesh_pos()
        chips = _other_chips(x, y)
        sibling = (x, y, 1 - c)
        sends = []
        for w, out in enumerate(outs):
            hr = out.shape[1] // 2
            for j, (px, py) in enumerate(chips):
                landed = out.at[2 * px + py, pl.ds(c * hr, hr)]
                cp = _remote(landed, landed, d2d_send.at[3 * w + j], d2d_recv.at[3 * w + j], sibling)
                cp.start()
                sends.append(cp)
        for w, out in enumerate(outs):
            hr = out.shape[1] // 2
            for j, (px, py) in enumerate(chips):
                other = out.at[2 * px + py, pl.ds((1 - c) * hr, hr)]
                _remote(other, other, d2d_send.at[3 * w + j], d2d_recv.at[3 * w + j], sibling).wait_recv()
        for cp in sends:
            cp.wait_send()

    dma = pltpu.SemaphoreType.DMA
    return pl.pallas_call(
        body, name="gather_finish",
        out_shape=[jax.ShapeDtypeStruct(l.shape, l.dtype) for l in lands],
        in_specs=[HBM] * n, out_specs=[HBM] * n,
        input_output_aliases={i: i for i in range(n)},
        scratch_shapes=[dma((3 * n,)), dma((3 * n,))],
    )(*lands)


def _adamw(w, g, m, v):
    m = ADAM_B1 * m + (1.0 - ADAM_B1) * g
    v = ADAM_B2 * v + (1.0 - ADAM_B2) * (g * g)
    m_hat = m / (1.0 - ADAM_B1 ** ADAM_STEP)
    v_hat = v / (1.0 - ADAM_B2 ** ADAM_STEP)
    delta = -ADAM_LR * (m_hat / (jnp.sqrt(v_hat) + ADAM_EPS) + ADAM_WD * w)
    return delta, m, v


def _adamw_big(partials, w, m, v):
    rows, cols = w.shape
    tr = 256

    def body(p_ref, w_ref, m_ref, v_ref, g_ref, d_ref, m2_ref, v2_ref):
        g = ((p_ref[0].astype(F32) + p_ref[1].astype(F32)) + p_ref[2].astype(F32)) + p_ref[3].astype(F32)
        g_ref[...] = g
        d_ref[...], m2_ref[...], v2_ref[...] = _adamw(w_ref[...], g, m_ref[...], v_ref[...])

    blk = pl.BlockSpec((tr, cols), lambda i: (i, 0))
    return pl.pallas_call(
        body, name="adamw_big", grid=(rows // tr,),
        in_specs=[pl.BlockSpec((N_CHIPS, tr, cols), lambda i: (0, i, 0)), blk, blk, blk],
        out_specs=[blk] * 4, out_shape=[jax.ShapeDtypeStruct((rows, cols), F32)] * 4,
        compiler_params=_params("parallel"),
    )(partials, w, m, v)


def _sum_devices(gathered, rows):
    cols = gathered.shape[1]

    def body(g_ref, o_ref):
        acc = g_ref[0:rows]
        for d in range(1, N_DEV):
            acc = acc + g_ref[d * rows:(d + 1) * rows]
        o_ref[...] = acc

    return pl.pallas_call(
        body, name="sum_devices", out_shape=jax.ShapeDtypeStruct((rows, cols), F32),
        in_specs=[pl.BlockSpec(memory_space=pltpu.VMEM)], out_specs=pl.BlockSpec(memory_space=pltpu.VMEM),
        compiler_params=pltpu.CompilerParams(vmem_limit_bytes=VMEM_LIMIT_V7X),
    )(gathered)


def _adamw_small(quads):
    n = len(quads)

    def body(*refs):
        ins, outs = refs[:4 * n], refs[4 * n:]
        for t in range(n):
            w, g, m, v = (r[...] for r in ins[4 * t:4 * t + 4])
            outs[3 * t][...], outs[3 * t + 1][...], outs[3 * t + 2][...] = _adamw(w, g, m, v)

    flat = [a for q in quads for a in q]
    vm = pl.BlockSpec(memory_space=pltpu.VMEM)
    res = pl.pallas_call(
        body, name="adamw_small",
        out_shape=[jax.ShapeDtypeStruct(q[0].shape, F32) for q in quads for _ in range(3)],
        in_specs=[vm] * (4 * n), out_specs=[vm] * (3 * n),
    )(*flat)
    return [tuple(res[3 * t:3 * t + 3]) for t in range(n)]


SMALL_PACK_ROWS = 96
_WEIGHTS = ['meta_tokens', 'g_pre_mix', 'w_in', 'conv_w', 'conv_b', 'w_a', 'b_a', 'w_x', 'b_x', 'lru_lambda',
            'attn_sinks', 'w_out', 'g_post_mix', 'g_pre_ffn', 'w_ff1', 'w_ff2', 'g_post_ffn']
_BIG = ['w_in', 'w_out', 'w_ff1', 'w_ff2']


def _pack_small(dmeta, g):
    z = lambda r, c: jnp.zeros((r, c), F32)
    rows = [
        dmeta,
        g['g_pre_mix'], g['g_post_mix'], g['g_pre_ffn'], g['g_post_ffn'],
        jnp.concatenate([g['conv_w'], z(4, 512)], axis=1),
        jnp.concatenate([g['conv_b'], g['b_a']], axis=1),
        jnp.concatenate([g['b_x'], g['lru_lambda']], axis=1),
        jnp.concatenate([g['attn_sinks'], z(1, D_MODEL - ATTN_HEADS)], axis=1),
        z(5, D_MODEL),
        g['w_a'].reshape(32, D_MODEL), g['w_x'].reshape(32, D_MODEL),
    ]
    return jnp.concatenate(rows, axis=0)


def _unpack_small(s, chip):
    return dict(
        meta_tokens=lax.dynamic_slice(s[0:16], (0, chip * 256), (16, 256)),
        g_pre_mix=s[16:17], g_post_mix=s[17:18], g_pre_ffn=s[18:19], g_post_ffn=s[19:20],
        conv_w=lax.dynamic_slice(s[20:24], (0, chip * 128), (4, 128)).reshape(1, 4, 128),
        conv_b=s[24:25, :512], b_a=s[24:25, 512:], b_x=s[25:26, :512], lru_lambda=s[25:26, 512:],
        attn_sinks=s[26:27, :ATTN_HEADS],
        w_a=s[32:64].reshape(1, LRU_BLOCKS, LRU_BLOCK, LRU_BLOCK),
        w_x=s[64:96].reshape(1, LRU_BLOCKS, LRU_BLOCK, LRU_BLOCK))


def _as2d(a):
    if a.ndim == 2:
        return a
    return a.reshape(-1, a.shape[-1])


def kernel(x, meta_tokens, g_pre_mix, w_in, conv_w, conv_b, w_a, b_a, w_x, b_x, lru_lambda, attn_sinks, w_out, g_post_mix, g_pre_ffn, w_ff1, w_ff2, g_post_ffn, loss_target, m_meta_tokens, m_g_pre_mix, m_w_in, m_conv_w, m_conv_b, m_w_a, m_b_a, m_w_x, m_b_x, m_lru_lambda, m_attn_sinks, m_w_out, m_g_post_mix, m_g_pre_ffn, m_w_ff1, m_w_ff2, m_g_post_ffn, v_meta_tokens, v_g_pre_mix, v_w_in, v_conv_w, v_conv_b, v_w_a, v_b_a, v_w_x, v_b_x, v_lru_lambda, v_attn_sinks, v_w_out, v_g_post_mix, v_g_pre_ffn, v_w_ff1, v_w_ff2, v_g_post_ffn):
    weights = dict(meta_tokens=meta_tokens, g_pre_mix=g_pre_mix, w_in=w_in, conv_w=conv_w, conv_b=conv_b, w_a=w_a,
                   b_a=b_a, w_x=w_x, b_x=b_x, lru_lambda=lru_lambda, attn_sinks=attn_sinks, w_out=w_out,
                   g_post_mix=g_post_mix, g_pre_ffn=g_pre_ffn, w_ff1=w_ff1, w_ff2=w_ff2, g_post_ffn=g_post_ffn)
    mom1 = dict(zip(_WEIGHTS, [m_meta_tokens, m_g_pre_mix, m_w_in, m_conv_w, m_conv_b, m_w_a, m_b_a, m_w_x, m_b_x,
                               m_lru_lambda, m_attn_sinks, m_w_out, m_g_post_mix, m_g_pre_ffn, m_w_ff1, m_w_ff2,
                               m_g_post_ffn]))
    mom2 = dict(zip(_WEIGHTS, [v_meta_tokens, v_g_pre_mix, v_w_in, v_conv_w, v_conv_b, v_w_a, v_b_a, v_w_x, v_b_x,
                               v_lru_lambda, v_attn_sinks, v_w_out, v_g_post_mix, v_g_pre_ffn, v_w_ff1, v_w_ff2,
                               v_g_post_ffn]))
    xi, yi, ci = _mesh_pos()
    chip = 2 * xi + yi

    tiny = jnp.concatenate([meta_tokens, jnp.pad(conv_w[0], ((0, 4), (0, 128)))], axis=0)
    chip_arr = jnp.reshape(chip, (1,)).astype(jnp.int32)
    shards, lands = zip(*[_prep_shard(w[0], chip_arr) for w in (w_in, w_out, w_ff1, w_ff2)])
    g_in, g_tiny = _gather_weights(shards[:1], lands[:1], tiny, _prep_tiny(tiny, chip_arr))
    w_in_full = jnp.concatenate([g_in[j] for j in range(N_CHIPS)], axis=1)
    meta_full = jnp.concatenate([g_tiny[j, :N_META] for j in range(N_CHIPS)], axis=1)
    conv_w_full = jnp.concatenate([g_tiny[j, N_META:N_META + 4, :128] for j in range(N_CHIPS)], axis=1)
    g_send, g_recv, late_thru, late_lands, token = _split_start(
        "gather_late_start", _gather_copies, shards[1:], lands[1:])

    def late_weights(after):
        _, landed = _split_wait("gather_late_wait", _gather_copies, g_send, g_recv, late_thru, late_lands, after)
        g_out, g_f1, g_f2 = _gather_finish(landed)
        return g_out.reshape(D_MODEL, D_MODEL), g_f1, g_f2

    pos = jnp.stack([ci, chip]).astype(jnp.int32)
    ffn = {}


    def on_ffn_grads(dw1, dw2):
        parts = [dw1, dw2]
        lands = [lax.empty((p.shape[0], p.shape[1] // 2, p.shape[2]), p.dtype) for p in parts]
        ffn['sib'] = _split_start("sibling_ffn_start", _sibling_copies, parts, lands, len(parts))
        return ffn['sib'][4]

    def on_outproj_bwd(dattn):
        send, recv, thru, lands, _ = ffn['sib']
        parts, from_sibling = _split_wait("sibling_ffn_wait", _sibling_copies, send, recv, thru, lands, [dattn])
        cparts_ffn, lands_ffn = zip(*[_chip_presum(p, r, pos) for p, r in zip(parts, from_sibling)])
        ffn['send'], ffn['recv'], ffn['thru'], ffn['lands'], token3 = _split_start(
            "scatter_ffn_start", _scatter_copies, cparts_ffn, lands_ffn)
        return token3

    head = jnp.concatenate([jnp.zeros((PAD_ROWS, D_MODEL), F32), meta_full], axis=0)
    loss, dx, dhead, grads = _local_step(head, x[0], loss_target[0], g_pre_mix, w_in_full, conv_w_full, conv_b, w_a[0],
                                         b_a, w_x[0], b_x, lru_lambda, attn_sinks, g_post_mix, g_pre_ffn, g_post_ffn,
                                         late_weights, on_ffn_grads, on_outproj_bwd, token)
    loss = lax.psum(loss[0, 0], ("x", "y", "c"))
    grad_x = dx[None]

    pack = _pack_small(dhead[PAD_ROWS:], grads)
    dev = jnp.reshape(4 * xi + 2 * yi + ci, (1,)).astype(jnp.int32)
    s_send, s_recv, s_thru, s_lands, token4 = _split_start(
        "gather_small_start", _all_peers_copies, [pack], [_prep_tiny(pack, dev, N_DEV)], N_DEV - 1)

    dw_in = grads['w_in']
    parts = [jnp.stack([dw_in[:, j * 448:(j + 1) * 448] for j in range(N_CHIPS)]),
             grads['w_out'].reshape(N_CHIPS, D_MODEL // N_CHIPS, D_MODEL)]
    cparts, own_lands = zip(*[_chip_presum(p, r, pos) for p, r in zip(parts, _sibling_exchange(parts, token4))])
    ffn_cparts, ffn_lands = _split_wait("scatter_ffn_wait", _scatter_copies, ffn['send'], ffn['recv'], ffn['thru'],
                                        ffn['lands'], cparts)
    chip_partials = _scatter_partials(cparts, own_lands, ffn_cparts, ffn_lands)

    g_out_d, delta, new_m, new_v = {}, {}, {}, {}
    for name, part in zip(_BIG, chip_partials):
        shp = weights[name].shape
        res = _adamw_big(part, weights[name][0], mom1[name][0], mom2[name][0])
        g_out_d[name], delta[name], new_m[name], new_v[name] = (r.reshape(shp) for r in res)

    _, (gathered,) = _split_wait("gather_small_wait", _all_peers_copies, s_send, s_recv, s_thru, s_lands,
                                 [g_out_d[n] for n in _BIG])
    small = _unpack_small(_sum_devices(gathered.reshape(N_DEV * SMALL_PACK_ROWS, D_MODEL), SMALL_PACK_ROWS), chip)
    small_names = [n for n in _WEIGHTS if n not in _BIG]
    quads = [(_as2d(weights[n]), _as2d(small[n]), _as2d(mom1[n]), _as2d(mom2[n])) for n in small_names]
    for name, (d, m2, v2) in zip(small_names, _adamw_small(quads)):
        shp = weights[name].shape
        g_out_d[name] = small[name].reshape(shp)
        delta[name], new_m[name], new_v[name] = d.reshape(shp), m2.reshape(shp), v2.reshape(shp)

    return (loss, grad_x, *[g_out_d[n] for n in _WEIGHTS], *[delta[n] for n in _WEIGHTS],
            *[new_m[n] for n in _WEIGHTS], *[new_v[n] for n in _WEIGHTS])
```

```python
import numpy as np
import jax
import jax.numpy as jnp
from jax import lax
from jax.experimental import pallas as pl
from jax.experimental.pallas import tpu as pltpu

F32 = jnp.float32
BF16 = jnp.bfloat16

D_MODEL = 1024
N_META = 16
BLOCK = 128
PAD_ROWS = BLOCK - N_META
HEAD_DIM = 64
ATTN_HEADS = 8
GQA_GROUP = 4
ATTN_WIDTH = 512
KV_WIDTH = 128
QKV_WIDTH = ATTN_WIDTH + 2 * KV_WIDTH
LRU_WIDTH = 512
LRU_BLOCKS = 8
LRU_BLOCK = 64
LRU_C = 8.0
IN_WIDTH = 1792
D_FF = 4096
N_CHIPS = 4
FF_CHUNK = D_FF // N_CHIPS
EPS = 1e-6
NEG = -1e30

ADAM_LR = 0.001
ADAM_B1 = 0.9
ADAM_B2 = 0.999
ADAM_EPS = 1e-08
ADAM_WD = 0.01
ADAM_STEP = 10

VMEM_LIMIT_V7X = 56 * 1024 * 1024
MESH = pl.DeviceIdType.MESH

NT = (((1,), (1,)), ((), ()))
TN = (((0,), (0,)), ((), ()))


def _row_tile(tp):
    return 640 if tp % 640 == 0 else BLOCK


def _wgrad_row_tile(tp):
    return 1664 if tp % 1664 == 0 else _row_tile(tp)


def _params(*sem):
    return pltpu.CompilerParams(dimension_semantics=sem, vmem_limit_bytes=VMEM_LIMIT_V7X)


def _dot(a, b):
    return jnp.dot(a, b, preferred_element_type=F32)


def _dot_nt(a, b):
    return lax.dot_general(a, b, NT, preferred_element_type=F32)


def _dot_tn(a, b):
    return lax.dot_general(a, b, TN, preferred_element_type=F32)


def _rms(x):
    rs = lax.rsqrt(jnp.mean(x * x, axis=-1, keepdims=True) + EPS)
    return x * rs, rs


def _rms_bwd(xhat, rs, g, dy):
    dyg = dy * g
    dx = rs * (dyg - xhat * jnp.mean(dyg * xhat, axis=-1, keepdims=True))
    dg = jnp.sum(dy * xhat, axis=0, keepdims=True)
    return dx, dg


def _gelu(x):
    k = 0.7978845608028654
    t = jnp.tanh(k * (x + 0.044715 * x * x * x))
    return 0.5 * x * (1.0 + t), t


def _gelu_grad(x, t):
    k = 0.7978845608028654
    return 0.5 * (1.0 + t) + 0.5 * x * (1.0 - t * t) * k * (1.0 + 3 * 0.044715 * x * x)


def _sigmoid(x):
    return 0.5 * jnp.tanh(0.5 * x) + 0.5


def _neg_expm1(x):
    series = x * (1.0 + x * 0.5 * (1.0 + x * (1.0 / 3.0) * (1.0 + x * 0.25 * (1.0 + x * 0.2))))
    return -jnp.where(jnp.abs(x) < 0.05, series, jnp.exp(x) - 1.0)


def _softplus(x):
    return jnp.maximum(x, 0.0) + jnp.log1p(jnp.exp(-jnp.abs(x)))


def _seq_specs(tr):
    qb = tr // BLOCK
    return [pl.BlockSpec((BLOCK, D_MODEL), lambda i, *_, s=s: (jnp.maximum(i * qb + s - 1, 0), 0)) for s in range(qb)]


def _seq_tile(head, pieces, i):
    first = jnp.where(i == 0, head, pieces[0][...])
    return jnp.concatenate([first] + [p[...] for p in pieces[1:]], axis=0)


def _inproj_fwd(head, x, g, w_in, token):
    tp = BLOCK + x.shape[0]
    tr = _row_tile(tp)
    qb = tr // BLOCK

    def body(*refs):
        head_ref, pieces = refs[0], refs[1:1 + qb]
        g_ref, w_ref, _, u_ref, qkv_ref, xr_ref, yr_ref = refs[1 + qb:]
        xhat, _ = _rms(_seq_tile(head_ref[...], pieces, pl.program_id(0)))
        u = (xhat * g_ref[...]).astype(BF16)
        u_ref[...] = u
        z = _dot(u, w_ref[...])
        qkv_ref[...] = z[:, :QKV_WIDTH].astype(BF16)
        xr_ref[...] = z[:, QKV_WIDTH:QKV_WIDTH + LRU_WIDTH]
        yr_ref[...] = z[:, QKV_WIDTH + LRU_WIDTH:]

    row = lambda w: pl.BlockSpec((tr, w), lambda i: (i, 0))
    full = lambda a: pl.BlockSpec(a.shape, lambda i: (0,) * a.ndim)
    return pl.pallas_call(
        body, name="inproj_fwd", grid=(tp // tr,),
        in_specs=[full(head)] + _seq_specs(tr) + [full(g), full(w_in), full(token)],
        out_specs=[row(D_MODEL), row(QKV_WIDTH), row(LRU_WIDTH), row(LRU_WIDTH)],
        out_shape=[jax.ShapeDtypeStruct((tp, D_MODEL), BF16), jax.ShapeDtypeStruct((tp, QKV_WIDTH), BF16),
                   jax.ShapeDtypeStruct((tp, LRU_WIDTH), F32), jax.ShapeDtypeStruct((tp, LRU_WIDTH), F32)],
        compiler_params=_params("parallel"),
    )(head, *([x] * qb), g, w_in, token)


GROUP_ROWS = GQA_GROUP * BLOCK


def _attn_bias():
    j = np.arange(2 * BLOCK)[:, None]
    i = np.arange(BLOCK)[None, :]
    band = (j - i >= 1) & (j - i <= BLOCK)
    out = []
    for n in range(3):
        ok = band & ((n - 1) * BLOCK + j >= PAD_ROWS) if n < 2 else band
        out.append(np.tile(np.where(ok, 0.0, NEG).astype(np.float32), (1, GQA_GROUP)))
    return jnp.asarray(np.stack(out))


def _stack_heads(a, g):
    heads = range(GQA_GROUP * g, GQA_GROUP * (g + 1))
    return jnp.concatenate([a[:, h * HEAD_DIM:(h + 1) * HEAD_DIM] for h in heads], axis=0)


def _unstack_heads(groups):
    return jnp.concatenate([p[h * BLOCK:(h + 1) * BLOCK] for p in groups for h in range(GQA_GROUP)], axis=1)


def _attn_probs_t(k_g, qg, bias, sink_row):
    st = _dot_nt(k_g, qg) + bias
    m = jnp.maximum(jnp.max(st, axis=0, keepdims=True), sink_row)
    p = jnp.exp(st - m)
    es = jnp.exp(sink_row - m)
    inv = 1.0 / (jnp.sum(p, axis=0, keepdims=True) + es)
    return p * inv, es * inv


def _attn_consts(sinks):
    return jnp.repeat(sinks.reshape(ATTN_HEADS), BLOCK).reshape(ATTN_HEADS // GQA_GROUP, GROUP_ROWS), _attn_bias()


_SINK_SPEC = pl.BlockSpec((ATTN_HEADS // GQA_GROUP, GROUP_ROWS), lambda n: (0, 0))
_BIAS_SPEC = pl.BlockSpec((3, 2 * BLOCK, GROUP_ROWS), lambda n: (0, 0, 0))
_QSCALE = HEAD_DIM ** -0.5


def _kv_specs(tr):
    qb = tr // BLOCK
    prev = lambda col: pl.BlockSpec((BLOCK, KV_WIDTH), lambda t: (jnp.maximum(t * qb - 1, 0), col))
    cur = lambda col: pl.BlockSpec((tr, KV_WIDTH), lambda t: (t, col))
    return [prev(4), cur(4), prev(5), cur(5)]


def _block_bias(b_ref, t, qb, i):
    return b_ref[2] if i >= 2 else b_ref[jnp.minimum(t * qb + i, 2)]


def _attn_fwd(qkv, sinks):
    tp = qkv.shape[0]
    tr = _row_tile(tp)
    qb = tr // BLOCK
    sink_rows, bias = _attn_consts(sinks)

    def body(s_ref, b_ref, q_ref, kp_ref, kc_ref, vp_ref, vc_ref, o_ref):
        t = pl.program_id(0)
        k_all = jnp.concatenate([kp_ref[...], kc_ref[...]], axis=0)
        v_all = jnp.concatenate([vp_ref[...], vc_ref[...]], axis=0)
        for i in range(qb):
            rows = slice(i * BLOCK, (i + 1) * BLOCK)
            q = q_ref[rows]
            k2, v2 = k_all[i * BLOCK:(i + 2) * BLOCK], v_all[i * BLOCK:(i + 2) * BLOCK]
            bias_n = _block_bias(b_ref, t, qb, i)
            outs = []
            for g in range(ATTN_HEADS // GQA_GROUP):
                cols = slice(g * HEAD_DIM, (g + 1) * HEAD_DIM)
                qg = _stack_heads(q, g) * jnp.asarray(_QSCALE, BF16)
                p, _ = _attn_probs_t(k2[:, cols], qg, bias_n, s_ref[g:g + 1])
                outs.append(_dot_tn(p.astype(BF16), v2[:, cols]))
            o_ref[rows] = _unstack_heads(outs).astype(BF16)

    return pl.pallas_call(
        body, name="attn_fwd", grid=(tp // tr,),
        in_specs=[_SINK_SPEC, _BIAS_SPEC, pl.BlockSpec((tr, ATTN_WIDTH), lambda t: (t, 0))] + _kv_specs(tr),
        out_specs=pl.BlockSpec((tr, ATTN_WIDTH), lambda t: (t, 0)),
        out_shape=jax.ShapeDtypeStruct((tp, ATTN_WIDTH), BF16),
        compiler_params=_params("parallel"),
    )(sink_rows, bias, qkv, qkv, qkv, qkv, qkv)


def _conv_taps(x, halo):
    ext = jnp.concatenate([halo, x], axis=0)
    return [ext[8:] if k == 3 else pltpu.roll(ext, 3 - k, 0)[8:] for k in range(4)]


def _lru_gates(xc, wa, ba, wx, bx, sp):
    xb = xc.astype(BF16)
    r = _sigmoid(_dot(xb, wa) + ba)
    ig = _sigmoid(_dot(xb, wx) + bx)
    log_a = (-LRU_C * sp) * r
    a = jnp.exp(log_a)
    mult = jnp.sqrt(_neg_expm1(2.0 * log_a))
    return xb, r, ig, a, mult


SUBLANES = 8


def _scan_fwd(a, b, h_in):
    n, width = a.shape
    a, b = (v.reshape(n // SUBLANES, SUBLANES, width) for v in (a, b))
    in_group = lax.broadcasted_iota(jnp.int32, a.shape, 1)
    for d in (1, 2, 4):
        keep = in_group >= d
        b = jnp.where(keep, a * pltpu.roll(b, d, 1) + b, b)
        a = jnp.where(keep, a * pltpu.roll(a, d, 1), a)
    a, b = a.reshape(n, width), b.reshape(n, width)
    out, carry = [], h_in
    for g in range(0, n, SUBLANES):
        h = a[g:g + SUBLANES] * carry + b[g:g + SUBLANES]
        out.append(h)
        carry = h[SUBLANES - 1:]
    return jnp.concatenate(out, axis=0)


def _scan_rev(c, b, g_in):
    n, width = c.shape
    c, b = (v.reshape(n // SUBLANES, SUBLANES, width) for v in (c, b))
    in_group = lax.broadcasted_iota(jnp.int32, c.shape, 1)
    for d in (1, 2, 4):
        keep = in_group < SUBLANES - d
        b = jnp.where(keep, b + c * pltpu.roll(b, SUBLANES - d, 1), b)
        c = jnp.where(keep, c * pltpu.roll(c, SUBLANES - d, 1), c)
    c, b = c.reshape(n, width), b.reshape(n, width)
    out, carry = [], g_in
    for g in range(n - SUBLANES, -1, -SUBLANES):
        r = b[g:g + SUBLANES] + c[g:g + SUBLANES] * carry
        out.append(r)
        carry = r[:1]
    return jnp.concatenate(out[::-1], axis=0)


def _lru_fwd(xr, yr, conv_w, conv_b, wa, ba, wx, bx, lam):
    tp = xr.shape[0]
    tr = _row_tile(tp)
    qb = tr // BLOCK

    def body(xr_ref, yr_ref, cw_ref, cb_ref, wa_ref, ba_ref, wx_ref, bx_ref, lam_ref, hr_ref, rec_ref, halo, hprev):
        t = pl.program_id(0)

        @pl.when(t == 0)
        def _():
            halo[...] = jnp.zeros_like(halo)
            hprev[...] = jnp.zeros_like(hprev)

        cw, cb = cw_ref[...], cb_ref[...]
        wa_m, ba_v, wx_m, bx_v = wa_ref[...], ba_ref[...], wx_ref[...], bx_ref[...]
        sp = _softplus(-lam_ref[...])
        before, h_last = halo[...], hprev[0:1]
        for i in range(qb):
            rows = slice(i * BLOCK, (i + 1) * BLOCK)
            x = xr_ref[rows]
            taps = _conv_taps(x, before)
            before = x[BLOCK - 8:]
            xc = cb + sum(cw[k:k + 1] * taps[k] for k in range(4))
            _, _, ig, a, mult = _lru_gates(xc, wa_m, ba_v, wx_m, bx_v, sp)
            u = mult * (ig * xc)
            if i == 0:
                pos = t * tr + lax.broadcasted_iota(jnp.int32, xc.shape, 0)
                u = jnp.where(pos >= PAD_ROWS, u, 0.0)
            h = _scan_fwd(a, u, h_last)
            h_last = h[BLOCK - 1:]
            hr_ref[rows] = h
            gl, _ = _gelu(yr_ref[rows])
            rec_ref[rows] = (gl * h).astype(BF16)
        halo[...] = before
        hprev[0:1] = h_last

    blk = pl.BlockSpec((tr, LRU_WIDTH), lambda t: (t, 0))
    full = lambda a: pl.BlockSpec(a.shape, lambda t: (0,) * a.ndim)
    small = [conv_w, conv_b, wa, ba, wx, bx, lam]
    return pl.pallas_call(
        body, name="lru_fwd", grid=(tp // tr,),
        in_specs=[blk, blk] + [full(a) for a in small],
        out_specs=[blk, blk],
        out_shape=[jax.ShapeDtypeStruct((tp, LRU_WIDTH), F32), jax.ShapeDtypeStruct((tp, LRU_WIDTH), BF16)],
        scratch_shapes=[pltpu.VMEM((8, LRU_WIDTH), F32), pltpu.VMEM((8, LRU_WIDTH), F32)],
        compiler_params=_params("arbitrary"),
    )(xr, yr, *small)


def _outproj_fwd(attn, rec, w_out, head, x, g_post_mix, g_pre_ffn):
    tp = attn.shape[0]
    tr = _row_tile(tp)
    qb = tr // BLOCK

    def body(*refs):
        a_ref, r_ref, w_ref, head_ref = refs[:4]
        pieces = refs[4:4 + qb]
        gm_ref, gf_ref, mix_ref, h1_ref, u1_ref = refs[4 + qb:]
        mix = _dot(a_ref[...], w_ref[:ATTN_WIDTH]) + _dot(r_ref[...], w_ref[ATTN_WIDTH:])
        mix_ref[...] = mix
        mhat, _ = _rms(mix)
        h1 = _seq_tile(head_ref[...], pieces, pl.program_id(0)) + mhat * gm_ref[...]
        h1_ref[...] = h1
        hhat, _ = _rms(h1)
        u1_ref[...] = (hhat * gf_ref[...]).astype(BF16)

    row = lambda w: pl.BlockSpec((tr, w), lambda i: (i, 0))
    full = lambda a: pl.BlockSpec(a.shape, lambda i: (0,) * a.ndim)
    return pl.pallas_call(
        body, name="outproj_fwd", grid=(tp // tr,),
        in_specs=[row(ATTN_WIDTH), row(LRU_WIDTH), full(w_out), full(head)] + _seq_specs(tr)
        + [full(g_post_mix), full(g_pre_ffn)],
        out_specs=[row(D_MODEL), row(D_MODEL), row(D_MODEL)],
        out_shape=[jax.ShapeDtypeStruct((tp, D_MODEL), F32), jax.ShapeDtypeStruct((tp, D_MODEL), F32),
                   jax.ShapeDtypeStruct((tp, D_MODEL), BF16)],
        compiler_params=_params("parallel"),
    )(attn, rec, w_out, head, *([x] * qb), g_post_mix, g_pre_ffn)


def _ffn_fwd(u1, w1, w2, h1, tgt, g_post_ffn):
    tp = h1.shape[0]
    tr = _row_tile(tp)
    qb = tr // BLOCK

    def body(*refs):
        u_ref, w1_ref, w2_ref, h1_ref = refs[:4]
        t_pieces = refs[4:4 + qb]
        g_ref, r1_ref, dy_ref, df2_ref, loss_ref, dg_ref, acc = refs[4 + qb:]
        i, c = pl.program_id(0), pl.program_id(1)

        @pl.when((i == 0) & (c == 0))
        def _():
            loss_ref[...] = jnp.zeros_like(loss_ref)
            dg_ref[...] = jnp.zeros_like(dg_ref)

        r = jnp.maximum(_dot(u_ref[...], w1_ref[0]), 0.0)
        r1_ref[...] = r.astype(BF16)
        part = _dot((r * r).astype(BF16), w2_ref[0])

        @pl.when(c == 0)
        def _():
            acc[...] = part

        @pl.when(c > 0)
        def _():
            acc[...] += part

        @pl.when(c == N_CHIPS - 1)
        def _():
            g = g_ref[...]
            fhat, rs = _rms(acc[...])
            h2 = h1_ref[...] + fhat * g
            rows = i * tr + lax.broadcasted_iota(jnp.int32, h2.shape, 0)
            tgt_tile = jnp.concatenate([p[...] for p in t_pieces], axis=0)
            err = jnp.where(rows >= BLOCK, h2 - tgt_tile, 0.0)
            dy = err * (1.0 / D_MODEL)
            dy_ref[...] = dy
            loss_ref[...] += (0.5 / D_MODEL) * jnp.sum(err * err)
            df2, dg = _rms_bwd(fhat, rs, g, dy)
            df2_ref[...] = df2.astype(BF16)
            dg_ref[...] += dg

    row = pl.BlockSpec((tr, D_MODEL), lambda i, c: (i, 0))
    full = lambda a: pl.BlockSpec(a.shape, lambda i, c: (0,) * a.ndim)
    return pl.pallas_call(
        body, name="ffn_fwd", grid=(tp // tr, N_CHIPS),
        in_specs=[row, pl.BlockSpec((1, D_MODEL, FF_CHUNK), lambda i, c: (c, 0, 0)),
                  pl.BlockSpec((1, FF_CHUNK, D_MODEL), lambda i, c: (c, 0, 0)), row] + _seq_specs(tr)
        + [full(g_post_ffn)],
        out_specs=[pl.BlockSpec((tr, FF_CHUNK), lambda i, c: (i, c)), row, row,
                   pl.BlockSpec((1, 1), lambda i, c: (0, 0)), pl.BlockSpec((1, D_MODEL), lambda i, c: (0, 0))],
        out_shape=[jax.ShapeDtypeStruct((tp, D_FF), BF16), jax.ShapeDtypeStruct((tp, D_MODEL), F32),
                   jax.ShapeDtypeStruct((tp, D_MODEL), BF16), jax.ShapeDtypeStruct((1, 1), F32),
                   jax.ShapeDtypeStruct((1, D_MODEL), F32)],
        scratch_shapes=[pltpu.VMEM((tr, D_MODEL), F32)],
        compiler_params=_params("arbitrary", "arbitrary"),
    )(u1, w1, w2, h1, *([tgt] * qb), g_post_ffn)


def _ffn_bwd_data(df2, r1, w1, w2, dy, h1, mix, g_pre_ffn, g_post_mix):
    tp = h1.shape[0]
    tr = _row_tile(tp)

    def body(df2_ref, r1_ref, w1_ref, w2_ref, dy_ref, h1_ref, mix_ref, gf_ref, gm_ref,
             da_ref, dh1_ref, dmix_ref, dgf_ref, dgm_ref, acc):
        i, c = pl.program_id(0), pl.program_id(1)

        @pl.when((i == 0) & (c == 0))
        def _():
            dgf_ref[...] = jnp.zeros_like(dgf_ref)
            dgm_ref[...] = jnp.zeros_like(dgm_ref)

        df = _dot_nt(df2_ref[...], w2_ref[0])
        da = (df * (2.0 * r1_ref[...].astype(F32))).astype(BF16)
        da_ref[...] = da
        part = _dot_nt(da, w1_ref[0])

        @pl.when(c == 0)
        def _():
            acc[...] = part

        @pl.when(c > 0)
        def _():
            acc[...] += part

        @pl.when(c == N_CHIPS - 1)
        def _():
            hhat, rs = _rms(h1_ref[...])
            dx, dgf = _rms_bwd(hhat, rs, gf_ref[...], acc[...])
            dh1 = dy_ref[...] + dx
            dh1_ref[...] = dh1
            dgf_ref[...] += dgf
            mhat, rsm = _rms(mix_ref[...])
            dmix, dgm = _rms_bwd(mhat, rsm, gm_ref[...], dh1)
            dmix_ref[...] = dmix.astype(BF16)
            dgm_ref[...] += dgm

    row = pl.BlockSpec((tr, D_MODEL), lambda i, c: (i, 0))
    chunk = pl.BlockSpec((tr, FF_CHUNK), lambda i, c: (i, c))
    gain = pl.BlockSpec((1, D_MODEL), lambda i, c: (0, 0))
    return pl.pallas_call(
        body, name="ffn_bwd_data", grid=(tp // tr, N_CHIPS),
        in_specs=[row, chunk, pl.BlockSpec((1, D_MODEL, FF_CHUNK), lambda i, c: (c, 0, 0)),
                  pl.BlockSpec((1, FF_CHUNK, D_MODEL), lambda i, c: (c, 0, 0)), row, row, row, gain, gain],
        out_specs=[chunk, row, row, gain, gain],
        out_shape=[jax.ShapeDtypeStruct((tp, D_FF), BF16), jax.ShapeDtypeStruct((tp, D_MODEL), F32),
                   jax.ShapeDtypeStruct((tp, D_MODEL), BF16), jax.ShapeDtypeStruct((1, D_MODEL), F32),
                   jax.ShapeDtypeStruct((1, D_MODEL), F32)],
        scratch_shapes=[pltpu.VMEM((tr, D_MODEL), F32)],
        compiler_params=_params("arbitrary", "arbitrary"),
    )(df2, r1, w1, w2, dy, h1, mix, g_pre_ffn, g_post_mix)


def _ffn_bwd_weights(u1, da1, r1, df2):
    tp = u1.shape[0]
    tr = _wgrad_row_tile(tp)

    def body(u_ref, da_ref, r1_ref, df2_ref, dw1_ref, dw2_ref):
        i = pl.program_id(1)
        r = r1_ref[...].astype(F32)
        p1 = _dot_tn(u_ref[...], da_ref[...])
        p2 = _dot_tn((r * r).astype(BF16), df2_ref[...])

        @pl.when(i == 0)
        def _():
            dw1_ref[0] = p1
            dw2_ref[0] = p2

        @pl.when(i > 0)
        def _():
            dw1_ref[0] += p1
            dw2_ref[0] += p2

    row = pl.BlockSpec((tr, D_MODEL), lambda c, i: (i, 0))
    chunk = pl.BlockSpec((tr, FF_CHUNK), lambda c, i: (i, c))
    return pl.pallas_call(
        body, name="ffn_bwd_weights", grid=(N_CHIPS, tp // tr),
        in_specs=[row, chunk, chunk, row],
        out_specs=[pl.BlockSpec((1, D_MODEL, FF_CHUNK), lambda c, i: (c, 0, 0)),
                   pl.BlockSpec((1, FF_CHUNK, D_MODEL), lambda c, i: (c, 0, 0))],
        out_shape=[jax.ShapeDtypeStruct((N_CHIPS, D_MODEL, FF_CHUNK), F32),
                   jax.ShapeDtypeStruct((N_CHIPS, FF_CHUNK, D_MODEL), F32)],
        compiler_params=_params("parallel", "arbitrary"),
    )(u1, da1, r1, df2)


def _outproj_bwd(dmix, w_out, attn, rec, token):
    tp = dmix.shape[0]
    tr = _wgrad_row_tile(tp)

    def body(dm_ref, w_ref, a_ref, r_ref, _, da_ref, dr_ref, dw_ref):
        i = pl.program_id(0)
        dm = dm_ref[...]
        dcat = _dot_nt(dm, w_ref[...])
        da_ref[...] = dcat[:, :ATTN_WIDTH].astype(BF16)
        dr_ref[...] = dcat[:, ATTN_WIDTH:]
        pa = _dot_tn(a_ref[...], dm)
        pr = _dot_tn(r_ref[...], dm)

        @pl.when(i == 0)
        def _():
            dw_ref[:ATTN_WIDTH] = pa
            dw_ref[ATTN_WIDTH:] = pr

        @pl.when(i > 0)
        def _():
            dw_ref[:ATTN_WIDTH] += pa
            dw_ref[ATTN_WIDTH:] += pr

    row = lambda w: pl.BlockSpec((tr, w), lambda i: (i, 0))
    full = pl.BlockSpec((D_MODEL, D_MODEL), lambda i: (0, 0))
    return pl.pallas_call(
        body, name="outproj_bwd", grid=(tp // tr,),
        in_specs=[row(D_MODEL), full, row(ATTN_WIDTH), row(LRU_WIDTH), pl.BlockSpec(token.shape, lambda i: (0, 0))],
        out_specs=[row(ATTN_WIDTH), row(LRU_WIDTH), full],
        out_shape=[jax.ShapeDtypeStruct((tp, ATTN_WIDTH), BF16), jax.ShapeDtypeStruct((tp, LRU_WIDTH), F32),
                   jax.ShapeDtypeStruct((D_MODEL, D_MODEL), F32)],
        compiler_params=_params("arbitrary"),
    )(dmix, w_out, attn, rec, token)


N_VEC_ROWS = 8


def _lru_bwd(xr, yr, hr, drec, conv_w, conv_b, wa, ba, wx, bx, lam, token):
    tp = xr.shape[0]
    tr = _row_tile(tp)
    qb, nt = tr // BLOCK, tp // tr

    def body(xr_ref, xh_ref, yr_ref, hr_ref, hp_ref, dr_ref, cw_ref, cb_ref, wa_ref, ba_ref, wx_ref, bx_ref, lam_ref, _,
             dxr_ref, dyr_ref, dwa_ref, dwx_ref, vec_ref, g_next, a_next, dxc_next, dsp):
        s = pl.program_id(0)
        t = nt - 1 - s

        @pl.when(s == 0)
        def _():
            g_next[...] = jnp.zeros_like(g_next)
            a_next[...] = jnp.zeros_like(a_next)
            dxc_next[...] = jnp.zeros_like(dxc_next)
            dsp[...] = jnp.zeros_like(dsp)
            dwa_ref[...] = jnp.zeros_like(dwa_ref)
            dwx_ref[...] = jnp.zeros_like(dwx_ref)
            vec_ref[...] = jnp.zeros_like(vec_ref)

        first_tile = t == 0
        cw, cb = cw_ref[...], cb_ref[...]
        lam_v = lam_ref[...]
        sp = _softplus(-lam_v)
        wa_m, ba_v, wx_m, bx_v = wa_ref[...], ba_ref[...], wx_ref[...], bx_ref[...]
        rows = lax.broadcasted_iota(jnp.int32, (BLOCK, LRU_WIDTH), 0)
        col = lambda v: jnp.sum(v, axis=0, keepdims=True)

        g_after, a_after, dxc_after = g_next[0:1], a_next[0:1], dxc_next[...]
        xbs, dgrs, dgis = [], [], []
        vec = [jnp.zeros((1, LRU_WIDTH), F32) for _ in range(N_VEC_ROWS)]
        for i in reversed(range(qb)):
            blk = slice(i * BLOCK, (i + 1) * BLOCK)
            if i == 0:
                x_before = jnp.where(first_tile, 0.0, xh_ref[...])
                h_before = jnp.where(first_tile, 0.0, hp_ref[7:8])
            else:
                x_before = xr_ref[i * BLOCK - 8:i * BLOCK]
                h_before = hr_ref[i * BLOCK - 1:i * BLOCK]
            taps = _conv_taps(xr_ref[blk], x_before)
            xc = cb + sum(cw[k:k + 1] * taps[k] for k in range(4))
            xb, r, ig, a, mult = _lru_gates(xc, wa_m, ba_v, wx_m, bx_v, sp)

            yr_v = yr_ref[blk]
            gl, th = _gelu(yr_v)
            h = hr_ref[blk]
            drec = dr_ref[blk]
            dyr_ref[blk] = (drec * h * _gelu_grad(yr_v, th)).astype(BF16)

            a_up = jnp.where(rows == BLOCK - 1, a_after, pltpu.roll(a, BLOCK - 1, 0))
            g = _scan_rev(a_up, drec * gl, g_after)
            g_after, a_after = g[0:1], a[0:1]

            h_prev = jnp.where(rows == 0, h_before, pltpu.roll(h, 1, 0))
            du, da = g, g * h_prev
            if i == 0:
                real = (t * tr + rows) >= PAD_ROWS
                du, da = jnp.where(real, du, 0.0), jnp.where(real, da, 0.0)
            dmult = du * (ig * xc)
            dig = du * (mult * xc)
            dxc = du * (mult * ig)
            dlog_a = da * a - dmult * (a * a / mult)
            if i == 0:
                dlog_a = jnp.where(real, dlog_a, 0.0)
            dgr = (dlog_a * (-LRU_C * sp)) * (r * (1.0 - r))
            dgi = dig * (ig * (1.0 - ig))
            dgr_b, dgi_b = dgr.astype(BF16), dgi.astype(BF16)
            dxc = dxc + _dot_nt(dgr_b, wa_m) + _dot_nt(dgi_b, wx_m)
            xbs.append(xb)
            dgrs.append(dgr_b)
            dgis.append(dgi_b)

            ext = jnp.concatenate([dxc, dxc_after], axis=0)
            up = [ext[:BLOCK] if j == 0 else pltpu.roll(ext, BLOCK + 8 - j, 0)[:BLOCK] for j in range(4)]
            dxr_ref[blk] = sum(cw[k:k + 1] * up[3 - k] for k in range(4)).astype(BF16)
            dxc_after = dxc[:8]

            for k in range(4):
                vec[k] = vec[k] + col(dxc * taps[k])
            vec[4] = vec[4] + col(dxc)
            vec[5] = vec[5] + col(dgr)
            vec[6] = vec[6] + col(dgi)
            vec[7] = vec[7] + col(dlog_a * (-LRU_C * r))

        g_next[0:1], a_next[0:1], dxc_next[...] = g_after, a_after, dxc_after
        xb_all = jnp.concatenate(xbs, axis=0)
        dwa_ref[...] += _dot_tn(xb_all, jnp.concatenate(dgrs, axis=0))
        dwx_ref[...] += _dot_tn(xb_all, jnp.concatenate(dgis, axis=0))
        for k in range(7):
            vec_ref[k:k + 1] += vec[k]
        dsp[0:1] += vec[7]

        @pl.when(s == nt - 1)
        def _():
            vec_ref[7:8] = dsp[0:1] * (-_sigmoid(-lam_v))

    blk_spec = pl.BlockSpec((tr, LRU_WIDTH), lambda s: (nt - 1 - s, 0))
    rows_before = pl.BlockSpec((8, LRU_WIDTH), lambda s: (jnp.maximum((nt - 1 - s) * (tr // 8) - 1, 0), 0))
    full = lambda a: pl.BlockSpec(a.shape, lambda s: (0,) * a.ndim)
    small = [conv_w, conv_b, wa, ba, wx, bx, lam, token]
    sq = pl.BlockSpec((LRU_WIDTH, LRU_WIDTH), lambda s: (0, 0))
    return pl.pallas_call(
        body, name="lru_bwd", grid=(nt,),
        in_specs=[blk_spec, rows_before, blk_spec, blk_spec, rows_before, blk_spec] + [full(a) for a in small],
        out_specs=[blk_spec, blk_spec, sq, sq, pl.BlockSpec((N_VEC_ROWS, LRU_WIDTH), lambda s: (0, 0))],
        out_shape=[jax.ShapeDtypeStruct((tp, LRU_WIDTH), BF16), jax.ShapeDtypeStruct((tp, LRU_WIDTH), BF16),
                   jax.ShapeDtypeStruct((LRU_WIDTH, LRU_WIDTH), F32), jax.ShapeDtypeStruct((LRU_WIDTH, LRU_WIDTH), F32),
                   jax.ShapeDtypeStruct((N_VEC_ROWS, LRU_WIDTH), F32)],
        scratch_shapes=[pltpu.VMEM((8, LRU_WIDTH), F32)] * 4,
        compiler_params=_params("arbitrary"),
    )(xr, xr, yr, hr, hr, drec, *small)


def _attn_bwd(qkv, dattn, sinks):
    tp = qkv.shape[0]
    tr = _row_tile(tp)
    qb, nt = tr // BLOCK, tp // tr
    n_groups = ATTN_HEADS // GQA_GROUP
    sink_rows, bias = _attn_consts(sinks)

    def body(s_ref, b_ref, q_ref, kp_ref, kc_ref, vp_ref, vc_ref, do_ref, dq_ref, dkv_ref, ex_ref, ds_ref, dsink):
        t = pl.program_id(0)

        @pl.when(t == 0)
        def _():
            dsink[...] = jnp.zeros_like(dsink)

        k_all = jnp.concatenate([kp_ref[...], kc_ref[...]], axis=0)
        v_all = jnp.concatenate([vp_ref[...], vc_ref[...]], axis=0)
        tail = None
        for i in range(qb):
            rows = slice(i * BLOCK, (i + 1) * BLOCK)
            q, do = q_ref[rows], do_ref[rows]
            k2, v2 = k_all[i * BLOCK:(i + 2) * BLOCK], v_all[i * BLOCK:(i + 2) * BLOCK]
            bias_n = _block_bias(b_ref, t, qb, i)
            dqs, dks, dvs = [], [], []
            for g in range(n_groups):
                cols = slice(g * HEAD_DIM, (g + 1) * HEAD_DIM)
                k_g, v_g = k2[:, cols], v2[:, cols]
                qg = _stack_heads(q, g) * jnp.asarray(_QSCALE, BF16)
                dog = _stack_heads(do, g)
                p, ps = _attn_probs_t(k_g, qg, bias_n, s_ref[g:g + 1])
                dpt = _dot_nt(v_g, dog)
                delta = jnp.sum(p * dpt, axis=0, keepdims=True)
                dst = (p * (dpt - delta)).astype(BF16)
                dqs.append(_dot_tn(dst, k_g) * _QSCALE)
                dks.append(_dot(dst, qg))
                dvs.append(_dot(p.astype(BF16), dog))
                dsink[g:g + 1] -= ps * delta
            dq_ref[rows] = _unstack_heads(dqs).astype(BF16)
            dkv = jnp.concatenate(dks + dvs, axis=1)
            if i == 0:
                ex_ref[0] = dkv[:BLOCK]
            else:
                dkv_ref[(i - 1) * BLOCK:i * BLOCK] = (tail + dkv[:BLOCK]).astype(BF16)
            tail = dkv[BLOCK:]
        dkv_ref[(qb - 1) * BLOCK:] = tail.astype(BF16)

        @pl.when(t == nt - 1)
        def _():
            lane = lax.broadcasted_iota(jnp.int32, (1, ATTN_HEADS), 1)
            acc = jnp.zeros((1, ATTN_HEADS), F32)
            for h in range(ATTN_HEADS):
                g, hh = divmod(h, GQA_GROUP)
                acc = acc + jnp.where(lane == h, jnp.sum(dsink[g:g + 1, hh * BLOCK:(hh + 1) * BLOCK]), 0.0)
            ds_ref[...] = acc

    cur = lambda w: pl.BlockSpec((tr, w), lambda t: (t, 0))
    return pl.pallas_call(
        body, name="attn_bwd", grid=(nt,),
        in_specs=[_SINK_SPEC, _BIAS_SPEC, cur(ATTN_WIDTH)] + _kv_specs(tr) + [cur(ATTN_WIDTH)],
        out_specs=[cur(ATTN_WIDTH), cur(2 * KV_WIDTH), pl.BlockSpec((1, BLOCK, 2 * KV_WIDTH), lambda t: (t, 0, 0)),
                   pl.BlockSpec((1, ATTN_HEADS), lambda t: (0, 0))],
        out_shape=[jax.ShapeDtypeStruct((tp, ATTN_WIDTH), BF16), jax.ShapeDtypeStruct((tp, 2 * KV_WIDTH), BF16),
                   jax.ShapeDtypeStruct((nt, BLOCK, 2 * KV_WIDTH), F32), jax.ShapeDtypeStruct((1, ATTN_HEADS), F32)],
        scratch_shapes=[pltpu.VMEM((n_groups, GROUP_ROWS), F32)],
        compiler_params=_params("arbitrary"),
    )(sink_rows, bias, qkv, qkv, qkv, qkv, qkv, dattn)


def _fix_dkv(dkv, dkv_extra):
    tp = dkv.shape[0]
    tr = _row_tile(tp)
    nt, qb = tp // tr, tr // BLOCK
    if nt == 1:
        return dkv

    def body(d_ref, ex_ref, o_ref):
        o_ref[...] = (d_ref[...].astype(F32) + ex_ref[0]).astype(BF16)

    last = pl.BlockSpec((BLOCK, 2 * KV_WIDTH), lambda t: (t * qb + qb - 1, 0))
    return pl.pallas_call(
        body, name="fix_dkv", grid=(nt - 1,),
        in_specs=[last, pl.BlockSpec((1, BLOCK, 2 * KV_WIDTH), lambda t: (t + 1, 0, 0))],
        out_specs=last, out_shape=jax.ShapeDtypeStruct(dkv.shape, dkv.dtype),
        input_output_aliases={0: 0}, compiler_params=_params("parallel"),
    )(dkv, dkv_extra)


def _inproj_wgrad(dq, dkv, dxr, dyr, u0):
    tp = dq.shape[0]
    tr = _wgrad_row_tile(tp)

    def body(dq_ref, dkv_ref, dxr_ref, dyr_ref, u_ref, dw_ref):
        i = pl.program_id(0)
        dz = jnp.concatenate([dq_ref[...], dkv_ref[...], dxr_ref[...], dyr_ref[...]], axis=1)
        pw = _dot_tn(u_ref[...], dz)

        @pl.when(i == 0)
        def _():
            dw_ref[...] = pw

        @pl.when(i > 0)
        def _():
            dw_ref[...] += pw

    row = lambda w: pl.BlockSpec((tr, w), lambda i: (i, 0))
    return pl.pallas_call(
        body, name="inproj_wgrad", grid=(tp // tr,),
        in_specs=[row(ATTN_WIDTH), row(2 * KV_WIDTH), row(LRU_WIDTH), row(LRU_WIDTH), row(D_MODEL)],
        out_specs=pl.BlockSpec((D_MODEL, IN_WIDTH), lambda i: (0, 0)),
        out_shape=jax.ShapeDtypeStruct((D_MODEL, IN_WIDTH), F32),
        compiler_params=_params("arbitrary"),
    )(dq, dkv, dxr, dyr, u0)


def _inproj_dgrad(dq, dkv, dxr, dyr, w_in, head, x, dh1, g, token):
    tp = dq.shape[0]
    tr = _row_tile(tp)
    nt, qb = tp // tr, tr // BLOCK

    def body(*refs):
        dq_ref, dkv_ref, dxr_ref, dyr_ref, w_ref, head_ref = refs[:6]
        pieces = refs[6:6 + qb]
        dh1_ref, g_ref, _, gx_ref, dhead_ref, dg_ref, buf, sems = refs[6 + qb:]
        i = pl.program_id(0)
        slot = i % 2

        def out_copy(step, at):
            return pltpu.make_async_copy(buf.at[at], gx_ref.at[pl.ds(step * tr - BLOCK, tr)], sems.at[at])

        dz = jnp.concatenate([dq_ref[...], dkv_ref[...], dxr_ref[...], dyr_ref[...]], axis=1)
        du = _dot_nt(dz, w_ref[...])
        hhat, rs = _rms(_seq_tile(head_ref[...], pieces, i))
        dx, dg = _rms_bwd(hhat, rs, g_ref[...], du)
        dh0 = dh1_ref[...] + dx

        @pl.when(i >= 3)
        def _():
            out_copy(i - 2, slot).wait()

        buf[slot] = dh0

        @pl.when(i == 0)
        def _():
            dg_ref[...] = dg
            dhead_ref[...] = dh0[:BLOCK]
            if tr > BLOCK:
                first = pltpu.make_async_copy(buf.at[0, pl.ds(BLOCK, tr - BLOCK)], gx_ref.at[pl.ds(0, tr - BLOCK)],
                                              sems.at[0])
                first.start()
                first.wait()

        @pl.when(i >= 1)
        def _():
            dg_ref[...] += dg
            out_copy(i, slot).start()

        @pl.when(i == nt - 1)
        def _():
            if nt >= 3:
                out_copy(nt - 2, (nt - 2) % 2).wait()
            if nt >= 2:
                out_copy(nt - 1, (nt - 1) % 2).wait()

    row = lambda w: pl.BlockSpec((tr, w), lambda i: (i, 0))
    full = lambda shape: pl.BlockSpec(shape, lambda i: (0,) * len(shape))
    return pl.pallas_call(
        body, name="inproj_dgrad", grid=(tp // tr,),
        in_specs=[row(ATTN_WIDTH), row(2 * KV_WIDTH), row(LRU_WIDTH), row(LRU_WIDTH), full(w_in.shape),
                  full(head.shape)] + _seq_specs(tr) + [row(D_MODEL), full(g.shape), full(token.shape)],
        out_specs=[pl.BlockSpec(memory_space=pl.ANY), full((BLOCK, D_MODEL)), full((1, D_MODEL))],
        out_shape=[jax.ShapeDtypeStruct(x.shape, F32), jax.ShapeDtypeStruct((BLOCK, D_MODEL), F32),
                   jax.ShapeDtypeStruct((1, D_MODEL), F32)],
        scratch_shapes=[pltpu.VMEM((2, tr, D_MODEL), F32), pltpu.SemaphoreType.DMA((2,))],
        compiler_params=_params("arbitrary"),
    )(dq, dkv, dxr, dyr, w_in, head, *([x] * qb), dh1, g, token)


def _dense_block_diag(w):
    eye = jnp.eye(LRU_BLOCKS, dtype=w.dtype)
    return (w[:, :, None, :] * eye[:, None, :, None]).reshape(LRU_WIDTH, LRU_WIDTH)


def _diag_blocks(dense):
    d4 = dense.reshape(LRU_BLOCKS, LRU_BLOCK, LRU_BLOCKS, LRU_BLOCK)
    return jnp.stack([d4[n, :, n, :] for n in range(LRU_BLOCKS)])


def _local_step(head, x, tgt, g_pre_mix, w_in, conv_w, conv_b, w_a, b_a, w_x, b_x, lam, sinks, g_post_mix,
                g_pre_ffn, g_post_ffn, late_weights, on_ffn_grads, on_outproj_bwd, on_mixer_grads, token):
    wa = _dense_block_diag(w_a).astype(BF16)
    wx = _dense_block_diag(w_x).astype(BF16)

    u0, qkv, xr, yr = _inproj_fwd(head, x, g_pre_mix, w_in, token)
    attn = _attn_fwd(qkv, sinks)
    hr, rec = _lru_fwd(xr, yr, conv_w, conv_b, wa, b_a, wx, b_x, lam)
    w_out, w1, w2 = late_weights([attn, rec])
    mix, h1, u1 = _outproj_fwd(attn, rec, w_out, head, x, g_post_mix, g_pre_ffn)
    r1, dy, df2, loss, dg_post_ffn = _ffn_fwd(u1, w1, w2, h1, tgt, g_post_ffn)

    da1, dh1, dmix, dg_pre_ffn, dg_post_mix = _ffn_bwd_data(df2, r1, w1, w2, dy, h1, mix, g_pre_ffn, g_post_mix)
    dw1, dw2 = _ffn_bwd_weights(u1, da1, r1, df2)
    token2 = on_ffn_grads(dw1, dw2)
    dattn, drec, dw_out = _outproj_bwd(dmix, w_out, attn, rec, token2)
    token3 = on_outproj_bwd(dattn)
    dxr, dyr, dwa, dwx, vec = _lru_bwd(xr, yr, hr, drec, conv_w, conv_b, wa, b_a, wx, b_x, lam, token3)
    dq, dkv, dkv_extra, dsinks = _attn_bwd(qkv, dattn, sinks)
    dkv = _fix_dkv(dkv, dkv_extra)
    dw_in = _inproj_wgrad(dq, dkv, dxr, dyr, u0)
    token4 = on_mixer_grads(dw_in, dw_out)
    dx, dhead, dg_pre_mix = _inproj_dgrad(dq, dkv, dxr, dyr, w_in, head, x, dh1, g_pre_mix, token4)

    grads = dict(
        g_pre_mix=dg_pre_mix, conv_w=vec[0:4], conv_b=vec[4:5], w_a=_diag_blocks(dwa), b_a=vec[5:6],
        w_x=_diag_blocks(dwx), b_x=vec[6:7], lru_lambda=vec[7:8], attn_sinks=dsinks,
        g_post_mix=dg_post_mix, g_pre_ffn=dg_pre_ffn, g_post_ffn=dg_post_ffn)
    return loss, dx, dhead, grads


HBM = pl.BlockSpec(memory_space=pltpu.HBM)


def _mesh_pos():
    return lax.axis_index("x"), lax.axis_index("y"), lax.axis_index("c")


def _other_chips(x, y):
    return [(1 - x, y), (x, 1 - y), (1 - x, 1 - y)]


def _remote(src, dst, send_sem, recv_sem, to):
    return pltpu.make_async_remote_copy(src_ref=src, dst_ref=dst, send_sem=send_sem, recv_sem=recv_sem,
                                        device_id=to, device_id_type=MESH)


def _gather_weights(shards, lands, tiny, tiny_land):
    nbig = len(shards)

    def body(*refs):
        srcs, tiny_src = refs[:nbig], refs[nbig]
        outs, tiny_out = refs[2 * nbig + 2:3 * nbig + 2], refs[3 * nbig + 2]
        ici_send, ici_recv, d2d_send, d2d_recv, tiny_send, tiny_recv = refs[3 * nbig + 3:]
        x, y, c = _mesh_pos()
        me = 2 * x + y
        chips = _other_chips(x, y)
        sibling = (x, y, 1 - c)
        sends = []
        for w, (src, out) in enumerate(zip(srcs, outs)):
            hr = src.shape[0] // 2
            for j, chip in enumerate(chips):
                k = 3 * w + j
                cp = _remote(src.at[pl.ds(c * hr, hr)], out.at[me, pl.ds(c * hr, hr)],
                             ici_send.at[k], ici_recv.at[k], (*chip, c))
                cp.start()
                sends.append(cp)
        for j, chip in enumerate(chips):
            cp = _remote(tiny_src, tiny_out.at[me], tiny_send.at[j], tiny_recv.at[j], (*chip, c))
            cp.start()
            sends.append(cp)
        for w, (src, out) in enumerate(zip(srcs, outs)):
            hr = src.shape[0] // 2
            for j, (px, py) in enumerate(chips):
                k = 3 * w + j
                landed = out.at[2 * px + py, pl.ds(c * hr, hr)]
                _remote(landed, landed, ici_send.at[k], ici_recv.at[k], sibling).wait_recv()
                cp = _remote(landed, landed, d2d_send.at[k], d2d_recv.at[k], sibling)
                cp.start()
                sends.append(cp)
        for w, (src, out) in enumerate(zip(srcs, outs)):
            hr = src.shape[0] // 2
            for j, (px, py) in enumerate(chips):
                k = 3 * w + j
                other = out.at[2 * px + py, pl.ds((1 - c) * hr, hr)]
                _remote(other, other, d2d_send.at[k], d2d_recv.at[k], sibling).wait_recv()
        for j, (px, py) in enumerate(chips):
            blk = tiny_out.at[2 * px + py]
            _remote(blk, blk, tiny_send.at[j], tiny_recv.at[j], sibling).wait_recv()
        for cp in sends:
            cp.wait_send()

    out_shape = [jax.ShapeDtypeStruct(l.shape, l.dtype) for l in list(lands) + [tiny_land]]
    n = 3 * nbig
    return pl.pallas_call(
        body, name="gather_weights", out_shape=out_shape,
        in_specs=[HBM] * (2 * nbig + 2), out_specs=[HBM] * (nbig + 1),
        input_output_aliases={nbig + 1 + i: i for i in range(nbig + 1)},
        scratch_shapes=[pltpu.SemaphoreType.DMA((n,)),
                        pltpu.SemaphoreType.DMA((n,)), pltpu.SemaphoreType.DMA((n,)), pltpu.SemaphoreType.DMA((n,)),
                        pltpu.SemaphoreType.DMA((3,)), pltpu.SemaphoreType.DMA((3,))],
    )(*shards, tiny, *lands, tiny_land)


def _prep_shard(w, me):
    rows, cols = w.shape
    tr = 256 if rows % 256 == 0 else rows

    def body(me_ref, w_ref, s_ref, l_ref):
        b = w_ref[...].astype(BF16)
        s_ref[...] = b
        l_ref[0] = b

    return pl.pallas_call(
        body, name="prep_shard",
        grid_spec=pltpu.PrefetchScalarGridSpec(
            num_scalar_prefetch=1, grid=(rows // tr,),
            in_specs=[pl.BlockSpec((tr, cols), lambda i, me_ref: (i, 0))],
            out_specs=[pl.BlockSpec((tr, cols), lambda i, me_ref: (i, 0)),
                       pl.BlockSpec((1, tr, cols), lambda i, me_ref: (me_ref[0], i, 0))]),
        out_shape=[jax.ShapeDtypeStruct((rows, cols), BF16), jax.ShapeDtypeStruct((N_CHIPS, rows, cols), BF16)],
        compiler_params=_params("parallel"),
    )(me, w)


def _prep_tiny(tiny, me, slots=N_CHIPS):
    def body(me_ref, t_ref, l_ref):
        l_ref[0] = t_ref[...]

    return pl.pallas_call(
        body, name="prep_tiny",
        grid_spec=pltpu.PrefetchScalarGridSpec(
            num_scalar_prefetch=1, grid=(1,),
            in_specs=[pl.BlockSpec(tiny.shape, lambda i, me_ref: (0, 0))],
            out_specs=pl.BlockSpec((1,) + tiny.shape, lambda i, me_ref: (me_ref[0], 0, 0))),
        out_shape=jax.ShapeDtypeStruct((slots,) + tiny.shape, tiny.dtype),
    )(me, tiny)


N_DEV = 8


def _sibling_exchange(parts, token):
    def body(*refs):
        n = len(parts)
        srcs, outs, send_sems, recv_sems = refs[:n], refs[n + 1:2 * n + 1], refs[2 * n + 1], refs[2 * n + 2]
        x, y, c = _mesh_pos()
        sibling = (x, y, 1 - c)
        cps = []
        for w, (src, out) in enumerate(zip(srcs, outs)):
            hr = src.shape[1] // 2
            cp = _remote(src.at[:, pl.ds((1 - c) * hr, hr)], out, send_sems.at[w], recv_sems.at[w], sibling)
            cp.start()
            cps.append(cp)
        for cp in cps:
            cp.wait()

    n = len(parts)
    return pl.pallas_call(
        body, name="sibling_exchange",
        out_shape=[jax.ShapeDtypeStruct((p.shape[0], p.shape[1] // 2, p.shape[2]), p.dtype) for p in parts],
        in_specs=[HBM] * n + [pl.BlockSpec(memory_space=pl.ANY)], out_specs=[HBM] * n,
        scratch_shapes=[pltpu.SemaphoreType.DMA((n,)), pltpu.SemaphoreType.DMA((n,))],
    )(*parts, token)


def _chip_presum(part, from_sibling, pos):
    _, hr, cols = from_sibling.shape
    tr = 256 if hr % 256 == 0 else hr
    steps = hr // tr

    def body(pos_ref, a_ref, b_ref, o_ref, land_ref):
        s = (a_ref[...] + b_ref[...]).astype(BF16)
        o_ref[...] = s

        @pl.when(pl.program_id(1) == pos_ref[1])
        def _():
            land_ref[...] = s

    return pl.pallas_call(
        body, name="chip_presum",
        grid_spec=pltpu.PrefetchScalarGridSpec(
            num_scalar_prefetch=1, grid=(steps, N_CHIPS),
            in_specs=[pl.BlockSpec((1, tr, cols), lambda i, j, p: (j, p[0] * steps + i, 0)),
                      pl.BlockSpec((1, tr, cols), lambda i, j, p: (j, i, 0))],
            out_specs=[pl.BlockSpec((1, tr, cols), lambda i, j, p: (j, i, 0)),
                       pl.BlockSpec((1, tr, cols), lambda i, j, p: (p[1], p[0] * steps + i, 0))]),
        out_shape=[jax.ShapeDtypeStruct(from_sibling.shape, BF16),
                   jax.ShapeDtypeStruct((N_CHIPS, 2 * hr, cols), BF16)],
        compiler_params=_params("arbitrary", "arbitrary"),
    )(pos, part, from_sibling)


def _scatter_partials(cparts, lands, done_cparts=(), done_lands=()):
    n_new = len(cparts)
    nw = n_new + len(done_cparts)

    def body(*refs):
        srcs = refs[:nw]
        outs = refs[2 * nw:3 * nw]
        own_send, own_recv, ici_send, ici_recv, d2d_send, d2d_recv = refs[3 * nw:]
        x, y, c = _mesh_pos()
        me = 2 * x + y
        chips = _other_chips(x, y)
        sibling = (x, y, 1 - c)
        sends = []
        for w in list(range(n_new, nw)) + list(range(n_new)):
            src, out = srcs[w], outs[w]
            hr = src.shape[1]
            mine = out.at[me, pl.ds(c * hr, hr)]
            cp = _remote(src.at[me], mine, own_send.at[w], own_recv.at[w], sibling)
            cp.start()
            sends.append(cp)
            for j, (px, py) in enumerate(chips):
                if w >= n_new:
                    break
                k = 3 * w + j
                cp = _remote(src.at[2 * px + py], mine, ici_send.at[k], ici_recv.at[k], (px, py, c))
                cp.start()
                sends.append(cp)
        for w in list(range(n_new, nw)) + list(range(n_new)):
            src, out = srcs[w], outs[w]
            hr = src.shape[1]
            for j, (px, py) in enumerate(chips):
                k = 3 * w + j
                landed = out.at[2 * px + py, pl.ds(c * hr, hr)]
                if w < n_new:
                    _remote(landed, landed, ici_send.at[k], ici_recv.at[k], sibling).wait_recv()
                cp = _remote(landed, landed, d2d_send.at[k], d2d_recv.at[k], sibling)
                cp.start()
                sends.append(cp)
        for w, (src, out) in enumerate(zip(srcs, outs)):
            hr = src.shape[1]
            other = out.at[me, pl.ds((1 - c) * hr, hr)]
            _remote(other, other, own_send.at[w], own_recv.at[w], sibling).wait_recv()
            for j, (px, py) in enumerate(chips):
                k = 3 * w + j
                other = out.at[2 * px + py, pl.ds((1 - c) * hr, hr)]
                _remote(other, other, d2d_send.at[k], d2d_recv.at[k], sibling).wait_recv()
        for cp in sends:
            cp.wait_send()

    n = 3 * nw
    dma = pltpu.SemaphoreType.DMA
    every = list(cparts) + list(done_cparts)
    every_lands = list(lands) + list(done_lands)
    return pl.pallas_call(
        body, name="scatter_partials",
        out_shape=[jax.ShapeDtypeStruct(l.shape, l.dtype) for l in every_lands],
        in_specs=[HBM] * (2 * nw), out_specs=[HBM] * nw,
        input_output_aliases={nw + i: i for i in range(nw)},
        scratch_shapes=[dma((nw,)), dma((nw,)), dma((n,)), dma((n,)), dma((n,)), dma((n,))],
    )(*every, *every_lands)


SEM = pl.BlockSpec(memory_space=pltpu.SEMAPHORE)
SPLIT_COPY = pltpu.CompilerParams(has_side_effects=pltpu.SideEffectType.DATAFLOW_SIDE_EFFECTING)


def _hbm(a):
    return pltpu.with_memory_space_constraint(a, pltpu.HBM)


def _gather_copies(srcs, lands, send_sems, recv_sems):
    x, y, c = _mesh_pos()
    me = 2 * x + y
    sends, recvs = [], []
    for w, (src, land) in enumerate(zip(srcs, lands)):
        hr = src.shape[0] // 2
        for j, (px, py) in enumerate(_other_chips(x, y)):
            k = 3 * w + j
            sends.append(_remote(src.at[pl.ds(c * hr, hr)], land.at[me, pl.ds(c * hr, hr)],
                                 send_sems.at[k], recv_sems.at[k], (px, py, c)))
            got = land.at[2 * px + py, pl.ds(c * hr, hr)]
            recvs.append(_remote(got, got, send_sems.at[k], recv_sems.at[k], (px, py, c)))
    return sends, recvs


def _scatter_copies(srcs, lands, send_sems, recv_sems):
    x, y, c = _mesh_pos()
    me = 2 * x + y
    sends, recvs = [], []
    for w, (src, land) in enumerate(zip(srcs, lands)):
        hr = src.shape[1]
        for j, (px, py) in enumerate(_other_chips(x, y)):
            k = 3 * w + j
            sends.append(_remote(src.at[2 * px + py], land.at[me, pl.ds(c * hr, hr)],
                                 send_sems.at[k], recv_sems.at[k], (px, py, c)))
            got = land.at[2 * px + py, pl.ds(c * hr, hr)]
            recvs.append(_remote(got, got, send_sems.at[k], recv_sems.at[k], (px, py, c)))
    return sends, recvs


def _sibling_copies(srcs, lands, send_sems, recv_sems):
    x, y, c = _mesh_pos()
    sibling = (x, y, 1 - c)
    sends, recvs = [], []
    for w, (src, land) in enumerate(zip(srcs, lands)):
        hr = src.shape[1] // 2
        sends.append(_remote(src.at[:, pl.ds((1 - c) * hr, hr)], land, send_sems.at[w], recv_sems.at[w], sibling))
        recvs.append(_remote(land, land, send_sems.at[w], recv_sems.at[w], sibling))
    return sends, recvs


def _all_peers_copies(srcs, lands, send_sems, recv_sems):
    x, y, c = _mesh_pos()
    (src,), (land,) = srcs, lands
    flip = lambda v, bit: 1 - v if bit else v
    sends, recvs = [], []
    for k in range(N_DEV - 1):
        px, py, pc = flip(x, (k + 1) & 4), flip(y, (k + 1) & 2), flip(c, (k + 1) & 1)
        sends.append(_remote(src, land.at[4 * x + 2 * y + c], send_sems.at[k], recv_sems.at[k], (px, py, pc)))
        got = land.at[4 * px + 2 * py + pc]
        recvs.append(_remote(got, got, send_sems.at[k], recv_sems.at[k], (px, py, pc)))
    return sends, recvs


def _split_start(name, copies_of, srcs, land_shapes, n_copies=None):
    n = len(srcs)
    k = 3 * n if n_copies is None else n_copies

    def body(*refs):
        src_refs, land_refs = refs[:n], refs[n:2 * n]
        send_sems, recv_sems = refs[2 * n], refs[2 * n + 1]
        token = refs[-1]
        sends, _ = copies_of(src_refs, land_refs, send_sems, recv_sems)
        for cp in sends:
            cp.start()
        token[...] = jnp.zeros_like(token)

    lands = [_hbm(s) for s in land_shapes]
    dma = pltpu.SemaphoreType.DMA
    res = pl.pallas_call(
        body, name=name,
        out_shape=(dma((k,)), dma((k,)), *[pltpu.HBM(s.shape, s.dtype) for s in srcs],
                   *[pltpu.HBM(s.shape, s.dtype) for s in land_shapes], jax.ShapeDtypeStruct((8, 128), F32)),
        in_specs=[HBM] * (2 * n),
        out_specs=(SEM, SEM, *([HBM] * (2 * n)), pl.BlockSpec(memory_space=pltpu.VMEM)),
        input_output_aliases={i: 2 + i for i in range(2 * n)},
        compiler_params=SPLIT_COPY,
    )(*[_hbm(s) for s in srcs], *lands)
    return res[0], res[1], list(res[2:2 + n]), list(res[2 + n:2 + 2 * n]), res[-1]


def _split_wait(name, copies_of, send_sems, recv_sems, srcs, lands, after):
    n = len(srcs)

    def body(*refs):
        src_refs, land_refs = refs[:n], refs[n:2 * n]
        sends, recvs = copies_of(src_refs, land_refs, refs[2 * n], refs[2 * n + 1])
        for cp in sends:
            cp.wait_send()
        for cp in recvs:
            cp.wait_recv()

    res = pl.pallas_call(
        body, name=name,
        out_shape=tuple(pltpu.HBM(s.shape, s.dtype) for s in list(srcs) + list(lands)),
        in_specs=[HBM] * (2 * n) + [SEM, SEM] + [pl.BlockSpec(memory_space=pl.ANY)] * len(after),
        out_specs=tuple([HBM] * (2 * n)),
        input_output_aliases={i: i for i in range(2 * n)},
        compiler_params=SPLIT_COPY,
    )(*srcs, *lands, send_sems, recv_sems, *after)
    return list(res[:n]), list(res[n:])


def _gather_finish(lands):
    n = len(lands)

    def body(*refs):
        outs = refs[n:2 * n]
        d2d_send, d2d_recv = refs[2 * n:]
        x, y, c = _mesh_pos()
        chips = _other_chips(x, y)
        sibling = (x, y, 1 - c)
        sends = []
        for w, out in enumerate(outs):
            hr = out.shape[1] // 2
            for j, (px, py) in enumerate(chips):
                landed = out.at[2 * px + py, pl.ds(c * hr, hr)]
                cp = _remote(landed, landed, d2d_send.at[3 * w + j], d2d_recv.at[3 * w + j], sibling)
                cp.start()
                sends.append(cp)
        for w, out in enumerate(outs):
            hr = out.shape[1] // 2
            for j, (px, py) in enumerate(chips):
                other = out.at[2 * px + py, pl.ds((1 - c) * hr, hr)]
                _remote(other, other, d2d_send.at[3 * w + j], d2d_recv.at[3 * w + j], sibling).wait_recv()
        for cp in sends:
            cp.wait_send()

    dma = pltpu.SemaphoreType.DMA
    return pl.pallas_call(
        body, name="gather_finish",
        out_shape=[jax.ShapeDtypeStruct(l.shape, l.dtype) for l in lands],
        in_specs=[HBM] * n, out_specs=[HBM] * n,
        input_output_aliases={i: i for i in range(n)},
        scratch_shapes=[dma((3 * n,)), dma((3 * n,))],
    )(*lands)


def _adamw(w, g, m, v):
    m = ADAM_B1 * m + (1.0 - ADAM_B1) * g
    v = ADAM_B2 * v + (1.0 - ADAM_B2) * (g * g)
    m_hat = m / (1.0 - ADAM_B1 ** ADAM_STEP)
    v_hat = v / (1.0 - ADAM_B2 ** ADAM_STEP)
    delta = -ADAM_LR * (m_hat / (jnp.sqrt(v_hat) + ADAM_EPS) + ADAM_WD * w)
    return delta, m, v


def _adamw_big(partials, w, m, v):
    rows, cols = w.shape
    tr = 256

    def body(p_ref, w_ref, m_ref, v_ref, g_ref, d_ref, m2_ref, v2_ref):
        g = ((p_ref[0].astype(F32) + p_ref[1].astype(F32)) + p_ref[2].astype(F32)) + p_ref[3].astype(F32)
        g_ref[...] = g
        d_ref[...], m2_ref[...], v2_ref[...] = _adamw(w_ref[...], g, m_ref[...], v_ref[...])

    blk = pl.BlockSpec((tr, cols), lambda i: (i, 0))
    return pl.pallas_call(
        body, name="adamw_big", grid=(rows // tr,),
        in_specs=[pl.BlockSpec((N_CHIPS, tr, cols), lambda i: (0, i, 0)), blk, blk, blk],
        out_specs=[blk] * 4, out_shape=[jax.ShapeDtypeStruct((rows, cols), F32)] * 4,
        compiler_params=_params("parallel"),
    )(partials, w, m, v)


def _sum_devices(gathered, rows):
    cols = gathered.shape[1]

    def body(g_ref, o_ref):
        acc = g_ref[0:rows]
        for d in range(1, N_DEV):
            acc = acc + g_ref[d * rows:(d + 1) * rows]
        o_ref[...] = acc

    return pl.pallas_call(
        body, name="sum_devices", out_shape=jax.ShapeDtypeStruct((rows, cols), F32),
        in_specs=[pl.BlockSpec(memory_space=pltpu.VMEM)], out_specs=pl.BlockSpec(memory_space=pltpu.VMEM),
        compiler_params=pltpu.CompilerParams(vmem_limit_bytes=VMEM_LIMIT_V7X),
    )(gathered)


def _adamw_small(quads):
    n = len(quads)

    def body(*refs):
        ins, outs = refs[:4 * n], refs[4 * n:]
        for t in range(n):
            w, g, m, v = (r[...] for r in ins[4 * t:4 * t + 4])
            outs[3 * t][...], outs[3 * t + 1][...], outs[3 * t + 2][...] = _adamw(w, g, m, v)

    flat = [a for q in quads for a in q]
    vm = pl.BlockSpec(memory_space=pltpu.VMEM)
    res = pl.pallas_call(
        body, name="adamw_small",
        out_shape=[jax.ShapeDtypeStruct(q[0].shape, F32) for q in quads for _ in range(3)],
        in_specs=[vm] * (4 * n), out_specs=[vm] * (3 * n),
    )(*flat)
    return [tuple(res[3 * t:3 * t + 3]) for t in range(n)]


SMALL_PACK_ROWS = 96
_WEIGHTS = ['meta_tokens', 'g_pre_mix', 'w_in', 'conv_w', 'conv_b', 'w_a', 'b_a', 'w_x', 'b_x', 'lru_lambda',
            'attn_sinks', 'w_out', 'g_post_mix', 'g_pre_ffn', 'w_ff1', 'w_ff2', 'g_post_ffn']
_BIG = ['w_in', 'w_out', 'w_ff1', 'w_ff2']


def _pack_small(dmeta, g):
    z = lambda r, c: jnp.zeros((r, c), F32)
    rows = [
        dmeta,
        g['g_pre_mix'], g['g_post_mix'], g['g_pre_ffn'], g['g_post_ffn'],
        jnp.concatenate([g['conv_w'], z(4, 512)], axis=1),
        jnp.concatenate([g['conv_b'], g['b_a']], axis=1),
        jnp.concatenate([g['b_x'], g['lru_lambda']], axis=1),
        jnp.concatenate([g['attn_sinks'], z(1, D_MODEL - ATTN_HEADS)], axis=1),
        z(5, D_MODEL),
        g['w_a'].reshape(32, D_MODEL), g['w_x'].reshape(32, D_MODEL),
    ]
    return jnp.concatenate(rows, axis=0)


def _unpack_small(s, chip):
    return dict(
        meta_tokens=lax.dynamic_slice(s[0:16], (0, chip * 256), (16, 256)),
        g_pre_mix=s[16:17], g_post_mix=s[17:18], g_pre_ffn=s[18:19], g_post_ffn=s[19:20],
        conv_w=lax.dynamic_slice(s[20:24], (0, chip * 128), (4, 128)).reshape(1, 4, 128),
        conv_b=s[24:25, :512], b_a=s[24:25, 512:], b_x=s[25:26, :512], lru_lambda=s[25:26, 512:],
        attn_sinks=s[26:27, :ATTN_HEADS],
        w_a=s[32:64].reshape(1, LRU_BLOCKS, LRU_BLOCK, LRU_BLOCK),
        w_x=s[64:96].reshape(1, LRU_BLOCKS, LRU_BLOCK, LRU_BLOCK))


def _as2d(a):
    if a.ndim == 2:
        return a
    return a.reshape(-1, a.shape[-1])


def kernel(x, meta_tokens, g_pre_mix, w_in, conv_w, conv_b, w_a, b_a, w_x, b_x, lru_lambda, attn_sinks, w_out, g_post_mix, g_pre_ffn, w_ff1, w_ff2, g_post_ffn, loss_target, m_meta_tokens, m_g_pre_mix, m_w_in, m_conv_w, m_conv_b, m_w_a, m_b_a, m_w_x, m_b_x, m_lru_lambda, m_attn_sinks, m_w_out, m_g_post_mix, m_g_pre_ffn, m_w_ff1, m_w_ff2, m_g_post_ffn, v_meta_tokens, v_g_pre_mix, v_w_in, v_conv_w, v_conv_b, v_w_a, v_b_a, v_w_x, v_b_x, v_lru_lambda, v_attn_sinks, v_w_out, v_g_post_mix, v_g_pre_ffn, v_w_ff1, v_w_ff2, v_g_post_ffn):
    weights = dict(meta_tokens=meta_tokens, g_pre_mix=g_pre_mix, w_in=w_in, conv_w=conv_w, conv_b=conv_b, w_a=w_a,
                   b_a=b_a, w_x=w_x, b_x=b_x, lru_lambda=lru_lambda, attn_sinks=attn_sinks, w_out=w_out,
                   g_post_mix=g_post_mix, g_pre_ffn=g_pre_ffn, w_ff1=w_ff1, w_ff2=w_ff2, g_post_ffn=g_post_ffn)
    mom1 = dict(zip(_WEIGHTS, [m_meta_tokens, m_g_pre_mix, m_w_in, m_conv_w, m_conv_b, m_w_a, m_b_a, m_w_x, m_b_x,
                               m_lru_lambda, m_attn_sinks, m_w_out, m_g_post_mix, m_g_pre_ffn, m_w_ff1, m_w_ff2,
                               m_g_post_ffn]))
    mom2 = dict(zip(_WEIGHTS, [v_meta_tokens, v_g_pre_mix, v_w_in, v_conv_w, v_conv_b, v_w_a, v_b_a, v_w_x, v_b_x,
                               v_lru_lambda, v_attn_sinks, v_w_out, v_g_post_mix, v_g_pre_ffn, v_w_ff1, v_w_ff2,
                               v_g_post_ffn]))
    xi, yi, ci = _mesh_pos()
    chip = 2 * xi + yi

    tiny = jnp.concatenate([meta_tokens, jnp.pad(conv_w[0], ((0, 4), (0, 128)))], axis=0)
    chip_arr = jnp.reshape(chip, (1,)).astype(jnp.int32)
    shards, lands = zip(*[_prep_shard(w[0], chip_arr) for w in (w_in, w_out, w_ff1, w_ff2)])
    g_in, g_tiny = _gather_weights(shards[:1], lands[:1], tiny, _prep_tiny(tiny, chip_arr))
    w_in_full = jnp.concatenate([g_in[j] for j in range(N_CHIPS)], axis=1)
    meta_full = jnp.concatenate([g_tiny[j, :N_META] for j in range(N_CHIPS)], axis=1)
    conv_w_full = jnp.concatenate([g_tiny[j, N_META:N_META + 4, :128] for j in range(N_CHIPS)], axis=1)
    g_send, g_recv, late_thru, late_lands, token = _split_start(
        "gather_late_start", _gather_copies, shards[1:], lands[1:])

    def late_weights(after):
        _, landed = _split_wait("gather_late_wait", _gather_copies, g_send, g_recv, late_thru, late_lands, after)
        g_out, g_f1, g_f2 = _gather_finish(landed)
        return g_out.reshape(D_MODEL, D_MODEL), g_f1, g_f2

    pos = jnp.stack([ci, chip]).astype(jnp.int32)
    ffn = {}


    def on_ffn_grads(dw1, dw2):
        parts = [dw1, dw2]
        lands = [lax.empty((p.shape[0], p.shape[1] // 2, p.shape[2]), p.dtype) for p in parts]
        ffn['sib'] = _split_start("sibling_ffn_start", _sibling_copies, parts, lands, len(parts))
        return ffn['sib'][4]

    def on_outproj_bwd(dattn):
        send, recv, thru, lands, _ = ffn['sib']
        parts, from_sibling = _split_wait("sibling_ffn_wait", _sibling_copies, send, recv, thru, lands, [dattn])
        cparts_ffn, lands_ffn = zip(*[_chip_presum(p, r, pos) for p, r in zip(parts, from_sibling)])
        ffn['send'], ffn['recv'], ffn['thru'], ffn['lands'], token3 = _split_start(
            "scatter_ffn_start", _scatter_copies, cparts_ffn, lands_ffn)
        return token3

    def on_mixer_grads(dw_in, dw_out):
        parts = [jnp.stack([dw_in[:, j * 448:(j + 1) * 448] for j in range(N_CHIPS)]),
                 dw_out.reshape(N_CHIPS, D_MODEL // N_CHIPS, D_MODEL)]
        cparts, lands = zip(*[_chip_presum(p, r, pos) for p, r in zip(parts, _sibling_exchange(parts, pos))])
        ffn['mixer'] = _split_start("scatter_mixer_start", _scatter_copies, cparts, lands)
        return ffn['mixer'][4]

    head = jnp.concatenate([jnp.zeros((PAD_ROWS, D_MODEL), F32), meta_full], axis=0)
    loss, dx, dhead, grads = _local_step(head, x[0], loss_target[0], g_pre_mix, w_in_full, conv_w_full, conv_b, w_a[0],
                                         b_a, w_x[0], b_x, lru_lambda, attn_sinks, g_post_mix, g_pre_ffn, g_post_ffn,
                                         late_weights, on_ffn_grads, on_outproj_bwd, on_mixer_grads, token)
    loss = lax.psum(loss[0, 0], ("x", "y", "c"))
    grad_x = dx[None]

    pack = _pack_small(dhead[PAD_ROWS:], grads)
    dev = jnp.reshape(4 * xi + 2 * yi + ci, (1,)).astype(jnp.int32)
    s_send, s_recv, s_thru, s_lands, token5 = _split_start(
        "gather_small_start", _all_peers_copies, [pack], [_prep_tiny(pack, dev, N_DEV)], N_DEV - 1)

    send, recv, thru, lands, _ = ffn['mixer']
    mixer_cparts, mixer_lands = _split_wait("scatter_mixer_wait", _scatter_copies, send, recv, thru, lands, [token5])
    ffn_cparts, ffn_lands = _split_wait("scatter_ffn_wait", _scatter_copies, ffn['send'], ffn['recv'], ffn['thru'],
                                        ffn['lands'], mixer_lands)
    chip_partials = _scatter_partials([], [], mixer_cparts + ffn_cparts, mixer_lands + ffn_lands)

    g_out_d, delta, new_m, new_v = {}, {}, {}, {}
    for name, part in zip(_BIG, chip_partials):
        shp = weights[name].shape
        res = _adamw_big(part, weights[name][0], mom1[name][0], mom2[name][0])
        g_out_d[name], delta[name], new_m[name], new_v[name] = (r.reshape(shp) for r in res)

    _, (gathered,) = _split_wait("gather_small_wait", _all_peers_copies, s_send, s_recv, s_thru, s_lands,
                                 [g_out_d[n] for n in _BIG])
    small = _unpack_small(_sum_devices(gathered.reshape(N_DEV * SMALL_PACK_ROWS, D_MODEL), SMALL_PACK_ROWS), chip)
    small_names = [n for n in _WEIGHTS if n not in _BIG]
    quads = [(_as2d(weights[n]), _as2d(small[n]), _as2d(mom1[n]), _as2d(mom2[n])) for n in small_names]
    for name, (d, m2, v2) in zip(small_names, _adamw_small(quads)):
        shp = weights[name].shape
        g_out_d[name] = small[name].reshape(shp)
        delta[name], new_m[name], new_v[name] = d.reshape(shp), m2.reshape(shp), v2.reshape(shp)

    return (loss, grad_x, *[g_out_d[n] for n in _WEIGHTS], *[delta[n] for n in _WEIGHTS],
            *[new_m[n] for n in _WEIGHTS], *[new_v[n] for n in _WEIGHTS])
```

```python
import numpy as np
import jax
import jax.numpy as jnp
from jax import lax
from jax.experimental import pallas as pl
from jax.experimental.pallas import tpu as pltpu

F32 = jnp.float32
BF16 = jnp.bfloat16

D_MODEL = 1024
N_META = 16
BLOCK = 128
PAD_ROWS = BLOCK - N_META
HEAD_DIM = 64
ATTN_HEADS = 8
GQA_GROUP = 4
ATTN_WIDTH = 512
KV_WIDTH = 128
QKV_WIDTH = ATTN_WIDTH + 2 * KV_WIDTH
LRU_WIDTH = 512
LRU_BLOCKS = 8
LRU_BLOCK = 64
LRU_C = 8.0
IN_WIDTH = 1792
D_FF = 4096
N_CHIPS = 4
FF_CHUNK = D_FF // N_CHIPS
EPS = 1e-6
NEG = -1e30

ADAM_LR = 0.001
ADAM_B1 = 0.9
ADAM_B2 = 0.999
ADAM_EPS = 1e-08
ADAM_WD = 0.01
ADAM_STEP = 10

VMEM_LIMIT_V7X = 56 * 1024 * 1024
MESH = pl.DeviceIdType.MESH

NT = (((1,), (1,)), ((), ()))
TN = (((0,), (0,)), ((), ()))


def _row_tile(tp):
    return 640 if tp % 640 == 0 else BLOCK


def _wgrad_row_tile(tp):
    return 1664 if tp % 1664 == 0 else _row_tile(tp)


def _params(*sem):
    return pltpu.CompilerParams(dimension_semantics=sem, vmem_limit_bytes=VMEM_LIMIT_V7X)


def _dot(a, b):
    return jnp.dot(a, b, preferred_element_type=F32)


def _dot_nt(a, b):
    return lax.dot_general(a, b, NT, preferred_element_type=F32)


def _dot_tn(a, b):
    return lax.dot_general(a, b, TN, preferred_element_type=F32)


def _rms(x):
    rs = lax.rsqrt(jnp.mean(x * x, axis=-1, keepdims=True) + EPS)
    return x * rs, rs


def _rms_bwd(xhat, rs, g, dy):
    dyg = dy * g
    dx = rs * (dyg - xhat * jnp.mean(dyg * xhat, axis=-1, keepdims=True))
    dg = jnp.sum(dy * xhat, axis=0, keepdims=True)
    return dx, dg


def _gelu(x):
    k = 0.7978845608028654
    t = jnp.tanh(k * (x + 0.044715 * x * x * x))
    return 0.5 * x * (1.0 + t), t


def _gelu_grad(x, t):
    k = 0.7978845608028654
    return 0.5 * (1.0 + t) + 0.5 * x * (1.0 - t * t) * k * (1.0 + 3 * 0.044715 * x * x)


def _sigmoid(x):
    return 0.5 * jnp.tanh(0.5 * x) + 0.5


def _neg_expm1(x):
    series = x * (1.0 + x * 0.5 * (1.0 + x * (1.0 / 3.0) * (1.0 + x * 0.25 * (1.0 + x * 0.2))))
    return -jnp.where(jnp.abs(x) < 0.05, series, jnp.exp(x) - 1.0)


def _softplus(x):
    return jnp.maximum(x, 0.0) + jnp.log1p(jnp.exp(-jnp.abs(x)))


def _seq_specs(tr):
    qb = tr // BLOCK
    return [pl.BlockSpec((BLOCK, D_MODEL), lambda i, *_, s=s: (jnp.maximum(i * qb + s - 1, 0), 0)) for s in range(qb)]


def _seq_tile(head, pieces, i):
    first = jnp.where(i == 0, head, pieces[0][...])
    return jnp.concatenate([first] + [p[...] for p in pieces[1:]], axis=0)


def _inproj_fwd(head, x, g, w_in, token):
    tp = BLOCK + x.shape[0]
    tr = _row_tile(tp)
    qb = tr // BLOCK

    def body(*refs):
        head_ref, pieces = refs[0], refs[1:1 + qb]
        g_ref, w_ref, _, u_ref, qkv_ref, xr_ref, yr_ref = refs[1 + qb:]
        xhat, _ = _rms(_seq_tile(head_ref[...], pieces, pl.program_id(0)))
        u = (xhat * g_ref[...]).astype(BF16)
        u_ref[...] = u
        z = _dot_nt(u, w_ref[...])
        qkv_ref[...] = z[:, :QKV_WIDTH].astype(BF16)
        xr_ref[...] = z[:, QKV_WIDTH:QKV_WIDTH + LRU_WIDTH]
        yr_ref[...] = z[:, QKV_WIDTH + LRU_WIDTH:]

    row = lambda w: pl.BlockSpec((tr, w), lambda i: (i, 0))
    full = lambda a: pl.BlockSpec(a.shape, lambda i: (0,) * a.ndim)
    return pl.pallas_call(
        body, name="inproj_fwd", grid=(tp // tr,),
        in_specs=[full(head)] + _seq_specs(tr) + [full(g), full(w_in), full(token)],
        out_specs=[row(D_MODEL), row(QKV_WIDTH), row(LRU_WIDTH), row(LRU_WIDTH)],
        out_shape=[jax.ShapeDtypeStruct((tp, D_MODEL), BF16), jax.ShapeDtypeStruct((tp, QKV_WIDTH), BF16),
                   jax.ShapeDtypeStruct((tp, LRU_WIDTH), F32), jax.ShapeDtypeStruct((tp, LRU_WIDTH), F32)],
        compiler_params=_params("parallel"),
    )(head, *([x] * qb), g, w_in, token)


GROUP_ROWS = GQA_GROUP * BLOCK


def _attn_bias():
    j = np.arange(2 * BLOCK)[:, None]
    i = np.arange(BLOCK)[None, :]
    band = (j - i >= 1) & (j - i <= BLOCK)
    out = []
    for n in range(3):
        ok = band & ((n - 1) * BLOCK + j >= PAD_ROWS) if n < 2 else band
        out.append(np.tile(np.where(ok, 0.0, NEG).astype(np.float32), (1, GQA_GROUP)))
    return jnp.asarray(np.stack(out))


def _stack_heads(a, g):
    heads = range(GQA_GROUP * g, GQA_GROUP * (g + 1))
    return jnp.concatenate([a[:, h * HEAD_DIM:(h + 1) * HEAD_DIM] for h in heads], axis=0)


def _unstack_heads(groups):
    return jnp.concatenate([p[h * BLOCK:(h + 1) * BLOCK] for p in groups for h in range(GQA_GROUP)], axis=1)


def _attn_probs_t(k_g, qg, bias, sink_row):
    st = _dot_nt(k_g, qg) + bias
    m = jnp.maximum(jnp.max(st, axis=0, keepdims=True), sink_row)
    p = jnp.exp(st - m)
    es = jnp.exp(sink_row - m)
    inv = 1.0 / (jnp.sum(p, axis=0, keepdims=True) + es)
    return p * inv, es * inv


def _attn_consts(sinks):
    return jnp.repeat(sinks.reshape(ATTN_HEADS), BLOCK).reshape(ATTN_HEADS // GQA_GROUP, GROUP_ROWS), _attn_bias()


_SINK_SPEC = pl.BlockSpec((ATTN_HEADS // GQA_GROUP, GROUP_ROWS), lambda n: (0, 0))
_BIAS_SPEC = pl.BlockSpec((3, 2 * BLOCK, GROUP_ROWS), lambda n: (0, 0, 0))
_QSCALE = HEAD_DIM ** -0.5


def _kv_specs(tr):
    qb = tr // BLOCK
    prev = lambda col: pl.BlockSpec((BLOCK, KV_WIDTH), lambda t: (jnp.maximum(t * qb - 1, 0), col))
    cur = lambda col: pl.BlockSpec((tr, KV_WIDTH), lambda t: (t, col))
    return [prev(4), cur(4), prev(5), cur(5)]


def _block_bias(b_ref, t, qb, i):
    return b_ref[2] if i >= 2 else b_ref[jnp.minimum(t * qb + i, 2)]


def _attn_fwd(qkv, sinks):
    tp = qkv.shape[0]
    tr = _row_tile(tp)
    qb = tr // BLOCK
    sink_rows, bias = _attn_consts(sinks)

    def body(s_ref, b_ref, q_ref, kp_ref, kc_ref, vp_ref, vc_ref, o_ref):
        t = pl.program_id(0)
        k_all = jnp.concatenate([kp_ref[...], kc_ref[...]], axis=0)
        v_all = jnp.concatenate([vp_ref[...], vc_ref[...]], axis=0)
        for i in range(qb):
            rows = slice(i * BLOCK, (i + 1) * BLOCK)
            q = q_ref[rows]
            k2, v2 = k_all[i * BLOCK:(i + 2) * BLOCK], v_all[i * BLOCK:(i + 2) * BLOCK]
            bias_n = _block_bias(b_ref, t, qb, i)
            outs = []
            for g in range(ATTN_HEADS // GQA_GROUP):
                cols = slice(g * HEAD_DIM, (g + 1) * HEAD_DIM)
                qg = _stack_heads(q, g) * jnp.asarray(_QSCALE, BF16)
                p, _ = _attn_probs_t(k2[:, cols], qg, bias_n, s_ref[g:g + 1])
                outs.append(_dot_tn(p.astype(BF16), v2[:, cols]))
            o_ref[rows] = _unstack_heads(outs).astype(BF16)

    return pl.pallas_call(
        body, name="attn_fwd", grid=(tp // tr,),
        in_specs=[_SINK_SPEC, _BIAS_SPEC, pl.BlockSpec((tr, ATTN_WIDTH), lambda t: (t, 0))] + _kv_specs(tr),
        out_specs=pl.BlockSpec((tr, ATTN_WIDTH), lambda t: (t, 0)),
        out_shape=jax.ShapeDtypeStruct((tp, ATTN_WIDTH), BF16),
        compiler_params=_params("parallel"),
    )(sink_rows, bias, qkv, qkv, qkv, qkv, qkv)


def _conv_taps(x, halo):
    ext = jnp.concatenate([halo, x], axis=0)
    return [ext[8:] if k == 3 else pltpu.roll(ext, 3 - k, 0)[8:] for k in range(4)]


def _lru_gates(xc, wa, ba, wx, bx, sp):
    xb = xc.astype(BF16)
    r = _sigmoid(_dot(xb, wa) + ba)
    ig = _sigmoid(_dot(xb, wx) + bx)
    log_a = (-LRU_C * sp) * r
    a = jnp.exp(log_a)
    mult = jnp.sqrt(_neg_expm1(2.0 * log_a))
    return xb, r, ig, a, mult


SUBLANES = 8


def _scan_fwd(a, b, h_in):
    n, width = a.shape
    a, b = (v.reshape(n // SUBLANES, SUBLANES, width) for v in (a, b))
    in_group = lax.broadcasted_iota(jnp.int32, a.shape, 1)
    for d in (1, 2, 4):
        keep = in_group >= d
        b = jnp.where(keep, a * pltpu.roll(b, d, 1) + b, b)
        a = jnp.where(keep, a * pltpu.roll(a, d, 1), a)
    a, b = a.reshape(n, width), b.reshape(n, width)
    out, carry = [], h_in
    for g in range(0, n, SUBLANES):
        h = a[g:g + SUBLANES] * carry + b[g:g + SUBLANES]
        out.append(h)
        carry = h[SUBLANES - 1:]
    return jnp.concatenate(out, axis=0)


def _scan_rev(c, b, g_in):
    n, width = c.shape
    c, b = (v.reshape(n // SUBLANES, SUBLANES, width) for v in (c, b))
    in_group = lax.broadcasted_iota(jnp.int32, c.shape, 1)
    for d in (1, 2, 4):
        keep = in_group < SUBLANES - d
        b = jnp.where(keep, b + c * pltpu.roll(b, SUBLANES - d, 1), b)
        c = jnp.where(keep, c * pltpu.roll(c, SUBLANES - d, 1), c)
    c, b = c.reshape(n, width), b.reshape(n, width)
    out, carry = [], g_in
    for g in range(n - SUBLANES, -1, -SUBLANES):
        r = b[g:g + SUBLANES] + c[g:g + SUBLANES] * carry
        out.append(r)
        carry = r[:1]
    return jnp.concatenate(out[::-1], axis=0)


def _lru_fwd(xr, yr, conv_w, conv_b, wa, ba, wx, bx, lam):
    tp = xr.shape[0]
    tr = _row_tile(tp)
    qb = tr // BLOCK

    def body(xr_ref, yr_ref, cw_ref, cb_ref, wa_ref, ba_ref, wx_ref, bx_ref, lam_ref, hr_ref, rec_ref, halo, hprev):
        t = pl.program_id(0)

        @pl.when(t == 0)
        def _():
            halo[...] = jnp.zeros_like(halo)
            hprev[...] = jnp.zeros_like(hprev)

        cw, cb = cw_ref[...], cb_ref[...]
        wa_m, ba_v, wx_m, bx_v = wa_ref[...], ba_ref[...], wx_ref[...], bx_ref[...]
        sp = _softplus(-lam_ref[...])
        before, h_last = halo[...], hprev[0:1]
        for i in range(qb):
            rows = slice(i * BLOCK, (i + 1) * BLOCK)
            x = xr_ref[rows]
            taps = _conv_taps(x, before)
            before = x[BLOCK - 8:]
            xc = cb + sum(cw[k:k + 1] * taps[k] for k in range(4))
            _, _, ig, a, mult = _lru_gates(xc, wa_m, ba_v, wx_m, bx_v, sp)
            u = mult * (ig * xc)
            if i == 0:
                pos = t * tr + lax.broadcasted_iota(jnp.int32, xc.shape, 0)
                u = jnp.where(pos >= PAD_ROWS, u, 0.0)
            h = _scan_fwd(a, u, h_last)
            h_last = h[BLOCK - 1:]
            hr_ref[rows] = h
            gl, _ = _gelu(yr_ref[rows])
            rec_ref[rows] = (gl * h).astype(BF16)
        halo[...] = before
        hprev[0:1] = h_last

    blk = pl.BlockSpec((tr, LRU_WIDTH), lambda t: (t, 0))
    full = lambda a: pl.BlockSpec(a.shape, lambda t: (0,) * a.ndim)
    small = [conv_w, conv_b, wa, ba, wx, bx, lam]
    return pl.pallas_call(
        body, name="lru_fwd", grid=(tp // tr,),
        in_specs=[blk, blk] + [full(a) for a in small],
        out_specs=[blk, blk],
        out_shape=[jax.ShapeDtypeStruct((tp, LRU_WIDTH), F32), jax.ShapeDtypeStruct((tp, LRU_WIDTH), BF16)],
        scratch_shapes=[pltpu.VMEM((8, LRU_WIDTH), F32), pltpu.VMEM((8, LRU_WIDTH), F32)],
        compiler_params=_params("arbitrary"),
    )(xr, yr, *small)


def _outproj_fwd(attn, rec, w_out, head, x, g_post_mix, g_pre_ffn):
    tp = attn.shape[0]
    tr = _row_tile(tp)
    qb = tr // BLOCK

    def body(*refs):
        a_ref, r_ref, w_ref, head_ref = refs[:4]
        pieces = refs[4:4 + qb]
        gm_ref, gf_ref, mix_ref, h1_ref, u1_ref = refs[4 + qb:]
        mix = _dot(a_ref[...], w_ref[:ATTN_WIDTH]) + _dot(r_ref[...], w_ref[ATTN_WIDTH:])
        mix_ref[...] = mix
        mhat, _ = _rms(mix)
        h1 = _seq_tile(head_ref[...], pieces, pl.program_id(0)) + mhat * gm_ref[...]
        h1_ref[...] = h1
        hhat, _ = _rms(h1)
        u1_ref[...] = (hhat * gf_ref[...]).astype(BF16)

    row = lambda w: pl.BlockSpec((tr, w), lambda i: (i, 0))
    full = lambda a: pl.BlockSpec(a.shape, lambda i: (0,) * a.ndim)
    return pl.pallas_call(
        body, name="outproj_fwd", grid=(tp // tr,),
        in_specs=[row(ATTN_WIDTH), row(LRU_WIDTH), full(w_out), full(head)] + _seq_specs(tr)
        + [full(g_post_mix), full(g_pre_ffn)],
        out_specs=[row(D_MODEL), row(D_MODEL), row(D_MODEL)],
        out_shape=[jax.ShapeDtypeStruct((tp, D_MODEL), F32), jax.ShapeDtypeStruct((tp, D_MODEL), F32),
                   jax.ShapeDtypeStruct((tp, D_MODEL), BF16)],
        compiler_params=_params("parallel"),
    )(attn, rec, w_out, head, *([x] * qb), g_post_mix, g_pre_ffn)


def _ffn_fwd(u1, w1, w2, h1, tgt, g_post_ffn):
    tp = h1.shape[0]
    tr = _row_tile(tp)
    qb = tr // BLOCK

    def body(*refs):
        u_ref, w1_ref, w2_ref, h1_ref = refs[:4]
        t_pieces = refs[4:4 + qb]
        g_ref, r1_ref, dy_ref, df2_ref, loss_ref, dg_ref, acc = refs[4 + qb:]
        i, c = pl.program_id(0), pl.program_id(1)

        @pl.when((i == 0) & (c == 0))
        def _():
            loss_ref[...] = jnp.zeros_like(loss_ref)
            dg_ref[...] = jnp.zeros_like(dg_ref)

        r = jnp.maximum(_dot(u_ref[...], w1_ref[0]), 0.0)
        r1_ref[...] = r.astype(BF16)
        part = _dot((r * r).astype(BF16), w2_ref[0])

        @pl.when(c == 0)
        def _():
            acc[...] = part

        @pl.when(c > 0)
        def _():
            acc[...] += part

        @pl.when(c == N_CHIPS - 1)
        def _():
            g = g_ref[...]
            fhat, rs = _rms(acc[...])
            h2 = h1_ref[...] + fhat * g
            rows = i * tr + lax.broadcasted_iota(jnp.int32, h2.shape, 0)
            tgt_tile = jnp.concatenate([p[...] for p in t_pieces], axis=0)
            err = jnp.where(rows >= BLOCK, h2 - tgt_tile, 0.0)
            dy = err * (1.0 / D_MODEL)
            dy_ref[...] = dy
            loss_ref[...] += (0.5 / D_MODEL) * jnp.sum(err * err)
            df2, dg = _rms_bwd(fhat, rs, g, dy)
            df2_ref[...] = df2.astype(BF16)
            dg_ref[...] += dg

    row = pl.BlockSpec((tr, D_MODEL), lambda i, c: (i, 0))
    full = lambda a: pl.BlockSpec(a.shape, lambda i, c: (0,) * a.ndim)
    return pl.pallas_call(
        body, name="ffn_fwd", grid=(tp // tr, N_CHIPS),
        in_specs=[row, pl.BlockSpec((1, D_MODEL, FF_CHUNK), lambda i, c: (c, 0, 0)),
                  pl.BlockSpec((1, FF_CHUNK, D_MODEL), lambda i, c: (c, 0, 0)), row] + _seq_specs(tr)
        + [full(g_post_ffn)],
        out_specs=[pl.BlockSpec((tr, FF_CHUNK), lambda i, c: (i, c)), row, row,
                   pl.BlockSpec((1, 1), lambda i, c: (0, 0)), pl.BlockSpec((1, D_MODEL), lambda i, c: (0, 0))],
        out_shape=[jax.ShapeDtypeStruct((tp, D_FF), BF16), jax.ShapeDtypeStruct((tp, D_MODEL), F32),
                   jax.ShapeDtypeStruct((tp, D_MODEL), BF16), jax.ShapeDtypeStruct((1, 1), F32),
                   jax.ShapeDtypeStruct((1, D_MODEL), F32)],
        scratch_shapes=[pltpu.VMEM((tr, D_MODEL), F32)],
        compiler_params=_params("arbitrary", "arbitrary"),
    )(u1, w1, w2, h1, *([tgt] * qb), g_post_ffn)


def _ffn_bwd_data(df2, r1, w1, w2, dy, h1, mix, g_pre_ffn, g_post_mix):
    tp = h1.shape[0]
    tr = _row_tile(tp)

    def body(df2_ref, r1_ref, w1_ref, w2_ref, dy_ref, h1_ref, mix_ref, gf_ref, gm_ref,
             da_ref, dh1_ref, dmix_ref, dgf_ref, dgm_ref, acc):
        i, c = pl.program_id(0), pl.program_id(1)

        @pl.when((i == 0) & (c == 0))
        def _():
            dgf_ref[...] = jnp.zeros_like(dgf_ref)
            dgm_ref[...] = jnp.zeros_like(dgm_ref)

        df = _dot_nt(df2_ref[...], w2_ref[0])
        da = (df * (2.0 * r1_ref[...].astype(F32))).astype(BF16)
        da_ref[...] = da
        part = _dot_nt(da, w1_ref[0])

        @pl.when(c == 0)
        def _():
            acc[...] = part

        @pl.when(c > 0)
        def _():
            acc[...] += part

        @pl.when(c == N_CHIPS - 1)
        def _():
            hhat, rs = _rms(h1_ref[...])
            dx, dgf = _rms_bwd(hhat, rs, gf_ref[...], acc[...])
            dh1 = dy_ref[...] + dx
            dh1_ref[...] = dh1
            dgf_ref[...] += dgf
            mhat, rsm = _rms(mix_ref[...])
            dmix, dgm = _rms_bwd(mhat, rsm, gm_ref[...], dh1)
            dmix_ref[...] = dmix.astype(BF16)
            dgm_ref[...] += dgm

    row = pl.BlockSpec((tr, D_MODEL), lambda i, c: (i, 0))
    chunk = pl.BlockSpec((tr, FF_CHUNK), lambda i, c: (i, c))
    gain = pl.BlockSpec((1, D_MODEL), lambda i, c: (0, 0))
    return pl.pallas_call(
        body, name="ffn_bwd_data", grid=(tp // tr, N_CHIPS),
        in_specs=[row, chunk, pl.BlockSpec((1, D_MODEL, FF_CHUNK), lambda i, c: (c, 0, 0)),
                  pl.BlockSpec((1, FF_CHUNK, D_MODEL), lambda i, c: (c, 0, 0)), row, row, row, gain, gain],
        out_specs=[chunk, row, row, gain, gain],
        out_shape=[jax.ShapeDtypeStruct((tp, D_FF), BF16), jax.ShapeDtypeStruct((tp, D_MODEL), F32),
                   jax.ShapeDtypeStruct((tp, D_MODEL), BF16), jax.ShapeDtypeStruct((1, D_MODEL), F32),
                   jax.ShapeDtypeStruct((1, D_MODEL), F32)],
        scratch_shapes=[pltpu.VMEM((tr, D_MODEL), F32)],
        compiler_params=_params("arbitrary", "arbitrary"),
    )(df2, r1, w1, w2, dy, h1, mix, g_pre_ffn, g_post_mix)


def _ffn_bwd_weights(u1, da1, r1, df2):
    tp = u1.shape[0]
    tr = _wgrad_row_tile(tp)

    def body(u_ref, da_ref, r1_ref, df2_ref, dw1_ref, dw2_ref):
        i = pl.program_id(1)
        r = r1_ref[...].astype(F32)
        p1 = _dot_tn(u_ref[...], da_ref[...])
        p2 = _dot_tn((r * r).astype(BF16), df2_ref[...])

        @pl.when(i == 0)
        def _():
            dw1_ref[0] = p1
            dw2_ref[0] = p2

        @pl.when(i > 0)
        def _():
            dw1_ref[0] += p1
            dw2_ref[0] += p2

    row = pl.BlockSpec((tr, D_MODEL), lambda c, i: (i, 0))
    chunk = pl.BlockSpec((tr, FF_CHUNK), lambda c, i: (i, c))
    return pl.pallas_call(
        body, name="ffn_bwd_weights", grid=(N_CHIPS, tp // tr),
        in_specs=[row, chunk, chunk, row],
        out_specs=[pl.BlockSpec((1, D_MODEL, FF_CHUNK), lambda c, i: (c, 0, 0)),
                   pl.BlockSpec((1, FF_CHUNK, D_MODEL), lambda c, i: (c, 0, 0))],
        out_shape=[jax.ShapeDtypeStruct((N_CHIPS, D_MODEL, FF_CHUNK), F32),
                   jax.ShapeDtypeStruct((N_CHIPS, FF_CHUNK, D_MODEL), F32)],
        compiler_params=_params("parallel", "arbitrary"),
    )(u1, da1, r1, df2)


def _outproj_bwd(dmix, w_out, attn, rec, token):
    tp = dmix.shape[0]
    tr = _wgrad_row_tile(tp)

    def body(dm_ref, w_ref, a_ref, r_ref, _, da_ref, dr_ref, dw_ref):
        i = pl.program_id(0)
        dm = dm_ref[...]
        dcat = _dot_nt(dm, w_ref[...])
        da_ref[...] = dcat[:, :ATTN_WIDTH].astype(BF16)
        dr_ref[...] = dcat[:, ATTN_WIDTH:]
        pa = _dot_tn(a_ref[...], dm)
        pr = _dot_tn(r_ref[...], dm)

        @pl.when(i == 0)
        def _():
            dw_ref[:ATTN_WIDTH] = pa
            dw_ref[ATTN_WIDTH:] = pr

        @pl.when(i > 0)
        def _():
            dw_ref[:ATTN_WIDTH] += pa
            dw_ref[ATTN_WIDTH:] += pr

    row = lambda w: pl.BlockSpec((tr, w), lambda i: (i, 0))
    full = pl.BlockSpec((D_MODEL, D_MODEL), lambda i: (0, 0))
    return pl.pallas_call(
        body, name="outproj_bwd", grid=(tp // tr,),
        in_specs=[row(D_MODEL), full, row(ATTN_WIDTH), row(LRU_WIDTH), pl.BlockSpec(token.shape, lambda i: (0, 0))],
        out_specs=[row(ATTN_WIDTH), row(LRU_WIDTH), full],
        out_shape=[jax.ShapeDtypeStruct((tp, ATTN_WIDTH), BF16), jax.ShapeDtypeStruct((tp, LRU_WIDTH), F32),
                   jax.ShapeDtypeStruct((D_MODEL, D_MODEL), F32)],
        compiler_params=_params("arbitrary"),
    )(dmix, w_out, attn, rec, token)


N_VEC_ROWS = 8


def _lru_bwd(xr, yr, hr, drec, conv_w, conv_b, wa, ba, wx, bx, lam, token):
    tp = xr.shape[0]
    tr = _row_tile(tp)
    qb, nt = tr // BLOCK, tp // tr

    def body(xr_ref, xh_ref, yr_ref, hr_ref, hp_ref, dr_ref, cw_ref, cb_ref, wa_ref, ba_ref, wx_ref, bx_ref, lam_ref, _,
             dxr_ref, dyr_ref, dwa_ref, dwx_ref, vec_ref, g_next, a_next, dxc_next, dsp):
        s = pl.program_id(0)
        t = nt - 1 - s

        @pl.when(s == 0)
        def _():
            g_next[...] = jnp.zeros_like(g_next)
            a_next[...] = jnp.zeros_like(a_next)
            dxc_next[...] = jnp.zeros_like(dxc_next)
            dsp[...] = jnp.zeros_like(dsp)
            dwa_ref[...] = jnp.zeros_like(dwa_ref)
            dwx_ref[...] = jnp.zeros_like(dwx_ref)
            vec_ref[...] = jnp.zeros_like(vec_ref)

        first_tile = t == 0
        cw, cb = cw_ref[...], cb_ref[...]
        lam_v = lam_ref[...]
        sp = _softplus(-lam_v)
        wa_m, ba_v, wx_m, bx_v = wa_ref[...], ba_ref[...], wx_ref[...], bx_ref[...]
        rows = lax.broadcasted_iota(jnp.int32, (BLOCK, LRU_WIDTH), 0)
        col = lambda v: jnp.sum(v, axis=0, keepdims=True)

        g_after, a_after, dxc_after = g_next[0:1], a_next[0:1], dxc_next[...]
        xbs, dgrs, dgis = [], [], []
        vec = [jnp.zeros((1, LRU_WIDTH), F32) for _ in range(N_VEC_ROWS)]
        for i in reversed(range(qb)):
            blk = slice(i * BLOCK, (i + 1) * BLOCK)
            if i == 0:
                x_before = jnp.where(first_tile, 0.0, xh_ref[...])
                h_before = jnp.where(first_tile, 0.0, hp_ref[7:8])
            else:
                x_before = xr_ref[i * BLOCK - 8:i * BLOCK]
                h_before = hr_ref[i * BLOCK - 1:i * BLOCK]
            taps = _conv_taps(xr_ref[blk], x_before)
            xc = cb + sum(cw[k:k + 1] * taps[k] for k in range(4))
            xb, r, ig, a, mult = _lru_gates(xc, wa_m, ba_v, wx_m, bx_v, sp)

            yr_v = yr_ref[blk]
            gl, th = _gelu(yr_v)
            h = hr_ref[blk]
            drec = dr_ref[blk]
            dyr_ref[blk] = (drec * h * _gelu_grad(yr_v, th)).astype(BF16)

            a_up = jnp.where(rows == BLOCK - 1, a_after, pltpu.roll(a, BLOCK - 1, 0))
            g = _scan_rev(a_up, drec * gl, g_after)
            g_after, a_after = g[0:1], a[0:1]

            h_prev = jnp.where(rows == 0, h_before, pltpu.roll(h, 1, 0))
            du, da = g, g * h_prev
            if i == 0:
                real = (t * tr + rows) >= PAD_ROWS
                du, da = jnp.where(real, du, 0.0), jnp.where(real, da, 0.0)
            dmult = du * (ig * xc)
            dig = du * (mult * xc)
            dxc = du * (mult * ig)
            dlog_a = da * a - dmult * (a * a / mult)
            if i == 0:
                dlog_a = jnp.where(real, dlog_a, 0.0)
            dgr = (dlog_a * (-LRU_C * sp)) * (r * (1.0 - r))
            dgi = dig * (ig * (1.0 - ig))
            dgr_b, dgi_b = dgr.astype(BF16), dgi.astype(BF16)
            dxc = dxc + _dot_nt(dgr_b, wa_m) + _dot_nt(dgi_b, wx_m)
            xbs.append(xb)
            dgrs.append(dgr_b)
            dgis.append(dgi_b)

            ext = jnp.concatenate([dxc, dxc_after], axis=0)
            up = [ext[:BLOCK] if j == 0 else pltpu.roll(ext, BLOCK + 8 - j, 0)[:BLOCK] for j in range(4)]
            dxr_ref[blk] = sum(cw[k:k + 1] * up[3 - k] for k in range(4)).astype(BF16)
            dxc_after = dxc[:8]

            for k in range(4):
                vec[k] = vec[k] + col(dxc * taps[k])
            vec[4] = vec[4] + col(dxc)
            vec[5] = vec[5] + col(dgr)
            vec[6] = vec[6] + col(dgi)
            vec[7] = vec[7] + col(dlog_a * (-LRU_C * r))

        g_next[0:1], a_next[0:1], dxc_next[...] = g_after, a_after, dxc_after
        xb_all = jnp.concatenate(xbs, axis=0)
        dwa_ref[...] += _dot_tn(xb_all, jnp.concatenate(dgrs, axis=0))
        dwx_ref[...] += _dot_tn(xb_all, jnp.concatenate(dgis, axis=0))
        for k in range(7):
            vec_ref[k:k + 1] += vec[k]
        dsp[0:1] += vec[7]

        @pl.when(s == nt - 1)
        def _():
            vec_ref[7:8] = dsp[0:1] * (-_sigmoid(-lam_v))

    blk_spec = pl.BlockSpec((tr, LRU_WIDTH), lambda s: (nt - 1 - s, 0))
    rows_before = pl.BlockSpec((8, LRU_WIDTH), lambda s: (jnp.maximum((nt - 1 - s) * (tr // 8) - 1, 0), 0))
    full = lambda a: pl.BlockSpec(a.shape, lambda s: (0,) * a.ndim)
    small = [conv_w, conv_b, wa, ba, wx, bx, lam, token]
    sq = pl.BlockSpec((LRU_WIDTH, LRU_WIDTH), lambda s: (0, 0))
    return pl.pallas_call(
        body, name="lru_bwd", grid=(nt,),
        in_specs=[blk_spec, rows_before, blk_spec, blk_spec, rows_before, blk_spec] + [full(a) for a in small],
        out_specs=[blk_spec, blk_spec, sq, sq, pl.BlockSpec((N_VEC_ROWS, LRU_WIDTH), lambda s: (0, 0))],
        out_shape=[jax.ShapeDtypeStruct((tp, LRU_WIDTH), BF16), jax.ShapeDtypeStruct((tp, LRU_WIDTH), BF16),
                   jax.ShapeDtypeStruct((LRU_WIDTH, LRU_WIDTH), F32), jax.ShapeDtypeStruct((LRU_WIDTH, LRU_WIDTH), F32),
                   jax.ShapeDtypeStruct((N_VEC_ROWS, LRU_WIDTH), F32)],
        scratch_shapes=[pltpu.VMEM((8, LRU_WIDTH), F32)] * 4,
        compiler_params=_params("arbitrary"),
    )(xr, xr, yr, hr, hr, drec, *small)


def _attn_bwd(qkv, dattn, sinks):
    tp = qkv.shape[0]
    tr = _row_tile(tp)
    qb, nt = tr // BLOCK, tp // tr
    n_groups = ATTN_HEADS // GQA_GROUP
    sink_rows, bias = _attn_consts(sinks)

    def body(s_ref, b_ref, q_ref, kp_ref, kc_ref, vp_ref, vc_ref, do_ref, dq_ref, dkv_ref, ex_ref, ds_ref, dsink):
        t = pl.program_id(0)

        @pl.when(t == 0)
        def _():
            dsink[...] = jnp.zeros_like(dsink)

        k_all = jnp.concatenate([kp_ref[...], kc_ref[...]], axis=0)
        v_all = jnp.concatenate([vp_ref[...], vc_ref[...]], axis=0)
        tail = None
        for i in range(qb):
            rows = slice(i * BLOCK, (i + 1) * BLOCK)
            q, do = q_ref[rows], do_ref[rows]
            k2, v2 = k_all[i * BLOCK:(i + 2) * BLOCK], v_all[i * BLOCK:(i + 2) * BLOCK]
            bias_n = _block_bias(b_ref, t, qb, i)
            dqs, dks, dvs = [], [], []
            for g in range(n_groups):
                cols = slice(g * HEAD_DIM, (g + 1) * HEAD_DIM)
                k_g, v_g = k2[:, cols], v2[:, cols]
                qg = _stack_heads(q, g) * jnp.asarray(_QSCALE, BF16)
                dog = _stack_heads(do, g)
                p, ps = _attn_probs_t(k_g, qg, bias_n, s_ref[g:g + 1])
                dpt = _dot_nt(v_g, dog)
                delta = jnp.sum(p * dpt, axis=0, keepdims=True)
                dst = (p * (dpt - delta)).astype(BF16)
                dqs.append(_dot_tn(dst, k_g) * _QSCALE)
                dks.append(_dot(dst, qg))
                dvs.append(_dot(p.astype(BF16), dog))
                dsink[g:g + 1] -= ps * delta
            dq_ref[rows] = _unstack_heads(dqs).astype(BF16)
            dkv = jnp.concatenate(dks + dvs, axis=1)
            if i == 0:
                ex_ref[0] = dkv[:BLOCK]
            else:
                dkv_ref[(i - 1) * BLOCK:i * BLOCK] = (tail + dkv[:BLOCK]).astype(BF16)
            tail = dkv[BLOCK:]
        dkv_ref[(qb - 1) * BLOCK:] = tail.astype(BF16)

        @pl.when(t == nt - 1)
        def _():
            lane = lax.broadcasted_iota(jnp.int32, (1, ATTN_HEADS), 1)
            acc = jnp.zeros((1, ATTN_HEADS), F32)
            for h in range(ATTN_HEADS):
                g, hh = divmod(h, GQA_GROUP)
                acc = acc + jnp.where(lane == h, jnp.sum(dsink[g:g + 1, hh * BLOCK:(hh + 1) * BLOCK]), 0.0)
            ds_ref[...] = acc

    cur = lambda w: pl.BlockSpec((tr, w), lambda t: (t, 0))
    return pl.pallas_call(
        body, name="attn_bwd", grid=(nt,),
        in_specs=[_SINK_SPEC, _BIAS_SPEC, cur(ATTN_WIDTH)] + _kv_specs(tr) + [cur(ATTN_WIDTH)],
        out_specs=[cur(ATTN_WIDTH), cur(2 * KV_WIDTH), pl.BlockSpec((1, BLOCK, 2 * KV_WIDTH), lambda t: (t, 0, 0)),
                   pl.BlockSpec((1, ATTN_HEADS), lambda t: (0, 0))],
        out_shape=[jax.ShapeDtypeStruct((tp, ATTN_WIDTH), BF16), jax.ShapeDtypeStruct((tp, 2 * KV_WIDTH), BF16),
                   jax.ShapeDtypeStruct((nt, BLOCK, 2 * KV_WIDTH), F32), jax.ShapeDtypeStruct((1, ATTN_HEADS), F32)],
        scratch_shapes=[pltpu.VMEM((n_groups, GROUP_ROWS), F32)],
        compiler_params=_params("arbitrary"),
    )(sink_rows, bias, qkv, qkv, qkv, qkv, qkv, dattn)


def _fix_dkv(dkv, dkv_extra):
    tp = dkv.shape[0]
    tr = _row_tile(tp)
    nt, qb = tp // tr, tr // BLOCK
    if nt == 1:
        return dkv

    def body(d_ref, ex_ref, o_ref):
        o_ref[...] = (d_ref[...].astype(F32) + ex_ref[0]).astype(BF16)

    last = pl.BlockSpec((BLOCK, 2 * KV_WIDTH), lambda t: (t * qb + qb - 1, 0))
    return pl.pallas_call(
        body, name="fix_dkv", grid=(nt - 1,),
        in_specs=[last, pl.BlockSpec((1, BLOCK, 2 * KV_WIDTH), lambda t: (t + 1, 0, 0))],
        out_specs=last, out_shape=jax.ShapeDtypeStruct(dkv.shape, dkv.dtype),
        input_output_aliases={0: 0}, compiler_params=_params("parallel"),
    )(dkv, dkv_extra)


def _inproj_wgrad(dq, dkv, dxr, dyr, u0):
    tp = dq.shape[0]
    tr = _wgrad_row_tile(tp)

    def body(dq_ref, dkv_ref, dxr_ref, dyr_ref, u_ref, dw_ref):
        i = pl.program_id(0)
        dz = jnp.concatenate([dq_ref[...], dkv_ref[...], dxr_ref[...], dyr_ref[...]], axis=1)
        pw = _dot_tn(dz, u_ref[...])

        @pl.when(i == 0)
        def _():
            dw_ref[...] = pw

        @pl.when(i > 0)
        def _():
            dw_ref[...] += pw

    row = lambda w: pl.BlockSpec((tr, w), lambda i: (i, 0))
    return pl.pallas_call(
        body, name="inproj_wgrad", grid=(tp // tr,),
        in_specs=[row(ATTN_WIDTH), row(2 * KV_WIDTH), row(LRU_WIDTH), row(LRU_WIDTH), row(D_MODEL)],
        out_specs=pl.BlockSpec((IN_WIDTH, D_MODEL), lambda i: (0, 0)),
        out_shape=jax.ShapeDtypeStruct((IN_WIDTH, D_MODEL), F32),
        compiler_params=_params("arbitrary"),
    )(dq, dkv, dxr, dyr, u0)


def _inproj_dgrad(dq, dkv, dxr, dyr, w_in, head, x, dh1, g, token):
    tp = dq.shape[0]
    tr = _row_tile(tp)
    nt, qb = tp // tr, tr // BLOCK

    def body(*refs):
        dq_ref, dkv_ref, dxr_ref, dyr_ref, w_ref, head_ref = refs[:6]
        pieces = refs[6:6 + qb]
        dh1_ref, g_ref, _, gx_ref, dhead_ref, dg_ref, buf, sems = refs[6 + qb:]
        i = pl.program_id(0)
        slot = i % 2

        def out_copy(step, at):
            return pltpu.make_async_copy(buf.at[at], gx_ref.at[pl.ds(step * tr - BLOCK, tr)], sems.at[at])

        dz = jnp.concatenate([dq_ref[...], dkv_ref[...], dxr_ref[...], dyr_ref[...]], axis=1)
        du = _dot(dz, w_ref[...])
        hhat, rs = _rms(_seq_tile(head_ref[...], pieces, i))
        dx, dg = _rms_bwd(hhat, rs, g_ref[...], du)
        dh0 = dh1_ref[...] + dx

        @pl.when(i >= 3)
        def _():
            out_copy(i - 2, slot).wait()

        buf[slot] = dh0

        @pl.when(i == 0)
        def _():
            dg_ref[...] = dg
            dhead_ref[...] = dh0[:BLOCK]
            if tr > BLOCK:
                first = pltpu.make_async_copy(buf.at[0, pl.ds(BLOCK, tr - BLOCK)], gx_ref.at[pl.ds(0, tr - BLOCK)],
                                              sems.at[0])
                first.start()
                first.wait()

        @pl.when(i >= 1)
        def _():
            dg_ref[...] += dg
            out_copy(i, slot).start()

        @pl.when(i == nt - 1)
        def _():
            if nt >= 3:
                out_copy(nt - 2, (nt - 2) % 2).wait()
            if nt >= 2:
                out_copy(nt - 1, (nt - 1) % 2).wait()

    row = lambda w: pl.BlockSpec((tr, w), lambda i: (i, 0))
    full = lambda shape: pl.BlockSpec(shape, lambda i: (0,) * len(shape))
    return pl.pallas_call(
        body, name="inproj_dgrad", grid=(tp // tr,),
        in_specs=[row(ATTN_WIDTH), row(2 * KV_WIDTH), row(LRU_WIDTH), row(LRU_WIDTH), full(w_in.shape),
                  full(head.shape)] + _seq_specs(tr) + [row(D_MODEL), full(g.shape), full(token.shape)],
        out_specs=[pl.BlockSpec(memory_space=pl.ANY), full((BLOCK, D_MODEL)), full((1, D_MODEL))],
        out_shape=[jax.ShapeDtypeStruct(x.shape, F32), jax.ShapeDtypeStruct((BLOCK, D_MODEL), F32),
                   jax.ShapeDtypeStruct((1, D_MODEL), F32)],
        scratch_shapes=[pltpu.VMEM((2, tr, D_MODEL), F32), pltpu.SemaphoreType.DMA((2,))],
        compiler_params=_params("arbitrary"),
    )(dq, dkv, dxr, dyr, w_in, head, *([x] * qb), dh1, g, token)


def _dense_block_diag(w):
    eye = jnp.eye(LRU_BLOCKS, dtype=w.dtype)
    return (w[:, :, None, :] * eye[:, None, :, None]).reshape(LRU_WIDTH, LRU_WIDTH)


def _diag_blocks(dense):
    d4 = dense.reshape(LRU_BLOCKS, LRU_BLOCK, LRU_BLOCKS, LRU_BLOCK)
    return jnp.stack([d4[n, :, n, :] for n in range(LRU_BLOCKS)])


def _local_step(head, x, tgt, g_pre_mix, w_in, conv_w, conv_b, w_a, b_a, w_x, b_x, lam, sinks, g_post_mix,
                g_pre_ffn, g_post_ffn, late_weights, on_ffn_grads, on_outproj_bwd, on_mixer_grads, token):
    wa = _dense_block_diag(w_a).astype(BF16)
    wx = _dense_block_diag(w_x).astype(BF16)

    u0, qkv, xr, yr = _inproj_fwd(head, x, g_pre_mix, w_in, token)
    attn = _attn_fwd(qkv, sinks)
    hr, rec = _lru_fwd(xr, yr, conv_w, conv_b, wa, b_a, wx, b_x, lam)
    w_out, w1, w2 = late_weights([attn, rec])
    mix, h1, u1 = _outproj_fwd(attn, rec, w_out, head, x, g_post_mix, g_pre_ffn)
    r1, dy, df2, loss, dg_post_ffn = _ffn_fwd(u1, w1, w2, h1, tgt, g_post_ffn)

    da1, dh1, dmix, dg_pre_ffn, dg_post_mix = _ffn_bwd_data(df2, r1, w1, w2, dy, h1, mix, g_pre_ffn, g_post_mix)
    dw1, dw2 = _ffn_bwd_weights(u1, da1, r1, df2)
    token2 = on_ffn_grads(dw1, dw2)
    dattn, drec, dw_out = _outproj_bwd(dmix, w_out, attn, rec, token2)
    token3 = on_outproj_bwd(dattn)
    dxr, dyr, dwa, dwx, vec = _lru_bwd(xr, yr, hr, drec, conv_w, conv_b, wa, b_a, wx, b_x, lam, token3)
    dq, dkv, dkv_extra, dsinks = _attn_bwd(qkv, dattn, sinks)
    dkv = _fix_dkv(dkv, dkv_extra)
    dw_in = _inproj_wgrad(dq, dkv, dxr, dyr, u0)
    token4 = on_mixer_grads(dw_in, dw_out)
    dx, dhead, dg_pre_mix = _inproj_dgrad(dq, dkv, dxr, dyr, w_in, head, x, dh1, g_pre_mix, token4)

    grads = dict(
        g_pre_mix=dg_pre_mix, conv_w=vec[0:4], conv_b=vec[4:5], w_a=_diag_blocks(dwa), b_a=vec[5:6],
        w_x=_diag_blocks(dwx), b_x=vec[6:7], lru_lambda=vec[7:8], attn_sinks=dsinks,
        g_post_mix=dg_post_mix, g_pre_ffn=dg_pre_ffn, g_post_ffn=dg_post_ffn)
    return loss, dx, dhead, grads


HBM = pl.BlockSpec(memory_space=pltpu.HBM)


def _mesh_pos():
    return lax.axis_index("x"), lax.axis_index("y"), lax.axis_index("c")


def _other_chips(x, y):
    return [(1 - x, y), (x, 1 - y), (1 - x, 1 - y)]


def _remote(src, dst, send_sem, recv_sem, to):
    return pltpu.make_async_remote_copy(src_ref=src, dst_ref=dst, send_sem=send_sem, recv_sem=recv_sem,
                                        device_id=to, device_id_type=MESH)


def _gather_weights(shards, lands, tiny, tiny_land):
    nbig = len(shards)

    def body(*refs):
        srcs, tiny_src = refs[:nbig], refs[nbig]
        outs, tiny_out = refs[2 * nbig + 2:3 * nbig + 2], refs[3 * nbig + 2]
        ici_send, ici_recv, d2d_send, d2d_recv, tiny_send, tiny_recv = refs[3 * nbig + 3:]
        x, y, c = _mesh_pos()
        me = 2 * x + y
        chips = _other_chips(x, y)
        sibling = (x, y, 1 - c)
        sends = []
        for w, (src, out) in enumerate(zip(srcs, outs)):
            hr = src.shape[0] // 2
            for j, chip in enumerate(chips):
                k = 3 * w + j
                cp = _remote(src.at[pl.ds(c * hr, hr)], out.at[me, pl.ds(c * hr, hr)],
                             ici_send.at[k], ici_recv.at[k], (*chip, c))
                cp.start()
                sends.append(cp)
        for j, chip in enumerate(chips):
            cp = _remote(tiny_src, tiny_out.at[me], tiny_send.at[j], tiny_recv.at[j], (*chip, c))
            cp.start()
            sends.append(cp)
        for w, (src, out) in enumerate(zip(srcs, outs)):
            hr = src.shape[0] // 2
            for j, (px, py) in enumerate(chips):
                k = 3 * w + j
                landed = out.at[2 * px + py, pl.ds(c * hr, hr)]
                _remote(landed, landed, ici_send.at[k], ici_recv.at[k], sibling).wait_recv()
                cp = _remote(landed, landed, d2d_send.at[k], d2d_recv.at[k], sibling)
                cp.start()
                sends.append(cp)
        for w, (src, out) in enumerate(zip(srcs, outs)):
            hr = src.shape[0] // 2
            for j, (px, py) in enumerate(chips):
                k = 3 * w + j
                other = out.at[2 * px + py, pl.ds((1 - c) * hr, hr)]
                _remote(other, other, d2d_send.at[k], d2d_recv.at[k], sibling).wait_recv()
        for j, (px, py) in enumerate(chips):
            blk = tiny_out.at[2 * px + py]
            _remote(blk, blk, tiny_send.at[j], tiny_recv.at[j], sibling).wait_recv()
        for cp in sends:
            cp.wait_send()

    out_shape = [jax.ShapeDtypeStruct(l.shape, l.dtype) for l in list(lands) + [tiny_land]]
    n = 3 * nbig
    return pl.pallas_call(
        body, name="gather_weights", out_shape=out_shape,
        in_specs=[HBM] * (2 * nbig + 2), out_specs=[HBM] * (nbig + 1),
        input_output_aliases={nbig + 1 + i: i for i in range(nbig + 1)},
        scratch_shapes=[pltpu.SemaphoreType.DMA((n,)),
                        pltpu.SemaphoreType.DMA((n,)), pltpu.SemaphoreType.DMA((n,)), pltpu.SemaphoreType.DMA((n,)),
                        pltpu.SemaphoreType.DMA((3,)), pltpu.SemaphoreType.DMA((3,))],
    )(*shards, tiny, *lands, tiny_land)


def _prep_shard(w, me):
    rows, cols = w.shape
    tr = 256 if rows % 256 == 0 else rows

    def body(me_ref, w_ref, s_ref, l_ref):
        b = w_ref[...].astype(BF16)
        s_ref[...] = b
        l_ref[0] = b

    return pl.pallas_call(
        body, name="prep_shard",
        grid_spec=pltpu.PrefetchScalarGridSpec(
            num_scalar_prefetch=1, grid=(rows // tr,),
            in_specs=[pl.BlockSpec((tr, cols), lambda i, me_ref: (i, 0))],
            out_specs=[pl.BlockSpec((tr, cols), lambda i, me_ref: (i, 0)),
                       pl.BlockSpec((1, tr, cols), lambda i, me_ref: (me_ref[0], i, 0))]),
        out_shape=[jax.ShapeDtypeStruct((rows, cols), BF16), jax.ShapeDtypeStruct((N_CHIPS, rows, cols), BF16)],
        compiler_params=_params("parallel"),
    )(me, w)


def _prep_tiny(tiny, me, slots=N_CHIPS):
    def body(me_ref, t_ref, l_ref):
        l_ref[0] = t_ref[...]

    return pl.pallas_call(
        body, name="prep_tiny",
        grid_spec=pltpu.PrefetchScalarGridSpec(
            num_scalar_prefetch=1, grid=(1,),
            in_specs=[pl.BlockSpec(tiny.shape, lambda i, me_ref: (0, 0))],
            out_specs=pl.BlockSpec((1,) + tiny.shape, lambda i, me_ref: (me_ref[0], 0, 0))),
        out_shape=jax.ShapeDtypeStruct((slots,) + tiny.shape, tiny.dtype),
    )(me, tiny)


N_DEV = 8


def _sibling_exchange(parts, token):
    def body(*refs):
        n = len(parts)
        srcs, outs, send_sems, recv_sems = refs[:n], refs[n + 1:2 * n + 1], refs[2 * n + 1], refs[2 * n + 2]
        x, y, c = _mesh_pos()
        sibling = (x, y, 1 - c)
        cps = []
        for w, (src, out) in enumerate(zip(srcs, outs)):
            hr = src.shape[1] // 2
            cp = _remote(src.at[:, pl.ds((1 - c) * hr, hr)], out, send_sems.at[w], recv_sems.at[w], sibling)
            cp.start()
            cps.append(cp)
        for cp in cps:
            cp.wait()

    n = len(parts)
    return pl.pallas_call(
        body, name="sibling_exchange",
        out_shape=[jax.ShapeDtypeStruct((p.shape[0], p.shape[1] // 2, p.shape[2]), p.dtype) for p in parts],
        in_specs=[HBM] * n + [pl.BlockSpec(memory_space=pl.ANY)], out_specs=[HBM] * n,
        scratch_shapes=[pltpu.SemaphoreType.DMA((n,)), pltpu.SemaphoreType.DMA((n,))],
    )(*parts, token)


def _chip_presum(part, from_sibling, pos):
    _, hr, cols = from_sibling.shape
    tr = 256 if hr % 256 == 0 else hr
    steps = hr // tr

    def body(pos_ref, a_ref, b_ref, o_ref, land_ref):
        s = (a_ref[...] + b_ref[...]).astype(BF16)
        o_ref[...] = s

        @pl.when(pl.program_id(1) == pos_ref[1])
        def _():
            land_ref[...] = s

    return pl.pallas_call(
        body, name="chip_presum",
        grid_spec=pltpu.PrefetchScalarGridSpec(
            num_scalar_prefetch=1, grid=(steps, N_CHIPS),
            in_specs=[pl.BlockSpec((1, tr, cols), lambda i, j, p: (j, p[0] * steps + i, 0)),
                      pl.BlockSpec((1, tr, cols), lambda i, j, p: (j, i, 0))],
            out_specs=[pl.BlockSpec((1, tr, cols), lambda i, j, p: (j, i, 0)),
                       pl.BlockSpec((1, tr, cols), lambda i, j, p: (p[1], p[0] * steps + i, 0))]),
        out_shape=[jax.ShapeDtypeStruct(from_sibling.shape, BF16),
                   jax.ShapeDtypeStruct((N_CHIPS, 2 * hr, cols), BF16)],
        compiler_params=_params("arbitrary", "arbitrary"),
    )(pos, part, from_sibling)


def _scatter_partials(cparts, lands, done_cparts=(), done_lands=()):
    n_new = len(cparts)
    nw = n_new + len(done_cparts)

    def body(*refs):
        srcs = refs[:nw]
        outs = refs[2 * nw:3 * nw]
        own_send, own_recv, ici_send, ici_recv, d2d_send, d2d_recv = refs[3 * nw:]
        x, y, c = _mesh_pos()
        me = 2 * x + y
        chips = _other_chips(x, y)
        sibling = (x, y, 1 - c)
        sends = []
        for w in list(range(n_new, nw)) + list(range(n_new)):
            src, out = srcs[w], outs[w]
            hr = src.shape[1]
            mine = out.at[me, pl.ds(c * hr, hr)]
            cp = _remote(src.at[me], mine, own_send.at[w], own_recv.at[w], sibling)
            cp.start()
            sends.append(cp)
            for j, (px, py) in enumerate(chips):
                if w >= n_new:
                    break
                k = 3 * w + j
                cp = _remote(src.at[2 * px + py], mine, ici_send.at[k], ici_recv.at[k], (px, py, c))
                cp.start()
                sends.append(cp)
        for w in list(range(n_new, nw)) + list(range(n_new)):
            src, out = srcs[w], outs[w]
            hr = src.shape[1]
            for j, (px, py) in enumerate(chips):
                k = 3 * w + j
                landed = out.at[2 * px + py, pl.ds(c * hr, hr)]
                if w < n_new:
                    _remote(landed, landed, ici_send.at[k], ici_recv.at[k], sibling).wait_recv()
                cp = _remote(landed, landed, d2d_send.at[k], d2d_recv.at[k], sibling)
                cp.start()
                sends.append(cp)
        for w, (src, out) in enumerate(zip(srcs, outs)):
            hr = src.shape[1]
            other = out.at[me, pl.ds((1 - c) * hr, hr)]
            _remote(other, other, own_send.at[w], own_recv.at[w], sibling).wait_recv()
            for j, (px, py) in enumerate(chips):
                k = 3 * w + j
                other = out.at[2 * px + py, pl.ds((1 - c) * hr, hr)]
                _remote(other, other, d2d_send.at[k], d2d_recv.at[k], sibling).wait_recv()
        for cp in sends:
            cp.wait_send()

    n = 3 * nw
    dma = pltpu.SemaphoreType.DMA
    every = list(cparts) + list(done_cparts)
    every_lands = list(lands) + list(done_lands)
    return pl.pallas_call(
        body, name="scatter_partials",
        out_shape=[jax.ShapeDtypeStruct(l.shape, l.dtype) for l in every_lands],
        in_specs=[HBM] * (2 * nw), out_specs=[HBM] * nw,
        input_output_aliases={nw + i: i for i in range(nw)},
        scratch_shapes=[dma((nw,)), dma((nw,)), dma((n,)), dma((n,)), dma((n,)), dma((n,))],
    )(*every, *every_lands)


SEM = pl.BlockSpec(memory_space=pltpu.SEMAPHORE)
SPLIT_COPY = pltpu.CompilerParams(has_side_effects=pltpu.SideEffectType.DATAFLOW_SIDE_EFFECTING)


def _hbm(a):
    return pltpu.with_memory_space_constraint(a, pltpu.HBM)


def _gather_copies(srcs, lands, send_sems, recv_sems):
    x, y, c = _mesh_pos()
    me = 2 * x + y
    sends, recvs = [], []
    for w, (src, land) in enumerate(zip(srcs, lands)):
        hr = src.shape[0] // 2
        for j, (px, py) in enumerate(_other_chips(x, y)):
            k = 3 * w + j
            sends.append(_remote(src.at[pl.ds(c * hr, hr)], land.at[me, pl.ds(c * hr, hr)],
                                 send_sems.at[k], recv_sems.at[k], (px, py, c)))
            got = land.at[2 * px + py, pl.ds(c * hr, hr)]
            recvs.append(_remote(got, got, send_sems.at[k], recv_sems.at[k], (px, py, c)))
    return sends, recvs


def _scatter_copies(srcs, lands, send_sems, recv_sems):
    x, y, c = _mesh_pos()
    me = 2 * x + y
    sends, recvs = [], []
    for w, (src, land) in enumerate(zip(srcs, lands)):
        hr = src.shape[1]
        for j, (px, py) in enumerate(_other_chips(x, y)):
            k = 3 * w + j
            sends.append(_remote(src.at[2 * px + py], land.at[me, pl.ds(c * hr, hr)],
                                 send_sems.at[k], recv_sems.at[k], (px, py, c)))
            got = land.at[2 * px + py, pl.ds(c * hr, hr)]
            recvs.append(_remote(got, got, send_sems.at[k], recv_sems.at[k], (px, py, c)))
    return sends, recvs


def _sibling_copies(srcs, lands, send_sems, recv_sems):
    x, y, c = _mesh_pos()
    sibling = (x, y, 1 - c)
    sends, recvs = [], []
    for w, (src, land) in enumerate(zip(srcs, lands)):
        hr = src.shape[1] // 2
        sends.append(_remote(src.at[:, pl.ds((1 - c) * hr, hr)], land, send_sems.at[w], recv_sems.at[w], sibling))
        recvs.append(_remote(land, land, send_sems.at[w], recv_sems.at[w], sibling))
    return sends, recvs


def _all_peers_copies(srcs, lands, send_sems, recv_sems):
    x, y, c = _mesh_pos()
    (src,), (land,) = srcs, lands
    flip = lambda v, bit: 1 - v if bit else v
    sends, recvs = [], []
    for k in range(N_DEV - 1):
        px, py, pc = flip(x, (k + 1) & 4), flip(y, (k + 1) & 2), flip(c, (k + 1) & 1)
        sends.append(_remote(src, land.at[4 * x + 2 * y + c], send_sems.at[k], recv_sems.at[k], (px, py, pc)))
        got = land.at[4 * px + 2 * py + pc]
        recvs.append(_remote(got, got, send_sems.at[k], recv_sems.at[k], (px, py, pc)))
    return sends, recvs


def _split_start(name, copies_of, srcs, land_shapes, n_copies=None):
    n = len(srcs)
    k = 3 * n if n_copies is None else n_copies

    def body(*refs):
        src_refs, land_refs = refs[:n], refs[n:2 * n]
        send_sems, recv_sems = refs[2 * n], refs[2 * n + 1]
        token = refs[-1]
        sends, _ = copies_of(src_refs, land_refs, send_sems, recv_sems)
        for cp in sends:
            cp.start()
        token[...] = jnp.zeros_like(token)

    lands = [_hbm(s) for s in land_shapes]
    dma = pltpu.SemaphoreType.DMA
    res = pl.pallas_call(
        body, name=name,
        out_shape=(dma((k,)), dma((k,)), *[pltpu.HBM(s.shape, s.dtype) for s in srcs],
                   *[pltpu.HBM(s.shape, s.dtype) for s in land_shapes], jax.ShapeDtypeStruct((8, 128), F32)),
        in_specs=[HBM] * (2 * n),
        out_specs=(SEM, SEM, *([HBM] * (2 * n)), pl.BlockSpec(memory_space=pltpu.VMEM)),
        input_output_aliases={i: 2 + i for i in range(2 * n)},
        compiler_params=SPLIT_COPY,
    )(*[_hbm(s) for s in srcs], *lands)
    return res[0], res[1], list(res[2:2 + n]), list(res[2 + n:2 + 2 * n]), res[-1]


def _split_wait(name, copies_of, send_sems, recv_sems, srcs, lands, after):
    n = len(srcs)

    def body(*refs):
        src_refs, land_refs = refs[:n], refs[n:2 * n]
        sends, recvs = copies_of(src_refs, land_refs, refs[2 * n], refs[2 * n + 1])
        for cp in sends:
            cp.wait_send()
        for cp in recvs:
            cp.wait_recv()

    res = pl.pallas_call(
        body, name=name,
        out_shape=tuple(pltpu.HBM(s.shape, s.dtype) for s in list(srcs) + list(lands)),
        in_specs=[HBM] * (2 * n) + [SEM, SEM] + [pl.BlockSpec(memory_space=pl.ANY)] * len(after),
        out_specs=tuple([HBM] * (2 * n)),
        input_output_aliases={i: i for i in range(2 * n)},
        compiler_params=SPLIT_COPY,
    )(*srcs, *lands, send_sems, recv_sems, *after)
    return list(res[:n]), list(res[n:])


def _gather_finish(lands):
    n = len(lands)

    def body(*refs):
        outs = refs[n:2 * n]
        d2d_send, d2d_recv = refs[2 * n:]
        x, y, c = _mesh_pos()
        chips = _other_chips(x, y)
        sibling = (x, y, 1 - c)
        sends = []
        for w, out in enumerate(outs):
            hr = out.shape[1] // 2
            for j, (px, py) in enumerate(chips):
                landed = out.at[2 * px + py, pl.ds(c * hr, hr)]
                cp = _remote(landed, landed, d2d_send.at[3 * w + j], d2d_recv.at[3 * w + j], sibling)
                cp.start()
                sends.append(cp)
        for w, out in enumerate(outs):
            hr = out.shape[1] // 2
            for j, (px, py) in enumerate(chips):
                other = out.at[2 * px + py, pl.ds((1 - c) * hr, hr)]
                _remote(other, other, d2d_send.at[3 * w + j], d2d_recv.at[3 * w + j], sibling).wait_recv()
        for cp in sends:
            cp.wait_send()

    dma = pltpu.SemaphoreType.DMA
    return pl.pallas_call(
        body, name="gather_finish",
        out_shape=[jax.ShapeDtypeStruct(l.shape, l.dtype) for l in lands],
        in_specs=[HBM] * n, out_specs=[HBM] * n,
        input_output_aliases={i: i for i in range(n)},
        scratch_shapes=[dma((3 * n,)), dma((3 * n,))],
    )(*lands)


def _adamw(w, g, m, v):
    m = ADAM_B1 * m + (1.0 - ADAM_B1) * g
    v = ADAM_B2 * v + (1.0 - ADAM_B2) * (g * g)
    m_hat = m / (1.0 - ADAM_B1 ** ADAM_STEP)
    v_hat = v / (1.0 - ADAM_B2 ** ADAM_STEP)
    delta = -ADAM_LR * (m_hat / (jnp.sqrt(v_hat) + ADAM_EPS) + ADAM_WD * w)
    return delta, m, v


def _adamw_big(partials, w, m, v):
    rows, cols = w.shape
    tr = 256 if rows % 256 == 0 else rows

    def body(p_ref, w_ref, m_ref, v_ref, g_ref, d_ref, m2_ref, v2_ref):
        g = ((p_ref[0].astype(F32) + p_ref[1].astype(F32)) + p_ref[2].astype(F32)) + p_ref[3].astype(F32)
        g_ref[...] = g
        d_ref[...], m2_ref[...], v2_ref[...] = _adamw(w_ref[...], g, m_ref[...], v_ref[...])

    blk = pl.BlockSpec((tr, cols), lambda i: (i, 0))
    return pl.pallas_call(
        body, name="adamw_big", grid=(rows // tr,),
        in_specs=[pl.BlockSpec((N_CHIPS, tr, cols), lambda i: (0, i, 0)), blk, blk, blk],
        out_specs=[blk] * 4, out_shape=[jax.ShapeDtypeStruct((rows, cols), F32)] * 4,
        compiler_params=_params("parallel"),
    )(partials, w, m, v)


def _sum_devices(gathered, rows):
    cols = gathered.shape[1]

    def body(g_ref, o_ref):
        acc = g_ref[0:rows]
        for d in range(1, N_DEV):
            acc = acc + g_ref[d * rows:(d + 1) * rows]
        o_ref[...] = acc

    return pl.pallas_call(
        body, name="sum_devices", out_shape=jax.ShapeDtypeStruct((rows, cols), F32),
        in_specs=[pl.BlockSpec(memory_space=pltpu.VMEM)], out_specs=pl.BlockSpec(memory_space=pltpu.VMEM),
        compiler_params=pltpu.CompilerParams(vmem_limit_bytes=VMEM_LIMIT_V7X),
    )(gathered)


def _adamw_small(quads):
    n = len(quads)

    def body(*refs):
        ins, outs = refs[:4 * n], refs[4 * n:]
        for t in range(n):
            w, g, m, v = (r[...] for r in ins[4 * t:4 * t + 4])
            outs[3 * t][...], outs[3 * t + 1][...], outs[3 * t + 2][...] = _adamw(w, g, m, v)

    flat = [a for q in quads for a in q]
    vm = pl.BlockSpec(memory_space=pltpu.VMEM)
    res = pl.pallas_call(
        body, name="adamw_small",
        out_shape=[jax.ShapeDtypeStruct(q[0].shape, F32) for q in quads for _ in range(3)],
        in_specs=[vm] * (4 * n), out_specs=[vm] * (3 * n),
    )(*flat)
    return [tuple(res[3 * t:3 * t + 3]) for t in range(n)]


SMALL_PACK_ROWS = 96
_WEIGHTS = ['meta_tokens', 'g_pre_mix', 'w_in', 'conv_w', 'conv_b', 'w_a', 'b_a', 'w_x', 'b_x', 'lru_lambda',
            'attn_sinks', 'w_out', 'g_post_mix', 'g_pre_ffn', 'w_ff1', 'w_ff2', 'g_post_ffn']
_BIG = ['w_in', 'w_out', 'w_ff1', 'w_ff2']


def _pack_small(dmeta, g, loss):
    z = lambda r, c: jnp.zeros((r, c), F32)
    rows = [
        dmeta,
        g['g_pre_mix'], g['g_post_mix'], g['g_pre_ffn'], g['g_post_ffn'],
        jnp.concatenate([g['conv_w'], z(4, 512)], axis=1),
        jnp.concatenate([g['conv_b'], g['b_a']], axis=1),
        jnp.concatenate([g['b_x'], g['lru_lambda']], axis=1),
        jnp.concatenate([g['attn_sinks'], z(1, D_MODEL - ATTN_HEADS)], axis=1),
        jnp.concatenate([loss, z(1, D_MODEL - 1)], axis=1),
        z(4, D_MODEL),
        g['w_a'].reshape(32, D_MODEL), g['w_x'].reshape(32, D_MODEL),
    ]
    return jnp.concatenate(rows, axis=0)


def _unpack_small(s, chip):
    return dict(
        meta_tokens=lax.dynamic_slice(s[0:16], (0, chip * 256), (16, 256)),
        g_pre_mix=s[16:17], g_post_mix=s[17:18], g_pre_ffn=s[18:19], g_post_ffn=s[19:20],
        conv_w=lax.dynamic_slice(s[20:24], (0, chip * 128), (4, 128)).reshape(1, 4, 128),
        conv_b=s[24:25, :512], b_a=s[24:25, 512:], b_x=s[25:26, :512], lru_lambda=s[25:26, 512:],
        attn_sinks=s[26:27, :ATTN_HEADS], loss=s[27, 0],
        w_a=s[32:64].reshape(1, LRU_BLOCKS, LRU_BLOCK, LRU_BLOCK),
        w_x=s[64:96].reshape(1, LRU_BLOCKS, LRU_BLOCK, LRU_BLOCK))


def _as2d(a):
    if a.ndim == 2:
        return a
    return a.reshape(-1, a.shape[-1])


def kernel(x, meta_tokens, g_pre_mix, w_in, conv_w, conv_b, w_a, b_a, w_x, b_x, lru_lambda, attn_sinks, w_out, g_post_mix, g_pre_ffn, w_ff1, w_ff2, g_post_ffn, loss_target, m_meta_tokens, m_g_pre_mix, m_w_in, m_conv_w, m_conv_b, m_w_a, m_b_a, m_w_x, m_b_x, m_lru_lambda, m_attn_sinks, m_w_out, m_g_post_mix, m_g_pre_ffn, m_w_ff1, m_w_ff2, m_g_post_ffn, v_meta_tokens, v_g_pre_mix, v_w_in, v_conv_w, v_conv_b, v_w_a, v_b_a, v_w_x, v_b_x, v_lru_lambda, v_attn_sinks, v_w_out, v_g_post_mix, v_g_pre_ffn, v_w_ff1, v_w_ff2, v_g_post_ffn):
    weights = dict(meta_tokens=meta_tokens, g_pre_mix=g_pre_mix, w_in=w_in, conv_w=conv_w, conv_b=conv_b, w_a=w_a,
                   b_a=b_a, w_x=w_x, b_x=b_x, lru_lambda=lru_lambda, attn_sinks=attn_sinks, w_out=w_out,
                   g_post_mix=g_post_mix, g_pre_ffn=g_pre_ffn, w_ff1=w_ff1, w_ff2=w_ff2, g_post_ffn=g_post_ffn)
    mom1 = dict(zip(_WEIGHTS, [m_meta_tokens, m_g_pre_mix, m_w_in, m_conv_w, m_conv_b, m_w_a, m_b_a, m_w_x, m_b_x,
                               m_lru_lambda, m_attn_sinks, m_w_out, m_g_post_mix, m_g_pre_ffn, m_w_ff1, m_w_ff2,
                               m_g_post_ffn]))
    mom2 = dict(zip(_WEIGHTS, [v_meta_tokens, v_g_pre_mix, v_w_in, v_conv_w, v_conv_b, v_w_a, v_b_a, v_w_x, v_b_x,
                               v_lru_lambda, v_attn_sinks, v_w_out, v_g_post_mix, v_g_pre_ffn, v_w_ff1, v_w_ff2,
                               v_g_post_ffn]))
    xi, yi, ci = _mesh_pos()
    chip = 2 * xi + yi

    tiny = jnp.concatenate([meta_tokens, jnp.pad(conv_w[0], ((0, 4), (0, 128)))], axis=0)
    chip_arr = jnp.reshape(chip, (1,)).astype(jnp.int32)
    big2d = lambda a, name: a[0].T if name == 'w_in' else a[0]
    shards, lands = zip(*[_prep_shard(big2d(weights[n], n), chip_arr) for n in _BIG])
    g_in, g_tiny = _gather_weights(shards[:1], lands[:1], tiny, _prep_tiny(tiny, chip_arr))
    w_in_full = g_in.reshape(IN_WIDTH, D_MODEL)
    meta_full = jnp.concatenate([g_tiny[j, :N_META] for j in range(N_CHIPS)], axis=1)
    conv_w_full = jnp.concatenate([g_tiny[j, N_META:N_META + 4, :128] for j in range(N_CHIPS)], axis=1)
    g_send, g_recv, late_thru, late_lands, token = _split_start(
        "gather_late_start", _gather_copies, shards[1:], lands[1:])

    def late_weights(after):
        _, landed = _split_wait("gather_late_wait", _gather_copies, g_send, g_recv, late_thru, late_lands, after)
        g_out, g_f1, g_f2 = _gather_finish(landed)
        return g_out.reshape(D_MODEL, D_MODEL), g_f1, g_f2

    pos = jnp.stack([ci, chip]).astype(jnp.int32)
    ffn = {}


    def on_ffn_grads(dw1, dw2):
        parts = [dw1, dw2]
        lands = [lax.empty((p.shape[0], p.shape[1] // 2, p.shape[2]), p.dtype) for p in parts]
        ffn['sib'] = _split_start("sibling_ffn_start", _sibling_copies, parts, lands, len(parts))
        return ffn['sib'][4]

    def on_outproj_bwd(dattn):
        send, recv, thru, lands, _ = ffn['sib']
        parts, from_sibling = _split_wait("sibling_ffn_wait", _sibling_copies, send, recv, thru, lands, [dattn])
        cparts_ffn, lands_ffn = zip(*[_chip_presum(p, r, pos) for p, r in zip(parts, from_sibling)])
        ffn['send'], ffn['recv'], ffn['thru'], ffn['lands'], token3 = _split_start(
            "scatter_ffn_start", _scatter_copies, cparts_ffn, lands_ffn)
        return token3

    def on_mixer_grads(dw_in, dw_out):
        parts = [dw_in.reshape(N_CHIPS, IN_WIDTH // N_CHIPS, D_MODEL),
                 dw_out.reshape(N_CHIPS, D_MODEL // N_CHIPS, D_MODEL)]
        cparts, lands = zip(*[_chip_presum(p, r, pos) for p, r in zip(parts, _sibling_exchange(parts, pos))])
        ffn['mixer'] = _split_start("scatter_mixer_start", _scatter_copies, cparts, lands)
        return ffn['mixer'][4]

    head = jnp.concatenate([jnp.zeros((PAD_ROWS, D_MODEL), F32), meta_full], axis=0)
    loss, dx, dhead, grads = _local_step(head, x[0], loss_target[0], g_pre_mix, w_in_full, conv_w_full, conv_b, w_a[0],
                                         b_a, w_x[0], b_x, lru_lambda, attn_sinks, g_post_mix, g_pre_ffn, g_post_ffn,
                                         late_weights, on_ffn_grads, on_outproj_bwd, on_mixer_grads, token)
    grad_x = dx[None]

    pack = _pack_small(dhead[PAD_ROWS:], grads, loss)
    dev = jnp.reshape(4 * xi + 2 * yi + ci, (1,)).astype(jnp.int32)
    s_send, s_recv, s_thru, s_lands, token5 = _split_start(
        "gather_small_start", _all_peers_copies, [pack], [_prep_tiny(pack, dev, N_DEV)], N_DEV - 1)

    send, recv, thru, lands, _ = ffn['mixer']
    mixer_cparts, mixer_lands = _split_wait("scatter_mixer_wait", _scatter_copies, send, recv, thru, lands, [token5])
    ffn_cparts, ffn_lands = _split_wait("scatter_ffn_wait", _scatter_copies, ffn['send'], ffn['recv'], ffn['thru'],
                                        ffn['lands'], mixer_lands)
    chip_partials = _scatter_partials([], [], mixer_cparts + ffn_cparts, mixer_lands + ffn_lands)

    g_out_d, delta, new_m, new_v = {}, {}, {}, {}
    for name, part in zip(_BIG, chip_partials):
        shp = weights[name].shape
        res = _adamw_big(part, big2d(weights[name], name), big2d(mom1[name], name), big2d(mom2[name], name))
        g_out_d[name], delta[name], new_m[name], new_v[name] = (big2d(r[None], name).reshape(shp) for r in res)

    _, (gathered,) = _split_wait("gather_small_wait", _all_peers_copies, s_send, s_recv, s_thru, s_lands,
                                 [g_out_d[n] for n in _BIG])
    small = _unpack_small(_sum_devices(gathered.reshape(N_DEV * SMALL_PACK_ROWS, D_MODEL), SMALL_PACK_ROWS), chip)
    loss = small['loss']
    small_names = [n for n in _WEIGHTS if n not in _BIG]
    quads = [(_as2d(weights[n]), _as2d(small[n]), _as2d(mom1[n]), _as2d(mom2[n])) for n in small_names]
    for name, (d, m2, v2) in zip(small_names, _adamw_small(quads)):
        shp = weights[name].shape
        g_out_d[name] = small[name].reshape(shp)
        delta[name], new_m[name], new_v[name] = d.reshape(shp), m2.reshape(shp), v2.reshape(shp)

    return (loss, grad_x, *[g_out_d[n] for n in _WEIGHTS], *[delta[n] for n in _WEIGHTS],
            *[new_m[n] for n in _WEIGHTS], *[new_v[n] for n in _WEIGHTS])
```

```python
import numpy as np
import jax
import jax.numpy as jnp
from jax import lax
from jax.experimental import pallas as pl
from jax.experimental.pallas import tpu as pltpu

F32 = jnp.float32
BF16 = jnp.bfloat16

D_MODEL = 1024
N_META = 16
BLOCK = 128
PAD_ROWS = BLOCK - N_META
HEAD_DIM = 64
ATTN_HEADS = 8
GQA_GROUP = 4
ATTN_WIDTH = 512
KV_WIDTH = 128
QKV_WIDTH = ATTN_WIDTH + 2 * KV_WIDTH
LRU_WIDTH = 512
LRU_BLOCKS = 8
LRU_BLOCK = 64
LRU_C = 8.0
IN_WIDTH = 1792
D_FF = 4096
N_CHIPS = 4
FF_CHUNK = D_FF // N_CHIPS
EPS = 1e-6
NEG = -1e30

ADAM_LR = 0.001
ADAM_B1 = 0.9
ADAM_B2 = 0.999
ADAM_EPS = 1e-08
ADAM_WD = 0.01
ADAM_STEP = 10

VMEM_LIMIT_V7X = 56 * 1024 * 1024
MESH = pl.DeviceIdType.MESH

NT = (((1,), (1,)), ((), ()))
TN = (((0,), (0,)), ((), ()))


def _row_tile(tp):
    return 640 if tp % 640 == 0 else BLOCK


def _wgrad_row_tile(tp):
    return 1664 if tp % 1664 == 0 else _row_tile(tp)


def _params(*sem):
    return pltpu.CompilerParams(dimension_semantics=sem, vmem_limit_bytes=VMEM_LIMIT_V7X)


def _dot(a, b):
    return jnp.dot(a, b, preferred_element_type=F32)


def _dot_nt(a, b):
    return lax.dot_general(a, b, NT, preferred_element_type=F32)


def _dot_tn(a, b):
    return lax.dot_general(a, b, TN, preferred_element_type=F32)


def _rms(x):
    rs = lax.rsqrt(jnp.mean(x * x, axis=-1, keepdims=True) + EPS)
    return x * rs, rs


def _rms_bwd(xhat, rs, g, dy):
    dyg = dy * g
    dx = rs * (dyg - xhat * jnp.mean(dyg * xhat, axis=-1, keepdims=True))
    dg = jnp.sum(dy * xhat, axis=0, keepdims=True)
    return dx, dg


def _gelu(x):
    k = 0.7978845608028654
    t = jnp.tanh(x * (k + (k * 0.044715) * (x * x)))
    return (0.5 * x) * (1.0 + t), t


def _gelu_grad(x, t):
    k = 0.7978845608028654
    return 0.5 * (1.0 + t) + 0.5 * x * (1.0 - t * t) * k * (1.0 + 3 * 0.044715 * x * x)


def _sigmoid(x):
    return 0.5 * jnp.tanh(0.5 * x) + 0.5


def _one_minus_exp2(y):
    t = jnp.tanh(y)
    return (-2.0 * t) / (1.0 - t)


def _softplus(x):
    return jnp.maximum(x, 0.0) + jnp.log1p(jnp.exp(-jnp.abs(x)))


def _seq_specs(tr, delay=0):
    qb = tr // BLOCK
    tile = lambda i: jnp.maximum(i - delay, 0)
    return [pl.BlockSpec((BLOCK, D_MODEL), lambda i, *_, s=s: (jnp.maximum(tile(i) * qb + s - 1, 0), 0))
            for s in range(qb)]


def _seq_tile(head, pieces, i):
    first = jnp.where(i == 0, head, pieces[0][...])
    return jnp.concatenate([first] + [p[...] for p in pieces[1:]], axis=0)


def _inproj_fwd(head, x, g, w_in, token):
    tp = BLOCK + x.shape[0]
    tr = _row_tile(tp)
    qb = tr // BLOCK

    def body(*refs):
        head_ref, pieces = refs[0], refs[1:1 + qb]
        g_ref, w_ref, _, u_ref, qkv_ref, xr_ref, yr_ref = refs[1 + qb:]
        xhat, _ = _rms(_seq_tile(head_ref[...], pieces, pl.program_id(0)))
        u = (xhat * g_ref[...]).astype(BF16)
        u_ref[...] = u
        z = _dot_nt(u, w_ref[...])
        qkv_ref[...] = z[:, :QKV_WIDTH].astype(BF16)
        xr_ref[...] = z[:, QKV_WIDTH:QKV_WIDTH + LRU_WIDTH]
        yr_ref[...] = z[:, QKV_WIDTH + LRU_WIDTH:]

    row = lambda w: pl.BlockSpec((tr, w), lambda i: (i, 0))
    full = lambda a: pl.BlockSpec(a.shape, lambda i: (0,) * a.ndim)
    return pl.pallas_call(
        body, name="inproj_fwd", grid=(tp // tr,),
        in_specs=[full(head)] + _seq_specs(tr) + [full(g), full(w_in), full(token)],
        out_specs=[row(D_MODEL), row(QKV_WIDTH), row(LRU_WIDTH), row(LRU_WIDTH)],
        out_shape=[jax.ShapeDtypeStruct((tp, D_MODEL), BF16), jax.ShapeDtypeStruct((tp, QKV_WIDTH), BF16),
                   jax.ShapeDtypeStruct((tp, LRU_WIDTH), F32), jax.ShapeDtypeStruct((tp, LRU_WIDTH), F32)],
        compiler_params=_params("parallel"),
    )(head, *([x] * qb), g, w_in, token)


GROUP_ROWS = GQA_GROUP * BLOCK


def _attn_bias():
    j = np.arange(2 * BLOCK)[:, None]
    i = np.arange(BLOCK)[None, :]
    band = (j - i >= 1) & (j - i <= BLOCK)
    out = []
    for n in range(3):
        ok = band & ((n - 1) * BLOCK + j >= PAD_ROWS) if n < 2 else band
        out.append(np.tile(np.where(ok, 0.0, NEG).astype(np.float32), (1, GQA_GROUP)))
    return jnp.asarray(np.stack(out))


def _stack_heads(a, g):
    heads = range(GQA_GROUP * g, GQA_GROUP * (g + 1))
    return jnp.concatenate([a[:, h * HEAD_DIM:(h + 1) * HEAD_DIM] for h in heads], axis=0)


def _unstack_heads(groups):
    return jnp.concatenate([p[h * BLOCK:(h + 1) * BLOCK] for p in groups for h in range(GQA_GROUP)], axis=1)


def _attn_probs_t(k_g, qg, bias, sink_row):
    st = _dot_nt(k_g, qg) + bias
    m = jnp.maximum(jnp.max(st, axis=0, keepdims=True), sink_row)
    p = jnp.exp(st - m)
    es = jnp.exp(sink_row - m)
    inv = 1.0 / (jnp.sum(p, axis=0, keepdims=True) + es)
    return p * inv, es * inv


def _attn_consts(sinks):
    return jnp.repeat(sinks.reshape(ATTN_HEADS), BLOCK).reshape(ATTN_HEADS // GQA_GROUP, GROUP_ROWS), _attn_bias()


_SINK_SPEC = pl.BlockSpec((ATTN_HEADS // GQA_GROUP, GROUP_ROWS), lambda n: (0, 0))
_BIAS_SPEC = pl.BlockSpec((3, 2 * BLOCK, GROUP_ROWS), lambda n: (0, 0, 0))
_QSCALE = HEAD_DIM ** -0.5


def _kv_specs(tr):
    qb = tr // BLOCK
    prev = lambda col: pl.BlockSpec((BLOCK, KV_WIDTH), lambda t: (jnp.maximum(t * qb - 1, 0), col))
    cur = lambda col: pl.BlockSpec((tr, KV_WIDTH), lambda t: (t, col))
    return [prev(4), cur(4), prev(5), cur(5)]


def _block_bias(b_ref, t, qb, i):
    return b_ref[2] if i >= 2 else b_ref[jnp.minimum(t * qb + i, 2)]


def _attn_fwd(qkv, sinks):
    tp = qkv.shape[0]
    tr = _row_tile(tp)
    qb = tr // BLOCK
    sink_rows, bias = _attn_consts(sinks)

    def body(s_ref, b_ref, q_ref, kp_ref, kc_ref, vp_ref, vc_ref, o_ref):
        t = pl.program_id(0)
        k_all = jnp.concatenate([kp_ref[...], kc_ref[...]], axis=0)
        v_all = jnp.concatenate([vp_ref[...], vc_ref[...]], axis=0)
        for i in range(qb):
            rows = slice(i * BLOCK, (i + 1) * BLOCK)
            q = q_ref[rows]
            k2, v2 = k_all[i * BLOCK:(i + 2) * BLOCK], v_all[i * BLOCK:(i + 2) * BLOCK]
            bias_n = _block_bias(b_ref, t, qb, i)
            outs = []
            for g in range(ATTN_HEADS // GQA_GROUP):
                cols = slice(g * HEAD_DIM, (g + 1) * HEAD_DIM)
                qg = _stack_heads(q, g) * jnp.asarray(_QSCALE, BF16)
                p, _ = _attn_probs_t(k2[:, cols], qg, bias_n, s_ref[g:g + 1])
                outs.append(_dot_tn(p.astype(BF16), v2[:, cols]))
            o_ref[rows] = _unstack_heads(outs).astype(BF16)

    return pl.pallas_call(
        body, name="attn_fwd", grid=(tp // tr,),
        in_specs=[_SINK_SPEC, _BIAS_SPEC, pl.BlockSpec((tr, ATTN_WIDTH), lambda t: (t, 0))] + _kv_specs(tr),
        out_specs=pl.BlockSpec((tr, ATTN_WIDTH), lambda t: (t, 0)),
        out_shape=jax.ShapeDtypeStruct((tp, ATTN_WIDTH), BF16),
        compiler_params=_params("parallel"),
    )(sink_rows, bias, qkv, qkv, qkv, qkv, qkv)


def _conv_taps(x, halo):
    ext = jnp.concatenate([halo, x], axis=0)
    return [ext[8:] if k == 3 else pltpu.roll(ext, 3 - k, 0)[8:] for k in range(4)]


def _lru_gates(xc, wa, ba, wx, bx, sp):
    xb = xc.astype(BF16)
    r = _sigmoid(_dot(xb, wa) + ba)
    ig = _sigmoid(_dot(xb, wx) + bx)
    log_a = (-LRU_C * sp) * r
    a = jnp.exp(log_a)
    mult = jnp.sqrt(_one_minus_exp2(log_a))
    return xb, r, ig, a, mult


SUBLANES = 8


def _scan_fwd(a, b, h_in):
    n, width = a.shape
    a, b = (v.reshape(n // SUBLANES, SUBLANES, width) for v in (a, b))
    in_group = lax.broadcasted_iota(jnp.int32, a.shape, 1)
    for d in (1, 2, 4):
        keep = in_group >= d
        b = jnp.where(keep, a * pltpu.roll(b, d, 1) + b, b)
        a = jnp.where(keep, a * pltpu.roll(a, d, 1), a)
    a, b = a.reshape(n, width), b.reshape(n, width)
    out, carry = [], h_in
    for g in range(0, n, SUBLANES):
        h = a[g:g + SUBLANES] * carry + b[g:g + SUBLANES]
        out.append(h)
        carry = h[SUBLANES - 1:]
    return jnp.concatenate(out, axis=0)


def _scan_rev(c, b, g_in):
    n, width = c.shape
    c, b = (v.reshape(n // SUBLANES, SUBLANES, width) for v in (c, b))
    in_group = lax.broadcasted_iota(jnp.int32, c.shape, 1)
    for d in (1, 2, 4):
        keep = in_group < SUBLANES - d
        b = jnp.where(keep, b + c * pltpu.roll(b, SUBLANES - d, 1), b)
        c = jnp.where(keep, c * pltpu.roll(c, SUBLANES - d, 1), c)
    c, b = c.reshape(n, width), b.reshape(n, width)
    out, carry = [], g_in
    for g in range(n - SUBLANES, -1, -SUBLANES):
        r = b[g:g + SUBLANES] + c[g:g + SUBLANES] * carry
        out.append(r)
        carry = r[:1]
    return jnp.concatenate(out[::-1], axis=0)


def _lru_fwd(xr, yr, conv_w, conv_b, wa, ba, wx, bx, lam):
    tp = xr.shape[0]
    tr = _row_tile(tp)
    qb = tr // BLOCK

    def body(xr_ref, yr_ref, cw_ref, cb_ref, wa_ref, ba_ref, wx_ref, bx_ref, lam_ref, hr_ref, rec_ref, halo, hprev):
        t = pl.program_id(0)

        @pl.when(t == 0)
        def _():
            halo[...] = jnp.zeros_like(halo)
            hprev[...] = jnp.zeros_like(hprev)

        cw, cb = cw_ref[...], cb_ref[...]
        wa_m, ba_v, wx_m, bx_v = wa_ref[...], ba_ref[...], wx_ref[...], bx_ref[...]
        sp = _softplus(-lam_ref[...])
        before, h_last = halo[...], hprev[0:1]
        for i in range(qb):
            rows = slice(i * BLOCK, (i + 1) * BLOCK)
            x = xr_ref[rows]
            taps = _conv_taps(x, before)
            before = x[BLOCK - 8:]
            xc = cb + sum(cw[k:k + 1] * taps[k] for k in range(4))
            _, _, ig, a, mult = _lru_gates(xc, wa_m, ba_v, wx_m, bx_v, sp)
            u = mult * (ig * xc)
            if i == 0:
                pos = t * tr + lax.broadcasted_iota(jnp.int32, xc.shape, 0)
                u = jnp.where(pos >= PAD_ROWS, u, 0.0)
            h = _scan_fwd(a, u, h_last)
            h_last = h[BLOCK - 1:]
            hr_ref[rows] = h
            gl, _ = _gelu(yr_ref[rows])
            rec_ref[rows] = (gl * h).astype(BF16)
        halo[...] = before
        hprev[0:1] = h_last

    blk = pl.BlockSpec((tr, LRU_WIDTH), lambda t: (t, 0))
    full = lambda a: pl.BlockSpec(a.shape, lambda t: (0,) * a.ndim)
    small = [conv_w, conv_b, wa, ba, wx, bx, lam]
    return pl.pallas_call(
        body, name="lru_fwd", grid=(tp // tr,),
        in_specs=[blk, blk] + [full(a) for a in small],
        out_specs=[blk, blk],
        out_shape=[jax.ShapeDtypeStruct((tp, LRU_WIDTH), F32), jax.ShapeDtypeStruct((tp, LRU_WIDTH), BF16)],
        scratch_shapes=[pltpu.VMEM((8, LRU_WIDTH), F32), pltpu.VMEM((8, LRU_WIDTH), F32)],
        compiler_params=_params("arbitrary"),
    )(xr, yr, *small)


def _outproj_fwd(attn, rec, w_out, head, x, g_post_mix, g_pre_ffn):
    tp = attn.shape[0]
    tr = _row_tile(tp)
    qb = tr // BLOCK

    def body(*refs):
        a_ref, r_ref, w_ref, head_ref = refs[:4]
        pieces = refs[4:4 + qb]
        gm_ref, gf_ref, mix_ref, h1_ref, u1_ref = refs[4 + qb:]
        mix = _dot(a_ref[...], w_ref[:ATTN_WIDTH]) + _dot(r_ref[...], w_ref[ATTN_WIDTH:])
        mix_ref[...] = mix
        mhat, _ = _rms(mix)
        h1 = _seq_tile(head_ref[...], pieces, pl.program_id(0)) + mhat * gm_ref[...]
        h1_ref[...] = h1
        hhat, _ = _rms(h1)
        u1_ref[...] = (hhat * gf_ref[...]).astype(BF16)

    row = lambda w: pl.BlockSpec((tr, w), lambda i: (i, 0))
    full = lambda a: pl.BlockSpec(a.shape, lambda i: (0,) * a.ndim)
    return pl.pallas_call(
        body, name="outproj_fwd", grid=(tp // tr,),
        in_specs=[row(ATTN_WIDTH), row(LRU_WIDTH), full(w_out), full(head)] + _seq_specs(tr)
        + [full(g_post_mix), full(g_pre_ffn)],
        out_specs=[row(D_MODEL), row(D_MODEL), row(D_MODEL)],
        out_shape=[jax.ShapeDtypeStruct((tp, D_MODEL), F32), jax.ShapeDtypeStruct((tp, D_MODEL), F32),
                   jax.ShapeDtypeStruct((tp, D_MODEL), BF16)],
        compiler_params=_params("parallel"),
    )(attn, rec, w_out, head, *([x] * qb), g_post_mix, g_pre_ffn)


def _ffn_fwd(u1, w1, w2, h1, tgt, g_post_ffn):
    tp = h1.shape[0]
    tr = _row_tile(tp)
    qb, nt = tr // BLOCK, tp // tr

    def body(*refs):
        u_ref, w1_ref, w2_ref, h1_ref = refs[:4]
        t_pieces = refs[4:4 + qb]
        g_ref, r1_ref, dy_ref, df2_ref, loss_ref, dg_ref, acc = refs[4 + qb:]
        i, c = pl.program_id(0), pl.program_id(1)
        cur = i % 2

        @pl.when((i == 0) & (c == 0))
        def _():
            loss_ref[...] = jnp.zeros_like(loss_ref)
            dg_ref[...] = jnp.zeros_like(dg_ref)
            acc[1] = jnp.zeros((tr, D_MODEL), F32)

        def matmuls():
            r = jnp.maximum(_dot(u_ref[...], w1_ref[0]), 0.0)
            r1_ref[...] = r.astype(BF16)
            return _dot((r * r).astype(BF16), w2_ref[0])

        def finish_previous_tile(valid):
            g = g_ref[...]
            fhat, rs = _rms(acc[1 - cur])
            h2 = h1_ref[...] + fhat * g
            rows = (i - 1) * tr + lax.broadcasted_iota(jnp.int32, h2.shape, 0)
            tgt_tile = jnp.concatenate([p[...] for p in t_pieces], axis=0)
            err = jnp.where((rows >= BLOCK) & valid, h2 - tgt_tile, 0.0)
            dy = err * (1.0 / D_MODEL)
            dy_ref[...] = dy
            loss_ref[...] += (0.5 / D_MODEL) * jnp.sum(err * err)
            df2, dg = _rms_bwd(fhat, rs, g, dy)
            df2_ref[...] = df2.astype(BF16)
            dg_ref[...] += dg

        @pl.when((c == 0) & (i < nt))
        def _():
            finish_previous_tile(i >= 1)
            acc[cur] = matmuls()

        @pl.when((c > 0) & (i < nt))
        def _():
            acc[cur] += matmuls()

        @pl.when((c == 0) & (i == nt))
        def _():
            finish_previous_tile(True)

    last = nt - 1
    this_row = pl.BlockSpec((tr, D_MODEL), lambda i, c: (jnp.minimum(i, last), 0))
    prev_row = pl.BlockSpec((tr, D_MODEL), lambda i, c: (jnp.maximum(i - 1, 0), 0))
    full = lambda a: pl.BlockSpec(a.shape, lambda i, c: (0,) * a.ndim)
    return pl.pallas_call(
        body, name="ffn_fwd", grid=(nt + 1, N_CHIPS),
        in_specs=[this_row, pl.BlockSpec((1, D_MODEL, FF_CHUNK), lambda i, c: (c, 0, 0)),
                  pl.BlockSpec((1, FF_CHUNK, D_MODEL), lambda i, c: (c, 0, 0)), prev_row] + _seq_specs(tr, delay=1)
        + [full(g_post_ffn)],
        out_specs=[pl.BlockSpec((tr, FF_CHUNK), lambda i, c: (jnp.minimum(i, last), jnp.where(i < nt, c, N_CHIPS - 1))),
                   prev_row, prev_row,
                   pl.BlockSpec((1, 1), lambda i, c: (0, 0)), pl.BlockSpec((1, D_MODEL), lambda i, c: (0, 0))],
        out_shape=[jax.ShapeDtypeStruct((tp, D_FF), BF16), jax.ShapeDtypeStruct((tp, D_MODEL), F32),
                   jax.ShapeDtypeStruct((tp, D_MODEL), BF16), jax.ShapeDtypeStruct((1, 1), F32),
                   jax.ShapeDtypeStruct((1, D_MODEL), F32)],
        scratch_shapes=[pltpu.VMEM((2, tr, D_MODEL), F32)],
        compiler_params=_params("arbitrary", "arbitrary"),
    )(u1, w1, w2, h1, *([tgt] * qb), g_post_ffn)


def _ffn_bwd_data(df2, r1, w1, w2, dy, h1, mix, g_pre_ffn, g_post_mix):
    tp = h1.shape[0]
    tr = _row_tile(tp)
    nt = tp // tr

    def body(df2_ref, r1_ref, w1_ref, w2_ref, dy_ref, h1_ref, mix_ref, gf_ref, gm_ref,
             da_ref, dh1_ref, dmix_ref, dgf_ref, dgm_ref, acc):
        i, c = pl.program_id(0), pl.program_id(1)
        cur = i % 2

        @pl.when((i == 0) & (c == 0))
        def _():
            dgf_ref[...] = jnp.zeros_like(dgf_ref)
            dgm_ref[...] = jnp.zeros_like(dgm_ref)
            acc[1] = jnp.zeros((tr, D_MODEL), F32)

        def matmuls():
            df = _dot_nt(df2_ref[...], w2_ref[0])
            da = (df * (2.0 * r1_ref[...].astype(F32))).astype(BF16)
            da_ref[...] = da
            return _dot_nt(da, w1_ref[0])

        def finish_previous_tile(valid):
            hhat, rs = _rms(h1_ref[...])
            dx, dgf = _rms_bwd(hhat, rs, gf_ref[...], acc[1 - cur])
            dh1 = dy_ref[...] + dx
            dh1_ref[...] = dh1
            mhat, rsm = _rms(mix_ref[...])
            dmix, dgm = _rms_bwd(mhat, rsm, gm_ref[...], dh1)
            dmix_ref[...] = dmix.astype(BF16)
            dgf_ref[...] += jnp.where(valid, dgf, 0.0)
            dgm_ref[...] += jnp.where(valid, dgm, 0.0)

        @pl.when((c == 0) & (i < nt))
        def _():
            finish_previous_tile(i >= 1)
            acc[cur] = matmuls()

        @pl.when((c > 0) & (i < nt))
        def _():
            acc[cur] += matmuls()

        @pl.when((c == 0) & (i == nt))
        def _():
            finish_previous_tile(True)

    last = nt - 1
    this_row = pl.BlockSpec((tr, D_MODEL), lambda i, c: (jnp.minimum(i, last), 0))
    prev_row = pl.BlockSpec((tr, D_MODEL), lambda i, c: (jnp.maximum(i - 1, 0), 0))
    chunk = pl.BlockSpec((tr, FF_CHUNK), lambda i, c: (jnp.minimum(i, last), jnp.where(i < nt, c, N_CHIPS - 1)))
    gain = pl.BlockSpec((1, D_MODEL), lambda i, c: (0, 0))
    return pl.pallas_call(
        body, name="ffn_bwd_data", grid=(nt + 1, N_CHIPS),
        in_specs=[this_row, chunk, pl.BlockSpec((1, D_MODEL, FF_CHUNK), lambda i, c: (c, 0, 0)),
                  pl.BlockSpec((1, FF_CHUNK, D_MODEL), lambda i, c: (c, 0, 0)), prev_row, prev_row, prev_row, gain, gain],
        out_specs=[chunk, prev_row, prev_row, gain, gain],
        out_shape=[jax.ShapeDtypeStruct((tp, D_FF), BF16), jax.ShapeDtypeStruct((tp, D_MODEL), F32),
                   jax.ShapeDtypeStruct((tp, D_MODEL), BF16), jax.ShapeDtypeStruct((1, D_MODEL), F32),
                   jax.ShapeDtypeStruct((1, D_MODEL), F32)],
        scratch_shapes=[pltpu.VMEM((2, tr, D_MODEL), F32)],
        compiler_params=_params("arbitrary", "arbitrary"),
    )(df2, r1, w1, w2, dy, h1, mix, g_pre_ffn, g_post_mix)


def _ffn_bwd_weights(u1, da1, r1, df2):
    tp = u1.shape[0]
    tr = _wgrad_row_tile(tp)

    def body(u_ref, da_ref, r1_ref, df2_ref, dw1_ref, dw2_ref):
        i = pl.program_id(1)
        r = r1_ref[...].astype(F32)
        p1 = _dot_tn(u_ref[...], da_ref[...])
        p2 = _dot_tn((r * r).astype(BF16), df2_ref[...])

        @pl.when(i == 0)
        def _():
            dw1_ref[0] = p1
            dw2_ref[0] = p2

        @pl.when(i > 0)
        def _():
            dw1_ref[0] += p1
            dw2_ref[0] += p2

    row = pl.BlockSpec((tr, D_MODEL), lambda c, i: (i, 0))
    chunk = pl.BlockSpec((tr, FF_CHUNK), lambda c, i: (i, c))
    return pl.pallas_call(
        body, name="ffn_bwd_weights", grid=(N_CHIPS, tp // tr),
        in_specs=[row, chunk, chunk, row],
        out_specs=[pl.BlockSpec((1, D_MODEL, FF_CHUNK), lambda c, i: (c, 0, 0)),
                   pl.BlockSpec((1, FF_CHUNK, D_MODEL), lambda c, i: (c, 0, 0))],
        out_shape=[jax.ShapeDtypeStruct((N_CHIPS, D_MODEL, FF_CHUNK), F32),
                   jax.ShapeDtypeStruct((N_CHIPS, FF_CHUNK, D_MODEL), F32)],
        compiler_params=_params("parallel", "arbitrary"),
    )(u1, da1, r1, df2)


def _outproj_bwd(dmix, w_out, attn, rec, token):
    tp = dmix.shape[0]
    tr = _wgrad_row_tile(tp)

    def body(dm_ref, w_ref, a_ref, r_ref, _, da_ref, dr_ref, dw_ref):
        i = pl.program_id(0)
        dm = dm_ref[...]
        dcat = _dot_nt(dm, w_ref[...])
        da_ref[...] = dcat[:, :ATTN_WIDTH].astype(BF16)
        dr_ref[...] = dcat[:, ATTN_WIDTH:]
        pa = _dot_tn(a_ref[...], dm)
        pr = _dot_tn(r_ref[...], dm)

        @pl.when(i == 0)
        def _():
            dw_ref[:ATTN_WIDTH] = pa
            dw_ref[ATTN_WIDTH:] = pr

        @pl.when(i > 0)
        def _():
            dw_ref[:ATTN_WIDTH] += pa
            dw_ref[ATTN_WIDTH:] += pr

    row = lambda w: pl.BlockSpec((tr, w), lambda i: (i, 0))
    full = pl.BlockSpec((D_MODEL, D_MODEL), lambda i: (0, 0))
    return pl.pallas_call(
        body, name="outproj_bwd", grid=(tp // tr,),
        in_specs=[row(D_MODEL), full, row(ATTN_WIDTH), row(LRU_WIDTH), pl.BlockSpec(token.shape, lambda i: (0, 0))],
        out_specs=[row(ATTN_WIDTH), row(LRU_WIDTH), full],
        out_shape=[jax.ShapeDtypeStruct((tp, ATTN_WIDTH), BF16), jax.ShapeDtypeStruct((tp, LRU_WIDTH), F32),
                   jax.ShapeDtypeStruct((D_MODEL, D_MODEL), F32)],
        compiler_params=_params("arbitrary"),
    )(dmix, w_out, attn, rec, token)


N_VEC_ROWS = 8


def _lru_bwd(xr, yr, hr, drec, conv_w, conv_b, wa, ba, wx, bx, lam, token):
    tp = xr.shape[0]
    tr = _row_tile(tp)
    qb, nt = tr // BLOCK, tp // tr

    def body(xr_ref, xh_ref, yr_ref, hr_ref, hp_ref, dr_ref, cw_ref, cb_ref, wa_ref, ba_ref, wx_ref, bx_ref, lam_ref, _,
             dxr_ref, dyr_ref, dwa_ref, dwx_ref, vec_ref, g_next, a_next, dxc_next, dsp):
        s = pl.program_id(0)
        t = nt - 1 - s

        @pl.when(s == 0)
        def _():
            g_next[...] = jnp.zeros_like(g_next)
            a_next[...] = jnp.zeros_like(a_next)
            dxc_next[...] = jnp.zeros_like(dxc_next)
            dsp[...] = jnp.zeros_like(dsp)
            dwa_ref[...] = jnp.zeros_like(dwa_ref)
            dwx_ref[...] = jnp.zeros_like(dwx_ref)
            vec_ref[...] = jnp.zeros_like(vec_ref)

        first_tile = t == 0
        cw, cb = cw_ref[...], cb_ref[...]
        lam_v = lam_ref[...]
        sp = _softplus(-lam_v)
        wa_m, ba_v, wx_m, bx_v = wa_ref[...], ba_ref[...], wx_ref[...], bx_ref[...]
        rows = lax.broadcasted_iota(jnp.int32, (BLOCK, LRU_WIDTH), 0)
        col = lambda v: jnp.sum(v, axis=0, keepdims=True)

        g_after, a_after, dxc_after = g_next[0:1], a_next[0:1], dxc_next[...]
        xbs, dgrs, dgis = [], [], []
        vec = [jnp.zeros((1, LRU_WIDTH), F32) for _ in range(N_VEC_ROWS)]
        for i in reversed(range(qb)):
            blk = slice(i * BLOCK, (i + 1) * BLOCK)
            if i == 0:
                x_before = jnp.where(first_tile, 0.0, xh_ref[...])
                h_before = jnp.where(first_tile, 0.0, hp_ref[7:8])
            else:
                x_before = xr_ref[i * BLOCK - 8:i * BLOCK]
                h_before = hr_ref[i * BLOCK - 1:i * BLOCK]
            taps = _conv_taps(xr_ref[blk], x_before)
            xc = cb + sum(cw[k:k + 1] * taps[k] for k in range(4))
            xb, r, ig, a, mult = _lru_gates(xc, wa_m, ba_v, wx_m, bx_v, sp)

            yr_v = yr_ref[blk]
            gl, th = _gelu(yr_v)
            h = hr_ref[blk]
            drec = dr_ref[blk]
            dyr_ref[blk] = (drec * h * _gelu_grad(yr_v, th)).astype(BF16)

            a_up = jnp.where(rows == BLOCK - 1, a_after, pltpu.roll(a, BLOCK - 1, 0))
            g = _scan_rev(a_up, drec * gl, g_after)
            g_after, a_after = g[0:1], a[0:1]

            h_prev = jnp.where(rows == 0, h_before, pltpu.roll(h, 1, 0))
            du, da = g, g * h_prev
            if i == 0:
                real = (t * tr + rows) >= PAD_ROWS
                du, da = jnp.where(real, du, 0.0), jnp.where(real, da, 0.0)
            dmult = du * (ig * xc)
            dig = du * (mult * xc)
            dxc = du * (mult * ig)
            dlog_a = da * a - dmult * (a * a / mult)
            if i == 0:
                dlog_a = jnp.where(real, dlog_a, 0.0)
            dgr = (dlog_a * (-LRU_C * sp)) * (r * (1.0 - r))
            dgi = dig * (ig * (1.0 - ig))
            dgr_b, dgi_b = dgr.astype(BF16), dgi.astype(BF16)
            dxc = dxc + _dot_nt(dgr_b, wa_m) + _dot_nt(dgi_b, wx_m)
            xbs.append(xb)
            dgrs.append(dgr_b)
            dgis.append(dgi_b)

            ext = jnp.concatenate([dxc, dxc_after], axis=0)
            up = [ext[:BLOCK] if j == 0 else pltpu.roll(ext, BLOCK + 8 - j, 0)[:BLOCK] for j in range(4)]
            dxr_ref[blk] = sum(cw[k:k + 1] * up[3 - k] for k in range(4)).astype(BF16)
            dxc_after = dxc[:8]

            for k in range(4):
                vec[k] = vec[k] + col(dxc * taps[k])
            vec[4] = vec[4] + col(dxc)
            vec[5] = vec[5] + col(dgr)
            vec[6] = vec[6] + col(dgi)
            vec[7] = vec[7] + col(dlog_a * (-LRU_C * r))

        g_next[0:1], a_next[0:1], dxc_next[...] = g_after, a_after, dxc_after
        xb_all = jnp.concatenate(xbs, axis=0)
        dwa_ref[...] += _dot_tn(xb_all, jnp.concatenate(dgrs, axis=0))
        dwx_ref[...] += _dot_tn(xb_all, jnp.concatenate(dgis, axis=0))
        for k in range(7):
            vec_ref[k:k + 1] += vec[k]
        dsp[0:1] += vec[7]

        @pl.when(s == nt - 1)
        def _():
            vec_ref[7:8] = dsp[0:1] * (-_sigmoid(-lam_v))

    blk_spec = pl.BlockSpec((tr, LRU_WIDTH), lambda s: (nt - 1 - s, 0))
    rows_before = pl.BlockSpec((8, LRU_WIDTH), lambda s: (jnp.maximum((nt - 1 - s) * (tr // 8) - 1, 0), 0))
    full = lambda a: pl.BlockSpec(a.shape, lambda s: (0,) * a.ndim)
    small = [conv_w, conv_b, wa, ba, wx, bx, lam, token]
    sq = pl.BlockSpec((LRU_WIDTH, LRU_WIDTH), lambda s: (0, 0))
    return pl.pallas_call(
        body, name="lru_bwd", grid=(nt,),
        in_specs=[blk_spec, rows_before, blk_spec, blk_spec, rows_before, blk_spec] + [full(a) for a in small],
        out_specs=[blk_spec, blk_spec, sq, sq, pl.BlockSpec((N_VEC_ROWS, LRU_WIDTH), lambda s: (0, 0))],
        out_shape=[jax.ShapeDtypeStruct((tp, LRU_WIDTH), BF16), jax.ShapeDtypeStruct((tp, LRU_WIDTH), BF16),
                   jax.ShapeDtypeStruct((LRU_WIDTH, LRU_WIDTH), F32), jax.ShapeDtypeStruct((LRU_WIDTH, LRU_WIDTH), F32),
                   jax.ShapeDtypeStruct((N_VEC_ROWS, LRU_WIDTH), F32)],
        scratch_shapes=[pltpu.VMEM((8, LRU_WIDTH), F32)] * 4,
        compiler_params=_params("arbitrary"),
    )(xr, xr, yr, hr, hr, drec, *small)


def _attn_bwd(qkv, dattn, sinks):
    tp = qkv.shape[0]
    tr = _row_tile(tp)
    qb, nt = tr // BLOCK, tp // tr
    n_groups = ATTN_HEADS // GQA_GROUP
    sink_rows, bias = _attn_consts(sinks)

    def body(s_ref, b_ref, q_ref, kp_ref, kc_ref, vp_ref, vc_ref, do_ref, dq_ref, dkv_ref, ex_ref, ds_ref, dsink):
        t = pl.program_id(0)

        @pl.when(t == 0)
        def _():
            dsink[...] = jnp.zeros_like(dsink)

        k_all = jnp.concatenate([kp_ref[...], kc_ref[...]], axis=0)
        v_all = jnp.concatenate([vp_ref[...], vc_ref[...]], axis=0)
        tail = None
        for i in range(qb):
            rows = slice(i * BLOCK, (i + 1) * BLOCK)
            q, do = q_ref[rows], do_ref[rows]
            k2, v2 = k_all[i * BLOCK:(i + 2) * BLOCK], v_all[i * BLOCK:(i + 2) * BLOCK]
            bias_n = _block_bias(b_ref, t, qb, i)
            dqs, dks, dvs = [], [], []
            for g in range(n_groups):
                cols = slice(g * HEAD_DIM, (g + 1) * HEAD_DIM)
                k_g, v_g = k2[:, cols], v2[:, cols]
                qg = _stack_heads(q, g) * jnp.asarray(_QSCALE, BF16)
                dog = _stack_heads(do, g)
                p, ps = _attn_probs_t(k_g, qg, bias_n, s_ref[g:g + 1])
                dpt = _dot_nt(v_g, dog)
                delta = jnp.sum(p * dpt, axis=0, keepdims=True)
                dst = (p * (dpt - delta)).astype(BF16)
                dqs.append(_dot_tn(dst, k_g) * _QSCALE)
                dks.append(_dot(dst, qg))
                dvs.append(_dot(p.astype(BF16), dog))
                dsink[g:g + 1] -= ps * delta
            dq_ref[rows] = _unstack_heads(dqs).astype(BF16)
            dkv = jnp.concatenate(dks + dvs, axis=1)
            if i == 0:
                ex_ref[0] = dkv[:BLOCK]
            else:
                dkv_ref[(i - 1) * BLOCK:i * BLOCK] = (tail + dkv[:BLOCK]).astype(BF16)
            tail = dkv[BLOCK:]
        dkv_ref[(qb - 1) * BLOCK:] = tail.astype(BF16)

        @pl.when(t == nt - 1)
        def _():
            lane = lax.broadcasted_iota(jnp.int32, (1, ATTN_HEADS), 1)
            acc = jnp.zeros((1, ATTN_HEADS), F32)
            for h in range(ATTN_HEADS):
                g, hh = divmod(h, GQA_GROUP)
                acc = acc + jnp.where(lane == h, jnp.sum(dsink[g:g + 1, hh * BLOCK:(hh + 1) * BLOCK]), 0.0)
            ds_ref[...] = acc

    cur = lambda w: pl.BlockSpec((tr, w), lambda t: (t, 0))
    return pl.pallas_call(
        body, name="attn_bwd", grid=(nt,),
        in_specs=[_SINK_SPEC, _BIAS_SPEC, cur(ATTN_WIDTH)] + _kv_specs(tr) + [cur(ATTN_WIDTH)],
        out_specs=[cur(ATTN_WIDTH), cur(2 * KV_WIDTH), pl.BlockSpec((1, BLOCK, 2 * KV_WIDTH), lambda t: (t, 0, 0)),
                   pl.BlockSpec((1, ATTN_HEADS), lambda t: (0, 0))],
        out_shape=[jax.ShapeDtypeStruct((tp, ATTN_WIDTH), BF16), jax.ShapeDtypeStruct((tp, 2 * KV_WIDTH), BF16),
                   jax.ShapeDtypeStruct((nt, BLOCK, 2 * KV_WIDTH), F32), jax.ShapeDtypeStruct((1, ATTN_HEADS), F32)],
        scratch_shapes=[pltpu.VMEM((n_groups, GROUP_ROWS), F32)],
        compiler_params=_params("arbitrary"),
    )(sink_rows, bias, qkv, qkv, qkv, qkv, qkv, dattn)


def _fix_dkv(dkv, dkv_extra):
    tp = dkv.shape[0]
    tr = _row_tile(tp)
    nt, qb = tp // tr, tr // BLOCK
    if nt == 1:
        return dkv

    def body(d_ref, ex_ref, o_ref):
        o_ref[...] = (d_ref[...].astype(F32) + ex_ref[0]).astype(BF16)

    last = pl.BlockSpec((BLOCK, 2 * KV_WIDTH), lambda t: (t * qb + qb - 1, 0))
    return pl.pallas_call(
        body, name="fix_dkv", grid=(nt - 1,),
        in_specs=[last, pl.BlockSpec((1, BLOCK, 2 * KV_WIDTH), lambda t: (t + 1, 0, 0))],
        out_specs=last, out_shape=jax.ShapeDtypeStruct(dkv.shape, dkv.dtype),
        input_output_aliases={0: 0}, compiler_params=_params("parallel"),
    )(dkv, dkv_extra)


def _inproj_wgrad(dq, dkv, dxr, dyr, u0):
    tp = dq.shape[0]
    tr = _wgrad_row_tile(tp)

    def body(dq_ref, dkv_ref, dxr_ref, dyr_ref, u_ref, dw_ref):
        i = pl.program_id(0)
        dz = jnp.concatenate([dq_ref[...], dkv_ref[...], dxr_ref[...], dyr_ref[...]], axis=1)
        pw = _dot_tn(dz, u_ref[...])

        @pl.when(i == 0)
        def _():
            dw_ref[...] = pw

        @pl.when(i > 0)
        def _():
            dw_ref[...] += pw

    row = lambda w: pl.BlockSpec((tr, w), lambda i: (i, 0))
    return pl.pallas_call(
        body, name="inproj_wgrad", grid=(tp // tr,),
        in_specs=[row(ATTN_WIDTH), row(2 * KV_WIDTH), row(LRU_WIDTH), row(LRU_WIDTH), row(D_MODEL)],
        out_specs=pl.BlockSpec((IN_WIDTH, D_MODEL), lambda i: (0, 0)),
        out_shape=jax.ShapeDtypeStruct((IN_WIDTH, D_MODEL), F32),
        compiler_params=_params("arbitrary"),
    )(dq, dkv, dxr, dyr, u0)


def _inproj_dgrad(dq, dkv, dxr, dyr, w_in, head, x, dh1, g, token):
    tp = dq.shape[0]
    tr = _row_tile(tp)
    nt, qb = tp // tr, tr // BLOCK

    def body(*refs):
        dq_ref, dkv_ref, dxr_ref, dyr_ref, w_ref, head_ref = refs[:6]
        pieces = refs[6:6 + qb]
        dh1_ref, g_ref, _, gx_ref, dhead_ref, dg_ref, buf, sems = refs[6 + qb:]
        i = pl.program_id(0)
        slot = i % 2

        def out_copy(step, at):
            return pltpu.make_async_copy(buf.at[at], gx_ref.at[pl.ds(step * tr - BLOCK, tr)], sems.at[at])

        dz = jnp.concatenate([dq_ref[...], dkv_ref[...], dxr_ref[...], dyr_ref[...]], axis=1)
        du = _dot(dz, w_ref[...])
        hhat, rs = _rms(_seq_tile(head_ref[...], pieces, i))
        dx, dg = _rms_bwd(hhat, rs, g_ref[...], du)
        dh0 = dh1_ref[...] + dx

        @pl.when(i >= 3)
        def _():
            out_copy(i - 2, slot).wait()

        buf[slot] = dh0

        @pl.when(i == 0)
        def _():
            dg_ref[...] = dg
            dhead_ref[...] = dh0[:BLOCK]
            if tr > BLOCK:
                first = pltpu.make_async_copy(buf.at[0, pl.ds(BLOCK, tr - BLOCK)], gx_ref.at[pl.ds(0, tr - BLOCK)],
                                              sems.at[0])
                first.start()
                first.wait()

        @pl.when(i >= 1)
        def _():
            dg_ref[...] += dg
            out_copy(i, slot).start()

        @pl.when(i == nt - 1)
        def _():
            if nt >= 3:
                out_copy(nt - 2, (nt - 2) % 2).wait()
            if nt >= 2:
                out_copy(nt - 1, (nt - 1) % 2).wait()

    row = lambda w: pl.BlockSpec((tr, w), lambda i: (i, 0))
    full = lambda shape: pl.BlockSpec(shape, lambda i: (0,) * len(shape))
    return pl.pallas_call(
        body, name="inproj_dgrad", grid=(tp // tr,),
        in_specs=[row(ATTN_WIDTH), row(2 * KV_WIDTH), row(LRU_WIDTH), row(LRU_WIDTH), full(w_in.shape),
                  full(head.shape)] + _seq_specs(tr) + [row(D_MODEL), full(g.shape), full(token.shape)],
        out_specs=[pl.BlockSpec(memory_space=pl.ANY), full((BLOCK, D_MODEL)), full((1, D_MODEL))],
        out_shape=[jax.ShapeDtypeStruct(x.shape, F32), jax.ShapeDtypeStruct((BLOCK, D_MODEL), F32),
                   jax.ShapeDtypeStruct((1, D_MODEL), F32)],
        scratch_shapes=[pltpu.VMEM((2, tr, D_MODEL), F32), pltpu.SemaphoreType.DMA((2,))],
        compiler_params=_params("arbitrary"),
    )(dq, dkv, dxr, dyr, w_in, head, *([x] * qb), dh1, g, token)


def _dense_block_diag(w):
    eye = jnp.eye(LRU_BLOCKS, dtype=w.dtype)
    return (w[:, :, None, :] * eye[:, None, :, None]).reshape(LRU_WIDTH, LRU_WIDTH)


def _diag_blocks(dense):
    d4 = dense.reshape(LRU_BLOCKS, LRU_BLOCK, LRU_BLOCKS, LRU_BLOCK)
    return jnp.stack([d4[n, :, n, :] for n in range(LRU_BLOCKS)])


def _local_step(head, x, tgt, g_pre_mix, w_in, conv_w, conv_b, w_a, b_a, w_x, b_x, lam, sinks, g_post_mix,
                g_pre_ffn, g_post_ffn, late_weights, on_ffn_grads, on_outproj_bwd, on_mixer_grads, token):
    wa = _dense_block_diag(w_a).astype(BF16)
    wx = _dense_block_diag(w_x).astype(BF16)

    u0, qkv, xr, yr = _inproj_fwd(head, x, g_pre_mix, w_in, token)
    attn = _attn_fwd(qkv, sinks)
    hr, rec = _lru_fwd(xr, yr, conv_w, conv_b, wa, b_a, wx, b_x, lam)
    w_out, w1, w2 = late_weights([attn, rec])
    mix, h1, u1 = _outproj_fwd(attn, rec, w_out, head, x, g_post_mix, g_pre_ffn)
    r1, dy, df2, loss, dg_post_ffn = _ffn_fwd(u1, w1, w2, h1, tgt, g_post_ffn)

    da1, dh1, dmix, dg_pre_ffn, dg_post_mix = _ffn_bwd_data(df2, r1, w1, w2, dy, h1, mix, g_pre_ffn, g_post_mix)
    dw1, dw2 = _ffn_bwd_weights(u1, da1, r1, df2)
    token2 = on_ffn_grads(dw1, dw2)
    dattn, drec, dw_out = _outproj_bwd(dmix, w_out, attn, rec, token2)
    token3 = on_outproj_bwd(dattn)
    dxr, dyr, dwa, dwx, vec = _lru_bwd(xr, yr, hr, drec, conv_w, conv_b, wa, b_a, wx, b_x, lam, token3)
    dq, dkv, dkv_extra, dsinks = _attn_bwd(qkv, dattn, sinks)
    dkv = _fix_dkv(dkv, dkv_extra)
    dw_in = _inproj_wgrad(dq, dkv, dxr, dyr, u0)
    token4 = on_mixer_grads(dw_in, dw_out)
    dx, dhead, dg_pre_mix = _inproj_dgrad(dq, dkv, dxr, dyr, w_in, head, x, dh1, g_pre_mix, token4)

    grads = dict(
        g_pre_mix=dg_pre_mix, conv_w=vec[0:4], conv_b=vec[4:5], w_a=_diag_blocks(dwa), b_a=vec[5:6],
        w_x=_diag_blocks(dwx), b_x=vec[6:7], lru_lambda=vec[7:8], attn_sinks=dsinks,
        g_post_mix=dg_post_mix, g_pre_ffn=dg_pre_ffn, g_post_ffn=dg_post_ffn)
    return loss, dx, dhead, grads


HBM = pl.BlockSpec(memory_space=pltpu.HBM)


def _mesh_pos():
    return lax.axis_index("x"), lax.axis_index("y"), lax.axis_index("c")


def _other_chips(x, y):
    return [(1 - x, y), (x, 1 - y), (1 - x, 1 - y)]


def _remote(src, dst, send_sem, recv_sem, to):
    return pltpu.make_async_remote_copy(src_ref=src, dst_ref=dst, send_sem=send_sem, recv_sem=recv_sem,
                                        device_id=to, device_id_type=MESH)


def _gather_weights(shards, lands, tiny, tiny_land):
    nbig = len(shards)

    def body(*refs):
        srcs, tiny_src = refs[:nbig], refs[nbig]
        outs, tiny_out = refs[2 * nbig + 2:3 * nbig + 2], refs[3 * nbig + 2]
        ici_send, ici_recv, d2d_send, d2d_recv, tiny_send, tiny_recv = refs[3 * nbig + 3:]
        x, y, c = _mesh_pos()
        me = 2 * x + y
        chips = _other_chips(x, y)
        sibling = (x, y, 1 - c)
        sends = []
        for w, (src, out) in enumerate(zip(srcs, outs)):
            hr = src.shape[0] // 2
            for j, chip in enumerate(chips):
                k = 3 * w + j
                cp = _remote(src.at[pl.ds(c * hr, hr)], out.at[me, pl.ds(c * hr, hr)],
                             ici_send.at[k], ici_recv.at[k], (*chip, c))
                cp.start()
                sends.append(cp)
        for j, chip in enumerate(chips):
            cp = _remote(tiny_src, tiny_out.at[me], tiny_send.at[j], tiny_recv.at[j], (*chip, c))
            cp.start()
            sends.append(cp)
        for w, (src, out) in enumerate(zip(srcs, outs)):
            hr = src.shape[0] // 2
            for j, (px, py) in enumerate(chips):
                k = 3 * w + j
                landed = out.at[2 * px + py, pl.ds(c * hr, hr)]
                _remote(landed, landed, ici_send.at[k], ici_recv.at[k], sibling).wait_recv()
                cp = _remote(landed, landed, d2d_send.at[k], d2d_recv.at[k], sibling)
                cp.start()
                sends.append(cp)
        for w, (src, out) in enumerate(zip(srcs, outs)):
            hr = src.shape[0] // 2
            for j, (px, py) in enumerate(chips):
                k = 3 * w + j
                other = out.at[2 * px + py, pl.ds((1 - c) * hr, hr)]
                _remote(other, other, d2d_send.at[k], d2d_recv.at[k], sibling).wait_recv()
        for j, (px, py) in enumerate(chips):
            blk = tiny_out.at[2 * px + py]
            _remote(blk, blk, tiny_send.at[j], tiny_recv.at[j], sibling).wait_recv()
        for cp in sends:
            cp.wait_send()

    out_shape = [jax.ShapeDtypeStruct(l.shape, l.dtype) for l in list(lands) + [tiny_land]]
    n = 3 * nbig
    return pl.pallas_call(
        body, name="gather_weights", out_shape=out_shape,
        in_specs=[HBM] * (2 * nbig + 2), out_specs=[HBM] * (nbig + 1),
        input_output_aliases={nbig + 1 + i: i for i in range(nbig + 1)},
        scratch_shapes=[pltpu.SemaphoreType.DMA((n,)),
                        pltpu.SemaphoreType.DMA((n,)), pltpu.SemaphoreType.DMA((n,)), pltpu.SemaphoreType.DMA((n,)),
                        pltpu.SemaphoreType.DMA((3,)), pltpu.SemaphoreType.DMA((3,))],
    )(*shards, tiny, *lands, tiny_land)


def _prep_shard(w, me):
    rows, cols = w.shape
    tr = 256 if rows % 256 == 0 else rows

    def body(me_ref, w_ref, s_ref, l_ref):
        b = w_ref[...].astype(BF16)
        s_ref[...] = b
        l_ref[0] = b

    return pl.pallas_call(
        body, name="prep_shard",
        grid_spec=pltpu.PrefetchScalarGridSpec(
            num_scalar_prefetch=1, grid=(rows // tr,),
            in_specs=[pl.BlockSpec((tr, cols), lambda i, me_ref: (i, 0))],
            out_specs=[pl.BlockSpec((tr, cols), lambda i, me_ref: (i, 0)),
                       pl.BlockSpec((1, tr, cols), lambda i, me_ref: (me_ref[0], i, 0))]),
        out_shape=[jax.ShapeDtypeStruct((rows, cols), BF16), jax.ShapeDtypeStruct((N_CHIPS, rows, cols), BF16)],
        compiler_params=_params("parallel"),
    )(me, w)


def _prep_tiny(tiny, me, slots=N_CHIPS):
    def body(me_ref, t_ref, l_ref):
        l_ref[0] = t_ref[...]

    return pl.pallas_call(
        body, name="prep_tiny",
        grid_spec=pltpu.PrefetchScalarGridSpec(
            num_scalar_prefetch=1, grid=(1,),
            in_specs=[pl.BlockSpec(tiny.shape, lambda i, me_ref: (0, 0))],
            out_specs=pl.BlockSpec((1,) + tiny.shape, lambda i, me_ref: (me_ref[0], 0, 0))),
        out_shape=jax.ShapeDtypeStruct((slots,) + tiny.shape, tiny.dtype),
    )(me, tiny)


N_DEV = 8


def _sibling_exchange(parts, token):
    def body(*refs):
        n = len(parts)
        srcs, outs, send_sems, recv_sems = refs[:n], refs[n + 1:2 * n + 1], refs[2 * n + 1], refs[2 * n + 2]
        x, y, c = _mesh_pos()
        sibling = (x, y, 1 - c)
        cps = []
        for w, (src, out) in enumerate(zip(srcs, outs)):
            hr = src.shape[1] // 2
            cp = _remote(src.at[:, pl.ds((1 - c) * hr, hr)], out, send_sems.at[w], recv_sems.at[w], sibling)
            cp.start()
            cps.append(cp)
        for cp in cps:
            cp.wait()

    n = len(parts)
    return pl.pallas_call(
        body, name="sibling_exchange",
        out_shape=[jax.ShapeDtypeStruct((p.shape[0], p.shape[1] // 2, p.shape[2]), p.dtype) for p in parts],
        in_specs=[HBM] * n + [pl.BlockSpec(memory_space=pl.ANY)], out_specs=[HBM] * n,
        scratch_shapes=[pltpu.SemaphoreType.DMA((n,)), pltpu.SemaphoreType.DMA((n,))],
    )(*parts, token)


def _chip_presum(part, from_sibling, pos):
    _, hr, cols = from_sibling.shape
    tr = 256 if hr % 256 == 0 else hr
    steps = hr // tr

    def body(pos_ref, a_ref, b_ref, o_ref, land_ref):
        s = (a_ref[...] + b_ref[...]).astype(BF16)
        o_ref[...] = s

        @pl.when(pl.program_id(1) == pos_ref[1])
        def _():
            land_ref[...] = s

    return pl.pallas_call(
        body, name="chip_presum",
        grid_spec=pltpu.PrefetchScalarGridSpec(
            num_scalar_prefetch=1, grid=(steps, N_CHIPS),
            in_specs=[pl.BlockSpec((1, tr, cols), lambda i, j, p: (j, p[0] * steps + i, 0)),
                      pl.BlockSpec((1, tr, cols), lambda i, j, p: (j, i, 0))],
            out_specs=[pl.BlockSpec((1, tr, cols), lambda i, j, p: (j, i, 0)),
                       pl.BlockSpec((1, tr, cols), lambda i, j, p: (p[1], p[0] * steps + i, 0))]),
        out_shape=[jax.ShapeDtypeStruct(from_sibling.shape, BF16),
                   jax.ShapeDtypeStruct((N_CHIPS, 2 * hr, cols), BF16)],
        compiler_params=_params("arbitrary", "arbitrary"),
    )(pos, part, from_sibling)


def _scatter_partials(cparts, lands, done_cparts=(), done_lands=()):
    n_new = len(cparts)
    nw = n_new + len(done_cparts)

    def body(*refs):
        srcs = refs[:nw]
        outs = refs[2 * nw:3 * nw]
        own_send, own_recv, ici_send, ici_recv, d2d_send, d2d_recv = refs[3 * nw:]
        x, y, c = _mesh_pos()
        me = 2 * x + y
        chips = _other_chips(x, y)
        sibling = (x, y, 1 - c)
        sends = []
        for w in list(range(n_new, nw)) + list(range(n_new)):
            src, out = srcs[w], outs[w]
            hr = src.shape[1]
            mine = out.at[me, pl.ds(c * hr, hr)]
            cp = _remote(src.at[me], mine, own_send.at[w], own_recv.at[w], sibling)
            cp.start()
            sends.append(cp)
            for j, (px, py) in enumerate(chips):
                if w >= n_new:
                    break
                k = 3 * w + j
                cp = _remote(src.at[2 * px + py], mine, ici_send.at[k], ici_recv.at[k], (px, py, c))
                cp.start()
                sends.append(cp)
        for w in list(range(n_new, nw)) + list(range(n_new)):
            src, out = srcs[w], outs[w]
            hr = src.shape[1]
            for j, (px, py) in enumerate(chips):
                k = 3 * w + j
                landed = out.at[2 * px + py, pl.ds(c * hr, hr)]
                if w < n_new:
                    _remote(landed, landed, ici_send.at[k], ici_recv.at[k], sibling).wait_recv()
                cp = _remote(landed, landed, d2d_send.at[k], d2d_recv.at[k], sibling)
                cp.start()
                sends.append(cp)
        for w, (src, out) in enumerate(zip(srcs, outs)):
            hr = src.shape[1]
            other = out.at[me, pl.ds((1 - c) * hr, hr)]
            _remote(other, other, own_send.at[w], own_recv.at[w], sibling).wait_recv()
            for j, (px, py) in enumerate(chips):
                k = 3 * w + j
                other = out.at[2 * px + py, pl.ds((1 - c) * hr, hr)]
                _remote(other, other, d2d_send.at[k], d2d_recv.at[k], sibling).wait_recv()
        for cp in sends:
            cp.wait_send()

    n = 3 * nw
    dma = pltpu.SemaphoreType.DMA
    every = list(cparts) + list(done_cparts)
    every_lands = list(lands) + list(done_lands)
    return pl.pallas_call(
        body, name="scatter_partials",
        out_shape=[jax.ShapeDtypeStruct(l.shape, l.dtype) for l in every_lands],
        in_specs=[HBM] * (2 * nw), out_specs=[HBM] * nw,
        input_output_aliases={nw + i: i for i in range(nw)},
        scratch_shapes=[dma((nw,)), dma((nw,)), dma((n,)), dma((n,)), dma((n,)), dma((n,))],
    )(*every, *every_lands)


SEM = pl.BlockSpec(memory_space=pltpu.SEMAPHORE)
SPLIT_COPY = pltpu.CompilerParams(has_side_effects=pltpu.SideEffectType.DATAFLOW_SIDE_EFFECTING)


def _hbm(a):
    return pltpu.with_memory_space_constraint(a, pltpu.HBM)


def _gather_copies(srcs, lands, send_sems, recv_sems):
    x, y, c = _mesh_pos()
    me = 2 * x + y
    sends, recvs = [], []
    for w, (src, land) in enumerate(zip(srcs, lands)):
        hr = src.shape[0] // 2
        for j, (px, py) in enumerate(_other_chips(x, y)):
            k = 3 * w + j
            sends.append(_remote(src.at[pl.ds(c * hr, hr)], land.at[me, pl.ds(c * hr, hr)],
                                 send_sems.at[k], recv_sems.at[k], (px, py, c)))
            got = land.at[2 * px + py, pl.ds(c * hr, hr)]
            recvs.append(_remote(got, got, send_sems.at[k], recv_sems.at[k], (px, py, c)))
    return sends, recvs


def _scatter_copies(srcs, lands, send_sems, recv_sems):
    x, y, c = _mesh_pos()
    me = 2 * x + y
    sends, recvs = [], []
    for w, (src, land) in enumerate(zip(srcs, lands)):
        hr = src.shape[1]
        for j, (px, py) in enumerate(_other_chips(x, y)):
            k = 3 * w + j
            sends.append(_remote(src.at[2 * px + py], land.at[me, pl.ds(c * hr, hr)],
                                 send_sems.at[k], recv_sems.at[k], (px, py, c)))
            got = land.at[2 * px + py, pl.ds(c * hr, hr)]
            recvs.append(_remote(got, got, send_sems.at[k], recv_sems.at[k], (px, py, c)))
    return sends, recvs


def _sibling_copies(srcs, lands, send_sems, recv_sems):
    x, y, c = _mesh_pos()
    sibling = (x, y, 1 - c)
    sends, recvs = [], []
    for w, (src, land) in enumerate(zip(srcs, lands)):
        hr = src.shape[1] // 2
        sends.append(_remote(src.at[:, pl.ds((1 - c) * hr, hr)], land, send_sems.at[w], recv_sems.at[w], sibling))
        recvs.append(_remote(land, land, send_sems.at[w], recv_sems.at[w], sibling))
    return sends, recvs


def _all_peers_copies(srcs, lands, send_sems, recv_sems):
    x, y, c = _mesh_pos()
    (src,), (land,) = srcs, lands
    flip = lambda v, bit: 1 - v if bit else v
    sends, recvs = [], []
    for k in range(N_DEV - 1):
        px, py, pc = flip(x, (k + 1) & 4), flip(y, (k + 1) & 2), flip(c, (k + 1) & 1)
        sends.append(_remote(src, land.at[4 * x + 2 * y + c], send_sems.at[k], recv_sems.at[k], (px, py, pc)))
        got = land.at[4 * px + 2 * py + pc]
        recvs.append(_remote(got, got, send_sems.at[k], recv_sems.at[k], (px, py, pc)))
    return sends, recvs


def _split_start(name, copies_of, srcs, land_shapes, n_copies=None):
    n = len(srcs)
    k = 3 * n if n_copies is None else n_copies

    def body(*refs):
        src_refs, land_refs = refs[:n], refs[n:2 * n]
        send_sems, recv_sems = refs[2 * n], refs[2 * n + 1]
        token = refs[-1]
        sends, _ = copies_of(src_refs, land_refs, send_sems, recv_sems)
        for cp in sends:
            cp.start()
        token[...] = jnp.zeros_like(token)

    lands = [_hbm(s) for s in land_shapes]
    dma = pltpu.SemaphoreType.DMA
    res = pl.pallas_call(
        body, name=name,
        out_shape=(dma((k,)), dma((k,)), *[pltpu.HBM(s.shape, s.dtype) for s in srcs],
                   *[pltpu.HBM(s.shape, s.dtype) for s in land_shapes], jax.ShapeDtypeStruct((8, 128), F32)),
        in_specs=[HBM] * (2 * n),
        out_specs=(SEM, SEM, *([HBM] * (2 * n)), pl.BlockSpec(memory_space=pltpu.VMEM)),
        input_output_aliases={i: 2 + i for i in range(2 * n)},
        compiler_params=SPLIT_COPY,
    )(*[_hbm(s) for s in srcs], *lands)
    return res[0], res[1], list(res[2:2 + n]), list(res[2 + n:2 + 2 * n]), res[-1]


def _split_wait(name, copies_of, send_sems, recv_sems, srcs, lands, after):
    n = len(srcs)

    def body(*refs):
        src_refs, land_refs = refs[:n], refs[n:2 * n]
        sends, recvs = copies_of(src_refs, land_refs, refs[2 * n], refs[2 * n + 1])
        for cp in sends:
            cp.wait_send()
        for cp in recvs:
            cp.wait_recv()

    res = pl.pallas_call(
        body, name=name,
        out_shape=tuple(pltpu.HBM(s.shape, s.dtype) for s in list(srcs) + list(lands)),
        in_specs=[HBM] * (2 * n) + [SEM, SEM] + [pl.BlockSpec(memory_space=pl.ANY)] * len(after),
        out_specs=tuple([HBM] * (2 * n)),
        input_output_aliases={i: i for i in range(2 * n)},
        compiler_params=SPLIT_COPY,
    )(*srcs, *lands, send_sems, recv_sems, *after)
    return list(res[:n]), list(res[n:])


def _gather_finish(lands):
    n = len(lands)

    def body(*refs):
        outs = refs[n:2 * n]
        d2d_send, d2d_recv = refs[2 * n:]
        x, y, c = _mesh_pos()
        chips = _other_chips(x, y)
        sibling = (x, y, 1 - c)
        sends = []
        for w, out in enumerate(outs):
            hr = out.shape[1] // 2
            for j, (px, py) in enumerate(chips):
                landed = out.at[2 * px + py, pl.ds(c * hr, hr)]
                cp = _remote(landed, landed, d2d_send.at[3 * w + j], d2d_recv.at[3 * w + j], sibling)
                cp.start()
                sends.append(cp)
        for w, out in enumerate(outs):
            hr = out.shape[1] // 2
            for j, (px, py) in enumerate(chips):
                other = out.at[2 * px + py, pl.ds((1 - c) * hr, hr)]
                _remote(other, other, d2d_send.at[3 * w + j], d2d_recv.at[3 * w + j], sibling).wait_recv()
        for cp in sends:
            cp.wait_send()

    dma = pltpu.SemaphoreType.DMA
    return pl.pallas_call(
        body, name="gather_finish",
        out_shape=[jax.ShapeDtypeStruct(l.shape, l.dtype) for l in lands],
        in_specs=[HBM] * n, out_specs=[HBM] * n,
        input_output_aliases={i: i for i in range(n)},
        scratch_shapes=[dma((3 * n,)), dma((3 * n,))],
    )(*lands)


def _adamw(w, g, m, v):
    m = ADAM_B1 * m + (1.0 - ADAM_B1) * g
    v = ADAM_B2 * v + (1.0 - ADAM_B2) * (g * g)
    m_hat = m / (1.0 - ADAM_B1 ** ADAM_STEP)
    v_hat = v / (1.0 - ADAM_B2 ** ADAM_STEP)
    delta = -ADAM_LR * (m_hat / (jnp.sqrt(v_hat) + ADAM_EPS) + ADAM_WD * w)
    return delta, m, v


def _adamw_big(partials, w, m, v):
    rows, cols = w.shape
    tr = 256 if rows % 256 == 0 else rows

    def body(p_ref, w_ref, m_ref, v_ref, g_ref, d_ref, m2_ref, v2_ref):
        g = ((p_ref[0].astype(F32) + p_ref[1].astype(F32)) + p_ref[2].astype(F32)) + p_ref[3].astype(F32)
        g_ref[...] = g
        d_ref[...], m2_ref[...], v2_ref[...] = _adamw(w_ref[...], g, m_ref[...], v_ref[...])

    blk = pl.BlockSpec((tr, cols), lambda i: (i, 0))
    return pl.pallas_call(
        body, name="adamw_big", grid=(rows // tr,),
        in_specs=[pl.BlockSpec((N_CHIPS, tr, cols), lambda i: (0, i, 0)), blk, blk, blk],
        out_specs=[blk] * 4, out_shape=[jax.ShapeDtypeStruct((rows, cols), F32)] * 4,
        compiler_params=_params("parallel"),
    )(partials, w, m, v)


def _sum_devices(gathered, rows):
    cols = gathered.shape[1]

    def body(g_ref, o_ref):
        acc = g_ref[0:rows]
        for d in range(1, N_DEV):
            acc = acc + g_ref[d * rows:(d + 1) * rows]
        o_ref[...] = acc

    return pl.pallas_call(
        body, name="sum_devices", out_shape=jax.ShapeDtypeStruct((rows, cols), F32),
        in_specs=[pl.BlockSpec(memory_space=pltpu.VMEM)], out_specs=pl.BlockSpec(memory_space=pltpu.VMEM),
        compiler_params=pltpu.CompilerParams(vmem_limit_bytes=VMEM_LIMIT_V7X),
    )(gathered)


def _adamw_small(quads):
    n = len(quads)

    def body(*refs):
        ins, outs = refs[:4 * n], refs[4 * n:]
        for t in range(n):
            w, g, m, v = (r[...] for r in ins[4 * t:4 * t + 4])
            outs[3 * t][...], outs[3 * t + 1][...], outs[3 * t + 2][...] = _adamw(w, g, m, v)

    flat = [a for q in quads for a in q]
    vm = pl.BlockSpec(memory_space=pltpu.VMEM)
    res = pl.pallas_call(
        body, name="adamw_small",
        out_shape=[jax.ShapeDtypeStruct(q[0].shape, F32) for q in quads for _ in range(3)],
        in_specs=[vm] * (4 * n), out_specs=[vm] * (3 * n),
    )(*flat)
    return [tuple(res[3 * t:3 * t + 3]) for t in range(n)]


SMALL_PACK_ROWS = 96
_WEIGHTS = ['meta_tokens', 'g_pre_mix', 'w_in', 'conv_w', 'conv_b', 'w_a', 'b_a', 'w_x', 'b_x', 'lru_lambda',
            'attn_sinks', 'w_out', 'g_post_mix', 'g_pre_ffn', 'w_ff1', 'w_ff2', 'g_post_ffn']
_BIG = ['w_in', 'w_out', 'w_ff1', 'w_ff2']


def _pack_small(dmeta, g, loss):
    z = lambda r, c: jnp.zeros((r, c), F32)
    rows = [
        dmeta,
        g['g_pre_mix'], g['g_post_mix'], g['g_pre_ffn'], g['g_post_ffn'],
        jnp.concatenate([g['conv_w'], z(4, 512)], axis=1),
        jnp.concatenate([g['conv_b'], g['b_a']], axis=1),
        jnp.concatenate([g['b_x'], g['lru_lambda']], axis=1),
        jnp.concatenate([g['attn_sinks'], z(1, D_MODEL - ATTN_HEADS)], axis=1),
        jnp.concatenate([loss, z(1, D_MODEL - 1)], axis=1),
        z(4, D_MODEL),
        g['w_a'].reshape(32, D_MODEL), g['w_x'].reshape(32, D_MODEL),
    ]
    return jnp.concatenate(rows, axis=0)


def _unpack_small(s, chip):
    return dict(
        meta_tokens=lax.dynamic_slice(s[0:16], (0, chip * 256), (16, 256)),
        g_pre_mix=s[16:17], g_post_mix=s[17:18], g_pre_ffn=s[18:19], g_post_ffn=s[19:20],
        conv_w=lax.dynamic_slice(s[20:24], (0, chip * 128), (4, 128)).reshape(1, 4, 128),
        conv_b=s[24:25, :512], b_a=s[24:25, 512:], b_x=s[25:26, :512], lru_lambda=s[25:26, 512:],
        attn_sinks=s[26:27, :ATTN_HEADS], loss=s[27, 0],
        w_a=s[32:64].reshape(1, LRU_BLOCKS, LRU_BLOCK, LRU_BLOCK),
        w_x=s[64:96].reshape(1, LRU_BLOCKS, LRU_BLOCK, LRU_BLOCK))


def _as2d(a):
    if a.ndim == 2:
        return a
    return a.reshape(-1, a.shape[-1])


def kernel(x, meta_tokens, g_pre_mix, w_in, conv_w, conv_b, w_a, b_a, w_x, b_x, lru_lambda, attn_sinks, w_out, g_post_mix, g_pre_ffn, w_ff1, w_ff2, g_post_ffn, loss_target, m_meta_tokens, m_g_pre_mix, m_w_in, m_conv_w, m_conv_b, m_w_a, m_b_a, m_w_x, m_b_x, m_lru_lambda, m_attn_sinks, m_w_out, m_g_post_mix, m_g_pre_ffn, m_w_ff1, m_w_ff2, m_g_post_ffn, v_meta_tokens, v_g_pre_mix, v_w_in, v_conv_w, v_conv_b, v_w_a, v_b_a, v_w_x, v_b_x, v_lru_lambda, v_attn_sinks, v_w_out, v_g_post_mix, v_g_pre_ffn, v_w_ff1, v_w_ff2, v_g_post_ffn):
    weights = dict(meta_tokens=meta_tokens, g_pre_mix=g_pre_mix, w_in=w_in, conv_w=conv_w, conv_b=conv_b, w_a=w_a,
                   b_a=b_a, w_x=w_x, b_x=b_x, lru_lambda=lru_lambda, attn_sinks=attn_sinks, w_out=w_out,
                   g_post_mix=g_post_mix, g_pre_ffn=g_pre_ffn, w_ff1=w_ff1, w_ff2=w_ff2, g_post_ffn=g_post_ffn)
    mom1 = dict(zip(_WEIGHTS, [m_meta_tokens, m_g_pre_mix, m_w_in, m_conv_w, m_conv_b, m_w_a, m_b_a, m_w_x, m_b_x,
                               m_lru_lambda, m_attn_sinks, m_w_out, m_g_post_mix, m_g_pre_ffn, m_w_ff1, m_w_ff2,
                               m_g_post_ffn]))
    mom2 = dict(zip(_WEIGHTS, [v_meta_tokens, v_g_pre_mix, v_w_in, v_conv_w, v_conv_b, v_w_a, v_b_a, v_w_x, v_b_x,
                               v_lru_lambda, v_attn_sinks, v_w_out, v_g_post_mix, v_g_pre_ffn, v_w_ff1, v_w_ff2,
                               v_g_post_ffn]))
    xi, yi, ci = _mesh_pos()
    chip = 2 * xi + yi

    tiny = jnp.concatenate([meta_tokens, jnp.pad(conv_w[0], ((0, 4), (0, 128)))], axis=0)
    chip_arr = jnp.reshape(chip, (1,)).astype(jnp.int32)
    big2d = lambda a, name: a[0].T if name == 'w_in' else a[0]
    shards, lands = zip(*[_prep_shard(big2d(weights[n], n), chip_arr) for n in _BIG])
    g_in, g_tiny = _gather_weights(shards[:1], lands[:1], tiny, _prep_tiny(tiny, chip_arr))
    w_in_full = g_in.reshape(IN_WIDTH, D_MODEL)
    meta_full = jnp.concatenate([g_tiny[j, :N_META] for j in range(N_CHIPS)], axis=1)
    conv_w_full = jnp.concatenate([g_tiny[j, N_META:N_META + 4, :128] for j in range(N_CHIPS)], axis=1)
    g_send, g_recv, late_thru, late_lands, token = _split_start(
        "gather_late_start", _gather_copies, shards[1:], lands[1:])

    def late_weights(after):
        _, landed = _split_wait("gather_late_wait", _gather_copies, g_send, g_recv, late_thru, late_lands, after)
        g_out, g_f1, g_f2 = _gather_finish(landed)
        return g_out.reshape(D_MODEL, D_MODEL), g_f1, g_f2

    pos = jnp.stack([ci, chip]).astype(jnp.int32)
    ffn = {}


    def on_ffn_grads(dw1, dw2):
        parts = [dw1, dw2]
        lands = [lax.empty((p.shape[0], p.shape[1] // 2, p.shape[2]), p.dtype) for p in parts]
        ffn['sib'] = _split_start("sibling_ffn_start", _sibling_copies, parts, lands, len(parts))
        return ffn['sib'][4]

    def on_outproj_bwd(dattn):
        send, recv, thru, lands, _ = ffn['sib']
        parts, from_sibling = _split_wait("sibling_ffn_wait", _sibling_copies, send, recv, thru, lands, [dattn])
        cparts_ffn, lands_ffn = zip(*[_chip_presum(p, r, pos) for p, r in zip(parts, from_sibling)])
        ffn['send'], ffn['recv'], ffn['thru'], ffn['lands'], token3 = _split_start(
            "scatter_ffn_start", _scatter_copies, cparts_ffn, lands_ffn)
        return token3

    def on_mixer_grads(dw_in, dw_out):
        parts = [dw_in.reshape(N_CHIPS, IN_WIDTH // N_CHIPS, D_MODEL),
                 dw_out.reshape(N_CHIPS, D_MODEL // N_CHIPS, D_MODEL)]
        cparts, lands = zip(*[_chip_presum(p, r, pos) for p, r in zip(parts, _sibling_exchange(parts, pos))])
        ffn['mixer'] = _split_start("scatter_mixer_start", _scatter_copies, cparts, lands)
        return ffn['mixer'][4]

    head = jnp.concatenate([jnp.zeros((PAD_ROWS, D_MODEL), F32), meta_full], axis=0)
    loss, dx, dhead, grads = _local_step(head, x[0], loss_target[0], g_pre_mix, w_in_full, conv_w_full, conv_b, w_a[0],
                                         b_a, w_x[0], b_x, lru_lambda, attn_sinks, g_post_mix, g_pre_ffn, g_post_ffn,
                                         late_weights, on_ffn_grads, on_outproj_bwd, on_mixer_grads, token)
    grad_x = dx[None]

    pack = _pack_small(dhead[PAD_ROWS:], grads, loss)
    dev = jnp.reshape(4 * xi + 2 * yi + ci, (1,)).astype(jnp.int32)
    s_send, s_recv, s_thru, s_lands, token5 = _split_start(
        "gather_small_start", _all_peers_copies, [pack], [_prep_tiny(pack, dev, N_DEV)], N_DEV - 1)

    send, recv, thru, lands, _ = ffn['mixer']
    mixer_cparts, mixer_lands = _split_wait("scatter_mixer_wait", _scatter_copies, send, recv, thru, lands, [token5])
    ffn_cparts, ffn_lands = _split_wait("scatter_ffn_wait", _scatter_copies, ffn['send'], ffn['recv'], ffn['thru'],
                                        ffn['lands'], mixer_lands)
    chip_partials = _scatter_partials([], [], mixer_cparts + ffn_cparts, mixer_lands + ffn_lands)

    g_out_d, delta, new_m, new_v = {}, {}, {}, {}
    for name, part in zip(_BIG, chip_partials):
        shp = weights[name].shape
        res = _adamw_big(part, big2d(weights[name], name), big2d(mom1[name], name), big2d(mom2[name], name))
        g_out_d[name], delta[name], new_m[name], new_v[name] = (big2d(r[None], name).reshape(shp) for r in res)

    _, (gathered,) = _split_wait("gather_small_wait", _all_peers_copies, s_send, s_recv, s_thru, s_lands,
                                 [g_out_d[n] for n in _BIG])
    small = _unpack_small(_sum_devices(gathered.reshape(N_DEV * SMALL_PACK_ROWS, D_MODEL), SMALL_PACK_ROWS), chip)
    loss = small['loss']
    small_names = [n for n in _WEIGHTS if n not in _BIG]
    quads = [(_as2d(weights[n]), _as2d(small[n]), _as2d(mom1[n]), _as2d(mom2[n])) for n in small_names]
    for name, (d, m2, v2) in zip(small_names, _adamw_small(quads)):
        shp = weights[name].shape
        g_out_d[name] = small[name].reshape(shp)
        delta[name], new_m[name], new_v[name] = d.reshape(shp), m2.reshape(shp), v2.reshape(shp)

    return (loss, grad_x, *[g_out_d[n] for n in _WEIGHTS], *[delta[n] for n in _WEIGHTS],
            *[new_m[n] for n in _WEIGHTS], *[new_v[n] for n in _WEIGHTS])
```

```python
import numpy as np
import jax
import jax.numpy as jnp
from jax import lax
from jax.experimental import pallas as pl
from jax.experimental.pallas import tpu as pltpu

F32 = jnp.float32
BF16 = jnp.bfloat16

D_MODEL = 1024
N_META = 16
BLOCK = 128
PAD_ROWS = BLOCK - N_META
HEAD_DIM = 64
ATTN_HEADS = 8
GQA_GROUP = 4
ATTN_WIDTH = 512
KV_WIDTH = 128
QKV_WIDTH = ATTN_WIDTH + 2 * KV_WIDTH
LRU_WIDTH = 512
LRU_BLOCKS = 8
LRU_BLOCK = 64
LRU_C = 8.0
IN_WIDTH = 1792
D_FF = 4096
N_CHIPS = 4
FF_CHUNK = D_FF // N_CHIPS
EPS = 1e-6
NEG = -1e30

ADAM_LR = 0.001
ADAM_B1 = 0.9
ADAM_B2 = 0.999
ADAM_EPS = 1e-08
ADAM_WD = 0.01
ADAM_STEP = 10

VMEM_LIMIT_V7X = 62 * 1024 * 1024
MESH = pl.DeviceIdType.MESH

NT = (((1,), (1,)), ((), ()))
TN = (((0,), (0,)), ((), ()))


def _row_tile(tp):
    return 640 if tp % 640 == 0 else BLOCK


def _wgrad_row_tile(tp):
    return 1664 if tp % 1664 == 0 else _row_tile(tp)


def _params(*sem):
    return pltpu.CompilerParams(dimension_semantics=sem, vmem_limit_bytes=VMEM_LIMIT_V7X)


def _dot(a, b):
    return jnp.dot(a, b, preferred_element_type=F32)


def _dot_nt(a, b):
    return lax.dot_general(a, b, NT, preferred_element_type=F32)


def _dot_tn(a, b):
    return lax.dot_general(a, b, TN, preferred_element_type=F32)


def _rms(x):
    rs = lax.rsqrt(jnp.mean(x * x, axis=-1, keepdims=True) + EPS)
    return x * rs, rs


def _rms_bwd(xhat, rs, g, dy):
    dyg = dy * g
    dx = rs * (dyg - xhat * jnp.mean(dyg * xhat, axis=-1, keepdims=True))
    dg = jnp.sum(dy * xhat, axis=0, keepdims=True)
    return dx, dg


def _gelu(x):
    k = 0.7978845608028654
    t = jnp.tanh(x * (k + (k * 0.044715) * (x * x)))
    return (0.5 * x) * (1.0 + t), t


def _gelu_grad(x, t):
    k = 0.7978845608028654
    return 0.5 * (1.0 + t) + 0.5 * x * (1.0 - t * t) * k * (1.0 + 3 * 0.044715 * x * x)


def _sigmoid(x):
    return 0.5 * jnp.tanh(0.5 * x) + 0.5


def _one_minus_exp2(y):
    t = jnp.tanh(y)
    return (-2.0 * t) / (1.0 - t)


def _softplus(x):
    return jnp.maximum(x, 0.0) + jnp.log1p(jnp.exp(-jnp.abs(x)))


def _seq_specs(tr, delay=0):
    qb = tr // BLOCK
    tile = lambda i: jnp.maximum(i - delay, 0)
    return [pl.BlockSpec((BLOCK, D_MODEL), lambda i, *_, s=s: (jnp.maximum(tile(i) * qb + s - 1, 0), 0))
            for s in range(qb)]


def _seq_tile(head, pieces, i):
    first = jnp.where(i == 0, head, pieces[0][...])
    return jnp.concatenate([first] + [p[...] for p in pieces[1:]], axis=0)


def _inproj_fwd(head, x, g, w_in, token):
    tp = BLOCK + x.shape[0]
    tr = _row_tile(tp)
    qb = tr // BLOCK

    def body(*refs):
        head_ref, pieces = refs[0], refs[1:1 + qb]
        g_ref, w_ref, _, u_ref, qkv_ref, xr_ref, yr_ref = refs[1 + qb:]
        xhat, _ = _rms(_seq_tile(head_ref[...], pieces, pl.program_id(0)))
        u = (xhat * g_ref[...]).astype(BF16)
        u_ref[...] = u
        z = _dot_nt(u, w_ref[...])
        qkv_ref[...] = z[:, :QKV_WIDTH].astype(BF16)
        xr_ref[...] = z[:, QKV_WIDTH:QKV_WIDTH + LRU_WIDTH]
        yr_ref[...] = z[:, QKV_WIDTH + LRU_WIDTH:]

    row = lambda w: pl.BlockSpec((tr, w), lambda i: (i, 0))
    full = lambda a: pl.BlockSpec(a.shape, lambda i: (0,) * a.ndim)
    return pl.pallas_call(
        body, name="inproj_fwd", grid=(tp // tr,),
        in_specs=[full(head)] + _seq_specs(tr) + [full(g), full(w_in), full(token)],
        out_specs=[row(D_MODEL), row(QKV_WIDTH), row(LRU_WIDTH), row(LRU_WIDTH)],
        out_shape=[jax.ShapeDtypeStruct((tp, D_MODEL), BF16), jax.ShapeDtypeStruct((tp, QKV_WIDTH), BF16),
                   jax.ShapeDtypeStruct((tp, LRU_WIDTH), F32), jax.ShapeDtypeStruct((tp, LRU_WIDTH), F32)],
        compiler_params=_params("parallel"),
    )(head, *([x] * qb), g, w_in, token)


GROUP_ROWS = GQA_GROUP * BLOCK


def _attn_bias():
    j = np.arange(2 * BLOCK)[:, None]
    i = np.arange(BLOCK)[None, :]
    band = (j - i >= 1) & (j - i <= BLOCK)
    out = []
    for n in range(3):
        ok = band & ((n - 1) * BLOCK + j >= PAD_ROWS) if n < 2 else band
        out.append(np.tile(np.where(ok, 0.0, NEG).astype(np.float32), (1, GQA_GROUP)))
    return jnp.asarray(np.stack(out))


def _stack_heads(a, g):
    heads = range(GQA_GROUP * g, GQA_GROUP * (g + 1))
    return jnp.concatenate([a[:, h * HEAD_DIM:(h + 1) * HEAD_DIM] for h in heads], axis=0)


def _unstack_heads(groups):
    return jnp.concatenate([p[h * BLOCK:(h + 1) * BLOCK] for p in groups for h in range(GQA_GROUP)], axis=1)


def _attn_probs_t(k_g, qg, bias, sink_row):
    st = _dot_nt(k_g, qg) + bias
    m = jnp.maximum(jnp.max(st, axis=0, keepdims=True), sink_row)
    p = jnp.exp(st - m)
    es = jnp.exp(sink_row - m)
    inv = 1.0 / (jnp.sum(p, axis=0, keepdims=True) + es)
    return p * inv, es * inv


def _attn_consts(sinks):
    return jnp.repeat(sinks.reshape(ATTN_HEADS), BLOCK).reshape(ATTN_HEADS // GQA_GROUP, GROUP_ROWS), _attn_bias()


_SINK_SPEC = pl.BlockSpec((ATTN_HEADS // GQA_GROUP, GROUP_ROWS), lambda n: (0, 0))
_BIAS_SPEC = pl.BlockSpec((3, 2 * BLOCK, GROUP_ROWS), lambda n: (0, 0, 0))
_QSCALE = HEAD_DIM ** -0.5


def _kv_specs(tr):
    qb = tr // BLOCK
    prev = lambda col: pl.BlockSpec((BLOCK, KV_WIDTH), lambda t: (jnp.maximum(t * qb - 1, 0), col))
    cur = lambda col: pl.BlockSpec((tr, KV_WIDTH), lambda t: (t, col))
    return [prev(4), cur(4), prev(5), cur(5)]


def _block_bias(b_ref, t, qb, i):
    return b_ref[2] if i >= 2 else b_ref[jnp.minimum(t * qb + i, 2)]


def _attn_fwd(qkv, sinks):
    tp = qkv.shape[0]
    tr = _row_tile(tp)
    qb = tr // BLOCK
    sink_rows, bias = _attn_consts(sinks)

    def body(s_ref, b_ref, q_ref, kp_ref, kc_ref, vp_ref, vc_ref, o_ref):
        t = pl.program_id(0)
        k_all = jnp.concatenate([kp_ref[...], kc_ref[...]], axis=0)
        v_all = jnp.concatenate([vp_ref[...], vc_ref[...]], axis=0)
        for i in range(qb):
            rows = slice(i * BLOCK, (i + 1) * BLOCK)
            q = q_ref[rows]
            k2, v2 = k_all[i * BLOCK:(i + 2) * BLOCK], v_all[i * BLOCK:(i + 2) * BLOCK]
            bias_n = _block_bias(b_ref, t, qb, i)
            outs = []
            for g in range(ATTN_HEADS // GQA_GROUP):
                cols = slice(g * HEAD_DIM, (g + 1) * HEAD_DIM)
                qg = _stack_heads(q, g) * jnp.asarray(_QSCALE, BF16)
                p, _ = _attn_probs_t(k2[:, cols], qg, bias_n, s_ref[g:g + 1])
                outs.append(_dot_tn(p.astype(BF16), v2[:, cols]))
            o_ref[rows] = _unstack_heads(outs).astype(BF16)

    return pl.pallas_call(
        body, name="attn_fwd", grid=(tp // tr,),
        in_specs=[_SINK_SPEC, _BIAS_SPEC, pl.BlockSpec((tr, ATTN_WIDTH), lambda t: (t, 0))] + _kv_specs(tr),
        out_specs=pl.BlockSpec((tr, ATTN_WIDTH), lambda t: (t, 0)),
        out_shape=jax.ShapeDtypeStruct((tp, ATTN_WIDTH), BF16),
        compiler_params=_params("parallel"),
    )(sink_rows, bias, qkv, qkv, qkv, qkv, qkv)


def _conv_taps(x, halo):
    ext = jnp.concatenate([halo, x], axis=0)
    return [ext[8:] if k == 3 else pltpu.roll(ext, 3 - k, 0)[8:] for k in range(4)]


def _lru_gates(xc, wa, ba, wx, bx, sp):
    xb = xc.astype(BF16)
    r = _sigmoid(_dot(xb, wa) + ba)
    ig = _sigmoid(_dot(xb, wx) + bx)
    log_a = (-LRU_C * sp) * r
    a = jnp.exp(log_a)
    mult = jnp.sqrt(_one_minus_exp2(log_a))
    return xb, r, ig, a, mult


SUBLANES = 8


def _scan_fwd(a, b, h_in):
    n, width = a.shape
    a, b = (v.reshape(n // SUBLANES, SUBLANES, width) for v in (a, b))
    in_group = lax.broadcasted_iota(jnp.int32, a.shape, 1)
    for d in (1, 2, 4):
        keep = in_group >= d
        b = jnp.where(keep, a * pltpu.roll(b, d, 1) + b, b)
        a = jnp.where(keep, a * pltpu.roll(a, d, 1), a)
    a, b = a.reshape(n, width), b.reshape(n, width)
    out, carry = [], h_in
    for g in range(0, n, SUBLANES):
        h = a[g:g + SUBLANES] * carry + b[g:g + SUBLANES]
        out.append(h)
        carry = h[SUBLANES - 1:]
    return jnp.concatenate(out, axis=0)


def _scan_rev(c, b, g_in):
    n, width = c.shape
    c, b = (v.reshape(n // SUBLANES, SUBLANES, width) for v in (c, b))
    in_group = lax.broadcasted_iota(jnp.int32, c.shape, 1)
    for d in (1, 2, 4):
        keep = in_group < SUBLANES - d
        b = jnp.where(keep, b + c * pltpu.roll(b, SUBLANES - d, 1), b)
        c = jnp.where(keep, c * pltpu.roll(c, SUBLANES - d, 1), c)
    c, b = c.reshape(n, width), b.reshape(n, width)
    out, carry = [], g_in
    for g in range(n - SUBLANES, -1, -SUBLANES):
        r = b[g:g + SUBLANES] + c[g:g + SUBLANES] * carry
        out.append(r)
        carry = r[:1]
    return jnp.concatenate(out[::-1], axis=0)


def _lru_fwd(xr, yr, conv_w, conv_b, wa, ba, wx, bx, lam):
    tp = xr.shape[0]
    tr = _row_tile(tp)
    qb = tr // BLOCK

    def body(xr_ref, yr_ref, cw_ref, cb_ref, wa_ref, ba_ref, wx_ref, bx_ref, lam_ref, hr_ref, rec_ref, halo, hprev):
        t = pl.program_id(0)

        @pl.when(t == 0)
        def _():
            halo[...] = jnp.zeros_like(halo)
            hprev[...] = jnp.zeros_like(hprev)

        cw, cb = cw_ref[...], cb_ref[...]
        wa_m, ba_v, wx_m, bx_v = wa_ref[...], ba_ref[...], wx_ref[...], bx_ref[...]
        sp = _softplus(-lam_ref[...])
        before, h_last = halo[...], hprev[0:1]
        for i in range(qb):
            rows = slice(i * BLOCK, (i + 1) * BLOCK)
            x = xr_ref[rows]
            taps = _conv_taps(x, before)
            before = x[BLOCK - 8:]
            xc = cb + sum(cw[k:k + 1] * taps[k] for k in range(4))
            _, _, ig, a, mult = _lru_gates(xc, wa_m, ba_v, wx_m, bx_v, sp)
            u = mult * (ig * xc)
            if i == 0:
                pos = t * tr + lax.broadcasted_iota(jnp.int32, xc.shape, 0)
                u = jnp.where(pos >= PAD_ROWS, u, 0.0)
            h = _scan_fwd(a, u, h_last)
            h_last = h[BLOCK - 1:]
            hr_ref[rows] = h
            gl, _ = _gelu(yr_ref[rows])
            rec_ref[rows] = (gl * h).astype(BF16)
        halo[...] = before
        hprev[0:1] = h_last

    blk = pl.BlockSpec((tr, LRU_WIDTH), lambda t: (t, 0))
    full = lambda a: pl.BlockSpec(a.shape, lambda t: (0,) * a.ndim)
    small = [conv_w, conv_b, wa, ba, wx, bx, lam]
    return pl.pallas_call(
        body, name="lru_fwd", grid=(tp // tr,),
        in_specs=[blk, blk] + [full(a) for a in small],
        out_specs=[blk, blk],
        out_shape=[jax.ShapeDtypeStruct((tp, LRU_WIDTH), F32), jax.ShapeDtypeStruct((tp, LRU_WIDTH), BF16)],
        scratch_shapes=[pltpu.VMEM((8, LRU_WIDTH), F32), pltpu.VMEM((8, LRU_WIDTH), F32)],
        compiler_params=_params("arbitrary"),
    )(xr, yr, *small)


def _outproj_fwd(attn, rec, w_out, head, x, g_post_mix, g_pre_ffn):
    tp = attn.shape[0]
    tr = _row_tile(tp)
    qb = tr // BLOCK

    def body(*refs):
        a_ref, r_ref, w_ref, head_ref = refs[:4]
        pieces = refs[4:4 + qb]
        gm_ref, gf_ref, mix_ref, h1_ref, u1_ref = refs[4 + qb:]
        mix = _dot(a_ref[...], w_ref[:ATTN_WIDTH]) + _dot(r_ref[...], w_ref[ATTN_WIDTH:])
        mix_ref[...] = mix
        mhat, _ = _rms(mix)
        h1 = _seq_tile(head_ref[...], pieces, pl.program_id(0)) + mhat * gm_ref[...]
        h1_ref[...] = h1
        hhat, _ = _rms(h1)
        u1_ref[...] = (hhat * gf_ref[...]).astype(BF16)

    row = lambda w: pl.BlockSpec((tr, w), lambda i: (i, 0))
    full = lambda a: pl.BlockSpec(a.shape, lambda i: (0,) * a.ndim)
    return pl.pallas_call(
        body, name="outproj_fwd", grid=(tp // tr,),
        in_specs=[row(ATTN_WIDTH), row(LRU_WIDTH), full(w_out), full(head)] + _seq_specs(tr)
        + [full(g_post_mix), full(g_pre_ffn)],
        out_specs=[row(D_MODEL), row(D_MODEL), row(D_MODEL)],
        out_shape=[jax.ShapeDtypeStruct((tp, D_MODEL), F32), jax.ShapeDtypeStruct((tp, D_MODEL), F32),
                   jax.ShapeDtypeStruct((tp, D_MODEL), BF16)],
        compiler_params=_params("parallel"),
    )(attn, rec, w_out, head, *([x] * qb), g_post_mix, g_pre_ffn)


def _resident(a):
    return pl.BlockSpec(a.shape, lambda *_: (0,) * a.ndim, pipeline_mode=pl.Buffered(1))


def _ffn_fwd(u1, w1, w2, h1, tgt, g_post_ffn):
    tp = h1.shape[0]
    tr = _row_tile(tp)
    qb, nt = tr // BLOCK, tp // tr

    def body(*refs):
        u_ref, w1_ref, w2_ref, h1_ref = refs[:4]
        t_pieces = refs[4:4 + qb]
        g_ref, r1_ref, dy_ref, df2_ref, loss_ref, dg_ref, acc = refs[4 + qb:]
        i, c = pl.program_id(0), pl.program_id(1)
        cur = i % 2

        @pl.when((i == 0) & (c == 0))
        def _():
            loss_ref[...] = jnp.zeros_like(loss_ref)
            dg_ref[...] = jnp.zeros_like(dg_ref)
            acc[1] = jnp.zeros((tr, D_MODEL), F32)

        def matmuls():
            r = jnp.maximum(_dot(u_ref[...], w1_ref[c]), 0.0)
            r1_ref[...] = r.astype(BF16)
            return _dot((r * r).astype(BF16), w2_ref[c])

        def finish_previous_tile(valid):
            g = g_ref[...]
            fhat, rs = _rms(acc[1 - cur])
            h2 = h1_ref[...] + fhat * g
            rows = (i - 1) * tr + lax.broadcasted_iota(jnp.int32, h2.shape, 0)
            tgt_tile = jnp.concatenate([p[...] for p in t_pieces], axis=0)
            err = jnp.where((rows >= BLOCK) & valid, h2 - tgt_tile, 0.0)
            dy = err * (1.0 / D_MODEL)
            dy_ref[...] = dy
            loss_ref[...] += (0.5 / D_MODEL) * jnp.sum(err * err)
            df2, dg = _rms_bwd(fhat, rs, g, dy)
            df2_ref[...] = df2.astype(BF16)
            dg_ref[...] += dg

        @pl.when((c == 0) & (i < nt))
        def _():
            finish_previous_tile(i >= 1)
            acc[cur] = matmuls()

        @pl.when((c > 0) & (i < nt))
        def _():
            acc[cur] += matmuls()

        @pl.when((c == 0) & (i == nt))
        def _():
            finish_previous_tile(True)

    last = nt - 1
    this_row = pl.BlockSpec((tr, D_MODEL), lambda i, c: (jnp.minimum(i, last), 0))
    prev_row = pl.BlockSpec((tr, D_MODEL), lambda i, c: (jnp.maximum(i - 1, 0), 0))
    full = lambda a: pl.BlockSpec(a.shape, lambda i, c: (0,) * a.ndim)
    return pl.pallas_call(
        body, name="ffn_fwd", grid=(nt + 1, N_CHIPS),
        in_specs=[this_row, _resident(w1), _resident(w2), prev_row] + _seq_specs(tr, delay=1) + [full(g_post_ffn)],
        out_specs=[pl.BlockSpec((tr, FF_CHUNK), lambda i, c: (jnp.minimum(i, last), jnp.where(i < nt, c, N_CHIPS - 1))),
                   prev_row, prev_row,
                   pl.BlockSpec((1, 1), lambda i, c: (0, 0)), pl.BlockSpec((1, D_MODEL), lambda i, c: (0, 0))],
        out_shape=[jax.ShapeDtypeStruct((tp, D_FF), BF16), jax.ShapeDtypeStruct((tp, D_MODEL), F32),
                   jax.ShapeDtypeStruct((tp, D_MODEL), BF16), jax.ShapeDtypeStruct((1, 1), F32),
                   jax.ShapeDtypeStruct((1, D_MODEL), F32)],
        scratch_shapes=[pltpu.VMEM((2, tr, D_MODEL), F32)],
        compiler_params=_params("arbitrary", "arbitrary"),
    )(u1, w1, w2, h1, *([tgt] * qb), g_post_ffn)


def _ffn_bwd_data(df2, r1, w1, w2, dy, h1, mix, g_pre_ffn, g_post_mix):
    tp = h1.shape[0]
    tr = _row_tile(tp)
    nt = tp // tr

    def body(df2_ref, r1_ref, w1_ref, w2_ref, dy_ref, h1_ref, mix_ref, gf_ref, gm_ref,
             da_ref, dh1_ref, dmix_ref, dgf_ref, dgm_ref, acc):
        i, c = pl.program_id(0), pl.program_id(1)
        cur = i % 2

        @pl.when((i == 0) & (c == 0))
        def _():
            dgf_ref[...] = jnp.zeros_like(dgf_ref)
            dgm_ref[...] = jnp.zeros_like(dgm_ref)
            acc[1] = jnp.zeros((tr, D_MODEL), F32)

        def matmuls():
            df = _dot_nt(df2_ref[...], w2_ref[c])
            da = (df * (2.0 * r1_ref[...].astype(F32))).astype(BF16)
            da_ref[...] = da
            return _dot_nt(da, w1_ref[c])

        def finish_previous_tile(valid):
            hhat, rs = _rms(h1_ref[...])
            dx, dgf = _rms_bwd(hhat, rs, gf_ref[...], acc[1 - cur])
            dh1 = dy_ref[...] + dx
            dh1_ref[...] = dh1
            mhat, rsm = _rms(mix_ref[...])
            dmix, dgm = _rms_bwd(mhat, rsm, gm_ref[...], dh1)
            dmix_ref[...] = dmix.astype(BF16)
            dgf_ref[...] += jnp.where(valid, dgf, 0.0)
            dgm_ref[...] += jnp.where(valid, dgm, 0.0)

        @pl.when((c == 0) & (i < nt))
        def _():
            finish_previous_tile(i >= 1)
            acc[cur] = matmuls()

        @pl.when((c > 0) & (i < nt))
        def _():
            acc[cur] += matmuls()

        @pl.when((c == 0) & (i == nt))
        def _():
            finish_previous_tile(True)

    last = nt - 1
    this_row = pl.BlockSpec((tr, D_MODEL), lambda i, c: (jnp.minimum(i, last), 0))
    prev_row = pl.BlockSpec((tr, D_MODEL), lambda i, c: (jnp.maximum(i - 1, 0), 0))
    chunk = pl.BlockSpec((tr, FF_CHUNK), lambda i, c: (jnp.minimum(i, last), jnp.where(i < nt, c, N_CHIPS - 1)))
    gain = pl.BlockSpec((1, D_MODEL), lambda i, c: (0, 0))
    return pl.pallas_call(
        body, name="ffn_bwd_data", grid=(nt + 1, N_CHIPS),
        in_specs=[this_row, chunk, _resident(w1), _resident(w2), prev_row, prev_row, prev_row, gain, gain],
        out_specs=[chunk, prev_row, prev_row, gain, gain],
        out_shape=[jax.ShapeDtypeStruct((tp, D_FF), BF16), jax.ShapeDtypeStruct((tp, D_MODEL), F32),
                   jax.ShapeDtypeStruct((tp, D_MODEL), BF16), jax.ShapeDtypeStruct((1, D_MODEL), F32),
                   jax.ShapeDtypeStruct((1, D_MODEL), F32)],
        scratch_shapes=[pltpu.VMEM((2, tr, D_MODEL), F32)],
        compiler_params=_params("arbitrary", "arbitrary"),
    )(df2, r1, w1, w2, dy, h1, mix, g_pre_ffn, g_post_mix)


def _ffn_bwd_weights(u1, da1, r1, df2):
    tp = u1.shape[0]
    tr = _wgrad_row_tile(tp)

    def body(u_ref, da_ref, r1_ref, df2_ref, dw1_ref, dw2_ref):
        i = pl.program_id(1)
        r = r1_ref[...].astype(F32)
        p1 = _dot_tn(u_ref[...], da_ref[...])
        p2 = _dot_tn((r * r).astype(BF16), df2_ref[...])

        @pl.when(i == 0)
        def _():
            dw1_ref[0] = p1
            dw2_ref[0] = p2

        @pl.when(i > 0)
        def _():
            dw1_ref[0] += p1
            dw2_ref[0] += p2

    row = pl.BlockSpec((tr, D_MODEL), lambda c, i: (i, 0))
    chunk = pl.BlockSpec((tr, FF_CHUNK), lambda c, i: (i, c))
    return pl.pallas_call(
        body, name="ffn_bwd_weights", grid=(N_CHIPS, tp // tr),
        in_specs=[row, chunk, chunk, row],
        out_specs=[pl.BlockSpec((1, D_MODEL, FF_CHUNK), lambda c, i: (c, 0, 0)),
                   pl.BlockSpec((1, FF_CHUNK, D_MODEL), lambda c, i: (c, 0, 0))],
        out_shape=[jax.ShapeDtypeStruct((N_CHIPS, D_MODEL, FF_CHUNK), F32),
                   jax.ShapeDtypeStruct((N_CHIPS, FF_CHUNK, D_MODEL), F32)],
        compiler_params=_params("parallel", "arbitrary"),
    )(u1, da1, r1, df2)


def _outproj_bwd(dmix, w_out, attn, rec, token):
    tp = dmix.shape[0]
    tr = _wgrad_row_tile(tp)

    def body(dm_ref, w_ref, a_ref, r_ref, _, da_ref, dr_ref, dw_ref):
        i = pl.program_id(0)
        dm = dm_ref[...]
        dcat = _dot_nt(dm, w_ref[...])
        da_ref[...] = dcat[:, :ATTN_WIDTH].astype(BF16)
        dr_ref[...] = dcat[:, ATTN_WIDTH:]
        pa = _dot_tn(a_ref[...], dm)
        pr = _dot_tn(r_ref[...], dm)

        @pl.when(i == 0)
        def _():
            dw_ref[:ATTN_WIDTH] = pa
            dw_ref[ATTN_WIDTH:] = pr

        @pl.when(i > 0)
        def _():
            dw_ref[:ATTN_WIDTH] += pa
            dw_ref[ATTN_WIDTH:] += pr

    row = lambda w: pl.BlockSpec((tr, w), lambda i: (i, 0))
    full = pl.BlockSpec((D_MODEL, D_MODEL), lambda i: (0, 0))
    return pl.pallas_call(
        body, name="outproj_bwd", grid=(tp // tr,),
        in_specs=[row(D_MODEL), full, row(ATTN_WIDTH), row(LRU_WIDTH), pl.BlockSpec(token.shape, lambda i: (0, 0))],
        out_specs=[row(ATTN_WIDTH), row(LRU_WIDTH), full],
        out_shape=[jax.ShapeDtypeStruct((tp, ATTN_WIDTH), BF16), jax.ShapeDtypeStruct((tp, LRU_WIDTH), F32),
                   jax.ShapeDtypeStruct((D_MODEL, D_MODEL), F32)],
        compiler_params=_params("arbitrary"),
    )(dmix, w_out, attn, rec, token)


N_VEC_ROWS = 8


def _lru_bwd(xr, yr, hr, drec, conv_w, conv_b, wa, ba, wx, bx, lam, token):
    tp = xr.shape[0]
    tr = _row_tile(tp)
    qb, nt = tr // BLOCK, tp // tr

    def body(xr_ref, xh_ref, yr_ref, hr_ref, hp_ref, dr_ref, cw_ref, cb_ref, wa_ref, ba_ref, wx_ref, bx_ref, lam_ref, _,
             dxr_ref, dyr_ref, dwa_ref, dwx_ref, vec_ref, g_next, a_next, dxc_next, dsp):
        s = pl.program_id(0)
        t = nt - 1 - s

        @pl.when(s == 0)
        def _():
            g_next[...] = jnp.zeros_like(g_next)
            a_next[...] = jnp.zeros_like(a_next)
            dxc_next[...] = jnp.zeros_like(dxc_next)
            dsp[...] = jnp.zeros_like(dsp)
            dwa_ref[...] = jnp.zeros_like(dwa_ref)
            dwx_ref[...] = jnp.zeros_like(dwx_ref)
            vec_ref[...] = jnp.zeros_like(vec_ref)

        first_tile = t == 0
        cw, cb = cw_ref[...], cb_ref[...]
        lam_v = lam_ref[...]
        sp = _softplus(-lam_v)
        wa_m, ba_v, wx_m, bx_v = wa_ref[...], ba_ref[...], wx_ref[...], bx_ref[...]
        rows = lax.broadcasted_iota(jnp.int32, (BLOCK, LRU_WIDTH), 0)
        col = lambda v: jnp.sum(v, axis=0, keepdims=True)

        g_after, a_after, dxc_after = g_next[0:1], a_next[0:1], dxc_next[...]
        xbs, dgrs, dgis = [], [], []
        vec = [jnp.zeros((1, LRU_WIDTH), F32) for _ in range(N_VEC_ROWS)]
        for i in reversed(range(qb)):
            blk = slice(i * BLOCK, (i + 1) * BLOCK)
            if i == 0:
                x_before = jnp.where(first_tile, 0.0, xh_ref[...])
                h_before = jnp.where(first_tile, 0.0, hp_ref[7:8])
            else:
                x_before = xr_ref[i * BLOCK - 8:i * BLOCK]
                h_before = hr_ref[i * BLOCK - 1:i * BLOCK]
            taps = _conv_taps(xr_ref[blk], x_before)
            xc = cb + sum(cw[k:k + 1] * taps[k] for k in range(4))
            xb, r, ig, a, mult = _lru_gates(xc, wa_m, ba_v, wx_m, bx_v, sp)

            yr_v = yr_ref[blk]
            gl, th = _gelu(yr_v)
            h = hr_ref[blk]
            drec = dr_ref[blk]
            dyr_ref[blk] = (drec * h * _gelu_grad(yr_v, th)).astype(BF16)

            a_up = jnp.where(rows == BLOCK - 1, a_after, pltpu.roll(a, BLOCK - 1, 0))
            g = _scan_rev(a_up, drec * gl, g_after)
            g_after, a_after = g[0:1], a[0:1]

            h_prev = jnp.where(rows == 0, h_before, pltpu.roll(h, 1, 0))
            du, da = g, g * h_prev
            if i == 0:
                real = (t * tr + rows) >= PAD_ROWS
                du, da = jnp.where(real, du, 0.0), jnp.where(real, da, 0.0)
            dmult = du * (ig * xc)
            dig = du * (mult * xc)
            dxc = du * (mult * ig)
            dlog_a = da * a - dmult * (a * a / mult)
            if i == 0:
                dlog_a = jnp.where(real, dlog_a, 0.0)
            dgr = (dlog_a * (-LRU_C * sp)) * (r * (1.0 - r))
            dgi = dig * (ig * (1.0 - ig))
            dgr_b, dgi_b = dgr.astype(BF16), dgi.astype(BF16)
            dxc = dxc + _dot_nt(dgr_b, wa_m) + _dot_nt(dgi_b, wx_m)
            xbs.append(xb)
            dgrs.append(dgr_b)
            dgis.append(dgi_b)

            ext = jnp.concatenate([dxc, dxc_after], axis=0)
            up = [ext[:BLOCK] if j == 0 else pltpu.roll(ext, BLOCK + 8 - j, 0)[:BLOCK] for j in range(4)]
            dxr_ref[blk] = sum(cw[k:k + 1] * up[3 - k] for k in range(4)).astype(BF16)
            dxc_after = dxc[:8]

            for k in range(4):
                vec[k] = vec[k] + col(dxc * taps[k])
            vec[4] = vec[4] + col(dxc)
            vec[5] = vec[5] + col(dgr)
            vec[6] = vec[6] + col(dgi)
            vec[7] = vec[7] + col(dlog_a * (-LRU_C * r))

        g_next[0:1], a_next[0:1], dxc_next[...] = g_after, a_after, dxc_after
        xb_all = jnp.concatenate(xbs, axis=0)
        dwa_ref[...] += _dot_tn(xb_all, jnp.concatenate(dgrs, axis=0))
        dwx_ref[...] += _dot_tn(xb_all, jnp.concatenate(dgis, axis=0))
        for k in range(7):
            vec_ref[k:k + 1] += vec[k]
        dsp[0:1] += vec[7]

        @pl.when(s == nt - 1)
        def _():
            vec_ref[7:8] = dsp[0:1] * (-_sigmoid(-lam_v))

    blk_spec = pl.BlockSpec((tr, LRU_WIDTH), lambda s: (nt - 1 - s, 0))
    rows_before = pl.BlockSpec((8, LRU_WIDTH), lambda s: (jnp.maximum((nt - 1 - s) * (tr // 8) - 1, 0), 0))
    full = lambda a: pl.BlockSpec(a.shape, lambda s: (0,) * a.ndim)
    small = [conv_w, conv_b, wa, ba, wx, bx, lam, token]
    sq = pl.BlockSpec((LRU_WIDTH, LRU_WIDTH), lambda s: (0, 0))
    return pl.pallas_call(
        body, name="lru_bwd", grid=(nt,),
        in_specs=[blk_spec, rows_before, blk_spec, blk_spec, rows_before, blk_spec] + [full(a) for a in small],
        out_specs=[blk_spec, blk_spec, sq, sq, pl.BlockSpec((N_VEC_ROWS, LRU_WIDTH), lambda s: (0, 0))],
        out_shape=[jax.ShapeDtypeStruct((tp, LRU_WIDTH), BF16), jax.ShapeDtypeStruct((tp, LRU_WIDTH), BF16),
                   jax.ShapeDtypeStruct((LRU_WIDTH, LRU_WIDTH), F32), jax.ShapeDtypeStruct((LRU_WIDTH, LRU_WIDTH), F32),
                   jax.ShapeDtypeStruct((N_VEC_ROWS, LRU_WIDTH), F32)],
        scratch_shapes=[pltpu.VMEM((8, LRU_WIDTH), F32)] * 4,
        compiler_params=_params("arbitrary"),
    )(xr, xr, yr, hr, hr, drec, *small)


def _attn_bwd(qkv, dattn, sinks):
    tp = qkv.shape[0]
    tr = _row_tile(tp)
    qb, nt = tr // BLOCK, tp // tr
    n_groups = ATTN_HEADS // GQA_GROUP
    sink_rows, bias = _attn_consts(sinks)

    def body(s_ref, b_ref, q_ref, kp_ref, kc_ref, vp_ref, vc_ref, do_ref, dq_ref, dkv_ref, ex_ref, ds_ref, dsink):
        t = pl.program_id(0)

        @pl.when(t == 0)
        def _():
            dsink[...] = jnp.zeros_like(dsink)

        k_all = jnp.concatenate([kp_ref[...], kc_ref[...]], axis=0)
        v_all = jnp.concatenate([vp_ref[...], vc_ref[...]], axis=0)
        tail = None
        for i in range(qb):
            rows = slice(i * BLOCK, (i + 1) * BLOCK)
            q, do = q_ref[rows], do_ref[rows]
            k2, v2 = k_all[i * BLOCK:(i + 2) * BLOCK], v_all[i * BLOCK:(i + 2) * BLOCK]
            bias_n = _block_bias(b_ref, t, qb, i)
            dqs, dks, dvs = [], [], []
            for g in range(n_groups):
                cols = slice(g * HEAD_DIM, (g + 1) * HEAD_DIM)
                k_g, v_g = k2[:, cols], v2[:, cols]
                qg = _stack_heads(q, g) * jnp.asarray(_QSCALE, BF16)
                dog = _stack_heads(do, g)
                p, ps = _attn_probs_t(k_g, qg, bias_n, s_ref[g:g + 1])
                dpt = _dot_nt(v_g, dog)
                delta = jnp.sum(p * dpt, axis=0, keepdims=True)
                dst = (p * (dpt - delta)).astype(BF16)
                dqs.append(_dot_tn(dst, k_g) * _QSCALE)
                dks.append(_dot(dst, qg))
                dvs.append(_dot(p.astype(BF16), dog))
                dsink[g:g + 1] -= ps * delta
            dq_ref[rows] = _unstack_heads(dqs).astype(BF16)
            dkv = jnp.concatenate(dks + dvs, axis=1)
            if i == 0:
                ex_ref[0] = dkv[:BLOCK]
            else:
                dkv_ref[(i - 1) * BLOCK:i * BLOCK] = (tail + dkv[:BLOCK]).astype(BF16)
            tail = dkv[BLOCK:]
        dkv_ref[(qb - 1) * BLOCK:] = tail.astype(BF16)

        @pl.when(t == nt - 1)
        def _():
            lane = lax.broadcasted_iota(jnp.int32, (1, ATTN_HEADS), 1)
            acc = jnp.zeros((1, ATTN_HEADS), F32)
            for h in range(ATTN_HEADS):
                g, hh = divmod(h, GQA_GROUP)
                acc = acc + jnp.where(lane == h, jnp.sum(dsink[g:g + 1, hh * BLOCK:(hh + 1) * BLOCK]), 0.0)
            ds_ref[...] = acc

    cur = lambda w: pl.BlockSpec((tr, w), lambda t: (t, 0))
    return pl.pallas_call(
        body, name="attn_bwd", grid=(nt,),
        in_specs=[_SINK_SPEC, _BIAS_SPEC, cur(ATTN_WIDTH)] + _kv_specs(tr) + [cur(ATTN_WIDTH)],
        out_specs=[cur(ATTN_WIDTH), cur(2 * KV_WIDTH), pl.BlockSpec((1, BLOCK, 2 * KV_WIDTH), lambda t: (t, 0, 0)),
                   pl.BlockSpec((1, ATTN_HEADS), lambda t: (0, 0))],
        out_shape=[jax.ShapeDtypeStruct((tp, ATTN_WIDTH), BF16), jax.ShapeDtypeStruct((tp, 2 * KV_WIDTH), BF16),
                   jax.ShapeDtypeStruct((nt, BLOCK, 2 * KV_WIDTH), F32), jax.ShapeDtypeStruct((1, ATTN_HEADS), F32)],
        scratch_shapes=[pltpu.VMEM((n_groups, GROUP_ROWS), F32)],
        compiler_params=_params("arbitrary"),
    )(sink_rows, bias, qkv, qkv, qkv, qkv, qkv, dattn)


def _fix_dkv(dkv, dkv_extra):
    tp = dkv.shape[0]
    tr = _row_tile(tp)
    nt, qb = tp // tr, tr // BLOCK
    if nt == 1:
        return dkv

    def body(d_ref, ex_ref, o_ref):
        o_ref[...] = (d_ref[...].astype(F32) + ex_ref[0]).astype(BF16)

    last = pl.BlockSpec((BLOCK, 2 * KV_WIDTH), lambda t: (t * qb + qb - 1, 0))
    return pl.pallas_call(
        body, name="fix_dkv", grid=(nt - 1,),
        in_specs=[last, pl.BlockSpec((1, BLOCK, 2 * KV_WIDTH), lambda t: (t + 1, 0, 0))],
        out_specs=last, out_shape=jax.ShapeDtypeStruct(dkv.shape, dkv.dtype),
        input_output_aliases={0: 0}, compiler_params=_params("parallel"),
    )(dkv, dkv_extra)


def _inproj_wgrad(dq, dkv, dxr, dyr, u0):
    tp = dq.shape[0]
    tr = _wgrad_row_tile(tp)

    def body(dq_ref, dkv_ref, dxr_ref, dyr_ref, u_ref, dw_ref):
        i = pl.program_id(0)
        dz = jnp.concatenate([dq_ref[...], dkv_ref[...], dxr_ref[...], dyr_ref[...]], axis=1)
        pw = _dot_tn(dz, u_ref[...])

        @pl.when(i == 0)
        def _():
            dw_ref[...] = pw

        @pl.when(i > 0)
        def _():
            dw_ref[...] += pw

    row = lambda w: pl.BlockSpec((tr, w), lambda i: (i, 0))
    return pl.pallas_call(
        body, name="inproj_wgrad", grid=(tp // tr,),
        in_specs=[row(ATTN_WIDTH), row(2 * KV_WIDTH), row(LRU_WIDTH), row(LRU_WIDTH), row(D_MODEL)],
        out_specs=pl.BlockSpec((IN_WIDTH, D_MODEL), lambda i: (0, 0)),
        out_shape=jax.ShapeDtypeStruct((IN_WIDTH, D_MODEL), F32),
        compiler_params=_params("arbitrary"),
    )(dq, dkv, dxr, dyr, u0)


def _inproj_dgrad(dq, dkv, dxr, dyr, w_in, head, x, dh1, g, token):
    tp = dq.shape[0]
    tr = _row_tile(tp)
    nt, qb = tp // tr, tr // BLOCK

    def body(*refs):
        dq_ref, dkv_ref, dxr_ref, dyr_ref, w_ref, head_ref = refs[:6]
        pieces = refs[6:6 + qb]
        dh1_ref, g_ref, _, gx_ref, dhead_ref, dg_ref, buf, sems = refs[6 + qb:]
        i = pl.program_id(0)
        slot = i % 2

        def out_copy(step, at):
            return pltpu.make_async_copy(buf.at[at], gx_ref.at[pl.ds(step * tr - BLOCK, tr)], sems.at[at])

        dz = jnp.concatenate([dq_ref[...], dkv_ref[...], dxr_ref[...], dyr_ref[...]], axis=1)
        du = _dot(dz, w_ref[...])
        hhat, rs = _rms(_seq_tile(head_ref[...], pieces, i))
        dx, dg = _rms_bwd(hhat, rs, g_ref[...], du)
        dh0 = dh1_ref[...] + dx

        @pl.when(i >= 3)
        def _():
            out_copy(i - 2, slot).wait()

        buf[slot] = dh0

        @pl.when(i == 0)
        def _():
            dg_ref[...] = dg
            dhead_ref[...] = dh0[:BLOCK]
            if tr > BLOCK:
                first = pltpu.make_async_copy(buf.at[0, pl.ds(BLOCK, tr - BLOCK)], gx_ref.at[pl.ds(0, tr - BLOCK)],
                                              sems.at[0])
                first.start()
                first.wait()

        @pl.when(i >= 1)
        def _():
            dg_ref[...] += dg
            out_copy(i, slot).start()

        @pl.when(i == nt - 1)
        def _():
            if nt >= 3:
                out_copy(nt - 2, (nt - 2) % 2).wait()
            if nt >= 2:
                out_copy(nt - 1, (nt - 1) % 2).wait()

    row = lambda w: pl.BlockSpec((tr, w), lambda i: (i, 0))
    full = lambda shape: pl.BlockSpec(shape, lambda i: (0,) * len(shape))
    return pl.pallas_call(
        body, name="inproj_dgrad", grid=(tp // tr,),
        in_specs=[row(ATTN_WIDTH), row(2 * KV_WIDTH), row(LRU_WIDTH), row(LRU_WIDTH), full(w_in.shape),
                  full(head.shape)] + _seq_specs(tr) + [row(D_MODEL), full(g.shape), full(token.shape)],
        out_specs=[pl.BlockSpec(memory_space=pl.ANY), full((BLOCK, D_MODEL)), full((1, D_MODEL))],
        out_shape=[jax.ShapeDtypeStruct(x.shape, F32), jax.ShapeDtypeStruct((BLOCK, D_MODEL), F32),
                   jax.ShapeDtypeStruct((1, D_MODEL), F32)],
        scratch_shapes=[pltpu.VMEM((2, tr, D_MODEL), F32), pltpu.SemaphoreType.DMA((2,))],
        compiler_params=_params("arbitrary"),
    )(dq, dkv, dxr, dyr, w_in, head, *([x] * qb), dh1, g, token)


def _dense_block_diag(w):
    eye = jnp.eye(LRU_BLOCKS, dtype=w.dtype)
    return (w[:, :, None, :] * eye[:, None, :, None]).reshape(LRU_WIDTH, LRU_WIDTH)


def _diag_blocks(dense):
    d4 = dense.reshape(LRU_BLOCKS, LRU_BLOCK, LRU_BLOCKS, LRU_BLOCK)
    return jnp.stack([d4[n, :, n, :] for n in range(LRU_BLOCKS)])


def _local_step(head, x, tgt, g_pre_mix, w_in, conv_w, conv_b, w_a, b_a, w_x, b_x, lam, sinks, g_post_mix,
                g_pre_ffn, g_post_ffn, late_weights, on_ffn_grads, on_outproj_bwd, on_mixer_grads, token):
    wa = _dense_block_diag(w_a).astype(BF16)
    wx = _dense_block_diag(w_x).astype(BF16)

    u0, qkv, xr, yr = _inproj_fwd(head, x, g_pre_mix, w_in, token)
    attn = _attn_fwd(qkv, sinks)
    hr, rec = _lru_fwd(xr, yr, conv_w, conv_b, wa, b_a, wx, b_x, lam)
    w_out, w1, w2 = late_weights([attn, rec])
    mix, h1, u1 = _outproj_fwd(attn, rec, w_out, head, x, g_post_mix, g_pre_ffn)
    r1, dy, df2, loss, dg_post_ffn = _ffn_fwd(u1, w1, w2, h1, tgt, g_post_ffn)

    da1, dh1, dmix, dg_pre_ffn, dg_post_mix = _ffn_bwd_data(df2, r1, w1, w2, dy, h1, mix, g_pre_ffn, g_post_mix)
    dw1, dw2 = _ffn_bwd_weights(u1, da1, r1, df2)
    token2 = on_ffn_grads(dw1, dw2)
    dattn, drec, dw_out = _outproj_bwd(dmix, w_out, attn, rec, token2)
    token3 = on_outproj_bwd(dattn)
    dxr, dyr, dwa, dwx, vec = _lru_bwd(xr, yr, hr, drec, conv_w, conv_b, wa, b_a, wx, b_x, lam, token3)
    dq, dkv, dkv_extra, dsinks = _attn_bwd(qkv, dattn, sinks)
    dkv = _fix_dkv(dkv, dkv_extra)
    dw_in = _inproj_wgrad(dq, dkv, dxr, dyr, u0)
    token4 = on_mixer_grads(dw_in, dw_out)
    dx, dhead, dg_pre_mix = _inproj_dgrad(dq, dkv, dxr, dyr, w_in, head, x, dh1, g_pre_mix, token4)

    grads = dict(
        g_pre_mix=dg_pre_mix, conv_w=vec[0:4], conv_b=vec[4:5], w_a=_diag_blocks(dwa), b_a=vec[5:6],
        w_x=_diag_blocks(dwx), b_x=vec[6:7], lru_lambda=vec[7:8], attn_sinks=dsinks,
        g_post_mix=dg_post_mix, g_pre_ffn=dg_pre_ffn, g_post_ffn=dg_post_ffn)
    return loss, dx, dhead, grads


HBM = pl.BlockSpec(memory_space=pltpu.HBM)


def _mesh_pos():
    return lax.axis_index("x"), lax.axis_index("y"), lax.axis_index("c")


def _other_chips(x, y):
    return [(1 - x, y), (x, 1 - y), (1 - x, 1 - y)]


def _remote(src, dst, send_sem, recv_sem, to):
    return pltpu.make_async_remote_copy(src_ref=src, dst_ref=dst, send_sem=send_sem, recv_sem=recv_sem,
                                        device_id=to, device_id_type=MESH)


def _gather_weights(shards, lands, tiny, tiny_land):
    nbig = len(shards)

    def body(*refs):
        srcs, tiny_src = refs[:nbig], refs[nbig]
        outs, tiny_out = refs[2 * nbig + 2:3 * nbig + 2], refs[3 * nbig + 2]
        ici_send, ici_recv, d2d_send, d2d_recv, tiny_send, tiny_recv = refs[3 * nbig + 3:]
        x, y, c = _mesh_pos()
        me = 2 * x + y
        chips = _other_chips(x, y)
        sibling = (x, y, 1 - c)
        sends = []
        for w, (src, out) in enumerate(zip(srcs, outs)):
            hr = src.shape[0] // 2
            for j, chip in enumerate(chips):
                k = 3 * w + j
                cp = _remote(src.at[pl.ds(c * hr, hr)], out.at[me, pl.ds(c * hr, hr)],
                             ici_send.at[k], ici_recv.at[k], (*chip, c))
                cp.start()
                sends.append(cp)
        for j, chip in enumerate(chips):
            cp = _remote(tiny_src, tiny_out.at[me], tiny_send.at[j], tiny_recv.at[j], (*chip, c))
            cp.start()
            sends.append(cp)
        for w, (src, out) in enumerate(zip(srcs, outs)):
            hr = src.shape[0] // 2
            for j, (px, py) in enumerate(chips):
                k = 3 * w + j
                landed = out.at[2 * px + py, pl.ds(c * hr, hr)]
                _remote(landed, landed, ici_send.at[k], ici_recv.at[k], sibling).wait_recv()
                cp = _remote(landed, landed, d2d_send.at[k], d2d_recv.at[k], sibling)
                cp.start()
                sends.append(cp)
        for w, (src, out) in enumerate(zip(srcs, outs)):
            hr = src.shape[0] // 2
            for j, (px, py) in enumerate(chips):
                k = 3 * w + j
                other = out.at[2 * px + py, pl.ds((1 - c) * hr, hr)]
                _remote(other, other, d2d_send.at[k], d2d_recv.at[k], sibling).wait_recv()
        for j, (px, py) in enumerate(chips):
            blk = tiny_out.at[2 * px + py]
            _remote(blk, blk, tiny_send.at[j], tiny_recv.at[j], sibling).wait_recv()
        for cp in sends:
            cp.wait_send()

    out_shape = [jax.ShapeDtypeStruct(l.shape, l.dtype) for l in list(lands) + [tiny_land]]
    n = 3 * nbig
    return pl.pallas_call(
        body, name="gather_weights", out_shape=out_shape,
        in_specs=[HBM] * (2 * nbig + 2), out_specs=[HBM] * (nbig + 1),
        input_output_aliases={nbig + 1 + i: i for i in range(nbig + 1)},
        scratch_shapes=[pltpu.SemaphoreType.DMA((n,)),
                        pltpu.SemaphoreType.DMA((n,)), pltpu.SemaphoreType.DMA((n,)), pltpu.SemaphoreType.DMA((n,)),
                        pltpu.SemaphoreType.DMA((3,)), pltpu.SemaphoreType.DMA((3,))],
    )(*shards, tiny, *lands, tiny_land)


def _prep_shard(w, me):
    rows, cols = w.shape
    tr = 256 if rows % 256 == 0 else rows

    def body(me_ref, w_ref, s_ref, l_ref):
        b = w_ref[...].astype(BF16)
        s_ref[...] = b
        l_ref[0] = b

    return pl.pallas_call(
        body, name="prep_shard",
        grid_spec=pltpu.PrefetchScalarGridSpec(
            num_scalar_prefetch=1, grid=(rows // tr,),
            in_specs=[pl.BlockSpec((tr, cols), lambda i, me_ref: (i, 0))],
            out_specs=[pl.BlockSpec((tr, cols), lambda i, me_ref: (i, 0)),
                       pl.BlockSpec((1, tr, cols), lambda i, me_ref: (me_ref[0], i, 0))]),
        out_shape=[jax.ShapeDtypeStruct((rows, cols), BF16), jax.ShapeDtypeStruct((N_CHIPS, rows, cols), BF16)],
        compiler_params=_params("parallel"),
    )(me, w)


def _prep_tiny(tiny, me, slots=N_CHIPS):
    def body(me_ref, t_ref, l_ref):
        l_ref[0] = t_ref[...]

    return pl.pallas_call(
        body, name="prep_tiny",
        grid_spec=pltpu.PrefetchScalarGridSpec(
            num_scalar_prefetch=1, grid=(1,),
            in_specs=[pl.BlockSpec(tiny.shape, lambda i, me_ref: (0, 0))],
            out_specs=pl.BlockSpec((1,) + tiny.shape, lambda i, me_ref: (me_ref[0], 0, 0))),
        out_shape=jax.ShapeDtypeStruct((slots,) + tiny.shape, tiny.dtype),
    )(me, tiny)


N_DEV = 8


def _sibling_exchange(parts, token):
    def body(*refs):
        n = len(parts)
        srcs, outs, send_sems, recv_sems = refs[:n], refs[n + 1:2 * n + 1], refs[2 * n + 1], refs[2 * n + 2]
        x, y, c = _mesh_pos()
        sibling = (x, y, 1 - c)
        cps = []
        for w, (src, out) in enumerate(zip(srcs, outs)):
            hr = src.shape[1] // 2
            cp = _remote(src.at[:, pl.ds((1 - c) * hr, hr)], out, send_sems.at[w], recv_sems.at[w], sibling)
            cp.start()
            cps.append(cp)
        for cp in cps:
            cp.wait()

    n = len(parts)
    return pl.pallas_call(
        body, name="sibling_exchange",
        out_shape=[jax.ShapeDtypeStruct((p.shape[0], p.shape[1] // 2, p.shape[2]), p.dtype) for p in parts],
        in_specs=[HBM] * n + [pl.BlockSpec(memory_space=pl.ANY)], out_specs=[HBM] * n,
        scratch_shapes=[pltpu.SemaphoreType.DMA((n,)), pltpu.SemaphoreType.DMA((n,))],
    )(*parts, token)


def _chip_presum(part, from_sibling, pos):
    _, hr, cols = from_sibling.shape
    tr = 256 if hr % 256 == 0 else hr
    steps = hr // tr

    def body(pos_ref, a_ref, b_ref, o_ref, land_ref):
        s = (a_ref[...] + b_ref[...]).astype(BF16)
        o_ref[...] = s

        @pl.when(pl.program_id(1) == pos_ref[1])
        def _():
            land_ref[...] = s

    return pl.pallas_call(
        body, name="chip_presum",
        grid_spec=pltpu.PrefetchScalarGridSpec(
            num_scalar_prefetch=1, grid=(steps, N_CHIPS),
            in_specs=[pl.BlockSpec((1, tr, cols), lambda i, j, p: (j, p[0] * steps + i, 0)),
                      pl.BlockSpec((1, tr, cols), lambda i, j, p: (j, i, 0))],
            out_specs=[pl.BlockSpec((1, tr, cols), lambda i, j, p: (j, i, 0)),
                       pl.BlockSpec((1, tr, cols), lambda i, j, p: (p[1], p[0] * steps + i, 0))]),
        out_shape=[jax.ShapeDtypeStruct(from_sibling.shape, BF16),
                   jax.ShapeDtypeStruct((N_CHIPS, 2 * hr, cols), BF16)],
        compiler_params=_params("arbitrary", "arbitrary"),
    )(pos, part, from_sibling)


def _scatter_partials(cparts, lands, done_cparts=(), done_lands=()):
    n_new = len(cparts)
    nw = n_new + len(done_cparts)

    def body(*refs):
        srcs = refs[:nw]
        outs = refs[2 * nw:3 * nw]
        own_send, own_recv, ici_send, ici_recv, d2d_send, d2d_recv = refs[3 * nw:]
        x, y, c = _mesh_pos()
        me = 2 * x + y
        chips = _other_chips(x, y)
        sibling = (x, y, 1 - c)
        sends = []
        for w in list(range(n_new, nw)) + list(range(n_new)):
            src, out = srcs[w], outs[w]
            hr = src.shape[1]
            mine = out.at[me, pl.ds(c * hr, hr)]
            cp = _remote(src.at[me], mine, own_send.at[w], own_recv.at[w], sibling)
            cp.start()
            sends.append(cp)
            for j, (px, py) in enumerate(chips):
                if w >= n_new:
                    break
                k = 3 * w + j
                cp = _remote(src.at[2 * px + py], mine, ici_send.at[k], ici_recv.at[k], (px, py, c))
                cp.start()
                sends.append(cp)
        for w in list(range(n_new, nw)) + list(range(n_new)):
            src, out = srcs[w], outs[w]
            hr = src.shape[1]
            for j, (px, py) in enumerate(chips):
                k = 3 * w + j
                landed = out.at[2 * px + py, pl.ds(c * hr, hr)]
                if w < n_new:
                    _remote(landed, landed, ici_send.at[k], ici_recv.at[k], sibling).wait_recv()
                cp = _remote(landed, landed, d2d_send.at[k], d2d_recv.at[k], sibling)
                cp.start()
                sends.append(cp)
        for w, (src, out) in enumerate(zip(srcs, outs)):
            hr = src.shape[1]
            other = out.at[me, pl.ds((1 - c) * hr, hr)]
            _remote(other, other, own_send.at[w], own_recv.at[w], sibling).wait_recv()
            for j, (px, py) in enumerate(chips):
                k = 3 * w + j
                other = out.at[2 * px + py, pl.ds((1 - c) * hr, hr)]
                _remote(other, other, d2d_send.at[k], d2d_recv.at[k], sibling).wait_recv()
        for cp in sends:
            cp.wait_send()

    n = 3 * nw
    dma = pltpu.SemaphoreType.DMA
    every = list(cparts) + list(done_cparts)
    every_lands = list(lands) + list(done_lands)
    return pl.pallas_call(
        body, name="scatter_partials",
        out_shape=[jax.ShapeDtypeStruct(l.shape, l.dtype) for l in every_lands],
        in_specs=[HBM] * (2 * nw), out_specs=[HBM] * nw,
        input_output_aliases={nw + i: i for i in range(nw)},
        scratch_shapes=[dma((nw,)), dma((nw,)), dma((n,)), dma((n,)), dma((n,)), dma((n,))],
    )(*every, *every_lands)


SEM = pl.BlockSpec(memory_space=pltpu.SEMAPHORE)
SPLIT_COPY = pltpu.CompilerParams(has_side_effects=pltpu.SideEffectType.DATAFLOW_SIDE_EFFECTING)


def _hbm(a):
    return pltpu.with_memory_space_constraint(a, pltpu.HBM)


def _gather_copies(srcs, lands, send_sems, recv_sems):
    x, y, c = _mesh_pos()
    me = 2 * x + y
    sends, recvs = [], []
    for w, (src, land) in enumerate(zip(srcs, lands)):
        hr = src.shape[0] // 2
        for j, (px, py) in enumerate(_other_chips(x, y)):
            k = 3 * w + j
            sends.append(_remote(src.at[pl.ds(c * hr, hr)], land.at[me, pl.ds(c * hr, hr)],
                                 send_sems.at[k], recv_sems.at[k], (px, py, c)))
            got = land.at[2 * px + py, pl.ds(c * hr, hr)]
            recvs.append(_remote(got, got, send_sems.at[k], recv_sems.at[k], (px, py, c)))
    return sends, recvs


def _scatter_copies(srcs, lands, send_sems, recv_sems):
    x, y, c = _mesh_pos()
    me = 2 * x + y
    sends, recvs = [], []
    for w, (src, land) in enumerate(zip(srcs, lands)):
        hr = src.shape[1]
        for j, (px, py) in enumerate(_other_chips(x, y)):
            k = 3 * w + j
            sends.append(_remote(src.at[2 * px + py], land.at[me, pl.ds(c * hr, hr)],
                                 send_sems.at[k], recv_sems.at[k], (px, py, c)))
            got = land.at[2 * px + py, pl.ds(c * hr, hr)]
            recvs.append(_remote(got, got, send_sems.at[k], recv_sems.at[k], (px, py, c)))
    return sends, recvs


def _sibling_copies(srcs, lands, send_sems, recv_sems):
    x, y, c = _mesh_pos()
    sibling = (x, y, 1 - c)
    sends, recvs = [], []
    for w, (src, land) in enumerate(zip(srcs, lands)):
        hr = src.shape[1] // 2
        sends.append(_remote(src.at[:, pl.ds((1 - c) * hr, hr)], land, send_sems.at[w], recv_sems.at[w], sibling))
        recvs.append(_remote(land, land, send_sems.at[w], recv_sems.at[w], sibling))
    return sends, recvs


def _all_peers_copies(srcs, lands, send_sems, recv_sems):
    x, y, c = _mesh_pos()
    (src,), (land,) = srcs, lands
    flip = lambda v, bit: 1 - v if bit else v
    sends, recvs = [], []
    for k in range(N_DEV - 1):
        px, py, pc = flip(x, (k + 1) & 4), flip(y, (k + 1) & 2), flip(c, (k + 1) & 1)
        sends.append(_remote(src, land.at[4 * x + 2 * y + c], send_sems.at[k], recv_sems.at[k], (px, py, pc)))
        got = land.at[4 * px + 2 * py + pc]
        recvs.append(_remote(got, got, send_sems.at[k], recv_sems.at[k], (px, py, pc)))
    return sends, recvs


def _split_start(name, copies_of, srcs, land_shapes, n_copies=None):
    n = len(srcs)
    k = 3 * n if n_copies is None else n_copies

    def body(*refs):
        src_refs, land_refs = refs[:n], refs[n:2 * n]
        send_sems, recv_sems = refs[2 * n], refs[2 * n + 1]
        token = refs[-1]
        sends, _ = copies_of(src_refs, land_refs, send_sems, recv_sems)
        for cp in sends:
            cp.start()
        token[...] = jnp.zeros_like(token)

    lands = [_hbm(s) for s in land_shapes]
    dma = pltpu.SemaphoreType.DMA
    res = pl.pallas_call(
        body, name=name,
        out_shape=(dma((k,)), dma((k,)), *[pltpu.HBM(s.shape, s.dtype) for s in srcs],
                   *[pltpu.HBM(s.shape, s.dtype) for s in land_shapes], jax.ShapeDtypeStruct((8, 128), F32)),
        in_specs=[HBM] * (2 * n),
        out_specs=(SEM, SEM, *([HBM] * (2 * n)), pl.BlockSpec(memory_space=pltpu.VMEM)),
        input_output_aliases={i: 2 + i for i in range(2 * n)},
        compiler_params=SPLIT_COPY,
    )(*[_hbm(s) for s in srcs], *lands)
    return res[0], res[1], list(res[2:2 + n]), list(res[2 + n:2 + 2 * n]), res[-1]


def _split_wait(name, copies_of, send_sems, recv_sems, srcs, lands, after):
    n = len(srcs)

    def body(*refs):
        src_refs, land_refs = refs[:n], refs[n:2 * n]
        sends, recvs = copies_of(src_refs, land_refs, refs[2 * n], refs[2 * n + 1])
        for cp in sends:
            cp.wait_send()
        for cp in recvs:
            cp.wait_recv()

    res = pl.pallas_call(
        body, name=name,
        out_shape=tuple(pltpu.HBM(s.shape, s.dtype) for s in list(srcs) + list(lands)),
        in_specs=[HBM] * (2 * n) + [SEM, SEM] + [pl.BlockSpec(memory_space=pl.ANY)] * len(after),
        out_specs=tuple([HBM] * (2 * n)),
        input_output_aliases={i: i for i in range(2 * n)},
        compiler_params=SPLIT_COPY,
    )(*srcs, *lands, send_sems, recv_sems, *after)
    return list(res[:n]), list(res[n:])


def _gather_finish(lands):
    n = len(lands)

    def body(*refs):
        outs = refs[n:2 * n]
        d2d_send, d2d_recv = refs[2 * n:]
        x, y, c = _mesh_pos()
        chips = _other_chips(x, y)
        sibling = (x, y, 1 - c)
        sends = []
        for w, out in enumerate(outs):
            hr = out.shape[1] // 2
            for j, (px, py) in enumerate(chips):
                landed = out.at[2 * px + py, pl.ds(c * hr, hr)]
                cp = _remote(landed, landed, d2d_send.at[3 * w + j], d2d_recv.at[3 * w + j], sibling)
                cp.start()
                sends.append(cp)
        for w, out in enumerate(outs):
            hr = out.shape[1] // 2
            for j, (px, py) in enumerate(chips):
                other = out.at[2 * px + py, pl.ds((1 - c) * hr, hr)]
                _remote(other, other, d2d_send.at[3 * w + j], d2d_recv.at[3 * w + j], sibling).wait_recv()
        for cp in sends:
            cp.wait_send()

    dma = pltpu.SemaphoreType.DMA
    return pl.pallas_call(
        body, name="gather_finish",
        out_shape=[jax.ShapeDtypeStruct(l.shape, l.dtype) for l in lands],
        in_specs=[HBM] * n, out_specs=[HBM] * n,
        input_output_aliases={i: i for i in range(n)},
        scratch_shapes=[dma((3 * n,)), dma((3 * n,))],
    )(*lands)


def _adamw(w, g, m, v):
    m = ADAM_B1 * m + (1.0 - ADAM_B1) * g
    v = ADAM_B2 * v + (1.0 - ADAM_B2) * (g * g)
    m_hat = m / (1.0 - ADAM_B1 ** ADAM_STEP)
    v_hat = v / (1.0 - ADAM_B2 ** ADAM_STEP)
    delta = -ADAM_LR * (m_hat / (jnp.sqrt(v_hat) + ADAM_EPS) + ADAM_WD * w)
    return delta, m, v


def _adamw_big(partials, w, m, v):
    rows, cols = w.shape
    tr = 256 if rows % 256 == 0 else rows

    def body(p_ref, w_ref, m_ref, v_ref, g_ref, d_ref, m2_ref, v2_ref):
        g = ((p_ref[0].astype(F32) + p_ref[1].astype(F32)) + p_ref[2].astype(F32)) + p_ref[3].astype(F32)
        g_ref[...] = g
        d_ref[...], m2_ref[...], v2_ref[...] = _adamw(w_ref[...], g, m_ref[...], v_ref[...])

    blk = pl.BlockSpec((tr, cols), lambda i: (i, 0))
    return pl.pallas_call(
        body, name="adamw_big", grid=(rows // tr,),
        in_specs=[pl.BlockSpec((N_CHIPS, tr, cols), lambda i: (0, i, 0)), blk, blk, blk],
        out_specs=[blk] * 4, out_shape=[jax.ShapeDtypeStruct((rows, cols), F32)] * 4,
        compiler_params=_params("parallel"),
    )(partials, w, m, v)


def _sum_devices(gathered, rows):
    cols = gathered.shape[1]

    def body(g_ref, o_ref):
        acc = g_ref[0:rows]
        for d in range(1, N_DEV):
            acc = acc + g_ref[d * rows:(d + 1) * rows]
        o_ref[...] = acc

    return pl.pallas_call(
        body, name="sum_devices", out_shape=jax.ShapeDtypeStruct((rows, cols), F32),
        in_specs=[pl.BlockSpec(memory_space=pltpu.VMEM)], out_specs=pl.BlockSpec(memory_space=pltpu.VMEM),
        compiler_params=pltpu.CompilerParams(vmem_limit_bytes=VMEM_LIMIT_V7X),
    )(gathered)


def _adamw_small(quads):
    n = len(quads)

    def body(*refs):
        ins, outs = refs[:4 * n], refs[4 * n:]
        for t in range(n):
            w, g, m, v = (r[...] for r in ins[4 * t:4 * t + 4])
            outs[3 * t][...], outs[3 * t + 1][...], outs[3 * t + 2][...] = _adamw(w, g, m, v)

    flat = [a for q in quads for a in q]
    vm = pl.BlockSpec(memory_space=pltpu.VMEM)
    res = pl.pallas_call(
        body, name="adamw_small",
        out_shape=[jax.ShapeDtypeStruct(q[0].shape, F32) for q in quads for _ in range(3)],
        in_specs=[vm] * (4 * n), out_specs=[vm] * (3 * n),
    )(*flat)
    return [tuple(res[3 * t:3 * t + 3]) for t in range(n)]


SMALL_PACK_ROWS = 96
_WEIGHTS = ['meta_tokens', 'g_pre_mix', 'w_in', 'conv_w', 'conv_b', 'w_a', 'b_a', 'w_x', 'b_x', 'lru_lambda',
            'attn_sinks', 'w_out', 'g_post_mix', 'g_pre_ffn', 'w_ff1', 'w_ff2', 'g_post_ffn']
_BIG = ['w_in', 'w_out', 'w_ff1', 'w_ff2']


def _pack_small(dmeta, g, loss):
    z = lambda r, c: jnp.zeros((r, c), F32)
    rows = [
        dmeta,
        g['g_pre_mix'], g['g_post_mix'], g['g_pre_ffn'], g['g_post_ffn'],
        jnp.concatenate([g['conv_w'], z(4, 512)], axis=1),
        jnp.concatenate([g['conv_b'], g['b_a']], axis=1),
        jnp.concatenate([g['b_x'], g['lru_lambda']], axis=1),
        jnp.concatenate([g['attn_sinks'], z(1, D_MODEL - ATTN_HEADS)], axis=1),
        jnp.concatenate([loss, z(1, D_MODEL - 1)], axis=1),
        z(4, D_MODEL),
        g['w_a'].reshape(32, D_MODEL), g['w_x'].reshape(32, D_MODEL),
    ]
    return jnp.concatenate(rows, axis=0)


def _unpack_small(s, chip):
    return dict(
        meta_tokens=lax.dynamic_slice(s[0:16], (0, chip * 256), (16, 256)),
        g_pre_mix=s[16:17], g_post_mix=s[17:18], g_pre_ffn=s[18:19], g_post_ffn=s[19:20],
        conv_w=lax.dynamic_slice(s[20:24], (0, chip * 128), (4, 128)).reshape(1, 4, 128),
        conv_b=s[24:25, :512], b_a=s[24:25, 512:], b_x=s[25:26, :512], lru_lambda=s[25:26, 512:],
        attn_sinks=s[26:27, :ATTN_HEADS], loss=s[27, 0],
        w_a=s[32:64].reshape(1, LRU_BLOCKS, LRU_BLOCK, LRU_BLOCK),
        w_x=s[64:96].reshape(1, LRU_BLOCKS, LRU_BLOCK, LRU_BLOCK))


def _as2d(a):
    if a.ndim == 2:
        return a
    return a.reshape(-1, a.shape[-1])


def kernel(x, meta_tokens, g_pre_mix, w_in, conv_w, conv_b, w_a, b_a, w_x, b_x, lru_lambda, attn_sinks, w_out, g_post_mix, g_pre_ffn, w_ff1, w_ff2, g_post_ffn, loss_target, m_meta_tokens, m_g_pre_mix, m_w_in, m_conv_w, m_conv_b, m_w_a, m_b_a, m_w_x, m_b_x, m_lru_lambda, m_attn_sinks, m_w_out, m_g_post_mix, m_g_pre_ffn, m_w_ff1, m_w_ff2, m_g_post_ffn, v_meta_tokens, v_g_pre_mix, v_w_in, v_conv_w, v_conv_b, v_w_a, v_b_a, v_w_x, v_b_x, v_lru_lambda, v_attn_sinks, v_w_out, v_g_post_mix, v_g_pre_ffn, v_w_ff1, v_w_ff2, v_g_post_ffn):
    weights = dict(meta_tokens=meta_tokens, g_pre_mix=g_pre_mix, w_in=w_in, conv_w=conv_w, conv_b=conv_b, w_a=w_a,
                   b_a=b_a, w_x=w_x, b_x=b_x, lru_lambda=lru_lambda, attn_sinks=attn_sinks, w_out=w_out,
                   g_post_mix=g_post_mix, g_pre_ffn=g_pre_ffn, w_ff1=w_ff1, w_ff2=w_ff2, g_post_ffn=g_post_ffn)
    mom1 = dict(zip(_WEIGHTS, [m_meta_tokens, m_g_pre_mix, m_w_in, m_conv_w, m_conv_b, m_w_a, m_b_a, m_w_x, m_b_x,
                               m_lru_lambda, m_attn_sinks, m_w_out, m_g_post_mix, m_g_pre_ffn, m_w_ff1, m_w_ff2,
                               m_g_post_ffn]))
    mom2 = dict(zip(_WEIGHTS, [v_meta_tokens, v_g_pre_mix, v_w_in, v_conv_w, v_conv_b, v_w_a, v_b_a, v_w_x, v_b_x,
                               v_lru_lambda, v_attn_sinks, v_w_out, v_g_post_mix, v_g_pre_ffn, v_w_ff1, v_w_ff2,
                               v_g_post_ffn]))
    xi, yi, ci = _mesh_pos()
    chip = 2 * xi + yi

    tiny = jnp.concatenate([meta_tokens, jnp.pad(conv_w[0], ((0, 4), (0, 128)))], axis=0)
    chip_arr = jnp.reshape(chip, (1,)).astype(jnp.int32)
    big2d = lambda a, name: a[0].T if name == 'w_in' else a[0]
    shards, lands = zip(*[_prep_shard(big2d(weights[n], n), chip_arr) for n in _BIG])
    g_in, g_tiny = _gather_weights(shards[:1], lands[:1], tiny, _prep_tiny(tiny, chip_arr))
    w_in_full = g_in.reshape(IN_WIDTH, D_MODEL)
    meta_full = jnp.concatenate([g_tiny[j, :N_META] for j in range(N_CHIPS)], axis=1)
    conv_w_full = jnp.concatenate([g_tiny[j, N_META:N_META + 4, :128] for j in range(N_CHIPS)], axis=1)
    g_send, g_recv, late_thru, late_lands, token = _split_start(
        "gather_late_start", _gather_copies, shards[1:], lands[1:])

    def late_weights(after):
        _, landed = _split_wait("gather_late_wait", _gather_copies, g_send, g_recv, late_thru, late_lands, after)
        g_out, g_f1, g_f2 = _gather_finish(landed)
        return g_out.reshape(D_MODEL, D_MODEL), g_f1, g_f2

    pos = jnp.stack([ci, chip]).astype(jnp.int32)
    ffn = {}


    def on_ffn_grads(dw1, dw2):
        parts = [dw1, dw2]
        lands = [lax.empty((p.shape[0], p.shape[1] // 2, p.shape[2]), p.dtype) for p in parts]
        ffn['sib'] = _split_start("sibling_ffn_start", _sibling_copies, parts, lands, len(parts))
        return ffn['sib'][4]

    def on_outproj_bwd(dattn):
        send, recv, thru, lands, _ = ffn['sib']
        parts, from_sibling = _split_wait("sibling_ffn_wait", _sibling_copies, send, recv, thru, lands, [dattn])
        cparts_ffn, lands_ffn = zip(*[_chip_presum(p, r, pos) for p, r in zip(parts, from_sibling)])
        ffn['send'], ffn['recv'], ffn['thru'], ffn['lands'], token3 = _split_start(
            "scatter_ffn_start", _scatter_copies, cparts_ffn, lands_ffn)
        return token3

    def on_mixer_grads(dw_in, dw_out):
        parts = [dw_in.reshape(N_CHIPS, IN_WIDTH // N_CHIPS, D_MODEL),
                 dw_out.reshape(N_CHIPS, D_MODEL // N_CHIPS, D_MODEL)]
        cparts, lands = zip(*[_chip_presum(p, r, pos) for p, r in zip(parts, _sibling_exchange(parts, pos))])
        ffn['mixer'] = _split_start("scatter_mixer_start", _scatter_copies, cparts, lands)
        return ffn['mixer'][4]

    head = jnp.concatenate([jnp.zeros((PAD_ROWS, D_MODEL), F32), meta_full], axis=0)
    loss, dx, dhead, grads = _local_step(head, x[0], loss_target[0], g_pre_mix, w_in_full, conv_w_full, conv_b, w_a[0],
                                         b_a, w_x[0], b_x, lru_lambda, attn_sinks, g_post_mix, g_pre_ffn, g_post_ffn,
                                         late_weights, on_ffn_grads, on_outproj_bwd, on_mixer_grads, token)
    grad_x = dx[None]

    pack = _pack_small(dhead[PAD_ROWS:], grads, loss)
    dev = jnp.reshape(4 * xi + 2 * yi + ci, (1,)).astype(jnp.int32)
    s_send, s_recv, s_thru, s_lands, token5 = _split_start(
        "gather_small_start", _all_peers_copies, [pack], [_prep_tiny(pack, dev, N_DEV)], N_DEV - 1)

    send, recv, thru, lands, _ = ffn['mixer']
    mixer_cparts, mixer_lands = _split_wait("scatter_mixer_wait", _scatter_copies, send, recv, thru, lands, [token5])
    ffn_cparts, ffn_lands = _split_wait("scatter_ffn_wait", _scatter_copies, ffn['send'], ffn['recv'], ffn['thru'],
                                        ffn['lands'], mixer_lands)
    chip_partials = _scatter_partials([], [], mixer_cparts + ffn_cparts, mixer_lands + ffn_lands)

    g_out_d, delta, new_m, new_v = {}, {}, {}, {}
    for name, part in zip(_BIG, chip_partials):
        shp = weights[name].shape
        res = _adamw_big(part, big2d(weights[name], name), big2d(mom1[name], name), big2d(mom2[name], name))
        g_out_d[name], delta[name], new_m[name], new_v[name] = (big2d(r[None], name).reshape(shp) for r in res)

    _, (gathered,) = _split_wait("gather_small_wait", _all_peers_copies, s_send, s_recv, s_thru, s_lands,
                                 [g_out_d[n] for n in _BIG])
    small = _unpack_small(_sum_devices(gathered.reshape(N_DEV * SMALL_PACK_ROWS, D_MODEL), SMALL_PACK_ROWS), chip)
    loss = small['loss']
    small_names = [n for n in _WEIGHTS if n not in _BIG]
    quads = [(_as2d(weights[n]), _as2d(small[n]), _as2d(mom1[n]), _as2d(mom2[n])) for n in small_names]
    for name, (d, m2, v2) in zip(small_names, _adamw_small(quads)):
        shp = weights[name].shape
        g_out_d[name] = small[name].reshape(shp)
        delta[name], new_m[name], new_v[name] = d.reshape(shp), m2.reshape(shp), v2.reshape(shp)

    return (loss, grad_x, *[g_out_d[n] for n in _WEIGHTS], *[delta[n] for n in _WEIGHTS],
            *[new_m[n] for n in _WEIGHTS], *[new_v[n] for n in _WEIGHTS])
```

```python
import numpy as np
import jax
import jax.numpy as jnp
from jax import lax
from jax.experimental import pallas as pl
from jax.experimental.pallas import tpu as pltpu

F32 = jnp.float32
BF16 = jnp.bfloat16

D_MODEL = 1024
N_META = 16
BLOCK = 128
PAD_ROWS = BLOCK - N_META
HEAD_DIM = 64
ATTN_HEADS = 8
GQA_GROUP = 4
ATTN_WIDTH = 512
KV_WIDTH = 128
QKV_WIDTH = ATTN_WIDTH + 2 * KV_WIDTH
LRU_WIDTH = 512
LRU_BLOCKS = 8
LRU_BLOCK = 64
LRU_C = 8.0
IN_WIDTH = 1792
D_FF = 4096
N_CHIPS = 4
FF_CHUNK = D_FF // N_CHIPS
EPS = 1e-6
NEG = -1e30

ADAM_LR = 0.001
ADAM_B1 = 0.9
ADAM_B2 = 0.999
ADAM_EPS = 1e-08
ADAM_WD = 0.01
ADAM_STEP = 10

VMEM_LIMIT_V7X = 62 * 1024 * 1024
MESH = pl.DeviceIdType.MESH

NT = (((1,), (1,)), ((), ()))
TN = (((0,), (0,)), ((), ()))


def _row_tile(tp):
    return 640 if tp % 640 == 0 else BLOCK


def _wgrad_row_tile(tp):
    return 1664 if tp % 1664 == 0 else _row_tile(tp)


def _params(*sem):
    return pltpu.CompilerParams(dimension_semantics=sem, vmem_limit_bytes=VMEM_LIMIT_V7X)


def _dot(a, b):
    return jnp.dot(a, b, preferred_element_type=F32)


def _dot_nt(a, b):
    return lax.dot_general(a, b, NT, preferred_element_type=F32)


def _dot_tn(a, b):
    return lax.dot_general(a, b, TN, preferred_element_type=F32)


def _rms(x):
    rs = lax.rsqrt(jnp.mean(x * x, axis=-1, keepdims=True) + EPS)
    return x * rs, rs


def _rms_bwd(xhat, rs, g, dy):
    dyg = dy * g
    dx = rs * (dyg - xhat * jnp.mean(dyg * xhat, axis=-1, keepdims=True))
    dg = jnp.sum(dy * xhat, axis=0, keepdims=True)
    return dx, dg


def _gelu(x):
    k = 0.7978845608028654
    t = jnp.tanh(x * (k + (k * 0.044715) * (x * x)))
    return (0.5 * x) * (1.0 + t), t


def _gelu_grad(x, t):
    k = 0.7978845608028654
    return 0.5 * (1.0 + t) + 0.5 * x * (1.0 - t * t) * k * (1.0 + 3 * 0.044715 * x * x)


def _sigmoid(x):
    return 0.5 * jnp.tanh(0.5 * x) + 0.5


def _one_minus_exp2(y):
    t = jnp.tanh(y)
    return (-2.0 * t) / (1.0 - t)


def _softplus(x):
    return jnp.maximum(x, 0.0) + jnp.log1p(jnp.exp(-jnp.abs(x)))


def _seq_specs(tr, delay=0):
    qb = tr // BLOCK
    tile = lambda i: jnp.maximum(i - delay, 0)
    return [pl.BlockSpec((BLOCK, D_MODEL), lambda i, *_, s=s: (jnp.maximum(tile(i) * qb + s - 1, 0), 0))
            for s in range(qb)]


def _seq_tile(head, pieces, i):
    first = jnp.where(i == 0, head, pieces[0][...])
    return jnp.concatenate([first] + [p[...] for p in pieces[1:]], axis=0)


def _inproj_fwd(head, x, g, w_in, token):
    tp = BLOCK + x.shape[0]
    tr = _row_tile(tp)
    qb = tr // BLOCK

    def body(*refs):
        head_ref, pieces = refs[0], refs[1:1 + qb]
        g_ref, w_ref, _, u_ref, qkv_ref, xr_ref, yr_ref = refs[1 + qb:]
        xhat, _ = _rms(_seq_tile(head_ref[...], pieces, pl.program_id(0)))
        u = (xhat * g_ref[...]).astype(BF16)
        u_ref[...] = u
        z = _dot_nt(u, w_ref[...])
        qkv_ref[...] = z[:, :QKV_WIDTH].astype(BF16)
        xr_ref[...] = z[:, QKV_WIDTH:QKV_WIDTH + LRU_WIDTH]
        yr_ref[...] = z[:, QKV_WIDTH + LRU_WIDTH:]

    row = lambda w: pl.BlockSpec((tr, w), lambda i: (i, 0))
    full = lambda a: pl.BlockSpec(a.shape, lambda i: (0,) * a.ndim)
    return pl.pallas_call(
        body, name="inproj_fwd", grid=(tp // tr,),
        in_specs=[full(head)] + _seq_specs(tr) + [full(g), full(w_in), full(token)],
        out_specs=[row(D_MODEL), row(QKV_WIDTH), row(LRU_WIDTH), row(LRU_WIDTH)],
        out_shape=[jax.ShapeDtypeStruct((tp, D_MODEL), BF16), jax.ShapeDtypeStruct((tp, QKV_WIDTH), BF16),
                   jax.ShapeDtypeStruct((tp, LRU_WIDTH), F32), jax.ShapeDtypeStruct((tp, LRU_WIDTH), F32)],
        compiler_params=_params("parallel"),
    )(head, *([x] * qb), g, w_in, token)


GROUP_ROWS = GQA_GROUP * BLOCK


def _attn_bias():
    j = np.arange(2 * BLOCK)[:, None]
    i = np.arange(BLOCK)[None, :]
    band = (j - i >= 1) & (j - i <= BLOCK)
    out = []
    for n in range(3):
        ok = band & ((n - 1) * BLOCK + j >= PAD_ROWS) if n < 2 else band
        out.append(np.tile(np.where(ok, 0.0, NEG).astype(np.float32), (1, GQA_GROUP)))
    return jnp.asarray(np.stack(out))


def _stack_heads(a, g):
    heads = range(GQA_GROUP * g, GQA_GROUP * (g + 1))
    return jnp.concatenate([a[:, h * HEAD_DIM:(h + 1) * HEAD_DIM] for h in heads], axis=0)


def _unstack_heads(groups):
    return jnp.concatenate([p[h * BLOCK:(h + 1) * BLOCK] for p in groups for h in range(GQA_GROUP)], axis=1)


def _attn_probs_t(k_g, qg, bias, sink_row):
    st = _dot_nt(k_g, qg) + bias
    m = jnp.maximum(jnp.max(st, axis=0, keepdims=True), sink_row)
    p = jnp.exp(st - m)
    es = jnp.exp(sink_row - m)
    inv = 1.0 / (jnp.sum(p, axis=0, keepdims=True) + es)
    return p * inv, es * inv


def _attn_consts(sinks):
    return jnp.repeat(sinks.reshape(ATTN_HEADS), BLOCK).reshape(ATTN_HEADS // GQA_GROUP, GROUP_ROWS), _attn_bias()


_SINK_SPEC = pl.BlockSpec((ATTN_HEADS // GQA_GROUP, GROUP_ROWS), lambda n: (0, 0))
_BIAS_SPEC = pl.BlockSpec((3, 2 * BLOCK, GROUP_ROWS), lambda n: (0, 0, 0))
_QSCALE = HEAD_DIM ** -0.5


def _kv_specs(tr):
    qb = tr // BLOCK
    prev = lambda col: pl.BlockSpec((BLOCK, KV_WIDTH), lambda t: (jnp.maximum(t * qb - 1, 0), col))
    cur = lambda col: pl.BlockSpec((tr, KV_WIDTH), lambda t: (t, col))
    return [prev(4), cur(4), prev(5), cur(5)]


def _block_bias(b_ref, t, qb, i):
    return b_ref[2] if i >= 2 else b_ref[jnp.minimum(t * qb + i, 2)]


def _attn_fwd(qkv, sinks):
    tp = qkv.shape[0]
    tr = _row_tile(tp)
    qb = tr // BLOCK
    sink_rows, bias = _attn_consts(sinks)

    def body(s_ref, b_ref, q_ref, kp_ref, kc_ref, vp_ref, vc_ref, o_ref):
        t = pl.program_id(0)
        k_all = jnp.concatenate([kp_ref[...], kc_ref[...]], axis=0)
        v_all = jnp.concatenate([vp_ref[...], vc_ref[...]], axis=0)
        for i in range(qb):
            rows = slice(i * BLOCK, (i + 1) * BLOCK)
            q = q_ref[rows]
            k2, v2 = k_all[i * BLOCK:(i + 2) * BLOCK], v_all[i * BLOCK:(i + 2) * BLOCK]
            bias_n = _block_bias(b_ref, t, qb, i)
            outs = []
            for g in range(ATTN_HEADS // GQA_GROUP):
                cols = slice(g * HEAD_DIM, (g + 1) * HEAD_DIM)
                qg = _stack_heads(q, g) * jnp.asarray(_QSCALE, BF16)
                p, _ = _attn_probs_t(k2[:, cols], qg, bias_n, s_ref[g:g + 1])
                outs.append(_dot_tn(p.astype(BF16), v2[:, cols]))
            o_ref[rows] = _unstack_heads(outs).astype(BF16)

    return pl.pallas_call(
        body, name="attn_fwd", grid=(tp // tr,),
        in_specs=[_SINK_SPEC, _BIAS_SPEC, pl.BlockSpec((tr, ATTN_WIDTH), lambda t: (t, 0))] + _kv_specs(tr),
        out_specs=pl.BlockSpec((tr, ATTN_WIDTH), lambda t: (t, 0)),
        out_shape=jax.ShapeDtypeStruct((tp, ATTN_WIDTH), BF16),
        compiler_params=_params("parallel"),
    )(sink_rows, bias, qkv, qkv, qkv, qkv, qkv)


def _conv_taps(x, halo):
    ext = jnp.concatenate([halo, x], axis=0)
    return [ext[8:] if k == 3 else pltpu.roll(ext, 3 - k, 0)[8:] for k in range(4)]


def _lru_gates(xc, wa, ba, wx, bx, sp):
    xb = xc.astype(BF16)
    r = _sigmoid(_dot(xb, wa) + ba)
    ig = _sigmoid(_dot(xb, wx) + bx)
    log_a = (-LRU_C * sp) * r
    a = jnp.exp(log_a)
    mult = jnp.sqrt(_one_minus_exp2(log_a))
    return xb, r, ig, a, mult


SUBLANES = 8


def _scan_fwd(a, b, h_in):
    n, width = a.shape
    a, b = (v.reshape(n // SUBLANES, SUBLANES, width) for v in (a, b))
    in_group = lax.broadcasted_iota(jnp.int32, a.shape, 1)
    for d in (1, 2, 4):
        keep = in_group >= d
        b = jnp.where(keep, a * pltpu.roll(b, d, 1) + b, b)
        a = jnp.where(keep, a * pltpu.roll(a, d, 1), a)
    a, b = a.reshape(n, width), b.reshape(n, width)
    out, carry = [], h_in
    for g in range(0, n, SUBLANES):
        h = a[g:g + SUBLANES] * carry + b[g:g + SUBLANES]
        out.append(h)
        carry = h[SUBLANES - 1:]
    return jnp.concatenate(out, axis=0)


def _scan_rev(c, b, g_in):
    n, width = c.shape
    c, b = (v.reshape(n // SUBLANES, SUBLANES, width) for v in (c, b))
    in_group = lax.broadcasted_iota(jnp.int32, c.shape, 1)
    for d in (1, 2, 4):
        keep = in_group < SUBLANES - d
        b = jnp.where(keep, b + c * pltpu.roll(b, SUBLANES - d, 1), b)
        c = jnp.where(keep, c * pltpu.roll(c, SUBLANES - d, 1), c)
    c, b = c.reshape(n, width), b.reshape(n, width)
    out, carry = [], g_in
    for g in range(n - SUBLANES, -1, -SUBLANES):
        r = b[g:g + SUBLANES] + c[g:g + SUBLANES] * carry
        out.append(r)
        carry = r[:1]
    return jnp.concatenate(out[::-1], axis=0)


def _lru_fwd(xr, yr, conv_w, conv_b, wa, ba, wx, bx, lam):
    tp = xr.shape[0]
    tr = _row_tile(tp)
    qb = tr // BLOCK

    def body(xr_ref, yr_ref, cw_ref, cb_ref, wa_ref, ba_ref, wx_ref, bx_ref, lam_ref, hr_ref, rec_ref, halo, hprev):
        t = pl.program_id(0)

        @pl.when(t == 0)
        def _():
            halo[...] = jnp.zeros_like(halo)
            hprev[...] = jnp.zeros_like(hprev)

        cw, cb = cw_ref[...], cb_ref[...]
        wa_m, ba_v, wx_m, bx_v = wa_ref[...], ba_ref[...], wx_ref[...], bx_ref[...]
        sp = _softplus(-lam_ref[...])
        before, h_last = halo[...], hprev[0:1]
        for i in range(qb):
            rows = slice(i * BLOCK, (i + 1) * BLOCK)
            x = xr_ref[rows]
            taps = _conv_taps(x, before)
            before = x[BLOCK - 8:]
            xc = cb + sum(cw[k:k + 1] * taps[k] for k in range(4))
            _, _, ig, a, mult = _lru_gates(xc, wa_m, ba_v, wx_m, bx_v, sp)
            u = mult * (ig * xc)
            if i == 0:
                pos = t * tr + lax.broadcasted_iota(jnp.int32, xc.shape, 0)
                u = jnp.where(pos >= PAD_ROWS, u, 0.0)
            h = _scan_fwd(a, u, h_last)
            h_last = h[BLOCK - 1:]
            hr_ref[rows] = h
            gl, _ = _gelu(yr_ref[rows])
            rec_ref[rows] = (gl * h).astype(BF16)
        halo[...] = before
        hprev[0:1] = h_last

    blk = pl.BlockSpec((tr, LRU_WIDTH), lambda t: (t, 0))
    full = lambda a: pl.BlockSpec(a.shape, lambda t: (0,) * a.ndim)
    small = [conv_w, conv_b, wa, ba, wx, bx, lam]
    return pl.pallas_call(
        body, name="lru_fwd", grid=(tp // tr,),
        in_specs=[blk, blk] + [full(a) for a in small],
        out_specs=[blk, blk],
        out_shape=[jax.ShapeDtypeStruct((tp, LRU_WIDTH), F32), jax.ShapeDtypeStruct((tp, LRU_WIDTH), BF16)],
        scratch_shapes=[pltpu.VMEM((8, LRU_WIDTH), F32), pltpu.VMEM((8, LRU_WIDTH), F32)],
        compiler_params=_params("arbitrary"),
    )(xr, yr, *small)


def _outproj_fwd(attn, rec, w_out, head, x, g_post_mix, g_pre_ffn):
    tp = attn.shape[0]
    tr = _row_tile(tp)
    qb = tr // BLOCK

    def body(*refs):
        a_ref, r_ref, w_ref, head_ref = refs[:4]
        pieces = refs[4:4 + qb]
        gm_ref, gf_ref, mix_ref, h1_ref, u1_ref = refs[4 + qb:]
        mix = _dot(a_ref[...], w_ref[:ATTN_WIDTH]) + _dot(r_ref[...], w_ref[ATTN_WIDTH:])
        mix_ref[...] = mix
        mhat, _ = _rms(mix)
        h1 = _seq_tile(head_ref[...], pieces, pl.program_id(0)) + mhat * gm_ref[...]
        h1_ref[...] = h1
        hhat, _ = _rms(h1)
        u1_ref[...] = (hhat * gf_ref[...]).astype(BF16)

    row = lambda w: pl.BlockSpec((tr, w), lambda i: (i, 0))
    full = lambda a: pl.BlockSpec(a.shape, lambda i: (0,) * a.ndim)
    return pl.pallas_call(
        body, name="outproj_fwd", grid=(tp // tr,),
        in_specs=[row(ATTN_WIDTH), row(LRU_WIDTH), full(w_out), full(head)] + _seq_specs(tr)
        + [full(g_post_mix), full(g_pre_ffn)],
        out_specs=[row(D_MODEL), row(D_MODEL), row(D_MODEL)],
        out_shape=[jax.ShapeDtypeStruct((tp, D_MODEL), F32), jax.ShapeDtypeStruct((tp, D_MODEL), F32),
                   jax.ShapeDtypeStruct((tp, D_MODEL), BF16)],
        compiler_params=_params("parallel"),
    )(attn, rec, w_out, head, *([x] * qb), g_post_mix, g_pre_ffn)


def _resident(a):
    return pl.BlockSpec(a.shape, lambda *_: (0,) * a.ndim, pipeline_mode=pl.Buffered(1))


def _ffn_fwd(u1, w1, w2, h1, tgt, g_post_ffn):
    tp = h1.shape[0]
    tr = _row_tile(tp)
    qb, nt = tr // BLOCK, tp // tr
    sr = tr // N_CHIPS

    def body(*refs):
        u_ref, w1_ref, w2_ref, h1_ref = refs[:4]
        t_pieces = refs[4:4 + qb]
        g_ref, r1_ref, dy_ref, df2_ref, loss_ref, dg_ref, acc = refs[4 + qb:]
        i, c = pl.program_id(0), pl.program_id(1)
        cur = i % 2

        @pl.when((i == 0) & (c == 0))
        def _():
            loss_ref[...] = jnp.zeros_like(loss_ref)
            dg_ref[...] = jnp.zeros_like(dg_ref)
            acc[1] = jnp.zeros((tr, D_MODEL), F32)

        def matmuls():
            r = jnp.maximum(_dot(u_ref[...], w1_ref[c]), 0.0)
            r1_ref[...] = r.astype(BF16)
            return _dot((r * r).astype(BF16), w2_ref[c])

        def finish_previous_tile(k, valid):
            lo, hi = k * sr, (k + 1) * sr
            g = g_ref[...]
            fhat, rs = _rms(acc[1 - cur, lo:hi])
            h2 = h1_ref[lo:hi] + fhat * g
            rows = (i - 1) * tr + lo + lax.broadcasted_iota(jnp.int32, h2.shape, 0)
            tgt = jnp.concatenate([p[max(lo - s * BLOCK, 0):min(hi - s * BLOCK, BLOCK)] for s, p in enumerate(t_pieces)
                                   if lo < (s + 1) * BLOCK and hi > s * BLOCK], axis=0)
            err = jnp.where((rows >= BLOCK) & valid, h2 - tgt, 0.0)
            dy = err * (1.0 / D_MODEL)
            dy_ref[lo:hi] = dy
            loss_ref[...] += (0.5 / D_MODEL) * jnp.sum(err * err)
            df2, dg = _rms_bwd(fhat, rs, g, dy)
            df2_ref[lo:hi] = df2.astype(BF16)
            dg_ref[...] += dg

        for k in range(N_CHIPS):
            @pl.when((c == k) & (i < nt))
            def _(k=k):
                finish_previous_tile(k, i >= 1)
                if k == 0:
                    acc[cur] = matmuls()
                else:
                    acc[cur] += matmuls()

            @pl.when((c == k) & (i == nt))
            def _(k=k):
                finish_previous_tile(k, True)

    last = nt - 1
    this_row = pl.BlockSpec((tr, D_MODEL), lambda i, c: (jnp.minimum(i, last), 0))
    prev_row = pl.BlockSpec((tr, D_MODEL), lambda i, c: (jnp.maximum(i - 1, 0), 0))
    full = lambda a: pl.BlockSpec(a.shape, lambda i, c: (0,) * a.ndim)
    return pl.pallas_call(
        body, name="ffn_fwd", grid=(nt + 1, N_CHIPS),
        in_specs=[this_row, _resident(w1), _resident(w2), prev_row] + _seq_specs(tr, delay=1) + [full(g_post_ffn)],
        out_specs=[pl.BlockSpec((tr, FF_CHUNK), lambda i, c: (jnp.minimum(i, last), jnp.where(i < nt, c, N_CHIPS - 1))),
                   prev_row, prev_row,
                   pl.BlockSpec((1, 1), lambda i, c: (0, 0)), pl.BlockSpec((1, D_MODEL), lambda i, c: (0, 0))],
        out_shape=[jax.ShapeDtypeStruct((tp, D_FF), BF16), jax.ShapeDtypeStruct((tp, D_MODEL), F32),
                   jax.ShapeDtypeStruct((tp, D_MODEL), BF16), jax.ShapeDtypeStruct((1, 1), F32),
                   jax.ShapeDtypeStruct((1, D_MODEL), F32)],
        scratch_shapes=[pltpu.VMEM((2, tr, D_MODEL), F32)],
        compiler_params=_params("arbitrary", "arbitrary"),
    )(u1, w1, w2, h1, *([tgt] * qb), g_post_ffn)


def _ffn_bwd_data(df2, r1, w1, w2, dy, h1, mix, g_pre_ffn, g_post_mix):
    tp = h1.shape[0]
    tr = _row_tile(tp)
    nt = tp // tr
    sr = tr // N_CHIPS

    def body(df2_ref, r1_ref, w1_ref, w2_ref, dy_ref, h1_ref, mix_ref, gf_ref, gm_ref,
             da_ref, dh1_ref, dmix_ref, dgf_ref, dgm_ref, acc):
        i, c = pl.program_id(0), pl.program_id(1)
        cur = i % 2

        @pl.when((i == 0) & (c == 0))
        def _():
            dgf_ref[...] = jnp.zeros_like(dgf_ref)
            dgm_ref[...] = jnp.zeros_like(dgm_ref)
            acc[1] = jnp.zeros((tr, D_MODEL), F32)

        def matmuls():
            df = _dot_nt(df2_ref[...], w2_ref[c])
            da = (df * (2.0 * r1_ref[...].astype(F32))).astype(BF16)
            da_ref[...] = da
            return _dot_nt(da, w1_ref[c])

        def finish_previous_tile(k, valid):
            lo, hi = k * sr, (k + 1) * sr
            hhat, rs = _rms(h1_ref[lo:hi])
            dx, dgf = _rms_bwd(hhat, rs, gf_ref[...], acc[1 - cur, lo:hi])
            dh1 = dy_ref[lo:hi] + dx
            dh1_ref[lo:hi] = dh1
            mhat, rsm = _rms(mix_ref[lo:hi])
            dmix, dgm = _rms_bwd(mhat, rsm, gm_ref[...], dh1)
            dmix_ref[lo:hi] = dmix.astype(BF16)
            dgf_ref[...] += jnp.where(valid, dgf, 0.0)
            dgm_ref[...] += jnp.where(valid, dgm, 0.0)

        for k in range(N_CHIPS):
            @pl.when((c == k) & (i < nt))
            def _(k=k):
                finish_previous_tile(k, i >= 1)
                if k == 0:
                    acc[cur] = matmuls()
                else:
                    acc[cur] += matmuls()

            @pl.when((c == k) & (i == nt))
            def _(k=k):
                finish_previous_tile(k, True)

    last = nt - 1
    this_row = pl.BlockSpec((tr, D_MODEL), lambda i, c: (jnp.minimum(i, last), 0))
    prev_row = pl.BlockSpec((tr, D_MODEL), lambda i, c: (jnp.maximum(i - 1, 0), 0))
    chunk = pl.BlockSpec((tr, FF_CHUNK), lambda i, c: (jnp.minimum(i, last), jnp.where(i < nt, c, N_CHIPS - 1)))
    gain = pl.BlockSpec((1, D_MODEL), lambda i, c: (0, 0))
    return pl.pallas_call(
        body, name="ffn_bwd_data", grid=(nt + 1, N_CHIPS),
        in_specs=[this_row, chunk, _resident(w1), _resident(w2), prev_row, prev_row, prev_row, gain, gain],
        out_specs=[chunk, prev_row, prev_row, gain, gain],
        out_shape=[jax.ShapeDtypeStruct((tp, D_FF), BF16), jax.ShapeDtypeStruct((tp, D_MODEL), F32),
                   jax.ShapeDtypeStruct((tp, D_MODEL), BF16), jax.ShapeDtypeStruct((1, D_MODEL), F32),
                   jax.ShapeDtypeStruct((1, D_MODEL), F32)],
        scratch_shapes=[pltpu.VMEM((2, tr, D_MODEL), F32)],
        compiler_params=_params("arbitrary", "arbitrary"),
    )(df2, r1, w1, w2, dy, h1, mix, g_pre_ffn, g_post_mix)


def _ffn_bwd_weights(u1, da1, r1, df2):
    tp = u1.shape[0]
    tr = _wgrad_row_tile(tp)

    def body(u_ref, da_ref, r1_ref, df2_ref, dw1_ref, dw2_ref):
        i = pl.program_id(1)
        r = r1_ref[...].astype(F32)
        p1 = _dot_tn(u_ref[...], da_ref[...])
        p2 = _dot_tn((r * r).astype(BF16), df2_ref[...])

        @pl.when(i == 0)
        def _():
            dw1_ref[0] = p1
            dw2_ref[0] = p2

        @pl.when(i > 0)
        def _():
            dw1_ref[0] += p1
            dw2_ref[0] += p2

    row = pl.BlockSpec((tr, D_MODEL), lambda c, i: (i, 0))
    chunk = pl.BlockSpec((tr, FF_CHUNK), lambda c, i: (i, c))
    return pl.pallas_call(
        body, name="ffn_bwd_weights", grid=(N_CHIPS, tp // tr),
        in_specs=[row, chunk, chunk, row],
        out_specs=[pl.BlockSpec((1, D_MODEL, FF_CHUNK), lambda c, i: (c, 0, 0)),
                   pl.BlockSpec((1, FF_CHUNK, D_MODEL), lambda c, i: (c, 0, 0))],
        out_shape=[jax.ShapeDtypeStruct((N_CHIPS, D_MODEL, FF_CHUNK), F32),
                   jax.ShapeDtypeStruct((N_CHIPS, FF_CHUNK, D_MODEL), F32)],
        compiler_params=_params("parallel", "arbitrary"),
    )(u1, da1, r1, df2)


def _outproj_bwd(dmix, w_out, attn, rec, token):
    tp = dmix.shape[0]
    tr = _wgrad_row_tile(tp)

    def body(dm_ref, w_ref, a_ref, r_ref, _, da_ref, dr_ref, dw_ref):
        i = pl.program_id(0)
        dm = dm_ref[...]
        dcat = _dot_nt(dm, w_ref[...])
        da_ref[...] = dcat[:, :ATTN_WIDTH].astype(BF16)
        dr_ref[...] = dcat[:, ATTN_WIDTH:]
        pa = _dot_tn(a_ref[...], dm)
        pr = _dot_tn(r_ref[...], dm)

        @pl.when(i == 0)
        def _():
            dw_ref[:ATTN_WIDTH] = pa
            dw_ref[ATTN_WIDTH:] = pr

        @pl.when(i > 0)
        def _():
            dw_ref[:ATTN_WIDTH] += pa
            dw_ref[ATTN_WIDTH:] += pr

    row = lambda w: pl.BlockSpec((tr, w), lambda i: (i, 0))
    full = pl.BlockSpec((D_MODEL, D_MODEL), lambda i: (0, 0))
    return pl.pallas_call(
        body, name="outproj_bwd", grid=(tp // tr,),
        in_specs=[row(D_MODEL), full, row(ATTN_WIDTH), row(LRU_WIDTH), pl.BlockSpec(token.shape, lambda i: (0, 0))],
        out_specs=[row(ATTN_WIDTH), row(LRU_WIDTH), full],
        out_shape=[jax.ShapeDtypeStruct((tp, ATTN_WIDTH), BF16), jax.ShapeDtypeStruct((tp, LRU_WIDTH), F32),
                   jax.ShapeDtypeStruct((D_MODEL, D_MODEL), F32)],
        compiler_params=_params("arbitrary"),
    )(dmix, w_out, attn, rec, token)


N_VEC_ROWS = 8


def _lru_bwd(xr, yr, hr, drec, conv_w, conv_b, wa, ba, wx, bx, lam, token):
    tp = xr.shape[0]
    tr = _row_tile(tp)
    qb, nt = tr // BLOCK, tp // tr

    def body(xr_ref, xh_ref, yr_ref, hr_ref, hp_ref, dr_ref, cw_ref, cb_ref, wa_ref, ba_ref, wx_ref, bx_ref, lam_ref, _,
             dxr_ref, dyr_ref, dwa_ref, dwx_ref, vec_ref, g_next, a_next, dxc_next, dsp):
        s = pl.program_id(0)
        t = nt - 1 - s

        @pl.when(s == 0)
        def _():
            g_next[...] = jnp.zeros_like(g_next)
            a_next[...] = jnp.zeros_like(a_next)
            dxc_next[...] = jnp.zeros_like(dxc_next)
            dsp[...] = jnp.zeros_like(dsp)
            dwa_ref[...] = jnp.zeros_like(dwa_ref)
            dwx_ref[...] = jnp.zeros_like(dwx_ref)
            vec_ref[...] = jnp.zeros_like(vec_ref)

        first_tile = t == 0
        cw, cb = cw_ref[...], cb_ref[...]
        lam_v = lam_ref[...]
        sp = _softplus(-lam_v)
        wa_m, ba_v, wx_m, bx_v = wa_ref[...], ba_ref[...], wx_ref[...], bx_ref[...]
        rows = lax.broadcasted_iota(jnp.int32, (BLOCK, LRU_WIDTH), 0)
        col = lambda v: jnp.sum(v, axis=0, keepdims=True)

        g_after, a_after, dxc_after = g_next[0:1], a_next[0:1], dxc_next[...]
        xbs, dgrs, dgis = [], [], []
        vec = [jnp.zeros((1, LRU_WIDTH), F32) for _ in range(N_VEC_ROWS)]
        for i in reversed(range(qb)):
            blk = slice(i * BLOCK, (i + 1) * BLOCK)
            if i == 0:
                x_before = jnp.where(first_tile, 0.0, xh_ref[...])
                h_before = jnp.where(first_tile, 0.0, hp_ref[7:8])
            else:
                x_before = xr_ref[i * BLOCK - 8:i * BLOCK]
                h_before = hr_ref[i * BLOCK - 1:i * BLOCK]
            taps = _conv_taps(xr_ref[blk], x_before)
            xc = cb + sum(cw[k:k + 1] * taps[k] for k in range(4))
            xb, r, ig, a, mult = _lru_gates(xc, wa_m, ba_v, wx_m, bx_v, sp)

            yr_v = yr_ref[blk]
            gl, th = _gelu(yr_v)
            h = hr_ref[blk]
            drec = dr_ref[blk]
            dyr_ref[blk] = (drec * h * _gelu_grad(yr_v, th)).astype(BF16)

            a_up = jnp.where(rows == BLOCK - 1, a_after, pltpu.roll(a, BLOCK - 1, 0))
            g = _scan_rev(a_up, drec * gl, g_after)
            g_after, a_after = g[0:1], a[0:1]

            h_prev = jnp.where(rows == 0, h_before, pltpu.roll(h, 1, 0))
            du, da = g, g * h_prev
            if i == 0:
                real = (t * tr + rows) >= PAD_ROWS
                du, da = jnp.where(real, du, 0.0), jnp.where(real, da, 0.0)
            dmult = du * (ig * xc)
            dig = du * (mult * xc)
            dxc = du * (mult * ig)
            dlog_a = da * a - dmult * (a * a / mult)
            if i == 0:
                dlog_a = jnp.where(real, dlog_a, 0.0)
            dgr = (dlog_a * (-LRU_C * sp)) * (r * (1.0 - r))
            dgi = dig * (ig * (1.0 - ig))
            dgr_b, dgi_b = dgr.astype(BF16), dgi.astype(BF16)
            dxc = dxc + _dot_nt(dgr_b, wa_m) + _dot_nt(dgi_b, wx_m)
            xbs.append(xb)
            dgrs.append(dgr_b)
            dgis.append(dgi_b)

            ext = jnp.concatenate([dxc, dxc_after], axis=0)
            up = [ext[:BLOCK] if j == 0 else pltpu.roll(ext, BLOCK + 8 - j, 0)[:BLOCK] for j in range(4)]
            dxr_ref[blk] = sum(cw[k:k + 1] * up[3 - k] for k in range(4)).astype(BF16)
            dxc_after = dxc[:8]

            for k in range(4):
                vec[k] = vec[k] + col(dxc * taps[k])
            vec[4] = vec[4] + col(dxc)
            vec[5] = vec[5] + col(dgr)
            vec[6] = vec[6] + col(dgi)
            vec[7] = vec[7] + col(dlog_a * (-LRU_C * r))

        g_next[0:1], a_next[0:1], dxc_next[...] = g_after, a_after, dxc_after
        xb_all = jnp.concatenate(xbs, axis=0)
        dwa_ref[...] += _dot_tn(xb_all, jnp.concatenate(dgrs, axis=0))
        dwx_ref[...] += _dot_tn(xb_all, jnp.concatenate(dgis, axis=0))
        for k in range(7):
            vec_ref[k:k + 1] += vec[k]
        dsp[0:1] += vec[7]

        @pl.when(s == nt - 1)
        def _():
            vec_ref[7:8] = dsp[0:1] * (-_sigmoid(-lam_v))

    blk_spec = pl.BlockSpec((tr, LRU_WIDTH), lambda s: (nt - 1 - s, 0))
    rows_before = pl.BlockSpec((8, LRU_WIDTH), lambda s: (jnp.maximum((nt - 1 - s) * (tr // 8) - 1, 0), 0))
    full = lambda a: pl.BlockSpec(a.shape, lambda s: (0,) * a.ndim)
    small = [conv_w, conv_b, wa, ba, wx, bx, lam, token]
    sq = pl.BlockSpec((LRU_WIDTH, LRU_WIDTH), lambda s: (0, 0))
    return pl.pallas_call(
        body, name="lru_bwd", grid=(nt,),
        in_specs=[blk_spec, rows_before, blk_spec, blk_spec, rows_before, blk_spec] + [full(a) for a in small],
        out_specs=[blk_spec, blk_spec, sq, sq, pl.BlockSpec((N_VEC_ROWS, LRU_WIDTH), lambda s: (0, 0))],
        out_shape=[jax.ShapeDtypeStruct((tp, LRU_WIDTH), BF16), jax.ShapeDtypeStruct((tp, LRU_WIDTH), BF16),
                   jax.ShapeDtypeStruct((LRU_WIDTH, LRU_WIDTH), F32), jax.ShapeDtypeStruct((LRU_WIDTH, LRU_WIDTH), F32),
                   jax.ShapeDtypeStruct((N_VEC_ROWS, LRU_WIDTH), F32)],
        scratch_shapes=[pltpu.VMEM((8, LRU_WIDTH), F32)] * 4,
        compiler_params=_params("arbitrary"),
    )(xr, xr, yr, hr, hr, drec, *small)


def _attn_bwd(qkv, dattn, sinks):
    tp = qkv.shape[0]
    tr = _row_tile(tp)
    qb, nt = tr // BLOCK, tp // tr
    n_groups = ATTN_HEADS // GQA_GROUP
    sink_rows, bias = _attn_consts(sinks)

    def body(s_ref, b_ref, q_ref, kp_ref, kc_ref, vp_ref, vc_ref, do_ref, dq_ref, dkv_ref, ex_ref, ds_ref, dsink):
        t = pl.program_id(0)

        @pl.when(t == 0)
        def _():
            dsink[...] = jnp.zeros_like(dsink)

        k_all = jnp.concatenate([kp_ref[...], kc_ref[...]], axis=0)
        v_all = jnp.concatenate([vp_ref[...], vc_ref[...]], axis=0)
        tail = None
        for i in range(qb):
            rows = slice(i * BLOCK, (i + 1) * BLOCK)
            q, do = q_ref[rows], do_ref[rows]
            k2, v2 = k_all[i * BLOCK:(i + 2) * BLOCK], v_all[i * BLOCK:(i + 2) * BLOCK]
            bias_n = _block_bias(b_ref, t, qb, i)
            dqs, dks, dvs = [], [], []
            for g in range(n_groups):
                cols = slice(g * HEAD_DIM, (g + 1) * HEAD_DIM)
                k_g, v_g = k2[:, cols], v2[:, cols]
                qg = _stack_heads(q, g) * jnp.asarray(_QSCALE, BF16)
                dog = _stack_heads(do, g)
                p, ps = _attn_probs_t(k_g, qg, bias_n, s_ref[g:g + 1])
                dpt = _dot_nt(v_g, dog)
                delta = jnp.sum(p * dpt, axis=0, keepdims=True)
                dst = (p * (dpt - delta)).astype(BF16)
                dqs.append(_dot_tn(dst, k_g) * _QSCALE)
                dks.append(_dot(dst, qg))
                dvs.append(_dot(p.astype(BF16), dog))
                dsink[g:g + 1] -= ps * delta
            dq_ref[rows] = _unstack_heads(dqs).astype(BF16)
            dkv = jnp.concatenate(dks + dvs, axis=1)
            if i == 0:
                ex_ref[0] = dkv[:BLOCK]
            else:
                dkv_ref[(i - 1) * BLOCK:i * BLOCK] = (tail + dkv[:BLOCK]).astype(BF16)
            tail = dkv[BLOCK:]
        dkv_ref[(qb - 1) * BLOCK:] = tail.astype(BF16)

        @pl.when(t == nt - 1)
        def _():
            lane = lax.broadcasted_iota(jnp.int32, (1, ATTN_HEADS), 1)
            acc = jnp.zeros((1, ATTN_HEADS), F32)
            for h in range(ATTN_HEADS):
                g, hh = divmod(h, GQA_GROUP)
                acc = acc + jnp.where(lane == h, jnp.sum(dsink[g:g + 1, hh * BLOCK:(hh + 1) * BLOCK]), 0.0)
            ds_ref[...] = acc

    cur = lambda w: pl.BlockSpec((tr, w), lambda t: (t, 0))
    return pl.pallas_call(
        body, name="attn_bwd", grid=(nt,),
        in_specs=[_SINK_SPEC, _BIAS_SPEC, cur(ATTN_WIDTH)] + _kv_specs(tr) + [cur(ATTN_WIDTH)],
        out_specs=[cur(ATTN_WIDTH), cur(2 * KV_WIDTH), pl.BlockSpec((1, BLOCK, 2 * KV_WIDTH), lambda t: (t, 0, 0)),
                   pl.BlockSpec((1, ATTN_HEADS), lambda t: (0, 0))],
        out_shape=[jax.ShapeDtypeStruct((tp, ATTN_WIDTH), BF16), jax.ShapeDtypeStruct((tp, 2 * KV_WIDTH), BF16),
                   jax.ShapeDtypeStruct((nt, BLOCK, 2 * KV_WIDTH), F32), jax.ShapeDtypeStruct((1, ATTN_HEADS), F32)],
        scratch_shapes=[pltpu.VMEM((n_groups, GROUP_ROWS), F32)],
        compiler_params=_params("arbitrary"),
    )(sink_rows, bias, qkv, qkv, qkv, qkv, qkv, dattn)


def _fix_dkv(dkv, dkv_extra):
    tp = dkv.shape[0]
    tr = _row_tile(tp)
    nt, qb = tp // tr, tr // BLOCK
    if nt == 1:
        return dkv

    def body(d_ref, ex_ref, o_ref):
        o_ref[...] = (d_ref[...].astype(F32) + ex_ref[0]).astype(BF16)

    last = pl.BlockSpec((BLOCK, 2 * KV_WIDTH), lambda t: (t * qb + qb - 1, 0))
    return pl.pallas_call(
        body, name="fix_dkv", grid=(nt - 1,),
        in_specs=[last, pl.BlockSpec((1, BLOCK, 2 * KV_WIDTH), lambda t: (t + 1, 0, 0))],
        out_specs=last, out_shape=jax.ShapeDtypeStruct(dkv.shape, dkv.dtype),
        input_output_aliases={0: 0}, compiler_params=_params("parallel"),
    )(dkv, dkv_extra)


def _inproj_wgrad(dq, dkv, dxr, dyr, u0):
    tp = dq.shape[0]
    tr = _wgrad_row_tile(tp)

    def body(dq_ref, dkv_ref, dxr_ref, dyr_ref, u_ref, dw_ref):
        i = pl.program_id(0)
        dz = jnp.concatenate([dq_ref[...], dkv_ref[...], dxr_ref[...], dyr_ref[...]], axis=1)
        pw = _dot_tn(dz, u_ref[...])

        @pl.when(i == 0)
        def _():
            dw_ref[...] = pw

        @pl.when(i > 0)
        def _():
            dw_ref[...] += pw

    row = lambda w: pl.BlockSpec((tr, w), lambda i: (i, 0))
    return pl.pallas_call(
        body, name="inproj_wgrad", grid=(tp // tr,),
        in_specs=[row(ATTN_WIDTH), row(2 * KV_WIDTH), row(LRU_WIDTH), row(LRU_WIDTH), row(D_MODEL)],
        out_specs=pl.BlockSpec((IN_WIDTH, D_MODEL), lambda i: (0, 0)),
        out_shape=jax.ShapeDtypeStruct((IN_WIDTH, D_MODEL), F32),
        compiler_params=_params("arbitrary"),
    )(dq, dkv, dxr, dyr, u0)


def _inproj_dgrad(dq, dkv, dxr, dyr, w_in, head, x, dh1, g, token):
    tp = dq.shape[0]
    tr = _row_tile(tp)
    nt, qb = tp // tr, tr // BLOCK

    def body(*refs):
        dq_ref, dkv_ref, dxr_ref, dyr_ref, w_ref, head_ref = refs[:6]
        pieces = refs[6:6 + qb]
        dh1_ref, g_ref, _, gx_ref, dhead_ref, dg_ref, buf, sems = refs[6 + qb:]
        i = pl.program_id(0)
        slot = i % 2

        def out_copy(step, at):
            return pltpu.make_async_copy(buf.at[at], gx_ref.at[pl.ds(step * tr - BLOCK, tr)], sems.at[at])

        dz = jnp.concatenate([dq_ref[...], dkv_ref[...], dxr_ref[...], dyr_ref[...]], axis=1)
        du = _dot(dz, w_ref[...])
        hhat, rs = _rms(_seq_tile(head_ref[...], pieces, i))
        dx, dg = _rms_bwd(hhat, rs, g_ref[...], du)
        dh0 = dh1_ref[...] + dx

        @pl.when(i >= 3)
        def _():
            out_copy(i - 2, slot).wait()

        buf[slot] = dh0

        @pl.when(i == 0)
        def _():
            dg_ref[...] = dg
            dhead_ref[...] = dh0[:BLOCK]
            if tr > BLOCK:
                first = pltpu.make_async_copy(buf.at[0, pl.ds(BLOCK, tr - BLOCK)], gx_ref.at[pl.ds(0, tr - BLOCK)],
                                              sems.at[0])
                first.start()
                first.wait()

        @pl.when(i >= 1)
        def _():
            dg_ref[...] += dg
            out_copy(i, slot).start()

        @pl.when(i == nt - 1)
        def _():
            if nt >= 3:
                out_copy(nt - 2, (nt - 2) % 2).wait()
            if nt >= 2:
                out_copy(nt - 1, (nt - 1) % 2).wait()

    row = lambda w: pl.BlockSpec((tr, w), lambda i: (i, 0))
    full = lambda shape: pl.BlockSpec(shape, lambda i: (0,) * len(shape))
    return pl.pallas_call(
        body, name="inproj_dgrad", grid=(tp // tr,),
        in_specs=[row(ATTN_WIDTH), row(2 * KV_WIDTH), row(LRU_WIDTH), row(LRU_WIDTH), full(w_in.shape),
                  full(head.shape)] + _seq_specs(tr) + [row(D_MODEL), full(g.shape), full(token.shape)],
        out_specs=[pl.BlockSpec(memory_space=pl.ANY), full((BLOCK, D_MODEL)), full((1, D_MODEL))],
        out_shape=[jax.ShapeDtypeStruct(x.shape, F32), jax.ShapeDtypeStruct((BLOCK, D_MODEL), F32),
                   jax.ShapeDtypeStruct((1, D_MODEL), F32)],
        scratch_shapes=[pltpu.VMEM((2, tr, D_MODEL), F32), pltpu.SemaphoreType.DMA((2,))],
        compiler_params=_params("arbitrary"),
    )(dq, dkv, dxr, dyr, w_in, head, *([x] * qb), dh1, g, token)


def _dense_block_diag(w):
    eye = jnp.eye(LRU_BLOCKS, dtype=w.dtype)
    return (w[:, :, None, :] * eye[:, None, :, None]).reshape(LRU_WIDTH, LRU_WIDTH)


def _diag_blocks(dense):
    d4 = dense.reshape(LRU_BLOCKS, LRU_BLOCK, LRU_BLOCKS, LRU_BLOCK)
    return jnp.stack([d4[n, :, n, :] for n in range(LRU_BLOCKS)])


def _local_step(head, x, tgt, g_pre_mix, w_in, conv_w, conv_b, w_a, b_a, w_x, b_x, lam, sinks, g_post_mix,
                g_pre_ffn, g_post_ffn, late_weights, on_ffn_grads, on_outproj_bwd, on_mixer_grads, token):
    wa = _dense_block_diag(w_a).astype(BF16)
    wx = _dense_block_diag(w_x).astype(BF16)

    u0, qkv, xr, yr = _inproj_fwd(head, x, g_pre_mix, w_in, token)
    attn = _attn_fwd(qkv, sinks)
    hr, rec = _lru_fwd(xr, yr, conv_w, conv_b, wa, b_a, wx, b_x, lam)
    w_out, w1, w2 = late_weights([attn, rec])
    mix, h1, u1 = _outproj_fwd(attn, rec, w_out, head, x, g_post_mix, g_pre_ffn)
    r1, dy, df2, loss, dg_post_ffn = _ffn_fwd(u1, w1, w2, h1, tgt, g_post_ffn)

    da1, dh1, dmix, dg_pre_ffn, dg_post_mix = _ffn_bwd_data(df2, r1, w1, w2, dy, h1, mix, g_pre_ffn, g_post_mix)
    dw1, dw2 = _ffn_bwd_weights(u1, da1, r1, df2)
    token2 = on_ffn_grads(dw1, dw2)
    dattn, drec, dw_out = _outproj_bwd(dmix, w_out, attn, rec, token2)
    token3 = on_outproj_bwd(dattn)
    dxr, dyr, dwa, dwx, vec = _lru_bwd(xr, yr, hr, drec, conv_w, conv_b, wa, b_a, wx, b_x, lam, token3)
    dq, dkv, dkv_extra, dsinks = _attn_bwd(qkv, dattn, sinks)
    dkv = _fix_dkv(dkv, dkv_extra)
    dw_in = _inproj_wgrad(dq, dkv, dxr, dyr, u0)
    token4 = on_mixer_grads(dw_in, dw_out)
    dx, dhead, dg_pre_mix = _inproj_dgrad(dq, dkv, dxr, dyr, w_in, head, x, dh1, g_pre_mix, token4)

    grads = dict(
        g_pre_mix=dg_pre_mix, conv_w=vec[0:4], conv_b=vec[4:5], w_a=_diag_blocks(dwa), b_a=vec[5:6],
        w_x=_diag_blocks(dwx), b_x=vec[6:7], lru_lambda=vec[7:8], attn_sinks=dsinks,
        g_post_mix=dg_post_mix, g_pre_ffn=dg_pre_ffn, g_post_ffn=dg_post_ffn)
    return loss, dx, dhead, grads


HBM = pl.BlockSpec(memory_space=pltpu.HBM)


def _mesh_pos():
    return lax.axis_index("x"), lax.axis_index("y"), lax.axis_index("c")


def _other_chips(x, y):
    return [(1 - x, y), (x, 1 - y), (1 - x, 1 - y)]


def _remote(src, dst, send_sem, recv_sem, to):
    return pltpu.make_async_remote_copy(src_ref=src, dst_ref=dst, send_sem=send_sem, recv_sem=recv_sem,
                                        device_id=to, device_id_type=MESH)


def _gather_weights(shards, lands, tiny, tiny_land):
    nbig = len(shards)

    def body(*refs):
        srcs, tiny_src = refs[:nbig], refs[nbig]
        outs, tiny_out = refs[2 * nbig + 2:3 * nbig + 2], refs[3 * nbig + 2]
        ici_send, ici_recv, d2d_send, d2d_recv, tiny_send, tiny_recv = refs[3 * nbig + 3:]
        x, y, c = _mesh_pos()
        me = 2 * x + y
        chips = _other_chips(x, y)
        sibling = (x, y, 1 - c)
        sends = []
        for w, (src, out) in enumerate(zip(srcs, outs)):
            hr = src.shape[0] // 2
            for j, chip in enumerate(chips):
                k = 3 * w + j
                cp = _remote(src.at[pl.ds(c * hr, hr)], out.at[me, pl.ds(c * hr, hr)],
                             ici_send.at[k], ici_recv.at[k], (*chip, c))
                cp.start()
                sends.append(cp)
        for j, chip in enumerate(chips):
            cp = _remote(tiny_src, tiny_out.at[me], tiny_send.at[j], tiny_recv.at[j], (*chip, c))
            cp.start()
            sends.append(cp)
        for w, (src, out) in enumerate(zip(srcs, outs)):
            hr = src.shape[0] // 2
            for j, (px, py) in enumerate(chips):
                k = 3 * w + j
                landed = out.at[2 * px + py, pl.ds(c * hr, hr)]
                _remote(landed, landed, ici_send.at[k], ici_recv.at[k], sibling).wait_recv()
                cp = _remote(landed, landed, d2d_send.at[k], d2d_recv.at[k], sibling)
                cp.start()
                sends.append(cp)
        for w, (src, out) in enumerate(zip(srcs, outs)):
            hr = src.shape[0] // 2
            for j, (px, py) in enumerate(chips):
                k = 3 * w + j
                other = out.at[2 * px + py, pl.ds((1 - c) * hr, hr)]
                _remote(other, other, d2d_send.at[k], d2d_recv.at[k], sibling).wait_recv()
        for j, (px, py) in enumerate(chips):
            blk = tiny_out.at[2 * px + py]
            _remote(blk, blk, tiny_send.at[j], tiny_recv.at[j], sibling).wait_recv()
        for cp in sends:
            cp.wait_send()

    out_shape = [jax.ShapeDtypeStruct(l.shape, l.dtype) for l in list(lands) + [tiny_land]]
    n = 3 * nbig
    return pl.pallas_call(
        body, name="gather_weights", out_shape=out_shape,
        in_specs=[HBM] * (2 * nbig + 2), out_specs=[HBM] * (nbig + 1),
        input_output_aliases={nbig + 1 + i: i for i in range(nbig + 1)},
        scratch_shapes=[pltpu.SemaphoreType.DMA((n,)),
                        pltpu.SemaphoreType.DMA((n,)), pltpu.SemaphoreType.DMA((n,)), pltpu.SemaphoreType.DMA((n,)),
                        pltpu.SemaphoreType.DMA((3,)), pltpu.SemaphoreType.DMA((3,))],
    )(*shards, tiny, *lands, tiny_land)


def _prep_shard(w, me):
    rows, cols = w.shape
    tr = 256 if rows % 256 == 0 else rows

    def body(me_ref, w_ref, s_ref, l_ref):
        b = w_ref[...].astype(BF16)
        s_ref[...] = b
        l_ref[0] = b

    return pl.pallas_call(
        body, name="prep_shard",
        grid_spec=pltpu.PrefetchScalarGridSpec(
            num_scalar_prefetch=1, grid=(rows // tr,),
            in_specs=[pl.BlockSpec((tr, cols), lambda i, me_ref: (i, 0))],
            out_specs=[pl.BlockSpec((tr, cols), lambda i, me_ref: (i, 0)),
                       pl.BlockSpec((1, tr, cols), lambda i, me_ref: (me_ref[0], i, 0))]),
        out_shape=[jax.ShapeDtypeStruct((rows, cols), BF16), jax.ShapeDtypeStruct((N_CHIPS, rows, cols), BF16)],
        compiler_params=_params("parallel"),
    )(me, w)


def _prep_tiny(tiny, me, slots=N_CHIPS):
    def body(me_ref, t_ref, l_ref):
        l_ref[0] = t_ref[...]

    return pl.pallas_call(
        body, name="prep_tiny",
        grid_spec=pltpu.PrefetchScalarGridSpec(
            num_scalar_prefetch=1, grid=(1,),
            in_specs=[pl.BlockSpec(tiny.shape, lambda i, me_ref: (0, 0))],
            out_specs=pl.BlockSpec((1,) + tiny.shape, lambda i, me_ref: (me_ref[0], 0, 0))),
        out_shape=jax.ShapeDtypeStruct((slots,) + tiny.shape, tiny.dtype),
    )(me, tiny)


N_DEV = 8


def _sibling_exchange(parts, token):
    def body(*refs):
        n = len(parts)
        srcs, outs, send_sems, recv_sems = refs[:n], refs[n + 1:2 * n + 1], refs[2 * n + 1], refs[2 * n + 2]
        x, y, c = _mesh_pos()
        sibling = (x, y, 1 - c)
        cps = []
        for w, (src, out) in enumerate(zip(srcs, outs)):
            hr = src.shape[1] // 2
            cp = _remote(src.at[:, pl.ds((1 - c) * hr, hr)], out, send_sems.at[w], recv_sems.at[w], sibling)
            cp.start()
            cps.append(cp)
        for cp in cps:
            cp.wait()

    n = len(parts)
    return pl.pallas_call(
        body, name="sibling_exchange",
        out_shape=[jax.ShapeDtypeStruct((p.shape[0], p.shape[1] // 2, p.shape[2]), p.dtype) for p in parts],
        in_specs=[HBM] * n + [pl.BlockSpec(memory_space=pl.ANY)], out_specs=[HBM] * n,
        scratch_shapes=[pltpu.SemaphoreType.DMA((n,)), pltpu.SemaphoreType.DMA((n,))],
    )(*parts, token)


def _chip_presum(part, from_sibling, pos):
    _, hr, cols = from_sibling.shape
    tr = 256 if hr % 256 == 0 else hr
    steps = hr // tr

    def body(pos_ref, a_ref, b_ref, o_ref, land_ref):
        s = (a_ref[...] + b_ref[...]).astype(BF16)
        o_ref[...] = s

        @pl.when(pl.program_id(1) == pos_ref[1])
        def _():
            land_ref[...] = s

    return pl.pallas_call(
        body, name="chip_presum",
        grid_spec=pltpu.PrefetchScalarGridSpec(
            num_scalar_prefetch=1, grid=(steps, N_CHIPS),
            in_specs=[pl.BlockSpec((1, tr, cols), lambda i, j, p: (j, p[0] * steps + i, 0)),
                      pl.BlockSpec((1, tr, cols), lambda i, j, p: (j, i, 0))],
            out_specs=[pl.BlockSpec((1, tr, cols), lambda i, j, p: (j, i, 0)),
                       pl.BlockSpec((1, tr, cols), lambda i, j, p: (p[1], p[0] * steps + i, 0))]),
        out_shape=[jax.ShapeDtypeStruct(from_sibling.shape, BF16),
                   jax.ShapeDtypeStruct((N_CHIPS, 2 * hr, cols), BF16)],
        compiler_params=_params("arbitrary", "arbitrary"),
    )(pos, part, from_sibling)


def _scatter_partials(cparts, lands, done_cparts=(), done_lands=()):
    n_new = len(cparts)
    nw = n_new + len(done_cparts)

    def body(*refs):
        srcs = refs[:nw]
        outs = refs[2 * nw:3 * nw]
        own_send, own_recv, ici_send, ici_recv, d2d_send, d2d_recv = refs[3 * nw:]
        x, y, c = _mesh_pos()
        me = 2 * x + y
        chips = _other_chips(x, y)
        sibling = (x, y, 1 - c)
        sends = []
        for w in list(range(n_new, nw)) + list(range(n_new)):
            src, out = srcs[w], outs[w]
            hr = src.shape[1]
            mine = out.at[me, pl.ds(c * hr, hr)]
            cp = _remote(src.at[me], mine, own_send.at[w], own_recv.at[w], sibling)
            cp.start()
            sends.append(cp)
            for j, (px, py) in enumerate(chips):
                if w >= n_new:
                    break
                k = 3 * w + j
                cp = _remote(src.at[2 * px + py], mine, ici_send.at[k], ici_recv.at[k], (px, py, c))
                cp.start()
                sends.append(cp)
        for w in list(range(n_new, nw)) + list(range(n_new)):
            src, out = srcs[w], outs[w]
            hr = src.shape[1]
            for j, (px, py) in enumerate(chips):
                k = 3 * w + j
                landed = out.at[2 * px + py, pl.ds(c * hr, hr)]
                if w < n_new:
                    _remote(landed, landed, ici_send.at[k], ici_recv.at[k], sibling).wait_recv()
                cp = _remote(landed, landed, d2d_send.at[k], d2d_recv.at[k], sibling)
                cp.start()
                sends.append(cp)
        for w, (src, out) in enumerate(zip(srcs, outs)):
            hr = src.shape[1]
            other = out.at[me, pl.ds((1 - c) * hr, hr)]
            _remote(other, other, own_send.at[w], own_recv.at[w], sibling).wait_recv()
            for j, (px, py) in enumerate(chips):
                k = 3 * w + j
                other = out.at[2 * px + py, pl.ds((1 - c) * hr, hr)]
                _remote(other, other, d2d_send.at[k], d2d_recv.at[k], sibling).wait_recv()
        for cp in sends:
            cp.wait_send()

    n = 3 * nw
    dma = pltpu.SemaphoreType.DMA
    every = list(cparts) + list(done_cparts)
    every_lands = list(lands) + list(done_lands)
    return pl.pallas_call(
        body, name="scatter_partials",
        out_shape=[jax.ShapeDtypeStruct(l.shape, l.dtype) for l in every_lands],
        in_specs=[HBM] * (2 * nw), out_specs=[HBM] * nw,
        input_output_aliases={nw + i: i for i in range(nw)},
        scratch_shapes=[dma((nw,)), dma((nw,)), dma((n,)), dma((n,)), dma((n,)), dma((n,))],
    )(*every, *every_lands)


SEM = pl.BlockSpec(memory_space=pltpu.SEMAPHORE)
SPLIT_COPY = pltpu.CompilerParams(has_side_effects=pltpu.SideEffectType.DATAFLOW_SIDE_EFFECTING)


def _hbm(a):
    return pltpu.with_memory_space_constraint(a, pltpu.HBM)


def _gather_copies(srcs, lands, send_sems, recv_sems):
    x, y, c = _mesh_pos()
    me = 2 * x + y
    sends, recvs = [], []
    for w, (src, land) in enumerate(zip(srcs, lands)):
        hr = src.shape[0] // 2
        for j, (px, py) in enumerate(_other_chips(x, y)):
            k = 3 * w + j
            sends.append(_remote(src.at[pl.ds(c * hr, hr)], land.at[me, pl.ds(c * hr, hr)],
                                 send_sems.at[k], recv_sems.at[k], (px, py, c)))
            got = land.at[2 * px + py, pl.ds(c * hr, hr)]
            recvs.append(_remote(got, got, send_sems.at[k], recv_sems.at[k], (px, py, c)))
    return sends, recvs


def _scatter_copies(srcs, lands, send_sems, recv_sems):
    x, y, c = _mesh_pos()
    me = 2 * x + y
    sends, recvs = [], []
    for w, (src, land) in enumerate(zip(srcs, lands)):
        hr = src.shape[1]
        for j, (px, py) in enumerate(_other_chips(x, y)):
            k = 3 * w + j
            sends.append(_remote(src.at[2 * px + py], land.at[me, pl.ds(c * hr, hr)],
                                 send_sems.at[k], recv_sems.at[k], (px, py, c)))
            got = land.at[2 * px + py, pl.ds(c * hr, hr)]
            recvs.append(_remote(got, got, send_sems.at[k], recv_sems.at[k], (px, py, c)))
    return sends, recvs


def _sibling_copies(srcs, lands, send_sems, recv_sems):
    x, y, c = _mesh_pos()
    sibling = (x, y, 1 - c)
    sends, recvs = [], []
    for w, (src, land) in enumerate(zip(srcs, lands)):
        hr = src.shape[1] // 2
        sends.append(_remote(src.at[:, pl.ds((1 - c) * hr, hr)], land, send_sems.at[w], recv_sems.at[w], sibling))
        recvs.append(_remote(land, land, send_sems.at[w], recv_sems.at[w], sibling))
    return sends, recvs


def _all_peers_copies(srcs, lands, send_sems, recv_sems):
    x, y, c = _mesh_pos()
    (src,), (land,) = srcs, lands
    flip = lambda v, bit: 1 - v if bit else v
    sends, recvs = [], []
    for k in range(N_DEV - 1):
        px, py, pc = flip(x, (k + 1) & 4), flip(y, (k + 1) & 2), flip(c, (k + 1) & 1)
        sends.append(_remote(src, land.at[4 * x + 2 * y + c], send_sems.at[k], recv_sems.at[k], (px, py, pc)))
        got = land.at[4 * px + 2 * py + pc]
        recvs.append(_remote(got, got, send_sems.at[k], recv_sems.at[k], (px, py, pc)))
    return sends, recvs


def _split_start(name, copies_of, srcs, land_shapes, n_copies=None):
    n = len(srcs)
    k = 3 * n if n_copies is None else n_copies

    def body(*refs):
        src_refs, land_refs = refs[:n], refs[n:2 * n]
        send_sems, recv_sems = refs[2 * n], refs[2 * n + 1]
        token = refs[-1]
        sends, _ = copies_of(src_refs, land_refs, send_sems, recv_sems)
        for cp in sends:
            cp.start()
        token[...] = jnp.zeros_like(token)

    lands = [_hbm(s) for s in land_shapes]
    dma = pltpu.SemaphoreType.DMA
    res = pl.pallas_call(
        body, name=name,
        out_shape=(dma((k,)), dma((k,)), *[pltpu.HBM(s.shape, s.dtype) for s in srcs],
                   *[pltpu.HBM(s.shape, s.dtype) for s in land_shapes], jax.ShapeDtypeStruct((8, 128), F32)),
        in_specs=[HBM] * (2 * n),
        out_specs=(SEM, SEM, *([HBM] * (2 * n)), pl.BlockSpec(memory_space=pltpu.VMEM)),
        input_output_aliases={i: 2 + i for i in range(2 * n)},
        compiler_params=SPLIT_COPY,
    )(*[_hbm(s) for s in srcs], *lands)
    return res[0], res[1], list(res[2:2 + n]), list(res[2 + n:2 + 2 * n]), res[-1]


def _split_wait(name, copies_of, send_sems, recv_sems, srcs, lands, after):
    n = len(srcs)

    def body(*refs):
        src_refs, land_refs = refs[:n], refs[n:2 * n]
        sends, recvs = copies_of(src_refs, land_refs, refs[2 * n], refs[2 * n + 1])
        for cp in sends:
            cp.wait_send()
        for cp in recvs:
            cp.wait_recv()

    res = pl.pallas_call(
        body, name=name,
        out_shape=tuple(pltpu.HBM(s.shape, s.dtype) for s in list(srcs) + list(lands)),
        in_specs=[HBM] * (2 * n) + [SEM, SEM] + [pl.BlockSpec(memory_space=pl.ANY)] * len(after),
        out_specs=tuple([HBM] * (2 * n)),
        input_output_aliases={i: i for i in range(2 * n)},
        compiler_params=SPLIT_COPY,
    )(*srcs, *lands, send_sems, recv_sems, *after)
    return list(res[:n]), list(res[n:])


def _gather_finish(lands):
    n = len(lands)

    def body(*refs):
        outs = refs[n:2 * n]
        d2d_send, d2d_recv = refs[2 * n:]
        x, y, c = _mesh_pos()
        chips = _other_chips(x, y)
        sibling = (x, y, 1 - c)
        sends = []
        for w, out in enumerate(outs):
            hr = out.shape[1] // 2
            for j, (px, py) in enumerate(chips):
                landed = out.at[2 * px + py, pl.ds(c * hr, hr)]
                cp = _remote(landed, landed, d2d_send.at[3 * w + j], d2d_recv.at[3 * w + j], sibling)
                cp.start()
                sends.append(cp)
        for w, out in enumerate(outs):
            hr = out.shape[1] // 2
            for j, (px, py) in enumerate(chips):
                other = out.at[2 * px + py, pl.ds((1 - c) * hr, hr)]
                _remote(other, other, d2d_send.at[3 * w + j], d2d_recv.at[3 * w + j], sibling).wait_recv()
        for cp in sends:
            cp.wait_send()

    dma = pltpu.SemaphoreType.DMA
    return pl.pallas_call(
        body, name="gather_finish",
        out_shape=[jax.ShapeDtypeStruct(l.shape, l.dtype) for l in lands],
        in_specs=[HBM] * n, out_specs=[HBM] * n,
        input_output_aliases={i: i for i in range(n)},
        scratch_shapes=[dma((3 * n,)), dma((3 * n,))],
    )(*lands)


def _adamw(w, g, m, v):
    m = ADAM_B1 * m + (1.0 - ADAM_B1) * g
    v = ADAM_B2 * v + (1.0 - ADAM_B2) * (g * g)
    m_hat = m / (1.0 - ADAM_B1 ** ADAM_STEP)
    v_hat = v / (1.0 - ADAM_B2 ** ADAM_STEP)
    delta = -ADAM_LR * (m_hat / (jnp.sqrt(v_hat) + ADAM_EPS) + ADAM_WD * w)
    return delta, m, v


def _adamw_big(partials, w, m, v):
    rows, cols = w.shape
    tr = 256 if rows % 256 == 0 else rows

    def body(p_ref, w_ref, m_ref, v_ref, g_ref, d_ref, m2_ref, v2_ref):
        g = ((p_ref[0].astype(F32) + p_ref[1].astype(F32)) + p_ref[2].astype(F32)) + p_ref[3].astype(F32)
        g_ref[...] = g
        d_ref[...], m2_ref[...], v2_ref[...] = _adamw(w_ref[...], g, m_ref[...], v_ref[...])

    blk = pl.BlockSpec((tr, cols), lambda i: (i, 0))
    return pl.pallas_call(
        body, name="adamw_big", grid=(rows // tr,),
        in_specs=[pl.BlockSpec((N_CHIPS, tr, cols), lambda i: (0, i, 0)), blk, blk, blk],
        out_specs=[blk] * 4, out_shape=[jax.ShapeDtypeStruct((rows, cols), F32)] * 4,
        compiler_params=_params("parallel"),
    )(partials, w, m, v)


def _sum_devices(gathered, rows):
    cols = gathered.shape[1]

    def body(g_ref, o_ref):
        acc = g_ref[0:rows]
        for d in range(1, N_DEV):
            acc = acc + g_ref[d * rows:(d + 1) * rows]
        o_ref[...] = acc

    return pl.pallas_call(
        body, name="sum_devices", out_shape=jax.ShapeDtypeStruct((rows, cols), F32),
        in_specs=[pl.BlockSpec(memory_space=pltpu.VMEM)], out_specs=pl.BlockSpec(memory_space=pltpu.VMEM),
        compiler_params=pltpu.CompilerParams(vmem_limit_bytes=VMEM_LIMIT_V7X),
    )(gathered)


def _adamw_small(quads):
    n = len(quads)

    def body(*refs):
        ins, outs = refs[:4 * n], refs[4 * n:]
        for t in range(n):
            w, g, m, v = (r[...] for r in ins[4 * t:4 * t + 4])
            outs[3 * t][...], outs[3 * t + 1][...], outs[3 * t + 2][...] = _adamw(w, g, m, v)

    flat = [a for q in quads for a in q]
    vm = pl.BlockSpec(memory_space=pltpu.VMEM)
    res = pl.pallas_call(
        body, name="adamw_small",
        out_shape=[jax.ShapeDtypeStruct(q[0].shape, F32) for q in quads for _ in range(3)],
        in_specs=[vm] * (4 * n), out_specs=[vm] * (3 * n),
    )(*flat)
    return [tuple(res[3 * t:3 * t + 3]) for t in range(n)]


SMALL_PACK_ROWS = 96
_WEIGHTS = ['meta_tokens', 'g_pre_mix', 'w_in', 'conv_w', 'conv_b', 'w_a', 'b_a', 'w_x', 'b_x', 'lru_lambda',
            'attn_sinks', 'w_out', 'g_post_mix', 'g_pre_ffn', 'w_ff1', 'w_ff2', 'g_post_ffn']
_BIG = ['w_in', 'w_out', 'w_ff1', 'w_ff2']


def _pack_small(dmeta, g, loss):
    z = lambda r, c: jnp.zeros((r, c), F32)
    rows = [
        dmeta,
        g['g_pre_mix'], g['g_post_mix'], g['g_pre_ffn'], g['g_post_ffn'],
        jnp.concatenate([g['conv_w'], z(4, 512)], axis=1),
        jnp.concatenate([g['conv_b'], g['b_a']], axis=1),
        jnp.concatenate([g['b_x'], g['lru_lambda']], axis=1),
        jnp.concatenate([g['attn_sinks'], z(1, D_MODEL - ATTN_HEADS)], axis=1),
        jnp.concatenate([loss, z(1, D_MODEL - 1)], axis=1),
        z(4, D_MODEL),
        g['w_a'].reshape(32, D_MODEL), g['w_x'].reshape(32, D_MODEL),
    ]
    return jnp.concatenate(rows, axis=0)


def _unpack_small(s, chip):
    return dict(
        meta_tokens=lax.dynamic_slice(s[0:16], (0, chip * 256), (16, 256)),
        g_pre_mix=s[16:17], g_post_mix=s[17:18], g_pre_ffn=s[18:19], g_post_ffn=s[19:20],
        conv_w=lax.dynamic_slice(s[20:24], (0, chip * 128), (4, 128)).reshape(1, 4, 128),
        conv_b=s[24:25, :512], b_a=s[24:25, 512:], b_x=s[25:26, :512], lru_lambda=s[25:26, 512:],
        attn_sinks=s[26:27, :ATTN_HEADS], loss=s[27, 0],
        w_a=s[32:64].reshape(1, LRU_BLOCKS, LRU_BLOCK, LRU_BLOCK),
        w_x=s[64:96].reshape(1, LRU_BLOCKS, LRU_BLOCK, LRU_BLOCK))


def _as2d(a):
    if a.ndim == 2:
        return a
    return a.reshape(-1, a.shape[-1])


def kernel(x, meta_tokens, g_pre_mix, w_in, conv_w, conv_b, w_a, b_a, w_x, b_x, lru_lambda, attn_sinks, w_out, g_post_mix, g_pre_ffn, w_ff1, w_ff2, g_post_ffn, loss_target, m_meta_tokens, m_g_pre_mix, m_w_in, m_conv_w, m_conv_b, m_w_a, m_b_a, m_w_x, m_b_x, m_lru_lambda, m_attn_sinks, m_w_out, m_g_post_mix, m_g_pre_ffn, m_w_ff1, m_w_ff2, m_g_post_ffn, v_meta_tokens, v_g_pre_mix, v_w_in, v_conv_w, v_conv_b, v_w_a, v_b_a, v_w_x, v_b_x, v_lru_lambda, v_attn_sinks, v_w_out, v_g_post_mix, v_g_pre_ffn, v_w_ff1, v_w_ff2, v_g_post_ffn):
    weights = dict(meta_tokens=meta_tokens, g_pre_mix=g_pre_mix, w_in=w_in, conv_w=conv_w, conv_b=conv_b, w_a=w_a,
                   b_a=b_a, w_x=w_x, b_x=b_x, lru_lambda=lru_lambda, attn_sinks=attn_sinks, w_out=w_out,
                   g_post_mix=g_post_mix, g_pre_ffn=g_pre_ffn, w_ff1=w_ff1, w_ff2=w_ff2, g_post_ffn=g_post_ffn)
    mom1 = dict(zip(_WEIGHTS, [m_meta_tokens, m_g_pre_mix, m_w_in, m_conv_w, m_conv_b, m_w_a, m_b_a, m_w_x, m_b_x,
                               m_lru_lambda, m_attn_sinks, m_w_out, m_g_post_mix, m_g_pre_ffn, m_w_ff1, m_w_ff2,
                               m_g_post_ffn]))
    mom2 = dict(zip(_WEIGHTS, [v_meta_tokens, v_g_pre_mix, v_w_in, v_conv_w, v_conv_b, v_w_a, v_b_a, v_w_x, v_b_x,
                               v_lru_lambda, v_attn_sinks, v_w_out, v_g_post_mix, v_g_pre_ffn, v_w_ff1, v_w_ff2,
                               v_g_post_ffn]))
    xi, yi, ci = _mesh_pos()
    chip = 2 * xi + yi

    tiny = jnp.concatenate([meta_tokens, jnp.pad(conv_w[0], ((0, 4), (0, 128)))], axis=0)
    chip_arr = jnp.reshape(chip, (1,)).astype(jnp.int32)
    big2d = lambda a, name: a[0].T if name == 'w_in' else a[0]
    shards, lands = zip(*[_prep_shard(big2d(weights[n], n), chip_arr) for n in _BIG])
    g_in, g_tiny = _gather_weights(shards[:1], lands[:1], tiny, _prep_tiny(tiny, chip_arr))
    w_in_full = g_in.reshape(IN_WIDTH, D_MODEL)
    meta_full = jnp.concatenate([g_tiny[j, :N_META] for j in range(N_CHIPS)], axis=1)
    conv_w_full = jnp.concatenate([g_tiny[j, N_META:N_META + 4, :128] for j in range(N_CHIPS)], axis=1)
    g_send, g_recv, late_thru, late_lands, token = _split_start(
        "gather_late_start", _gather_copies, shards[1:], lands[1:])

    def late_weights(after):
        _, landed = _split_wait("gather_late_wait", _gather_copies, g_send, g_recv, late_thru, late_lands, after)
        g_out, g_f1, g_f2 = _gather_finish(landed)
        return g_out.reshape(D_MODEL, D_MODEL), g_f1, g_f2

    pos = jnp.stack([ci, chip]).astype(jnp.int32)
    ffn = {}


    def on_ffn_grads(dw1, dw2):
        parts = [dw1, dw2]
        lands = [lax.empty((p.shape[0], p.shape[1] // 2, p.shape[2]), p.dtype) for p in parts]
        ffn['sib'] = _split_start("sibling_ffn_start", _sibling_copies, parts, lands, len(parts))
        return ffn['sib'][4]

    def on_outproj_bwd(dattn):
        send, recv, thru, lands, _ = ffn['sib']
        parts, from_sibling = _split_wait("sibling_ffn_wait", _sibling_copies, send, recv, thru, lands, [dattn])
        cparts_ffn, lands_ffn = zip(*[_chip_presum(p, r, pos) for p, r in zip(parts, from_sibling)])
        ffn['send'], ffn['recv'], ffn['thru'], ffn['lands'], token3 = _split_start(
            "scatter_ffn_start", _scatter_copies, cparts_ffn, lands_ffn)
        return token3

    def on_mixer_grads(dw_in, dw_out):
        parts = [dw_in.reshape(N_CHIPS, IN_WIDTH // N_CHIPS, D_MODEL),
                 dw_out.reshape(N_CHIPS, D_MODEL // N_CHIPS, D_MODEL)]
        cparts, lands = zip(*[_chip_presum(p, r, pos) for p, r in zip(parts, _sibling_exchange(parts, pos))])
        ffn['mixer'] = _split_start("scatter_mixer_start", _scatter_copies, cparts, lands)
        return ffn['mixer'][4]

    head = jnp.concatenate([jnp.zeros((PAD_ROWS, D_MODEL), F32), meta_full], axis=0)
    loss, dx, dhead, grads = _local_step(head, x[0], loss_target[0], g_pre_mix, w_in_full, conv_w_full, conv_b, w_a[0],
                                         b_a, w_x[0], b_x, lru_lambda, attn_sinks, g_post_mix, g_pre_ffn, g_post_ffn,
                                         late_weights, on_ffn_grads, on_outproj_bwd, on_mixer_grads, token)
    grad_x = dx[None]

    pack = _pack_small(dhead[PAD_ROWS:], grads, loss)
    dev = jnp.reshape(4 * xi + 2 * yi + ci, (1,)).astype(jnp.int32)
    s_send, s_recv, s_thru, s_lands, token5 = _split_start(
        "gather_small_start", _all_peers_copies, [pack], [_prep_tiny(pack, dev, N_DEV)], N_DEV - 1)

    send, recv, thru, lands, _ = ffn['mixer']
    mixer_cparts, mixer_lands = _split_wait("scatter_mixer_wait", _scatter_copies, send, recv, thru, lands, [token5])
    ffn_cparts, ffn_lands = _split_wait("scatter_ffn_wait", _scatter_copies, ffn['send'], ffn['recv'], ffn['thru'],
                                        ffn['lands'], mixer_lands)
    chip_partials = _scatter_partials([], [], mixer_cparts + ffn_cparts, mixer_lands + ffn_lands)

    g_out_d, delta, new_m, new_v = {}, {}, {}, {}
    for name, part in zip(_BIG, chip_partials):
        shp = weights[name].shape
        res = _adamw_big(part, big2d(weights[name], name), big2d(mom1[name], name), big2d(mom2[name], name))
        g_out_d[name], delta[name], new_m[name], new_v[name] = (big2d(r[None], name).reshape(shp) for r in res)

    _, (gathered,) = _split_wait("gather_small_wait", _all_peers_copies, s_send, s_recv, s_thru, s_lands,
                                 [g_out_d[n] for n in _BIG])
    small = _unpack_small(_sum_devices(gathered.reshape(N_DEV * SMALL_PACK_ROWS, D_MODEL), SMALL_PACK_ROWS), chip)
    loss = small['loss']
    small_names = [n for n in _WEIGHTS if n not in _BIG]
    quads = [(_as2d(weights[n]), _as2d(small[n]), _as2d(mom1[n]), _as2d(mom2[n])) for n in small_names]
    for name, (d, m2, v2) in zip(small_names, _adamw_small(quads)):
        shp = weights[name].shape
        g_out_d[name] = small[name].reshape(shp)
        delta[name], new_m[name], new_v[name] = d.reshape(shp), m2.reshape(shp), v2.reshape(shp)

    return (loss, grad_x, *[g_out_d[n] for n in _WEIGHTS], *[delta[n] for n in _WEIGHTS],
            *[new_m[n] for n in _WEIGHTS], *[new_v[n] for n in _WEIGHTS])
```

```python
import numpy as np
import jax
import jax.numpy as jnp
from jax import lax
from jax.experimental import pallas as pl
from jax.experimental.pallas import tpu as pltpu

F32 = jnp.float32
BF16 = jnp.bfloat16

D_MODEL = 1024
N_META = 16
BLOCK = 128
PAD_ROWS = BLOCK - N_META
HEAD_DIM = 64
ATTN_HEADS = 8
GQA_GROUP = 4
ATTN_WIDTH = 512
KV_WIDTH = 128
QKV_WIDTH = ATTN_WIDTH + 2 * KV_WIDTH
LRU_WIDTH = 512
LRU_BLOCKS = 8
LRU_BLOCK = 64
LRU_C = 8.0
IN_WIDTH = 1792
D_FF = 4096
N_CHIPS = 4
FF_CHUNK = D_FF // N_CHIPS
EPS = 1e-6
NEG = -1e30

ADAM_LR = 0.001
ADAM_B1 = 0.9
ADAM_B2 = 0.999
ADAM_EPS = 1e-08
ADAM_WD = 0.01
ADAM_STEP = 10

VMEM_LIMIT_V7X = 62 * 1024 * 1024
MESH = pl.DeviceIdType.MESH

NT = (((1,), (1,)), ((), ()))
TN = (((0,), (0,)), ((), ()))


def _row_tile(tp):
    return 640 if tp % 640 == 0 else BLOCK


def _wgrad_row_tile(tp):
    return 1664 if tp % 1664 == 0 else _row_tile(tp)


def _params(*sem):
    return pltpu.CompilerParams(dimension_semantics=sem, vmem_limit_bytes=VMEM_LIMIT_V7X)


def _dot(a, b):
    return jnp.dot(a, b, preferred_element_type=F32)


def _dot_nt(a, b):
    return lax.dot_general(a, b, NT, preferred_element_type=F32)


def _dot_tn(a, b):
    return lax.dot_general(a, b, TN, preferred_element_type=F32)


def _rms(x):
    rs = lax.rsqrt(jnp.mean(x * x, axis=-1, keepdims=True) + EPS)
    return x * rs, rs


def _rms_bwd(xhat, rs, g, dy):
    dyg = dy * g
    dx = rs * (dyg - xhat * jnp.mean(dyg * xhat, axis=-1, keepdims=True))
    dg = jnp.sum(dy * xhat, axis=0, keepdims=True)
    return dx, dg


def _gelu(x):
    k = 0.7978845608028654
    t = jnp.tanh(x * (k + (k * 0.044715) * (x * x)))
    return (0.5 * x) * (1.0 + t), t


def _gelu_grad(x, t):
    k = 0.7978845608028654
    return 0.5 * (1.0 + t) + 0.5 * x * (1.0 - t * t) * k * (1.0 + 3 * 0.044715 * x * x)


def _sigmoid(x):
    return 0.5 * jnp.tanh(0.5 * x) + 0.5


def _one_minus_exp2(y):
    t = jnp.tanh(y)
    return (-2.0 * t) / (1.0 - t)


def _softplus(x):
    return jnp.maximum(x, 0.0) + jnp.log1p(jnp.exp(-jnp.abs(x)))


def _seq_specs(tr, delay=0):
    qb = tr // BLOCK
    tile = lambda i: jnp.maximum(i - delay, 0)
    return [pl.BlockSpec((BLOCK, D_MODEL), lambda i, *_, s=s: (jnp.maximum(tile(i) * qb + s - 1, 0), 0))
            for s in range(qb)]


def _seq_tile(head, pieces, i):
    first = jnp.where(i == 0, head, pieces[0][...])
    return jnp.concatenate([first] + [p[...] for p in pieces[1:]], axis=0)


def _inproj_fwd(head, x, g, w_in, token):
    tp = BLOCK + x.shape[0]
    tr = _row_tile(tp)
    qb = tr // BLOCK

    def body(*refs):
        head_ref, pieces = refs[0], refs[1:1 + qb]
        g_ref, w_ref, _, u_ref, qkv_ref, xr_ref, yr_ref = refs[1 + qb:]
        xhat, _ = _rms(_seq_tile(head_ref[...], pieces, pl.program_id(0)))
        u = (xhat * g_ref[...]).astype(BF16)
        u_ref[...] = u
        z = _dot_nt(u, w_ref[...])
        qkv_ref[...] = z[:, :QKV_WIDTH].astype(BF16)
        xr_ref[...] = z[:, QKV_WIDTH:QKV_WIDTH + LRU_WIDTH]
        yr_ref[...] = z[:, QKV_WIDTH + LRU_WIDTH:]

    row = lambda w: pl.BlockSpec((tr, w), lambda i: (i, 0))
    full = lambda a: pl.BlockSpec(a.shape, lambda i: (0,) * a.ndim)
    return pl.pallas_call(
        body, name="inproj_fwd", grid=(tp // tr,),
        in_specs=[full(head)] + _seq_specs(tr) + [full(g), full(w_in), full(token)],
        out_specs=[row(D_MODEL), row(QKV_WIDTH), row(LRU_WIDTH), row(LRU_WIDTH)],
        out_shape=[jax.ShapeDtypeStruct((tp, D_MODEL), BF16), jax.ShapeDtypeStruct((tp, QKV_WIDTH), BF16),
                   jax.ShapeDtypeStruct((tp, LRU_WIDTH), F32), jax.ShapeDtypeStruct((tp, LRU_WIDTH), F32)],
        compiler_params=_params("parallel"),
    )(head, *([x] * qb), g, w_in, token)


GROUP_ROWS = GQA_GROUP * BLOCK


def _attn_bias():
    j = np.arange(2 * BLOCK)[:, None]
    i = np.arange(BLOCK)[None, :]
    band = (j - i >= 1) & (j - i <= BLOCK)
    out = []
    for n in range(3):
        ok = band & ((n - 1) * BLOCK + j >= PAD_ROWS) if n < 2 else band
        out.append(np.tile(np.where(ok, 0.0, NEG).astype(np.float32), (1, GQA_GROUP)))
    return jnp.asarray(np.stack(out))


def _stack_heads(a, g):
    heads = range(GQA_GROUP * g, GQA_GROUP * (g + 1))
    return jnp.concatenate([a[:, h * HEAD_DIM:(h + 1) * HEAD_DIM] for h in heads], axis=0)


def _unstack_heads(groups):
    return jnp.concatenate([p[h * BLOCK:(h + 1) * BLOCK] for p in groups for h in range(GQA_GROUP)], axis=1)


def _attn_probs_t(k_g, qg, bias, sink_row):
    st = _dot_nt(k_g, qg) + bias
    m = jnp.maximum(jnp.max(st, axis=0, keepdims=True), sink_row)
    p = jnp.exp(st - m)
    es = jnp.exp(sink_row - m)
    inv = 1.0 / (jnp.sum(p, axis=0, keepdims=True) + es)
    return p * inv, es * inv


def _attn_consts(sinks):
    return jnp.repeat(sinks.reshape(ATTN_HEADS), BLOCK).reshape(ATTN_HEADS // GQA_GROUP, GROUP_ROWS), _attn_bias()


_SINK_SPEC = pl.BlockSpec((ATTN_HEADS // GQA_GROUP, GROUP_ROWS), lambda n: (0, 0))
_BIAS_SPEC = pl.BlockSpec((3, 2 * BLOCK, GROUP_ROWS), lambda n: (0, 0, 0))
_QSCALE = HEAD_DIM ** -0.5


def _kv_specs(tr):
    qb = tr // BLOCK
    prev = lambda col: pl.BlockSpec((BLOCK, KV_WIDTH), lambda t: (jnp.maximum(t * qb - 1, 0), col))
    cur = lambda col: pl.BlockSpec((tr, KV_WIDTH), lambda t: (t, col))
    return [prev(4), cur(4), prev(5), cur(5)]


def _block_bias(b_ref, t, qb, i):
    return b_ref[2] if i >= 2 else b_ref[jnp.minimum(t * qb + i, 2)]


N_KV = ATTN_HEADS // GQA_GROUP


def _prob_specs(qb):
    return [pl.BlockSpec((qb, N_KV, 2 * BLOCK, GROUP_ROWS), lambda t: (t, 0, 0, 0)),
            pl.BlockSpec((qb, SUBLANES, GROUP_ROWS), lambda t: (t, 0, 0))]


def _attn_fwd(qkv, sinks):
    tp = qkv.shape[0]
    tr = _row_tile(tp)
    qb, nb = tr // BLOCK, tp // BLOCK
    sink_rows, bias = _attn_consts(sinks)

    def body(s_ref, b_ref, q_ref, kp_ref, kc_ref, vp_ref, vc_ref, o_ref, p_ref, ps_ref):
        t = pl.program_id(0)
        k_all = jnp.concatenate([kp_ref[...], kc_ref[...]], axis=0)
        v_all = jnp.concatenate([vp_ref[...], vc_ref[...]], axis=0)
        for i in range(qb):
            rows = slice(i * BLOCK, (i + 1) * BLOCK)
            q = q_ref[rows]
            k2, v2 = k_all[i * BLOCK:(i + 2) * BLOCK], v_all[i * BLOCK:(i + 2) * BLOCK]
            bias_n = _block_bias(b_ref, t, qb, i)
            outs, sink_probs = [], []
            for g in range(N_KV):
                cols = slice(g * HEAD_DIM, (g + 1) * HEAD_DIM)
                qg = _stack_heads(q, g) * jnp.asarray(_QSCALE, BF16)
                p, ps = _attn_probs_t(k2[:, cols], qg, bias_n, s_ref[g:g + 1])
                pb = p.astype(BF16)
                p_ref[i, g] = pb
                sink_probs.append(ps)
                outs.append(_dot_tn(pb, v2[:, cols]))
            o_ref[rows] = _unstack_heads(outs).astype(BF16)
            ps_ref[i] = jnp.concatenate(sink_probs + [jnp.zeros((SUBLANES - N_KV, GROUP_ROWS), F32)], axis=0)

    return pl.pallas_call(
        body, name="attn_fwd", grid=(tp // tr,),
        in_specs=[_SINK_SPEC, _BIAS_SPEC, pl.BlockSpec((tr, ATTN_WIDTH), lambda t: (t, 0))] + _kv_specs(tr),
        out_specs=[pl.BlockSpec((tr, ATTN_WIDTH), lambda t: (t, 0))] + _prob_specs(qb),
        out_shape=[jax.ShapeDtypeStruct((tp, ATTN_WIDTH), BF16),
                   jax.ShapeDtypeStruct((nb, N_KV, 2 * BLOCK, GROUP_ROWS), BF16),
                   jax.ShapeDtypeStruct((nb, SUBLANES, GROUP_ROWS), F32)],
        compiler_params=_params("parallel"),
    )(sink_rows, bias, qkv, qkv, qkv, qkv, qkv)


def _conv_taps(x, halo):
    ext = jnp.concatenate([halo, x], axis=0)
    return [ext[8:] if k == 3 else pltpu.roll(ext, 3 - k, 0)[8:] for k in range(4)]


def _lru_gates(xc, wa, ba, wx, bx, sp):
    xb = xc.astype(BF16)
    r = _sigmoid(_dot(xb, wa) + ba)
    ig = _sigmoid(_dot(xb, wx) + bx)
    log_a = (-LRU_C * sp) * r
    a = jnp.exp(log_a)
    mult = jnp.sqrt(_one_minus_exp2(log_a))
    return xb, r, ig, a, mult


SUBLANES = 8


def _scan_fwd(a, b, h_in):
    n, width = a.shape
    a, b = (v.reshape(n // SUBLANES, SUBLANES, width) for v in (a, b))
    in_group = lax.broadcasted_iota(jnp.int32, a.shape, 1)
    for d in (1, 2, 4):
        keep = in_group >= d
        b = jnp.where(keep, a * pltpu.roll(b, d, 1) + b, b)
        a = jnp.where(keep, a * pltpu.roll(a, d, 1), a)
    a, b = a.reshape(n, width), b.reshape(n, width)
    out, carry = [], h_in
    for g in range(0, n, SUBLANES):
        h = a[g:g + SUBLANES] * carry + b[g:g + SUBLANES]
        out.append(h)
        carry = h[SUBLANES - 1:]
    return jnp.concatenate(out, axis=0)


def _scan_rev(c, b, g_in):
    n, width = c.shape
    c, b = (v.reshape(n // SUBLANES, SUBLANES, width) for v in (c, b))
    in_group = lax.broadcasted_iota(jnp.int32, c.shape, 1)
    for d in (1, 2, 4):
        keep = in_group < SUBLANES - d
        b = jnp.where(keep, b + c * pltpu.roll(b, SUBLANES - d, 1), b)
        c = jnp.where(keep, c * pltpu.roll(c, SUBLANES - d, 1), c)
    c, b = c.reshape(n, width), b.reshape(n, width)
    out, carry = [], g_in
    for g in range(n - SUBLANES, -1, -SUBLANES):
        r = b[g:g + SUBLANES] + c[g:g + SUBLANES] * carry
        out.append(r)
        carry = r[:1]
    return jnp.concatenate(out[::-1], axis=0)


def _lru_fwd(xr, yr, conv_w, conv_b, wa, ba, wx, bx, lam):
    tp = xr.shape[0]
    tr = _row_tile(tp)
    qb = tr // BLOCK

    def body(xr_ref, yr_ref, cw_ref, cb_ref, wa_ref, ba_ref, wx_ref, bx_ref, lam_ref, hr_ref, rec_ref, halo, hprev):
        t = pl.program_id(0)

        @pl.when(t == 0)
        def _():
            halo[...] = jnp.zeros_like(halo)
            hprev[...] = jnp.zeros_like(hprev)

        cw, cb = cw_ref[...], cb_ref[...]
        wa_m, ba_v, wx_m, bx_v = wa_ref[...], ba_ref[...], wx_ref[...], bx_ref[...]
        sp = _softplus(-lam_ref[...])
        before, h_last = halo[...], hprev[0:1]
        for i in range(qb):
            rows = slice(i * BLOCK, (i + 1) * BLOCK)
            x = xr_ref[rows]
            taps = _conv_taps(x, before)
            before = x[BLOCK - 8:]
            xc = cb + sum(cw[k:k + 1] * taps[k] for k in range(4))
            _, _, ig, a, mult = _lru_gates(xc, wa_m, ba_v, wx_m, bx_v, sp)
            u = mult * (ig * xc)
            if i == 0:
                pos = t * tr + lax.broadcasted_iota(jnp.int32, xc.shape, 0)
                u = jnp.where(pos >= PAD_ROWS, u, 0.0)
            h = _scan_fwd(a, u, h_last)
            h_last = h[BLOCK - 1:]
            hr_ref[rows] = h
            gl, _ = _gelu(yr_ref[rows])
            rec_ref[rows] = (gl * h).astype(BF16)
        halo[...] = before
        hprev[0:1] = h_last

    blk = pl.BlockSpec((tr, LRU_WIDTH), lambda t: (t, 0))
    full = lambda a: pl.BlockSpec(a.shape, lambda t: (0,) * a.ndim)
    small = [conv_w, conv_b, wa, ba, wx, bx, lam]
    return pl.pallas_call(
        body, name="lru_fwd", grid=(tp // tr,),
        in_specs=[blk, blk] + [full(a) for a in small],
        out_specs=[blk, blk],
        out_shape=[jax.ShapeDtypeStruct((tp, LRU_WIDTH), F32), jax.ShapeDtypeStruct((tp, LRU_WIDTH), BF16)],
        scratch_shapes=[pltpu.VMEM((8, LRU_WIDTH), F32), pltpu.VMEM((8, LRU_WIDTH), F32)],
        compiler_params=_params("arbitrary"),
    )(xr, yr, *small)


def _outproj_fwd(attn, rec, w_out, head, x, g_post_mix, g_pre_ffn):
    tp = attn.shape[0]
    tr = _row_tile(tp)
    qb = tr // BLOCK

    def body(*refs):
        a_ref, r_ref, w_ref, head_ref = refs[:4]
        pieces = refs[4:4 + qb]
        gm_ref, gf_ref, mix_ref, h1_ref, u1_ref = refs[4 + qb:]
        mix = _dot(a_ref[...], w_ref[:ATTN_WIDTH]) + _dot(r_ref[...], w_ref[ATTN_WIDTH:])
        mix_ref[...] = mix
        mhat, _ = _rms(mix)
        h1 = _seq_tile(head_ref[...], pieces, pl.program_id(0)) + mhat * gm_ref[...]
        h1_ref[...] = h1
        hhat, _ = _rms(h1)
        u1_ref[...] = (hhat * gf_ref[...]).astype(BF16)

    row = lambda w: pl.BlockSpec((tr, w), lambda i: (i, 0))
    full = lambda a: pl.BlockSpec(a.shape, lambda i: (0,) * a.ndim)
    return pl.pallas_call(
        body, name="outproj_fwd", grid=(tp // tr,),
        in_specs=[row(ATTN_WIDTH), row(LRU_WIDTH), full(w_out), full(head)] + _seq_specs(tr)
        + [full(g_post_mix), full(g_pre_ffn)],
        out_specs=[row(D_MODEL), row(D_MODEL), row(D_MODEL)],
        out_shape=[jax.ShapeDtypeStruct((tp, D_MODEL), F32), jax.ShapeDtypeStruct((tp, D_MODEL), F32),
                   jax.ShapeDtypeStruct((tp, D_MODEL), BF16)],
        compiler_params=_params("parallel"),
    )(attn, rec, w_out, head, *([x] * qb), g_post_mix, g_pre_ffn)


def _resident(a):
    return pl.BlockSpec(a.shape, lambda *_: (0,) * a.ndim, pipeline_mode=pl.Buffered(1))


def _ffn_fwd(u1, w1, w2, h1, tgt, g_post_ffn):
    tp = h1.shape[0]
    tr = _row_tile(tp)
    qb, nt = tr // BLOCK, tp // tr
    sr = tr // N_CHIPS

    def body(*refs):
        u_ref, w1_ref, w2_ref, h1_ref = refs[:4]
        t_pieces = refs[4:4 + qb]
        g_ref, r1_ref, dy_ref, df2_ref, loss_ref, dg_ref, acc = refs[4 + qb:]
        i, c = pl.program_id(0), pl.program_id(1)
        cur = i % 2

        @pl.when((i == 0) & (c == 0))
        def _():
            loss_ref[...] = jnp.zeros_like(loss_ref)
            dg_ref[...] = jnp.zeros_like(dg_ref)
            acc[1] = jnp.zeros((tr, D_MODEL), F32)

        def matmuls():
            r = jnp.maximum(_dot(u_ref[...], w1_ref[c]), 0.0)
            r1_ref[...] = r.astype(BF16)
            return _dot((r * r).astype(BF16), w2_ref[c])

        def finish_previous_tile(k, valid):
            lo, hi = k * sr, (k + 1) * sr
            g = g_ref[...]
            fhat, rs = _rms(acc[1 - cur, lo:hi])
            h2 = h1_ref[lo:hi] + fhat * g
            rows = (i - 1) * tr + lo + lax.broadcasted_iota(jnp.int32, h2.shape, 0)
            tgt = jnp.concatenate([p[max(lo - s * BLOCK, 0):min(hi - s * BLOCK, BLOCK)] for s, p in enumerate(t_pieces)
                                   if lo < (s + 1) * BLOCK and hi > s * BLOCK], axis=0)
            err = jnp.where((rows >= BLOCK) & valid, h2 - tgt, 0.0)
            dy = err * (1.0 / D_MODEL)
            dy_ref[lo:hi] = dy
            loss_ref[...] += (0.5 / D_MODEL) * jnp.sum(err * err)
            df2, dg = _rms_bwd(fhat, rs, g, dy)
            df2_ref[lo:hi] = df2.astype(BF16)
            dg_ref[...] += dg

        for k in range(N_CHIPS):
            @pl.when((c == k) & (i < nt))
            def _(k=k):
                finish_previous_tile(k, i >= 1)
                if k == 0:
                    acc[cur] = matmuls()
                else:
                    acc[cur] += matmuls()

            @pl.when((c == k) & (i == nt))
            def _(k=k):
                finish_previous_tile(k, True)

    last = nt - 1
    this_row = pl.BlockSpec((tr, D_MODEL), lambda i, c: (jnp.minimum(i, last), 0))
    prev_row = pl.BlockSpec((tr, D_MODEL), lambda i, c: (jnp.maximum(i - 1, 0), 0))
    full = lambda a: pl.BlockSpec(a.shape, lambda i, c: (0,) * a.ndim)
    return pl.pallas_call(
        body, name="ffn_fwd", grid=(nt + 1, N_CHIPS),
        in_specs=[this_row, _resident(w1), _resident(w2), prev_row] + _seq_specs(tr, delay=1) + [full(g_post_ffn)],
        out_specs=[pl.BlockSpec((tr, FF_CHUNK), lambda i, c: (jnp.minimum(i, last), jnp.where(i < nt, c, N_CHIPS - 1))),
                   prev_row, prev_row,
                   pl.BlockSpec((1, 1), lambda i, c: (0, 0)), pl.BlockSpec((1, D_MODEL), lambda i, c: (0, 0))],
        out_shape=[jax.ShapeDtypeStruct((tp, D_FF), BF16), jax.ShapeDtypeStruct((tp, D_MODEL), F32),
                   jax.ShapeDtypeStruct((tp, D_MODEL), BF16), jax.ShapeDtypeStruct((1, 1), F32),
                   jax.ShapeDtypeStruct((1, D_MODEL), F32)],
        scratch_shapes=[pltpu.VMEM((2, tr, D_MODEL), F32)],
        compiler_params=_params("arbitrary", "arbitrary"),
    )(u1, w1, w2, h1, *([tgt] * qb), g_post_ffn)


def _ffn_bwd_data(df2, r1, w1, w2, dy, h1, mix, g_pre_ffn, g_post_mix):
    tp = h1.shape[0]
    tr = _row_tile(tp)
    nt = tp // tr
    sr = tr // N_CHIPS

    def body(df2_ref, r1_ref, w1_ref, w2_ref, dy_ref, h1_ref, mix_ref, gf_ref, gm_ref,
             da_ref, dh1_ref, dmix_ref, dgf_ref, dgm_ref, acc):
        i, c = pl.program_id(0), pl.program_id(1)
        cur = i % 2

        @pl.when((i == 0) & (c == 0))
        def _():
            dgf_ref[...] = jnp.zeros_like(dgf_ref)
            dgm_ref[...] = jnp.zeros_like(dgm_ref)
            acc[1] = jnp.zeros((tr, D_MODEL), F32)

        def matmuls():
            df = _dot_nt(df2_ref[...], w2_ref[c])
            da = (df * (2.0 * r1_ref[...].astype(F32))).astype(BF16)
            da_ref[...] = da
            return _dot_nt(da, w1_ref[c])

        def finish_previous_tile(k, valid):
            lo, hi = k * sr, (k + 1) * sr
            hhat, rs = _rms(h1_ref[lo:hi])
            dx, dgf = _rms_bwd(hhat, rs, gf_ref[...], acc[1 - cur, lo:hi])
            dh1 = dy_ref[lo:hi] + dx
            dh1_ref[lo:hi] = dh1
            mhat, rsm = _rms(mix_ref[lo:hi])
            dmix, dgm = _rms_bwd(mhat, rsm, gm_ref[...], dh1)
            dmix_ref[lo:hi] = dmix.astype(BF16)
            dgf_ref[...] += jnp.where(valid, dgf, 0.0)
            dgm_ref[...] += jnp.where(valid, dgm, 0.0)

        for k in range(N_CHIPS):
            @pl.when((c == k) & (i < nt))
            def _(k=k):
                finish_previous_tile(k, i >= 1)
                if k == 0:
                    acc[cur] = matmuls()
                else:
                    acc[cur] += matmuls()

            @pl.when((c == k) & (i == nt))
            def _(k=k):
                finish_previous_tile(k, True)

    last = nt - 1
    this_row = pl.BlockSpec((tr, D_MODEL), lambda i, c: (jnp.minimum(i, last), 0))
    prev_row = pl.BlockSpec((tr, D_MODEL), lambda i, c: (jnp.maximum(i - 1, 0), 0))
    chunk = pl.BlockSpec((tr, FF_CHUNK), lambda i, c: (jnp.minimum(i, last), jnp.where(i < nt, c, N_CHIPS - 1)))
    gain = pl.BlockSpec((1, D_MODEL), lambda i, c: (0, 0))
    return pl.pallas_call(
        body, name="ffn_bwd_data", grid=(nt + 1, N_CHIPS),
        in_specs=[this_row, chunk, _resident(w1), _resident(w2), prev_row, prev_row, prev_row, gain, gain],
        out_specs=[chunk, prev_row, prev_row, gain, gain],
        out_shape=[jax.ShapeDtypeStruct((tp, D_FF), BF16), jax.ShapeDtypeStruct((tp, D_MODEL), F32),
                   jax.ShapeDtypeStruct((tp, D_MODEL), BF16), jax.ShapeDtypeStruct((1, D_MODEL), F32),
                   jax.ShapeDtypeStruct((1, D_MODEL), F32)],
        scratch_shapes=[pltpu.VMEM((2, tr, D_MODEL), F32)],
        compiler_params=_params("arbitrary", "arbitrary"),
    )(df2, r1, w1, w2, dy, h1, mix, g_pre_ffn, g_post_mix)


def _ffn_bwd_weights(u1, da1, r1, df2):
    tp = u1.shape[0]
    tr = _wgrad_row_tile(tp)

    def body(u_ref, da_ref, r1_ref, df2_ref, dw1_ref, dw2_ref):
        i = pl.program_id(1)
        r = r1_ref[...].astype(F32)
        p1 = _dot_tn(u_ref[...], da_ref[...])
        p2 = _dot_tn((r * r).astype(BF16), df2_ref[...])

        @pl.when(i == 0)
        def _():
            dw1_ref[0] = p1
            dw2_ref[0] = p2

        @pl.when(i > 0)
        def _():
            dw1_ref[0] += p1
            dw2_ref[0] += p2

    row = pl.BlockSpec((tr, D_MODEL), lambda c, i: (i, 0))
    chunk = pl.BlockSpec((tr, FF_CHUNK), lambda c, i: (i, c))
    return pl.pallas_call(
        body, name="ffn_bwd_weights", grid=(N_CHIPS, tp // tr),
        in_specs=[row, chunk, chunk, row],
        out_specs=[pl.BlockSpec((1, D_MODEL, FF_CHUNK), lambda c, i: (c, 0, 0)),
                   pl.BlockSpec((1, FF_CHUNK, D_MODEL), lambda c, i: (c, 0, 0))],
        out_shape=[jax.ShapeDtypeStruct((N_CHIPS, D_MODEL, FF_CHUNK), F32),
                   jax.ShapeDtypeStruct((N_CHIPS, FF_CHUNK, D_MODEL), F32)],
        compiler_params=_params("parallel", "arbitrary"),
    )(u1, da1, r1, df2)


def _outproj_bwd(dmix, w_out, attn, rec, token):
    tp = dmix.shape[0]
    tr = _wgrad_row_tile(tp)

    def body(dm_ref, w_ref, a_ref, r_ref, _, da_ref, dr_ref, dw_ref):
        i = pl.program_id(0)
        dm = dm_ref[...]
        dcat = _dot_nt(dm, w_ref[...])
        da_ref[...] = dcat[:, :ATTN_WIDTH].astype(BF16)
        dr_ref[...] = dcat[:, ATTN_WIDTH:]
        pa = _dot_tn(a_ref[...], dm)
        pr = _dot_tn(r_ref[...], dm)

        @pl.when(i == 0)
        def _():
            dw_ref[:ATTN_WIDTH] = pa
            dw_ref[ATTN_WIDTH:] = pr

        @pl.when(i > 0)
        def _():
            dw_ref[:ATTN_WIDTH] += pa
            dw_ref[ATTN_WIDTH:] += pr

    row = lambda w: pl.BlockSpec((tr, w), lambda i: (i, 0))
    full = pl.BlockSpec((D_MODEL, D_MODEL), lambda i: (0, 0))
    return pl.pallas_call(
        body, name="outproj_bwd", grid=(tp // tr,),
        in_specs=[row(D_MODEL), full, row(ATTN_WIDTH), row(LRU_WIDTH), pl.BlockSpec(token.shape, lambda i: (0, 0))],
        out_specs=[row(ATTN_WIDTH), row(LRU_WIDTH), full],
        out_shape=[jax.ShapeDtypeStruct((tp, ATTN_WIDTH), BF16), jax.ShapeDtypeStruct((tp, LRU_WIDTH), F32),
                   jax.ShapeDtypeStruct((D_MODEL, D_MODEL), F32)],
        compiler_params=_params("arbitrary"),
    )(dmix, w_out, attn, rec, token)


N_VEC_ROWS = 8


def _lru_bwd(xr, yr, hr, drec, conv_w, conv_b, wa, ba, wx, bx, lam, token):
    tp = xr.shape[0]
    tr = _row_tile(tp)
    qb, nt = tr // BLOCK, tp // tr

    def body(xr_ref, xh_ref, yr_ref, hr_ref, hp_ref, dr_ref, cw_ref, cb_ref, wa_ref, ba_ref, wx_ref, bx_ref, lam_ref, _,
             dxr_ref, dyr_ref, dwa_ref, dwx_ref, vec_ref, g_next, a_next, dxc_next, dsp):
        s = pl.program_id(0)
        t = nt - 1 - s

        @pl.when(s == 0)
        def _():
            g_next[...] = jnp.zeros_like(g_next)
            a_next[...] = jnp.zeros_like(a_next)
            dxc_next[...] = jnp.zeros_like(dxc_next)
            dsp[...] = jnp.zeros_like(dsp)
            dwa_ref[...] = jnp.zeros_like(dwa_ref)
            dwx_ref[...] = jnp.zeros_like(dwx_ref)
            vec_ref[...] = jnp.zeros_like(vec_ref)

        first_tile = t == 0
        cw, cb = cw_ref[...], cb_ref[...]
        lam_v = lam_ref[...]
        sp = _softplus(-lam_v)
        wa_m, ba_v, wx_m, bx_v = wa_ref[...], ba_ref[...], wx_ref[...], bx_ref[...]
        rows = lax.broadcasted_iota(jnp.int32, (BLOCK, LRU_WIDTH), 0)
        col = lambda v: jnp.sum(v, axis=0, keepdims=True)

        g_after, a_after, dxc_after = g_next[0:1], a_next[0:1], dxc_next[...]
        xbs, dgrs, dgis = [], [], []
        vec = [jnp.zeros((1, LRU_WIDTH), F32) for _ in range(N_VEC_ROWS)]
        for i in reversed(range(qb)):
            blk = slice(i * BLOCK, (i + 1) * BLOCK)
            if i == 0:
                x_before = jnp.where(first_tile, 0.0, xh_ref[...])
                h_before = jnp.where(first_tile, 0.0, hp_ref[7:8])
            else:
                x_before = xr_ref[i * BLOCK - 8:i * BLOCK]
                h_before = hr_ref[i * BLOCK - 1:i * BLOCK]
            taps = _conv_taps(xr_ref[blk], x_before)
            xc = cb + sum(cw[k:k + 1] * taps[k] for k in range(4))
            xb, r, ig, a, mult = _lru_gates(xc, wa_m, ba_v, wx_m, bx_v, sp)

            yr_v = yr_ref[blk]
            gl, th = _gelu(yr_v)
            h = hr_ref[blk]
            drec = dr_ref[blk]
            dyr_ref[blk] = (drec * h * _gelu_grad(yr_v, th)).astype(BF16)

            a_up = jnp.where(rows == BLOCK - 1, a_after, pltpu.roll(a, BLOCK - 1, 0))
            g = _scan_rev(a_up, drec * gl, g_after)
            g_after, a_after = g[0:1], a[0:1]

            h_prev = jnp.where(rows == 0, h_before, pltpu.roll(h, 1, 0))
            du, da = g, g * h_prev
            if i == 0:
                real = (t * tr + rows) >= PAD_ROWS
                du, da = jnp.where(real, du, 0.0), jnp.where(real, da, 0.0)
            dmult = du * (ig * xc)
            dig = du * (mult * xc)
            dxc = du * (mult * ig)
            dlog_a = da * a - dmult * (a * a / mult)
            if i == 0:
                dlog_a = jnp.where(real, dlog_a, 0.0)
            dgr = (dlog_a * (-LRU_C * sp)) * (r * (1.0 - r))
            dgi = dig * (ig * (1.0 - ig))
            dgr_b, dgi_b = dgr.astype(BF16), dgi.astype(BF16)
            dxc = dxc + _dot_nt(dgr_b, wa_m) + _dot_nt(dgi_b, wx_m)
            xbs.append(xb)
            dgrs.append(dgr_b)
            dgis.append(dgi_b)

            ext = jnp.concatenate([dxc, dxc_after], axis=0)
            up = [ext[:BLOCK] if j == 0 else pltpu.roll(ext, BLOCK + 8 - j, 0)[:BLOCK] for j in range(4)]
            dxr_ref[blk] = sum(cw[k:k + 1] * up[3 - k] for k in range(4)).astype(BF16)
            dxc_after = dxc[:8]

            for k in range(4):
                vec[k] = vec[k] + col(dxc * taps[k])
            vec[4] = vec[4] + col(dxc)
            vec[5] = vec[5] + col(dgr)
            vec[6] = vec[6] + col(dgi)
            vec[7] = vec[7] + col(dlog_a * (-LRU_C * r))

        g_next[0:1], a_next[0:1], dxc_next[...] = g_after, a_after, dxc_after
        xb_all = jnp.concatenate(xbs, axis=0)
        dwa_ref[...] += _dot_tn(xb_all, jnp.concatenate(dgrs, axis=0))
        dwx_ref[...] += _dot_tn(xb_all, jnp.concatenate(dgis, axis=0))
        for k in range(7):
            vec_ref[k:k + 1] += vec[k]
        dsp[0:1] += vec[7]

        @pl.when(s == nt - 1)
        def _():
            vec_ref[7:8] = dsp[0:1] * (-_sigmoid(-lam_v))

    blk_spec = pl.BlockSpec((tr, LRU_WIDTH), lambda s: (nt - 1 - s, 0))
    rows_before = pl.BlockSpec((8, LRU_WIDTH), lambda s: (jnp.maximum((nt - 1 - s) * (tr // 8) - 1, 0), 0))
    full = lambda a: pl.BlockSpec(a.shape, lambda s: (0,) * a.ndim)
    small = [conv_w, conv_b, wa, ba, wx, bx, lam, token]
    sq = pl.BlockSpec((LRU_WIDTH, LRU_WIDTH), lambda s: (0, 0))
    return pl.pallas_call(
        body, name="lru_bwd", grid=(nt,),
        in_specs=[blk_spec, rows_before, blk_spec, blk_spec, rows_before, blk_spec] + [full(a) for a in small],
        out_specs=[blk_spec, blk_spec, sq, sq, pl.BlockSpec((N_VEC_ROWS, LRU_WIDTH), lambda s: (0, 0))],
        out_shape=[jax.ShapeDtypeStruct((tp, LRU_WIDTH), BF16), jax.ShapeDtypeStruct((tp, LRU_WIDTH), BF16),
                   jax.ShapeDtypeStruct((LRU_WIDTH, LRU_WIDTH), F32), jax.ShapeDtypeStruct((LRU_WIDTH, LRU_WIDTH), F32),
                   jax.ShapeDtypeStruct((N_VEC_ROWS, LRU_WIDTH), F32)],
        scratch_shapes=[pltpu.VMEM((8, LRU_WIDTH), F32)] * 4,
        compiler_params=_params("arbitrary"),
    )(xr, xr, yr, hr, hr, drec, *small)


def _attn_bwd(qkv, dattn, probs, sink_probs):
    tp = qkv.shape[0]
    tr = _row_tile(tp)
    qb, nt = tr // BLOCK, tp // tr
    n_groups = N_KV

    def body(p_ref, ps_ref, q_ref, kp_ref, kc_ref, vp_ref, vc_ref, do_ref, dq_ref, dkv_ref, ex_ref, ds_ref, dsink):
        t = pl.program_id(0)

        @pl.when(t == 0)
        def _():
            dsink[...] = jnp.zeros_like(dsink)

        k_all = jnp.concatenate([kp_ref[...], kc_ref[...]], axis=0)
        v_all = jnp.concatenate([vp_ref[...], vc_ref[...]], axis=0)
        tail = None
        for i in range(qb):
            rows = slice(i * BLOCK, (i + 1) * BLOCK)
            q, do = q_ref[rows], do_ref[rows]
            k2, v2 = k_all[i * BLOCK:(i + 2) * BLOCK], v_all[i * BLOCK:(i + 2) * BLOCK]
            dqs, dks, dvs = [], [], []
            for g in range(n_groups):
                cols = slice(g * HEAD_DIM, (g + 1) * HEAD_DIM)
                k_g, v_g = k2[:, cols], v2[:, cols]
                qg = _stack_heads(q, g) * jnp.asarray(_QSCALE, BF16)
                dog = _stack_heads(do, g)
                pb = p_ref[i, g]
                p = pb.astype(F32)
                dpt = _dot_nt(v_g, dog)
                delta = jnp.sum(p * dpt, axis=0, keepdims=True)
                dst = (p * (dpt - delta)).astype(BF16)
                dqs.append(_dot_tn(dst, k_g) * _QSCALE)
                dks.append(_dot(dst, qg))
                dvs.append(_dot(pb, dog))
                dsink[g:g + 1] -= ps_ref[i, g:g + 1] * delta
            dq_ref[rows] = _unstack_heads(dqs).astype(BF16)
            dkv = jnp.concatenate(dks + dvs, axis=1)
            if i == 0:
                ex_ref[0] = dkv[:BLOCK]
            else:
                dkv_ref[(i - 1) * BLOCK:i * BLOCK] = (tail + dkv[:BLOCK]).astype(BF16)
            tail = dkv[BLOCK:]
        dkv_ref[(qb - 1) * BLOCK:] = tail.astype(BF16)

        @pl.when(t == nt - 1)
        def _():
            lane = lax.broadcasted_iota(jnp.int32, (1, ATTN_HEADS), 1)
            acc = jnp.zeros((1, ATTN_HEADS), F32)
            for h in range(ATTN_HEADS):
                g, hh = divmod(h, GQA_GROUP)
                acc = acc + jnp.where(lane == h, jnp.sum(dsink[g:g + 1, hh * BLOCK:(hh + 1) * BLOCK]), 0.0)
            ds_ref[...] = acc

    cur = lambda w: pl.BlockSpec((tr, w), lambda t: (t, 0))
    return pl.pallas_call(
        body, name="attn_bwd", grid=(nt,),
        in_specs=_prob_specs(qb) + [cur(ATTN_WIDTH)] + _kv_specs(tr) + [cur(ATTN_WIDTH)],
        out_specs=[cur(ATTN_WIDTH), cur(2 * KV_WIDTH), pl.BlockSpec((1, BLOCK, 2 * KV_WIDTH), lambda t: (t, 0, 0)),
                   pl.BlockSpec((1, ATTN_HEADS), lambda t: (0, 0))],
        out_shape=[jax.ShapeDtypeStruct((tp, ATTN_WIDTH), BF16), jax.ShapeDtypeStruct((tp, 2 * KV_WIDTH), BF16),
                   jax.ShapeDtypeStruct((nt, BLOCK, 2 * KV_WIDTH), F32), jax.ShapeDtypeStruct((1, ATTN_HEADS), F32)],
        scratch_shapes=[pltpu.VMEM((n_groups, GROUP_ROWS), F32)],
        compiler_params=_params("arbitrary"),
    )(probs, sink_probs, qkv, qkv, qkv, qkv, qkv, dattn)


def _fix_dkv(dkv, dkv_extra):
    tp = dkv.shape[0]
    tr = _row_tile(tp)
    nt, qb = tp // tr, tr // BLOCK
    if nt == 1:
        return dkv

    def body(d_ref, ex_ref, o_ref):
        o_ref[...] = (d_ref[...].astype(F32) + ex_ref[0]).astype(BF16)

    last = pl.BlockSpec((BLOCK, 2 * KV_WIDTH), lambda t: (t * qb + qb - 1, 0))
    return pl.pallas_call(
        body, name="fix_dkv", grid=(nt - 1,),
        in_specs=[last, pl.BlockSpec((1, BLOCK, 2 * KV_WIDTH), lambda t: (t + 1, 0, 0))],
        out_specs=last, out_shape=jax.ShapeDtypeStruct(dkv.shape, dkv.dtype),
        input_output_aliases={0: 0}, compiler_params=_params("parallel"),
    )(dkv, dkv_extra)


def _inproj_wgrad(dq, dkv, dxr, dyr, u0):
    tp = dq.shape[0]
    tr = _wgrad_row_tile(tp)

    def body(dq_ref, dkv_ref, dxr_ref, dyr_ref, u_ref, dw_ref):
        i = pl.program_id(0)
        dz = jnp.concatenate([dq_ref[...], dkv_ref[...], dxr_ref[...], dyr_ref[...]], axis=1)
        pw = _dot_tn(dz, u_ref[...])

        @pl.when(i == 0)
        def _():
            dw_ref[...] = pw

        @pl.when(i > 0)
        def _():
            dw_ref[...] += pw

    row = lambda w: pl.BlockSpec((tr, w), lambda i: (i, 0))
    return pl.pallas_call(
        body, name="inproj_wgrad", grid=(tp // tr,),
        in_specs=[row(ATTN_WIDTH), row(2 * KV_WIDTH), row(LRU_WIDTH), row(LRU_WIDTH), row(D_MODEL)],
        out_specs=pl.BlockSpec((IN_WIDTH, D_MODEL), lambda i: (0, 0)),
        out_shape=jax.ShapeDtypeStruct((IN_WIDTH, D_MODEL), F32),
        compiler_params=_params("arbitrary"),
    )(dq, dkv, dxr, dyr, u0)


def _inproj_dgrad(dq, dkv, dxr, dyr, w_in, head, x, dh1, g, token):
    tp = dq.shape[0]
    tr = _row_tile(tp)
    nt, qb = tp // tr, tr // BLOCK

    def body(*refs):
        dq_ref, dkv_ref, dxr_ref, dyr_ref, w_ref, head_ref = refs[:6]
        pieces = refs[6:6 + qb]
        dh1_ref, g_ref, _, gx_ref, dhead_ref, dg_ref, buf, sems = refs[6 + qb:]
        i = pl.program_id(0)
        slot = i % 2

        def out_copy(step, at):
            return pltpu.make_async_copy(buf.at[at], gx_ref.at[pl.ds(step * tr - BLOCK, tr)], sems.at[at])

        dz = jnp.concatenate([dq_ref[...], dkv_ref[...], dxr_ref[...], dyr_ref[...]], axis=1)
        du = _dot(dz, w_ref[...])
        hhat, rs = _rms(_seq_tile(head_ref[...], pieces, i))
        dx, dg = _rms_bwd(hhat, rs, g_ref[...], du)
        dh0 = dh1_ref[...] + dx

        @pl.when(i >= 3)
        def _():
            out_copy(i - 2, slot).wait()

        buf[slot] = dh0

        @pl.when(i == 0)
        def _():
            dg_ref[...] = dg
            dhead_ref[...] = dh0[:BLOCK]
            if tr > BLOCK:
                first = pltpu.make_async_copy(buf.at[0, pl.ds(BLOCK, tr - BLOCK)], gx_ref.at[pl.ds(0, tr - BLOCK)],
                                              sems.at[0])
                first.start()
                first.wait()

        @pl.when(i >= 1)
        def _():
            dg_ref[...] += dg
            out_copy(i, slot).start()

        @pl.when(i == nt - 1)
        def _():
            if nt >= 3:
                out_copy(nt - 2, (nt - 2) % 2).wait()
            if nt >= 2:
                out_copy(nt - 1, (nt - 1) % 2).wait()

    row = lambda w: pl.BlockSpec((tr, w), lambda i: (i, 0))
    full = lambda shape: pl.BlockSpec(shape, lambda i: (0,) * len(shape))
    return pl.pallas_call(
        body, name="inproj_dgrad", grid=(tp // tr,),
        in_specs=[row(ATTN_WIDTH), row(2 * KV_WIDTH), row(LRU_WIDTH), row(LRU_WIDTH), full(w_in.shape),
                  full(head.shape)] + _seq_specs(tr) + [row(D_MODEL), full(g.shape), full(token.shape)],
        out_specs=[pl.BlockSpec(memory_space=pl.ANY), full((BLOCK, D_MODEL)), full((1, D_MODEL))],
        out_shape=[jax.ShapeDtypeStruct(x.shape, F32), jax.ShapeDtypeStruct((BLOCK, D_MODEL), F32),
                   jax.ShapeDtypeStruct((1, D_MODEL), F32)],
        scratch_shapes=[pltpu.VMEM((2, tr, D_MODEL), F32), pltpu.SemaphoreType.DMA((2,))],
        compiler_params=_params("arbitrary"),
    )(dq, dkv, dxr, dyr, w_in, head, *([x] * qb), dh1, g, token)


def _dense_block_diag(w):
    eye = jnp.eye(LRU_BLOCKS, dtype=w.dtype)
    return (w[:, :, None, :] * eye[:, None, :, None]).reshape(LRU_WIDTH, LRU_WIDTH)


def _diag_blocks(dense):
    d4 = dense.reshape(LRU_BLOCKS, LRU_BLOCK, LRU_BLOCKS, LRU_BLOCK)
    return jnp.stack([d4[n, :, n, :] for n in range(LRU_BLOCKS)])


def _local_step(head, x, tgt, g_pre_mix, w_in, conv_w, conv_b, w_a, b_a, w_x, b_x, lam, sinks, g_post_mix,
                g_pre_ffn, g_post_ffn, late_weights, on_ffn_grads, on_outproj_bwd, on_mixer_grads, token):
    wa = _dense_block_diag(w_a).astype(BF16)
    wx = _dense_block_diag(w_x).astype(BF16)

    u0, qkv, xr, yr = _inproj_fwd(head, x, g_pre_mix, w_in, token)
    attn, probs, sink_probs = _attn_fwd(qkv, sinks)
    hr, rec = _lru_fwd(xr, yr, conv_w, conv_b, wa, b_a, wx, b_x, lam)
    w_out, w1, w2 = late_weights([attn, rec])
    mix, h1, u1 = _outproj_fwd(attn, rec, w_out, head, x, g_post_mix, g_pre_ffn)
    r1, dy, df2, loss, dg_post_ffn = _ffn_fwd(u1, w1, w2, h1, tgt, g_post_ffn)

    da1, dh1, dmix, dg_pre_ffn, dg_post_mix = _ffn_bwd_data(df2, r1, w1, w2, dy, h1, mix, g_pre_ffn, g_post_mix)
    dw1, dw2 = _ffn_bwd_weights(u1, da1, r1, df2)
    token2 = on_ffn_grads(dw1, dw2)
    dattn, drec, dw_out = _outproj_bwd(dmix, w_out, attn, rec, token2)
    token3 = on_outproj_bwd(dattn)
    dxr, dyr, dwa, dwx, vec = _lru_bwd(xr, yr, hr, drec, conv_w, conv_b, wa, b_a, wx, b_x, lam, token3)
    dq, dkv, dkv_extra, dsinks = _attn_bwd(qkv, dattn, probs, sink_probs)
    dkv = _fix_dkv(dkv, dkv_extra)
    dw_in = _inproj_wgrad(dq, dkv, dxr, dyr, u0)
    token4 = on_mixer_grads(dw_in, dw_out)
    dx, dhead, dg_pre_mix = _inproj_dgrad(dq, dkv, dxr, dyr, w_in, head, x, dh1, g_pre_mix, token4)

    grads = dict(
        g_pre_mix=dg_pre_mix, conv_w=vec[0:4], conv_b=vec[4:5], w_a=_diag_blocks(dwa), b_a=vec[5:6],
        w_x=_diag_blocks(dwx), b_x=vec[6:7], lru_lambda=vec[7:8], attn_sinks=dsinks,
        g_post_mix=dg_post_mix, g_pre_ffn=dg_pre_ffn, g_post_ffn=dg_post_ffn)
    return loss, dx, dhead, grads


HBM = pl.BlockSpec(memory_space=pltpu.HBM)


def _mesh_pos():
    return lax.axis_index("x"), lax.axis_index("y"), lax.axis_index("c")


def _other_chips(x, y):
    return [(1 - x, y), (x, 1 - y), (1 - x, 1 - y)]


def _remote(src, dst, send_sem, recv_sem, to):
    return pltpu.make_async_remote_copy(src_ref=src, dst_ref=dst, send_sem=send_sem, recv_sem=recv_sem,
                                        device_id=to, device_id_type=MESH)


def _gather_weights(shards, lands, tiny, tiny_land):
    nbig = len(shards)

    def body(*refs):
        srcs, tiny_src = refs[:nbig], refs[nbig]
        outs, tiny_out = refs[2 * nbig + 2:3 * nbig + 2], refs[3 * nbig + 2]
        ici_send, ici_recv, d2d_send, d2d_recv, tiny_send, tiny_recv = refs[3 * nbig + 3:]
        x, y, c = _mesh_pos()
        me = 2 * x + y
        chips = _other_chips(x, y)
        sibling = (x, y, 1 - c)
        sends = []
        for w, (src, out) in enumerate(zip(srcs, outs)):
            hr = src.shape[0] // 2
            for j, chip in enumerate(chips):
                k = 3 * w + j
                cp = _remote(src.at[pl.ds(c * hr, hr)], out.at[me, pl.ds(c * hr, hr)],
                             ici_send.at[k], ici_recv.at[k], (*chip, c))
                cp.start()
                sends.append(cp)
        for j, chip in enumerate(chips):
            cp = _remote(tiny_src, tiny_out.at[me], tiny_send.at[j], tiny_recv.at[j], (*chip, c))
            cp.start()
            sends.append(cp)
        for w, (src, out) in enumerate(zip(srcs, outs)):
            hr = src.shape[0] // 2
            for j, (px, py) in enumerate(chips):
                k = 3 * w + j
                landed = out.at[2 * px + py, pl.ds(c * hr, hr)]
                _remote(landed, landed, ici_send.at[k], ici_recv.at[k], sibling).wait_recv()
                cp = _remote(landed, landed, d2d_send.at[k], d2d_recv.at[k], sibling)
                cp.start()
                sends.append(cp)
        for w, (src, out) in enumerate(zip(srcs, outs)):
            hr = src.shape[0] // 2
            for j, (px, py) in enumerate(chips):
                k = 3 * w + j
                other = out.at[2 * px + py, pl.ds((1 - c) * hr, hr)]
                _remote(other, other, d2d_send.at[k], d2d_recv.at[k], sibling).wait_recv()
        for j, (px, py) in enumerate(chips):
            blk = tiny_out.at[2 * px + py]
            _remote(blk, blk, tiny_send.at[j], tiny_recv.at[j], sibling).wait_recv()
        for cp in sends:
            cp.wait_send()

    out_shape = [jax.ShapeDtypeStruct(l.shape, l.dtype) for l in list(lands) + [tiny_land]]
    n = 3 * nbig
    return pl.pallas_call(
        body, name="gather_weights", out_shape=out_shape,
        in_specs=[HBM] * (2 * nbig + 2), out_specs=[HBM] * (nbig + 1),
        input_output_aliases={nbig + 1 + i: i for i in range(nbig + 1)},
        scratch_shapes=[pltpu.SemaphoreType.DMA((n,)),
                        pltpu.SemaphoreType.DMA((n,)), pltpu.SemaphoreType.DMA((n,)), pltpu.SemaphoreType.DMA((n,)),
                        pltpu.SemaphoreType.DMA((3,)), pltpu.SemaphoreType.DMA((3,))],
    )(*shards, tiny, *lands, tiny_land)


def _prep_shard(w, me):
    rows, cols = w.shape
    tr = 256 if rows % 256 == 0 else rows

    def body(me_ref, w_ref, s_ref, l_ref):
        b = w_ref[...].astype(BF16)
        s_ref[...] = b
        l_ref[0] = b

    return pl.pallas_call(
        body, name="prep_shard",
        grid_spec=pltpu.PrefetchScalarGridSpec(
            num_scalar_prefetch=1, grid=(rows // tr,),
            in_specs=[pl.BlockSpec((tr, cols), lambda i, me_ref: (i, 0))],
            out_specs=[pl.BlockSpec((tr, cols), lambda i, me_ref: (i, 0)),
                       pl.BlockSpec((1, tr, cols), lambda i, me_ref: (me_ref[0], i, 0))]),
        out_shape=[jax.ShapeDtypeStruct((rows, cols), BF16), jax.ShapeDtypeStruct((N_CHIPS, rows, cols), BF16)],
        compiler_params=_params("parallel"),
    )(me, w)


def _prep_tiny(tiny, me, slots=N_CHIPS):
    def body(me_ref, t_ref, l_ref):
        l_ref[0] = t_ref[...]

    return pl.pallas_call(
        body, name="prep_tiny",
        grid_spec=pltpu.PrefetchScalarGridSpec(
            num_scalar_prefetch=1, grid=(1,),
            in_specs=[pl.BlockSpec(tiny.shape, lambda i, me_ref: (0, 0))],
            out_specs=pl.BlockSpec((1,) + tiny.shape, lambda i, me_ref: (me_ref[0], 0, 0))),
        out_shape=jax.ShapeDtypeStruct((slots,) + tiny.shape, tiny.dtype),
    )(me, tiny)


N_DEV = 8


def _sibling_exchange(parts, token):
    def body(*refs):
        n = len(parts)
        srcs, outs, send_sems, recv_sems = refs[:n], refs[n + 1:2 * n + 1], refs[2 * n + 1], refs[2 * n + 2]
        x, y, c = _mesh_pos()
        sibling = (x, y, 1 - c)
        cps = []
        for w, (src, out) in enumerate(zip(srcs, outs)):
            hr = src.shape[1] // 2
            cp = _remote(src.at[:, pl.ds((1 - c) * hr, hr)], out, send_sems.at[w], recv_sems.at[w], sibling)
            cp.start()
            cps.append(cp)
        for cp in cps:
            cp.wait()

    n = len(parts)
    return pl.pallas_call(
        body, name="sibling_exchange",
        out_shape=[jax.ShapeDtypeStruct((p.shape[0], p.shape[1] // 2, p.shape[2]), p.dtype) for p in parts],
        in_specs=[HBM] * n + [pl.BlockSpec(memory_space=pl.ANY)], out_specs=[HBM] * n,
        scratch_shapes=[pltpu.SemaphoreType.DMA((n,)), pltpu.SemaphoreType.DMA((n,))],
    )(*parts, token)


def _chip_presum(part, from_sibling, pos):
    _, hr, cols = from_sibling.shape
    tr = 256 if hr % 256 == 0 else hr
    steps = hr // tr

    def body(pos_ref, a_ref, b_ref, o_ref, land_ref):
        s = (a_ref[...] + b_ref[...]).astype(BF16)
        o_ref[...] = s

        @pl.when(pl.program_id(1) == pos_ref[1])
        def _():
            land_ref[...] = s

    return pl.pallas_call(
        body, name="chip_presum",
        grid_spec=pltpu.PrefetchScalarGridSpec(
            num_scalar_prefetch=1, grid=(steps, N_CHIPS),
            in_specs=[pl.BlockSpec((1, tr, cols), lambda i, j, p: (j, p[0] * steps + i, 0)),
                      pl.BlockSpec((1, tr, cols), lambda i, j, p: (j, i, 0))],
            out_specs=[pl.BlockSpec((1, tr, cols), lambda i, j, p: (j, i, 0)),
                       pl.BlockSpec((1, tr, cols), lambda i, j, p: (p[1], p[0] * steps + i, 0))]),
        out_shape=[jax.ShapeDtypeStruct(from_sibling.shape, BF16),
                   jax.ShapeDtypeStruct((N_CHIPS, 2 * hr, cols), BF16)],
        compiler_params=_params("arbitrary", "arbitrary"),
    )(pos, part, from_sibling)


def _scatter_partials(cparts, lands, done_cparts=(), done_lands=()):
    n_new = len(cparts)
    nw = n_new + len(done_cparts)

    def body(*refs):
        srcs = refs[:nw]
        outs = refs[2 * nw:3 * nw]
        own_send, own_recv, ici_send, ici_recv, d2d_send, d2d_recv = refs[3 * nw:]
        x, y, c = _mesh_pos()
        me = 2 * x + y
        chips = _other_chips(x, y)
        sibling = (x, y, 1 - c)
        sends = []
        for w in list(range(n_new, nw)) + list(range(n_new)):
            src, out = srcs[w], outs[w]
            hr = src.shape[1]
            mine = out.at[me, pl.ds(c * hr, hr)]
            cp = _remote(src.at[me], mine, own_send.at[w], own_recv.at[w], sibling)
            cp.start()
            sends.append(cp)
            for j, (px, py) in enumerate(chips):
                if w >= n_new:
                    break
                k = 3 * w + j
                cp = _remote(src.at[2 * px + py], mine, ici_send.at[k], ici_recv.at[k], (px, py, c))
                cp.start()
                sends.append(cp)
        for w in list(range(n_new, nw)) + list(range(n_new)):
            src, out = srcs[w], outs[w]
            hr = src.shape[1]
            for j, (px, py) in enumerate(chips):
                k = 3 * w + j
                landed = out.at[2 * px + py, pl.ds(c * hr, hr)]
                if w < n_new:
                    _remote(landed, landed, ici_send.at[k], ici_recv.at[k], sibling).wait_recv()
                cp = _remote(landed, landed, d2d_send.at[k], d2d_recv.at[k], sibling)
                cp.start()
                sends.append(cp)
        for w, (src, out) in enumerate(zip(srcs, outs)):
            hr = src.shape[1]
            other = out.at[me, pl.ds((1 - c) * hr, hr)]
            _remote(other, other, own_send.at[w], own_recv.at[w], sibling).wait_recv()
            for j, (px, py) in enumerate(chips):
                k = 3 * w + j
                other = out.at[2 * px + py, pl.ds((1 - c) * hr, hr)]
                _remote(other, other, d2d_send.at[k], d2d_recv.at[k], sibling).wait_recv()
        for cp in sends:
            cp.wait_send()

    n = 3 * nw
    dma = pltpu.SemaphoreType.DMA
    every = list(cparts) + list(done_cparts)
    every_lands = list(lands) + list(done_lands)
    return pl.pallas_call(
        body, name="scatter_partials",
        out_shape=[jax.ShapeDtypeStruct(l.shape, l.dtype) for l in every_lands],
        in_specs=[HBM] * (2 * nw), out_specs=[HBM] * nw,
        input_output_aliases={nw + i: i for i in range(nw)},
        scratch_shapes=[dma((nw,)), dma((nw,)), dma((n,)), dma((n,)), dma((n,)), dma((n,))],
    )(*every, *every_lands)


SEM = pl.BlockSpec(memory_space=pltpu.SEMAPHORE)
SPLIT_COPY = pltpu.CompilerParams(has_side_effects=pltpu.SideEffectType.DATAFLOW_SIDE_EFFECTING)


def _hbm(a):
    return pltpu.with_memory_space_constraint(a, pltpu.HBM)


def _gather_copies(srcs, lands, send_sems, recv_sems):
    x, y, c = _mesh_pos()
    me = 2 * x + y
    sends, recvs = [], []
    for w, (src, land) in enumerate(zip(srcs, lands)):
        hr = src.shape[0] // 2
        for j, (px, py) in enumerate(_other_chips(x, y)):
            k = 3 * w + j
            sends.append(_remote(src.at[pl.ds(c * hr, hr)], land.at[me, pl.ds(c * hr, hr)],
                                 send_sems.at[k], recv_sems.at[k], (px, py, c)))
            got = land.at[2 * px + py, pl.ds(c * hr, hr)]
            recvs.append(_remote(got, got, send_sems.at[k], recv_sems.at[k], (px, py, c)))
    return sends, recvs


def _scatter_copies(srcs, lands, send_sems, recv_sems):
    x, y, c = _mesh_pos()
    me = 2 * x + y
    sends, recvs = [], []
    for w, (src, land) in enumerate(zip(srcs, lands)):
        hr = src.shape[1]
        for j, (px, py) in enumerate(_other_chips(x, y)):
            k = 3 * w + j
            sends.append(_remote(src.at[2 * px + py], land.at[me, pl.ds(c * hr, hr)],
                                 send_sems.at[k], recv_sems.at[k], (px, py, c)))
            got = land.at[2 * px + py, pl.ds(c * hr, hr)]
            recvs.append(_remote(got, got, send_sems.at[k], recv_sems.at[k], (px, py, c)))
    return sends, recvs


def _sibling_copies(srcs, lands, send_sems, recv_sems):
    x, y, c = _mesh_pos()
    sibling = (x, y, 1 - c)
    sends, recvs = [], []
    for w, (src, land) in enumerate(zip(srcs, lands)):
        hr = src.shape[1] // 2
        sends.append(_remote(src.at[:, pl.ds((1 - c) * hr, hr)], land, send_sems.at[w], recv_sems.at[w], sibling))
        recvs.append(_remote(land, land, send_sems.at[w], recv_sems.at[w], sibling))
    return sends, recvs


def _all_peers_copies(srcs, lands, send_sems, recv_sems):
    x, y, c = _mesh_pos()
    (src,), (land,) = srcs, lands
    flip = lambda v, bit: 1 - v if bit else v
    sends, recvs = [], []
    for k in range(N_DEV - 1):
        px, py, pc = flip(x, (k + 1) & 4), flip(y, (k + 1) & 2), flip(c, (k + 1) & 1)
        sends.append(_remote(src, land.at[4 * x + 2 * y + c], send_sems.at[k], recv_sems.at[k], (px, py, pc)))
        got = land.at[4 * px + 2 * py + pc]
        recvs.append(_remote(got, got, send_sems.at[k], recv_sems.at[k], (px, py, pc)))
    return sends, recvs


def _split_start(name, copies_of, srcs, land_shapes, n_copies=None):
    n = len(srcs)
    k = 3 * n if n_copies is None else n_copies

    def body(*refs):
        src_refs, land_refs = refs[:n], refs[n:2 * n]
        send_sems, recv_sems = refs[2 * n], refs[2 * n + 1]
        token = refs[-1]
        sends, _ = copies_of(src_refs, land_refs, send_sems, recv_sems)
        for cp in sends:
            cp.start()
        token[...] = jnp.zeros_like(token)

    lands = [_hbm(s) for s in land_shapes]
    dma = pltpu.SemaphoreType.DMA
    res = pl.pallas_call(
        body, name=name,
        out_shape=(dma((k,)), dma((k,)), *[pltpu.HBM(s.shape, s.dtype) for s in srcs],
                   *[pltpu.HBM(s.shape, s.dtype) for s in land_shapes], jax.ShapeDtypeStruct((8, 128), F32)),
        in_specs=[HBM] * (2 * n),
        out_specs=(SEM, SEM, *([HBM] * (2 * n)), pl.BlockSpec(memory_space=pltpu.VMEM)),
        input_output_aliases={i: 2 + i for i in range(2 * n)},
        compiler_params=SPLIT_COPY,
    )(*[_hbm(s) for s in srcs], *lands)
    return res[0], res[1], list(res[2:2 + n]), list(res[2 + n:2 + 2 * n]), res[-1]


def _split_wait(name, copies_of, send_sems, recv_sems, srcs, lands, after):
    n = len(srcs)

    def body(*refs):
        src_refs, land_refs = refs[:n], refs[n:2 * n]
        sends, recvs = copies_of(src_refs, land_refs, refs[2 * n], refs[2 * n + 1])
        for cp in sends:
            cp.wait_send()
        for cp in recvs:
            cp.wait_recv()

    res = pl.pallas_call(
        body, name=name,
        out_shape=tuple(pltpu.HBM(s.shape, s.dtype) for s in list(srcs) + list(lands)),
        in_specs=[HBM] * (2 * n) + [SEM, SEM] + [pl.BlockSpec(memory_space=pl.ANY)] * len(after),
        out_specs=tuple([HBM] * (2 * n)),
        input_output_aliases={i: i for i in range(2 * n)},
        compiler_params=SPLIT_COPY,
    )(*srcs, *lands, send_sems, recv_sems, *after)
    return list(res[:n]), list(res[n:])


def _gather_finish(lands):
    n = len(lands)

    def body(*refs):
        outs = refs[n:2 * n]
        d2d_send, d2d_recv = refs[2 * n:]
        x, y, c = _mesh_pos()
        chips = _other_chips(x, y)
        sibling = (x, y, 1 - c)
        sends = []
        for w, out in enumerate(outs):
            hr = out.shape[1] // 2
            for j, (px, py) in enumerate(chips):
                landed = out.at[2 * px + py, pl.ds(c * hr, hr)]
                cp = _remote(landed, landed, d2d_send.at[3 * w + j], d2d_recv.at[3 * w + j], sibling)
                cp.start()
                sends.append(cp)
        for w, out in enumerate(outs):
            hr = out.shape[1] // 2
            for j, (px, py) in enumerate(chips):
                other = out.at[2 * px + py, pl.ds((1 - c) * hr, hr)]
                _remote(other, other, d2d_send.at[3 * w + j], d2d_recv.at[3 * w + j], sibling).wait_recv()
        for cp in sends:
            cp.wait_send()

    dma = pltpu.SemaphoreType.DMA
    return pl.pallas_call(
        body, name="gather_finish",
        out_shape=[jax.ShapeDtypeStruct(l.shape, l.dtype) for l in lands],
        in_specs=[HBM] * n, out_specs=[HBM] * n,
        input_output_aliases={i: i for i in range(n)},
        scratch_shapes=[dma((3 * n,)), dma((3 * n,))],
    )(*lands)


def _adamw(w, g, m, v):
    m = ADAM_B1 * m + (1.0 - ADAM_B1) * g
    v = ADAM_B2 * v + (1.0 - ADAM_B2) * (g * g)
    m_hat = m / (1.0 - ADAM_B1 ** ADAM_STEP)
    v_hat = v / (1.0 - ADAM_B2 ** ADAM_STEP)
    delta = -ADAM_LR * (m_hat / (jnp.sqrt(v_hat) + ADAM_EPS) + ADAM_WD * w)
    return delta, m, v


def _adamw_big(partials, w, m, v):
    rows, cols = w.shape
    tr = 256 if rows % 256 == 0 else rows

    def body(p_ref, w_ref, m_ref, v_ref, g_ref, d_ref, m2_ref, v2_ref):
        g = ((p_ref[0].astype(F32) + p_ref[1].astype(F32)) + p_ref[2].astype(F32)) + p_ref[3].astype(F32)
        g_ref[...] = g
        d_ref[...], m2_ref[...], v2_ref[...] = _adamw(w_ref[...], g, m_ref[...], v_ref[...])

    blk = pl.BlockSpec((tr, cols), lambda i: (i, 0))
    return pl.pallas_call(
        body, name="adamw_big", grid=(rows // tr,),
        in_specs=[pl.BlockSpec((N_CHIPS, tr, cols), lambda i: (0, i, 0)), blk, blk, blk],
        out_specs=[blk] * 4, out_shape=[jax.ShapeDtypeStruct((rows, cols), F32)] * 4,
        compiler_params=_params("parallel"),
    )(partials, w, m, v)


def _sum_devices(gathered, rows):
    cols = gathered.shape[1]

    def body(g_ref, o_ref):
        acc = g_ref[0:rows]
        for d in range(1, N_DEV):
            acc = acc + g_ref[d * rows:(d + 1) * rows]
        o_ref[...] = acc

    return pl.pallas_call(
        body, name="sum_devices", out_shape=jax.ShapeDtypeStruct((rows, cols), F32),
        in_specs=[pl.BlockSpec(memory_space=pltpu.VMEM)], out_specs=pl.BlockSpec(memory_space=pltpu.VMEM),
        compiler_params=pltpu.CompilerParams(vmem_limit_bytes=VMEM_LIMIT_V7X),
    )(gathered)


def _adamw_small(quads):
    n = len(quads)

    def body(*refs):
        ins, outs = refs[:4 * n], refs[4 * n:]
        for t in range(n):
            w, g, m, v = (r[...] for r in ins[4 * t:4 * t + 4])
            outs[3 * t][...], outs[3 * t + 1][...], outs[3 * t + 2][...] = _adamw(w, g, m, v)

    flat = [a for q in quads for a in q]
    vm = pl.BlockSpec(memory_space=pltpu.VMEM)
    res = pl.pallas_call(
        body, name="adamw_small",
        out_shape=[jax.ShapeDtypeStruct(q[0].shape, F32) for q in quads for _ in range(3)],
        in_specs=[vm] * (4 * n), out_specs=[vm] * (3 * n),
    )(*flat)
    return [tuple(res[3 * t:3 * t + 3]) for t in range(n)]


SMALL_PACK_ROWS = 96
_WEIGHTS = ['meta_tokens', 'g_pre_mix', 'w_in', 'conv_w', 'conv_b', 'w_a', 'b_a', 'w_x', 'b_x', 'lru_lambda',
            'attn_sinks', 'w_out', 'g_post_mix', 'g_pre_ffn', 'w_ff1', 'w_ff2', 'g_post_ffn']
_BIG = ['w_in', 'w_out', 'w_ff1', 'w_ff2']


def _pack_small(dmeta, g, loss):
    z = lambda r, c: jnp.zeros((r, c), F32)
    rows = [
        dmeta,
        g['g_pre_mix'], g['g_post_mix'], g['g_pre_ffn'], g['g_post_ffn'],
        jnp.concatenate([g['conv_w'], z(4, 512)], axis=1),
        jnp.concatenate([g['conv_b'], g['b_a']], axis=1),
        jnp.concatenate([g['b_x'], g['lru_lambda']], axis=1),
        jnp.concatenate([g['attn_sinks'], z(1, D_MODEL - ATTN_HEADS)], axis=1),
        jnp.concatenate([loss, z(1, D_MODEL - 1)], axis=1),
        z(4, D_MODEL),
        g['w_a'].reshape(32, D_MODEL), g['w_x'].reshape(32, D_MODEL),
    ]
    return jnp.concatenate(rows, axis=0)


def _unpack_small(s, chip):
    return dict(
        meta_tokens=lax.dynamic_slice(s[0:16], (0, chip * 256), (16, 256)),
        g_pre_mix=s[16:17], g_post_mix=s[17:18], g_pre_ffn=s[18:19], g_post_ffn=s[19:20],
        conv_w=lax.dynamic_slice(s[20:24], (0, chip * 128), (4, 128)).reshape(1, 4, 128),
        conv_b=s[24:25, :512], b_a=s[24:25, 512:], b_x=s[25:26, :512], lru_lambda=s[25:26, 512:],
        attn_sinks=s[26:27, :ATTN_HEADS], loss=s[27, 0],
        w_a=s[32:64].reshape(1, LRU_BLOCKS, LRU_BLOCK, LRU_BLOCK),
        w_x=s[64:96].reshape(1, LRU_BLOCKS, LRU_BLOCK, LRU_BLOCK))


def _as2d(a):
    if a.ndim == 2:
        return a
    return a.reshape(-1, a.shape[-1])


def kernel(x, meta_tokens, g_pre_mix, w_in, conv_w, conv_b, w_a, b_a, w_x, b_x, lru_lambda, attn_sinks, w_out, g_post_mix, g_pre_ffn, w_ff1, w_ff2, g_post_ffn, loss_target, m_meta_tokens, m_g_pre_mix, m_w_in, m_conv_w, m_conv_b, m_w_a, m_b_a, m_w_x, m_b_x, m_lru_lambda, m_attn_sinks, m_w_out, m_g_post_mix, m_g_pre_ffn, m_w_ff1, m_w_ff2, m_g_post_ffn, v_meta_tokens, v_g_pre_mix, v_w_in, v_conv_w, v_conv_b, v_w_a, v_b_a, v_w_x, v_b_x, v_lru_lambda, v_attn_sinks, v_w_out, v_g_post_mix, v_g_pre_ffn, v_w_ff1, v_w_ff2, v_g_post_ffn):
    weights = dict(meta_tokens=meta_tokens, g_pre_mix=g_pre_mix, w_in=w_in, conv_w=conv_w, conv_b=conv_b, w_a=w_a,
                   b_a=b_a, w_x=w_x, b_x=b_x, lru_lambda=lru_lambda, attn_sinks=attn_sinks, w_out=w_out,
                   g_post_mix=g_post_mix, g_pre_ffn=g_pre_ffn, w_ff1=w_ff1, w_ff2=w_ff2, g_post_ffn=g_post_ffn)
    mom1 = dict(zip(_WEIGHTS, [m_meta_tokens, m_g_pre_mix, m_w_in, m_conv_w, m_conv_b, m_w_a, m_b_a, m_w_x, m_b_x,
                               m_lru_lambda, m_attn_sinks, m_w_out, m_g_post_mix, m_g_pre_ffn, m_w_ff1, m_w_ff2,
                               m_g_post_ffn]))
    mom2 = dict(zip(_WEIGHTS, [v_meta_tokens, v_g_pre_mix, v_w_in, v_conv_w, v_conv_b, v_w_a, v_b_a, v_w_x, v_b_x,
                               v_lru_lambda, v_attn_sinks, v_w_out, v_g_post_mix, v_g_pre_ffn, v_w_ff1, v_w_ff2,
                               v_g_post_ffn]))
    xi, yi, ci = _mesh_pos()
    chip = 2 * xi + yi

    tiny = jnp.concatenate([meta_tokens, jnp.pad(conv_w[0], ((0, 4), (0, 128)))], axis=0)
    chip_arr = jnp.reshape(chip, (1,)).astype(jnp.int32)
    big2d = lambda a, name: a[0].T if name == 'w_in' else a[0]
    shards, lands = zip(*[_prep_shard(big2d(weights[n], n), chip_arr) for n in _BIG])
    g_in, g_tiny = _gather_weights(shards[:1], lands[:1], tiny, _prep_tiny(tiny, chip_arr))
    w_in_full = g_in.reshape(IN_WIDTH, D_MODEL)
    meta_full = jnp.concatenate([g_tiny[j, :N_META] for j in range(N_CHIPS)], axis=1)
    conv_w_full = jnp.concatenate([g_tiny[j, N_META:N_META + 4, :128] for j in range(N_CHIPS)], axis=1)
    g_send, g_recv, late_thru, late_lands, token = _split_start(
        "gather_late_start", _gather_copies, shards[1:], lands[1:])

    def late_weights(after):
        _, landed = _split_wait("gather_late_wait", _gather_copies, g_send, g_recv, late_thru, late_lands, after)
        g_out, g_f1, g_f2 = _gather_finish(landed)
        return g_out.reshape(D_MODEL, D_MODEL), g_f1, g_f2

    pos = jnp.stack([ci, chip]).astype(jnp.int32)
    ffn = {}


    def on_ffn_grads(dw1, dw2):
        parts = [dw1, dw2]
        lands = [lax.empty((p.shape[0], p.shape[1] // 2, p.shape[2]), p.dtype) for p in parts]
        ffn['sib'] = _split_start("sibling_ffn_start", _sibling_copies, parts, lands, len(parts))
        return ffn['sib'][4]

    def on_outproj_bwd(dattn):
        send, recv, thru, lands, _ = ffn['sib']
        parts, from_sibling = _split_wait("sibling_ffn_wait", _sibling_copies, send, recv, thru, lands, [dattn])
        cparts_ffn, lands_ffn = zip(*[_chip_presum(p, r, pos) for p, r in zip(parts, from_sibling)])
        ffn['send'], ffn['recv'], ffn['thru'], ffn['lands'], token3 = _split_start(
            "scatter_ffn_start", _scatter_copies, cparts_ffn, lands_ffn)
        return token3

    def on_mixer_grads(dw_in, dw_out):
        parts = [dw_in.reshape(N_CHIPS, IN_WIDTH // N_CHIPS, D_MODEL),
                 dw_out.reshape(N_CHIPS, D_MODEL // N_CHIPS, D_MODEL)]
        cparts, lands = zip(*[_chip_presum(p, r, pos) for p, r in zip(parts, _sibling_exchange(parts, pos))])
        ffn['mixer'] = _split_start("scatter_mixer_start", _scatter_copies, cparts, lands)
        return ffn['mixer'][4]

    head = jnp.concatenate([jnp.zeros((PAD_ROWS, D_MODEL), F32), meta_full], axis=0)
    loss, dx, dhead, grads = _local_step(head, x[0], loss_target[0], g_pre_mix, w_in_full, conv_w_full, conv_b, w_a[0],
                                         b_a, w_x[0], b_x, lru_lambda, attn_sinks, g_post_mix, g_pre_ffn, g_post_ffn,
                                         late_weights, on_ffn_grads, on_outproj_bwd, on_mixer_grads, token)
    grad_x = dx[None]

    pack = _pack_small(dhead[PAD_ROWS:], grads, loss)
    dev = jnp.reshape(4 * xi + 2 * yi + ci, (1,)).astype(jnp.int32)
    s_send, s_recv, s_thru, s_lands, token5 = _split_start(
        "gather_small_start", _all_peers_copies, [pack], [_prep_tiny(pack, dev, N_DEV)], N_DEV - 1)

    send, recv, thru, lands, _ = ffn['mixer']
    mixer_cparts, mixer_lands = _split_wait("scatter_mixer_wait", _scatter_copies, send, recv, thru, lands, [token5])
    ffn_cparts, ffn_lands = _split_wait("scatter_ffn_wait", _scatter_copies, ffn['send'], ffn['recv'], ffn['thru'],
                                        ffn['lands'], mixer_lands)
    chip_partials = _scatter_partials([], [], mixer_cparts + ffn_cparts, mixer_lands + ffn_lands)

    g_out_d, delta, new_m, new_v = {}, {}, {}, {}
    for name, part in zip(_BIG, chip_partials):
        shp = weights[name].shape
        res = _adamw_big(part, big2d(weights[name], name), big2d(mom1[name], name), big2d(mom2[name], name))
        g_out_d[name], delta[name], new_m[name], new_v[name] = (big2d(r[None], name).reshape(shp) for r in res)

    _, (gathered,) = _split_wait("gather_small_wait", _all_peers_copies, s_send, s_recv, s_thru, s_lands,
                                 [g_out_d[n] for n in _BIG])
    small = _unpack_small(_sum_devices(gathered.reshape(N_DEV * SMALL_PACK_ROWS, D_MODEL), SMALL_PACK_ROWS), chip)
    loss = small['loss']
    small_names = [n for n in _WEIGHTS if n not in _BIG]
    quads = [(_as2d(weights[n]), _as2d(small[n]), _as2d(mom1[n]), _as2d(mom2[n])) for n in small_names]
    for name, (d, m2, v2) in zip(small_names, _adamw_small(quads)):
        shp = weights[name].shape
        g_out_d[name] = small[name].reshape(shp)
        delta[name], new_m[name], new_v[name] = d.reshape(shp), m2.reshape(shp), v2.reshape(shp)

    return (loss, grad_x, *[g_out_d[n] for n in _WEIGHTS], *[delta[n] for n in _WEIGHTS],
            *[new_m[n] for n in _WEIGHTS], *[new_v[n] for n in _WEIGHTS])
```

```python
import numpy as np
import jax
import jax.numpy as jnp
from jax import lax
from jax.experimental import pallas as pl
from jax.experimental.pallas import tpu as pltpu

F32 = jnp.float32
BF16 = jnp.bfloat16

D_MODEL = 1024
N_META = 16
BLOCK = 128
PAD_ROWS = BLOCK - N_META
HEAD_DIM = 64
ATTN_HEADS = 8
GQA_GROUP = 4
ATTN_WIDTH = 512
KV_WIDTH = 128
QKV_WIDTH = ATTN_WIDTH + 2 * KV_WIDTH
LRU_WIDTH = 512
LRU_BLOCKS = 8
LRU_BLOCK = 64
LRU_C = 8.0
IN_WIDTH = 1792
D_FF = 4096
N_CHIPS = 4
FF_CHUNK = D_FF // N_CHIPS
EPS = 1e-6
NEG = -1e30

ADAM_LR = 0.001
ADAM_B1 = 0.9
ADAM_B2 = 0.999
ADAM_EPS = 1e-08
ADAM_WD = 0.01
ADAM_STEP = 10

VMEM_LIMIT_V7X = 62 * 1024 * 1024
MESH = pl.DeviceIdType.MESH

NT = (((1,), (1,)), ((), ()))
TN = (((0,), (0,)), ((), ()))


def _row_tile(tp):
    return 640 if tp % 640 == 0 else BLOCK


def _wgrad_row_tile(tp):
    return 1664 if tp % 1664 == 0 else _row_tile(tp)


def _params(*sem):
    return pltpu.CompilerParams(dimension_semantics=sem, vmem_limit_bytes=VMEM_LIMIT_V7X)


def _dot(a, b):
    return jnp.dot(a, b, preferred_element_type=F32)


def _dot_nt(a, b):
    return lax.dot_general(a, b, NT, preferred_element_type=F32)


def _dot_tn(a, b):
    return lax.dot_general(a, b, TN, preferred_element_type=F32)


def _rms(x):
    rs = lax.rsqrt(jnp.mean(x * x, axis=-1, keepdims=True) + EPS)
    return x * rs, rs


def _rms_bwd(xhat, rs, g, dy):
    dyg = dy * g
    dx = rs * (dyg - xhat * jnp.mean(dyg * xhat, axis=-1, keepdims=True))
    dg = jnp.sum(dy * xhat, axis=0, keepdims=True)
    return dx, dg


def _gelu(x):
    k = 0.7978845608028654
    t = jnp.tanh(x * (k + (k * 0.044715) * (x * x)))
    return (0.5 * x) * (1.0 + t), t


def _gelu_grad(x, t):
    k = 0.7978845608028654
    return 0.5 * (1.0 + t) + 0.5 * x * (1.0 - t * t) * k * (1.0 + 3 * 0.044715 * x * x)


def _sigmoid(x):
    return 0.5 * jnp.tanh(0.5 * x) + 0.5


def _one_minus_exp2(y):
    t = jnp.tanh(y)
    return (-2.0 * t) / (1.0 - t)


def _softplus(x):
    return jnp.maximum(x, 0.0) + jnp.log1p(jnp.exp(-jnp.abs(x)))


def _seq_specs(tr, delay=0):
    qb = tr // BLOCK
    tile = lambda i: jnp.maximum(i - delay, 0)
    return [pl.BlockSpec((BLOCK, D_MODEL), lambda i, *_, s=s: (jnp.maximum(tile(i) * qb + s - 1, 0), 0))
            for s in range(qb)]


def _seq_tile(head, pieces, i):
    first = jnp.where(i == 0, head, pieces[0][...])
    return jnp.concatenate([first] + [p[...] for p in pieces[1:]], axis=0)


def _inproj_fwd(head, x, g, w_in, token):
    tp = BLOCK + x.shape[0]
    tr = _row_tile(tp)
    qb = tr // BLOCK

    def body(*refs):
        head_ref, pieces = refs[0], refs[1:1 + qb]
        g_ref, w_ref, _, u_ref, qkv_ref, xr_ref, yr_ref = refs[1 + qb:]
        xhat, _ = _rms(_seq_tile(head_ref[...], pieces, pl.program_id(0)))
        u = (xhat * g_ref[...]).astype(BF16)
        u_ref[...] = u
        z = _dot_nt(u, w_ref[...])
        qkv_ref[...] = z[:, :QKV_WIDTH].astype(BF16)
        xr_ref[...] = z[:, QKV_WIDTH:QKV_WIDTH + LRU_WIDTH]
        yr_ref[...] = z[:, QKV_WIDTH + LRU_WIDTH:]

    row = lambda w: pl.BlockSpec((tr, w), lambda i: (i, 0))
    full = lambda a: pl.BlockSpec(a.shape, lambda i: (0,) * a.ndim)
    return pl.pallas_call(
        body, name="inproj_fwd", grid=(tp // tr,),
        in_specs=[full(head)] + _seq_specs(tr) + [full(g), full(w_in), full(token)],
        out_specs=[row(D_MODEL), row(QKV_WIDTH), row(LRU_WIDTH), row(LRU_WIDTH)],
        out_shape=[jax.ShapeDtypeStruct((tp, D_MODEL), BF16), jax.ShapeDtypeStruct((tp, QKV_WIDTH), BF16),
                   jax.ShapeDtypeStruct((tp, LRU_WIDTH), F32), jax.ShapeDtypeStruct((tp, LRU_WIDTH), F32)],
        compiler_params=_params("parallel"),
    )(head, *([x] * qb), g, w_in, token)


GROUP_ROWS = GQA_GROUP * BLOCK


def _attn_bias():
    j = np.arange(2 * BLOCK)[:, None]
    i = np.arange(BLOCK)[None, :]
    band = (j - i >= 1) & (j - i <= BLOCK)
    out = []
    for n in range(3):
        ok = band & ((n - 1) * BLOCK + j >= PAD_ROWS) if n < 2 else band
        out.append(np.tile(np.where(ok, 0.0, NEG).astype(np.float32), (1, GQA_GROUP)))
    return jnp.asarray(np.stack(out))


def _stack_heads(a, g):
    heads = range(GQA_GROUP * g, GQA_GROUP * (g + 1))
    return jnp.concatenate([a[:, h * HEAD_DIM:(h + 1) * HEAD_DIM] for h in heads], axis=0)


def _unstack_heads(groups):
    return jnp.concatenate([p[h * BLOCK:(h + 1) * BLOCK] for p in groups for h in range(GQA_GROUP)], axis=1)


def _attn_probs_t(k_g, qg, bias, sink_row):
    st = _dot_nt(k_g, qg) + bias
    m = jnp.maximum(jnp.max(st, axis=0, keepdims=True), sink_row)
    p = jnp.exp(st - m)
    es = jnp.exp(sink_row - m)
    inv = 1.0 / (jnp.sum(p, axis=0, keepdims=True) + es)
    return p * inv, es * inv


def _attn_consts(sinks):
    return jnp.repeat(sinks.reshape(ATTN_HEADS), BLOCK).reshape(ATTN_HEADS // GQA_GROUP, GROUP_ROWS), _attn_bias()


_SINK_SPEC = pl.BlockSpec((ATTN_HEADS // GQA_GROUP, GROUP_ROWS), lambda n: (0, 0))
_BIAS_SPEC = pl.BlockSpec((3, 2 * BLOCK, GROUP_ROWS), lambda n: (0, 0, 0))
_QSCALE = HEAD_DIM ** -0.5


def _kv_specs(tr):
    qb = tr // BLOCK
    prev = lambda col: pl.BlockSpec((BLOCK, KV_WIDTH), lambda t: (jnp.maximum(t * qb - 1, 0), col))
    cur = lambda col: pl.BlockSpec((tr, KV_WIDTH), lambda t: (t, col))
    return [prev(4), cur(4), prev(5), cur(5)]


def _block_bias(b_ref, t, qb, i):
    return b_ref[2] if i >= 2 else b_ref[jnp.minimum(t * qb + i, 2)]


N_KV = ATTN_HEADS // GQA_GROUP


def _prob_specs(qb):
    return [pl.BlockSpec((qb, N_KV, 2 * BLOCK, GROUP_ROWS), lambda t: (t, 0, 0, 0)),
            pl.BlockSpec((qb, SUBLANES, GROUP_ROWS), lambda t: (t, 0, 0))]


def _attn_fwd(qkv, sinks):
    tp = qkv.shape[0]
    tr = _row_tile(tp)
    qb, nb = tr // BLOCK, tp // BLOCK
    sink_rows, bias = _attn_consts(sinks)

    def body(s_ref, b_ref, q_ref, kp_ref, kc_ref, vp_ref, vc_ref, o_ref, p_ref, ps_ref):
        t = pl.program_id(0)
        k_all = jnp.concatenate([kp_ref[...], kc_ref[...]], axis=0)
        v_all = jnp.concatenate([vp_ref[...], vc_ref[...]], axis=0)
        for i in range(qb):
            rows = slice(i * BLOCK, (i + 1) * BLOCK)
            q = q_ref[rows]
            k2, v2 = k_all[i * BLOCK:(i + 2) * BLOCK], v_all[i * BLOCK:(i + 2) * BLOCK]
            bias_n = _block_bias(b_ref, t, qb, i)
            outs, sink_probs = [], []
            for g in range(N_KV):
                cols = slice(g * HEAD_DIM, (g + 1) * HEAD_DIM)
                qg = _stack_heads(q, g) * jnp.asarray(_QSCALE, BF16)
                p, ps = _attn_probs_t(k2[:, cols], qg, bias_n, s_ref[g:g + 1])
                pb = p.astype(BF16)
                p_ref[i, g] = pb
                sink_probs.append(ps)
                outs.append(_dot_tn(pb, v2[:, cols]))
            o_ref[rows] = _unstack_heads(outs).astype(BF16)
            ps_ref[i] = jnp.concatenate(sink_probs + [jnp.zeros((SUBLANES - N_KV, GROUP_ROWS), F32)], axis=0)

    return pl.pallas_call(
        body, name="attn_fwd", grid=(tp // tr,),
        in_specs=[_SINK_SPEC, _BIAS_SPEC, pl.BlockSpec((tr, ATTN_WIDTH), lambda t: (t, 0))] + _kv_specs(tr),
        out_specs=[pl.BlockSpec((tr, ATTN_WIDTH), lambda t: (t, 0))] + _prob_specs(qb),
        out_shape=[jax.ShapeDtypeStruct((tp, ATTN_WIDTH), BF16),
                   jax.ShapeDtypeStruct((nb, N_KV, 2 * BLOCK, GROUP_ROWS), BF16),
                   jax.ShapeDtypeStruct((nb, SUBLANES, GROUP_ROWS), F32)],
        compiler_params=_params("parallel"),
    )(sink_rows, bias, qkv, qkv, qkv, qkv, qkv)


def _conv_taps(x, halo):
    ext = jnp.concatenate([halo, x], axis=0)
    return [ext[8:] if k == 3 else pltpu.roll(ext, 3 - k, 0)[8:] for k in range(4)]


def _lru_gates(xc, wa, ba, wx, bx, sp):
    xb = xc.astype(BF16)
    r = _sigmoid(_dot(xb, wa) + ba)
    ig = _sigmoid(_dot(xb, wx) + bx)
    log_a = (-LRU_C * sp) * r
    a = jnp.exp(log_a)
    mult = jnp.sqrt(_one_minus_exp2(log_a))
    return xb, r, ig, a, mult


SUBLANES = 8


def _scan_fwd(a, b, h_in):
    n, width = a.shape
    a, b = (v.reshape(n // SUBLANES, SUBLANES, width) for v in (a, b))
    in_group = lax.broadcasted_iota(jnp.int32, a.shape, 1)
    for d in (1, 2, 4):
        keep = in_group >= d
        b = jnp.where(keep, a * pltpu.roll(b, d, 1) + b, b)
        a = jnp.where(keep, a * pltpu.roll(a, d, 1), a)
    a, b = a.reshape(n, width), b.reshape(n, width)
    out, carry = [], h_in
    for g in range(0, n, SUBLANES):
        h = a[g:g + SUBLANES] * carry + b[g:g + SUBLANES]
        out.append(h)
        carry = h[SUBLANES - 1:]
    return jnp.concatenate(out, axis=0)


def _scan_rev(c, b, g_in):
    n, width = c.shape
    c, b = (v.reshape(n // SUBLANES, SUBLANES, width) for v in (c, b))
    in_group = lax.broadcasted_iota(jnp.int32, c.shape, 1)
    for d in (1, 2, 4):
        keep = in_group < SUBLANES - d
        b = jnp.where(keep, b + c * pltpu.roll(b, SUBLANES - d, 1), b)
        c = jnp.where(keep, c * pltpu.roll(c, SUBLANES - d, 1), c)
    c, b = c.reshape(n, width), b.reshape(n, width)
    out, carry = [], g_in
    for g in range(n - SUBLANES, -1, -SUBLANES):
        r = b[g:g + SUBLANES] + c[g:g + SUBLANES] * carry
        out.append(r)
        carry = r[:1]
    return jnp.concatenate(out[::-1], axis=0)


def _lru_fwd(xr, yr, conv_w, conv_b, wa, ba, wx, bx, lam):
    tp = xr.shape[0]
    tr = _row_tile(tp)
    qb = tr // BLOCK

    def body(xr_ref, yr_ref, cw_ref, cb_ref, wa_ref, ba_ref, wx_ref, bx_ref, lam_ref, hr_ref, rec_ref, halo, hprev):
        t = pl.program_id(0)

        @pl.when(t == 0)
        def _():
            halo[...] = jnp.zeros_like(halo)
            hprev[...] = jnp.zeros_like(hprev)

        cw, cb = cw_ref[...], cb_ref[...]
        wa_m, ba_v, wx_m, bx_v = wa_ref[...], ba_ref[...], wx_ref[...], bx_ref[...]
        sp = _softplus(-lam_ref[...])
        before, h_last = halo[...], hprev[0:1]
        for i in range(qb):
            rows = slice(i * BLOCK, (i + 1) * BLOCK)
            x = xr_ref[rows]
            taps = _conv_taps(x, before)
            before = x[BLOCK - 8:]
            xc = cb + sum(cw[k:k + 1] * taps[k] for k in range(4))
            _, _, ig, a, mult = _lru_gates(xc, wa_m, ba_v, wx_m, bx_v, sp)
            u = mult * (ig * xc)
            if i == 0:
                pos = t * tr + lax.broadcasted_iota(jnp.int32, xc.shape, 0)
                u = jnp.where(pos >= PAD_ROWS, u, 0.0)
            h = _scan_fwd(a, u, h_last)
            h_last = h[BLOCK - 1:]
            hr_ref[rows] = h
            gl, _ = _gelu(yr_ref[rows])
            rec_ref[rows] = (gl * h).astype(BF16)
        halo[...] = before
        hprev[0:1] = h_last

    blk = pl.BlockSpec((tr, LRU_WIDTH), lambda t: (t, 0))
    full = lambda a: pl.BlockSpec(a.shape, lambda t: (0,) * a.ndim)
    small = [conv_w, conv_b, wa, ba, wx, bx, lam]
    return pl.pallas_call(
        body, name="lru_fwd", grid=(tp // tr,),
        in_specs=[blk, blk] + [full(a) for a in small],
        out_specs=[blk, blk],
        out_shape=[jax.ShapeDtypeStruct((tp, LRU_WIDTH), F32), jax.ShapeDtypeStruct((tp, LRU_WIDTH), BF16)],
        scratch_shapes=[pltpu.VMEM((8, LRU_WIDTH), F32), pltpu.VMEM((8, LRU_WIDTH), F32)],
        compiler_params=_params("arbitrary"),
    )(xr, yr, *small)


def _outproj_fwd(attn, rec, w_out, head, x, g_post_mix, g_pre_ffn):
    tp = attn.shape[0]
    tr = _row_tile(tp)
    qb = tr // BLOCK

    def body(*refs):
        a_ref, r_ref, w_ref, head_ref = refs[:4]
        pieces = refs[4:4 + qb]
        gm_ref, gf_ref, mix_ref, h1_ref, u1_ref = refs[4 + qb:]
        mix = _dot(a_ref[...], w_ref[:ATTN_WIDTH]) + _dot(r_ref[...], w_ref[ATTN_WIDTH:])
        mix_ref[...] = mix
        mhat, _ = _rms(mix)
        h1 = _seq_tile(head_ref[...], pieces, pl.program_id(0)) + mhat * gm_ref[...]
        h1_ref[...] = h1
        hhat, _ = _rms(h1)
        u1_ref[...] = (hhat * gf_ref[...]).astype(BF16)

    row = lambda w: pl.BlockSpec((tr, w), lambda i: (i, 0))
    full = lambda a: pl.BlockSpec(a.shape, lambda i: (0,) * a.ndim)
    return pl.pallas_call(
        body, name="outproj_fwd", grid=(tp // tr,),
        in_specs=[row(ATTN_WIDTH), row(LRU_WIDTH), full(w_out), full(head)] + _seq_specs(tr)
        + [full(g_post_mix), full(g_pre_ffn)],
        out_specs=[row(D_MODEL), row(D_MODEL), row(D_MODEL)],
        out_shape=[jax.ShapeDtypeStruct((tp, D_MODEL), F32), jax.ShapeDtypeStruct((tp, D_MODEL), F32),
                   jax.ShapeDtypeStruct((tp, D_MODEL), BF16)],
        compiler_params=_params("parallel"),
    )(attn, rec, w_out, head, *([x] * qb), g_post_mix, g_pre_ffn)


def _resident(a):
    return pl.BlockSpec(a.shape, lambda *_: (0,) * a.ndim, pipeline_mode=pl.Buffered(1))


def _ffn_fwd(u1, w1, w2, h1, tgt, g_post_ffn):
    tp = h1.shape[0]
    tr = _row_tile(tp)
    qb, nt = tr // BLOCK, tp // tr
    sr = tr // N_CHIPS

    def body(*refs):
        u_ref, w1_ref, w2_ref, h1_ref = refs[:4]
        t_pieces = refs[4:4 + qb]
        g_ref, r1_ref, dy_ref, df2_ref, loss_ref, dg_ref, acc = refs[4 + qb:]
        i, c = pl.program_id(0), pl.program_id(1)
        cur = i % 2

        @pl.when((i == 0) & (c == 0))
        def _():
            loss_ref[...] = jnp.zeros_like(loss_ref)
            dg_ref[...] = jnp.zeros_like(dg_ref)
            acc[1] = jnp.zeros((tr, D_MODEL), F32)

        def matmuls():
            r = jnp.maximum(_dot(u_ref[...], w1_ref[c]), 0.0)
            r1_ref[...] = r.astype(BF16)
            return _dot((r * r).astype(BF16), w2_ref[c])

        def finish_previous_tile(k, valid):
            lo, hi = k * sr, (k + 1) * sr
            g = g_ref[...]
            fhat, rs = _rms(acc[1 - cur, lo:hi])
            h2 = h1_ref[lo:hi] + fhat * g
            rows = (i - 1) * tr + lo + lax.broadcasted_iota(jnp.int32, h2.shape, 0)
            tgt = jnp.concatenate([p[max(lo - s * BLOCK, 0):min(hi - s * BLOCK, BLOCK)] for s, p in enumerate(t_pieces)
                                   if lo < (s + 1) * BLOCK and hi > s * BLOCK], axis=0)
            err = jnp.where((rows >= BLOCK) & valid, h2 - tgt, 0.0)
            dy = err * (1.0 / D_MODEL)
            dy_ref[lo:hi] = dy
            loss_ref[...] += (0.5 / D_MODEL) * jnp.sum(err * err)
            df2, dg = _rms_bwd(fhat, rs, g, dy)
            df2_ref[lo:hi] = df2.astype(BF16)
            dg_ref[...] += dg

        for k in range(N_CHIPS):
            @pl.when((c == k) & (i < nt))
            def _(k=k):
                finish_previous_tile(k, i >= 1)
                if k == 0:
                    acc[cur] = matmuls()
                else:
                    acc[cur] += matmuls()

            @pl.when((c == k) & (i == nt))
            def _(k=k):
                finish_previous_tile(k, True)

    last = nt - 1
    this_row = pl.BlockSpec((tr, D_MODEL), lambda i, c: (jnp.minimum(i, last), 0))
    prev_row = pl.BlockSpec((tr, D_MODEL), lambda i, c: (jnp.maximum(i - 1, 0), 0))
    full = lambda a: pl.BlockSpec(a.shape, lambda i, c: (0,) * a.ndim)
    return pl.pallas_call(
        body, name="ffn_fwd", grid=(nt + 1, N_CHIPS),
        in_specs=[this_row, _resident(w1), _resident(w2), prev_row] + _seq_specs(tr, delay=1) + [full(g_post_ffn)],
        out_specs=[pl.BlockSpec((tr, FF_CHUNK), lambda i, c: (jnp.minimum(i, last), jnp.where(i < nt, c, N_CHIPS - 1))),
                   prev_row, prev_row,
                   pl.BlockSpec((1, 1), lambda i, c: (0, 0)), pl.BlockSpec((1, D_MODEL), lambda i, c: (0, 0))],
        out_shape=[jax.ShapeDtypeStruct((tp, D_FF), BF16), jax.ShapeDtypeStruct((tp, D_MODEL), F32),
                   jax.ShapeDtypeStruct((tp, D_MODEL), BF16), jax.ShapeDtypeStruct((1, 1), F32),
                   jax.ShapeDtypeStruct((1, D_MODEL), F32)],
        scratch_shapes=[pltpu.VMEM((2, tr, D_MODEL), F32)],
        compiler_params=_params("arbitrary", "arbitrary"),
    )(u1, w1, w2, h1, *([tgt] * qb), g_post_ffn)


def _ffn_bwd_data(df2, r1, w1, w2, dy, h1, mix, g_pre_ffn, g_post_mix):
    tp = h1.shape[0]
    tr = _row_tile(tp)
    nt = tp // tr
    sr = tr // N_CHIPS

    def body(df2_ref, r1_ref, w1_ref, w2_ref, dy_ref, h1_ref, mix_ref, gf_ref, gm_ref,
             da_ref, dh1_ref, dmix_ref, dgf_ref, dgm_ref, acc):
        i, c = pl.program_id(0), pl.program_id(1)
        cur = i % 2

        @pl.when((i == 0) & (c == 0))
        def _():
            dgf_ref[...] = jnp.zeros_like(dgf_ref)
            dgm_ref[...] = jnp.zeros_like(dgm_ref)
            acc[1] = jnp.zeros((tr, D_MODEL), F32)

        def matmuls():
            df = _dot_nt(df2_ref[...], w2_ref[c])
            da = (df * (2.0 * r1_ref[...].astype(F32))).astype(BF16)
            da_ref[...] = da
            return _dot_nt(da, w1_ref[c])

        def finish_previous_tile(k, valid):
            lo, hi = k * sr, (k + 1) * sr
            hhat, rs = _rms(h1_ref[lo:hi])
            dx, dgf = _rms_bwd(hhat, rs, gf_ref[...], acc[1 - cur, lo:hi])
            dh1 = dy_ref[lo:hi] + dx
            dh1_ref[lo:hi] = dh1
            mhat, rsm = _rms(mix_ref[lo:hi])
            dmix, dgm = _rms_bwd(mhat, rsm, gm_ref[...], dh1)
            dmix_ref[lo:hi] = dmix.astype(BF16)
            dgf_ref[...] += jnp.where(valid, dgf, 0.0)
            dgm_ref[...] += jnp.where(valid, dgm, 0.0)

        for k in range(N_CHIPS):
            @pl.when((c == k) & (i < nt))
            def _(k=k):
                finish_previous_tile(k, i >= 1)
                if k == 0:
                    acc[cur] = matmuls()
                else:
                    acc[cur] += matmuls()

            @pl.when((c == k) & (i == nt))
            def _(k=k):
                finish_previous_tile(k, True)

    last = nt - 1
    this_row = pl.BlockSpec((tr, D_MODEL), lambda i, c: (jnp.minimum(i, last), 0))
    prev_row = pl.BlockSpec((tr, D_MODEL), lambda i, c: (jnp.maximum(i - 1, 0), 0))
    chunk = pl.BlockSpec((tr, FF_CHUNK), lambda i, c: (jnp.minimum(i, last), jnp.where(i < nt, c, N_CHIPS - 1)))
    gain = pl.BlockSpec((1, D_MODEL), lambda i, c: (0, 0))
    return pl.pallas_call(
        body, name="ffn_bwd_data", grid=(nt + 1, N_CHIPS),
        in_specs=[this_row, chunk, _resident(w1), _resident(w2), prev_row, prev_row, prev_row, gain, gain],
        out_specs=[chunk, prev_row, prev_row, gain, gain],
        out_shape=[jax.ShapeDtypeStruct((tp, D_FF), BF16), jax.ShapeDtypeStruct((tp, D_MODEL), F32),
                   jax.ShapeDtypeStruct((tp, D_MODEL), BF16), jax.ShapeDtypeStruct((1, D_MODEL), F32),
                   jax.ShapeDtypeStruct((1, D_MODEL), F32)],
        scratch_shapes=[pltpu.VMEM((2, tr, D_MODEL), F32)],
        compiler_params=_params("arbitrary", "arbitrary"),
    )(df2, r1, w1, w2, dy, h1, mix, g_pre_ffn, g_post_mix)


def _ffn_bwd_weights(u1, da1, r1, df2):
    tp = u1.shape[0]
    tr = _wgrad_row_tile(tp)

    def body(u_ref, da_ref, r1_ref, df2_ref, dw1_ref, dw2_ref):
        i = pl.program_id(1)
        r = r1_ref[...].astype(F32)
        p1 = _dot_tn(u_ref[...], da_ref[...])
        p2 = _dot_tn((r * r).astype(BF16), df2_ref[...])

        @pl.when(i == 0)
        def _():
            dw1_ref[0] = p1
            dw2_ref[0] = p2

        @pl.when(i > 0)
        def _():
            dw1_ref[0] += p1
            dw2_ref[0] += p2

    row = pl.BlockSpec((tr, D_MODEL), lambda c, i: (i, 0))
    chunk = pl.BlockSpec((tr, FF_CHUNK), lambda c, i: (i, c))
    return pl.pallas_call(
        body, name="ffn_bwd_weights", grid=(N_CHIPS, tp // tr),
        in_specs=[row, chunk, chunk, row],
        out_specs=[pl.BlockSpec((1, D_MODEL, FF_CHUNK), lambda c, i: (c, 0, 0)),
                   pl.BlockSpec((1, FF_CHUNK, D_MODEL), lambda c, i: (c, 0, 0))],
        out_shape=[jax.ShapeDtypeStruct((N_CHIPS, D_MODEL, FF_CHUNK), F32),
                   jax.ShapeDtypeStruct((N_CHIPS, FF_CHUNK, D_MODEL), F32)],
        compiler_params=_params("parallel", "arbitrary"),
    )(u1, da1, r1, df2)


def _outproj_bwd(dmix, w_out, attn, rec, token):
    tp = dmix.shape[0]
    tr = _wgrad_row_tile(tp)

    def body(dm_ref, w_ref, a_ref, r_ref, _, da_ref, dr_ref, dw_ref):
        i = pl.program_id(0)
        dm = dm_ref[...]
        dcat = _dot_nt(dm, w_ref[...])
        da_ref[...] = dcat[:, :ATTN_WIDTH].astype(BF16)
        dr_ref[...] = dcat[:, ATTN_WIDTH:]
        pa = _dot_tn(a_ref[...], dm)
        pr = _dot_tn(r_ref[...], dm)

        @pl.when(i == 0)
        def _():
            dw_ref[:ATTN_WIDTH] = pa
            dw_ref[ATTN_WIDTH:] = pr

        @pl.when(i > 0)
        def _():
            dw_ref[:ATTN_WIDTH] += pa
            dw_ref[ATTN_WIDTH:] += pr

    row = lambda w: pl.BlockSpec((tr, w), lambda i: (i, 0))
    full = pl.BlockSpec((D_MODEL, D_MODEL), lambda i: (0, 0))
    return pl.pallas_call(
        body, name="outproj_bwd", grid=(tp // tr,),
        in_specs=[row(D_MODEL), full, row(ATTN_WIDTH), row(LRU_WIDTH), pl.BlockSpec(token.shape, lambda i: (0, 0))],
        out_specs=[row(ATTN_WIDTH), row(LRU_WIDTH), full],
        out_shape=[jax.ShapeDtypeStruct((tp, ATTN_WIDTH), BF16), jax.ShapeDtypeStruct((tp, LRU_WIDTH), F32),
                   jax.ShapeDtypeStruct((D_MODEL, D_MODEL), F32)],
        compiler_params=_params("arbitrary"),
    )(dmix, w_out, attn, rec, token)


N_VEC_ROWS = 8


def _lru_bwd(xr, yr, hr, drec, conv_w, conv_b, wa, ba, wx, bx, lam, token):
    tp = xr.shape[0]
    tr = _row_tile(tp)
    qb, nt = tr // BLOCK, tp // tr

    def body(xr_ref, xh_ref, yr_ref, hr_ref, hp_ref, dr_ref, cw_ref, cb_ref, wa_ref, ba_ref, wx_ref, bx_ref, lam_ref, _,
             dxr_ref, dyr_ref, dwa_ref, dwx_ref, vec_ref, g_next, a_next, dxc_next, dsp):
        s = pl.program_id(0)
        t = nt - 1 - s

        @pl.when(s == 0)
        def _():
            g_next[...] = jnp.zeros_like(g_next)
            a_next[...] = jnp.zeros_like(a_next)
            dxc_next[...] = jnp.zeros_like(dxc_next)
            dsp[...] = jnp.zeros_like(dsp)
            dwa_ref[...] = jnp.zeros_like(dwa_ref)
            dwx_ref[...] = jnp.zeros_like(dwx_ref)
            vec_ref[...] = jnp.zeros_like(vec_ref)

        first_tile = t == 0
        cw, cb = cw_ref[...], cb_ref[...]
        lam_v = lam_ref[...]
        sp = _softplus(-lam_v)
        wa_m, ba_v, wx_m, bx_v = wa_ref[...], ba_ref[...], wx_ref[...], bx_ref[...]
        rows = lax.broadcasted_iota(jnp.int32, (BLOCK, LRU_WIDTH), 0)
        col = lambda v: jnp.sum(v, axis=0, keepdims=True)

        g_after, a_after, dxc_after = g_next[0:1], a_next[0:1], dxc_next[...]
        xbs, dgrs, dgis = [], [], []
        vec = [jnp.zeros((1, LRU_WIDTH), F32) for _ in range(N_VEC_ROWS)]
        for i in reversed(range(qb)):
            blk = slice(i * BLOCK, (i + 1) * BLOCK)
            if i == 0:
                x_before = jnp.where(first_tile, 0.0, xh_ref[...])
                h_before = jnp.where(first_tile, 0.0, hp_ref[7:8])
            else:
                x_before = xr_ref[i * BLOCK - 8:i * BLOCK]
                h_before = hr_ref[i * BLOCK - 1:i * BLOCK]
            taps = _conv_taps(xr_ref[blk], x_before)
            xc = cb + sum(cw[k:k + 1] * taps[k] for k in range(4))
            xb, r, ig, a, mult = _lru_gates(xc, wa_m, ba_v, wx_m, bx_v, sp)

            yr_v = yr_ref[blk]
            gl, th = _gelu(yr_v)
            h = hr_ref[blk]
            drec = dr_ref[blk]
            dyr_ref[blk] = (drec * h * _gelu_grad(yr_v, th)).astype(BF16)

            a_up = jnp.where(rows == BLOCK - 1, a_after, pltpu.roll(a, BLOCK - 1, 0))
            g = _scan_rev(a_up, drec * gl, g_after)
            g_after, a_after = g[0:1], a[0:1]

            h_prev = jnp.where(rows == 0, h_before, pltpu.roll(h, 1, 0))
            du, da = g, g * h_prev
            if i == 0:
                real = (t * tr + rows) >= PAD_ROWS
                du, da = jnp.where(real, du, 0.0), jnp.where(real, da, 0.0)
            dmult = du * (ig * xc)
            dig = du * (mult * xc)
            dxc = du * (mult * ig)
            dlog_a = da * a - dmult * (a * a / mult)
            if i == 0:
                dlog_a = jnp.where(real, dlog_a, 0.0)
            dgr = (dlog_a * (-LRU_C * sp)) * (r * (1.0 - r))
            dgi = dig * (ig * (1.0 - ig))
            dgr_b, dgi_b = dgr.astype(BF16), dgi.astype(BF16)
            dxc = dxc + _dot_nt(dgr_b, wa_m) + _dot_nt(dgi_b, wx_m)
            xbs.append(xb)
            dgrs.append(dgr_b)
            dgis.append(dgi_b)

            ext = jnp.concatenate([dxc, dxc_after], axis=0)
            up = [ext[:BLOCK] if j == 0 else pltpu.roll(ext, BLOCK + 8 - j, 0)[:BLOCK] for j in range(4)]
            dxr_ref[blk] = sum(cw[k:k + 1] * up[3 - k] for k in range(4)).astype(BF16)
            dxc_after = dxc[:8]

            for k in range(4):
                vec[k] = vec[k] + col(dxc * taps[k])
            vec[4] = vec[4] + col(dxc)
            vec[5] = vec[5] + col(dgr)
            vec[6] = vec[6] + col(dgi)
            vec[7] = vec[7] + col(dlog_a * (-LRU_C * r))

        g_next[0:1], a_next[0:1], dxc_next[...] = g_after, a_after, dxc_after
        xb_all = jnp.concatenate(xbs, axis=0)
        dwa_ref[...] += _dot_tn(xb_all, jnp.concatenate(dgrs, axis=0))
        dwx_ref[...] += _dot_tn(xb_all, jnp.concatenate(dgis, axis=0))
        for k in range(7):
            vec_ref[k:k + 1] += vec[k]
        dsp[0:1] += vec[7]

        @pl.when(s == nt - 1)
        def _():
            vec_ref[7:8] = dsp[0:1] * (-_sigmoid(-lam_v))

    blk_spec = pl.BlockSpec((tr, LRU_WIDTH), lambda s: (nt - 1 - s, 0))
    rows_before = pl.BlockSpec((8, LRU_WIDTH), lambda s: (jnp.maximum((nt - 1 - s) * (tr // 8) - 1, 0), 0))
    full = lambda a: pl.BlockSpec(a.shape, lambda s: (0,) * a.ndim)
    small = [conv_w, conv_b, wa, ba, wx, bx, lam, token]
    sq = pl.BlockSpec((LRU_WIDTH, LRU_WIDTH), lambda s: (0, 0))
    return pl.pallas_call(
        body, name="lru_bwd", grid=(nt,),
        in_specs=[blk_spec, rows_before, blk_spec, blk_spec, rows_before, blk_spec] + [full(a) for a in small],
        out_specs=[blk_spec, blk_spec, sq, sq, pl.BlockSpec((N_VEC_ROWS, LRU_WIDTH), lambda s: (0, 0))],
        out_shape=[jax.ShapeDtypeStruct((tp, LRU_WIDTH), BF16), jax.ShapeDtypeStruct((tp, LRU_WIDTH), BF16),
                   jax.ShapeDtypeStruct((LRU_WIDTH, LRU_WIDTH), F32), jax.ShapeDtypeStruct((LRU_WIDTH, LRU_WIDTH), F32),
                   jax.ShapeDtypeStruct((N_VEC_ROWS, LRU_WIDTH), F32)],
        scratch_shapes=[pltpu.VMEM((8, LRU_WIDTH), F32)] * 4,
        compiler_params=_params("arbitrary"),
    )(xr, xr, yr, hr, hr, drec, *small)


def _attn_bwd(qkv, dattn, probs, sink_probs):
    tp = qkv.shape[0]
    tr = _row_tile(tp)
    qb, nt = tr // BLOCK, tp // tr
    n_groups = N_KV

    def body(p_ref, ps_ref, q_ref, kp_ref, kc_ref, vp_ref, vc_ref, do_ref, dq_ref, dkv_ref, ex_ref, ds_ref, dsink):
        t = pl.program_id(0)

        @pl.when(t == 0)
        def _():
            dsink[...] = jnp.zeros_like(dsink)

        k_all = jnp.concatenate([kp_ref[...], kc_ref[...]], axis=0)
        v_all = jnp.concatenate([vp_ref[...], vc_ref[...]], axis=0)
        tail = None
        for i in range(qb):
            rows = slice(i * BLOCK, (i + 1) * BLOCK)
            q, do = q_ref[rows], do_ref[rows]
            k2, v2 = k_all[i * BLOCK:(i + 2) * BLOCK], v_all[i * BLOCK:(i + 2) * BLOCK]
            dqs, dks, dvs = [], [], []
            for g in range(n_groups):
                cols = slice(g * HEAD_DIM, (g + 1) * HEAD_DIM)
                k_g, v_g = k2[:, cols], v2[:, cols]
                qg = _stack_heads(q, g) * jnp.asarray(_QSCALE, BF16)
                dog = _stack_heads(do, g)
                pb = p_ref[i, g]
                p = pb.astype(F32)
                dpt = _dot_nt(v_g, dog)
                delta = jnp.sum(p * dpt, axis=0, keepdims=True)
                dst = (p * (dpt - delta)).astype(BF16)
                dqs.append(_dot_tn(dst, k_g) * _QSCALE)
                dks.append(_dot(dst, qg))
                dvs.append(_dot(pb, dog))
                dsink[g:g + 1] -= ps_ref[i, g:g + 1] * delta
            dq_ref[rows] = _unstack_heads(dqs).astype(BF16)
            dkv = jnp.concatenate(dks + dvs, axis=1)
            if i == 0:
                ex_ref[0] = dkv[:BLOCK]
            else:
                dkv_ref[(i - 1) * BLOCK:i * BLOCK] = (tail + dkv[:BLOCK]).astype(BF16)
            tail = dkv[BLOCK:]
        dkv_ref[(qb - 1) * BLOCK:] = tail.astype(BF16)

        @pl.when(t == nt - 1)
        def _():
            lane = lax.broadcasted_iota(jnp.int32, (1, ATTN_HEADS), 1)
            acc = jnp.zeros((1, ATTN_HEADS), F32)
            for h in range(ATTN_HEADS):
                g, hh = divmod(h, GQA_GROUP)
                acc = acc + jnp.where(lane == h, jnp.sum(dsink[g:g + 1, hh * BLOCK:(hh + 1) * BLOCK]), 0.0)
            ds_ref[...] = acc

    cur = lambda w: pl.BlockSpec((tr, w), lambda t: (t, 0))
    return pl.pallas_call(
        body, name="attn_bwd", grid=(nt,),
        in_specs=_prob_specs(qb) + [cur(ATTN_WIDTH)] + _kv_specs(tr) + [cur(ATTN_WIDTH)],
        out_specs=[cur(ATTN_WIDTH), cur(2 * KV_WIDTH), pl.BlockSpec((1, BLOCK, 2 * KV_WIDTH), lambda t: (t, 0, 0)),
                   pl.BlockSpec((1, ATTN_HEADS), lambda t: (0, 0))],
        out_shape=[jax.ShapeDtypeStruct((tp, ATTN_WIDTH), BF16), jax.ShapeDtypeStruct((tp, 2 * KV_WIDTH), BF16),
                   jax.ShapeDtypeStruct((nt, BLOCK, 2 * KV_WIDTH), F32), jax.ShapeDtypeStruct((1, ATTN_HEADS), F32)],
        scratch_shapes=[pltpu.VMEM((n_groups, GROUP_ROWS), F32)],
        compiler_params=_params("arbitrary"),
    )(probs, sink_probs, qkv, qkv, qkv, qkv, qkv, dattn)


def _fix_dkv(dkv, dkv_extra):
    tp = dkv.shape[0]
    tr = _row_tile(tp)
    nt, qb = tp // tr, tr // BLOCK
    if nt == 1:
        return dkv

    def body(d_ref, ex_ref, o_ref):
        o_ref[...] = (d_ref[...].astype(F32) + ex_ref[0]).astype(BF16)

    last = pl.BlockSpec((BLOCK, 2 * KV_WIDTH), lambda t: (t * qb + qb - 1, 0))
    return pl.pallas_call(
        body, name="fix_dkv", grid=(nt - 1,),
        in_specs=[last, pl.BlockSpec((1, BLOCK, 2 * KV_WIDTH), lambda t: (t + 1, 0, 0))],
        out_specs=last, out_shape=jax.ShapeDtypeStruct(dkv.shape, dkv.dtype),
        input_output_aliases={0: 0}, compiler_params=_params("parallel"),
    )(dkv, dkv_extra)


def _inproj_wgrad(dq, dkv, dxr, dyr, u0):
    tp = dq.shape[0]
    tr = _wgrad_row_tile(tp)

    def body(dq_ref, dkv_ref, dxr_ref, dyr_ref, u_ref, dw_ref):
        i = pl.program_id(0)
        dz = jnp.concatenate([dq_ref[...], dkv_ref[...], dxr_ref[...], dyr_ref[...]], axis=1)
        pw = _dot_tn(dz, u_ref[...])

        @pl.when(i == 0)
        def _():
            dw_ref[...] = pw

        @pl.when(i > 0)
        def _():
            dw_ref[...] += pw

    row = lambda w: pl.BlockSpec((tr, w), lambda i: (i, 0))
    return pl.pallas_call(
        body, name="inproj_wgrad", grid=(tp // tr,),
        in_specs=[row(ATTN_WIDTH), row(2 * KV_WIDTH), row(LRU_WIDTH), row(LRU_WIDTH), row(D_MODEL)],
        out_specs=pl.BlockSpec((IN_WIDTH, D_MODEL), lambda i: (0, 0)),
        out_shape=jax.ShapeDtypeStruct((IN_WIDTH, D_MODEL), F32),
        compiler_params=_params("arbitrary"),
    )(dq, dkv, dxr, dyr, u0)


def _inproj_dgrad(dq, dkv, dxr, dyr, w_in, head, x, dh1, g, token):
    tp = dq.shape[0]
    tr = _row_tile(tp)
    nt, qb = tp // tr, tr // BLOCK

    def body(*refs):
        dq_ref, dkv_ref, dxr_ref, dyr_ref, w_ref, head_ref = refs[:6]
        pieces = refs[6:6 + qb]
        dh1_ref, g_ref, _, gx_ref, dhead_ref, dg_ref, du_scr, buf, sems = refs[6 + qb:]
        i = pl.program_id(0)
        j = i - 1
        cur = i % 2
        slot = 1 - cur

        def out_copy(step, at):
            return pltpu.make_async_copy(buf.at[at], gx_ref.at[pl.ds(step * tr - BLOCK, tr)], sems.at[at])

        @pl.when(i == 0)
        def _():
            dg_ref[...] = jnp.zeros_like(dg_ref)
            du_scr[1] = jnp.zeros((tr, D_MODEL), F32)

        @pl.when(j >= 3)
        def _():
            out_copy(j - 2, slot).wait()

        def norm_backward():
            hhat, rs = _rms(_seq_tile(head_ref[...], pieces, j))
            dx, dg = _rms_bwd(hhat, rs, g_ref[...], du_scr[slot])
            return dh1_ref[...] + dx, dg

        def emit(dh0, dg):
            buf[slot] = dh0

            @pl.when(j == 0)
            def _():
                dg_ref[...] += dg
                dhead_ref[...] = dh0[:BLOCK]
                if tr > BLOCK:
                    first = pltpu.make_async_copy(buf.at[0, pl.ds(BLOCK, tr - BLOCK)], gx_ref.at[pl.ds(0, tr - BLOCK)],
                                                  sems.at[0])
                    first.start()
                    first.wait()

            @pl.when(j >= 1)
            def _():
                dg_ref[...] += dg
                out_copy(j, slot).start()

        @pl.when(i < nt)
        def _():
            dh0, dg = norm_backward()
            dz = jnp.concatenate([dq_ref[...], dkv_ref[...], dxr_ref[...], dyr_ref[...]], axis=1)
            du_scr[cur] = _dot(dz, w_ref[...])
            emit(dh0, dg)

        @pl.when(i == nt)
        def _():
            emit(*norm_backward())
            if nt >= 3:
                out_copy(nt - 2, (nt - 2) % 2).wait()
            if nt >= 2:
                out_copy(nt - 1, (nt - 1) % 2).wait()

    last = nt - 1
    row = lambda w: pl.BlockSpec((tr, w), lambda i: (jnp.minimum(i, last), 0))
    full = lambda shape: pl.BlockSpec(shape, lambda i: (0,) * len(shape))
    return pl.pallas_call(
        body, name="inproj_dgrad", grid=(nt + 1,),
        in_specs=[row(ATTN_WIDTH), row(2 * KV_WIDTH), row(LRU_WIDTH), row(LRU_WIDTH), full(w_in.shape),
                  full(head.shape)] + _seq_specs(tr, delay=1)
        + [pl.BlockSpec((tr, D_MODEL), lambda i: (jnp.maximum(i - 1, 0), 0)), full(g.shape), full(token.shape)],
        out_specs=[pl.BlockSpec(memory_space=pl.ANY), full((BLOCK, D_MODEL)), full((1, D_MODEL))],
        out_shape=[jax.ShapeDtypeStruct(x.shape, F32), jax.ShapeDtypeStruct((BLOCK, D_MODEL), F32),
                   jax.ShapeDtypeStruct((1, D_MODEL), F32)],
        scratch_shapes=[pltpu.VMEM((2, tr, D_MODEL), F32), pltpu.VMEM((2, tr, D_MODEL), F32),
                        pltpu.SemaphoreType.DMA((2,))],
        compiler_params=_params("arbitrary"),
    )(dq, dkv, dxr, dyr, w_in, head, *([x] * qb), dh1, g, token)


def _dense_block_diag(w):
    eye = jnp.eye(LRU_BLOCKS, dtype=w.dtype)
    return (w[:, :, None, :] * eye[:, None, :, None]).reshape(LRU_WIDTH, LRU_WIDTH)


def _diag_blocks(dense):
    d4 = dense.reshape(LRU_BLOCKS, LRU_BLOCK, LRU_BLOCKS, LRU_BLOCK)
    return jnp.stack([d4[n, :, n, :] for n in range(LRU_BLOCKS)])


def _local_step(head, x, tgt, g_pre_mix, w_in, conv_w, conv_b, w_a, b_a, w_x, b_x, lam, sinks, g_post_mix,
                g_pre_ffn, g_post_ffn, late_weights, on_ffn_grads, on_outproj_bwd, on_mixer_grads, token):
    wa = _dense_block_diag(w_a).astype(BF16)
    wx = _dense_block_diag(w_x).astype(BF16)

    u0, qkv, xr, yr = _inproj_fwd(head, x, g_pre_mix, w_in, token)
    attn, probs, sink_probs = _attn_fwd(qkv, sinks)
    hr, rec = _lru_fwd(xr, yr, conv_w, conv_b, wa, b_a, wx, b_x, lam)
    w_out, w1, w2 = late_weights([attn, rec])
    mix, h1, u1 = _outproj_fwd(attn, rec, w_out, head, x, g_post_mix, g_pre_ffn)
    r1, dy, df2, loss, dg_post_ffn = _ffn_fwd(u1, w1, w2, h1, tgt, g_post_ffn)

    da1, dh1, dmix, dg_pre_ffn, dg_post_mix = _ffn_bwd_data(df2, r1, w1, w2, dy, h1, mix, g_pre_ffn, g_post_mix)
    dw1, dw2 = _ffn_bwd_weights(u1, da1, r1, df2)
    token2 = on_ffn_grads(dw1, dw2)
    dattn, drec, dw_out = _outproj_bwd(dmix, w_out, attn, rec, token2)
    token3 = on_outproj_bwd(dattn)
    dxr, dyr, dwa, dwx, vec = _lru_bwd(xr, yr, hr, drec, conv_w, conv_b, wa, b_a, wx, b_x, lam, token3)
    dq, dkv, dkv_extra, dsinks = _attn_bwd(qkv, dattn, probs, sink_probs)
    dkv = _fix_dkv(dkv, dkv_extra)
    dw_in = _inproj_wgrad(dq, dkv, dxr, dyr, u0)
    token4 = on_mixer_grads(dw_in, dw_out)
    dx, dhead, dg_pre_mix = _inproj_dgrad(dq, dkv, dxr, dyr, w_in, head, x, dh1, g_pre_mix, token4)

    grads = dict(
        g_pre_mix=dg_pre_mix, conv_w=vec[0:4], conv_b=vec[4:5], w_a=_diag_blocks(dwa), b_a=vec[5:6],
        w_x=_diag_blocks(dwx), b_x=vec[6:7], lru_lambda=vec[7:8], attn_sinks=dsinks,
        g_post_mix=dg_post_mix, g_pre_ffn=dg_pre_ffn, g_post_ffn=dg_post_ffn)
    return loss, dx, dhead, grads


HBM = pl.BlockSpec(memory_space=pltpu.HBM)


def _mesh_pos():
    return lax.axis_index("x"), lax.axis_index("y"), lax.axis_index("c")


def _other_chips(x, y):
    return [(1 - x, y), (x, 1 - y), (1 - x, 1 - y)]


def _remote(src, dst, send_sem, recv_sem, to):
    return pltpu.make_async_remote_copy(src_ref=src, dst_ref=dst, send_sem=send_sem, recv_sem=recv_sem,
                                        device_id=to, device_id_type=MESH)


def _gather_weights(shards, lands, tiny, tiny_land):
    nbig = len(shards)

    def body(*refs):
        srcs, tiny_src = refs[:nbig], refs[nbig]
        outs, tiny_out = refs[2 * nbig + 2:3 * nbig + 2], refs[3 * nbig + 2]
        ici_send, ici_recv, d2d_send, d2d_recv, tiny_send, tiny_recv = refs[3 * nbig + 3:]
        x, y, c = _mesh_pos()
        me = 2 * x + y
        chips = _other_chips(x, y)
        sibling = (x, y, 1 - c)
        sends = []
        for w, (src, out) in enumerate(zip(srcs, outs)):
            hr = src.shape[0] // 2
            for j, chip in enumerate(chips):
                k = 3 * w + j
                cp = _remote(src.at[pl.ds(c * hr, hr)], out.at[me, pl.ds(c * hr, hr)],
                             ici_send.at[k], ici_recv.at[k], (*chip, c))
                cp.start()
                sends.append(cp)
        for j, chip in enumerate(chips):
            cp = _remote(tiny_src, tiny_out.at[me], tiny_send.at[j], tiny_recv.at[j], (*chip, c))
            cp.start()
            sends.append(cp)
        for w, (src, out) in enumerate(zip(srcs, outs)):
            hr = src.shape[0] // 2
            for j, (px, py) in enumerate(chips):
                k = 3 * w + j
                landed = out.at[2 * px + py, pl.ds(c * hr, hr)]
                _remote(landed, landed, ici_send.at[k], ici_recv.at[k], sibling).wait_recv()
                cp = _remote(landed, landed, d2d_send.at[k], d2d_recv.at[k], sibling)
                cp.start()
                sends.append(cp)
        for w, (src, out) in enumerate(zip(srcs, outs)):
            hr = src.shape[0] // 2
            for j, (px, py) in enumerate(chips):
                k = 3 * w + j
                other = out.at[2 * px + py, pl.ds((1 - c) * hr, hr)]
                _remote(other, other, d2d_send.at[k], d2d_recv.at[k], sibling).wait_recv()
        for j, (px, py) in enumerate(chips):
            blk = tiny_out.at[2 * px + py]
            _remote(blk, blk, tiny_send.at[j], tiny_recv.at[j], sibling).wait_recv()
        for cp in sends:
            cp.wait_send()

    out_shape = [jax.ShapeDtypeStruct(l.shape, l.dtype) for l in list(lands) + [tiny_land]]
    n = 3 * nbig
    return pl.pallas_call(
        body, name="gather_weights", out_shape=out_shape,
        in_specs=[HBM] * (2 * nbig + 2), out_specs=[HBM] * (nbig + 1),
        input_output_aliases={nbig + 1 + i: i for i in range(nbig + 1)},
        scratch_shapes=[pltpu.SemaphoreType.DMA((n,)),
                        pltpu.SemaphoreType.DMA((n,)), pltpu.SemaphoreType.DMA((n,)), pltpu.SemaphoreType.DMA((n,)),
                        pltpu.SemaphoreType.DMA((3,)), pltpu.SemaphoreType.DMA((3,))],
    )(*shards, tiny, *lands, tiny_land)


def _prep_shard(w, me):
    rows, cols = w.shape
    tr = 256 if rows % 256 == 0 else rows

    def body(me_ref, w_ref, s_ref, l_ref):
        b = w_ref[...].astype(BF16)
        s_ref[...] = b
        l_ref[0] = b

    return pl.pallas_call(
        body, name="prep_shard",
        grid_spec=pltpu.PrefetchScalarGridSpec(
            num_scalar_prefetch=1, grid=(rows // tr,),
            in_specs=[pl.BlockSpec((tr, cols), lambda i, me_ref: (i, 0))],
            out_specs=[pl.BlockSpec((tr, cols), lambda i, me_ref: (i, 0)),
                       pl.BlockSpec((1, tr, cols), lambda i, me_ref: (me_ref[0], i, 0))]),
        out_shape=[jax.ShapeDtypeStruct((rows, cols), BF16), jax.ShapeDtypeStruct((N_CHIPS, rows, cols), BF16)],
        compiler_params=_params("parallel"),
    )(me, w)


def _prep_tiny(tiny, me, slots=N_CHIPS):
    def body(me_ref, t_ref, l_ref):
        l_ref[0] = t_ref[...]

    return pl.pallas_call(
        body, name="prep_tiny",
        grid_spec=pltpu.PrefetchScalarGridSpec(
            num_scalar_prefetch=1, grid=(1,),
            in_specs=[pl.BlockSpec(tiny.shape, lambda i, me_ref: (0, 0))],
            out_specs=pl.BlockSpec((1,) + tiny.shape, lambda i, me_ref: (me_ref[0], 0, 0))),
        out_shape=jax.ShapeDtypeStruct((slots,) + tiny.shape, tiny.dtype),
    )(me, tiny)


N_DEV = 8


def _sibling_exchange(parts, token):
    def body(*refs):
        n = len(parts)
        srcs, outs, send_sems, recv_sems = refs[:n], refs[n + 1:2 * n + 1], refs[2 * n + 1], refs[2 * n + 2]
        x, y, c = _mesh_pos()
        sibling = (x, y, 1 - c)
        cps = []
        for w, (src, out) in enumerate(zip(srcs, outs)):
            hr = src.shape[1] // 2
            cp = _remote(src.at[:, pl.ds((1 - c) * hr, hr)], out, send_sems.at[w], recv_sems.at[w], sibling)
            cp.start()
            cps.append(cp)
        for cp in cps:
            cp.wait()

    n = len(parts)
    return pl.pallas_call(
        body, name="sibling_exchange",
        out_shape=[jax.ShapeDtypeStruct((p.shape[0], p.shape[1] // 2, p.shape[2]), p.dtype) for p in parts],
        in_specs=[HBM] * n + [pl.BlockSpec(memory_space=pl.ANY)], out_specs=[HBM] * n,
        scratch_shapes=[pltpu.SemaphoreType.DMA((n,)), pltpu.SemaphoreType.DMA((n,))],
    )(*parts, token)


def _chip_presum(part, from_sibling, pos):
    _, hr, cols = from_sibling.shape
    tr = 256 if hr % 256 == 0 else hr
    steps = hr // tr

    def body(pos_ref, a_ref, b_ref, o_ref, land_ref):
        s = (a_ref[...] + b_ref[...]).astype(BF16)
        o_ref[...] = s

        @pl.when(pl.program_id(1) == pos_ref[1])
        def _():
            land_ref[...] = s

    return pl.pallas_call(
        body, name="chip_presum",
        grid_spec=pltpu.PrefetchScalarGridSpec(
            num_scalar_prefetch=1, grid=(steps, N_CHIPS),
            in_specs=[pl.BlockSpec((1, tr, cols), lambda i, j, p: (j, p[0] * steps + i, 0)),
                      pl.BlockSpec((1, tr, cols), lambda i, j, p: (j, i, 0))],
            out_specs=[pl.BlockSpec((1, tr, cols), lambda i, j, p: (j, i, 0)),
                       pl.BlockSpec((1, tr, cols), lambda i, j, p: (p[1], p[0] * steps + i, 0))]),
        out_shape=[jax.ShapeDtypeStruct(from_sibling.shape, BF16),
                   jax.ShapeDtypeStruct((N_CHIPS, 2 * hr, cols), BF16)],
        compiler_params=_params("arbitrary", "arbitrary"),
    )(pos, part, from_sibling)


def _scatter_partials(cparts, lands, done_cparts=(), done_lands=()):
    n_new = len(cparts)
    nw = n_new + len(done_cparts)

    def body(*refs):
        srcs = refs[:nw]
        outs = refs[2 * nw:3 * nw]
        own_send, own_recv, ici_send, ici_recv, d2d_send, d2d_recv = refs[3 * nw:]
        x, y, c = _mesh_pos()
        me = 2 * x + y
        chips = _other_chips(x, y)
        sibling = (x, y, 1 - c)
        sends = []
        for w in list(range(n_new, nw)) + list(range(n_new)):
            src, out = srcs[w], outs[w]
            hr = src.shape[1]
            mine = out.at[me, pl.ds(c * hr, hr)]
            cp = _remote(src.at[me], mine, own_send.at[w], own_recv.at[w], sibling)
            cp.start()
            sends.append(cp)
            for j, (px, py) in enumerate(chips):
                if w >= n_new:
                    break
                k = 3 * w + j
                cp = _remote(src.at[2 * px + py], mine, ici_send.at[k], ici_recv.at[k], (px, py, c))
                cp.start()
                sends.append(cp)
        for w in list(range(n_new, nw)) + list(range(n_new)):
            src, out = srcs[w], outs[w]
            hr = src.shape[1]
            for j, (px, py) in enumerate(chips):
                k = 3 * w + j
                landed = out.at[2 * px + py, pl.ds(c * hr, hr)]
                if w < n_new:
                    _remote(landed, landed, ici_send.at[k], ici_recv.at[k], sibling).wait_recv()
                cp = _remote(landed, landed, d2d_send.at[k], d2d_recv.at[k], sibling)
                cp.start()
                sends.append(cp)
        for w, (src, out) in enumerate(zip(srcs, outs)):
            hr = src.shape[1]
            other = out.at[me, pl.ds((1 - c) * hr, hr)]
            _remote(other, other, own_send.at[w], own_recv.at[w], sibling).wait_recv()
            for j, (px, py) in enumerate(chips):
                k = 3 * w + j
                other = out.at[2 * px + py, pl.ds((1 - c) * hr, hr)]
                _remote(other, other, d2d_send.at[k], d2d_recv.at[k], sibling).wait_recv()
        for cp in sends:
            cp.wait_send()

    n = 3 * nw
    dma = pltpu.SemaphoreType.DMA
    every = list(cparts) + list(done_cparts)
    every_lands = list(lands) + list(done_lands)
    return pl.pallas_call(
        body, name="scatter_partials",
        out_shape=[jax.ShapeDtypeStruct(l.shape, l.dtype) for l in every_lands],
        in_specs=[HBM] * (2 * nw), out_specs=[HBM] * nw,
        input_output_aliases={nw + i: i for i in range(nw)},
        scratch_shapes=[dma((nw,)), dma((nw,)), dma((n,)), dma((n,)), dma((n,)), dma((n,))],
    )(*every, *every_lands)


SEM = pl.BlockSpec(memory_space=pltpu.SEMAPHORE)
SPLIT_COPY = pltpu.CompilerParams(has_side_effects=pltpu.SideEffectType.DATAFLOW_SIDE_EFFECTING)


def _hbm(a):
    return pltpu.with_memory_space_constraint(a, pltpu.HBM)


def _gather_copies(srcs, lands, send_sems, recv_sems):
    x, y, c = _mesh_pos()
    me = 2 * x + y
    sends, recvs = [], []
    for w, (src, land) in enumerate(zip(srcs, lands)):
        hr = src.shape[0] // 2
        for j, (px, py) in enumerate(_other_chips(x, y)):
            k = 3 * w + j
            sends.append(_remote(src.at[pl.ds(c * hr, hr)], land.at[me, pl.ds(c * hr, hr)],
                                 send_sems.at[k], recv_sems.at[k], (px, py, c)))
            got = land.at[2 * px + py, pl.ds(c * hr, hr)]
            recvs.append(_remote(got, got, send_sems.at[k], recv_sems.at[k], (px, py, c)))
    return sends, recvs


def _scatter_copies(srcs, lands, send_sems, recv_sems):
    x, y, c = _mesh_pos()
    me = 2 * x + y
    sends, recvs = [], []
    for w, (src, land) in enumerate(zip(srcs, lands)):
        hr = src.shape[1]
        for j, (px, py) in enumerate(_other_chips(x, y)):
            k = 3 * w + j
            sends.append(_remote(src.at[2 * px + py], land.at[me, pl.ds(c * hr, hr)],
                                 send_sems.at[k], recv_sems.at[k], (px, py, c)))
            got = land.at[2 * px + py, pl.ds(c * hr, hr)]
            recvs.append(_remote(got, got, send_sems.at[k], recv_sems.at[k], (px, py, c)))
    return sends, recvs


def _sibling_copies(srcs, lands, send_sems, recv_sems):
    x, y, c = _mesh_pos()
    sibling = (x, y, 1 - c)
    sends, recvs = [], []
    for w, (src, land) in enumerate(zip(srcs, lands)):
        hr = src.shape[1] // 2
        sends.append(_remote(src.at[:, pl.ds((1 - c) * hr, hr)], land, send_sems.at[w], recv_sems.at[w], sibling))
        recvs.append(_remote(land, land, send_sems.at[w], recv_sems.at[w], sibling))
    return sends, recvs


def _all_peers_copies(srcs, lands, send_sems, recv_sems):
    x, y, c = _mesh_pos()
    (src,), (land,) = srcs, lands
    flip = lambda v, bit: 1 - v if bit else v
    sends, recvs = [], []
    for k in range(N_DEV - 1):
        px, py, pc = flip(x, (k + 1) & 4), flip(y, (k + 1) & 2), flip(c, (k + 1) & 1)
        sends.append(_remote(src, land.at[4 * x + 2 * y + c], send_sems.at[k], recv_sems.at[k], (px, py, pc)))
        got = land.at[4 * px + 2 * py + pc]
        recvs.append(_remote(got, got, send_sems.at[k], recv_sems.at[k], (px, py, pc)))
    return sends, recvs


def _split_start(name, copies_of, srcs, land_shapes, n_copies=None):
    n = len(srcs)
    k = 3 * n if n_copies is None else n_copies

    def body(*refs):
        src_refs, land_refs = refs[:n], refs[n:2 * n]
        send_sems, recv_sems = refs[2 * n], refs[2 * n + 1]
        token = refs[-1]
        sends, _ = copies_of(src_refs, land_refs, send_sems, recv_sems)
        for cp in sends:
            cp.start()
        token[...] = jnp.zeros_like(token)

    lands = [_hbm(s) for s in land_shapes]
    dma = pltpu.SemaphoreType.DMA
    res = pl.pallas_call(
        body, name=name,
        out_shape=(dma((k,)), dma((k,)), *[pltpu.HBM(s.shape, s.dtype) for s in srcs],
                   *[pltpu.HBM(s.shape, s.dtype) for s in land_shapes], jax.ShapeDtypeStruct((8, 128), F32)),
        in_specs=[HBM] * (2 * n),
        out_specs=(SEM, SEM, *([HBM] * (2 * n)), pl.BlockSpec(memory_space=pltpu.VMEM)),
        input_output_aliases={i: 2 + i for i in range(2 * n)},
        compiler_params=SPLIT_COPY,
    )(*[_hbm(s) for s in srcs], *lands)
    return res[0], res[1], list(res[2:2 + n]), list(res[2 + n:2 + 2 * n]), res[-1]


def _split_wait(name, copies_of, send_sems, recv_sems, srcs, lands, after):
    n = len(srcs)

    def body(*refs):
        src_refs, land_refs = refs[:n], refs[n:2 * n]
        sends, recvs = copies_of(src_refs, land_refs, refs[2 * n], refs[2 * n + 1])
        for cp in sends:
            cp.wait_send()
        for cp in recvs:
            cp.wait_recv()

    res = pl.pallas_call(
        body, name=name,
        out_shape=tuple(pltpu.HBM(s.shape, s.dtype) for s in list(srcs) + list(lands)),
        in_specs=[HBM] * (2 * n) + [SEM, SEM] + [pl.BlockSpec(memory_space=pl.ANY)] * len(after),
        out_specs=tuple([HBM] * (2 * n)),
        input_output_aliases={i: i for i in range(2 * n)},
        compiler_params=SPLIT_COPY,
    )(*srcs, *lands, send_sems, recv_sems, *after)
    return list(res[:n]), list(res[n:])


def _gather_finish(lands):
    n = len(lands)

    def body(*refs):
        outs = refs[n:2 * n]
        d2d_send, d2d_recv = refs[2 * n:]
        x, y, c = _mesh_pos()
        chips = _other_chips(x, y)
        sibling = (x, y, 1 - c)
        sends = []
        for w, out in enumerate(outs):
            hr = out.shape[1] // 2
            for j, (px, py) in enumerate(chips):
                landed = out.at[2 * px + py, pl.ds(c * hr, hr)]
                cp = _remote(landed, landed, d2d_send.at[3 * w + j], d2d_recv.at[3 * w + j], sibling)
                cp.start()
                sends.append(cp)
        for w, out in enumerate(outs):
            hr = out.shape[1] // 2
            for j, (px, py) in enumerate(chips):
                other = out.at[2 * px + py, pl.ds((1 - c) * hr, hr)]
                _remote(other, other, d2d_send.at[3 * w + j], d2d_recv.at[3 * w + j], sibling).wait_recv()
        for cp in sends:
            cp.wait_send()

    dma = pltpu.SemaphoreType.DMA
    return pl.pallas_call(
        body, name="gather_finish",
        out_shape=[jax.ShapeDtypeStruct(l.shape, l.dtype) for l in lands],
        in_specs=[HBM] * n, out_specs=[HBM] * n,
        input_output_aliases={i: i for i in range(n)},
        scratch_shapes=[dma((3 * n,)), dma((3 * n,))],
    )(*lands)


def _adamw(w, g, m, v):
    m = ADAM_B1 * m + (1.0 - ADAM_B1) * g
    v = ADAM_B2 * v + (1.0 - ADAM_B2) * (g * g)
    m_hat = m / (1.0 - ADAM_B1 ** ADAM_STEP)
    v_hat = v / (1.0 - ADAM_B2 ** ADAM_STEP)
    delta = -ADAM_LR * (m_hat / (jnp.sqrt(v_hat) + ADAM_EPS) + ADAM_WD * w)
    return delta, m, v


def _adamw_big(partials, w, m, v):
    rows, cols = w.shape
    tr = 256 if rows % 256 == 0 else rows

    def body(p_ref, w_ref, m_ref, v_ref, g_ref, d_ref, m2_ref, v2_ref):
        g = ((p_ref[0].astype(F32) + p_ref[1].astype(F32)) + p_ref[2].astype(F32)) + p_ref[3].astype(F32)
        g_ref[...] = g
        d_ref[...], m2_ref[...], v2_ref[...] = _adamw(w_ref[...], g, m_ref[...], v_ref[...])

    blk = pl.BlockSpec((tr, cols), lambda i: (i, 0))
    return pl.pallas_call(
        body, name="adamw_big", grid=(rows // tr,),
        in_specs=[pl.BlockSpec((N_CHIPS, tr, cols), lambda i: (0, i, 0)), blk, blk, blk],
        out_specs=[blk] * 4, out_shape=[jax.ShapeDtypeStruct((rows, cols), F32)] * 4,
        compiler_params=_params("parallel"),
    )(partials, w, m, v)


def _sum_devices(gathered, rows):
    cols = gathered.shape[1]

    def body(g_ref, o_ref):
        acc = g_ref[0:rows]
        for d in range(1, N_DEV):
            acc = acc + g_ref[d * rows:(d + 1) * rows]
        o_ref[...] = acc

    return pl.pallas_call(
        body, name="sum_devices", out_shape=jax.ShapeDtypeStruct((rows, cols), F32),
        in_specs=[pl.BlockSpec(memory_space=pltpu.VMEM)], out_specs=pl.BlockSpec(memory_space=pltpu.VMEM),
        compiler_params=pltpu.CompilerParams(vmem_limit_bytes=VMEM_LIMIT_V7X),
    )(gathered)


def _adamw_small(quads):
    n = len(quads)

    def body(*refs):
        ins, outs = refs[:4 * n], refs[4 * n:]
        for t in range(n):
            w, g, m, v = (r[...] for r in ins[4 * t:4 * t + 4])
            outs[3 * t][...], outs[3 * t + 1][...], outs[3 * t + 2][...] = _adamw(w, g, m, v)

    flat = [a for q in quads for a in q]
    vm = pl.BlockSpec(memory_space=pltpu.VMEM)
    res = pl.pallas_call(
        body, name="adamw_small",
        out_shape=[jax.ShapeDtypeStruct(q[0].shape, F32) for q in quads for _ in range(3)],
        in_specs=[vm] * (4 * n), out_specs=[vm] * (3 * n),
    )(*flat)
    return [tuple(res[3 * t:3 * t + 3]) for t in range(n)]


SMALL_PACK_ROWS = 96
_WEIGHTS = ['meta_tokens', 'g_pre_mix', 'w_in', 'conv_w', 'conv_b', 'w_a', 'b_a', 'w_x', 'b_x', 'lru_lambda',
            'attn_sinks', 'w_out', 'g_post_mix', 'g_pre_ffn', 'w_ff1', 'w_ff2', 'g_post_ffn']
_BIG = ['w_in', 'w_out', 'w_ff1', 'w_ff2']


def _pack_small(dmeta, g, loss):
    z = lambda r, c: jnp.zeros((r, c), F32)
    rows = [
        dmeta,
        g['g_pre_mix'], g['g_post_mix'], g['g_pre_ffn'], g['g_post_ffn'],
        jnp.concatenate([g['conv_w'], z(4, 512)], axis=1),
        jnp.concatenate([g['conv_b'], g['b_a']], axis=1),
        jnp.concatenate([g['b_x'], g['lru_lambda']], axis=1),
        jnp.concatenate([g['attn_sinks'], z(1, D_MODEL - ATTN_HEADS)], axis=1),
        jnp.concatenate([loss, z(1, D_MODEL - 1)], axis=1),
        z(4, D_MODEL),
        g['w_a'].reshape(32, D_MODEL), g['w_x'].reshape(32, D_MODEL),
    ]
    return jnp.concatenate(rows, axis=0)


def _unpack_small(s, chip):
    return dict(
        meta_tokens=lax.dynamic_slice(s[0:16], (0, chip * 256), (16, 256)),
        g_pre_mix=s[16:17], g_post_mix=s[17:18], g_pre_ffn=s[18:19], g_post_ffn=s[19:20],
        conv_w=lax.dynamic_slice(s[20:24], (0, chip * 128), (4, 128)).reshape(1, 4, 128),
        conv_b=s[24:25, :512], b_a=s[24:25, 512:], b_x=s[25:26, :512], lru_lambda=s[25:26, 512:],
        attn_sinks=s[26:27, :ATTN_HEADS], loss=s[27, 0],
        w_a=s[32:64].reshape(1, LRU_BLOCKS, LRU_BLOCK, LRU_BLOCK),
        w_x=s[64:96].reshape(1, LRU_BLOCKS, LRU_BLOCK, LRU_BLOCK))


def _as2d(a):
    if a.ndim == 2:
        return a
    return a.reshape(-1, a.shape[-1])


def kernel(x, meta_tokens, g_pre_mix, w_in, conv_w, conv_b, w_a, b_a, w_x, b_x, lru_lambda, attn_sinks, w_out, g_post_mix, g_pre_ffn, w_ff1, w_ff2, g_post_ffn, loss_target, m_meta_tokens, m_g_pre_mix, m_w_in, m_conv_w, m_conv_b, m_w_a, m_b_a, m_w_x, m_b_x, m_lru_lambda, m_attn_sinks, m_w_out, m_g_post_mix, m_g_pre_ffn, m_w_ff1, m_w_ff2, m_g_post_ffn, v_meta_tokens, v_g_pre_mix, v_w_in, v_conv_w, v_conv_b, v_w_a, v_b_a, v_w_x, v_b_x, v_lru_lambda, v_attn_sinks, v_w_out, v_g_post_mix, v_g_pre_ffn, v_w_ff1, v_w_ff2, v_g_post_ffn):
    weights = dict(meta_tokens=meta_tokens, g_pre_mix=g_pre_mix, w_in=w_in, conv_w=conv_w, conv_b=conv_b, w_a=w_a,
                   b_a=b_a, w_x=w_x, b_x=b_x, lru_lambda=lru_lambda, attn_sinks=attn_sinks, w_out=w_out,
                   g_post_mix=g_post_mix, g_pre_ffn=g_pre_ffn, w_ff1=w_ff1, w_ff2=w_ff2, g_post_ffn=g_post_ffn)
    mom1 = dict(zip(_WEIGHTS, [m_meta_tokens, m_g_pre_mix, m_w_in, m_conv_w, m_conv_b, m_w_a, m_b_a, m_w_x, m_b_x,
                               m_lru_lambda, m_attn_sinks, m_w_out, m_g_post_mix, m_g_pre_ffn, m_w_ff1, m_w_ff2,
                               m_g_post_ffn]))
    mom2 = dict(zip(_WEIGHTS, [v_meta_tokens, v_g_pre_mix, v_w_in, v_conv_w, v_conv_b, v_w_a, v_b_a, v_w_x, v_b_x,
                               v_lru_lambda, v_attn_sinks, v_w_out, v_g_post_mix, v_g_pre_ffn, v_w_ff1, v_w_ff2,
                               v_g_post_ffn]))
    xi, yi, ci = _mesh_pos()
    chip = 2 * xi + yi

    tiny = jnp.concatenate([meta_tokens, jnp.pad(conv_w[0], ((0, 4), (0, 128)))], axis=0)
    chip_arr = jnp.reshape(chip, (1,)).astype(jnp.int32)
    big2d = lambda a, name: a[0].T if name == 'w_in' else a[0]
    shards, lands = zip(*[_prep_shard(big2d(weights[n], n), chip_arr) for n in _BIG])
    g_in, g_tiny = _gather_weights(shards[:1], lands[:1], tiny, _prep_tiny(tiny, chip_arr))
    w_in_full = g_in.reshape(IN_WIDTH, D_MODEL)
    meta_full = jnp.concatenate([g_tiny[j, :N_META] for j in range(N_CHIPS)], axis=1)
    conv_w_full = jnp.concatenate([g_tiny[j, N_META:N_META + 4, :128] for j in range(N_CHIPS)], axis=1)
    g_send, g_recv, late_thru, late_lands, token = _split_start(
        "gather_late_start", _gather_copies, shards[1:], lands[1:])

    def late_weights(after):
        _, landed = _split_wait("gather_late_wait", _gather_copies, g_send, g_recv, late_thru, late_lands, after)
        g_out, g_f1, g_f2 = _gather_finish(landed)
        return g_out.reshape(D_MODEL, D_MODEL), g_f1, g_f2

    pos = jnp.stack([ci, chip]).astype(jnp.int32)
    ffn = {}


    def on_ffn_grads(dw1, dw2):
        parts = [dw1, dw2]
        lands = [lax.empty((p.shape[0], p.shape[1] // 2, p.shape[2]), p.dtype) for p in parts]
        ffn['sib'] = _split_start("sibling_ffn_start", _sibling_copies, parts, lands, len(parts))
        return ffn['sib'][4]

    def on_outproj_bwd(dattn):
        send, recv, thru, lands, _ = ffn['sib']
        parts, from_sibling = _split_wait("sibling_ffn_wait", _sibling_copies, send, recv, thru, lands, [dattn])
        cparts_ffn, lands_ffn = zip(*[_chip_presum(p, r, pos) for p, r in zip(parts, from_sibling)])
        ffn['send'], ffn['recv'], ffn['thru'], ffn['lands'], token3 = _split_start(
            "scatter_ffn_start", _scatter_copies, cparts_ffn, lands_ffn)
        return token3

    def on_mixer_grads(dw_in, dw_out):
        parts = [dw_in.reshape(N_CHIPS, IN_WIDTH // N_CHIPS, D_MODEL),
                 dw_out.reshape(N_CHIPS, D_MODEL // N_CHIPS, D_MODEL)]
        cparts, lands = zip(*[_chip_presum(p, r, pos) for p, r in zip(parts, _sibling_exchange(parts, pos))])
        ffn['mixer'] = _split_start("scatter_mixer_start", _scatter_copies, cparts, lands)
        return ffn['mixer'][4]

    head = jnp.concatenate([jnp.zeros((PAD_ROWS, D_MODEL), F32), meta_full], axis=0)
    loss, dx, dhead, grads = _local_step(head, x[0], loss_target[0], g_pre_mix, w_in_full, conv_w_full, conv_b, w_a[0],
                                         b_a, w_x[0], b_x, lru_lambda, attn_sinks, g_post_mix, g_pre_ffn, g_post_ffn,
                                         late_weights, on_ffn_grads, on_outproj_bwd, on_mixer_grads, token)
    grad_x = dx[None]

    pack = _pack_small(dhead[PAD_ROWS:], grads, loss)
    dev = jnp.reshape(4 * xi + 2 * yi + ci, (1,)).astype(jnp.int32)
    s_send, s_recv, s_thru, s_lands, token5 = _split_start(
        "gather_small_start", _all_peers_copies, [pack], [_prep_tiny(pack, dev, N_DEV)], N_DEV - 1)

    send, recv, thru, lands, _ = ffn['mixer']
    mixer_cparts, mixer_lands = _split_wait("scatter_mixer_wait", _scatter_copies, send, recv, thru, lands, [token5])
    ffn_cparts, ffn_lands = _split_wait("scatter_ffn_wait", _scatter_copies, ffn['send'], ffn['recv'], ffn['thru'],
                                        ffn['lands'], mixer_lands)
    chip_partials = _scatter_partials([], [], mixer_cparts + ffn_cparts, mixer_lands + ffn_lands)

    g_out_d, delta, new_m, new_v = {}, {}, {}, {}
    for name, part in zip(_BIG, chip_partials):
        shp = weights[name].shape
        res = _adamw_big(part, big2d(weights[name], name), big2d(mom1[name], name), big2d(mom2[name], name))
        g_out_d[name], delta[name], new_m[name], new_v[name] = (big2d(r[None], name).reshape(shp) for r in res)

    _, (gathered,) = _split_wait("gather_small_wait", _all_peers_copies, s_send, s_recv, s_thru, s_lands,
                                 [g_out_d[n] for n in _BIG])
    small = _unpack_small(_sum_devices(gathered.reshape(N_DEV * SMALL_PACK_ROWS, D_MODEL), SMALL_PACK_ROWS), chip)
    loss = small['loss']
    small_names = [n for n in _WEIGHTS if n not in _BIG]
    quads = [(_as2d(weights[n]), _as2d(small[n]), _as2d(mom1[n]), _as2d(mom2[n])) for n in small_names]
    for name, (d, m2, v2) in zip(small_names, _adamw_small(quads)):
        shp = weights[name].shape
        g_out_d[name] = small[name].reshape(shp)
        delta[name], new_m[name], new_v[name] = d.reshape(shp), m2.reshape(shp), v2.reshape(shp)

    return (loss, grad_x, *[g_out_d[n] for n in _WEIGHTS], *[delta[n] for n in _WEIGHTS],
            *[new_m[n] for n in _WEIGHTS], *[new_v[n] for n in _WEIGHTS])
```

```python
import numpy as np
import jax
import jax.numpy as jnp
from jax import lax
from jax.experimental import pallas as pl
from jax.experimental.pallas import tpu as pltpu

F32 = jnp.float32
BF16 = jnp.bfloat16

D_MODEL = 1024
N_META = 16
BLOCK = 128
PAD_ROWS = BLOCK - N_META
HEAD_DIM = 64
ATTN_HEADS = 8
GQA_GROUP = 4
ATTN_WIDTH = 512
KV_WIDTH = 128
QKV_WIDTH = ATTN_WIDTH + 2 * KV_WIDTH
LRU_WIDTH = 512
LRU_BLOCKS = 8
LRU_BLOCK = 64
LRU_C = 8.0
IN_WIDTH = 1792
D_FF = 4096
N_CHIPS = 4
FF_CHUNK = D_FF // N_CHIPS
EPS = 1e-6
NEG = -1e30

ADAM_LR = 0.001
ADAM_B1 = 0.9
ADAM_B2 = 0.999
ADAM_EPS = 1e-08
ADAM_WD = 0.01
ADAM_STEP = 10

VMEM_LIMIT_V7X = 62 * 1024 * 1024
MESH = pl.DeviceIdType.MESH

NT = (((1,), (1,)), ((), ()))
TN = (((0,), (0,)), ((), ()))


def _row_tile(tp):
    return 640 if tp % 640 == 0 else BLOCK


def _wgrad_row_tile(tp):
    return 1664 if tp % 1664 == 0 else _row_tile(tp)


def _params(*sem):
    return pltpu.CompilerParams(dimension_semantics=sem, vmem_limit_bytes=VMEM_LIMIT_V7X)


def _dot(a, b):
    return jnp.dot(a, b, preferred_element_type=F32)


def _dot_nt(a, b):
    return lax.dot_general(a, b, NT, preferred_element_type=F32)


def _dot_tn(a, b):
    return lax.dot_general(a, b, TN, preferred_element_type=F32)


def _rms(x):
    rs = lax.rsqrt(jnp.mean(x * x, axis=-1, keepdims=True) + EPS)
    return x * rs, rs


def _rms_bwd(xhat, rs, g, dy):
    dyg = dy * g
    dx = rs * (dyg - xhat * jnp.mean(dyg * xhat, axis=-1, keepdims=True))
    dg = jnp.sum(dy * xhat, axis=0, keepdims=True)
    return dx, dg


def _gelu(x):
    k = 0.7978845608028654
    t = jnp.tanh(x * (k + (k * 0.044715) * (x * x)))
    return (0.5 * x) * (1.0 + t), t


def _gelu_grad(x, t):
    k = 0.7978845608028654
    return 0.5 * (1.0 + t) + 0.5 * x * (1.0 - t * t) * k * (1.0 + 3 * 0.044715 * x * x)


def _sigmoid(x):
    return 0.5 * jnp.tanh(0.5 * x) + 0.5


def _one_minus_exp2(y):
    t = jnp.tanh(y)
    return (-2.0 * t) / (1.0 - t)


def _softplus(x):
    return jnp.maximum(x, 0.0) + jnp.log1p(jnp.exp(-jnp.abs(x)))


def _seq_specs(tr, delay=0):
    qb = tr // BLOCK
    tile = lambda i: jnp.maximum(i - delay, 0)
    return [pl.BlockSpec((BLOCK, D_MODEL), lambda i, *_, s=s: (jnp.maximum(tile(i) * qb + s - 1, 0), 0))
            for s in range(qb)]


def _seq_tile(head, pieces, i):
    first = jnp.where(i == 0, head, pieces[0][...])
    return jnp.concatenate([first] + [p[...] for p in pieces[1:]], axis=0)


def _inproj_fwd(head, x, g, w_in, token):
    tp = BLOCK + x.shape[0]
    tr = _row_tile(tp)
    qb = tr // BLOCK

    def body(*refs):
        head_ref, pieces = refs[0], refs[1:1 + qb]
        g_ref, w_ref, _, u_ref, qkv_ref, xr_ref, yr_ref = refs[1 + qb:]
        xhat, _ = _rms(_seq_tile(head_ref[...], pieces, pl.program_id(0)))
        u = (xhat * g_ref[...]).astype(BF16)
        u_ref[...] = u
        z = _dot_nt(u, w_ref[...])
        qkv_ref[...] = z[:, :QKV_WIDTH].astype(BF16)
        xr_ref[...] = z[:, QKV_WIDTH:QKV_WIDTH + LRU_WIDTH]
        yr_ref[...] = z[:, QKV_WIDTH + LRU_WIDTH:]

    row = lambda w: pl.BlockSpec((tr, w), lambda i: (i, 0))
    full = lambda a: pl.BlockSpec(a.shape, lambda i: (0,) * a.ndim)
    return pl.pallas_call(
        body, name="inproj_fwd", grid=(tp // tr,),
        in_specs=[full(head)] + _seq_specs(tr) + [full(g), full(w_in), full(token)],
        out_specs=[row(D_MODEL), row(QKV_WIDTH), row(LRU_WIDTH), row(LRU_WIDTH)],
        out_shape=[jax.ShapeDtypeStruct((tp, D_MODEL), BF16), jax.ShapeDtypeStruct((tp, QKV_WIDTH), BF16),
                   jax.ShapeDtypeStruct((tp, LRU_WIDTH), F32), jax.ShapeDtypeStruct((tp, LRU_WIDTH), F32)],
        compiler_params=_params("parallel"),
    )(head, *([x] * qb), g, w_in, token)


GROUP_ROWS = GQA_GROUP * BLOCK


def _attn_bias():
    j = np.arange(2 * BLOCK)[:, None]
    i = np.arange(BLOCK)[None, :]
    band = (j - i >= 1) & (j - i <= BLOCK)
    out = []
    for n in range(3):
        ok = band & ((n - 1) * BLOCK + j >= PAD_ROWS) if n < 2 else band
        out.append(np.tile(np.where(ok, 0.0, NEG).astype(np.float32), (1, GQA_GROUP)))
    return jnp.asarray(np.stack(out))


def _stack_heads(a, g):
    heads = range(GQA_GROUP * g, GQA_GROUP * (g + 1))
    return jnp.concatenate([a[:, h * HEAD_DIM:(h + 1) * HEAD_DIM] for h in heads], axis=0)


def _unstack_heads(groups):
    return jnp.concatenate([p[h * BLOCK:(h + 1) * BLOCK] for p in groups for h in range(GQA_GROUP)], axis=1)


def _attn_probs_t(k_g, qg, bias, sink_row):
    st = _dot_nt(k_g, qg) + bias
    m = jnp.maximum(jnp.max(st, axis=0, keepdims=True), sink_row)
    p = jnp.exp(st - m)
    es = jnp.exp(sink_row - m)
    inv = 1.0 / (jnp.sum(p, axis=0, keepdims=True) + es)
    return p * inv, es * inv


def _attn_consts(sinks):
    return jnp.repeat(sinks.reshape(ATTN_HEADS), BLOCK).reshape(ATTN_HEADS // GQA_GROUP, GROUP_ROWS), _attn_bias()


_SINK_SPEC = pl.BlockSpec((ATTN_HEADS // GQA_GROUP, GROUP_ROWS), lambda n: (0, 0))
_BIAS_SPEC = pl.BlockSpec((3, 2 * BLOCK, GROUP_ROWS), lambda n: (0, 0, 0))
_QSCALE = HEAD_DIM ** -0.5


def _kv_specs(tr):
    qb = tr // BLOCK
    prev = lambda col: pl.BlockSpec((BLOCK, KV_WIDTH), lambda t: (jnp.maximum(t * qb - 1, 0), col))
    cur = lambda col: pl.BlockSpec((tr, KV_WIDTH), lambda t: (t, col))
    return [prev(4), cur(4), prev(5), cur(5)]


def _block_bias(b_ref, t, qb, i):
    return b_ref[2] if i >= 2 else b_ref[jnp.minimum(t * qb + i, 2)]


N_KV = ATTN_HEADS // GQA_GROUP


def _prob_specs(qb):
    return [pl.BlockSpec((qb, N_KV, 2 * BLOCK, GROUP_ROWS), lambda t: (t, 0, 0, 0)),
            pl.BlockSpec((qb, SUBLANES, GROUP_ROWS), lambda t: (t, 0, 0))]


def _attn_fwd(qkv, sinks):
    tp = qkv.shape[0]
    tr = _row_tile(tp)
    qb, nb = tr // BLOCK, tp // BLOCK
    sink_rows, bias = _attn_consts(sinks)

    def body(s_ref, b_ref, q_ref, kp_ref, kc_ref, vp_ref, vc_ref, o_ref, p_ref, ps_ref):
        t = pl.program_id(0)
        k_all = jnp.concatenate([kp_ref[...], kc_ref[...]], axis=0)
        v_all = jnp.concatenate([vp_ref[...], vc_ref[...]], axis=0)
        for i in range(qb):
            rows = slice(i * BLOCK, (i + 1) * BLOCK)
            q = q_ref[rows]
            k2, v2 = k_all[i * BLOCK:(i + 2) * BLOCK], v_all[i * BLOCK:(i + 2) * BLOCK]
            bias_n = _block_bias(b_ref, t, qb, i)
            outs, sink_probs = [], []
            for g in range(N_KV):
                cols = slice(g * HEAD_DIM, (g + 1) * HEAD_DIM)
                qg = _stack_heads(q, g) * jnp.asarray(_QSCALE, BF16)
                p, ps = _attn_probs_t(k2[:, cols], qg, bias_n, s_ref[g:g + 1])
                pb = p.astype(BF16)
                p_ref[i, g] = pb
                sink_probs.append(ps)
                outs.append(_dot_tn(pb, v2[:, cols]))
            o_ref[rows] = _unstack_heads(outs).astype(BF16)
            ps_ref[i] = jnp.concatenate(sink_probs + [jnp.zeros((SUBLANES - N_KV, GROUP_ROWS), F32)], axis=0)

    return pl.pallas_call(
        body, name="attn_fwd", grid=(tp // tr,),
        in_specs=[_SINK_SPEC, _BIAS_SPEC, pl.BlockSpec((tr, ATTN_WIDTH), lambda t: (t, 0))] + _kv_specs(tr),
        out_specs=[pl.BlockSpec((tr, ATTN_WIDTH), lambda t: (t, 0))] + _prob_specs(qb),
        out_shape=[jax.ShapeDtypeStruct((tp, ATTN_WIDTH), BF16),
                   jax.ShapeDtypeStruct((nb, N_KV, 2 * BLOCK, GROUP_ROWS), BF16),
                   jax.ShapeDtypeStruct((nb, SUBLANES, GROUP_ROWS), F32)],
        compiler_params=_params("parallel"),
    )(sink_rows, bias, qkv, qkv, qkv, qkv, qkv)


def _conv_taps(x, halo):
    ext = jnp.concatenate([halo, x], axis=0)
    return [ext[8:] if k == 3 else pltpu.roll(ext, 3 - k, 0)[8:] for k in range(4)]


def _lru_gates(xc, wa, ba, wx, bx, sp):
    xb = xc.astype(BF16)
    r = _sigmoid(_dot(xb, wa) + ba)
    ig = _sigmoid(_dot(xb, wx) + bx)
    log_a = (-LRU_C * sp) * r
    a = jnp.exp(log_a)
    mult = jnp.sqrt(_one_minus_exp2(log_a))
    return xb, r, ig, a, mult


SUBLANES = 8


def _scan_fwd(a, b, h_in):
    n, width = a.shape
    a, b = (v.reshape(n // SUBLANES, SUBLANES, width) for v in (a, b))
    in_group = lax.broadcasted_iota(jnp.int32, a.shape, 1)
    for d in (1, 2, 4):
        keep = in_group >= d
        b = jnp.where(keep, a * pltpu.roll(b, d, 1) + b, b)
        a = jnp.where(keep, a * pltpu.roll(a, d, 1), a)
    a, b = a.reshape(n, width), b.reshape(n, width)
    out, carry = [], h_in
    for g in range(0, n, SUBLANES):
        h = a[g:g + SUBLANES] * carry + b[g:g + SUBLANES]
        out.append(h)
        carry = h[SUBLANES - 1:]
    return jnp.concatenate(out, axis=0)


def _scan_rev(c, b, g_in):
    n, width = c.shape
    c, b = (v.reshape(n // SUBLANES, SUBLANES, width) for v in (c, b))
    in_group = lax.broadcasted_iota(jnp.int32, c.shape, 1)
    for d in (1, 2, 4):
        keep = in_group < SUBLANES - d
        b = jnp.where(keep, b + c * pltpu.roll(b, SUBLANES - d, 1), b)
        c = jnp.where(keep, c * pltpu.roll(c, SUBLANES - d, 1), c)
    c, b = c.reshape(n, width), b.reshape(n, width)
    out, carry = [], g_in
    for g in range(n - SUBLANES, -1, -SUBLANES):
        r = b[g:g + SUBLANES] + c[g:g + SUBLANES] * carry
        out.append(r)
        carry = r[:1]
    return jnp.concatenate(out[::-1], axis=0)


def _lru_fwd(xr, yr, conv_w, conv_b, wa, ba, wx, bx, lam):
    tp = xr.shape[0]
    tr = _row_tile(tp)
    qb = tr // BLOCK

    def body(xr_ref, yr_ref, cw_ref, cb_ref, wa_ref, ba_ref, wx_ref, bx_ref, lam_ref, hr_ref, rec_ref, halo, hprev):
        t = pl.program_id(0)

        @pl.when(t == 0)
        def _():
            halo[...] = jnp.zeros_like(halo)
            hprev[...] = jnp.zeros_like(hprev)

        cw, cb = cw_ref[...], cb_ref[...]
        wa_m, ba_v, wx_m, bx_v = wa_ref[...], ba_ref[...], wx_ref[...], bx_ref[...]
        sp = _softplus(-lam_ref[...])
        before, h_last = halo[...], hprev[0:1]
        for i in range(qb):
            rows = slice(i * BLOCK, (i + 1) * BLOCK)
            x = xr_ref[rows]
            taps = _conv_taps(x, before)
            before = x[BLOCK - 8:]
            xc = cb + sum(cw[k:k + 1] * taps[k] for k in range(4))
            _, _, ig, a, mult = _lru_gates(xc, wa_m, ba_v, wx_m, bx_v, sp)
            u = mult * (ig * xc)
            if i == 0:
                pos = t * tr + lax.broadcasted_iota(jnp.int32, xc.shape, 0)
                u = jnp.where(pos >= PAD_ROWS, u, 0.0)
            h = _scan_fwd(a, u, h_last)
            h_last = h[BLOCK - 1:]
            hr_ref[rows] = h
            gl, _ = _gelu(yr_ref[rows])
            rec_ref[rows] = (gl * h).astype(BF16)
        halo[...] = before
        hprev[0:1] = h_last

    blk = pl.BlockSpec((tr, LRU_WIDTH), lambda t: (t, 0))
    full = lambda a: pl.BlockSpec(a.shape, lambda t: (0,) * a.ndim)
    small = [conv_w, conv_b, wa, ba, wx, bx, lam]
    return pl.pallas_call(
        body, name="lru_fwd", grid=(tp // tr,),
        in_specs=[blk, blk] + [full(a) for a in small],
        out_specs=[blk, blk],
        out_shape=[jax.ShapeDtypeStruct((tp, LRU_WIDTH), F32), jax.ShapeDtypeStruct((tp, LRU_WIDTH), BF16)],
        scratch_shapes=[pltpu.VMEM((8, LRU_WIDTH), F32), pltpu.VMEM((8, LRU_WIDTH), F32)],
        compiler_params=_params("arbitrary"),
    )(xr, yr, *small)


def _inproj_lru_fwd(head, x, g, w_in, conv_w, conv_b, wa, ba, wx, bx, lam, token):
    tp = BLOCK + x.shape[0]
    tr = _row_tile(tp)
    qb, nt = tr // BLOCK, tp // tr
    small = [conv_w, conv_b, wa, ba, wx, bx, lam]

    def body(*refs):
        head_ref, pieces = refs[0], refs[1:1 + qb]
        g_ref, w_ref, _, cw_ref, cb_ref, wa_ref, ba_ref, wx_ref, bx_ref, lam_ref = refs[1 + qb:11 + qb]
        u_ref, qkv_ref, xr_ref, yr_ref, hr_ref, rec_ref, zbuf, halo, hprev = refs[11 + qb:]
        i = pl.program_id(0)
        cur = i % 2

        @pl.when(i == 0)
        def _():
            halo[...] = jnp.zeros_like(halo)
            hprev[...] = jnp.zeros_like(hprev)
            zbuf[1] = jnp.zeros((tr, 2 * LRU_WIDTH), F32)

        def recurrent_branch(valid):
            cw, cb = cw_ref[...], cb_ref[...]
            wa_m, ba_v, wx_m, bx_v = wa_ref[...], ba_ref[...], wx_ref[...], bx_ref[...]
            sp = _softplus(-lam_ref[...])
            before, h_last = halo[...], hprev[0:1]
            for b in range(qb):
                rows = slice(b * BLOCK, (b + 1) * BLOCK)
                xy = zbuf[1 - cur, rows]
                xin = xy[:, :LRU_WIDTH]
                taps = _conv_taps(xin, before)
                before = xin[BLOCK - 8:]
                xc = cb + sum(cw[k:k + 1] * taps[k] for k in range(4))
                _, _, ig, a, mult = _lru_gates(xc, wa_m, ba_v, wx_m, bx_v, sp)
                u = mult * (ig * xc)
                if b == 0:
                    pos = (i - 1) * tr + lax.broadcasted_iota(jnp.int32, xc.shape, 0)
                    u = jnp.where(pos >= PAD_ROWS, u, 0.0)
                h = _scan_fwd(a, u, h_last)
                h_last = h[BLOCK - 1:]
                hr_ref[rows] = h
                gl, _ = _gelu(xy[:, LRU_WIDTH:])
                rec_ref[rows] = (gl * h).astype(BF16)
            halo[...] = jnp.where(valid, before, 0.0)
            hprev[0:1] = jnp.where(valid, h_last, 0.0)

        def projection():
            xhat, _ = _rms(_seq_tile(head_ref[...], pieces, i))
            u = (xhat * g_ref[...]).astype(BF16)
            u_ref[...] = u
            z = _dot_nt(u, w_ref[...])
            qkv_ref[...] = z[:, :QKV_WIDTH].astype(BF16)
            xr_ref[...] = z[:, QKV_WIDTH:QKV_WIDTH + LRU_WIDTH]
            yr_ref[...] = z[:, QKV_WIDTH + LRU_WIDTH:]
            zbuf[cur] = z[:, QKV_WIDTH:]

        @pl.when(i < nt)
        def _():
            recurrent_branch(i >= 1)
            projection()

        @pl.when(i == nt)
        def _():
            recurrent_branch(True)

    last = nt - 1
    this_row = lambda w: pl.BlockSpec((tr, w), lambda i: (jnp.minimum(i, last), 0))
    prev_row = lambda w: pl.BlockSpec((tr, w), lambda i: (jnp.maximum(i - 1, 0), 0))
    full = lambda a: pl.BlockSpec(a.shape, lambda i: (0,) * a.ndim)
    piece_specs = [pl.BlockSpec((BLOCK, D_MODEL), lambda i, s=s: (jnp.maximum(jnp.minimum(i, last) * qb + s - 1, 0), 0))
                   for s in range(qb)]
    return pl.pallas_call(
        body, name="inproj_lru_fwd", grid=(nt + 1,),
        in_specs=[full(head)] + piece_specs + [full(g), full(w_in), full(token)] + [full(a) for a in small],
        out_specs=[this_row(D_MODEL), this_row(QKV_WIDTH), this_row(LRU_WIDTH), this_row(LRU_WIDTH),
                   prev_row(LRU_WIDTH), prev_row(LRU_WIDTH)],
        out_shape=[jax.ShapeDtypeStruct((tp, D_MODEL), BF16), jax.ShapeDtypeStruct((tp, QKV_WIDTH), BF16),
                   jax.ShapeDtypeStruct((tp, LRU_WIDTH), F32), jax.ShapeDtypeStruct((tp, LRU_WIDTH), F32),
                   jax.ShapeDtypeStruct((tp, LRU_WIDTH), F32), jax.ShapeDtypeStruct((tp, LRU_WIDTH), BF16)],
        scratch_shapes=[pltpu.VMEM((2, tr, 2 * LRU_WIDTH), F32), pltpu.VMEM((8, LRU_WIDTH), F32),
                        pltpu.VMEM((8, LRU_WIDTH), F32)],
        compiler_params=_params("arbitrary"),
    )(head, *([x] * qb), g, w_in, token, *small)


def _outproj_fwd(attn, rec, w_out, head, x, g_post_mix, g_pre_ffn):
    tp = attn.shape[0]
    tr = _row_tile(tp)
    qb = tr // BLOCK

    def body(*refs):
        a_ref, r_ref, w_ref, head_ref = refs[:4]
        pieces = refs[4:4 + qb]
        gm_ref, gf_ref, mix_ref, h1_ref, u1_ref = refs[4 + qb:]
        mix = _dot(a_ref[...], w_ref[:ATTN_WIDTH]) + _dot(r_ref[...], w_ref[ATTN_WIDTH:])
        mix_ref[...] = mix
        mhat, _ = _rms(mix)
        h1 = _seq_tile(head_ref[...], pieces, pl.program_id(0)) + mhat * gm_ref[...]
        h1_ref[...] = h1
        hhat, _ = _rms(h1)
        u1_ref[...] = (hhat * gf_ref[...]).astype(BF16)

    row = lambda w: pl.BlockSpec((tr, w), lambda i: (i, 0))
    full = lambda a: pl.BlockSpec(a.shape, lambda i: (0,) * a.ndim)
    return pl.pallas_call(
        body, name="outproj_fwd", grid=(tp // tr,),
        in_specs=[row(ATTN_WIDTH), row(LRU_WIDTH), full(w_out), full(head)] + _seq_specs(tr)
        + [full(g_post_mix), full(g_pre_ffn)],
        out_specs=[row(D_MODEL), row(D_MODEL), row(D_MODEL)],
        out_shape=[jax.ShapeDtypeStruct((tp, D_MODEL), F32), jax.ShapeDtypeStruct((tp, D_MODEL), F32),
                   jax.ShapeDtypeStruct((tp, D_MODEL), BF16)],
        compiler_params=_params("parallel"),
    )(attn, rec, w_out, head, *([x] * qb), g_post_mix, g_pre_ffn)


def _resident(a):
    return pl.BlockSpec(a.shape, lambda *_: (0,) * a.ndim, pipeline_mode=pl.Buffered(1))


def _ffn_fwd(u1, w1, w2, h1, tgt, g_post_ffn):
    tp = h1.shape[0]
    tr = _row_tile(tp)
    qb, nt = tr // BLOCK, tp // tr
    sr = tr // N_CHIPS

    def body(*refs):
        u_ref, w1_ref, w2_ref, h1_ref = refs[:4]
        t_pieces = refs[4:4 + qb]
        g_ref, r1_ref, dy_ref, df2_ref, loss_ref, dg_ref, acc = refs[4 + qb:]
        i, c = pl.program_id(0), pl.program_id(1)
        cur = i % 2

        @pl.when((i == 0) & (c == 0))
        def _():
            loss_ref[...] = jnp.zeros_like(loss_ref)
            dg_ref[...] = jnp.zeros_like(dg_ref)
            acc[1] = jnp.zeros((tr, D_MODEL), F32)

        def matmuls():
            r = jnp.maximum(_dot(u_ref[...], w1_ref[c]), 0.0)
            r1_ref[...] = r.astype(BF16)
            return _dot((r * r).astype(BF16), w2_ref[c])

        def finish_previous_tile(k, valid):
            lo, hi = k * sr, (k + 1) * sr
            g = g_ref[...]
            fhat, rs = _rms(acc[1 - cur, lo:hi])
            h2 = h1_ref[lo:hi] + fhat * g
            rows = (i - 1) * tr + lo + lax.broadcasted_iota(jnp.int32, h2.shape, 0)
            tgt = jnp.concatenate([p[max(lo - s * BLOCK, 0):min(hi - s * BLOCK, BLOCK)] for s, p in enumerate(t_pieces)
                                   if lo < (s + 1) * BLOCK and hi > s * BLOCK], axis=0)
            err = jnp.where((rows >= BLOCK) & valid, h2 - tgt, 0.0)
            dy = err * (1.0 / D_MODEL)
            dy_ref[lo:hi] = dy
            loss_ref[...] += (0.5 / D_MODEL) * jnp.sum(err * err)
            df2, dg = _rms_bwd(fhat, rs, g, dy)
            df2_ref[lo:hi] = df2.astype(BF16)
            dg_ref[...] += dg

        for k in range(N_CHIPS):
            @pl.when((c == k) & (i < nt))
            def _(k=k):
                finish_previous_tile(k, i >= 1)
                if k == 0:
                    acc[cur] = matmuls()
                else:
                    acc[cur] += matmuls()

            @pl.when((c == k) & (i == nt))
            def _(k=k):
                finish_previous_tile(k, True)

    last = nt - 1
    this_row = pl.BlockSpec((tr, D_MODEL), lambda i, c: (jnp.minimum(i, last), 0))
    prev_row = pl.BlockSpec((tr, D_MODEL), lambda i, c: (jnp.maximum(i - 1, 0), 0))
    full = lambda a: pl.BlockSpec(a.shape, lambda i, c: (0,) * a.ndim)
    return pl.pallas_call(
        body, name="ffn_fwd", grid=(nt + 1, N_CHIPS),
        in_specs=[this_row, _resident(w1), _resident(w2), prev_row] + _seq_specs(tr, delay=1) + [full(g_post_ffn)],
        out_specs=[pl.BlockSpec((tr, FF_CHUNK), lambda i, c: (jnp.minimum(i, last), jnp.where(i < nt, c, N_CHIPS - 1))),
                   prev_row, prev_row,
                   pl.BlockSpec((1, 1), lambda i, c: (0, 0)), pl.BlockSpec((1, D_MODEL), lambda i, c: (0, 0))],
        out_shape=[jax.ShapeDtypeStruct((tp, D_FF), BF16), jax.ShapeDtypeStruct((tp, D_MODEL), F32),
                   jax.ShapeDtypeStruct((tp, D_MODEL), BF16), jax.ShapeDtypeStruct((1, 1), F32),
                   jax.ShapeDtypeStruct((1, D_MODEL), F32)],
        scratch_shapes=[pltpu.VMEM((2, tr, D_MODEL), F32)],
        compiler_params=_params("arbitrary", "arbitrary"),
    )(u1, w1, w2, h1, *([tgt] * qb), g_post_ffn)


def _ffn_bwd_data(df2, r1, w1, w2, dy, h1, mix, g_pre_ffn, g_post_mix):
    tp = h1.shape[0]
    tr = _row_tile(tp)
    nt = tp // tr
    sr = tr // N_CHIPS

    def body(df2_ref, r1_ref, w1_ref, w2_ref, dy_ref, h1_ref, mix_ref, gf_ref, gm_ref,
             da_ref, dh1_ref, dmix_ref, dgf_ref, dgm_ref, acc):
        i, c = pl.program_id(0), pl.program_id(1)
        cur = i % 2

        @pl.when((i == 0) & (c == 0))
        def _():
            dgf_ref[...] = jnp.zeros_like(dgf_ref)
            dgm_ref[...] = jnp.zeros_like(dgm_ref)
            acc[1] = jnp.zeros((tr, D_MODEL), F32)

        def matmuls():
            df = _dot_nt(df2_ref[...], w2_ref[c])
            da = (df * (2.0 * r1_ref[...].astype(F32))).astype(BF16)
            da_ref[...] = da
            return _dot_nt(da, w1_ref[c])

        def finish_previous_tile(k, valid):
            lo, hi = k * sr, (k + 1) * sr
            hhat, rs = _rms(h1_ref[lo:hi])
            dx, dgf = _rms_bwd(hhat, rs, gf_ref[...], acc[1 - cur, lo:hi])
            dh1 = dy_ref[lo:hi] + dx
            dh1_ref[lo:hi] = dh1
            mhat, rsm = _rms(mix_ref[lo:hi])
            dmix, dgm = _rms_bwd(mhat, rsm, gm_ref[...], dh1)
            dmix_ref[lo:hi] = dmix.astype(BF16)
            dgf_ref[...] += jnp.where(valid, dgf, 0.0)
            dgm_ref[...] += jnp.where(valid, dgm, 0.0)

        for k in range(N_CHIPS):
            @pl.when((c == k) & (i < nt))
            def _(k=k):
                finish_previous_tile(k, i >= 1)
                if k == 0:
                    acc[cur] = matmuls()
                else:
                    acc[cur] += matmuls()

            @pl.when((c == k) & (i == nt))
            def _(k=k):
                finish_previous_tile(k, True)

    last = nt - 1
    this_row = pl.BlockSpec((tr, D_MODEL), lambda i, c: (jnp.minimum(i, last), 0))
    prev_row = pl.BlockSpec((tr, D_MODEL), lambda i, c: (jnp.maximum(i - 1, 0), 0))
    chunk = pl.BlockSpec((tr, FF_CHUNK), lambda i, c: (jnp.minimum(i, last), jnp.where(i < nt, c, N_CHIPS - 1)))
    gain = pl.BlockSpec((1, D_MODEL), lambda i, c: (0, 0))
    return pl.pallas_call(
        body, name="ffn_bwd_data", grid=(nt + 1, N_CHIPS),
        in_specs=[this_row, chunk, _resident(w1), _resident(w2), prev_row, prev_row, prev_row, gain, gain],
        out_specs=[chunk, prev_row, prev_row, gain, gain],
        out_shape=[jax.ShapeDtypeStruct((tp, D_FF), BF16), jax.ShapeDtypeStruct((tp, D_MODEL), F32),
                   jax.ShapeDtypeStruct((tp, D_MODEL), BF16), jax.ShapeDtypeStruct((1, D_MODEL), F32),
                   jax.ShapeDtypeStruct((1, D_MODEL), F32)],
        scratch_shapes=[pltpu.VMEM((2, tr, D_MODEL), F32)],
        compiler_params=_params("arbitrary", "arbitrary"),
    )(df2, r1, w1, w2, dy, h1, mix, g_pre_ffn, g_post_mix)


def _ffn_bwd_weights(u1, da1, r1, df2):
    tp = u1.shape[0]
    tr = _wgrad_row_tile(tp)

    def body(u_ref, da_ref, r1_ref, df2_ref, dw1_ref, dw2_ref):
        i = pl.program_id(1)
        r = r1_ref[...].astype(F32)
        p1 = _dot_tn(u_ref[...], da_ref[...])
        p2 = _dot_tn((r * r).astype(BF16), df2_ref[...])

        @pl.when(i == 0)
        def _():
            dw1_ref[0] = p1
            dw2_ref[0] = p2

        @pl.when(i > 0)
        def _():
            dw1_ref[0] += p1
            dw2_ref[0] += p2

    row = pl.BlockSpec((tr, D_MODEL), lambda c, i: (i, 0))
    chunk = pl.BlockSpec((tr, FF_CHUNK), lambda c, i: (i, c))
    return pl.pallas_call(
        body, name="ffn_bwd_weights", grid=(N_CHIPS, tp // tr),
        in_specs=[row, chunk, chunk, row],
        out_specs=[pl.BlockSpec((1, D_MODEL, FF_CHUNK), lambda c, i: (c, 0, 0)),
                   pl.BlockSpec((1, FF_CHUNK, D_MODEL), lambda c, i: (c, 0, 0))],
        out_shape=[jax.ShapeDtypeStruct((N_CHIPS, D_MODEL, FF_CHUNK), F32),
                   jax.ShapeDtypeStruct((N_CHIPS, FF_CHUNK, D_MODEL), F32)],
        compiler_params=_params("parallel", "arbitrary"),
    )(u1, da1, r1, df2)


def _outproj_bwd(dmix, w_out, attn, rec, token):
    tp = dmix.shape[0]
    tr = _wgrad_row_tile(tp)

    def body(dm_ref, w_ref, a_ref, r_ref, _, da_ref, dr_ref, dw_ref):
        i = pl.program_id(0)
        dm = dm_ref[...]
        dcat = _dot_nt(dm, w_ref[...])
        da_ref[...] = dcat[:, :ATTN_WIDTH].astype(BF16)
        dr_ref[...] = dcat[:, ATTN_WIDTH:]
        pa = _dot_tn(a_ref[...], dm)
        pr = _dot_tn(r_ref[...], dm)

        @pl.when(i == 0)
        def _():
            dw_ref[:ATTN_WIDTH] = pa
            dw_ref[ATTN_WIDTH:] = pr

        @pl.when(i > 0)
        def _():
            dw_ref[:ATTN_WIDTH] += pa
            dw_ref[ATTN_WIDTH:] += pr

    row = lambda w: pl.BlockSpec((tr, w), lambda i: (i, 0))
    full = pl.BlockSpec((D_MODEL, D_MODEL), lambda i: (0, 0))
    return pl.pallas_call(
        body, name="outproj_bwd", grid=(tp // tr,),
        in_specs=[row(D_MODEL), full, row(ATTN_WIDTH), row(LRU_WIDTH), pl.BlockSpec(token.shape, lambda i: (0, 0))],
        out_specs=[row(ATTN_WIDTH), row(LRU_WIDTH), full],
        out_shape=[jax.ShapeDtypeStruct((tp, ATTN_WIDTH), BF16), jax.ShapeDtypeStruct((tp, LRU_WIDTH), F32),
                   jax.ShapeDtypeStruct((D_MODEL, D_MODEL), F32)],
        compiler_params=_params("arbitrary"),
    )(dmix, w_out, attn, rec, token)


N_VEC_ROWS = 8


def _lru_bwd(xr, yr, hr, drec, conv_w, conv_b, wa, ba, wx, bx, lam, token):
    tp = xr.shape[0]
    tr = _row_tile(tp)
    qb, nt = tr // BLOCK, tp // tr

    def body(xr_ref, xh_ref, yr_ref, hr_ref, hp_ref, dr_ref, cw_ref, cb_ref, wa_ref, ba_ref, wx_ref, bx_ref, lam_ref, _,
             dxr_ref, dyr_ref, dwa_ref, dwx_ref, vec_ref, g_next, a_next, dxc_next, dsp):
        s = pl.program_id(0)
        t = nt - 1 - s

        @pl.when(s == 0)
        def _():
            g_next[...] = jnp.zeros_like(g_next)
            a_next[...] = jnp.zeros_like(a_next)
            dxc_next[...] = jnp.zeros_like(dxc_next)
            dsp[...] = jnp.zeros_like(dsp)
            dwa_ref[...] = jnp.zeros_like(dwa_ref)
            dwx_ref[...] = jnp.zeros_like(dwx_ref)
            vec_ref[...] = jnp.zeros_like(vec_ref)

        first_tile = t == 0
        cw, cb = cw_ref[...], cb_ref[...]
        lam_v = lam_ref[...]
        sp = _softplus(-lam_v)
        wa_m, ba_v, wx_m, bx_v = wa_ref[...], ba_ref[...], wx_ref[...], bx_ref[...]
        rows = lax.broadcasted_iota(jnp.int32, (BLOCK, LRU_WIDTH), 0)
        col = lambda v: jnp.sum(v, axis=0, keepdims=True)

        g_after, a_after, dxc_after = g_next[0:1], a_next[0:1], dxc_next[...]
        xbs, dgrs, dgis = [], [], []
        vec = [jnp.zeros((1, LRU_WIDTH), F32) for _ in range(N_VEC_ROWS)]
        for i in reversed(range(qb)):
            blk = slice(i * BLOCK, (i + 1) * BLOCK)
            if i == 0:
                x_before = jnp.where(first_tile, 0.0, xh_ref[...])
                h_before = jnp.where(first_tile, 0.0, hp_ref[7:8])
            else:
                x_before = xr_ref[i * BLOCK - 8:i * BLOCK]
                h_before = hr_ref[i * BLOCK - 1:i * BLOCK]
            taps = _conv_taps(xr_ref[blk], x_before)
            xc = cb + sum(cw[k:k + 1] * taps[k] for k in range(4))
            xb, r, ig, a, mult = _lru_gates(xc, wa_m, ba_v, wx_m, bx_v, sp)

            yr_v = yr_ref[blk]
            gl, th = _gelu(yr_v)
            h = hr_ref[blk]
            drec = dr_ref[blk]
            dyr_ref[blk] = (drec * h * _gelu_grad(yr_v, th)).astype(BF16)

            a_up = jnp.where(rows == BLOCK - 1, a_after, pltpu.roll(a, BLOCK - 1, 0))
            g = _scan_rev(a_up, drec * gl, g_after)
            g_after, a_after = g[0:1], a[0:1]

            h_prev = jnp.where(rows == 0, h_before, pltpu.roll(h, 1, 0))
            du, da = g, g * h_prev
            if i == 0:
                real = (t * tr + rows) >= PAD_ROWS
                du, da = jnp.where(real, du, 0.0), jnp.where(real, da, 0.0)
            dmult = du * (ig * xc)
            dig = du * (mult * xc)
            dxc = du * (mult * ig)
            dlog_a = da * a - dmult * (a * a / mult)
            if i == 0:
                dlog_a = jnp.where(real, dlog_a, 0.0)
            dgr = (dlog_a * (-LRU_C * sp)) * (r * (1.0 - r))
            dgi = dig * (ig * (1.0 - ig))
            dgr_b, dgi_b = dgr.astype(BF16), dgi.astype(BF16)
            dxc = dxc + _dot_nt(dgr_b, wa_m) + _dot_nt(dgi_b, wx_m)
            xbs.append(xb)
            dgrs.append(dgr_b)
            dgis.append(dgi_b)

            ext = jnp.concatenate([dxc, dxc_after], axis=0)
            up = [ext[:BLOCK] if j == 0 else pltpu.roll(ext, BLOCK + 8 - j, 0)[:BLOCK] for j in range(4)]
            dxr_ref[blk] = sum(cw[k:k + 1] * up[3 - k] for k in range(4)).astype(BF16)
            dxc_after = dxc[:8]

            for k in range(4):
                vec[k] = vec[k] + col(dxc * taps[k])
            vec[4] = vec[4] + col(dxc)
            vec[5] = vec[5] + col(dgr)
            vec[6] = vec[6] + col(dgi)
            vec[7] = vec[7] + col(dlog_a * (-LRU_C * r))

        g_next[0:1], a_next[0:1], dxc_next[...] = g_after, a_after, dxc_after
        xb_all = jnp.concatenate(xbs, axis=0)
        dwa_ref[...] += _dot_tn(xb_all, jnp.concatenate(dgrs, axis=0))
        dwx_ref[...] += _dot_tn(xb_all, jnp.concatenate(dgis, axis=0))
        for k in range(7):
            vec_ref[k:k + 1] += vec[k]
        dsp[0:1] += vec[7]

        @pl.when(s == nt - 1)
        def _():
            vec_ref[7:8] = dsp[0:1] * (-_sigmoid(-lam_v))

    blk_spec = pl.BlockSpec((tr, LRU_WIDTH), lambda s: (nt - 1 - s, 0))
    rows_before = pl.BlockSpec((8, LRU_WIDTH), lambda s: (jnp.maximum((nt - 1 - s) * (tr // 8) - 1, 0), 0))
    full = lambda a: pl.BlockSpec(a.shape, lambda s: (0,) * a.ndim)
    small = [conv_w, conv_b, wa, ba, wx, bx, lam, token]
    sq = pl.BlockSpec((LRU_WIDTH, LRU_WIDTH), lambda s: (0, 0))
    return pl.pallas_call(
        body, name="lru_bwd", grid=(nt,),
        in_specs=[blk_spec, rows_before, blk_spec, blk_spec, rows_before, blk_spec] + [full(a) for a in small],
        out_specs=[blk_spec, blk_spec, sq, sq, pl.BlockSpec((N_VEC_ROWS, LRU_WIDTH), lambda s: (0, 0))],
        out_shape=[jax.ShapeDtypeStruct((tp, LRU_WIDTH), BF16), jax.ShapeDtypeStruct((tp, LRU_WIDTH), BF16),
                   jax.ShapeDtypeStruct((LRU_WIDTH, LRU_WIDTH), F32), jax.ShapeDtypeStruct((LRU_WIDTH, LRU_WIDTH), F32),
                   jax.ShapeDtypeStruct((N_VEC_ROWS, LRU_WIDTH), F32)],
        scratch_shapes=[pltpu.VMEM((8, LRU_WIDTH), F32)] * 4,
        compiler_params=_params("arbitrary"),
    )(xr, xr, yr, hr, hr, drec, *small)


def _attn_bwd(qkv, dattn, probs, sink_probs):
    tp = qkv.shape[0]
    tr = _row_tile(tp)
    qb, nt = tr // BLOCK, tp // tr
    n_groups = N_KV

    def body(p_ref, ps_ref, q_ref, kp_ref, kc_ref, vp_ref, vc_ref, do_ref, dq_ref, dkv_ref, ex_ref, ds_ref, dsink):
        t = pl.program_id(0)

        @pl.when(t == 0)
        def _():
            dsink[...] = jnp.zeros_like(dsink)

        k_all = jnp.concatenate([kp_ref[...], kc_ref[...]], axis=0)
        v_all = jnp.concatenate([vp_ref[...], vc_ref[...]], axis=0)
        tail = None
        for i in range(qb):
            rows = slice(i * BLOCK, (i + 1) * BLOCK)
            q, do = q_ref[rows], do_ref[rows]
            k2, v2 = k_all[i * BLOCK:(i + 2) * BLOCK], v_all[i * BLOCK:(i + 2) * BLOCK]
            dqs, dks, dvs = [], [], []
            for g in range(n_groups):
                cols = slice(g * HEAD_DIM, (g + 1) * HEAD_DIM)
                k_g, v_g = k2[:, cols], v2[:, cols]
                qg = _stack_heads(q, g) * jnp.asarray(_QSCALE, BF16)
                dog = _stack_heads(do, g)
                pb = p_ref[i, g]
                p = pb.astype(F32)
                dpt = _dot_nt(v_g, dog)
                delta = jnp.sum(p * dpt, axis=0, keepdims=True)
                dst = (p * (dpt - delta)).astype(BF16)
                dqs.append(_dot_tn(dst, k_g) * _QSCALE)
                dks.append(_dot(dst, qg))
                dvs.append(_dot(pb, dog))
                dsink[g:g + 1] -= ps_ref[i, g:g + 1] * delta
            dq_ref[rows] = _unstack_heads(dqs).astype(BF16)
            dkv = jnp.concatenate(dks + dvs, axis=1)
            if i == 0:
                ex_ref[0] = dkv[:BLOCK]
            else:
                dkv_ref[(i - 1) * BLOCK:i * BLOCK] = (tail + dkv[:BLOCK]).astype(BF16)
            tail = dkv[BLOCK:]
        dkv_ref[(qb - 1) * BLOCK:] = tail.astype(BF16)

        @pl.when(t == nt - 1)
        def _():
            lane = lax.broadcasted_iota(jnp.int32, (1, ATTN_HEADS), 1)
            acc = jnp.zeros((1, ATTN_HEADS), F32)
            for h in range(ATTN_HEADS):
                g, hh = divmod(h, GQA_GROUP)
                acc = acc + jnp.where(lane == h, jnp.sum(dsink[g:g + 1, hh * BLOCK:(hh + 1) * BLOCK]), 0.0)
            ds_ref[...] = acc

    cur = lambda w: pl.BlockSpec((tr, w), lambda t: (t, 0))
    return pl.pallas_call(
        body, name="attn_bwd", grid=(nt,),
        in_specs=_prob_specs(qb) + [cur(ATTN_WIDTH)] + _kv_specs(tr) + [cur(ATTN_WIDTH)],
        out_specs=[cur(ATTN_WIDTH), cur(2 * KV_WIDTH), pl.BlockSpec((1, BLOCK, 2 * KV_WIDTH), lambda t: (t, 0, 0)),
                   pl.BlockSpec((1, ATTN_HEADS), lambda t: (0, 0))],
        out_shape=[jax.ShapeDtypeStruct((tp, ATTN_WIDTH), BF16), jax.ShapeDtypeStruct((tp, 2 * KV_WIDTH), BF16),
                   jax.ShapeDtypeStruct((nt, BLOCK, 2 * KV_WIDTH), F32), jax.ShapeDtypeStruct((1, ATTN_HEADS), F32)],
        scratch_shapes=[pltpu.VMEM((n_groups, GROUP_ROWS), F32)],
        compiler_params=_params("arbitrary"),
    )(probs, sink_probs, qkv, qkv, qkv, qkv, qkv, dattn)


def _fix_dkv(dkv, dkv_extra):
    tp = dkv.shape[0]
    tr = _row_tile(tp)
    nt, qb = tp // tr, tr // BLOCK
    if nt == 1:
        return dkv

    def body(d_ref, ex_ref, o_ref):
        o_ref[...] = (d_ref[...].astype(F32) + ex_ref[0]).astype(BF16)

    last = pl.BlockSpec((BLOCK, 2 * KV_WIDTH), lambda t: (t * qb + qb - 1, 0))
    return pl.pallas_call(
        body, name="fix_dkv", grid=(nt - 1,),
        in_specs=[last, pl.BlockSpec((1, BLOCK, 2 * KV_WIDTH), lambda t: (t + 1, 0, 0))],
        out_specs=last, out_shape=jax.ShapeDtypeStruct(dkv.shape, dkv.dtype),
        input_output_aliases={0: 0}, compiler_params=_params("parallel"),
    )(dkv, dkv_extra)


def _inproj_wgrad(dq, dkv, dxr, dyr, u0):
    tp = dq.shape[0]
    tr = _wgrad_row_tile(tp)

    def body(dq_ref, dkv_ref, dxr_ref, dyr_ref, u_ref, dw_ref):
        i = pl.program_id(0)
        dz = jnp.concatenate([dq_ref[...], dkv_ref[...], dxr_ref[...], dyr_ref[...]], axis=1)
        pw = _dot_tn(dz, u_ref[...])

        @pl.when(i == 0)
        def _():
            dw_ref[...] = pw

        @pl.when(i > 0)
        def _():
            dw_ref[...] += pw

    row = lambda w: pl.BlockSpec((tr, w), lambda i: (i, 0))
    return pl.pallas_call(
        body, name="inproj_wgrad", grid=(tp // tr,),
        in_specs=[row(ATTN_WIDTH), row(2 * KV_WIDTH), row(LRU_WIDTH), row(LRU_WIDTH), row(D_MODEL)],
        out_specs=pl.BlockSpec((IN_WIDTH, D_MODEL), lambda i: (0, 0)),
        out_shape=jax.ShapeDtypeStruct((IN_WIDTH, D_MODEL), F32),
        compiler_params=_params("arbitrary"),
    )(dq, dkv, dxr, dyr, u0)


def _inproj_dgrad(dq, dkv, dxr, dyr, w_in, head, x, dh1, g, token):
    tp = dq.shape[0]
    tr = _row_tile(tp)
    nt, qb = tp // tr, tr // BLOCK

    def body(*refs):
        dq_ref, dkv_ref, dxr_ref, dyr_ref, w_ref, head_ref = refs[:6]
        pieces = refs[6:6 + qb]
        dh1_ref, g_ref, _, gx_ref, dhead_ref, dg_ref, buf, sems = refs[6 + qb:]
        i = pl.program_id(0)
        slot = i % 2

        def out_copy(step, at):
            return pltpu.make_async_copy(buf.at[at], gx_ref.at[pl.ds(step * tr - BLOCK, tr)], sems.at[at])

        dz = jnp.concatenate([dq_ref[...], dkv_ref[...], dxr_ref[...], dyr_ref[...]], axis=1)
        du = _dot(dz, w_ref[...])
        hhat, rs = _rms(_seq_tile(head_ref[...], pieces, i))
        dx, dg = _rms_bwd(hhat, rs, g_ref[...], du)
        dh0 = dh1_ref[...] + dx

        @pl.when(i >= 3)
        def _():
            out_copy(i - 2, slot).wait()

        buf[slot] = dh0

        @pl.when(i == 0)
        def _():
            dg_ref[...] = dg
            dhead_ref[...] = dh0[:BLOCK]
            if tr > BLOCK:
                first = pltpu.make_async_copy(buf.at[0, pl.ds(BLOCK, tr - BLOCK)], gx_ref.at[pl.ds(0, tr - BLOCK)],
                                              sems.at[0])
                first.start()
                first.wait()

        @pl.when(i >= 1)
        def _():
            dg_ref[...] += dg
            out_copy(i, slot).start()

        @pl.when(i == nt - 1)
        def _():
            if nt >= 3:
                out_copy(nt - 2, (nt - 2) % 2).wait()
            if nt >= 2:
                out_copy(nt - 1, (nt - 1) % 2).wait()

    row = lambda w: pl.BlockSpec((tr, w), lambda i: (i, 0))
    full = lambda shape: pl.BlockSpec(shape, lambda i: (0,) * len(shape))
    return pl.pallas_call(
        body, name="inproj_dgrad", grid=(tp // tr,),
        in_specs=[row(ATTN_WIDTH), row(2 * KV_WIDTH), row(LRU_WIDTH), row(LRU_WIDTH), full(w_in.shape),
                  full(head.shape)] + _seq_specs(tr) + [row(D_MODEL), full(g.shape), full(token.shape)],
        out_specs=[pl.BlockSpec(memory_space=pl.ANY), full((BLOCK, D_MODEL)), full((1, D_MODEL))],
        out_shape=[jax.ShapeDtypeStruct(x.shape, F32), jax.ShapeDtypeStruct((BLOCK, D_MODEL), F32),
                   jax.ShapeDtypeStruct((1, D_MODEL), F32)],
        scratch_shapes=[pltpu.VMEM((2, tr, D_MODEL), F32), pltpu.SemaphoreType.DMA((2,))],
        compiler_params=_params("arbitrary"),
    )(dq, dkv, dxr, dyr, w_in, head, *([x] * qb), dh1, g, token)


def _dense_block_diag(w):
    eye = jnp.eye(LRU_BLOCKS, dtype=w.dtype)
    return (w[:, :, None, :] * eye[:, None, :, None]).reshape(LRU_WIDTH, LRU_WIDTH)


def _diag_blocks(dense):
    d4 = dense.reshape(LRU_BLOCKS, LRU_BLOCK, LRU_BLOCKS, LRU_BLOCK)
    return jnp.stack([d4[n, :, n, :] for n in range(LRU_BLOCKS)])


def _local_step(head, x, tgt, g_pre_mix, w_in, conv_w, conv_b, w_a, b_a, w_x, b_x, lam, sinks, g_post_mix,
                g_pre_ffn, g_post_ffn, late_weights, on_ffn_grads, on_outproj_bwd, on_mixer_grads, token):
    wa = _dense_block_diag(w_a).astype(BF16)
    wx = _dense_block_diag(w_x).astype(BF16)

    u0, qkv, xr, yr, hr, rec = _inproj_lru_fwd(head, x, g_pre_mix, w_in, conv_w, conv_b, wa, b_a, wx, b_x, lam, token)
    attn, probs, sink_probs = _attn_fwd(qkv, sinks)
    w_out, w1, w2 = late_weights([attn, rec])
    mix, h1, u1 = _outproj_fwd(attn, rec, w_out, head, x, g_post_mix, g_pre_ffn)
    r1, dy, df2, loss, dg_post_ffn = _ffn_fwd(u1, w1, w2, h1, tgt, g_post_ffn)

    da1, dh1, dmix, dg_pre_ffn, dg_post_mix = _ffn_bwd_data(df2, r1, w1, w2, dy, h1, mix, g_pre_ffn, g_post_mix)
    dw1, dw2 = _ffn_bwd_weights(u1, da1, r1, df2)
    token2 = on_ffn_grads(dw1, dw2)
    dattn, drec, dw_out = _outproj_bwd(dmix, w_out, attn, rec, token2)
    token3 = on_outproj_bwd(dattn)
    dxr, dyr, dwa, dwx, vec = _lru_bwd(xr, yr, hr, drec, conv_w, conv_b, wa, b_a, wx, b_x, lam, token3)
    dq, dkv, dkv_extra, dsinks = _attn_bwd(qkv, dattn, probs, sink_probs)
    dkv = _fix_dkv(dkv, dkv_extra)
    dw_in = _inproj_wgrad(dq, dkv, dxr, dyr, u0)
    token4 = on_mixer_grads(dw_in, dw_out)
    dx, dhead, dg_pre_mix = _inproj_dgrad(dq, dkv, dxr, dyr, w_in, head, x, dh1, g_pre_mix, token4)

    grads = dict(
        g_pre_mix=dg_pre_mix, conv_w=vec[0:4], conv_b=vec[4:5], w_a=_diag_blocks(dwa), b_a=vec[5:6],
        w_x=_diag_blocks(dwx), b_x=vec[6:7], lru_lambda=vec[7:8], attn_sinks=dsinks,
        g_post_mix=dg_post_mix, g_pre_ffn=dg_pre_ffn, g_post_ffn=dg_post_ffn)
    return loss, dx, dhead, grads


HBM = pl.BlockSpec(memory_space=pltpu.HBM)


def _mesh_pos():
    return lax.axis_index("x"), lax.axis_index("y"), lax.axis_index("c")


def _other_chips(x, y):
    return [(1 - x, y), (x, 1 - y), (1 - x, 1 - y)]


def _remote(src, dst, send_sem, recv_sem, to):
    return pltpu.make_async_remote_copy(src_ref=src, dst_ref=dst, send_sem=send_sem, recv_sem=recv_sem,
                                        device_id=to, device_id_type=MESH)


def _gather_weights(shards, lands, tiny, tiny_land):
    nbig = len(shards)

    def body(*refs):
        srcs, tiny_src = refs[:nbig], refs[nbig]
        outs, tiny_out = refs[2 * nbig + 2:3 * nbig + 2], refs[3 * nbig + 2]
        ici_send, ici_recv, d2d_send, d2d_recv, tiny_send, tiny_recv = refs[3 * nbig + 3:]
        x, y, c = _mesh_pos()
        me = 2 * x + y
        chips = _other_chips(x, y)
        sibling = (x, y, 1 - c)
        sends = []
        for w, (src, out) in enumerate(zip(srcs, outs)):
            hr = src.shape[0] // 2
            for j, chip in enumerate(chips):
                k = 3 * w + j
                cp = _remote(src.at[pl.ds(c * hr, hr)], out.at[me, pl.ds(c * hr, hr)],
                             ici_send.at[k], ici_recv.at[k], (*chip, c))
                cp.start()
                sends.append(cp)
        for j, chip in enumerate(chips):
            cp = _remote(tiny_src, tiny_out.at[me], tiny_send.at[j], tiny_recv.at[j], (*chip, c))
            cp.start()
            sends.append(cp)
        for w, (src, out) in enumerate(zip(srcs, outs)):
            hr = src.shape[0] // 2
            for j, (px, py) in enumerate(chips):
                k = 3 * w + j
                landed = out.at[2 * px + py, pl.ds(c * hr, hr)]
                _remote(landed, landed, ici_send.at[k], ici_recv.at[k], sibling).wait_recv()
                cp = _remote(landed, landed, d2d_send.at[k], d2d_recv.at[k], sibling)
                cp.start()
                sends.append(cp)
        for w, (src, out) in enumerate(zip(srcs, outs)):
            hr = src.shape[0] // 2
            for j, (px, py) in enumerate(chips):
                k = 3 * w + j
                other = out.at[2 * px + py, pl.ds((1 - c) * hr, hr)]
                _remote(other, other, d2d_send.at[k], d2d_recv.at[k], sibling).wait_recv()
        for j, (px, py) in enumerate(chips):
            blk = tiny_out.at[2 * px + py]
            _remote(blk, blk, tiny_send.at[j], tiny_recv.at[j], sibling).wait_recv()
        for cp in sends:
            cp.wait_send()

    out_shape = [jax.ShapeDtypeStruct(l.shape, l.dtype) for l in list(lands) + [tiny_land]]
    n = 3 * nbig
    return pl.pallas_call(
        body, name="gather_weights", out_shape=out_shape,
        in_specs=[HBM] * (2 * nbig + 2), out_specs=[HBM] * (nbig + 1),
        input_output_aliases={nbig + 1 + i: i for i in range(nbig + 1)},
        scratch_shapes=[pltpu.SemaphoreType.DMA((n,)),
                        pltpu.SemaphoreType.DMA((n,)), pltpu.SemaphoreType.DMA((n,)), pltpu.SemaphoreType.DMA((n,)),
                        pltpu.SemaphoreType.DMA((3,)), pltpu.SemaphoreType.DMA((3,))],
    )(*shards, tiny, *lands, tiny_land)


def _prep_shard(w, me):
    rows, cols = w.shape
    tr = 256 if rows % 256 == 0 else rows

    def body(me_ref, w_ref, s_ref, l_ref):
        b = w_ref[...].astype(BF16)
        s_ref[...] = b
        l_ref[0] = b

    return pl.pallas_call(
        body, name="prep_shard",
        grid_spec=pltpu.PrefetchScalarGridSpec(
            num_scalar_prefetch=1, grid=(rows // tr,),
            in_specs=[pl.BlockSpec((tr, cols), lambda i, me_ref: (i, 0))],
            out_specs=[pl.BlockSpec((tr, cols), lambda i, me_ref: (i, 0)),
                       pl.BlockSpec((1, tr, cols), lambda i, me_ref: (me_ref[0], i, 0))]),
        out_shape=[jax.ShapeDtypeStruct((rows, cols), BF16), jax.ShapeDtypeStruct((N_CHIPS, rows, cols), BF16)],
        compiler_params=_params("parallel"),
    )(me, w)


def _prep_tiny(tiny, me, slots=N_CHIPS):
    def body(me_ref, t_ref, l_ref):
        l_ref[0] = t_ref[...]

    return pl.pallas_call(
        body, name="prep_tiny",
        grid_spec=pltpu.PrefetchScalarGridSpec(
            num_scalar_prefetch=1, grid=(1,),
            in_specs=[pl.BlockSpec(tiny.shape, lambda i, me_ref: (0, 0))],
            out_specs=pl.BlockSpec((1,) + tiny.shape, lambda i, me_ref: (me_ref[0], 0, 0))),
        out_shape=jax.ShapeDtypeStruct((slots,) + tiny.shape, tiny.dtype),
    )(me, tiny)


N_DEV = 8


def _sibling_exchange(parts, token):
    def body(*refs):
        n = len(parts)
        srcs, outs, send_sems, recv_sems = refs[:n], refs[n + 1:2 * n + 1], refs[2 * n + 1], refs[2 * n + 2]
        x, y, c = _mesh_pos()
        sibling = (x, y, 1 - c)
        cps = []
        for w, (src, out) in enumerate(zip(srcs, outs)):
            hr = src.shape[1] // 2
            cp = _remote(src.at[:, pl.ds((1 - c) * hr, hr)], out, send_sems.at[w], recv_sems.at[w], sibling)
            cp.start()
            cps.append(cp)
        for cp in cps:
            cp.wait()

    n = len(parts)
    return pl.pallas_call(
        body, name="sibling_exchange",
        out_shape=[jax.ShapeDtypeStruct((p.shape[0], p.shape[1] // 2, p.shape[2]), p.dtype) for p in parts],
        in_specs=[HBM] * n + [pl.BlockSpec(memory_space=pl.ANY)], out_specs=[HBM] * n,
        scratch_shapes=[pltpu.SemaphoreType.DMA((n,)), pltpu.SemaphoreType.DMA((n,))],
    )(*parts, token)


def _chip_presum(part, from_sibling, pos):
    _, hr, cols = from_sibling.shape
    tr = 256 if hr % 256 == 0 else hr
    steps = hr // tr

    def body(pos_ref, a_ref, b_ref, o_ref, land_ref):
        s = (a_ref[...] + b_ref[...]).astype(BF16)
        o_ref[...] = s

        @pl.when(pl.program_id(1) == pos_ref[1])
        def _():
            land_ref[...] = s

    return pl.pallas_call(
        body, name="chip_presum",
        grid_spec=pltpu.PrefetchScalarGridSpec(
            num_scalar_prefetch=1, grid=(steps, N_CHIPS),
            in_specs=[pl.BlockSpec((1, tr, cols), lambda i, j, p: (j, p[0] * steps + i, 0)),
                      pl.BlockSpec((1, tr, cols), lambda i, j, p: (j, i, 0))],
            out_specs=[pl.BlockSpec((1, tr, cols), lambda i, j, p: (j, i, 0)),
                       pl.BlockSpec((1, tr, cols), lambda i, j, p: (p[1], p[0] * steps + i, 0))]),
        out_shape=[jax.ShapeDtypeStruct(from_sibling.shape, BF16),
                   jax.ShapeDtypeStruct((N_CHIPS, 2 * hr, cols), BF16)],
        compiler_params=_params("arbitrary", "arbitrary"),
    )(pos, part, from_sibling)


def _scatter_partials(cparts, lands, done_cparts=(), done_lands=()):
    n_new = len(cparts)
    nw = n_new + len(done_cparts)

    def body(*refs):
        srcs = refs[:nw]
        outs = refs[2 * nw:3 * nw]
        own_send, own_recv, ici_send, ici_recv, d2d_send, d2d_recv = refs[3 * nw:]
        x, y, c = _mesh_pos()
        me = 2 * x + y
        chips = _other_chips(x, y)
        sibling = (x, y, 1 - c)
        sends = []
        for w in list(range(n_new, nw)) + list(range(n_new)):
            src, out = srcs[w], outs[w]
            hr = src.shape[1]
            mine = out.at[me, pl.ds(c * hr, hr)]
            cp = _remote(src.at[me], mine, own_send.at[w], own_recv.at[w], sibling)
            cp.start()
            sends.append(cp)
            for j, (px, py) in enumerate(chips):
                if w >= n_new:
                    break
                k = 3 * w + j
                cp = _remote(src.at[2 * px + py], mine, ici_send.at[k], ici_recv.at[k], (px, py, c))
                cp.start()
                sends.append(cp)
        for w in list(range(n_new, nw)) + list(range(n_new)):
            src, out = srcs[w], outs[w]
            hr = src.shape[1]
            for j, (px, py) in enumerate(chips):
                k = 3 * w + j
                landed = out.at[2 * px + py, pl.ds(c * hr, hr)]
                if w < n_new:
                    _remote(landed, landed, ici_send.at[k], ici_recv.at[k], sibling).wait_recv()
                cp = _remote(landed, landed, d2d_send.at[k], d2d_recv.at[k], sibling)
                cp.start()
                sends.append(cp)
        for w, (src, out) in enumerate(zip(srcs, outs)):
            hr = src.shape[1]
            other = out.at[me, pl.ds((1 - c) * hr, hr)]
            _remote(other, other, own_send.at[w], own_recv.at[w], sibling).wait_recv()
            for j, (px, py) in enumerate(chips):
                k = 3 * w + j
                other = out.at[2 * px + py, pl.ds((1 - c) * hr, hr)]
                _remote(other, other, d2d_send.at[k], d2d_recv.at[k], sibling).wait_recv()
        for cp in sends:
            cp.wait_send()

    n = 3 * nw
    dma = pltpu.SemaphoreType.DMA
    every = list(cparts) + list(done_cparts)
    every_lands = list(lands) + list(done_lands)
    return pl.pallas_call(
        body, name="scatter_partials",
        out_shape=[jax.ShapeDtypeStruct(l.shape, l.dtype) for l in every_lands],
        in_specs=[HBM] * (2 * nw), out_specs=[HBM] * nw,
        input_output_aliases={nw + i: i for i in range(nw)},
        scratch_shapes=[dma((nw,)), dma((nw,)), dma((n,)), dma((n,)), dma((n,)), dma((n,))],
    )(*every, *every_lands)


SEM = pl.BlockSpec(memory_space=pltpu.SEMAPHORE)
SPLIT_COPY = pltpu.CompilerParams(has_side_effects=pltpu.SideEffectType.DATAFLOW_SIDE_EFFECTING)


def _hbm(a):
    return pltpu.with_memory_space_constraint(a, pltpu.HBM)


def _gather_copies(srcs, lands, send_sems, recv_sems):
    x, y, c = _mesh_pos()
    me = 2 * x + y
    sends, recvs = [], []
    for w, (src, land) in enumerate(zip(srcs, lands)):
        hr = src.shape[0] // 2
        for j, (px, py) in enumerate(_other_chips(x, y)):
            k = 3 * w + j
            sends.append(_remote(src.at[pl.ds(c * hr, hr)], land.at[me, pl.ds(c * hr, hr)],
                                 send_sems.at[k], recv_sems.at[k], (px, py, c)))
            got = land.at[2 * px + py, pl.ds(c * hr, hr)]
            recvs.append(_remote(got, got, send_sems.at[k], recv_sems.at[k], (px, py, c)))
    return sends, recvs


def _scatter_copies(srcs, lands, send_sems, recv_sems):
    x, y, c = _mesh_pos()
    me = 2 * x + y
    sends, recvs = [], []
    for w, (src, land) in enumerate(zip(srcs, lands)):
        hr = src.shape[1]
        for j, (px, py) in enumerate(_other_chips(x, y)):
            k = 3 * w + j
            sends.append(_remote(src.at[2 * px + py], land.at[me, pl.ds(c * hr, hr)],
                                 send_sems.at[k], recv_sems.at[k], (px, py, c)))
            got = land.at[2 * px + py, pl.ds(c * hr, hr)]
            recvs.append(_remote(got, got, send_sems.at[k], recv_sems.at[k], (px, py, c)))
    return sends, recvs


def _sibling_copies(srcs, lands, send_sems, recv_sems):
    x, y, c = _mesh_pos()
    sibling = (x, y, 1 - c)
    sends, recvs = [], []
    for w, (src, land) in enumerate(zip(srcs, lands)):
        hr = src.shape[1] // 2
        sends.append(_remote(src.at[:, pl.ds((1 - c) * hr, hr)], land, send_sems.at[w], recv_sems.at[w], sibling))
        recvs.append(_remote(land, land, send_sems.at[w], recv_sems.at[w], sibling))
    return sends, recvs


def _all_peers_copies(srcs, lands, send_sems, recv_sems):
    x, y, c = _mesh_pos()
    (src,), (land,) = srcs, lands
    flip = lambda v, bit: 1 - v if bit else v
    sends, recvs = [], []
    for k in range(N_DEV - 1):
        px, py, pc = flip(x, (k + 1) & 4), flip(y, (k + 1) & 2), flip(c, (k + 1) & 1)
        sends.append(_remote(src, land.at[4 * x + 2 * y + c], send_sems.at[k], recv_sems.at[k], (px, py, pc)))
        got = land.at[4 * px + 2 * py + pc]
        recvs.append(_remote(got, got, send_sems.at[k], recv_sems.at[k], (px, py, pc)))
    return sends, recvs


def _split_start(name, copies_of, srcs, land_shapes, n_copies=None):
    n = len(srcs)
    k = 3 * n if n_copies is None else n_copies

    def body(*refs):
        src_refs, land_refs = refs[:n], refs[n:2 * n]
        send_sems, recv_sems = refs[2 * n], refs[2 * n + 1]
        token = refs[-1]
        sends, _ = copies_of(src_refs, land_refs, send_sems, recv_sems)
        for cp in sends:
            cp.start()
        token[...] = jnp.zeros_like(token)

    lands = [_hbm(s) for s in land_shapes]
    dma = pltpu.SemaphoreType.DMA
    res = pl.pallas_call(
        body, name=name,
        out_shape=(dma((k,)), dma((k,)), *[pltpu.HBM(s.shape, s.dtype) for s in srcs],
                   *[pltpu.HBM(s.shape, s.dtype) for s in land_shapes], jax.ShapeDtypeStruct((8, 128), F32)),
        in_specs=[HBM] * (2 * n),
        out_specs=(SEM, SEM, *([HBM] * (2 * n)), pl.BlockSpec(memory_space=pltpu.VMEM)),
        input_output_aliases={i: 2 + i for i in range(2 * n)},
        compiler_params=SPLIT_COPY,
    )(*[_hbm(s) for s in srcs], *lands)
    return res[0], res[1], list(res[2:2 + n]), list(res[2 + n:2 + 2 * n]), res[-1]


def _split_wait(name, copies_of, send_sems, recv_sems, srcs, lands, after):
    n = len(srcs)

    def body(*refs):
        src_refs, land_refs = refs[:n], refs[n:2 * n]
        sends, recvs = copies_of(src_refs, land_refs, refs[2 * n], refs[2 * n + 1])
        for cp in sends:
            cp.wait_send()
        for cp in recvs:
            cp.wait_recv()

    res = pl.pallas_call(
        body, name=name,
        out_shape=tuple(pltpu.HBM(s.shape, s.dtype) for s in list(srcs) + list(lands)),
        in_specs=[HBM] * (2 * n) + [SEM, SEM] + [pl.BlockSpec(memory_space=pl.ANY)] * len(after),
        out_specs=tuple([HBM] * (2 * n)),
        input_output_aliases={i: i for i in range(2 * n)},
        compiler_params=SPLIT_COPY,
    )(*srcs, *lands, send_sems, recv_sems, *after)
    return list(res[:n]), list(res[n:])


def _gather_finish(lands):
    n = len(lands)

    def body(*refs):
        outs = refs[n:2 * n]
        d2d_send, d2d_recv = refs[2 * n:]
        x, y, c = _mesh_pos()
        chips = _other_chips(x, y)
        sibling = (x, y, 1 - c)
        sends = []
        for w, out in enumerate(outs):
            hr = out.shape[1] // 2
            for j, (px, py) in enumerate(chips):
                landed = out.at[2 * px + py, pl.ds(c * hr, hr)]
                cp = _remote(landed, landed, d2d_send.at[3 * w + j], d2d_recv.at[3 * w + j], sibling)
                cp.start()
                sends.append(cp)
        for w, out in enumerate(outs):
            hr = out.shape[1] // 2
            for j, (px, py) in enumerate(chips):
                other = out.at[2 * px + py, pl.ds((1 - c) * hr, hr)]
                _remote(other, other, d2d_send.at[3 * w + j], d2d_recv.at[3 * w + j], sibling).wait_recv()
        for cp in sends:
            cp.wait_send()

    dma = pltpu.SemaphoreType.DMA
    return pl.pallas_call(
        body, name="gather_finish",
        out_shape=[jax.ShapeDtypeStruct(l.shape, l.dtype) for l in lands],
        in_specs=[HBM] * n, out_specs=[HBM] * n,
        input_output_aliases={i: i for i in range(n)},
        scratch_shapes=[dma((3 * n,)), dma((3 * n,))],
    )(*lands)


def _adamw(w, g, m, v):
    m = ADAM_B1 * m + (1.0 - ADAM_B1) * g
    v = ADAM_B2 * v + (1.0 - ADAM_B2) * (g * g)
    m_hat = m / (1.0 - ADAM_B1 ** ADAM_STEP)
    v_hat = v / (1.0 - ADAM_B2 ** ADAM_STEP)
    delta = -ADAM_LR * (m_hat / (jnp.sqrt(v_hat) + ADAM_EPS) + ADAM_WD * w)
    return delta, m, v


def _adamw_big(partials, w, m, v):
    rows, cols = w.shape
    tr = 256 if rows % 256 == 0 else rows

    def body(p_ref, w_ref, m_ref, v_ref, g_ref, d_ref, m2_ref, v2_ref):
        g = ((p_ref[0].astype(F32) + p_ref[1].astype(F32)) + p_ref[2].astype(F32)) + p_ref[3].astype(F32)
        g_ref[...] = g
        d_ref[...], m2_ref[...], v2_ref[...] = _adamw(w_ref[...], g, m_ref[...], v_ref[...])

    blk = pl.BlockSpec((tr, cols), lambda i: (i, 0))
    return pl.pallas_call(
        body, name="adamw_big", grid=(rows // tr,),
        in_specs=[pl.BlockSpec((N_CHIPS, tr, cols), lambda i: (0, i, 0)), blk, blk, blk],
        out_specs=[blk] * 4, out_shape=[jax.ShapeDtypeStruct((rows, cols), F32)] * 4,
        compiler_params=_params("parallel"),
    )(partials, w, m, v)


def _sum_devices(gathered, rows):
    cols = gathered.shape[1]

    def body(g_ref, o_ref):
        acc = g_ref[0:rows]
        for d in range(1, N_DEV):
            acc = acc + g_ref[d * rows:(d + 1) * rows]
        o_ref[...] = acc

    return pl.pallas_call(
        body, name="sum_devices", out_shape=jax.ShapeDtypeStruct((rows, cols), F32),
        in_specs=[pl.BlockSpec(memory_space=pltpu.VMEM)], out_specs=pl.BlockSpec(memory_space=pltpu.VMEM),
        compiler_params=pltpu.CompilerParams(vmem_limit_bytes=VMEM_LIMIT_V7X),
    )(gathered)


def _adamw_small(quads):
    n = len(quads)

    def body(*refs):
        ins, outs = refs[:4 * n], refs[4 * n:]
        for t in range(n):
            w, g, m, v = (r[...] for r in ins[4 * t:4 * t + 4])
            outs[3 * t][...], outs[3 * t + 1][...], outs[3 * t + 2][...] = _adamw(w, g, m, v)

    flat = [a for q in quads for a in q]
    vm = pl.BlockSpec(memory_space=pltpu.VMEM)
    res = pl.pallas_call(
        body, name="adamw_small",
        out_shape=[jax.ShapeDtypeStruct(q[0].shape, F32) for q in quads for _ in range(3)],
        in_specs=[vm] * (4 * n), out_specs=[vm] * (3 * n),
    )(*flat)
    return [tuple(res[3 * t:3 * t + 3]) for t in range(n)]


SMALL_PACK_ROWS = 96
_WEIGHTS = ['meta_tokens', 'g_pre_mix', 'w_in', 'conv_w', 'conv_b', 'w_a', 'b_a', 'w_x', 'b_x', 'lru_lambda',
            'attn_sinks', 'w_out', 'g_post_mix', 'g_pre_ffn', 'w_ff1', 'w_ff2', 'g_post_ffn']
_BIG = ['w_in', 'w_out', 'w_ff1', 'w_ff2']


def _pack_small(dmeta, g, loss):
    z = lambda r, c: jnp.zeros((r, c), F32)
    rows = [
        dmeta,
        g['g_pre_mix'], g['g_post_mix'], g['g_pre_ffn'], g['g_post_ffn'],
        jnp.concatenate([g['conv_w'], z(4, 512)], axis=1),
        jnp.concatenate([g['conv_b'], g['b_a']], axis=1),
        jnp.concatenate([g['b_x'], g['lru_lambda']], axis=1),
        jnp.concatenate([g['attn_sinks'], z(1, D_MODEL - ATTN_HEADS)], axis=1),
        jnp.concatenate([loss, z(1, D_MODEL - 1)], axis=1),
        z(4, D_MODEL),
        g['w_a'].reshape(32, D_MODEL), g['w_x'].reshape(32, D_MODEL),
    ]
    return jnp.concatenate(rows, axis=0)


def _unpack_small(s, chip):
    return dict(
        meta_tokens=lax.dynamic_slice(s[0:16], (0, chip * 256), (16, 256)),
        g_pre_mix=s[16:17], g_post_mix=s[17:18], g_pre_ffn=s[18:19], g_post_ffn=s[19:20],
        conv_w=lax.dynamic_slice(s[20:24], (0, chip * 128), (4, 128)).reshape(1, 4, 128),
        conv_b=s[24:25, :512], b_a=s[24:25, 512:], b_x=s[25:26, :512], lru_lambda=s[25:26, 512:],
        attn_sinks=s[26:27, :ATTN_HEADS], loss=s[27, 0],
        w_a=s[32:64].reshape(1, LRU_BLOCKS, LRU_BLOCK, LRU_BLOCK),
        w_x=s[64:96].reshape(1, LRU_BLOCKS, LRU_BLOCK, LRU_BLOCK))


def _as2d(a):
    if a.ndim == 2:
        return a
    return a.reshape(-1, a.shape[-1])


def kernel(x, meta_tokens, g_pre_mix, w_in, conv_w, conv_b, w_a, b_a, w_x, b_x, lru_lambda, attn_sinks, w_out, g_post_mix, g_pre_ffn, w_ff1, w_ff2, g_post_ffn, loss_target, m_meta_tokens, m_g_pre_mix, m_w_in, m_conv_w, m_conv_b, m_w_a, m_b_a, m_w_x, m_b_x, m_lru_lambda, m_attn_sinks, m_w_out, m_g_post_mix, m_g_pre_ffn, m_w_ff1, m_w_ff2, m_g_post_ffn, v_meta_tokens, v_g_pre_mix, v_w_in, v_conv_w, v_conv_b, v_w_a, v_b_a, v_w_x, v_b_x, v_lru_lambda, v_attn_sinks, v_w_out, v_g_post_mix, v_g_pre_ffn, v_w_ff1, v_w_ff2, v_g_post_ffn):
    weights = dict(meta_tokens=meta_tokens, g_pre_mix=g_pre_mix, w_in=w_in, conv_w=conv_w, conv_b=conv_b, w_a=w_a,
                   b_a=b_a, w_x=w_x, b_x=b_x, lru_lambda=lru_lambda, attn_sinks=attn_sinks, w_out=w_out,
                   g_post_mix=g_post_mix, g_pre_ffn=g_pre_ffn, w_ff1=w_ff1, w_ff2=w_ff2, g_post_ffn=g_post_ffn)
    mom1 = dict(zip(_WEIGHTS, [m_meta_tokens, m_g_pre_mix, m_w_in, m_conv_w, m_conv_b, m_w_a, m_b_a, m_w_x, m_b_x,
                               m_lru_lambda, m_attn_sinks, m_w_out, m_g_post_mix, m_g_pre_ffn, m_w_ff1, m_w_ff2,
                               m_g_post_ffn]))
    mom2 = dict(zip(_WEIGHTS, [v_meta_tokens, v_g_pre_mix, v_w_in, v_conv_w, v_conv_b, v_w_a, v_b_a, v_w_x, v_b_x,
                               v_lru_lambda, v_attn_sinks, v_w_out, v_g_post_mix, v_g_pre_ffn, v_w_ff1, v_w_ff2,
                               v_g_post_ffn]))
    xi, yi, ci = _mesh_pos()
    chip = 2 * xi + yi

    tiny = jnp.concatenate([meta_tokens, jnp.pad(conv_w[0], ((0, 4), (0, 128)))], axis=0)
    chip_arr = jnp.reshape(chip, (1,)).astype(jnp.int32)
    big2d = lambda a, name: a[0].T if name == 'w_in' else a[0]
    shards, lands = zip(*[_prep_shard(big2d(weights[n], n), chip_arr) for n in _BIG])
    g_in, g_tiny = _gather_weights(shards[:1], lands[:1], tiny, _prep_tiny(tiny, chip_arr))
    w_in_full = g_in.reshape(IN_WIDTH, D_MODEL)
    meta_full = jnp.concatenate([g_tiny[j, :N_META] for j in range(N_CHIPS)], axis=1)
    conv_w_full = jnp.concatenate([g_tiny[j, N_META:N_META + 4, :128] for j in range(N_CHIPS)], axis=1)
    g_send, g_recv, late_thru, late_lands, token = _split_start(
        "gather_late_start", _gather_copies, shards[1:], lands[1:])

    def late_weights(after):
        _, landed = _split_wait("gather_late_wait", _gather_copies, g_send, g_recv, late_thru, late_lands, after)
        g_out, g_f1, g_f2 = _gather_finish(landed)
        return g_out.reshape(D_MODEL, D_MODEL), g_f1, g_f2

    pos = jnp.stack([ci, chip]).astype(jnp.int32)
    ffn = {}


    def on_ffn_grads(dw1, dw2):
        parts = [dw1, dw2]
        lands = [lax.empty((p.shape[0], p.shape[1] // 2, p.shape[2]), p.dtype) for p in parts]
        ffn['sib'] = _split_start("sibling_ffn_start", _sibling_copies, parts, lands, len(parts))
        return ffn['sib'][4]

    def on_outproj_bwd(dattn):
        send, recv, thru, lands, _ = ffn['sib']
        parts, from_sibling = _split_wait("sibling_ffn_wait", _sibling_copies, send, recv, thru, lands, [dattn])
        cparts_ffn, lands_ffn = zip(*[_chip_presum(p, r, pos) for p, r in zip(parts, from_sibling)])
        ffn['send'], ffn['recv'], ffn['thru'], ffn['lands'], token3 = _split_start(
            "scatter_ffn_start", _scatter_copies, cparts_ffn, lands_ffn)
        return token3

    def on_mixer_grads(dw_in, dw_out):
        parts = [dw_in.reshape(N_CHIPS, IN_WIDTH // N_CHIPS, D_MODEL),
                 dw_out.reshape(N_CHIPS, D_MODEL // N_CHIPS, D_MODEL)]
        cparts, lands = zip(*[_chip_presum(p, r, pos) for p, r in zip(parts, _sibling_exchange(parts, pos))])
        ffn['mixer'] = _split_start("scatter_mixer_start", _scatter_copies, cparts, lands)
        return ffn['mixer'][4]

    head = jnp.concatenate([jnp.zeros((PAD_ROWS, D_MODEL), F32), meta_full], axis=0)
    loss, dx, dhead, grads = _local_step(head, x[0], loss_target[0], g_pre_mix, w_in_full, conv_w_full, conv_b, w_a[0],
                                         b_a, w_x[0], b_x, lru_lambda, attn_sinks, g_post_mix, g_pre_ffn, g_post_ffn,
                                         late_weights, on_ffn_grads, on_outproj_bwd, on_mixer_grads, token)
    grad_x = dx[None]

    pack = _pack_small(dhead[PAD_ROWS:], grads, loss)
    dev = jnp.reshape(4 * xi + 2 * yi + ci, (1,)).astype(jnp.int32)
    s_send, s_recv, s_thru, s_lands, token5 = _split_start(
        "gather_small_start", _all_peers_copies, [pack], [_prep_tiny(pack, dev, N_DEV)], N_DEV - 1)

    send, recv, thru, lands, _ = ffn['mixer']
    mixer_cparts, mixer_lands = _split_wait("scatter_mixer_wait", _scatter_copies, send, recv, thru, lands, [token5])
    ffn_cparts, ffn_lands = _split_wait("scatter_ffn_wait", _scatter_copies, ffn['send'], ffn['recv'], ffn['thru'],
                                        ffn['lands'], mixer_lands)
    chip_partials = _scatter_partials([], [], mixer_cparts + ffn_cparts, mixer_lands + ffn_lands)

    g_out_d, delta, new_m, new_v = {}, {}, {}, {}
    for name, part in zip(_BIG, chip_partials):
        shp = weights[name].shape
        res = _adamw_big(part, big2d(weights[name], name), big2d(mom1[name], name), big2d(mom2[name], name))
        g_out_d[name], delta[name], new_m[name], new_v[name] = (big2d(r[None], name).reshape(shp) for r in res)

    _, (gathered,) = _split_wait("gather_small_wait", _all_peers_copies, s_send, s_recv, s_thru, s_lands,
                                 [g_out_d[n] for n in _BIG])
    small = _unpack_small(_sum_devices(gathered.reshape(N_DEV * SMALL_PACK_ROWS, D_MODEL), SMALL_PACK_ROWS), chip)
    loss = small['loss']
    small_names = [n for n in _WEIGHTS if n not in _BIG]
    quads = [(_as2d(weights[n]), _as2d(small[n]), _as2d(mom1[n]), _as2d(mom2[n])) for n in small_names]
    for name, (d, m2, v2) in zip(small_names, _adamw_small(quads)):
        shp = weights[name].shape
        g_out_d[name] = small[name].reshape(shp)
        delta[name], new_m[name], new_v[name] = d.reshape(shp), m2.reshape(shp), v2.reshape(shp)

    return (loss, grad_x, *[g_out_d[n] for n in _WEIGHTS], *[delta[n] for n in _WEIGHTS],
            *[new_m[n] for n in _WEIGHTS], *[new_v[n] for n in _WEIGHTS])
```

```python
import numpy as np
import jax
import jax.numpy as jnp
from jax import lax
from jax.experimental import pallas as pl
from jax.experimental.pallas import tpu as pltpu

F32 = jnp.float32
BF16 = jnp.bfloat16

D_MODEL = 1024
N_META = 16
BLOCK = 128
PAD_ROWS = BLOCK - N_META
HEAD_DIM = 64
ATTN_HEADS = 8
GQA_GROUP = 4
ATTN_WIDTH = 512
KV_WIDTH = 128
QKV_WIDTH = ATTN_WIDTH + 2 * KV_WIDTH
LRU_WIDTH = 512
LRU_BLOCKS = 8
LRU_BLOCK = 64
LRU_C = 8.0
IN_WIDTH = 1792
D_FF = 4096
N_CHIPS = 4
FF_CHUNK = D_FF // N_CHIPS
EPS = 1e-6
NEG = -1e30

ADAM_LR = 0.001
ADAM_B1 = 0.9
ADAM_B2 = 0.999
ADAM_EPS = 1e-08
ADAM_WD = 0.01
ADAM_STEP = 10

VMEM_LIMIT_V7X = 62 * 1024 * 1024
MESH = pl.DeviceIdType.MESH

NT = (((1,), (1,)), ((), ()))
TN = (((0,), (0,)), ((), ()))


def _row_tile(tp):
    return 640 if tp % 640 == 0 else BLOCK


def _wgrad_row_tile(tp):
    return 1664 if tp % 1664 == 0 else _row_tile(tp)


def _params(*sem):
    return pltpu.CompilerParams(dimension_semantics=sem, vmem_limit_bytes=VMEM_LIMIT_V7X)


def _dot(a, b):
    return jnp.dot(a, b, preferred_element_type=F32)


def _dot_nt(a, b):
    return lax.dot_general(a, b, NT, preferred_element_type=F32)


def _dot_tn(a, b):
    return lax.dot_general(a, b, TN, preferred_element_type=F32)


def _rms(x):
    rs = lax.rsqrt(jnp.mean(x * x, axis=-1, keepdims=True) + EPS)
    return x * rs, rs


def _rms_bwd(xhat, rs, g, dy):
    dyg = dy * g
    dx = rs * (dyg - xhat * jnp.mean(dyg * xhat, axis=-1, keepdims=True))
    dg = jnp.sum(dy * xhat, axis=0, keepdims=True)
    return dx, dg


def _gelu(x):
    k = 0.7978845608028654
    t = jnp.tanh(x * (k + (k * 0.044715) * (x * x)))
    return (0.5 * x) * (1.0 + t), t


def _gelu_grad(x, t):
    k = 0.7978845608028654
    return 0.5 * (1.0 + t) + 0.5 * x * (1.0 - t * t) * k * (1.0 + 3 * 0.044715 * x * x)


def _sigmoid(x):
    return 0.5 * jnp.tanh(0.5 * x) + 0.5


def _one_minus_exp2(y):
    t = jnp.tanh(y)
    return (-2.0 * t) / (1.0 - t)


def _softplus(x):
    return jnp.maximum(x, 0.0) + jnp.log1p(jnp.exp(-jnp.abs(x)))


def _seq_specs(tr, delay=0):
    qb = tr // BLOCK
    tile = lambda i: jnp.maximum(i - delay, 0)
    return [pl.BlockSpec((BLOCK, D_MODEL), lambda i, *_, s=s: (jnp.maximum(tile(i) * qb + s - 1, 0), 0))
            for s in range(qb)]


def _seq_tile(head, pieces, i):
    first = jnp.where(i == 0, head, pieces[0][...])
    return jnp.concatenate([first] + [p[...] for p in pieces[1:]], axis=0)


def _inproj_fwd(head, x, g, w_in, token):
    tp = BLOCK + x.shape[0]
    tr = _row_tile(tp)
    qb = tr // BLOCK

    def body(*refs):
        head_ref, pieces = refs[0], refs[1:1 + qb]
        g_ref, w_ref, _, u_ref, qkv_ref, xr_ref, yr_ref = refs[1 + qb:]
        xhat, _ = _rms(_seq_tile(head_ref[...], pieces, pl.program_id(0)))
        u = (xhat * g_ref[...]).astype(BF16)
        u_ref[...] = u
        z = _dot_nt(u, w_ref[...])
        qkv_ref[...] = z[:, :QKV_WIDTH].astype(BF16)
        xr_ref[...] = z[:, QKV_WIDTH:QKV_WIDTH + LRU_WIDTH]
        yr_ref[...] = z[:, QKV_WIDTH + LRU_WIDTH:]

    row = lambda w: pl.BlockSpec((tr, w), lambda i: (i, 0))
    full = lambda a: pl.BlockSpec(a.shape, lambda i: (0,) * a.ndim)
    return pl.pallas_call(
        body, name="inproj_fwd", grid=(tp // tr,),
        in_specs=[full(head)] + _seq_specs(tr) + [full(g), full(w_in), full(token)],
        out_specs=[row(D_MODEL), row(QKV_WIDTH), row(LRU_WIDTH), row(LRU_WIDTH)],
        out_shape=[jax.ShapeDtypeStruct((tp, D_MODEL), BF16), jax.ShapeDtypeStruct((tp, QKV_WIDTH), BF16),
                   jax.ShapeDtypeStruct((tp, LRU_WIDTH), F32), jax.ShapeDtypeStruct((tp, LRU_WIDTH), F32)],
        compiler_params=_params("parallel"),
    )(head, *([x] * qb), g, w_in, token)


GROUP_ROWS = GQA_GROUP * BLOCK


def _attn_bias():
    j = np.arange(2 * BLOCK)[:, None]
    i = np.arange(BLOCK)[None, :]
    band = (j - i >= 1) & (j - i <= BLOCK)
    out = []
    for n in range(3):
        ok = band & ((n - 1) * BLOCK + j >= PAD_ROWS) if n < 2 else band
        out.append(np.tile(np.where(ok, 0.0, NEG).astype(np.float32), (1, GQA_GROUP)))
    return jnp.asarray(np.stack(out))


def _stack_heads(a, g):
    heads = range(GQA_GROUP * g, GQA_GROUP * (g + 1))
    return jnp.concatenate([a[:, h * HEAD_DIM:(h + 1) * HEAD_DIM] for h in heads], axis=0)


def _unstack_heads(groups):
    return jnp.concatenate([p[h * BLOCK:(h + 1) * BLOCK] for p in groups for h in range(GQA_GROUP)], axis=1)


def _attn_probs_t(k_g, qg, bias, sink_row):
    st = _dot_nt(k_g, qg) + bias
    m = jnp.maximum(jnp.max(st, axis=0, keepdims=True), sink_row)
    p = jnp.exp(st - m)
    es = jnp.exp(sink_row - m)
    inv = 1.0 / (jnp.sum(p, axis=0, keepdims=True) + es)
    return p * inv, es * inv


def _attn_consts(sinks):
    return jnp.repeat(sinks.reshape(ATTN_HEADS), BLOCK).reshape(ATTN_HEADS // GQA_GROUP, GROUP_ROWS), _attn_bias()


_SINK_SPEC = pl.BlockSpec((ATTN_HEADS // GQA_GROUP, GROUP_ROWS), lambda n: (0, 0))
_BIAS_SPEC = pl.BlockSpec((3, 2 * BLOCK, GROUP_ROWS), lambda n: (0, 0, 0))
_QSCALE = HEAD_DIM ** -0.5


def _kv_specs(tr):
    qb = tr // BLOCK
    prev = lambda col: pl.BlockSpec((BLOCK, KV_WIDTH), lambda t: (jnp.maximum(t * qb - 1, 0), col))
    cur = lambda col: pl.BlockSpec((tr, KV_WIDTH), lambda t: (t, col))
    return [prev(4), cur(4), prev(5), cur(5)]


def _block_bias(b_ref, t, qb, i):
    return b_ref[2] if i >= 2 else b_ref[jnp.minimum(t * qb + i, 2)]


N_KV = ATTN_HEADS // GQA_GROUP


def _prob_specs(qb):
    return [pl.BlockSpec((qb, N_KV, 2 * BLOCK, GROUP_ROWS), lambda t: (t, 0, 0, 0)),
            pl.BlockSpec((qb, SUBLANES, GROUP_ROWS), lambda t: (t, 0, 0))]


def _attn_fwd(qkv, sinks):
    tp = qkv.shape[0]
    tr = _row_tile(tp)
    qb, nb = tr // BLOCK, tp // BLOCK
    sink_rows, bias = _attn_consts(sinks)

    def body(s_ref, b_ref, q_ref, kp_ref, kc_ref, vp_ref, vc_ref, o_ref, p_ref, ps_ref):
        t = pl.program_id(0)
        k_all = jnp.concatenate([kp_ref[...], kc_ref[...]], axis=0)
        v_all = jnp.concatenate([vp_ref[...], vc_ref[...]], axis=0)
        for i in range(qb):
            rows = slice(i * BLOCK, (i + 1) * BLOCK)
            q = q_ref[rows]
            k2, v2 = k_all[i * BLOCK:(i + 2) * BLOCK], v_all[i * BLOCK:(i + 2) * BLOCK]
            bias_n = _block_bias(b_ref, t, qb, i)
            outs, sink_probs = [], []
            for g in range(N_KV):
                cols = slice(g * HEAD_DIM, (g + 1) * HEAD_DIM)
                qg = _stack_heads(q, g) * jnp.asarray(_QSCALE, BF16)
                p, ps = _attn_probs_t(k2[:, cols], qg, bias_n, s_ref[g:g + 1])
                pb = p.astype(BF16)
                p_ref[i, g] = pb
                sink_probs.append(ps)
                outs.append(_dot_tn(pb, v2[:, cols]))
            o_ref[rows] = _unstack_heads(outs).astype(BF16)
            ps_ref[i] = jnp.concatenate(sink_probs + [jnp.zeros((SUBLANES - N_KV, GROUP_ROWS), F32)], axis=0)

    return pl.pallas_call(
        body, name="attn_fwd", grid=(tp // tr,),
        in_specs=[_SINK_SPEC, _BIAS_SPEC, pl.BlockSpec((tr, ATTN_WIDTH), lambda t: (t, 0))] + _kv_specs(tr),
        out_specs=[pl.BlockSpec((tr, ATTN_WIDTH), lambda t: (t, 0))] + _prob_specs(qb),
        out_shape=[jax.ShapeDtypeStruct((tp, ATTN_WIDTH), BF16),
                   jax.ShapeDtypeStruct((nb, N_KV, 2 * BLOCK, GROUP_ROWS), BF16),
                   jax.ShapeDtypeStruct((nb, SUBLANES, GROUP_ROWS), F32)],
        compiler_params=_params("parallel"),
    )(sink_rows, bias, qkv, qkv, qkv, qkv, qkv)


def _conv_taps(x, halo):
    ext = jnp.concatenate([halo, x], axis=0)
    return [ext[8:] if k == 3 else pltpu.roll(ext, 3 - k, 0)[8:] for k in range(4)]


def _lru_gates(xc, wa, ba, wx, bx, sp):
    xb = xc.astype(BF16)
    r = _sigmoid(_dot(xb, wa) + ba)
    ig = _sigmoid(_dot(xb, wx) + bx)
    log_a = (-LRU_C * sp) * r
    a = jnp.exp(log_a)
    mult = jnp.sqrt(_one_minus_exp2(log_a))
    return xb, r, ig, a, mult


SUBLANES = 8


def _scan_fwd(a, b, h_in):
    n, width = a.shape
    a, b = (v.reshape(n // SUBLANES, SUBLANES, width) for v in (a, b))
    in_group = lax.broadcasted_iota(jnp.int32, a.shape, 1)
    for d in (1, 2, 4):
        keep = in_group >= d
        b = jnp.where(keep, a * pltpu.roll(b, d, 1) + b, b)
        a = jnp.where(keep, a * pltpu.roll(a, d, 1), a)
    a, b = a.reshape(n, width), b.reshape(n, width)
    out, carry = [], h_in
    for g in range(0, n, SUBLANES):
        h = a[g:g + SUBLANES] * carry + b[g:g + SUBLANES]
        out.append(h)
        carry = h[SUBLANES - 1:]
    return jnp.concatenate(out, axis=0)


def _scan_rev(c, b, g_in):
    n, width = c.shape
    c, b = (v.reshape(n // SUBLANES, SUBLANES, width) for v in (c, b))
    in_group = lax.broadcasted_iota(jnp.int32, c.shape, 1)
    for d in (1, 2, 4):
        keep = in_group < SUBLANES - d
        b = jnp.where(keep, b + c * pltpu.roll(b, SUBLANES - d, 1), b)
        c = jnp.where(keep, c * pltpu.roll(c, SUBLANES - d, 1), c)
    c, b = c.reshape(n, width), b.reshape(n, width)
    out, carry = [], g_in
    for g in range(n - SUBLANES, -1, -SUBLANES):
        r = b[g:g + SUBLANES] + c[g:g + SUBLANES] * carry
        out.append(r)
        carry = r[:1]
    return jnp.concatenate(out[::-1], axis=0)


def _lru_fwd(xr, yr, conv_w, conv_b, wa, ba, wx, bx, lam):
    tp = xr.shape[0]
    tr = _row_tile(tp)
    qb = tr // BLOCK

    def body(xr_ref, yr_ref, cw_ref, cb_ref, wa_ref, ba_ref, wx_ref, bx_ref, lam_ref, hr_ref, rec_ref, halo, hprev):
        t = pl.program_id(0)

        @pl.when(t == 0)
        def _():
            halo[...] = jnp.zeros_like(halo)
            hprev[...] = jnp.zeros_like(hprev)

        cw, cb = cw_ref[...], cb_ref[...]
        wa_m, ba_v, wx_m, bx_v = wa_ref[...], ba_ref[...], wx_ref[...], bx_ref[...]
        sp = _softplus(-lam_ref[...])
        before, h_last = halo[...], hprev[0:1]
        for i in range(qb):
            rows = slice(i * BLOCK, (i + 1) * BLOCK)
            x = xr_ref[rows]
            taps = _conv_taps(x, before)
            before = x[BLOCK - 8:]
            xc = cb + sum(cw[k:k + 1] * taps[k] for k in range(4))
            _, _, ig, a, mult = _lru_gates(xc, wa_m, ba_v, wx_m, bx_v, sp)
            u = mult * (ig * xc)
            if i == 0:
                pos = t * tr + lax.broadcasted_iota(jnp.int32, xc.shape, 0)
                u = jnp.where(pos >= PAD_ROWS, u, 0.0)
            h = _scan_fwd(a, u, h_last)
            h_last = h[BLOCK - 1:]
            hr_ref[rows] = h
            gl, _ = _gelu(yr_ref[rows])
            rec_ref[rows] = (gl * h).astype(BF16)
        halo[...] = before
        hprev[0:1] = h_last

    blk = pl.BlockSpec((tr, LRU_WIDTH), lambda t: (t, 0))
    full = lambda a: pl.BlockSpec(a.shape, lambda t: (0,) * a.ndim)
    small = [conv_w, conv_b, wa, ba, wx, bx, lam]
    return pl.pallas_call(
        body, name="lru_fwd", grid=(tp // tr,),
        in_specs=[blk, blk] + [full(a) for a in small],
        out_specs=[blk, blk],
        out_shape=[jax.ShapeDtypeStruct((tp, LRU_WIDTH), F32), jax.ShapeDtypeStruct((tp, LRU_WIDTH), BF16)],
        scratch_shapes=[pltpu.VMEM((8, LRU_WIDTH), F32), pltpu.VMEM((8, LRU_WIDTH), F32)],
        compiler_params=_params("arbitrary"),
    )(xr, yr, *small)


def _inproj_lru_fwd(head, x, g, w_in, conv_w, conv_b, wa, ba, wx, bx, lam, token):
    tp = BLOCK + x.shape[0]
    tr = _row_tile(tp)
    qb, nt = tr // BLOCK, tp // tr
    small = [conv_w, conv_b, wa, ba, wx, bx, lam]

    def body(*refs):
        head_ref, pieces = refs[0], refs[1:1 + qb]
        g_ref, w_ref, _, cw_ref, cb_ref, wa_ref, ba_ref, wx_ref, bx_ref, lam_ref = refs[1 + qb:11 + qb]
        u_ref, qkv_ref, xr_ref, yr_ref, hr_ref, rec_ref, zbuf, halo, hprev = refs[11 + qb:]
        i = pl.program_id(0)
        cur = i % 2

        @pl.when(i == 0)
        def _():
            halo[...] = jnp.zeros_like(halo)
            hprev[...] = jnp.zeros_like(hprev)
            zbuf[1] = jnp.zeros((tr, 2 * LRU_WIDTH), F32)

        def recurrent_branch(valid):
            cw, cb = cw_ref[...], cb_ref[...]
            wa_m, ba_v, wx_m, bx_v = wa_ref[...], ba_ref[...], wx_ref[...], bx_ref[...]
            sp = _softplus(-lam_ref[...])
            before, h_last = halo[...], hprev[0:1]
            for b in range(qb):
                rows = slice(b * BLOCK, (b + 1) * BLOCK)
                xy = zbuf[1 - cur, rows]
                xin = xy[:, :LRU_WIDTH]
                taps = _conv_taps(xin, before)
                before = xin[BLOCK - 8:]
                xc = cb + sum(cw[k:k + 1] * taps[k] for k in range(4))
                _, _, ig, a, mult = _lru_gates(xc, wa_m, ba_v, wx_m, bx_v, sp)
                u = mult * (ig * xc)
                if b == 0:
                    pos = (i - 1) * tr + lax.broadcasted_iota(jnp.int32, xc.shape, 0)
                    u = jnp.where(pos >= PAD_ROWS, u, 0.0)
                h = _scan_fwd(a, u, h_last)
                h_last = h[BLOCK - 1:]
                hr_ref[rows] = h
                gl, _ = _gelu(xy[:, LRU_WIDTH:])
                rec_ref[rows] = (gl * h).astype(BF16)
            halo[...] = jnp.where(valid, before, 0.0)
            hprev[0:1] = jnp.where(valid, h_last, 0.0)

        def projection():
            xhat, _ = _rms(_seq_tile(head_ref[...], pieces, i))
            u = (xhat * g_ref[...]).astype(BF16)
            u_ref[...] = u
            z = _dot_nt(u, w_ref[...])
            qkv_ref[...] = z[:, :QKV_WIDTH].astype(BF16)
            xr_ref[...] = z[:, QKV_WIDTH:QKV_WIDTH + LRU_WIDTH]
            yr_ref[...] = z[:, QKV_WIDTH + LRU_WIDTH:]
            zbuf[cur] = z[:, QKV_WIDTH:]

        @pl.when(i < nt)
        def _():
            recurrent_branch(i >= 1)
            projection()

        @pl.when(i == nt)
        def _():
            recurrent_branch(True)

    last = nt - 1
    this_row = lambda w: pl.BlockSpec((tr, w), lambda i: (jnp.minimum(i, last), 0))
    prev_row = lambda w: pl.BlockSpec((tr, w), lambda i: (jnp.maximum(i - 1, 0), 0))
    full = lambda a: pl.BlockSpec(a.shape, lambda i: (0,) * a.ndim)
    piece_specs = [pl.BlockSpec((BLOCK, D_MODEL), lambda i, s=s: (jnp.maximum(jnp.minimum(i, last) * qb + s - 1, 0), 0))
                   for s in range(qb)]
    return pl.pallas_call(
        body, name="inproj_lru_fwd", grid=(nt + 1,),
        in_specs=[full(head)] + piece_specs + [full(g), full(w_in), full(token)] + [full(a) for a in small],
        out_specs=[this_row(D_MODEL), this_row(QKV_WIDTH), this_row(LRU_WIDTH), this_row(LRU_WIDTH),
                   prev_row(LRU_WIDTH), prev_row(LRU_WIDTH)],
        out_shape=[jax.ShapeDtypeStruct((tp, D_MODEL), BF16), jax.ShapeDtypeStruct((tp, QKV_WIDTH), BF16),
                   jax.ShapeDtypeStruct((tp, LRU_WIDTH), F32), jax.ShapeDtypeStruct((tp, LRU_WIDTH), F32),
                   jax.ShapeDtypeStruct((tp, LRU_WIDTH), F32), jax.ShapeDtypeStruct((tp, LRU_WIDTH), BF16)],
        scratch_shapes=[pltpu.VMEM((2, tr, 2 * LRU_WIDTH), F32), pltpu.VMEM((8, LRU_WIDTH), F32),
                        pltpu.VMEM((8, LRU_WIDTH), F32)],
        compiler_params=_params("arbitrary"),
    )(head, *([x] * qb), g, w_in, token, *small)


def _outproj_fwd(attn, rec, w_out, head, x, g_post_mix, g_pre_ffn):
    tp = attn.shape[0]
    tr = _row_tile(tp)
    qb = tr // BLOCK

    def body(*refs):
        a_ref, r_ref, w_ref, head_ref = refs[:4]
        pieces = refs[4:4 + qb]
        gm_ref, gf_ref, mix_ref, h1_ref, u1_ref = refs[4 + qb:]
        mix = _dot(a_ref[...], w_ref[:ATTN_WIDTH]) + _dot(r_ref[...], w_ref[ATTN_WIDTH:])
        mix_ref[...] = mix
        mhat, _ = _rms(mix)
        h1 = _seq_tile(head_ref[...], pieces, pl.program_id(0)) + mhat * gm_ref[...]
        h1_ref[...] = h1
        hhat, _ = _rms(h1)
        u1_ref[...] = (hhat * gf_ref[...]).astype(BF16)

    row = lambda w: pl.BlockSpec((tr, w), lambda i: (i, 0))
    full = lambda a: pl.BlockSpec(a.shape, lambda i: (0,) * a.ndim)
    return pl.pallas_call(
        body, name="outproj_fwd", grid=(tp // tr,),
        in_specs=[row(ATTN_WIDTH), row(LRU_WIDTH), full(w_out), full(head)] + _seq_specs(tr)
        + [full(g_post_mix), full(g_pre_ffn)],
        out_specs=[row(D_MODEL), row(D_MODEL), row(D_MODEL)],
        out_shape=[jax.ShapeDtypeStruct((tp, D_MODEL), F32), jax.ShapeDtypeStruct((tp, D_MODEL), F32),
                   jax.ShapeDtypeStruct((tp, D_MODEL), BF16)],
        compiler_params=_params("parallel"),
    )(attn, rec, w_out, head, *([x] * qb), g_post_mix, g_pre_ffn)


def _resident(a):
    return pl.BlockSpec(a.shape, lambda *_: (0,) * a.ndim, pipeline_mode=pl.Buffered(1))


def _ffn_fwd(u1, w1, w2, h1, tgt, g_post_ffn):
    tp = h1.shape[0]
    tr = _row_tile(tp)
    qb, nt = tr // BLOCK, tp // tr
    sr = tr // N_CHIPS

    def body(*refs):
        u_ref, w1_ref, w2_ref, h1_ref = refs[:4]
        t_pieces = refs[4:4 + qb]
        g_ref, r1_ref, dy_ref, df2_ref, loss_ref, dg_ref, acc = refs[4 + qb:]
        i, c = pl.program_id(0), pl.program_id(1)
        cur = i % 2

        @pl.when((i == 0) & (c == 0))
        def _():
            loss_ref[...] = jnp.zeros_like(loss_ref)
            dg_ref[...] = jnp.zeros_like(dg_ref)
            acc[1] = jnp.zeros((tr, D_MODEL), F32)

        def matmuls():
            r = jnp.maximum(_dot(u_ref[...], w1_ref[c]), 0.0)
            r1_ref[...] = r.astype(BF16)
            return _dot((r * r).astype(BF16), w2_ref[c])

        def finish_previous_tile(k, valid):
            lo, hi = k * sr, (k + 1) * sr
            g = g_ref[...]
            fhat, rs = _rms(acc[1 - cur, lo:hi])
            h2 = h1_ref[...] + fhat * g
            rows = (i - 1) * tr + lo + lax.broadcasted_iota(jnp.int32, h2.shape, 0)
            tgt = jnp.concatenate([p[max(lo - s * BLOCK, 0):min(hi - s * BLOCK, BLOCK)] for s, p in enumerate(t_pieces)
                                   if lo < (s + 1) * BLOCK and hi > s * BLOCK], axis=0)
            err = jnp.where((rows >= BLOCK) & valid, h2 - tgt, 0.0)
            dy = err * (1.0 / D_MODEL)
            dy_ref[...] = dy
            loss_ref[...] += (0.5 / D_MODEL) * jnp.sum(err * err)
            df2, dg = _rms_bwd(fhat, rs, g, dy)
            df2_ref[...] = df2.astype(BF16)
            dg_ref[...] += dg

        for k in range(N_CHIPS):
            @pl.when((c == k) & (i < nt))
            def _(k=k):
                finish_previous_tile(k, i >= 1)
                if k == 0:
                    acc[cur] = matmuls()
                else:
                    acc[cur] += matmuls()

            @pl.when((c == k) & (i == nt))
            def _(k=k):
                finish_previous_tile(k, True)

    last = nt - 1
    this_row = pl.BlockSpec((tr, D_MODEL), lambda i, c: (jnp.minimum(i, last), 0))
    prev_quarter = pl.BlockSpec((sr, D_MODEL), lambda i, c: (jnp.maximum(i - 1, 0) * N_CHIPS + c, 0))
    prev_quarter_out = pl.BlockSpec(
        (sr, D_MODEL), lambda i, c: (jnp.where(i == 0, nt * N_CHIPS, (i - 1) * N_CHIPS + c), 0))
    full = lambda a: pl.BlockSpec(a.shape, lambda i, c: (0,) * a.ndim)
    return pl.pallas_call(
        body, name="ffn_fwd", grid=(nt + 1, N_CHIPS),
        in_specs=[this_row, _resident(w1), _resident(w2), prev_quarter] + _seq_specs(tr, delay=1) + [full(g_post_ffn)],
        out_specs=[pl.BlockSpec((tr, FF_CHUNK), lambda i, c: (jnp.minimum(i, last), jnp.where(i < nt, c, N_CHIPS - 1))),
                   prev_quarter_out, prev_quarter_out,
                   pl.BlockSpec((1, 1), lambda i, c: (0, 0)), pl.BlockSpec((1, D_MODEL), lambda i, c: (0, 0))],
        out_shape=[jax.ShapeDtypeStruct((tp, D_FF), BF16), jax.ShapeDtypeStruct((tp + sr, D_MODEL), F32),
                   jax.ShapeDtypeStruct((tp + sr, D_MODEL), BF16), jax.ShapeDtypeStruct((1, 1), F32),
                   jax.ShapeDtypeStruct((1, D_MODEL), F32)],
        scratch_shapes=[pltpu.VMEM((2, tr, D_MODEL), F32)],
        compiler_params=_params("arbitrary", "arbitrary"),
    )(u1, w1, w2, h1, *([tgt] * qb), g_post_ffn)


def _ffn_bwd_data(df2, r1, w1, w2, dy, h1, mix, g_pre_ffn, g_post_mix):
    tp = h1.shape[0]
    tr = _row_tile(tp)
    nt = tp // tr
    sr = tr // N_CHIPS

    def body(df2_ref, r1_ref, w1_ref, w2_ref, dy_ref, h1_ref, mix_ref, gf_ref, gm_ref,
             da_ref, dh1_ref, dmix_ref, dgf_ref, dgm_ref, acc):
        i, c = pl.program_id(0), pl.program_id(1)
        cur = i % 2

        @pl.when((i == 0) & (c == 0))
        def _():
            dgf_ref[...] = jnp.zeros_like(dgf_ref)
            dgm_ref[...] = jnp.zeros_like(dgm_ref)
            acc[1] = jnp.zeros((tr, D_MODEL), F32)

        def matmuls():
            df = _dot_nt(df2_ref[...], w2_ref[c])
            da = (df * (2.0 * r1_ref[...].astype(F32))).astype(BF16)
            da_ref[...] = da
            return _dot_nt(da, w1_ref[c])

        def finish_previous_tile(k, valid):
            lo, hi = k * sr, (k + 1) * sr
            hhat, rs = _rms(h1_ref[...])
            dx, dgf = _rms_bwd(hhat, rs, gf_ref[...], acc[1 - cur, lo:hi])
            dh1 = dy_ref[...] + dx
            dh1_ref[...] = dh1
            mhat, rsm = _rms(mix_ref[...])
            dmix, dgm = _rms_bwd(mhat, rsm, gm_ref[...], dh1)
            dmix_ref[...] = dmix.astype(BF16)
            dgf_ref[...] += jnp.where(valid, dgf, 0.0)
            dgm_ref[...] += jnp.where(valid, dgm, 0.0)

        for k in range(N_CHIPS):
            @pl.when((c == k) & (i < nt))
            def _(k=k):
                finish_previous_tile(k, i >= 1)
                if k == 0:
                    acc[cur] = matmuls()
                else:
                    acc[cur] += matmuls()

            @pl.when((c == k) & (i == nt))
            def _(k=k):
                finish_previous_tile(k, True)

    last = nt - 1
    this_row = pl.BlockSpec((tr, D_MODEL), lambda i, c: (jnp.minimum(i, last), 0))
    prev_quarter = pl.BlockSpec((sr, D_MODEL), lambda i, c: (jnp.maximum(i - 1, 0) * N_CHIPS + c, 0))
    prev_quarter_out = pl.BlockSpec(
        (sr, D_MODEL), lambda i, c: (jnp.where(i == 0, nt * N_CHIPS, (i - 1) * N_CHIPS + c), 0))
    chunk = pl.BlockSpec((tr, FF_CHUNK), lambda i, c: (jnp.minimum(i, last), jnp.where(i < nt, c, N_CHIPS - 1)))
    gain = pl.BlockSpec((1, D_MODEL), lambda i, c: (0, 0))
    return pl.pallas_call(
        body, name="ffn_bwd_data", grid=(nt + 1, N_CHIPS),
        in_specs=[this_row, chunk, _resident(w1), _resident(w2), prev_quarter, prev_quarter, prev_quarter, gain, gain],
        out_specs=[chunk, prev_quarter_out, prev_quarter_out, gain, gain],
        out_shape=[jax.ShapeDtypeStruct((tp, D_FF), BF16), jax.ShapeDtypeStruct((tp + sr, D_MODEL), F32),
                   jax.ShapeDtypeStruct((tp + sr, D_MODEL), BF16), jax.ShapeDtypeStruct((1, D_MODEL), F32),
                   jax.ShapeDtypeStruct((1, D_MODEL), F32)],
        scratch_shapes=[pltpu.VMEM((2, tr, D_MODEL), F32)],
        compiler_params=_params("arbitrary", "arbitrary"),
    )(df2, r1, w1, w2, dy, h1, mix, g_pre_ffn, g_post_mix)


def _ffn_bwd_weights(u1, da1, r1, df2):
    tp = u1.shape[0]
    tr = _wgrad_row_tile(tp)

    def body(u_ref, da_ref, r1_ref, df2_ref, dw1_ref, dw2_ref):
        i = pl.program_id(1)
        r = r1_ref[...].astype(F32)
        p1 = _dot_tn(u_ref[...], da_ref[...])
        p2 = _dot_tn((r * r).astype(BF16), df2_ref[...])

        @pl.when(i == 0)
        def _():
            dw1_ref[0] = p1
            dw2_ref[0] = p2

        @pl.when(i > 0)
        def _():
            dw1_ref[0] += p1
            dw2_ref[0] += p2

    row = pl.BlockSpec((tr, D_MODEL), lambda c, i: (i, 0))
    chunk = pl.BlockSpec((tr, FF_CHUNK), lambda c, i: (i, c))
    return pl.pallas_call(
        body, name="ffn_bwd_weights", grid=(N_CHIPS, tp // tr),
        in_specs=[row, chunk, chunk, row],
        out_specs=[pl.BlockSpec((1, D_MODEL, FF_CHUNK), lambda c, i: (c, 0, 0)),
                   pl.BlockSpec((1, FF_CHUNK, D_MODEL), lambda c, i: (c, 0, 0))],
        out_shape=[jax.ShapeDtypeStruct((N_CHIPS, D_MODEL, FF_CHUNK), F32),
                   jax.ShapeDtypeStruct((N_CHIPS, FF_CHUNK, D_MODEL), F32)],
        compiler_params=_params("parallel", "arbitrary"),
    )(u1, da1, r1, df2)


def _outproj_bwd(dmix, w_out, attn, rec, token):
    tp = attn.shape[0]
    tr = _wgrad_row_tile(tp)

    def body(dm_ref, w_ref, a_ref, r_ref, _, da_ref, dr_ref, dw_ref):
        i = pl.program_id(0)
        dm = dm_ref[...]
        dcat = _dot_nt(dm, w_ref[...])
        da_ref[...] = dcat[:, :ATTN_WIDTH].astype(BF16)
        dr_ref[...] = dcat[:, ATTN_WIDTH:]
        pa = _dot_tn(a_ref[...], dm)
        pr = _dot_tn(r_ref[...], dm)

        @pl.when(i == 0)
        def _():
            dw_ref[:ATTN_WIDTH] = pa
            dw_ref[ATTN_WIDTH:] = pr

        @pl.when(i > 0)
        def _():
            dw_ref[:ATTN_WIDTH] += pa
            dw_ref[ATTN_WIDTH:] += pr

    row = lambda w: pl.BlockSpec((tr, w), lambda i: (i, 0))
    full = pl.BlockSpec((D_MODEL, D_MODEL), lambda i: (0, 0))
    return pl.pallas_call(
        body, name="outproj_bwd", grid=(tp // tr,),
        in_specs=[row(D_MODEL), full, row(ATTN_WIDTH), row(LRU_WIDTH), pl.BlockSpec(token.shape, lambda i: (0, 0))],
        out_specs=[row(ATTN_WIDTH), row(LRU_WIDTH), full],
        out_shape=[jax.ShapeDtypeStruct((tp, ATTN_WIDTH), BF16), jax.ShapeDtypeStruct((tp, LRU_WIDTH), F32),
                   jax.ShapeDtypeStruct((D_MODEL, D_MODEL), F32)],
        compiler_params=_params("arbitrary"),
    )(dmix, w_out, attn, rec, token)


N_VEC_ROWS = 8


def _lru_bwd(xr, yr, hr, drec, conv_w, conv_b, wa, ba, wx, bx, lam, token):
    tp = xr.shape[0]
    tr = _row_tile(tp)
    qb, nt = tr // BLOCK, tp // tr

    def body(xr_ref, xh_ref, yr_ref, hr_ref, hp_ref, dr_ref, cw_ref, cb_ref, wa_ref, ba_ref, wx_ref, bx_ref, lam_ref, _,
             dxr_ref, dyr_ref, dwa_ref, dwx_ref, vec_ref, g_next, a_next, dxc_next, dsp):
        s = pl.program_id(0)
        t = nt - 1 - s

        @pl.when(s == 0)
        def _():
            g_next[...] = jnp.zeros_like(g_next)
            a_next[...] = jnp.zeros_like(a_next)
            dxc_next[...] = jnp.zeros_like(dxc_next)
            dsp[...] = jnp.zeros_like(dsp)
            dwa_ref[...] = jnp.zeros_like(dwa_ref)
            dwx_ref[...] = jnp.zeros_like(dwx_ref)
            vec_ref[...] = jnp.zeros_like(vec_ref)

        first_tile = t == 0
        cw, cb = cw_ref[...], cb_ref[...]
        lam_v = lam_ref[...]
        sp = _softplus(-lam_v)
        wa_m, ba_v, wx_m, bx_v = wa_ref[...], ba_ref[...], wx_ref[...], bx_ref[...]
        rows = lax.broadcasted_iota(jnp.int32, (BLOCK, LRU_WIDTH), 0)
        col = lambda v: jnp.sum(v, axis=0, keepdims=True)

        g_after, a_after, dxc_after = g_next[0:1], a_next[0:1], dxc_next[...]
        xbs, dgrs, dgis = [], [], []
        vec = [jnp.zeros((1, LRU_WIDTH), F32) for _ in range(N_VEC_ROWS)]
        for i in reversed(range(qb)):
            blk = slice(i * BLOCK, (i + 1) * BLOCK)
            if i == 0:
                x_before = jnp.where(first_tile, 0.0, xh_ref[...])
                h_before = jnp.where(first_tile, 0.0, hp_ref[7:8])
            else:
                x_before = xr_ref[i * BLOCK - 8:i * BLOCK]
                h_before = hr_ref[i * BLOCK - 1:i * BLOCK]
            taps = _conv_taps(xr_ref[blk], x_before)
            xc = cb + sum(cw[k:k + 1] * taps[k] for k in range(4))
            xb, r, ig, a, mult = _lru_gates(xc, wa_m, ba_v, wx_m, bx_v, sp)

            yr_v = yr_ref[blk]
            gl, th = _gelu(yr_v)
            h = hr_ref[blk]
            drec = dr_ref[blk]
            dyr_ref[blk] = (drec * h * _gelu_grad(yr_v, th)).astype(BF16)

            a_up = jnp.where(rows == BLOCK - 1, a_after, pltpu.roll(a, BLOCK - 1, 0))
            g = _scan_rev(a_up, drec * gl, g_after)
            g_after, a_after = g[0:1], a[0:1]

            h_prev = jnp.where(rows == 0, h_before, pltpu.roll(h, 1, 0))
            du, da = g, g * h_prev
            if i == 0:
                real = (t * tr + rows) >= PAD_ROWS
                du, da = jnp.where(real, du, 0.0), jnp.where(real, da, 0.0)
            dmult = du * (ig * xc)
            dig = du * (mult * xc)
            dxc = du * (mult * ig)
            dlog_a = da * a - dmult * (a * a / mult)
            if i == 0:
                dlog_a = jnp.where(real, dlog_a, 0.0)
            dgr = (dlog_a * (-LRU_C * sp)) * (r * (1.0 - r))
            dgi = dig * (ig * (1.0 - ig))
            dgr_b, dgi_b = dgr.astype(BF16), dgi.astype(BF16)
            dxc = dxc + _dot_nt(dgr_b, wa_m) + _dot_nt(dgi_b, wx_m)
            xbs.append(xb)
            dgrs.append(dgr_b)
            dgis.append(dgi_b)

            ext = jnp.concatenate([dxc, dxc_after], axis=0)
            up = [ext[:BLOCK] if j == 0 else pltpu.roll(ext, BLOCK + 8 - j, 0)[:BLOCK] for j in range(4)]
            dxr_ref[blk] = sum(cw[k:k + 1] * up[3 - k] for k in range(4)).astype(BF16)
            dxc_after = dxc[:8]

            for k in range(4):
                vec[k] = vec[k] + col(dxc * taps[k])
            vec[4] = vec[4] + col(dxc)
            vec[5] = vec[5] + col(dgr)
            vec[6] = vec[6] + col(dgi)
            vec[7] = vec[7] + col(dlog_a * (-LRU_C * r))

        g_next[0:1], a_next[0:1], dxc_next[...] = g_after, a_after, dxc_after
        xb_all = jnp.concatenate(xbs, axis=0)
        dwa_ref[...] += _dot_tn(xb_all, jnp.concatenate(dgrs, axis=0))
        dwx_ref[...] += _dot_tn(xb_all, jnp.concatenate(dgis, axis=0))
        for k in range(7):
            vec_ref[k:k + 1] += vec[k]
        dsp[0:1] += vec[7]

        @pl.when(s == nt - 1)
        def _():
            vec_ref[7:8] = dsp[0:1] * (-_sigmoid(-lam_v))

    blk_spec = pl.BlockSpec((tr, LRU_WIDTH), lambda s: (nt - 1 - s, 0))
    rows_before = pl.BlockSpec((8, LRU_WIDTH), lambda s: (jnp.maximum((nt - 1 - s) * (tr // 8) - 1, 0), 0))
    full = lambda a: pl.BlockSpec(a.shape, lambda s: (0,) * a.ndim)
    small = [conv_w, conv_b, wa, ba, wx, bx, lam, token]
    sq = pl.BlockSpec((LRU_WIDTH, LRU_WIDTH), lambda s: (0, 0))
    return pl.pallas_call(
        body, name="lru_bwd", grid=(nt,),
        in_specs=[blk_spec, rows_before, blk_spec, blk_spec, rows_before, blk_spec] + [full(a) for a in small],
        out_specs=[blk_spec, blk_spec, sq, sq, pl.BlockSpec((N_VEC_ROWS, LRU_WIDTH), lambda s: (0, 0))],
        out_shape=[jax.ShapeDtypeStruct((tp, LRU_WIDTH), BF16), jax.ShapeDtypeStruct((tp, LRU_WIDTH), BF16),
                   jax.ShapeDtypeStruct((LRU_WIDTH, LRU_WIDTH), F32), jax.ShapeDtypeStruct((LRU_WIDTH, LRU_WIDTH), F32),
                   jax.ShapeDtypeStruct((N_VEC_ROWS, LRU_WIDTH), F32)],
        scratch_shapes=[pltpu.VMEM((8, LRU_WIDTH), F32)] * 4,
        compiler_params=_params("arbitrary"),
    )(xr, xr, yr, hr, hr, drec, *small)


def _attn_bwd(qkv, dattn, probs, sink_probs):
    tp = qkv.shape[0]
    tr = _row_tile(tp)
    qb, nt = tr // BLOCK, tp // tr
    n_groups = N_KV

    def body(p_ref, ps_ref, q_ref, kp_ref, kc_ref, vp_ref, vc_ref, do_ref, dq_ref, dkv_ref, ex_ref, ds_ref, dsink):
        t = pl.program_id(0)

        @pl.when(t == 0)
        def _():
            dsink[...] = jnp.zeros_like(dsink)

        k_all = jnp.concatenate([kp_ref[...], kc_ref[...]], axis=0)
        v_all = jnp.concatenate([vp_ref[...], vc_ref[...]], axis=0)
        tail = None
        for i in range(qb):
            rows = slice(i * BLOCK, (i + 1) * BLOCK)
            q, do = q_ref[rows], do_ref[rows]
            k2, v2 = k_all[i * BLOCK:(i + 2) * BLOCK], v_all[i * BLOCK:(i + 2) * BLOCK]
            dqs, dks, dvs = [], [], []
            for g in range(n_groups):
                cols = slice(g * HEAD_DIM, (g + 1) * HEAD_DIM)
                k_g, v_g = k2[:, cols], v2[:, cols]
                qg = _stack_heads(q, g) * jnp.asarray(_QSCALE, BF16)
                dog = _stack_heads(do, g)
                pb = p_ref[i, g]
                p = pb.astype(F32)
                dpt = _dot_nt(v_g, dog)
                delta = jnp.sum(p * dpt, axis=0, keepdims=True)
                dst = (p * (dpt - delta)).astype(BF16)
                dqs.append(_dot_tn(dst, k_g) * _QSCALE)
                dks.append(_dot(dst, qg))
                dvs.append(_dot(pb, dog))
                dsink[g:g + 1] -= ps_ref[i, g:g + 1] * delta
            dq_ref[rows] = _unstack_heads(dqs).astype(BF16)
            dkv = jnp.concatenate(dks + dvs, axis=1)
            if i == 0:
                ex_ref[0] = dkv[:BLOCK]
            else:
                dkv_ref[(i - 1) * BLOCK:i * BLOCK] = (tail + dkv[:BLOCK]).astype(BF16)
            tail = dkv[BLOCK:]
        dkv_ref[(qb - 1) * BLOCK:] = tail.astype(BF16)

        @pl.when(t == nt - 1)
        def _():
            lane = lax.broadcasted_iota(jnp.int32, (1, ATTN_HEADS), 1)
            acc = jnp.zeros((1, ATTN_HEADS), F32)
            for h in range(ATTN_HEADS):
                g, hh = divmod(h, GQA_GROUP)
                acc = acc + jnp.where(lane == h, jnp.sum(dsink[g:g + 1, hh * BLOCK:(hh + 1) * BLOCK]), 0.0)
            ds_ref[...] = acc

    cur = lambda w: pl.BlockSpec((tr, w), lambda t: (t, 0))
    return pl.pallas_call(
        body, name="attn_bwd", grid=(nt,),
        in_specs=_prob_specs(qb) + [cur(ATTN_WIDTH)] + _kv_specs(tr) + [cur(ATTN_WIDTH)],
        out_specs=[cur(ATTN_WIDTH), cur(2 * KV_WIDTH), pl.BlockSpec((1, BLOCK, 2 * KV_WIDTH), lambda t: (t, 0, 0)),
                   pl.BlockSpec((1, ATTN_HEADS), lambda t: (0, 0))],
        out_shape=[jax.ShapeDtypeStruct((tp, ATTN_WIDTH), BF16), jax.ShapeDtypeStruct((tp, 2 * KV_WIDTH), BF16),
                   jax.ShapeDtypeStruct((nt, BLOCK, 2 * KV_WIDTH), F32), jax.ShapeDtypeStruct((1, ATTN_HEADS), F32)],
        scratch_shapes=[pltpu.VMEM((n_groups, GROUP_ROWS), F32)],
        compiler_params=_params("arbitrary"),
    )(probs, sink_probs, qkv, qkv, qkv, qkv, qkv, dattn)


def _fix_dkv(dkv, dkv_extra):
    tp = dkv.shape[0]
    tr = _row_tile(tp)
    nt, qb = tp // tr, tr // BLOCK
    if nt == 1:
        return dkv

    def body(d_ref, ex_ref, o_ref):
        o_ref[...] = (d_ref[...].astype(F32) + ex_ref[0]).astype(BF16)

    last = pl.BlockSpec((BLOCK, 2 * KV_WIDTH), lambda t: (t * qb + qb - 1, 0))
    return pl.pallas_call(
        body, name="fix_dkv", grid=(nt - 1,),
        in_specs=[last, pl.BlockSpec((1, BLOCK, 2 * KV_WIDTH), lambda t: (t + 1, 0, 0))],
        out_specs=last, out_shape=jax.ShapeDtypeStruct(dkv.shape, dkv.dtype),
        input_output_aliases={0: 0}, compiler_params=_params("parallel"),
    )(dkv, dkv_extra)


def _inproj_wgrad(dq, dkv, dxr, dyr, u0):
    tp = dq.shape[0]
    tr = _wgrad_row_tile(tp)

    def body(dq_ref, dkv_ref, dxr_ref, dyr_ref, u_ref, dw_ref):
        i = pl.program_id(0)
        dz = jnp.concatenate([dq_ref[...], dkv_ref[...], dxr_ref[...], dyr_ref[...]], axis=1)
        pw = _dot_tn(dz, u_ref[...])

        @pl.when(i == 0)
        def _():
            dw_ref[...] = pw

        @pl.when(i > 0)
        def _():
            dw_ref[...] += pw

    row = lambda w: pl.BlockSpec((tr, w), lambda i: (i, 0))
    return pl.pallas_call(
        body, name="inproj_wgrad", grid=(tp // tr,),
        in_specs=[row(ATTN_WIDTH), row(2 * KV_WIDTH), row(LRU_WIDTH), row(LRU_WIDTH), row(D_MODEL)],
        out_specs=pl.BlockSpec((IN_WIDTH, D_MODEL), lambda i: (0, 0)),
        out_shape=jax.ShapeDtypeStruct((IN_WIDTH, D_MODEL), F32),
        compiler_params=_params("arbitrary"),
    )(dq, dkv, dxr, dyr, u0)


def _inproj_dgrad(dq, dkv, dxr, dyr, w_in, head, x, dh1, g, token):
    tp = dq.shape[0]
    tr = _row_tile(tp)
    nt, qb = tp // tr, tr // BLOCK

    def body(*refs):
        dq_ref, dkv_ref, dxr_ref, dyr_ref, w_ref, head_ref = refs[:6]
        pieces = refs[6:6 + qb]
        dh1_ref, g_ref, _, gx_ref, dhead_ref, dg_ref, buf, sems = refs[6 + qb:]
        i = pl.program_id(0)
        slot = i % 2

        def out_copy(step, at):
            return pltpu.make_async_copy(buf.at[at], gx_ref.at[pl.ds(step * tr - BLOCK, tr)], sems.at[at])

        dz = jnp.concatenate([dq_ref[...], dkv_ref[...], dxr_ref[...], dyr_ref[...]], axis=1)
        du = _dot(dz, w_ref[...])
        hhat, rs = _rms(_seq_tile(head_ref[...], pieces, i))
        dx, dg = _rms_bwd(hhat, rs, g_ref[...], du)
        dh0 = dh1_ref[...] + dx

        @pl.when(i >= 3)
        def _():
            out_copy(i - 2, slot).wait()

        buf[slot] = dh0

        @pl.when(i == 0)
        def _():
            dg_ref[...] = dg
            dhead_ref[...] = dh0[:BLOCK]
            if tr > BLOCK:
                first = pltpu.make_async_copy(buf.at[0, pl.ds(BLOCK, tr - BLOCK)], gx_ref.at[pl.ds(0, tr - BLOCK)],
                                              sems.at[0])
                first.start()
                first.wait()

        @pl.when(i >= 1)
        def _():
            dg_ref[...] += dg
            out_copy(i, slot).start()

        @pl.when(i == nt - 1)
        def _():
            if nt >= 3:
                out_copy(nt - 2, (nt - 2) % 2).wait()
            if nt >= 2:
                out_copy(nt - 1, (nt - 1) % 2).wait()

    row = lambda w: pl.BlockSpec((tr, w), lambda i: (i, 0))
    full = lambda shape: pl.BlockSpec(shape, lambda i: (0,) * len(shape))
    return pl.pallas_call(
        body, name="inproj_dgrad", grid=(tp // tr,),
        in_specs=[row(ATTN_WIDTH), row(2 * KV_WIDTH), row(LRU_WIDTH), row(LRU_WIDTH), full(w_in.shape),
                  full(head.shape)] + _seq_specs(tr) + [row(D_MODEL), full(g.shape), full(token.shape)],
        out_specs=[pl.BlockSpec(memory_space=pl.ANY), full((BLOCK, D_MODEL)), full((1, D_MODEL))],
        out_shape=[jax.ShapeDtypeStruct(x.shape, F32), jax.ShapeDtypeStruct((BLOCK, D_MODEL), F32),
                   jax.ShapeDtypeStruct((1, D_MODEL), F32)],
        scratch_shapes=[pltpu.VMEM((2, tr, D_MODEL), F32), pltpu.SemaphoreType.DMA((2,))],
        compiler_params=_params("arbitrary"),
    )(dq, dkv, dxr, dyr, w_in, head, *([x] * qb), dh1, g, token)


def _dense_block_diag(w):
    eye = jnp.eye(LRU_BLOCKS, dtype=w.dtype)
    return (w[:, :, None, :] * eye[:, None, :, None]).reshape(LRU_WIDTH, LRU_WIDTH)


def _diag_blocks(dense):
    d4 = dense.reshape(LRU_BLOCKS, LRU_BLOCK, LRU_BLOCKS, LRU_BLOCK)
    return jnp.stack([d4[n, :, n, :] for n in range(LRU_BLOCKS)])


def _local_step(head, x, tgt, g_pre_mix, w_in, conv_w, conv_b, w_a, b_a, w_x, b_x, lam, sinks, g_post_mix,
                g_pre_ffn, g_post_ffn, late_weights, on_ffn_grads, on_outproj_bwd, on_mixer_grads, token):
    wa = _dense_block_diag(w_a).astype(BF16)
    wx = _dense_block_diag(w_x).astype(BF16)

    u0, qkv, xr, yr, hr, rec = _inproj_lru_fwd(head, x, g_pre_mix, w_in, conv_w, conv_b, wa, b_a, wx, b_x, lam, token)
    attn, probs, sink_probs = _attn_fwd(qkv, sinks)
    w_out, w1, w2 = late_weights([attn, rec])
    mix, h1, u1 = _outproj_fwd(attn, rec, w_out, head, x, g_post_mix, g_pre_ffn)
    r1, dy, df2, loss, dg_post_ffn = _ffn_fwd(u1, w1, w2, h1, tgt, g_post_ffn)

    da1, dh1, dmix, dg_pre_ffn, dg_post_mix = _ffn_bwd_data(df2, r1, w1, w2, dy, h1, mix, g_pre_ffn, g_post_mix)
    dw1, dw2 = _ffn_bwd_weights(u1, da1, r1, df2)
    token2 = on_ffn_grads(dw1, dw2)
    dattn, drec, dw_out = _outproj_bwd(dmix, w_out, attn, rec, token2)
    token3 = on_outproj_bwd(dattn)
    dxr, dyr, dwa, dwx, vec = _lru_bwd(xr, yr, hr, drec, conv_w, conv_b, wa, b_a, wx, b_x, lam, token3)
    dq, dkv, dkv_extra, dsinks = _attn_bwd(qkv, dattn, probs, sink_probs)
    dkv = _fix_dkv(dkv, dkv_extra)
    dw_in = _inproj_wgrad(dq, dkv, dxr, dyr, u0)
    token4 = on_mixer_grads(dw_in, dw_out)
    dx, dhead, dg_pre_mix = _inproj_dgrad(dq, dkv, dxr, dyr, w_in, head, x, dh1, g_pre_mix, token4)

    grads = dict(
        g_pre_mix=dg_pre_mix, conv_w=vec[0:4], conv_b=vec[4:5], w_a=_diag_blocks(dwa), b_a=vec[5:6],
        w_x=_diag_blocks(dwx), b_x=vec[6:7], lru_lambda=vec[7:8], attn_sinks=dsinks,
        g_post_mix=dg_post_mix, g_pre_ffn=dg_pre_ffn, g_post_ffn=dg_post_ffn)
    return loss, dx, dhead, grads


HBM = pl.BlockSpec(memory_space=pltpu.HBM)


def _mesh_pos():
    return lax.axis_index("x"), lax.axis_index("y"), lax.axis_index("c")


def _other_chips(x, y):
    return [(1 - x, y), (x, 1 - y), (1 - x, 1 - y)]


def _remote(src, dst, send_sem, recv_sem, to):
    return pltpu.make_async_remote_copy(src_ref=src, dst_ref=dst, send_sem=send_sem, recv_sem=recv_sem,
                                        device_id=to, device_id_type=MESH)


def _gather_weights(shards, lands, tiny, tiny_land):
    nbig = len(shards)

    def body(*refs):
        srcs, tiny_src = refs[:nbig], refs[nbig]
        outs, tiny_out = refs[2 * nbig + 2:3 * nbig + 2], refs[3 * nbig + 2]
        ici_send, ici_recv, d2d_send, d2d_recv, tiny_send, tiny_recv = refs[3 * nbig + 3:]
        x, y, c = _mesh_pos()
        me = 2 * x + y
        chips = _other_chips(x, y)
        sibling = (x, y, 1 - c)
        sends = []
        for w, (src, out) in enumerate(zip(srcs, outs)):
            hr = src.shape[0] // 2
            for j, chip in enumerate(chips):
                k = 3 * w + j
                cp = _remote(src.at[pl.ds(c * hr, hr)], out.at[me, pl.ds(c * hr, hr)],
                             ici_send.at[k], ici_recv.at[k], (*chip, c))
                cp.start()
                sends.append(cp)
        for j, chip in enumerate(chips):
            cp = _remote(tiny_src, tiny_out.at[me], tiny_send.at[j], tiny_recv.at[j], (*chip, c))
            cp.start()
            sends.append(cp)
        for w, (src, out) in enumerate(zip(srcs, outs)):
            hr = src.shape[0] // 2
            for j, (px, py) in enumerate(chips):
                k = 3 * w + j
                landed = out.at[2 * px + py, pl.ds(c * hr, hr)]
                _remote(landed, landed, ici_send.at[k], ici_recv.at[k], sibling).wait_recv()
                cp = _remote(landed, landed, d2d_send.at[k], d2d_recv.at[k], sibling)
                cp.start()
                sends.append(cp)
        for w, (src, out) in enumerate(zip(srcs, outs)):
            hr = src.shape[0] // 2
            for j, (px, py) in enumerate(chips):
                k = 3 * w + j
                other = out.at[2 * px + py, pl.ds((1 - c) * hr, hr)]
                _remote(other, other, d2d_send.at[k], d2d_recv.at[k], sibling).wait_recv()
        for j, (px, py) in enumerate(chips):
            blk = tiny_out.at[2 * px + py]
            _remote(blk, blk, tiny_send.at[j], tiny_recv.at[j], sibling).wait_recv()
        for cp in sends:
            cp.wait_send()

    out_shape = [jax.ShapeDtypeStruct(l.shape, l.dtype) for l in list(lands) + [tiny_land]]
    n = 3 * nbig
    return pl.pallas_call(
        body, name="gather_weights", out_shape=out_shape,
        in_specs=[HBM] * (2 * nbig + 2), out_specs=[HBM] * (nbig + 1),
        input_output_aliases={nbig + 1 + i: i for i in range(nbig + 1)},
        scratch_shapes=[pltpu.SemaphoreType.DMA((n,)),
                        pltpu.SemaphoreType.DMA((n,)), pltpu.SemaphoreType.DMA((n,)), pltpu.SemaphoreType.DMA((n,)),
                        pltpu.SemaphoreType.DMA((3,)), pltpu.SemaphoreType.DMA((3,))],
    )(*shards, tiny, *lands, tiny_land)


def _prep_shard(w, me):
    rows, cols = w.shape
    tr = 256 if rows % 256 == 0 else rows

    def body(me_ref, w_ref, s_ref, l_ref):
        b = w_ref[...].astype(BF16)
        s_ref[...] = b
        l_ref[0] = b

    return pl.pallas_call(
        body, name="prep_shard",
        grid_spec=pltpu.PrefetchScalarGridSpec(
            num_scalar_prefetch=1, grid=(rows // tr,),
            in_specs=[pl.BlockSpec((tr, cols), lambda i, me_ref: (i, 0))],
            out_specs=[pl.BlockSpec((tr, cols), lambda i, me_ref: (i, 0)),
                       pl.BlockSpec((1, tr, cols), lambda i, me_ref: (me_ref[0], i, 0))]),
        out_shape=[jax.ShapeDtypeStruct((rows, cols), BF16), jax.ShapeDtypeStruct((N_CHIPS, rows, cols), BF16)],
        compiler_params=_params("parallel"),
    )(me, w)


def _prep_tiny(tiny, me, slots=N_CHIPS):
    def body(me_ref, t_ref, l_ref):
        l_ref[0] = t_ref[...]

    return pl.pallas_call(
        body, name="prep_tiny",
        grid_spec=pltpu.PrefetchScalarGridSpec(
            num_scalar_prefetch=1, grid=(1,),
            in_specs=[pl.BlockSpec(tiny.shape, lambda i, me_ref: (0, 0))],
            out_specs=pl.BlockSpec((1,) + tiny.shape, lambda i, me_ref: (me_ref[0], 0, 0))),
        out_shape=jax.ShapeDtypeStruct((slots,) + tiny.shape, tiny.dtype),
    )(me, tiny)


N_DEV = 8


def _sibling_exchange(parts, token):
    def body(*refs):
        n = len(parts)
        srcs, outs, send_sems, recv_sems = refs[:n], refs[n + 1:2 * n + 1], refs[2 * n + 1], refs[2 * n + 2]
        x, y, c = _mesh_pos()
        sibling = (x, y, 1 - c)
        cps = []
        for w, (src, out) in enumerate(zip(srcs, outs)):
            hr = src.shape[1] // 2
            cp = _remote(src.at[:, pl.ds((1 - c) * hr, hr)], out, send_sems.at[w], recv_sems.at[w], sibling)
            cp.start()
            cps.append(cp)
        for cp in cps:
            cp.wait()

    n = len(parts)
    return pl.pallas_call(
        body, name="sibling_exchange",
        out_shape=[jax.ShapeDtypeStruct((p.shape[0], p.shape[1] // 2, p.shape[2]), p.dtype) for p in parts],
        in_specs=[HBM] * n + [pl.BlockSpec(memory_space=pl.ANY)], out_specs=[HBM] * n,
        scratch_shapes=[pltpu.SemaphoreType.DMA((n,)), pltpu.SemaphoreType.DMA((n,))],
    )(*parts, token)


def _chip_presum(part, from_sibling, pos):
    _, hr, cols = from_sibling.shape
    tr = 256 if hr % 256 == 0 else hr
    steps = hr // tr

    def body(pos_ref, a_ref, b_ref, o_ref, land_ref):
        s = (a_ref[...] + b_ref[...]).astype(BF16)
        o_ref[...] = s

        @pl.when(pl.program_id(1) == pos_ref[1])
        def _():
            land_ref[...] = s

    return pl.pallas_call(
        body, name="chip_presum",
        grid_spec=pltpu.PrefetchScalarGridSpec(
            num_scalar_prefetch=1, grid=(steps, N_CHIPS),
            in_specs=[pl.BlockSpec((1, tr, cols), lambda i, j, p: (j, p[0] * steps + i, 0)),
                      pl.BlockSpec((1, tr, cols), lambda i, j, p: (j, i, 0))],
            out_specs=[pl.BlockSpec((1, tr, cols), lambda i, j, p: (j, i, 0)),
                       pl.BlockSpec((1, tr, cols), lambda i, j, p: (p[1], p[0] * steps + i, 0))]),
        out_shape=[jax.ShapeDtypeStruct(from_sibling.shape, BF16),
                   jax.ShapeDtypeStruct((N_CHIPS, 2 * hr, cols), BF16)],
        compiler_params=_params("arbitrary", "arbitrary"),
    )(pos, part, from_sibling)


def _scatter_partials(cparts, lands, done_cparts=(), done_lands=()):
    n_new = len(cparts)
    nw = n_new + len(done_cparts)

    def body(*refs):
        srcs = refs[:nw]
        outs = refs[2 * nw:3 * nw]
        own_send, own_recv, ici_send, ici_recv, d2d_send, d2d_recv = refs[3 * nw:]
        x, y, c = _mesh_pos()
        me = 2 * x + y
        chips = _other_chips(x, y)
        sibling = (x, y, 1 - c)
        sends = []
        for w in list(range(n_new, nw)) + list(range(n_new)):
            src, out = srcs[w], outs[w]
            hr = src.shape[1]
            mine = out.at[me, pl.ds(c * hr, hr)]
            cp = _remote(src.at[me], mine, own_send.at[w], own_recv.at[w], sibling)
            cp.start()
            sends.append(cp)
            for j, (px, py) in enumerate(chips):
                if w >= n_new:
                    break
                k = 3 * w + j
                cp = _remote(src.at[2 * px + py], mine, ici_send.at[k], ici_recv.at[k], (px, py, c))
                cp.start()
                sends.append(cp)
        for w in list(range(n_new, nw)) + list(range(n_new)):
            src, out = srcs[w], outs[w]
            hr = src.shape[1]
            for j, (px, py) in enumerate(chips):
                k = 3 * w + j
                landed = out.at[2 * px + py, pl.ds(c * hr, hr)]
                if w < n_new:
                    _remote(landed, landed, ici_send.at[k], ici_recv.at[k], sibling).wait_recv()
                cp = _remote(landed, landed, d2d_send.at[k], d2d_recv.at[k], sibling)
                cp.start()
                sends.append(cp)
        for w, (src, out) in enumerate(zip(srcs, outs)):
            hr = src.shape[1]
            other = out.at[me, pl.ds((1 - c) * hr, hr)]
            _remote(other, other, own_send.at[w], own_recv.at[w], sibling).wait_recv()
            for j, (px, py) in enumerate(chips):
                k = 3 * w + j
                other = out.at[2 * px + py, pl.ds((1 - c) * hr, hr)]
                _remote(other, other, d2d_send.at[k], d2d_recv.at[k], sibling).wait_recv()
        for cp in sends:
            cp.wait_send()

    n = 3 * nw
    dma = pltpu.SemaphoreType.DMA
    every = list(cparts) + list(done_cparts)
    every_lands = list(lands) + list(done_lands)
    return pl.pallas_call(
        body, name="scatter_partials",
        out_shape=[jax.ShapeDtypeStruct(l.shape, l.dtype) for l in every_lands],
        in_specs=[HBM] * (2 * nw), out_specs=[HBM] * nw,
        input_output_aliases={nw + i: i for i in range(nw)},
        scratch_shapes=[dma((nw,)), dma((nw,)), dma((n,)), dma((n,)), dma((n,)), dma((n,))],
    )(*every, *every_lands)


SEM = pl.BlockSpec(memory_space=pltpu.SEMAPHORE)
SPLIT_COPY = pltpu.CompilerParams(has_side_effects=pltpu.SideEffectType.DATAFLOW_SIDE_EFFECTING)


def _hbm(a):
    return pltpu.with_memory_space_constraint(a, pltpu.HBM)


def _gather_copies(srcs, lands, send_sems, recv_sems):
    x, y, c = _mesh_pos()
    me = 2 * x + y
    sends, recvs = [], []
    for w, (src, land) in enumerate(zip(srcs, lands)):
        hr = src.shape[0] // 2
        for j, (px, py) in enumerate(_other_chips(x, y)):
            k = 3 * w + j
            sends.append(_remote(src.at[pl.ds(c * hr, hr)], land.at[me, pl.ds(c * hr, hr)],
                                 send_sems.at[k], recv_sems.at[k], (px, py, c)))
            got = land.at[2 * px + py, pl.ds(c * hr, hr)]
            recvs.append(_remote(got, got, send_sems.at[k], recv_sems.at[k], (px, py, c)))
    return sends, recvs


def _scatter_copies(srcs, lands, send_sems, recv_sems):
    x, y, c = _mesh_pos()
    me = 2 * x + y
    sends, recvs = [], []
    for w, (src, land) in enumerate(zip(srcs, lands)):
        hr = src.shape[1]
        for j, (px, py) in enumerate(_other_chips(x, y)):
            k = 3 * w + j
            sends.append(_remote(src.at[2 * px + py], land.at[me, pl.ds(c * hr, hr)],
                                 send_sems.at[k], recv_sems.at[k], (px, py, c)))
            got = land.at[2 * px + py, pl.ds(c * hr, hr)]
            recvs.append(_remote(got, got, send_sems.at[k], recv_sems.at[k], (px, py, c)))
    return sends, recvs


def _sibling_copies(srcs, lands, send_sems, recv_sems):
    x, y, c = _mesh_pos()
    sibling = (x, y, 1 - c)
    sends, recvs = [], []
    for w, (src, land) in enumerate(zip(srcs, lands)):
        hr = src.shape[1] // 2
        sends.append(_remote(src.at[:, pl.ds((1 - c) * hr, hr)], land, send_sems.at[w], recv_sems.at[w], sibling))
        recvs.append(_remote(land, land, send_sems.at[w], recv_sems.at[w], sibling))
    return sends, recvs


def _all_peers_copies(srcs, lands, send_sems, recv_sems):
    x, y, c = _mesh_pos()
    (src,), (land,) = srcs, lands
    flip = lambda v, bit: 1 - v if bit else v
    sends, recvs = [], []
    for k in range(N_DEV - 1):
        px, py, pc = flip(x, (k + 1) & 4), flip(y, (k + 1) & 2), flip(c, (k + 1) & 1)
        sends.append(_remote(src, land.at[4 * x + 2 * y + c], send_sems.at[k], recv_sems.at[k], (px, py, pc)))
        got = land.at[4 * px + 2 * py + pc]
        recvs.append(_remote(got, got, send_sems.at[k], recv_sems.at[k], (px, py, pc)))
    return sends, recvs


def _split_start(name, copies_of, srcs, land_shapes, n_copies=None):
    n = len(srcs)
    k = 3 * n if n_copies is None else n_copies

    def body(*refs):
        src_refs, land_refs = refs[:n], refs[n:2 * n]
        send_sems, recv_sems = refs[2 * n], refs[2 * n + 1]
        token = refs[-1]
        sends, _ = copies_of(src_refs, land_refs, send_sems, recv_sems)
        for cp in sends:
            cp.start()
        token[...] = jnp.zeros_like(token)

    lands = [_hbm(s) for s in land_shapes]
    dma = pltpu.SemaphoreType.DMA
    res = pl.pallas_call(
        body, name=name,
        out_shape=(dma((k,)), dma((k,)), *[pltpu.HBM(s.shape, s.dtype) for s in srcs],
                   *[pltpu.HBM(s.shape, s.dtype) for s in land_shapes], jax.ShapeDtypeStruct((8, 128), F32)),
        in_specs=[HBM] * (2 * n),
        out_specs=(SEM, SEM, *([HBM] * (2 * n)), pl.BlockSpec(memory_space=pltpu.VMEM)),
        input_output_aliases={i: 2 + i for i in range(2 * n)},
        compiler_params=SPLIT_COPY,
    )(*[_hbm(s) for s in srcs], *lands)
    return res[0], res[1], list(res[2:2 + n]), list(res[2 + n:2 + 2 * n]), res[-1]


def _split_wait(name, copies_of, send_sems, recv_sems, srcs, lands, after):
    n = len(srcs)

    def body(*refs):
        src_refs, land_refs = refs[:n], refs[n:2 * n]
        sends, recvs = copies_of(src_refs, land_refs, refs[2 * n], refs[2 * n + 1])
        for cp in sends:
            cp.wait_send()
        for cp in recvs:
            cp.wait_recv()

    res = pl.pallas_call(
        body, name=name,
        out_shape=tuple(pltpu.HBM(s.shape, s.dtype) for s in list(srcs) + list(lands)),
        in_specs=[HBM] * (2 * n) + [SEM, SEM] + [pl.BlockSpec(memory_space=pl.ANY)] * len(after),
        out_specs=tuple([HBM] * (2 * n)),
        input_output_aliases={i: i for i in range(2 * n)},
        compiler_params=SPLIT_COPY,
    )(*srcs, *lands, send_sems, recv_sems, *after)
    return list(res[:n]), list(res[n:])


def _gather_finish(lands):
    n = len(lands)

    def body(*refs):
        outs = refs[n:2 * n]
        d2d_send, d2d_recv = refs[2 * n:]
        x, y, c = _mesh_pos()
        chips = _other_chips(x, y)
        sibling = (x, y, 1 - c)
        sends = []
        for w, out in enumerate(outs):
            hr = out.shape[1] // 2
            for j, (px, py) in enumerate(chips):
                landed = out.at[2 * px + py, pl.ds(c * hr, hr)]
                cp = _remote(landed, landed, d2d_send.at[3 * w + j], d2d_recv.at[3 * w + j], sibling)
                cp.start()
                sends.append(cp)
        for w, out in enumerate(outs):
            hr = out.shape[1] // 2
            for j, (px, py) in enumerate(chips):
                other = out.at[2 * px + py, pl.ds((1 - c) * hr, hr)]
                _remote(other, other, d2d_send.at[3 * w + j], d2d_recv.at[3 * w + j], sibling).wait_recv()
        for cp in sends:
            cp.wait_send()

    dma = pltpu.SemaphoreType.DMA
    return pl.pallas_call(
        body, name="gather_finish",
        out_shape=[jax.ShapeDtypeStruct(l.shape, l.dtype) for l in lands],
        in_specs=[HBM] * n, out_specs=[HBM] * n,
        input_output_aliases={i: i for i in range(n)},
        scratch_shapes=[dma((3 * n,)), dma((3 * n,))],
    )(*lands)


def _adamw(w, g, m, v):
    m = ADAM_B1 * m + (1.0 - ADAM_B1) * g
    v = ADAM_B2 * v + (1.0 - ADAM_B2) * (g * g)
    m_hat = m / (1.0 - ADAM_B1 ** ADAM_STEP)
    v_hat = v / (1.0 - ADAM_B2 ** ADAM_STEP)
    delta = -ADAM_LR * (m_hat / (jnp.sqrt(v_hat) + ADAM_EPS) + ADAM_WD * w)
    return delta, m, v


def _adamw_big(partials, w, m, v):
    rows, cols = w.shape
    tr = 256 if rows % 256 == 0 else rows

    def body(p_ref, w_ref, m_ref, v_ref, g_ref, d_ref, m2_ref, v2_ref):
        g = ((p_ref[0].astype(F32) + p_ref[1].astype(F32)) + p_ref[2].astype(F32)) + p_ref[3].astype(F32)
        g_ref[...] = g
        d_ref[...], m2_ref[...], v2_ref[...] = _adamw(w_ref[...], g, m_ref[...], v_ref[...])

    blk = pl.BlockSpec((tr, cols), lambda i: (i, 0))
    return pl.pallas_call(
        body, name="adamw_big", grid=(rows // tr,),
        in_specs=[pl.BlockSpec((N_CHIPS, tr, cols), lambda i: (0, i, 0)), blk, blk, blk],
        out_specs=[blk] * 4, out_shape=[jax.ShapeDtypeStruct((rows, cols), F32)] * 4,
        compiler_params=_params("parallel"),
    )(partials, w, m, v)


def _sum_devices(gathered, rows):
    cols = gathered.shape[1]

    def body(g_ref, o_ref):
        acc = g_ref[0:rows]
        for d in range(1, N_DEV):
            acc = acc + g_ref[d * rows:(d + 1) * rows]
        o_ref[...] = acc

    return pl.pallas_call(
        body, name="sum_devices", out_shape=jax.ShapeDtypeStruct((rows, cols), F32),
        in_specs=[pl.BlockSpec(memory_space=pltpu.VMEM)], out_specs=pl.BlockSpec(memory_space=pltpu.VMEM),
        compiler_params=pltpu.CompilerParams(vmem_limit_bytes=VMEM_LIMIT_V7X),
    )(gathered)


def _adamw_small(quads):
    n = len(quads)

    def body(*refs):
        ins, outs = refs[:4 * n], refs[4 * n:]
        for t in range(n):
            w, g, m, v = (r[...] for r in ins[4 * t:4 * t + 4])
            outs[3 * t][...], outs[3 * t + 1][...], outs[3 * t + 2][...] = _adamw(w, g, m, v)

    flat = [a for q in quads for a in q]
    vm = pl.BlockSpec(memory_space=pltpu.VMEM)
    res = pl.pallas_call(
        body, name="adamw_small",
        out_shape=[jax.ShapeDtypeStruct(q[0].shape, F32) for q in quads for _ in range(3)],
        in_specs=[vm] * (4 * n), out_specs=[vm] * (3 * n),
    )(*flat)
    return [tuple(res[3 * t:3 * t + 3]) for t in range(n)]


SMALL_PACK_ROWS = 96
_WEIGHTS = ['meta_tokens', 'g_pre_mix', 'w_in', 'conv_w', 'conv_b', 'w_a', 'b_a', 'w_x', 'b_x', 'lru_lambda',
            'attn_sinks', 'w_out', 'g_post_mix', 'g_pre_ffn', 'w_ff1', 'w_ff2', 'g_post_ffn']
_BIG = ['w_in', 'w_out', 'w_ff1', 'w_ff2']


def _pack_small(dmeta, g, loss):
    z = lambda r, c: jnp.zeros((r, c), F32)
    rows = [
        dmeta,
        g['g_pre_mix'], g['g_post_mix'], g['g_pre_ffn'], g['g_post_ffn'],
        jnp.concatenate([g['conv_w'], z(4, 512)], axis=1),
        jnp.concatenate([g['conv_b'], g['b_a']], axis=1),
        jnp.concatenate([g['b_x'], g['lru_lambda']], axis=1),
        jnp.concatenate([g['attn_sinks'], z(1, D_MODEL - ATTN_HEADS)], axis=1),
        jnp.concatenate([loss, z(1, D_MODEL - 1)], axis=1),
        z(4, D_MODEL),
        g['w_a'].reshape(32, D_MODEL), g['w_x'].reshape(32, D_MODEL),
    ]
    return jnp.concatenate(rows, axis=0)


def _unpack_small(s, chip):
    return dict(
        meta_tokens=lax.dynamic_slice(s[0:16], (0, chip * 256), (16, 256)),
        g_pre_mix=s[16:17], g_post_mix=s[17:18], g_pre_ffn=s[18:19], g_post_ffn=s[19:20],
        conv_w=lax.dynamic_slice(s[20:24], (0, chip * 128), (4, 128)).reshape(1, 4, 128),
        conv_b=s[24:25, :512], b_a=s[24:25, 512:], b_x=s[25:26, :512], lru_lambda=s[25:26, 512:],
        attn_sinks=s[26:27, :ATTN_HEADS], loss=s[27, 0],
        w_a=s[32:64].reshape(1, LRU_BLOCKS, LRU_BLOCK, LRU_BLOCK),
        w_x=s[64:96].reshape(1, LRU_BLOCKS, LRU_BLOCK, LRU_BLOCK))


def _as2d(a):
    if a.ndim == 2:
        return a
    return a.reshape(-1, a.shape[-1])


def kernel(x, meta_tokens, g_pre_mix, w_in, conv_w, conv_b, w_a, b_a, w_x, b_x, lru_lambda, attn_sinks, w_out, g_post_mix, g_pre_ffn, w_ff1, w_ff2, g_post_ffn, loss_target, m_meta_tokens, m_g_pre_mix, m_w_in, m_conv_w, m_conv_b, m_w_a, m_b_a, m_w_x, m_b_x, m_lru_lambda, m_attn_sinks, m_w_out, m_g_post_mix, m_g_pre_ffn, m_w_ff1, m_w_ff2, m_g_post_ffn, v_meta_tokens, v_g_pre_mix, v_w_in, v_conv_w, v_conv_b, v_w_a, v_b_a, v_w_x, v_b_x, v_lru_lambda, v_attn_sinks, v_w_out, v_g_post_mix, v_g_pre_ffn, v_w_ff1, v_w_ff2, v_g_post_ffn):
    weights = dict(meta_tokens=meta_tokens, g_pre_mix=g_pre_mix, w_in=w_in, conv_w=conv_w, conv_b=conv_b, w_a=w_a,
                   b_a=b_a, w_x=w_x, b_x=b_x, lru_lambda=lru_lambda, attn_sinks=attn_sinks, w_out=w_out,
                   g_post_mix=g_post_mix, g_pre_ffn=g_pre_ffn, w_ff1=w_ff1, w_ff2=w_ff2, g_post_ffn=g_post_ffn)
    mom1 = dict(zip(_WEIGHTS, [m_meta_tokens, m_g_pre_mix, m_w_in, m_conv_w, m_conv_b, m_w_a, m_b_a, m_w_x, m_b_x,
                               m_lru_lambda, m_attn_sinks, m_w_out, m_g_post_mix, m_g_pre_ffn, m_w_ff1, m_w_ff2,
                               m_g_post_ffn]))
    mom2 = dict(zip(_WEIGHTS, [v_meta_tokens, v_g_pre_mix, v_w_in, v_conv_w, v_conv_b, v_w_a, v_b_a, v_w_x, v_b_x,
                               v_lru_lambda, v_attn_sinks, v_w_out, v_g_post_mix, v_g_pre_ffn, v_w_ff1, v_w_ff2,
                               v_g_post_ffn]))
    xi, yi, ci = _mesh_pos()
    chip = 2 * xi + yi

    tiny = jnp.concatenate([meta_tokens, jnp.pad(conv_w[0], ((0, 4), (0, 128)))], axis=0)
    chip_arr = jnp.reshape(chip, (1,)).astype(jnp.int32)
    big2d = lambda a, name: a[0].T if name == 'w_in' else a[0]
    shards, lands = zip(*[_prep_shard(big2d(weights[n], n), chip_arr) for n in _BIG])
    g_in, g_tiny = _gather_weights(shards[:1], lands[:1], tiny, _prep_tiny(tiny, chip_arr))
    w_in_full = g_in.reshape(IN_WIDTH, D_MODEL)
    meta_full = jnp.concatenate([g_tiny[j, :N_META] for j in range(N_CHIPS)], axis=1)
    conv_w_full = jnp.concatenate([g_tiny[j, N_META:N_META + 4, :128] for j in range(N_CHIPS)], axis=1)
    g_send, g_recv, late_thru, late_lands, token = _split_start(
        "gather_late_start", _gather_copies, shards[1:], lands[1:])

    def late_weights(after):
        _, landed = _split_wait("gather_late_wait", _gather_copies, g_send, g_recv, late_thru, late_lands, after)
        g_out, g_f1, g_f2 = _gather_finish(landed)
        return g_out.reshape(D_MODEL, D_MODEL), g_f1, g_f2

    pos = jnp.stack([ci, chip]).astype(jnp.int32)
    ffn = {}


    def on_ffn_grads(dw1, dw2):
        parts = [dw1, dw2]
        lands = [lax.empty((p.shape[0], p.shape[1] // 2, p.shape[2]), p.dtype) for p in parts]
        ffn['sib'] = _split_start("sibling_ffn_start", _sibling_copies, parts, lands, len(parts))
        return ffn['sib'][4]

    def on_outproj_bwd(dattn):
        send, recv, thru, lands, _ = ffn['sib']
        parts, from_sibling = _split_wait("sibling_ffn_wait", _sibling_copies, send, recv, thru, lands, [dattn])
        cparts_ffn, lands_ffn = zip(*[_chip_presum(p, r, pos) for p, r in zip(parts, from_sibling)])
        ffn['send'], ffn['recv'], ffn['thru'], ffn['lands'], token3 = _split_start(
            "scatter_ffn_start", _scatter_copies, cparts_ffn, lands_ffn)
        return token3

    def on_mixer_grads(dw_in, dw_out):
        parts = [dw_in.reshape(N_CHIPS, IN_WIDTH // N_CHIPS, D_MODEL),
                 dw_out.reshape(N_CHIPS, D_MODEL // N_CHIPS, D_MODEL)]
        cparts, lands = zip(*[_chip_presum(p, r, pos) for p, r in zip(parts, _sibling_exchange(parts, pos))])
        ffn['mixer'] = _split_start("scatter_mixer_start", _scatter_copies, cparts, lands)
        return ffn['mixer'][4]

    head = jnp.concatenate([jnp.zeros((PAD_ROWS, D_MODEL), F32), meta_full], axis=0)
    loss, dx, dhead, grads = _local_step(head, x[0], loss_target[0], g_pre_mix, w_in_full, conv_w_full, conv_b, w_a[0],
                                         b_a, w_x[0], b_x, lru_lambda, attn_sinks, g_post_mix, g_pre_ffn, g_post_ffn,
                                         late_weights, on_ffn_grads, on_outproj_bwd, on_mixer_grads, token)
    grad_x = dx[None]

    pack = _pack_small(dhead[PAD_ROWS:], grads, loss)
    dev = jnp.reshape(4 * xi + 2 * yi + ci, (1,)).astype(jnp.int32)
    s_send, s_recv, s_thru, s_lands, token5 = _split_start(
        "gather_small_start", _all_peers_copies, [pack], [_prep_tiny(pack, dev, N_DEV)], N_DEV - 1)

    send, recv, thru, lands, _ = ffn['mixer']
    mixer_cparts, mixer_lands = _split_wait("scatter_mixer_wait", _scatter_copies, send, recv, thru, lands, [token5])
    ffn_cparts, ffn_lands = _split_wait("scatter_ffn_wait", _scatter_copies, ffn['send'], ffn['recv'], ffn['thru'],
                                        ffn['lands'], mixer_lands)
    chip_partials = _scatter_partials([], [], mixer_cparts + ffn_cparts, mixer_lands + ffn_lands)

    g_out_d, delta, new_m, new_v = {}, {}, {}, {}
    for name, part in zip(_BIG, chip_partials):
        shp = weights[name].shape
        res = _adamw_big(part, big2d(weights[name], name), big2d(mom1[name], name), big2d(mom2[name], name))
        g_out_d[name], delta[name], new_m[name], new_v[name] = (big2d(r[None], name).reshape(shp) for r in res)

    _, (gathered,) = _split_wait("gather_small_wait", _all_peers_copies, s_send, s_recv, s_thru, s_lands,
                                 [g_out_d[n] for n in _BIG])
    small = _unpack_small(_sum_devices(gathered.reshape(N_DEV * SMALL_PACK_ROWS, D_MODEL), SMALL_PACK_ROWS), chip)
    loss = small['loss']
    small_names = [n for n in _WEIGHTS if n not in _BIG]
    quads = [(_as2d(weights[n]), _as2d(small[n]), _as2d(mom1[n]), _as2d(mom2[n])) for n in small_names]
    for name, (d, m2, v2) in zip(small_names, _adamw_small(quads)):
        shp = weights[name].shape
        g_out_d[name] = small[name].reshape(shp)
        delta[name], new_m[name], new_v[name] = d.reshape(shp), m2.reshape(shp), v2.reshape(shp)

    return (loss, grad_x, *[g_out_d[n] for n in _WEIGHTS], *[delta[n] for n in _WEIGHTS],
            *[new_m[n] for n in _WEIGHTS], *[new_v[n] for n in _WEIGHTS])
```

```python
import numpy as np
import jax
import jax.numpy as jnp
from jax import lax
from jax.experimental import pallas as pl
from jax.experimental.pallas import tpu as pltpu

F32 = jnp.float32
BF16 = jnp.bfloat16

D_MODEL = 1024
N_META = 16
BLOCK = 128
PAD_ROWS = BLOCK - N_META
HEAD_DIM = 64
ATTN_HEADS = 8
GQA_GROUP = 4
ATTN_WIDTH = 512
KV_WIDTH = 128
QKV_WIDTH = ATTN_WIDTH + 2 * KV_WIDTH
LRU_WIDTH = 512
LRU_BLOCKS = 8
LRU_BLOCK = 64
LRU_C = 8.0
IN_WIDTH = 1792
D_FF = 4096
N_CHIPS = 4
FF_CHUNK = D_FF // N_CHIPS
EPS = 1e-6
NEG = -1e30

ADAM_LR = 0.001
ADAM_B1 = 0.9
ADAM_B2 = 0.999
ADAM_EPS = 1e-08
ADAM_WD = 0.01
ADAM_STEP = 10

VMEM_LIMIT_V7X = 62 * 1024 * 1024
MESH = pl.DeviceIdType.MESH

NT = (((1,), (1,)), ((), ()))
TN = (((0,), (0,)), ((), ()))


def _row_tile(tp):
    return 640 if tp % 640 == 0 else BLOCK


def _wgrad_row_tile(tp):
    return 1664 if tp % 1664 == 0 else _row_tile(tp)


def _params(*sem):
    return pltpu.CompilerParams(dimension_semantics=sem, vmem_limit_bytes=VMEM_LIMIT_V7X)


def _dot(a, b):
    return jnp.dot(a, b, preferred_element_type=F32)


def _dot_nt(a, b):
    return lax.dot_general(a, b, NT, preferred_element_type=F32)


def _dot_tn(a, b):
    return lax.dot_general(a, b, TN, preferred_element_type=F32)


def _rms(x):
    rs = lax.rsqrt(jnp.mean(x * x, axis=-1, keepdims=True) + EPS)
    return x * rs, rs


def _rms_bwd(xhat, rs, g, dy):
    dyg = dy * g
    dx = rs * (dyg - xhat * jnp.mean(dyg * xhat, axis=-1, keepdims=True))
    dg = jnp.sum(dy * xhat, axis=0, keepdims=True)
    return dx, dg


def _gelu(x):
    k = 0.7978845608028654
    t = jnp.tanh(x * (k + (k * 0.044715) * (x * x)))
    return (0.5 * x) * (1.0 + t), t


def _gelu_grad(x, t):
    k = 0.7978845608028654
    return 0.5 * (1.0 + t) + 0.5 * x * (1.0 - t * t) * k * (1.0 + 3 * 0.044715 * x * x)


def _sigmoid(x):
    return 0.5 * jnp.tanh(0.5 * x) + 0.5


def _one_minus_exp2(y):
    t = jnp.tanh(y)
    return (-2.0 * t) / (1.0 - t)


def _softplus(x):
    return jnp.maximum(x, 0.0) + jnp.log1p(jnp.exp(-jnp.abs(x)))


def _seq_specs(tr, delay=0):
    qb = tr // BLOCK
    tile = lambda i: jnp.maximum(i - delay, 0)
    return [pl.BlockSpec((BLOCK, D_MODEL), lambda i, *_, s=s: (jnp.maximum(tile(i) * qb + s - 1, 0), 0))
            for s in range(qb)]


def _seq_tile(head, pieces, i):
    first = jnp.where(i == 0, head, pieces[0][...])
    return jnp.concatenate([first] + [p[...] for p in pieces[1:]], axis=0)


GROUP_ROWS = GQA_GROUP * BLOCK


def _attn_bias():
    j = np.arange(2 * BLOCK)[:, None]
    i = np.arange(BLOCK)[None, :]
    band = (j - i >= 1) & (j - i <= BLOCK)
    out = []
    for n in range(3):
        ok = band & ((n - 1) * BLOCK + j >= PAD_ROWS) if n < 2 else band
        out.append(np.tile(np.where(ok, 0.0, NEG).astype(np.float32), (1, GQA_GROUP)))
    return jnp.asarray(np.stack(out))


def _heads_t(at, g):
    heads = range(GQA_GROUP * g, GQA_GROUP * (g + 1))
    return jnp.concatenate([at[h * HEAD_DIM:(h + 1) * HEAD_DIM] for h in heads], axis=1).astype(BF16)


def _from_heads_t(groups):
    pairs = []
    for p in groups:
        for h in range(0, GQA_GROUP, 2):
            two = jnp.concatenate([p[:, h * BLOCK:(h + 1) * BLOCK], p[:, (h + 1) * BLOCK:(h + 2) * BLOCK]], axis=0)
            pairs.append(two.T)
    return jnp.concatenate(pairs, axis=1)


def _stack_heads(a, g):
    heads = range(GQA_GROUP * g, GQA_GROUP * (g + 1))
    return jnp.concatenate([a[:, h * HEAD_DIM:(h + 1) * HEAD_DIM] for h in heads], axis=0)


def _unstack_heads(groups):
    return jnp.concatenate([p[h * BLOCK:(h + 1) * BLOCK] for p in groups for h in range(GQA_GROUP)], axis=1)


def _attn_probs_t(k_g, qg, bias, sink_row):
    st = _dot_nt(k_g, qg) + bias
    m = jnp.maximum(jnp.max(st, axis=0, keepdims=True), sink_row)
    p = jnp.exp(st - m)
    es = jnp.exp(sink_row - m)
    inv = 1.0 / (jnp.sum(p, axis=0, keepdims=True) + es)
    return p * inv, es * inv


def _attn_consts(sinks):
    return jnp.repeat(sinks.reshape(ATTN_HEADS), BLOCK).reshape(ATTN_HEADS // GQA_GROUP, GROUP_ROWS), _attn_bias()


_SINK_SPEC = pl.BlockSpec((ATTN_HEADS // GQA_GROUP, GROUP_ROWS), lambda n: (0, 0))
_BIAS_SPEC = pl.BlockSpec((3, 2 * BLOCK, GROUP_ROWS), lambda n: (0, 0, 0))
_QSCALE = HEAD_DIM ** -0.5


def _kv_specs(tr):
    qb = tr // BLOCK
    prev = lambda col: pl.BlockSpec((BLOCK, KV_WIDTH), lambda t: (jnp.maximum(t * qb - 1, 0), col))
    cur = lambda col: pl.BlockSpec((tr, KV_WIDTH), lambda t: (t, col))
    return [prev(4), cur(4), prev(5), cur(5)]


def _block_bias(b_ref, t, qb, i):
    return b_ref[2] if i >= 2 else b_ref[jnp.minimum(t * qb + i, 2)]


N_KV = ATTN_HEADS // GQA_GROUP


def _prob_specs(qb):
    return [pl.BlockSpec((qb, N_KV, 2 * BLOCK, GROUP_ROWS), lambda t: (t, 0, 0, 0)),
            pl.BlockSpec((qb, SUBLANES, GROUP_ROWS), lambda t: (t, 0, 0))]


def _attn_fwd(qkv, sinks):
    tp = qkv.shape[0]
    tr = _row_tile(tp)
    qb, nb = tr // BLOCK, tp // BLOCK
    sink_rows, bias = _attn_consts(sinks)

    def body(s_ref, b_ref, q_ref, kp_ref, kc_ref, vp_ref, vc_ref, o_ref, p_ref, ps_ref):
        t = pl.program_id(0)
        k_all = jnp.concatenate([kp_ref[...], kc_ref[...]], axis=0)
        v_all = jnp.concatenate([vp_ref[...], vc_ref[...]], axis=0)
        for i in range(qb):
            rows = slice(i * BLOCK, (i + 1) * BLOCK)
            q = q_ref[rows]
            k2, v2 = k_all[i * BLOCK:(i + 2) * BLOCK], v_all[i * BLOCK:(i + 2) * BLOCK]
            bias_n = _block_bias(b_ref, t, qb, i)
            outs, sink_probs = [], []
            for g in range(N_KV):
                cols = slice(g * HEAD_DIM, (g + 1) * HEAD_DIM)
                qg = _stack_heads(q, g) * jnp.asarray(_QSCALE, BF16)
                p, ps = _attn_probs_t(k2[:, cols], qg, bias_n, s_ref[g:g + 1])
                pb = p.astype(BF16)
                p_ref[i, g] = pb
                sink_probs.append(ps)
                outs.append(_dot_tn(pb, v2[:, cols]))
            o_ref[rows] = _unstack_heads(outs).astype(BF16)
            ps_ref[i] = jnp.concatenate(sink_probs + [jnp.zeros((SUBLANES - N_KV, GROUP_ROWS), F32)], axis=0)

    return pl.pallas_call(
        body, name="attn_fwd", grid=(tp // tr,),
        in_specs=[_SINK_SPEC, _BIAS_SPEC, pl.BlockSpec((tr, ATTN_WIDTH), lambda t: (t, 0))] + _kv_specs(tr),
        out_specs=[pl.BlockSpec((tr, ATTN_WIDTH), lambda t: (t, 0))] + _prob_specs(qb),
        out_shape=[jax.ShapeDtypeStruct((tp, ATTN_WIDTH), BF16),
                   jax.ShapeDtypeStruct((nb, N_KV, 2 * BLOCK, GROUP_ROWS), BF16),
                   jax.ShapeDtypeStruct((nb, SUBLANES, GROUP_ROWS), F32)],
        compiler_params=_params("parallel"),
    )(sink_rows, bias, qkv, qkv, qkv, qkv, qkv)


def _conv_taps(x, halo):
    ext = jnp.concatenate([halo, x], axis=0)
    return [ext[8:] if k == 3 else pltpu.roll(ext, 3 - k, 0)[8:] for k in range(4)]


def _lru_gates(xc, wa, ba, wx, bx, sp):
    xb = xc.astype(BF16)
    r = _sigmoid(_dot(xb, wa) + ba)
    ig = _sigmoid(_dot(xb, wx) + bx)
    log_a = (-LRU_C * sp) * r
    a = jnp.exp(log_a)
    mult = jnp.sqrt(_one_minus_exp2(log_a))
    return xb, r, ig, a, mult


SUBLANES = 8


def _scan_fwd(a, b, h_in):
    n, width = a.shape
    a, b = (v.reshape(n // SUBLANES, SUBLANES, width) for v in (a, b))
    in_group = lax.broadcasted_iota(jnp.int32, a.shape, 1)
    for d in (1, 2, 4):
        keep = in_group >= d
        b = jnp.where(keep, a * pltpu.roll(b, d, 1) + b, b)
        a = jnp.where(keep, a * pltpu.roll(a, d, 1), a)
    a, b = a.reshape(n, width), b.reshape(n, width)
    out, carry = [], h_in
    for g in range(0, n, SUBLANES):
        h = a[g:g + SUBLANES] * carry + b[g:g + SUBLANES]
        out.append(h)
        carry = h[SUBLANES - 1:]
    return jnp.concatenate(out, axis=0)


def _scan_rev(c, b, g_in):
    n, width = c.shape
    c, b = (v.reshape(n // SUBLANES, SUBLANES, width) for v in (c, b))
    in_group = lax.broadcasted_iota(jnp.int32, c.shape, 1)
    for d in (1, 2, 4):
        keep = in_group < SUBLANES - d
        b = jnp.where(keep, b + c * pltpu.roll(b, SUBLANES - d, 1), b)
        c = jnp.where(keep, c * pltpu.roll(c, SUBLANES - d, 1), c)
    c, b = c.reshape(n, width), b.reshape(n, width)
    out, carry = [], g_in
    for g in range(n - SUBLANES, -1, -SUBLANES):
        r = b[g:g + SUBLANES] + c[g:g + SUBLANES] * carry
        out.append(r)
        carry = r[:1]
    return jnp.concatenate(out[::-1], axis=0)


def _inproj_lru_fwd(head, x, g, w_in, conv_w, conv_b, wa, ba, wx, bx, lam, token):
    tp = BLOCK + x.shape[0]
    tr = _row_tile(tp)
    qb, nt = tr // BLOCK, tp // tr
    small = [conv_w, conv_b, wa, ba, wx, bx, lam]

    def body(*refs):
        head_ref, pieces = refs[0], refs[1:1 + qb]
        g_ref, w_ref, _, cw_ref, cb_ref, wa_ref, ba_ref, wx_ref, bx_ref, lam_ref = refs[1 + qb:11 + qb]
        u_ref, qkv_ref, xr_ref, yr_ref, hr_ref, rec_ref, zbuf, halo, hprev = refs[11 + qb:]
        i = pl.program_id(0)
        cur = i % 2

        @pl.when(i == 0)
        def _():
            halo[...] = jnp.zeros_like(halo)
            hprev[...] = jnp.zeros_like(hprev)
            zbuf[1] = jnp.zeros((tr, 2 * LRU_WIDTH), F32)

        def recurrent_branch(valid):
            cw, cb = cw_ref[...], cb_ref[...]
            wa_m, ba_v, wx_m, bx_v = wa_ref[...], ba_ref[...], wx_ref[...], bx_ref[...]
            sp = _softplus(-lam_ref[...])
            before, h_last = halo[...], hprev[0:1]
            for b in range(qb):
                rows = slice(b * BLOCK, (b + 1) * BLOCK)
                xy = zbuf[1 - cur, rows]
                xin = xy[:, :LRU_WIDTH]
                taps = _conv_taps(xin, before)
                before = xin[BLOCK - 8:]
                xc = cb + sum(cw[k:k + 1] * taps[k] for k in range(4))
                _, _, ig, a, mult = _lru_gates(xc, wa_m, ba_v, wx_m, bx_v, sp)
                u = mult * (ig * xc)
                if b == 0:
                    pos = (i - 1) * tr + lax.broadcasted_iota(jnp.int32, xc.shape, 0)
                    u = jnp.where(pos >= PAD_ROWS, u, 0.0)
                h = _scan_fwd(a, u, h_last)
                h_last = h[BLOCK - 1:]
                hr_ref[rows] = h
                gl, _ = _gelu(xy[:, LRU_WIDTH:])
                rec_ref[rows] = (gl * h).astype(BF16)
            halo[...] = jnp.where(valid, before, 0.0)
            hprev[0:1] = jnp.where(valid, h_last, 0.0)

        def projection():
            xhat, _ = _rms(_seq_tile(head_ref[...], pieces, i))
            u = (xhat * g_ref[...]).astype(BF16)
            u_ref[...] = u
            z = _dot_nt(u, w_ref[...])
            qkv_ref[...] = z[:, :QKV_WIDTH].astype(BF16)
            xr_ref[...] = z[:, QKV_WIDTH:QKV_WIDTH + LRU_WIDTH]
            yr_ref[...] = z[:, QKV_WIDTH + LRU_WIDTH:]
            zbuf[cur] = z[:, QKV_WIDTH:]

        @pl.when(i < nt)
        def _():
            recurrent_branch(i >= 1)
            projection()

        @pl.when(i == nt)
        def _():
            recurrent_branch(True)

    last = nt - 1
    this_row = lambda w: pl.BlockSpec((tr, w), lambda i: (jnp.minimum(i, last), 0))
    prev_row = lambda w: pl.BlockSpec((tr, w), lambda i: (jnp.maximum(i - 1, 0), 0))
    full = lambda a: pl.BlockSpec(a.shape, lambda i: (0,) * a.ndim)
    piece_specs = [pl.BlockSpec((BLOCK, D_MODEL), lambda i, s=s: (jnp.maximum(jnp.minimum(i, last) * qb + s - 1, 0), 0))
                   for s in range(qb)]
    return pl.pallas_call(
        body, name="inproj_lru_fwd", grid=(nt + 1,),
        in_specs=[full(head)] + piece_specs + [full(g), full(w_in), full(token)] + [full(a) for a in small],
        out_specs=[this_row(D_MODEL), this_row(QKV_WIDTH), this_row(LRU_WIDTH), this_row(LRU_WIDTH),
                   prev_row(LRU_WIDTH), prev_row(LRU_WIDTH)],
        out_shape=[jax.ShapeDtypeStruct((tp, D_MODEL), BF16), jax.ShapeDtypeStruct((tp, QKV_WIDTH), BF16),
                   jax.ShapeDtypeStruct((tp, LRU_WIDTH), F32), jax.ShapeDtypeStruct((tp, LRU_WIDTH), F32),
                   jax.ShapeDtypeStruct((tp, LRU_WIDTH), F32), jax.ShapeDtypeStruct((tp, LRU_WIDTH), BF16)],
        scratch_shapes=[pltpu.VMEM((2, tr, 2 * LRU_WIDTH), F32), pltpu.VMEM((8, LRU_WIDTH), F32),
                        pltpu.VMEM((8, LRU_WIDTH), F32)],
        compiler_params=_params("arbitrary"),
    )(head, *([x] * qb), g, w_in, token, *small)


def _outproj_fwd(attn, rec, w_out, head, x, g_post_mix, g_pre_ffn):
    tp = attn.shape[0]
    tr = _row_tile(tp)
    qb = tr // BLOCK

    def body(*refs):
        a_ref, r_ref, w_ref, head_ref = refs[:4]
        pieces = refs[4:4 + qb]
        gm_ref, gf_ref, mix_ref, h1_ref, u1_ref = refs[4 + qb:]
        mix = _dot(a_ref[...], w_ref[:ATTN_WIDTH]) + _dot(r_ref[...], w_ref[ATTN_WIDTH:])
        mix_ref[...] = mix
        mhat, _ = _rms(mix)
        h1 = _seq_tile(head_ref[...], pieces, pl.program_id(0)) + mhat * gm_ref[...]
        h1_ref[...] = h1
        hhat, _ = _rms(h1)
        u1_ref[...] = (hhat * gf_ref[...]).astype(BF16)

    row = lambda w: pl.BlockSpec((tr, w), lambda i: (i, 0))
    full = lambda a: pl.BlockSpec(a.shape, lambda i: (0,) * a.ndim)
    return pl.pallas_call(
        body, name="outproj_fwd", grid=(tp // tr,),
        in_specs=[row(ATTN_WIDTH), row(LRU_WIDTH), full(w_out), full(head)] + _seq_specs(tr)
        + [full(g_post_mix), full(g_pre_ffn)],
        out_specs=[row(D_MODEL), row(D_MODEL), row(D_MODEL)],
        out_shape=[jax.ShapeDtypeStruct((tp, D_MODEL), F32), jax.ShapeDtypeStruct((tp, D_MODEL), F32),
                   jax.ShapeDtypeStruct((tp, D_MODEL), BF16)],
        compiler_params=_params("parallel"),
    )(attn, rec, w_out, head, *([x] * qb), g_post_mix, g_pre_ffn)


def _resident(a):
    return pl.BlockSpec(a.shape, lambda *_: (0,) * a.ndim, pipeline_mode=pl.Buffered(1))


def _ffn_fwd(u1, w1, w2, h1, tgt, g_post_ffn):
    tp = h1.shape[0]
    tr = _row_tile(tp)
    qb, nt = tr // BLOCK, tp // tr
    sr = tr // N_CHIPS

    def body(*refs):
        u_ref, w1_ref, w2_ref, h1_ref = refs[:4]
        t_pieces = refs[4:4 + qb]
        g_ref, r1_ref, dy_ref, df2_ref, loss_ref, dg_ref, acc = refs[4 + qb:]
        i, c = pl.program_id(0), pl.program_id(1)
        cur = i % 2

        @pl.when((i == 0) & (c == 0))
        def _():
            loss_ref[...] = jnp.zeros_like(loss_ref)
            dg_ref[...] = jnp.zeros_like(dg_ref)
            acc[1] = jnp.zeros((tr, D_MODEL), F32)

        def matmuls():
            r = jnp.maximum(_dot(u_ref[...], w1_ref[c]), 0.0)
            r1_ref[...] = r.astype(BF16)
            return _dot((r * r).astype(BF16), w2_ref[c])

        def finish_previous_tile(k, valid):
            lo, hi = k * sr, (k + 1) * sr
            g = g_ref[...]
            fhat, rs = _rms(acc[1 - cur, lo:hi])
            h2 = h1_ref[...] + fhat * g
            rows = (i - 1) * tr + lo + lax.broadcasted_iota(jnp.int32, h2.shape, 0)
            tgt = jnp.concatenate([p[max(lo - s * BLOCK, 0):min(hi - s * BLOCK, BLOCK)] for s, p in enumerate(t_pieces)
                                   if lo < (s + 1) * BLOCK and hi > s * BLOCK], axis=0)
            err = jnp.where((rows >= BLOCK) & valid, h2 - tgt, 0.0)
            dy = err * (1.0 / D_MODEL)
            dy_ref[...] = dy
            loss_ref[...] += (0.5 / D_MODEL) * jnp.sum(err * err)
            df2, dg = _rms_bwd(fhat, rs, g, dy)
            df2_ref[...] = df2.astype(BF16)
            dg_ref[...] += dg

        for k in range(N_CHIPS):
            @pl.when((c == k) & (i < nt))
            def _(k=k):
                finish_previous_tile(k, i >= 1)
                if k == 0:
                    acc[cur] = matmuls()
                else:
                    acc[cur] += matmuls()

            @pl.when((c == k) & (i == nt))
            def _(k=k):
                finish_previous_tile(k, True)

    last = nt - 1
    this_row = pl.BlockSpec((tr, D_MODEL), lambda i, c: (jnp.minimum(i, last), 0))
    prev_quarter = pl.BlockSpec((sr, D_MODEL), lambda i, c: (jnp.maximum(i - 1, 0) * N_CHIPS + c, 0))
    prev_quarter_out = pl.BlockSpec(
        (sr, D_MODEL), lambda i, c: (jnp.where(i == 0, nt * N_CHIPS, (i - 1) * N_CHIPS + c), 0))
    full = lambda a: pl.BlockSpec(a.shape, lambda i, c: (0,) * a.ndim)
    return pl.pallas_call(
        body, name="ffn_fwd", grid=(nt + 1, N_CHIPS),
        in_specs=[this_row, _resident(w1), _resident(w2), prev_quarter] + _seq_specs(tr, delay=1) + [full(g_post_ffn)],
        out_specs=[pl.BlockSpec((tr, FF_CHUNK), lambda i, c: (jnp.minimum(i, last), jnp.where(i < nt, c, N_CHIPS - 1))),
                   prev_quarter_out, prev_quarter_out,
                   pl.BlockSpec((1, 1), lambda i, c: (0, 0)), pl.BlockSpec((1, D_MODEL), lambda i, c: (0, 0))],
        out_shape=[jax.ShapeDtypeStruct((tp, D_FF), BF16), jax.ShapeDtypeStruct((tp + sr, D_MODEL), F32),
                   jax.ShapeDtypeStruct((tp + sr, D_MODEL), BF16), jax.ShapeDtypeStruct((1, 1), F32),
                   jax.ShapeDtypeStruct((1, D_MODEL), F32)],
        scratch_shapes=[pltpu.VMEM((2, tr, D_MODEL), F32)],
        compiler_params=_params("arbitrary", "arbitrary"),
    )(u1, w1, w2, h1, *([tgt] * qb), g_post_ffn)


def _ffn_bwd_data(df2, r1, w1, w2, dy, h1, mix, g_pre_ffn, g_post_mix):
    tp = h1.shape[0]
    tr = _row_tile(tp)
    nt = tp // tr
    sr = tr // N_CHIPS

    def body(df2_ref, r1_ref, w1_ref, w2_ref, dy_ref, h1_ref, mix_ref, gf_ref, gm_ref,
             da_ref, dh1_ref, dmix_ref, dgf_ref, dgm_ref, acc):
        i, c = pl.program_id(0), pl.program_id(1)
        cur = i % 2

        @pl.when((i == 0) & (c == 0))
        def _():
            dgf_ref[...] = jnp.zeros_like(dgf_ref)
            dgm_ref[...] = jnp.zeros_like(dgm_ref)
            acc[1] = jnp.zeros((tr, D_MODEL), F32)

        def matmuls():
            df = _dot_nt(df2_ref[...], w2_ref[c])
            da = (df * (2.0 * r1_ref[...].astype(F32))).astype(BF16)
            da_ref[...] = da
            return _dot_nt(da, w1_ref[c])

        def finish_previous_tile(k, valid):
            lo, hi = k * sr, (k + 1) * sr
            hhat, rs = _rms(h1_ref[...])
            dx, dgf = _rms_bwd(hhat, rs, gf_ref[...], acc[1 - cur, lo:hi])
            dh1 = dy_ref[...] + dx
            dh1_ref[...] = dh1
            mhat, rsm = _rms(mix_ref[...])
            dmix, dgm = _rms_bwd(mhat, rsm, gm_ref[...], dh1)
            dmix_ref[...] = dmix.astype(BF16)
            dgf_ref[...] += jnp.where(valid, dgf, 0.0)
            dgm_ref[...] += jnp.where(valid, dgm, 0.0)

        for k in range(N_CHIPS):
            @pl.when((c == k) & (i < nt))
            def _(k=k):
                finish_previous_tile(k, i >= 1)
                if k == 0:
                    acc[cur] = matmuls()
                else:
                    acc[cur] += matmuls()

            @pl.when((c == k) & (i == nt))
            def _(k=k):
                finish_previous_tile(k, True)

    last = nt - 1
    this_row = pl.BlockSpec((tr, D_MODEL), lambda i, c: (jnp.minimum(i, last), 0))
    prev_quarter = pl.BlockSpec((sr, D_MODEL), lambda i, c: (jnp.maximum(i - 1, 0) * N_CHIPS + c, 0))
    prev_quarter_out = pl.BlockSpec(
        (sr, D_MODEL), lambda i, c: (jnp.where(i == 0, nt * N_CHIPS, (i - 1) * N_CHIPS + c), 0))
    chunk = pl.BlockSpec((tr, FF_CHUNK), lambda i, c: (jnp.minimum(i, last), jnp.where(i < nt, c, N_CHIPS - 1)))
    gain = pl.BlockSpec((1, D_MODEL), lambda i, c: (0, 0))
    return pl.pallas_call(
        body, name="ffn_bwd_data", grid=(nt + 1, N_CHIPS),
        in_specs=[this_row, chunk, _resident(w1), _resident(w2), prev_quarter, prev_quarter, prev_quarter, gain, gain],
        out_specs=[chunk, prev_quarter_out, prev_quarter_out, gain, gain],
        out_shape=[jax.ShapeDtypeStruct((tp, D_FF), BF16), jax.ShapeDtypeStruct((tp + sr, D_MODEL), F32),
                   jax.ShapeDtypeStruct((tp + sr, D_MODEL), BF16), jax.ShapeDtypeStruct((1, D_MODEL), F32),
                   jax.ShapeDtypeStruct((1, D_MODEL), F32)],
        scratch_shapes=[pltpu.VMEM((2, tr, D_MODEL), F32)],
        compiler_params=_params("arbitrary", "arbitrary"),
    )(df2, r1, w1, w2, dy, h1, mix, g_pre_ffn, g_post_mix)


def _ffn_bwd_weights(u1, da1, r1, df2):
    tp = u1.shape[0]
    tr = _wgrad_row_tile(tp)

    def body(u_ref, da_ref, r1_ref, df2_ref, dw1_ref, dw2_ref):
        i = pl.program_id(1)
        r = r1_ref[...].astype(F32)
        p1 = _dot_tn(u_ref[...], da_ref[...])
        p2 = _dot_tn((r * r).astype(BF16), df2_ref[...])

        @pl.when(i == 0)
        def _():
            dw1_ref[0] = p1
            dw2_ref[0] = p2

        @pl.when(i > 0)
        def _():
            dw1_ref[0] += p1
            dw2_ref[0] += p2

    row = pl.BlockSpec((tr, D_MODEL), lambda c, i: (i, 0))
    chunk = pl.BlockSpec((tr, FF_CHUNK), lambda c, i: (i, c))
    return pl.pallas_call(
        body, name="ffn_bwd_weights", grid=(N_CHIPS, tp // tr),
        in_specs=[row, chunk, chunk, row],
        out_specs=[pl.BlockSpec((1, D_MODEL, FF_CHUNK), lambda c, i: (c, 0, 0)),
                   pl.BlockSpec((1, FF_CHUNK, D_MODEL), lambda c, i: (c, 0, 0))],
        out_shape=[jax.ShapeDtypeStruct((N_CHIPS, D_MODEL, FF_CHUNK), F32),
                   jax.ShapeDtypeStruct((N_CHIPS, FF_CHUNK, D_MODEL), F32)],
        compiler_params=_params("parallel", "arbitrary"),
    )(u1, da1, r1, df2)


def _outproj_bwd(dmix, w_out, attn, rec, token):
    tp = attn.shape[0]
    tr = _wgrad_row_tile(tp)

    def body(dm_ref, w_ref, a_ref, r_ref, _, da_ref, dr_ref, dw_ref):
        i = pl.program_id(0)
        dm = dm_ref[...]
        dcat = _dot_nt(dm, w_ref[...])
        da_ref[...] = dcat[:, :ATTN_WIDTH].astype(BF16)
        dr_ref[...] = dcat[:, ATTN_WIDTH:]
        pa = _dot_tn(a_ref[...], dm)
        pr = _dot_tn(r_ref[...], dm)

        @pl.when(i == 0)
        def _():
            dw_ref[:ATTN_WIDTH] = pa
            dw_ref[ATTN_WIDTH:] = pr

        @pl.when(i > 0)
        def _():
            dw_ref[:ATTN_WIDTH] += pa
            dw_ref[ATTN_WIDTH:] += pr

    row = lambda w: pl.BlockSpec((tr, w), lambda i: (i, 0))
    full = pl.BlockSpec((D_MODEL, D_MODEL), lambda i: (0, 0))
    return pl.pallas_call(
        body, name="outproj_bwd", grid=(tp // tr,),
        in_specs=[row(D_MODEL), full, row(ATTN_WIDTH), row(LRU_WIDTH), pl.BlockSpec(token.shape, lambda i: (0, 0))],
        out_specs=[row(ATTN_WIDTH), row(LRU_WIDTH), full],
        out_shape=[jax.ShapeDtypeStruct((tp, ATTN_WIDTH), BF16), jax.ShapeDtypeStruct((tp, LRU_WIDTH), F32),
                   jax.ShapeDtypeStruct((D_MODEL, D_MODEL), F32)],
        compiler_params=_params("arbitrary"),
    )(dmix, w_out, attn, rec, token)


N_VEC_ROWS = 8


def _lru_bwd(xr, yr, hr, drec, conv_w, conv_b, wa, ba, wx, bx, lam, token):
    tp = xr.shape[0]
    tr = _row_tile(tp)
    qb, nt = tr // BLOCK, tp // tr

    def body(xr_ref, xh_ref, yr_ref, hr_ref, hp_ref, dr_ref, cw_ref, cb_ref, wa_ref, ba_ref, wx_ref, bx_ref, lam_ref, _,
             dxr_ref, dyr_ref, dwa_ref, dwx_ref, vec_ref, g_next, a_next, dxc_next, dsp):
        s = pl.program_id(0)
        t = nt - 1 - s

        @pl.when(s == 0)
        def _():
            g_next[...] = jnp.zeros_like(g_next)
            a_next[...] = jnp.zeros_like(a_next)
            dxc_next[...] = jnp.zeros_like(dxc_next)
            dsp[...] = jnp.zeros_like(dsp)
            dwa_ref[...] = jnp.zeros_like(dwa_ref)
            dwx_ref[...] = jnp.zeros_like(dwx_ref)
            vec_ref[...] = jnp.zeros_like(vec_ref)

        first_tile = t == 0
        cw, cb = cw_ref[...], cb_ref[...]
        lam_v = lam_ref[...]
        sp = _softplus(-lam_v)
        wa_m, ba_v, wx_m, bx_v = wa_ref[...], ba_ref[...], wx_ref[...], bx_ref[...]
        rows = lax.broadcasted_iota(jnp.int32, (BLOCK, LRU_WIDTH), 0)
        col = lambda v: jnp.sum(v, axis=0, keepdims=True)

        g_after, a_after, dxc_after = g_next[0:1], a_next[0:1], dxc_next[...]
        xbs, dgrs, dgis = [], [], []
        vec = [jnp.zeros((1, LRU_WIDTH), F32) for _ in range(N_VEC_ROWS)]
        for i in reversed(range(qb)):
            blk = slice(i * BLOCK, (i + 1) * BLOCK)
            if i == 0:
                x_before = jnp.where(first_tile, 0.0, xh_ref[...])
                h_before = jnp.where(first_tile, 0.0, hp_ref[7:8])
            else:
                x_before = xr_ref[i * BLOCK - 8:i * BLOCK]
                h_before = hr_ref[i * BLOCK - 1:i * BLOCK]
            taps = _conv_taps(xr_ref[blk], x_before)
            xc = cb + sum(cw[k:k + 1] * taps[k] for k in range(4))
            xb, r, ig, a, mult = _lru_gates(xc, wa_m, ba_v, wx_m, bx_v, sp)

            yr_v = yr_ref[blk]
            gl, th = _gelu(yr_v)
            h = hr_ref[blk]
            drec = dr_ref[blk]
            dyr_ref[blk] = (drec * h * _gelu_grad(yr_v, th)).astype(BF16)

            a_up = jnp.where(rows == BLOCK - 1, a_after, pltpu.roll(a, BLOCK - 1, 0))
            g = _scan_rev(a_up, drec * gl, g_after)
            g_after, a_after = g[0:1], a[0:1]

            h_prev = jnp.where(rows == 0, h_before, pltpu.roll(h, 1, 0))
            du, da = g, g * h_prev
            if i == 0:
                real = (t * tr + rows) >= PAD_ROWS
                du, da = jnp.where(real, du, 0.0), jnp.where(real, da, 0.0)
            dmult = du * (ig * xc)
            dig = du * (mult * xc)
            dxc = du * (mult * ig)
            dlog_a = da * a - dmult * (a * a / mult)
            if i == 0:
                dlog_a = jnp.where(real, dlog_a, 0.0)
            dgr = (dlog_a * (-LRU_C * sp)) * (r * (1.0 - r))
            dgi = dig * (ig * (1.0 - ig))
            dgr_b, dgi_b = dgr.astype(BF16), dgi.astype(BF16)
            dxc = dxc + _dot_nt(dgr_b, wa_m) + _dot_nt(dgi_b, wx_m)
            xbs.append(xb)
            dgrs.append(dgr_b)
            dgis.append(dgi_b)

            ext = jnp.concatenate([dxc, dxc_after], axis=0)
            up = [ext[:BLOCK] if j == 0 else pltpu.roll(ext, BLOCK + 8 - j, 0)[:BLOCK] for j in range(4)]
            dxr_ref[blk] = sum(cw[k:k + 1] * up[3 - k] for k in range(4)).astype(BF16)
            dxc_after = dxc[:8]

            for k in range(4):
                vec[k] = vec[k] + col(dxc * taps[k])
            vec[4] = vec[4] + col(dxc)
            vec[5] = vec[5] + col(dgr)
            vec[6] = vec[6] + col(dgi)
            vec[7] = vec[7] + col(dlog_a * (-LRU_C * r))

        g_next[0:1], a_next[0:1], dxc_next[...] = g_after, a_after, dxc_after
        xb_all = jnp.concatenate(xbs, axis=0)
        dwa_ref[...] += _dot_tn(xb_all, jnp.concatenate(dgrs, axis=0))
        dwx_ref[...] += _dot_tn(xb_all, jnp.concatenate(dgis, axis=0))
        for k in range(7):
            vec_ref[k:k + 1] += vec[k]
        dsp[0:1] += vec[7]

        @pl.when(s == nt - 1)
        def _():
            vec_ref[7:8] = dsp[0:1] * (-_sigmoid(-lam_v))

    blk_spec = pl.BlockSpec((tr, LRU_WIDTH), lambda s: (nt - 1 - s, 0))
    rows_before = pl.BlockSpec((8, LRU_WIDTH), lambda s: (jnp.maximum((nt - 1 - s) * (tr // 8) - 1, 0), 0))
    full = lambda a: pl.BlockSpec(a.shape, lambda s: (0,) * a.ndim)
    small = [conv_w, conv_b, wa, ba, wx, bx, lam, token]
    sq = pl.BlockSpec((LRU_WIDTH, LRU_WIDTH), lambda s: (0, 0))
    return pl.pallas_call(
        body, name="lru_bwd", grid=(nt,),
        in_specs=[blk_spec, rows_before, blk_spec, blk_spec, rows_before, blk_spec] + [full(a) for a in small],
        out_specs=[blk_spec, blk_spec, sq, sq, pl.BlockSpec((N_VEC_ROWS, LRU_WIDTH), lambda s: (0, 0))],
        out_shape=[jax.ShapeDtypeStruct((tp, LRU_WIDTH), BF16), jax.ShapeDtypeStruct((tp, LRU_WIDTH), BF16),
                   jax.ShapeDtypeStruct((LRU_WIDTH, LRU_WIDTH), F32), jax.ShapeDtypeStruct((LRU_WIDTH, LRU_WIDTH), F32),
                   jax.ShapeDtypeStruct((N_VEC_ROWS, LRU_WIDTH), F32)],
        scratch_shapes=[pltpu.VMEM((8, LRU_WIDTH), F32)] * 4,
        compiler_params=_params("arbitrary"),
    )(xr, xr, yr, hr, hr, drec, *small)


def _attn_bwd(qkv, dattn, probs, sink_probs):
    tp = qkv.shape[0]
    tr = _row_tile(tp)
    qb, nt = tr // BLOCK, tp // tr
    n_groups = N_KV

    def body(p_ref, ps_ref, q_ref, kp_ref, kc_ref, vp_ref, vc_ref, do_ref, dq_ref, dkv_ref, ex_ref, ds_ref, dsink):
        t = pl.program_id(0)

        @pl.when(t == 0)
        def _():
            dsink[...] = jnp.zeros_like(dsink)

        k_all = jnp.concatenate([kp_ref[...], kc_ref[...]], axis=0)
        v_all = jnp.concatenate([vp_ref[...], vc_ref[...]], axis=0)
        tail = None
        for i in range(qb):
            rows = slice(i * BLOCK, (i + 1) * BLOCK)
            qt = (q_ref[rows].astype(F32) * _QSCALE).T
            dot = do_ref[rows].astype(F32).T
            k2, v2 = k_all[i * BLOCK:(i + 2) * BLOCK], v_all[i * BLOCK:(i + 2) * BLOCK]
            dqs, dks, dvs = [], [], []
            for g in range(n_groups):
                cols = slice(g * HEAD_DIM, (g + 1) * HEAD_DIM)
                k_g, v_g = k2[:, cols], v2[:, cols]
                qgt, dogt = _heads_t(qt, g), _heads_t(dot, g)
                pb = p_ref[i, g]
                p = pb.astype(F32)
                dpt = _dot(v_g, dogt)
                delta = jnp.sum(p * dpt, axis=0, keepdims=True)
                dst = (p * (dpt - delta)).astype(BF16)
                dqs.append(_dot_tn(k_g, dst) * _QSCALE)
                dks.append(_dot_nt(qgt, dst))
                dvs.append(_dot_nt(dogt, pb))
                dsink[g:g + 1] -= ps_ref[i, g:g + 1] * delta
            dq_ref[rows] = _from_heads_t(dqs).astype(BF16)
            dkv = jnp.concatenate([jnp.concatenate(dks, axis=0).T, jnp.concatenate(dvs, axis=0).T], axis=1)
            if i == 0:
                ex_ref[0] = dkv[:BLOCK]
            else:
                dkv_ref[(i - 1) * BLOCK:i * BLOCK] = (tail + dkv[:BLOCK]).astype(BF16)
            tail = dkv[BLOCK:]
        dkv_ref[(qb - 1) * BLOCK:] = tail.astype(BF16)

        @pl.when(t == nt - 1)
        def _():
            lane = lax.broadcasted_iota(jnp.int32, (1, ATTN_HEADS), 1)
            acc = jnp.zeros((1, ATTN_HEADS), F32)
            for h in range(ATTN_HEADS):
                g, hh = divmod(h, GQA_GROUP)
                acc = acc + jnp.where(lane == h, jnp.sum(dsink[g:g + 1, hh * BLOCK:(hh + 1) * BLOCK]), 0.0)
            ds_ref[...] = acc

    cur = lambda w: pl.BlockSpec((tr, w), lambda t: (t, 0))
    return pl.pallas_call(
        body, name="attn_bwd", grid=(nt,),
        in_specs=_prob_specs(qb) + [cur(ATTN_WIDTH)] + _kv_specs(tr) + [cur(ATTN_WIDTH)],
        out_specs=[cur(ATTN_WIDTH), cur(2 * KV_WIDTH), pl.BlockSpec((1, BLOCK, 2 * KV_WIDTH), lambda t: (t, 0, 0)),
                   pl.BlockSpec((1, ATTN_HEADS), lambda t: (0, 0))],
        out_shape=[jax.ShapeDtypeStruct((tp, ATTN_WIDTH), BF16), jax.ShapeDtypeStruct((tp, 2 * KV_WIDTH), BF16),
                   jax.ShapeDtypeStruct((nt, BLOCK, 2 * KV_WIDTH), F32), jax.ShapeDtypeStruct((1, ATTN_HEADS), F32)],
        scratch_shapes=[pltpu.VMEM((n_groups, GROUP_ROWS), F32)],
        compiler_params=_params("arbitrary"),
    )(probs, sink_probs, qkv, qkv, qkv, qkv, qkv, dattn)


def _fix_dkv(dkv, dkv_extra):
    tp = dkv.shape[0]
    tr = _row_tile(tp)
    nt, qb = tp // tr, tr // BLOCK
    if nt == 1:
        return dkv

    def body(d_ref, ex_ref, o_ref):
        o_ref[...] = (d_ref[...].astype(F32) + ex_ref[0]).astype(BF16)

    last = pl.BlockSpec((BLOCK, 2 * KV_WIDTH), lambda t: (t * qb + qb - 1, 0))
    return pl.pallas_call(
        body, name="fix_dkv", grid=(nt - 1,),
        in_specs=[last, pl.BlockSpec((1, BLOCK, 2 * KV_WIDTH), lambda t: (t + 1, 0, 0))],
        out_specs=last, out_shape=jax.ShapeDtypeStruct(dkv.shape, dkv.dtype),
        input_output_aliases={0: 0}, compiler_params=_params("parallel"),
    )(dkv, dkv_extra)


def _inproj_wgrad(dq, dkv, dxr, dyr, u0):
    tp = dq.shape[0]
    tr = _wgrad_row_tile(tp)

    def body(dq_ref, dkv_ref, dxr_ref, dyr_ref, u_ref, dw_ref):
        i = pl.program_id(0)
        dz = jnp.concatenate([dq_ref[...], dkv_ref[...], dxr_ref[...], dyr_ref[...]], axis=1)
        pw = _dot_tn(dz, u_ref[...])

        @pl.when(i == 0)
        def _():
            dw_ref[...] = pw

        @pl.when(i > 0)
        def _():
            dw_ref[...] += pw

    row = lambda w: pl.BlockSpec((tr, w), lambda i: (i, 0))
    return pl.pallas_call(
        body, name="inproj_wgrad", grid=(tp // tr,),
        in_specs=[row(ATTN_WIDTH), row(2 * KV_WIDTH), row(LRU_WIDTH), row(LRU_WIDTH), row(D_MODEL)],
        out_specs=pl.BlockSpec((IN_WIDTH, D_MODEL), lambda i: (0, 0)),
        out_shape=jax.ShapeDtypeStruct((IN_WIDTH, D_MODEL), F32),
        compiler_params=_params("arbitrary"),
    )(dq, dkv, dxr, dyr, u0)


def _inproj_dgrad(dq, dkv, dxr, dyr, w_in, head, x, dh1, g, token):
    tp = dq.shape[0]
    tr = _row_tile(tp)
    nt, qb = tp // tr, tr // BLOCK

    def body(*refs):
        dq_ref, dkv_ref, dxr_ref, dyr_ref, w_ref, head_ref = refs[:6]
        pieces = refs[6:6 + qb]
        dh1_ref, g_ref, _, gx_ref, dhead_ref, dg_ref, buf, sems = refs[6 + qb:]
        i = pl.program_id(0)
        slot = i % 2

        def out_copy(step, at):
            return pltpu.make_async_copy(buf.at[at], gx_ref.at[pl.ds(step * tr - BLOCK, tr)], sems.at[at])

        dz = jnp.concatenate([dq_ref[...], dkv_ref[...], dxr_ref[...], dyr_ref[...]], axis=1)
        du = _dot(dz, w_ref[...])
        hhat, rs = _rms(_seq_tile(head_ref[...], pieces, i))
        dx, dg = _rms_bwd(hhat, rs, g_ref[...], du)
        dh0 = dh1_ref[...] + dx

        @pl.when(i >= 3)
        def _():
            out_copy(i - 2, slot).wait()

        buf[slot] = dh0

        @pl.when(i == 0)
        def _():
            dg_ref[...] = dg
            dhead_ref[...] = dh0[:BLOCK]
            if tr > BLOCK:
                first = pltpu.make_async_copy(buf.at[0, pl.ds(BLOCK, tr - BLOCK)], gx_ref.at[pl.ds(0, tr - BLOCK)],
                                              sems.at[0])
                first.start()
                first.wait()

        @pl.when(i >= 1)
        def _():
            dg_ref[...] += dg
            out_copy(i, slot).start()

        @pl.when(i == nt - 1)
        def _():
            if nt >= 3:
                out_copy(nt - 2, (nt - 2) % 2).wait()
            if nt >= 2:
                out_copy(nt - 1, (nt - 1) % 2).wait()

    row = lambda w: pl.BlockSpec((tr, w), lambda i: (i, 0))
    full = lambda shape: pl.BlockSpec(shape, lambda i: (0,) * len(shape))
    return pl.pallas_call(
        body, name="inproj_dgrad", grid=(tp // tr,),
        in_specs=[row(ATTN_WIDTH), row(2 * KV_WIDTH), row(LRU_WIDTH), row(LRU_WIDTH), full(w_in.shape),
                  full(head.shape)] + _seq_specs(tr) + [row(D_MODEL), full(g.shape), full(token.shape)],
        out_specs=[pl.BlockSpec(memory_space=pl.ANY), full((BLOCK, D_MODEL)), full((1, D_MODEL))],
        out_shape=[jax.ShapeDtypeStruct(x.shape, F32), jax.ShapeDtypeStruct((BLOCK, D_MODEL), F32),
                   jax.ShapeDtypeStruct((1, D_MODEL), F32)],
        scratch_shapes=[pltpu.VMEM((2, tr, D_MODEL), F32), pltpu.SemaphoreType.DMA((2,))],
        compiler_params=_params("arbitrary"),
    )(dq, dkv, dxr, dyr, w_in, head, *([x] * qb), dh1, g, token)


def _dense_block_diag(w):
    eye = jnp.eye(LRU_BLOCKS, dtype=w.dtype)
    return (w[:, :, None, :] * eye[:, None, :, None]).reshape(LRU_WIDTH, LRU_WIDTH)


def _diag_blocks(dense):
    d4 = dense.reshape(LRU_BLOCKS, LRU_BLOCK, LRU_BLOCKS, LRU_BLOCK)
    return jnp.stack([d4[n, :, n, :] for n in range(LRU_BLOCKS)])


def _local_step(head, x, tgt, g_pre_mix, w_in, conv_w, conv_b, w_a, b_a, w_x, b_x, lam, sinks, g_post_mix,
                g_pre_ffn, g_post_ffn, late_weights, on_ffn_grads, on_outproj_bwd, on_mixer_grads, token):
    wa = _dense_block_diag(w_a).astype(BF16)
    wx = _dense_block_diag(w_x).astype(BF16)

    u0, qkv, xr, yr, hr, rec = _inproj_lru_fwd(head, x, g_pre_mix, w_in, conv_w, conv_b, wa, b_a, wx, b_x, lam, token)
    attn, probs, sink_probs = _attn_fwd(qkv, sinks)
    w_out, w1, w2 = late_weights([attn, rec])
    mix, h1, u1 = _outproj_fwd(attn, rec, w_out, head, x, g_post_mix, g_pre_ffn)
    r1, dy, df2, loss, dg_post_ffn = _ffn_fwd(u1, w1, w2, h1, tgt, g_post_ffn)

    da1, dh1, dmix, dg_pre_ffn, dg_post_mix = _ffn_bwd_data(df2, r1, w1, w2, dy, h1, mix, g_pre_ffn, g_post_mix)
    dw1, dw2 = _ffn_bwd_weights(u1, da1, r1, df2)
    token2 = on_ffn_grads(dw1, dw2)
    dattn, drec, dw_out = _outproj_bwd(dmix, w_out, attn, rec, token2)
    token3 = on_outproj_bwd(dattn)
    dxr, dyr, dwa, dwx, vec = _lru_bwd(xr, yr, hr, drec, conv_w, conv_b, wa, b_a, wx, b_x, lam, token3)
    dq, dkv, dkv_extra, dsinks = _attn_bwd(qkv, dattn, probs, sink_probs)
    dkv = _fix_dkv(dkv, dkv_extra)
    dw_in = _inproj_wgrad(dq, dkv, dxr, dyr, u0)
    token4 = on_mixer_grads(dw_in, dw_out)
    dx, dhead, dg_pre_mix = _inproj_dgrad(dq, dkv, dxr, dyr, w_in, head, x, dh1, g_pre_mix, token4)

    grads = dict(
        g_pre_mix=dg_pre_mix, conv_w=vec[0:4], conv_b=vec[4:5], w_a=_diag_blocks(dwa), b_a=vec[5:6],
        w_x=_diag_blocks(dwx), b_x=vec[6:7], lru_lambda=vec[7:8], attn_sinks=dsinks,
        g_post_mix=dg_post_mix, g_pre_ffn=dg_pre_ffn, g_post_ffn=dg_post_ffn)
    return loss, dx, dhead, grads


HBM = pl.BlockSpec(memory_space=pltpu.HBM)


def _mesh_pos():
    return lax.axis_index("x"), lax.axis_index("y"), lax.axis_index("c")


def _other_chips(x, y):
    return [(1 - x, y), (x, 1 - y), (1 - x, 1 - y)]


def _remote(src, dst, send_sem, recv_sem, to):
    return pltpu.make_async_remote_copy(src_ref=src, dst_ref=dst, send_sem=send_sem, recv_sem=recv_sem,
                                        device_id=to, device_id_type=MESH)


def _gather_weights(shards, lands, tiny, tiny_land):
    nbig = len(shards)

    def body(*refs):
        srcs, tiny_src = refs[:nbig], refs[nbig]
        outs, tiny_out = refs[2 * nbig + 2:3 * nbig + 2], refs[3 * nbig + 2]
        ici_send, ici_recv, d2d_send, d2d_recv, tiny_send, tiny_recv = refs[3 * nbig + 3:]
        x, y, c = _mesh_pos()
        me = 2 * x + y
        chips = _other_chips(x, y)
        sibling = (x, y, 1 - c)
        sends = []
        for w, (src, out) in enumerate(zip(srcs, outs)):
            hr = src.shape[0] // 2
            for j, chip in enumerate(chips):
                k = 3 * w + j
                cp = _remote(src.at[pl.ds(c * hr, hr)], out.at[me, pl.ds(c * hr, hr)],
                             ici_send.at[k], ici_recv.at[k], (*chip, c))
                cp.start()
                sends.append(cp)
        for j, chip in enumerate(chips):
            cp = _remote(tiny_src, tiny_out.at[me], tiny_send.at[j], tiny_recv.at[j], (*chip, c))
            cp.start()
            sends.append(cp)
        for w, (src, out) in enumerate(zip(srcs, outs)):
            hr = src.shape[0] // 2
            for j, (px, py) in enumerate(chips):
                k = 3 * w + j
                landed = out.at[2 * px + py, pl.ds(c * hr, hr)]
                _remote(landed, landed, ici_send.at[k], ici_recv.at[k], sibling).wait_recv()
                cp = _remote(landed, landed, d2d_send.at[k], d2d_recv.at[k], sibling)
                cp.start()
                sends.append(cp)
        for w, (src, out) in enumerate(zip(srcs, outs)):
            hr = src.shape[0] // 2
            for j, (px, py) in enumerate(chips):
                k = 3 * w + j
                other = out.at[2 * px + py, pl.ds((1 - c) * hr, hr)]
                _remote(other, other, d2d_send.at[k], d2d_recv.at[k], sibling).wait_recv()
        for j, (px, py) in enumerate(chips):
            blk = tiny_out.at[2 * px + py]
            _remote(blk, blk, tiny_send.at[j], tiny_recv.at[j], sibling).wait_recv()
        for cp in sends:
            cp.wait_send()

    out_shape = [jax.ShapeDtypeStruct(l.shape, l.dtype) for l in list(lands) + [tiny_land]]
    n = 3 * nbig
    return pl.pallas_call(
        body, name="gather_weights", out_shape=out_shape,
        in_specs=[HBM] * (2 * nbig + 2), out_specs=[HBM] * (nbig + 1),
        input_output_aliases={nbig + 1 + i: i for i in range(nbig + 1)},
        scratch_shapes=[pltpu.SemaphoreType.DMA((n,)),
                        pltpu.SemaphoreType.DMA((n,)), pltpu.SemaphoreType.DMA((n,)), pltpu.SemaphoreType.DMA((n,)),
                        pltpu.SemaphoreType.DMA((3,)), pltpu.SemaphoreType.DMA((3,))],
    )(*shards, tiny, *lands, tiny_land)


def _prep_shard(w, me):
    rows, cols = w.shape
    tr = 256 if rows % 256 == 0 else rows

    def body(me_ref, w_ref, s_ref, l_ref):
        b = w_ref[...].astype(BF16)
        s_ref[...] = b
        l_ref[0] = b

    return pl.pallas_call(
        body, name="prep_shard",
        grid_spec=pltpu.PrefetchScalarGridSpec(
            num_scalar_prefetch=1, grid=(rows // tr,),
            in_specs=[pl.BlockSpec((tr, cols), lambda i, me_ref: (i, 0))],
            out_specs=[pl.BlockSpec((tr, cols), lambda i, me_ref: (i, 0)),
                       pl.BlockSpec((1, tr, cols), lambda i, me_ref: (me_ref[0], i, 0))]),
        out_shape=[jax.ShapeDtypeStruct((rows, cols), BF16), jax.ShapeDtypeStruct((N_CHIPS, rows, cols), BF16)],
        compiler_params=_params("parallel"),
    )(me, w)


def _prep_tiny(tiny, me, slots=N_CHIPS):
    def body(me_ref, t_ref, l_ref):
        l_ref[0] = t_ref[...]

    return pl.pallas_call(
        body, name="prep_tiny",
        grid_spec=pltpu.PrefetchScalarGridSpec(
            num_scalar_prefetch=1, grid=(1,),
            in_specs=[pl.BlockSpec(tiny.shape, lambda i, me_ref: (0, 0))],
            out_specs=pl.BlockSpec((1,) + tiny.shape, lambda i, me_ref: (me_ref[0], 0, 0))),
        out_shape=jax.ShapeDtypeStruct((slots,) + tiny.shape, tiny.dtype),
    )(me, tiny)


N_DEV = 8


def _sibling_exchange(parts, token):
    def body(*refs):
        n = len(parts)
        srcs, outs, send_sems, recv_sems = refs[:n], refs[n + 1:2 * n + 1], refs[2 * n + 1], refs[2 * n + 2]
        x, y, c = _mesh_pos()
        sibling = (x, y, 1 - c)
        cps = []
        for w, (src, out) in enumerate(zip(srcs, outs)):
            hr = src.shape[1] // 2
            cp = _remote(src.at[:, pl.ds((1 - c) * hr, hr)], out, send_sems.at[w], recv_sems.at[w], sibling)
            cp.start()
            cps.append(cp)
        for cp in cps:
            cp.wait()

    n = len(parts)
    return pl.pallas_call(
        body, name="sibling_exchange",
        out_shape=[jax.ShapeDtypeStruct((p.shape[0], p.shape[1] // 2, p.shape[2]), p.dtype) for p in parts],
        in_specs=[HBM] * n + [pl.BlockSpec(memory_space=pl.ANY)], out_specs=[HBM] * n,
        scratch_shapes=[pltpu.SemaphoreType.DMA((n,)), pltpu.SemaphoreType.DMA((n,))],
    )(*parts, token)


def _chip_presum(part, from_sibling, pos):
    _, hr, cols = from_sibling.shape
    tr = 256 if hr % 256 == 0 else hr
    steps = hr // tr

    def body(pos_ref, a_ref, b_ref, o_ref, land_ref):
        s = (a_ref[...] + b_ref[...]).astype(BF16)
        o_ref[...] = s

        @pl.when(pl.program_id(1) == pos_ref[1])
        def _():
            land_ref[...] = s

    return pl.pallas_call(
        body, name="chip_presum",
        grid_spec=pltpu.PrefetchScalarGridSpec(
            num_scalar_prefetch=1, grid=(steps, N_CHIPS),
            in_specs=[pl.BlockSpec((1, tr, cols), lambda i, j, p: (j, p[0] * steps + i, 0)),
                      pl.BlockSpec((1, tr, cols), lambda i, j, p: (j, i, 0))],
            out_specs=[pl.BlockSpec((1, tr, cols), lambda i, j, p: (j, i, 0)),
                       pl.BlockSpec((1, tr, cols), lambda i, j, p: (p[1], p[0] * steps + i, 0))]),
        out_shape=[jax.ShapeDtypeStruct(from_sibling.shape, BF16),
                   jax.ShapeDtypeStruct((N_CHIPS, 2 * hr, cols), BF16)],
        compiler_params=_params("arbitrary", "arbitrary"),
    )(pos, part, from_sibling)


def _scatter_partials(cparts, lands, done_cparts=(), done_lands=()):
    n_new = len(cparts)
    nw = n_new + len(done_cparts)

    def body(*refs):
        srcs = refs[:nw]
        outs = refs[2 * nw:3 * nw]
        own_send, own_recv, ici_send, ici_recv, d2d_send, d2d_recv = refs[3 * nw:]
        x, y, c = _mesh_pos()
        me = 2 * x + y
        chips = _other_chips(x, y)
        sibling = (x, y, 1 - c)
        sends = []
        for w in list(range(n_new, nw)) + list(range(n_new)):
            src, out = srcs[w], outs[w]
            hr = src.shape[1]
            mine = out.at[me, pl.ds(c * hr, hr)]
            cp = _remote(src.at[me], mine, own_send.at[w], own_recv.at[w], sibling)
            cp.start()
            sends.append(cp)
            for j, (px, py) in enumerate(chips):
                if w >= n_new:
                    break
                k = 3 * w + j
                cp = _remote(src.at[2 * px + py], mine, ici_send.at[k], ici_recv.at[k], (px, py, c))
                cp.start()
                sends.append(cp)
        for w in list(range(n_new, nw)) + list(range(n_new)):
            src, out = srcs[w], outs[w]
            hr = src.shape[1]
            for j, (px, py) in enumerate(chips):
                k = 3 * w + j
                landed = out.at[2 * px + py, pl.ds(c * hr, hr)]
                if w < n_new:
                    _remote(landed, landed, ici_send.at[k], ici_recv.at[k], sibling).wait_recv()
                cp = _remote(landed, landed, d2d_send.at[k], d2d_recv.at[k], sibling)
                cp.start()
                sends.append(cp)
        for w, (src, out) in enumerate(zip(srcs, outs)):
            hr = src.shape[1]
            other = out.at[me, pl.ds((1 - c) * hr, hr)]
            _remote(other, other, own_send.at[w], own_recv.at[w], sibling).wait_recv()
            for j, (px, py) in enumerate(chips):
                k = 3 * w + j
                other = out.at[2 * px + py, pl.ds((1 - c) * hr, hr)]
                _remote(other, other, d2d_send.at[k], d2d_recv.at[k], sibling).wait_recv()
        for cp in sends:
            cp.wait_send()

    n = 3 * nw
    dma = pltpu.SemaphoreType.DMA
    every = list(cparts) + list(done_cparts)
    every_lands = list(lands) + list(done_lands)
    return pl.pallas_call(
        body, name="scatter_partials",
        out_shape=[jax.ShapeDtypeStruct(l.shape, l.dtype) for l in every_lands],
        in_specs=[HBM] * (2 * nw), out_specs=[HBM] * nw,
        input_output_aliases={nw + i: i for i in range(nw)},
        scratch_shapes=[dma((nw,)), dma((nw,)), dma((n,)), dma((n,)), dma((n,)), dma((n,))],
    )(*every, *every_lands)


SEM = pl.BlockSpec(memory_space=pltpu.SEMAPHORE)
SPLIT_COPY = pltpu.CompilerParams(has_side_effects=pltpu.SideEffectType.DATAFLOW_SIDE_EFFECTING)


def _hbm(a):
    return pltpu.with_memory_space_constraint(a, pltpu.HBM)


def _gather_copies(srcs, lands, send_sems, recv_sems):
    x, y, c = _mesh_pos()
    me = 2 * x + y
    sends, recvs = [], []
    for w, (src, land) in enumerate(zip(srcs, lands)):
        hr = src.shape[0] // 2
        for j, (px, py) in enumerate(_other_chips(x, y)):
            k = 3 * w + j
            sends.append(_remote(src.at[pl.ds(c * hr, hr)], land.at[me, pl.ds(c * hr, hr)],
                                 send_sems.at[k], recv_sems.at[k], (px, py, c)))
            got = land.at[2 * px + py, pl.ds(c * hr, hr)]
            recvs.append(_remote(got, got, send_sems.at[k], recv_sems.at[k], (px, py, c)))
    return sends, recvs


def _scatter_copies(srcs, lands, send_sems, recv_sems):
    x, y, c = _mesh_pos()
    me = 2 * x + y
    sends, recvs = [], []
    for w, (src, land) in enumerate(zip(srcs, lands)):
        hr = src.shape[1]
        for j, (px, py) in enumerate(_other_chips(x, y)):
            k = 3 * w + j
            sends.append(_remote(src.at[2 * px + py], land.at[me, pl.ds(c * hr, hr)],
                                 send_sems.at[k], recv_sems.at[k], (px, py, c)))
            got = land.at[2 * px + py, pl.ds(c * hr, hr)]
            recvs.append(_remote(got, got, send_sems.at[k], recv_sems.at[k], (px, py, c)))
    return sends, recvs


def _sibling_copies(srcs, lands, send_sems, recv_sems):
    x, y, c = _mesh_pos()
    sibling = (x, y, 1 - c)
    sends, recvs = [], []
    for w, (src, land) in enumerate(zip(srcs, lands)):
        hr = src.shape[1] // 2
        sends.append(_remote(src.at[:, pl.ds((1 - c) * hr, hr)], land, send_sems.at[w], recv_sems.at[w], sibling))
        recvs.append(_remote(land, land, send_sems.at[w], recv_sems.at[w], sibling))
    return sends, recvs


def _all_peers_copies(srcs, lands, send_sems, recv_sems):
    x, y, c = _mesh_pos()
    (src,), (land,) = srcs, lands
    flip = lambda v, bit: 1 - v if bit else v
    sends, recvs = [], []
    for k in range(N_DEV - 1):
        px, py, pc = flip(x, (k + 1) & 4), flip(y, (k + 1) & 2), flip(c, (k + 1) & 1)
        sends.append(_remote(src, land.at[4 * x + 2 * y + c], send_sems.at[k], recv_sems.at[k], (px, py, pc)))
        got = land.at[4 * px + 2 * py + pc]
        recvs.append(_remote(got, got, send_sems.at[k], recv_sems.at[k], (px, py, pc)))
    return sends, recvs


def _split_start(name, copies_of, srcs, land_shapes, n_copies=None):
    n = len(srcs)
    k = 3 * n if n_copies is None else n_copies

    def body(*refs):
        src_refs, land_refs = refs[:n], refs[n:2 * n]
        send_sems, recv_sems = refs[2 * n], refs[2 * n + 1]
        token = refs[-1]
        sends, _ = copies_of(src_refs, land_refs, send_sems, recv_sems)
        for cp in sends:
            cp.start()
        token[...] = jnp.zeros_like(token)

    lands = [_hbm(s) for s in land_shapes]
    dma = pltpu.SemaphoreType.DMA
    res = pl.pallas_call(
        body, name=name,
        out_shape=(dma((k,)), dma((k,)), *[pltpu.HBM(s.shape, s.dtype) for s in srcs],
                   *[pltpu.HBM(s.shape, s.dtype) for s in land_shapes], jax.ShapeDtypeStruct((8, 128), F32)),
        in_specs=[HBM] * (2 * n),
        out_specs=(SEM, SEM, *([HBM] * (2 * n)), pl.BlockSpec(memory_space=pltpu.VMEM)),
        input_output_aliases={i: 2 + i for i in range(2 * n)},
        compiler_params=SPLIT_COPY,
    )(*[_hbm(s) for s in srcs], *lands)
    return res[0], res[1], list(res[2:2 + n]), list(res[2 + n:2 + 2 * n]), res[-1]


def _split_wait(name, copies_of, send_sems, recv_sems, srcs, lands, after):
    n = len(srcs)

    def body(*refs):
        src_refs, land_refs = refs[:n], refs[n:2 * n]
        sends, recvs = copies_of(src_refs, land_refs, refs[2 * n], refs[2 * n + 1])
        for cp in sends:
            cp.wait_send()
        for cp in recvs:
            cp.wait_recv()

    res = pl.pallas_call(
        body, name=name,
        out_shape=tuple(pltpu.HBM(s.shape, s.dtype) for s in list(srcs) + list(lands)),
        in_specs=[HBM] * (2 * n) + [SEM, SEM] + [pl.BlockSpec(memory_space=pl.ANY)] * len(after),
        out_specs=tuple([HBM] * (2 * n)),
        input_output_aliases={i: i for i in range(2 * n)},
        compiler_params=SPLIT_COPY,
    )(*srcs, *lands, send_sems, recv_sems, *after)
    return list(res[:n]), list(res[n:])


def _gather_finish(lands):
    n = len(lands)

    def body(*refs):
        outs = refs[n:2 * n]
        d2d_send, d2d_recv = refs[2 * n:]
        x, y, c = _mesh_pos()
        chips = _other_chips(x, y)
        sibling = (x, y, 1 - c)
        sends = []
        for w, out in enumerate(outs):
            hr = out.shape[1] // 2
            for j, (px, py) in enumerate(chips):
                landed = out.at[2 * px + py, pl.ds(c * hr, hr)]
                cp = _remote(landed, landed, d2d_send.at[3 * w + j], d2d_recv.at[3 * w + j], sibling)
                cp.start()
                sends.append(cp)
        for w, out in enumerate(outs):
            hr = out.shape[1] // 2
            for j, (px, py) in enumerate(chips):
                other = out.at[2 * px + py, pl.ds((1 - c) * hr, hr)]
                _remote(other, other, d2d_send.at[3 * w + j], d2d_recv.at[3 * w + j], sibling).wait_recv()
        for cp in sends:
            cp.wait_send()

    dma = pltpu.SemaphoreType.DMA
    return pl.pallas_call(
        body, name="gather_finish",
        out_shape=[jax.ShapeDtypeStruct(l.shape, l.dtype) for l in lands],
        in_specs=[HBM] * n, out_specs=[HBM] * n,
        input_output_aliases={i: i for i in range(n)},
        scratch_shapes=[dma((3 * n,)), dma((3 * n,))],
    )(*lands)


def _adamw(w, g, m, v):
    m = ADAM_B1 * m + (1.0 - ADAM_B1) * g
    v = ADAM_B2 * v + (1.0 - ADAM_B2) * (g * g)
    m_hat = m / (1.0 - ADAM_B1 ** ADAM_STEP)
    v_hat = v / (1.0 - ADAM_B2 ** ADAM_STEP)
    delta = -ADAM_LR * (m_hat / (jnp.sqrt(v_hat) + ADAM_EPS) + ADAM_WD * w)
    return delta, m, v


def _adamw_big(partials, w, m, v):
    rows, cols = w.shape
    tr = 256 if rows % 256 == 0 else rows

    def body(p_ref, w_ref, m_ref, v_ref, g_ref, d_ref, m2_ref, v2_ref):
        g = ((p_ref[0].astype(F32) + p_ref[1].astype(F32)) + p_ref[2].astype(F32)) + p_ref[3].astype(F32)
        g_ref[...] = g
        d_ref[...], m2_ref[...], v2_ref[...] = _adamw(w_ref[...], g, m_ref[...], v_ref[...])

    blk = pl.BlockSpec((tr, cols), lambda i: (i, 0))
    return pl.pallas_call(
        body, name="adamw_big", grid=(rows // tr,),
        in_specs=[pl.BlockSpec((N_CHIPS, tr, cols), lambda i: (0, i, 0)), blk, blk, blk],
        out_specs=[blk] * 4, out_shape=[jax.ShapeDtypeStruct((rows, cols), F32)] * 4,
        compiler_params=_params("parallel"),
    )(partials, w, m, v)


def _sum_devices(gathered, rows):
    cols = gathered.shape[1]

    def body(g_ref, o_ref):
        acc = g_ref[0:rows]
        for d in range(1, N_DEV):
            acc = acc + g_ref[d * rows:(d + 1) * rows]
        o_ref[...] = acc

    return pl.pallas_call(
        body, name="sum_devices", out_shape=jax.ShapeDtypeStruct((rows, cols), F32),
        in_specs=[pl.BlockSpec(memory_space=pltpu.VMEM)], out_specs=pl.BlockSpec(memory_space=pltpu.VMEM),
        compiler_params=pltpu.CompilerParams(vmem_limit_bytes=VMEM_LIMIT_V7X),
    )(gathered)


def _adamw_small(quads):
    n = len(quads)

    def body(*refs):
        ins, outs = refs[:4 * n], refs[4 * n:]
        for t in range(n):
            w, g, m, v = (r[...] for r in ins[4 * t:4 * t + 4])
            outs[3 * t][...], outs[3 * t + 1][...], outs[3 * t + 2][...] = _adamw(w, g, m, v)

    flat = [a for q in quads for a in q]
    vm = pl.BlockSpec(memory_space=pltpu.VMEM)
    res = pl.pallas_call(
        body, name="adamw_small",
        out_shape=[jax.ShapeDtypeStruct(q[0].shape, F32) for q in quads for _ in range(3)],
        in_specs=[vm] * (4 * n), out_specs=[vm] * (3 * n),
    )(*flat)
    return [tuple(res[3 * t:3 * t + 3]) for t in range(n)]


SMALL_PACK_ROWS = 96
_WEIGHTS = ['meta_tokens', 'g_pre_mix', 'w_in', 'conv_w', 'conv_b', 'w_a', 'b_a', 'w_x', 'b_x', 'lru_lambda',
            'attn_sinks', 'w_out', 'g_post_mix', 'g_pre_ffn', 'w_ff1', 'w_ff2', 'g_post_ffn']
_BIG = ['w_in', 'w_out', 'w_ff1', 'w_ff2']


def _pack_small(dmeta, g, loss):
    z = lambda r, c: jnp.zeros((r, c), F32)
    rows = [
        dmeta,
        g['g_pre_mix'], g['g_post_mix'], g['g_pre_ffn'], g['g_post_ffn'],
        jnp.concatenate([g['conv_w'], z(4, 512)], axis=1),
        jnp.concatenate([g['conv_b'], g['b_a']], axis=1),
        jnp.concatenate([g['b_x'], g['lru_lambda']], axis=1),
        jnp.concatenate([g['attn_sinks'], z(1, D_MODEL - ATTN_HEADS)], axis=1),
        jnp.concatenate([loss, z(1, D_MODEL - 1)], axis=1),
        z(4, D_MODEL),
        g['w_a'].reshape(32, D_MODEL), g['w_x'].reshape(32, D_MODEL),
    ]
    return jnp.concatenate(rows, axis=0)


def _unpack_small(s, chip):
    return dict(
        meta_tokens=lax.dynamic_slice(s[0:16], (0, chip * 256), (16, 256)),
        g_pre_mix=s[16:17], g_post_mix=s[17:18], g_pre_ffn=s[18:19], g_post_ffn=s[19:20],
        conv_w=lax.dynamic_slice(s[20:24], (0, chip * 128), (4, 128)).reshape(1, 4, 128),
        conv_b=s[24:25, :512], b_a=s[24:25, 512:], b_x=s[25:26, :512], lru_lambda=s[25:26, 512:],
        attn_sinks=s[26:27, :ATTN_HEADS], loss=s[27, 0],
        w_a=s[32:64].reshape(1, LRU_BLOCKS, LRU_BLOCK, LRU_BLOCK),
        w_x=s[64:96].reshape(1, LRU_BLOCKS, LRU_BLOCK, LRU_BLOCK))


def _as2d(a):
    if a.ndim == 2:
        return a
    return a.reshape(-1, a.shape[-1])


def kernel(x, meta_tokens, g_pre_mix, w_in, conv_w, conv_b, w_a, b_a, w_x, b_x, lru_lambda, attn_sinks, w_out, g_post_mix, g_pre_ffn, w_ff1, w_ff2, g_post_ffn, loss_target, m_meta_tokens, m_g_pre_mix, m_w_in, m_conv_w, m_conv_b, m_w_a, m_b_a, m_w_x, m_b_x, m_lru_lambda, m_attn_sinks, m_w_out, m_g_post_mix, m_g_pre_ffn, m_w_ff1, m_w_ff2, m_g_post_ffn, v_meta_tokens, v_g_pre_mix, v_w_in, v_conv_w, v_conv_b, v_w_a, v_b_a, v_w_x, v_b_x, v_lru_lambda, v_attn_sinks, v_w_out, v_g_post_mix, v_g_pre_ffn, v_w_ff1, v_w_ff2, v_g_post_ffn):
    weights = dict(meta_tokens=meta_tokens, g_pre_mix=g_pre_mix, w_in=w_in, conv_w=conv_w, conv_b=conv_b, w_a=w_a,
                   b_a=b_a, w_x=w_x, b_x=b_x, lru_lambda=lru_lambda, attn_sinks=attn_sinks, w_out=w_out,
                   g_post_mix=g_post_mix, g_pre_ffn=g_pre_ffn, w_ff1=w_ff1, w_ff2=w_ff2, g_post_ffn=g_post_ffn)
    mom1 = dict(zip(_WEIGHTS, [m_meta_tokens, m_g_pre_mix, m_w_in, m_conv_w, m_conv_b, m_w_a, m_b_a, m_w_x, m_b_x,
                               m_lru_lambda, m_attn_sinks, m_w_out, m_g_post_mix, m_g_pre_ffn, m_w_ff1, m_w_ff2,
                               m_g_post_ffn]))
    mom2 = dict(zip(_WEIGHTS, [v_meta_tokens, v_g_pre_mix, v_w_in, v_conv_w, v_conv_b, v_w_a, v_b_a, v_w_x, v_b_x,
                               v_lru_lambda, v_attn_sinks, v_w_out, v_g_post_mix, v_g_pre_ffn, v_w_ff1, v_w_ff2,
                               v_g_post_ffn]))
    xi, yi, ci = _mesh_pos()
    chip = 2 * xi + yi

    tiny = jnp.concatenate([meta_tokens, jnp.pad(conv_w[0], ((0, 4), (0, 128)))], axis=0)
    chip_arr = jnp.reshape(chip, (1,)).astype(jnp.int32)
    big2d = lambda a, name: a[0].T if name == 'w_in' else a[0]
    shards, lands = zip(*[_prep_shard(big2d(weights[n], n), chip_arr) for n in _BIG])
    g_in, g_tiny = _gather_weights(shards[:1], lands[:1], tiny, _prep_tiny(tiny, chip_arr))
    w_in_full = g_in.reshape(IN_WIDTH, D_MODEL)
    meta_full = jnp.concatenate([g_tiny[j, :N_META] for j in range(N_CHIPS)], axis=1)
    conv_w_full = jnp.concatenate([g_tiny[j, N_META:N_META + 4, :128] for j in range(N_CHIPS)], axis=1)
    g_send, g_recv, late_thru, late_lands, token = _split_start(
        "gather_late_start", _gather_copies, shards[1:], lands[1:])

    def late_weights(after):
        _, landed = _split_wait("gather_late_wait", _gather_copies, g_send, g_recv, late_thru, late_lands, after)
        g_out, g_f1, g_f2 = _gather_finish(landed)
        return g_out.reshape(D_MODEL, D_MODEL), g_f1, g_f2

    pos = jnp.stack([ci, chip]).astype(jnp.int32)
    ffn = {}


    def on_ffn_grads(dw1, dw2):
        parts = [dw1, dw2]
        lands = [lax.empty((p.shape[0], p.shape[1] // 2, p.shape[2]), p.dtype) for p in parts]
        ffn['sib'] = _split_start("sibling_ffn_start", _sibling_copies, parts, lands, len(parts))
        return ffn['sib'][4]

    def on_outproj_bwd(dattn):
        send, recv, thru, lands, _ = ffn['sib']
        parts, from_sibling = _split_wait("sibling_ffn_wait", _sibling_copies, send, recv, thru, lands, [dattn])
        cparts_ffn, lands_ffn = zip(*[_chip_presum(p, r, pos) for p, r in zip(parts, from_sibling)])
        ffn['send'], ffn['recv'], ffn['thru'], ffn['lands'], token3 = _split_start(
            "scatter_ffn_start", _scatter_copies, cparts_ffn, lands_ffn)
        return token3

    def on_mixer_grads(dw_in, dw_out):
        parts = [dw_in.reshape(N_CHIPS, IN_WIDTH // N_CHIPS, D_MODEL),
                 dw_out.reshape(N_CHIPS, D_MODEL // N_CHIPS, D_MODEL)]
        cparts, lands = zip(*[_chip_presum(p, r, pos) for p, r in zip(parts, _sibling_exchange(parts, pos))])
        ffn['mixer'] = _split_start("scatter_mixer_start", _scatter_copies, cparts, lands)
        return ffn['mixer'][4]

    head = jnp.concatenate([jnp.zeros((PAD_ROWS, D_MODEL), F32), meta_full], axis=0)
    loss, dx, dhead, grads = _local_step(head, x[0], loss_target[0], g_pre_mix, w_in_full, conv_w_full, conv_b, w_a[0],
                                         b_a, w_x[0], b_x, lru_lambda, attn_sinks, g_post_mix, g_pre_ffn, g_post_ffn,
                                         late_weights, on_ffn_grads, on_outproj_bwd, on_mixer_grads, token)
    grad_x = dx[None]

    pack = _pack_small(dhead[PAD_ROWS:], grads, loss)
    dev = jnp.reshape(4 * xi + 2 * yi + ci, (1,)).astype(jnp.int32)
    s_send, s_recv, s_thru, s_lands, token5 = _split_start(
        "gather_small_start", _all_peers_copies, [pack], [_prep_tiny(pack, dev, N_DEV)], N_DEV - 1)

    send, recv, thru, lands, _ = ffn['mixer']
    mixer_cparts, mixer_lands = _split_wait("scatter_mixer_wait", _scatter_copies, send, recv, thru, lands, [token5])
    ffn_cparts, ffn_lands = _split_wait("scatter_ffn_wait", _scatter_copies, ffn['send'], ffn['recv'], ffn['thru'],
                                        ffn['lands'], mixer_lands)
    chip_partials = _scatter_partials([], [], mixer_cparts + ffn_cparts, mixer_lands + ffn_lands)

    g_out_d, delta, new_m, new_v = {}, {}, {}, {}
    for name, part in zip(_BIG, chip_partials):
        shp = weights[name].shape
        res = _adamw_big(part, big2d(weights[name], name), big2d(mom1[name], name), big2d(mom2[name], name))
        g_out_d[name], delta[name], new_m[name], new_v[name] = (big2d(r[None], name).reshape(shp) for r in res)

    _, (gathered,) = _split_wait("gather_small_wait", _all_peers_copies, s_send, s_recv, s_thru, s_lands,
                                 [g_out_d[n] for n in _BIG])
    small = _unpack_small(_sum_devices(gathered.reshape(N_DEV * SMALL_PACK_ROWS, D_MODEL), SMALL_PACK_ROWS), chip)
    loss = small['loss']
    small_names = [n for n in _WEIGHTS if n not in _BIG]
    quads = [(_as2d(weights[n]), _as2d(small[n]), _as2d(mom1[n]), _as2d(mom2[n])) for n in small_names]
    for name, (d, m2, v2) in zip(small_names, _adamw_small(quads)):
        shp = weights[name].shape
        g_out_d[name] = small[name].reshape(shp)
        delta[name], new_m[name], new_v[name] = d.reshape(shp), m2.reshape(shp), v2.reshape(shp)

    return (loss, grad_x, *[g_out_d[n] for n in _WEIGHTS], *[delta[n] for n in _WEIGHTS],
            *[new_m[n] for n in _WEIGHTS], *[new_v[n] for n in _WEIGHTS])
```

```python
import numpy as np
import jax
import jax.numpy as jnp
from jax import lax
from jax.experimental import pallas as pl
from jax.experimental.pallas import tpu as pltpu

F32 = jnp.float32
BF16 = jnp.bfloat16

D_MODEL = 1024
N_META = 16
BLOCK = 128
PAD_ROWS = BLOCK - N_META
HEAD_DIM = 64
ATTN_HEADS = 8
GQA_GROUP = 4
ATTN_WIDTH = 512
KV_WIDTH = 128
QKV_WIDTH = ATTN_WIDTH + 2 * KV_WIDTH
LRU_WIDTH = 512
LRU_BLOCKS = 8
LRU_BLOCK = 64
LRU_C = 8.0
IN_WIDTH = 1792
D_FF = 4096
N_CHIPS = 4
FF_CHUNK = D_FF // N_CHIPS
EPS = 1e-6
NEG = -1e30

ADAM_LR = 0.001
ADAM_B1 = 0.9
ADAM_B2 = 0.999
ADAM_EPS = 1e-08
ADAM_WD = 0.01
ADAM_STEP = 10

VMEM_LIMIT_V7X = 62 * 1024 * 1024
MESH = pl.DeviceIdType.MESH

NT = (((1,), (1,)), ((), ()))
TN = (((0,), (0,)), ((), ()))


def _row_tile(tp):
    return 640 if tp % 640 == 0 else BLOCK


def _wgrad_row_tile(tp):
    return 1664 if tp % 1664 == 0 else _row_tile(tp)


def _params(*sem):
    return pltpu.CompilerParams(dimension_semantics=sem, vmem_limit_bytes=VMEM_LIMIT_V7X)


def _dot(a, b):
    return jnp.dot(a, b, preferred_element_type=F32)


def _dot_nt(a, b):
    return lax.dot_general(a, b, NT, preferred_element_type=F32)


def _dot_tn(a, b):
    return lax.dot_general(a, b, TN, preferred_element_type=F32)


def _rms(x):
    rs = lax.rsqrt(jnp.mean(x * x, axis=-1, keepdims=True) + EPS)
    return x * rs, rs


def _rms_bwd(xhat, rs, g, dy):
    dyg = dy * g
    dx = rs * (dyg - xhat * jnp.mean(dyg * xhat, axis=-1, keepdims=True))
    dg = jnp.sum(dy * xhat, axis=0, keepdims=True)
    return dx, dg


def _gelu(x):
    k = 0.7978845608028654
    t = jnp.tanh(x * (k + (k * 0.044715) * (x * x)))
    return (0.5 * x) * (1.0 + t), t


def _gelu_grad(x, t):
    k = 0.7978845608028654
    return 0.5 * (1.0 + t) + 0.5 * x * (1.0 - t * t) * k * (1.0 + 3 * 0.044715 * x * x)


def _sigmoid(x):
    return 0.5 * jnp.tanh(0.5 * x) + 0.5


def _one_minus_exp2(y):
    t = jnp.tanh(y)
    return (-2.0 * t) / (1.0 - t)


def _softplus(x):
    return jnp.maximum(x, 0.0) + jnp.log1p(jnp.exp(-jnp.abs(x)))


def _seq_specs(tr, delay=0):
    qb = tr // BLOCK
    tile = lambda i: jnp.maximum(i - delay, 0)
    return [pl.BlockSpec((BLOCK, D_MODEL), lambda i, *_, s=s: (jnp.maximum(tile(i) * qb + s - 1, 0), 0))
            for s in range(qb)]


def _seq_tile(head, pieces, i):
    first = jnp.where(i == 0, head, pieces[0][...])
    return jnp.concatenate([first] + [p[...] for p in pieces[1:]], axis=0)


GROUP_ROWS = GQA_GROUP * BLOCK


def _attn_bias():
    j = np.arange(2 * BLOCK)[:, None]
    i = np.arange(BLOCK)[None, :]
    band = (j - i >= 1) & (j - i <= BLOCK)
    out = []
    for n in range(3):
        ok = band & ((n - 1) * BLOCK + j >= PAD_ROWS) if n < 2 else band
        out.append(np.tile(np.where(ok, 0.0, NEG).astype(np.float32), (1, GQA_GROUP)))
    return jnp.asarray(np.stack(out))


def _heads_t(at, g):
    heads = range(GQA_GROUP * g, GQA_GROUP * (g + 1))
    return jnp.concatenate([at[h * HEAD_DIM:(h + 1) * HEAD_DIM] for h in heads], axis=1).astype(BF16)


def _from_heads_t(groups):
    pairs = []
    for p in groups:
        for h in range(0, GQA_GROUP, 2):
            two = jnp.concatenate([p[:, h * BLOCK:(h + 1) * BLOCK], p[:, (h + 1) * BLOCK:(h + 2) * BLOCK]], axis=0)
            pairs.append(two.T)
    return jnp.concatenate(pairs, axis=1)


def _stack_heads(a, g):
    heads = range(GQA_GROUP * g, GQA_GROUP * (g + 1))
    return jnp.concatenate([a[:, h * HEAD_DIM:(h + 1) * HEAD_DIM] for h in heads], axis=0)


def _unstack_heads(groups):
    return jnp.concatenate([p[h * BLOCK:(h + 1) * BLOCK] for p in groups for h in range(GQA_GROUP)], axis=1)


def _attn_probs_t(k_g, qg, bias, sink_row):
    st = _dot_nt(k_g, qg) + bias
    m = jnp.maximum(jnp.max(st, axis=0, keepdims=True), sink_row)
    p = jnp.exp(st - m)
    es = jnp.exp(sink_row - m)
    inv = 1.0 / (jnp.sum(p, axis=0, keepdims=True) + es)
    return p * inv, es * inv


def _attn_consts(sinks):
    return jnp.repeat(sinks.reshape(ATTN_HEADS), BLOCK).reshape(ATTN_HEADS // GQA_GROUP, GROUP_ROWS), _attn_bias()


_SINK_SPEC = pl.BlockSpec((ATTN_HEADS // GQA_GROUP, GROUP_ROWS), lambda n: (0, 0))
_BIAS_SPEC = pl.BlockSpec((3, 2 * BLOCK, GROUP_ROWS), lambda n: (0, 0, 0))
_QSCALE = HEAD_DIM ** -0.5


def _kv_specs(tr):
    qb = tr // BLOCK
    prev = lambda col: pl.BlockSpec((BLOCK, KV_WIDTH), lambda t: (jnp.maximum(t * qb - 1, 0), col))
    cur = lambda col: pl.BlockSpec((tr, KV_WIDTH), lambda t: (t, col))
    return [prev(4), cur(4), prev(5), cur(5)]


def _block_bias(b_ref, t, qb, i):
    return b_ref[2] if i >= 2 else b_ref[jnp.minimum(t * qb + i, 2)]


N_KV = ATTN_HEADS // GQA_GROUP


def _prob_specs(qb):
    return [pl.BlockSpec((qb, N_KV, 2 * BLOCK, GROUP_ROWS), lambda t: (t, 0, 0, 0)),
            pl.BlockSpec((qb, SUBLANES, GROUP_ROWS), lambda t: (t, 0, 0))]


def _attn_fwd(qkv, sinks):
    tp = qkv.shape[0]
    tr = _row_tile(tp)
    qb, nb = tr // BLOCK, tp // BLOCK
    sink_rows, bias = _attn_consts(sinks)

    def body(s_ref, b_ref, q_ref, kp_ref, kc_ref, vp_ref, vc_ref, o_ref, p_ref, ps_ref):
        t = pl.program_id(0)
        k_all = jnp.concatenate([kp_ref[...], kc_ref[...]], axis=0)
        v_all = jnp.concatenate([vp_ref[...], vc_ref[...]], axis=0)
        for i in range(qb):
            rows = slice(i * BLOCK, (i + 1) * BLOCK)
            q = q_ref[rows]
            k2, v2 = k_all[i * BLOCK:(i + 2) * BLOCK], v_all[i * BLOCK:(i + 2) * BLOCK]
            bias_n = _block_bias(b_ref, t, qb, i)
            outs, sink_probs = [], []
            for g in range(N_KV):
                cols = slice(g * HEAD_DIM, (g + 1) * HEAD_DIM)
                qg = _stack_heads(q, g) * jnp.asarray(_QSCALE, BF16)
                p, ps = _attn_probs_t(k2[:, cols], qg, bias_n, s_ref[g:g + 1])
                pb = p.astype(BF16)
                p_ref[i, g] = pb
                sink_probs.append(ps)
                outs.append(_dot_tn(pb, v2[:, cols]))
            o_ref[rows] = _unstack_heads(outs).astype(BF16)
            ps_ref[i] = jnp.concatenate(sink_probs + [jnp.zeros((SUBLANES - N_KV, GROUP_ROWS), F32)], axis=0)

    return pl.pallas_call(
        body, name="attn_fwd", grid=(tp // tr,),
        in_specs=[_SINK_SPEC, _BIAS_SPEC, pl.BlockSpec((tr, ATTN_WIDTH), lambda t: (t, 0))] + _kv_specs(tr),
        out_specs=[pl.BlockSpec((tr, ATTN_WIDTH), lambda t: (t, 0))] + _prob_specs(qb),
        out_shape=[jax.ShapeDtypeStruct((tp, ATTN_WIDTH), BF16),
                   jax.ShapeDtypeStruct((nb, N_KV, 2 * BLOCK, GROUP_ROWS), BF16),
                   jax.ShapeDtypeStruct((nb, SUBLANES, GROUP_ROWS), F32)],
        compiler_params=_params("parallel"),
    )(sink_rows, bias, qkv, qkv, qkv, qkv, qkv)


def _conv_taps(x, halo):
    ext = jnp.concatenate([halo, x], axis=0)
    return [ext[8:] if k == 3 else pltpu.roll(ext, 3 - k, 0)[8:] for k in range(4)]


def _lru_gates(xc, wa, ba, wx, bx, sp):
    xb = xc.astype(BF16)
    r = _sigmoid(_dot(xb, wa) + ba)
    ig = _sigmoid(_dot(xb, wx) + bx)
    log_a = (-LRU_C * sp) * r
    a = jnp.exp(log_a)
    mult = jnp.sqrt(_one_minus_exp2(log_a))
    return xb, r, ig, a, mult


SUBLANES = 8


def _scan_fwd(a, b, h_in):
    n, width = a.shape
    a, b = (v.reshape(n // SUBLANES, SUBLANES, width) for v in (a, b))
    in_group = lax.broadcasted_iota(jnp.int32, a.shape, 1)
    for d in (1, 2, 4):
        keep = in_group >= d
        b = jnp.where(keep, a * pltpu.roll(b, d, 1) + b, b)
        a = jnp.where(keep, a * pltpu.roll(a, d, 1), a)
    a, b = a.reshape(n, width), b.reshape(n, width)
    out, carry = [], h_in
    for g in range(0, n, SUBLANES):
        h = a[g:g + SUBLANES] * carry + b[g:g + SUBLANES]
        out.append(h)
        carry = h[SUBLANES - 1:]
    return jnp.concatenate(out, axis=0)


def _scan_rev(c, b, g_in):
    n, width = c.shape
    c, b = (v.reshape(n // SUBLANES, SUBLANES, width) for v in (c, b))
    in_group = lax.broadcasted_iota(jnp.int32, c.shape, 1)
    for d in (1, 2, 4):
        keep = in_group < SUBLANES - d
        b = jnp.where(keep, b + c * pltpu.roll(b, SUBLANES - d, 1), b)
        c = jnp.where(keep, c * pltpu.roll(c, SUBLANES - d, 1), c)
    c, b = c.reshape(n, width), b.reshape(n, width)
    out, carry = [], g_in
    for g in range(n - SUBLANES, -1, -SUBLANES):
        r = b[g:g + SUBLANES] + c[g:g + SUBLANES] * carry
        out.append(r)
        carry = r[:1]
    return jnp.concatenate(out[::-1], axis=0)


def _inproj_lru_fwd(head, x, g, w_in, conv_w, conv_b, wa, ba, wx, bx, lam, token):
    tp = BLOCK + x.shape[0]
    tr = _row_tile(tp)
    qb, nt = tr // BLOCK, tp // tr
    small = [conv_w, conv_b, wa, ba, wx, bx, lam]

    def body(*refs):
        head_ref, pieces = refs[0], refs[1:1 + qb]
        g_ref, w_ref, _, cw_ref, cb_ref, wa_ref, ba_ref, wx_ref, bx_ref, lam_ref = refs[1 + qb:11 + qb]
        u_ref, qkv_ref, xr_ref, yr_ref, hr_ref, rec_ref, zbuf, halo, hprev = refs[11 + qb:]
        i = pl.program_id(0)
        cur = i % 2

        @pl.when(i == 0)
        def _():
            halo[...] = jnp.zeros_like(halo)
            hprev[...] = jnp.zeros_like(hprev)
            zbuf[1] = jnp.zeros((tr, 2 * LRU_WIDTH), F32)

        def recurrent_branch(valid):
            cw, cb = cw_ref[...], cb_ref[...]
            wa_m, ba_v, wx_m, bx_v = wa_ref[...], ba_ref[...], wx_ref[...], bx_ref[...]
            sp = _softplus(-lam_ref[...])
            before, h_last = halo[...], hprev[0:1]
            for b in range(qb):
                rows = slice(b * BLOCK, (b + 1) * BLOCK)
                xy = zbuf[1 - cur, rows]
                xin = xy[:, :LRU_WIDTH]
                taps = _conv_taps(xin, before)
                before = xin[BLOCK - 8:]
                xc = cb + sum(cw[k:k + 1] * taps[k] for k in range(4))
                _, _, ig, a, mult = _lru_gates(xc, wa_m, ba_v, wx_m, bx_v, sp)
                u = mult * (ig * xc)
                if b == 0:
                    pos = (i - 1) * tr + lax.broadcasted_iota(jnp.int32, xc.shape, 0)
                    u = jnp.where(pos >= PAD_ROWS, u, 0.0)
                h = _scan_fwd(a, u, h_last)
                h_last = h[BLOCK - 1:]
                hr_ref[rows] = h
                gl, _ = _gelu(xy[:, LRU_WIDTH:])
                rec_ref[rows] = (gl * h).astype(BF16)
            halo[...] = jnp.where(valid, before, 0.0)
            hprev[0:1] = jnp.where(valid, h_last, 0.0)

        def projection():
            xhat, _ = _rms(_seq_tile(head_ref[...], pieces, i))
            u = (xhat * g_ref[...]).astype(BF16)
            u_ref[...] = u
            z = _dot_nt(u, w_ref[...])
            qkv_ref[...] = z[:, :QKV_WIDTH].astype(BF16)
            xr_ref[...] = z[:, QKV_WIDTH:QKV_WIDTH + LRU_WIDTH]
            yr_ref[...] = z[:, QKV_WIDTH + LRU_WIDTH:]
            zbuf[cur] = z[:, QKV_WIDTH:]

        @pl.when(i < nt)
        def _():
            recurrent_branch(i >= 1)
            projection()

        @pl.when(i == nt)
        def _():
            recurrent_branch(True)

    last = nt - 1
    this_row = lambda w: pl.BlockSpec((tr, w), lambda i: (jnp.minimum(i, last), 0))
    prev_row = lambda w: pl.BlockSpec((tr, w), lambda i: (jnp.maximum(i - 1, 0), 0))
    full = lambda a: pl.BlockSpec(a.shape, lambda i: (0,) * a.ndim)
    piece_specs = [pl.BlockSpec((BLOCK, D_MODEL), lambda i, s=s: (jnp.maximum(jnp.minimum(i, last) * qb + s - 1, 0), 0))
                   for s in range(qb)]
    return pl.pallas_call(
        body, name="inproj_lru_fwd", grid=(nt + 1,),
        in_specs=[full(head)] + piece_specs + [full(g), full(w_in), full(token)] + [full(a) for a in small],
        out_specs=[this_row(D_MODEL), this_row(QKV_WIDTH), this_row(LRU_WIDTH), this_row(LRU_WIDTH),
                   prev_row(LRU_WIDTH), prev_row(LRU_WIDTH)],
        out_shape=[jax.ShapeDtypeStruct((tp, D_MODEL), BF16), jax.ShapeDtypeStruct((tp, QKV_WIDTH), BF16),
                   jax.ShapeDtypeStruct((tp, LRU_WIDTH), F32), jax.ShapeDtypeStruct((tp, LRU_WIDTH), F32),
                   jax.ShapeDtypeStruct((tp, LRU_WIDTH), F32), jax.ShapeDtypeStruct((tp, LRU_WIDTH), BF16)],
        scratch_shapes=[pltpu.VMEM((2, tr, 2 * LRU_WIDTH), F32), pltpu.VMEM((8, LRU_WIDTH), F32),
                        pltpu.VMEM((8, LRU_WIDTH), F32)],
        compiler_params=_params("arbitrary"),
    )(head, *([x] * qb), g, w_in, token, *small)


def _outproj_fwd(attn, rec, w_out, head, x, g_post_mix, g_pre_ffn):
    tp = attn.shape[0]
    tr = _row_tile(tp)
    qb = tr // BLOCK

    def body(*refs):
        a_ref, r_ref, w_ref, head_ref = refs[:4]
        pieces = refs[4:4 + qb]
        gm_ref, gf_ref, mix_ref, h1_ref, u1_ref = refs[4 + qb:]
        mix = _dot(a_ref[...], w_ref[:ATTN_WIDTH]) + _dot(r_ref[...], w_ref[ATTN_WIDTH:])
        mix_ref[...] = mix
        mhat, _ = _rms(mix)
        h1 = _seq_tile(head_ref[...], pieces, pl.program_id(0)) + mhat * gm_ref[...]
        h1_ref[...] = h1
        hhat, _ = _rms(h1)
        u1_ref[...] = (hhat * gf_ref[...]).astype(BF16)

    row = lambda w: pl.BlockSpec((tr, w), lambda i: (i, 0))
    full = lambda a: pl.BlockSpec(a.shape, lambda i: (0,) * a.ndim)
    return pl.pallas_call(
        body, name="outproj_fwd", grid=(tp // tr,),
        in_specs=[row(ATTN_WIDTH), row(LRU_WIDTH), full(w_out), full(head)] + _seq_specs(tr)
        + [full(g_post_mix), full(g_pre_ffn)],
        out_specs=[row(D_MODEL), row(D_MODEL), row(D_MODEL)],
        out_shape=[jax.ShapeDtypeStruct((tp, D_MODEL), F32), jax.ShapeDtypeStruct((tp, D_MODEL), F32),
                   jax.ShapeDtypeStruct((tp, D_MODEL), BF16)],
        compiler_params=_params("parallel"),
    )(attn, rec, w_out, head, *([x] * qb), g_post_mix, g_pre_ffn)


def _resident(a):
    return pl.BlockSpec(a.shape, lambda *_: (0,) * a.ndim, pipeline_mode=pl.Buffered(1))


def _ffn_fwd(u1, w1, w2, h1, tgt, g_post_ffn):
    tp = h1.shape[0]
    tr = _row_tile(tp)
    qb, nt = tr // BLOCK, tp // tr
    sr = tr // N_CHIPS

    def body(*refs):
        u_ref, w1_ref, w2_ref, h1_ref = refs[:4]
        t_pieces = refs[4:4 + qb]
        g_ref, r1_ref, dy_ref, df2_ref, loss_ref, dg_ref, acc = refs[4 + qb:]
        i, c = pl.program_id(0), pl.program_id(1)
        cur = i % 2

        @pl.when((i == 0) & (c == 0))
        def _():
            loss_ref[...] = jnp.zeros_like(loss_ref)
            dg_ref[...] = jnp.zeros_like(dg_ref)
            acc[1] = jnp.zeros((tr, D_MODEL), F32)

        def matmuls():
            r = jnp.maximum(_dot(u_ref[...], w1_ref[c]), 0.0)
            r1_ref[...] = r.astype(BF16)
            return _dot((r * r).astype(BF16), w2_ref[c])

        def finish_previous_tile(k, valid):
            lo, hi = k * sr, (k + 1) * sr
            g = g_ref[...]
            fhat, rs = _rms(acc[1 - cur, lo:hi])
            h2 = h1_ref[...] + fhat * g
            rows = (i - 1) * tr + lo + lax.broadcasted_iota(jnp.int32, h2.shape, 0)
            tgt = jnp.concatenate([p[max(lo - s * BLOCK, 0):min(hi - s * BLOCK, BLOCK)] for s, p in enumerate(t_pieces)
                                   if lo < (s + 1) * BLOCK and hi > s * BLOCK], axis=0)
            err = jnp.where((rows >= BLOCK) & valid, h2 - tgt, 0.0)
            dy = err * (1.0 / D_MODEL)
            dy_ref[...] = dy
            loss_ref[...] += (0.5 / D_MODEL) * jnp.sum(err * err)
            df2, dg = _rms_bwd(fhat, rs, g, dy)
            df2_ref[...] = df2.astype(BF16)
            dg_ref[...] += dg

        for k in range(N_CHIPS):
            @pl.when((c == k) & (i < nt))
            def _(k=k):
                finish_previous_tile(k, i >= 1)
                if k == 0:
                    acc[cur] = matmuls()
                else:
                    acc[cur] += matmuls()

            @pl.when((c == k) & (i == nt))
            def _(k=k):
                finish_previous_tile(k, True)

    last = nt - 1
    this_row = pl.BlockSpec((tr, D_MODEL), lambda i, c: (jnp.minimum(i, last), 0))
    prev_quarter = pl.BlockSpec((sr, D_MODEL), lambda i, c: (jnp.maximum(i - 1, 0) * N_CHIPS + c, 0))
    prev_quarter_out = pl.BlockSpec(
        (sr, D_MODEL), lambda i, c: (jnp.where(i == 0, nt * N_CHIPS, (i - 1) * N_CHIPS + c), 0))
    full = lambda a: pl.BlockSpec(a.shape, lambda i, c: (0,) * a.ndim)
    return pl.pallas_call(
        body, name="ffn_fwd", grid=(nt + 1, N_CHIPS),
        in_specs=[this_row, _resident(w1), _resident(w2), prev_quarter] + _seq_specs(tr, delay=1) + [full(g_post_ffn)],
        out_specs=[pl.BlockSpec((tr, FF_CHUNK), lambda i, c: (jnp.minimum(i, last), jnp.where(i < nt, c, N_CHIPS - 1))),
                   prev_quarter_out, prev_quarter_out,
                   pl.BlockSpec((1, 1), lambda i, c: (0, 0)), pl.BlockSpec((1, D_MODEL), lambda i, c: (0, 0))],
        out_shape=[jax.ShapeDtypeStruct((tp, D_FF), BF16), jax.ShapeDtypeStruct((tp + sr, D_MODEL), F32),
                   jax.ShapeDtypeStruct((tp + sr, D_MODEL), BF16), jax.ShapeDtypeStruct((1, 1), F32),
                   jax.ShapeDtypeStruct((1, D_MODEL), F32)],
        scratch_shapes=[pltpu.VMEM((2, tr, D_MODEL), F32)],
        compiler_params=_params("arbitrary", "arbitrary"),
    )(u1, w1, w2, h1, *([tgt] * qb), g_post_ffn)


def _ffn_bwd_data(df2, r1, w1, w2, dy, h1, mix, g_pre_ffn, g_post_mix):
    tp = h1.shape[0]
    tr = _row_tile(tp)
    nt = tp // tr
    sr = tr // N_CHIPS

    def body(df2_ref, r1_ref, w1_ref, w2_ref, dy_ref, h1_ref, mix_ref, gf_ref, gm_ref,
             da_ref, dh1_ref, dmix_ref, dgf_ref, dgm_ref, acc):
        i, c = pl.program_id(0), pl.program_id(1)
        cur = i % 2

        @pl.when((i == 0) & (c == 0))
        def _():
            dgf_ref[...] = jnp.zeros_like(dgf_ref)
            dgm_ref[...] = jnp.zeros_like(dgm_ref)
            acc[1] = jnp.zeros((tr, D_MODEL), F32)

        def matmuls():
            df = _dot_nt(df2_ref[...], w2_ref[c])
            da = (df * (2.0 * r1_ref[...].astype(F32))).astype(BF16)
            da_ref[...] = da
            return _dot_nt(da, w1_ref[c])

        def finish_previous_tile(k, valid):
            lo, hi = k * sr, (k + 1) * sr
            hhat, rs = _rms(h1_ref[...])
            dx, dgf = _rms_bwd(hhat, rs, gf_ref[...], acc[1 - cur, lo:hi])
            dh1 = dy_ref[...] + dx
            dh1_ref[...] = dh1
            mhat, rsm = _rms(mix_ref[...])
            dmix, dgm = _rms_bwd(mhat, rsm, gm_ref[...], dh1)
            dmix_ref[...] = dmix.astype(BF16)
            dgf_ref[...] += jnp.where(valid, dgf, 0.0)
            dgm_ref[...] += jnp.where(valid, dgm, 0.0)

        for k in range(N_CHIPS):
            @pl.when((c == k) & (i < nt))
            def _(k=k):
                finish_previous_tile(k, i >= 1)
                if k == 0:
                    acc[cur] = matmuls()
                else:
                    acc[cur] += matmuls()

            @pl.when((c == k) & (i == nt))
            def _(k=k):
                finish_previous_tile(k, True)

    last = nt - 1
    this_row = pl.BlockSpec((tr, D_MODEL), lambda i, c: (jnp.minimum(i, last), 0))
    prev_quarter = pl.BlockSpec((sr, D_MODEL), lambda i, c: (jnp.maximum(i - 1, 0) * N_CHIPS + c, 0))
    prev_quarter_out = pl.BlockSpec(
        (sr, D_MODEL), lambda i, c: (jnp.where(i == 0, nt * N_CHIPS, (i - 1) * N_CHIPS + c), 0))
    chunk = pl.BlockSpec((tr, FF_CHUNK), lambda i, c: (jnp.minimum(i, last), jnp.where(i < nt, c, N_CHIPS - 1)))
    gain = pl.BlockSpec((1, D_MODEL), lambda i, c: (0, 0))
    return pl.pallas_call(
        body, name="ffn_bwd_data", grid=(nt + 1, N_CHIPS),
        in_specs=[this_row, chunk, _resident(w1), _resident(w2), prev_quarter, prev_quarter, prev_quarter, gain, gain],
        out_specs=[chunk, prev_quarter_out, prev_quarter_out, gain, gain],
        out_shape=[jax.ShapeDtypeStruct((tp, D_FF), BF16), jax.ShapeDtypeStruct((tp + sr, D_MODEL), F32),
                   jax.ShapeDtypeStruct((tp + sr, D_MODEL), BF16), jax.ShapeDtypeStruct((1, D_MODEL), F32),
                   jax.ShapeDtypeStruct((1, D_MODEL), F32)],
        scratch_shapes=[pltpu.VMEM((2, tr, D_MODEL), F32)],
        compiler_params=_params("arbitrary", "arbitrary"),
    )(df2, r1, w1, w2, dy, h1, mix, g_pre_ffn, g_post_mix)


def _ffn_bwd_weights(u1, da1, r1, df2):
    tp = u1.shape[0]
    tr = _wgrad_row_tile(tp)

    def body(u_ref, da_ref, r1_ref, df2_ref, dw1_ref, dw2_ref):
        i = pl.program_id(1)
        r = r1_ref[...].astype(F32)
        p1 = _dot_tn(u_ref[...], da_ref[...])
        p2 = _dot_tn((r * r).astype(BF16), df2_ref[...])

        @pl.when(i == 0)
        def _():
            dw1_ref[0] = p1
            dw2_ref[0] = p2

        @pl.when(i > 0)
        def _():
            dw1_ref[0] += p1
            dw2_ref[0] += p2

    row = pl.BlockSpec((tr, D_MODEL), lambda c, i: (i, 0))
    chunk = pl.BlockSpec((tr, FF_CHUNK), lambda c, i: (i, c))
    return pl.pallas_call(
        body, name="ffn_bwd_weights", grid=(N_CHIPS, tp // tr),
        in_specs=[row, chunk, chunk, row],
        out_specs=[pl.BlockSpec((1, D_MODEL, FF_CHUNK), lambda c, i: (c, 0, 0)),
                   pl.BlockSpec((1, FF_CHUNK, D_MODEL), lambda c, i: (c, 0, 0))],
        out_shape=[jax.ShapeDtypeStruct((N_CHIPS, D_MODEL, FF_CHUNK), F32),
                   jax.ShapeDtypeStruct((N_CHIPS, FF_CHUNK, D_MODEL), F32)],
        compiler_params=_params("parallel", "arbitrary"),
    )(u1, da1, r1, df2)


N_VEC_ROWS = 8


def _outproj_lru_bwd(dmix, w_out, attn, rec, xr, yr, hr, conv_w, conv_b, wa, ba, wx, bx, lam, token):
    tp = xr.shape[0]
    tr = _row_tile(tp)
    qb, nt = tr // BLOCK, tp // tr

    def body(dm_ref, w_ref, at_ref, rc_ref, xr_ref, xh_ref, yr_ref, hr_ref, hp_ref,
             cw_ref, cb_ref, wa_ref, ba_ref, wx_ref, bx_ref, lam_ref, _,
             dxr_ref, dyr_ref, dat_ref, dwo_ref, dwa_ref, dwx_ref, vec_ref, g_next, a_next, dxc_next, dsp):
        s = pl.program_id(0)
        t = nt - 1 - s

        @pl.when(s == 0)
        def _():
            g_next[...] = jnp.zeros_like(g_next)
            a_next[...] = jnp.zeros_like(a_next)
            dxc_next[...] = jnp.zeros_like(dxc_next)
            dsp[...] = jnp.zeros_like(dsp)
            dwo_ref[...] = jnp.zeros_like(dwo_ref)
            dwa_ref[...] = jnp.zeros_like(dwa_ref)
            dwx_ref[...] = jnp.zeros_like(dwx_ref)
            vec_ref[...] = jnp.zeros_like(vec_ref)

        dm = dm_ref[...]
        dcat = _dot_nt(dm, w_ref[...])
        dat_ref[...] = dcat[:, :ATTN_WIDTH].astype(BF16)
        drec_tile = dcat[:, ATTN_WIDTH:]
        dwo_ref[:ATTN_WIDTH] += _dot_tn(at_ref[...], dm)
        dwo_ref[ATTN_WIDTH:] += _dot_tn(rc_ref[...], dm)

        first_tile = t == 0
        cw, cb = cw_ref[...], cb_ref[...]
        lam_v = lam_ref[...]
        sp = _softplus(-lam_v)
        wa_m, ba_v, wx_m, bx_v = wa_ref[...], ba_ref[...], wx_ref[...], bx_ref[...]
        rows = lax.broadcasted_iota(jnp.int32, (BLOCK, LRU_WIDTH), 0)
        col = lambda v: jnp.sum(v, axis=0, keepdims=True)

        g_after, a_after, dxc_after = g_next[0:1], a_next[0:1], dxc_next[...]
        xbs, dgrs, dgis = [], [], []
        vec = [jnp.zeros((1, LRU_WIDTH), F32) for _ in range(N_VEC_ROWS)]
        for i in reversed(range(qb)):
            blk = slice(i * BLOCK, (i + 1) * BLOCK)
            if i == 0:
                x_before = jnp.where(first_tile, 0.0, xh_ref[...])
                h_before = jnp.where(first_tile, 0.0, hp_ref[7:8])
            else:
                x_before = xr_ref[i * BLOCK - 8:i * BLOCK]
                h_before = hr_ref[i * BLOCK - 1:i * BLOCK]
            taps = _conv_taps(xr_ref[blk], x_before)
            xc = cb + sum(cw[k:k + 1] * taps[k] for k in range(4))
            xb, r, ig, a, mult = _lru_gates(xc, wa_m, ba_v, wx_m, bx_v, sp)

            yr_v = yr_ref[blk]
            gl, th = _gelu(yr_v)
            h = hr_ref[blk]
            drec = drec_tile[blk]
            dyr_ref[blk] = (drec * h * _gelu_grad(yr_v, th)).astype(BF16)

            a_up = jnp.where(rows == BLOCK - 1, a_after, pltpu.roll(a, BLOCK - 1, 0))
            g = _scan_rev(a_up, drec * gl, g_after)
            g_after, a_after = g[0:1], a[0:1]

            h_prev = jnp.where(rows == 0, h_before, pltpu.roll(h, 1, 0))
            du, da = g, g * h_prev
            if i == 0:
                real = (t * tr + rows) >= PAD_ROWS
                du, da = jnp.where(real, du, 0.0), jnp.where(real, da, 0.0)
            dmult = du * (ig * xc)
            dig = du * (mult * xc)
            dxc = du * (mult * ig)
            dlog_a = da * a - dmult * (a * a / mult)
            if i == 0:
                dlog_a = jnp.where(real, dlog_a, 0.0)
            dgr = (dlog_a * (-LRU_C * sp)) * (r * (1.0 - r))
            dgi = dig * (ig * (1.0 - ig))
            dgr_b, dgi_b = dgr.astype(BF16), dgi.astype(BF16)
            dxc = dxc + _dot_nt(dgr_b, wa_m) + _dot_nt(dgi_b, wx_m)
            xbs.append(xb)
            dgrs.append(dgr_b)
            dgis.append(dgi_b)

            ext = jnp.concatenate([dxc, dxc_after], axis=0)
            up = [ext[:BLOCK] if j == 0 else pltpu.roll(ext, BLOCK + 8 - j, 0)[:BLOCK] for j in range(4)]
            dxr_ref[blk] = sum(cw[k:k + 1] * up[3 - k] for k in range(4)).astype(BF16)
            dxc_after = dxc[:8]

            for k in range(4):
                vec[k] = vec[k] + col(dxc * taps[k])
            vec[4] = vec[4] + col(dxc)
            vec[5] = vec[5] + col(dgr)
            vec[6] = vec[6] + col(dgi)
            vec[7] = vec[7] + col(dlog_a * (-LRU_C * r))

        g_next[0:1], a_next[0:1], dxc_next[...] = g_after, a_after, dxc_after
        xb_all = jnp.concatenate(xbs, axis=0)
        dwa_ref[...] += _dot_tn(xb_all, jnp.concatenate(dgrs, axis=0))
        dwx_ref[...] += _dot_tn(xb_all, jnp.concatenate(dgis, axis=0))
        for k in range(7):
            vec_ref[k:k + 1] += vec[k]
        dsp[0:1] += vec[7]

        @pl.when(s == nt - 1)
        def _():
            vec_ref[7:8] = dsp[0:1] * (-_sigmoid(-lam_v))

    blk_spec = pl.BlockSpec((tr, LRU_WIDTH), lambda s: (nt - 1 - s, 0))
    rows_before = pl.BlockSpec((8, LRU_WIDTH), lambda s: (jnp.maximum((nt - 1 - s) * (tr // 8) - 1, 0), 0))
    full = lambda a: pl.BlockSpec(a.shape, lambda s: (0,) * a.ndim)
    small = [conv_w, conv_b, wa, ba, wx, bx, lam, token]
    sq = pl.BlockSpec((LRU_WIDTH, LRU_WIDTH), lambda s: (0, 0))
    wide = pl.BlockSpec((tr, D_MODEL), lambda s: (nt - 1 - s, 0))
    whole = pl.BlockSpec((D_MODEL, D_MODEL), lambda s: (0, 0))
    return pl.pallas_call(
        body, name="outproj_lru_bwd", grid=(nt,),
        in_specs=[wide, whole, blk_spec, blk_spec, blk_spec, rows_before, blk_spec, blk_spec, rows_before]
        + [full(a) for a in small],
        out_specs=[blk_spec, blk_spec, blk_spec, whole, sq, sq, pl.BlockSpec((N_VEC_ROWS, LRU_WIDTH), lambda s: (0, 0))],
        out_shape=[jax.ShapeDtypeStruct((tp, LRU_WIDTH), BF16), jax.ShapeDtypeStruct((tp, LRU_WIDTH), BF16),
                   jax.ShapeDtypeStruct((tp, ATTN_WIDTH), BF16), jax.ShapeDtypeStruct((D_MODEL, D_MODEL), F32),
                   jax.ShapeDtypeStruct((LRU_WIDTH, LRU_WIDTH), F32), jax.ShapeDtypeStruct((LRU_WIDTH, LRU_WIDTH), F32),
                   jax.ShapeDtypeStruct((N_VEC_ROWS, LRU_WIDTH), F32)],
        scratch_shapes=[pltpu.VMEM((8, LRU_WIDTH), F32)] * 4,
        compiler_params=_params("arbitrary"),
    )(dmix, w_out, attn, rec, xr, xr, yr, hr, hr, *small)


def _attn_bwd(qkv, dattn, probs, sink_probs, token):
    tp = qkv.shape[0]
    tr = _row_tile(tp)
    qb, nt = tr // BLOCK, tp // tr
    n_groups = N_KV

    def body(p_ref, ps_ref, q_ref, kp_ref, kc_ref, vp_ref, vc_ref, do_ref, _, dq_ref, dkv_ref, ex_ref, ds_ref, dsink):
        t = pl.program_id(0)

        @pl.when(t == 0)
        def _():
            dsink[...] = jnp.zeros_like(dsink)

        k_all = jnp.concatenate([kp_ref[...], kc_ref[...]], axis=0)
        v_all = jnp.concatenate([vp_ref[...], vc_ref[...]], axis=0)
        tail = None
        for i in range(qb):
            rows = slice(i * BLOCK, (i + 1) * BLOCK)
            qt = (q_ref[rows].astype(F32) * _QSCALE).T
            dot = do_ref[rows].astype(F32).T
            k2, v2 = k_all[i * BLOCK:(i + 2) * BLOCK], v_all[i * BLOCK:(i + 2) * BLOCK]
            dqs, dks, dvs = [], [], []
            for g in range(n_groups):
                cols = slice(g * HEAD_DIM, (g + 1) * HEAD_DIM)
                k_g, v_g = k2[:, cols], v2[:, cols]
                qgt, dogt = _heads_t(qt, g), _heads_t(dot, g)
                pb = p_ref[i, g]
                p = pb.astype(F32)
                dpt = _dot(v_g, dogt)
                delta = jnp.sum(p * dpt, axis=0, keepdims=True)
                dst = (p * (dpt - delta)).astype(BF16)
                dqs.append(_dot_tn(k_g, dst) * _QSCALE)
                dks.append(_dot_nt(qgt, dst))
                dvs.append(_dot_nt(dogt, pb))
                dsink[g:g + 1] -= ps_ref[i, g:g + 1] * delta
            dq_ref[rows] = _from_heads_t(dqs).astype(BF16)
            dkv = jnp.concatenate([jnp.concatenate(dks, axis=0).T, jnp.concatenate(dvs, axis=0).T], axis=1)
            if i == 0:
                ex_ref[0] = dkv[:BLOCK]
            else:
                dkv_ref[(i - 1) * BLOCK:i * BLOCK] = (tail + dkv[:BLOCK]).astype(BF16)
            tail = dkv[BLOCK:]
        dkv_ref[(qb - 1) * BLOCK:] = tail.astype(BF16)

        @pl.when(t == nt - 1)
        def _():
            lane = lax.broadcasted_iota(jnp.int32, (1, ATTN_HEADS), 1)
            acc = jnp.zeros((1, ATTN_HEADS), F32)
            for h in range(ATTN_HEADS):
                g, hh = divmod(h, GQA_GROUP)
                acc = acc + jnp.where(lane == h, jnp.sum(dsink[g:g + 1, hh * BLOCK:(hh + 1) * BLOCK]), 0.0)
            ds_ref[...] = acc

    cur = lambda w: pl.BlockSpec((tr, w), lambda t: (t, 0))
    return pl.pallas_call(
        body, name="attn_bwd", grid=(nt,),
        in_specs=_prob_specs(qb) + [cur(ATTN_WIDTH)] + _kv_specs(tr)
        + [cur(ATTN_WIDTH), pl.BlockSpec(token.shape, lambda t: (0, 0))],
        out_specs=[cur(ATTN_WIDTH), cur(2 * KV_WIDTH), pl.BlockSpec((1, BLOCK, 2 * KV_WIDTH), lambda t: (t, 0, 0)),
                   pl.BlockSpec((1, ATTN_HEADS), lambda t: (0, 0))],
        out_shape=[jax.ShapeDtypeStruct((tp, ATTN_WIDTH), BF16), jax.ShapeDtypeStruct((tp, 2 * KV_WIDTH), BF16),
                   jax.ShapeDtypeStruct((nt, BLOCK, 2 * KV_WIDTH), F32), jax.ShapeDtypeStruct((1, ATTN_HEADS), F32)],
        scratch_shapes=[pltpu.VMEM((n_groups, GROUP_ROWS), F32)],
        compiler_params=_params("arbitrary"),
    )(probs, sink_probs, qkv, qkv, qkv, qkv, qkv, dattn, token)


def _fix_dkv(dkv, dkv_extra):
    tp = dkv.shape[0]
    tr = _row_tile(tp)
    nt, qb = tp // tr, tr // BLOCK
    if nt == 1:
        return dkv

    def body(d_ref, ex_ref, o_ref):
        o_ref[...] = (d_ref[...].astype(F32) + ex_ref[0]).astype(BF16)

    last = pl.BlockSpec((BLOCK, 2 * KV_WIDTH), lambda t: (t * qb + qb - 1, 0))
    return pl.pallas_call(
        body, name="fix_dkv", grid=(nt - 1,),
        in_specs=[last, pl.BlockSpec((1, BLOCK, 2 * KV_WIDTH), lambda t: (t + 1, 0, 0))],
        out_specs=last, out_shape=jax.ShapeDtypeStruct(dkv.shape, dkv.dtype),
        input_output_aliases={0: 0}, compiler_params=_params("parallel"),
    )(dkv, dkv_extra)


def _inproj_wgrad(dq, dkv, dxr, dyr, u0):
    tp = dq.shape[0]
    tr = _wgrad_row_tile(tp)

    def body(dq_ref, dkv_ref, dxr_ref, dyr_ref, u_ref, dw_ref):
        i = pl.program_id(0)
        dz = jnp.concatenate([dq_ref[...], dkv_ref[...], dxr_ref[...], dyr_ref[...]], axis=1)
        pw = _dot_tn(dz, u_ref[...])

        @pl.when(i == 0)
        def _():
            dw_ref[...] = pw

        @pl.when(i > 0)
        def _():
            dw_ref[...] += pw

    row = lambda w: pl.BlockSpec((tr, w), lambda i: (i, 0))
    return pl.pallas_call(
        body, name="inproj_wgrad", grid=(tp // tr,),
        in_specs=[row(ATTN_WIDTH), row(2 * KV_WIDTH), row(LRU_WIDTH), row(LRU_WIDTH), row(D_MODEL)],
        out_specs=pl.BlockSpec((IN_WIDTH, D_MODEL), lambda i: (0, 0)),
        out_shape=jax.ShapeDtypeStruct((IN_WIDTH, D_MODEL), F32),
        compiler_params=_params("arbitrary"),
    )(dq, dkv, dxr, dyr, u0)


def _inproj_dgrad(dq, dkv, dxr, dyr, w_in, head, x, dh1, g, token):
    tp = dq.shape[0]
    tr = _row_tile(tp)
    nt, qb = tp // tr, tr // BLOCK

    def body(*refs):
        dq_ref, dkv_ref, dxr_ref, dyr_ref, w_ref, head_ref = refs[:6]
        pieces = refs[6:6 + qb]
        dh1_ref, g_ref, _, gx_ref, dhead_ref, dg_ref, buf, sems = refs[6 + qb:]
        i = pl.program_id(0)
        slot = i % 2

        def out_copy(step, at):
            return pltpu.make_async_copy(buf.at[at], gx_ref.at[pl.ds(step * tr - BLOCK, tr)], sems.at[at])

        dz = jnp.concatenate([dq_ref[...], dkv_ref[...], dxr_ref[...], dyr_ref[...]], axis=1)
        du = _dot(dz, w_ref[...])
        hhat, rs = _rms(_seq_tile(head_ref[...], pieces, i))
        dx, dg = _rms_bwd(hhat, rs, g_ref[...], du)
        dh0 = dh1_ref[...] + dx

        @pl.when(i >= 3)
        def _():
            out_copy(i - 2, slot).wait()

        buf[slot] = dh0

        @pl.when(i == 0)
        def _():
            dg_ref[...] = dg
            dhead_ref[...] = dh0[:BLOCK]
            if tr > BLOCK:
                first = pltpu.make_async_copy(buf.at[0, pl.ds(BLOCK, tr - BLOCK)], gx_ref.at[pl.ds(0, tr - BLOCK)],
                                              sems.at[0])
                first.start()
                first.wait()

        @pl.when(i >= 1)
        def _():
            dg_ref[...] += dg
            out_copy(i, slot).start()

        @pl.when(i == nt - 1)
        def _():
            if nt >= 3:
                out_copy(nt - 2, (nt - 2) % 2).wait()
            if nt >= 2:
                out_copy(nt - 1, (nt - 1) % 2).wait()

    row = lambda w: pl.BlockSpec((tr, w), lambda i: (i, 0))
    full = lambda shape: pl.BlockSpec(shape, lambda i: (0,) * len(shape))
    return pl.pallas_call(
        body, name="inproj_dgrad", grid=(tp // tr,),
        in_specs=[row(ATTN_WIDTH), row(2 * KV_WIDTH), row(LRU_WIDTH), row(LRU_WIDTH), full(w_in.shape),
                  full(head.shape)] + _seq_specs(tr) + [row(D_MODEL), full(g.shape), full(token.shape)],
        out_specs=[pl.BlockSpec(memory_space=pl.ANY), full((BLOCK, D_MODEL)), full((1, D_MODEL))],
        out_shape=[jax.ShapeDtypeStruct(x.shape, F32), jax.ShapeDtypeStruct((BLOCK, D_MODEL), F32),
                   jax.ShapeDtypeStruct((1, D_MODEL), F32)],
        scratch_shapes=[pltpu.VMEM((2, tr, D_MODEL), F32), pltpu.SemaphoreType.DMA((2,))],
        compiler_params=_params("arbitrary"),
    )(dq, dkv, dxr, dyr, w_in, head, *([x] * qb), dh1, g, token)


def _dense_block_diag(w):
    eye = jnp.eye(LRU_BLOCKS, dtype=w.dtype)
    return (w[:, :, None, :] * eye[:, None, :, None]).reshape(LRU_WIDTH, LRU_WIDTH)


def _diag_blocks(dense):
    d4 = dense.reshape(LRU_BLOCKS, LRU_BLOCK, LRU_BLOCKS, LRU_BLOCK)
    return jnp.stack([d4[n, :, n, :] for n in range(LRU_BLOCKS)])


def _local_step(head, x, tgt, g_pre_mix, w_in, conv_w, conv_b, w_a, b_a, w_x, b_x, lam, sinks, g_post_mix,
                g_pre_ffn, g_post_ffn, late_weights, on_ffn_grads, on_outproj_bwd, on_mixer_grads, token):
    wa = _dense_block_diag(w_a).astype(BF16)
    wx = _dense_block_diag(w_x).astype(BF16)

    u0, qkv, xr, yr, hr, rec = _inproj_lru_fwd(head, x, g_pre_mix, w_in, conv_w, conv_b, wa, b_a, wx, b_x, lam, token)
    attn, probs, sink_probs = _attn_fwd(qkv, sinks)
    w_out, w1, w2 = late_weights([attn, rec])
    mix, h1, u1 = _outproj_fwd(attn, rec, w_out, head, x, g_post_mix, g_pre_ffn)
    r1, dy, df2, loss, dg_post_ffn = _ffn_fwd(u1, w1, w2, h1, tgt, g_post_ffn)

    da1, dh1, dmix, dg_pre_ffn, dg_post_mix = _ffn_bwd_data(df2, r1, w1, w2, dy, h1, mix, g_pre_ffn, g_post_mix)
    dw1, dw2 = _ffn_bwd_weights(u1, da1, r1, df2)
    token2 = on_ffn_grads(dw1, dw2)
    dxr, dyr, dattn, dw_out, dwa, dwx, vec = _outproj_lru_bwd(dmix, w_out, attn, rec, xr, yr, hr, conv_w, conv_b,
                                                              wa, b_a, wx, b_x, lam, token2)
    token3 = on_outproj_bwd(dattn)
    dq, dkv, dkv_extra, dsinks = _attn_bwd(qkv, dattn, probs, sink_probs, token3)
    dkv = _fix_dkv(dkv, dkv_extra)
    dw_in = _inproj_wgrad(dq, dkv, dxr, dyr, u0)
    token4 = on_mixer_grads(dw_in, dw_out)
    dx, dhead, dg_pre_mix = _inproj_dgrad(dq, dkv, dxr, dyr, w_in, head, x, dh1, g_pre_mix, token4)

    grads = dict(
        g_pre_mix=dg_pre_mix, conv_w=vec[0:4], conv_b=vec[4:5], w_a=_diag_blocks(dwa), b_a=vec[5:6],
        w_x=_diag_blocks(dwx), b_x=vec[6:7], lru_lambda=vec[7:8], attn_sinks=dsinks,
        g_post_mix=dg_post_mix, g_pre_ffn=dg_pre_ffn, g_post_ffn=dg_post_ffn)
    return loss, dx, dhead, grads


HBM = pl.BlockSpec(memory_space=pltpu.HBM)


def _mesh_pos():
    return lax.axis_index("x"), lax.axis_index("y"), lax.axis_index("c")


def _other_chips(x, y):
    return [(1 - x, y), (x, 1 - y), (1 - x, 1 - y)]


def _remote(src, dst, send_sem, recv_sem, to):
    return pltpu.make_async_remote_copy(src_ref=src, dst_ref=dst, send_sem=send_sem, recv_sem=recv_sem,
                                        device_id=to, device_id_type=MESH)


def _gather_weights(shards, lands, tiny, tiny_land):
    nbig = len(shards)

    def body(*refs):
        srcs, tiny_src = refs[:nbig], refs[nbig]
        outs, tiny_out = refs[2 * nbig + 2:3 * nbig + 2], refs[3 * nbig + 2]
        ici_send, ici_recv, d2d_send, d2d_recv, tiny_send, tiny_recv = refs[3 * nbig + 3:]
        x, y, c = _mesh_pos()
        me = 2 * x + y
        chips = _other_chips(x, y)
        sibling = (x, y, 1 - c)
        sends = []
        for w, (src, out) in enumerate(zip(srcs, outs)):
            hr = src.shape[0] // 2
            for j, chip in enumerate(chips):
                k = 3 * w + j
                cp = _remote(src.at[pl.ds(c * hr, hr)], out.at[me, pl.ds(c * hr, hr)],
                             ici_send.at[k], ici_recv.at[k], (*chip, c))
                cp.start()
                sends.append(cp)
        for j, chip in enumerate(chips):
            cp = _remote(tiny_src, tiny_out.at[me], tiny_send.at[j], tiny_recv.at[j], (*chip, c))
            cp.start()
            sends.append(cp)
        for w, (src, out) in enumerate(zip(srcs, outs)):
            hr = src.shape[0] // 2
            for j, (px, py) in enumerate(chips):
                k = 3 * w + j
                landed = out.at[2 * px + py, pl.ds(c * hr, hr)]
                _remote(landed, landed, ici_send.at[k], ici_recv.at[k], sibling).wait_recv()
                cp = _remote(landed, landed, d2d_send.at[k], d2d_recv.at[k], sibling)
                cp.start()
                sends.append(cp)
        for w, (src, out) in enumerate(zip(srcs, outs)):
            hr = src.shape[0] // 2
            for j, (px, py) in enumerate(chips):
                k = 3 * w + j
                other = out.at[2 * px + py, pl.ds((1 - c) * hr, hr)]
                _remote(other, other, d2d_send.at[k], d2d_recv.at[k], sibling).wait_recv()
        for j, (px, py) in enumerate(chips):
            blk = tiny_out.at[2 * px + py]
            _remote(blk, blk, tiny_send.at[j], tiny_recv.at[j], sibling).wait_recv()
        for cp in sends:
            cp.wait_send()

    out_shape = [jax.ShapeDtypeStruct(l.shape, l.dtype) for l in list(lands) + [tiny_land]]
    n = 3 * nbig
    return pl.pallas_call(
        body, name="gather_weights", out_shape=out_shape,
        in_specs=[HBM] * (2 * nbig + 2), out_specs=[HBM] * (nbig + 1),
        input_output_aliases={nbig + 1 + i: i for i in range(nbig + 1)},
        scratch_shapes=[pltpu.SemaphoreType.DMA((n,)),
                        pltpu.SemaphoreType.DMA((n,)), pltpu.SemaphoreType.DMA((n,)), pltpu.SemaphoreType.DMA((n,)),
                        pltpu.SemaphoreType.DMA((3,)), pltpu.SemaphoreType.DMA((3,))],
    )(*shards, tiny, *lands, tiny_land)


def _prep_shard(w, me):
    rows, cols = w.shape
    tr = 256 if rows % 256 == 0 else rows

    def body(me_ref, w_ref, s_ref, l_ref):
        b = w_ref[...].astype(BF16)
        s_ref[...] = b
        l_ref[0] = b

    return pl.pallas_call(
        body, name="prep_shard",
        grid_spec=pltpu.PrefetchScalarGridSpec(
            num_scalar_prefetch=1, grid=(rows // tr,),
            in_specs=[pl.BlockSpec((tr, cols), lambda i, me_ref: (i, 0))],
            out_specs=[pl.BlockSpec((tr, cols), lambda i, me_ref: (i, 0)),
                       pl.BlockSpec((1, tr, cols), lambda i, me_ref: (me_ref[0], i, 0))]),
        out_shape=[jax.ShapeDtypeStruct((rows, cols), BF16), jax.ShapeDtypeStruct((N_CHIPS, rows, cols), BF16)],
        compiler_params=_params("parallel"),
    )(me, w)


def _prep_tiny(tiny, me, slots=N_CHIPS):
    def body(me_ref, t_ref, l_ref):
        l_ref[0] = t_ref[...]

    return pl.pallas_call(
        body, name="prep_tiny",
        grid_spec=pltpu.PrefetchScalarGridSpec(
            num_scalar_prefetch=1, grid=(1,),
            in_specs=[pl.BlockSpec(tiny.shape, lambda i, me_ref: (0, 0))],
            out_specs=pl.BlockSpec((1,) + tiny.shape, lambda i, me_ref: (me_ref[0], 0, 0))),
        out_shape=jax.ShapeDtypeStruct((slots,) + tiny.shape, tiny.dtype),
    )(me, tiny)


N_DEV = 8


def _sibling_exchange(parts, token):
    def body(*refs):
        n = len(parts)
        srcs, outs, send_sems, recv_sems = refs[:n], refs[n + 1:2 * n + 1], refs[2 * n + 1], refs[2 * n + 2]
        x, y, c = _mesh_pos()
        sibling = (x, y, 1 - c)
        cps = []
        for w, (src, out) in enumerate(zip(srcs, outs)):
            hr = src.shape[1] // 2
            cp = _remote(src.at[:, pl.ds((1 - c) * hr, hr)], out, send_sems.at[w], recv_sems.at[w], sibling)
            cp.start()
            cps.append(cp)
        for cp in cps:
            cp.wait()

    n = len(parts)
    return pl.pallas_call(
        body, name="sibling_exchange",
        out_shape=[jax.ShapeDtypeStruct((p.shape[0], p.shape[1] // 2, p.shape[2]), p.dtype) for p in parts],
        in_specs=[HBM] * n + [pl.BlockSpec(memory_space=pl.ANY)], out_specs=[HBM] * n,
        scratch_shapes=[pltpu.SemaphoreType.DMA((n,)), pltpu.SemaphoreType.DMA((n,))],
    )(*parts, token)


def _chip_presum(part, from_sibling, pos):
    _, hr, cols = from_sibling.shape
    tr = 256 if hr % 256 == 0 else hr
    steps = hr // tr

    def body(pos_ref, a_ref, b_ref, o_ref, land_ref):
        s = (a_ref[...] + b_ref[...]).astype(BF16)
        o_ref[...] = s

        @pl.when(pl.program_id(1) == pos_ref[1])
        def _():
            land_ref[...] = s

    return pl.pallas_call(
        body, name="chip_presum",
        grid_spec=pltpu.PrefetchScalarGridSpec(
            num_scalar_prefetch=1, grid=(steps, N_CHIPS),
            in_specs=[pl.BlockSpec((1, tr, cols), lambda i, j, p: (j, p[0] * steps + i, 0)),
                      pl.BlockSpec((1, tr, cols), lambda i, j, p: (j, i, 0))],
            out_specs=[pl.BlockSpec((1, tr, cols), lambda i, j, p: (j, i, 0)),
                       pl.BlockSpec((1, tr, cols), lambda i, j, p: (p[1], p[0] * steps + i, 0))]),
        out_shape=[jax.ShapeDtypeStruct(from_sibling.shape, BF16),
                   jax.ShapeDtypeStruct((N_CHIPS, 2 * hr, cols), BF16)],
        compiler_params=_params("arbitrary", "arbitrary"),
    )(pos, part, from_sibling)


def _scatter_partials(cparts, lands, done_cparts=(), done_lands=()):
    n_new = len(cparts)
    nw = n_new + len(done_cparts)

    def body(*refs):
        srcs = refs[:nw]
        outs = refs[2 * nw:3 * nw]
        own_send, own_recv, ici_send, ici_recv, d2d_send, d2d_recv = refs[3 * nw:]
        x, y, c = _mesh_pos()
        me = 2 * x + y
        chips = _other_chips(x, y)
        sibling = (x, y, 1 - c)
        sends = []
        for w in list(range(n_new, nw)) + list(range(n_new)):
            src, out = srcs[w], outs[w]
            hr = src.shape[1]
            mine = out.at[me, pl.ds(c * hr, hr)]
            cp = _remote(src.at[me], mine, own_send.at[w], own_recv.at[w], sibling)
            cp.start()
            sends.append(cp)
            for j, (px, py) in enumerate(chips):
                if w >= n_new:
                    break
                k = 3 * w + j
                cp = _remote(src.at[2 * px + py], mine, ici_send.at[k], ici_recv.at[k], (px, py, c))
                cp.start()
                sends.append(cp)
        for w in list(range(n_new, nw)) + list(range(n_new)):
            src, out = srcs[w], outs[w]
            hr = src.shape[1]
            for j, (px, py) in enumerate(chips):
                k = 3 * w + j
                landed = out.at[2 * px + py, pl.ds(c * hr, hr)]
                if w < n_new:
                    _remote(landed, landed, ici_send.at[k], ici_recv.at[k], sibling).wait_recv()
                cp = _remote(landed, landed, d2d_send.at[k], d2d_recv.at[k], sibling)
                cp.start()
                sends.append(cp)
        for w, (src, out) in enumerate(zip(srcs, outs)):
            hr = src.shape[1]
            other = out.at[me, pl.ds((1 - c) * hr, hr)]
            _remote(other, other, own_send.at[w], own_recv.at[w], sibling).wait_recv()
            for j, (px, py) in enumerate(chips):
                k = 3 * w + j
                other = out.at[2 * px + py, pl.ds((1 - c) * hr, hr)]
                _remote(other, other, d2d_send.at[k], d2d_recv.at[k], sibling).wait_recv()
        for cp in sends:
            cp.wait_send()

    n = 3 * nw
    dma = pltpu.SemaphoreType.DMA
    every = list(cparts) + list(done_cparts)
    every_lands = list(lands) + list(done_lands)
    return pl.pallas_call(
        body, name="scatter_partials",
        out_shape=[jax.ShapeDtypeStruct(l.shape, l.dtype) for l in every_lands],
        in_specs=[HBM] * (2 * nw), out_specs=[HBM] * nw,
        input_output_aliases={nw + i: i for i in range(nw)},
        scratch_shapes=[dma((nw,)), dma((nw,)), dma((n,)), dma((n,)), dma((n,)), dma((n,))],
    )(*every, *every_lands)


SEM = pl.BlockSpec(memory_space=pltpu.SEMAPHORE)
SPLIT_COPY = pltpu.CompilerParams(has_side_effects=pltpu.SideEffectType.DATAFLOW_SIDE_EFFECTING)


def _hbm(a):
    return pltpu.with_memory_space_constraint(a, pltpu.HBM)


def _gather_copies(srcs, lands, send_sems, recv_sems):
    x, y, c = _mesh_pos()
    me = 2 * x + y
    sends, recvs = [], []
    for w, (src, land) in enumerate(zip(srcs, lands)):
        hr = src.shape[0] // 2
        for j, (px, py) in enumerate(_other_chips(x, y)):
            k = 3 * w + j
            sends.append(_remote(src.at[pl.ds(c * hr, hr)], land.at[me, pl.ds(c * hr, hr)],
                                 send_sems.at[k], recv_sems.at[k], (px, py, c)))
            got = land.at[2 * px + py, pl.ds(c * hr, hr)]
            recvs.append(_remote(got, got, send_sems.at[k], recv_sems.at[k], (px, py, c)))
    return sends, recvs


def _scatter_copies(srcs, lands, send_sems, recv_sems):
    x, y, c = _mesh_pos()
    me = 2 * x + y
    sends, recvs = [], []
    for w, (src, land) in enumerate(zip(srcs, lands)):
        hr = src.shape[1]
        for j, (px, py) in enumerate(_other_chips(x, y)):
            k = 3 * w + j
            sends.append(_remote(src.at[2 * px + py], land.at[me, pl.ds(c * hr, hr)],
                                 send_sems.at[k], recv_sems.at[k], (px, py, c)))
            got = land.at[2 * px + py, pl.ds(c * hr, hr)]
            recvs.append(_remote(got, got, send_sems.at[k], recv_sems.at[k], (px, py, c)))
    return sends, recvs


def _sibling_copies(srcs, lands, send_sems, recv_sems):
    x, y, c = _mesh_pos()
    sibling = (x, y, 1 - c)
    sends, recvs = [], []
    for w, (src, land) in enumerate(zip(srcs, lands)):
        hr = src.shape[1] // 2
        sends.append(_remote(src.at[:, pl.ds((1 - c) * hr, hr)], land, send_sems.at[w], recv_sems.at[w], sibling))
        recvs.append(_remote(land, land, send_sems.at[w], recv_sems.at[w], sibling))
    return sends, recvs


def _all_peers_copies(srcs, lands, send_sems, recv_sems):
    x, y, c = _mesh_pos()
    (src,), (land,) = srcs, lands
    flip = lambda v, bit: 1 - v if bit else v
    sends, recvs = [], []
    for k in range(N_DEV - 1):
        px, py, pc = flip(x, (k + 1) & 4), flip(y, (k + 1) & 2), flip(c, (k + 1) & 1)
        sends.append(_remote(src, land.at[4 * x + 2 * y + c], send_sems.at[k], recv_sems.at[k], (px, py, pc)))
        got = land.at[4 * px + 2 * py + pc]
        recvs.append(_remote(got, got, send_sems.at[k], recv_sems.at[k], (px, py, pc)))
    return sends, recvs


def _split_start(name, copies_of, srcs, land_shapes, n_copies=None):
    n = len(srcs)
    k = 3 * n if n_copies is None else n_copies

    def body(*refs):
        src_refs, land_refs = refs[:n], refs[n:2 * n]
        send_sems, recv_sems = refs[2 * n], refs[2 * n + 1]
        token = refs[-1]
        sends, _ = copies_of(src_refs, land_refs, send_sems, recv_sems)
        for cp in sends:
            cp.start()
        token[...] = jnp.zeros_like(token)

    lands = [_hbm(s) for s in land_shapes]
    dma = pltpu.SemaphoreType.DMA
    res = pl.pallas_call(
        body, name=name,
        out_shape=(dma((k,)), dma((k,)), *[pltpu.HBM(s.shape, s.dtype) for s in srcs],
                   *[pltpu.HBM(s.shape, s.dtype) for s in land_shapes], jax.ShapeDtypeStruct((8, 128), F32)),
        in_specs=[HBM] * (2 * n),
        out_specs=(SEM, SEM, *([HBM] * (2 * n)), pl.BlockSpec(memory_space=pltpu.VMEM)),
        input_output_aliases={i: 2 + i for i in range(2 * n)},
        compiler_params=SPLIT_COPY,
    )(*[_hbm(s) for s in srcs], *lands)
    return res[0], res[1], list(res[2:2 + n]), list(res[2 + n:2 + 2 * n]), res[-1]


def _split_wait(name, copies_of, send_sems, recv_sems, srcs, lands, after):
    n = len(srcs)

    def body(*refs):
        src_refs, land_refs = refs[:n], refs[n:2 * n]
        sends, recvs = copies_of(src_refs, land_refs, refs[2 * n], refs[2 * n + 1])
        for cp in sends:
            cp.wait_send()
        for cp in recvs:
            cp.wait_recv()

    res = pl.pallas_call(
        body, name=name,
        out_shape=tuple(pltpu.HBM(s.shape, s.dtype) for s in list(srcs) + list(lands)),
        in_specs=[HBM] * (2 * n) + [SEM, SEM] + [pl.BlockSpec(memory_space=pl.ANY)] * len(after),
        out_specs=tuple([HBM] * (2 * n)),
        input_output_aliases={i: i for i in range(2 * n)},
        compiler_params=SPLIT_COPY,
    )(*srcs, *lands, send_sems, recv_sems, *after)
    return list(res[:n]), list(res[n:])


def _gather_finish(lands):
    n = len(lands)

    def body(*refs):
        outs = refs[n:2 * n]
        d2d_send, d2d_recv = refs[2 * n:]
        x, y, c = _mesh_pos()
        chips = _other_chips(x, y)
        sibling = (x, y, 1 - c)
        sends = []
        for w, out in enumerate(outs):
            hr = out.shape[1] // 2
            for j, (px, py) in enumerate(chips):
                landed = out.at[2 * px + py, pl.ds(c * hr, hr)]
                cp = _remote(landed, landed, d2d_send.at[3 * w + j], d2d_recv.at[3 * w + j], sibling)
                cp.start()
                sends.append(cp)
        for w, out in enumerate(outs):
            hr = out.shape[1] // 2
            for j, (px, py) in enumerate(chips):
                other = out.at[2 * px + py, pl.ds((1 - c) * hr, hr)]
                _remote(other, other, d2d_send.at[3 * w + j], d2d_recv.at[3 * w + j], sibling).wait_recv()
        for cp in sends:
            cp.wait_send()

    dma = pltpu.SemaphoreType.DMA
    return pl.pallas_call(
        body, name="gather_finish",
        out_shape=[jax.ShapeDtypeStruct(l.shape, l.dtype) for l in lands],
        in_specs=[HBM] * n, out_specs=[HBM] * n,
        input_output_aliases={i: i for i in range(n)},
        scratch_shapes=[dma((3 * n,)), dma((3 * n,))],
    )(*lands)


def _adamw(w, g, m, v):
    m = ADAM_B1 * m + (1.0 - ADAM_B1) * g
    v = ADAM_B2 * v + (1.0 - ADAM_B2) * (g * g)
    m_hat = m / (1.0 - ADAM_B1 ** ADAM_STEP)
    v_hat = v / (1.0 - ADAM_B2 ** ADAM_STEP)
    delta = -ADAM_LR * (m_hat / (jnp.sqrt(v_hat) + ADAM_EPS) + ADAM_WD * w)
    return delta, m, v


def _adamw_big(partials, w, m, v):
    rows, cols = w.shape
    tr = 256 if rows % 256 == 0 else rows

    def body(p_ref, w_ref, m_ref, v_ref, g_ref, d_ref, m2_ref, v2_ref):
        g = ((p_ref[0].astype(F32) + p_ref[1].astype(F32)) + p_ref[2].astype(F32)) + p_ref[3].astype(F32)
        g_ref[...] = g
        d_ref[...], m2_ref[...], v2_ref[...] = _adamw(w_ref[...], g, m_ref[...], v_ref[...])

    blk = pl.BlockSpec((tr, cols), lambda i: (i, 0))
    return pl.pallas_call(
        body, name="adamw_big", grid=(rows // tr,),
        in_specs=[pl.BlockSpec((N_CHIPS, tr, cols), lambda i: (0, i, 0)), blk, blk, blk],
        out_specs=[blk] * 4, out_shape=[jax.ShapeDtypeStruct((rows, cols), F32)] * 4,
        compiler_params=_params("parallel"),
    )(partials, w, m, v)


def _sum_devices(gathered, rows):
    cols = gathered.shape[1]

    def body(g_ref, o_ref):
        acc = g_ref[0:rows]
        for d in range(1, N_DEV):
            acc = acc + g_ref[d * rows:(d + 1) * rows]
        o_ref[...] = acc

    return pl.pallas_call(
        body, name="sum_devices", out_shape=jax.ShapeDtypeStruct((rows, cols), F32),
        in_specs=[pl.BlockSpec(memory_space=pltpu.VMEM)], out_specs=pl.BlockSpec(memory_space=pltpu.VMEM),
        compiler_params=pltpu.CompilerParams(vmem_limit_bytes=VMEM_LIMIT_V7X),
    )(gathered)


def _adamw_small(quads):
    n = len(quads)

    def body(*refs):
        ins, outs = refs[:4 * n], refs[4 * n:]
        for t in range(n):
            w, g, m, v = (r[...] for r in ins[4 * t:4 * t + 4])
            outs[3 * t][...], outs[3 * t + 1][...], outs[3 * t + 2][...] = _adamw(w, g, m, v)

    flat = [a for q in quads for a in q]
    vm = pl.BlockSpec(memory_space=pltpu.VMEM)
    res = pl.pallas_call(
        body, name="adamw_small",
        out_shape=[jax.ShapeDtypeStruct(q[0].shape, F32) for q in quads for _ in range(3)],
        in_specs=[vm] * (4 * n), out_specs=[vm] * (3 * n),
    )(*flat)
    return [tuple(res[3 * t:3 * t + 3]) for t in range(n)]


SMALL_PACK_ROWS = 96
_WEIGHTS = ['meta_tokens', 'g_pre_mix', 'w_in', 'conv_w', 'conv_b', 'w_a', 'b_a', 'w_x', 'b_x', 'lru_lambda',
            'attn_sinks', 'w_out', 'g_post_mix', 'g_pre_ffn', 'w_ff1', 'w_ff2', 'g_post_ffn']
_BIG = ['w_in', 'w_out', 'w_ff1', 'w_ff2']


def _pack_small(dmeta, g, loss):
    z = lambda r, c: jnp.zeros((r, c), F32)
    rows = [
        dmeta,
        g['g_pre_mix'], g['g_post_mix'], g['g_pre_ffn'], g['g_post_ffn'],
        jnp.concatenate([g['conv_w'], z(4, 512)], axis=1),
        jnp.concatenate([g['conv_b'], g['b_a']], axis=1),
        jnp.concatenate([g['b_x'], g['lru_lambda']], axis=1),
        jnp.concatenate([g['attn_sinks'], z(1, D_MODEL - ATTN_HEADS)], axis=1),
        jnp.concatenate([loss, z(1, D_MODEL - 1)], axis=1),
        z(4, D_MODEL),
        g['w_a'].reshape(32, D_MODEL), g['w_x'].reshape(32, D_MODEL),
    ]
    return jnp.concatenate(rows, axis=0)


def _unpack_small(s, chip):
    return dict(
        meta_tokens=lax.dynamic_slice(s[0:16], (0, chip * 256), (16, 256)),
        g_pre_mix=s[16:17], g_post_mix=s[17:18], g_pre_ffn=s[18:19], g_post_ffn=s[19:20],
        conv_w=lax.dynamic_slice(s[20:24], (0, chip * 128), (4, 128)).reshape(1, 4, 128),
        conv_b=s[24:25, :512], b_a=s[24:25, 512:], b_x=s[25:26, :512], lru_lambda=s[25:26, 512:],
        attn_sinks=s[26:27, :ATTN_HEADS], loss=s[27, 0],
        w_a=s[32:64].reshape(1, LRU_BLOCKS, LRU_BLOCK, LRU_BLOCK),
        w_x=s[64:96].reshape(1, LRU_BLOCKS, LRU_BLOCK, LRU_BLOCK))


def _as2d(a):
    if a.ndim == 2:
        return a
    return a.reshape(-1, a.shape[-1])


def kernel(x, meta_tokens, g_pre_mix, w_in, conv_w, conv_b, w_a, b_a, w_x, b_x, lru_lambda, attn_sinks, w_out, g_post_mix, g_pre_ffn, w_ff1, w_ff2, g_post_ffn, loss_target, m_meta_tokens, m_g_pre_mix, m_w_in, m_conv_w, m_conv_b, m_w_a, m_b_a, m_w_x, m_b_x, m_lru_lambda, m_attn_sinks, m_w_out, m_g_post_mix, m_g_pre_ffn, m_w_ff1, m_w_ff2, m_g_post_ffn, v_meta_tokens, v_g_pre_mix, v_w_in, v_conv_w, v_conv_b, v_w_a, v_b_a, v_w_x, v_b_x, v_lru_lambda, v_attn_sinks, v_w_out, v_g_post_mix, v_g_pre_ffn, v_w_ff1, v_w_ff2, v_g_post_ffn):
    weights = dict(meta_tokens=meta_tokens, g_pre_mix=g_pre_mix, w_in=w_in, conv_w=conv_w, conv_b=conv_b, w_a=w_a,
                   b_a=b_a, w_x=w_x, b_x=b_x, lru_lambda=lru_lambda, attn_sinks=attn_sinks, w_out=w_out,
                   g_post_mix=g_post_mix, g_pre_ffn=g_pre_ffn, w_ff1=w_ff1, w_ff2=w_ff2, g_post_ffn=g_post_ffn)
    mom1 = dict(zip(_WEIGHTS, [m_meta_tokens, m_g_pre_mix, m_w_in, m_conv_w, m_conv_b, m_w_a, m_b_a, m_w_x, m_b_x,
                               m_lru_lambda, m_attn_sinks, m_w_out, m_g_post_mix, m_g_pre_ffn, m_w_ff1, m_w_ff2,
                               m_g_post_ffn]))
    mom2 = dict(zip(_WEIGHTS, [v_meta_tokens, v_g_pre_mix, v_w_in, v_conv_w, v_conv_b, v_w_a, v_b_a, v_w_x, v_b_x,
                               v_lru_lambda, v_attn_sinks, v_w_out, v_g_post_mix, v_g_pre_ffn, v_w_ff1, v_w_ff2,
                               v_g_post_ffn]))
    xi, yi, ci = _mesh_pos()
    chip = 2 * xi + yi

    tiny = jnp.concatenate([meta_tokens, jnp.pad(conv_w[0], ((0, 4), (0, 128)))], axis=0)
    chip_arr = jnp.reshape(chip, (1,)).astype(jnp.int32)
    big2d = lambda a, name: a[0].T if name == 'w_in' else a[0]
    shards, lands = zip(*[_prep_shard(big2d(weights[n], n), chip_arr) for n in _BIG])
    g_in, g_tiny = _gather_weights(shards[:1], lands[:1], tiny, _prep_tiny(tiny, chip_arr))
    w_in_full = g_in.reshape(IN_WIDTH, D_MODEL)
    meta_full = jnp.concatenate([g_tiny[j, :N_META] for j in range(N_CHIPS)], axis=1)
    conv_w_full = jnp.concatenate([g_tiny[j, N_META:N_META + 4, :128] for j in range(N_CHIPS)], axis=1)
    g_send, g_recv, late_thru, late_lands, token = _split_start(
        "gather_late_start", _gather_copies, shards[1:], lands[1:])

    def late_weights(after):
        _, landed = _split_wait("gather_late_wait", _gather_copies, g_send, g_recv, late_thru, late_lands, after)
        g_out, g_f1, g_f2 = _gather_finish(landed)
        return g_out.reshape(D_MODEL, D_MODEL), g_f1, g_f2

    pos = jnp.stack([ci, chip]).astype(jnp.int32)
    ffn = {}


    def on_ffn_grads(dw1, dw2):
        parts = [dw1, dw2]
        lands = [lax.empty((p.shape[0], p.shape[1] // 2, p.shape[2]), p.dtype) for p in parts]
        ffn['sib'] = _split_start("sibling_ffn_start", _sibling_copies, parts, lands, len(parts))
        return ffn['sib'][4]

    def on_outproj_bwd(dattn):
        send, recv, thru, lands, _ = ffn['sib']
        parts, from_sibling = _split_wait("sibling_ffn_wait", _sibling_copies, send, recv, thru, lands, [dattn])
        cparts_ffn, lands_ffn = zip(*[_chip_presum(p, r, pos) for p, r in zip(parts, from_sibling)])
        ffn['send'], ffn['recv'], ffn['thru'], ffn['lands'], token3 = _split_start(
            "scatter_ffn_start", _scatter_copies, cparts_ffn, lands_ffn)
        return token3

    def on_mixer_grads(dw_in, dw_out):
        parts = [dw_in.reshape(N_CHIPS, IN_WIDTH // N_CHIPS, D_MODEL),
                 dw_out.reshape(N_CHIPS, D_MODEL // N_CHIPS, D_MODEL)]
        cparts, lands = zip(*[_chip_presum(p, r, pos) for p, r in zip(parts, _sibling_exchange(parts, pos))])
        ffn['mixer'] = _split_start("scatter_mixer_start", _scatter_copies, cparts, lands)
        return ffn['mixer'][4]

    head = jnp.concatenate([jnp.zeros((PAD_ROWS, D_MODEL), F32), meta_full], axis=0)
    loss, dx, dhead, grads = _local_step(head, x[0], loss_target[0], g_pre_mix, w_in_full, conv_w_full, conv_b, w_a[0],
                                         b_a, w_x[0], b_x, lru_lambda, attn_sinks, g_post_mix, g_pre_ffn, g_post_ffn,
                                         late_weights, on_ffn_grads, on_outproj_bwd, on_mixer_grads, token)
    grad_x = dx[None]

    pack = _pack_small(dhead[PAD_ROWS:], grads, loss)
    dev = jnp.reshape(4 * xi + 2 * yi + ci, (1,)).astype(jnp.int32)
    s_send, s_recv, s_thru, s_lands, token5 = _split_start(
        "gather_small_start", _all_peers_copies, [pack], [_prep_tiny(pack, dev, N_DEV)], N_DEV - 1)

    send, recv, thru, lands, _ = ffn['mixer']
    mixer_cparts, mixer_lands = _split_wait("scatter_mixer_wait", _scatter_copies, send, recv, thru, lands, [token5])
    ffn_cparts, ffn_lands = _split_wait("scatter_ffn_wait", _scatter_copies, ffn['send'], ffn['recv'], ffn['thru'],
                                        ffn['lands'], mixer_lands)
    chip_partials = _scatter_partials([], [], mixer_cparts + ffn_cparts, mixer_lands + ffn_lands)

    g_out_d, delta, new_m, new_v = {}, {}, {}, {}
    for name, part in zip(_BIG, chip_partials):
        shp = weights[name].shape
        res = _adamw_big(part, big2d(weights[name], name), big2d(mom1[name], name), big2d(mom2[name], name))
        g_out_d[name], delta[name], new_m[name], new_v[name] = (big2d(r[None], name).reshape(shp) for r in res)

    _, (gathered,) = _split_wait("gather_small_wait", _all_peers_copies, s_send, s_recv, s_thru, s_lands,
                                 [g_out_d[n] for n in _BIG])
    small = _unpack_small(_sum_devices(gathered.reshape(N_DEV * SMALL_PACK_ROWS, D_MODEL), SMALL_PACK_ROWS), chip)
    loss = small['loss']
    small_names = [n for n in _WEIGHTS if n not in _BIG]
    quads = [(_as2d(weights[n]), _as2d(small[n]), _as2d(mom1[n]), _as2d(mom2[n])) for n in small_names]
    for name, (d, m2, v2) in zip(small_names, _adamw_small(quads)):
        shp = weights[name].shape
        g_out_d[name] = small[name].reshape(shp)
        delta[name], new_m[name], new_v[name] = d.reshape(shp), m2.reshape(shp), v2.reshape(shp)

    return (loss, grad_x, *[g_out_d[n] for n in _WEIGHTS], *[delta[n] for n in _WEIGHTS],
            *[new_m[n] for n in _WEIGHTS], *[new_v[n] for n in _WEIGHTS])
```

```python
import numpy as np
import jax
import jax.numpy as jnp
from jax import lax
from jax.experimental import pallas as pl
from jax.experimental.pallas import tpu as pltpu

F32 = jnp.float32
BF16 = jnp.bfloat16

D_MODEL = 1024
N_META = 16
BLOCK = 128
PAD_ROWS = BLOCK - N_META
HEAD_DIM = 64
ATTN_HEADS = 8
GQA_GROUP = 4
ATTN_WIDTH = 512
KV_WIDTH = 128
QKV_WIDTH = ATTN_WIDTH + 2 * KV_WIDTH
LRU_WIDTH = 512
LRU_BLOCKS = 8
LRU_BLOCK = 64
LRU_C = 8.0
IN_WIDTH = 1792
D_FF = 4096
N_CHIPS = 4
FF_CHUNK = D_FF // N_CHIPS
EPS = 1e-6
NEG = -1e30

ADAM_LR = 0.001
ADAM_B1 = 0.9
ADAM_B2 = 0.999
ADAM_EPS = 1e-08
ADAM_WD = 0.01
ADAM_STEP = 10

VMEM_LIMIT_V7X = 62 * 1024 * 1024
MESH = pl.DeviceIdType.MESH

NT = (((1,), (1,)), ((), ()))
TN = (((0,), (0,)), ((), ()))


def _row_tile(tp):
    return 640 if tp % 640 == 0 else BLOCK


def _wgrad_row_tile(tp):
    return 1664 if tp % 1664 == 0 else _row_tile(tp)


def _params(*sem):
    return pltpu.CompilerParams(dimension_semantics=sem, vmem_limit_bytes=VMEM_LIMIT_V7X)


def _dot(a, b):
    return jnp.dot(a, b, preferred_element_type=F32)


def _dot_nt(a, b):
    return lax.dot_general(a, b, NT, preferred_element_type=F32)


def _dot_tn(a, b):
    return lax.dot_general(a, b, TN, preferred_element_type=F32)


def _rms(x):
    rs = lax.rsqrt(jnp.mean(x * x, axis=-1, keepdims=True) + EPS)
    return x * rs, rs


def _rms_bwd(xhat, rs, g, dy):
    dyg = dy * g
    dx = rs * (dyg - xhat * jnp.mean(dyg * xhat, axis=-1, keepdims=True))
    dg = jnp.sum(dy * xhat, axis=0, keepdims=True)
    return dx, dg


def _gelu(x):
    k = 0.7978845608028654
    t = jnp.tanh(x * (k + (k * 0.044715) * (x * x)))
    return (0.5 * x) * (1.0 + t), t


def _gelu_grad(x, t):
    k = 0.7978845608028654
    return 0.5 * (1.0 + t) + 0.5 * x * (1.0 - t * t) * k * (1.0 + 3 * 0.044715 * x * x)


def _sigmoid(x):
    return 0.5 * jnp.tanh(0.5 * x) + 0.5


def _one_minus_exp2(y):
    t = jnp.tanh(y)
    return (-2.0 * t) / (1.0 - t)


def _softplus(x):
    return jnp.maximum(x, 0.0) + jnp.log1p(jnp.exp(-jnp.abs(x)))


def _seq_specs(tr, delay=0):
    qb = tr // BLOCK
    tile = lambda i: jnp.maximum(i - delay, 0)
    return [pl.BlockSpec((BLOCK, D_MODEL), lambda i, *_, s=s: (jnp.maximum(tile(i) * qb + s - 1, 0), 0))
            for s in range(qb)]


def _seq_tile(head, pieces, i):
    first = jnp.where(i == 0, head, pieces[0][...])
    return jnp.concatenate([first] + [p[...] for p in pieces[1:]], axis=0)


GROUP_ROWS = GQA_GROUP * BLOCK


def _attn_bias():
    j = np.arange(2 * BLOCK)[:, None]
    i = np.arange(BLOCK)[None, :]
    band = (j - i >= 1) & (j - i <= BLOCK)
    out = []
    for n in range(3):
        ok = band & ((n - 1) * BLOCK + j >= PAD_ROWS) if n < 2 else band
        out.append(np.tile(np.where(ok, 0.0, NEG).astype(np.float32), (1, GQA_GROUP)))
    return jnp.asarray(np.stack(out))


def _heads_t(at, g):
    heads = range(GQA_GROUP * g, GQA_GROUP * (g + 1))
    return jnp.concatenate([at[h * HEAD_DIM:(h + 1) * HEAD_DIM] for h in heads], axis=1).astype(BF16)


def _from_heads_t(groups):
    pairs = []
    for p in groups:
        for h in range(0, GQA_GROUP, 2):
            two = jnp.concatenate([p[:, h * BLOCK:(h + 1) * BLOCK], p[:, (h + 1) * BLOCK:(h + 2) * BLOCK]], axis=0)
            pairs.append(two.T)
    return jnp.concatenate(pairs, axis=1)


def _stack_heads(a, g):
    heads = range(GQA_GROUP * g, GQA_GROUP * (g + 1))
    return jnp.concatenate([a[:, h * HEAD_DIM:(h + 1) * HEAD_DIM] for h in heads], axis=0)


def _unstack_heads(groups):
    return jnp.concatenate([p[h * BLOCK:(h + 1) * BLOCK] for p in groups for h in range(GQA_GROUP)], axis=1)


def _attn_probs_t(k_g, qg, bias, sink_row):
    st = _dot_nt(k_g, qg) + bias
    m = jnp.maximum(jnp.max(st, axis=0, keepdims=True), sink_row)
    p = jnp.exp(st - m)
    es = jnp.exp(sink_row - m)
    inv = 1.0 / (jnp.sum(p, axis=0, keepdims=True) + es)
    return p * inv, es * inv


def _attn_consts(sinks):
    return jnp.repeat(sinks.reshape(ATTN_HEADS), BLOCK).reshape(ATTN_HEADS // GQA_GROUP, GROUP_ROWS), _attn_bias()


_SINK_SPEC = pl.BlockSpec((ATTN_HEADS // GQA_GROUP, GROUP_ROWS), lambda n: (0, 0))
_BIAS_SPEC = pl.BlockSpec((3, 2 * BLOCK, GROUP_ROWS), lambda n: (0, 0, 0))
_QSCALE = HEAD_DIM ** -0.5


def _kv_specs(tr):
    qb = tr // BLOCK
    prev = lambda col: pl.BlockSpec((BLOCK, KV_WIDTH), lambda t: (jnp.maximum(t * qb - 1, 0), col))
    cur = lambda col: pl.BlockSpec((tr, KV_WIDTH), lambda t: (t, col))
    return [prev(4), cur(4), prev(5), cur(5)]


def _block_bias(b_ref, t, qb, i):
    return b_ref[2] if i >= 2 else b_ref[jnp.minimum(t * qb + i, 2)]


N_KV = ATTN_HEADS // GQA_GROUP


def _prob_specs(qb):
    return [pl.BlockSpec((qb, N_KV, 2 * BLOCK, GROUP_ROWS), lambda t: (t, 0, 0, 0)),
            pl.BlockSpec((qb, SUBLANES, GROUP_ROWS), lambda t: (t, 0, 0))]


def _attn_fwd(qkv, sinks):
    tp = qkv.shape[0]
    tr = _row_tile(tp)
    qb, nb = tr // BLOCK, tp // BLOCK
    sink_rows, bias = _attn_consts(sinks)

    def body(s_ref, b_ref, q_ref, kp_ref, kc_ref, vp_ref, vc_ref, o_ref, p_ref, ps_ref):
        t = pl.program_id(0)
        k_all = jnp.concatenate([kp_ref[...], kc_ref[...]], axis=0)
        v_all = jnp.concatenate([vp_ref[...], vc_ref[...]], axis=0)
        for i in range(qb):
            rows = slice(i * BLOCK, (i + 1) * BLOCK)
            q = q_ref[rows]
            k2, v2 = k_all[i * BLOCK:(i + 2) * BLOCK], v_all[i * BLOCK:(i + 2) * BLOCK]
            bias_n = _block_bias(b_ref, t, qb, i)
            outs, sink_probs = [], []
            for g in range(N_KV):
                cols = slice(g * HEAD_DIM, (g + 1) * HEAD_DIM)
                qg = _stack_heads(q, g) * jnp.asarray(_QSCALE, BF16)
                p, ps = _attn_probs_t(k2[:, cols], qg, bias_n, s_ref[g:g + 1])
                pb = p.astype(BF16)
                p_ref[i, g] = pb
                sink_probs.append(ps)
                outs.append(_dot_tn(pb, v2[:, cols]))
            o_ref[rows] = _unstack_heads(outs).astype(BF16)
            ps_ref[i] = jnp.concatenate(sink_probs + [jnp.zeros((SUBLANES - N_KV, GROUP_ROWS), F32)], axis=0)

    return pl.pallas_call(
        body, name="attn_fwd", grid=(tp // tr,),
        in_specs=[_SINK_SPEC, _BIAS_SPEC, pl.BlockSpec((tr, ATTN_WIDTH), lambda t: (t, 0))] + _kv_specs(tr),
        out_specs=[pl.BlockSpec((tr, ATTN_WIDTH), lambda t: (t, 0))] + _prob_specs(qb),
        out_shape=[jax.ShapeDtypeStruct((tp, ATTN_WIDTH), BF16),
                   jax.ShapeDtypeStruct((nb, N_KV, 2 * BLOCK, GROUP_ROWS), BF16),
                   jax.ShapeDtypeStruct((nb, SUBLANES, GROUP_ROWS), F32)],
        compiler_params=_params("parallel"),
    )(sink_rows, bias, qkv, qkv, qkv, qkv, qkv)


def _conv_taps(x, halo):
    ext = jnp.concatenate([halo, x], axis=0)
    return [ext[8:] if k == 3 else pltpu.roll(ext, 3 - k, 0)[8:] for k in range(4)]


def _lru_gates(xc, wa, ba, wx, bx, sp):
    xb = xc.astype(BF16)
    r = _sigmoid(_dot(xb, wa) + ba)
    ig = _sigmoid(_dot(xb, wx) + bx)
    log_a = (-LRU_C * sp) * r
    a = jnp.exp(log_a)
    mult = jnp.sqrt(_one_minus_exp2(log_a))
    return xb, r, ig, a, mult


SUBLANES = 8


def _scan_fwd(a, b, h_in):
    n, width = a.shape
    a, b = (v.reshape(n // SUBLANES, SUBLANES, width) for v in (a, b))
    in_group = lax.broadcasted_iota(jnp.int32, a.shape, 1)
    for d in (1, 2, 4):
        keep = in_group >= d
        b = jnp.where(keep, a * pltpu.roll(b, d, 1) + b, b)
        a = jnp.where(keep, a * pltpu.roll(a, d, 1), a)
    a, b = a.reshape(n, width), b.reshape(n, width)
    out, carry = [], h_in
    for g in range(0, n, SUBLANES):
        h = a[g:g + SUBLANES] * carry + b[g:g + SUBLANES]
        out.append(h)
        carry = h[SUBLANES - 1:]
    return jnp.concatenate(out, axis=0)


def _scan_rev(c, b, g_in):
    n, width = c.shape
    c, b = (v.reshape(n // SUBLANES, SUBLANES, width) for v in (c, b))
    in_group = lax.broadcasted_iota(jnp.int32, c.shape, 1)
    for d in (1, 2, 4):
        keep = in_group < SUBLANES - d
        b = jnp.where(keep, b + c * pltpu.roll(b, SUBLANES - d, 1), b)
        c = jnp.where(keep, c * pltpu.roll(c, SUBLANES - d, 1), c)
    c, b = c.reshape(n, width), b.reshape(n, width)
    out, carry = [], g_in
    for g in range(n - SUBLANES, -1, -SUBLANES):
        r = b[g:g + SUBLANES] + c[g:g + SUBLANES] * carry
        out.append(r)
        carry = r[:1]
    return jnp.concatenate(out[::-1], axis=0)


def _inproj_lru_fwd(head, x, g, w_in, conv_w, conv_b, wa, ba, wx, bx, lam, token):
    tp = BLOCK + x.shape[0]
    tr = _row_tile(tp)
    qb, nt = tr // BLOCK, tp // tr
    small = [conv_w, conv_b, wa, ba, wx, bx, lam]

    def body(*refs):
        head_ref, pieces = refs[0], refs[1:1 + qb]
        g_ref, w_ref, _, cw_ref, cb_ref, wa_ref, ba_ref, wx_ref, bx_ref, lam_ref = refs[1 + qb:11 + qb]
        u_ref, qkv_ref, xr_ref, yr_ref, hr_ref, rec_ref, zbuf, halo, hprev = refs[11 + qb:]
        i = pl.program_id(0)
        cur = i % 2

        @pl.when(i == 0)
        def _():
            halo[...] = jnp.zeros_like(halo)
            hprev[...] = jnp.zeros_like(hprev)
            zbuf[1] = jnp.zeros((tr, 2 * LRU_WIDTH), F32)

        def recurrent_branch(valid):
            cw, cb = cw_ref[...], cb_ref[...]
            wa_m, ba_v, wx_m, bx_v = wa_ref[...], ba_ref[...], wx_ref[...], bx_ref[...]
            sp = _softplus(-lam_ref[...])
            before, h_last = halo[...], hprev[0:1]
            for b in range(qb):
                rows = slice(b * BLOCK, (b + 1) * BLOCK)
                xy = zbuf[1 - cur, rows]
                xin = xy[:, :LRU_WIDTH]
                taps = _conv_taps(xin, before)
                before = xin[BLOCK - 8:]
                xc = cb + sum(cw[k:k + 1] * taps[k] for k in range(4))
                _, _, ig, a, mult = _lru_gates(xc, wa_m, ba_v, wx_m, bx_v, sp)
                u = mult * (ig * xc)
                if b == 0:
                    pos = (i - 1) * tr + lax.broadcasted_iota(jnp.int32, xc.shape, 0)
                    u = jnp.where(pos >= PAD_ROWS, u, 0.0)
                h = _scan_fwd(a, u, h_last)
                h_last = h[BLOCK - 1:]
                hr_ref[rows] = h
                gl, _ = _gelu(xy[:, LRU_WIDTH:])
                rec_ref[rows] = (gl * h).astype(BF16)
            halo[...] = jnp.where(valid, before, 0.0)
            hprev[0:1] = jnp.where(valid, h_last, 0.0)

        def projection():
            xhat, _ = _rms(_seq_tile(head_ref[...], pieces, i))
            u = (xhat * g_ref[...]).astype(BF16)
            u_ref[...] = u
            z = _dot_nt(u, w_ref[...])
            qkv_ref[...] = z[:, :QKV_WIDTH].astype(BF16)
            xr_ref[...] = z[:, QKV_WIDTH:QKV_WIDTH + LRU_WIDTH]
            yr_ref[...] = z[:, QKV_WIDTH + LRU_WIDTH:]
            zbuf[cur] = z[:, QKV_WIDTH:]

        @pl.when(i < nt)
        def _():
            recurrent_branch(i >= 1)
            projection()

        @pl.when(i == nt)
        def _():
            recurrent_branch(True)

    last = nt - 1
    this_row = lambda w: pl.BlockSpec((tr, w), lambda i: (jnp.minimum(i, last), 0))
    prev_row = lambda w: pl.BlockSpec((tr, w), lambda i: (jnp.maximum(i - 1, 0), 0))
    full = lambda a: pl.BlockSpec(a.shape, lambda i: (0,) * a.ndim)
    piece_specs = [pl.BlockSpec((BLOCK, D_MODEL), lambda i, s=s: (jnp.maximum(jnp.minimum(i, last) * qb + s - 1, 0), 0))
                   for s in range(qb)]
    return pl.pallas_call(
        body, name="inproj_lru_fwd", grid=(nt + 1,),
        in_specs=[full(head)] + piece_specs + [full(g), full(w_in), full(token)] + [full(a) for a in small],
        out_specs=[this_row(D_MODEL), this_row(QKV_WIDTH), this_row(LRU_WIDTH), this_row(LRU_WIDTH),
                   prev_row(LRU_WIDTH), prev_row(LRU_WIDTH)],
        out_shape=[jax.ShapeDtypeStruct((tp, D_MODEL), BF16), jax.ShapeDtypeStruct((tp, QKV_WIDTH), BF16),
                   jax.ShapeDtypeStruct((tp, LRU_WIDTH), F32), jax.ShapeDtypeStruct((tp, LRU_WIDTH), F32),
                   jax.ShapeDtypeStruct((tp, LRU_WIDTH), F32), jax.ShapeDtypeStruct((tp, LRU_WIDTH), BF16)],
        scratch_shapes=[pltpu.VMEM((2, tr, 2 * LRU_WIDTH), F32), pltpu.VMEM((8, LRU_WIDTH), F32),
                        pltpu.VMEM((8, LRU_WIDTH), F32)],
        compiler_params=_params("arbitrary"),
    )(head, *([x] * qb), g, w_in, token, *small)


def _outproj_fwd(attn, rec, w_out, head, x, g_post_mix, g_pre_ffn):
    tp = attn.shape[0]
    tr = _row_tile(tp)
    qb = tr // BLOCK

    def body(*refs):
        a_ref, r_ref, w_ref, head_ref = refs[:4]
        pieces = refs[4:4 + qb]
        gm_ref, gf_ref, mix_ref, h1_ref, u1_ref = refs[4 + qb:]
        mix = _dot(a_ref[...], w_ref[:ATTN_WIDTH]) + _dot(r_ref[...], w_ref[ATTN_WIDTH:])
        mix_ref[...] = mix
        mhat, _ = _rms(mix)
        h1 = _seq_tile(head_ref[...], pieces, pl.program_id(0)) + mhat * gm_ref[...]
        h1_ref[...] = h1
        hhat, _ = _rms(h1)
        u1_ref[...] = (hhat * gf_ref[...]).astype(BF16)

    row = lambda w: pl.BlockSpec((tr, w), lambda i: (i, 0))
    full = lambda a: pl.BlockSpec(a.shape, lambda i: (0,) * a.ndim)
    return pl.pallas_call(
        body, name="outproj_fwd", grid=(tp // tr,),
        in_specs=[row(ATTN_WIDTH), row(LRU_WIDTH), full(w_out), full(head)] + _seq_specs(tr)
        + [full(g_post_mix), full(g_pre_ffn)],
        out_specs=[row(D_MODEL), row(D_MODEL), row(D_MODEL)],
        out_shape=[jax.ShapeDtypeStruct((tp, D_MODEL), F32), jax.ShapeDtypeStruct((tp, D_MODEL), F32),
                   jax.ShapeDtypeStruct((tp, D_MODEL), BF16)],
        compiler_params=_params("parallel"),
    )(attn, rec, w_out, head, *([x] * qb), g_post_mix, g_pre_ffn)


FFN_STEPS = 4
CHUNKS_PER_STEP = N_CHIPS // FFN_STEPS
FFN_COLS = CHUNKS_PER_STEP * FF_CHUNK


def _resident(a):
    return pl.BlockSpec(a.shape, lambda *_: (0,) * a.ndim, pipeline_mode=pl.Buffered(1))


def _ffn_fwd(u1, w1, w2, h1, tgt, g_post_ffn):
    tp = h1.shape[0]
    tr = _row_tile(tp)
    qb, nt = tr // BLOCK, tp // tr
    sr = tr // FFN_STEPS

    def body(*refs):
        u_ref, w1_ref, w2_ref, h1_ref = refs[:4]
        t_pieces = refs[4:4 + qb]
        g_ref, r1_ref, dy_ref, df2_ref, loss_ref, dg_ref, acc = refs[4 + qb:]
        i, c = pl.program_id(0), pl.program_id(1)
        cur = i % 2

        @pl.when((i == 0) & (c == 0))
        def _():
            loss_ref[...] = jnp.zeros_like(loss_ref)
            dg_ref[...] = jnp.zeros_like(dg_ref)
            acc[1] = jnp.zeros((tr, D_MODEL), F32)

        def matmuls():
            u, part = u_ref[...], None
            for j in range(CHUNKS_PER_STEP):
                chunk = c * CHUNKS_PER_STEP + j
                r = jnp.maximum(_dot(u, w1_ref[chunk]), 0.0)
                r1_ref[:, j * FF_CHUNK:(j + 1) * FF_CHUNK] = r.astype(BF16)
                p = _dot((r * r).astype(BF16), w2_ref[chunk])
                part = p if part is None else part + p
            return part

        def finish_previous_tile(k, valid):
            lo, hi = k * sr, (k + 1) * sr
            g = g_ref[...]
            fhat, rs = _rms(acc[1 - cur, lo:hi])
            h2 = h1_ref[...] + fhat * g
            rows = (i - 1) * tr + lo + lax.broadcasted_iota(jnp.int32, h2.shape, 0)
            tgt = jnp.concatenate([p[max(lo - s * BLOCK, 0):min(hi - s * BLOCK, BLOCK)] for s, p in enumerate(t_pieces)
                                   if lo < (s + 1) * BLOCK and hi > s * BLOCK], axis=0)
            err = jnp.where((rows >= BLOCK) & valid, h2 - tgt, 0.0)
            dy = err * (1.0 / D_MODEL)
            dy_ref[...] = dy
            loss_ref[...] += (0.5 / D_MODEL) * jnp.sum(err * err)
            df2, dg = _rms_bwd(fhat, rs, g, dy)
            df2_ref[...] = df2.astype(BF16)
            dg_ref[...] += dg

        for k in range(FFN_STEPS):
            @pl.when((c == k) & (i < nt))
            def _(k=k):
                finish_previous_tile(k, i >= 1)
                if k == 0:
                    acc[cur] = matmuls()
                else:
                    acc[cur] += matmuls()

            @pl.when((c == k) & (i == nt))
            def _(k=k):
                finish_previous_tile(k, True)

    last = nt - 1
    this_row = pl.BlockSpec((tr, D_MODEL), lambda i, c: (jnp.minimum(i, last), 0))
    prev_quarter = pl.BlockSpec((sr, D_MODEL), lambda i, c: (jnp.maximum(i - 1, 0) * FFN_STEPS + c, 0))
    prev_quarter_out = pl.BlockSpec(
        (sr, D_MODEL), lambda i, c: (jnp.where(i == 0, nt * FFN_STEPS, (i - 1) * FFN_STEPS + c), 0))
    full = lambda a: pl.BlockSpec(a.shape, lambda i, c: (0,) * a.ndim)
    return pl.pallas_call(
        body, name="ffn_fwd", grid=(nt + 1, FFN_STEPS),
        in_specs=[this_row, _resident(w1), _resident(w2), prev_quarter] + _seq_specs(tr, delay=1) + [full(g_post_ffn)],
        out_specs=[pl.BlockSpec((tr, FFN_COLS), lambda i, c: (jnp.minimum(i, last), jnp.where(i < nt, c, FFN_STEPS - 1))),
                   prev_quarter_out, prev_quarter_out,
                   pl.BlockSpec((1, 1), lambda i, c: (0, 0)), pl.BlockSpec((1, D_MODEL), lambda i, c: (0, 0))],
        out_shape=[jax.ShapeDtypeStruct((tp, D_FF), BF16), jax.ShapeDtypeStruct((tp + sr, D_MODEL), F32),
                   jax.ShapeDtypeStruct((tp + sr, D_MODEL), BF16), jax.ShapeDtypeStruct((1, 1), F32),
                   jax.ShapeDtypeStruct((1, D_MODEL), F32)],
        scratch_shapes=[pltpu.VMEM((2, tr, D_MODEL), F32)],
        compiler_params=_params("arbitrary", "arbitrary"),
    )(u1, w1, w2, h1, *([tgt] * qb), g_post_ffn)


def _ffn_bwd_data(df2, r1, w1, w2, dy, h1, mix, g_pre_ffn, g_post_mix):
    tp = h1.shape[0]
    tr = _row_tile(tp)
    nt = tp // tr
    sr = tr // FFN_STEPS

    def body(df2_ref, r1_ref, w1_ref, w2_ref, dy_ref, h1_ref, mix_ref, gf_ref, gm_ref,
             da_ref, dh1_ref, dmix_ref, dgf_ref, dgm_ref, acc):
        i, c = pl.program_id(0), pl.program_id(1)
        cur = i % 2

        @pl.when((i == 0) & (c == 0))
        def _():
            dgf_ref[...] = jnp.zeros_like(dgf_ref)
            dgm_ref[...] = jnp.zeros_like(dgm_ref)
            acc[1] = jnp.zeros((tr, D_MODEL), F32)

        def matmuls():
            df2, part = df2_ref[...], None
            for j in range(CHUNKS_PER_STEP):
                chunk, cols = c * CHUNKS_PER_STEP + j, slice(j * FF_CHUNK, (j + 1) * FF_CHUNK)
                df = _dot_nt(df2, w2_ref[chunk])
                da = (df * (2.0 * r1_ref[:, cols].astype(F32))).astype(BF16)
                da_ref[:, cols] = da
                p = _dot_nt(da, w1_ref[chunk])
                part = p if part is None else part + p
            return part

        def finish_previous_tile(k, valid):
            lo, hi = k * sr, (k + 1) * sr
            hhat, rs = _rms(h1_ref[...])
            dx, dgf = _rms_bwd(hhat, rs, gf_ref[...], acc[1 - cur, lo:hi])
            dh1 = dy_ref[...] + dx
            dh1_ref[...] = dh1
            mhat, rsm = _rms(mix_ref[...])
            dmix, dgm = _rms_bwd(mhat, rsm, gm_ref[...], dh1)
            dmix_ref[...] = dmix.astype(BF16)
            dgf_ref[...] += jnp.where(valid, dgf, 0.0)
            dgm_ref[...] += jnp.where(valid, dgm, 0.0)

        for k in range(FFN_STEPS):
            @pl.when((c == k) & (i < nt))
            def _(k=k):
                finish_previous_tile(k, i >= 1)
                if k == 0:
                    acc[cur] = matmuls()
                else:
                    acc[cur] += matmuls()

            @pl.when((c == k) & (i == nt))
            def _(k=k):
                finish_previous_tile(k, True)

    last = nt - 1
    this_row = pl.BlockSpec((tr, D_MODEL), lambda i, c: (jnp.minimum(i, last), 0))
    prev_quarter = pl.BlockSpec((sr, D_MODEL), lambda i, c: (jnp.maximum(i - 1, 0) * FFN_STEPS + c, 0))
    prev_quarter_out = pl.BlockSpec(
        (sr, D_MODEL), lambda i, c: (jnp.where(i == 0, nt * FFN_STEPS, (i - 1) * FFN_STEPS + c), 0))
    chunk = pl.BlockSpec((tr, FFN_COLS), lambda i, c: (jnp.minimum(i, last), jnp.where(i < nt, c, FFN_STEPS - 1)))
    gain = pl.BlockSpec((1, D_MODEL), lambda i, c: (0, 0))
    return pl.pallas_call(
        body, name="ffn_bwd_data", grid=(nt + 1, FFN_STEPS),
        in_specs=[this_row, chunk, _resident(w1), _resident(w2), prev_quarter, prev_quarter, prev_quarter, gain, gain],
        out_specs=[chunk, prev_quarter_out, prev_quarter_out, gain, gain],
        out_shape=[jax.ShapeDtypeStruct((tp, D_FF), BF16), jax.ShapeDtypeStruct((tp + sr, D_MODEL), F32),
                   jax.ShapeDtypeStruct((tp + sr, D_MODEL), BF16), jax.ShapeDtypeStruct((1, D_MODEL), F32),
                   jax.ShapeDtypeStruct((1, D_MODEL), F32)],
        scratch_shapes=[pltpu.VMEM((2, tr, D_MODEL), F32)],
        compiler_params=_params("arbitrary", "arbitrary"),
    )(df2, r1, w1, w2, dy, h1, mix, g_pre_ffn, g_post_mix)


def _ffn_bwd_weights(u1, da1, r1, df2):
    tp = u1.shape[0]
    tr = _wgrad_row_tile(tp)

    def body(u_ref, da_ref, r1_ref, df2_ref, dw1_ref, dw2_ref):
        i = pl.program_id(1)
        r = r1_ref[...].astype(F32)
        p1 = _dot_tn(u_ref[...], da_ref[...])
        p2 = _dot_tn((r * r).astype(BF16), df2_ref[...])

        @pl.when(i == 0)
        def _():
            dw1_ref[0] = p1
            dw2_ref[0] = p2

        @pl.when(i > 0)
        def _():
            dw1_ref[0] += p1
            dw2_ref[0] += p2

    row = pl.BlockSpec((tr, D_MODEL), lambda c, i: (i, 0))
    chunk = pl.BlockSpec((tr, FF_CHUNK), lambda c, i: (i, c))
    return pl.pallas_call(
        body, name="ffn_bwd_weights", grid=(N_CHIPS, tp // tr),
        in_specs=[row, chunk, chunk, row],
        out_specs=[pl.BlockSpec((1, D_MODEL, FF_CHUNK), lambda c, i: (c, 0, 0)),
                   pl.BlockSpec((1, FF_CHUNK, D_MODEL), lambda c, i: (c, 0, 0))],
        out_shape=[jax.ShapeDtypeStruct((N_CHIPS, D_MODEL, FF_CHUNK), F32),
                   jax.ShapeDtypeStruct((N_CHIPS, FF_CHUNK, D_MODEL), F32)],
        compiler_params=_params("parallel", "arbitrary"),
    )(u1, da1, r1, df2)


N_VEC_ROWS = 8


def _outproj_lru_bwd(dmix, w_out, attn, rec, xr, yr, hr, conv_w, conv_b, wa, ba, wx, bx, lam, token):
    tp = xr.shape[0]
    tr = _row_tile(tp)
    qb, nt = tr // BLOCK, tp // tr

    def body(dm_ref, w_ref, at_ref, rc_ref, xr_ref, xh_ref, yr_ref, hr_ref, hp_ref,
             cw_ref, cb_ref, wa_ref, ba_ref, wx_ref, bx_ref, lam_ref, _,
             dxr_ref, dyr_ref, dat_ref, dwo_ref, dwa_ref, dwx_ref, vec_ref, g_next, a_next, dxc_next, dsp):
        s = pl.program_id(0)
        t = nt - 1 - s

        @pl.when(s == 0)
        def _():
            g_next[...] = jnp.zeros_like(g_next)
            a_next[...] = jnp.zeros_like(a_next)
            dxc_next[...] = jnp.zeros_like(dxc_next)
            dsp[...] = jnp.zeros_like(dsp)
            dwo_ref[...] = jnp.zeros_like(dwo_ref)
            dwa_ref[...] = jnp.zeros_like(dwa_ref)
            dwx_ref[...] = jnp.zeros_like(dwx_ref)
            vec_ref[...] = jnp.zeros_like(vec_ref)

        dm = dm_ref[...]
        dcat = _dot_nt(dm, w_ref[...])
        dat_ref[...] = dcat[:, :ATTN_WIDTH].astype(BF16)
        drec_tile = dcat[:, ATTN_WIDTH:]
        dwo_ref[:ATTN_WIDTH] += _dot_tn(at_ref[...], dm)
        dwo_ref[ATTN_WIDTH:] += _dot_tn(rc_ref[...], dm)

        first_tile = t == 0
        cw, cb = cw_ref[...], cb_ref[...]
        lam_v = lam_ref[...]
        sp = _softplus(-lam_v)
        wa_m, ba_v, wx_m, bx_v = wa_ref[...], ba_ref[...], wx_ref[...], bx_ref[...]
        rows = lax.broadcasted_iota(jnp.int32, (BLOCK, LRU_WIDTH), 0)
        col = lambda v: jnp.sum(v, axis=0, keepdims=True)

        g_after, a_after, dxc_after = g_next[0:1], a_next[0:1], dxc_next[...]
        xbs, dgrs, dgis = [], [], []
        vec = [jnp.zeros((1, LRU_WIDTH), F32) for _ in range(N_VEC_ROWS)]
        for i in reversed(range(qb)):
            blk = slice(i * BLOCK, (i + 1) * BLOCK)
            if i == 0:
                x_before = jnp.where(first_tile, 0.0, xh_ref[...])
                h_before = jnp.where(first_tile, 0.0, hp_ref[7:8])
            else:
                x_before = xr_ref[i * BLOCK - 8:i * BLOCK]
                h_before = hr_ref[i * BLOCK - 1:i * BLOCK]
            taps = _conv_taps(xr_ref[blk], x_before)
            xc = cb + sum(cw[k:k + 1] * taps[k] for k in range(4))
            xb, r, ig, a, mult = _lru_gates(xc, wa_m, ba_v, wx_m, bx_v, sp)

            yr_v = yr_ref[blk]
            gl, th = _gelu(yr_v)
            h = hr_ref[blk]
            drec = drec_tile[blk]
            dyr_ref[blk] = (drec * h * _gelu_grad(yr_v, th)).astype(BF16)

            a_up = jnp.where(rows == BLOCK - 1, a_after, pltpu.roll(a, BLOCK - 1, 0))
            g = _scan_rev(a_up, drec * gl, g_after)
            g_after, a_after = g[0:1], a[0:1]

            h_prev = jnp.where(rows == 0, h_before, pltpu.roll(h, 1, 0))
            du, da = g, g * h_prev
            if i == 0:
                real = (t * tr + rows) >= PAD_ROWS
                du, da = jnp.where(real, du, 0.0), jnp.where(real, da, 0.0)
            dmult = du * (ig * xc)
            dig = du * (mult * xc)
            dxc = du * (mult * ig)
            dlog_a = da * a - dmult * (a * a / mult)
            if i == 0:
                dlog_a = jnp.where(real, dlog_a, 0.0)
            dgr = (dlog_a * (-LRU_C * sp)) * (r * (1.0 - r))
            dgi = dig * (ig * (1.0 - ig))
            dgr_b, dgi_b = dgr.astype(BF16), dgi.astype(BF16)
            dxc = dxc + _dot_nt(dgr_b, wa_m) + _dot_nt(dgi_b, wx_m)
            xbs.append(xb)
            dgrs.append(dgr_b)
            dgis.append(dgi_b)

            ext = jnp.concatenate([dxc, dxc_after], axis=0)
            up = [ext[:BLOCK] if j == 0 else pltpu.roll(ext, BLOCK + 8 - j, 0)[:BLOCK] for j in range(4)]
            dxr_ref[blk] = sum(cw[k:k + 1] * up[3 - k] for k in range(4)).astype(BF16)
            dxc_after = dxc[:8]

            for k in range(4):
                vec[k] = vec[k] + col(dxc * taps[k])
            vec[4] = vec[4] + col(dxc)
            vec[5] = vec[5] + col(dgr)
            vec[6] = vec[6] + col(dgi)
            vec[7] = vec[7] + col(dlog_a * (-LRU_C * r))

        g_next[0:1], a_next[0:1], dxc_next[...] = g_after, a_after, dxc_after
        xb_all = jnp.concatenate(xbs, axis=0)
        dwa_ref[...] += _dot_tn(xb_all, jnp.concatenate(dgrs, axis=0))
        dwx_ref[...] += _dot_tn(xb_all, jnp.concatenate(dgis, axis=0))
        for k in range(7):
            vec_ref[k:k + 1] += vec[k]
        dsp[0:1] += vec[7]

        @pl.when(s == nt - 1)
        def _():
            vec_ref[7:8] = dsp[0:1] * (-_sigmoid(-lam_v))

    blk_spec = pl.BlockSpec((tr, LRU_WIDTH), lambda s: (nt - 1 - s, 0))
    rows_before = pl.BlockSpec((8, LRU_WIDTH), lambda s: (jnp.maximum((nt - 1 - s) * (tr // 8) - 1, 0), 0))
    full = lambda a: pl.BlockSpec(a.shape, lambda s: (0,) * a.ndim)
    small = [conv_w, conv_b, wa, ba, wx, bx, lam, token]
    sq = pl.BlockSpec((LRU_WIDTH, LRU_WIDTH), lambda s: (0, 0))
    wide = pl.BlockSpec((tr, D_MODEL), lambda s: (nt - 1 - s, 0))
    whole = pl.BlockSpec((D_MODEL, D_MODEL), lambda s: (0, 0))
    return pl.pallas_call(
        body, name="outproj_lru_bwd", grid=(nt,),
        in_specs=[wide, whole, blk_spec, blk_spec, blk_spec, rows_before, blk_spec, blk_spec, rows_before]
        + [full(a) for a in small],
        out_specs=[blk_spec, blk_spec, blk_spec, whole, sq, sq, pl.BlockSpec((N_VEC_ROWS, LRU_WIDTH), lambda s: (0, 0))],
        out_shape=[jax.ShapeDtypeStruct((tp, LRU_WIDTH), BF16), jax.ShapeDtypeStruct((tp, LRU_WIDTH), BF16),
                   jax.ShapeDtypeStruct((tp, ATTN_WIDTH), BF16), jax.ShapeDtypeStruct((D_MODEL, D_MODEL), F32),
                   jax.ShapeDtypeStruct((LRU_WIDTH, LRU_WIDTH), F32), jax.ShapeDtypeStruct((LRU_WIDTH, LRU_WIDTH), F32),
                   jax.ShapeDtypeStruct((N_VEC_ROWS, LRU_WIDTH), F32)],
        scratch_shapes=[pltpu.VMEM((8, LRU_WIDTH), F32)] * 4,
        compiler_params=_params("arbitrary"),
    )(dmix, w_out, attn, rec, xr, xr, yr, hr, hr, *small)


def _attn_bwd(qkv, dattn, probs, sink_probs, token):
    tp = qkv.shape[0]
    tr = _row_tile(tp)
    qb, nt = tr // BLOCK, tp // tr
    n_groups = N_KV

    def body(p_ref, ps_ref, q_ref, kp_ref, kc_ref, vp_ref, vc_ref, do_ref, _, dq_ref, dkv_ref, ex_ref, ds_ref, dsink):
        t = pl.program_id(0)

        @pl.when(t == 0)
        def _():
            dsink[...] = jnp.zeros_like(dsink)

        k_all = jnp.concatenate([kp_ref[...], kc_ref[...]], axis=0)
        v_all = jnp.concatenate([vp_ref[...], vc_ref[...]], axis=0)
        tail = None
        for i in range(qb):
            rows = slice(i * BLOCK, (i + 1) * BLOCK)
            qt = (q_ref[rows].astype(F32) * _QSCALE).T
            dot = do_ref[rows].astype(F32).T
            k2, v2 = k_all[i * BLOCK:(i + 2) * BLOCK], v_all[i * BLOCK:(i + 2) * BLOCK]
            dqs, dks, dvs = [], [], []
            for g in range(n_groups):
                cols = slice(g * HEAD_DIM, (g + 1) * HEAD_DIM)
                k_g, v_g = k2[:, cols], v2[:, cols]
                qgt, dogt = _heads_t(qt, g), _heads_t(dot, g)
                pb = p_ref[i, g]
                p = pb.astype(F32)
                dpt = _dot(v_g, dogt)
                delta = jnp.sum(p * dpt, axis=0, keepdims=True)
                dst = (p * (dpt - delta)).astype(BF16)
                dqs.append(_dot_tn(k_g, dst) * _QSCALE)
                dks.append(_dot_nt(qgt, dst))
                dvs.append(_dot_nt(dogt, pb))
                dsink[g:g + 1] -= ps_ref[i, g:g + 1] * delta
            dq_ref[rows] = _from_heads_t(dqs).astype(BF16)
            dkv = jnp.concatenate([jnp.concatenate(dks, axis=0).T, jnp.concatenate(dvs, axis=0).T], axis=1)
            if i == 0:
                ex_ref[0] = dkv[:BLOCK]
            else:
                dkv_ref[(i - 1) * BLOCK:i * BLOCK] = (tail + dkv[:BLOCK]).astype(BF16)
            tail = dkv[BLOCK:]
        dkv_ref[(qb - 1) * BLOCK:] = tail.astype(BF16)

        @pl.when(t == nt - 1)
        def _():
            lane = lax.broadcasted_iota(jnp.int32, (1, ATTN_HEADS), 1)
            acc = jnp.zeros((1, ATTN_HEADS), F32)
            for h in range(ATTN_HEADS):
                g, hh = divmod(h, GQA_GROUP)
                acc = acc + jnp.where(lane == h, jnp.sum(dsink[g:g + 1, hh * BLOCK:(hh + 1) * BLOCK]), 0.0)
            ds_ref[...] = acc

    cur = lambda w: pl.BlockSpec((tr, w), lambda t: (t, 0))
    return pl.pallas_call(
        body, name="attn_bwd", grid=(nt,),
        in_specs=_prob_specs(qb) + [cur(ATTN_WIDTH)] + _kv_specs(tr)
        + [cur(ATTN_WIDTH), pl.BlockSpec(token.shape, lambda t: (0, 0))],
        out_specs=[cur(ATTN_WIDTH), cur(2 * KV_WIDTH), pl.BlockSpec((1, BLOCK, 2 * KV_WIDTH), lambda t: (t, 0, 0)),
                   pl.BlockSpec((1, ATTN_HEADS), lambda t: (0, 0))],
        out_shape=[jax.ShapeDtypeStruct((tp, ATTN_WIDTH), BF16), jax.ShapeDtypeStruct((tp, 2 * KV_WIDTH), BF16),
                   jax.ShapeDtypeStruct((nt, BLOCK, 2 * KV_WIDTH), F32), jax.ShapeDtypeStruct((1, ATTN_HEADS), F32)],
        scratch_shapes=[pltpu.VMEM((n_groups, GROUP_ROWS), F32)],
        compiler_params=_params("arbitrary"),
    )(probs, sink_probs, qkv, qkv, qkv, qkv, qkv, dattn, token)


def _fix_dkv(dkv, dkv_extra):
    tp = dkv.shape[0]
    tr = _row_tile(tp)
    nt, qb = tp // tr, tr // BLOCK
    if nt == 1:
        return dkv

    def body(d_ref, ex_ref, o_ref):
        o_ref[...] = (d_ref[...].astype(F32) + ex_ref[0]).astype(BF16)

    last = pl.BlockSpec((BLOCK, 2 * KV_WIDTH), lambda t: (t * qb + qb - 1, 0))
    return pl.pallas_call(
        body, name="fix_dkv", grid=(nt - 1,),
        in_specs=[last, pl.BlockSpec((1, BLOCK, 2 * KV_WIDTH), lambda t: (t + 1, 0, 0))],
        out_specs=last, out_shape=jax.ShapeDtypeStruct(dkv.shape, dkv.dtype),
        input_output_aliases={0: 0}, compiler_params=_params("parallel"),
    )(dkv, dkv_extra)


def _inproj_wgrad(dq, dkv, dxr, dyr, u0):
    tp = dq.shape[0]
    tr = _wgrad_row_tile(tp)

    def body(dq_ref, dkv_ref, dxr_ref, dyr_ref, u_ref, dw_ref):
        i = pl.program_id(0)
        dz = jnp.concatenate([dq_ref[...], dkv_ref[...], dxr_ref[...], dyr_ref[...]], axis=1)
        pw = _dot_tn(dz, u_ref[...])

        @pl.when(i == 0)
        def _():
            dw_ref[...] = pw

        @pl.when(i > 0)
        def _():
            dw_ref[...] += pw

    row = lambda w: pl.BlockSpec((tr, w), lambda i: (i, 0))
    return pl.pallas_call(
        body, name="inproj_wgrad", grid=(tp // tr,),
        in_specs=[row(ATTN_WIDTH), row(2 * KV_WIDTH), row(LRU_WIDTH), row(LRU_WIDTH), row(D_MODEL)],
        out_specs=pl.BlockSpec((IN_WIDTH, D_MODEL), lambda i: (0, 0)),
        out_shape=jax.ShapeDtypeStruct((IN_WIDTH, D_MODEL), F32),
        compiler_params=_params("arbitrary"),
    )(dq, dkv, dxr, dyr, u0)


def _inproj_dgrad(dq, dkv, dxr, dyr, w_in, head, x, dh1, g, token):
    tp = dq.shape[0]
    tr = _row_tile(tp)
    nt, qb = tp // tr, tr // BLOCK

    def body(*refs):
        dq_ref, dkv_ref, dxr_ref, dyr_ref, w_ref, head_ref = refs[:6]
        pieces = refs[6:6 + qb]
        dh1_ref, g_ref, _, gx_ref, dhead_ref, dg_ref, buf, sems = refs[6 + qb:]
        i = pl.program_id(0)
        slot = i % 2

        def out_copy(step, at):
            return pltpu.make_async_copy(buf.at[at], gx_ref.at[pl.ds(step * tr - BLOCK, tr)], sems.at[at])

        dz = jnp.concatenate([dq_ref[...], dkv_ref[...], dxr_ref[...], dyr_ref[...]], axis=1)
        du = _dot(dz, w_ref[...])
        hhat, rs = _rms(_seq_tile(head_ref[...], pieces, i))
        dx, dg = _rms_bwd(hhat, rs, g_ref[...], du)
        dh0 = dh1_ref[...] + dx

        @pl.when(i >= 3)
        def _():
            out_copy(i - 2, slot).wait()

        buf[slot] = dh0

        @pl.when(i == 0)
        def _():
            dg_ref[...] = dg
            dhead_ref[...] = dh0[:BLOCK]
            if tr > BLOCK:
                first = pltpu.make_async_copy(buf.at[0, pl.ds(BLOCK, tr - BLOCK)], gx_ref.at[pl.ds(0, tr - BLOCK)],
                                              sems.at[0])
                first.start()
                first.wait()

        @pl.when(i >= 1)
        def _():
            dg_ref[...] += dg
            out_copy(i, slot).start()

        @pl.when(i == nt - 1)
        def _():
            if nt >= 3:
                out_copy(nt - 2, (nt - 2) % 2).wait()
            if nt >= 2:
                out_copy(nt - 1, (nt - 1) % 2).wait()

    row = lambda w: pl.BlockSpec((tr, w), lambda i: (i, 0))
    full = lambda shape: pl.BlockSpec(shape, lambda i: (0,) * len(shape))
    return pl.pallas_call(
        body, name="inproj_dgrad", grid=(tp // tr,),
        in_specs=[row(ATTN_WIDTH), row(2 * KV_WIDTH), row(LRU_WIDTH), row(LRU_WIDTH), full(w_in.shape),
                  full(head.shape)] + _seq_specs(tr) + [row(D_MODEL), full(g.shape), full(token.shape)],
        out_specs=[pl.BlockSpec(memory_space=pl.ANY), full((BLOCK, D_MODEL)), full((1, D_MODEL))],
        out_shape=[jax.ShapeDtypeStruct(x.shape, F32), jax.ShapeDtypeStruct((BLOCK, D_MODEL), F32),
                   jax.ShapeDtypeStruct((1, D_MODEL), F32)],
        scratch_shapes=[pltpu.VMEM((2, tr, D_MODEL), F32), pltpu.SemaphoreType.DMA((2,))],
        compiler_params=_params("arbitrary"),
    )(dq, dkv, dxr, dyr, w_in, head, *([x] * qb), dh1, g, token)


def _dense_block_diag(w):
    eye = jnp.eye(LRU_BLOCKS, dtype=w.dtype)
    return (w[:, :, None, :] * eye[:, None, :, None]).reshape(LRU_WIDTH, LRU_WIDTH)


def _diag_blocks(dense):
    d4 = dense.reshape(LRU_BLOCKS, LRU_BLOCK, LRU_BLOCKS, LRU_BLOCK)
    return jnp.stack([d4[n, :, n, :] for n in range(LRU_BLOCKS)])


def _local_step(head, x, tgt, g_pre_mix, w_in, conv_w, conv_b, w_a, b_a, w_x, b_x, lam, sinks, g_post_mix,
                g_pre_ffn, g_post_ffn, late_weights, on_ffn_grads, on_outproj_bwd, on_mixer_grads, token):
    wa = _dense_block_diag(w_a).astype(BF16)
    wx = _dense_block_diag(w_x).astype(BF16)

    u0, qkv, xr, yr, hr, rec = _inproj_lru_fwd(head, x, g_pre_mix, w_in, conv_w, conv_b, wa, b_a, wx, b_x, lam, token)
    attn, probs, sink_probs = _attn_fwd(qkv, sinks)
    w_out, w1, w2 = late_weights([attn, rec])
    mix, h1, u1 = _outproj_fwd(attn, rec, w_out, head, x, g_post_mix, g_pre_ffn)
    r1, dy, df2, loss, dg_post_ffn = _ffn_fwd(u1, w1, w2, h1, tgt, g_post_ffn)

    da1, dh1, dmix, dg_pre_ffn, dg_post_mix = _ffn_bwd_data(df2, r1, w1, w2, dy, h1, mix, g_pre_ffn, g_post_mix)
    dw1, dw2 = _ffn_bwd_weights(u1, da1, r1, df2)
    token2 = on_ffn_grads(dw1, dw2)
    dxr, dyr, dattn, dw_out, dwa, dwx, vec = _outproj_lru_bwd(dmix, w_out, attn, rec, xr, yr, hr, conv_w, conv_b,
                                                              wa, b_a, wx, b_x, lam, token2)
    token3 = on_outproj_bwd(dattn)
    dq, dkv, dkv_extra, dsinks = _attn_bwd(qkv, dattn, probs, sink_probs, token3)
    dkv = _fix_dkv(dkv, dkv_extra)
    dw_in = _inproj_wgrad(dq, dkv, dxr, dyr, u0)
    token4 = on_mixer_grads(dw_in, dw_out)
    dx, dhead, dg_pre_mix = _inproj_dgrad(dq, dkv, dxr, dyr, w_in, head, x, dh1, g_pre_mix, token4)

    grads = dict(
        g_pre_mix=dg_pre_mix, conv_w=vec[0:4], conv_b=vec[4:5], w_a=_diag_blocks(dwa), b_a=vec[5:6],
        w_x=_diag_blocks(dwx), b_x=vec[6:7], lru_lambda=vec[7:8], attn_sinks=dsinks,
        g_post_mix=dg_post_mix, g_pre_ffn=dg_pre_ffn, g_post_ffn=dg_post_ffn)
    return loss, dx, dhead, grads


HBM = pl.BlockSpec(memory_space=pltpu.HBM)


def _mesh_pos():
    return lax.axis_index("x"), lax.axis_index("y"), lax.axis_index("c")


def _other_chips(x, y):
    return [(1 - x, y), (x, 1 - y), (1 - x, 1 - y)]


def _remote(src, dst, send_sem, recv_sem, to):
    return pltpu.make_async_remote_copy(src_ref=src, dst_ref=dst, send_sem=send_sem, recv_sem=recv_sem,
                                        device_id=to, device_id_type=MESH)


def _gather_weights(shards, lands, tiny, tiny_land):
    nbig = len(shards)

    def body(*refs):
        srcs, tiny_src = refs[:nbig], refs[nbig]
        outs, tiny_out = refs[2 * nbig + 2:3 * nbig + 2], refs[3 * nbig + 2]
        ici_send, ici_recv, d2d_send, d2d_recv, tiny_send, tiny_recv = refs[3 * nbig + 3:]
        x, y, c = _mesh_pos()
        me = 2 * x + y
        chips = _other_chips(x, y)
        sibling = (x, y, 1 - c)
        sends = []
        for w, (src, out) in enumerate(zip(srcs, outs)):
            hr = src.shape[0] // 2
            for j, chip in enumerate(chips):
                k = 3 * w + j
                cp = _remote(src.at[pl.ds(c * hr, hr)], out.at[me, pl.ds(c * hr, hr)],
                             ici_send.at[k], ici_recv.at[k], (*chip, c))
                cp.start()
                sends.append(cp)
        for j, chip in enumerate(chips):
            cp = _remote(tiny_src, tiny_out.at[me], tiny_send.at[j], tiny_recv.at[j], (*chip, c))
            cp.start()
            sends.append(cp)
        for w, (src, out) in enumerate(zip(srcs, outs)):
            hr = src.shape[0] // 2
            for j, (px, py) in enumerate(chips):
                k = 3 * w + j
                landed = out.at[2 * px + py, pl.ds(c * hr, hr)]
                _remote(landed, landed, ici_send.at[k], ici_recv.at[k], sibling).wait_recv()
                cp = _remote(landed, landed, d2d_send.at[k], d2d_recv.at[k], sibling)
                cp.start()
                sends.append(cp)
        for w, (src, out) in enumerate(zip(srcs, outs)):
            hr = src.shape[0] // 2
            for j, (px, py) in enumerate(chips):
                k = 3 * w + j
                other = out.at[2 * px + py, pl.ds((1 - c) * hr, hr)]
                _remote(other, other, d2d_send.at[k], d2d_recv.at[k], sibling).wait_recv()
        for j, (px, py) in enumerate(chips):
            blk = tiny_out.at[2 * px + py]
            _remote(blk, blk, tiny_send.at[j], tiny_recv.at[j], sibling).wait_recv()
        for cp in sends:
            cp.wait_send()

    out_shape = [jax.ShapeDtypeStruct(l.shape, l.dtype) for l in list(lands) + [tiny_land]]
    n = 3 * nbig
    return pl.pallas_call(
        body, name="gather_weights", out_shape=out_shape,
        in_specs=[HBM] * (2 * nbig + 2), out_specs=[HBM] * (nbig + 1),
        input_output_aliases={nbig + 1 + i: i for i in range(nbig + 1)},
        scratch_shapes=[pltpu.SemaphoreType.DMA((n,)),
                        pltpu.SemaphoreType.DMA((n,)), pltpu.SemaphoreType.DMA((n,)), pltpu.SemaphoreType.DMA((n,)),
                        pltpu.SemaphoreType.DMA((3,)), pltpu.SemaphoreType.DMA((3,))],
    )(*shards, tiny, *lands, tiny_land)


def _prep_shard(w, me):
    rows, cols = w.shape
    tr = 256 if rows % 256 == 0 else rows

    def body(me_ref, w_ref, s_ref, l_ref):
        b = w_ref[...].astype(BF16)
        s_ref[...] = b
        l_ref[0] = b

    return pl.pallas_call(
        body, name="prep_shard",
        grid_spec=pltpu.PrefetchScalarGridSpec(
            num_scalar_prefetch=1, grid=(rows // tr,),
            in_specs=[pl.BlockSpec((tr, cols), lambda i, me_ref: (i, 0))],
            out_specs=[pl.BlockSpec((tr, cols), lambda i, me_ref: (i, 0)),
                       pl.BlockSpec((1, tr, cols), lambda i, me_ref: (me_ref[0], i, 0))]),
        out_shape=[jax.ShapeDtypeStruct((rows, cols), BF16), jax.ShapeDtypeStruct((N_CHIPS, rows, cols), BF16)],
        compiler_params=_params("parallel"),
    )(me, w)


def _prep_tiny(tiny, me, slots=N_CHIPS):
    def body(me_ref, t_ref, l_ref):
        l_ref[0] = t_ref[...]

    return pl.pallas_call(
        body, name="prep_tiny",
        grid_spec=pltpu.PrefetchScalarGridSpec(
            num_scalar_prefetch=1, grid=(1,),
            in_specs=[pl.BlockSpec(tiny.shape, lambda i, me_ref: (0, 0))],
            out_specs=pl.BlockSpec((1,) + tiny.shape, lambda i, me_ref: (me_ref[0], 0, 0))),
        out_shape=jax.ShapeDtypeStruct((slots,) + tiny.shape, tiny.dtype),
    )(me, tiny)


N_DEV = 8


def _sibling_exchange(parts, token):
    def body(*refs):
        n = len(parts)
        srcs, outs, send_sems, recv_sems = refs[:n], refs[n + 1:2 * n + 1], refs[2 * n + 1], refs[2 * n + 2]
        x, y, c = _mesh_pos()
        sibling = (x, y, 1 - c)
        cps = []
        for w, (src, out) in enumerate(zip(srcs, outs)):
            hr = src.shape[1] // 2
            cp = _remote(src.at[:, pl.ds((1 - c) * hr, hr)], out, send_sems.at[w], recv_sems.at[w], sibling)
            cp.start()
            cps.append(cp)
        for cp in cps:
            cp.wait()

    n = len(parts)
    return pl.pallas_call(
        body, name="sibling_exchange",
        out_shape=[jax.ShapeDtypeStruct((p.shape[0], p.shape[1] // 2, p.shape[2]), p.dtype) for p in parts],
        in_specs=[HBM] * n + [pl.BlockSpec(memory_space=pl.ANY)], out_specs=[HBM] * n,
        scratch_shapes=[pltpu.SemaphoreType.DMA((n,)), pltpu.SemaphoreType.DMA((n,))],
    )(*parts, token)


def _chip_presum(part, from_sibling, pos):
    _, hr, cols = from_sibling.shape
    tr = 256 if hr % 256 == 0 else hr
    steps = hr // tr

    def body(pos_ref, a_ref, b_ref, o_ref, land_ref):
        s = (a_ref[...] + b_ref[...]).astype(BF16)
        o_ref[...] = s

        @pl.when(pl.program_id(1) == pos_ref[1])
        def _():
            land_ref[...] = s

    return pl.pallas_call(
        body, name="chip_presum",
        grid_spec=pltpu.PrefetchScalarGridSpec(
            num_scalar_prefetch=1, grid=(steps, N_CHIPS),
            in_specs=[pl.BlockSpec((1, tr, cols), lambda i, j, p: (j, p[0] * steps + i, 0)),
                      pl.BlockSpec((1, tr, cols), lambda i, j, p: (j, i, 0))],
            out_specs=[pl.BlockSpec((1, tr, cols), lambda i, j, p: (j, i, 0)),
                       pl.BlockSpec((1, tr, cols), lambda i, j, p: (p[1], p[0] * steps + i, 0))]),
        out_shape=[jax.ShapeDtypeStruct(from_sibling.shape, BF16),
                   jax.ShapeDtypeStruct((N_CHIPS, 2 * hr, cols), BF16)],
        compiler_params=_params("arbitrary", "arbitrary"),
    )(pos, part, from_sibling)


def _scatter_partials(cparts, lands, done_cparts=(), done_lands=()):
    n_new = len(cparts)
    nw = n_new + len(done_cparts)

    def body(*refs):
        srcs = refs[:nw]
        outs = refs[2 * nw:3 * nw]
        own_send, own_recv, ici_send, ici_recv, d2d_send, d2d_recv = refs[3 * nw:]
        x, y, c = _mesh_pos()
        me = 2 * x + y
        chips = _other_chips(x, y)
        sibling = (x, y, 1 - c)
        sends = []
        for w in list(range(n_new, nw)) + list(range(n_new)):
            src, out = srcs[w], outs[w]
            hr = src.shape[1]
            mine = out.at[me, pl.ds(c * hr, hr)]
            cp = _remote(src.at[me], mine, own_send.at[w], own_recv.at[w], sibling)
            cp.start()
            sends.append(cp)
            for j, (px, py) in enumerate(chips):
                if w >= n_new:
                    break
                k = 3 * w + j
                cp = _remote(src.at[2 * px + py], mine, ici_send.at[k], ici_recv.at[k], (px, py, c))
                cp.start()
                sends.append(cp)
        for w in list(range(n_new, nw)) + list(range(n_new)):
            src, out = srcs[w], outs[w]
            hr = src.shape[1]
            for j, (px, py) in enumerate(chips):
                k = 3 * w + j
                landed = out.at[2 * px + py, pl.ds(c * hr, hr)]
                if w < n_new:
                    _remote(landed, landed, ici_send.at[k], ici_recv.at[k], sibling).wait_recv()
                cp = _remote(landed, landed, d2d_send.at[k], d2d_recv.at[k], sibling)
                cp.start()
                sends.append(cp)
        for w, (src, out) in enumerate(zip(srcs, outs)):
            hr = src.shape[1]
            other = out.at[me, pl.ds((1 - c) * hr, hr)]
            _remote(other, other, own_send.at[w], own_recv.at[w], sibling).wait_recv()
            for j, (px, py) in enumerate(chips):
                k = 3 * w + j
                other = out.at[2 * px + py, pl.ds((1 - c) * hr, hr)]
                _remote(other, other, d2d_send.at[k], d2d_recv.at[k], sibling).wait_recv()
        for cp in sends:
            cp.wait_send()

    n = 3 * nw
    dma = pltpu.SemaphoreType.DMA
    every = list(cparts) + list(done_cparts)
    every_lands = list(lands) + list(done_lands)
    return pl.pallas_call(
        body, name="scatter_partials",
        out_shape=[jax.ShapeDtypeStruct(l.shape, l.dtype) for l in every_lands],
        in_specs=[HBM] * (2 * nw), out_specs=[HBM] * nw,
        input_output_aliases={nw + i: i for i in range(nw)},
        scratch_shapes=[dma((nw,)), dma((nw,)), dma((n,)), dma((n,)), dma((n,)), dma((n,))],
    )(*every, *every_lands)


SEM = pl.BlockSpec(memory_space=pltpu.SEMAPHORE)
SPLIT_COPY = pltpu.CompilerParams(has_side_effects=pltpu.SideEffectType.DATAFLOW_SIDE_EFFECTING)


def _hbm(a):
    return pltpu.with_memory_space_constraint(a, pltpu.HBM)


def _gather_copies(srcs, lands, send_sems, recv_sems):
    x, y, c = _mesh_pos()
    me = 2 * x + y
    sends, recvs = [], []
    for w, (src, land) in enumerate(zip(srcs, lands)):
        hr = src.shape[0] // 2
        for j, (px, py) in enumerate(_other_chips(x, y)):
            k = 3 * w + j
            sends.append(_remote(src.at[pl.ds(c * hr, hr)], land.at[me, pl.ds(c * hr, hr)],
                                 send_sems.at[k], recv_sems.at[k], (px, py, c)))
            got = land.at[2 * px + py, pl.ds(c * hr, hr)]
            recvs.append(_remote(got, got, send_sems.at[k], recv_sems.at[k], (px, py, c)))
    return sends, recvs


def _scatter_copies(srcs, lands, send_sems, recv_sems):
    x, y, c = _mesh_pos()
    me = 2 * x + y
    sends, recvs = [], []
    for w, (src, land) in enumerate(zip(srcs, lands)):
        hr = src.shape[1]
        for j, (px, py) in enumerate(_other_chips(x, y)):
            k = 3 * w + j
            sends.append(_remote(src.at[2 * px + py], land.at[me, pl.ds(c * hr, hr)],
                                 send_sems.at[k], recv_sems.at[k], (px, py, c)))
            got = land.at[2 * px + py, pl.ds(c * hr, hr)]
            recvs.append(_remote(got, got, send_sems.at[k], recv_sems.at[k], (px, py, c)))
    return sends, recvs


def _sibling_copies(srcs, lands, send_sems, recv_sems):
    x, y, c = _mesh_pos()
    sibling = (x, y, 1 - c)
    sends, recvs = [], []
    for w, (src, land) in enumerate(zip(srcs, lands)):
        hr = src.shape[1] // 2
        sends.append(_remote(src.at[:, pl.ds((1 - c) * hr, hr)], land, send_sems.at[w], recv_sems.at[w], sibling))
        recvs.append(_remote(land, land, send_sems.at[w], recv_sems.at[w], sibling))
    return sends, recvs


def _inchip_copies(srcs, lands, send_sems, recv_sems):
    x, y, c = _mesh_pos()
    me = 2 * x + y
    sibling = (x, y, 1 - c)
    sends, recvs = [], []
    for w, (src, land) in enumerate(zip(srcs, lands)):
        hr = src.shape[1]
        mine, other = pl.ds(c * hr, hr), pl.ds((1 - c) * hr, hr)
        blocks = [(me, src.at[me])] + [(2 * px + py, None) for px, py in _other_chips(x, y)]
        for j, (blk, own_src) in enumerate(blocks):
            k = 4 * w + j
            landed = land.at[blk, mine]
            sends.append(_remote(landed if own_src is None else own_src, landed, send_sems.at[k], recv_sems.at[k], sibling))
            got = land.at[blk, other]
            recvs.append(_remote(got, got, send_sems.at[k], recv_sems.at[k], sibling))
    return sends, recvs


def _all_peers_copies(srcs, lands, send_sems, recv_sems):
    x, y, c = _mesh_pos()
    (src,), (land,) = srcs, lands
    flip = lambda v, bit: 1 - v if bit else v
    sends, recvs = [], []
    for k in range(N_DEV - 1):
        px, py, pc = flip(x, (k + 1) & 4), flip(y, (k + 1) & 2), flip(c, (k + 1) & 1)
        sends.append(_remote(src, land.at[4 * x + 2 * y + c], send_sems.at[k], recv_sems.at[k], (px, py, pc)))
        got = land.at[4 * px + 2 * py + pc]
        recvs.append(_remote(got, got, send_sems.at[k], recv_sems.at[k], (px, py, pc)))
    return sends, recvs


def _split_start(name, copies_of, srcs, land_shapes, n_copies=None):
    n = len(srcs)
    k = 3 * n if n_copies is None else n_copies

    def body(*refs):
        src_refs, land_refs = refs[:n], refs[n:2 * n]
        send_sems, recv_sems = refs[2 * n], refs[2 * n + 1]
        token = refs[-1]
        sends, _ = copies_of(src_refs, land_refs, send_sems, recv_sems)
        for cp in sends:
            cp.start()
        token[...] = jnp.zeros_like(token)

    lands = [_hbm(s) for s in land_shapes]
    dma = pltpu.SemaphoreType.DMA
    res = pl.pallas_call(
        body, name=name,
        out_shape=(dma((k,)), dma((k,)), *[pltpu.HBM(s.shape, s.dtype) for s in srcs],
                   *[pltpu.HBM(s.shape, s.dtype) for s in land_shapes], jax.ShapeDtypeStruct((8, 128), F32)),
        in_specs=[HBM] * (2 * n),
        out_specs=(SEM, SEM, *([HBM] * (2 * n)), pl.BlockSpec(memory_space=pltpu.VMEM)),
        input_output_aliases={i: 2 + i for i in range(2 * n)},
        compiler_params=SPLIT_COPY,
    )(*[_hbm(s) for s in srcs], *lands)
    return res[0], res[1], list(res[2:2 + n]), list(res[2 + n:2 + 2 * n]), res[-1]


def _split_wait(name, copies_of, send_sems, recv_sems, srcs, lands, after):
    n = len(srcs)

    def body(*refs):
        src_refs, land_refs = refs[:n], refs[n:2 * n]
        sends, recvs = copies_of(src_refs, land_refs, refs[2 * n], refs[2 * n + 1])
        for cp in sends:
            cp.wait_send()
        for cp in recvs:
            cp.wait_recv()

    res = pl.pallas_call(
        body, name=name,
        out_shape=tuple(pltpu.HBM(s.shape, s.dtype) for s in list(srcs) + list(lands)),
        in_specs=[HBM] * (2 * n) + [SEM, SEM] + [pl.BlockSpec(memory_space=pl.ANY)] * len(after),
        out_specs=tuple([HBM] * (2 * n)),
        input_output_aliases={i: i for i in range(2 * n)},
        compiler_params=SPLIT_COPY,
    )(*srcs, *lands, send_sems, recv_sems, *after)
    return list(res[:n]), list(res[n:])


def _gather_finish(lands):
    n = len(lands)

    def body(*refs):
        outs = refs[n:2 * n]
        d2d_send, d2d_recv = refs[2 * n:]
        x, y, c = _mesh_pos()
        chips = _other_chips(x, y)
        sibling = (x, y, 1 - c)
        sends = []
        for w, out in enumerate(outs):
            hr = out.shape[1] // 2
            for j, (px, py) in enumerate(chips):
                landed = out.at[2 * px + py, pl.ds(c * hr, hr)]
                cp = _remote(landed, landed, d2d_send.at[3 * w + j], d2d_recv.at[3 * w + j], sibling)
                cp.start()
                sends.append(cp)
        for w, out in enumerate(outs):
            hr = out.shape[1] // 2
            for j, (px, py) in enumerate(chips):
                other = out.at[2 * px + py, pl.ds((1 - c) * hr, hr)]
                _remote(other, other, d2d_send.at[3 * w + j], d2d_recv.at[3 * w + j], sibling).wait_recv()
        for cp in sends:
            cp.wait_send()

    dma = pltpu.SemaphoreType.DMA
    return pl.pallas_call(
        body, name="gather_finish",
        out_shape=[jax.ShapeDtypeStruct(l.shape, l.dtype) for l in lands],
        in_specs=[HBM] * n, out_specs=[HBM] * n,
        input_output_aliases={i: i for i in range(n)},
        scratch_shapes=[dma((3 * n,)), dma((3 * n,))],
    )(*lands)


def _adamw(w, g, m, v):
    m = ADAM_B1 * m + (1.0 - ADAM_B1) * g
    v = ADAM_B2 * v + (1.0 - ADAM_B2) * (g * g)
    m_hat = m / (1.0 - ADAM_B1 ** ADAM_STEP)
    v_hat = v / (1.0 - ADAM_B2 ** ADAM_STEP)
    delta = -ADAM_LR * (m_hat / (jnp.sqrt(v_hat) + ADAM_EPS) + ADAM_WD * w)
    return delta, m, v


def _adamw_big(partials, w, m, v):
    rows, cols = w.shape
    tr = 256 if rows % 256 == 0 else rows

    def body(p_ref, w_ref, m_ref, v_ref, g_ref, d_ref, m2_ref, v2_ref):
        g = ((p_ref[0].astype(F32) + p_ref[1].astype(F32)) + p_ref[2].astype(F32)) + p_ref[3].astype(F32)
        g_ref[...] = g
        d_ref[...], m2_ref[...], v2_ref[...] = _adamw(w_ref[...], g, m_ref[...], v_ref[...])

    blk = pl.BlockSpec((tr, cols), lambda i: (i, 0))
    return pl.pallas_call(
        body, name="adamw_big", grid=(rows // tr,),
        in_specs=[pl.BlockSpec((N_CHIPS, tr, cols), lambda i: (0, i, 0)), blk, blk, blk],
        out_specs=[blk] * 4, out_shape=[jax.ShapeDtypeStruct((rows, cols), F32)] * 4,
        compiler_params=_params("parallel"),
    )(partials, w, m, v)


def _sum_devices(gathered, rows):
    cols = gathered.shape[1]

    def body(g_ref, o_ref):
        acc = g_ref[0:rows]
        for d in range(1, N_DEV):
            acc = acc + g_ref[d * rows:(d + 1) * rows]
        o_ref[...] = acc

    return pl.pallas_call(
        body, name="sum_devices", out_shape=jax.ShapeDtypeStruct((rows, cols), F32),
        in_specs=[pl.BlockSpec(memory_space=pltpu.VMEM)], out_specs=pl.BlockSpec(memory_space=pltpu.VMEM),
        compiler_params=pltpu.CompilerParams(vmem_limit_bytes=VMEM_LIMIT_V7X),
    )(gathered)


def _adamw_small(quads):
    n = len(quads)

    def body(*refs):
        ins, outs = refs[:4 * n], refs[4 * n:]
        for t in range(n):
            w, g, m, v = (r[...] for r in ins[4 * t:4 * t + 4])
            outs[3 * t][...], outs[3 * t + 1][...], outs[3 * t + 2][...] = _adamw(w, g, m, v)

    flat = [a for q in quads for a in q]
    vm = pl.BlockSpec(memory_space=pltpu.VMEM)
    res = pl.pallas_call(
        body, name="adamw_small",
        out_shape=[jax.ShapeDtypeStruct(q[0].shape, F32) for q in quads for _ in range(3)],
        in_specs=[vm] * (4 * n), out_specs=[vm] * (3 * n),
    )(*flat)
    return [tuple(res[3 * t:3 * t + 3]) for t in range(n)]


SMALL_PACK_ROWS = 96
META_COLS = D_MODEL // N_CHIPS
CONV_COLS = LRU_WIDTH // N_CHIPS
_WEIGHTS = ['meta_tokens', 'g_pre_mix', 'w_in', 'conv_w', 'conv_b', 'w_a', 'b_a', 'w_x', 'b_x', 'lru_lambda',
            'attn_sinks', 'w_out', 'g_post_mix', 'g_pre_ffn', 'w_ff1', 'w_ff2', 'g_post_ffn']
_BIG = ['w_in', 'w_out', 'w_ff1', 'w_ff2']


def _pack_small(dmeta, g, loss):
    z = lambda r, c: jnp.zeros((r, c), F32)
    rows = [
        dmeta,
        g['g_pre_mix'], g['g_post_mix'], g['g_pre_ffn'], g['g_post_ffn'],
        jnp.concatenate([g['conv_w'], z(4, 512)], axis=1),
        jnp.concatenate([g['conv_b'], g['b_a']], axis=1),
        jnp.concatenate([g['b_x'], g['lru_lambda']], axis=1),
        jnp.concatenate([g['attn_sinks'], z(1, D_MODEL - ATTN_HEADS)], axis=1),
        jnp.concatenate([loss, z(1, D_MODEL - 1)], axis=1),
        z(4, D_MODEL),
        g['w_a'].reshape(32, D_MODEL), g['w_x'].reshape(32, D_MODEL),
    ]
    return jnp.concatenate(rows, axis=0)


def _unpack_small(s, chip):
    return dict(
        meta_tokens=lax.dynamic_slice(s[0:N_META], (0, chip * META_COLS), (N_META, META_COLS)),
        g_pre_mix=s[16:17], g_post_mix=s[17:18], g_pre_ffn=s[18:19], g_post_ffn=s[19:20],
        conv_w=lax.dynamic_slice(s[20:24], (0, chip * CONV_COLS), (4, CONV_COLS)).reshape(1, 4, CONV_COLS),
        conv_b=s[24:25, :512], b_a=s[24:25, 512:], b_x=s[25:26, :512], lru_lambda=s[25:26, 512:],
        attn_sinks=s[26:27, :ATTN_HEADS], loss=s[27, 0],
        w_a=s[32:64].reshape(1, LRU_BLOCKS, LRU_BLOCK, LRU_BLOCK),
        w_x=s[64:96].reshape(1, LRU_BLOCKS, LRU_BLOCK, LRU_BLOCK))


def _as2d(a):
    if a.ndim == 2:
        return a
    return a.reshape(-1, a.shape[-1])


def kernel(x, meta_tokens, g_pre_mix, w_in, conv_w, conv_b, w_a, b_a, w_x, b_x, lru_lambda, attn_sinks, w_out, g_post_mix, g_pre_ffn, w_ff1, w_ff2, g_post_ffn, loss_target, m_meta_tokens, m_g_pre_mix, m_w_in, m_conv_w, m_conv_b, m_w_a, m_b_a, m_w_x, m_b_x, m_lru_lambda, m_attn_sinks, m_w_out, m_g_post_mix, m_g_pre_ffn, m_w_ff1, m_w_ff2, m_g_post_ffn, v_meta_tokens, v_g_pre_mix, v_w_in, v_conv_w, v_conv_b, v_w_a, v_b_a, v_w_x, v_b_x, v_lru_lambda, v_attn_sinks, v_w_out, v_g_post_mix, v_g_pre_ffn, v_w_ff1, v_w_ff2, v_g_post_ffn):
    weights = dict(meta_tokens=meta_tokens, g_pre_mix=g_pre_mix, w_in=w_in, conv_w=conv_w, conv_b=conv_b, w_a=w_a,
                   b_a=b_a, w_x=w_x, b_x=b_x, lru_lambda=lru_lambda, attn_sinks=attn_sinks, w_out=w_out,
                   g_post_mix=g_post_mix, g_pre_ffn=g_pre_ffn, w_ff1=w_ff1, w_ff2=w_ff2, g_post_ffn=g_post_ffn)
    mom1 = dict(zip(_WEIGHTS, [m_meta_tokens, m_g_pre_mix, m_w_in, m_conv_w, m_conv_b, m_w_a, m_b_a, m_w_x, m_b_x,
                               m_lru_lambda, m_attn_sinks, m_w_out, m_g_post_mix, m_g_pre_ffn, m_w_ff1, m_w_ff2,
                               m_g_post_ffn]))
    mom2 = dict(zip(_WEIGHTS, [v_meta_tokens, v_g_pre_mix, v_w_in, v_conv_w, v_conv_b, v_w_a, v_b_a, v_w_x, v_b_x,
                               v_lru_lambda, v_attn_sinks, v_w_out, v_g_post_mix, v_g_pre_ffn, v_w_ff1, v_w_ff2,
                               v_g_post_ffn]))
    xi, yi, ci = _mesh_pos()
    chip = 2 * xi + yi

    tiny = jnp.concatenate([meta_tokens, jnp.pad(conv_w[0], ((0, 4), (0, 128)))], axis=0)
    chip_arr = jnp.reshape(chip, (1,)).astype(jnp.int32)
    big2d = lambda a, name: a[0].T if name == 'w_in' else a[0]
    shards, lands = zip(*[_prep_shard(big2d(weights[n], n), chip_arr) for n in _BIG])
    g_in, g_tiny = _gather_weights(shards[:1], lands[:1], tiny, _prep_tiny(tiny, chip_arr))
    w_in_full = g_in.reshape(IN_WIDTH, D_MODEL)
    meta_full = jnp.concatenate([g_tiny[j, :N_META] for j in range(N_CHIPS)], axis=1)
    conv_w_full = jnp.concatenate([g_tiny[j, N_META:N_META + 4, :128] for j in range(N_CHIPS)], axis=1)
    g_send, g_recv, late_thru, late_lands, token = _split_start(
        "gather_late_start", _gather_copies, shards[1:], lands[1:])

    def late_weights(after):
        _, landed = _split_wait("gather_late_wait", _gather_copies, g_send, g_recv, late_thru, late_lands, after)
        g_out, g_f1, g_f2 = _gather_finish(landed)
        return g_out.reshape(D_MODEL, D_MODEL), g_f1, g_f2

    pos = jnp.stack([ci, chip]).astype(jnp.int32)
    ffn = {}


    def on_ffn_grads(dw1, dw2):
        parts = [dw1, dw2]
        lands = [lax.empty((p.shape[0], p.shape[1] // 2, p.shape[2]), p.dtype) for p in parts]
        ffn['sib'] = _split_start("sibling_ffn_start", _sibling_copies, parts, lands, len(parts))
        return ffn['sib'][4]

    def on_outproj_bwd(dattn):
        send, recv, thru, lands, _ = ffn['sib']
        parts, from_sibling = _split_wait("sibling_ffn_wait", _sibling_copies, send, recv, thru, lands, [dattn])
        cparts_ffn, lands_ffn = zip(*[_chip_presum(p, r, pos) for p, r in zip(parts, from_sibling)])
        ffn['send'], ffn['recv'], ffn['thru'], ffn['lands'], token3 = _split_start(
            "scatter_ffn_start", _scatter_copies, cparts_ffn, lands_ffn)
        return token3

    def on_mixer_grads(dw_in, dw_out):
        ffn_cparts, ffn_lands = _split_wait("scatter_ffn_wait", _scatter_copies, ffn['send'], ffn['recv'],
                                            ffn['thru'], ffn['lands'], [dw_in])
        ffn['inchip'] = _split_start("inchip_ffn_start", _inchip_copies, ffn_cparts, ffn_lands, 4 * len(ffn_cparts))
        parts = [dw_in.reshape(N_CHIPS, IN_WIDTH // N_CHIPS, D_MODEL),
                 dw_out.reshape(N_CHIPS, D_MODEL // N_CHIPS, D_MODEL)]
        from_sibling = _sibling_exchange(parts, ffn['inchip'][4])
        cparts, lands = zip(*[_chip_presum(p, r, pos) for p, r in zip(parts, from_sibling)])
        ffn['mixer'] = _split_start("scatter_mixer_start", _scatter_copies, cparts, lands)
        return ffn['mixer'][4]

    head = jnp.concatenate([jnp.zeros((PAD_ROWS, D_MODEL), F32), meta_full], axis=0)
    loss, dx, dhead, grads = _local_step(head, x[0], loss_target[0], g_pre_mix, w_in_full, conv_w_full, conv_b, w_a[0],
                                         b_a, w_x[0], b_x, lru_lambda, attn_sinks, g_post_mix, g_pre_ffn, g_post_ffn,
                                         late_weights, on_ffn_grads, on_outproj_bwd, on_mixer_grads, token)
    grad_x = dx[None]

    pack = _pack_small(dhead[PAD_ROWS:], grads, loss)
    dev = jnp.reshape(4 * xi + 2 * yi + ci, (1,)).astype(jnp.int32)
    s_send, s_recv, s_thru, s_lands, token5 = _split_start(
        "gather_small_start", _all_peers_copies, [pack], [_prep_tiny(pack, dev, N_DEV)], N_DEV - 1)

    send, recv, thru, lands, _ = ffn['mixer']
    mixer_cparts, mixer_lands = _split_wait("scatter_mixer_wait", _scatter_copies, send, recv, thru, lands, [token5])
    mixer_partials = _scatter_partials([], [], mixer_cparts, mixer_lands)
    send, recv, thru, lands, _ = ffn['inchip']
    _, ffn_partials = _split_wait("inchip_ffn_wait", _inchip_copies, send, recv, thru, lands, mixer_partials)
    chip_partials = list(mixer_partials) + ffn_partials

    g_out_d, delta, new_m, new_v = {}, {}, {}, {}
    for name, part in zip(_BIG, chip_partials):
        shp = weights[name].shape
        res = _adamw_big(part, big2d(weights[name], name), big2d(mom1[name], name), big2d(mom2[name], name))
        g_out_d[name], delta[name], new_m[name], new_v[name] = (big2d(r[None], name).reshape(shp) for r in res)

    _, (gathered,) = _split_wait("gather_small_wait", _all_peers_copies, s_send, s_recv, s_thru, s_lands,
                                 [g_out_d[n] for n in _BIG])
    small = _unpack_small(_sum_devices(gathered.reshape(N_DEV * SMALL_PACK_ROWS, D_MODEL), SMALL_PACK_ROWS), chip)
    loss = small['loss']
    small_names = [n for n in _WEIGHTS if n not in _BIG]
    quads = [(_as2d(weights[n]), _as2d(small[n]), _as2d(mom1[n]), _as2d(mom2[n])) for n in small_names]
    for name, (d, m2, v2) in zip(small_names, _adamw_small(quads)):
        shp = weights[name].shape
        g_out_d[name] = small[name].reshape(shp)
        delta[name], new_m[name], new_v[name] = d.reshape(shp), m2.reshape(shp), v2.reshape(shp)

    return (loss, grad_x, *[g_out_d[n] for n in _WEIGHTS], *[delta[n] for n in _WEIGHTS],
            *[new_m[n] for n in _WEIGHTS], *[new_v[n] for n in _WEIGHTS])
```

```python
import numpy as np
import jax
import jax.numpy as jnp
from jax import lax
from jax.experimental import pallas as pl
from jax.experimental.pallas import tpu as pltpu

F32 = jnp.float32
BF16 = jnp.bfloat16

D_MODEL = 1024
N_META = 16
BLOCK = 128
PAD_ROWS = BLOCK - N_META
HEAD_DIM = 64
ATTN_HEADS = 8
GQA_GROUP = 4
ATTN_WIDTH = 512
KV_WIDTH = 128
QKV_WIDTH = ATTN_WIDTH + 2 * KV_WIDTH
LRU_WIDTH = 512
LRU_BLOCKS = 8
LRU_BLOCK = 64
LRU_C = 8.0
IN_WIDTH = 1792
D_FF = 4096
N_CHIPS = 4
FF_CHUNK = D_FF // N_CHIPS
EPS = 1e-6
NEG = -1e30

ADAM_LR = 0.001
ADAM_B1 = 0.9
ADAM_B2 = 0.999
ADAM_EPS = 1e-08
ADAM_WD = 0.01
ADAM_STEP = 10

VMEM_LIMIT_V7X = 62 * 1024 * 1024
MESH = pl.DeviceIdType.MESH

NT = (((1,), (1,)), ((), ()))
TN = (((0,), (0,)), ((), ()))


def _row_tile(tp):
    return 640 if tp % 640 == 0 else BLOCK


def _wgrad_row_tile(tp):
    return 1664 if tp % 1664 == 0 else _row_tile(tp)


def _params(*sem):
    return pltpu.CompilerParams(dimension_semantics=sem, vmem_limit_bytes=VMEM_LIMIT_V7X)


def _dot(a, b):
    return jnp.dot(a, b, preferred_element_type=F32)


def _dot_nt(a, b):
    return lax.dot_general(a, b, NT, preferred_element_type=F32)


def _dot_tn(a, b):
    return lax.dot_general(a, b, TN, preferred_element_type=F32)


def _rms(x):
    rs = lax.rsqrt(jnp.mean(x * x, axis=-1, keepdims=True) + EPS)
    return x * rs, rs


def _rms_bwd(xhat, rs, g, dy):
    dyg = dy * g
    dx = rs * (dyg - xhat * jnp.mean(dyg * xhat, axis=-1, keepdims=True))
    dg = jnp.sum(dy * xhat, axis=0, keepdims=True)
    return dx, dg


def _gelu(x):
    k = 0.7978845608028654
    t = jnp.tanh(x * (k + (k * 0.044715) * (x * x)))
    return (0.5 * x) * (1.0 + t), t


def _gelu_grad(x, t):
    k = 0.7978845608028654
    return 0.5 * (1.0 + t) + 0.5 * x * (1.0 - t * t) * k * (1.0 + 3 * 0.044715 * x * x)


def _sigmoid(x):
    return 0.5 * jnp.tanh(0.5 * x) + 0.5


def _one_minus_exp2(y):
    t = jnp.tanh(y)
    return (-2.0 * t) / (1.0 - t)


def _softplus(x):
    return jnp.maximum(x, 0.0) + jnp.log1p(jnp.exp(-jnp.abs(x)))


def _seq_specs(tr, delay=0):
    qb = tr // BLOCK
    tile = lambda i: jnp.maximum(i - delay, 0)
    return [pl.BlockSpec((BLOCK, D_MODEL), lambda i, *_, s=s: (jnp.maximum(tile(i) * qb + s - 1, 0), 0))
            for s in range(qb)]


def _seq_tile(head, pieces, i):
    first = jnp.where(i == 0, head, pieces[0][...])
    return jnp.concatenate([first] + [p[...] for p in pieces[1:]], axis=0)


GROUP_ROWS = GQA_GROUP * BLOCK


def _attn_bias():
    j = np.arange(2 * BLOCK)[:, None]
    i = np.arange(BLOCK)[None, :]
    band = (j - i >= 1) & (j - i <= BLOCK)
    out = []
    for n in range(3):
        ok = band & ((n - 1) * BLOCK + j >= PAD_ROWS) if n < 2 else band
        out.append(np.tile(np.where(ok, 0.0, NEG).astype(np.float32), (1, GQA_GROUP)))
    return jnp.asarray(np.stack(out))


def _heads_t(at, g):
    heads = range(GQA_GROUP * g, GQA_GROUP * (g + 1))
    return jnp.concatenate([at[h * HEAD_DIM:(h + 1) * HEAD_DIM] for h in heads], axis=1).astype(BF16)


def _from_heads_t(groups):
    pairs = []
    for p in groups:
        for h in range(0, GQA_GROUP, 2):
            two = jnp.concatenate([p[:, h * BLOCK:(h + 1) * BLOCK], p[:, (h + 1) * BLOCK:(h + 2) * BLOCK]], axis=0)
            pairs.append(two.T)
    return jnp.concatenate(pairs, axis=1)


def _stack_heads(a, g):
    heads = range(GQA_GROUP * g, GQA_GROUP * (g + 1))
    return jnp.concatenate([a[:, h * HEAD_DIM:(h + 1) * HEAD_DIM] for h in heads], axis=0)


def _unstack_heads(groups):
    return jnp.concatenate([p[h * BLOCK:(h + 1) * BLOCK] for p in groups for h in range(GQA_GROUP)], axis=1)


def _attn_probs_t(k_g, qg, bias, sink_row):
    st = _dot_nt(k_g, qg) + bias
    m = jnp.maximum(jnp.max(st, axis=0, keepdims=True), sink_row)
    p = jnp.exp(st - m)
    es = jnp.exp(sink_row - m)
    inv = 1.0 / (jnp.sum(p, axis=0, keepdims=True) + es)
    return p * inv, es * inv


def _attn_consts(sinks):
    return jnp.repeat(sinks.reshape(ATTN_HEADS), BLOCK).reshape(ATTN_HEADS // GQA_GROUP, GROUP_ROWS), _attn_bias()


_SINK_SPEC = pl.BlockSpec((ATTN_HEADS // GQA_GROUP, GROUP_ROWS), lambda n: (0, 0))
_BIAS_SPEC = pl.BlockSpec((3, 2 * BLOCK, GROUP_ROWS), lambda n: (0, 0, 0))
_QSCALE = HEAD_DIM ** -0.5


def _kv_specs(tr):
    qb = tr // BLOCK
    prev = lambda col: pl.BlockSpec((BLOCK, KV_WIDTH), lambda t: (jnp.maximum(t * qb - 1, 0), col))
    cur = lambda col: pl.BlockSpec((tr, KV_WIDTH), lambda t: (t, col))
    return [prev(4), cur(4), prev(5), cur(5)]


def _block_bias(b_ref, t, qb, i):
    return b_ref[2] if i >= 2 else b_ref[jnp.minimum(t * qb + i, 2)]


N_KV = ATTN_HEADS // GQA_GROUP


def _prob_specs(qb):
    return [pl.BlockSpec((qb, N_KV, 2 * BLOCK, GROUP_ROWS), lambda t: (t, 0, 0, 0)),
            pl.BlockSpec((qb, SUBLANES, GROUP_ROWS), lambda t: (t, 0, 0))]


def _attn_fwd(qkv, sinks):
    tp = qkv.shape[0]
    tr = _row_tile(tp)
    qb, nb = tr // BLOCK, tp // BLOCK
    sink_rows, bias = _attn_consts(sinks)

    def body(s_ref, b_ref, q_ref, kp_ref, kc_ref, vp_ref, vc_ref, o_ref, p_ref, ps_ref):
        t = pl.program_id(0)
        k_all = jnp.concatenate([kp_ref[...], kc_ref[...]], axis=0)
        v_all = jnp.concatenate([vp_ref[...], vc_ref[...]], axis=0)
        for i in range(qb):
            rows = slice(i * BLOCK, (i + 1) * BLOCK)
            q = q_ref[rows]
            k2, v2 = k_all[i * BLOCK:(i + 2) * BLOCK], v_all[i * BLOCK:(i + 2) * BLOCK]
            bias_n = _block_bias(b_ref, t, qb, i)
            outs, sink_probs = [], []
            for g in range(N_KV):
                cols = slice(g * HEAD_DIM, (g + 1) * HEAD_DIM)
                qg = _stack_heads(q, g) * jnp.asarray(_QSCALE, BF16)
                p, ps = _attn_probs_t(k2[:, cols], qg, bias_n, s_ref[g:g + 1])
                pb = p.astype(BF16)
                p_ref[i, g] = pb
                sink_probs.append(ps)
                outs.append(_dot_tn(pb, v2[:, cols]))
            o_ref[rows] = _unstack_heads(outs).astype(BF16)
            ps_ref[i] = jnp.concatenate(sink_probs + [jnp.zeros((SUBLANES - N_KV, GROUP_ROWS), F32)], axis=0)

    return pl.pallas_call(
        body, name="attn_fwd", grid=(tp // tr,),
        in_specs=[_SINK_SPEC, _BIAS_SPEC, pl.BlockSpec((tr, ATTN_WIDTH), lambda t: (t, 0))] + _kv_specs(tr),
        out_specs=[pl.BlockSpec((tr, ATTN_WIDTH), lambda t: (t, 0))] + _prob_specs(qb),
        out_shape=[jax.ShapeDtypeStruct((tp, ATTN_WIDTH), BF16),
                   jax.ShapeDtypeStruct((nb, N_KV, 2 * BLOCK, GROUP_ROWS), BF16),
                   jax.ShapeDtypeStruct((nb, SUBLANES, GROUP_ROWS), F32)],
        compiler_params=_params("parallel"),
    )(sink_rows, bias, qkv, qkv, qkv, qkv, qkv)


def _conv_taps(x, halo):
    ext = jnp.concatenate([halo, x], axis=0)
    return [ext[8:] if k == 3 else pltpu.roll(ext, 3 - k, 0)[8:] for k in range(4)]


def _lru_gates(xc, wa, ba, wx, bx, sp):
    xb = xc.astype(BF16)
    r = _sigmoid(_dot(xb, wa) + ba)
    ig = _sigmoid(_dot(xb, wx) + bx)
    log_a = (-LRU_C * sp) * r
    a = jnp.exp(log_a)
    mult = jnp.sqrt(_one_minus_exp2(log_a))
    return xb, r, ig, a, mult


SUBLANES = 8


def _scan_fwd(a, b, h_in):
    n, width = a.shape
    a, b = (v.reshape(n // SUBLANES, SUBLANES, width) for v in (a, b))
    in_group = lax.broadcasted_iota(jnp.int32, a.shape, 1)
    for d in (1, 2, 4):
        keep = in_group >= d
        b = jnp.where(keep, a * pltpu.roll(b, d, 1) + b, b)
        a = jnp.where(keep, a * pltpu.roll(a, d, 1), a)
    a, b = a.reshape(n, width), b.reshape(n, width)
    out, carry = [], h_in
    for g in range(0, n, SUBLANES):
        h = a[g:g + SUBLANES] * carry + b[g:g + SUBLANES]
        out.append(h)
        carry = h[SUBLANES - 1:]
    return jnp.concatenate(out, axis=0)


def _scan_rev(c, b, g_in):
    n, width = c.shape
    c, b = (v.reshape(n // SUBLANES, SUBLANES, width) for v in (c, b))
    in_group = lax.broadcasted_iota(jnp.int32, c.shape, 1)
    for d in (1, 2, 4):
        keep = in_group < SUBLANES - d
        b = jnp.where(keep, b + c * pltpu.roll(b, SUBLANES - d, 1), b)
        c = jnp.where(keep, c * pltpu.roll(c, SUBLANES - d, 1), c)
    c, b = c.reshape(n, width), b.reshape(n, width)
    out, carry = [], g_in
    for g in range(n - SUBLANES, -1, -SUBLANES):
        r = b[g:g + SUBLANES] + c[g:g + SUBLANES] * carry
        out.append(r)
        carry = r[:1]
    return jnp.concatenate(out[::-1], axis=0)


def _inproj_lru_fwd(head, x, g, w_in, conv_w, conv_b, wa, ba, wx, bx, lam, token):
    tp = BLOCK + x.shape[0]
    tr = _row_tile(tp)
    qb, nt = tr // BLOCK, tp // tr
    small = [conv_w, conv_b, wa, ba, wx, bx, lam]

    def body(*refs):
        head_ref, pieces = refs[0], refs[1:1 + qb]
        g_ref, w_ref, _, cw_ref, cb_ref, wa_ref, ba_ref, wx_ref, bx_ref, lam_ref = refs[1 + qb:11 + qb]
        u_ref, qkv_ref, xr_ref, yr_ref, hr_ref, rec_ref, zbuf, halo, hprev = refs[11 + qb:]
        i = pl.program_id(0)
        cur = i % 2

        @pl.when(i == 0)
        def _():
            halo[...] = jnp.zeros_like(halo)
            hprev[...] = jnp.zeros_like(hprev)
            zbuf[1] = jnp.zeros((tr, 2 * LRU_WIDTH), F32)

        def recurrent_branch(valid):
            cw, cb = cw_ref[...], cb_ref[...]
            wa_m, ba_v, wx_m, bx_v = wa_ref[...], ba_ref[...], wx_ref[...], bx_ref[...]
            sp = _softplus(-lam_ref[...])
            before, h_last = halo[...], hprev[0:1]
            for b in range(qb):
                rows = slice(b * BLOCK, (b + 1) * BLOCK)
                xy = zbuf[1 - cur, rows]
                xin = xy[:, :LRU_WIDTH]
                taps = _conv_taps(xin, before)
                before = xin[BLOCK - 8:]
                xc = cb + sum(cw[k:k + 1] * taps[k] for k in range(4))
                _, _, ig, a, mult = _lru_gates(xc, wa_m, ba_v, wx_m, bx_v, sp)
                u = mult * (ig * xc)
                if b == 0:
                    pos = (i - 1) * tr + lax.broadcasted_iota(jnp.int32, xc.shape, 0)
                    u = jnp.where(pos >= PAD_ROWS, u, 0.0)
                h = _scan_fwd(a, u, h_last)
                h_last = h[BLOCK - 1:]
                hr_ref[rows] = h
                gl, _ = _gelu(xy[:, LRU_WIDTH:])
                rec_ref[rows] = (gl * h).astype(BF16)
            halo[...] = jnp.where(valid, before, 0.0)
            hprev[0:1] = jnp.where(valid, h_last, 0.0)

        def projection():
            xhat, _ = _rms(_seq_tile(head_ref[...], pieces, i))
            u = (xhat * g_ref[...]).astype(BF16)
            u_ref[...] = u
            z = _dot_nt(u, w_ref[...])
            qkv_ref[...] = z[:, :QKV_WIDTH].astype(BF16)
            xr_ref[...] = z[:, QKV_WIDTH:QKV_WIDTH + LRU_WIDTH]
            yr_ref[...] = z[:, QKV_WIDTH + LRU_WIDTH:]
            zbuf[cur] = z[:, QKV_WIDTH:]

        @pl.when(i < nt)
        def _():
            recurrent_branch(i >= 1)
            projection()

        @pl.when(i == nt)
        def _():
            recurrent_branch(True)

    last = nt - 1
    this_row = lambda w: pl.BlockSpec((tr, w), lambda i: (jnp.minimum(i, last), 0))
    prev_row = lambda w: pl.BlockSpec((tr, w), lambda i: (jnp.maximum(i - 1, 0), 0))
    full = lambda a: pl.BlockSpec(a.shape, lambda i: (0,) * a.ndim)
    piece_specs = [pl.BlockSpec((BLOCK, D_MODEL), lambda i, s=s: (jnp.maximum(jnp.minimum(i, last) * qb + s - 1, 0), 0))
                   for s in range(qb)]
    return pl.pallas_call(
        body, name="inproj_lru_fwd", grid=(nt + 1,),
        in_specs=[full(head)] + piece_specs + [full(g), full(w_in), full(token)] + [full(a) for a in small],
        out_specs=[this_row(D_MODEL), this_row(QKV_WIDTH), this_row(LRU_WIDTH), this_row(LRU_WIDTH),
                   prev_row(LRU_WIDTH), prev_row(LRU_WIDTH)],
        out_shape=[jax.ShapeDtypeStruct((tp, D_MODEL), BF16), jax.ShapeDtypeStruct((tp, QKV_WIDTH), BF16),
                   jax.ShapeDtypeStruct((tp, LRU_WIDTH), F32), jax.ShapeDtypeStruct((tp, LRU_WIDTH), F32),
                   jax.ShapeDtypeStruct((tp, LRU_WIDTH), F32), jax.ShapeDtypeStruct((tp, LRU_WIDTH), BF16)],
        scratch_shapes=[pltpu.VMEM((2, tr, 2 * LRU_WIDTH), F32), pltpu.VMEM((8, LRU_WIDTH), F32),
                        pltpu.VMEM((8, LRU_WIDTH), F32)],
        compiler_params=_params("arbitrary"),
    )(head, *([x] * qb), g, w_in, token, *small)


def _outproj_fwd(attn, rec, w_out, head, x, g_post_mix, g_pre_ffn):
    tp = attn.shape[0]
    tr = _row_tile(tp)
    qb = tr // BLOCK

    def body(*refs):
        a_ref, r_ref, w_ref, head_ref = refs[:4]
        pieces = refs[4:4 + qb]
        gm_ref, gf_ref, mix_ref, h1_ref, u1_ref = refs[4 + qb:]
        mix = _dot(a_ref[...], w_ref[:ATTN_WIDTH]) + _dot(r_ref[...], w_ref[ATTN_WIDTH:])
        mix_ref[...] = mix
        mhat, _ = _rms(mix)
        h1 = _seq_tile(head_ref[...], pieces, pl.program_id(0)) + mhat * gm_ref[...]
        h1_ref[...] = h1
        hhat, _ = _rms(h1)
        u1_ref[...] = (hhat * gf_ref[...]).astype(BF16)

    row = lambda w: pl.BlockSpec((tr, w), lambda i: (i, 0))
    full = lambda a: pl.BlockSpec(a.shape, lambda i: (0,) * a.ndim)
    return pl.pallas_call(
        body, name="outproj_fwd", grid=(tp // tr,),
        in_specs=[row(ATTN_WIDTH), row(LRU_WIDTH), full(w_out), full(head)] + _seq_specs(tr)
        + [full(g_post_mix), full(g_pre_ffn)],
        out_specs=[row(D_MODEL), row(D_MODEL), row(D_MODEL)],
        out_shape=[jax.ShapeDtypeStruct((tp, D_MODEL), F32), jax.ShapeDtypeStruct((tp, D_MODEL), F32),
                   jax.ShapeDtypeStruct((tp, D_MODEL), BF16)],
        compiler_params=_params("parallel"),
    )(attn, rec, w_out, head, *([x] * qb), g_post_mix, g_pre_ffn)


FFN_STEPS = N_CHIPS


def _resident(a):
    return pl.BlockSpec(a.shape, lambda *_: (0,) * a.ndim, pipeline_mode=pl.Buffered(1))


def _ffn_fwd(u1, w1, w2, h1, tgt, g_post_ffn):
    tp = h1.shape[0]
    tr = _row_tile(tp)
    qb, nt = tr // BLOCK, tp // tr
    sr = tr // FFN_STEPS

    def body(*refs):
        u_ref, w1_ref, w2_ref, h1_ref = refs[:4]
        t_pieces = refs[4:4 + qb]
        g_ref, r1_ref, dy_ref, df2_ref, loss_ref, dg_ref, acc = refs[4 + qb:]
        i, c = pl.program_id(0), pl.program_id(1)
        cur = i % 2

        @pl.when((i == 0) & (c == 0))
        def _():
            loss_ref[...] = jnp.zeros_like(loss_ref)
            dg_ref[...] = jnp.zeros_like(dg_ref)
            acc[1] = jnp.zeros((tr, D_MODEL), F32)

        def matmuls():
            r = jnp.maximum(_dot(u_ref[...], w1_ref[c]), 0.0)
            r1_ref[...] = r.astype(BF16)
            return _dot((r * r).astype(BF16), w2_ref[c])

        def finish_previous_tile(k, valid):
            lo, hi = k * sr, (k + 1) * sr
            g = g_ref[...]
            fhat, rs = _rms(acc[1 - cur, lo:hi])
            h2 = h1_ref[...] + fhat * g
            rows = (i - 1) * tr + lo + lax.broadcasted_iota(jnp.int32, h2.shape, 0)
            tgt = jnp.concatenate([p[max(lo - s * BLOCK, 0):min(hi - s * BLOCK, BLOCK)] for s, p in enumerate(t_pieces)
                                   if lo < (s + 1) * BLOCK and hi > s * BLOCK], axis=0)
            err = jnp.where((rows >= BLOCK) & valid, h2 - tgt, 0.0)
            dy = err * (1.0 / D_MODEL)
            dy_ref[...] = dy
            loss_ref[...] += (0.5 / D_MODEL) * jnp.sum(err * err)
            df2, dg = _rms_bwd(fhat, rs, g, dy)
            df2_ref[...] = df2.astype(BF16)
            dg_ref[...] += dg

        for k in range(FFN_STEPS):
            @pl.when((c == k) & (i < nt))
            def _(k=k):
                finish_previous_tile(k, i >= 1)
                if k == 0:
                    acc[cur] = matmuls()
                else:
                    acc[cur] += matmuls()

            @pl.when((c == k) & (i == nt))
            def _(k=k):
                finish_previous_tile(k, True)

    last = nt - 1
    this_row = pl.BlockSpec((tr, D_MODEL), lambda i, c: (jnp.minimum(i, last), 0))
    prev_quarter = pl.BlockSpec((sr, D_MODEL), lambda i, c: (jnp.maximum(i - 1, 0) * FFN_STEPS + c, 0))
    prev_quarter_out = pl.BlockSpec(
        (sr, D_MODEL), lambda i, c: (jnp.where(i == 0, nt * FFN_STEPS, (i - 1) * FFN_STEPS + c), 0))
    full = lambda a: pl.BlockSpec(a.shape, lambda i, c: (0,) * a.ndim)
    return pl.pallas_call(
        body, name="ffn_fwd", grid=(nt + 1, FFN_STEPS),
        in_specs=[this_row, _resident(w1), _resident(w2), prev_quarter] + _seq_specs(tr, delay=1) + [full(g_post_ffn)],
        out_specs=[pl.BlockSpec((tr, FF_CHUNK), lambda i, c: (jnp.minimum(i, last), jnp.where(i < nt, c, FFN_STEPS - 1))),
                   prev_quarter_out, prev_quarter_out,
                   pl.BlockSpec((1, 1), lambda i, c: (0, 0)), pl.BlockSpec((1, D_MODEL), lambda i, c: (0, 0))],
        out_shape=[jax.ShapeDtypeStruct((tp, D_FF), BF16), jax.ShapeDtypeStruct((tp + sr, D_MODEL), F32),
                   jax.ShapeDtypeStruct((tp + sr, D_MODEL), BF16), jax.ShapeDtypeStruct((1, 1), F32),
                   jax.ShapeDtypeStruct((1, D_MODEL), F32)],
        scratch_shapes=[pltpu.VMEM((2, tr, D_MODEL), F32)],
        compiler_params=_params("arbitrary", "arbitrary"),
    )(u1, w1, w2, h1, *([tgt] * qb), g_post_ffn)


def _ffn_bwd_data(df2, r1, w1, w2, dy, h1, mix, g_pre_ffn, g_post_mix):
    tp = h1.shape[0]
    tr = _row_tile(tp)
    nt = tp // tr
    sr = tr // FFN_STEPS

    def body(df2_ref, r1_ref, w1_ref, w2_ref, dy_ref, h1_ref, mix_ref, gf_ref, gm_ref,
             da_ref, dh1_ref, dmix_ref, dgf_ref, dgm_ref, acc):
        i, c = pl.program_id(0), pl.program_id(1)
        cur = i % 2

        @pl.when((i == 0) & (c == 0))
        def _():
            dgf_ref[...] = jnp.zeros_like(dgf_ref)
            dgm_ref[...] = jnp.zeros_like(dgm_ref)
            acc[1] = jnp.zeros((tr, D_MODEL), F32)

        def matmuls():
            df = _dot_nt(df2_ref[...], w2_ref[c])
            da = (df * (2.0 * r1_ref[...].astype(F32))).astype(BF16)
            da_ref[...] = da
            return _dot_nt(da, w1_ref[c])

        def finish_previous_tile(k, valid):
            lo, hi = k * sr, (k + 1) * sr
            hhat, rs = _rms(h1_ref[...])
            dx, dgf = _rms_bwd(hhat, rs, gf_ref[...], acc[1 - cur, lo:hi])
            dh1 = dy_ref[...] + dx
            dh1_ref[...] = dh1
            mhat, rsm = _rms(mix_ref[...])
            dmix, dgm = _rms_bwd(mhat, rsm, gm_ref[...], dh1)
            dmix_ref[...] = dmix.astype(BF16)
            dgf_ref[...] += jnp.where(valid, dgf, 0.0)
            dgm_ref[...] += jnp.where(valid, dgm, 0.0)

        for k in range(FFN_STEPS):
            @pl.when((c == k) & (i < nt))
            def _(k=k):
                finish_previous_tile(k, i >= 1)
                if k == 0:
                    acc[cur] = matmuls()
                else:
                    acc[cur] += matmuls()

            @pl.when((c == k) & (i == nt))
            def _(k=k):
                finish_previous_tile(k, True)

    last = nt - 1
    this_row = pl.BlockSpec((tr, D_MODEL), lambda i, c: (jnp.minimum(i, last), 0))
    prev_quarter = pl.BlockSpec((sr, D_MODEL), lambda i, c: (jnp.maximum(i - 1, 0) * FFN_STEPS + c, 0))
    prev_quarter_out = pl.BlockSpec(
        (sr, D_MODEL), lambda i, c: (jnp.where(i == 0, nt * FFN_STEPS, (i - 1) * FFN_STEPS + c), 0))
    chunk = pl.BlockSpec((tr, FF_CHUNK), lambda i, c: (jnp.minimum(i, last), jnp.where(i < nt, c, FFN_STEPS - 1)))
    gain = pl.BlockSpec((1, D_MODEL), lambda i, c: (0, 0))
    return pl.pallas_call(
        body, name="ffn_bwd_data", grid=(nt + 1, FFN_STEPS),
        in_specs=[this_row, chunk, _resident(w1), _resident(w2), prev_quarter, prev_quarter, prev_quarter, gain, gain],
        out_specs=[chunk, prev_quarter_out, prev_quarter_out, gain, gain],
        out_shape=[jax.ShapeDtypeStruct((tp, D_FF), BF16), jax.ShapeDtypeStruct((tp + sr, D_MODEL), F32),
                   jax.ShapeDtypeStruct((tp + sr, D_MODEL), BF16), jax.ShapeDtypeStruct((1, D_MODEL), F32),
                   jax.ShapeDtypeStruct((1, D_MODEL), F32)],
        scratch_shapes=[pltpu.VMEM((2, tr, D_MODEL), F32)],
        compiler_params=_params("arbitrary", "arbitrary"),
    )(df2, r1, w1, w2, dy, h1, mix, g_pre_ffn, g_post_mix)


def _ffn_bwd_weights(u1, da1, r1, df2):
    tp = u1.shape[0]
    tr = _wgrad_row_tile(tp)

    def body(u_ref, da_ref, r1_ref, df2_ref, dw1_ref, dw2_ref):
        i = pl.program_id(1)
        r = r1_ref[...].astype(F32)
        p1 = _dot_tn(u_ref[...], da_ref[...])
        p2 = _dot_tn((r * r).astype(BF16), df2_ref[...])

        @pl.when(i == 0)
        def _():
            dw1_ref[0] = p1
            dw2_ref[0] = p2

        @pl.when(i > 0)
        def _():
            dw1_ref[0] += p1
            dw2_ref[0] += p2

    row = pl.BlockSpec((tr, D_MODEL), lambda c, i: (i, 0))
    chunk = pl.BlockSpec((tr, FF_CHUNK), lambda c, i: (i, c))
    return pl.pallas_call(
        body, name="ffn_bwd_weights", grid=(N_CHIPS, tp // tr),
        in_specs=[row, chunk, chunk, row],
        out_specs=[pl.BlockSpec((1, D_MODEL, FF_CHUNK), lambda c, i: (c, 0, 0)),
                   pl.BlockSpec((1, FF_CHUNK, D_MODEL), lambda c, i: (c, 0, 0))],
        out_shape=[jax.ShapeDtypeStruct((N_CHIPS, D_MODEL, FF_CHUNK), F32),
                   jax.ShapeDtypeStruct((N_CHIPS, FF_CHUNK, D_MODEL), F32)],
        compiler_params=_params("parallel", "arbitrary"),
    )(u1, da1, r1, df2)


N_VEC_ROWS = 8


def _outproj_lru_bwd(dmix, w_out, attn, rec, xr, yr, hr, conv_w, conv_b, wa, ba, wx, bx, lam, token):
    tp = xr.shape[0]
    tr = _row_tile(tp)
    qb, nt = tr // BLOCK, tp // tr

    def body(dm_ref, w_ref, at_ref, rc_ref, xr_ref, xh_ref, yr_ref, hr_ref, hp_ref,
             cw_ref, cb_ref, wa_ref, ba_ref, wx_ref, bx_ref, lam_ref, _,
             dxr_ref, dyr_ref, dat_ref, dwo_ref, dwa_ref, dwx_ref, vec_ref, g_next, a_next, dxc_next, dsp):
        s = pl.program_id(0)
        t = nt - 1 - s

        @pl.when(s == 0)
        def _():
            g_next[...] = jnp.zeros_like(g_next)
            a_next[...] = jnp.zeros_like(a_next)
            dxc_next[...] = jnp.zeros_like(dxc_next)
            dsp[...] = jnp.zeros_like(dsp)
            dwo_ref[...] = jnp.zeros_like(dwo_ref)
            dwa_ref[...] = jnp.zeros_like(dwa_ref)
            dwx_ref[...] = jnp.zeros_like(dwx_ref)
            vec_ref[...] = jnp.zeros_like(vec_ref)

        dm = dm_ref[...]
        dcat = _dot_nt(dm, w_ref[...])
        dat_ref[...] = dcat[:, :ATTN_WIDTH].astype(BF16)
        drec_tile = dcat[:, ATTN_WIDTH:]
        dwo_ref[:ATTN_WIDTH] += _dot_tn(at_ref[...], dm)
        dwo_ref[ATTN_WIDTH:] += _dot_tn(rc_ref[...], dm)

        first_tile = t == 0
        cw, cb = cw_ref[...], cb_ref[...]
        lam_v = lam_ref[...]
        sp = _softplus(-lam_v)
        wa_m, ba_v, wx_m, bx_v = wa_ref[...], ba_ref[...], wx_ref[...], bx_ref[...]
        rows = lax.broadcasted_iota(jnp.int32, (BLOCK, LRU_WIDTH), 0)
        col = lambda v: jnp.sum(v, axis=0, keepdims=True)

        g_after, a_after, dxc_after = g_next[0:1], a_next[0:1], dxc_next[...]
        xbs, dgrs, dgis = [], [], []
        vec = [jnp.zeros((1, LRU_WIDTH), F32) for _ in range(N_VEC_ROWS)]
        for i in reversed(range(qb)):
            blk = slice(i * BLOCK, (i + 1) * BLOCK)
            if i == 0:
                x_before = jnp.where(first_tile, 0.0, xh_ref[...])
                h_before = jnp.where(first_tile, 0.0, hp_ref[7:8])
            else:
                x_before = xr_ref[i * BLOCK - 8:i * BLOCK]
                h_before = hr_ref[i * BLOCK - 1:i * BLOCK]
            taps = _conv_taps(xr_ref[blk], x_before)
            xc = cb + sum(cw[k:k + 1] * taps[k] for k in range(4))
            xb, r, ig, a, mult = _lru_gates(xc, wa_m, ba_v, wx_m, bx_v, sp)

            yr_v = yr_ref[blk]
            gl, th = _gelu(yr_v)
            h = hr_ref[blk]
            drec = drec_tile[blk]
            dyr_ref[blk] = (drec * h * _gelu_grad(yr_v, th)).astype(BF16)

            a_up = jnp.where(rows == BLOCK - 1, a_after, pltpu.roll(a, BLOCK - 1, 0))
            g = _scan_rev(a_up, drec * gl, g_after)
            g_after, a_after = g[0:1], a[0:1]

            h_prev = jnp.where(rows == 0, h_before, pltpu.roll(h, 1, 0))
            du, da = g, g * h_prev
            if i == 0:
                real = (t * tr + rows) >= PAD_ROWS
                du, da = jnp.where(real, du, 0.0), jnp.where(real, da, 0.0)
            dmult = du * (ig * xc)
            dig = du * (mult * xc)
            dxc = du * (mult * ig)
            dlog_a = da * a - dmult * (a * a / mult)
            if i == 0:
                dlog_a = jnp.where(real, dlog_a, 0.0)
            dgr = (dlog_a * (-LRU_C * sp)) * (r * (1.0 - r))
            dgi = dig * (ig * (1.0 - ig))
            dgr_b, dgi_b = dgr.astype(BF16), dgi.astype(BF16)
            dxc = dxc + _dot_nt(dgr_b, wa_m) + _dot_nt(dgi_b, wx_m)
            xbs.append(xb)
            dgrs.append(dgr_b)
            dgis.append(dgi_b)

            ext = jnp.concatenate([dxc, dxc_after], axis=0)
            up = [ext[:BLOCK] if j == 0 else pltpu.roll(ext, BLOCK + 8 - j, 0)[:BLOCK] for j in range(4)]
            dxr_ref[blk] = sum(cw[k:k + 1] * up[3 - k] for k in range(4)).astype(BF16)
            dxc_after = dxc[:8]

            for k in range(4):
                vec[k] = vec[k] + col(dxc * taps[k])
            vec[4] = vec[4] + col(dxc)
            vec[5] = vec[5] + col(dgr)
            vec[6] = vec[6] + col(dgi)
            vec[7] = vec[7] + col(dlog_a * (-LRU_C * r))

        g_next[0:1], a_next[0:1], dxc_next[...] = g_after, a_after, dxc_after
        xb_all = jnp.concatenate(xbs, axis=0)
        dwa_ref[...] += _dot_tn(xb_all, jnp.concatenate(dgrs, axis=0))
        dwx_ref[...] += _dot_tn(xb_all, jnp.concatenate(dgis, axis=0))
        for k in range(7):
            vec_ref[k:k + 1] += vec[k]
        dsp[0:1] += vec[7]

        @pl.when(s == nt - 1)
        def _():
            vec_ref[7:8] = dsp[0:1] * (-_sigmoid(-lam_v))

    blk_spec = pl.BlockSpec((tr, LRU_WIDTH), lambda s: (nt - 1 - s, 0))
    rows_before = pl.BlockSpec((8, LRU_WIDTH), lambda s: (jnp.maximum((nt - 1 - s) * (tr // 8) - 1, 0), 0))
    full = lambda a: pl.BlockSpec(a.shape, lambda s: (0,) * a.ndim)
    small = [conv_w, conv_b, wa, ba, wx, bx, lam, token]
    sq = pl.BlockSpec((LRU_WIDTH, LRU_WIDTH), lambda s: (0, 0))
    wide = pl.BlockSpec((tr, D_MODEL), lambda s: (nt - 1 - s, 0))
    whole = pl.BlockSpec((D_MODEL, D_MODEL), lambda s: (0, 0))
    return pl.pallas_call(
        body, name="outproj_lru_bwd", grid=(nt,),
        in_specs=[wide, whole, blk_spec, blk_spec, blk_spec, rows_before, blk_spec, blk_spec, rows_before]
        + [full(a) for a in small],
        out_specs=[blk_spec, blk_spec, blk_spec, whole, sq, sq, pl.BlockSpec((N_VEC_ROWS, LRU_WIDTH), lambda s: (0, 0))],
        out_shape=[jax.ShapeDtypeStruct((tp, LRU_WIDTH), BF16), jax.ShapeDtypeStruct((tp, LRU_WIDTH), BF16),
                   jax.ShapeDtypeStruct((tp, ATTN_WIDTH), BF16), jax.ShapeDtypeStruct((D_MODEL, D_MODEL), F32),
                   jax.ShapeDtypeStruct((LRU_WIDTH, LRU_WIDTH), F32), jax.ShapeDtypeStruct((LRU_WIDTH, LRU_WIDTH), F32),
                   jax.ShapeDtypeStruct((N_VEC_ROWS, LRU_WIDTH), F32)],
        scratch_shapes=[pltpu.VMEM((8, LRU_WIDTH), F32)] * 4,
        compiler_params=_params("arbitrary"),
    )(dmix, w_out, attn, rec, xr, xr, yr, hr, hr, *small)


def _attn_bwd_tile(tp):
    return _wgrad_row_tile(tp)


def _attn_bwd(qkv, dattn, probs, sink_probs, token):
    tp = qkv.shape[0]
    tr = _attn_bwd_tile(tp)
    qb, nt = tr // BLOCK, tp // tr
    n_groups = N_KV

    def body(p_ref, ps_ref, q_ref, kp_ref, kc_ref, vp_ref, vc_ref, do_ref, _, dq_ref, dkv_ref, ex_ref, ds_ref, dsink):
        t = pl.program_id(0)

        @pl.when(t == 0)
        def _():
            dsink[...] = jnp.zeros_like(dsink)

        k_all = jnp.concatenate([kp_ref[...], kc_ref[...]], axis=0)
        v_all = jnp.concatenate([vp_ref[...], vc_ref[...]], axis=0)
        tail = None
        for i in range(qb):
            rows = slice(i * BLOCK, (i + 1) * BLOCK)
            qt = (q_ref[rows].astype(F32) * _QSCALE).T
            dot = do_ref[rows].astype(F32).T
            k2, v2 = k_all[i * BLOCK:(i + 2) * BLOCK], v_all[i * BLOCK:(i + 2) * BLOCK]
            dqs, dks, dvs = [], [], []
            for g in range(n_groups):
                cols = slice(g * HEAD_DIM, (g + 1) * HEAD_DIM)
                k_g, v_g = k2[:, cols], v2[:, cols]
                qgt, dogt = _heads_t(qt, g), _heads_t(dot, g)
                pb = p_ref[i, g]
                p = pb.astype(F32)
                dpt = _dot(v_g, dogt)
                delta = jnp.sum(p * dpt, axis=0, keepdims=True)
                dst = (p * (dpt - delta)).astype(BF16)
                dqs.append(_dot_tn(k_g, dst) * _QSCALE)
                dks.append(_dot_nt(qgt, dst))
                dvs.append(_dot_nt(dogt, pb))
                dsink[g:g + 1] -= ps_ref[i, g:g + 1] * delta
            dq_ref[rows] = _from_heads_t(dqs).astype(BF16)
            dkv = jnp.concatenate([jnp.concatenate(dks, axis=0).T, jnp.concatenate(dvs, axis=0).T], axis=1)
            if i == 0:
                ex_ref[0] = dkv[:BLOCK]
            else:
                dkv_ref[(i - 1) * BLOCK:i * BLOCK] = (tail + dkv[:BLOCK]).astype(BF16)
            tail = dkv[BLOCK:]
        dkv_ref[(qb - 1) * BLOCK:] = tail.astype(BF16)

        @pl.when(t == nt - 1)
        def _():
            lane = lax.broadcasted_iota(jnp.int32, (1, ATTN_HEADS), 1)
            acc = jnp.zeros((1, ATTN_HEADS), F32)
            for h in range(ATTN_HEADS):
                g, hh = divmod(h, GQA_GROUP)
                acc = acc + jnp.where(lane == h, jnp.sum(dsink[g:g + 1, hh * BLOCK:(hh + 1) * BLOCK]), 0.0)
            ds_ref[...] = acc

    cur = lambda w: pl.BlockSpec((tr, w), lambda t: (t, 0))
    return pl.pallas_call(
        body, name="attn_bwd", grid=(nt,),
        in_specs=_prob_specs(qb) + [cur(ATTN_WIDTH)] + _kv_specs(tr)
        + [cur(ATTN_WIDTH), pl.BlockSpec(token.shape, lambda t: (0, 0))],
        out_specs=[cur(ATTN_WIDTH), cur(2 * KV_WIDTH), pl.BlockSpec((1, BLOCK, 2 * KV_WIDTH), lambda t: (t, 0, 0)),
                   pl.BlockSpec((1, ATTN_HEADS), lambda t: (0, 0))],
        out_shape=[jax.ShapeDtypeStruct((tp, ATTN_WIDTH), BF16), jax.ShapeDtypeStruct((tp, 2 * KV_WIDTH), BF16),
                   jax.ShapeDtypeStruct((nt, BLOCK, 2 * KV_WIDTH), F32), jax.ShapeDtypeStruct((1, ATTN_HEADS), F32)],
        scratch_shapes=[pltpu.VMEM((n_groups, GROUP_ROWS), F32)],
        compiler_params=_params("arbitrary"),
    )(probs, sink_probs, qkv, qkv, qkv, qkv, qkv, dattn, token)


def _fix_dkv(dkv, dkv_extra):
    tp = dkv.shape[0]
    tr = _attn_bwd_tile(tp)
    nt, qb = tp // tr, tr // BLOCK
    if nt == 1:
        return dkv

    def body(d_ref, ex_ref, o_ref):
        o_ref[...] = (d_ref[...].astype(F32) + ex_ref[0]).astype(BF16)

    last = pl.BlockSpec((BLOCK, 2 * KV_WIDTH), lambda t: (t * qb + qb - 1, 0))
    return pl.pallas_call(
        body, name="fix_dkv", grid=(nt - 1,),
        in_specs=[last, pl.BlockSpec((1, BLOCK, 2 * KV_WIDTH), lambda t: (t + 1, 0, 0))],
        out_specs=last, out_shape=jax.ShapeDtypeStruct(dkv.shape, dkv.dtype),
        input_output_aliases={0: 0}, compiler_params=_params("parallel"),
    )(dkv, dkv_extra)


def _inproj_wgrad(dq, dkv, dxr, dyr, u0):
    tp = dq.shape[0]
    tr = _wgrad_row_tile(tp)

    def body(dq_ref, dkv_ref, dxr_ref, dyr_ref, u_ref, dw_ref):
        i = pl.program_id(0)
        dz = jnp.concatenate([dq_ref[...], dkv_ref[...], dxr_ref[...], dyr_ref[...]], axis=1)
        pw = _dot_tn(dz, u_ref[...])

        @pl.when(i == 0)
        def _():
            dw_ref[...] = pw

        @pl.when(i > 0)
        def _():
            dw_ref[...] += pw

    row = lambda w: pl.BlockSpec((tr, w), lambda i: (i, 0))
    return pl.pallas_call(
        body, name="inproj_wgrad", grid=(tp // tr,),
        in_specs=[row(ATTN_WIDTH), row(2 * KV_WIDTH), row(LRU_WIDTH), row(LRU_WIDTH), row(D_MODEL)],
        out_specs=pl.BlockSpec((IN_WIDTH, D_MODEL), lambda i: (0, 0)),
        out_shape=jax.ShapeDtypeStruct((IN_WIDTH, D_MODEL), F32),
        compiler_params=_params("arbitrary"),
    )(dq, dkv, dxr, dyr, u0)


def _inproj_dgrad(dq, dkv, dxr, dyr, w_in, head, x, dh1, g, token):
    tp = dq.shape[0]
    tr = _row_tile(tp)
    nt, qb = tp // tr, tr // BLOCK

    def body(*refs):
        dq_ref, dkv_ref, dxr_ref, dyr_ref, w_ref, head_ref = refs[:6]
        pieces = refs[6:6 + qb]
        dh1_ref, g_ref, _, gx_ref, dhead_ref, dg_ref, buf, sems = refs[6 + qb:]
        i = pl.program_id(0)
        slot = i % 2

        def out_copy(step, at):
            return pltpu.make_async_copy(buf.at[at], gx_ref.at[pl.ds(step * tr - BLOCK, tr)], sems.at[at])

        dz = jnp.concatenate([dq_ref[...], dkv_ref[...], dxr_ref[...], dyr_ref[...]], axis=1)
        du = _dot(dz, w_ref[...])
        hhat, rs = _rms(_seq_tile(head_ref[...], pieces, i))
        dx, dg = _rms_bwd(hhat, rs, g_ref[...], du)
        dh0 = dh1_ref[...] + dx

        @pl.when(i >= 3)
        def _():
            out_copy(i - 2, slot).wait()

        buf[slot] = dh0

        @pl.when(i == 0)
        def _():
            dg_ref[...] = dg
            dhead_ref[...] = dh0[:BLOCK]
            if tr > BLOCK:
                first = pltpu.make_async_copy(buf.at[0, pl.ds(BLOCK, tr - BLOCK)], gx_ref.at[pl.ds(0, tr - BLOCK)],
                                              sems.at[0])
                first.start()
                first.wait()

        @pl.when(i >= 1)
        def _():
            dg_ref[...] += dg
            out_copy(i, slot).start()

        @pl.when(i == nt - 1)
        def _():
            if nt >= 3:
                out_copy(nt - 2, (nt - 2) % 2).wait()
            if nt >= 2:
                out_copy(nt - 1, (nt - 1) % 2).wait()

    row = lambda w: pl.BlockSpec((tr, w), lambda i: (i, 0))
    full = lambda shape: pl.BlockSpec(shape, lambda i: (0,) * len(shape))
    return pl.pallas_call(
        body, name="inproj_dgrad", grid=(tp // tr,),
        in_specs=[row(ATTN_WIDTH), row(2 * KV_WIDTH), row(LRU_WIDTH), row(LRU_WIDTH), full(w_in.shape),
                  full(head.shape)] + _seq_specs(tr) + [row(D_MODEL), full(g.shape), full(token.shape)],
        out_specs=[pl.BlockSpec(memory_space=pl.ANY), full((BLOCK, D_MODEL)), full((1, D_MODEL))],
        out_shape=[jax.ShapeDtypeStruct(x.shape, F32), jax.ShapeDtypeStruct((BLOCK, D_MODEL), F32),
                   jax.ShapeDtypeStruct((1, D_MODEL), F32)],
        scratch_shapes=[pltpu.VMEM((2, tr, D_MODEL), F32), pltpu.SemaphoreType.DMA((2,))],
        compiler_params=_params("arbitrary"),
    )(dq, dkv, dxr, dyr, w_in, head, *([x] * qb), dh1, g, token)


def _dense_block_diag(w):
    eye = jnp.eye(LRU_BLOCKS, dtype=w.dtype)
    return (w[:, :, None, :] * eye[:, None, :, None]).reshape(LRU_WIDTH, LRU_WIDTH)


def _diag_blocks(dense):
    d4 = dense.reshape(LRU_BLOCKS, LRU_BLOCK, LRU_BLOCKS, LRU_BLOCK)
    return jnp.stack([d4[n, :, n, :] for n in range(LRU_BLOCKS)])


def _local_step(head, x, tgt, g_pre_mix, w_in, conv_w, conv_b, w_a, b_a, w_x, b_x, lam, sinks, g_post_mix,
                g_pre_ffn, g_post_ffn, late_weights, on_ffn_grads, on_outproj_bwd, on_mixer_grads, token):
    wa = _dense_block_diag(w_a).astype(BF16)
    wx = _dense_block_diag(w_x).astype(BF16)

    u0, qkv, xr, yr, hr, rec = _inproj_lru_fwd(head, x, g_pre_mix, w_in, conv_w, conv_b, wa, b_a, wx, b_x, lam, token)
    attn, probs, sink_probs = _attn_fwd(qkv, sinks)
    w_out, w1, w2 = late_weights([attn, rec])
    mix, h1, u1 = _outproj_fwd(attn, rec, w_out, head, x, g_post_mix, g_pre_ffn)
    r1, dy, df2, loss, dg_post_ffn = _ffn_fwd(u1, w1, w2, h1, tgt, g_post_ffn)

    da1, dh1, dmix, dg_pre_ffn, dg_post_mix = _ffn_bwd_data(df2, r1, w1, w2, dy, h1, mix, g_pre_ffn, g_post_mix)
    dw1, dw2 = _ffn_bwd_weights(u1, da1, r1, df2)
    token2 = on_ffn_grads(dw1, dw2)
    dxr, dyr, dattn, dw_out, dwa, dwx, vec = _outproj_lru_bwd(dmix, w_out, attn, rec, xr, yr, hr, conv_w, conv_b,
                                                              wa, b_a, wx, b_x, lam, token2)
    token3 = on_outproj_bwd(dattn)
    dq, dkv, dkv_extra, dsinks = _attn_bwd(qkv, dattn, probs, sink_probs, token3)
    dkv = _fix_dkv(dkv, dkv_extra)
    dw_in = _inproj_wgrad(dq, dkv, dxr, dyr, u0)
    token4 = on_mixer_grads(dw_in, dw_out)
    dx, dhead, dg_pre_mix = _inproj_dgrad(dq, dkv, dxr, dyr, w_in, head, x, dh1, g_pre_mix, token4)

    grads = dict(
        g_pre_mix=dg_pre_mix, conv_w=vec[0:4], conv_b=vec[4:5], w_a=_diag_blocks(dwa), b_a=vec[5:6],
        w_x=_diag_blocks(dwx), b_x=vec[6:7], lru_lambda=vec[7:8], attn_sinks=dsinks,
        g_post_mix=dg_post_mix, g_pre_ffn=dg_pre_ffn, g_post_ffn=dg_post_ffn)
    return loss, dx, dhead, grads


HBM = pl.BlockSpec(memory_space=pltpu.HBM)


def _mesh_pos():
    return lax.axis_index("x"), lax.axis_index("y"), lax.axis_index("c")


def _other_chips(x, y):
    return [(1 - x, y), (x, 1 - y), (1 - x, 1 - y)]


def _remote(src, dst, send_sem, recv_sem, to):
    return pltpu.make_async_remote_copy(src_ref=src, dst_ref=dst, send_sem=send_sem, recv_sem=recv_sem,
                                        device_id=to, device_id_type=MESH)


def _gather_weights(shards, lands, tiny, tiny_land):
    nbig = len(shards)

    def body(*refs):
        srcs, tiny_src = refs[:nbig], refs[nbig]
        outs, tiny_out = refs[2 * nbig + 2:3 * nbig + 2], refs[3 * nbig + 2]
        ici_send, ici_recv, d2d_send, d2d_recv, tiny_send, tiny_recv = refs[3 * nbig + 3:]
        x, y, c = _mesh_pos()
        me = 2 * x + y
        chips = _other_chips(x, y)
        sibling = (x, y, 1 - c)
        sends = []
        for w, (src, out) in enumerate(zip(srcs, outs)):
            hr = src.shape[0] // 2
            for j, chip in enumerate(chips):
                k = 3 * w + j
                cp = _remote(src.at[pl.ds(c * hr, hr)], out.at[me, pl.ds(c * hr, hr)],
                             ici_send.at[k], ici_recv.at[k], (*chip, c))
                cp.start()
                sends.append(cp)
        for j, chip in enumerate(chips):
            cp = _remote(tiny_src, tiny_out.at[me], tiny_send.at[j], tiny_recv.at[j], (*chip, c))
            cp.start()
            sends.append(cp)
        for w, (src, out) in enumerate(zip(srcs, outs)):
            hr = src.shape[0] // 2
            for j, (px, py) in enumerate(chips):
                k = 3 * w + j
                landed = out.at[2 * px + py, pl.ds(c * hr, hr)]
                _remote(landed, landed, ici_send.at[k], ici_recv.at[k], sibling).wait_recv()
                cp = _remote(landed, landed, d2d_send.at[k], d2d_recv.at[k], sibling)
                cp.start()
                sends.append(cp)
        for w, (src, out) in enumerate(zip(srcs, outs)):
            hr = src.shape[0] // 2
            for j, (px, py) in enumerate(chips):
                k = 3 * w + j
                other = out.at[2 * px + py, pl.ds((1 - c) * hr, hr)]
                _remote(other, other, d2d_send.at[k], d2d_recv.at[k], sibling).wait_recv()
        for j, (px, py) in enumerate(chips):
            blk = tiny_out.at[2 * px + py]
            _remote(blk, blk, tiny_send.at[j], tiny_recv.at[j], sibling).wait_recv()
        for cp in sends:
            cp.wait_send()

    out_shape = [jax.ShapeDtypeStruct(l.shape, l.dtype) for l in list(lands) + [tiny_land]]
    n = 3 * nbig
    return pl.pallas_call(
        body, name="gather_weights", out_shape=out_shape,
        in_specs=[HBM] * (2 * nbig + 2), out_specs=[HBM] * (nbig + 1),
        input_output_aliases={nbig + 1 + i: i for i in range(nbig + 1)},
        scratch_shapes=[pltpu.SemaphoreType.DMA((n,)),
                        pltpu.SemaphoreType.DMA((n,)), pltpu.SemaphoreType.DMA((n,)), pltpu.SemaphoreType.DMA((n,)),
                        pltpu.SemaphoreType.DMA((3,)), pltpu.SemaphoreType.DMA((3,))],
    )(*shards, tiny, *lands, tiny_land)


def _prep_shard(w, me):
    rows, cols = w.shape
    tr = 256 if rows % 256 == 0 else rows

    def body(me_ref, w_ref, s_ref, l_ref):
        b = w_ref[...].astype(BF16)
        s_ref[...] = b
        l_ref[0] = b

    return pl.pallas_call(
        body, name="prep_shard",
        grid_spec=pltpu.PrefetchScalarGridSpec(
            num_scalar_prefetch=1, grid=(rows // tr,),
            in_specs=[pl.BlockSpec((tr, cols), lambda i, me_ref: (i, 0))],
            out_specs=[pl.BlockSpec((tr, cols), lambda i, me_ref: (i, 0)),
                       pl.BlockSpec((1, tr, cols), lambda i, me_ref: (me_ref[0], i, 0))]),
        out_shape=[jax.ShapeDtypeStruct((rows, cols), BF16), jax.ShapeDtypeStruct((N_CHIPS, rows, cols), BF16)],
        compiler_params=_params("parallel"),
    )(me, w)


def _prep_tiny(tiny, me, slots=N_CHIPS):
    def body(me_ref, t_ref, l_ref):
        l_ref[0] = t_ref[...]

    return pl.pallas_call(
        body, name="prep_tiny",
        grid_spec=pltpu.PrefetchScalarGridSpec(
            num_scalar_prefetch=1, grid=(1,),
            in_specs=[pl.BlockSpec(tiny.shape, lambda i, me_ref: (0, 0))],
            out_specs=pl.BlockSpec((1,) + tiny.shape, lambda i, me_ref: (me_ref[0], 0, 0))),
        out_shape=jax.ShapeDtypeStruct((slots,) + tiny.shape, tiny.dtype),
    )(me, tiny)


N_DEV = 8


def _sibling_exchange(parts, token):
    def body(*refs):
        n = len(parts)
        srcs, outs, send_sems, recv_sems = refs[:n], refs[n + 1:2 * n + 1], refs[2 * n + 1], refs[2 * n + 2]
        x, y, c = _mesh_pos()
        sibling = (x, y, 1 - c)
        cps = []
        for w, (src, out) in enumerate(zip(srcs, outs)):
            hr = src.shape[1] // 2
            cp = _remote(src.at[:, pl.ds((1 - c) * hr, hr)], out, send_sems.at[w], recv_sems.at[w], sibling)
            cp.start()
            cps.append(cp)
        for cp in cps:
            cp.wait()

    n = len(parts)
    return pl.pallas_call(
        body, name="sibling_exchange",
        out_shape=[jax.ShapeDtypeStruct((p.shape[0], p.shape[1] // 2, p.shape[2]), p.dtype) for p in parts],
        in_specs=[HBM] * n + [pl.BlockSpec(memory_space=pl.ANY)], out_specs=[HBM] * n,
        scratch_shapes=[pltpu.SemaphoreType.DMA((n,)), pltpu.SemaphoreType.DMA((n,))],
    )(*parts, token)


def _chip_presum(part, from_sibling, pos):
    _, hr, cols = from_sibling.shape
    tr = 256 if hr % 256 == 0 else hr
    steps = hr // tr

    def body(pos_ref, a_ref, b_ref, o_ref, land_ref):
        s = (a_ref[...] + b_ref[...]).astype(BF16)
        o_ref[...] = s

        @pl.when(pl.program_id(1) == pos_ref[1])
        def _():
            land_ref[...] = s

    return pl.pallas_call(
        body, name="chip_presum",
        grid_spec=pltpu.PrefetchScalarGridSpec(
            num_scalar_prefetch=1, grid=(steps, N_CHIPS),
            in_specs=[pl.BlockSpec((1, tr, cols), lambda i, j, p: (j, p[0] * steps + i, 0)),
                      pl.BlockSpec((1, tr, cols), lambda i, j, p: (j, i, 0))],
            out_specs=[pl.BlockSpec((1, tr, cols), lambda i, j, p: (j, i, 0)),
                       pl.BlockSpec((1, tr, cols), lambda i, j, p: (p[1], p[0] * steps + i, 0))]),
        out_shape=[jax.ShapeDtypeStruct(from_sibling.shape, BF16),
                   jax.ShapeDtypeStruct((N_CHIPS, 2 * hr, cols), BF16)],
        compiler_params=_params("arbitrary", "arbitrary"),
    )(pos, part, from_sibling)


def _scatter_partials(cparts, lands, done_cparts=(), done_lands=()):
    n_new = len(cparts)
    nw = n_new + len(done_cparts)

    def body(*refs):
        srcs = refs[:nw]
        outs = refs[2 * nw:3 * nw]
        own_send, own_recv, ici_send, ici_recv, d2d_send, d2d_recv = refs[3 * nw:]
        x, y, c = _mesh_pos()
        me = 2 * x + y
        chips = _other_chips(x, y)
        sibling = (x, y, 1 - c)
        sends = []
        for w in list(range(n_new, nw)) + list(range(n_new)):
            src, out = srcs[w], outs[w]
            hr = src.shape[1]
            mine = out.at[me, pl.ds(c * hr, hr)]
            cp = _remote(src.at[me], mine, own_send.at[w], own_recv.at[w], sibling)
            cp.start()
            sends.append(cp)
            for j, (px, py) in enumerate(chips):
                if w >= n_new:
                    break
                k = 3 * w + j
                cp = _remote(src.at[2 * px + py], mine, ici_send.at[k], ici_recv.at[k], (px, py, c))
                cp.start()
                sends.append(cp)
        for w in list(range(n_new, nw)) + list(range(n_new)):
            src, out = srcs[w], outs[w]
            hr = src.shape[1]
            for j, (px, py) in enumerate(chips):
                k = 3 * w + j
                landed = out.at[2 * px + py, pl.ds(c * hr, hr)]
                if w < n_new:
                    _remote(landed, landed, ici_send.at[k], ici_recv.at[k], sibling).wait_recv()
                cp = _remote(landed, landed, d2d_send.at[k], d2d_recv.at[k], sibling)
                cp.start()
                sends.append(cp)
        for w, (src, out) in enumerate(zip(srcs, outs)):
            hr = src.shape[1]
            other = out.at[me, pl.ds((1 - c) * hr, hr)]
            _remote(other, other, own_send.at[w], own_recv.at[w], sibling).wait_recv()
            for j, (px, py) in enumerate(chips):
                k = 3 * w + j
                other = out.at[2 * px + py, pl.ds((1 - c) * hr, hr)]
                _remote(other, other, d2d_send.at[k], d2d_recv.at[k], sibling).wait_recv()
        for cp in sends:
            cp.wait_send()

    n = 3 * nw
    dma = pltpu.SemaphoreType.DMA
    every = list(cparts) + list(done_cparts)
    every_lands = list(lands) + list(done_lands)
    return pl.pallas_call(
        body, name="scatter_partials",
        out_shape=[jax.ShapeDtypeStruct(l.shape, l.dtype) for l in every_lands],
        in_specs=[HBM] * (2 * nw), out_specs=[HBM] * nw,
        input_output_aliases={nw + i: i for i in range(nw)},
        scratch_shapes=[dma((nw,)), dma((nw,)), dma((n,)), dma((n,)), dma((n,)), dma((n,))],
    )(*every, *every_lands)


SEM = pl.BlockSpec(memory_space=pltpu.SEMAPHORE)
SPLIT_COPY = pltpu.CompilerParams(has_side_effects=pltpu.SideEffectType.DATAFLOW_SIDE_EFFECTING)


def _hbm(a):
    return pltpu.with_memory_space_constraint(a, pltpu.HBM)


def _gather_copies(srcs, lands, send_sems, recv_sems):
    x, y, c = _mesh_pos()
    me = 2 * x + y
    sends, recvs = [], []
    for w, (src, land) in enumerate(zip(srcs, lands)):
        hr = src.shape[0] // 2
        for j, (px, py) in enumerate(_other_chips(x, y)):
            k = 3 * w + j
            sends.append(_remote(src.at[pl.ds(c * hr, hr)], land.at[me, pl.ds(c * hr, hr)],
                                 send_sems.at[k], recv_sems.at[k], (px, py, c)))
            got = land.at[2 * px + py, pl.ds(c * hr, hr)]
            recvs.append(_remote(got, got, send_sems.at[k], recv_sems.at[k], (px, py, c)))
    return sends, recvs


def _scatter_copies(srcs, lands, send_sems, recv_sems):
    x, y, c = _mesh_pos()
    me = 2 * x + y
    sends, recvs = [], []
    for w, (src, land) in enumerate(zip(srcs, lands)):
        hr = src.shape[1]
        for j, (px, py) in enumerate(_other_chips(x, y)):
            k = 3 * w + j
            sends.append(_remote(src.at[2 * px + py], land.at[me, pl.ds(c * hr, hr)],
                                 send_sems.at[k], recv_sems.at[k], (px, py, c)))
            got = land.at[2 * px + py, pl.ds(c * hr, hr)]
            recvs.append(_remote(got, got, send_sems.at[k], recv_sems.at[k], (px, py, c)))
    return sends, recvs


def _sibling_copies(srcs, lands, send_sems, recv_sems):
    x, y, c = _mesh_pos()
    sibling = (x, y, 1 - c)
    sends, recvs = [], []
    for w, (src, land) in enumerate(zip(srcs, lands)):
        hr = src.shape[1] // 2
        sends.append(_remote(src.at[:, pl.ds((1 - c) * hr, hr)], land, send_sems.at[w], recv_sems.at[w], sibling))
        recvs.append(_remote(land, land, send_sems.at[w], recv_sems.at[w], sibling))
    return sends, recvs


def _all_peers_copies(srcs, lands, send_sems, recv_sems):
    x, y, c = _mesh_pos()
    (src,), (land,) = srcs, lands
    flip = lambda v, bit: 1 - v if bit else v
    sends, recvs = [], []
    for k in range(N_DEV - 1):
        px, py, pc = flip(x, (k + 1) & 4), flip(y, (k + 1) & 2), flip(c, (k + 1) & 1)
        sends.append(_remote(src, land.at[4 * x + 2 * y + c], send_sems.at[k], recv_sems.at[k], (px, py, pc)))
        got = land.at[4 * px + 2 * py + pc]
        recvs.append(_remote(got, got, send_sems.at[k], recv_sems.at[k], (px, py, pc)))
    return sends, recvs


def _split_start(name, copies_of, srcs, land_shapes, n_copies=None):
    n = len(srcs)
    k = 3 * n if n_copies is None else n_copies

    def body(*refs):
        src_refs, land_refs = refs[:n], refs[n:2 * n]
        send_sems, recv_sems = refs[2 * n], refs[2 * n + 1]
        token = refs[-1]
        sends, _ = copies_of(src_refs, land_refs, send_sems, recv_sems)
        for cp in sends:
            cp.start()
        token[...] = jnp.zeros_like(token)

    lands = [_hbm(s) for s in land_shapes]
    dma = pltpu.SemaphoreType.DMA
    res = pl.pallas_call(
        body, name=name,
        out_shape=(dma((k,)), dma((k,)), *[pltpu.HBM(s.shape, s.dtype) for s in srcs],
                   *[pltpu.HBM(s.shape, s.dtype) for s in land_shapes], jax.ShapeDtypeStruct((8, 128), F32)),
        in_specs=[HBM] * (2 * n),
        out_specs=(SEM, SEM, *([HBM] * (2 * n)), pl.BlockSpec(memory_space=pltpu.VMEM)),
        input_output_aliases={i: 2 + i for i in range(2 * n)},
        compiler_params=SPLIT_COPY,
    )(*[_hbm(s) for s in srcs], *lands)
    return res[0], res[1], list(res[2:2 + n]), list(res[2 + n:2 + 2 * n]), res[-1]


def _split_wait(name, copies_of, send_sems, recv_sems, srcs, lands, after):
    n = len(srcs)

    def body(*refs):
        src_refs, land_refs = refs[:n], refs[n:2 * n]
        sends, recvs = copies_of(src_refs, land_refs, refs[2 * n], refs[2 * n + 1])
        for cp in sends:
            cp.wait_send()
        for cp in recvs:
            cp.wait_recv()

    res = pl.pallas_call(
        body, name=name,
        out_shape=tuple(pltpu.HBM(s.shape, s.dtype) for s in list(srcs) + list(lands)),
        in_specs=[HBM] * (2 * n) + [SEM, SEM] + [pl.BlockSpec(memory_space=pl.ANY)] * len(after),
        out_specs=tuple([HBM] * (2 * n)),
        input_output_aliases={i: i for i in range(2 * n)},
        compiler_params=SPLIT_COPY,
    )(*srcs, *lands, send_sems, recv_sems, *after)
    return list(res[:n]), list(res[n:])


def _gather_finish(lands):
    n = len(lands)

    def body(*refs):
        outs = refs[n:2 * n]
        d2d_send, d2d_recv = refs[2 * n:]
        x, y, c = _mesh_pos()
        chips = _other_chips(x, y)
        sibling = (x, y, 1 - c)
        sends = []
        for w, out in enumerate(outs):
            hr = out.shape[1] // 2
            for j, (px, py) in enumerate(chips):
                landed = out.at[2 * px + py, pl.ds(c * hr, hr)]
                cp = _remote(landed, landed, d2d_send.at[3 * w + j], d2d_recv.at[3 * w + j], sibling)
                cp.start()
                sends.append(cp)
        for w, out in enumerate(outs):
            hr = out.shape[1] // 2
            for j, (px, py) in enumerate(chips):
                other = out.at[2 * px + py, pl.ds((1 - c) * hr, hr)]
                _remote(other, other, d2d_send.at[3 * w + j], d2d_recv.at[3 * w + j], sibling).wait_recv()
        for cp in sends:
            cp.wait_send()

    dma = pltpu.SemaphoreType.DMA
    return pl.pallas_call(
        body, name="gather_finish",
        out_shape=[jax.ShapeDtypeStruct(l.shape, l.dtype) for l in lands],
        in_specs=[HBM] * n, out_specs=[HBM] * n,
        input_output_aliases={i: i for i in range(n)},
        scratch_shapes=[dma((3 * n,)), dma((3 * n,))],
    )(*lands)


def _adamw(w, g, m, v):
    m = ADAM_B1 * m + (1.0 - ADAM_B1) * g
    v = ADAM_B2 * v + (1.0 - ADAM_B2) * (g * g)
    m_hat = m / (1.0 - ADAM_B1 ** ADAM_STEP)
    v_hat = v / (1.0 - ADAM_B2 ** ADAM_STEP)
    delta = -ADAM_LR * (m_hat / (jnp.sqrt(v_hat) + ADAM_EPS) + ADAM_WD * w)
    return delta, m, v


def _adamw_big(partials, w, m, v):
    rows, cols = w.shape
    tr = 256 if rows % 256 == 0 else rows

    def body(p_ref, w_ref, m_ref, v_ref, g_ref, d_ref, m2_ref, v2_ref):
        g = ((p_ref[0].astype(F32) + p_ref[1].astype(F32)) + p_ref[2].astype(F32)) + p_ref[3].astype(F32)
        g_ref[...] = g
        d_ref[...], m2_ref[...], v2_ref[...] = _adamw(w_ref[...], g, m_ref[...], v_ref[...])

    blk = pl.BlockSpec((tr, cols), lambda i: (i, 0))
    return pl.pallas_call(
        body, name="adamw_big", grid=(rows // tr,),
        in_specs=[pl.BlockSpec((N_CHIPS, tr, cols), lambda i: (0, i, 0)), blk, blk, blk],
        out_specs=[blk] * 4, out_shape=[jax.ShapeDtypeStruct((rows, cols), F32)] * 4,
        compiler_params=_params("parallel"),
    )(partials, w, m, v)


def _sum_devices(gathered, rows):
    cols = gathered.shape[1]

    def body(g_ref, o_ref):
        acc = g_ref[0:rows]
        for d in range(1, N_DEV):
            acc = acc + g_ref[d * rows:(d + 1) * rows]
        o_ref[...] = acc

    return pl.pallas_call(
        body, name="sum_devices", out_shape=jax.ShapeDtypeStruct((rows, cols), F32),
        in_specs=[pl.BlockSpec(memory_space=pltpu.VMEM)], out_specs=pl.BlockSpec(memory_space=pltpu.VMEM),
        compiler_params=pltpu.CompilerParams(vmem_limit_bytes=VMEM_LIMIT_V7X),
    )(gathered)


def _adamw_small(quads):
    n = len(quads)

    def body(*refs):
        ins, outs = refs[:4 * n], refs[4 * n:]
        for t in range(n):
            w, g, m, v = (r[...] for r in ins[4 * t:4 * t + 4])
            outs[3 * t][...], outs[3 * t + 1][...], outs[3 * t + 2][...] = _adamw(w, g, m, v)

    flat = [a for q in quads for a in q]
    vm = pl.BlockSpec(memory_space=pltpu.VMEM)
    res = pl.pallas_call(
        body, name="adamw_small",
        out_shape=[jax.ShapeDtypeStruct(q[0].shape, F32) for q in quads for _ in range(3)],
        in_specs=[vm] * (4 * n), out_specs=[vm] * (3 * n),
    )(*flat)
    return [tuple(res[3 * t:3 * t + 3]) for t in range(n)]


SMALL_PACK_ROWS = 96
META_COLS = D_MODEL // N_CHIPS
CONV_COLS = LRU_WIDTH // N_CHIPS
_WEIGHTS = ['meta_tokens', 'g_pre_mix', 'w_in', 'conv_w', 'conv_b', 'w_a', 'b_a', 'w_x', 'b_x', 'lru_lambda',
            'attn_sinks', 'w_out', 'g_post_mix', 'g_pre_ffn', 'w_ff1', 'w_ff2', 'g_post_ffn']
_BIG = ['w_in', 'w_out', 'w_ff1', 'w_ff2']


def _pack_small(dmeta, g, loss):
    z = lambda r, c: jnp.zeros((r, c), F32)
    rows = [
        dmeta,
        g['g_pre_mix'], g['g_post_mix'], g['g_pre_ffn'], g['g_post_ffn'],
        jnp.concatenate([g['conv_w'], z(4, 512)], axis=1),
        jnp.concatenate([g['conv_b'], g['b_a']], axis=1),
        jnp.concatenate([g['b_x'], g['lru_lambda']], axis=1),
        jnp.concatenate([g['attn_sinks'], z(1, D_MODEL - ATTN_HEADS)], axis=1),
        jnp.concatenate([loss, z(1, D_MODEL - 1)], axis=1),
        z(4, D_MODEL),
        g['w_a'].reshape(32, D_MODEL), g['w_x'].reshape(32, D_MODEL),
    ]
    return jnp.concatenate(rows, axis=0)


def _unpack_small(s, chip):
    return dict(
        meta_tokens=lax.dynamic_slice(s[0:N_META], (0, chip * META_COLS), (N_META, META_COLS)),
        g_pre_mix=s[16:17], g_post_mix=s[17:18], g_pre_ffn=s[18:19], g_post_ffn=s[19:20],
        conv_w=lax.dynamic_slice(s[20:24], (0, chip * CONV_COLS), (4, CONV_COLS)).reshape(1, 4, CONV_COLS),
        conv_b=s[24:25, :512], b_a=s[24:25, 512:], b_x=s[25:26, :512], lru_lambda=s[25:26, 512:],
        attn_sinks=s[26:27, :ATTN_HEADS], loss=s[27, 0],
        w_a=s[32:64].reshape(1, LRU_BLOCKS, LRU_BLOCK, LRU_BLOCK),
        w_x=s[64:96].reshape(1, LRU_BLOCKS, LRU_BLOCK, LRU_BLOCK))


def _as2d(a):
    if a.ndim == 2:
        return a
    return a.reshape(-1, a.shape[-1])


def kernel(x, meta_tokens, g_pre_mix, w_in, conv_w, conv_b, w_a, b_a, w_x, b_x, lru_lambda, attn_sinks, w_out, g_post_mix, g_pre_ffn, w_ff1, w_ff2, g_post_ffn, loss_target, m_meta_tokens, m_g_pre_mix, m_w_in, m_conv_w, m_conv_b, m_w_a, m_b_a, m_w_x, m_b_x, m_lru_lambda, m_attn_sinks, m_w_out, m_g_post_mix, m_g_pre_ffn, m_w_ff1, m_w_ff2, m_g_post_ffn, v_meta_tokens, v_g_pre_mix, v_w_in, v_conv_w, v_conv_b, v_w_a, v_b_a, v_w_x, v_b_x, v_lru_lambda, v_attn_sinks, v_w_out, v_g_post_mix, v_g_pre_ffn, v_w_ff1, v_w_ff2, v_g_post_ffn):
    weights = dict(meta_tokens=meta_tokens, g_pre_mix=g_pre_mix, w_in=w_in, conv_w=conv_w, conv_b=conv_b, w_a=w_a,
                   b_a=b_a, w_x=w_x, b_x=b_x, lru_lambda=lru_lambda, attn_sinks=attn_sinks, w_out=w_out,
                   g_post_mix=g_post_mix, g_pre_ffn=g_pre_ffn, w_ff1=w_ff1, w_ff2=w_ff2, g_post_ffn=g_post_ffn)
    mom1 = dict(zip(_WEIGHTS, [m_meta_tokens, m_g_pre_mix, m_w_in, m_conv_w, m_conv_b, m_w_a, m_b_a, m_w_x, m_b_x,
                               m_lru_lambda, m_attn_sinks, m_w_out, m_g_post_mix, m_g_pre_ffn, m_w_ff1, m_w_ff2,
                               m_g_post_ffn]))
    mom2 = dict(zip(_WEIGHTS, [v_meta_tokens, v_g_pre_mix, v_w_in, v_conv_w, v_conv_b, v_w_a, v_b_a, v_w_x, v_b_x,
                               v_lru_lambda, v_attn_sinks, v_w_out, v_g_post_mix, v_g_pre_ffn, v_w_ff1, v_w_ff2,
                               v_g_post_ffn]))
    xi, yi, ci = _mesh_pos()
    chip = 2 * xi + yi

    tiny = jnp.concatenate([meta_tokens, jnp.pad(conv_w[0], ((0, 4), (0, 128)))], axis=0)
    chip_arr = jnp.reshape(chip, (1,)).astype(jnp.int32)
    big2d = lambda a, name: a[0].T if name == 'w_in' else a[0]
    shards, lands = zip(*[_prep_shard(big2d(weights[n], n), chip_arr) for n in _BIG])
    g_in, g_tiny = _gather_weights(shards[:1], lands[:1], tiny, _prep_tiny(tiny, chip_arr))
    w_in_full = g_in.reshape(IN_WIDTH, D_MODEL)
    meta_full = jnp.concatenate([g_tiny[j, :N_META] for j in range(N_CHIPS)], axis=1)
    conv_w_full = jnp.concatenate([g_tiny[j, N_META:N_META + 4, :128] for j in range(N_CHIPS)], axis=1)
    g_send, g_recv, late_thru, late_lands, token = _split_start(
        "gather_late_start", _gather_copies, shards[1:], lands[1:])

    def late_weights(after):
        _, landed = _split_wait("gather_late_wait", _gather_copies, g_send, g_recv, late_thru, late_lands, after)
        g_out, g_f1, g_f2 = _gather_finish(landed)
        return g_out.reshape(D_MODEL, D_MODEL), g_f1, g_f2

    pos = jnp.stack([ci, chip]).astype(jnp.int32)
    ffn = {}


    def on_ffn_grads(dw1, dw2):
        parts = [dw1, dw2]
        lands = [lax.empty((p.shape[0], p.shape[1] // 2, p.shape[2]), p.dtype) for p in parts]
        ffn['sib'] = _split_start("sibling_ffn_start", _sibling_copies, parts, lands, len(parts))
        return ffn['sib'][4]

    def on_outproj_bwd(dattn):
        send, recv, thru, lands, _ = ffn['sib']
        parts, from_sibling = _split_wait("sibling_ffn_wait", _sibling_copies, send, recv, thru, lands, [dattn])
        cparts_ffn, lands_ffn = zip(*[_chip_presum(p, r, pos) for p, r in zip(parts, from_sibling)])
        ffn['send'], ffn['recv'], ffn['thru'], ffn['lands'], token3 = _split_start(
            "scatter_ffn_start", _scatter_copies, cparts_ffn, lands_ffn)
        return token3

    def on_mixer_grads(dw_in, dw_out):
        parts = [dw_in.reshape(N_CHIPS, IN_WIDTH // N_CHIPS, D_MODEL),
                 dw_out.reshape(N_CHIPS, D_MODEL // N_CHIPS, D_MODEL)]
        cparts, lands = zip(*[_chip_presum(p, r, pos) for p, r in zip(parts, _sibling_exchange(parts, pos))])
        ffn['mixer'] = _split_start("scatter_mixer_start", _scatter_copies, cparts, lands)
        return ffn['mixer'][4]

    head = jnp.concatenate([jnp.zeros((PAD_ROWS, D_MODEL), F32), meta_full], axis=0)
    loss, dx, dhead, grads = _local_step(head, x[0], loss_target[0], g_pre_mix, w_in_full, conv_w_full, conv_b, w_a[0],
                                         b_a, w_x[0], b_x, lru_lambda, attn_sinks, g_post_mix, g_pre_ffn, g_post_ffn,
                                         late_weights, on_ffn_grads, on_outproj_bwd, on_mixer_grads, token)
    grad_x = dx[None]

    pack = _pack_small(dhead[PAD_ROWS:], grads, loss)
    dev = jnp.reshape(4 * xi + 2 * yi + ci, (1,)).astype(jnp.int32)
    s_send, s_recv, s_thru, s_lands, token5 = _split_start(
        "gather_small_start", _all_peers_copies, [pack], [_prep_tiny(pack, dev, N_DEV)], N_DEV - 1)

    send, recv, thru, lands, _ = ffn['mixer']
    mixer_cparts, mixer_lands = _split_wait("scatter_mixer_wait", _scatter_copies, send, recv, thru, lands, [token5])
    ffn_cparts, ffn_lands = _split_wait("scatter_ffn_wait", _scatter_copies, ffn['send'], ffn['recv'], ffn['thru'],
                                        ffn['lands'], mixer_lands)
    chip_partials = _scatter_partials([], [], mixer_cparts + ffn_cparts, mixer_lands + ffn_lands)

    g_out_d, delta, new_m, new_v = {}, {}, {}, {}
    for name, part in zip(_BIG, chip_partials):
        shp = weights[name].shape
        res = _adamw_big(part, big2d(weights[name], name), big2d(mom1[name], name), big2d(mom2[name], name))
        g_out_d[name], delta[name], new_m[name], new_v[name] = (big2d(r[None], name).reshape(shp) for r in res)

    _, (gathered,) = _split_wait("gather_small_wait", _all_peers_copies, s_send, s_recv, s_thru, s_lands,
                                 [g_out_d[n] for n in _BIG])
    small = _unpack_small(_sum_devices(gathered.reshape(N_DEV * SMALL_PACK_ROWS, D_MODEL), SMALL_PACK_ROWS), chip)
    loss = small['loss']
    small_names = [n for n in _WEIGHTS if n not in _BIG]
    quads = [(_as2d(weights[n]), _as2d(small[n]), _as2d(mom1[n]), _as2d(mom2[n])) for n in small_names]
    for name, (d, m2, v2) in zip(small_names, _adamw_small(quads)):
        shp = weights[name].shape
        g_out_d[name] = small[name].reshape(shp)
        delta[name], new_m[name], new_v[name] = d.reshape(shp), m2.reshape(shp), v2.reshape(shp)

    return (loss, grad_x, *[g_out_d[n] for n in _WEIGHTS], *[delta[n] for n in _WEIGHTS],
            *[new_m[n] for n in _WEIGHTS], *[new_v[n] for n in _WEIGHTS])
```

```python
import numpy as np
import jax
import jax.numpy as jnp
from jax import lax
from jax.experimental import pallas as pl
from jax.experimental.pallas import tpu as pltpu

F32 = jnp.float32
BF16 = jnp.bfloat16

D_MODEL = 1024
N_META = 16
BLOCK = 128
PAD_ROWS = BLOCK - N_META
HEAD_DIM = 64
ATTN_HEADS = 8
GQA_GROUP = 4
ATTN_WIDTH = 512
KV_WIDTH = 128
QKV_WIDTH = ATTN_WIDTH + 2 * KV_WIDTH
LRU_WIDTH = 512
LRU_BLOCKS = 8
LRU_BLOCK = 64
LRU_C = 8.0
IN_WIDTH = 1792
D_FF = 4096
N_CHIPS = 4
FF_CHUNK = D_FF // N_CHIPS
EPS = 1e-6
NEG = -1e30

ADAM_LR = 0.001
ADAM_B1 = 0.9
ADAM_B2 = 0.999
ADAM_EPS = 1e-08
ADAM_WD = 0.01
ADAM_STEP = 10

VMEM_LIMIT_V7X = 62 * 1024 * 1024
MESH = pl.DeviceIdType.MESH

NT = (((1,), (1,)), ((), ()))
TN = (((0,), (0,)), ((), ()))


def _row_tile(tp):
    return 640 if tp % 640 == 0 else BLOCK


def _wgrad_row_tile(tp):
    return 1664 if tp % 1664 == 0 else _row_tile(tp)


def _params(*sem):
    return pltpu.CompilerParams(dimension_semantics=sem, vmem_limit_bytes=VMEM_LIMIT_V7X)


def _dot(a, b):
    return jnp.dot(a, b, preferred_element_type=F32)


def _dot_nt(a, b):
    return lax.dot_general(a, b, NT, preferred_element_type=F32)


def _dot_tn(a, b):
    return lax.dot_general(a, b, TN, preferred_element_type=F32)


def _rms(x):
    rs = lax.rsqrt(jnp.mean(x * x, axis=-1, keepdims=True) + EPS)
    return x * rs, rs


def _rms_bwd(xhat, rs, g, dy):
    dyg = dy * g
    dx = rs * (dyg - xhat * jnp.mean(dyg * xhat, axis=-1, keepdims=True))
    dg = jnp.sum(dy * xhat, axis=0, keepdims=True)
    return dx, dg


def _gelu(x):
    k = 0.7978845608028654
    t = jnp.tanh(x * (k + (k * 0.044715) * (x * x)))
    return (0.5 * x) * (1.0 + t), t


def _gelu_grad(x, t):
    k = 0.7978845608028654
    return 0.5 * (1.0 + t) + 0.5 * x * (1.0 - t * t) * k * (1.0 + 3 * 0.044715 * x * x)


def _sigmoid(x):
    return 0.5 * jnp.tanh(0.5 * x) + 0.5


def _one_minus_exp2(y):
    t = jnp.tanh(y)
    return (-2.0 * t) / (1.0 - t)


def _softplus(x):
    return jnp.maximum(x, 0.0) + jnp.log1p(jnp.exp(-jnp.abs(x)))


def _seq_specs(tr, delay=0):
    qb = tr // BLOCK
    tile = lambda i: jnp.maximum(i - delay, 0)
    return [pl.BlockSpec((BLOCK, D_MODEL), lambda i, *_, s=s: (jnp.maximum(tile(i) * qb + s - 1, 0), 0))
            for s in range(qb)]


def _seq_tile(head, pieces, i):
    first = jnp.where(i == 0, head, pieces[0][...])
    return jnp.concatenate([first] + [p[...] for p in pieces[1:]], axis=0)


GROUP_ROWS = GQA_GROUP * BLOCK


def _attn_bias():
    j = np.arange(2 * BLOCK)[:, None]
    i = np.arange(BLOCK)[None, :]
    band = (j - i >= 1) & (j - i <= BLOCK)
    out = []
    for n in range(3):
        ok = band & ((n - 1) * BLOCK + j >= PAD_ROWS) if n < 2 else band
        out.append(np.tile(np.where(ok, 0.0, NEG).astype(np.float32), (1, GQA_GROUP)))
    return jnp.asarray(np.stack(out))


def _heads_t(at, g):
    heads = range(GQA_GROUP * g, GQA_GROUP * (g + 1))
    return jnp.concatenate([at[h * HEAD_DIM:(h + 1) * HEAD_DIM] for h in heads], axis=1).astype(BF16)


def _from_heads_t(groups):
    pairs = []
    for p in groups:
        for h in range(0, GQA_GROUP, 2):
            two = jnp.concatenate([p[:, h * BLOCK:(h + 1) * BLOCK], p[:, (h + 1) * BLOCK:(h + 2) * BLOCK]], axis=0)
            pairs.append(two.T)
    return jnp.concatenate(pairs, axis=1)


def _stack_heads(a, g):
    heads = range(GQA_GROUP * g, GQA_GROUP * (g + 1))
    return jnp.concatenate([a[:, h * HEAD_DIM:(h + 1) * HEAD_DIM] for h in heads], axis=0)


def _unstack_heads(groups):
    return jnp.concatenate([p[h * BLOCK:(h + 1) * BLOCK] for p in groups for h in range(GQA_GROUP)], axis=1)


def _attn_probs_t(k_g, qg, bias, sink_row):
    st = _dot_nt(k_g, qg) + bias
    m = jnp.maximum(jnp.max(st, axis=0, keepdims=True), sink_row)
    p = jnp.exp(st - m)
    es = jnp.exp(sink_row - m)
    inv = 1.0 / (jnp.sum(p, axis=0, keepdims=True) + es)
    return p * inv, es * inv


def _attn_consts(sinks):
    return jnp.repeat(sinks.reshape(ATTN_HEADS), BLOCK).reshape(ATTN_HEADS // GQA_GROUP, GROUP_ROWS), _attn_bias()


_SINK_SPEC = pl.BlockSpec((ATTN_HEADS // GQA_GROUP, GROUP_ROWS), lambda n: (0, 0))
_BIAS_SPEC = pl.BlockSpec((3, 2 * BLOCK, GROUP_ROWS), lambda n: (0, 0, 0))
_QSCALE = HEAD_DIM ** -0.5


def _kv_specs(tr):
    qb = tr // BLOCK
    prev = lambda col: pl.BlockSpec((BLOCK, KV_WIDTH), lambda t: (jnp.maximum(t * qb - 1, 0), col))
    cur = lambda col: pl.BlockSpec((tr, KV_WIDTH), lambda t: (t, col))
    return [prev(4), cur(4), prev(5), cur(5)]


def _block_bias(b_ref, t, qb, i):
    return b_ref[2] if i >= 2 else b_ref[jnp.minimum(t * qb + i, 2)]


N_KV = ATTN_HEADS // GQA_GROUP


def _prob_specs(qb):
    return [pl.BlockSpec((qb, N_KV, 2 * BLOCK, GROUP_ROWS), lambda t: (t, 0, 0, 0)),
            pl.BlockSpec((qb, SUBLANES, GROUP_ROWS), lambda t: (t, 0, 0))]


def _attn_fwd(qkv, sinks):
    tp = qkv.shape[0]
    tr = _attn_tile(tp)
    qb, nb = tr // BLOCK, tp // BLOCK
    sink_rows, bias = _attn_consts(sinks)

    def body(s_ref, b_ref, q_ref, kp_ref, kc_ref, vp_ref, vc_ref, o_ref, p_ref, ps_ref):
        t = pl.program_id(0)
        k_all = jnp.concatenate([kp_ref[...], kc_ref[...]], axis=0)
        v_all = jnp.concatenate([vp_ref[...], vc_ref[...]], axis=0)
        for i in range(qb):
            rows = slice(i * BLOCK, (i + 1) * BLOCK)
            q = q_ref[rows]
            k2, v2 = k_all[i * BLOCK:(i + 2) * BLOCK], v_all[i * BLOCK:(i + 2) * BLOCK]
            bias_n = _block_bias(b_ref, t, qb, i)
            outs, sink_probs = [], []
            for g in range(N_KV):
                cols = slice(g * HEAD_DIM, (g + 1) * HEAD_DIM)
                qg = _stack_heads(q, g) * jnp.asarray(_QSCALE, BF16)
                p, ps = _attn_probs_t(k2[:, cols], qg, bias_n, s_ref[g:g + 1])
                pb = p.astype(BF16)
                p_ref[i, g] = pb
                sink_probs.append(ps)
                outs.append(_dot_tn(pb, v2[:, cols]))
            o_ref[rows] = _unstack_heads(outs).astype(BF16)
            ps_ref[i] = jnp.concatenate(sink_probs + [jnp.zeros((SUBLANES - N_KV, GROUP_ROWS), F32)], axis=0)

    return pl.pallas_call(
        body, name="attn_fwd", grid=(tp // tr,),
        in_specs=[_SINK_SPEC, _BIAS_SPEC, pl.BlockSpec((tr, ATTN_WIDTH), lambda t: (t, 0))] + _kv_specs(tr),
        out_specs=[pl.BlockSpec((tr, ATTN_WIDTH), lambda t: (t, 0))] + _prob_specs(qb),
        out_shape=[jax.ShapeDtypeStruct((tp, ATTN_WIDTH), BF16),
                   jax.ShapeDtypeStruct((nb, N_KV, 2 * BLOCK, GROUP_ROWS), BF16),
                   jax.ShapeDtypeStruct((nb, SUBLANES, GROUP_ROWS), F32)],
        compiler_params=_params("parallel"),
    )(sink_rows, bias, qkv, qkv, qkv, qkv, qkv)


def _conv_taps(x, halo):
    ext = jnp.concatenate([halo, x], axis=0)
    return [ext[8:] if k == 3 else pltpu.roll(ext, 3 - k, 0)[8:] for k in range(4)]


def _lru_gates(xc, wa, ba, wx, bx, sp):
    xb = xc.astype(BF16)
    r = _sigmoid(_dot(xb, wa) + ba)
    ig = _sigmoid(_dot(xb, wx) + bx)
    log_a = (-LRU_C * sp) * r
    a = jnp.exp(log_a)
    mult = jnp.sqrt(_one_minus_exp2(log_a))
    return xb, r, ig, a, mult


SUBLANES = 8


def _scan_fwd(a, b, h_in):
    n, width = a.shape
    a, b = (v.reshape(n // SUBLANES, SUBLANES, width) for v in (a, b))
    in_group = lax.broadcasted_iota(jnp.int32, a.shape, 1)
    for d in (1, 2, 4):
        keep = in_group >= d
        b = jnp.where(keep, a * pltpu.roll(b, d, 1) + b, b)
        a = jnp.where(keep, a * pltpu.roll(a, d, 1), a)
    a, b = a.reshape(n, width), b.reshape(n, width)
    out, carry = [], h_in
    for g in range(0, n, SUBLANES):
        h = a[g:g + SUBLANES] * carry + b[g:g + SUBLANES]
        out.append(h)
        carry = h[SUBLANES - 1:]
    return jnp.concatenate(out, axis=0)


def _scan_rev(c, b, g_in):
    n, width = c.shape
    c, b = (v.reshape(n // SUBLANES, SUBLANES, width) for v in (c, b))
    in_group = lax.broadcasted_iota(jnp.int32, c.shape, 1)
    for d in (1, 2, 4):
        keep = in_group < SUBLANES - d
        b = jnp.where(keep, b + c * pltpu.roll(b, SUBLANES - d, 1), b)
        c = jnp.where(keep, c * pltpu.roll(c, SUBLANES - d, 1), c)
    c, b = c.reshape(n, width), b.reshape(n, width)
    out, carry = [], g_in
    for g in range(n - SUBLANES, -1, -SUBLANES):
        r = b[g:g + SUBLANES] + c[g:g + SUBLANES] * carry
        out.append(r)
        carry = r[:1]
    return jnp.concatenate(out[::-1], axis=0)


def _inproj_lru_fwd(head, x, g, w_in, conv_w, conv_b, wa, ba, wx, bx, lam, token):
    tp = BLOCK + x.shape[0]
    tr = _row_tile(tp)
    qb, nt = tr // BLOCK, tp // tr
    small = [conv_w, conv_b, wa, ba, wx, bx, lam]

    def body(*refs):
        head_ref, pieces = refs[0], refs[1:1 + qb]
        g_ref, w_ref, _, cw_ref, cb_ref, wa_ref, ba_ref, wx_ref, bx_ref, lam_ref = refs[1 + qb:11 + qb]
        u_ref, qkv_ref, xr_ref, yr_ref, hr_ref, rec_ref, zbuf, halo, hprev = refs[11 + qb:]
        i = pl.program_id(0)
        cur = i % 2

        @pl.when(i == 0)
        def _():
            halo[...] = jnp.zeros_like(halo)
            hprev[...] = jnp.zeros_like(hprev)
            zbuf[1] = jnp.zeros((tr, 2 * LRU_WIDTH), F32)

        def recurrent_branch(valid):
            cw, cb = cw_ref[...], cb_ref[...]
            wa_m, ba_v, wx_m, bx_v = wa_ref[...], ba_ref[...], wx_ref[...], bx_ref[...]
            sp = _softplus(-lam_ref[...])
            before, h_last = halo[...], hprev[0:1]
            for b in range(qb):
                rows = slice(b * BLOCK, (b + 1) * BLOCK)
                xy = zbuf[1 - cur, rows]
                xin = xy[:, :LRU_WIDTH]
                taps = _conv_taps(xin, before)
                before = xin[BLOCK - 8:]
                xc = cb + sum(cw[k:k + 1] * taps[k] for k in range(4))
                _, _, ig, a, mult = _lru_gates(xc, wa_m, ba_v, wx_m, bx_v, sp)
                u = mult * (ig * xc)
                if b == 0:
                    pos = (i - 1) * tr + lax.broadcasted_iota(jnp.int32, xc.shape, 0)
                    u = jnp.where(pos >= PAD_ROWS, u, 0.0)
                h = _scan_fwd(a, u, h_last)
                h_last = h[BLOCK - 1:]
                hr_ref[rows] = h
                gl, _ = _gelu(xy[:, LRU_WIDTH:])
                rec_ref[rows] = (gl * h).astype(BF16)
            halo[...] = jnp.where(valid, before, 0.0)
            hprev[0:1] = jnp.where(valid, h_last, 0.0)

        def projection():
            xhat, _ = _rms(_seq_tile(head_ref[...], pieces, i))
            u = (xhat * g_ref[...]).astype(BF16)
            u_ref[...] = u
            z = _dot_nt(u, w_ref[...])
            qkv_ref[...] = z[:, :QKV_WIDTH].astype(BF16)
            xr_ref[...] = z[:, QKV_WIDTH:QKV_WIDTH + LRU_WIDTH]
            yr_ref[...] = z[:, QKV_WIDTH + LRU_WIDTH:]
            zbuf[cur] = z[:, QKV_WIDTH:]

        @pl.when(i < nt)
        def _():
            recurrent_branch(i >= 1)
            projection()

        @pl.when(i == nt)
        def _():
            recurrent_branch(True)

    last = nt - 1
    this_row = lambda w: pl.BlockSpec((tr, w), lambda i: (jnp.minimum(i, last), 0))
    prev_row = lambda w: pl.BlockSpec((tr, w), lambda i: (jnp.maximum(i - 1, 0), 0))
    full = lambda a: pl.BlockSpec(a.shape, lambda i: (0,) * a.ndim)
    piece_specs = [pl.BlockSpec((BLOCK, D_MODEL), lambda i, s=s: (jnp.maximum(jnp.minimum(i, last) * qb + s - 1, 0), 0))
                   for s in range(qb)]
    return pl.pallas_call(
        body, name="inproj_lru_fwd", grid=(nt + 1,),
        in_specs=[full(head)] + piece_specs + [full(g), full(w_in), full(token)] + [full(a) for a in small],
        out_specs=[this_row(D_MODEL), this_row(QKV_WIDTH), this_row(LRU_WIDTH), this_row(LRU_WIDTH),
                   prev_row(LRU_WIDTH), prev_row(LRU_WIDTH)],
        out_shape=[jax.ShapeDtypeStruct((tp, D_MODEL), BF16), jax.ShapeDtypeStruct((tp, QKV_WIDTH), BF16),
                   jax.ShapeDtypeStruct((tp, LRU_WIDTH), F32), jax.ShapeDtypeStruct((tp, LRU_WIDTH), F32),
                   jax.ShapeDtypeStruct((tp, LRU_WIDTH), F32), jax.ShapeDtypeStruct((tp, LRU_WIDTH), BF16)],
        scratch_shapes=[pltpu.VMEM((2, tr, 2 * LRU_WIDTH), F32), pltpu.VMEM((8, LRU_WIDTH), F32),
                        pltpu.VMEM((8, LRU_WIDTH), F32)],
        compiler_params=_params("arbitrary"),
    )(head, *([x] * qb), g, w_in, token, *small)


def _outproj_fwd(attn, rec, w_out, head, x, g_post_mix, g_pre_ffn):
    tp = attn.shape[0]
    tr = _row_tile(tp)
    qb = tr // BLOCK

    def body(*refs):
        a_ref, r_ref, w_ref, head_ref = refs[:4]
        pieces = refs[4:4 + qb]
        gm_ref, gf_ref, mix_ref, h1_ref, u1_ref = refs[4 + qb:]
        mix = _dot(a_ref[...], w_ref[:ATTN_WIDTH]) + _dot(r_ref[...], w_ref[ATTN_WIDTH:])
        mix_ref[...] = mix
        mhat, _ = _rms(mix)
        h1 = _seq_tile(head_ref[...], pieces, pl.program_id(0)) + mhat * gm_ref[...]
        h1_ref[...] = h1
        hhat, _ = _rms(h1)
        u1_ref[...] = (hhat * gf_ref[...]).astype(BF16)

    row = lambda w: pl.BlockSpec((tr, w), lambda i: (i, 0))
    full = lambda a: pl.BlockSpec(a.shape, lambda i: (0,) * a.ndim)
    return pl.pallas_call(
        body, name="outproj_fwd", grid=(tp // tr,),
        in_specs=[row(ATTN_WIDTH), row(LRU_WIDTH), full(w_out), full(head)] + _seq_specs(tr)
        + [full(g_post_mix), full(g_pre_ffn)],
        out_specs=[row(D_MODEL), row(D_MODEL), row(D_MODEL)],
        out_shape=[jax.ShapeDtypeStruct((tp, D_MODEL), F32), jax.ShapeDtypeStruct((tp, D_MODEL), F32),
                   jax.ShapeDtypeStruct((tp, D_MODEL), BF16)],
        compiler_params=_params("parallel"),
    )(attn, rec, w_out, head, *([x] * qb), g_post_mix, g_pre_ffn)


FFN_STEPS = N_CHIPS


def _resident(a):
    return pl.BlockSpec(a.shape, lambda *_: (0,) * a.ndim, pipeline_mode=pl.Buffered(1))


def _ffn_fwd(u1, w1, w2, h1, tgt, g_post_ffn):
    tp = h1.shape[0]
    tr = _row_tile(tp)
    qb, nt = tr // BLOCK, tp // tr
    sr = tr // FFN_STEPS

    def body(*refs):
        u_ref, w1_ref, w2_ref, h1_ref = refs[:4]
        t_pieces = refs[4:4 + qb]
        g_ref, r1_ref, dy_ref, df2_ref, loss_ref, dg_ref, acc = refs[4 + qb:]
        i, c = pl.program_id(0), pl.program_id(1)
        cur = i % 2

        @pl.when((i == 0) & (c == 0))
        def _():
            loss_ref[...] = jnp.zeros_like(loss_ref)
            dg_ref[...] = jnp.zeros_like(dg_ref)
            acc[1] = jnp.zeros((tr, D_MODEL), F32)

        def matmuls():
            r = jnp.maximum(_dot(u_ref[...], w1_ref[c]), 0.0)
            r1_ref[...] = r.astype(BF16)
            return _dot((r * r).astype(BF16), w2_ref[c])

        def finish_previous_tile(k, valid):
            lo, hi = k * sr, (k + 1) * sr
            g = g_ref[...]
            fhat, rs = _rms(acc[1 - cur, lo:hi])
            h2 = h1_ref[...] + fhat * g
            rows = (i - 1) * tr + lo + lax.broadcasted_iota(jnp.int32, h2.shape, 0)
            tgt = jnp.concatenate([p[max(lo - s * BLOCK, 0):min(hi - s * BLOCK, BLOCK)] for s, p in enumerate(t_pieces)
                                   if lo < (s + 1) * BLOCK and hi > s * BLOCK], axis=0)
            err = jnp.where((rows >= BLOCK) & valid, h2 - tgt, 0.0)
            dy = err * (1.0 / D_MODEL)
            dy_ref[...] = dy
            loss_ref[...] += (0.5 / D_MODEL) * jnp.sum(err * err)
            df2, dg = _rms_bwd(fhat, rs, g, dy)
            df2_ref[...] = df2.astype(BF16)
            dg_ref[...] += dg

        for k in range(FFN_STEPS):
            @pl.when((c == k) & (i < nt))
            def _(k=k):
                finish_previous_tile(k, i >= 1)
                if k == 0:
                    acc[cur] = matmuls()
                else:
                    acc[cur] += matmuls()

            @pl.when((c == k) & (i == nt))
            def _(k=k):
                finish_previous_tile(k, True)

    last = nt - 1
    this_row = pl.BlockSpec((tr, D_MODEL), lambda i, c: (jnp.minimum(i, last), 0))
    prev_quarter = pl.BlockSpec((sr, D_MODEL), lambda i, c: (jnp.maximum(i - 1, 0) * FFN_STEPS + c, 0))
    prev_quarter_out = pl.BlockSpec(
        (sr, D_MODEL), lambda i, c: (jnp.where(i == 0, nt * FFN_STEPS, (i - 1) * FFN_STEPS + c), 0))
    full = lambda a: pl.BlockSpec(a.shape, lambda i, c: (0,) * a.ndim)
    return pl.pallas_call(
        body, name="ffn_fwd", grid=(nt + 1, FFN_STEPS),
        in_specs=[this_row, _resident(w1), _resident(w2), prev_quarter] + _seq_specs(tr, delay=1) + [full(g_post_ffn)],
        out_specs=[pl.BlockSpec((tr, FF_CHUNK), lambda i, c: (jnp.minimum(i, last), jnp.where(i < nt, c, FFN_STEPS - 1))),
                   prev_quarter_out, prev_quarter_out,
                   pl.BlockSpec((1, 1), lambda i, c: (0, 0)), pl.BlockSpec((1, D_MODEL), lambda i, c: (0, 0))],
        out_shape=[jax.ShapeDtypeStruct((tp, D_FF), BF16), jax.ShapeDtypeStruct((tp + sr, D_MODEL), F32),
                   jax.ShapeDtypeStruct((tp + sr, D_MODEL), BF16), jax.ShapeDtypeStruct((1, 1), F32),
                   jax.ShapeDtypeStruct((1, D_MODEL), F32)],
        scratch_shapes=[pltpu.VMEM((2, tr, D_MODEL), F32)],
        compiler_params=_params("arbitrary", "arbitrary"),
    )(u1, w1, w2, h1, *([tgt] * qb), g_post_ffn)


def _ffn_bwd_data(df2, r1, w1, w2, dy, h1, mix, g_pre_ffn, g_post_mix):
    tp = h1.shape[0]
    tr = _row_tile(tp)
    nt = tp // tr
    sr = tr // FFN_STEPS

    def body(df2_ref, r1_ref, w1_ref, w2_ref, dy_ref, h1_ref, mix_ref, gf_ref, gm_ref,
             da_ref, dh1_ref, dmix_ref, dgf_ref, dgm_ref, acc):
        i, c = pl.program_id(0), pl.program_id(1)
        cur = i % 2

        @pl.when((i == 0) & (c == 0))
        def _():
            dgf_ref[...] = jnp.zeros_like(dgf_ref)
            dgm_ref[...] = jnp.zeros_like(dgm_ref)
            acc[1] = jnp.zeros((tr, D_MODEL), F32)

        def matmuls():
            df = _dot_nt(df2_ref[...], w2_ref[c])
            da = (df * (2.0 * r1_ref[...].astype(F32))).astype(BF16)
            da_ref[...] = da
            return _dot_nt(da, w1_ref[c])

        def finish_previous_tile(k, valid):
            lo, hi = k * sr, (k + 1) * sr
            hhat, rs = _rms(h1_ref[...])
            dx, dgf = _rms_bwd(hhat, rs, gf_ref[...], acc[1 - cur, lo:hi])
            dh1 = dy_ref[...] + dx
            dh1_ref[...] = dh1
            mhat, rsm = _rms(mix_ref[...])
            dmix, dgm = _rms_bwd(mhat, rsm, gm_ref[...], dh1)
            dmix_ref[...] = dmix.astype(BF16)
            dgf_ref[...] += jnp.where(valid, dgf, 0.0)
            dgm_ref[...] += jnp.where(valid, dgm, 0.0)

        for k in range(FFN_STEPS):
            @pl.when((c == k) & (i < nt))
            def _(k=k):
                finish_previous_tile(k, i >= 1)
                if k == 0:
                    acc[cur] = matmuls()
                else:
                    acc[cur] += matmuls()

            @pl.when((c == k) & (i == nt))
            def _(k=k):
                finish_previous_tile(k, True)

    last = nt - 1
    this_row = pl.BlockSpec((tr, D_MODEL), lambda i, c: (jnp.minimum(i, last), 0))
    prev_quarter = pl.BlockSpec((sr, D_MODEL), lambda i, c: (jnp.maximum(i - 1, 0) * FFN_STEPS + c, 0))
    prev_quarter_out = pl.BlockSpec(
        (sr, D_MODEL), lambda i, c: (jnp.where(i == 0, nt * FFN_STEPS, (i - 1) * FFN_STEPS + c), 0))
    chunk = pl.BlockSpec((tr, FF_CHUNK), lambda i, c: (jnp.minimum(i, last), jnp.where(i < nt, c, FFN_STEPS - 1)))
    gain = pl.BlockSpec((1, D_MODEL), lambda i, c: (0, 0))
    return pl.pallas_call(
        body, name="ffn_bwd_data", grid=(nt + 1, FFN_STEPS),
        in_specs=[this_row, chunk, _resident(w1), _resident(w2), prev_quarter, prev_quarter, prev_quarter, gain, gain],
        out_specs=[chunk, prev_quarter_out, prev_quarter_out, gain, gain],
        out_shape=[jax.ShapeDtypeStruct((tp, D_FF), BF16), jax.ShapeDtypeStruct((tp + sr, D_MODEL), F32),
                   jax.ShapeDtypeStruct((tp + sr, D_MODEL), BF16), jax.ShapeDtypeStruct((1, D_MODEL), F32),
                   jax.ShapeDtypeStruct((1, D_MODEL), F32)],
        scratch_shapes=[pltpu.VMEM((2, tr, D_MODEL), F32)],
        compiler_params=_params("arbitrary", "arbitrary"),
    )(df2, r1, w1, w2, dy, h1, mix, g_pre_ffn, g_post_mix)


def _ffn_bwd_weights(u1, da1, r1, df2):
    tp = u1.shape[0]
    tr = _wgrad_row_tile(tp)

    def body(u_ref, da_ref, r1_ref, df2_ref, dw1_ref, dw2_ref):
        i = pl.program_id(1)
        r = r1_ref[...].astype(F32)
        p1 = _dot_tn(u_ref[...], da_ref[...])
        p2 = _dot_tn((r * r).astype(BF16), df2_ref[...])

        @pl.when(i == 0)
        def _():
            dw1_ref[0] = p1
            dw2_ref[0] = p2

        @pl.when(i > 0)
        def _():
            dw1_ref[0] += p1
            dw2_ref[0] += p2

    row = pl.BlockSpec((tr, D_MODEL), lambda c, i: (i, 0))
    chunk = pl.BlockSpec((tr, FF_CHUNK), lambda c, i: (i, c))
    return pl.pallas_call(
        body, name="ffn_bwd_weights", grid=(N_CHIPS, tp // tr),
        in_specs=[row, chunk, chunk, row],
        out_specs=[pl.BlockSpec((1, D_MODEL, FF_CHUNK), lambda c, i: (c, 0, 0)),
                   pl.BlockSpec((1, FF_CHUNK, D_MODEL), lambda c, i: (c, 0, 0))],
        out_shape=[jax.ShapeDtypeStruct((N_CHIPS, D_MODEL, FF_CHUNK), F32),
                   jax.ShapeDtypeStruct((N_CHIPS, FF_CHUNK, D_MODEL), F32)],
        compiler_params=_params("parallel", "arbitrary"),
    )(u1, da1, r1, df2)


N_VEC_ROWS = 8


def _outproj_lru_bwd(dmix, w_out, attn, rec, xr, yr, hr, conv_w, conv_b, wa, ba, wx, bx, lam, token):
    tp = xr.shape[0]
    tr = _row_tile(tp)
    qb, nt = tr // BLOCK, tp // tr

    def body(dm_ref, w_ref, at_ref, rc_ref, xr_ref, xh_ref, yr_ref, hr_ref, hp_ref,
             cw_ref, cb_ref, wa_ref, ba_ref, wx_ref, bx_ref, lam_ref, _,
             dxr_ref, dyr_ref, dat_ref, dwo_ref, dwa_ref, dwx_ref, vec_ref, g_next, a_next, dxc_next, dsp):
        s = pl.program_id(0)
        t = nt - 1 - s

        @pl.when(s == 0)
        def _():
            g_next[...] = jnp.zeros_like(g_next)
            a_next[...] = jnp.zeros_like(a_next)
            dxc_next[...] = jnp.zeros_like(dxc_next)
            dsp[...] = jnp.zeros_like(dsp)
            dwo_ref[...] = jnp.zeros_like(dwo_ref)
            dwa_ref[...] = jnp.zeros_like(dwa_ref)
            dwx_ref[...] = jnp.zeros_like(dwx_ref)
            vec_ref[...] = jnp.zeros_like(vec_ref)

        dm = dm_ref[...]
        dcat = _dot_nt(dm, w_ref[...])
        dat_ref[...] = dcat[:, :ATTN_WIDTH].astype(BF16)
        drec_tile = dcat[:, ATTN_WIDTH:]
        dwo_ref[:ATTN_WIDTH] += _dot_tn(at_ref[...], dm)
        dwo_ref[ATTN_WIDTH:] += _dot_tn(rc_ref[...], dm)

        first_tile = t == 0
        cw, cb = cw_ref[...], cb_ref[...]
        lam_v = lam_ref[...]
        sp = _softplus(-lam_v)
        wa_m, ba_v, wx_m, bx_v = wa_ref[...], ba_ref[...], wx_ref[...], bx_ref[...]
        rows = lax.broadcasted_iota(jnp.int32, (BLOCK, LRU_WIDTH), 0)
        col = lambda v: jnp.sum(v, axis=0, keepdims=True)

        g_after, a_after, dxc_after = g_next[0:1], a_next[0:1], dxc_next[...]
        xbs, dgrs, dgis = [], [], []
        vec = [jnp.zeros((1, LRU_WIDTH), F32) for _ in range(N_VEC_ROWS)]
        for i in reversed(range(qb)):
            blk = slice(i * BLOCK, (i + 1) * BLOCK)
            if i == 0:
                x_before = jnp.where(first_tile, 0.0, xh_ref[...])
                h_before = jnp.where(first_tile, 0.0, hp_ref[7:8])
            else:
                x_before = xr_ref[i * BLOCK - 8:i * BLOCK]
                h_before = hr_ref[i * BLOCK - 1:i * BLOCK]
            taps = _conv_taps(xr_ref[blk], x_before)
            xc = cb + sum(cw[k:k + 1] * taps[k] for k in range(4))
            xb, r, ig, a, mult = _lru_gates(xc, wa_m, ba_v, wx_m, bx_v, sp)

            yr_v = yr_ref[blk]
            gl, th = _gelu(yr_v)
            h = hr_ref[blk]
            drec = drec_tile[blk]
            dyr_ref[blk] = (drec * h * _gelu_grad(yr_v, th)).astype(BF16)

            a_up = jnp.where(rows == BLOCK - 1, a_after, pltpu.roll(a, BLOCK - 1, 0))
            g = _scan_rev(a_up, drec * gl, g_after)
            g_after, a_after = g[0:1], a[0:1]

            h_prev = jnp.where(rows == 0, h_before, pltpu.roll(h, 1, 0))
            du, da = g, g * h_prev
            if i == 0:
                real = (t * tr + rows) >= PAD_ROWS
                du, da = jnp.where(real, du, 0.0), jnp.where(real, da, 0.0)
            dmult = du * (ig * xc)
            dig = du * (mult * xc)
            dxc = du * (mult * ig)
            dlog_a = da * a - dmult * (a * a / mult)
            if i == 0:
                dlog_a = jnp.where(real, dlog_a, 0.0)
            dgr = (dlog_a * (-LRU_C * sp)) * (r * (1.0 - r))
            dgi = dig * (ig * (1.0 - ig))
            dgr_b, dgi_b = dgr.astype(BF16), dgi.astype(BF16)
            dxc = dxc + _dot_nt(dgr_b, wa_m) + _dot_nt(dgi_b, wx_m)
            xbs.append(xb)
            dgrs.append(dgr_b)
            dgis.append(dgi_b)

            ext = jnp.concatenate([dxc, dxc_after], axis=0)
            up = [ext[:BLOCK] if j == 0 else pltpu.roll(ext, BLOCK + 8 - j, 0)[:BLOCK] for j in range(4)]
            dxr_ref[blk] = sum(cw[k:k + 1] * up[3 - k] for k in range(4)).astype(BF16)
            dxc_after = dxc[:8]

            for k in range(4):
                vec[k] = vec[k] + col(dxc * taps[k])
            vec[4] = vec[4] + col(dxc)
            vec[5] = vec[5] + col(dgr)
            vec[6] = vec[6] + col(dgi)
            vec[7] = vec[7] + col(dlog_a * (-LRU_C * r))

        g_next[0:1], a_next[0:1], dxc_next[...] = g_after, a_after, dxc_after
        xb_all = jnp.concatenate(xbs, axis=0)
        dwa_ref[...] += _dot_tn(xb_all, jnp.concatenate(dgrs, axis=0))
        dwx_ref[...] += _dot_tn(xb_all, jnp.concatenate(dgis, axis=0))
        for k in range(7):
            vec_ref[k:k + 1] += vec[k]
        dsp[0:1] += vec[7]

        @pl.when(s == nt - 1)
        def _():
            vec_ref[7:8] = dsp[0:1] * (-_sigmoid(-lam_v))

    blk_spec = pl.BlockSpec((tr, LRU_WIDTH), lambda s: (nt - 1 - s, 0))
    rows_before = pl.BlockSpec((8, LRU_WIDTH), lambda s: (jnp.maximum((nt - 1 - s) * (tr // 8) - 1, 0), 0))
    full = lambda a: pl.BlockSpec(a.shape, lambda s: (0,) * a.ndim)
    small = [conv_w, conv_b, wa, ba, wx, bx, lam, token]
    sq = pl.BlockSpec((LRU_WIDTH, LRU_WIDTH), lambda s: (0, 0))
    wide = pl.BlockSpec((tr, D_MODEL), lambda s: (nt - 1 - s, 0))
    whole = pl.BlockSpec((D_MODEL, D_MODEL), lambda s: (0, 0))
    return pl.pallas_call(
        body, name="outproj_lru_bwd", grid=(nt,),
        in_specs=[wide, whole, blk_spec, blk_spec, blk_spec, rows_before, blk_spec, blk_spec, rows_before]
        + [full(a) for a in small],
        out_specs=[blk_spec, blk_spec, blk_spec, whole, sq, sq, pl.BlockSpec((N_VEC_ROWS, LRU_WIDTH), lambda s: (0, 0))],
        out_shape=[jax.ShapeDtypeStruct((tp, LRU_WIDTH), BF16), jax.ShapeDtypeStruct((tp, LRU_WIDTH), BF16),
                   jax.ShapeDtypeStruct((tp, ATTN_WIDTH), BF16), jax.ShapeDtypeStruct((D_MODEL, D_MODEL), F32),
                   jax.ShapeDtypeStruct((LRU_WIDTH, LRU_WIDTH), F32), jax.ShapeDtypeStruct((LRU_WIDTH, LRU_WIDTH), F32),
                   jax.ShapeDtypeStruct((N_VEC_ROWS, LRU_WIDTH), F32)],
        scratch_shapes=[pltpu.VMEM((8, LRU_WIDTH), F32)] * 4,
        compiler_params=_params("arbitrary"),
    )(dmix, w_out, attn, rec, xr, xr, yr, hr, hr, *small)


def _attn_tile(tp):
    return _wgrad_row_tile(tp)


def _attn_bwd(qkv, dattn, probs, sink_probs, token):
    tp = qkv.shape[0]
    tr = _attn_tile(tp)
    qb, nt = tr // BLOCK, tp // tr
    n_groups = N_KV

    def body(p_ref, ps_ref, q_ref, kp_ref, kc_ref, vp_ref, vc_ref, do_ref, _, dq_ref, dkv_ref, ex_ref, ds_ref, dsink):
        t = pl.program_id(0)

        @pl.when(t == 0)
        def _():
            dsink[...] = jnp.zeros_like(dsink)

        k_all = jnp.concatenate([kp_ref[...], kc_ref[...]], axis=0)
        v_all = jnp.concatenate([vp_ref[...], vc_ref[...]], axis=0)
        tail = None
        for i in range(qb):
            rows = slice(i * BLOCK, (i + 1) * BLOCK)
            qt = (q_ref[rows].astype(F32) * _QSCALE).T
            dot = do_ref[rows].astype(F32).T
            k2, v2 = k_all[i * BLOCK:(i + 2) * BLOCK], v_all[i * BLOCK:(i + 2) * BLOCK]
            dqs, dks, dvs = [], [], []
            for g in range(n_groups):
                cols = slice(g * HEAD_DIM, (g + 1) * HEAD_DIM)
                k_g, v_g = k2[:, cols], v2[:, cols]
                qgt, dogt = _heads_t(qt, g), _heads_t(dot, g)
                pb = p_ref[i, g]
                p = pb.astype(F32)
                dpt = _dot(v_g, dogt)
                delta = jnp.sum(p * dpt, axis=0, keepdims=True)
                dst = (p * (dpt - delta)).astype(BF16)
                dqs.append(_dot_tn(k_g, dst) * _QSCALE)
                dks.append(_dot_nt(qgt, dst))
                dvs.append(_dot_nt(dogt, pb))
                dsink[g:g + 1] -= ps_ref[i, g:g + 1] * delta
            dq_ref[rows] = _from_heads_t(dqs).astype(BF16)
            dkv = jnp.concatenate([jnp.concatenate(dks, axis=0).T, jnp.concatenate(dvs, axis=0).T], axis=1)
            if i == 0:
                ex_ref[0] = dkv[:BLOCK]
            else:
                dkv_ref[(i - 1) * BLOCK:i * BLOCK] = (tail + dkv[:BLOCK]).astype(BF16)
            tail = dkv[BLOCK:]
        dkv_ref[(qb - 1) * BLOCK:] = tail.astype(BF16)

        @pl.when(t == nt - 1)
        def _():
            lane = lax.broadcasted_iota(jnp.int32, (1, ATTN_HEADS), 1)
            acc = jnp.zeros((1, ATTN_HEADS), F32)
            for h in range(ATTN_HEADS):
                g, hh = divmod(h, GQA_GROUP)
                acc = acc + jnp.where(lane == h, jnp.sum(dsink[g:g + 1, hh * BLOCK:(hh + 1) * BLOCK]), 0.0)
            ds_ref[...] = acc

    cur = lambda w: pl.BlockSpec((tr, w), lambda t: (t, 0))
    return pl.pallas_call(
        body, name="attn_bwd", grid=(nt,),
        in_specs=_prob_specs(qb) + [cur(ATTN_WIDTH)] + _kv_specs(tr)
        + [cur(ATTN_WIDTH), pl.BlockSpec(token.shape, lambda t: (0, 0))],
        out_specs=[cur(ATTN_WIDTH), cur(2 * KV_WIDTH), pl.BlockSpec((1, BLOCK, 2 * KV_WIDTH), lambda t: (t, 0, 0)),
                   pl.BlockSpec((1, ATTN_HEADS), lambda t: (0, 0))],
        out_shape=[jax.ShapeDtypeStruct((tp, ATTN_WIDTH), BF16), jax.ShapeDtypeStruct((tp, 2 * KV_WIDTH), BF16),
                   jax.ShapeDtypeStruct((nt, BLOCK, 2 * KV_WIDTH), F32), jax.ShapeDtypeStruct((1, ATTN_HEADS), F32)],
        scratch_shapes=[pltpu.VMEM((n_groups, GROUP_ROWS), F32)],
        compiler_params=_params("arbitrary"),
    )(probs, sink_probs, qkv, qkv, qkv, qkv, qkv, dattn, token)


def _fix_dkv(dkv, dkv_extra):
    tp = dkv.shape[0]
    tr = _attn_tile(tp)
    nt, qb = tp // tr, tr // BLOCK
    if nt == 1:
        return dkv

    def body(d_ref, ex_ref, o_ref):
        o_ref[...] = (d_ref[...].astype(F32) + ex_ref[0]).astype(BF16)

    last = pl.BlockSpec((BLOCK, 2 * KV_WIDTH), lambda t: (t * qb + qb - 1, 0))
    return pl.pallas_call(
        body, name="fix_dkv", grid=(nt - 1,),
        in_specs=[last, pl.BlockSpec((1, BLOCK, 2 * KV_WIDTH), lambda t: (t + 1, 0, 0))],
        out_specs=last, out_shape=jax.ShapeDtypeStruct(dkv.shape, dkv.dtype),
        input_output_aliases={0: 0}, compiler_params=_params("parallel"),
    )(dkv, dkv_extra)


def _inproj_wgrad(dq, dkv, dxr, dyr, u0):
    tp = dq.shape[0]
    tr = _wgrad_row_tile(tp)

    def body(dq_ref, dkv_ref, dxr_ref, dyr_ref, u_ref, dw_ref):
        i = pl.program_id(0)
        dz = jnp.concatenate([dq_ref[...], dkv_ref[...], dxr_ref[...], dyr_ref[...]], axis=1)
        pw = _dot_tn(dz, u_ref[...])

        @pl.when(i == 0)
        def _():
            dw_ref[...] = pw

        @pl.when(i > 0)
        def _():
            dw_ref[...] += pw

    row = lambda w: pl.BlockSpec((tr, w), lambda i: (i, 0))
    return pl.pallas_call(
        body, name="inproj_wgrad", grid=(tp // tr,),
        in_specs=[row(ATTN_WIDTH), row(2 * KV_WIDTH), row(LRU_WIDTH), row(LRU_WIDTH), row(D_MODEL)],
        out_specs=pl.BlockSpec((IN_WIDTH, D_MODEL), lambda i: (0, 0)),
        out_shape=jax.ShapeDtypeStruct((IN_WIDTH, D_MODEL), F32),
        compiler_params=_params("arbitrary"),
    )(dq, dkv, dxr, dyr, u0)


def _inproj_dgrad(dq, dkv, dxr, dyr, w_in, head, x, dh1, g, token):
    tp = dq.shape[0]
    tr = _row_tile(tp)
    nt, qb = tp // tr, tr // BLOCK

    def body(*refs):
        dq_ref, dkv_ref, dxr_ref, dyr_ref, w_ref, head_ref = refs[:6]
        pieces = refs[6:6 + qb]
        dh1_ref, g_ref, _, gx_ref, dhead_ref, dg_ref, buf, sems = refs[6 + qb:]
        i = pl.program_id(0)
        slot = i % 2

        def out_copy(step, at):
            return pltpu.make_async_copy(buf.at[at], gx_ref.at[pl.ds(step * tr - BLOCK, tr)], sems.at[at])

        dz = jnp.concatenate([dq_ref[...], dkv_ref[...], dxr_ref[...], dyr_ref[...]], axis=1)
        du = _dot(dz, w_ref[...])
        hhat, rs = _rms(_seq_tile(head_ref[...], pieces, i))
        dx, dg = _rms_bwd(hhat, rs, g_ref[...], du)
        dh0 = dh1_ref[...] + dx

        @pl.when(i >= 3)
        def _():
            out_copy(i - 2, slot).wait()

        buf[slot] = dh0

        @pl.when(i == 0)
        def _():
            dg_ref[...] = dg
            dhead_ref[...] = dh0[:BLOCK]
            if tr > BLOCK:
                first = pltpu.make_async_copy(buf.at[0, pl.ds(BLOCK, tr - BLOCK)], gx_ref.at[pl.ds(0, tr - BLOCK)],
                                              sems.at[0])
                first.start()
                first.wait()

        @pl.when(i >= 1)
        def _():
            dg_ref[...] += dg
            out_copy(i, slot).start()

        @pl.when(i == nt - 1)
        def _():
            if nt >= 3:
                out_copy(nt - 2, (nt - 2) % 2).wait()
            if nt >= 2:
                out_copy(nt - 1, (nt - 1) % 2).wait()

    row = lambda w: pl.BlockSpec((tr, w), lambda i: (i, 0))
    full = lambda shape: pl.BlockSpec(shape, lambda i: (0,) * len(shape))
    return pl.pallas_call(
        body, name="inproj_dgrad", grid=(tp // tr,),
        in_specs=[row(ATTN_WIDTH), row(2 * KV_WIDTH), row(LRU_WIDTH), row(LRU_WIDTH), full(w_in.shape),
                  full(head.shape)] + _seq_specs(tr) + [row(D_MODEL), full(g.shape), full(token.shape)],
        out_specs=[pl.BlockSpec(memory_space=pl.ANY), full((BLOCK, D_MODEL)), full((1, D_MODEL))],
        out_shape=[jax.ShapeDtypeStruct(x.shape, F32), jax.ShapeDtypeStruct((BLOCK, D_MODEL), F32),
                   jax.ShapeDtypeStruct((1, D_MODEL), F32)],
        scratch_shapes=[pltpu.VMEM((2, tr, D_MODEL), F32), pltpu.SemaphoreType.DMA((2,))],
        compiler_params=_params("arbitrary"),
    )(dq, dkv, dxr, dyr, w_in, head, *([x] * qb), dh1, g, token)


def _dense_block_diag(w):
    eye = jnp.eye(LRU_BLOCKS, dtype=w.dtype)
    return (w[:, :, None, :] * eye[:, None, :, None]).reshape(LRU_WIDTH, LRU_WIDTH)


def _diag_blocks(dense):
    d4 = dense.reshape(LRU_BLOCKS, LRU_BLOCK, LRU_BLOCKS, LRU_BLOCK)
    return jnp.stack([d4[n, :, n, :] for n in range(LRU_BLOCKS)])


def _local_step(head, x, tgt, g_pre_mix, w_in, conv_w, conv_b, w_a, b_a, w_x, b_x, lam, sinks, g_post_mix,
                g_pre_ffn, g_post_ffn, late_weights, on_ffn_grads, on_outproj_bwd, on_mixer_grads, token):
    wa = _dense_block_diag(w_a).astype(BF16)
    wx = _dense_block_diag(w_x).astype(BF16)

    u0, qkv, xr, yr, hr, rec = _inproj_lru_fwd(head, x, g_pre_mix, w_in, conv_w, conv_b, wa, b_a, wx, b_x, lam, token)
    attn, probs, sink_probs = _attn_fwd(qkv, sinks)
    w_out, w1, w2 = late_weights([attn, rec])
    mix, h1, u1 = _outproj_fwd(attn, rec, w_out, head, x, g_post_mix, g_pre_ffn)
    r1, dy, df2, loss, dg_post_ffn = _ffn_fwd(u1, w1, w2, h1, tgt, g_post_ffn)

    da1, dh1, dmix, dg_pre_ffn, dg_post_mix = _ffn_bwd_data(df2, r1, w1, w2, dy, h1, mix, g_pre_ffn, g_post_mix)
    dw1, dw2 = _ffn_bwd_weights(u1, da1, r1, df2)
    token2 = on_ffn_grads(dw1, dw2)
    dxr, dyr, dattn, dw_out, dwa, dwx, vec = _outproj_lru_bwd(dmix, w_out, attn, rec, xr, yr, hr, conv_w, conv_b,
                                                              wa, b_a, wx, b_x, lam, token2)
    token3 = on_outproj_bwd(dattn)
    dq, dkv, dkv_extra, dsinks = _attn_bwd(qkv, dattn, probs, sink_probs, token3)
    dkv = _fix_dkv(dkv, dkv_extra)
    dw_in = _inproj_wgrad(dq, dkv, dxr, dyr, u0)
    token4 = on_mixer_grads(dw_in, dw_out)
    dx, dhead, dg_pre_mix = _inproj_dgrad(dq, dkv, dxr, dyr, w_in, head, x, dh1, g_pre_mix, token4)

    grads = dict(
        g_pre_mix=dg_pre_mix, conv_w=vec[0:4], conv_b=vec[4:5], w_a=_diag_blocks(dwa), b_a=vec[5:6],
        w_x=_diag_blocks(dwx), b_x=vec[6:7], lru_lambda=vec[7:8], attn_sinks=dsinks,
        g_post_mix=dg_post_mix, g_pre_ffn=dg_pre_ffn, g_post_ffn=dg_post_ffn)
    return loss, dx, dhead, grads


HBM = pl.BlockSpec(memory_space=pltpu.HBM)


def _mesh_pos():
    return lax.axis_index("x"), lax.axis_index("y"), lax.axis_index("c")


def _other_chips(x, y):
    return [(1 - x, y), (x, 1 - y), (1 - x, 1 - y)]


def _remote(src, dst, send_sem, recv_sem, to):
    return pltpu.make_async_remote_copy(src_ref=src, dst_ref=dst, send_sem=send_sem, recv_sem=recv_sem,
                                        device_id=to, device_id_type=MESH)


def _gather_weights(shards, lands, tiny, tiny_land):
    nbig = len(shards)

    def body(*refs):
        srcs, tiny_src = refs[:nbig], refs[nbig]
        outs, tiny_out = refs[2 * nbig + 2:3 * nbig + 2], refs[3 * nbig + 2]
        ici_send, ici_recv, d2d_send, d2d_recv, tiny_send, tiny_recv = refs[3 * nbig + 3:]
        x, y, c = _mesh_pos()
        me = 2 * x + y
        chips = _other_chips(x, y)
        sibling = (x, y, 1 - c)
        sends = []
        for w, (src, out) in enumerate(zip(srcs, outs)):
            hr = src.shape[0] // 2
            for j, chip in enumerate(chips):
                k = 3 * w + j
                cp = _remote(src.at[pl.ds(c * hr, hr)], out.at[me, pl.ds(c * hr, hr)],
                             ici_send.at[k], ici_recv.at[k], (*chip, c))
                cp.start()
                sends.append(cp)
        for j, chip in enumerate(chips):
            cp = _remote(tiny_src, tiny_out.at[me], tiny_send.at[j], tiny_recv.at[j], (*chip, c))
            cp.start()
            sends.append(cp)
        for w, (src, out) in enumerate(zip(srcs, outs)):
            hr = src.shape[0] // 2
            for j, (px, py) in enumerate(chips):
                k = 3 * w + j
                landed = out.at[2 * px + py, pl.ds(c * hr, hr)]
                _remote(landed, landed, ici_send.at[k], ici_recv.at[k], sibling).wait_recv()
                cp = _remote(landed, landed, d2d_send.at[k], d2d_recv.at[k], sibling)
                cp.start()
                sends.append(cp)
        for w, (src, out) in enumerate(zip(srcs, outs)):
            hr = src.shape[0] // 2
            for j, (px, py) in enumerate(chips):
                k = 3 * w + j
                other = out.at[2 * px + py, pl.ds((1 - c) * hr, hr)]
                _remote(other, other, d2d_send.at[k], d2d_recv.at[k], sibling).wait_recv()
        for j, (px, py) in enumerate(chips):
            blk = tiny_out.at[2 * px + py]
            _remote(blk, blk, tiny_send.at[j], tiny_recv.at[j], sibling).wait_recv()
        for cp in sends:
            cp.wait_send()

    out_shape = [jax.ShapeDtypeStruct(l.shape, l.dtype) for l in list(lands) + [tiny_land]]
    n = 3 * nbig
    return pl.pallas_call(
        body, name="gather_weights", out_shape=out_shape,
        in_specs=[HBM] * (2 * nbig + 2), out_specs=[HBM] * (nbig + 1),
        input_output_aliases={nbig + 1 + i: i for i in range(nbig + 1)},
        scratch_shapes=[pltpu.SemaphoreType.DMA((n,)),
                        pltpu.SemaphoreType.DMA((n,)), pltpu.SemaphoreType.DMA((n,)), pltpu.SemaphoreType.DMA((n,)),
                        pltpu.SemaphoreType.DMA((3,)), pltpu.SemaphoreType.DMA((3,))],
    )(*shards, tiny, *lands, tiny_land)


def _prep_shard(w, me):
    rows, cols = w.shape
    tr = 256 if rows % 256 == 0 else rows

    def body(me_ref, w_ref, s_ref, l_ref):
        b = w_ref[...].astype(BF16)
        s_ref[...] = b
        l_ref[0] = b

    return pl.pallas_call(
        body, name="prep_shard",
        grid_spec=pltpu.PrefetchScalarGridSpec(
            num_scalar_prefetch=1, grid=(rows // tr,),
            in_specs=[pl.BlockSpec((tr, cols), lambda i, me_ref: (i, 0))],
            out_specs=[pl.BlockSpec((tr, cols), lambda i, me_ref: (i, 0)),
                       pl.BlockSpec((1, tr, cols), lambda i, me_ref: (me_ref[0], i, 0))]),
        out_shape=[jax.ShapeDtypeStruct((rows, cols), BF16), jax.ShapeDtypeStruct((N_CHIPS, rows, cols), BF16)],
        compiler_params=_params("parallel"),
    )(me, w)


def _prep_tiny(tiny, me, slots=N_CHIPS):
    def body(me_ref, t_ref, l_ref):
        l_ref[0] = t_ref[...]

    return pl.pallas_call(
        body, name="prep_tiny",
        grid_spec=pltpu.PrefetchScalarGridSpec(
            num_scalar_prefetch=1, grid=(1,),
            in_specs=[pl.BlockSpec(tiny.shape, lambda i, me_ref: (0, 0))],
            out_specs=pl.BlockSpec((1,) + tiny.shape, lambda i, me_ref: (me_ref[0], 0, 0))),
        out_shape=jax.ShapeDtypeStruct((slots,) + tiny.shape, tiny.dtype),
    )(me, tiny)


N_DEV = 8


def _sibling_exchange(parts, token):
    def body(*refs):
        n = len(parts)
        srcs, outs, send_sems, recv_sems = refs[:n], refs[n + 1:2 * n + 1], refs[2 * n + 1], refs[2 * n + 2]
        x, y, c = _mesh_pos()
        sibling = (x, y, 1 - c)
        cps = []
        for w, (src, out) in enumerate(zip(srcs, outs)):
            hr = src.shape[1] // 2
            cp = _remote(src.at[:, pl.ds((1 - c) * hr, hr)], out, send_sems.at[w], recv_sems.at[w], sibling)
            cp.start()
            cps.append(cp)
        for cp in cps:
            cp.wait()

    n = len(parts)
    return pl.pallas_call(
        body, name="sibling_exchange",
        out_shape=[jax.ShapeDtypeStruct((p.shape[0], p.shape[1] // 2, p.shape[2]), p.dtype) for p in parts],
        in_specs=[HBM] * n + [pl.BlockSpec(memory_space=pl.ANY)], out_specs=[HBM] * n,
        scratch_shapes=[pltpu.SemaphoreType.DMA((n,)), pltpu.SemaphoreType.DMA((n,))],
    )(*parts, token)


def _chip_presum(part, from_sibling, pos):
    _, hr, cols = from_sibling.shape
    tr = 256 if hr % 256 == 0 else hr
    steps = hr // tr

    def body(pos_ref, a_ref, b_ref, o_ref, land_ref):
        s = (a_ref[...] + b_ref[...]).astype(BF16)
        o_ref[...] = s

        @pl.when(pl.program_id(1) == pos_ref[1])
        def _():
            land_ref[...] = s

    return pl.pallas_call(
        body, name="chip_presum",
        grid_spec=pltpu.PrefetchScalarGridSpec(
            num_scalar_prefetch=1, grid=(steps, N_CHIPS),
            in_specs=[pl.BlockSpec((1, tr, cols), lambda i, j, p: (j, p[0] * steps + i, 0)),
                      pl.BlockSpec((1, tr, cols), lambda i, j, p: (j, i, 0))],
            out_specs=[pl.BlockSpec((1, tr, cols), lambda i, j, p: (j, i, 0)),
                       pl.BlockSpec((1, tr, cols), lambda i, j, p: (p[1], p[0] * steps + i, 0))]),
        out_shape=[jax.ShapeDtypeStruct(from_sibling.shape, BF16),
                   jax.ShapeDtypeStruct((N_CHIPS, 2 * hr, cols), BF16)],
        compiler_params=_params("arbitrary", "arbitrary"),
    )(pos, part, from_sibling)


def _scatter_partials(cparts, lands, done_cparts=(), done_lands=()):
    n_new = len(cparts)
    nw = n_new + len(done_cparts)

    def body(*refs):
        srcs = refs[:nw]
        outs = refs[2 * nw:3 * nw]
        own_send, own_recv, ici_send, ici_recv, d2d_send, d2d_recv = refs[3 * nw:]
        x, y, c = _mesh_pos()
        me = 2 * x + y
        chips = _other_chips(x, y)
        sibling = (x, y, 1 - c)
        sends = []
        for w in list(range(n_new, nw)) + list(range(n_new)):
            src, out = srcs[w], outs[w]
            hr = src.shape[1]
            mine = out.at[me, pl.ds(c * hr, hr)]
            cp = _remote(src.at[me], mine, own_send.at[w], own_recv.at[w], sibling)
            cp.start()
            sends.append(cp)
            for j, (px, py) in enumerate(chips):
                if w >= n_new:
                    break
                k = 3 * w + j
                cp = _remote(src.at[2 * px + py], mine, ici_send.at[k], ici_recv.at[k], (px, py, c))
                cp.start()
                sends.append(cp)
        for w in list(range(n_new, nw)) + list(range(n_new)):
            src, out = srcs[w], outs[w]
            hr = src.shape[1]
            for j, (px, py) in enumerate(chips):
                k = 3 * w + j
                landed = out.at[2 * px + py, pl.ds(c * hr, hr)]
                if w < n_new:
                    _remote(landed, landed, ici_send.at[k], ici_recv.at[k], sibling).wait_recv()
                cp = _remote(landed, landed, d2d_send.at[k], d2d_recv.at[k], sibling)
                cp.start()
                sends.append(cp)
        for w, (src, out) in enumerate(zip(srcs, outs)):
            hr = src.shape[1]
            other = out.at[me, pl.ds((1 - c) * hr, hr)]
            _remote(other, other, own_send.at[w], own_recv.at[w], sibling).wait_recv()
            for j, (px, py) in enumerate(chips):
                k = 3 * w + j
                other = out.at[2 * px + py, pl.ds((1 - c) * hr, hr)]
                _remote(other, other, d2d_send.at[k], d2d_recv.at[k], sibling).wait_recv()
        for cp in sends:
            cp.wait_send()

    n = 3 * nw
    dma = pltpu.SemaphoreType.DMA
    every = list(cparts) + list(done_cparts)
    every_lands = list(lands) + list(done_lands)
    return pl.pallas_call(
        body, name="scatter_partials",
        out_shape=[jax.ShapeDtypeStruct(l.shape, l.dtype) for l in every_lands],
        in_specs=[HBM] * (2 * nw), out_specs=[HBM] * nw,
        input_output_aliases={nw + i: i for i in range(nw)},
        scratch_shapes=[dma((nw,)), dma((nw,)), dma((n,)), dma((n,)), dma((n,)), dma((n,))],
    )(*every, *every_lands)


SEM = pl.BlockSpec(memory_space=pltpu.SEMAPHORE)
SPLIT_COPY = pltpu.CompilerParams(has_side_effects=pltpu.SideEffectType.DATAFLOW_SIDE_EFFECTING)


def _hbm(a):
    return pltpu.with_memory_space_constraint(a, pltpu.HBM)


def _gather_copies(srcs, lands, send_sems, recv_sems):
    x, y, c = _mesh_pos()
    me = 2 * x + y
    sends, recvs = [], []
    for w, (src, land) in enumerate(zip(srcs, lands)):
        hr = src.shape[0] // 2
        for j, (px, py) in enumerate(_other_chips(x, y)):
            k = 3 * w + j
            sends.append(_remote(src.at[pl.ds(c * hr, hr)], land.at[me, pl.ds(c * hr, hr)],
                                 send_sems.at[k], recv_sems.at[k], (px, py, c)))
            got = land.at[2 * px + py, pl.ds(c * hr, hr)]
            recvs.append(_remote(got, got, send_sems.at[k], recv_sems.at[k], (px, py, c)))
    return sends, recvs


def _scatter_copies(srcs, lands, send_sems, recv_sems):
    x, y, c = _mesh_pos()
    me = 2 * x + y
    sends, recvs = [], []
    for w, (src, land) in enumerate(zip(srcs, lands)):
        hr = src.shape[1]
        for j, (px, py) in enumerate(_other_chips(x, y)):
            k = 3 * w + j
            sends.append(_remote(src.at[2 * px + py], land.at[me, pl.ds(c * hr, hr)],
                                 send_sems.at[k], recv_sems.at[k], (px, py, c)))
            got = land.at[2 * px + py, pl.ds(c * hr, hr)]
            recvs.append(_remote(got, got, send_sems.at[k], recv_sems.at[k], (px, py, c)))
    return sends, recvs


def _sibling_copies(srcs, lands, send_sems, recv_sems):
    x, y, c = _mesh_pos()
    sibling = (x, y, 1 - c)
    sends, recvs = [], []
    for w, (src, land) in enumerate(zip(srcs, lands)):
        hr = src.shape[1] // 2
        sends.append(_remote(src.at[:, pl.ds((1 - c) * hr, hr)], land, send_sems.at[w], recv_sems.at[w], sibling))
        recvs.append(_remote(land, land, send_sems.at[w], recv_sems.at[w], sibling))
    return sends, recvs


def _all_peers_copies(srcs, lands, send_sems, recv_sems):
    x, y, c = _mesh_pos()
    (src,), (land,) = srcs, lands
    flip = lambda v, bit: 1 - v if bit else v
    sends, recvs = [], []
    for k in range(N_DEV - 1):
        px, py, pc = flip(x, (k + 1) & 4), flip(y, (k + 1) & 2), flip(c, (k + 1) & 1)
        sends.append(_remote(src, land.at[4 * x + 2 * y + c], send_sems.at[k], recv_sems.at[k], (px, py, pc)))
        got = land.at[4 * px + 2 * py + pc]
        recvs.append(_remote(got, got, send_sems.at[k], recv_sems.at[k], (px, py, pc)))
    return sends, recvs


def _split_start(name, copies_of, srcs, land_shapes, n_copies=None):
    n = len(srcs)
    k = 3 * n if n_copies is None else n_copies

    def body(*refs):
        src_refs, land_refs = refs[:n], refs[n:2 * n]
        send_sems, recv_sems = refs[2 * n], refs[2 * n + 1]
        token = refs[-1]
        sends, _ = copies_of(src_refs, land_refs, send_sems, recv_sems)
        for cp in sends:
            cp.start()
        token[...] = jnp.zeros_like(token)

    lands = [_hbm(s) for s in land_shapes]
    dma = pltpu.SemaphoreType.DMA
    res = pl.pallas_call(
        body, name=name,
        out_shape=(dma((k,)), dma((k,)), *[pltpu.HBM(s.shape, s.dtype) for s in srcs],
                   *[pltpu.HBM(s.shape, s.dtype) for s in land_shapes], jax.ShapeDtypeStruct((8, 128), F32)),
        in_specs=[HBM] * (2 * n),
        out_specs=(SEM, SEM, *([HBM] * (2 * n)), pl.BlockSpec(memory_space=pltpu.VMEM)),
        input_output_aliases={i: 2 + i for i in range(2 * n)},
        compiler_params=SPLIT_COPY,
    )(*[_hbm(s) for s in srcs], *lands)
    return res[0], res[1], list(res[2:2 + n]), list(res[2 + n:2 + 2 * n]), res[-1]


def _split_wait(name, copies_of, send_sems, recv_sems, srcs, lands, after):
    n = len(srcs)

    def body(*refs):
        src_refs, land_refs = refs[:n], refs[n:2 * n]
        sends, recvs = copies_of(src_refs, land_refs, refs[2 * n], refs[2 * n + 1])
        for cp in sends:
            cp.wait_send()
        for cp in recvs:
            cp.wait_recv()

    res = pl.pallas_call(
        body, name=name,
        out_shape=tuple(pltpu.HBM(s.shape, s.dtype) for s in list(srcs) + list(lands)),
        in_specs=[HBM] * (2 * n) + [SEM, SEM] + [pl.BlockSpec(memory_space=pl.ANY)] * len(after),
        out_specs=tuple([HBM] * (2 * n)),
        input_output_aliases={i: i for i in range(2 * n)},
        compiler_params=SPLIT_COPY,
    )(*srcs, *lands, send_sems, recv_sems, *after)
    return list(res[:n]), list(res[n:])


def _gather_finish(lands):
    n = len(lands)

    def body(*refs):
        outs = refs[n:2 * n]
        d2d_send, d2d_recv = refs[2 * n:]
        x, y, c = _mesh_pos()
        chips = _other_chips(x, y)
        sibling = (x, y, 1 - c)
        sends = []
        for w, out in enumerate(outs):
            hr = out.shape[1] // 2
            for j, (px, py) in enumerate(chips):
                landed = out.at[2 * px + py, pl.ds(c * hr, hr)]
                cp = _remote(landed, landed, d2d_send.at[3 * w + j], d2d_recv.at[3 * w + j], sibling)
                cp.start()
                sends.append(cp)
        for w, out in enumerate(outs):
            hr = out.shape[1] // 2
            for j, (px, py) in enumerate(chips):
                other = out.at[2 * px + py, pl.ds((1 - c) * hr, hr)]
                _remote(other, other, d2d_send.at[3 * w + j], d2d_recv.at[3 * w + j], sibling).wait_recv()
        for cp in sends:
            cp.wait_send()

    dma = pltpu.SemaphoreType.DMA
    return pl.pallas_call(
        body, name="gather_finish",
        out_shape=[jax.ShapeDtypeStruct(l.shape, l.dtype) for l in lands],
        in_specs=[HBM] * n, out_specs=[HBM] * n,
        input_output_aliases={i: i for i in range(n)},
        scratch_shapes=[dma((3 * n,)), dma((3 * n,))],
    )(*lands)


def _adamw(w, g, m, v):
    m = ADAM_B1 * m + (1.0 - ADAM_B1) * g
    v = ADAM_B2 * v + (1.0 - ADAM_B2) * (g * g)
    m_hat = m / (1.0 - ADAM_B1 ** ADAM_STEP)
    v_hat = v / (1.0 - ADAM_B2 ** ADAM_STEP)
    delta = -ADAM_LR * (m_hat / (jnp.sqrt(v_hat) + ADAM_EPS) + ADAM_WD * w)
    return delta, m, v


def _adamw_big(partials, w, m, v):
    rows, cols = w.shape
    tr = 256 if rows % 256 == 0 else rows

    def body(p_ref, w_ref, m_ref, v_ref, g_ref, d_ref, m2_ref, v2_ref):
        g = ((p_ref[0].astype(F32) + p_ref[1].astype(F32)) + p_ref[2].astype(F32)) + p_ref[3].astype(F32)
        g_ref[...] = g
        d_ref[...], m2_ref[...], v2_ref[...] = _adamw(w_ref[...], g, m_ref[...], v_ref[...])

    blk = pl.BlockSpec((tr, cols), lambda i: (i, 0))
    return pl.pallas_call(
        body, name="adamw_big", grid=(rows // tr,),
        in_specs=[pl.BlockSpec((N_CHIPS, tr, cols), lambda i: (0, i, 0)), blk, blk, blk],
        out_specs=[blk] * 4, out_shape=[jax.ShapeDtypeStruct((rows, cols), F32)] * 4,
        compiler_params=_params("parallel"),
    )(partials, w, m, v)


def _sum_devices(gathered, rows):
    cols = gathered.shape[1]

    def body(g_ref, o_ref):
        acc = g_ref[0:rows]
        for d in range(1, N_DEV):
            acc = acc + g_ref[d * rows:(d + 1) * rows]
        o_ref[...] = acc

    return pl.pallas_call(
        body, name="sum_devices", out_shape=jax.ShapeDtypeStruct((rows, cols), F32),
        in_specs=[pl.BlockSpec(memory_space=pltpu.VMEM)], out_specs=pl.BlockSpec(memory_space=pltpu.VMEM),
        compiler_params=pltpu.CompilerParams(vmem_limit_bytes=VMEM_LIMIT_V7X),
    )(gathered)


def _adamw_small(quads):
    n = len(quads)

    def body(*refs):
        ins, outs = refs[:4 * n], refs[4 * n:]
        for t in range(n):
            w, g, m, v = (r[...] for r in ins[4 * t:4 * t + 4])
            outs[3 * t][...], outs[3 * t + 1][...], outs[3 * t + 2][...] = _adamw(w, g, m, v)

    flat = [a for q in quads for a in q]
    vm = pl.BlockSpec(memory_space=pltpu.VMEM)
    res = pl.pallas_call(
        body, name="adamw_small",
        out_shape=[jax.ShapeDtypeStruct(q[0].shape, F32) for q in quads for _ in range(3)],
        in_specs=[vm] * (4 * n), out_specs=[vm] * (3 * n),
    )(*flat)
    return [tuple(res[3 * t:3 * t + 3]) for t in range(n)]


SMALL_PACK_ROWS = 96
META_COLS = D_MODEL // N_CHIPS
CONV_COLS = LRU_WIDTH // N_CHIPS
_WEIGHTS = ['meta_tokens', 'g_pre_mix', 'w_in', 'conv_w', 'conv_b', 'w_a', 'b_a', 'w_x', 'b_x', 'lru_lambda',
            'attn_sinks', 'w_out', 'g_post_mix', 'g_pre_ffn', 'w_ff1', 'w_ff2', 'g_post_ffn']
_BIG = ['w_in', 'w_out', 'w_ff1', 'w_ff2']


def _pack_small(dmeta, g, loss):
    z = lambda r, c: jnp.zeros((r, c), F32)
    rows = [
        dmeta,
        g['g_pre_mix'], g['g_post_mix'], g['g_pre_ffn'], g['g_post_ffn'],
        jnp.concatenate([g['conv_w'], z(4, 512)], axis=1),
        jnp.concatenate([g['conv_b'], g['b_a']], axis=1),
        jnp.concatenate([g['b_x'], g['lru_lambda']], axis=1),
        jnp.concatenate([g['attn_sinks'], z(1, D_MODEL - ATTN_HEADS)], axis=1),
        jnp.concatenate([loss, z(1, D_MODEL - 1)], axis=1),
        z(4, D_MODEL),
        g['w_a'].reshape(32, D_MODEL), g['w_x'].reshape(32, D_MODEL),
    ]
    return jnp.concatenate(rows, axis=0)


def _unpack_small(s, chip):
    return dict(
        meta_tokens=lax.dynamic_slice(s[0:N_META], (0, chip * META_COLS), (N_META, META_COLS)),
        g_pre_mix=s[16:17], g_post_mix=s[17:18], g_pre_ffn=s[18:19], g_post_ffn=s[19:20],
        conv_w=lax.dynamic_slice(s[20:24], (0, chip * CONV_COLS), (4, CONV_COLS)).reshape(1, 4, CONV_COLS),
        conv_b=s[24:25, :512], b_a=s[24:25, 512:], b_x=s[25:26, :512], lru_lambda=s[25:26, 512:],
        attn_sinks=s[26:27, :ATTN_HEADS], loss=s[27, 0],
        w_a=s[32:64].reshape(1, LRU_BLOCKS, LRU_BLOCK, LRU_BLOCK),
        w_x=s[64:96].reshape(1, LRU_BLOCKS, LRU_BLOCK, LRU_BLOCK))


def _as2d(a):
    if a.ndim == 2:
        return a
    return a.reshape(-1, a.shape[-1])


def kernel(x, meta_tokens, g_pre_mix, w_in, conv_w, conv_b, w_a, b_a, w_x, b_x, lru_lambda, attn_sinks, w_out, g_post_mix, g_pre_ffn, w_ff1, w_ff2, g_post_ffn, loss_target, m_meta_tokens, m_g_pre_mix, m_w_in, m_conv_w, m_conv_b, m_w_a, m_b_a, m_w_x, m_b_x, m_lru_lambda, m_attn_sinks, m_w_out, m_g_post_mix, m_g_pre_ffn, m_w_ff1, m_w_ff2, m_g_post_ffn, v_meta_tokens, v_g_pre_mix, v_w_in, v_conv_w, v_conv_b, v_w_a, v_b_a, v_w_x, v_b_x, v_lru_lambda, v_attn_sinks, v_w_out, v_g_post_mix, v_g_pre_ffn, v_w_ff1, v_w_ff2, v_g_post_ffn):
    weights = dict(meta_tokens=meta_tokens, g_pre_mix=g_pre_mix, w_in=w_in, conv_w=conv_w, conv_b=conv_b, w_a=w_a,
                   b_a=b_a, w_x=w_x, b_x=b_x, lru_lambda=lru_lambda, attn_sinks=attn_sinks, w_out=w_out,
                   g_post_mix=g_post_mix, g_pre_ffn=g_pre_ffn, w_ff1=w_ff1, w_ff2=w_ff2, g_post_ffn=g_post_ffn)
    mom1 = dict(zip(_WEIGHTS, [m_meta_tokens, m_g_pre_mix, m_w_in, m_conv_w, m_conv_b, m_w_a, m_b_a, m_w_x, m_b_x,
                               m_lru_lambda, m_attn_sinks, m_w_out, m_g_post_mix, m_g_pre_ffn, m_w_ff1, m_w_ff2,
                               m_g_post_ffn]))
    mom2 = dict(zip(_WEIGHTS, [v_meta_tokens, v_g_pre_mix, v_w_in, v_conv_w, v_conv_b, v_w_a, v_b_a, v_w_x, v_b_x,
                               v_lru_lambda, v_attn_sinks, v_w_out, v_g_post_mix, v_g_pre_ffn, v_w_ff1, v_w_ff2,
                               v_g_post_ffn]))
    xi, yi, ci = _mesh_pos()
    chip = 2 * xi + yi

    tiny = jnp.concatenate([meta_tokens, jnp.pad(conv_w[0], ((0, 4), (0, 128)))], axis=0)
    chip_arr = jnp.reshape(chip, (1,)).astype(jnp.int32)
    big2d = lambda a, name: a[0].T if name == 'w_in' else a[0]
    shards, lands = zip(*[_prep_shard(big2d(weights[n], n), chip_arr) for n in _BIG])
    g_in, g_tiny = _gather_weights(shards[:1], lands[:1], tiny, _prep_tiny(tiny, chip_arr))
    w_in_full = g_in.reshape(IN_WIDTH, D_MODEL)
    meta_full = jnp.concatenate([g_tiny[j, :N_META] for j in range(N_CHIPS)], axis=1)
    conv_w_full = jnp.concatenate([g_tiny[j, N_META:N_META + 4, :128] for j in range(N_CHIPS)], axis=1)
    g_send, g_recv, late_thru, late_lands, token = _split_start(
        "gather_late_start", _gather_copies, shards[1:], lands[1:])

    def late_weights(after):
        _, landed = _split_wait("gather_late_wait", _gather_copies, g_send, g_recv, late_thru, late_lands, after)
        g_out, g_f1, g_f2 = _gather_finish(landed)
        return g_out.reshape(D_MODEL, D_MODEL), g_f1, g_f2

    pos = jnp.stack([ci, chip]).astype(jnp.int32)
    ffn = {}


    def on_ffn_grads(dw1, dw2):
        parts = [dw1, dw2]
        lands = [lax.empty((p.shape[0], p.shape[1] // 2, p.shape[2]), p.dtype) for p in parts]
        ffn['sib'] = _split_start("sibling_ffn_start", _sibling_copies, parts, lands, len(parts))
        return ffn['sib'][4]

    def on_outproj_bwd(dattn):
        send, recv, thru, lands, _ = ffn['sib']
        parts, from_sibling = _split_wait("sibling_ffn_wait", _sibling_copies, send, recv, thru, lands, [dattn])
        cparts_ffn, lands_ffn = zip(*[_chip_presum(p, r, pos) for p, r in zip(parts, from_sibling)])
        ffn['send'], ffn['recv'], ffn['thru'], ffn['lands'], token3 = _split_start(
            "scatter_ffn_start", _scatter_copies, cparts_ffn, lands_ffn)
        return token3

    def on_mixer_grads(dw_in, dw_out):
        parts = [dw_in.reshape(N_CHIPS, IN_WIDTH // N_CHIPS, D_MODEL),
                 dw_out.reshape(N_CHIPS, D_MODEL // N_CHIPS, D_MODEL)]
        cparts, lands = zip(*[_chip_presum(p, r, pos) for p, r in zip(parts, _sibling_exchange(parts, pos))])
        ffn['mixer'] = _split_start("scatter_mixer_start", _scatter_copies, cparts, lands)
        return ffn['mixer'][4]

    head = jnp.concatenate([jnp.zeros((PAD_ROWS, D_MODEL), F32), meta_full], axis=0)
    loss, dx, dhead, grads = _local_step(head, x[0], loss_target[0], g_pre_mix, w_in_full, conv_w_full, conv_b, w_a[0],
                                         b_a, w_x[0], b_x, lru_lambda, attn_sinks, g_post_mix, g_pre_ffn, g_post_ffn,
                                         late_weights, on_ffn_grads, on_outproj_bwd, on_mixer_grads, token)
    grad_x = dx[None]

    pack = _pack_small(dhead[PAD_ROWS:], grads, loss)
    dev = jnp.reshape(4 * xi + 2 * yi + ci, (1,)).astype(jnp.int32)
    s_send, s_recv, s_thru, s_lands, token5 = _split_start(
        "gather_small_start", _all_peers_copies, [pack], [_prep_tiny(pack, dev, N_DEV)], N_DEV - 1)

    send, recv, thru, lands, _ = ffn['mixer']
    mixer_cparts, mixer_lands = _split_wait("scatter_mixer_wait", _scatter_copies, send, recv, thru, lands, [token5])
    ffn_cparts, ffn_lands = _split_wait("scatter_ffn_wait", _scatter_copies, ffn['send'], ffn['recv'], ffn['thru'],
                                        ffn['lands'], mixer_lands)
    chip_partials = _scatter_partials([], [], mixer_cparts + ffn_cparts, mixer_lands + ffn_lands)

    g_out_d, delta, new_m, new_v = {}, {}, {}, {}
    for name, part in zip(_BIG, chip_partials):
        shp = weights[name].shape
        res = _adamw_big(part, big2d(weights[name], name), big2d(mom1[name], name), big2d(mom2[name], name))
        g_out_d[name], delta[name], new_m[name], new_v[name] = (big2d(r[None], name).reshape(shp) for r in res)

    _, (gathered,) = _split_wait("gather_small_wait", _all_peers_copies, s_send, s_recv, s_thru, s_lands,
                                 [g_out_d[n] for n in _BIG])
    small = _unpack_small(_sum_devices(gathered.reshape(N_DEV * SMALL_PACK_ROWS, D_MODEL), SMALL_PACK_ROWS), chip)
    loss = small['loss']
    small_names = [n for n in _WEIGHTS if n not in _BIG]
    quads = [(_as2d(weights[n]), _as2d(small[n]), _as2d(mom1[n]), _as2d(mom2[n])) for n in small_names]
    for name, (d, m2, v2) in zip(small_names, _adamw_small(quads)):
        shp = weights[name].shape
        g_out_d[name] = small[name].reshape(shp)
        delta[name], new_m[name], new_v[name] = d.reshape(shp), m2.reshape(shp), v2.reshape(shp)

    return (loss, grad_x, *[g_out_d[n] for n in _WEIGHTS], *[delta[n] for n in _WEIGHTS],
            *[new_m[n] for n in _WEIGHTS], *[new_v[n] for n in _WEIGHTS])
```

```python
import numpy as np
import jax
import jax.numpy as jnp
from jax import lax
from jax.experimental import pallas as pl
from jax.experimental.pallas import tpu as pltpu

F32 = jnp.float32
BF16 = jnp.bfloat16

D_MODEL = 1024
N_META = 16
BLOCK = 128
PAD_ROWS = BLOCK - N_META
HEAD_DIM = 64
ATTN_HEADS = 8
GQA_GROUP = 4
ATTN_WIDTH = 512
KV_WIDTH = 128
QKV_WIDTH = ATTN_WIDTH + 2 * KV_WIDTH
LRU_WIDTH = 512
LRU_BLOCKS = 8
LRU_BLOCK = 64
LRU_C = 8.0
IN_WIDTH = 1792
D_FF = 4096
N_CHIPS = 4
FF_CHUNK = D_FF // N_CHIPS
EPS = 1e-6
NEG = -1e30

ADAM_LR = 0.001
ADAM_B1 = 0.9
ADAM_B2 = 0.999
ADAM_EPS = 1e-08
ADAM_WD = 0.01
ADAM_STEP = 10

VMEM_LIMIT_V7X = 62 * 1024 * 1024
MESH = pl.DeviceIdType.MESH

NT = (((1,), (1,)), ((), ()))
TN = (((0,), (0,)), ((), ()))


def _row_tile(tp):
    return 640 if tp % 640 == 0 else BLOCK


def _wgrad_row_tile(tp):
    return 1664 if tp % 1664 == 0 else _row_tile(tp)


def _params(*sem):
    return pltpu.CompilerParams(dimension_semantics=sem, vmem_limit_bytes=VMEM_LIMIT_V7X)


def _dot(a, b):
    return jnp.dot(a, b, preferred_element_type=F32)


def _dot_nt(a, b):
    return lax.dot_general(a, b, NT, preferred_element_type=F32)


def _dot_tn(a, b):
    return lax.dot_general(a, b, TN, preferred_element_type=F32)


def _rms(x):
    rs = lax.rsqrt(jnp.mean(x * x, axis=-1, keepdims=True) + EPS)
    return x * rs, rs


def _rms_bwd(xhat, rs, g, dy):
    dyg = dy * g
    dx = rs * (dyg - xhat * jnp.mean(dyg * xhat, axis=-1, keepdims=True))
    dg = jnp.sum(dy * xhat, axis=0, keepdims=True)
    return dx, dg


def _gelu(x):
    k = 0.7978845608028654
    t = jnp.tanh(x * (k + (k * 0.044715) * (x * x)))
    return (0.5 * x) * (1.0 + t), t


def _gelu_grad(x, t):
    k = 0.7978845608028654
    return 0.5 * (1.0 + t) + 0.5 * x * (1.0 - t * t) * k * (1.0 + 3 * 0.044715 * x * x)


def _sigmoid(x):
    return 0.5 * jnp.tanh(0.5 * x) + 0.5


def _one_minus_exp2(y):
    t = jnp.tanh(y)
    return (-2.0 * t) / (1.0 - t)


def _softplus(x):
    return jnp.maximum(x, 0.0) + jnp.log1p(jnp.exp(-jnp.abs(x)))


def _seq_specs(tr, delay=0):
    qb = tr // BLOCK
    tile = lambda i: jnp.maximum(i - delay, 0)
    return [pl.BlockSpec((BLOCK, D_MODEL), lambda i, *_, s=s: (jnp.maximum(tile(i) * qb + s - 1, 0), 0))
            for s in range(qb)]


def _seq_tile(head, pieces, i):
    first = jnp.where(i == 0, head, pieces[0][...])
    return jnp.concatenate([first] + [p[...] for p in pieces[1:]], axis=0)


GROUP_ROWS = GQA_GROUP * BLOCK


def _attn_bias():
    j = np.arange(2 * BLOCK)[:, None]
    i = np.arange(BLOCK)[None, :]
    band = (j - i >= 1) & (j - i <= BLOCK)
    out = []
    for n in range(3):
        ok = band & ((n - 1) * BLOCK + j >= PAD_ROWS) if n < 2 else band
        out.append(np.tile(np.where(ok, 0.0, NEG).astype(np.float32), (1, GQA_GROUP)))
    return jnp.asarray(np.stack(out))


def _heads_t(at, g):
    heads = range(GQA_GROUP * g, GQA_GROUP * (g + 1))
    return jnp.concatenate([at[h * HEAD_DIM:(h + 1) * HEAD_DIM] for h in heads], axis=1).astype(BF16)


def _from_heads_t(groups):
    pairs = []
    for p in groups:
        for h in range(0, GQA_GROUP, 2):
            two = jnp.concatenate([p[:, h * BLOCK:(h + 1) * BLOCK], p[:, (h + 1) * BLOCK:(h + 2) * BLOCK]], axis=0)
            pairs.append(two.T)
    return jnp.concatenate(pairs, axis=1)


def _stack_heads(a, g):
    heads = range(GQA_GROUP * g, GQA_GROUP * (g + 1))
    return jnp.concatenate([a[:, h * HEAD_DIM:(h + 1) * HEAD_DIM] for h in heads], axis=0)


def _unstack_heads(groups):
    return jnp.concatenate([p[h * BLOCK:(h + 1) * BLOCK] for p in groups for h in range(GQA_GROUP)], axis=1)


def _attn_probs_t(k_g, qg, bias, sink_row):
    st = _dot_nt(k_g, qg) + bias
    m = jnp.maximum(jnp.max(st, axis=0, keepdims=True), sink_row)
    p = jnp.exp(st - m)
    es = jnp.exp(sink_row - m)
    inv = 1.0 / (jnp.sum(p, axis=0, keepdims=True) + es)
    return p * inv, es * inv


def _attn_consts(sinks):
    return jnp.repeat(sinks.reshape(ATTN_HEADS), BLOCK).reshape(ATTN_HEADS // GQA_GROUP, GROUP_ROWS), _attn_bias()


_SINK_SPEC = pl.BlockSpec((ATTN_HEADS // GQA_GROUP, GROUP_ROWS), lambda n: (0, 0))
_BIAS_SPEC = pl.BlockSpec((3, 2 * BLOCK, GROUP_ROWS), lambda n: (0, 0, 0))
_QSCALE = HEAD_DIM ** -0.5


def _kv_specs(tr):
    qb = tr // BLOCK
    prev = lambda col: pl.BlockSpec((BLOCK, KV_WIDTH), lambda t: (jnp.maximum(t * qb - 1, 0), col))
    cur = lambda col: pl.BlockSpec((tr, KV_WIDTH), lambda t: (t, col))
    return [prev(4), cur(4), prev(5), cur(5)]


def _block_bias(b_ref, t, qb, i):
    return b_ref[2] if i >= 2 else b_ref[jnp.minimum(t * qb + i, 2)]


N_KV = ATTN_HEADS // GQA_GROUP


def _prob_specs(qb):
    return [pl.BlockSpec((qb, N_KV, 2 * BLOCK, GROUP_ROWS), lambda t: (t, 0, 0, 0)),
            pl.BlockSpec((qb, SUBLANES, GROUP_ROWS), lambda t: (t, 0, 0))]


def _attn_fwd(qkv, sinks):
    tp = qkv.shape[0]
    tr = _row_tile(tp)
    qb, nb = tr // BLOCK, tp // BLOCK
    sink_rows, bias = _attn_consts(sinks)

    def body(s_ref, b_ref, q_ref, kp_ref, kc_ref, vp_ref, vc_ref, o_ref, p_ref, ps_ref):
        t = pl.program_id(0)
        k_all = jnp.concatenate([kp_ref[...], kc_ref[...]], axis=0)
        v_all = jnp.concatenate([vp_ref[...], vc_ref[...]], axis=0)
        for i in range(qb):
            rows = slice(i * BLOCK, (i + 1) * BLOCK)
            q = q_ref[rows]
            k2, v2 = k_all[i * BLOCK:(i + 2) * BLOCK], v_all[i * BLOCK:(i + 2) * BLOCK]
            bias_n = _block_bias(b_ref, t, qb, i)
            outs, sink_probs = [], []
            for g in range(N_KV):
                cols = slice(g * HEAD_DIM, (g + 1) * HEAD_DIM)
                qg = _stack_heads(q, g) * jnp.asarray(_QSCALE, BF16)
                p, ps = _attn_probs_t(k2[:, cols], qg, bias_n, s_ref[g:g + 1])
                pb = p.astype(BF16)
                p_ref[i, g] = pb
                sink_probs.append(ps)
                outs.append(_dot_tn(pb, v2[:, cols]))
            o_ref[rows] = _unstack_heads(outs).astype(BF16)
            ps_ref[i] = jnp.concatenate(sink_probs + [jnp.zeros((SUBLANES - N_KV, GROUP_ROWS), F32)], axis=0)

    return pl.pallas_call(
        body, name="attn_fwd", grid=(tp // tr,),
        in_specs=[_SINK_SPEC, _BIAS_SPEC, pl.BlockSpec((tr, ATTN_WIDTH), lambda t: (t, 0))] + _kv_specs(tr),
        out_specs=[pl.BlockSpec((tr, ATTN_WIDTH), lambda t: (t, 0))] + _prob_specs(qb),
        out_shape=[jax.ShapeDtypeStruct((tp, ATTN_WIDTH), BF16),
                   jax.ShapeDtypeStruct((nb, N_KV, 2 * BLOCK, GROUP_ROWS), BF16),
                   jax.ShapeDtypeStruct((nb, SUBLANES, GROUP_ROWS), F32)],
        compiler_params=_params("parallel"),
    )(sink_rows, bias, qkv, qkv, qkv, qkv, qkv)


def _conv_taps(x, halo):
    ext = jnp.concatenate([halo, x], axis=0)
    return [ext[8:] if k == 3 else pltpu.roll(ext, 3 - k, 0)[8:] for k in range(4)]


def _lru_gates(xc, wa, ba, wx, bx, sp):
    xb = xc.astype(BF16)
    r = _sigmoid(_dot(xb, wa) + ba)
    ig = _sigmoid(_dot(xb, wx) + bx)
    log_a = (-LRU_C * sp) * r
    a = jnp.exp(log_a)
    mult = jnp.sqrt(_one_minus_exp2(log_a))
    return xb, r, ig, a, mult


SUBLANES = 8


def _scan_fwd(a, b, h_in):
    n, width = a.shape
    a, b = (v.reshape(n // SUBLANES, SUBLANES, width) for v in (a, b))
    in_group = lax.broadcasted_iota(jnp.int32, a.shape, 1)
    for d in (1, 2, 4):
        keep = in_group >= d
        b = jnp.where(keep, a * pltpu.roll(b, d, 1) + b, b)
        a = jnp.where(keep, a * pltpu.roll(a, d, 1), a)
    a, b = a.reshape(n, width), b.reshape(n, width)
    out, carry = [], h_in
    for g in range(0, n, SUBLANES):
        h = a[g:g + SUBLANES] * carry + b[g:g + SUBLANES]
        out.append(h)
        carry = h[SUBLANES - 1:]
    return jnp.concatenate(out, axis=0)


def _scan_rev(c, b, g_in):
    n, width = c.shape
    c, b = (v.reshape(n // SUBLANES, SUBLANES, width) for v in (c, b))
    in_group = lax.broadcasted_iota(jnp.int32, c.shape, 1)
    for d in (1, 2, 4):
        keep = in_group < SUBLANES - d
        b = jnp.where(keep, b + c * pltpu.roll(b, SUBLANES - d, 1), b)
        c = jnp.where(keep, c * pltpu.roll(c, SUBLANES - d, 1), c)
    c, b = c.reshape(n, width), b.reshape(n, width)
    out, carry = [], g_in
    for g in range(n - SUBLANES, -1, -SUBLANES):
        r = b[g:g + SUBLANES] + c[g:g + SUBLANES] * carry
        out.append(r)
        carry = r[:1]
    return jnp.concatenate(out[::-1], axis=0)


def _inproj_lru_fwd(head, x, g, w_in, conv_w, conv_b, wa, ba, wx, bx, lam, token):
    tp = BLOCK + x.shape[0]
    tr = _row_tile(tp)
    qb, nt = tr // BLOCK, tp // tr
    small = [conv_w, conv_b, wa, ba, wx, bx, lam]

    def body(*refs):
        head_ref, pieces = refs[0], refs[1:1 + qb]
        g_ref, w_ref, _, cw_ref, cb_ref, wa_ref, ba_ref, wx_ref, bx_ref, lam_ref = refs[1 + qb:11 + qb]
        u_ref, qkv_ref, xr_ref, yr_ref, hr_ref, rec_ref, zbuf, halo, hprev = refs[11 + qb:]
        i = pl.program_id(0)
        cur = i % 2

        @pl.when(i == 0)
        def _():
            halo[...] = jnp.zeros_like(halo)
            hprev[...] = jnp.zeros_like(hprev)
            zbuf[1] = jnp.zeros((tr, 2 * LRU_WIDTH), F32)

        def recurrent_branch(valid):
            cw, cb = cw_ref[...], cb_ref[...]
            wa_m, ba_v, wx_m, bx_v = wa_ref[...], ba_ref[...], wx_ref[...], bx_ref[...]
            sp = _softplus(-lam_ref[...])
            before, h_last = halo[...], hprev[0:1]
            for b in range(qb):
                rows = slice(b * BLOCK, (b + 1) * BLOCK)
                xy = zbuf[1 - cur, rows]
                xin = xy[:, :LRU_WIDTH]
                taps = _conv_taps(xin, before)
                before = xin[BLOCK - 8:]
                xc = cb + sum(cw[k:k + 1] * taps[k] for k in range(4))
                _, _, ig, a, mult = _lru_gates(xc, wa_m, ba_v, wx_m, bx_v, sp)
                u = mult * (ig * xc)
                if b == 0:
                    pos = (i - 1) * tr + lax.broadcasted_iota(jnp.int32, xc.shape, 0)
                    u = jnp.where(pos >= PAD_ROWS, u, 0.0)
                h = _scan_fwd(a, u, h_last)
                h_last = h[BLOCK - 1:]
                hr_ref[rows] = h
                gl, _ = _gelu(xy[:, LRU_WIDTH:])
                rec_ref[rows] = (gl * h).astype(BF16)
            halo[...] = jnp.where(valid, before, 0.0)
            hprev[0:1] = jnp.where(valid, h_last, 0.0)

        def projection():
            xhat, _ = _rms(_seq_tile(head_ref[...], pieces, i))
            u = (xhat * g_ref[...]).astype(BF16)
            u_ref[...] = u
            z = _dot_nt(u, w_ref[...])
            qkv_ref[...] = z[:, :QKV_WIDTH].astype(BF16)
            xr_ref[...] = z[:, QKV_WIDTH:QKV_WIDTH + LRU_WIDTH]
            yr_ref[...] = z[:, QKV_WIDTH + LRU_WIDTH:]
            zbuf[cur] = z[:, QKV_WIDTH:]

        @pl.when(i < nt)
        def _():
            recurrent_branch(i >= 1)
            projection()

        @pl.when(i == nt)
        def _():
            recurrent_branch(True)

    last = nt - 1
    this_row = lambda w: pl.BlockSpec((tr, w), lambda i: (jnp.minimum(i, last), 0))
    prev_row = lambda w: pl.BlockSpec((tr, w), lambda i: (jnp.maximum(i - 1, 0), 0))
    full = lambda a: pl.BlockSpec(a.shape, lambda i: (0,) * a.ndim)
    piece_specs = [pl.BlockSpec((BLOCK, D_MODEL), lambda i, s=s: (jnp.maximum(jnp.minimum(i, last) * qb + s - 1, 0), 0))
                   for s in range(qb)]
    return pl.pallas_call(
        body, name="inproj_lru_fwd", grid=(nt + 1,),
        in_specs=[full(head)] + piece_specs + [full(g), full(w_in), full(token)] + [full(a) for a in small],
        out_specs=[this_row(D_MODEL), this_row(QKV_WIDTH), this_row(LRU_WIDTH), this_row(LRU_WIDTH),
                   prev_row(LRU_WIDTH), prev_row(LRU_WIDTH)],
        out_shape=[jax.ShapeDtypeStruct((tp, D_MODEL), BF16), jax.ShapeDtypeStruct((tp, QKV_WIDTH), BF16),
                   jax.ShapeDtypeStruct((tp, LRU_WIDTH), F32), jax.ShapeDtypeStruct((tp, LRU_WIDTH), F32),
                   jax.ShapeDtypeStruct((tp, LRU_WIDTH), F32), jax.ShapeDtypeStruct((tp, LRU_WIDTH), BF16)],
        scratch_shapes=[pltpu.VMEM((2, tr, 2 * LRU_WIDTH), F32), pltpu.VMEM((8, LRU_WIDTH), F32),
                        pltpu.VMEM((8, LRU_WIDTH), F32)],
        compiler_params=_params("arbitrary"),
    )(head, *([x] * qb), g, w_in, token, *small)


def _outproj_fwd(attn, rec, w_out, head, x, g_post_mix, g_pre_ffn):
    tp = attn.shape[0]
    tr = _row_tile(tp)
    qb = tr // BLOCK

    def body(*refs):
        a_ref, r_ref, w_ref, head_ref = refs[:4]
        pieces = refs[4:4 + qb]
        gm_ref, gf_ref, mix_ref, h1_ref, u1_ref = refs[4 + qb:]
        mix = _dot(a_ref[...], w_ref[:ATTN_WIDTH]) + _dot(r_ref[...], w_ref[ATTN_WIDTH:])
        mix_ref[...] = mix
        mhat, _ = _rms(mix)
        h1 = _seq_tile(head_ref[...], pieces, pl.program_id(0)) + mhat * gm_ref[...]
        h1_ref[...] = h1
        hhat, _ = _rms(h1)
        u1_ref[...] = (hhat * gf_ref[...]).astype(BF16)

    row = lambda w: pl.BlockSpec((tr, w), lambda i: (i, 0))
    full = lambda a: pl.BlockSpec(a.shape, lambda i: (0,) * a.ndim)
    return pl.pallas_call(
        body, name="outproj_fwd", grid=(tp // tr,),
        in_specs=[row(ATTN_WIDTH), row(LRU_WIDTH), full(w_out), full(head)] + _seq_specs(tr)
        + [full(g_post_mix), full(g_pre_ffn)],
        out_specs=[row(D_MODEL), row(D_MODEL), row(D_MODEL)],
        out_shape=[jax.ShapeDtypeStruct((tp, D_MODEL), F32), jax.ShapeDtypeStruct((tp, D_MODEL), F32),
                   jax.ShapeDtypeStruct((tp, D_MODEL), BF16)],
        compiler_params=_params("parallel"),
    )(attn, rec, w_out, head, *([x] * qb), g_post_mix, g_pre_ffn)


FFN_STEPS = N_CHIPS


def _resident(a):
    return pl.BlockSpec(a.shape, lambda *_: (0,) * a.ndim, pipeline_mode=pl.Buffered(1))


def _ffn_fwd(u1, w1, w2, h1, tgt, g_post_ffn):
    tp = h1.shape[0]
    tr = _row_tile(tp)
    qb, nt = tr // BLOCK, tp // tr
    sr = tr // FFN_STEPS

    def body(*refs):
        u_ref, w1_ref, w2_ref, h1_ref = refs[:4]
        t_pieces = refs[4:4 + qb]
        g_ref, r1_ref, dy_ref, df2_ref, loss_ref, dg_ref, acc = refs[4 + qb:]
        i, c = pl.program_id(0), pl.program_id(1)
        cur = i % 2

        @pl.when((i == 0) & (c == 0))
        def _():
            loss_ref[...] = jnp.zeros_like(loss_ref)
            dg_ref[...] = jnp.zeros_like(dg_ref)
            acc[1] = jnp.zeros((tr, D_MODEL), F32)

        def matmuls():
            r = jnp.maximum(_dot(u_ref[...], w1_ref[c]), 0.0)
            r1_ref[...] = r.astype(BF16)
            return _dot((r * r).astype(BF16), w2_ref[c])

        def finish_previous_tile(k, valid):
            lo, hi = k * sr, (k + 1) * sr
            g = g_ref[...]
            fhat, rs = _rms(acc[1 - cur, lo:hi])
            h2 = h1_ref[...] + fhat * g
            rows = (i - 1) * tr + lo + lax.broadcasted_iota(jnp.int32, h2.shape, 0)
            tgt = jnp.concatenate([p[max(lo - s * BLOCK, 0):min(hi - s * BLOCK, BLOCK)] for s, p in enumerate(t_pieces)
                                   if lo < (s + 1) * BLOCK and hi > s * BLOCK], axis=0)
            err = jnp.where((rows >= BLOCK) & valid, h2 - tgt, 0.0)
            dy = err * (1.0 / D_MODEL)
            dy_ref[...] = dy
            loss_ref[...] += (0.5 / D_MODEL) * jnp.sum(err * err)
            df2, dg = _rms_bwd(fhat, rs, g, dy)
            df2_ref[...] = df2.astype(BF16)
            dg_ref[...] += dg

        for k in range(FFN_STEPS):
            @pl.when((c == k) & (i < nt))
            def _(k=k):
                finish_previous_tile(k, i >= 1)
                if k == 0:
                    acc[cur] = matmuls()
                else:
                    acc[cur] += matmuls()

            @pl.when((c == k) & (i == nt))
            def _(k=k):
                finish_previous_tile(k, True)

    last = nt - 1
    this_row = pl.BlockSpec((tr, D_MODEL), lambda i, c: (jnp.minimum(i, last), 0))
    prev_quarter = pl.BlockSpec((sr, D_MODEL), lambda i, c: (jnp.maximum(i - 1, 0) * FFN_STEPS + c, 0))
    prev_quarter_out = pl.BlockSpec(
        (sr, D_MODEL), lambda i, c: (jnp.where(i == 0, nt * FFN_STEPS, (i - 1) * FFN_STEPS + c), 0))
    full = lambda a: pl.BlockSpec(a.shape, lambda i, c: (0,) * a.ndim)
    return pl.pallas_call(
        body, name="ffn_fwd", grid=(nt + 1, FFN_STEPS),
        in_specs=[this_row, _resident(w1), _resident(w2), prev_quarter] + _seq_specs(tr, delay=1) + [full(g_post_ffn)],
        out_specs=[pl.BlockSpec((tr, FF_CHUNK), lambda i, c: (jnp.minimum(i, last), jnp.where(i < nt, c, FFN_STEPS - 1))),
                   prev_quarter_out, prev_quarter_out,
                   pl.BlockSpec((1, 1), lambda i, c: (0, 0)), pl.BlockSpec((1, D_MODEL), lambda i, c: (0, 0))],
        out_shape=[jax.ShapeDtypeStruct((tp, D_FF), BF16), jax.ShapeDtypeStruct((tp + sr, D_MODEL), F32),
                   jax.ShapeDtypeStruct((tp + sr, D_MODEL), BF16), jax.ShapeDtypeStruct((1, 1), F32),
                   jax.ShapeDtypeStruct((1, D_MODEL), F32)],
        scratch_shapes=[pltpu.VMEM((2, tr, D_MODEL), F32)],
        compiler_params=_params("arbitrary", "arbitrary"),
    )(u1, w1, w2, h1, *([tgt] * qb), g_post_ffn)


def _ffn_bwd_data(df2, r1, w1, w2, dy, h1, mix, g_pre_ffn, g_post_mix):
    tp = h1.shape[0]
    tr = _row_tile(tp)
    nt = tp // tr
    sr = tr // FFN_STEPS

    def body(df2_ref, r1_ref, w1_ref, w2_ref, dy_ref, h1_ref, mix_ref, gf_ref, gm_ref,
             da_ref, dh1_ref, dmix_ref, dgf_ref, dgm_ref, acc):
        i, c = pl.program_id(0), pl.program_id(1)
        cur = i % 2

        @pl.when((i == 0) & (c == 0))
        def _():
            dgf_ref[...] = jnp.zeros_like(dgf_ref)
            dgm_ref[...] = jnp.zeros_like(dgm_ref)
            acc[1] = jnp.zeros((tr, D_MODEL), F32)

        def matmuls():
            df = _dot_nt(df2_ref[...], w2_ref[c])
            da = (df * (2.0 * r1_ref[...].astype(F32))).astype(BF16)
            da_ref[...] = da
            return _dot_nt(da, w1_ref[c])

        def finish_previous_tile(k, valid):
            lo, hi = k * sr, (k + 1) * sr
            hhat, rs = _rms(h1_ref[...])
            dx, dgf = _rms_bwd(hhat, rs, gf_ref[...], acc[1 - cur, lo:hi])
            dh1 = dy_ref[...] + dx
            dh1_ref[...] = dh1
            mhat, rsm = _rms(mix_ref[...])
            dmix, dgm = _rms_bwd(mhat, rsm, gm_ref[...], dh1)
            dmix_ref[...] = dmix.astype(BF16)
            dgf_ref[...] += jnp.where(valid, dgf, 0.0)
            dgm_ref[...] += jnp.where(valid, dgm, 0.0)

        for k in range(FFN_STEPS):
            @pl.when((c == k) & (i < nt))
            def _(k=k):
                finish_previous_tile(k, i >= 1)
                if k == 0:
                    acc[cur] = matmuls()
                else:
                    acc[cur] += matmuls()

            @pl.when((c == k) & (i == nt))
            def _(k=k):
                finish_previous_tile(k, True)

    last = nt - 1
    this_row = pl.BlockSpec((tr, D_MODEL), lambda i, c: (jnp.minimum(i, last), 0))
    prev_quarter = pl.BlockSpec((sr, D_MODEL), lambda i, c: (jnp.maximum(i - 1, 0) * FFN_STEPS + c, 0))
    prev_quarter_out = pl.BlockSpec(
        (sr, D_MODEL), lambda i, c: (jnp.where(i == 0, nt * FFN_STEPS, (i - 1) * FFN_STEPS + c), 0))
    chunk = pl.BlockSpec((tr, FF_CHUNK), lambda i, c: (jnp.minimum(i, last), jnp.where(i < nt, c, FFN_STEPS - 1)))
    gain = pl.BlockSpec((1, D_MODEL), lambda i, c: (0, 0))
    return pl.pallas_call(
        body, name="ffn_bwd_data", grid=(nt + 1, FFN_STEPS),
        in_specs=[this_row, chunk, _resident(w1), _resident(w2), prev_quarter, prev_quarter, prev_quarter, gain, gain],
        out_specs=[chunk, prev_quarter_out, prev_quarter_out, gain, gain],
        out_shape=[jax.ShapeDtypeStruct((tp, D_FF), BF16), jax.ShapeDtypeStruct((tp + sr, D_MODEL), F32),
                   jax.ShapeDtypeStruct((tp + sr, D_MODEL), BF16), jax.ShapeDtypeStruct((1, D_MODEL), F32),
                   jax.ShapeDtypeStruct((1, D_MODEL), F32)],
        scratch_shapes=[pltpu.VMEM((2, tr, D_MODEL), F32)],
        compiler_params=_params("arbitrary", "arbitrary"),
    )(df2, r1, w1, w2, dy, h1, mix, g_pre_ffn, g_post_mix)


def _ffn_bwd_weights(u1, da1, r1, df2):
    tp = u1.shape[0]
    tr = _wgrad_row_tile(tp)

    def body(u_ref, da_ref, r1_ref, df2_ref, dw1_ref, dw2_ref):
        i = pl.program_id(1)

        def products():
            r = r1_ref[...].astype(F32)
            return _dot_tn(u_ref[...], da_ref[...]), _dot_tn((r * r).astype(BF16), df2_ref[...])

        @pl.when(i == 0)
        def _():
            dw1_ref[0], dw2_ref[0] = products()

        @pl.when(i > 0)
        def _():
            p1, p2 = products()
            dw1_ref[0] += p1
            dw2_ref[0] += p2

    row = pl.BlockSpec((tr, D_MODEL), lambda c, i: (i, 0))
    chunk = pl.BlockSpec((tr, FF_CHUNK), lambda c, i: (i, c))
    return pl.pallas_call(
        body, name="ffn_bwd_weights", grid=(N_CHIPS, tp // tr),
        in_specs=[row, chunk, chunk, row],
        out_specs=[pl.BlockSpec((1, D_MODEL, FF_CHUNK), lambda c, i: (c, 0, 0)),
                   pl.BlockSpec((1, FF_CHUNK, D_MODEL), lambda c, i: (c, 0, 0))],
        out_shape=[jax.ShapeDtypeStruct((N_CHIPS, D_MODEL, FF_CHUNK), F32),
                   jax.ShapeDtypeStruct((N_CHIPS, FF_CHUNK, D_MODEL), F32)],
        compiler_params=_params("parallel", "arbitrary"),
    )(u1, da1, r1, df2)


N_VEC_ROWS = 8


def _outproj_lru_bwd(dmix, w_out, attn, rec, xr, yr, hr, conv_w, conv_b, wa, ba, wx, bx, lam, token):
    tp = xr.shape[0]
    tr = _row_tile(tp)
    qb, nt = tr // BLOCK, tp // tr

    def body(dm_ref, w_ref, at_ref, rc_ref, xr_ref, xh_ref, yr_ref, hr_ref, hp_ref,
             cw_ref, cb_ref, wa_ref, ba_ref, wx_ref, bx_ref, lam_ref, _,
             dxr_ref, dyr_ref, dat_ref, dwo_ref, dwa_ref, dwx_ref, vec_ref, g_next, a_next, dxc_next, dsp):
        s = pl.program_id(0)
        t = nt - 1 - s

        @pl.when(s == 0)
        def _():
            g_next[...] = jnp.zeros_like(g_next)
            a_next[...] = jnp.zeros_like(a_next)
            dxc_next[...] = jnp.zeros_like(dxc_next)
            dsp[...] = jnp.zeros_like(dsp)
            dwo_ref[...] = jnp.zeros_like(dwo_ref)
            dwa_ref[...] = jnp.zeros_like(dwa_ref)
            dwx_ref[...] = jnp.zeros_like(dwx_ref)
            vec_ref[...] = jnp.zeros_like(vec_ref)

        dm = dm_ref[...]
        dcat = _dot_nt(dm, w_ref[...])
        dat_ref[...] = dcat[:, :ATTN_WIDTH].astype(BF16)
        drec_tile = dcat[:, ATTN_WIDTH:]
        dwo_ref[:ATTN_WIDTH] += _dot_tn(at_ref[...], dm)
        dwo_ref[ATTN_WIDTH:] += _dot_tn(rc_ref[...], dm)

        first_tile = t == 0
        cw, cb = cw_ref[...], cb_ref[...]
        lam_v = lam_ref[...]
        sp = _softplus(-lam_v)
        wa_m, ba_v, wx_m, bx_v = wa_ref[...], ba_ref[...], wx_ref[...], bx_ref[...]
        rows = lax.broadcasted_iota(jnp.int32, (BLOCK, LRU_WIDTH), 0)
        col = lambda v: jnp.sum(v, axis=0, keepdims=True)

        g_after, a_after, dxc_after = g_next[0:1], a_next[0:1], dxc_next[...]
        xbs, dgrs, dgis = [], [], []
        vec = [jnp.zeros((1, LRU_WIDTH), F32) for _ in range(N_VEC_ROWS)]
        for i in reversed(range(qb)):
            blk = slice(i * BLOCK, (i + 1) * BLOCK)
            if i == 0:
                x_before = jnp.where(first_tile, 0.0, xh_ref[...])
                h_before = jnp.where(first_tile, 0.0, hp_ref[7:8])
            else:
                x_before = xr_ref[i * BLOCK - 8:i * BLOCK]
                h_before = hr_ref[i * BLOCK - 1:i * BLOCK]
            taps = _conv_taps(xr_ref[blk], x_before)
            xc = cb + sum(cw[k:k + 1] * taps[k] for k in range(4))
            xb, r, ig, a, mult = _lru_gates(xc, wa_m, ba_v, wx_m, bx_v, sp)

            yr_v = yr_ref[blk]
            gl, th = _gelu(yr_v)
            h = hr_ref[blk]
            drec = drec_tile[blk]
            dyr_ref[blk] = (drec * h * _gelu_grad(yr_v, th)).astype(BF16)

            a_up = jnp.where(rows == BLOCK - 1, a_after, pltpu.roll(a, BLOCK - 1, 0))
            g = _scan_rev(a_up, drec * gl, g_after)
            g_after, a_after = g[0:1], a[0:1]

            h_prev = jnp.where(rows == 0, h_before, pltpu.roll(h, 1, 0))
            du, da = g, g * h_prev
            if i == 0:
                real = (t * tr + rows) >= PAD_ROWS
                du, da = jnp.where(real, du, 0.0), jnp.where(real, da, 0.0)
            dmult = du * (ig * xc)
            dig = du * (mult * xc)
            dxc = du * (mult * ig)
            dlog_a = da * a - dmult * (a * a / mult)
            if i == 0:
                dlog_a = jnp.where(real, dlog_a, 0.0)
            dgr = (dlog_a * (-LRU_C * sp)) * (r * (1.0 - r))
            dgi = dig * (ig * (1.0 - ig))
            dgr_b, dgi_b = dgr.astype(BF16), dgi.astype(BF16)
            dxc = dxc + _dot_nt(dgr_b, wa_m) + _dot_nt(dgi_b, wx_m)
            xbs.append(xb)
            dgrs.append(dgr_b)
            dgis.append(dgi_b)

            ext = jnp.concatenate([dxc, dxc_after], axis=0)
            up = [ext[:BLOCK] if j == 0 else pltpu.roll(ext, BLOCK + 8 - j, 0)[:BLOCK] for j in range(4)]
            dxr_ref[blk] = sum(cw[k:k + 1] * up[3 - k] for k in range(4)).astype(BF16)
            dxc_after = dxc[:8]

            for k in range(4):
                vec[k] = vec[k] + col(dxc * taps[k])
            vec[4] = vec[4] + col(dxc)
            vec[5] = vec[5] + col(dgr)
            vec[6] = vec[6] + col(dgi)
            vec[7] = vec[7] + col(dlog_a * (-LRU_C * r))

        g_next[0:1], a_next[0:1], dxc_next[...] = g_after, a_after, dxc_after
        xb_all = jnp.concatenate(xbs, axis=0)
        dwa_ref[...] += _dot_tn(xb_all, jnp.concatenate(dgrs, axis=0))
        dwx_ref[...] += _dot_tn(xb_all, jnp.concatenate(dgis, axis=0))
        for k in range(7):
            vec_ref[k:k + 1] += vec[k]
        dsp[0:1] += vec[7]

        @pl.when(s == nt - 1)
        def _():
            vec_ref[7:8] = dsp[0:1] * (-_sigmoid(-lam_v))

    blk_spec = pl.BlockSpec((tr, LRU_WIDTH), lambda s: (nt - 1 - s, 0))
    rows_before = pl.BlockSpec((8, LRU_WIDTH), lambda s: (jnp.maximum((nt - 1 - s) * (tr // 8) - 1, 0), 0))
    full = lambda a: pl.BlockSpec(a.shape, lambda s: (0,) * a.ndim)
    small = [conv_w, conv_b, wa, ba, wx, bx, lam, token]
    sq = pl.BlockSpec((LRU_WIDTH, LRU_WIDTH), lambda s: (0, 0))
    wide = pl.BlockSpec((tr, D_MODEL), lambda s: (nt - 1 - s, 0))
    whole = pl.BlockSpec((D_MODEL, D_MODEL), lambda s: (0, 0))
    return pl.pallas_call(
        body, name="outproj_lru_bwd", grid=(nt,),
        in_specs=[wide, whole, blk_spec, blk_spec, blk_spec, rows_before, blk_spec, blk_spec, rows_before]
        + [full(a) for a in small],
        out_specs=[blk_spec, blk_spec, blk_spec, whole, sq, sq, pl.BlockSpec((N_VEC_ROWS, LRU_WIDTH), lambda s: (0, 0))],
        out_shape=[jax.ShapeDtypeStruct((tp, LRU_WIDTH), BF16), jax.ShapeDtypeStruct((tp, LRU_WIDTH), BF16),
                   jax.ShapeDtypeStruct((tp, ATTN_WIDTH), BF16), jax.ShapeDtypeStruct((D_MODEL, D_MODEL), F32),
                   jax.ShapeDtypeStruct((LRU_WIDTH, LRU_WIDTH), F32), jax.ShapeDtypeStruct((LRU_WIDTH, LRU_WIDTH), F32),
                   jax.ShapeDtypeStruct((N_VEC_ROWS, LRU_WIDTH), F32)],
        scratch_shapes=[pltpu.VMEM((8, LRU_WIDTH), F32)] * 4,
        compiler_params=_params("arbitrary"),
    )(dmix, w_out, attn, rec, xr, xr, yr, hr, hr, *small)


def _attn_bwd_tile(tp):
    return _wgrad_row_tile(tp)


def _attn_bwd(qkv, dattn, probs, sink_probs, token):
    tp = qkv.shape[0]
    tr = _attn_bwd_tile(tp)
    qb, nt = tr // BLOCK, tp // tr
    n_groups = N_KV

    def body(p_ref, ps_ref, q_ref, kp_ref, kc_ref, vp_ref, vc_ref, do_ref, _, dq_ref, dkv_ref, ex_ref, ds_ref, dsink):
        t = pl.program_id(0)

        @pl.when(t == 0)
        def _():
            dsink[...] = jnp.zeros_like(dsink)

        k_all = jnp.concatenate([kp_ref[...], kc_ref[...]], axis=0)
        v_all = jnp.concatenate([vp_ref[...], vc_ref[...]], axis=0)
        tail = None
        for i in range(qb):
            rows = slice(i * BLOCK, (i + 1) * BLOCK)
            qt = (q_ref[rows].astype(F32) * _QSCALE).T
            dot = do_ref[rows].astype(F32).T
            k2, v2 = k_all[i * BLOCK:(i + 2) * BLOCK], v_all[i * BLOCK:(i + 2) * BLOCK]
            dqs, dks, dvs = [], [], []
            for g in range(n_groups):
                cols = slice(g * HEAD_DIM, (g + 1) * HEAD_DIM)
                k_g, v_g = k2[:, cols], v2[:, cols]
                qgt, dogt = _heads_t(qt, g), _heads_t(dot, g)
                pb = p_ref[i, g]
                p = pb.astype(F32)
                dpt = _dot(v_g, dogt)
                delta = jnp.sum(p * dpt, axis=0, keepdims=True)
                dst = (p * (dpt - delta)).astype(BF16)
                dqs.append(_dot_tn(k_g, dst) * _QSCALE)
                dks.append(_dot_nt(qgt, dst))
                dvs.append(_dot_nt(dogt, pb))
                dsink[g:g + 1] -= ps_ref[i, g:g + 1] * delta
            dq_ref[rows] = _from_heads_t(dqs).astype(BF16)
            dkv = jnp.concatenate([jnp.concatenate(dks, axis=0).T, jnp.concatenate(dvs, axis=0).T], axis=1)
            if i == 0:
                ex_ref[0] = dkv[:BLOCK]
            else:
                dkv_ref[(i - 1) * BLOCK:i * BLOCK] = (tail + dkv[:BLOCK]).astype(BF16)
            tail = dkv[BLOCK:]
        dkv_ref[(qb - 1) * BLOCK:] = tail.astype(BF16)

        @pl.when(t == nt - 1)
        def _():
            lane = lax.broadcasted_iota(jnp.int32, (1, ATTN_HEADS), 1)
            acc = jnp.zeros((1, ATTN_HEADS), F32)
            for h in range(ATTN_HEADS):
                g, hh = divmod(h, GQA_GROUP)
                acc = acc + jnp.where(lane == h, jnp.sum(dsink[g:g + 1, hh * BLOCK:(hh + 1) * BLOCK]), 0.0)
            ds_ref[...] = acc

    cur = lambda w: pl.BlockSpec((tr, w), lambda t: (t, 0))
    return pl.pallas_call(
        body, name="attn_bwd", grid=(nt,),
        in_specs=_prob_specs(qb) + [cur(ATTN_WIDTH)] + _kv_specs(tr)
        + [cur(ATTN_WIDTH), pl.BlockSpec(token.shape, lambda t: (0, 0))],
        out_specs=[cur(ATTN_WIDTH), cur(2 * KV_WIDTH), pl.BlockSpec((1, BLOCK, 2 * KV_WIDTH), lambda t: (t, 0, 0)),
                   pl.BlockSpec((1, ATTN_HEADS), lambda t: (0, 0))],
        out_shape=[jax.ShapeDtypeStruct((tp, ATTN_WIDTH), BF16), jax.ShapeDtypeStruct((tp, 2 * KV_WIDTH), BF16),
                   jax.ShapeDtypeStruct((nt, BLOCK, 2 * KV_WIDTH), F32), jax.ShapeDtypeStruct((1, ATTN_HEADS), F32)],
        scratch_shapes=[pltpu.VMEM((n_groups, GROUP_ROWS), F32)],
        compiler_params=_params("arbitrary"),
    )(probs, sink_probs, qkv, qkv, qkv, qkv, qkv, dattn, token)


def _fix_dkv(dkv, dkv_extra):
    tp = dkv.shape[0]
    tr = _attn_bwd_tile(tp)
    nt, qb = tp // tr, tr // BLOCK
    if nt == 1:
        return dkv

    def body(d_ref, ex_ref, o_ref):
        o_ref[...] = (d_ref[...].astype(F32) + ex_ref[0]).astype(BF16)

    last = pl.BlockSpec((BLOCK, 2 * KV_WIDTH), lambda t: (t * qb + qb - 1, 0))
    return pl.pallas_call(
        body, name="fix_dkv", grid=(nt - 1,),
        in_specs=[last, pl.BlockSpec((1, BLOCK, 2 * KV_WIDTH), lambda t: (t + 1, 0, 0))],
        out_specs=last, out_shape=jax.ShapeDtypeStruct(dkv.shape, dkv.dtype),
        input_output_aliases={0: 0}, compiler_params=_params("parallel"),
    )(dkv, dkv_extra)


def _inproj_wgrad(dq, dkv, dxr, dyr, u0):
    tp = dq.shape[0]
    tr = _wgrad_row_tile(tp)

    def body(dq_ref, dkv_ref, dxr_ref, dyr_ref, u_ref, dw_ref):
        i = pl.program_id(0)

        def product():
            dz = jnp.concatenate([dq_ref[...], dkv_ref[...], dxr_ref[...], dyr_ref[...]], axis=1)
            return _dot_tn(dz, u_ref[...])

        @pl.when(i == 0)
        def _():
            dw_ref[...] = product()

        @pl.when(i > 0)
        def _():
            dw_ref[...] += product()

    row = lambda w: pl.BlockSpec((tr, w), lambda i: (i, 0))
    return pl.pallas_call(
        body, name="inproj_wgrad", grid=(tp // tr,),
        in_specs=[row(ATTN_WIDTH), row(2 * KV_WIDTH), row(LRU_WIDTH), row(LRU_WIDTH), row(D_MODEL)],
        out_specs=pl.BlockSpec((IN_WIDTH, D_MODEL), lambda i: (0, 0)),
        out_shape=jax.ShapeDtypeStruct((IN_WIDTH, D_MODEL), F32),
        compiler_params=_params("arbitrary"),
    )(dq, dkv, dxr, dyr, u0)


def _inproj_dgrad(dq, dkv, dxr, dyr, w_in, head, x, dh1, g, token):
    tp = dq.shape[0]
    tr = _row_tile(tp)
    nt, qb = tp // tr, tr // BLOCK

    def body(*refs):
        dq_ref, dkv_ref, dxr_ref, dyr_ref, w_ref, head_ref = refs[:6]
        pieces = refs[6:6 + qb]
        dh1_ref, g_ref, _, gx_ref, dhead_ref, dg_ref, buf, sems = refs[6 + qb:]
        i = pl.program_id(0)
        slot = i % 2

        def out_copy(step, at):
            return pltpu.make_async_copy(buf.at[at], gx_ref.at[pl.ds(step * tr - BLOCK, tr)], sems.at[at])

        dz = jnp.concatenate([dq_ref[...], dkv_ref[...], dxr_ref[...], dyr_ref[...]], axis=1)
        du = _dot(dz, w_ref[...])
        hhat, rs = _rms(_seq_tile(head_ref[...], pieces, i))
        dx, dg = _rms_bwd(hhat, rs, g_ref[...], du)
        dh0 = dh1_ref[...] + dx

        @pl.when(i >= 3)
        def _():
            out_copy(i - 2, slot).wait()

        buf[slot] = dh0

        @pl.when(i == 0)
        def _():
            dg_ref[...] = dg
            dhead_ref[...] = dh0[:BLOCK]
            if tr > BLOCK:
                first = pltpu.make_async_copy(buf.at[0, pl.ds(BLOCK, tr - BLOCK)], gx_ref.at[pl.ds(0, tr - BLOCK)],
                                              sems.at[0])
                first.start()
                first.wait()

        @pl.when(i >= 1)
        def _():
            dg_ref[...] += dg
            out_copy(i, slot).start()

        @pl.when(i == nt - 1)
        def _():
            if nt >= 3:
                out_copy(nt - 2, (nt - 2) % 2).wait()
            if nt >= 2:
                out_copy(nt - 1, (nt - 1) % 2).wait()

    row = lambda w: pl.BlockSpec((tr, w), lambda i: (i, 0))
    full = lambda shape: pl.BlockSpec(shape, lambda i: (0,) * len(shape))
    return pl.pallas_call(
        body, name="inproj_dgrad", grid=(tp // tr,),
        in_specs=[row(ATTN_WIDTH), row(2 * KV_WIDTH), row(LRU_WIDTH), row(LRU_WIDTH), full(w_in.shape),
                  full(head.shape)] + _seq_specs(tr) + [row(D_MODEL), full(g.shape), full(token.shape)],
        out_specs=[pl.BlockSpec(memory_space=pl.ANY), full((BLOCK, D_MODEL)), full((1, D_MODEL))],
        out_shape=[jax.ShapeDtypeStruct(x.shape, F32), jax.ShapeDtypeStruct((BLOCK, D_MODEL), F32),
                   jax.ShapeDtypeStruct((1, D_MODEL), F32)],
        scratch_shapes=[pltpu.VMEM((2, tr, D_MODEL), F32), pltpu.SemaphoreType.DMA((2,))],
        compiler_params=_params("arbitrary"),
    )(dq, dkv, dxr, dyr, w_in, head, *([x] * qb), dh1, g, token)


def _dense_block_diag(w):
    eye = jnp.eye(LRU_BLOCKS, dtype=w.dtype)
    return (w[:, :, None, :] * eye[:, None, :, None]).reshape(LRU_WIDTH, LRU_WIDTH)


def _diag_blocks(dense):
    d4 = dense.reshape(LRU_BLOCKS, LRU_BLOCK, LRU_BLOCKS, LRU_BLOCK)
    return jnp.stack([d4[n, :, n, :] for n in range(LRU_BLOCKS)])


def _local_step(head, x, tgt, g_pre_mix, w_in, conv_w, conv_b, w_a, b_a, w_x, b_x, lam, sinks, g_post_mix,
                g_pre_ffn, g_post_ffn, late_weights, on_ffn_grads, on_outproj_bwd, on_mixer_grads, token):
    wa = _dense_block_diag(w_a).astype(BF16)
    wx = _dense_block_diag(w_x).astype(BF16)

    u0, qkv, xr, yr, hr, rec = _inproj_lru_fwd(head, x, g_pre_mix, w_in, conv_w, conv_b, wa, b_a, wx, b_x, lam, token)
    attn, probs, sink_probs = _attn_fwd(qkv, sinks)
    w_out, w1, w2 = late_weights([attn, rec])
    mix, h1, u1 = _outproj_fwd(attn, rec, w_out, head, x, g_post_mix, g_pre_ffn)
    r1, dy, df2, loss, dg_post_ffn = _ffn_fwd(u1, w1, w2, h1, tgt, g_post_ffn)

    da1, dh1, dmix, dg_pre_ffn, dg_post_mix = _ffn_bwd_data(df2, r1, w1, w2, dy, h1, mix, g_pre_ffn, g_post_mix)
    dw1, dw2 = _ffn_bwd_weights(u1, da1, r1, df2)
    token2 = on_ffn_grads(dw1, dw2)
    dxr, dyr, dattn, dw_out, dwa, dwx, vec = _outproj_lru_bwd(dmix, w_out, attn, rec, xr, yr, hr, conv_w, conv_b,
                                                              wa, b_a, wx, b_x, lam, token2)
    token3 = on_outproj_bwd(dattn)
    dq, dkv, dkv_extra, dsinks = _attn_bwd(qkv, dattn, probs, sink_probs, token3)
    dkv = _fix_dkv(dkv, dkv_extra)
    dw_in = _inproj_wgrad(dq, dkv, dxr, dyr, u0)
    token4 = on_mixer_grads(dw_in, dw_out)
    dx, dhead, dg_pre_mix = _inproj_dgrad(dq, dkv, dxr, dyr, w_in, head, x, dh1, g_pre_mix, token4)

    grads = dict(
        g_pre_mix=dg_pre_mix, conv_w=vec[0:4], conv_b=vec[4:5], w_a=_diag_blocks(dwa), b_a=vec[5:6],
        w_x=_diag_blocks(dwx), b_x=vec[6:7], lru_lambda=vec[7:8], attn_sinks=dsinks,
        g_post_mix=dg_post_mix, g_pre_ffn=dg_pre_ffn, g_post_ffn=dg_post_ffn)
    return loss, dx, dhead, grads


HBM = pl.BlockSpec(memory_space=pltpu.HBM)


def _mesh_pos():
    return lax.axis_index("x"), lax.axis_index("y"), lax.axis_index("c")


def _other_chips(x, y):
    return [(1 - x, y), (x, 1 - y), (1 - x, 1 - y)]


def _remote(src, dst, send_sem, recv_sem, to):
    return pltpu.make_async_remote_copy(src_ref=src, dst_ref=dst, send_sem=send_sem, recv_sem=recv_sem,
                                        device_id=to, device_id_type=MESH)


def _gather_weights(shards, lands, tiny, tiny_land):
    nbig = len(shards)

    def body(*refs):
        srcs, tiny_src = refs[:nbig], refs[nbig]
        outs, tiny_out = refs[2 * nbig + 2:3 * nbig + 2], refs[3 * nbig + 2]
        ici_send, ici_recv, d2d_send, d2d_recv, tiny_send, tiny_recv = refs[3 * nbig + 3:]
        x, y, c = _mesh_pos()
        me = 2 * x + y
        chips = _other_chips(x, y)
        sibling = (x, y, 1 - c)
        sends = []
        for w, (src, out) in enumerate(zip(srcs, outs)):
            hr = src.shape[0] // 2
            for j, chip in enumerate(chips):
                k = 3 * w + j
                cp = _remote(src.at[pl.ds(c * hr, hr)], out.at[me, pl.ds(c * hr, hr)],
                             ici_send.at[k], ici_recv.at[k], (*chip, c))
                cp.start()
                sends.append(cp)
        for j, chip in enumerate(chips):
            cp = _remote(tiny_src, tiny_out.at[me], tiny_send.at[j], tiny_recv.at[j], (*chip, c))
            cp.start()
            sends.append(cp)
        for w, (src, out) in enumerate(zip(srcs, outs)):
            hr = src.shape[0] // 2
            for j, (px, py) in enumerate(chips):
                k = 3 * w + j
                landed = out.at[2 * px + py, pl.ds(c * hr, hr)]
                _remote(landed, landed, ici_send.at[k], ici_recv.at[k], sibling).wait_recv()
                cp = _remote(landed, landed, d2d_send.at[k], d2d_recv.at[k], sibling)
                cp.start()
                sends.append(cp)
        for w, (src, out) in enumerate(zip(srcs, outs)):
            hr = src.shape[0] // 2
            for j, (px, py) in enumerate(chips):
                k = 3 * w + j
                other = out.at[2 * px + py, pl.ds((1 - c) * hr, hr)]
                _remote(other, other, d2d_send.at[k], d2d_recv.at[k], sibling).wait_recv()
        for j, (px, py) in enumerate(chips):
            blk = tiny_out.at[2 * px + py]
            _remote(blk, blk, tiny_send.at[j], tiny_recv.at[j], sibling).wait_recv()
        for cp in sends:
            cp.wait_send()

    out_shape = [jax.ShapeDtypeStruct(l.shape, l.dtype) for l in list(lands) + [tiny_land]]
    n = 3 * nbig
    return pl.pallas_call(
        body, name="gather_weights", out_shape=out_shape,
        in_specs=[HBM] * (2 * nbig + 2), out_specs=[HBM] * (nbig + 1),
        input_output_aliases={nbig + 1 + i: i for i in range(nbig + 1)},
        scratch_shapes=[pltpu.SemaphoreType.DMA((n,)),
                        pltpu.SemaphoreType.DMA((n,)), pltpu.SemaphoreType.DMA((n,)), pltpu.SemaphoreType.DMA((n,)),
                        pltpu.SemaphoreType.DMA((3,)), pltpu.SemaphoreType.DMA((3,))],
    )(*shards, tiny, *lands, tiny_land)


def _prep_shard(w, me):
    rows, cols = w.shape
    tr = 256 if rows % 256 == 0 else rows

    def body(me_ref, w_ref, s_ref, l_ref):
        b = w_ref[...].astype(BF16)
        s_ref[...] = b
        l_ref[0] = b

    return pl.pallas_call(
        body, name="prep_shard",
        grid_spec=pltpu.PrefetchScalarGridSpec(
            num_scalar_prefetch=1, grid=(rows // tr,),
            in_specs=[pl.BlockSpec((tr, cols), lambda i, me_ref: (i, 0))],
            out_specs=[pl.BlockSpec((tr, cols), lambda i, me_ref: (i, 0)),
                       pl.BlockSpec((1, tr, cols), lambda i, me_ref: (me_ref[0], i, 0))]),
        out_shape=[jax.ShapeDtypeStruct((rows, cols), BF16), jax.ShapeDtypeStruct((N_CHIPS, rows, cols), BF16)],
        compiler_params=_params("parallel"),
    )(me, w)


def _prep_tiny(tiny, me, slots=N_CHIPS):
    def body(me_ref, t_ref, l_ref):
        l_ref[0] = t_ref[...]

    return pl.pallas_call(
        body, name="prep_tiny",
        grid_spec=pltpu.PrefetchScalarGridSpec(
            num_scalar_prefetch=1, grid=(1,),
            in_specs=[pl.BlockSpec(tiny.shape, lambda i, me_ref: (0, 0))],
            out_specs=pl.BlockSpec((1,) + tiny.shape, lambda i, me_ref: (me_ref[0], 0, 0))),
        out_shape=jax.ShapeDtypeStruct((slots,) + tiny.shape, tiny.dtype),
    )(me, tiny)


N_DEV = 8


def _sibling_exchange(parts, token):
    def body(*refs):
        n = len(parts)
        srcs, outs, send_sems, recv_sems = refs[:n], refs[n + 1:2 * n + 1], refs[2 * n + 1], refs[2 * n + 2]
        x, y, c = _mesh_pos()
        sibling = (x, y, 1 - c)
        cps = []
        for w, (src, out) in enumerate(zip(srcs, outs)):
            hr = src.shape[1] // 2
            cp = _remote(src.at[:, pl.ds((1 - c) * hr, hr)], out, send_sems.at[w], recv_sems.at[w], sibling)
            cp.start()
            cps.append(cp)
        for cp in cps:
            cp.wait()

    n = len(parts)
    return pl.pallas_call(
        body, name="sibling_exchange",
        out_shape=[jax.ShapeDtypeStruct((p.shape[0], p.shape[1] // 2, p.shape[2]), p.dtype) for p in parts],
        in_specs=[HBM] * n + [pl.BlockSpec(memory_space=pl.ANY)], out_specs=[HBM] * n,
        scratch_shapes=[pltpu.SemaphoreType.DMA((n,)), pltpu.SemaphoreType.DMA((n,))],
    )(*parts, token)


def _chip_presum(part, from_sibling, pos):
    _, hr, cols = from_sibling.shape
    tr = 256 if hr % 256 == 0 else hr
    steps = hr // tr

    def body(pos_ref, a_ref, b_ref, o_ref, land_ref):
        s = (a_ref[...] + b_ref[...]).astype(BF16)
        o_ref[...] = s

        @pl.when(pl.program_id(1) == pos_ref[1])
        def _():
            land_ref[...] = s

    return pl.pallas_call(
        body, name="chip_presum",
        grid_spec=pltpu.PrefetchScalarGridSpec(
            num_scalar_prefetch=1, grid=(steps, N_CHIPS),
            in_specs=[pl.BlockSpec((1, tr, cols), lambda i, j, p: (j, p[0] * steps + i, 0)),
                      pl.BlockSpec((1, tr, cols), lambda i, j, p: (j, i, 0))],
            out_specs=[pl.BlockSpec((1, tr, cols), lambda i, j, p: (j, i, 0)),
                       pl.BlockSpec((1, tr, cols), lambda i, j, p: (p[1], p[0] * steps + i, 0))]),
        out_shape=[jax.ShapeDtypeStruct(from_sibling.shape, BF16),
                   jax.ShapeDtypeStruct((N_CHIPS, 2 * hr, cols), BF16)],
        compiler_params=_params("arbitrary", "arbitrary"),
    )(pos, part, from_sibling)


def _scatter_partials(cparts, lands, done_cparts=(), done_lands=()):
    n_new = len(cparts)
    nw = n_new + len(done_cparts)

    def body(*refs):
        srcs = refs[:nw]
        outs = refs[2 * nw:3 * nw]
        own_send, own_recv, ici_send, ici_recv, d2d_send, d2d_recv = refs[3 * nw:]
        x, y, c = _mesh_pos()
        me = 2 * x + y
        chips = _other_chips(x, y)
        sibling = (x, y, 1 - c)
        sends = []
        for w in list(range(n_new, nw)) + list(range(n_new)):
            src, out = srcs[w], outs[w]
            hr = src.shape[1]
            mine = out.at[me, pl.ds(c * hr, hr)]
            cp = _remote(src.at[me], mine, own_send.at[w], own_recv.at[w], sibling)
            cp.start()
            sends.append(cp)
            for j, (px, py) in enumerate(chips):
                if w >= n_new:
                    break
                k = 3 * w + j
                cp = _remote(src.at[2 * px + py], mine, ici_send.at[k], ici_recv.at[k], (px, py, c))
                cp.start()
                sends.append(cp)
        for w in list(range(n_new, nw)) + list(range(n_new)):
            src, out = srcs[w], outs[w]
            hr = src.shape[1]
            for j, (px, py) in enumerate(chips):
                k = 3 * w + j
                landed = out.at[2 * px + py, pl.ds(c * hr, hr)]
                if w < n_new:
                    _remote(landed, landed, ici_send.at[k], ici_recv.at[k], sibling).wait_recv()
                cp = _remote(landed, landed, d2d_send.at[k], d2d_recv.at[k], sibling)
                cp.start()
                sends.append(cp)
        for w, (src, out) in enumerate(zip(srcs, outs)):
            hr = src.shape[1]
            other = out.at[me, pl.ds((1 - c) * hr, hr)]
            _remote(other, other, own_send.at[w], own_recv.at[w], sibling).wait_recv()
            for j, (px, py) in enumerate(chips):
                k = 3 * w + j
                other = out.at[2 * px + py, pl.ds((1 - c) * hr, hr)]
                _remote(other, other, d2d_send.at[k], d2d_recv.at[k], sibling).wait_recv()
        for cp in sends:
            cp.wait_send()

    n = 3 * nw
    dma = pltpu.SemaphoreType.DMA
    every = list(cparts) + list(done_cparts)
    every_lands = list(lands) + list(done_lands)
    return pl.pallas_call(
        body, name="scatter_partials",
        out_shape=[jax.ShapeDtypeStruct(l.shape, l.dtype) for l in every_lands],
        in_specs=[HBM] * (2 * nw), out_specs=[HBM] * nw,
        input_output_aliases={nw + i: i for i in range(nw)},
        scratch_shapes=[dma((nw,)), dma((nw,)), dma((n,)), dma((n,)), dma((n,)), dma((n,))],
    )(*every, *every_lands)


SEM = pl.BlockSpec(memory_space=pltpu.SEMAPHORE)
SPLIT_COPY = pltpu.CompilerParams(has_side_effects=pltpu.SideEffectType.DATAFLOW_SIDE_EFFECTING)


def _hbm(a):
    return pltpu.with_memory_space_constraint(a, pltpu.HBM)


def _gather_copies(srcs, lands, send_sems, recv_sems):
    x, y, c = _mesh_pos()
    me = 2 * x + y
    sends, recvs = [], []
    for w, (src, land) in enumerate(zip(srcs, lands)):
        hr = src.shape[0] // 2
        for j, (px, py) in enumerate(_other_chips(x, y)):
            k = 3 * w + j
            sends.append(_remote(src.at[pl.ds(c * hr, hr)], land.at[me, pl.ds(c * hr, hr)],
                                 send_sems.at[k], recv_sems.at[k], (px, py, c)))
            got = land.at[2 * px + py, pl.ds(c * hr, hr)]
            recvs.append(_remote(got, got, send_sems.at[k], recv_sems.at[k], (px, py, c)))
    return sends, recvs


def _scatter_copies(srcs, lands, send_sems, recv_sems):
    x, y, c = _mesh_pos()
    me = 2 * x + y
    sends, recvs = [], []
    for w, (src, land) in enumerate(zip(srcs, lands)):
        hr = src.shape[1]
        for j, (px, py) in enumerate(_other_chips(x, y)):
            k = 3 * w + j
            sends.append(_remote(src.at[2 * px + py], land.at[me, pl.ds(c * hr, hr)],
                                 send_sems.at[k], recv_sems.at[k], (px, py, c)))
            got = land.at[2 * px + py, pl.ds(c * hr, hr)]
            recvs.append(_remote(got, got, send_sems.at[k], recv_sems.at[k], (px, py, c)))
    return sends, recvs


def _sibling_copies(srcs, lands, send_sems, recv_sems):
    x, y, c = _mesh_pos()
    sibling = (x, y, 1 - c)
    sends, recvs = [], []
    for w, (src, land) in enumerate(zip(srcs, lands)):
        hr = src.shape[1] // 2
        sends.append(_remote(src.at[:, pl.ds((1 - c) * hr, hr)], land, send_sems.at[w], recv_sems.at[w], sibling))
        recvs.append(_remote(land, land, send_sems.at[w], recv_sems.at[w], sibling))
    return sends, recvs


def _all_peers_copies(srcs, lands, send_sems, recv_sems):
    x, y, c = _mesh_pos()
    (src,), (land,) = srcs, lands
    flip = lambda v, bit: 1 - v if bit else v
    sends, recvs = [], []
    for k in range(N_DEV - 1):
        px, py, pc = flip(x, (k + 1) & 4), flip(y, (k + 1) & 2), flip(c, (k + 1) & 1)
        sends.append(_remote(src, land.at[4 * x + 2 * y + c], send_sems.at[k], recv_sems.at[k], (px, py, pc)))
        got = land.at[4 * px + 2 * py + pc]
        recvs.append(_remote(got, got, send_sems.at[k], recv_sems.at[k], (px, py, pc)))
    return sends, recvs


def _split_start(name, copies_of, srcs, land_shapes, n_copies=None):
    n = len(srcs)
    k = 3 * n if n_copies is None else n_copies

    def body(*refs):
        src_refs, land_refs = refs[:n], refs[n:2 * n]
        send_sems, recv_sems = refs[2 * n], refs[2 * n + 1]
        token = refs[-1]
        sends, _ = copies_of(src_refs, land_refs, send_sems, recv_sems)
        for cp in sends:
            cp.start()
        token[...] = jnp.zeros_like(token)

    lands = [_hbm(s) for s in land_shapes]
    dma = pltpu.SemaphoreType.DMA
    res = pl.pallas_call(
        body, name=name,
        out_shape=(dma((k,)), dma((k,)), *[pltpu.HBM(s.shape, s.dtype) for s in srcs],
                   *[pltpu.HBM(s.shape, s.dtype) for s in land_shapes], jax.ShapeDtypeStruct((8, 128), F32)),
        in_specs=[HBM] * (2 * n),
        out_specs=(SEM, SEM, *([HBM] * (2 * n)), pl.BlockSpec(memory_space=pltpu.VMEM)),
        input_output_aliases={i: 2 + i for i in range(2 * n)},
        compiler_params=SPLIT_COPY,
    )(*[_hbm(s) for s in srcs], *lands)
    return res[0], res[1], list(res[2:2 + n]), list(res[2 + n:2 + 2 * n]), res[-1]


def _split_wait(name, copies_of, send_sems, recv_sems, srcs, lands, after):
    n = len(srcs)

    def body(*refs):
        src_refs, land_refs = refs[:n], refs[n:2 * n]
        sends, recvs = copies_of(src_refs, land_refs, refs[2 * n], refs[2 * n + 1])
        for cp in sends:
            cp.wait_send()
        for cp in recvs:
            cp.wait_recv()

    res = pl.pallas_call(
        body, name=name,
        out_shape=tuple(pltpu.HBM(s.shape, s.dtype) for s in list(srcs) + list(lands)),
        in_specs=[HBM] * (2 * n) + [SEM, SEM] + [pl.BlockSpec(memory_space=pl.ANY)] * len(after),
        out_specs=tuple([HBM] * (2 * n)),
        input_output_aliases={i: i for i in range(2 * n)},
        compiler_params=SPLIT_COPY,
    )(*srcs, *lands, send_sems, recv_sems, *after)
    return list(res[:n]), list(res[n:])


def _gather_finish(lands):
    n = len(lands)

    def body(*refs):
        outs = refs[n:2 * n]
        d2d_send, d2d_recv = refs[2 * n:]
        x, y, c = _mesh_pos()
        chips = _other_chips(x, y)
        sibling = (x, y, 1 - c)
        sends = []
        for w, out in enumerate(outs):
            hr = out.shape[1] // 2
            for j, (px, py) in enumerate(chips):
                landed = out.at[2 * px + py, pl.ds(c * hr, hr)]
                cp = _remote(landed, landed, d2d_send.at[3 * w + j], d2d_recv.at[3 * w + j], sibling)
                cp.start()
                sends.append(cp)
        for w, out in enumerate(outs):
            hr = out.shape[1] // 2
            for j, (px, py) in enumerate(chips):
                other = out.at[2 * px + py, pl.ds((1 - c) * hr, hr)]
                _remote(other, other, d2d_send.at[3 * w + j], d2d_recv.at[3 * w + j], sibling).wait_recv()
        for cp in sends:
            cp.wait_send()

    dma = pltpu.SemaphoreType.DMA
    return pl.pallas_call(
        body, name="gather_finish",
        out_shape=[jax.ShapeDtypeStruct(l.shape, l.dtype) for l in lands],
        in_specs=[HBM] * n, out_specs=[HBM] * n,
        input_output_aliases={i: i for i in range(n)},
        scratch_shapes=[dma((3 * n,)), dma((3 * n,))],
    )(*lands)


def _adamw(w, g, m, v):
    m = ADAM_B1 * m + (1.0 - ADAM_B1) * g
    v = ADAM_B2 * v + (1.0 - ADAM_B2) * (g * g)
    m_hat = m / (1.0 - ADAM_B1 ** ADAM_STEP)
    v_hat = v / (1.0 - ADAM_B2 ** ADAM_STEP)
    delta = -ADAM_LR * (m_hat / (jnp.sqrt(v_hat) + ADAM_EPS) + ADAM_WD * w)
    return delta, m, v


def _adamw_big(partials, w, m, v):
    rows, cols = w.shape
    tr = 256 if rows % 256 == 0 else rows

    def body(p_ref, w_ref, m_ref, v_ref, g_ref, d_ref, m2_ref, v2_ref):
        g = ((p_ref[0].astype(F32) + p_ref[1].astype(F32)) + p_ref[2].astype(F32)) + p_ref[3].astype(F32)
        g_ref[...] = g
        d_ref[...], m2_ref[...], v2_ref[...] = _adamw(w_ref[...], g, m_ref[...], v_ref[...])

    blk = pl.BlockSpec((tr, cols), lambda i: (i, 0))
    return pl.pallas_call(
        body, name="adamw_big", grid=(rows // tr,),
        in_specs=[pl.BlockSpec((N_CHIPS, tr, cols), lambda i: (0, i, 0)), blk, blk, blk],
        out_specs=[blk] * 4, out_shape=[jax.ShapeDtypeStruct((rows, cols), F32)] * 4,
        compiler_params=_params("parallel"),
    )(partials, w, m, v)


def _sum_devices(gathered, rows):
    cols = gathered.shape[1]

    def body(g_ref, o_ref):
        acc = g_ref[0:rows]
        for d in range(1, N_DEV):
            acc = acc + g_ref[d * rows:(d + 1) * rows]
        o_ref[...] = acc

    return pl.pallas_call(
        body, name="sum_devices", out_shape=jax.ShapeDtypeStruct((rows, cols), F32),
        in_specs=[pl.BlockSpec(memory_space=pltpu.VMEM)], out_specs=pl.BlockSpec(memory_space=pltpu.VMEM),
        compiler_params=pltpu.CompilerParams(vmem_limit_bytes=VMEM_LIMIT_V7X),
    )(gathered)


def _adamw_small(quads):
    n = len(quads)

    def body(*refs):
        ins, outs = refs[:4 * n], refs[4 * n:]
        for t in range(n):
            w, g, m, v = (r[...] for r in ins[4 * t:4 * t + 4])
            outs[3 * t][...], outs[3 * t + 1][...], outs[3 * t + 2][...] = _adamw(w, g, m, v)

    flat = [a for q in quads for a in q]
    vm = pl.BlockSpec(memory_space=pltpu.VMEM)
    res = pl.pallas_call(
        body, name="adamw_small",
        out_shape=[jax.ShapeDtypeStruct(q[0].shape, F32) for q in quads for _ in range(3)],
        in_specs=[vm] * (4 * n), out_specs=[vm] * (3 * n),
    )(*flat)
    return [tuple(res[3 * t:3 * t + 3]) for t in range(n)]


SMALL_PACK_ROWS = 96
META_COLS = D_MODEL // N_CHIPS
CONV_COLS = LRU_WIDTH // N_CHIPS
_WEIGHTS = ['meta_tokens', 'g_pre_mix', 'w_in', 'conv_w', 'conv_b', 'w_a', 'b_a', 'w_x', 'b_x', 'lru_lambda',
            'attn_sinks', 'w_out', 'g_post_mix', 'g_pre_ffn', 'w_ff1', 'w_ff2', 'g_post_ffn']
_BIG = ['w_in', 'w_out', 'w_ff1', 'w_ff2']


def _pack_small(dmeta, g, loss):
    z = lambda r, c: jnp.zeros((r, c), F32)
    rows = [
        dmeta,
        g['g_pre_mix'], g['g_post_mix'], g['g_pre_ffn'], g['g_post_ffn'],
        jnp.concatenate([g['conv_w'], z(4, 512)], axis=1),
        jnp.concatenate([g['conv_b'], g['b_a']], axis=1),
        jnp.concatenate([g['b_x'], g['lru_lambda']], axis=1),
        jnp.concatenate([g['attn_sinks'], z(1, D_MODEL - ATTN_HEADS)], axis=1),
        jnp.concatenate([loss, z(1, D_MODEL - 1)], axis=1),
        z(4, D_MODEL),
        g['w_a'].reshape(32, D_MODEL), g['w_x'].reshape(32, D_MODEL),
    ]
    return jnp.concatenate(rows, axis=0)


def _unpack_small(s, chip):
    return dict(
        meta_tokens=lax.dynamic_slice(s[0:N_META], (0, chip * META_COLS), (N_META, META_COLS)),
        g_pre_mix=s[16:17], g_post_mix=s[17:18], g_pre_ffn=s[18:19], g_post_ffn=s[19:20],
        conv_w=lax.dynamic_slice(s[20:24], (0, chip * CONV_COLS), (4, CONV_COLS)).reshape(1, 4, CONV_COLS),
        conv_b=s[24:25, :512], b_a=s[24:25, 512:], b_x=s[25:26, :512], lru_lambda=s[25:26, 512:],
        attn_sinks=s[26:27, :ATTN_HEADS], loss=s[27, 0],
        w_a=s[32:64].reshape(1, LRU_BLOCKS, LRU_BLOCK, LRU_BLOCK),
        w_x=s[64:96].reshape(1, LRU_BLOCKS, LRU_BLOCK, LRU_BLOCK))


def _as2d(a):
    if a.ndim == 2:
        return a
    return a.reshape(-1, a.shape[-1])


def kernel(x, meta_tokens, g_pre_mix, w_in, conv_w, conv_b, w_a, b_a, w_x, b_x, lru_lambda, attn_sinks, w_out, g_post_mix, g_pre_ffn, w_ff1, w_ff2, g_post_ffn, loss_target, m_meta_tokens, m_g_pre_mix, m_w_in, m_conv_w, m_conv_b, m_w_a, m_b_a, m_w_x, m_b_x, m_lru_lambda, m_attn_sinks, m_w_out, m_g_post_mix, m_g_pre_ffn, m_w_ff1, m_w_ff2, m_g_post_ffn, v_meta_tokens, v_g_pre_mix, v_w_in, v_conv_w, v_conv_b, v_w_a, v_b_a, v_w_x, v_b_x, v_lru_lambda, v_attn_sinks, v_w_out, v_g_post_mix, v_g_pre_ffn, v_w_ff1, v_w_ff2, v_g_post_ffn):
    weights = dict(meta_tokens=meta_tokens, g_pre_mix=g_pre_mix, w_in=w_in, conv_w=conv_w, conv_b=conv_b, w_a=w_a,
                   b_a=b_a, w_x=w_x, b_x=b_x, lru_lambda=lru_lambda, attn_sinks=attn_sinks, w_out=w_out,
                   g_post_mix=g_post_mix, g_pre_ffn=g_pre_ffn, w_ff1=w_ff1, w_ff2=w_ff2, g_post_ffn=g_post_ffn)
    mom1 = dict(zip(_WEIGHTS, [m_meta_tokens, m_g_pre_mix, m_w_in, m_conv_w, m_conv_b, m_w_a, m_b_a, m_w_x, m_b_x,
                               m_lru_lambda, m_attn_sinks, m_w_out, m_g_post_mix, m_g_pre_ffn, m_w_ff1, m_w_ff2,
                               m_g_post_ffn]))
    mom2 = dict(zip(_WEIGHTS, [v_meta_tokens, v_g_pre_mix, v_w_in, v_conv_w, v_conv_b, v_w_a, v_b_a, v_w_x, v_b_x,
                               v_lru_lambda, v_attn_sinks, v_w_out, v_g_post_mix, v_g_pre_ffn, v_w_ff1, v_w_ff2,
                               v_g_post_ffn]))
    xi, yi, ci = _mesh_pos()
    chip = 2 * xi + yi

    tiny = jnp.concatenate([meta_tokens, jnp.pad(conv_w[0], ((0, 4), (0, 128)))], axis=0)
    chip_arr = jnp.reshape(chip, (1,)).astype(jnp.int32)
    big2d = lambda a, name: a[0].T if name == 'w_in' else a[0]
    shards, lands = zip(*[_prep_shard(big2d(weights[n], n), chip_arr) for n in _BIG])
    g_in, g_tiny = _gather_weights(shards[:1], lands[:1], tiny, _prep_tiny(tiny, chip_arr))
    w_in_full = g_in.reshape(IN_WIDTH, D_MODEL)
    meta_full = jnp.concatenate([g_tiny[j, :N_META] for j in range(N_CHIPS)], axis=1)
    conv_w_full = jnp.concatenate([g_tiny[j, N_META:N_META + 4, :128] for j in range(N_CHIPS)], axis=1)
    g_send, g_recv, late_thru, late_lands, token = _split_start(
        "gather_late_start", _gather_copies, shards[1:], lands[1:])

    def late_weights(after):
        _, landed = _split_wait("gather_late_wait", _gather_copies, g_send, g_recv, late_thru, late_lands, after)
        g_out, g_f1, g_f2 = _gather_finish(landed)
        return g_out.reshape(D_MODEL, D_MODEL), g_f1, g_f2

    pos = jnp.stack([ci, chip]).astype(jnp.int32)
    ffn = {}


    def on_ffn_grads(dw1, dw2):
        parts = [dw1, dw2]
        lands = [lax.empty((p.shape[0], p.shape[1] // 2, p.shape[2]), p.dtype) for p in parts]
        ffn['sib'] = _split_start("sibling_ffn_start", _sibling_copies, parts, lands, len(parts))
        return ffn['sib'][4]

    def on_outproj_bwd(dattn):
        send, recv, thru, lands, _ = ffn['sib']
        parts, from_sibling = _split_wait("sibling_ffn_wait", _sibling_copies, send, recv, thru, lands, [dattn])
        cparts_ffn, lands_ffn = zip(*[_chip_presum(p, r, pos) for p, r in zip(parts, from_sibling)])
        ffn['send'], ffn['recv'], ffn['thru'], ffn['lands'], token3 = _split_start(
            "scatter_ffn_start", _scatter_copies, cparts_ffn, lands_ffn)
        return token3

    def on_mixer_grads(dw_in, dw_out):
        parts = [dw_in.reshape(N_CHIPS, IN_WIDTH // N_CHIPS, D_MODEL),
                 dw_out.reshape(N_CHIPS, D_MODEL // N_CHIPS, D_MODEL)]
        cparts, lands = zip(*[_chip_presum(p, r, pos) for p, r in zip(parts, _sibling_exchange(parts, pos))])
        ffn['mixer'] = _split_start("scatter_mixer_start", _scatter_copies, cparts, lands)
        return ffn['mixer'][4]

    head = jnp.concatenate([jnp.zeros((PAD_ROWS, D_MODEL), F32), meta_full], axis=0)
    loss, dx, dhead, grads = _local_step(head, x[0], loss_target[0], g_pre_mix, w_in_full, conv_w_full, conv_b, w_a[0],
                                         b_a, w_x[0], b_x, lru_lambda, attn_sinks, g_post_mix, g_pre_ffn, g_post_ffn,
                                         late_weights, on_ffn_grads, on_outproj_bwd, on_mixer_grads, token)
    grad_x = dx[None]

    pack = _pack_small(dhead[PAD_ROWS:], grads, loss)
    dev = jnp.reshape(4 * xi + 2 * yi + ci, (1,)).astype(jnp.int32)
    s_send, s_recv, s_thru, s_lands, token5 = _split_start(
        "gather_small_start", _all_peers_copies, [pack], [_prep_tiny(pack, dev, N_DEV)], N_DEV - 1)

    send, recv, thru, lands, _ = ffn['mixer']
    mixer_cparts, mixer_lands = _split_wait("scatter_mixer_wait", _scatter_copies, send, recv, thru, lands, [token5])
    ffn_cparts, ffn_lands = _split_wait("scatter_ffn_wait", _scatter_copies, ffn['send'], ffn['recv'], ffn['thru'],
                                        ffn['lands'], mixer_lands)
    chip_partials = _scatter_partials([], [], mixer_cparts + ffn_cparts, mixer_lands + ffn_lands)

    g_out_d, delta, new_m, new_v = {}, {}, {}, {}
    for name, part in zip(_BIG, chip_partials):
        shp = weights[name].shape
        res = _adamw_big(part, big2d(weights[name], name), big2d(mom1[name], name), big2d(mom2[name], name))
        g_out_d[name], delta[name], new_m[name], new_v[name] = (big2d(r[None], name).reshape(shp) for r in res)

    _, (gathered,) = _split_wait("gather_small_wait", _all_peers_copies, s_send, s_recv, s_thru, s_lands,
                                 [g_out_d[n] for n in _BIG])
    small = _unpack_small(_sum_devices(gathered.reshape(N_DEV * SMALL_PACK_ROWS, D_MODEL), SMALL_PACK_ROWS), chip)
    loss = small['loss']
    small_names = [n for n in _WEIGHTS if n not in _BIG]
    quads = [(_as2d(weights[n]), _as2d(small[n]), _as2d(mom1[n]), _as2d(mom2[n])) for n in small_names]
    for name, (d, m2, v2) in zip(small_names, _adamw_small(quads)):
        shp = weights[name].shape
        g_out_d[name] = small[name].reshape(shp)
        delta[name], new_m[name], new_v[name] = d.reshape(shp), m2.reshape(shp), v2.reshape(shp)

    return (loss, grad_x, *[g_out_d[n] for n in _WEIGHTS], *[delta[n] for n in _WEIGHTS],
            *[new_m[n] for n in _WEIGHTS], *[new_v[n] for n in _WEIGHTS])
```

```python
import numpy as np
import jax
import jax.numpy as jnp
from jax import lax
from jax.experimental import pallas as pl
from jax.experimental.pallas import tpu as pltpu

F32 = jnp.float32
BF16 = jnp.bfloat16

D_MODEL = 1024
N_META = 16
BLOCK = 128
PAD_ROWS = BLOCK - N_META
HEAD_DIM = 64
ATTN_HEADS = 8
GQA_GROUP = 4
ATTN_WIDTH = 512
KV_WIDTH = 128
QKV_WIDTH = ATTN_WIDTH + 2 * KV_WIDTH
LRU_WIDTH = 512
LRU_BLOCKS = 8
LRU_BLOCK = 64
LRU_C = 8.0
IN_WIDTH = 1792
D_FF = 4096
N_CHIPS = 4
FF_CHUNK = D_FF // N_CHIPS
EPS = 1e-6
NEG = -1e30

ADAM_LR = 0.001
ADAM_B1 = 0.9
ADAM_B2 = 0.999
ADAM_EPS = 1e-08
ADAM_WD = 0.01
ADAM_STEP = 10

VMEM_LIMIT_V7X = 62 * 1024 * 1024
MESH = pl.DeviceIdType.MESH

NT = (((1,), (1,)), ((), ()))
TN = (((0,), (0,)), ((), ()))


def _row_tile(tp):
    return 640 if tp % 640 == 0 else BLOCK


def _elementwise_tile(rows):
    return 512 if rows % 512 == 0 else rows


def _wgrad_row_tile(tp):
    return 1664 if tp % 1664 == 0 else _row_tile(tp)


def _params(*sem):
    return pltpu.CompilerParams(dimension_semantics=sem, vmem_limit_bytes=VMEM_LIMIT_V7X)


def _dot(a, b):
    return jnp.dot(a, b, preferred_element_type=F32)


def _dot_nt(a, b):
    return lax.dot_general(a, b, NT, preferred_element_type=F32)


def _dot_tn(a, b):
    return lax.dot_general(a, b, TN, preferred_element_type=F32)


def _rms(x):
    rs = lax.rsqrt(jnp.mean(x * x, axis=-1, keepdims=True) + EPS)
    return x * rs, rs


def _rms_bwd(xhat, rs, g, dy):
    dyg = dy * g
    dx = rs * (dyg - xhat * jnp.mean(dyg * xhat, axis=-1, keepdims=True))
    dg = jnp.sum(dy * xhat, axis=0, keepdims=True)
    return dx, dg


def _gelu(x):
    k = 0.7978845608028654
    t = jnp.tanh(x * (k + (k * 0.044715) * (x * x)))
    return (0.5 * x) * (1.0 + t), t


def _gelu_grad(x, t):
    k = 0.7978845608028654
    return 0.5 * (1.0 + t) + 0.5 * x * (1.0 - t * t) * k * (1.0 + 3 * 0.044715 * x * x)


def _sigmoid(x):
    return 0.5 * jnp.tanh(0.5 * x) + 0.5


def _one_minus_exp2(y):
    t = jnp.tanh(y)
    return (-2.0 * t) / (1.0 - t)


def _softplus(x):
    return jnp.maximum(x, 0.0) + jnp.log1p(jnp.exp(-jnp.abs(x)))


def _seq_specs(tr, delay=0):
    qb = tr // BLOCK
    tile = lambda i: jnp.maximum(i - delay, 0)
    return [pl.BlockSpec((BLOCK, D_MODEL), lambda i, *_, s=s: (jnp.maximum(tile(i) * qb + s - 1, 0), 0))
            for s in range(qb)]


def _seq_tile(head, pieces, i):
    first = jnp.where(i == 0, head, pieces[0][...])
    return jnp.concatenate([first] + [p[...] for p in pieces[1:]], axis=0)


GROUP_ROWS = GQA_GROUP * BLOCK


def _attn_bias():
    j = np.arange(2 * BLOCK)[:, None]
    i = np.arange(BLOCK)[None, :]
    band = (j - i >= 1) & (j - i <= BLOCK)
    out = []
    for n in range(3):
        ok = band & ((n - 1) * BLOCK + j >= PAD_ROWS) if n < 2 else band
        out.append(np.tile(np.where(ok, 0.0, NEG).astype(np.float32), (1, GQA_GROUP)))
    return jnp.asarray(np.stack(out))


def _heads_t(at, g):
    heads = range(GQA_GROUP * g, GQA_GROUP * (g + 1))
    return jnp.concatenate([at[h * HEAD_DIM:(h + 1) * HEAD_DIM] for h in heads], axis=1).astype(BF16)


def _from_heads_t(groups):
    pairs = []
    for p in groups:
        for h in range(0, GQA_GROUP, 2):
            two = jnp.concatenate([p[:, h * BLOCK:(h + 1) * BLOCK], p[:, (h + 1) * BLOCK:(h + 2) * BLOCK]], axis=0)
            pairs.append(two.T)
    return jnp.concatenate(pairs, axis=1)


def _stack_heads(a, g):
    heads = range(GQA_GROUP * g, GQA_GROUP * (g + 1))
    return jnp.concatenate([a[:, h * HEAD_DIM:(h + 1) * HEAD_DIM] for h in heads], axis=0)


def _unstack_heads(groups):
    return jnp.concatenate([p[h * BLOCK:(h + 1) * BLOCK] for p in groups for h in range(GQA_GROUP)], axis=1)


def _attn_probs_t(k_g, qg, bias, sink_row):
    st = _dot_nt(k_g, qg) + bias
    m = jnp.maximum(jnp.max(st, axis=0, keepdims=True), sink_row)
    p = jnp.exp(st - m)
    es = jnp.exp(sink_row - m)
    inv = 1.0 / (jnp.sum(p, axis=0, keepdims=True) + es)
    return p * inv, es * inv


def _attn_consts(sinks):
    return jnp.repeat(sinks.reshape(ATTN_HEADS), BLOCK).reshape(ATTN_HEADS // GQA_GROUP, GROUP_ROWS), _attn_bias()


_SINK_SPEC = pl.BlockSpec((ATTN_HEADS // GQA_GROUP, GROUP_ROWS), lambda n: (0, 0))
_BIAS_SPEC = pl.BlockSpec((3, 2 * BLOCK, GROUP_ROWS), lambda n: (0, 0, 0))
_QSCALE = HEAD_DIM ** -0.5


def _kv_specs(tr):
    qb = tr // BLOCK
    prev = lambda col: pl.BlockSpec((BLOCK, KV_WIDTH), lambda t: (jnp.maximum(t * qb - 1, 0), col))
    cur = lambda col: pl.BlockSpec((tr, KV_WIDTH), lambda t: (t, col))
    return [prev(4), cur(4), prev(5), cur(5)]


def _block_bias(b_ref, t, qb, i):
    return b_ref[2] if i >= 2 else b_ref[jnp.minimum(t * qb + i, 2)]


N_KV = ATTN_HEADS // GQA_GROUP


def _prob_specs(qb):
    return [pl.BlockSpec((qb, N_KV, 2 * BLOCK, GROUP_ROWS), lambda t: (t, 0, 0, 0)),
            pl.BlockSpec((qb, SUBLANES, GROUP_ROWS), lambda t: (t, 0, 0))]


def _attn_fwd(qkv, sinks):
    tp = qkv.shape[0]
    tr = _row_tile(tp)
    qb, nb = tr // BLOCK, tp // BLOCK
    sink_rows, bias = _attn_consts(sinks)

    def body(s_ref, b_ref, q_ref, kp_ref, kc_ref, vp_ref, vc_ref, o_ref, p_ref, ps_ref):
        t = pl.program_id(0)
        k_all = jnp.concatenate([kp_ref[...], kc_ref[...]], axis=0)
        v_all = jnp.concatenate([vp_ref[...], vc_ref[...]], axis=0)
        for i in range(qb):
            rows = slice(i * BLOCK, (i + 1) * BLOCK)
            q = q_ref[rows]
            k2, v2 = k_all[i * BLOCK:(i + 2) * BLOCK], v_all[i * BLOCK:(i + 2) * BLOCK]
            bias_n = _block_bias(b_ref, t, qb, i)
            outs, sink_probs = [], []
            for g in range(N_KV):
                cols = slice(g * HEAD_DIM, (g + 1) * HEAD_DIM)
                qg = _stack_heads(q, g) * jnp.asarray(_QSCALE, BF16)
                p, ps = _attn_probs_t(k2[:, cols], qg, bias_n, s_ref[g:g + 1])
                pb = p.astype(BF16)
                p_ref[i, g] = pb
                sink_probs.append(ps)
                outs.append(_dot_tn(pb, v2[:, cols]))
            o_ref[rows] = _unstack_heads(outs).astype(BF16)
            ps_ref[i] = jnp.concatenate(sink_probs + [jnp.zeros((SUBLANES - N_KV, GROUP_ROWS), F32)], axis=0)

    return pl.pallas_call(
        body, name="attn_fwd", grid=(tp // tr,),
        in_specs=[_SINK_SPEC, _BIAS_SPEC, pl.BlockSpec((tr, ATTN_WIDTH), lambda t: (t, 0))] + _kv_specs(tr),
        out_specs=[pl.BlockSpec((tr, ATTN_WIDTH), lambda t: (t, 0))] + _prob_specs(qb),
        out_shape=[jax.ShapeDtypeStruct((tp, ATTN_WIDTH), BF16),
                   jax.ShapeDtypeStruct((nb, N_KV, 2 * BLOCK, GROUP_ROWS), BF16),
                   jax.ShapeDtypeStruct((nb, SUBLANES, GROUP_ROWS), F32)],
        compiler_params=_params("parallel"),
    )(sink_rows, bias, qkv, qkv, qkv, qkv, qkv)


def _conv_taps(x, halo):
    ext = jnp.concatenate([halo, x], axis=0)
    return [ext[8:] if k == 3 else pltpu.roll(ext, 3 - k, 0)[8:] for k in range(4)]


def _lru_gates(xc, wa, ba, wx, bx, sp):
    xb = xc.astype(BF16)
    r = _sigmoid(_dot(xb, wa) + ba)
    ig = _sigmoid(_dot(xb, wx) + bx)
    log_a = (-LRU_C * sp) * r
    a = jnp.exp(log_a)
    mult = jnp.sqrt(_one_minus_exp2(log_a))
    return xb, r, ig, a, mult


SUBLANES = 8


def _scan_fwd(a, b, h_in):
    n, width = a.shape
    a, b = (v.reshape(n // SUBLANES, SUBLANES, width) for v in (a, b))
    in_group = lax.broadcasted_iota(jnp.int32, a.shape, 1)
    for d in (1, 2, 4):
        keep = in_group >= d
        b = jnp.where(keep, a * pltpu.roll(b, d, 1) + b, b)
        a = jnp.where(keep, a * pltpu.roll(a, d, 1), a)
    a, b = a.reshape(n, width), b.reshape(n, width)
    out, carry = [], h_in
    for g in range(0, n, SUBLANES):
        h = a[g:g + SUBLANES] * carry + b[g:g + SUBLANES]
        out.append(h)
        carry = h[SUBLANES - 1:]
    return jnp.concatenate(out, axis=0)


def _scan_rev(c, b, g_in):
    n, width = c.shape
    c, b = (v.reshape(n // SUBLANES, SUBLANES, width) for v in (c, b))
    in_group = lax.broadcasted_iota(jnp.int32, c.shape, 1)
    for d in (1, 2, 4):
        keep = in_group < SUBLANES - d
        b = jnp.where(keep, b + c * pltpu.roll(b, SUBLANES - d, 1), b)
        c = jnp.where(keep, c * pltpu.roll(c, SUBLANES - d, 1), c)
    c, b = c.reshape(n, width), b.reshape(n, width)
    out, carry = [], g_in
    for g in range(n - SUBLANES, -1, -SUBLANES):
        r = b[g:g + SUBLANES] + c[g:g + SUBLANES] * carry
        out.append(r)
        carry = r[:1]
    return jnp.concatenate(out[::-1], axis=0)


def _inproj_lru_fwd(head, x, g, w_in, conv_w, conv_b, wa, ba, wx, bx, lam, token):
    tp = BLOCK + x.shape[0]
    tr = _row_tile(tp)
    qb, nt = tr // BLOCK, tp // tr
    small = [conv_w, conv_b, wa, ba, wx, bx, lam]

    def body(*refs):
        head_ref, pieces = refs[0], refs[1:1 + qb]
        g_ref, w_ref, _, cw_ref, cb_ref, wa_ref, ba_ref, wx_ref, bx_ref, lam_ref = refs[1 + qb:11 + qb]
        u_ref, qkv_ref, xr_ref, yr_ref, hr_ref, rec_ref, zbuf, halo, hprev = refs[11 + qb:]
        i = pl.program_id(0)
        cur = i % 2

        @pl.when(i == 0)
        def _():
            halo[...] = jnp.zeros_like(halo)
            hprev[...] = jnp.zeros_like(hprev)
            zbuf[1] = jnp.zeros((tr, 2 * LRU_WIDTH), F32)

        def recurrent_branch(valid):
            cw, cb = cw_ref[...], cb_ref[...]
            wa_m, ba_v, wx_m, bx_v = wa_ref[...], ba_ref[...], wx_ref[...], bx_ref[...]
            sp = _softplus(-lam_ref[...])
            before, h_last = halo[...], hprev[0:1]
            for b in range(qb):
                rows = slice(b * BLOCK, (b + 1) * BLOCK)
                xy = zbuf[1 - cur, rows]
                xin = xy[:, :LRU_WIDTH]
                taps = _conv_taps(xin, before)
                before = xin[BLOCK - 8:]
                xc = cb + sum(cw[k:k + 1] * taps[k] for k in range(4))
                _, _, ig, a, mult = _lru_gates(xc, wa_m, ba_v, wx_m, bx_v, sp)
                u = mult * (ig * xc)
                if b == 0:
                    pos = (i - 1) * tr + lax.broadcasted_iota(jnp.int32, xc.shape, 0)
                    u = jnp.where(pos >= PAD_ROWS, u, 0.0)
                h = _scan_fwd(a, u, h_last)
                h_last = h[BLOCK - 1:]
                hr_ref[rows] = h
                gl, _ = _gelu(xy[:, LRU_WIDTH:])
                rec_ref[rows] = (gl * h).astype(BF16)
            halo[...] = jnp.where(valid, before, 0.0)
            hprev[0:1] = jnp.where(valid, h_last, 0.0)

        def projection():
            xhat, _ = _rms(_seq_tile(head_ref[...], pieces, i))
            u = (xhat * g_ref[...]).astype(BF16)
            u_ref[...] = u
            z = _dot_nt(u, w_ref[...])
            qkv_ref[...] = z[:, :QKV_WIDTH].astype(BF16)
            xr_ref[...] = z[:, QKV_WIDTH:QKV_WIDTH + LRU_WIDTH]
            yr_ref[...] = z[:, QKV_WIDTH + LRU_WIDTH:]
            zbuf[cur] = z[:, QKV_WIDTH:]

        @pl.when(i < nt)
        def _():
            recurrent_branch(i >= 1)
            projection()

        @pl.when(i == nt)
        def _():
            recurrent_branch(True)

    last = nt - 1
    this_row = lambda w: pl.BlockSpec((tr, w), lambda i: (jnp.minimum(i, last), 0))
    prev_row = lambda w: pl.BlockSpec((tr, w), lambda i: (jnp.maximum(i - 1, 0), 0))
    full = lambda a: pl.BlockSpec(a.shape, lambda i: (0,) * a.ndim)
    piece_specs = [pl.BlockSpec((BLOCK, D_MODEL), lambda i, s=s: (jnp.maximum(jnp.minimum(i, last) * qb + s - 1, 0), 0))
                   for s in range(qb)]
    return pl.pallas_call(
        body, name="inproj_lru_fwd", grid=(nt + 1,),
        in_specs=[full(head)] + piece_specs + [full(g), full(w_in), full(token)] + [full(a) for a in small],
        out_specs=[this_row(D_MODEL), this_row(QKV_WIDTH), this_row(LRU_WIDTH), this_row(LRU_WIDTH),
                   prev_row(LRU_WIDTH), prev_row(LRU_WIDTH)],
        out_shape=[jax.ShapeDtypeStruct((tp, D_MODEL), BF16), jax.ShapeDtypeStruct((tp, QKV_WIDTH), BF16),
                   jax.ShapeDtypeStruct((tp, LRU_WIDTH), F32), jax.ShapeDtypeStruct((tp, LRU_WIDTH), F32),
                   jax.ShapeDtypeStruct((tp, LRU_WIDTH), F32), jax.ShapeDtypeStruct((tp, LRU_WIDTH), BF16)],
        scratch_shapes=[pltpu.VMEM((2, tr, 2 * LRU_WIDTH), F32), pltpu.VMEM((8, LRU_WIDTH), F32),
                        pltpu.VMEM((8, LRU_WIDTH), F32)],
        compiler_params=_params("arbitrary"),
    )(head, *([x] * qb), g, w_in, token, *small)


def _outproj_fwd(attn, rec, w_out, head, x, g_post_mix, g_pre_ffn):
    tp = attn.shape[0]
    tr = _row_tile(tp)
    qb = tr // BLOCK

    def body(*refs):
        a_ref, r_ref, w_ref, head_ref = refs[:4]
        pieces = refs[4:4 + qb]
        gm_ref, gf_ref, mix_ref, h1_ref, u1_ref = refs[4 + qb:]
        mix = _dot(a_ref[...], w_ref[:ATTN_WIDTH]) + _dot(r_ref[...], w_ref[ATTN_WIDTH:])
        mix_ref[...] = mix
        mhat, _ = _rms(mix)
        h1 = _seq_tile(head_ref[...], pieces, pl.program_id(0)) + mhat * gm_ref[...]
        h1_ref[...] = h1
        hhat, _ = _rms(h1)
        u1_ref[...] = (hhat * gf_ref[...]).astype(BF16)

    row = lambda w: pl.BlockSpec((tr, w), lambda i: (i, 0))
    full = lambda a: pl.BlockSpec(a.shape, lambda i: (0,) * a.ndim)
    return pl.pallas_call(
        body, name="outproj_fwd", grid=(tp // tr,),
        in_specs=[row(ATTN_WIDTH), row(LRU_WIDTH), full(w_out), full(head)] + _seq_specs(tr)
        + [full(g_post_mix), full(g_pre_ffn)],
        out_specs=[row(D_MODEL), row(D_MODEL), row(D_MODEL)],
        out_shape=[jax.ShapeDtypeStruct((tp, D_MODEL), F32), jax.ShapeDtypeStruct((tp, D_MODEL), F32),
                   jax.ShapeDtypeStruct((tp, D_MODEL), BF16)],
        compiler_params=_params("parallel"),
    )(attn, rec, w_out, head, *([x] * qb), g_post_mix, g_pre_ffn)


FFN_STEPS = N_CHIPS


def _resident(a):
    return pl.BlockSpec(a.shape, lambda *_: (0,) * a.ndim, pipeline_mode=pl.Buffered(1))


def _ffn_fwd(u1, w1, w2, h1, tgt, g_post_ffn):
    tp = h1.shape[0]
    tr = _row_tile(tp)
    qb, nt = tr // BLOCK, tp // tr
    sr = tr // FFN_STEPS

    def body(*refs):
        u_ref, w1_ref, w2_ref, h1_ref = refs[:4]
        t_pieces = refs[4:4 + qb]
        g_ref, r1_ref, dy_ref, df2_ref, loss_ref, dg_ref, acc = refs[4 + qb:]
        i, c = pl.program_id(0), pl.program_id(1)
        cur = i % 2

        @pl.when((i == 0) & (c == 0))
        def _():
            loss_ref[...] = jnp.zeros_like(loss_ref)
            dg_ref[...] = jnp.zeros_like(dg_ref)
            acc[1] = jnp.zeros((tr, D_MODEL), F32)

        def matmuls():
            r = jnp.maximum(_dot(u_ref[...], w1_ref[c]), 0.0)
            r1_ref[...] = r.astype(BF16)
            return _dot((r * r).astype(BF16), w2_ref[c])

        def finish_previous_tile(k, valid):
            lo, hi = k * sr, (k + 1) * sr
            g = g_ref[...]
            fhat, rs = _rms(acc[1 - cur, lo:hi])
            h2 = h1_ref[...] + fhat * g
            rows = (i - 1) * tr + lo + lax.broadcasted_iota(jnp.int32, h2.shape, 0)
            tgt = jnp.concatenate([p[max(lo - s * BLOCK, 0):min(hi - s * BLOCK, BLOCK)] for s, p in enumerate(t_pieces)
                                   if lo < (s + 1) * BLOCK and hi > s * BLOCK], axis=0)
            err = jnp.where((rows >= BLOCK) & valid, h2 - tgt, 0.0)
            dy = err * (1.0 / D_MODEL)
            dy_ref[...] = dy
            loss_ref[...] += (0.5 / D_MODEL) * jnp.sum(err * err)
            df2, dg = _rms_bwd(fhat, rs, g, dy)
            df2_ref[...] = df2.astype(BF16)
            dg_ref[...] += dg

        for k in range(FFN_STEPS):
            @pl.when((c == k) & (i < nt))
            def _(k=k):
                finish_previous_tile(k, i >= 1)
                if k == 0:
                    acc[cur] = matmuls()
                else:
                    acc[cur] += matmuls()

            @pl.when((c == k) & (i == nt))
            def _(k=k):
                finish_previous_tile(k, True)

    last = nt - 1
    this_row = pl.BlockSpec((tr, D_MODEL), lambda i, c: (jnp.minimum(i, last), 0))
    prev_quarter = pl.BlockSpec((sr, D_MODEL), lambda i, c: (jnp.maximum(i - 1, 0) * FFN_STEPS + c, 0))
    prev_quarter_out = pl.BlockSpec(
        (sr, D_MODEL), lambda i, c: (jnp.where(i == 0, nt * FFN_STEPS, (i - 1) * FFN_STEPS + c), 0))
    full = lambda a: pl.BlockSpec(a.shape, lambda i, c: (0,) * a.ndim)
    return pl.pallas_call(
        body, name="ffn_fwd", grid=(nt + 1, FFN_STEPS),
        in_specs=[this_row, _resident(w1), _resident(w2), prev_quarter] + _seq_specs(tr, delay=1) + [full(g_post_ffn)],
        out_specs=[pl.BlockSpec((tr, FF_CHUNK), lambda i, c: (jnp.minimum(i, last), jnp.where(i < nt, c, FFN_STEPS - 1))),
                   prev_quarter_out, prev_quarter_out,
                   pl.BlockSpec((1, 1), lambda i, c: (0, 0)), pl.BlockSpec((1, D_MODEL), lambda i, c: (0, 0))],
        out_shape=[jax.ShapeDtypeStruct((tp, D_FF), BF16), jax.ShapeDtypeStruct((tp + sr, D_MODEL), F32),
                   jax.ShapeDtypeStruct((tp + sr, D_MODEL), BF16), jax.ShapeDtypeStruct((1, 1), F32),
                   jax.ShapeDtypeStruct((1, D_MODEL), F32)],
        scratch_shapes=[pltpu.VMEM((2, tr, D_MODEL), F32)],
        compiler_params=_params("arbitrary", "arbitrary"),
    )(u1, w1, w2, h1, *([tgt] * qb), g_post_ffn)


def _ffn_bwd_data(df2, r1, w1, w2, dy, h1, mix, g_pre_ffn, g_post_mix):
    tp = h1.shape[0]
    tr = _row_tile(tp)
    nt = tp // tr
    sr = tr // FFN_STEPS

    def body(df2_ref, r1_ref, w1_ref, w2_ref, dy_ref, h1_ref, mix_ref, gf_ref, gm_ref,
             da_ref, dh1_ref, dmix_ref, dgf_ref, dgm_ref, acc):
        i, c = pl.program_id(0), pl.program_id(1)
        cur = i % 2

        @pl.when((i == 0) & (c == 0))
        def _():
            dgf_ref[...] = jnp.zeros_like(dgf_ref)
            dgm_ref[...] = jnp.zeros_like(dgm_ref)
            acc[1] = jnp.zeros((tr, D_MODEL), F32)

        def matmuls():
            df = _dot_nt(df2_ref[...], w2_ref[c])
            da = (df * (2.0 * r1_ref[...].astype(F32))).astype(BF16)
            da_ref[...] = da
            return _dot_nt(da, w1_ref[c])

        def finish_previous_tile(k, valid):
            lo, hi = k * sr, (k + 1) * sr
            hhat, rs = _rms(h1_ref[...])
            dx, dgf = _rms_bwd(hhat, rs, gf_ref[...], acc[1 - cur, lo:hi])
            dh1 = dy_ref[...] + dx
            dh1_ref[...] = dh1
            mhat, rsm = _rms(mix_ref[...])
            dmix, dgm = _rms_bwd(mhat, rsm, gm_ref[...], dh1)
            dmix_ref[...] = dmix.astype(BF16)
            dgf_ref[...] += jnp.where(valid, dgf, 0.0)
            dgm_ref[...] += jnp.where(valid, dgm, 0.0)

        for k in range(FFN_STEPS):
            @pl.when((c == k) & (i < nt))
            def _(k=k):
                finish_previous_tile(k, i >= 1)
                if k == 0:
                    acc[cur] = matmuls()
                else:
                    acc[cur] += matmuls()

            @pl.when((c == k) & (i == nt))
            def _(k=k):
                finish_previous_tile(k, True)

    last = nt - 1
    this_row = pl.BlockSpec((tr, D_MODEL), lambda i, c: (jnp.minimum(i, last), 0))
    prev_quarter = pl.BlockSpec((sr, D_MODEL), lambda i, c: (jnp.maximum(i - 1, 0) * FFN_STEPS + c, 0))
    prev_quarter_out = pl.BlockSpec(
        (sr, D_MODEL), lambda i, c: (jnp.where(i == 0, nt * FFN_STEPS, (i - 1) * FFN_STEPS + c), 0))
    chunk = pl.BlockSpec((tr, FF_CHUNK), lambda i, c: (jnp.minimum(i, last), jnp.where(i < nt, c, FFN_STEPS - 1)))
    gain = pl.BlockSpec((1, D_MODEL), lambda i, c: (0, 0))
    return pl.pallas_call(
        body, name="ffn_bwd_data", grid=(nt + 1, FFN_STEPS),
        in_specs=[this_row, chunk, _resident(w1), _resident(w2), prev_quarter, prev_quarter, prev_quarter, gain, gain],
        out_specs=[chunk, prev_quarter_out, prev_quarter_out, gain, gain],
        out_shape=[jax.ShapeDtypeStruct((tp, D_FF), BF16), jax.ShapeDtypeStruct((tp + sr, D_MODEL), F32),
                   jax.ShapeDtypeStruct((tp + sr, D_MODEL), BF16), jax.ShapeDtypeStruct((1, D_MODEL), F32),
                   jax.ShapeDtypeStruct((1, D_MODEL), F32)],
        scratch_shapes=[pltpu.VMEM((2, tr, D_MODEL), F32)],
        compiler_params=_params("arbitrary", "arbitrary"),
    )(df2, r1, w1, w2, dy, h1, mix, g_pre_ffn, g_post_mix)


def _ffn_bwd_weights(u1, da1, r1, df2):
    tp = u1.shape[0]
    tr = _wgrad_row_tile(tp)

    def body(u_ref, da_ref, r1_ref, df2_ref, dw1_ref, dw2_ref):
        i = pl.program_id(1)

        def products():
            r = r1_ref[...].astype(F32)
            return _dot_tn(u_ref[...], da_ref[...]), _dot_tn((r * r).astype(BF16), df2_ref[...])

        @pl.when(i == 0)
        def _():
            dw1_ref[0], dw2_ref[0] = products()

        @pl.when(i > 0)
        def _():
            p1, p2 = products()
            dw1_ref[0] += p1
            dw2_ref[0] += p2

    row = pl.BlockSpec((tr, D_MODEL), lambda c, i: (i, 0))
    chunk = pl.BlockSpec((tr, FF_CHUNK), lambda c, i: (i, c))
    return pl.pallas_call(
        body, name="ffn_bwd_weights", grid=(N_CHIPS, tp // tr),
        in_specs=[row, chunk, chunk, row],
        out_specs=[pl.BlockSpec((1, D_MODEL, FF_CHUNK), lambda c, i: (c, 0, 0)),
                   pl.BlockSpec((1, FF_CHUNK, D_MODEL), lambda c, i: (c, 0, 0))],
        out_shape=[jax.ShapeDtypeStruct((N_CHIPS, D_MODEL, FF_CHUNK), F32),
                   jax.ShapeDtypeStruct((N_CHIPS, FF_CHUNK, D_MODEL), F32)],
        compiler_params=_params("parallel", "arbitrary"),
    )(u1, da1, r1, df2)


N_VEC_ROWS = 8


def _outproj_lru_bwd(dmix, w_out, attn, rec, xr, yr, hr, conv_w, conv_b, wa, ba, wx, bx, lam, token):
    tp = xr.shape[0]
    tr = _row_tile(tp)
    qb, nt = tr // BLOCK, tp // tr

    def body(dm_ref, w_ref, at_ref, rc_ref, xr_ref, xh_ref, yr_ref, hr_ref, hp_ref,
             cw_ref, cb_ref, wa_ref, ba_ref, wx_ref, bx_ref, lam_ref, _,
             dxr_ref, dyr_ref, dat_ref, dwo_ref, dwa_ref, dwx_ref, vec_ref, g_next, a_next, dxc_next, dsp):
        s = pl.program_id(0)
        t = nt - 1 - s

        @pl.when(s == 0)
        def _():
            g_next[...] = jnp.zeros_like(g_next)
            a_next[...] = jnp.zeros_like(a_next)
            dxc_next[...] = jnp.zeros_like(dxc_next)
            dsp[...] = jnp.zeros_like(dsp)
            dwo_ref[...] = jnp.zeros_like(dwo_ref)
            dwa_ref[...] = jnp.zeros_like(dwa_ref)
            dwx_ref[...] = jnp.zeros_like(dwx_ref)
            vec_ref[...] = jnp.zeros_like(vec_ref)

        dm = dm_ref[...]
        dcat = _dot_nt(dm, w_ref[...])
        dat_ref[...] = dcat[:, :ATTN_WIDTH].astype(BF16)
        drec_tile = dcat[:, ATTN_WIDTH:]
        dwo_ref[:ATTN_WIDTH] += _dot_tn(at_ref[...], dm)
        dwo_ref[ATTN_WIDTH:] += _dot_tn(rc_ref[...], dm)

        first_tile = t == 0
        cw, cb = cw_ref[...], cb_ref[...]
        lam_v = lam_ref[...]
        sp = _softplus(-lam_v)
        wa_m, ba_v, wx_m, bx_v = wa_ref[...], ba_ref[...], wx_ref[...], bx_ref[...]
        rows = lax.broadcasted_iota(jnp.int32, (BLOCK, LRU_WIDTH), 0)
        col = lambda v: jnp.sum(v, axis=0, keepdims=True)

        g_after, a_after, dxc_after = g_next[0:1], a_next[0:1], dxc_next[...]
        xbs, dgrs, dgis = [], [], []
        vec = [jnp.zeros((1, LRU_WIDTH), F32) for _ in range(N_VEC_ROWS)]
        for i in reversed(range(qb)):
            blk = slice(i * BLOCK, (i + 1) * BLOCK)
            if i == 0:
                x_before = jnp.where(first_tile, 0.0, xh_ref[...])
                h_before = jnp.where(first_tile, 0.0, hp_ref[7:8])
            else:
                x_before = xr_ref[i * BLOCK - 8:i * BLOCK]
                h_before = hr_ref[i * BLOCK - 1:i * BLOCK]
            taps = _conv_taps(xr_ref[blk], x_before)
            xc = cb + sum(cw[k:k + 1] * taps[k] for k in range(4))
            xb, r, ig, a, mult = _lru_gates(xc, wa_m, ba_v, wx_m, bx_v, sp)

            yr_v = yr_ref[blk]
            gl, th = _gelu(yr_v)
            h = hr_ref[blk]
            drec = drec_tile[blk]
            dyr_ref[blk] = (drec * h * _gelu_grad(yr_v, th)).astype(BF16)

            a_up = jnp.where(rows == BLOCK - 1, a_after, pltpu.roll(a, BLOCK - 1, 0))
            g = _scan_rev(a_up, drec * gl, g_after)
            g_after, a_after = g[0:1], a[0:1]

            h_prev = jnp.where(rows == 0, h_before, pltpu.roll(h, 1, 0))
            du, da = g, g * h_prev
            if i == 0:
                real = (t * tr + rows) >= PAD_ROWS
                du, da = jnp.where(real, du, 0.0), jnp.where(real, da, 0.0)
            dmult = du * (ig * xc)
            dig = du * (mult * xc)
            dxc = du * (mult * ig)
            dlog_a = da * a - dmult * (a * a / mult)
            if i == 0:
                dlog_a = jnp.where(real, dlog_a, 0.0)
            dgr = (dlog_a * (-LRU_C * sp)) * (r * (1.0 - r))
            dgi = dig * (ig * (1.0 - ig))
            dgr_b, dgi_b = dgr.astype(BF16), dgi.astype(BF16)
            dxc = dxc + _dot_nt(dgr_b, wa_m) + _dot_nt(dgi_b, wx_m)
            xbs.append(xb)
            dgrs.append(dgr_b)
            dgis.append(dgi_b)

            ext = jnp.concatenate([dxc, dxc_after], axis=0)
            up = [ext[:BLOCK] if j == 0 else pltpu.roll(ext, BLOCK + 8 - j, 0)[:BLOCK] for j in range(4)]
            dxr_ref[blk] = sum(cw[k:k + 1] * up[3 - k] for k in range(4)).astype(BF16)
            dxc_after = dxc[:8]

            for k in range(4):
                vec[k] = vec[k] + col(dxc * taps[k])
            vec[4] = vec[4] + col(dxc)
            vec[5] = vec[5] + col(dgr)
            vec[6] = vec[6] + col(dgi)
            vec[7] = vec[7] + col(dlog_a * (-LRU_C * r))

        g_next[0:1], a_next[0:1], dxc_next[...] = g_after, a_after, dxc_after
        xb_all = jnp.concatenate(xbs, axis=0)
        dwa_ref[...] += _dot_tn(xb_all, jnp.concatenate(dgrs, axis=0))
        dwx_ref[...] += _dot_tn(xb_all, jnp.concatenate(dgis, axis=0))
        for k in range(7):
            vec_ref[k:k + 1] += vec[k]
        dsp[0:1] += vec[7]

        @pl.when(s == nt - 1)
        def _():
            vec_ref[7:8] = dsp[0:1] * (-_sigmoid(-lam_v))

    blk_spec = pl.BlockSpec((tr, LRU_WIDTH), lambda s: (nt - 1 - s, 0))
    rows_before = pl.BlockSpec((8, LRU_WIDTH), lambda s: (jnp.maximum((nt - 1 - s) * (tr // 8) - 1, 0), 0))
    full = lambda a: pl.BlockSpec(a.shape, lambda s: (0,) * a.ndim)
    small = [conv_w, conv_b, wa, ba, wx, bx, lam, token]
    sq = pl.BlockSpec((LRU_WIDTH, LRU_WIDTH), lambda s: (0, 0))
    wide = pl.BlockSpec((tr, D_MODEL), lambda s: (nt - 1 - s, 0))
    whole = pl.BlockSpec((D_MODEL, D_MODEL), lambda s: (0, 0))
    return pl.pallas_call(
        body, name="outproj_lru_bwd", grid=(nt,),
        in_specs=[wide, whole, blk_spec, blk_spec, blk_spec, rows_before, blk_spec, blk_spec, rows_before]
        + [full(a) for a in small],
        out_specs=[blk_spec, blk_spec, blk_spec, whole, sq, sq, pl.BlockSpec((N_VEC_ROWS, LRU_WIDTH), lambda s: (0, 0))],
        out_shape=[jax.ShapeDtypeStruct((tp, LRU_WIDTH), BF16), jax.ShapeDtypeStruct((tp, LRU_WIDTH), BF16),
                   jax.ShapeDtypeStruct((tp, ATTN_WIDTH), BF16), jax.ShapeDtypeStruct((D_MODEL, D_MODEL), F32),
                   jax.ShapeDtypeStruct((LRU_WIDTH, LRU_WIDTH), F32), jax.ShapeDtypeStruct((LRU_WIDTH, LRU_WIDTH), F32),
                   jax.ShapeDtypeStruct((N_VEC_ROWS, LRU_WIDTH), F32)],
        scratch_shapes=[pltpu.VMEM((8, LRU_WIDTH), F32)] * 4,
        compiler_params=_params("arbitrary"),
    )(dmix, w_out, attn, rec, xr, xr, yr, hr, hr, *small)


def _attn_bwd_tile(tp):
    return _wgrad_row_tile(tp)


def _attn_bwd(qkv, dattn, probs, sink_probs, token):
    tp = qkv.shape[0]
    tr = _attn_bwd_tile(tp)
    qb, nt = tr // BLOCK, tp // tr
    n_groups = N_KV

    def body(p_ref, ps_ref, q_ref, kp_ref, kc_ref, vp_ref, vc_ref, do_ref, _, dq_ref, dkv_ref, ex_ref, ds_ref, dsink):
        t = pl.program_id(0)

        @pl.when(t == 0)
        def _():
            dsink[...] = jnp.zeros_like(dsink)

        k_all = jnp.concatenate([kp_ref[...], kc_ref[...]], axis=0)
        v_all = jnp.concatenate([vp_ref[...], vc_ref[...]], axis=0)
        tail = None
        for i in range(qb):
            rows = slice(i * BLOCK, (i + 1) * BLOCK)
            qt = (q_ref[rows].astype(F32) * _QSCALE).T
            dot = do_ref[rows].astype(F32).T
            k2, v2 = k_all[i * BLOCK:(i + 2) * BLOCK], v_all[i * BLOCK:(i + 2) * BLOCK]
            dqs, dks, dvs = [], [], []
            for g in range(n_groups):
                cols = slice(g * HEAD_DIM, (g + 1) * HEAD_DIM)
                k_g, v_g = k2[:, cols], v2[:, cols]
                qgt, dogt = _heads_t(qt, g), _heads_t(dot, g)
                pb = p_ref[i, g]
                p = pb.astype(F32)
                dpt = _dot(v_g, dogt)
                delta = jnp.sum(p * dpt, axis=0, keepdims=True)
                dst = (p * (dpt - delta)).astype(BF16)
                dqs.append(_dot_tn(k_g, dst) * _QSCALE)
                dks.append(_dot_nt(qgt, dst))
                dvs.append(_dot_nt(dogt, pb))
                dsink[g:g + 1] -= ps_ref[i, g:g + 1] * delta
            dq_ref[rows] = _from_heads_t(dqs).astype(BF16)
            dkv = jnp.concatenate([jnp.concatenate(dks, axis=0).T, jnp.concatenate(dvs, axis=0).T], axis=1)
            if i == 0:
                ex_ref[0] = dkv[:BLOCK]
            else:
                dkv_ref[(i - 1) * BLOCK:i * BLOCK] = (tail + dkv[:BLOCK]).astype(BF16)
            tail = dkv[BLOCK:]
        dkv_ref[(qb - 1) * BLOCK:] = tail.astype(BF16)

        @pl.when(t == nt - 1)
        def _():
            lane = lax.broadcasted_iota(jnp.int32, (1, ATTN_HEADS), 1)
            acc = jnp.zeros((1, ATTN_HEADS), F32)
            for h in range(ATTN_HEADS):
                g, hh = divmod(h, GQA_GROUP)
                acc = acc + jnp.where(lane == h, jnp.sum(dsink[g:g + 1, hh * BLOCK:(hh + 1) * BLOCK]), 0.0)
            ds_ref[...] = acc

    cur = lambda w: pl.BlockSpec((tr, w), lambda t: (t, 0))
    return pl.pallas_call(
        body, name="attn_bwd", grid=(nt,),
        in_specs=_prob_specs(qb) + [cur(ATTN_WIDTH)] + _kv_specs(tr)
        + [cur(ATTN_WIDTH), pl.BlockSpec(token.shape, lambda t: (0, 0))],
        out_specs=[cur(ATTN_WIDTH), cur(2 * KV_WIDTH), pl.BlockSpec((1, BLOCK, 2 * KV_WIDTH), lambda t: (t, 0, 0)),
                   pl.BlockSpec((1, ATTN_HEADS), lambda t: (0, 0))],
        out_shape=[jax.ShapeDtypeStruct((tp, ATTN_WIDTH), BF16), jax.ShapeDtypeStruct((tp, 2 * KV_WIDTH), BF16),
                   jax.ShapeDtypeStruct((nt, BLOCK, 2 * KV_WIDTH), F32), jax.ShapeDtypeStruct((1, ATTN_HEADS), F32)],
        scratch_shapes=[pltpu.VMEM((n_groups, GROUP_ROWS), F32)],
        compiler_params=_params("arbitrary"),
    )(probs, sink_probs, qkv, qkv, qkv, qkv, qkv, dattn, token)


def _fix_dkv(dkv, dkv_extra):
    tp = dkv.shape[0]
    tr = _attn_bwd_tile(tp)
    nt, qb = tp // tr, tr // BLOCK
    if nt == 1:
        return dkv

    def body(d_ref, ex_ref, o_ref):
        o_ref[...] = (d_ref[...].astype(F32) + ex_ref[0]).astype(BF16)

    last = pl.BlockSpec((BLOCK, 2 * KV_WIDTH), lambda t: (t * qb + qb - 1, 0))
    return pl.pallas_call(
        body, name="fix_dkv", grid=(nt - 1,),
        in_specs=[last, pl.BlockSpec((1, BLOCK, 2 * KV_WIDTH), lambda t: (t + 1, 0, 0))],
        out_specs=last, out_shape=jax.ShapeDtypeStruct(dkv.shape, dkv.dtype),
        input_output_aliases={0: 0}, compiler_params=_params("parallel"),
    )(dkv, dkv_extra)


def _inproj_wgrad(dq, dkv, dxr, dyr, u0):
    tp = dq.shape[0]
    tr = _wgrad_row_tile(tp)

    def body(dq_ref, dkv_ref, dxr_ref, dyr_ref, u_ref, dw_ref):
        i = pl.program_id(0)

        def product():
            dz = jnp.concatenate([dq_ref[...], dkv_ref[...], dxr_ref[...], dyr_ref[...]], axis=1)
            return _dot_tn(dz, u_ref[...])

        @pl.when(i == 0)
        def _():
            dw_ref[...] = product()

        @pl.when(i > 0)
        def _():
            dw_ref[...] += product()

    row = lambda w: pl.BlockSpec((tr, w), lambda i: (i, 0))
    return pl.pallas_call(
        body, name="inproj_wgrad", grid=(tp // tr,),
        in_specs=[row(ATTN_WIDTH), row(2 * KV_WIDTH), row(LRU_WIDTH), row(LRU_WIDTH), row(D_MODEL)],
        out_specs=pl.BlockSpec((IN_WIDTH, D_MODEL), lambda i: (0, 0)),
        out_shape=jax.ShapeDtypeStruct((IN_WIDTH, D_MODEL), F32),
        compiler_params=_params("arbitrary"),
    )(dq, dkv, dxr, dyr, u0)


def _inproj_dgrad(dq, dkv, dxr, dyr, w_in, head, x, dh1, g, token):
    tp = dq.shape[0]
    tr = _row_tile(tp)
    nt, qb = tp // tr, tr // BLOCK

    def body(*refs):
        dq_ref, dkv_ref, dxr_ref, dyr_ref, w_ref, head_ref = refs[:6]
        pieces = refs[6:6 + qb]
        dh1_ref, g_ref, _, gx_ref, dhead_ref, dg_ref, buf, sems = refs[6 + qb:]
        i = pl.program_id(0)
        slot = i % 2

        def out_copy(step, at):
            return pltpu.make_async_copy(buf.at[at], gx_ref.at[pl.ds(step * tr - BLOCK, tr)], sems.at[at])

        dz = jnp.concatenate([dq_ref[...], dkv_ref[...], dxr_ref[...], dyr_ref[...]], axis=1)
        du = _dot(dz, w_ref[...])
        hhat, rs = _rms(_seq_tile(head_ref[...], pieces, i))
        dx, dg = _rms_bwd(hhat, rs, g_ref[...], du)
        dh0 = dh1_ref[...] + dx

        @pl.when(i >= 3)
        def _():
            out_copy(i - 2, slot).wait()

        buf[slot] = dh0

        @pl.when(i == 0)
        def _():
            dg_ref[...] = dg
            dhead_ref[...] = dh0[:BLOCK]
            if tr > BLOCK:
                first = pltpu.make_async_copy(buf.at[0, pl.ds(BLOCK, tr - BLOCK)], gx_ref.at[pl.ds(0, tr - BLOCK)],
                                              sems.at[0])
                first.start()
                first.wait()

        @pl.when(i >= 1)
        def _():
            dg_ref[...] += dg
            out_copy(i, slot).start()

        @pl.when(i == nt - 1)
        def _():
            if nt >= 3:
                out_copy(nt - 2, (nt - 2) % 2).wait()
            if nt >= 2:
                out_copy(nt - 1, (nt - 1) % 2).wait()

    row = lambda w: pl.BlockSpec((tr, w), lambda i: (i, 0))
    full = lambda shape: pl.BlockSpec(shape, lambda i: (0,) * len(shape))
    return pl.pallas_call(
        body, name="inproj_dgrad", grid=(tp // tr,),
        in_specs=[row(ATTN_WIDTH), row(2 * KV_WIDTH), row(LRU_WIDTH), row(LRU_WIDTH), full(w_in.shape),
                  full(head.shape)] + _seq_specs(tr) + [row(D_MODEL), full(g.shape), full(token.shape)],
        out_specs=[pl.BlockSpec(memory_space=pl.ANY), full((BLOCK, D_MODEL)), full((1, D_MODEL))],
        out_shape=[jax.ShapeDtypeStruct(x.shape, F32), jax.ShapeDtypeStruct((BLOCK, D_MODEL), F32),
                   jax.ShapeDtypeStruct((1, D_MODEL), F32)],
        scratch_shapes=[pltpu.VMEM((2, tr, D_MODEL), F32), pltpu.SemaphoreType.DMA((2,))],
        compiler_params=_params("arbitrary"),
    )(dq, dkv, dxr, dyr, w_in, head, *([x] * qb), dh1, g, token)


def _dense_block_diag(w):
    eye = jnp.eye(LRU_BLOCKS, dtype=w.dtype)
    return (w[:, :, None, :] * eye[:, None, :, None]).reshape(LRU_WIDTH, LRU_WIDTH)


def _diag_blocks(dense):
    d4 = dense.reshape(LRU_BLOCKS, LRU_BLOCK, LRU_BLOCKS, LRU_BLOCK)
    return jnp.stack([d4[n, :, n, :] for n in range(LRU_BLOCKS)])


def _local_step(head, x, tgt, g_pre_mix, w_in, conv_w, conv_b, w_a, b_a, w_x, b_x, lam, sinks, g_post_mix,
                g_pre_ffn, g_post_ffn, late_weights, on_ffn_grads, on_outproj_bwd, on_mixer_grads, token):
    wa = _dense_block_diag(w_a).astype(BF16)
    wx = _dense_block_diag(w_x).astype(BF16)

    u0, qkv, xr, yr, hr, rec = _inproj_lru_fwd(head, x, g_pre_mix, w_in, conv_w, conv_b, wa, b_a, wx, b_x, lam, token)
    attn, probs, sink_probs = _attn_fwd(qkv, sinks)
    w_out, w1, w2 = late_weights([attn, rec])
    mix, h1, u1 = _outproj_fwd(attn, rec, w_out, head, x, g_post_mix, g_pre_ffn)
    r1, dy, df2, loss, dg_post_ffn = _ffn_fwd(u1, w1, w2, h1, tgt, g_post_ffn)

    da1, dh1, dmix, dg_pre_ffn, dg_post_mix = _ffn_bwd_data(df2, r1, w1, w2, dy, h1, mix, g_pre_ffn, g_post_mix)
    dw1, dw2 = _ffn_bwd_weights(u1, da1, r1, df2)
    token2 = on_ffn_grads(dw1, dw2)
    dxr, dyr, dattn, dw_out, dwa, dwx, vec = _outproj_lru_bwd(dmix, w_out, attn, rec, xr, yr, hr, conv_w, conv_b,
                                                              wa, b_a, wx, b_x, lam, token2)
    token3 = on_outproj_bwd(dattn)
    dq, dkv, dkv_extra, dsinks = _attn_bwd(qkv, dattn, probs, sink_probs, token3)
    dkv = _fix_dkv(dkv, dkv_extra)
    dw_in = _inproj_wgrad(dq, dkv, dxr, dyr, u0)
    token4 = on_mixer_grads(dw_in, dw_out)
    dx, dhead, dg_pre_mix = _inproj_dgrad(dq, dkv, dxr, dyr, w_in, head, x, dh1, g_pre_mix, token4)

    grads = dict(
        g_pre_mix=dg_pre_mix, conv_w=vec[0:4], conv_b=vec[4:5], w_a=_diag_blocks(dwa), b_a=vec[5:6],
        w_x=_diag_blocks(dwx), b_x=vec[6:7], lru_lambda=vec[7:8], attn_sinks=dsinks,
        g_post_mix=dg_post_mix, g_pre_ffn=dg_pre_ffn, g_post_ffn=dg_post_ffn)
    return loss, dx, dhead, grads


HBM = pl.BlockSpec(memory_space=pltpu.HBM)


def _mesh_pos():
    return lax.axis_index("x"), lax.axis_index("y"), lax.axis_index("c")


def _other_chips(x, y):
    return [(1 - x, y), (x, 1 - y), (1 - x, 1 - y)]


def _remote(src, dst, send_sem, recv_sem, to):
    return pltpu.make_async_remote_copy(src_ref=src, dst_ref=dst, send_sem=send_sem, recv_sem=recv_sem,
                                        device_id=to, device_id_type=MESH)


def _gather_weights(shards, lands, tiny, tiny_land):
    nbig = len(shards)

    def body(*refs):
        srcs, tiny_src = refs[:nbig], refs[nbig]
        outs, tiny_out = refs[2 * nbig + 2:3 * nbig + 2], refs[3 * nbig + 2]
        ici_send, ici_recv, d2d_send, d2d_recv, tiny_send, tiny_recv = refs[3 * nbig + 3:]
        x, y, c = _mesh_pos()
        me = 2 * x + y
        chips = _other_chips(x, y)
        sibling = (x, y, 1 - c)
        sends = []
        for w, (src, out) in enumerate(zip(srcs, outs)):
            hr = src.shape[0] // 2
            for j, chip in enumerate(chips):
                k = 3 * w + j
                cp = _remote(src.at[pl.ds(c * hr, hr)], out.at[me, pl.ds(c * hr, hr)],
                             ici_send.at[k], ici_recv.at[k], (*chip, c))
                cp.start()
                sends.append(cp)
        for j, chip in enumerate(chips):
            cp = _remote(tiny_src, tiny_out.at[me], tiny_send.at[j], tiny_recv.at[j], (*chip, c))
            cp.start()
            sends.append(cp)
        for w, (src, out) in enumerate(zip(srcs, outs)):
            hr = src.shape[0] // 2
            for j, (px, py) in enumerate(chips):
                k = 3 * w + j
                landed = out.at[2 * px + py, pl.ds(c * hr, hr)]
                _remote(landed, landed, ici_send.at[k], ici_recv.at[k], sibling).wait_recv()
                cp = _remote(landed, landed, d2d_send.at[k], d2d_recv.at[k], sibling)
                cp.start()
                sends.append(cp)
        for w, (src, out) in enumerate(zip(srcs, outs)):
            hr = src.shape[0] // 2
            for j, (px, py) in enumerate(chips):
                k = 3 * w + j
                other = out.at[2 * px + py, pl.ds((1 - c) * hr, hr)]
                _remote(other, other, d2d_send.at[k], d2d_recv.at[k], sibling).wait_recv()
        for j, (px, py) in enumerate(chips):
            blk = tiny_out.at[2 * px + py]
            _remote(blk, blk, tiny_send.at[j], tiny_recv.at[j], sibling).wait_recv()
        for cp in sends:
            cp.wait_send()

    out_shape = [jax.ShapeDtypeStruct(l.shape, l.dtype) for l in list(lands) + [tiny_land]]
    n = 3 * nbig
    return pl.pallas_call(
        body, name="gather_weights", out_shape=out_shape,
        in_specs=[HBM] * (2 * nbig + 2), out_specs=[HBM] * (nbig + 1),
        input_output_aliases={nbig + 1 + i: i for i in range(nbig + 1)},
        scratch_shapes=[pltpu.SemaphoreType.DMA((n,)),
                        pltpu.SemaphoreType.DMA((n,)), pltpu.SemaphoreType.DMA((n,)), pltpu.SemaphoreType.DMA((n,)),
                        pltpu.SemaphoreType.DMA((3,)), pltpu.SemaphoreType.DMA((3,))],
    )(*shards, tiny, *lands, tiny_land)


def _prep_shard(w, me):
    rows, cols = w.shape
    tr = _elementwise_tile(rows)

    def body(me_ref, w_ref, s_ref, l_ref):
        b = w_ref[...].astype(BF16)
        s_ref[...] = b
        l_ref[0] = b

    return pl.pallas_call(
        body, name="prep_shard",
        grid_spec=pltpu.PrefetchScalarGridSpec(
            num_scalar_prefetch=1, grid=(rows // tr,),
            in_specs=[pl.BlockSpec((tr, cols), lambda i, me_ref: (i, 0))],
            out_specs=[pl.BlockSpec((tr, cols), lambda i, me_ref: (i, 0)),
                       pl.BlockSpec((1, tr, cols), lambda i, me_ref: (me_ref[0], i, 0))]),
        out_shape=[jax.ShapeDtypeStruct((rows, cols), BF16), jax.ShapeDtypeStruct((N_CHIPS, rows, cols), BF16)],
        compiler_params=_params("parallel"),
    )(me, w)


def _prep_tiny(tiny, me, slots=N_CHIPS):
    def body(me_ref, t_ref, l_ref):
        l_ref[0] = t_ref[...]

    return pl.pallas_call(
        body, name="prep_tiny",
        grid_spec=pltpu.PrefetchScalarGridSpec(
            num_scalar_prefetch=1, grid=(1,),
            in_specs=[pl.BlockSpec(tiny.shape, lambda i, me_ref: (0, 0))],
            out_specs=pl.BlockSpec((1,) + tiny.shape, lambda i, me_ref: (me_ref[0], 0, 0))),
        out_shape=jax.ShapeDtypeStruct((slots,) + tiny.shape, tiny.dtype),
    )(me, tiny)


N_DEV = 8


def _sibling_exchange(parts, token):
    def body(*refs):
        n = len(parts)
        srcs, outs, send_sems, recv_sems = refs[:n], refs[n + 1:2 * n + 1], refs[2 * n + 1], refs[2 * n + 2]
        x, y, c = _mesh_pos()
        sibling = (x, y, 1 - c)
        cps = []
        for w, (src, out) in enumerate(zip(srcs, outs)):
            hr = src.shape[1] // 2
            cp = _remote(src.at[:, pl.ds((1 - c) * hr, hr)], out, send_sems.at[w], recv_sems.at[w], sibling)
            cp.start()
            cps.append(cp)
        for cp in cps:
            cp.wait()

    n = len(parts)
    return pl.pallas_call(
        body, name="sibling_exchange",
        out_shape=[jax.ShapeDtypeStruct((p.shape[0], p.shape[1] // 2, p.shape[2]), p.dtype) for p in parts],
        in_specs=[HBM] * n + [pl.BlockSpec(memory_space=pl.ANY)], out_specs=[HBM] * n,
        scratch_shapes=[pltpu.SemaphoreType.DMA((n,)), pltpu.SemaphoreType.DMA((n,))],
    )(*parts, token)


def _chip_presum(part, from_sibling, pos):
    _, hr, cols = from_sibling.shape
    tr = _elementwise_tile(hr)
    steps = hr // tr

    def body(pos_ref, a_ref, b_ref, o_ref, land_ref):
        s = (a_ref[...] + b_ref[...]).astype(BF16)
        o_ref[...] = s

        @pl.when(pl.program_id(1) == pos_ref[1])
        def _():
            land_ref[...] = s

    return pl.pallas_call(
        body, name="chip_presum",
        grid_spec=pltpu.PrefetchScalarGridSpec(
            num_scalar_prefetch=1, grid=(steps, N_CHIPS),
            in_specs=[pl.BlockSpec((1, tr, cols), lambda i, j, p: (j, p[0] * steps + i, 0)),
                      pl.BlockSpec((1, tr, cols), lambda i, j, p: (j, i, 0))],
            out_specs=[pl.BlockSpec((1, tr, cols), lambda i, j, p: (j, i, 0)),
                       pl.BlockSpec((1, tr, cols), lambda i, j, p: (p[1], p[0] * steps + i, 0))]),
        out_shape=[jax.ShapeDtypeStruct(from_sibling.shape, BF16),
                   jax.ShapeDtypeStruct((N_CHIPS, 2 * hr, cols), BF16)],
        compiler_params=_params("arbitrary", "arbitrary"),
    )(pos, part, from_sibling)


def _scatter_partials(cparts, lands, done_cparts=(), done_lands=()):
    n_new = len(cparts)
    nw = n_new + len(done_cparts)

    def body(*refs):
        srcs = refs[:nw]
        outs = refs[2 * nw:3 * nw]
        own_send, own_recv, ici_send, ici_recv, d2d_send, d2d_recv = refs[3 * nw:]
        x, y, c = _mesh_pos()
        me = 2 * x + y
        chips = _other_chips(x, y)
        sibling = (x, y, 1 - c)
        sends = []
        for w in list(range(n_new, nw)) + list(range(n_new)):
            src, out = srcs[w], outs[w]
            hr = src.shape[1]
            mine = out.at[me, pl.ds(c * hr, hr)]
            cp = _remote(src.at[me], mine, own_send.at[w], own_recv.at[w], sibling)
            cp.start()
            sends.append(cp)
            for j, (px, py) in enumerate(chips):
                if w >= n_new:
                    break
                k = 3 * w + j
                cp = _remote(src.at[2 * px + py], mine, ici_send.at[k], ici_recv.at[k], (px, py, c))
                cp.start()
                sends.append(cp)
        for w in list(range(n_new, nw)) + list(range(n_new)):
            src, out = srcs[w], outs[w]
            hr = src.shape[1]
            for j, (px, py) in enumerate(chips):
                k = 3 * w + j
                landed = out.at[2 * px + py, pl.ds(c * hr, hr)]
                if w < n_new:
                    _remote(landed, landed, ici_send.at[k], ici_recv.at[k], sibling).wait_recv()
                cp = _remote(landed, landed, d2d_send.at[k], d2d_recv.at[k], sibling)
                cp.start()
                sends.append(cp)
        for w, (src, out) in enumerate(zip(srcs, outs)):
            hr = src.shape[1]
            other = out.at[me, pl.ds((1 - c) * hr, hr)]
            _remote(other, other, own_send.at[w], own_recv.at[w], sibling).wait_recv()
            for j, (px, py) in enumerate(chips):
                k = 3 * w + j
                other = out.at[2 * px + py, pl.ds((1 - c) * hr, hr)]
                _remote(other, other, d2d_send.at[k], d2d_recv.at[k], sibling).wait_recv()
        for cp in sends:
            cp.wait_send()

    n = 3 * nw
    dma = pltpu.SemaphoreType.DMA
    every = list(cparts) + list(done_cparts)
    every_lands = list(lands) + list(done_lands)
    return pl.pallas_call(
        body, name="scatter_partials",
        out_shape=[jax.ShapeDtypeStruct(l.shape, l.dtype) for l in every_lands],
        in_specs=[HBM] * (2 * nw), out_specs=[HBM] * nw,
        input_output_aliases={nw + i: i for i in range(nw)},
        scratch_shapes=[dma((nw,)), dma((nw,)), dma((n,)), dma((n,)), dma((n,)), dma((n,))],
    )(*every, *every_lands)


SEM = pl.BlockSpec(memory_space=pltpu.SEMAPHORE)
SPLIT_COPY = pltpu.CompilerParams(has_side_effects=pltpu.SideEffectType.DATAFLOW_SIDE_EFFECTING)


def _hbm(a):
    return pltpu.with_memory_space_constraint(a, pltpu.HBM)


def _gather_copies(srcs, lands, send_sems, recv_sems):
    x, y, c = _mesh_pos()
    me = 2 * x + y
    sends, recvs = [], []
    for w, (src, land) in enumerate(zip(srcs, lands)):
        hr = src.shape[0] // 2
        for j, (px, py) in enumerate(_other_chips(x, y)):
            k = 3 * w + j
            sends.append(_remote(src.at[pl.ds(c * hr, hr)], land.at[me, pl.ds(c * hr, hr)],
                                 send_sems.at[k], recv_sems.at[k], (px, py, c)))
            got = land.at[2 * px + py, pl.ds(c * hr, hr)]
            recvs.append(_remote(got, got, send_sems.at[k], recv_sems.at[k], (px, py, c)))
    return sends, recvs


def _scatter_copies(srcs, lands, send_sems, recv_sems):
    x, y, c = _mesh_pos()
    me = 2 * x + y
    sends, recvs = [], []
    for w, (src, land) in enumerate(zip(srcs, lands)):
        hr = src.shape[1]
        for j, (px, py) in enumerate(_other_chips(x, y)):
            k = 3 * w + j
            sends.append(_remote(src.at[2 * px + py], land.at[me, pl.ds(c * hr, hr)],
                                 send_sems.at[k], recv_sems.at[k], (px, py, c)))
            got = land.at[2 * px + py, pl.ds(c * hr, hr)]
            recvs.append(_remote(got, got, send_sems.at[k], recv_sems.at[k], (px, py, c)))
    return sends, recvs


def _sibling_copies(srcs, lands, send_sems, recv_sems):
    x, y, c = _mesh_pos()
    sibling = (x, y, 1 - c)
    sends, recvs = [], []
    for w, (src, land) in enumerate(zip(srcs, lands)):
        hr = src.shape[1] // 2
        sends.append(_remote(src.at[:, pl.ds((1 - c) * hr, hr)], land, send_sems.at[w], recv_sems.at[w], sibling))
        recvs.append(_remote(land, land, send_sems.at[w], recv_sems.at[w], sibling))
    return sends, recvs


def _all_peers_copies(srcs, lands, send_sems, recv_sems):
    x, y, c = _mesh_pos()
    (src,), (land,) = srcs, lands
    flip = lambda v, bit: 1 - v if bit else v
    sends, recvs = [], []
    for k in range(N_DEV - 1):
        px, py, pc = flip(x, (k + 1) & 4), flip(y, (k + 1) & 2), flip(c, (k + 1) & 1)
        sends.append(_remote(src, land.at[4 * x + 2 * y + c], send_sems.at[k], recv_sems.at[k], (px, py, pc)))
        got = land.at[4 * px + 2 * py + pc]
        recvs.append(_remote(got, got, send_sems.at[k], recv_sems.at[k], (px, py, pc)))
    return sends, recvs


def _split_start(name, copies_of, srcs, land_shapes, n_copies=None):
    n = len(srcs)
    k = 3 * n if n_copies is None else n_copies

    def body(*refs):
        src_refs, land_refs = refs[:n], refs[n:2 * n]
        send_sems, recv_sems = refs[2 * n], refs[2 * n + 1]
        token = refs[-1]
        sends, _ = copies_of(src_refs, land_refs, send_sems, recv_sems)
        for cp in sends:
            cp.start()
        token[...] = jnp.zeros_like(token)

    lands = [_hbm(s) for s in land_shapes]
    dma = pltpu.SemaphoreType.DMA
    res = pl.pallas_call(
        body, name=name,
        out_shape=(dma((k,)), dma((k,)), *[pltpu.HBM(s.shape, s.dtype) for s in srcs],
                   *[pltpu.HBM(s.shape, s.dtype) for s in land_shapes], jax.ShapeDtypeStruct((8, 128), F32)),
        in_specs=[HBM] * (2 * n),
        out_specs=(SEM, SEM, *([HBM] * (2 * n)), pl.BlockSpec(memory_space=pltpu.VMEM)),
        input_output_aliases={i: 2 + i for i in range(2 * n)},
        compiler_params=SPLIT_COPY,
    )(*[_hbm(s) for s in srcs], *lands)
    return res[0], res[1], list(res[2:2 + n]), list(res[2 + n:2 + 2 * n]), res[-1]


def _split_wait(name, copies_of, send_sems, recv_sems, srcs, lands, after):
    n = len(srcs)

    def body(*refs):
        src_refs, land_refs = refs[:n], refs[n:2 * n]
        sends, recvs = copies_of(src_refs, land_refs, refs[2 * n], refs[2 * n + 1])
        for cp in sends:
            cp.wait_send()
        for cp in recvs:
            cp.wait_recv()

    res = pl.pallas_call(
        body, name=name,
        out_shape=tuple(pltpu.HBM(s.shape, s.dtype) for s in list(srcs) + list(lands)),
        in_specs=[HBM] * (2 * n) + [SEM, SEM] + [pl.BlockSpec(memory_space=pl.ANY)] * len(after),
        out_specs=tuple([HBM] * (2 * n)),
        input_output_aliases={i: i for i in range(2 * n)},
        compiler_params=SPLIT_COPY,
    )(*srcs, *lands, send_sems, recv_sems, *after)
    return list(res[:n]), list(res[n:])


def _gather_finish(lands):
    n = len(lands)

    def body(*refs):
        outs = refs[n:2 * n]
        d2d_send, d2d_recv = refs[2 * n:]
        x, y, c = _mesh_pos()
        chips = _other_chips(x, y)
        sibling = (x, y, 1 - c)
        sends = []
        for w, out in enumerate(outs):
            hr = out.shape[1] // 2
            for j, (px, py) in enumerate(chips):
                landed = out.at[2 * px + py, pl.ds(c * hr, hr)]
                cp = _remote(landed, landed, d2d_send.at[3 * w + j], d2d_recv.at[3 * w + j], sibling)
                cp.start()
                sends.append(cp)
        for w, out in enumerate(outs):
            hr = out.shape[1] // 2
            for j, (px, py) in enumerate(chips):
                other = out.at[2 * px + py, pl.ds((1 - c) * hr, hr)]
                _remote(other, other, d2d_send.at[3 * w + j], d2d_recv.at[3 * w + j], sibling).wait_recv()
        for cp in sends:
            cp.wait_send()

    dma = pltpu.SemaphoreType.DMA
    return pl.pallas_call(
        body, name="gather_finish",
        out_shape=[jax.ShapeDtypeStruct(l.shape, l.dtype) for l in lands],
        in_specs=[HBM] * n, out_specs=[HBM] * n,
        input_output_aliases={i: i for i in range(n)},
        scratch_shapes=[dma((3 * n,)), dma((3 * n,))],
    )(*lands)


def _adamw(w, g, m, v):
    m = ADAM_B1 * m + (1.0 - ADAM_B1) * g
    v = ADAM_B2 * v + (1.0 - ADAM_B2) * (g * g)
    m_hat = m / (1.0 - ADAM_B1 ** ADAM_STEP)
    v_hat = v / (1.0 - ADAM_B2 ** ADAM_STEP)
    delta = -ADAM_LR * (m_hat / (jnp.sqrt(v_hat) + ADAM_EPS) + ADAM_WD * w)
    return delta, m, v


def _adamw_big(partials, w, m, v):
    rows, cols = w.shape
    tr = _elementwise_tile(rows)

    def body(p_ref, w_ref, m_ref, v_ref, g_ref, d_ref, m2_ref, v2_ref):
        g = ((p_ref[0].astype(F32) + p_ref[1].astype(F32)) + p_ref[2].astype(F32)) + p_ref[3].astype(F32)
        g_ref[...] = g
        d_ref[...], m2_ref[...], v2_ref[...] = _adamw(w_ref[...], g, m_ref[...], v_ref[...])

    blk = pl.BlockSpec((tr, cols), lambda i: (i, 0))
    return pl.pallas_call(
        body, name="adamw_big", grid=(rows // tr,),
        in_specs=[pl.BlockSpec((N_CHIPS, tr, cols), lambda i: (0, i, 0)), blk, blk, blk],
        out_specs=[blk] * 4, out_shape=[jax.ShapeDtypeStruct((rows, cols), F32)] * 4,
        compiler_params=_params("parallel"),
    )(partials, w, m, v)


def _sum_devices(gathered, rows):
    cols = gathered.shape[1]

    def body(g_ref, o_ref):
        acc = g_ref[0:rows]
        for d in range(1, N_DEV):
            acc = acc + g_ref[d * rows:(d + 1) * rows]
        o_ref[...] = acc

    return pl.pallas_call(
        body, name="sum_devices", out_shape=jax.ShapeDtypeStruct((rows, cols), F32),
        in_specs=[pl.BlockSpec(memory_space=pltpu.VMEM)], out_specs=pl.BlockSpec(memory_space=pltpu.VMEM),
        compiler_params=pltpu.CompilerParams(vmem_limit_bytes=VMEM_LIMIT_V7X),
    )(gathered)


def _adamw_small(quads):
    n = len(quads)

    def body(*refs):
        ins, outs = refs[:4 * n], refs[4 * n:]
        for t in range(n):
            w, g, m, v = (r[...] for r in ins[4 * t:4 * t + 4])
            outs[3 * t][...], outs[3 * t + 1][...], outs[3 * t + 2][...] = _adamw(w, g, m, v)

    flat = [a for q in quads for a in q]
    vm = pl.BlockSpec(memory_space=pltpu.VMEM)
    res = pl.pallas_call(
        body, name="adamw_small",
        out_shape=[jax.ShapeDtypeStruct(q[0].shape, F32) for q in quads for _ in range(3)],
        in_specs=[vm] * (4 * n), out_specs=[vm] * (3 * n),
    )(*flat)
    return [tuple(res[3 * t:3 * t + 3]) for t in range(n)]


SMALL_PACK_ROWS = 96
META_COLS = D_MODEL // N_CHIPS
CONV_COLS = LRU_WIDTH // N_CHIPS
_WEIGHTS = ['meta_tokens', 'g_pre_mix', 'w_in', 'conv_w', 'conv_b', 'w_a', 'b_a', 'w_x', 'b_x', 'lru_lambda',
            'attn_sinks', 'w_out', 'g_post_mix', 'g_pre_ffn', 'w_ff1', 'w_ff2', 'g_post_ffn']
_BIG = ['w_in', 'w_out', 'w_ff1', 'w_ff2']


def _pack_small(dmeta, g, loss):
    z = lambda r, c: jnp.zeros((r, c), F32)
    rows = [
        dmeta,
        g['g_pre_mix'], g['g_post_mix'], g['g_pre_ffn'], g['g_post_ffn'],
        jnp.concatenate([g['conv_w'], z(4, 512)], axis=1),
        jnp.concatenate([g['conv_b'], g['b_a']], axis=1),
        jnp.concatenate([g['b_x'], g['lru_lambda']], axis=1),
        jnp.concatenate([g['attn_sinks'], z(1, D_MODEL - ATTN_HEADS)], axis=1),
        jnp.concatenate([loss, z(1, D_MODEL - 1)], axis=1),
        z(4, D_MODEL),
        g['w_a'].reshape(32, D_MODEL), g['w_x'].reshape(32, D_MODEL),
    ]
    return jnp.concatenate(rows, axis=0)


def _unpack_small(s, chip):
    return dict(
        meta_tokens=lax.dynamic_slice(s[0:N_META], (0, chip * META_COLS), (N_META, META_COLS)),
        g_pre_mix=s[16:17], g_post_mix=s[17:18], g_pre_ffn=s[18:19], g_post_ffn=s[19:20],
        conv_w=lax.dynamic_slice(s[20:24], (0, chip * CONV_COLS), (4, CONV_COLS)).reshape(1, 4, CONV_COLS),
        conv_b=s[24:25, :512], b_a=s[24:25, 512:], b_x=s[25:26, :512], lru_lambda=s[25:26, 512:],
        attn_sinks=s[26:27, :ATTN_HEADS], loss=s[27, 0],
        w_a=s[32:64].reshape(1, LRU_BLOCKS, LRU_BLOCK, LRU_BLOCK),
        w_x=s[64:96].reshape(1, LRU_BLOCKS, LRU_BLOCK, LRU_BLOCK))


def _as2d(a):
    if a.ndim == 2:
        return a
    return a.reshape(-1, a.shape[-1])


def kernel(x, meta_tokens, g_pre_mix, w_in, conv_w, conv_b, w_a, b_a, w_x, b_x, lru_lambda, attn_sinks, w_out, g_post_mix, g_pre_ffn, w_ff1, w_ff2, g_post_ffn, loss_target, m_meta_tokens, m_g_pre_mix, m_w_in, m_conv_w, m_conv_b, m_w_a, m_b_a, m_w_x, m_b_x, m_lru_lambda, m_attn_sinks, m_w_out, m_g_post_mix, m_g_pre_ffn, m_w_ff1, m_w_ff2, m_g_post_ffn, v_meta_tokens, v_g_pre_mix, v_w_in, v_conv_w, v_conv_b, v_w_a, v_b_a, v_w_x, v_b_x, v_lru_lambda, v_attn_sinks, v_w_out, v_g_post_mix, v_g_pre_ffn, v_w_ff1, v_w_ff2, v_g_post_ffn):
    weights = dict(meta_tokens=meta_tokens, g_pre_mix=g_pre_mix, w_in=w_in, conv_w=conv_w, conv_b=conv_b, w_a=w_a,
                   b_a=b_a, w_x=w_x, b_x=b_x, lru_lambda=lru_lambda, attn_sinks=attn_sinks, w_out=w_out,
                   g_post_mix=g_post_mix, g_pre_ffn=g_pre_ffn, w_ff1=w_ff1, w_ff2=w_ff2, g_post_ffn=g_post_ffn)
    mom1 = dict(zip(_WEIGHTS, [m_meta_tokens, m_g_pre_mix, m_w_in, m_conv_w, m_conv_b, m_w_a, m_b_a, m_w_x, m_b_x,
                               m_lru_lambda, m_attn_sinks, m_w_out, m_g_post_mix, m_g_pre_ffn, m_w_ff1, m_w_ff2,
                               m_g_post_ffn]))
    mom2 = dict(zip(_WEIGHTS, [v_meta_tokens, v_g_pre_mix, v_w_in, v_conv_w, v_conv_b, v_w_a, v_b_a, v_w_x, v_b_x,
                               v_lru_lambda, v_attn_sinks, v_w_out, v_g_post_mix, v_g_pre_ffn, v_w_ff1, v_w_ff2,
                               v_g_post_ffn]))
    xi, yi, ci = _mesh_pos()
    chip = 2 * xi + yi

    tiny = jnp.concatenate([meta_tokens, jnp.pad(conv_w[0], ((0, 4), (0, 128)))], axis=0)
    chip_arr = jnp.reshape(chip, (1,)).astype(jnp.int32)
    big2d = lambda a, name: a[0].T if name == 'w_in' else a[0]
    shards, lands = zip(*[_prep_shard(big2d(weights[n], n), chip_arr) for n in _BIG])
    g_in, g_tiny = _gather_weights(shards[:1], lands[:1], tiny, _prep_tiny(tiny, chip_arr))
    w_in_full = g_in.reshape(IN_WIDTH, D_MODEL)
    meta_full = jnp.concatenate([g_tiny[j, :N_META] for j in range(N_CHIPS)], axis=1)
    conv_w_full = jnp.concatenate([g_tiny[j, N_META:N_META + 4, :128] for j in range(N_CHIPS)], axis=1)
    g_send, g_recv, late_thru, late_lands, token = _split_start(
        "gather_late_start", _gather_copies, shards[1:], lands[1:])

    def late_weights(after):
        _, landed = _split_wait("gather_late_wait", _gather_copies, g_send, g_recv, late_thru, late_lands, after)
        g_out, g_f1, g_f2 = _gather_finish(landed)
        return g_out.reshape(D_MODEL, D_MODEL), g_f1, g_f2

    pos = jnp.stack([ci, chip]).astype(jnp.int32)
    ffn = {}


    def on_ffn_grads(dw1, dw2):
        parts = [dw1, dw2]
        lands = [lax.empty((p.shape[0], p.shape[1] // 2, p.shape[2]), p.dtype) for p in parts]
        ffn['sib'] = _split_start("sibling_ffn_start", _sibling_copies, parts, lands, len(parts))
        return ffn['sib'][4]

    def on_outproj_bwd(dattn):
        send, recv, thru, lands, _ = ffn['sib']
        parts, from_sibling = _split_wait("sibling_ffn_wait", _sibling_copies, send, recv, thru, lands, [dattn])
        cparts_ffn, lands_ffn = zip(*[_chip_presum(p, r, pos) for p, r in zip(parts, from_sibling)])
        ffn['send'], ffn['recv'], ffn['thru'], ffn['lands'], token3 = _split_start(
            "scatter_ffn_start", _scatter_copies, cparts_ffn, lands_ffn)
        return token3

    def on_mixer_grads(dw_in, dw_out):
        parts = [dw_in.reshape(N_CHIPS, IN_WIDTH // N_CHIPS, D_MODEL),
                 dw_out.reshape(N_CHIPS, D_MODEL // N_CHIPS, D_MODEL)]
        cparts, lands = zip(*[_chip_presum(p, r, pos) for p, r in zip(parts, _sibling_exchange(parts, pos))])
        ffn['mixer'] = _split_start("scatter_mixer_start", _scatter_copies, cparts, lands)
        return ffn['mixer'][4]

    head = jnp.concatenate([jnp.zeros((PAD_ROWS, D_MODEL), F32), meta_full], axis=0)
    loss, dx, dhead, grads = _local_step(head, x[0], loss_target[0], g_pre_mix, w_in_full, conv_w_full, conv_b, w_a[0],
                                         b_a, w_x[0], b_x, lru_lambda, attn_sinks, g_post_mix, g_pre_ffn, g_post_ffn,
                                         late_weights, on_ffn_grads, on_outproj_bwd, on_mixer_grads, token)
    grad_x = dx[None]

    pack = _pack_small(dhead[PAD_ROWS:], grads, loss)
    dev = jnp.reshape(4 * xi + 2 * yi + ci, (1,)).astype(jnp.int32)
    s_send, s_recv, s_thru, s_lands, token5 = _split_start(
        "gather_small_start", _all_peers_copies, [pack], [_prep_tiny(pack, dev, N_DEV)], N_DEV - 1)

    send, recv, thru, lands, _ = ffn['mixer']
    mixer_cparts, mixer_lands = _split_wait("scatter_mixer_wait", _scatter_copies, send, recv, thru, lands, [token5])
    ffn_cparts, ffn_lands = _split_wait("scatter_ffn_wait", _scatter_copies, ffn['send'], ffn['recv'], ffn['thru'],
                                        ffn['lands'], mixer_lands)
    chip_partials = _scatter_partials([], [], mixer_cparts + ffn_cparts, mixer_lands + ffn_lands)

    g_out_d, delta, new_m, new_v = {}, {}, {}, {}
    for name, part in zip(_BIG, chip_partials):
        shp = weights[name].shape
        res = _adamw_big(part, big2d(weights[name], name), big2d(mom1[name], name), big2d(mom2[name], name))
        g_out_d[name], delta[name], new_m[name], new_v[name] = (big2d(r[None], name).reshape(shp) for r in res)

    _, (gathered,) = _split_wait("gather_small_wait", _all_peers_copies, s_send, s_recv, s_thru, s_lands,
                                 [g_out_d[n] for n in _BIG])
    small = _unpack_small(_sum_devices(gathered.reshape(N_DEV * SMALL_PACK_ROWS, D_MODEL), SMALL_PACK_ROWS), chip)
    loss = small['loss']
    small_names = [n for n in _WEIGHTS if n not in _BIG]
    quads = [(_as2d(weights[n]), _as2d(small[n]), _as2d(mom1[n]), _as2d(mom2[n])) for n in small_names]
    for name, (d, m2, v2) in zip(small_names, _adamw_small(quads)):
        shp = weights[name].shape
        g_out_d[name] = small[name].reshape(shp)
        delta[name], new_m[name], new_v[name] = d.reshape(shp), m2.reshape(shp), v2.reshape(shp)

    return (loss, grad_x, *[g_out_d[n] for n in _WEIGHTS], *[delta[n] for n in _WEIGHTS],
            *[new_m[n] for n in _WEIGHTS], *[new_v[n] for n in _WEIGHTS])
```

```python
import numpy as np
import jax
import jax.numpy as jnp
from jax import lax
from jax.experimental import pallas as pl
from jax.experimental.pallas import tpu as pltpu

F32 = jnp.float32
BF16 = jnp.bfloat16

D_MODEL = 1024
N_META = 16
BLOCK = 128
PAD_ROWS = BLOCK - N_META
HEAD_DIM = 64
ATTN_HEADS = 8
GQA_GROUP = 4
ATTN_WIDTH = 512
KV_WIDTH = 128
QKV_WIDTH = ATTN_WIDTH + 2 * KV_WIDTH
LRU_WIDTH = 512
LRU_BLOCKS = 8
LRU_BLOCK = 64
LRU_C = 8.0
IN_WIDTH = 1792
D_FF = 4096
N_CHIPS = 4
FF_CHUNK = D_FF // N_CHIPS
EPS = 1e-6
NEG = -1e30

ADAM_LR = 0.001
ADAM_B1 = 0.9
ADAM_B2 = 0.999
ADAM_EPS = 1e-08
ADAM_WD = 0.01
ADAM_STEP = 10

VMEM_LIMIT_V7X = 62 * 1024 * 1024
MESH = pl.DeviceIdType.MESH

NT = (((1,), (1,)), ((), ()))
TN = (((0,), (0,)), ((), ()))


def _row_tile(tp):
    return 640 if tp % 640 == 0 else BLOCK


def _elementwise_tile(rows):
    return 512 if rows % 512 == 0 else rows


def _wgrad_row_tile(tp):
    return 1664 if tp % 1664 == 0 else _row_tile(tp)


def _params(*sem):
    return pltpu.CompilerParams(dimension_semantics=sem, vmem_limit_bytes=VMEM_LIMIT_V7X)


def _dot(a, b):
    return jnp.dot(a, b, preferred_element_type=F32)


def _dot_nt(a, b):
    return lax.dot_general(a, b, NT, preferred_element_type=F32)


def _dot_tn(a, b):
    return lax.dot_general(a, b, TN, preferred_element_type=F32)


def _rms(x):
    rs = lax.rsqrt(jnp.mean(x * x, axis=-1, keepdims=True) + EPS)
    return x * rs, rs


def _rms_bwd(xhat, rs, g, dy):
    dyg = dy * g
    dx = rs * (dyg - xhat * jnp.mean(dyg * xhat, axis=-1, keepdims=True))
    dg = jnp.sum(dy * xhat, axis=0, keepdims=True)
    return dx, dg


def _gelu(x):
    k = 0.7978845608028654
    t = jnp.tanh(x * (k + (k * 0.044715) * (x * x)))
    return (0.5 * x) * (1.0 + t), t


def _gelu_grad(x, t):
    k = 0.7978845608028654
    return 0.5 * (1.0 + t) + 0.5 * x * (1.0 - t * t) * k * (1.0 + 3 * 0.044715 * x * x)


def _sigmoid(x):
    return 0.5 * jnp.tanh(0.5 * x) + 0.5


def _one_minus_exp2(y):
    t = jnp.tanh(y)
    return (-2.0 * t) / (1.0 - t)


def _softplus(x):
    return jnp.maximum(x, 0.0) + jnp.log1p(jnp.exp(-jnp.abs(x)))


def _seq_specs(tr, delay=0):
    qb = tr // BLOCK
    tile = lambda i: jnp.maximum(i - delay, 0)
    return [pl.BlockSpec((BLOCK, D_MODEL), lambda i, *_, s=s: (jnp.maximum(tile(i) * qb + s - 1, 0), 0))
            for s in range(qb)]


def _seq_tile(head, pieces, i):
    first = jnp.where(i == 0, head, pieces[0][...])
    return jnp.concatenate([first] + [p[...] for p in pieces[1:]], axis=0)


GROUP_ROWS = GQA_GROUP * BLOCK


def _attn_bias():
    j = np.arange(2 * BLOCK)[:, None]
    i = np.arange(BLOCK)[None, :]
    band = (j - i >= 1) & (j - i <= BLOCK)
    out = []
    for n in range(3):
        ok = band & ((n - 1) * BLOCK + j >= PAD_ROWS) if n < 2 else band
        out.append(np.tile(np.where(ok, 0.0, NEG).astype(np.float32), (1, GQA_GROUP)))
    return jnp.asarray(np.stack(out))


def _heads_t(at, g):
    heads = range(GQA_GROUP * g, GQA_GROUP * (g + 1))
    return jnp.concatenate([at[h * HEAD_DIM:(h + 1) * HEAD_DIM] for h in heads], axis=1).astype(BF16)


def _from_heads_t(groups):
    pairs = []
    for p in groups:
        for h in range(0, GQA_GROUP, 2):
            two = jnp.concatenate([p[:, h * BLOCK:(h + 1) * BLOCK], p[:, (h + 1) * BLOCK:(h + 2) * BLOCK]], axis=0)
            pairs.append(two.T)
    return jnp.concatenate(pairs, axis=1)


def _stack_heads(a, g):
    heads = range(GQA_GROUP * g, GQA_GROUP * (g + 1))
    return jnp.concatenate([a[:, h * HEAD_DIM:(h + 1) * HEAD_DIM] for h in heads], axis=0)


def _unstack_heads(groups):
    return jnp.concatenate([p[h * BLOCK:(h + 1) * BLOCK] for p in groups for h in range(GQA_GROUP)], axis=1)


def _attn_probs_t(k_g, qg, bias, sink_row):
    st = _dot_nt(k_g, qg) + bias
    m = jnp.maximum(jnp.max(st, axis=0, keepdims=True), sink_row)
    p = jnp.exp(st - m)
    es = jnp.exp(sink_row - m)
    inv = 1.0 / (jnp.sum(p, axis=0, keepdims=True) + es)
    return p * inv, es * inv


def _attn_consts(sinks):
    return jnp.repeat(sinks.reshape(ATTN_HEADS), BLOCK).reshape(ATTN_HEADS // GQA_GROUP, GROUP_ROWS), _attn_bias()


_SINK_SPEC = pl.BlockSpec((ATTN_HEADS // GQA_GROUP, GROUP_ROWS), lambda n: (0, 0))
_BIAS_SPEC = pl.BlockSpec((3, 2 * BLOCK, GROUP_ROWS), lambda n: (0, 0, 0))
_QSCALE = HEAD_DIM ** -0.5


def _kv_specs(tr):
    qb = tr // BLOCK
    prev = lambda col: pl.BlockSpec((BLOCK, KV_WIDTH), lambda t: (jnp.maximum(t * qb - 1, 0), col))
    cur = lambda col: pl.BlockSpec((tr, KV_WIDTH), lambda t: (t, col))
    return [prev(4), cur(4), prev(5), cur(5)]


def _block_bias(b_ref, t, qb, i):
    return b_ref[2] if i >= 2 else b_ref[jnp.minimum(t * qb + i, 2)]


N_KV = ATTN_HEADS // GQA_GROUP


def _prob_specs(qb):
    return [pl.BlockSpec((qb, N_KV, 2 * BLOCK, GROUP_ROWS), lambda t: (t, 0, 0, 0)),
            pl.BlockSpec((qb, SUBLANES, GROUP_ROWS), lambda t: (t, 0, 0))]


def _attn_fwd(qkv, sinks):
    tp = qkv.shape[0]
    tr = _row_tile(tp)
    qb, nb = tr // BLOCK, tp // BLOCK
    sink_rows, bias = _attn_consts(sinks)

    def body(s_ref, b_ref, q_ref, kp_ref, kc_ref, vp_ref, vc_ref, o_ref, p_ref, ps_ref):
        t = pl.program_id(0)
        k_all = jnp.concatenate([kp_ref[...], kc_ref[...]], axis=0)
        v_all = jnp.concatenate([vp_ref[...], vc_ref[...]], axis=0)
        for i in range(qb):
            rows = slice(i * BLOCK, (i + 1) * BLOCK)
            q = q_ref[rows]
            k2, v2 = k_all[i * BLOCK:(i + 2) * BLOCK], v_all[i * BLOCK:(i + 2) * BLOCK]
            bias_n = _block_bias(b_ref, t, qb, i)
            outs, sink_probs = [], []
            for g in range(N_KV):
                cols = slice(g * HEAD_DIM, (g + 1) * HEAD_DIM)
                qg = _stack_heads(q, g) * jnp.asarray(_QSCALE, BF16)
                p, ps = _attn_probs_t(k2[:, cols], qg, bias_n, s_ref[g:g + 1])
                pb = p.astype(BF16)
                p_ref[i, g] = pb
                sink_probs.append(ps)
                outs.append(_dot_tn(pb, v2[:, cols]))
            o_ref[rows] = _unstack_heads(outs).astype(BF16)
            ps_ref[i] = jnp.concatenate(sink_probs + [jnp.zeros((SUBLANES - N_KV, GROUP_ROWS), F32)], axis=0)

    return pl.pallas_call(
        body, name="attn_fwd", grid=(tp // tr,),
        in_specs=[_SINK_SPEC, _BIAS_SPEC, pl.BlockSpec((tr, ATTN_WIDTH), lambda t: (t, 0))] + _kv_specs(tr),
        out_specs=[pl.BlockSpec((tr, ATTN_WIDTH), lambda t: (t, 0))] + _prob_specs(qb),
        out_shape=[jax.ShapeDtypeStruct((tp, ATTN_WIDTH), BF16),
                   jax.ShapeDtypeStruct((nb, N_KV, 2 * BLOCK, GROUP_ROWS), BF16),
                   jax.ShapeDtypeStruct((nb, SUBLANES, GROUP_ROWS), F32)],
        compiler_params=_params("parallel"),
    )(sink_rows, bias, qkv, qkv, qkv, qkv, qkv)


def _conv_taps(x, halo):
    ext = jnp.concatenate([halo, x], axis=0)
    return [ext[8:] if k == 3 else pltpu.roll(ext, 3 - k, 0)[8:] for k in range(4)]


def _lru_gates(xc, wa, ba, wx, bx, sp):
    xb = xc.astype(BF16)
    r = _sigmoid(_dot(xb, wa) + ba)
    ig = _sigmoid(_dot(xb, wx) + bx)
    log_a = (-LRU_C * sp) * r
    a = jnp.exp(log_a)
    mult = jnp.sqrt(_one_minus_exp2(log_a))
    return xb, r, ig, a, mult


SUBLANES = 8


def _scan_fwd(a, b, h_in):
    n, width = a.shape
    a, b = (v.reshape(n // SUBLANES, SUBLANES, width) for v in (a, b))
    in_group = lax.broadcasted_iota(jnp.int32, a.shape, 1)
    for d in (1, 2, 4):
        keep = in_group >= d
        b = jnp.where(keep, a * pltpu.roll(b, d, 1) + b, b)
        a = jnp.where(keep, a * pltpu.roll(a, d, 1), a)
    a, b = a.reshape(n, width), b.reshape(n, width)
    out, carry = [], h_in
    for g in range(0, n, SUBLANES):
        h = a[g:g + SUBLANES] * carry + b[g:g + SUBLANES]
        out.append(h)
        carry = h[SUBLANES - 1:]
    return jnp.concatenate(out, axis=0)


def _scan_rev(c, b, g_in):
    n, width = c.shape
    c, b = (v.reshape(n // SUBLANES, SUBLANES, width) for v in (c, b))
    in_group = lax.broadcasted_iota(jnp.int32, c.shape, 1)
    for d in (1, 2, 4):
        keep = in_group < SUBLANES - d
        b = jnp.where(keep, b + c * pltpu.roll(b, SUBLANES - d, 1), b)
        c = jnp.where(keep, c * pltpu.roll(c, SUBLANES - d, 1), c)
    c, b = c.reshape(n, width), b.reshape(n, width)
    out, carry = [], g_in
    for g in range(n - SUBLANES, -1, -SUBLANES):
        r = b[g:g + SUBLANES] + c[g:g + SUBLANES] * carry
        out.append(r)
        carry = r[:1]
    return jnp.concatenate(out[::-1], axis=0)


def _inproj_lru_fwd(head, x, g, w_in, conv_w, conv_b, wa, ba, wx, bx, lam, token):
    tp = BLOCK + x.shape[0]
    tr = _row_tile(tp)
    qb, nt = tr // BLOCK, tp // tr
    small = [conv_w, conv_b, wa, ba, wx, bx, lam]

    def body(*refs):
        head_ref, pieces = refs[0], refs[1:1 + qb]
        g_ref, w_ref, _, cw_ref, cb_ref, wa_ref, ba_ref, wx_ref, bx_ref, lam_ref = refs[1 + qb:11 + qb]
        u_ref, qkv_ref, xr_ref, yr_ref, hr_ref, rec_ref, zbuf, halo, hprev = refs[11 + qb:]
        i = pl.program_id(0)
        cur = i % 2

        @pl.when(i == 0)
        def _():
            halo[...] = jnp.zeros_like(halo)
            hprev[...] = jnp.zeros_like(hprev)
            zbuf[1] = jnp.zeros((tr, 2 * LRU_WIDTH), F32)

        def recurrent_branch(valid):
            cw, cb = cw_ref[...], cb_ref[...]
            wa_m, ba_v, wx_m, bx_v = wa_ref[...], ba_ref[...], wx_ref[...], bx_ref[...]
            sp = _softplus(-lam_ref[...])
            before, h_last = halo[...], hprev[0:1]
            for b in range(qb):
                rows = slice(b * BLOCK, (b + 1) * BLOCK)
                xy = zbuf[1 - cur, rows]
                xin = xy[:, :LRU_WIDTH]
                taps = _conv_taps(xin, before)
                before = xin[BLOCK - 8:]
                xc = cb + sum(cw[k:k + 1] * taps[k] for k in range(4))
                _, _, ig, a, mult = _lru_gates(xc, wa_m, ba_v, wx_m, bx_v, sp)
                u = mult * (ig * xc)
                if b == 0:
                    pos = (i - 1) * tr + lax.broadcasted_iota(jnp.int32, xc.shape, 0)
                    u = jnp.where(pos >= PAD_ROWS, u, 0.0)
                h = _scan_fwd(a, u, h_last)
                h_last = h[BLOCK - 1:]
                hr_ref[rows] = h
                gl, _ = _gelu(xy[:, LRU_WIDTH:])
                rec_ref[rows] = (gl * h).astype(BF16)
            halo[...] = jnp.where(valid, before, 0.0)
            hprev[0:1] = jnp.where(valid, h_last, 0.0)

        def projection():
            xhat, _ = _rms(_seq_tile(head_ref[...], pieces, i))
            u = (xhat * g_ref[...]).astype(BF16)
            u_ref[...] = u
            z = _dot_nt(u, w_ref[...])
            qkv_ref[...] = z[:, :QKV_WIDTH].astype(BF16)
            xr_ref[...] = z[:, QKV_WIDTH:QKV_WIDTH + LRU_WIDTH]
            yr_ref[...] = z[:, QKV_WIDTH + LRU_WIDTH:]
            zbuf[cur] = z[:, QKV_WIDTH:]

        @pl.when(i < nt)
        def _():
            recurrent_branch(i >= 1)
            projection()

        @pl.when(i == nt)
        def _():
            recurrent_branch(True)

    last = nt - 1
    this_row = lambda w: pl.BlockSpec((tr, w), lambda i: (jnp.minimum(i, last), 0))
    prev_row = lambda w: pl.BlockSpec((tr, w), lambda i: (jnp.maximum(i - 1, 0), 0))
    full = lambda a: pl.BlockSpec(a.shape, lambda i: (0,) * a.ndim)
    piece_specs = [pl.BlockSpec((BLOCK, D_MODEL), lambda i, s=s: (jnp.maximum(jnp.minimum(i, last) * qb + s - 1, 0), 0))
                   for s in range(qb)]
    return pl.pallas_call(
        body, name="inproj_lru_fwd", grid=(nt + 1,),
        in_specs=[full(head)] + piece_specs + [full(g), full(w_in), full(token)] + [full(a) for a in small],
        out_specs=[this_row(D_MODEL), this_row(QKV_WIDTH), this_row(LRU_WIDTH), this_row(LRU_WIDTH),
                   prev_row(LRU_WIDTH), prev_row(LRU_WIDTH)],
        out_shape=[jax.ShapeDtypeStruct((tp, D_MODEL), BF16), jax.ShapeDtypeStruct((tp, QKV_WIDTH), BF16),
                   jax.ShapeDtypeStruct((tp, LRU_WIDTH), F32), jax.ShapeDtypeStruct((tp, LRU_WIDTH), F32),
                   jax.ShapeDtypeStruct((tp, LRU_WIDTH), F32), jax.ShapeDtypeStruct((tp, LRU_WIDTH), BF16)],
        scratch_shapes=[pltpu.VMEM((2, tr, 2 * LRU_WIDTH), F32), pltpu.VMEM((8, LRU_WIDTH), F32),
                        pltpu.VMEM((8, LRU_WIDTH), F32)],
        compiler_params=_params("arbitrary"),
    )(head, *([x] * qb), g, w_in, token, *small)


def _outproj_fwd(attn, rec, w_out, head, x, g_post_mix, g_pre_ffn):
    tp = attn.shape[0]
    tr = _row_tile(tp)
    qb = tr // BLOCK

    def body(*refs):
        a_ref, r_ref, w_ref, head_ref = refs[:4]
        pieces = refs[4:4 + qb]
        gm_ref, gf_ref, mix_ref, h1_ref, u1_ref = refs[4 + qb:]
        mix = _dot(a_ref[...], w_ref[:ATTN_WIDTH]) + _dot(r_ref[...], w_ref[ATTN_WIDTH:])
        mix_ref[...] = mix
        mhat, _ = _rms(mix)
        h1 = _seq_tile(head_ref[...], pieces, pl.program_id(0)) + mhat * gm_ref[...]
        h1_ref[...] = h1
        hhat, _ = _rms(h1)
        u1_ref[...] = (hhat * gf_ref[...]).astype(BF16)

    row = lambda w: pl.BlockSpec((tr, w), lambda i: (i, 0))
    full = lambda a: pl.BlockSpec(a.shape, lambda i: (0,) * a.ndim)
    return pl.pallas_call(
        body, name="outproj_fwd", grid=(tp // tr,),
        in_specs=[row(ATTN_WIDTH), row(LRU_WIDTH), full(w_out), full(head)] + _seq_specs(tr)
        + [full(g_post_mix), full(g_pre_ffn)],
        out_specs=[row(D_MODEL), row(D_MODEL), row(D_MODEL)],
        out_shape=[jax.ShapeDtypeStruct((tp, D_MODEL), F32), jax.ShapeDtypeStruct((tp, D_MODEL), F32),
                   jax.ShapeDtypeStruct((tp, D_MODEL), BF16)],
        compiler_params=_params("parallel"),
    )(attn, rec, w_out, head, *([x] * qb), g_post_mix, g_pre_ffn)


FFN_STEPS = N_CHIPS


def _resident(a):
    return pl.BlockSpec(a.shape, lambda *_: (0,) * a.ndim, pipeline_mode=pl.Buffered(1))


def _ffn_fwd(u1, w1, w2, h1, tgt, g_post_ffn):
    tp = h1.shape[0]
    tr = _row_tile(tp)
    qb, nt = tr // BLOCK, tp // tr
    sr = tr // FFN_STEPS

    def body(*refs):
        u_ref, w1_ref, w2_ref, h1_ref = refs[:4]
        t_pieces = refs[4:4 + qb]
        g_ref, r1_ref, dy_ref, df2_ref, loss_ref, dg_ref, acc = refs[4 + qb:]
        i, c = pl.program_id(0), pl.program_id(1)
        cur = i % 2

        @pl.when((i == 0) & (c == 0))
        def _():
            loss_ref[...] = jnp.zeros_like(loss_ref)
            dg_ref[...] = jnp.zeros_like(dg_ref)
            acc[1] = jnp.zeros((tr, D_MODEL), F32)

        def matmuls():
            r = jnp.maximum(_dot(u_ref[...], w1_ref[c]), 0.0)
            r1_ref[...] = r.astype(BF16)
            return _dot((r * r).astype(BF16), w2_ref[c])

        def finish_previous_tile(k, valid):
            lo, hi = k * sr, (k + 1) * sr
            g = g_ref[...]
            fhat, rs = _rms(acc[1 - cur, lo:hi])
            h2 = h1_ref[...] + fhat * g
            rows = (i - 1) * tr + lo + lax.broadcasted_iota(jnp.int32, h2.shape, 0)
            tgt = jnp.concatenate([p[max(lo - s * BLOCK, 0):min(hi - s * BLOCK, BLOCK)] for s, p in enumerate(t_pieces)
                                   if lo < (s + 1) * BLOCK and hi > s * BLOCK], axis=0)
            err = jnp.where((rows >= BLOCK) & valid, h2 - tgt, 0.0)
            dy = err * (1.0 / D_MODEL)
            dy_ref[...] = dy
            loss_ref[...] += (0.5 / D_MODEL) * jnp.sum(err * err)
            df2, dg = _rms_bwd(fhat, rs, g, dy)
            df2_ref[...] = df2.astype(BF16)
            dg_ref[...] += dg

        for k in range(FFN_STEPS):
            @pl.when((c == k) & (i < nt))
            def _(k=k):
                finish_previous_tile(k, i >= 1)
                if k == 0:
                    acc[cur] = matmuls()
                else:
                    acc[cur] += matmuls()

            @pl.when((c == k) & (i == nt))
            def _(k=k):
                finish_previous_tile(k, True)

    last = nt - 1
    this_row = pl.BlockSpec((tr, D_MODEL), lambda i, c: (jnp.minimum(i, last), 0))
    prev_quarter = pl.BlockSpec((sr, D_MODEL), lambda i, c: (jnp.maximum(i - 1, 0) * FFN_STEPS + c, 0))
    prev_quarter_out = pl.BlockSpec(
        (sr, D_MODEL), lambda i, c: (jnp.where(i == 0, nt * FFN_STEPS, (i - 1) * FFN_STEPS + c), 0))
    full = lambda a: pl.BlockSpec(a.shape, lambda i, c: (0,) * a.ndim)
    return pl.pallas_call(
        body, name="ffn_fwd", grid=(nt + 1, FFN_STEPS),
        in_specs=[this_row, _resident(w1), _resident(w2), prev_quarter] + _seq_specs(tr, delay=1) + [full(g_post_ffn)],
        out_specs=[pl.BlockSpec((tr, FF_CHUNK), lambda i, c: (jnp.minimum(i, last), jnp.where(i < nt, c, FFN_STEPS - 1))),
                   prev_quarter_out, prev_quarter_out,
                   pl.BlockSpec((1, 1), lambda i, c: (0, 0)), pl.BlockSpec((1, D_MODEL), lambda i, c: (0, 0))],
        out_shape=[jax.ShapeDtypeStruct((tp, D_FF), BF16), jax.ShapeDtypeStruct((tp + sr, D_MODEL), F32),
                   jax.ShapeDtypeStruct((tp + sr, D_MODEL), BF16), jax.ShapeDtypeStruct((1, 1), F32),
                   jax.ShapeDtypeStruct((1, D_MODEL), F32)],
        scratch_shapes=[pltpu.VMEM((2, tr, D_MODEL), F32)],
        compiler_params=_params("arbitrary", "arbitrary"),
    )(u1, w1, w2, h1, *([tgt] * qb), g_post_ffn)


def _ffn_bwd_data(df2, r1, w1, w2, dy, h1, mix, g_pre_ffn, g_post_mix):
    tp = h1.shape[0]
    tr = _row_tile(tp)
    nt = tp // tr
    sr = tr // FFN_STEPS

    def body(df2_ref, r1_ref, w1_ref, w2_ref, dy_ref, h1_ref, mix_ref, gf_ref, gm_ref,
             da_ref, dh1_ref, dmix_ref, dgf_ref, dgm_ref, acc):
        i, c = pl.program_id(0), pl.program_id(1)
        cur = i % 2

        @pl.when((i == 0) & (c == 0))
        def _():
            dgf_ref[...] = jnp.zeros_like(dgf_ref)
            dgm_ref[...] = jnp.zeros_like(dgm_ref)
            acc[1] = jnp.zeros((tr, D_MODEL), F32)

        def matmuls():
            df = _dot_nt(df2_ref[...], w2_ref[c])
            da = (df * (2.0 * r1_ref[...].astype(F32))).astype(BF16)
            da_ref[...] = da
            return _dot_nt(da, w1_ref[c])

        def finish_previous_tile(k, valid):
            lo, hi = k * sr, (k + 1) * sr
            hhat, rs = _rms(h1_ref[...])
            dx, dgf = _rms_bwd(hhat, rs, gf_ref[...], acc[1 - cur, lo:hi])
            dh1 = dy_ref[...] + dx
            dh1_ref[...] = dh1
            mhat, rsm = _rms(mix_ref[...])
            dmix, dgm = _rms_bwd(mhat, rsm, gm_ref[...], dh1)
            dmix_ref[...] = dmix.astype(BF16)
            dgf_ref[...] += jnp.where(valid, dgf, 0.0)
            dgm_ref[...] += jnp.where(valid, dgm, 0.0)

        for k in range(FFN_STEPS):
            @pl.when((c == k) & (i < nt))
            def _(k=k):
                finish_previous_tile(k, i >= 1)
                if k == 0:
                    acc[cur] = matmuls()
                else:
                    acc[cur] += matmuls()

            @pl.when((c == k) & (i == nt))
            def _(k=k):
                finish_previous_tile(k, True)

    last = nt - 1
    this_row = pl.BlockSpec((tr, D_MODEL), lambda i, c: (jnp.minimum(i, last), 0))
    prev_quarter = pl.BlockSpec((sr, D_MODEL), lambda i, c: (jnp.maximum(i - 1, 0) * FFN_STEPS + c, 0))
    prev_quarter_out = pl.BlockSpec(
        (sr, D_MODEL), lambda i, c: (jnp.where(i == 0, nt * FFN_STEPS, (i - 1) * FFN_STEPS + c), 0))
    chunk = pl.BlockSpec((tr, FF_CHUNK), lambda i, c: (jnp.minimum(i, last), jnp.where(i < nt, c, FFN_STEPS - 1)))
    gain = pl.BlockSpec((1, D_MODEL), lambda i, c: (0, 0))
    return pl.pallas_call(
        body, name="ffn_bwd_data", grid=(nt + 1, FFN_STEPS),
        in_specs=[this_row, chunk, _resident(w1), _resident(w2), prev_quarter, prev_quarter, prev_quarter, gain, gain],
        out_specs=[chunk, prev_quarter_out, prev_quarter_out, gain, gain],
        out_shape=[jax.ShapeDtypeStruct((tp, D_FF), BF16), jax.ShapeDtypeStruct((tp + sr, D_MODEL), F32),
                   jax.ShapeDtypeStruct((tp + sr, D_MODEL), BF16), jax.ShapeDtypeStruct((1, D_MODEL), F32),
                   jax.ShapeDtypeStruct((1, D_MODEL), F32)],
        scratch_shapes=[pltpu.VMEM((2, tr, D_MODEL), F32)],
        compiler_params=_params("arbitrary", "arbitrary"),
    )(df2, r1, w1, w2, dy, h1, mix, g_pre_ffn, g_post_mix)


def _ffn_bwd_weights(u1, da1, r1, df2):
    tp = u1.shape[0]
    tr = _wgrad_row_tile(tp)

    def body(u_ref, da_ref, r1_ref, df2_ref, dw1_ref, dw2_ref):
        i = pl.program_id(1)

        def products():
            r = r1_ref[...].astype(F32)
            return _dot_tn(u_ref[...], da_ref[...]), _dot_tn((r * r).astype(BF16), df2_ref[...])

        @pl.when(i == 0)
        def _():
            dw1_ref[0], dw2_ref[0] = products()

        @pl.when(i > 0)
        def _():
            p1, p2 = products()
            dw1_ref[0] += p1
            dw2_ref[0] += p2

    row = pl.BlockSpec((tr, D_MODEL), lambda c, i: (i, 0))
    chunk = pl.BlockSpec((tr, FF_CHUNK), lambda c, i: (i, c))
    return pl.pallas_call(
        body, name="ffn_bwd_weights", grid=(N_CHIPS, tp // tr),
        in_specs=[row, chunk, chunk, row],
        out_specs=[pl.BlockSpec((1, D_MODEL, FF_CHUNK), lambda c, i: (c, 0, 0)),
                   pl.BlockSpec((1, FF_CHUNK, D_MODEL), lambda c, i: (c, 0, 0))],
        out_shape=[jax.ShapeDtypeStruct((N_CHIPS, D_MODEL, FF_CHUNK), F32),
                   jax.ShapeDtypeStruct((N_CHIPS, FF_CHUNK, D_MODEL), F32)],
        compiler_params=_params("parallel", "arbitrary"),
    )(u1, da1, r1, df2)


N_VEC_ROWS = 8


def _outproj_lru_bwd(dmix, w_out, attn, rec, xr, yr, hr, conv_w, conv_b, wa, ba, wx, bx, lam, token):
    tp = xr.shape[0]
    tr = _row_tile(tp)
    qb, nt = tr // BLOCK, tp // tr

    def body(dm_ref, w_ref, at_ref, rc_ref, xr_ref, xh_ref, yr_ref, hr_ref, hp_ref,
             cw_ref, cb_ref, wa_ref, ba_ref, wx_ref, bx_ref, lam_ref, _,
             dxr_ref, dyr_ref, dat_ref, dwo_ref, dwa_ref, dwx_ref, vec_ref, g_next, a_next, dxc_next, dsp):
        s = pl.program_id(0)
        t = nt - 1 - s

        @pl.when(s == 0)
        def _():
            g_next[...] = jnp.zeros_like(g_next)
            a_next[...] = jnp.zeros_like(a_next)
            dxc_next[...] = jnp.zeros_like(dxc_next)
            dsp[...] = jnp.zeros_like(dsp)
            dwo_ref[...] = jnp.zeros_like(dwo_ref)
            dwa_ref[...] = jnp.zeros_like(dwa_ref)
            dwx_ref[...] = jnp.zeros_like(dwx_ref)
            vec_ref[...] = jnp.zeros_like(vec_ref)

        dm = dm_ref[...]
        dcat = _dot_nt(dm, w_ref[...])
        dat_ref[...] = dcat[:, :ATTN_WIDTH].astype(BF16)
        drec_tile = dcat[:, ATTN_WIDTH:]
        dwo_ref[:ATTN_WIDTH] += _dot_tn(at_ref[...], dm)
        dwo_ref[ATTN_WIDTH:] += _dot_tn(rc_ref[...], dm)

        first_tile = t == 0
        cw, cb = cw_ref[...], cb_ref[...]
        lam_v = lam_ref[...]
        sp = _softplus(-lam_v)
        wa_m, ba_v, wx_m, bx_v = wa_ref[...], ba_ref[...], wx_ref[...], bx_ref[...]
        rows = lax.broadcasted_iota(jnp.int32, (BLOCK, LRU_WIDTH), 0)
        col = lambda v: jnp.sum(v, axis=0, keepdims=True)

        g_after, a_after, dxc_after = g_next[0:1], a_next[0:1], dxc_next[...]
        xbs, dgrs, dgis = [], [], []
        vec = [jnp.zeros((1, LRU_WIDTH), F32) for _ in range(N_VEC_ROWS)]
        for i in reversed(range(qb)):
            blk = slice(i * BLOCK, (i + 1) * BLOCK)
            if i == 0:
                x_before = jnp.where(first_tile, 0.0, xh_ref[...])
                h_before = jnp.where(first_tile, 0.0, hp_ref[7:8])
            else:
                x_before = xr_ref[i * BLOCK - 8:i * BLOCK]
                h_before = hr_ref[i * BLOCK - 1:i * BLOCK]
            taps = _conv_taps(xr_ref[blk], x_before)
            xc = cb + sum(cw[k:k + 1] * taps[k] for k in range(4))
            xb, r, ig, a, mult = _lru_gates(xc, wa_m, ba_v, wx_m, bx_v, sp)

            yr_v = yr_ref[blk]
            gl, th = _gelu(yr_v)
            h = hr_ref[blk]
            drec = drec_tile[blk]
            dyr_ref[blk] = (drec * h * _gelu_grad(yr_v, th)).astype(BF16)

            a_up = jnp.where(rows == BLOCK - 1, a_after, pltpu.roll(a, BLOCK - 1, 0))
            g = _scan_rev(a_up, drec * gl, g_after)
            g_after, a_after = g[0:1], a[0:1]

            h_prev = jnp.where(rows == 0, h_before, pltpu.roll(h, 1, 0))
            du, da = g, g * h_prev
            if i == 0:
                real = (t * tr + rows) >= PAD_ROWS
                du, da = jnp.where(real, du, 0.0), jnp.where(real, da, 0.0)
            dmult = du * (ig * xc)
            dig = du * (mult * xc)
            dxc = du * (mult * ig)
            dlog_a = da * a - dmult * (a * a / mult)
            if i == 0:
                dlog_a = jnp.where(real, dlog_a, 0.0)
            dgr = (dlog_a * (-LRU_C * sp)) * (r * (1.0 - r))
            dgi = dig * (ig * (1.0 - ig))
            dgr_b, dgi_b = dgr.astype(BF16), dgi.astype(BF16)
            dxc = dxc + _dot_nt(dgr_b, wa_m) + _dot_nt(dgi_b, wx_m)
            xbs.append(xb)
            dgrs.append(dgr_b)
            dgis.append(dgi_b)

            ext = jnp.concatenate([dxc, dxc_after], axis=0)
            up = [ext[:BLOCK] if j == 0 else pltpu.roll(ext, BLOCK + 8 - j, 0)[:BLOCK] for j in range(4)]
            dxr_ref[blk] = sum(cw[k:k + 1] * up[3 - k] for k in range(4)).astype(BF16)
            dxc_after = dxc[:8]

            for k in range(4):
                vec[k] = vec[k] + col(dxc * taps[k])
            vec[4] = vec[4] + col(dxc)
            vec[5] = vec[5] + col(dgr)
            vec[6] = vec[6] + col(dgi)
            vec[7] = vec[7] + col(dlog_a * (-LRU_C * r))

        g_next[0:1], a_next[0:1], dxc_next[...] = g_after, a_after, dxc_after
        xb_all = jnp.concatenate(xbs, axis=0)
        dwa_ref[...] += _dot_tn(xb_all, jnp.concatenate(dgrs, axis=0))
        dwx_ref[...] += _dot_tn(xb_all, jnp.concatenate(dgis, axis=0))
        for k in range(7):
            vec_ref[k:k + 1] += vec[k]
        dsp[0:1] += vec[7]

        @pl.when(s == nt - 1)
        def _():
            vec_ref[7:8] = dsp[0:1] * (-_sigmoid(-lam_v))

    blk_spec = pl.BlockSpec((tr, LRU_WIDTH), lambda s: (nt - 1 - s, 0))
    rows_before = pl.BlockSpec((8, LRU_WIDTH), lambda s: (jnp.maximum((nt - 1 - s) * (tr // 8) - 1, 0), 0))
    full = lambda a: pl.BlockSpec(a.shape, lambda s: (0,) * a.ndim)
    small = [conv_w, conv_b, wa, ba, wx, bx, lam, token]
    sq = pl.BlockSpec((LRU_WIDTH, LRU_WIDTH), lambda s: (0, 0))
    wide = pl.BlockSpec((tr, D_MODEL), lambda s: (nt - 1 - s, 0))
    whole = pl.BlockSpec((D_MODEL, D_MODEL), lambda s: (0, 0))
    return pl.pallas_call(
        body, name="outproj_lru_bwd", grid=(nt,),
        in_specs=[wide, whole, blk_spec, blk_spec, blk_spec, rows_before, blk_spec, blk_spec, rows_before]
        + [full(a) for a in small],
        out_specs=[blk_spec, blk_spec, blk_spec, whole, sq, sq, pl.BlockSpec((N_VEC_ROWS, LRU_WIDTH), lambda s: (0, 0))],
        out_shape=[jax.ShapeDtypeStruct((tp, LRU_WIDTH), BF16), jax.ShapeDtypeStruct((tp, LRU_WIDTH), BF16),
                   jax.ShapeDtypeStruct((tp, ATTN_WIDTH), BF16), jax.ShapeDtypeStruct((D_MODEL, D_MODEL), F32),
                   jax.ShapeDtypeStruct((LRU_WIDTH, LRU_WIDTH), F32), jax.ShapeDtypeStruct((LRU_WIDTH, LRU_WIDTH), F32),
                   jax.ShapeDtypeStruct((N_VEC_ROWS, LRU_WIDTH), F32)],
        scratch_shapes=[pltpu.VMEM((8, LRU_WIDTH), F32)] * 4,
        compiler_params=_params("arbitrary"),
    )(dmix, w_out, attn, rec, xr, xr, yr, hr, hr, *small)


def _attn_bwd_tile(tp):
    return _wgrad_row_tile(tp)


def _attn_bwd(qkv, dattn, probs, sink_probs, token):
    tp = qkv.shape[0]
    tr = _attn_bwd_tile(tp)
    qb, nt = tr // BLOCK, tp // tr
    n_groups = N_KV

    def body(p_ref, ps_ref, q_ref, kp_ref, kc_ref, vp_ref, vc_ref, do_ref, _, dq_ref, dkv_ref, ex_ref, ds_ref, dsink):
        t = pl.program_id(0)

        @pl.when(t == 0)
        def _():
            dsink[...] = jnp.zeros_like(dsink)

        k_all = jnp.concatenate([kp_ref[...], kc_ref[...]], axis=0)
        v_all = jnp.concatenate([vp_ref[...], vc_ref[...]], axis=0)
        tail = None
        for i in range(qb):
            rows = slice(i * BLOCK, (i + 1) * BLOCK)
            qt = (q_ref[rows].astype(F32) * _QSCALE).T
            dot = do_ref[rows].astype(F32).T
            k2, v2 = k_all[i * BLOCK:(i + 2) * BLOCK], v_all[i * BLOCK:(i + 2) * BLOCK]
            dqs, dks, dvs = [], [], []
            for g in range(n_groups):
                cols = slice(g * HEAD_DIM, (g + 1) * HEAD_DIM)
                k_g, v_g = k2[:, cols], v2[:, cols]
                qgt, dogt = _heads_t(qt, g), _heads_t(dot, g)
                pb = p_ref[i, g]
                p = pb.astype(F32)
                dpt = _dot(v_g, dogt)
                delta = jnp.sum(p * dpt, axis=0, keepdims=True)
                dst = (p * (dpt - delta)).astype(BF16)
                dqs.append(_dot_tn(k_g, dst) * _QSCALE)
                dks.append(_dot_nt(qgt, dst))
                dvs.append(_dot_nt(dogt, pb))
                dsink[g:g + 1] -= ps_ref[i, g:g + 1] * delta
            dq_ref[rows] = _from_heads_t(dqs).astype(BF16)
            dkv = jnp.concatenate([jnp.concatenate(dks, axis=0).T, jnp.concatenate(dvs, axis=0).T], axis=1)
            if i == 0:
                ex_ref[0] = dkv[:BLOCK]
            else:
                dkv_ref[(i - 1) * BLOCK:i * BLOCK] = (tail + dkv[:BLOCK]).astype(BF16)
            tail = dkv[BLOCK:]
        dkv_ref[(qb - 1) * BLOCK:] = tail.astype(BF16)

        @pl.when(t == nt - 1)
        def _():
            lane = lax.broadcasted_iota(jnp.int32, (1, ATTN_HEADS), 1)
            acc = jnp.zeros((1, ATTN_HEADS), F32)
            for h in range(ATTN_HEADS):
                g, hh = divmod(h, GQA_GROUP)
                acc = acc + jnp.where(lane == h, jnp.sum(dsink[g:g + 1, hh * BLOCK:(hh + 1) * BLOCK]), 0.0)
            ds_ref[...] = acc

    cur = lambda w: pl.BlockSpec((tr, w), lambda t: (t, 0))
    return pl.pallas_call(
        body, name="attn_bwd", grid=(nt,),
        in_specs=_prob_specs(qb) + [cur(ATTN_WIDTH)] + _kv_specs(tr)
        + [cur(ATTN_WIDTH), pl.BlockSpec(token.shape, lambda t: (0, 0))],
        out_specs=[cur(ATTN_WIDTH), cur(2 * KV_WIDTH), pl.BlockSpec((1, BLOCK, 2 * KV_WIDTH), lambda t: (t, 0, 0)),
                   pl.BlockSpec((1, ATTN_HEADS), lambda t: (0, 0))],
        out_shape=[jax.ShapeDtypeStruct((tp, ATTN_WIDTH), BF16), jax.ShapeDtypeStruct((tp, 2 * KV_WIDTH), BF16),
                   jax.ShapeDtypeStruct((nt, BLOCK, 2 * KV_WIDTH), F32), jax.ShapeDtypeStruct((1, ATTN_HEADS), F32)],
        scratch_shapes=[pltpu.VMEM((n_groups, GROUP_ROWS), F32)],
        compiler_params=_params("arbitrary"),
    )(probs, sink_probs, qkv, qkv, qkv, qkv, qkv, dattn, token)


def _fix_dkv(dkv, dkv_extra):
    tp = dkv.shape[0]
    tr = _attn_bwd_tile(tp)
    nt, qb = tp // tr, tr // BLOCK
    if nt == 1:
        return dkv

    def body(d_ref, ex_ref, o_ref):
        o_ref[...] = (d_ref[...].astype(F32) + ex_ref[0]).astype(BF16)

    last = pl.BlockSpec((BLOCK, 2 * KV_WIDTH), lambda t: (t * qb + qb - 1, 0))
    return pl.pallas_call(
        body, name="fix_dkv", grid=(nt - 1,),
        in_specs=[last, pl.BlockSpec((1, BLOCK, 2 * KV_WIDTH), lambda t: (t + 1, 0, 0))],
        out_specs=last, out_shape=jax.ShapeDtypeStruct(dkv.shape, dkv.dtype),
        input_output_aliases={0: 0}, compiler_params=_params("parallel"),
    )(dkv, dkv_extra)


def _inproj_wgrad(dq, dkv, dxr, dyr, u0):
    tp = dq.shape[0]
    tr = _wgrad_row_tile(tp)

    def body(dq_ref, dkv_ref, dxr_ref, dyr_ref, u_ref, dw_ref):
        i = pl.program_id(0)

        def product():
            dz = jnp.concatenate([dq_ref[...], dkv_ref[...], dxr_ref[...], dyr_ref[...]], axis=1)
            return _dot_tn(dz, u_ref[...])

        @pl.when(i == 0)
        def _():
            dw_ref[...] = product()

        @pl.when(i > 0)
        def _():
            dw_ref[...] += product()

    row = lambda w: pl.BlockSpec((tr, w), lambda i: (i, 0))
    return pl.pallas_call(
        body, name="inproj_wgrad", grid=(tp // tr,),
        in_specs=[row(ATTN_WIDTH), row(2 * KV_WIDTH), row(LRU_WIDTH), row(LRU_WIDTH), row(D_MODEL)],
        out_specs=pl.BlockSpec((IN_WIDTH, D_MODEL), lambda i: (0, 0)),
        out_shape=jax.ShapeDtypeStruct((IN_WIDTH, D_MODEL), F32),
        compiler_params=_params("arbitrary"),
    )(dq, dkv, dxr, dyr, u0)


def _inproj_dgrad(dq, dkv, dxr, dyr, w_in, head, x, dh1, g, token):
    tp = dq.shape[0]
    tr = _row_tile(tp)
    nt, qb = tp // tr, tr // BLOCK

    def body(*refs):
        dq_ref, dkv_ref, dxr_ref, dyr_ref, w_ref, head_ref = refs[:6]
        pieces = refs[6:6 + qb]
        dh1_ref, g_ref, _, gx_ref, dhead_ref, dg_ref, buf, sems = refs[6 + qb:]
        i = pl.program_id(0)
        slot = i % 2

        def out_copy(step, at):
            return pltpu.make_async_copy(buf.at[at], gx_ref.at[pl.ds(step * tr - BLOCK, tr)], sems.at[at])

        dz = jnp.concatenate([dq_ref[...], dkv_ref[...], dxr_ref[...], dyr_ref[...]], axis=1)
        du = _dot(dz, w_ref[...])
        hhat, rs = _rms(_seq_tile(head_ref[...], pieces, i))
        dx, dg = _rms_bwd(hhat, rs, g_ref[...], du)
        dh0 = dh1_ref[...] + dx

        @pl.when(i >= 3)
        def _():
            out_copy(i - 2, slot).wait()

        buf[slot] = dh0

        @pl.when(i == 0)
        def _():
            dg_ref[...] = dg
            dhead_ref[...] = dh0[:BLOCK]
            if tr > BLOCK:
                first = pltpu.make_async_copy(buf.at[0, pl.ds(BLOCK, tr - BLOCK)], gx_ref.at[pl.ds(0, tr - BLOCK)],
                                              sems.at[0])
                first.start()
                first.wait()

        @pl.when(i >= 1)
        def _():
            dg_ref[...] += dg
            out_copy(i, slot).start()

        @pl.when(i == nt - 1)
        def _():
            if nt >= 3:
                out_copy(nt - 2, (nt - 2) % 2).wait()
            if nt >= 2:
                out_copy(nt - 1, (nt - 1) % 2).wait()

    row = lambda w: pl.BlockSpec((tr, w), lambda i: (i, 0))
    full = lambda shape: pl.BlockSpec(shape, lambda i: (0,) * len(shape))
    return pl.pallas_call(
        body, name="inproj_dgrad", grid=(tp // tr,),
        in_specs=[row(ATTN_WIDTH), row(2 * KV_WIDTH), row(LRU_WIDTH), row(LRU_WIDTH), full(w_in.shape),
                  full(head.shape)] + _seq_specs(tr) + [row(D_MODEL), full(g.shape), full(token.shape)],
        out_specs=[pl.BlockSpec(memory_space=pl.ANY), full((BLOCK, D_MODEL)), full((1, D_MODEL))],
        out_shape=[jax.ShapeDtypeStruct(x.shape, F32), jax.ShapeDtypeStruct((BLOCK, D_MODEL), F32),
                   jax.ShapeDtypeStruct((1, D_MODEL), F32)],
        scratch_shapes=[pltpu.VMEM((2, tr, D_MODEL), F32), pltpu.SemaphoreType.DMA((2,))],
        compiler_params=_params("arbitrary"),
    )(dq, dkv, dxr, dyr, w_in, head, *([x] * qb), dh1, g, token)


def _dense_block_diag(w):
    eye = jnp.eye(LRU_BLOCKS, dtype=w.dtype)
    return (w[:, :, None, :] * eye[:, None, :, None]).reshape(LRU_WIDTH, LRU_WIDTH)


def _diag_blocks(dense):
    d4 = dense.reshape(LRU_BLOCKS, LRU_BLOCK, LRU_BLOCKS, LRU_BLOCK)
    return jnp.stack([d4[n, :, n, :] for n in range(LRU_BLOCKS)])


def _local_step(head, x, tgt, g_pre_mix, w_in, conv_w, conv_b, w_a, b_a, w_x, b_x, lam, sinks, g_post_mix,
                g_pre_ffn, g_post_ffn, late_weights, on_ffn_grads, on_outproj_bwd, on_mixer_grads, token):
    wa = _dense_block_diag(w_a).astype(BF16)
    wx = _dense_block_diag(w_x).astype(BF16)

    u0, qkv, xr, yr, hr, rec = _inproj_lru_fwd(head, x, g_pre_mix, w_in, conv_w, conv_b, wa, b_a, wx, b_x, lam, token)
    attn, probs, sink_probs = _attn_fwd(qkv, sinks)
    w_out, w1, w2 = late_weights([attn, rec])
    mix, h1, u1 = _outproj_fwd(attn, rec, w_out, head, x, g_post_mix, g_pre_ffn)
    r1, dy, df2, loss, dg_post_ffn = _ffn_fwd(u1, w1, w2, h1, tgt, g_post_ffn)

    da1, dh1, dmix, dg_pre_ffn, dg_post_mix = _ffn_bwd_data(df2, r1, w1, w2, dy, h1, mix, g_pre_ffn, g_post_mix)
    dw1, dw2 = _ffn_bwd_weights(u1, da1, r1, df2)
    token2 = on_ffn_grads(dw1, dw2)
    dxr, dyr, dattn, dw_out, dwa, dwx, vec = _outproj_lru_bwd(dmix, w_out, attn, rec, xr, yr, hr, conv_w, conv_b,
                                                              wa, b_a, wx, b_x, lam, token2)
    token3 = on_outproj_bwd(dattn)
    dq, dkv, dkv_extra, dsinks = _attn_bwd(qkv, dattn, probs, sink_probs, token3)
    dkv = _fix_dkv(dkv, dkv_extra)
    dw_in = _inproj_wgrad(dq, dkv, dxr, dyr, u0)
    token4 = on_mixer_grads(dw_in, dw_out)
    dx, dhead, dg_pre_mix = _inproj_dgrad(dq, dkv, dxr, dyr, w_in, head, x, dh1, g_pre_mix, token4)

    grads = dict(
        g_pre_mix=dg_pre_mix, conv_w=vec[0:4], conv_b=vec[4:5], w_a=_diag_blocks(dwa), b_a=vec[5:6],
        w_x=_diag_blocks(dwx), b_x=vec[6:7], lru_lambda=vec[7:8], attn_sinks=dsinks,
        g_post_mix=dg_post_mix, g_pre_ffn=dg_pre_ffn, g_post_ffn=dg_post_ffn)
    return loss, dx, dhead, grads


HBM = pl.BlockSpec(memory_space=pltpu.HBM)


def _mesh_pos():
    return lax.axis_index("x"), lax.axis_index("y"), lax.axis_index("c")


def _other_chips(x, y):
    return [(1 - x, y), (x, 1 - y), (1 - x, 1 - y)]


def _remote(src, dst, send_sem, recv_sem, to):
    return pltpu.make_async_remote_copy(src_ref=src, dst_ref=dst, send_sem=send_sem, recv_sem=recv_sem,
                                        device_id=to, device_id_type=MESH)


def _gather_weights(shards, lands, tiny, tiny_land):
    nbig = len(shards)

    def body(*refs):
        srcs, tiny_src = refs[:nbig], refs[nbig]
        outs, tiny_out = refs[2 * nbig + 2:3 * nbig + 2], refs[3 * nbig + 2]
        ici_send, ici_recv, d2d_send, d2d_recv, tiny_send, tiny_recv = refs[3 * nbig + 3:]
        x, y, c = _mesh_pos()
        me = 2 * x + y
        chips = _other_chips(x, y)
        sibling = (x, y, 1 - c)
        sends = []
        for w, (src, out) in enumerate(zip(srcs, outs)):
            hr = src.shape[0] // 2
            for j, chip in enumerate(chips):
                k = 3 * w + j
                cp = _remote(src.at[pl.ds(c * hr, hr)], out.at[me, pl.ds(c * hr, hr)],
                             ici_send.at[k], ici_recv.at[k], (*chip, c))
                cp.start()
                sends.append(cp)
        for j, chip in enumerate(chips):
            cp = _remote(tiny_src, tiny_out.at[me], tiny_send.at[j], tiny_recv.at[j], (*chip, c))
            cp.start()
            sends.append(cp)
        for w, (src, out) in enumerate(zip(srcs, outs)):
            hr = src.shape[0] // 2
            for j, (px, py) in enumerate(chips):
                k = 3 * w + j
                landed = out.at[2 * px + py, pl.ds(c * hr, hr)]
                _remote(landed, landed, ici_send.at[k], ici_recv.at[k], sibling).wait_recv()
                cp = _remote(landed, landed, d2d_send.at[k], d2d_recv.at[k], sibling)
                cp.start()
                sends.append(cp)
        for w, (src, out) in enumerate(zip(srcs, outs)):
            hr = src.shape[0] // 2
            for j, (px, py) in enumerate(chips):
                k = 3 * w + j
                other = out.at[2 * px + py, pl.ds((1 - c) * hr, hr)]
                _remote(other, other, d2d_send.at[k], d2d_recv.at[k], sibling).wait_recv()
        for j, (px, py) in enumerate(chips):
            blk = tiny_out.at[2 * px + py]
            _remote(blk, blk, tiny_send.at[j], tiny_recv.at[j], sibling).wait_recv()
        for cp in sends:
            cp.wait_send()

    out_shape = [jax.ShapeDtypeStruct(l.shape, l.dtype) for l in list(lands) + [tiny_land]]
    n = 3 * nbig
    return pl.pallas_call(
        body, name="gather_weights", out_shape=out_shape,
        in_specs=[HBM] * (2 * nbig + 2), out_specs=[HBM] * (nbig + 1),
        input_output_aliases={nbig + 1 + i: i for i in range(nbig + 1)},
        scratch_shapes=[pltpu.SemaphoreType.DMA((n,)),
                        pltpu.SemaphoreType.DMA((n,)), pltpu.SemaphoreType.DMA((n,)), pltpu.SemaphoreType.DMA((n,)),
                        pltpu.SemaphoreType.DMA((3,)), pltpu.SemaphoreType.DMA((3,))],
    )(*shards, tiny, *lands, tiny_land)


def _prep_shard(w, me):
    rows, cols = w.shape
    tr = _elementwise_tile(rows)

    def body(me_ref, w_ref, s_ref, l_ref):
        b = w_ref[...].astype(BF16)
        s_ref[...] = b
        l_ref[0] = b

    return pl.pallas_call(
        body, name="prep_shard",
        grid_spec=pltpu.PrefetchScalarGridSpec(
            num_scalar_prefetch=1, grid=(rows // tr,),
            in_specs=[pl.BlockSpec((tr, cols), lambda i, me_ref: (i, 0))],
            out_specs=[pl.BlockSpec((tr, cols), lambda i, me_ref: (i, 0)),
                       pl.BlockSpec((1, tr, cols), lambda i, me_ref: (me_ref[0], i, 0))]),
        out_shape=[jax.ShapeDtypeStruct((rows, cols), BF16), jax.ShapeDtypeStruct((N_CHIPS, rows, cols), BF16)],
        compiler_params=_params("parallel"),
    )(me, w)


def _prep_tiny(tiny, me, slots=N_CHIPS):
    def body(me_ref, t_ref, l_ref):
        l_ref[0] = t_ref[...]

    return pl.pallas_call(
        body, name="prep_tiny",
        grid_spec=pltpu.PrefetchScalarGridSpec(
            num_scalar_prefetch=1, grid=(1,),
            in_specs=[pl.BlockSpec(tiny.shape, lambda i, me_ref: (0, 0))],
            out_specs=pl.BlockSpec((1,) + tiny.shape, lambda i, me_ref: (me_ref[0], 0, 0))),
        out_shape=jax.ShapeDtypeStruct((slots,) + tiny.shape, tiny.dtype),
    )(me, tiny)


N_DEV = 8


def _sibling_exchange(parts, token):
    def body(*refs):
        n = len(parts)
        srcs, outs, send_sems, recv_sems = refs[:n], refs[n + 1:2 * n + 1], refs[2 * n + 1], refs[2 * n + 2]
        x, y, c = _mesh_pos()
        sibling = (x, y, 1 - c)
        cps = []
        for w, (src, out) in enumerate(zip(srcs, outs)):
            hr = src.shape[1] // 2
            cp = _remote(src.at[:, pl.ds((1 - c) * hr, hr)], out, send_sems.at[w], recv_sems.at[w], sibling)
            cp.start()
            cps.append(cp)
        for cp in cps:
            cp.wait()

    n = len(parts)
    return pl.pallas_call(
        body, name="sibling_exchange",
        out_shape=[jax.ShapeDtypeStruct((p.shape[0], p.shape[1] // 2, p.shape[2]), p.dtype) for p in parts],
        in_specs=[HBM] * n + [pl.BlockSpec(memory_space=pl.ANY)], out_specs=[HBM] * n,
        scratch_shapes=[pltpu.SemaphoreType.DMA((n,)), pltpu.SemaphoreType.DMA((n,))],
    )(*parts, token)


def _chip_presum(part, from_sibling, pos):
    _, hr, cols = from_sibling.shape
    tr = _elementwise_tile(hr)
    steps = hr // tr

    def body(pos_ref, a_ref, b_ref, o_ref, land_ref):
        s = (a_ref[...] + b_ref[...]).astype(BF16)
        o_ref[...] = s

        @pl.when(pl.program_id(1) == pos_ref[1])
        def _():
            land_ref[...] = s

    return pl.pallas_call(
        body, name="chip_presum",
        grid_spec=pltpu.PrefetchScalarGridSpec(
            num_scalar_prefetch=1, grid=(steps, N_CHIPS),
            in_specs=[pl.BlockSpec((1, tr, cols), lambda i, j, p: (j, p[0] * steps + i, 0)),
                      pl.BlockSpec((1, tr, cols), lambda i, j, p: (j, i, 0))],
            out_specs=[pl.BlockSpec((1, tr, cols), lambda i, j, p: (j, i, 0)),
                       pl.BlockSpec((1, tr, cols), lambda i, j, p: (p[1], p[0] * steps + i, 0))]),
        out_shape=[jax.ShapeDtypeStruct(from_sibling.shape, BF16),
                   jax.ShapeDtypeStruct((N_CHIPS, 2 * hr, cols), BF16)],
        compiler_params=_params("arbitrary", "arbitrary"),
    )(pos, part, from_sibling)


def _scatter_partials(cparts, lands, done_cparts=(), done_lands=()):
    n_new = len(cparts)
    nw = n_new + len(done_cparts)

    def body(*refs):
        srcs = refs[:nw]
        outs = refs[2 * nw:3 * nw]
        own_send, own_recv, ici_send, ici_recv, d2d_send, d2d_recv = refs[3 * nw:]
        x, y, c = _mesh_pos()
        me = 2 * x + y
        chips = _other_chips(x, y)
        sibling = (x, y, 1 - c)
        sends = []
        for w in list(range(n_new, nw)) + list(range(n_new)):
            src, out = srcs[w], outs[w]
            hr = src.shape[1]
            mine = out.at[me, pl.ds(c * hr, hr)]
            cp = _remote(src.at[me], mine, own_send.at[w], own_recv.at[w], sibling)
            cp.start()
            sends.append(cp)
            for j, (px, py) in enumerate(chips):
                if w >= n_new:
                    break
                k = 3 * w + j
                cp = _remote(src.at[2 * px + py], mine, ici_send.at[k], ici_recv.at[k], (px, py, c))
                cp.start()
                sends.append(cp)
        for w in list(range(n_new, nw)) + list(range(n_new)):
            src, out = srcs[w], outs[w]
            hr = src.shape[1]
            for j, (px, py) in enumerate(chips):
                k = 3 * w + j
                landed = out.at[2 * px + py, pl.ds(c * hr, hr)]
                if w < n_new:
                    _remote(landed, landed, ici_send.at[k], ici_recv.at[k], sibling).wait_recv()
                cp = _remote(landed, landed, d2d_send.at[k], d2d_recv.at[k], sibling)
                cp.start()
                sends.append(cp)
        for w, (src, out) in enumerate(zip(srcs, outs)):
            hr = src.shape[1]
            other = out.at[me, pl.ds((1 - c) * hr, hr)]
            _remote(other, other, own_send.at[w], own_recv.at[w], sibling).wait_recv()
            for j, (px, py) in enumerate(chips):
                k = 3 * w + j
                other = out.at[2 * px + py, pl.ds((1 - c) * hr, hr)]
                _remote(other, other, d2d_send.at[k], d2d_recv.at[k], sibling).wait_recv()
        for cp in sends:
            cp.wait_send()

    n = 3 * nw
    dma = pltpu.SemaphoreType.DMA
    every = list(cparts) + list(done_cparts)
    every_lands = list(lands) + list(done_lands)
    return pl.pallas_call(
        body, name="scatter_partials",
        out_shape=[jax.ShapeDtypeStruct(l.shape, l.dtype) for l in every_lands],
        in_specs=[HBM] * (2 * nw), out_specs=[HBM] * nw,
        input_output_aliases={nw + i: i for i in range(nw)},
        scratch_shapes=[dma((nw,)), dma((nw,)), dma((n,)), dma((n,)), dma((n,)), dma((n,))],
    )(*every, *every_lands)


SEM = pl.BlockSpec(memory_space=pltpu.SEMAPHORE)
SPLIT_COPY = pltpu.CompilerParams(has_side_effects=pltpu.SideEffectType.DATAFLOW_SIDE_EFFECTING)


def _hbm(a):
    return pltpu.with_memory_space_constraint(a, pltpu.HBM)


def _gather_copies(srcs, lands, send_sems, recv_sems):
    x, y, c = _mesh_pos()
    me = 2 * x + y
    sends, recvs = [], []
    for w, (src, land) in enumerate(zip(srcs, lands)):
        hr = src.shape[0] // 2
        for j, (px, py) in enumerate(_other_chips(x, y)):
            k = 3 * w + j
            sends.append(_remote(src.at[pl.ds(c * hr, hr)], land.at[me, pl.ds(c * hr, hr)],
                                 send_sems.at[k], recv_sems.at[k], (px, py, c)))
            got = land.at[2 * px + py, pl.ds(c * hr, hr)]
            recvs.append(_remote(got, got, send_sems.at[k], recv_sems.at[k], (px, py, c)))
    return sends, recvs


def _scatter_copies(srcs, lands, send_sems, recv_sems):
    x, y, c = _mesh_pos()
    me = 2 * x + y
    sends, recvs = [], []
    for w, (src, land) in enumerate(zip(srcs, lands)):
        hr = src.shape[1]
        for j, (px, py) in enumerate(_other_chips(x, y)):
            k = 3 * w + j
            sends.append(_remote(src.at[2 * px + py], land.at[me, pl.ds(c * hr, hr)],
                                 send_sems.at[k], recv_sems.at[k], (px, py, c)))
            got = land.at[2 * px + py, pl.ds(c * hr, hr)]
            recvs.append(_remote(got, got, send_sems.at[k], recv_sems.at[k], (px, py, c)))
    return sends, recvs


def _sibling_copies(srcs, lands, send_sems, recv_sems):
    x, y, c = _mesh_pos()
    sibling = (x, y, 1 - c)
    sends, recvs = [], []
    for w, (src, land) in enumerate(zip(srcs, lands)):
        hr = src.shape[1] // 2
        sends.append(_remote(src.at[:, pl.ds((1 - c) * hr, hr)], land, send_sems.at[w], recv_sems.at[w], sibling))
        recvs.append(_remote(land, land, send_sems.at[w], recv_sems.at[w], sibling))
    return sends, recvs


def _inchip_copies(srcs, lands, send_sems, recv_sems):
    x, y, c = _mesh_pos()
    me = 2 * x + y
    sibling = (x, y, 1 - c)
    sends, recvs = [], []
    for w, (src, land) in enumerate(zip(srcs, lands)):
        hr = src.shape[1]
        mine, other = pl.ds(c * hr, hr), pl.ds((1 - c) * hr, hr)
        blocks = [(me, src.at[me])] + [(2 * px + py, None) for px, py in _other_chips(x, y)]
        for j, (blk, own_src) in enumerate(blocks):
            k = 4 * w + j
            landed = land.at[blk, mine]
            sends.append(_remote(landed if own_src is None else own_src, landed, send_sems.at[k], recv_sems.at[k], sibling))
            got = land.at[blk, other]
            recvs.append(_remote(got, got, send_sems.at[k], recv_sems.at[k], sibling))
    return sends, recvs


def _all_peers_copies(srcs, lands, send_sems, recv_sems):
    x, y, c = _mesh_pos()
    (src,), (land,) = srcs, lands
    flip = lambda v, bit: 1 - v if bit else v
    sends, recvs = [], []
    for k in range(N_DEV - 1):
        px, py, pc = flip(x, (k + 1) & 4), flip(y, (k + 1) & 2), flip(c, (k + 1) & 1)
        sends.append(_remote(src, land.at[4 * x + 2 * y + c], send_sems.at[k], recv_sems.at[k], (px, py, pc)))
        got = land.at[4 * px + 2 * py + pc]
        recvs.append(_remote(got, got, send_sems.at[k], recv_sems.at[k], (px, py, pc)))
    return sends, recvs


def _split_start(name, copies_of, srcs, land_shapes, n_copies=None):
    n = len(srcs)
    k = 3 * n if n_copies is None else n_copies

    def body(*refs):
        src_refs, land_refs = refs[:n], refs[n:2 * n]
        send_sems, recv_sems = refs[2 * n], refs[2 * n + 1]
        token = refs[-1]
        sends, _ = copies_of(src_refs, land_refs, send_sems, recv_sems)
        for cp in sends:
            cp.start()
        token[...] = jnp.zeros_like(token)

    lands = [_hbm(s) for s in land_shapes]
    dma = pltpu.SemaphoreType.DMA
    res = pl.pallas_call(
        body, name=name,
        out_shape=(dma((k,)), dma((k,)), *[pltpu.HBM(s.shape, s.dtype) for s in srcs],
                   *[pltpu.HBM(s.shape, s.dtype) for s in land_shapes], jax.ShapeDtypeStruct((8, 128), F32)),
        in_specs=[HBM] * (2 * n),
        out_specs=(SEM, SEM, *([HBM] * (2 * n)), pl.BlockSpec(memory_space=pltpu.VMEM)),
        input_output_aliases={i: 2 + i for i in range(2 * n)},
        compiler_params=SPLIT_COPY,
    )(*[_hbm(s) for s in srcs], *lands)
    return res[0], res[1], list(res[2:2 + n]), list(res[2 + n:2 + 2 * n]), res[-1]


def _split_wait(name, copies_of, send_sems, recv_sems, srcs, lands, after):
    n = len(srcs)

    def body(*refs):
        src_refs, land_refs = refs[:n], refs[n:2 * n]
        sends, recvs = copies_of(src_refs, land_refs, refs[2 * n], refs[2 * n + 1])
        for cp in sends:
            cp.wait_send()
        for cp in recvs:
            cp.wait_recv()

    res = pl.pallas_call(
        body, name=name,
        out_shape=tuple(pltpu.HBM(s.shape, s.dtype) for s in list(srcs) + list(lands)),
        in_specs=[HBM] * (2 * n) + [SEM, SEM] + [pl.BlockSpec(memory_space=pl.ANY)] * len(after),
        out_specs=tuple([HBM] * (2 * n)),
        input_output_aliases={i: i for i in range(2 * n)},
        compiler_params=SPLIT_COPY,
    )(*srcs, *lands, send_sems, recv_sems, *after)
    return list(res[:n]), list(res[n:])


def _gather_finish(lands):
    n = len(lands)

    def body(*refs):
        outs = refs[n:2 * n]
        d2d_send, d2d_recv = refs[2 * n:]
        x, y, c = _mesh_pos()
        chips = _other_chips(x, y)
        sibling = (x, y, 1 - c)
        sends = []
        for w, out in enumerate(outs):
            hr = out.shape[1] // 2
            for j, (px, py) in enumerate(chips):
                landed = out.at[2 * px + py, pl.ds(c * hr, hr)]
                cp = _remote(landed, landed, d2d_send.at[3 * w + j], d2d_recv.at[3 * w + j], sibling)
                cp.start()
                sends.append(cp)
        for w, out in enumerate(outs):
            hr = out.shape[1] // 2
            for j, (px, py) in enumerate(chips):
                other = out.at[2 * px + py, pl.ds((1 - c) * hr, hr)]
                _remote(other, other, d2d_send.at[3 * w + j], d2d_recv.at[3 * w + j], sibling).wait_recv()
        for cp in sends:
            cp.wait_send()

    dma = pltpu.SemaphoreType.DMA
    return pl.pallas_call(
        body, name="gather_finish",
        out_shape=[jax.ShapeDtypeStruct(l.shape, l.dtype) for l in lands],
        in_specs=[HBM] * n, out_specs=[HBM] * n,
        input_output_aliases={i: i for i in range(n)},
        scratch_shapes=[dma((3 * n,)), dma((3 * n,))],
    )(*lands)


def _adamw(w, g, m, v):
    m = ADAM_B1 * m + (1.0 - ADAM_B1) * g
    v = ADAM_B2 * v + (1.0 - ADAM_B2) * (g * g)
    m_hat = m / (1.0 - ADAM_B1 ** ADAM_STEP)
    v_hat = v / (1.0 - ADAM_B2 ** ADAM_STEP)
    delta = -ADAM_LR * (m_hat / (jnp.sqrt(v_hat) + ADAM_EPS) + ADAM_WD * w)
    return delta, m, v


def _adamw_big(partials, w, m, v):
    rows, cols = w.shape
    tr = _elementwise_tile(rows)

    def body(p_ref, w_ref, m_ref, v_ref, g_ref, d_ref, m2_ref, v2_ref):
        g = ((p_ref[0].astype(F32) + p_ref[1].astype(F32)) + p_ref[2].astype(F32)) + p_ref[3].astype(F32)
        g_ref[...] = g
        d_ref[...], m2_ref[...], v2_ref[...] = _adamw(w_ref[...], g, m_ref[...], v_ref[...])

    blk = pl.BlockSpec((tr, cols), lambda i: (i, 0))
    return pl.pallas_call(
        body, name="adamw_big", grid=(rows // tr,),
        in_specs=[pl.BlockSpec((N_CHIPS, tr, cols), lambda i: (0, i, 0)), blk, blk, blk],
        out_specs=[blk] * 4, out_shape=[jax.ShapeDtypeStruct((rows, cols), F32)] * 4,
        compiler_params=_params("parallel"),
    )(partials, w, m, v)


def _sum_devices(gathered, rows):
    cols = gathered.shape[1]

    def body(g_ref, o_ref):
        acc = g_ref[0:rows]
        for d in range(1, N_DEV):
            acc = acc + g_ref[d * rows:(d + 1) * rows]
        o_ref[...] = acc

    return pl.pallas_call(
        body, name="sum_devices", out_shape=jax.ShapeDtypeStruct((rows, cols), F32),
        in_specs=[pl.BlockSpec(memory_space=pltpu.VMEM)], out_specs=pl.BlockSpec(memory_space=pltpu.VMEM),
        compiler_params=pltpu.CompilerParams(vmem_limit_bytes=VMEM_LIMIT_V7X),
    )(gathered)


def _adamw_small(quads):
    n = len(quads)

    def body(*refs):
        ins, outs = refs[:4 * n], refs[4 * n:]
        for t in range(n):
            w, g, m, v = (r[...] for r in ins[4 * t:4 * t + 4])
            outs[3 * t][...], outs[3 * t + 1][...], outs[3 * t + 2][...] = _adamw(w, g, m, v)

    flat = [a for q in quads for a in q]
    vm = pl.BlockSpec(memory_space=pltpu.VMEM)
    res = pl.pallas_call(
        body, name="adamw_small",
        out_shape=[jax.ShapeDtypeStruct(q[0].shape, F32) for q in quads for _ in range(3)],
        in_specs=[vm] * (4 * n), out_specs=[vm] * (3 * n),
    )(*flat)
    return [tuple(res[3 * t:3 * t + 3]) for t in range(n)]


SMALL_PACK_ROWS = 96
META_COLS = D_MODEL // N_CHIPS
CONV_COLS = LRU_WIDTH // N_CHIPS
_WEIGHTS = ['meta_tokens', 'g_pre_mix', 'w_in', 'conv_w', 'conv_b', 'w_a', 'b_a', 'w_x', 'b_x', 'lru_lambda',
            'attn_sinks', 'w_out', 'g_post_mix', 'g_pre_ffn', 'w_ff1', 'w_ff2', 'g_post_ffn']
_BIG = ['w_in', 'w_out', 'w_ff1', 'w_ff2']


def _pack_small(dmeta, g, loss):
    z = lambda r, c: jnp.zeros((r, c), F32)
    rows = [
        dmeta,
        g['g_pre_mix'], g['g_post_mix'], g['g_pre_ffn'], g['g_post_ffn'],
        jnp.concatenate([g['conv_w'], z(4, 512)], axis=1),
        jnp.concatenate([g['conv_b'], g['b_a']], axis=1),
        jnp.concatenate([g['b_x'], g['lru_lambda']], axis=1),
        jnp.concatenate([g['attn_sinks'], z(1, D_MODEL - ATTN_HEADS)], axis=1),
        jnp.concatenate([loss, z(1, D_MODEL - 1)], axis=1),
        z(4, D_MODEL),
        g['w_a'].reshape(32, D_MODEL), g['w_x'].reshape(32, D_MODEL),
    ]
    return jnp.concatenate(rows, axis=0)


def _unpack_small(s, chip):
    return dict(
        meta_tokens=lax.dynamic_slice(s[0:N_META], (0, chip * META_COLS), (N_META, META_COLS)),
        g_pre_mix=s[16:17], g_post_mix=s[17:18], g_pre_ffn=s[18:19], g_post_ffn=s[19:20],
        conv_w=lax.dynamic_slice(s[20:24], (0, chip * CONV_COLS), (4, CONV_COLS)).reshape(1, 4, CONV_COLS),
        conv_b=s[24:25, :512], b_a=s[24:25, 512:], b_x=s[25:26, :512], lru_lambda=s[25:26, 512:],
        attn_sinks=s[26:27, :ATTN_HEADS], loss=s[27, 0],
        w_a=s[32:64].reshape(1, LRU_BLOCKS, LRU_BLOCK, LRU_BLOCK),
        w_x=s[64:96].reshape(1, LRU_BLOCKS, LRU_BLOCK, LRU_BLOCK))


def _as2d(a):
    if a.ndim == 2:
        return a
    return a.reshape(-1, a.shape[-1])


def kernel(x, meta_tokens, g_pre_mix, w_in, conv_w, conv_b, w_a, b_a, w_x, b_x, lru_lambda, attn_sinks, w_out, g_post_mix, g_pre_ffn, w_ff1, w_ff2, g_post_ffn, loss_target, m_meta_tokens, m_g_pre_mix, m_w_in, m_conv_w, m_conv_b, m_w_a, m_b_a, m_w_x, m_b_x, m_lru_lambda, m_attn_sinks, m_w_out, m_g_post_mix, m_g_pre_ffn, m_w_ff1, m_w_ff2, m_g_post_ffn, v_meta_tokens, v_g_pre_mix, v_w_in, v_conv_w, v_conv_b, v_w_a, v_b_a, v_w_x, v_b_x, v_lru_lambda, v_attn_sinks, v_w_out, v_g_post_mix, v_g_pre_ffn, v_w_ff1, v_w_ff2, v_g_post_ffn):
    weights = dict(meta_tokens=meta_tokens, g_pre_mix=g_pre_mix, w_in=w_in, conv_w=conv_w, conv_b=conv_b, w_a=w_a,
                   b_a=b_a, w_x=w_x, b_x=b_x, lru_lambda=lru_lambda, attn_sinks=attn_sinks, w_out=w_out,
                   g_post_mix=g_post_mix, g_pre_ffn=g_pre_ffn, w_ff1=w_ff1, w_ff2=w_ff2, g_post_ffn=g_post_ffn)
    mom1 = dict(zip(_WEIGHTS, [m_meta_tokens, m_g_pre_mix, m_w_in, m_conv_w, m_conv_b, m_w_a, m_b_a, m_w_x, m_b_x,
                               m_lru_lambda, m_attn_sinks, m_w_out, m_g_post_mix, m_g_pre_ffn, m_w_ff1, m_w_ff2,
                               m_g_post_ffn]))
    mom2 = dict(zip(_WEIGHTS, [v_meta_tokens, v_g_pre_mix, v_w_in, v_conv_w, v_conv_b, v_w_a, v_b_a, v_w_x, v_b_x,
                               v_lru_lambda, v_attn_sinks, v_w_out, v_g_post_mix, v_g_pre_ffn, v_w_ff1, v_w_ff2,
                               v_g_post_ffn]))
    xi, yi, ci = _mesh_pos()
    chip = 2 * xi + yi

    tiny = jnp.concatenate([meta_tokens, jnp.pad(conv_w[0], ((0, 4), (0, 128)))], axis=0)
    chip_arr = jnp.reshape(chip, (1,)).astype(jnp.int32)
    big2d = lambda a, name: a[0].T if name == 'w_in' else a[0]
    shards, lands = zip(*[_prep_shard(big2d(weights[n], n), chip_arr) for n in _BIG])
    g_in, g_tiny = _gather_weights(shards[:1], lands[:1], tiny, _prep_tiny(tiny, chip_arr))
    w_in_full = g_in.reshape(IN_WIDTH, D_MODEL)
    meta_full = jnp.concatenate([g_tiny[j, :N_META] for j in range(N_CHIPS)], axis=1)
    conv_w_full = jnp.concatenate([g_tiny[j, N_META:N_META + 4, :128] for j in range(N_CHIPS)], axis=1)
    g_send, g_recv, late_thru, late_lands, token = _split_start(
        "gather_late_start", _gather_copies, shards[1:], lands[1:])

    def late_weights(after):
        _, landed = _split_wait("gather_late_wait", _gather_copies, g_send, g_recv, late_thru, late_lands, after)
        g_out, g_f1, g_f2 = _gather_finish(landed)
        return g_out.reshape(D_MODEL, D_MODEL), g_f1, g_f2

    pos = jnp.stack([ci, chip]).astype(jnp.int32)
    ffn = {}


    def on_ffn_grads(dw1, dw2):
        parts = [dw1, dw2]
        lands = [lax.empty((p.shape[0], p.shape[1] // 2, p.shape[2]), p.dtype) for p in parts]
        ffn['sib'] = _split_start("sibling_ffn_start", _sibling_copies, parts, lands, len(parts))
        return ffn['sib'][4]

    def on_outproj_bwd(dattn):
        send, recv, thru, lands, _ = ffn['sib']
        parts, from_sibling = _split_wait("sibling_ffn_wait", _sibling_copies, send, recv, thru, lands, [dattn])
        cparts_ffn, lands_ffn = zip(*[_chip_presum(p, r, pos) for p, r in zip(parts, from_sibling)])
        ffn['send'], ffn['recv'], ffn['thru'], ffn['lands'], token3 = _split_start(
            "scatter_ffn_start", _scatter_copies, cparts_ffn, lands_ffn)
        return token3

    def on_mixer_grads(dw_in, dw_out):
        parts = [dw_in.reshape(N_CHIPS, IN_WIDTH // N_CHIPS, D_MODEL),
                 dw_out.reshape(N_CHIPS, D_MODEL // N_CHIPS, D_MODEL)]
        cparts, lands = zip(*[_chip_presum(p, r, pos) for p, r in zip(parts, _sibling_exchange(parts, pos))])
        ffn['mixer'] = _split_start("scatter_mixer_start", _scatter_copies, cparts, lands)
        ffn_cparts, ffn_lands = _split_wait("scatter_ffn_wait", _scatter_copies, ffn['send'], ffn['recv'],
                                            ffn['thru'], ffn['lands'], [ffn['mixer'][4]])
        ffn['inchip'] = _split_start("inchip_ffn_start", _inchip_copies, ffn_cparts, ffn_lands, 4 * len(ffn_cparts))
        return ffn['inchip'][4]

    head = jnp.concatenate([jnp.zeros((PAD_ROWS, D_MODEL), F32), meta_full], axis=0)
    loss, dx, dhead, grads = _local_step(head, x[0], loss_target[0], g_pre_mix, w_in_full, conv_w_full, conv_b, w_a[0],
                                         b_a, w_x[0], b_x, lru_lambda, attn_sinks, g_post_mix, g_pre_ffn, g_post_ffn,
                                         late_weights, on_ffn_grads, on_outproj_bwd, on_mixer_grads, token)
    grad_x = dx[None]

    pack = _pack_small(dhead[PAD_ROWS:], grads, loss)
    dev = jnp.reshape(4 * xi + 2 * yi + ci, (1,)).astype(jnp.int32)
    s_send, s_recv, s_thru, s_lands, token5 = _split_start(
        "gather_small_start", _all_peers_copies, [pack], [_prep_tiny(pack, dev, N_DEV)], N_DEV - 1)

    send, recv, thru, lands, _ = ffn['mixer']
    mixer_cparts, mixer_lands = _split_wait("scatter_mixer_wait", _scatter_copies, send, recv, thru, lands, [token5])
    mixer_partials = _scatter_partials([], [], mixer_cparts, mixer_lands)
    send, recv, thru, lands, _ = ffn['inchip']
    _, ffn_partials = _split_wait("inchip_ffn_wait", _inchip_copies, send, recv, thru, lands, mixer_partials)
    chip_partials = list(mixer_partials) + ffn_partials

    g_out_d, delta, new_m, new_v = {}, {}, {}, {}
    for name, part in zip(_BIG, chip_partials):
        shp = weights[name].shape
        res = _adamw_big(part, big2d(weights[name], name), big2d(mom1[name], name), big2d(mom2[name], name))
        g_out_d[name], delta[name], new_m[name], new_v[name] = (big2d(r[None], name).reshape(shp) for r in res)

    _, (gathered,) = _split_wait("gather_small_wait", _all_peers_copies, s_send, s_recv, s_thru, s_lands,
                                 [g_out_d[n] for n in _BIG])
    small = _unpack_small(_sum_devices(gathered.reshape(N_DEV * SMALL_PACK_ROWS, D_MODEL), SMALL_PACK_ROWS), chip)
    loss = small['loss']
    small_names = [n for n in _WEIGHTS if n not in _BIG]
    quads = [(_as2d(weights[n]), _as2d(small[n]), _as2d(mom1[n]), _as2d(mom2[n])) for n in small_names]
    for name, (d, m2, v2) in zip(small_names, _adamw_small(quads)):
        shp = weights[name].shape
        g_out_d[name] = small[name].reshape(shp)
        delta[name], new_m[name], new_v[name] = d.reshape(shp), m2.reshape(shp), v2.reshape(shp)

    return (loss, grad_x, *[g_out_d[n] for n in _WEIGHTS], *[delta[n] for n in _WEIGHTS],
            *[new_m[n] for n in _WEIGHTS], *[new_v[n] for n in _WEIGHTS])
```

```python
import numpy as np
import jax
import jax.numpy as jnp
from jax import lax
from jax.experimental import pallas as pl
from jax.experimental.pallas import tpu as pltpu

F32 = jnp.float32
BF16 = jnp.bfloat16

D_MODEL = 1024
N_META = 16
BLOCK = 128
PAD_ROWS = BLOCK - N_META
HEAD_DIM = 64
ATTN_HEADS = 8
GQA_GROUP = 4
ATTN_WIDTH = 512
KV_WIDTH = 128
QKV_WIDTH = ATTN_WIDTH + 2 * KV_WIDTH
LRU_WIDTH = 512
LRU_BLOCKS = 8
LRU_BLOCK = 64
LRU_C = 8.0
IN_WIDTH = 1792
D_FF = 4096
N_CHIPS = 4
FF_CHUNK = D_FF // N_CHIPS
EPS = 1e-6
NEG = -1e30

ADAM_LR = 0.001
ADAM_B1 = 0.9
ADAM_B2 = 0.999
ADAM_EPS = 1e-08
ADAM_WD = 0.01
ADAM_STEP = 10

VMEM_LIMIT_V7X = 62 * 1024 * 1024
MESH = pl.DeviceIdType.MESH

NT = (((1,), (1,)), ((), ()))
TN = (((0,), (0,)), ((), ()))


def _row_tile(tp):
    return 640 if tp % 640 == 0 else BLOCK


def _elementwise_tile(rows):
    return 512 if rows % 512 == 0 else rows


def _wgrad_row_tile(tp):
    return 1664 if tp % 1664 == 0 else _row_tile(tp)


def _params(*sem):
    return pltpu.CompilerParams(dimension_semantics=sem, vmem_limit_bytes=VMEM_LIMIT_V7X)


def _dot(a, b):
    return jnp.dot(a, b, preferred_element_type=F32)


def _dot_nt(a, b):
    return lax.dot_general(a, b, NT, preferred_element_type=F32)


def _dot_tn(a, b):
    return lax.dot_general(a, b, TN, preferred_element_type=F32)


def _rms(x):
    rs = lax.rsqrt(jnp.mean(x * x, axis=-1, keepdims=True) + EPS)
    return x * rs, rs


def _rms_bwd(xhat, rs, g, dy):
    dyg = dy * g
    dx = rs * (dyg - xhat * jnp.mean(dyg * xhat, axis=-1, keepdims=True))
    dg = jnp.sum(dy * xhat, axis=0, keepdims=True)
    return dx, dg


def _gelu(x):
    k = 0.7978845608028654
    t = jnp.tanh(x * (k + (k * 0.044715) * (x * x)))
    return (0.5 * x) * (1.0 + t), t


def _gelu_grad(x, t):
    k = 0.7978845608028654
    return 0.5 * (1.0 + t) + 0.5 * x * (1.0 - t * t) * k * (1.0 + 3 * 0.044715 * x * x)


def _sigmoid(x):
    return 0.5 * jnp.tanh(0.5 * x) + 0.5


def _one_minus_exp2(y):
    t = jnp.tanh(y)
    return (-2.0 * t) / (1.0 - t)


def _softplus(x):
    return jnp.maximum(x, 0.0) + jnp.log1p(jnp.exp(-jnp.abs(x)))


def _seq_specs(tr, delay=0):
    qb = tr // BLOCK
    tile = lambda i: jnp.maximum(i - delay, 0)
    return [pl.BlockSpec((BLOCK, D_MODEL), lambda i, *_, s=s: (jnp.maximum(tile(i) * qb + s - 1, 0), 0))
            for s in range(qb)]


def _seq_tile(head, pieces, i):
    first = jnp.where(i == 0, head, pieces[0][...])
    return jnp.concatenate([first] + [p[...] for p in pieces[1:]], axis=0)


GROUP_ROWS = GQA_GROUP * BLOCK


def _attn_bias():
    j = np.arange(2 * BLOCK)[:, None]
    i = np.arange(BLOCK)[None, :]
    band = (j - i >= 1) & (j - i <= BLOCK)
    out = []
    for n in range(3):
        ok = band & ((n - 1) * BLOCK + j >= PAD_ROWS) if n < 2 else band
        out.append(np.tile(np.where(ok, 0.0, NEG).astype(np.float32), (1, GQA_GROUP)))
    return jnp.asarray(np.stack(out))


def _heads_t(at, g):
    heads = range(GQA_GROUP * g, GQA_GROUP * (g + 1))
    return jnp.concatenate([at[h * HEAD_DIM:(h + 1) * HEAD_DIM] for h in heads], axis=1).astype(BF16)


def _from_heads_t(groups):
    pairs = []
    for p in groups:
        for h in range(0, GQA_GROUP, 2):
            two = jnp.concatenate([p[:, h * BLOCK:(h + 1) * BLOCK], p[:, (h + 1) * BLOCK:(h + 2) * BLOCK]], axis=0)
            pairs.append(two.T)
    return jnp.concatenate(pairs, axis=1)


def _stack_heads(a, g):
    heads = range(GQA_GROUP * g, GQA_GROUP * (g + 1))
    return jnp.concatenate([a[:, h * HEAD_DIM:(h + 1) * HEAD_DIM] for h in heads], axis=0)


def _unstack_heads(groups):
    return jnp.concatenate([p[h * BLOCK:(h + 1) * BLOCK] for p in groups for h in range(GQA_GROUP)], axis=1)


def _attn_probs_t(k_g, qg, bias, sink_row):
    st = _dot_nt(k_g, qg) + bias
    m = jnp.maximum(jnp.max(st, axis=0, keepdims=True), sink_row)
    p = jnp.exp(st - m)
    es = jnp.exp(sink_row - m)
    inv = 1.0 / (jnp.sum(p, axis=0, keepdims=True) + es)
    return p * inv, es * inv


def _attn_consts(sinks):
    return jnp.repeat(sinks.reshape(ATTN_HEADS), BLOCK).reshape(ATTN_HEADS // GQA_GROUP, GROUP_ROWS), _attn_bias()


_SINK_SPEC = pl.BlockSpec((ATTN_HEADS // GQA_GROUP, GROUP_ROWS), lambda n: (0, 0))
_BIAS_SPEC = pl.BlockSpec((3, 2 * BLOCK, GROUP_ROWS), lambda n: (0, 0, 0))
_QSCALE = HEAD_DIM ** -0.5


def _kv_specs(tr):
    qb = tr // BLOCK
    prev = lambda col: pl.BlockSpec((BLOCK, KV_WIDTH), lambda t: (jnp.maximum(t * qb - 1, 0), col))
    cur = lambda col: pl.BlockSpec((tr, KV_WIDTH), lambda t: (t, col))
    return [prev(4), cur(4), prev(5), cur(5)]


def _block_bias(b_ref, t, qb, i):
    return b_ref[2] if i >= 2 else b_ref[jnp.minimum(t * qb + i, 2)]


N_KV = ATTN_HEADS // GQA_GROUP


def _prob_specs(qb):
    return [pl.BlockSpec((qb, N_KV, 2 * BLOCK, GROUP_ROWS), lambda t: (t, 0, 0, 0)),
            pl.BlockSpec((qb, SUBLANES, GROUP_ROWS), lambda t: (t, 0, 0))]


def _attn_fwd(qkv, sinks):
    tp = qkv.shape[0]
    tr = _row_tile(tp)
    qb, nb = tr // BLOCK, tp // BLOCK
    sink_rows, bias = _attn_consts(sinks)

    def body(s_ref, b_ref, q_ref, kp_ref, kc_ref, vp_ref, vc_ref, o_ref, p_ref, ps_ref):
        t = pl.program_id(0)
        k_all = jnp.concatenate([kp_ref[...], kc_ref[...]], axis=0)
        v_all = jnp.concatenate([vp_ref[...], vc_ref[...]], axis=0)
        for i in range(qb):
            rows = slice(i * BLOCK, (i + 1) * BLOCK)
            q = q_ref[rows]
            k2, v2 = k_all[i * BLOCK:(i + 2) * BLOCK], v_all[i * BLOCK:(i + 2) * BLOCK]
            bias_n = _block_bias(b_ref, t, qb, i)
            outs, sink_probs = [], []
            for g in range(N_KV):
                cols = slice(g * HEAD_DIM, (g + 1) * HEAD_DIM)
                qg = _stack_heads(q, g) * jnp.asarray(_QSCALE, BF16)
                p, ps = _attn_probs_t(k2[:, cols], qg, bias_n, s_ref[g:g + 1])
                pb = p.astype(BF16)
                p_ref[i, g] = pb
                sink_probs.append(ps)
                outs.append(_dot_tn(pb, v2[:, cols]))
            o_ref[rows] = _unstack_heads(outs).astype(BF16)
            ps_ref[i] = jnp.concatenate(sink_probs + [jnp.zeros((SUBLANES - N_KV, GROUP_ROWS), F32)], axis=0)

    return pl.pallas_call(
        body, name="attn_fwd", grid=(tp // tr,),
        in_specs=[_SINK_SPEC, _BIAS_SPEC, pl.BlockSpec((tr, ATTN_WIDTH), lambda t: (t, 0))] + _kv_specs(tr),
        out_specs=[pl.BlockSpec((tr, ATTN_WIDTH), lambda t: (t, 0))] + _prob_specs(qb),
        out_shape=[jax.ShapeDtypeStruct((tp, ATTN_WIDTH), BF16),
                   jax.ShapeDtypeStruct((nb, N_KV, 2 * BLOCK, GROUP_ROWS), BF16),
                   jax.ShapeDtypeStruct((nb, SUBLANES, GROUP_ROWS), F32)],
        compiler_params=_params("parallel"),
    )(sink_rows, bias, qkv, qkv, qkv, qkv, qkv)


def _conv_taps(x, halo):
    ext = jnp.concatenate([halo, x], axis=0)
    return [ext[8:] if k == 3 else pltpu.roll(ext, 3 - k, 0)[8:] for k in range(4)]


def _lru_gates(xc, wa, ba, wx, bx, sp):
    xb = xc.astype(BF16)
    r = _sigmoid(_dot(xb, wa) + ba)
    ig = _sigmoid(_dot(xb, wx) + bx)
    log_a = (-LRU_C * sp) * r
    a = jnp.exp(log_a)
    mult = jnp.sqrt(_one_minus_exp2(log_a))
    return xb, r, ig, a, mult


SUBLANES = 8


def _scan_fwd(a, b, h_in):
    n, width = a.shape
    a, b = (v.reshape(n // SUBLANES, SUBLANES, width) for v in (a, b))
    in_group = lax.broadcasted_iota(jnp.int32, a.shape, 1)
    for d in (1, 2, 4):
        keep = in_group >= d
        b = jnp.where(keep, a * pltpu.roll(b, d, 1) + b, b)
        a = jnp.where(keep, a * pltpu.roll(a, d, 1), a)
    a, b = a.reshape(n, width), b.reshape(n, width)
    out, carry = [], h_in
    for g in range(0, n, SUBLANES):
        h = a[g:g + SUBLANES] * carry + b[g:g + SUBLANES]
        out.append(h)
        carry = h[SUBLANES - 1:]
    return jnp.concatenate(out, axis=0)


def _scan_rev(c, b, g_in):
    n, width = c.shape
    c, b = (v.reshape(n // SUBLANES, SUBLANES, width) for v in (c, b))
    in_group = lax.broadcasted_iota(jnp.int32, c.shape, 1)
    for d in (1, 2, 4):
        keep = in_group < SUBLANES - d
        b = jnp.where(keep, b + c * pltpu.roll(b, SUBLANES - d, 1), b)
        c = jnp.where(keep, c * pltpu.roll(c, SUBLANES - d, 1), c)
    c, b = c.reshape(n, width), b.reshape(n, width)
    out, carry = [], g_in
    for g in range(n - SUBLANES, -1, -SUBLANES):
        r = b[g:g + SUBLANES] + c[g:g + SUBLANES] * carry
        out.append(r)
        carry = r[:1]
    return jnp.concatenate(out[::-1], axis=0)


def _inproj_lru_fwd(head, x, g, w_in, conv_w, conv_b, wa, ba, wx, bx, lam, token):
    tp = BLOCK + x.shape[0]
    tr = _row_tile(tp)
    qb, nt = tr // BLOCK, tp // tr
    small = [conv_w, conv_b, wa, ba, wx, bx, lam]

    def body(*refs):
        head_ref, pieces = refs[0], refs[1:1 + qb]
        g_ref, w_ref, _, cw_ref, cb_ref, wa_ref, ba_ref, wx_ref, bx_ref, lam_ref = refs[1 + qb:11 + qb]
        u_ref, qkv_ref, xr_ref, yr_ref, hr_ref, rec_ref, zbuf, halo, hprev = refs[11 + qb:]
        i = pl.program_id(0)
        cur = i % 2

        @pl.when(i == 0)
        def _():
            halo[...] = jnp.zeros_like(halo)
            hprev[...] = jnp.zeros_like(hprev)
            zbuf[1] = jnp.zeros((tr, 2 * LRU_WIDTH), F32)

        def recurrent_branch(valid):
            cw, cb = cw_ref[...], cb_ref[...]
            wa_m, ba_v, wx_m, bx_v = wa_ref[...], ba_ref[...], wx_ref[...], bx_ref[...]
            sp = _softplus(-lam_ref[...])
            before, h_last = halo[...], hprev[0:1]
            for b in range(qb):
                rows = slice(b * BLOCK, (b + 1) * BLOCK)
                xy = zbuf[1 - cur, rows]
                xin = xy[:, :LRU_WIDTH]
                taps = _conv_taps(xin, before)
                before = xin[BLOCK - 8:]
                xc = cb + sum(cw[k:k + 1] * taps[k] for k in range(4))
                _, _, ig, a, mult = _lru_gates(xc, wa_m, ba_v, wx_m, bx_v, sp)
                u = mult * (ig * xc)
                if b == 0:
                    pos = (i - 1) * tr + lax.broadcasted_iota(jnp.int32, xc.shape, 0)
                    u = jnp.where(pos >= PAD_ROWS, u, 0.0)
                h = _scan_fwd(a, u, h_last)
                h_last = h[BLOCK - 1:]
                hr_ref[rows] = h
                gl, _ = _gelu(xy[:, LRU_WIDTH:])
                rec_ref[rows] = (gl * h).astype(BF16)
            halo[...] = jnp.where(valid, before, 0.0)
            hprev[0:1] = jnp.where(valid, h_last, 0.0)

        def projection():
            xhat, _ = _rms(_seq_tile(head_ref[...], pieces, i))
            u = (xhat * g_ref[...]).astype(BF16)
            u_ref[...] = u
            z = _dot_nt(u, w_ref[...])
            qkv_ref[...] = z[:, :QKV_WIDTH].astype(BF16)
            xr_ref[...] = z[:, QKV_WIDTH:QKV_WIDTH + LRU_WIDTH]
            yr_ref[...] = z[:, QKV_WIDTH + LRU_WIDTH:]
            zbuf[cur] = z[:, QKV_WIDTH:]

        @pl.when(i < nt)
        def _():
            recurrent_branch(i >= 1)
            projection()

        @pl.when(i == nt)
        def _():
            recurrent_branch(True)

    last = nt - 1
    this_row = lambda w: pl.BlockSpec((tr, w), lambda i: (jnp.minimum(i, last), 0))
    prev_row = lambda w: pl.BlockSpec((tr, w), lambda i: (jnp.maximum(i - 1, 0), 0))
    full = lambda a: pl.BlockSpec(a.shape, lambda i: (0,) * a.ndim)
    piece_specs = [pl.BlockSpec((BLOCK, D_MODEL), lambda i, s=s: (jnp.maximum(jnp.minimum(i, last) * qb + s - 1, 0), 0))
                   for s in range(qb)]
    return pl.pallas_call(
        body, name="inproj_lru_fwd", grid=(nt + 1,),
        in_specs=[full(head)] + piece_specs + [full(g), full(w_in), full(token)] + [full(a) for a in small],
        out_specs=[this_row(D_MODEL), this_row(QKV_WIDTH), this_row(LRU_WIDTH), this_row(LRU_WIDTH),
                   prev_row(LRU_WIDTH), prev_row(LRU_WIDTH)],
        out_shape=[jax.ShapeDtypeStruct((tp, D_MODEL), BF16), jax.ShapeDtypeStruct((tp, QKV_WIDTH), BF16),
                   jax.ShapeDtypeStruct((tp, LRU_WIDTH), F32), jax.ShapeDtypeStruct((tp, LRU_WIDTH), F32),
                   jax.ShapeDtypeStruct((tp, LRU_WIDTH), F32), jax.ShapeDtypeStruct((tp, LRU_WIDTH), BF16)],
        scratch_shapes=[pltpu.VMEM((2, tr, 2 * LRU_WIDTH), F32), pltpu.VMEM((8, LRU_WIDTH), F32),
                        pltpu.VMEM((8, LRU_WIDTH), F32)],
        compiler_params=_params("arbitrary"),
    )(head, *([x] * qb), g, w_in, token, *small)


def _outproj_fwd(attn, rec, w_out, head, x, g_post_mix, g_pre_ffn):
    tp = attn.shape[0]
    tr = _row_tile(tp)
    qb = tr // BLOCK

    def body(*refs):
        a_ref, r_ref, w_ref, head_ref = refs[:4]
        pieces = refs[4:4 + qb]
        gm_ref, gf_ref, mix_ref, h1_ref, u1_ref = refs[4 + qb:]
        mix = _dot(a_ref[...], w_ref[:ATTN_WIDTH]) + _dot(r_ref[...], w_ref[ATTN_WIDTH:])
        mix_ref[...] = mix
        mhat, _ = _rms(mix)
        h1 = _seq_tile(head_ref[...], pieces, pl.program_id(0)) + mhat * gm_ref[...]
        h1_ref[...] = h1
        hhat, _ = _rms(h1)
        u1_ref[...] = (hhat * gf_ref[...]).astype(BF16)

    row = lambda w: pl.BlockSpec((tr, w), lambda i: (i, 0))
    full = lambda a: pl.BlockSpec(a.shape, lambda i: (0,) * a.ndim)
    return pl.pallas_call(
        body, name="outproj_fwd", grid=(tp // tr,),
        in_specs=[row(ATTN_WIDTH), row(LRU_WIDTH), full(w_out), full(head)] + _seq_specs(tr)
        + [full(g_post_mix), full(g_pre_ffn)],
        out_specs=[row(D_MODEL), row(D_MODEL), row(D_MODEL)],
        out_shape=[jax.ShapeDtypeStruct((tp, D_MODEL), F32), jax.ShapeDtypeStruct((tp, D_MODEL), F32),
                   jax.ShapeDtypeStruct((tp, D_MODEL), BF16)],
        compiler_params=_params("parallel"),
    )(attn, rec, w_out, head, *([x] * qb), g_post_mix, g_pre_ffn)


FFN_STEPS = N_CHIPS


def _resident(a):
    return pl.BlockSpec(a.shape, lambda *_: (0,) * a.ndim, pipeline_mode=pl.Buffered(1))


def _ffn_fwd(u1, w1, w2, h1, tgt, g_post_ffn):
    tp = h1.shape[0]
    tr = _row_tile(tp)
    qb, nt = tr // BLOCK, tp // tr
    sr = tr // FFN_STEPS

    def body(*refs):
        u_ref, w1_ref, w2_ref, h1_ref = refs[:4]
        t_pieces = refs[4:4 + qb]
        g_ref, r1_ref, dy_ref, df2_ref, loss_ref, dg_ref, acc = refs[4 + qb:]
        i, c = pl.program_id(0), pl.program_id(1)
        cur = i % 2

        @pl.when((i == 0) & (c == 0))
        def _():
            loss_ref[...] = jnp.zeros_like(loss_ref)
            dg_ref[...] = jnp.zeros_like(dg_ref)
            acc[1] = jnp.zeros((tr, D_MODEL), F32)

        def matmuls():
            r = jnp.maximum(_dot(u_ref[...], w1_ref[c]), 0.0)
            r1_ref[...] = r.astype(BF16)
            return _dot((r * r).astype(BF16), w2_ref[c])

        def finish_previous_tile(k, valid):
            lo, hi = k * sr, (k + 1) * sr
            g = g_ref[...]
            fhat, rs = _rms(acc[1 - cur, lo:hi])
            h2 = h1_ref[...] + fhat * g
            rows = (i - 1) * tr + lo + lax.broadcasted_iota(jnp.int32, h2.shape, 0)
            tgt = jnp.concatenate([p[max(lo - s * BLOCK, 0):min(hi - s * BLOCK, BLOCK)] for s, p in enumerate(t_pieces)
                                   if lo < (s + 1) * BLOCK and hi > s * BLOCK], axis=0)
            err = jnp.where((rows >= BLOCK) & valid, h2 - tgt, 0.0)
            dy = err * (1.0 / D_MODEL)
            dy_ref[...] = dy
            loss_ref[...] += (0.5 / D_MODEL) * jnp.sum(err * err)
            df2, dg = _rms_bwd(fhat, rs, g, dy)
            df2_ref[...] = df2.astype(BF16)
            dg_ref[...] += dg

        for k in range(FFN_STEPS):
            @pl.when((c == k) & (i < nt))
            def _(k=k):
                finish_previous_tile(k, i >= 1)
                if k == 0:
                    acc[cur] = matmuls()
                else:
                    acc[cur] += matmuls()

            @pl.when((c == k) & (i == nt))
            def _(k=k):
                finish_previous_tile(k, True)

    last = nt - 1
    this_row = pl.BlockSpec((tr, D_MODEL), lambda i, c: (jnp.minimum(i, last), 0))
    prev_quarter = pl.BlockSpec((sr, D_MODEL), lambda i, c: (jnp.maximum(i - 1, 0) * FFN_STEPS + c, 0))
    prev_quarter_out = pl.BlockSpec(
        (sr, D_MODEL), lambda i, c: (jnp.where(i == 0, nt * FFN_STEPS, (i - 1) * FFN_STEPS + c), 0))
    full = lambda a: pl.BlockSpec(a.shape, lambda i, c: (0,) * a.ndim)
    return pl.pallas_call(
        body, name="ffn_fwd", grid=(nt + 1, FFN_STEPS),
        in_specs=[this_row, _resident(w1), _resident(w2), prev_quarter] + _seq_specs(tr, delay=1) + [full(g_post_ffn)],
        out_specs=[pl.BlockSpec((tr, FF_CHUNK), lambda i, c: (jnp.minimum(i, last), jnp.where(i < nt, c, FFN_STEPS - 1))),
                   prev_quarter_out, prev_quarter_out,
                   pl.BlockSpec((1, 1), lambda i, c: (0, 0)), pl.BlockSpec((1, D_MODEL), lambda i, c: (0, 0))],
        out_shape=[jax.ShapeDtypeStruct((tp, D_FF), BF16), jax.ShapeDtypeStruct((tp + sr, D_MODEL), F32),
                   jax.ShapeDtypeStruct((tp + sr, D_MODEL), BF16), jax.ShapeDtypeStruct((1, 1), F32),
                   jax.ShapeDtypeStruct((1, D_MODEL), F32)],
        scratch_shapes=[pltpu.VMEM((2, tr, D_MODEL), F32)],
        compiler_params=_params("arbitrary", "arbitrary"),
    )(u1, w1, w2, h1, *([tgt] * qb), g_post_ffn)


def _ffn_bwd_data(df2, r1, w1, w2, dy, h1, mix, g_pre_ffn, g_post_mix):
    tp = h1.shape[0]
    tr = _row_tile(tp)
    nt = tp // tr
    sr = tr // FFN_STEPS

    def body(df2_ref, r1_ref, w1_ref, w2_ref, dy_ref, h1_ref, mix_ref, gf_ref, gm_ref,
             da_ref, dh1_ref, dmix_ref, dgf_ref, dgm_ref, acc):
        i, c = pl.program_id(0), pl.program_id(1)
        cur = i % 2

        @pl.when((i == 0) & (c == 0))
        def _():
            dgf_ref[...] = jnp.zeros_like(dgf_ref)
            dgm_ref[...] = jnp.zeros_like(dgm_ref)
            acc[1] = jnp.zeros((tr, D_MODEL), F32)

        def matmuls():
            df = _dot_nt(df2_ref[...], w2_ref[c])
            da = (df * (2.0 * r1_ref[...].astype(F32))).astype(BF16)
            da_ref[...] = da
            return _dot_nt(da, w1_ref[c])

        def finish_previous_tile(k, valid):
            lo, hi = k * sr, (k + 1) * sr
            hhat, rs = _rms(h1_ref[...])
            dx, dgf = _rms_bwd(hhat, rs, gf_ref[...], acc[1 - cur, lo:hi])
            dh1 = dy_ref[...] + dx
            dh1_ref[...] = dh1
            mhat, rsm = _rms(mix_ref[...])
            dmix, dgm = _rms_bwd(mhat, rsm, gm_ref[...], dh1)
            dmix_ref[...] = dmix.astype(BF16)
            dgf_ref[...] += jnp.where(valid, dgf, 0.0)
            dgm_ref[...] += jnp.where(valid, dgm, 0.0)

        for k in range(FFN_STEPS):
            @pl.when((c == k) & (i < nt))
            def _(k=k):
                finish_previous_tile(k, i >= 1)
                if k == 0:
                    acc[cur] = matmuls()
                else:
                    acc[cur] += matmuls()

            @pl.when((c == k) & (i == nt))
            def _(k=k):
                finish_previous_tile(k, True)

    last = nt - 1
    this_row = pl.BlockSpec((tr, D_MODEL), lambda i, c: (jnp.minimum(i, last), 0))
    prev_quarter = pl.BlockSpec((sr, D_MODEL), lambda i, c: (jnp.maximum(i - 1, 0) * FFN_STEPS + c, 0))
    prev_quarter_out = pl.BlockSpec(
        (sr, D_MODEL), lambda i, c: (jnp.where(i == 0, nt * FFN_STEPS, (i - 1) * FFN_STEPS + c), 0))
    chunk = pl.BlockSpec((tr, FF_CHUNK), lambda i, c: (jnp.minimum(i, last), jnp.where(i < nt, c, FFN_STEPS - 1)))
    gain = pl.BlockSpec((1, D_MODEL), lambda i, c: (0, 0))
    return pl.pallas_call(
        body, name="ffn_bwd_data", grid=(nt + 1, FFN_STEPS),
        in_specs=[this_row, chunk, _resident(w1), _resident(w2), prev_quarter, prev_quarter, prev_quarter, gain, gain],
        out_specs=[chunk, prev_quarter_out, prev_quarter_out, gain, gain],
        out_shape=[jax.ShapeDtypeStruct((tp, D_FF), BF16), jax.ShapeDtypeStruct((tp + sr, D_MODEL), F32),
                   jax.ShapeDtypeStruct((tp + sr, D_MODEL), BF16), jax.ShapeDtypeStruct((1, D_MODEL), F32),
                   jax.ShapeDtypeStruct((1, D_MODEL), F32)],
        scratch_shapes=[pltpu.VMEM((2, tr, D_MODEL), F32)],
        compiler_params=_params("arbitrary", "arbitrary"),
    )(df2, r1, w1, w2, dy, h1, mix, g_pre_ffn, g_post_mix)


def _ffn_bwd_weights(u1, da1, r1, df2):
    tp = u1.shape[0]
    tr = _wgrad_row_tile(tp)

    def body(u_ref, da_ref, r1_ref, df2_ref, dw1_ref, dw2_ref):
        i = pl.program_id(1)

        def products():
            r = r1_ref[...].astype(F32)
            return _dot_tn(u_ref[...], da_ref[...]), _dot_tn((r * r).astype(BF16), df2_ref[...])

        @pl.when(i == 0)
        def _():
            dw1_ref[0], dw2_ref[0] = products()

        @pl.when(i > 0)
        def _():
            p1, p2 = products()
            dw1_ref[0] += p1
            dw2_ref[0] += p2

    row = pl.BlockSpec((tr, D_MODEL), lambda c, i: (i, 0))
    chunk = pl.BlockSpec((tr, FF_CHUNK), lambda c, i: (i, c))
    return pl.pallas_call(
        body, name="ffn_bwd_weights", grid=(N_CHIPS, tp // tr),
        in_specs=[row, chunk, chunk, row],
        out_specs=[pl.BlockSpec((1, D_MODEL, FF_CHUNK), lambda c, i: (c, 0, 0)),
                   pl.BlockSpec((1, FF_CHUNK, D_MODEL), lambda c, i: (c, 0, 0))],
        out_shape=[jax.ShapeDtypeStruct((N_CHIPS, D_MODEL, FF_CHUNK), F32),
                   jax.ShapeDtypeStruct((N_CHIPS, FF_CHUNK, D_MODEL), F32)],
        compiler_params=_params("parallel", "arbitrary"),
    )(u1, da1, r1, df2)


N_VEC_ROWS = 8


def _outproj_lru_bwd(dmix, w_out, attn, rec, xr, yr, hr, conv_w, conv_b, wa, ba, wx, bx, lam, token):
    tp = xr.shape[0]
    tr = _row_tile(tp)
    qb, nt = tr // BLOCK, tp // tr

    def body(dm_ref, w_ref, at_ref, rc_ref, xr_ref, xh_ref, yr_ref, hr_ref, hp_ref,
             cw_ref, cb_ref, wa_ref, ba_ref, wx_ref, bx_ref, lam_ref, _,
             dxr_ref, dyr_ref, dat_ref, dwo_ref, dwa_ref, dwx_ref, vec_ref, g_next, a_next, dxc_next, dsp):
        s = pl.program_id(0)
        t = nt - 1 - s

        @pl.when(s == 0)
        def _():
            g_next[...] = jnp.zeros_like(g_next)
            a_next[...] = jnp.zeros_like(a_next)
            dxc_next[...] = jnp.zeros_like(dxc_next)
            dsp[...] = jnp.zeros_like(dsp)
            dwo_ref[...] = jnp.zeros_like(dwo_ref)
            dwa_ref[...] = jnp.zeros_like(dwa_ref)
            dwx_ref[...] = jnp.zeros_like(dwx_ref)
            vec_ref[...] = jnp.zeros_like(vec_ref)

        dm = dm_ref[...]
        dcat = _dot_nt(dm, w_ref[...])
        dat_ref[...] = dcat[:, :ATTN_WIDTH].astype(BF16)
        drec_tile = dcat[:, ATTN_WIDTH:]
        dwo_ref[:ATTN_WIDTH] += _dot_tn(at_ref[...], dm)
        dwo_ref[ATTN_WIDTH:] += _dot_tn(rc_ref[...], dm)

        first_tile = t == 0
        cw, cb = cw_ref[...], cb_ref[...]
        lam_v = lam_ref[...]
        sp = _softplus(-lam_v)
        wa_m, ba_v, wx_m, bx_v = wa_ref[...], ba_ref[...], wx_ref[...], bx_ref[...]
        rows = lax.broadcasted_iota(jnp.int32, (BLOCK, LRU_WIDTH), 0)
        col = lambda v: jnp.sum(v, axis=0, keepdims=True)

        g_after, a_after, dxc_after = g_next[0:1], a_next[0:1], dxc_next[...]
        xbs, dgrs, dgis = [], [], []
        vec = [jnp.zeros((1, LRU_WIDTH), F32) for _ in range(N_VEC_ROWS)]
        for i in reversed(range(qb)):
            blk = slice(i * BLOCK, (i + 1) * BLOCK)
            if i == 0:
                x_before = jnp.where(first_tile, 0.0, xh_ref[...])
                h_before = jnp.where(first_tile, 0.0, hp_ref[7:8])
            else:
                x_before = xr_ref[i * BLOCK - 8:i * BLOCK]
                h_before = hr_ref[i * BLOCK - 1:i * BLOCK]
            taps = _conv_taps(xr_ref[blk], x_before)
            xc = cb + sum(cw[k:k + 1] * taps[k] for k in range(4))
            xb, r, ig, a, mult = _lru_gates(xc, wa_m, ba_v, wx_m, bx_v, sp)

            yr_v = yr_ref[blk]
            gl, th = _gelu(yr_v)
            h = hr_ref[blk]
            drec = drec_tile[blk]
            dyr_ref[blk] = (drec * h * _gelu_grad(yr_v, th)).astype(BF16)

            a_up = jnp.where(rows == BLOCK - 1, a_after, pltpu.roll(a, BLOCK - 1, 0))
            g = _scan_rev(a_up, drec * gl, g_after)
            g_after, a_after = g[0:1], a[0:1]

            h_prev = jnp.where(rows == 0, h_before, pltpu.roll(h, 1, 0))
            du, da = g, g * h_prev
            if i == 0:
                real = (t * tr + rows) >= PAD_ROWS
                du, da = jnp.where(real, du, 0.0), jnp.where(real, da, 0.0)
            dmult = du * (ig * xc)
            dig = du * (mult * xc)
            dxc = du * (mult * ig)
            dlog_a = da * a - dmult * (a * a / mult)
            if i == 0:
                dlog_a = jnp.where(real, dlog_a, 0.0)
            dgr = (dlog_a * (-LRU_C * sp)) * (r * (1.0 - r))
            dgi = dig * (ig * (1.0 - ig))
            dgr_b, dgi_b = dgr.astype(BF16), dgi.astype(BF16)
            dxc = dxc + _dot_nt(dgr_b, wa_m) + _dot_nt(dgi_b, wx_m)
            xbs.append(xb)
            dgrs.append(dgr_b)
            dgis.append(dgi_b)

            ext = jnp.concatenate([dxc, dxc_after], axis=0)
            up = [ext[:BLOCK] if j == 0 else pltpu.roll(ext, BLOCK + 8 - j, 0)[:BLOCK] for j in range(4)]
            dxr_ref[blk] = sum(cw[k:k + 1] * up[3 - k] for k in range(4)).astype(BF16)
            dxc_after = dxc[:8]

            for k in range(4):
                vec[k] = vec[k] + col(dxc * taps[k])
            vec[4] = vec[4] + col(dxc)
            vec[5] = vec[5] + col(dgr)
            vec[6] = vec[6] + col(dgi)
            vec[7] = vec[7] + col(dlog_a * (-LRU_C * r))

        g_next[0:1], a_next[0:1], dxc_next[...] = g_after, a_after, dxc_after
        xb_all = jnp.concatenate(xbs, axis=0)
        dwa_ref[...] += _dot_tn(xb_all, jnp.concatenate(dgrs, axis=0))
        dwx_ref[...] += _dot_tn(xb_all, jnp.concatenate(dgis, axis=0))
        for k in range(7):
            vec_ref[k:k + 1] += vec[k]
        dsp[0:1] += vec[7]

        @pl.when(s == nt - 1)
        def _():
            vec_ref[7:8] = dsp[0:1] * (-_sigmoid(-lam_v))

    blk_spec = pl.BlockSpec((tr, LRU_WIDTH), lambda s: (nt - 1 - s, 0))
    rows_before = pl.BlockSpec((8, LRU_WIDTH), lambda s: (jnp.maximum((nt - 1 - s) * (tr // 8) - 1, 0), 0))
    full = lambda a: pl.BlockSpec(a.shape, lambda s: (0,) * a.ndim)
    small = [conv_w, conv_b, wa, ba, wx, bx, lam, token]
    sq = pl.BlockSpec((LRU_WIDTH, LRU_WIDTH), lambda s: (0, 0))
    wide = pl.BlockSpec((tr, D_MODEL), lambda s: (nt - 1 - s, 0))
    whole = pl.BlockSpec((D_MODEL, D_MODEL), lambda s: (0, 0))
    return pl.pallas_call(
        body, name="outproj_lru_bwd", grid=(nt,),
        in_specs=[wide, whole, blk_spec, blk_spec, blk_spec, rows_before, blk_spec, blk_spec, rows_before]
        + [full(a) for a in small],
        out_specs=[blk_spec, blk_spec, blk_spec, whole, sq, sq, pl.BlockSpec((N_VEC_ROWS, LRU_WIDTH), lambda s: (0, 0))],
        out_shape=[jax.ShapeDtypeStruct((tp, LRU_WIDTH), BF16), jax.ShapeDtypeStruct((tp, LRU_WIDTH), BF16),
                   jax.ShapeDtypeStruct((tp, ATTN_WIDTH), BF16), jax.ShapeDtypeStruct((D_MODEL, D_MODEL), F32),
                   jax.ShapeDtypeStruct((LRU_WIDTH, LRU_WIDTH), F32), jax.ShapeDtypeStruct((LRU_WIDTH, LRU_WIDTH), F32),
                   jax.ShapeDtypeStruct((N_VEC_ROWS, LRU_WIDTH), F32)],
        scratch_shapes=[pltpu.VMEM((8, LRU_WIDTH), F32)] * 4,
        compiler_params=_params("arbitrary"),
    )(dmix, w_out, attn, rec, xr, xr, yr, hr, hr, *small)


def _attn_bwd_tile(tp):
    return _wgrad_row_tile(tp)


def _attn_bwd(qkv, dattn, probs, sink_probs, token):
    tp = qkv.shape[0]
    tr = _attn_bwd_tile(tp)
    qb, nt = tr // BLOCK, tp // tr
    n_groups = N_KV

    def body(p_ref, ps_ref, q_ref, kp_ref, kc_ref, vp_ref, vc_ref, do_ref, _, dq_ref, dkv_ref, ex_ref, ds_ref, dsink):
        t = pl.program_id(0)

        @pl.when(t == 0)
        def _():
            dsink[...] = jnp.zeros_like(dsink)

        k_all = jnp.concatenate([kp_ref[...], kc_ref[...]], axis=0)
        v_all = jnp.concatenate([vp_ref[...], vc_ref[...]], axis=0)
        tail = None
        for i in range(qb):
            rows = slice(i * BLOCK, (i + 1) * BLOCK)
            qt = (q_ref[rows].astype(F32) * _QSCALE).T
            dot = do_ref[rows].astype(F32).T
            k2, v2 = k_all[i * BLOCK:(i + 2) * BLOCK], v_all[i * BLOCK:(i + 2) * BLOCK]
            dqs, dks, dvs = [], [], []
            for g in range(n_groups):
                cols = slice(g * HEAD_DIM, (g + 1) * HEAD_DIM)
                k_g, v_g = k2[:, cols], v2[:, cols]
                qgt, dogt = _heads_t(qt, g), _heads_t(dot, g)
                pb = p_ref[i, g]
                p = pb.astype(F32)
                dpt = _dot(v_g, dogt)
                delta = jnp.sum(p * dpt, axis=0, keepdims=True)
                dst = (p * (dpt - delta)).astype(BF16)
                dqs.append(_dot_tn(k_g, dst) * _QSCALE)
                dks.append(_dot_nt(qgt, dst))
                dvs.append(_dot_nt(dogt, pb))
                dsink[g:g + 1] -= ps_ref[i, g:g + 1] * delta
            dq_ref[rows] = _from_heads_t(dqs).astype(BF16)
            dkv = jnp.concatenate([jnp.concatenate(dks, axis=0).T, jnp.concatenate(dvs, axis=0).T], axis=1)
            if i == 0:
                ex_ref[0] = dkv[:BLOCK]
            else:
                dkv_ref[(i - 1) * BLOCK:i * BLOCK] = (tail + dkv[:BLOCK]).astype(BF16)
            tail = dkv[BLOCK:]
        dkv_ref[(qb - 1) * BLOCK:] = tail.astype(BF16)

        @pl.when(t == nt - 1)
        def _():
            lane = lax.broadcasted_iota(jnp.int32, (1, ATTN_HEADS), 1)
            acc = jnp.zeros((1, ATTN_HEADS), F32)
            for h in range(ATTN_HEADS):
                g, hh = divmod(h, GQA_GROUP)
                acc = acc + jnp.where(lane == h, jnp.sum(dsink[g:g + 1, hh * BLOCK:(hh + 1) * BLOCK]), 0.0)
            ds_ref[...] = acc

    cur = lambda w: pl.BlockSpec((tr, w), lambda t: (t, 0))
    return pl.pallas_call(
        body, name="attn_bwd", grid=(nt,),
        in_specs=_prob_specs(qb) + [cur(ATTN_WIDTH)] + _kv_specs(tr)
        + [cur(ATTN_WIDTH), pl.BlockSpec(token.shape, lambda t: (0, 0))],
        out_specs=[cur(ATTN_WIDTH), cur(2 * KV_WIDTH), pl.BlockSpec((1, BLOCK, 2 * KV_WIDTH), lambda t: (t, 0, 0)),
                   pl.BlockSpec((1, ATTN_HEADS), lambda t: (0, 0))],
        out_shape=[jax.ShapeDtypeStruct((tp, ATTN_WIDTH), BF16), jax.ShapeDtypeStruct((tp, 2 * KV_WIDTH), BF16),
                   jax.ShapeDtypeStruct((nt, BLOCK, 2 * KV_WIDTH), F32), jax.ShapeDtypeStruct((1, ATTN_HEADS), F32)],
        scratch_shapes=[pltpu.VMEM((n_groups, GROUP_ROWS), F32)],
        compiler_params=_params("arbitrary"),
    )(probs, sink_probs, qkv, qkv, qkv, qkv, qkv, dattn, token)


def _fix_dkv(dkv, dkv_extra):
    tp = dkv.shape[0]
    tr = _attn_bwd_tile(tp)
    nt, qb = tp // tr, tr // BLOCK
    if nt == 1:
        return dkv

    def body(d_ref, ex_ref, o_ref):
        o_ref[...] = (d_ref[...].astype(F32) + ex_ref[0]).astype(BF16)

    last = pl.BlockSpec((BLOCK, 2 * KV_WIDTH), lambda t: (t * qb + qb - 1, 0))
    return pl.pallas_call(
        body, name="fix_dkv", grid=(nt - 1,),
        in_specs=[last, pl.BlockSpec((1, BLOCK, 2 * KV_WIDTH), lambda t: (t + 1, 0, 0))],
        out_specs=last, out_shape=jax.ShapeDtypeStruct(dkv.shape, dkv.dtype),
        input_output_aliases={0: 0}, compiler_params=_params("parallel"),
    )(dkv, dkv_extra)


def _inproj_wgrad(dq, dkv, dxr, dyr, u0):
    tp = dq.shape[0]
    tr = _wgrad_row_tile(tp)

    def body(dq_ref, dkv_ref, dxr_ref, dyr_ref, u_ref, dw_ref):
        i = pl.program_id(0)

        def product():
            dz = jnp.concatenate([dq_ref[...], dkv_ref[...], dxr_ref[...], dyr_ref[...]], axis=1)
            return _dot_tn(dz, u_ref[...])

        @pl.when(i == 0)
        def _():
            dw_ref[...] = product()

        @pl.when(i > 0)
        def _():
            dw_ref[...] += product()

    row = lambda w: pl.BlockSpec((tr, w), lambda i: (i, 0))
    return pl.pallas_call(
        body, name="inproj_wgrad", grid=(tp // tr,),
        in_specs=[row(ATTN_WIDTH), row(2 * KV_WIDTH), row(LRU_WIDTH), row(LRU_WIDTH), row(D_MODEL)],
        out_specs=pl.BlockSpec((IN_WIDTH, D_MODEL), lambda i: (0, 0)),
        out_shape=jax.ShapeDtypeStruct((IN_WIDTH, D_MODEL), F32),
        compiler_params=_params("arbitrary"),
    )(dq, dkv, dxr, dyr, u0)


def _inproj_dgrad(dq, dkv, dxr, dyr, w_in, head, x, dh1, g, token):
    tp = dq.shape[0]
    tr = _row_tile(tp)
    nt, qb = tp // tr, tr // BLOCK

    def body(*refs):
        dq_ref, dkv_ref, dxr_ref, dyr_ref, w_ref, head_ref = refs[:6]
        pieces = refs[6:6 + qb]
        dh1_ref, g_ref, _, gx_ref, dhead_ref, dg_ref, buf, sems = refs[6 + qb:]
        i = pl.program_id(0)
        slot = i % 2

        def out_copy(step, at):
            return pltpu.make_async_copy(buf.at[at], gx_ref.at[pl.ds(step * tr - BLOCK, tr)], sems.at[at])

        dz = jnp.concatenate([dq_ref[...], dkv_ref[...], dxr_ref[...], dyr_ref[...]], axis=1)
        du = _dot(dz, w_ref[...])
        hhat, rs = _rms(_seq_tile(head_ref[...], pieces, i))
        dx, dg = _rms_bwd(hhat, rs, g_ref[...], du)
        dh0 = dh1_ref[...] + dx

        @pl.when(i >= 3)
        def _():
            out_copy(i - 2, slot).wait()

        buf[slot] = dh0

        @pl.when(i == 0)
        def _():
            dg_ref[...] = dg
            dhead_ref[...] = dh0[:BLOCK]
            if tr > BLOCK:
                first = pltpu.make_async_copy(buf.at[0, pl.ds(BLOCK, tr - BLOCK)], gx_ref.at[pl.ds(0, tr - BLOCK)],
                                              sems.at[0])
                first.start()
                first.wait()

        @pl.when(i >= 1)
        def _():
            dg_ref[...] += dg
            out_copy(i, slot).start()

        @pl.when(i == nt - 1)
        def _():
            if nt >= 3:
                out_copy(nt - 2, (nt - 2) % 2).wait()
            if nt >= 2:
                out_copy(nt - 1, (nt - 1) % 2).wait()

    row = lambda w: pl.BlockSpec((tr, w), lambda i: (i, 0))
    full = lambda shape: pl.BlockSpec(shape, lambda i: (0,) * len(shape))
    return pl.pallas_call(
        body, name="inproj_dgrad", grid=(tp // tr,),
        in_specs=[row(ATTN_WIDTH), row(2 * KV_WIDTH), row(LRU_WIDTH), row(LRU_WIDTH), full(w_in.shape),
                  full(head.shape)] + _seq_specs(tr) + [row(D_MODEL), full(g.shape), full(token.shape)],
        out_specs=[pl.BlockSpec(memory_space=pl.ANY), full((BLOCK, D_MODEL)), full((1, D_MODEL))],
        out_shape=[jax.ShapeDtypeStruct(x.shape, F32), jax.ShapeDtypeStruct((BLOCK, D_MODEL), F32),
                   jax.ShapeDtypeStruct((1, D_MODEL), F32)],
        scratch_shapes=[pltpu.VMEM((2, tr, D_MODEL), F32), pltpu.SemaphoreType.DMA((2,))],
        compiler_params=_params("arbitrary"),
    )(dq, dkv, dxr, dyr, w_in, head, *([x] * qb), dh1, g, token)


def _dense_block_diag(w):
    eye = jnp.eye(LRU_BLOCKS, dtype=w.dtype)
    return (w[:, :, None, :] * eye[:, None, :, None]).reshape(LRU_WIDTH, LRU_WIDTH)


def _diag_blocks(dense):
    d4 = dense.reshape(LRU_BLOCKS, LRU_BLOCK, LRU_BLOCKS, LRU_BLOCK)
    return jnp.stack([d4[n, :, n, :] for n in range(LRU_BLOCKS)])


def _local_step(head, x, tgt, g_pre_mix, w_in, conv_w, conv_b, w_a, b_a, w_x, b_x, lam, sinks, g_post_mix,
                g_pre_ffn, g_post_ffn, late_weights, on_ffn_grads, on_outproj_bwd, on_mixer_grads, token):
    wa = _dense_block_diag(w_a).astype(BF16)
    wx = _dense_block_diag(w_x).astype(BF16)

    u0, qkv, xr, yr, hr, rec = _inproj_lru_fwd(head, x, g_pre_mix, w_in, conv_w, conv_b, wa, b_a, wx, b_x, lam, token)
    attn, probs, sink_probs = _attn_fwd(qkv, sinks)
    w_out, ffn_weights = late_weights([attn, rec])
    mix, h1, u1 = _outproj_fwd(attn, rec, w_out, head, x, g_post_mix, g_pre_ffn)
    w1, w2 = ffn_weights([u1])
    r1, dy, df2, loss, dg_post_ffn = _ffn_fwd(u1, w1, w2, h1, tgt, g_post_ffn)

    da1, dh1, dmix, dg_pre_ffn, dg_post_mix = _ffn_bwd_data(df2, r1, w1, w2, dy, h1, mix, g_pre_ffn, g_post_mix)
    dw1, dw2 = _ffn_bwd_weights(u1, da1, r1, df2)
    token2 = on_ffn_grads(dw1, dw2)
    dxr, dyr, dattn, dw_out, dwa, dwx, vec = _outproj_lru_bwd(dmix, w_out, attn, rec, xr, yr, hr, conv_w, conv_b,
                                                              wa, b_a, wx, b_x, lam, token2)
    token3 = on_outproj_bwd(dattn)
    dq, dkv, dkv_extra, dsinks = _attn_bwd(qkv, dattn, probs, sink_probs, token3)
    dkv = _fix_dkv(dkv, dkv_extra)
    dw_in = _inproj_wgrad(dq, dkv, dxr, dyr, u0)
    token4 = on_mixer_grads(dw_in, dw_out)
    dx, dhead, dg_pre_mix = _inproj_dgrad(dq, dkv, dxr, dyr, w_in, head, x, dh1, g_pre_mix, token4)

    grads = dict(
        g_pre_mix=dg_pre_mix, conv_w=vec[0:4], conv_b=vec[4:5], w_a=_diag_blocks(dwa), b_a=vec[5:6],
        w_x=_diag_blocks(dwx), b_x=vec[6:7], lru_lambda=vec[7:8], attn_sinks=dsinks,
        g_post_mix=dg_post_mix, g_pre_ffn=dg_pre_ffn, g_post_ffn=dg_post_ffn)
    return loss, dx, dhead, grads


HBM = pl.BlockSpec(memory_space=pltpu.HBM)


def _mesh_pos():
    return lax.axis_index("x"), lax.axis_index("y"), lax.axis_index("c")


def _other_chips(x, y):
    return [(1 - x, y), (x, 1 - y), (1 - x, 1 - y)]


def _remote(src, dst, send_sem, recv_sem, to):
    return pltpu.make_async_remote_copy(src_ref=src, dst_ref=dst, send_sem=send_sem, recv_sem=recv_sem,
                                        device_id=to, device_id_type=MESH)


def _gather_weights(shards, lands, tiny, tiny_land):
    nbig = len(shards)

    def body(*refs):
        srcs, tiny_src = refs[:nbig], refs[nbig]
        outs, tiny_out = refs[2 * nbig + 2:3 * nbig + 2], refs[3 * nbig + 2]
        ici_send, ici_recv, d2d_send, d2d_recv, tiny_send, tiny_recv = refs[3 * nbig + 3:]
        x, y, c = _mesh_pos()
        me = 2 * x + y
        chips = _other_chips(x, y)
        sibling = (x, y, 1 - c)
        sends = []
        for w, (src, out) in enumerate(zip(srcs, outs)):
            hr = src.shape[0] // 2
            for j, chip in enumerate(chips):
                k = 3 * w + j
                cp = _remote(src.at[pl.ds(c * hr, hr)], out.at[me, pl.ds(c * hr, hr)],
                             ici_send.at[k], ici_recv.at[k], (*chip, c))
                cp.start()
                sends.append(cp)
        for j, chip in enumerate(chips):
            cp = _remote(tiny_src, tiny_out.at[me], tiny_send.at[j], tiny_recv.at[j], (*chip, c))
            cp.start()
            sends.append(cp)
        for w, (src, out) in enumerate(zip(srcs, outs)):
            hr = src.shape[0] // 2
            for j, (px, py) in enumerate(chips):
                k = 3 * w + j
                landed = out.at[2 * px + py, pl.ds(c * hr, hr)]
                _remote(landed, landed, ici_send.at[k], ici_recv.at[k], sibling).wait_recv()
                cp = _remote(landed, landed, d2d_send.at[k], d2d_recv.at[k], sibling)
                cp.start()
                sends.append(cp)
        for w, (src, out) in enumerate(zip(srcs, outs)):
            hr = src.shape[0] // 2
            for j, (px, py) in enumerate(chips):
                k = 3 * w + j
                other = out.at[2 * px + py, pl.ds((1 - c) * hr, hr)]
                _remote(other, other, d2d_send.at[k], d2d_recv.at[k], sibling).wait_recv()
        for j, (px, py) in enumerate(chips):
            blk = tiny_out.at[2 * px + py]
            _remote(blk, blk, tiny_send.at[j], tiny_recv.at[j], sibling).wait_recv()
        for cp in sends:
            cp.wait_send()

    out_shape = [jax.ShapeDtypeStruct(l.shape, l.dtype) for l in list(lands) + [tiny_land]]
    n = 3 * nbig
    return pl.pallas_call(
        body, name="gather_weights", out_shape=out_shape,
        in_specs=[HBM] * (2 * nbig + 2), out_specs=[HBM] * (nbig + 1),
        input_output_aliases={nbig + 1 + i: i for i in range(nbig + 1)},
        scratch_shapes=[pltpu.SemaphoreType.DMA((n,)),
                        pltpu.SemaphoreType.DMA((n,)), pltpu.SemaphoreType.DMA((n,)), pltpu.SemaphoreType.DMA((n,)),
                        pltpu.SemaphoreType.DMA((3,)), pltpu.SemaphoreType.DMA((3,))],
    )(*shards, tiny, *lands, tiny_land)


def _prep_shard(w, me):
    rows, cols = w.shape
    tr = _elementwise_tile(rows)

    def body(me_ref, w_ref, s_ref, l_ref):
        b = w_ref[...].astype(BF16)
        s_ref[...] = b
        l_ref[0] = b

    return pl.pallas_call(
        body, name="prep_shard",
        grid_spec=pltpu.PrefetchScalarGridSpec(
            num_scalar_prefetch=1, grid=(rows // tr,),
            in_specs=[pl.BlockSpec((tr, cols), lambda i, me_ref: (i, 0))],
            out_specs=[pl.BlockSpec((tr, cols), lambda i, me_ref: (i, 0)),
                       pl.BlockSpec((1, tr, cols), lambda i, me_ref: (me_ref[0], i, 0))]),
        out_shape=[jax.ShapeDtypeStruct((rows, cols), BF16), jax.ShapeDtypeStruct((N_CHIPS, rows, cols), BF16)],
        compiler_params=_params("parallel"),
    )(me, w)


def _prep_tiny(tiny, me, slots=N_CHIPS):
    def body(me_ref, t_ref, l_ref):
        l_ref[0] = t_ref[...]

    return pl.pallas_call(
        body, name="prep_tiny",
        grid_spec=pltpu.PrefetchScalarGridSpec(
            num_scalar_prefetch=1, grid=(1,),
            in_specs=[pl.BlockSpec(tiny.shape, lambda i, me_ref: (0, 0))],
            out_specs=pl.BlockSpec((1,) + tiny.shape, lambda i, me_ref: (me_ref[0], 0, 0))),
        out_shape=jax.ShapeDtypeStruct((slots,) + tiny.shape, tiny.dtype),
    )(me, tiny)


N_DEV = 8


def _sibling_exchange(parts, token):
    def body(*refs):
        n = len(parts)
        srcs, outs, send_sems, recv_sems = refs[:n], refs[n + 1:2 * n + 1], refs[2 * n + 1], refs[2 * n + 2]
        x, y, c = _mesh_pos()
        sibling = (x, y, 1 - c)
        cps = []
        for w, (src, out) in enumerate(zip(srcs, outs)):
            hr = src.shape[1] // 2
            cp = _remote(src.at[:, pl.ds((1 - c) * hr, hr)], out, send_sems.at[w], recv_sems.at[w], sibling)
            cp.start()
            cps.append(cp)
        for cp in cps:
            cp.wait()

    n = len(parts)
    return pl.pallas_call(
        body, name="sibling_exchange",
        out_shape=[jax.ShapeDtypeStruct((p.shape[0], p.shape[1] // 2, p.shape[2]), p.dtype) for p in parts],
        in_specs=[HBM] * n + [pl.BlockSpec(memory_space=pl.ANY)], out_specs=[HBM] * n,
        scratch_shapes=[pltpu.SemaphoreType.DMA((n,)), pltpu.SemaphoreType.DMA((n,))],
    )(*parts, token)


def _chip_presum(part, from_sibling, pos):
    _, hr, cols = from_sibling.shape
    tr = _elementwise_tile(hr)
    steps = hr // tr

    def body(pos_ref, a_ref, b_ref, o_ref, land_ref):
        s = (a_ref[...] + b_ref[...]).astype(BF16)
        o_ref[...] = s

        @pl.when(pl.program_id(1) == pos_ref[1])
        def _():
            land_ref[...] = s

    return pl.pallas_call(
        body, name="chip_presum",
        grid_spec=pltpu.PrefetchScalarGridSpec(
            num_scalar_prefetch=1, grid=(steps, N_CHIPS),
            in_specs=[pl.BlockSpec((1, tr, cols), lambda i, j, p: (j, p[0] * steps + i, 0)),
                      pl.BlockSpec((1, tr, cols), lambda i, j, p: (j, i, 0))],
            out_specs=[pl.BlockSpec((1, tr, cols), lambda i, j, p: (j, i, 0)),
                       pl.BlockSpec((1, tr, cols), lambda i, j, p: (p[1], p[0] * steps + i, 0))]),
        out_shape=[jax.ShapeDtypeStruct(from_sibling.shape, BF16),
                   jax.ShapeDtypeStruct((N_CHIPS, 2 * hr, cols), BF16)],
        compiler_params=_params("arbitrary", "arbitrary"),
    )(pos, part, from_sibling)


def _scatter_partials(cparts, lands, done_cparts=(), done_lands=()):
    n_new = len(cparts)
    nw = n_new + len(done_cparts)

    def body(*refs):
        srcs = refs[:nw]
        outs = refs[2 * nw:3 * nw]
        own_send, own_recv, ici_send, ici_recv, d2d_send, d2d_recv = refs[3 * nw:]
        x, y, c = _mesh_pos()
        me = 2 * x + y
        chips = _other_chips(x, y)
        sibling = (x, y, 1 - c)
        sends = []
        for w in list(range(n_new, nw)) + list(range(n_new)):
            src, out = srcs[w], outs[w]
            hr = src.shape[1]
            mine = out.at[me, pl.ds(c * hr, hr)]
            cp = _remote(src.at[me], mine, own_send.at[w], own_recv.at[w], sibling)
            cp.start()
            sends.append(cp)
            for j, (px, py) in enumerate(chips):
                if w >= n_new:
                    break
                k = 3 * w + j
                cp = _remote(src.at[2 * px + py], mine, ici_send.at[k], ici_recv.at[k], (px, py, c))
                cp.start()
                sends.append(cp)
        for w in list(range(n_new, nw)) + list(range(n_new)):
            src, out = srcs[w], outs[w]
            hr = src.shape[1]
            for j, (px, py) in enumerate(chips):
                k = 3 * w + j
                landed = out.at[2 * px + py, pl.ds(c * hr, hr)]
                if w < n_new:
                    _remote(landed, landed, ici_send.at[k], ici_recv.at[k], sibling).wait_recv()
                cp = _remote(landed, landed, d2d_send.at[k], d2d_recv.at[k], sibling)
                cp.start()
                sends.append(cp)
        for w, (src, out) in enumerate(zip(srcs, outs)):
            hr = src.shape[1]
            other = out.at[me, pl.ds((1 - c) * hr, hr)]
            _remote(other, other, own_send.at[w], own_recv.at[w], sibling).wait_recv()
            for j, (px, py) in enumerate(chips):
                k = 3 * w + j
                other = out.at[2 * px + py, pl.ds((1 - c) * hr, hr)]
                _remote(other, other, d2d_send.at[k], d2d_recv.at[k], sibling).wait_recv()
        for cp in sends:
            cp.wait_send()

    n = 3 * nw
    dma = pltpu.SemaphoreType.DMA
    every = list(cparts) + list(done_cparts)
    every_lands = list(lands) + list(done_lands)
    return pl.pallas_call(
        body, name="scatter_partials",
        out_shape=[jax.ShapeDtypeStruct(l.shape, l.dtype) for l in every_lands],
        in_specs=[HBM] * (2 * nw), out_specs=[HBM] * nw,
        input_output_aliases={nw + i: i for i in range(nw)},
        scratch_shapes=[dma((nw,)), dma((nw,)), dma((n,)), dma((n,)), dma((n,)), dma((n,))],
    )(*every, *every_lands)


SEM = pl.BlockSpec(memory_space=pltpu.SEMAPHORE)
SPLIT_COPY = pltpu.CompilerParams(has_side_effects=pltpu.SideEffectType.DATAFLOW_SIDE_EFFECTING)


def _hbm(a):
    return pltpu.with_memory_space_constraint(a, pltpu.HBM)


def _gather_copies(srcs, lands, send_sems, recv_sems):
    x, y, c = _mesh_pos()
    me = 2 * x + y
    sends, recvs = [], []
    for w, (src, land) in enumerate(zip(srcs, lands)):
        hr = src.shape[0] // 2
        for j, (px, py) in enumerate(_other_chips(x, y)):
            k = 3 * w + j
            sends.append(_remote(src.at[pl.ds(c * hr, hr)], land.at[me, pl.ds(c * hr, hr)],
                                 send_sems.at[k], recv_sems.at[k], (px, py, c)))
            got = land.at[2 * px + py, pl.ds(c * hr, hr)]
            recvs.append(_remote(got, got, send_sems.at[k], recv_sems.at[k], (px, py, c)))
    return sends, recvs


def _scatter_copies(srcs, lands, send_sems, recv_sems):
    x, y, c = _mesh_pos()
    me = 2 * x + y
    sends, recvs = [], []
    for w, (src, land) in enumerate(zip(srcs, lands)):
        hr = src.shape[1]
        for j, (px, py) in enumerate(_other_chips(x, y)):
            k = 3 * w + j
            sends.append(_remote(src.at[2 * px + py], land.at[me, pl.ds(c * hr, hr)],
                                 send_sems.at[k], recv_sems.at[k], (px, py, c)))
            got = land.at[2 * px + py, pl.ds(c * hr, hr)]
            recvs.append(_remote(got, got, send_sems.at[k], recv_sems.at[k], (px, py, c)))
    return sends, recvs


def _sibling_copies(srcs, lands, send_sems, recv_sems):
    x, y, c = _mesh_pos()
    sibling = (x, y, 1 - c)
    sends, recvs = [], []
    for w, (src, land) in enumerate(zip(srcs, lands)):
        hr = src.shape[1] // 2
        sends.append(_remote(src.at[:, pl.ds((1 - c) * hr, hr)], land, send_sems.at[w], recv_sems.at[w], sibling))
        recvs.append(_remote(land, land, send_sems.at[w], recv_sems.at[w], sibling))
    return sends, recvs


def _inchip_copies(srcs, lands, send_sems, recv_sems):
    x, y, c = _mesh_pos()
    me = 2 * x + y
    sibling = (x, y, 1 - c)
    sends, recvs = [], []
    for w, (src, land) in enumerate(zip(srcs, lands)):
        hr = src.shape[1]
        mine, other = pl.ds(c * hr, hr), pl.ds((1 - c) * hr, hr)
        blocks = [(me, src.at[me])] + [(2 * px + py, None) for px, py in _other_chips(x, y)]
        for j, (blk, own_src) in enumerate(blocks):
            k = 4 * w + j
            landed = land.at[blk, mine]
            sends.append(_remote(landed if own_src is None else own_src, landed, send_sems.at[k], recv_sems.at[k], sibling))
            got = land.at[blk, other]
            recvs.append(_remote(got, got, send_sems.at[k], recv_sems.at[k], sibling))
    return sends, recvs


def _all_peers_copies(srcs, lands, send_sems, recv_sems):
    x, y, c = _mesh_pos()
    (src,), (land,) = srcs, lands
    flip = lambda v, bit: 1 - v if bit else v
    sends, recvs = [], []
    for k in range(N_DEV - 1):
        px, py, pc = flip(x, (k + 1) & 4), flip(y, (k + 1) & 2), flip(c, (k + 1) & 1)
        sends.append(_remote(src, land.at[4 * x + 2 * y + c], send_sems.at[k], recv_sems.at[k], (px, py, pc)))
        got = land.at[4 * px + 2 * py + pc]
        recvs.append(_remote(got, got, send_sems.at[k], recv_sems.at[k], (px, py, pc)))
    return sends, recvs


def _split_start(name, copies_of, srcs, land_shapes, n_copies=None):
    n = len(srcs)
    k = 3 * n if n_copies is None else n_copies

    def body(*refs):
        src_refs, land_refs = refs[:n], refs[n:2 * n]
        send_sems, recv_sems = refs[2 * n], refs[2 * n + 1]
        token = refs[-1]
        sends, _ = copies_of(src_refs, land_refs, send_sems, recv_sems)
        for cp in sends:
            cp.start()
        token[...] = jnp.zeros_like(token)

    lands = [_hbm(s) for s in land_shapes]
    dma = pltpu.SemaphoreType.DMA
    res = pl.pallas_call(
        body, name=name,
        out_shape=(dma((k,)), dma((k,)), *[pltpu.HBM(s.shape, s.dtype) for s in srcs],
                   *[pltpu.HBM(s.shape, s.dtype) for s in land_shapes], jax.ShapeDtypeStruct((8, 128), F32)),
        in_specs=[HBM] * (2 * n),
        out_specs=(SEM, SEM, *([HBM] * (2 * n)), pl.BlockSpec(memory_space=pltpu.VMEM)),
        input_output_aliases={i: 2 + i for i in range(2 * n)},
        compiler_params=SPLIT_COPY,
    )(*[_hbm(s) for s in srcs], *lands)
    return res[0], res[1], list(res[2:2 + n]), list(res[2 + n:2 + 2 * n]), res[-1]


def _split_wait(name, copies_of, send_sems, recv_sems, srcs, lands, after):
    n = len(srcs)

    def body(*refs):
        src_refs, land_refs = refs[:n], refs[n:2 * n]
        sends, recvs = copies_of(src_refs, land_refs, refs[2 * n], refs[2 * n + 1])
        for cp in sends:
            cp.wait_send()
        for cp in recvs:
            cp.wait_recv()

    res = pl.pallas_call(
        body, name=name,
        out_shape=tuple(pltpu.HBM(s.shape, s.dtype) for s in list(srcs) + list(lands)),
        in_specs=[HBM] * (2 * n) + [SEM, SEM] + [pl.BlockSpec(memory_space=pl.ANY)] * len(after),
        out_specs=tuple([HBM] * (2 * n)),
        input_output_aliases={i: i for i in range(2 * n)},
        compiler_params=SPLIT_COPY,
    )(*srcs, *lands, send_sems, recv_sems, *after)
    return list(res[:n]), list(res[n:])


def _forward_copies(srcs, lands, send_sems, recv_sems):
    x, y, c = _mesh_pos()
    sibling = (x, y, 1 - c)
    sends, recvs = [], []
    for w, land in enumerate(lands):
        hr = land.shape[1] // 2
        for j, (px, py) in enumerate(_other_chips(x, y)):
            k = 3 * w + j
            landed = land.at[2 * px + py, pl.ds(c * hr, hr)]
            sends.append(_remote(landed, landed, send_sems.at[k], recv_sems.at[k], sibling))
            other = land.at[2 * px + py, pl.ds((1 - c) * hr, hr)]
            recvs.append(_remote(other, other, send_sems.at[k], recv_sems.at[k], sibling))
    return sends, recvs


def _gather_finish(lands, n_forward):
    n = len(lands)

    def body(*refs):
        outs = refs[n:n + n_forward]
        d2d_send, d2d_recv = refs[2 * n:]
        x, y, c = _mesh_pos()
        chips = _other_chips(x, y)
        sibling = (x, y, 1 - c)
        sends = []
        for w, out in enumerate(outs):
            hr = out.shape[1] // 2
            for j, (px, py) in enumerate(chips):
                landed = out.at[2 * px + py, pl.ds(c * hr, hr)]
                cp = _remote(landed, landed, d2d_send.at[3 * w + j], d2d_recv.at[3 * w + j], sibling)
                cp.start()
                sends.append(cp)
        for w, out in enumerate(outs):
            hr = out.shape[1] // 2
            for j, (px, py) in enumerate(chips):
                other = out.at[2 * px + py, pl.ds((1 - c) * hr, hr)]
                _remote(other, other, d2d_send.at[3 * w + j], d2d_recv.at[3 * w + j], sibling).wait_recv()
        for cp in sends:
            cp.wait_send()

    dma = pltpu.SemaphoreType.DMA
    return pl.pallas_call(
        body, name="gather_finish",
        out_shape=[jax.ShapeDtypeStruct(l.shape, l.dtype) for l in lands],
        in_specs=[HBM] * n, out_specs=[HBM] * n,
        input_output_aliases={i: i for i in range(n)},
        scratch_shapes=[dma((3 * n,)), dma((3 * n,))],
    )(*lands)


def _adamw(w, g, m, v):
    m = ADAM_B1 * m + (1.0 - ADAM_B1) * g
    v = ADAM_B2 * v + (1.0 - ADAM_B2) * (g * g)
    m_hat = m / (1.0 - ADAM_B1 ** ADAM_STEP)
    v_hat = v / (1.0 - ADAM_B2 ** ADAM_STEP)
    delta = -ADAM_LR * (m_hat / (jnp.sqrt(v_hat) + ADAM_EPS) + ADAM_WD * w)
    return delta, m, v


def _adamw_big(partials, w, m, v):
    rows, cols = w.shape
    tr = _elementwise_tile(rows)

    def body(p_ref, w_ref, m_ref, v_ref, g_ref, d_ref, m2_ref, v2_ref):
        g = ((p_ref[0].astype(F32) + p_ref[1].astype(F32)) + p_ref[2].astype(F32)) + p_ref[3].astype(F32)
        g_ref[...] = g
        d_ref[...], m2_ref[...], v2_ref[...] = _adamw(w_ref[...], g, m_ref[...], v_ref[...])

    blk = pl.BlockSpec((tr, cols), lambda i: (i, 0))
    return pl.pallas_call(
        body, name="adamw_big", grid=(rows // tr,),
        in_specs=[pl.BlockSpec((N_CHIPS, tr, cols), lambda i: (0, i, 0)), blk, blk, blk],
        out_specs=[blk] * 4, out_shape=[jax.ShapeDtypeStruct((rows, cols), F32)] * 4,
        compiler_params=_params("parallel"),
    )(partials, w, m, v)


def _sum_devices(gathered, rows):
    cols = gathered.shape[1]

    def body(g_ref, o_ref):
        acc = g_ref[0:rows]
        for d in range(1, N_DEV):
            acc = acc + g_ref[d * rows:(d + 1) * rows]
        o_ref[...] = acc

    return pl.pallas_call(
        body, name="sum_devices", out_shape=jax.ShapeDtypeStruct((rows, cols), F32),
        in_specs=[pl.BlockSpec(memory_space=pltpu.VMEM)], out_specs=pl.BlockSpec(memory_space=pltpu.VMEM),
        compiler_params=pltpu.CompilerParams(vmem_limit_bytes=VMEM_LIMIT_V7X),
    )(gathered)


def _adamw_small(quads):
    n = len(quads)

    def body(*refs):
        ins, outs = refs[:4 * n], refs[4 * n:]
        for t in range(n):
            w, g, m, v = (r[...] for r in ins[4 * t:4 * t + 4])
            outs[3 * t][...], outs[3 * t + 1][...], outs[3 * t + 2][...] = _adamw(w, g, m, v)

    flat = [a for q in quads for a in q]
    vm = pl.BlockSpec(memory_space=pltpu.VMEM)
    res = pl.pallas_call(
        body, name="adamw_small",
        out_shape=[jax.ShapeDtypeStruct(q[0].shape, F32) for q in quads for _ in range(3)],
        in_specs=[vm] * (4 * n), out_specs=[vm] * (3 * n),
    )(*flat)
    return [tuple(res[3 * t:3 * t + 3]) for t in range(n)]


SMALL_PACK_ROWS = 96
META_COLS = D_MODEL // N_CHIPS
CONV_COLS = LRU_WIDTH // N_CHIPS
_WEIGHTS = ['meta_tokens', 'g_pre_mix', 'w_in', 'conv_w', 'conv_b', 'w_a', 'b_a', 'w_x', 'b_x', 'lru_lambda',
            'attn_sinks', 'w_out', 'g_post_mix', 'g_pre_ffn', 'w_ff1', 'w_ff2', 'g_post_ffn']
_BIG = ['w_in', 'w_out', 'w_ff1', 'w_ff2']


def _pack_small(dmeta, g, loss):
    z = lambda r, c: jnp.zeros((r, c), F32)
    rows = [
        dmeta,
        g['g_pre_mix'], g['g_post_mix'], g['g_pre_ffn'], g['g_post_ffn'],
        jnp.concatenate([g['conv_w'], z(4, 512)], axis=1),
        jnp.concatenate([g['conv_b'], g['b_a']], axis=1),
        jnp.concatenate([g['b_x'], g['lru_lambda']], axis=1),
        jnp.concatenate([g['attn_sinks'], z(1, D_MODEL - ATTN_HEADS)], axis=1),
        jnp.concatenate([loss, z(1, D_MODEL - 1)], axis=1),
        z(4, D_MODEL),
        g['w_a'].reshape(32, D_MODEL), g['w_x'].reshape(32, D_MODEL),
    ]
    return jnp.concatenate(rows, axis=0)


def _unpack_small(s, chip):
    return dict(
        meta_tokens=lax.dynamic_slice(s[0:N_META], (0, chip * META_COLS), (N_META, META_COLS)),
        g_pre_mix=s[16:17], g_post_mix=s[17:18], g_pre_ffn=s[18:19], g_post_ffn=s[19:20],
        conv_w=lax.dynamic_slice(s[20:24], (0, chip * CONV_COLS), (4, CONV_COLS)).reshape(1, 4, CONV_COLS),
        conv_b=s[24:25, :512], b_a=s[24:25, 512:], b_x=s[25:26, :512], lru_lambda=s[25:26, 512:],
        attn_sinks=s[26:27, :ATTN_HEADS], loss=s[27, 0],
        w_a=s[32:64].reshape(1, LRU_BLOCKS, LRU_BLOCK, LRU_BLOCK),
        w_x=s[64:96].reshape(1, LRU_BLOCKS, LRU_BLOCK, LRU_BLOCK))


def _as2d(a):
    if a.ndim == 2:
        return a
    return a.reshape(-1, a.shape[-1])


def kernel(x, meta_tokens, g_pre_mix, w_in, conv_w, conv_b, w_a, b_a, w_x, b_x, lru_lambda, attn_sinks, w_out, g_post_mix, g_pre_ffn, w_ff1, w_ff2, g_post_ffn, loss_target, m_meta_tokens, m_g_pre_mix, m_w_in, m_conv_w, m_conv_b, m_w_a, m_b_a, m_w_x, m_b_x, m_lru_lambda, m_attn_sinks, m_w_out, m_g_post_mix, m_g_pre_ffn, m_w_ff1, m_w_ff2, m_g_post_ffn, v_meta_tokens, v_g_pre_mix, v_w_in, v_conv_w, v_conv_b, v_w_a, v_b_a, v_w_x, v_b_x, v_lru_lambda, v_attn_sinks, v_w_out, v_g_post_mix, v_g_pre_ffn, v_w_ff1, v_w_ff2, v_g_post_ffn):
    weights = dict(meta_tokens=meta_tokens, g_pre_mix=g_pre_mix, w_in=w_in, conv_w=conv_w, conv_b=conv_b, w_a=w_a,
                   b_a=b_a, w_x=w_x, b_x=b_x, lru_lambda=lru_lambda, attn_sinks=attn_sinks, w_out=w_out,
                   g_post_mix=g_post_mix, g_pre_ffn=g_pre_ffn, w_ff1=w_ff1, w_ff2=w_ff2, g_post_ffn=g_post_ffn)
    mom1 = dict(zip(_WEIGHTS, [m_meta_tokens, m_g_pre_mix, m_w_in, m_conv_w, m_conv_b, m_w_a, m_b_a, m_w_x, m_b_x,
                               m_lru_lambda, m_attn_sinks, m_w_out, m_g_post_mix, m_g_pre_ffn, m_w_ff1, m_w_ff2,
                               m_g_post_ffn]))
    mom2 = dict(zip(_WEIGHTS, [v_meta_tokens, v_g_pre_mix, v_w_in, v_conv_w, v_conv_b, v_w_a, v_b_a, v_w_x, v_b_x,
                               v_lru_lambda, v_attn_sinks, v_w_out, v_g_post_mix, v_g_pre_ffn, v_w_ff1, v_w_ff2,
                               v_g_post_ffn]))
    xi, yi, ci = _mesh_pos()
    chip = 2 * xi + yi

    tiny = jnp.concatenate([meta_tokens, jnp.pad(conv_w[0], ((0, 4), (0, 128)))], axis=0)
    chip_arr = jnp.reshape(chip, (1,)).astype(jnp.int32)
    big2d = lambda a, name: a[0].T if name == 'w_in' else a[0]
    shards, lands = zip(*[_prep_shard(big2d(weights[n], n), chip_arr) for n in _BIG])
    g_in, g_tiny = _gather_weights(shards[:1], lands[:1], tiny, _prep_tiny(tiny, chip_arr))
    w_in_full = g_in.reshape(IN_WIDTH, D_MODEL)
    meta_full = jnp.concatenate([g_tiny[j, :N_META] for j in range(N_CHIPS)], axis=1)
    conv_w_full = jnp.concatenate([g_tiny[j, N_META:N_META + 4, :128] for j in range(N_CHIPS)], axis=1)
    g_send, g_recv, late_thru, late_lands, token = _split_start(
        "gather_late_start", _gather_copies, shards[1:], lands[1:])

    def late_weights(after):
        thru, landed = _split_wait("gather_late_wait", _gather_copies, g_send, g_recv, late_thru, late_lands, after)
        g_out, *ffn_landed = _gather_finish(landed, 1)
        f_send, f_recv, f_thru, f_lands, _ = _split_start("gather_forward_start", _forward_copies, thru[1:], ffn_landed)

        def ffn_weights(after):
            _, (g_f1, g_f2) = _split_wait("gather_forward_wait", _forward_copies, f_send, f_recv, f_thru, f_lands, after)
            return g_f1, g_f2

        return g_out.reshape(D_MODEL, D_MODEL), ffn_weights

    pos = jnp.stack([ci, chip]).astype(jnp.int32)
    ffn = {}


    def on_ffn_grads(dw1, dw2):
        parts = [dw1, dw2]
        lands = [lax.empty((p.shape[0], p.shape[1] // 2, p.shape[2]), p.dtype) for p in parts]
        ffn['sib'] = _split_start("sibling_ffn_start", _sibling_copies, parts, lands, len(parts))
        return ffn['sib'][4]

    def on_outproj_bwd(dattn):
        send, recv, thru, lands, _ = ffn['sib']
        parts, from_sibling = _split_wait("sibling_ffn_wait", _sibling_copies, send, recv, thru, lands, [dattn])
        cparts_ffn, lands_ffn = zip(*[_chip_presum(p, r, pos) for p, r in zip(parts, from_sibling)])
        ffn['send'], ffn['recv'], ffn['thru'], ffn['lands'], token3 = _split_start(
            "scatter_ffn_start", _scatter_copies, cparts_ffn, lands_ffn)
        return token3

    def on_mixer_grads(dw_in, dw_out):
        parts = [dw_in.reshape(N_CHIPS, IN_WIDTH // N_CHIPS, D_MODEL),
                 dw_out.reshape(N_CHIPS, D_MODEL // N_CHIPS, D_MODEL)]
        cparts, lands = zip(*[_chip_presum(p, r, pos) for p, r in zip(parts, _sibling_exchange(parts, pos))])
        ffn['mixer'] = _split_start("scatter_mixer_start", _scatter_copies, cparts, lands)
        ffn_cparts, ffn_lands = _split_wait("scatter_ffn_wait", _scatter_copies, ffn['send'], ffn['recv'],
                                            ffn['thru'], ffn['lands'], [ffn['mixer'][4]])
        ffn['inchip'] = _split_start("inchip_ffn_start", _inchip_copies, ffn_cparts, ffn_lands, 4 * len(ffn_cparts))
        return ffn['inchip'][4]

    head = jnp.concatenate([jnp.zeros((PAD_ROWS, D_MODEL), F32), meta_full], axis=0)
    loss, dx, dhead, grads = _local_step(head, x[0], loss_target[0], g_pre_mix, w_in_full, conv_w_full, conv_b, w_a[0],
                                         b_a, w_x[0], b_x, lru_lambda, attn_sinks, g_post_mix, g_pre_ffn, g_post_ffn,
                                         late_weights, on_ffn_grads, on_outproj_bwd, on_mixer_grads, token)
    grad_x = dx[None]

    pack = _pack_small(dhead[PAD_ROWS:], grads, loss)
    dev = jnp.reshape(4 * xi + 2 * yi + ci, (1,)).astype(jnp.int32)
    s_send, s_recv, s_thru, s_lands, token5 = _split_start(
        "gather_small_start", _all_peers_copies, [pack], [_prep_tiny(pack, dev, N_DEV)], N_DEV - 1)

    send, recv, thru, lands, _ = ffn['mixer']
    mixer_cparts, mixer_lands = _split_wait("scatter_mixer_wait", _scatter_copies, send, recv, thru, lands, [token5])
    m_send, m_recv, m_thru, m_lands, token6 = _split_start(
        "inchip_mixer_start", _inchip_copies, mixer_cparts, mixer_lands, 4 * len(mixer_cparts))
    send, recv, thru, lands, _ = ffn['inchip']
    _, ffn_partials = _split_wait("inchip_ffn_wait", _inchip_copies, send, recv, thru, lands, [token6])

    g_out_d, delta, new_m, new_v = {}, {}, {}, {}

    def adamw_big(names, partials):
        for name, part in zip(names, partials):
            shp = weights[name].shape
            res = _adamw_big(part, big2d(weights[name], name), big2d(mom1[name], name), big2d(mom2[name], name))
            g_out_d[name], delta[name], new_m[name], new_v[name] = (big2d(r[None], name).reshape(shp) for r in res)

    adamw_big(_BIG[2:], ffn_partials)
    _, mixer_partials = _split_wait("inchip_mixer_wait", _inchip_copies, m_send, m_recv, m_thru, m_lands,
                                    [g_out_d[n] for n in _BIG[2:]])
    adamw_big(_BIG[:2], mixer_partials)

    _, (gathered,) = _split_wait("gather_small_wait", _all_peers_copies, s_send, s_recv, s_thru, s_lands,
                                 [g_out_d[n] for n in _BIG])
    small = _unpack_small(_sum_devices(gathered.reshape(N_DEV * SMALL_PACK_ROWS, D_MODEL), SMALL_PACK_ROWS), chip)
    loss = small['loss']
    small_names = [n for n in _WEIGHTS if n not in _BIG]
    quads = [(_as2d(weights[n]), _as2d(small[n]), _as2d(mom1[n]), _as2d(mom2[n])) for n in small_names]
    for name, (d, m2, v2) in zip(small_names, _adamw_small(quads)):
        shp = weights[name].shape
        g_out_d[name] = small[name].reshape(shp)
        delta[name], new_m[name], new_v[name] = d.reshape(shp), m2.reshape(shp), v2.reshape(shp)

    return (loss, grad_x, *[g_out_d[n] for n in _WEIGHTS], *[delta[n] for n in _WEIGHTS],
            *[new_m[n] for n in _WEIGHTS], *[new_v[n] for n in _WEIGHTS])
```

```python
import numpy as np
import jax
import jax.numpy as jnp
from jax import lax
from jax.experimental import pallas as pl
from jax.experimental.pallas import tpu as pltpu

F32 = jnp.float32
BF16 = jnp.bfloat16

D_MODEL = 1024
N_META = 16
BLOCK = 128
PAD_ROWS = BLOCK - N_META
HEAD_DIM = 64
ATTN_HEADS = 8
GQA_GROUP = 4
ATTN_WIDTH = 512
KV_WIDTH = 128
QKV_WIDTH = ATTN_WIDTH + 2 * KV_WIDTH
LRU_WIDTH = 512
LRU_BLOCKS = 8
LRU_BLOCK = 64
LRU_C = 8.0
IN_WIDTH = 1792
D_FF = 4096
N_CHIPS = 4
FF_CHUNK = D_FF // N_CHIPS
EPS = 1e-6
NEG = -1e30

ADAM_LR = 0.001
ADAM_B1 = 0.9
ADAM_B2 = 0.999
ADAM_EPS = 1e-08
ADAM_WD = 0.01
ADAM_STEP = 10

VMEM_LIMIT_V7X = 62 * 1024 * 1024
MESH = pl.DeviceIdType.MESH

NT = (((1,), (1,)), ((), ()))
TN = (((0,), (0,)), ((), ()))


def _row_tile(tp):
    return 640 if tp % 640 == 0 else BLOCK


def _elementwise_tile(rows):
    return 512 if rows % 512 == 0 else rows


def _wgrad_row_tile(tp):
    return 1664 if tp % 1664 == 0 else _row_tile(tp)


def _params(*sem):
    return pltpu.CompilerParams(dimension_semantics=sem, vmem_limit_bytes=VMEM_LIMIT_V7X)


def _dot(a, b):
    return jnp.dot(a, b, preferred_element_type=F32)


def _dot_nt(a, b):
    return lax.dot_general(a, b, NT, preferred_element_type=F32)


def _dot_tn(a, b):
    return lax.dot_general(a, b, TN, preferred_element_type=F32)


def _rms(x):
    rs = lax.rsqrt(jnp.mean(x * x, axis=-1, keepdims=True) + EPS)
    return x * rs, rs


def _rms_bwd(xhat, rs, g, dy):
    dyg = dy * g
    dx = rs * (dyg - xhat * jnp.mean(dyg * xhat, axis=-1, keepdims=True))
    dg = jnp.sum(dy * xhat, axis=0, keepdims=True)
    return dx, dg


def _gelu(x):
    k = 0.7978845608028654
    t = jnp.tanh(x * (k + (k * 0.044715) * (x * x)))
    return (0.5 * x) * (1.0 + t), t


def _gelu_grad(x, t):
    k = 0.7978845608028654
    return 0.5 * (1.0 + t) + 0.5 * x * (1.0 - t * t) * k * (1.0 + 3 * 0.044715 * x * x)


def _sigmoid(x):
    return 0.5 * jnp.tanh(0.5 * x) + 0.5


def _one_minus_exp2(y):
    t = jnp.tanh(y)
    return (-2.0 * t) / (1.0 - t)


def _softplus(x):
    return jnp.maximum(x, 0.0) + jnp.log1p(jnp.exp(-jnp.abs(x)))


def _seq_specs(tr, delay=0):
    qb = tr // BLOCK
    tile = lambda i: jnp.maximum(i - delay, 0)
    return [pl.BlockSpec((BLOCK, D_MODEL), lambda i, *_, s=s: (jnp.maximum(tile(i) * qb + s - 1, 0), 0))
            for s in range(qb)]


def _seq_tile(head, pieces, i):
    first = jnp.where(i == 0, head, pieces[0][...])
    return jnp.concatenate([first] + [p[...] for p in pieces[1:]], axis=0)


GROUP_ROWS = GQA_GROUP * BLOCK


def _attn_bias():
    j = np.arange(2 * BLOCK)[:, None]
    i = np.arange(BLOCK)[None, :]
    band = (j - i >= 1) & (j - i <= BLOCK)
    out = []
    for n in range(3):
        ok = band & ((n - 1) * BLOCK + j >= PAD_ROWS) if n < 2 else band
        out.append(np.tile(np.where(ok, 0.0, NEG).astype(np.float32), (1, GQA_GROUP)))
    return jnp.asarray(np.stack(out))


def _heads_t(at, g):
    heads = range(GQA_GROUP * g, GQA_GROUP * (g + 1))
    return jnp.concatenate([at[h * HEAD_DIM:(h + 1) * HEAD_DIM] for h in heads], axis=1).astype(BF16)


def _from_heads_t(groups):
    pairs = []
    for p in groups:
        for h in range(0, GQA_GROUP, 2):
            two = jnp.concatenate([p[:, h * BLOCK:(h + 1) * BLOCK], p[:, (h + 1) * BLOCK:(h + 2) * BLOCK]], axis=0)
            pairs.append(two.T)
    return jnp.concatenate(pairs, axis=1)


def _stack_heads(a, g):
    heads = range(GQA_GROUP * g, GQA_GROUP * (g + 1))
    return jnp.concatenate([a[:, h * HEAD_DIM:(h + 1) * HEAD_DIM] for h in heads], axis=0)


def _unstack_heads(groups):
    return jnp.concatenate([p[h * BLOCK:(h + 1) * BLOCK] for p in groups for h in range(GQA_GROUP)], axis=1)


def _attn_probs_t(k_g, qg, bias, sink_row):
    st = _dot_nt(k_g, qg) + bias
    m = jnp.maximum(jnp.max(st, axis=0, keepdims=True), sink_row)
    p = jnp.exp(st - m)
    es = jnp.exp(sink_row - m)
    inv = 1.0 / (jnp.sum(p, axis=0, keepdims=True) + es)
    return p * inv, es * inv


def _attn_consts(sinks):
    return jnp.repeat(sinks.reshape(ATTN_HEADS), BLOCK).reshape(ATTN_HEADS // GQA_GROUP, GROUP_ROWS), _attn_bias()


_SINK_SPEC = pl.BlockSpec((ATTN_HEADS // GQA_GROUP, GROUP_ROWS), lambda n: (0, 0))
_BIAS_SPEC = pl.BlockSpec((3, 2 * BLOCK, GROUP_ROWS), lambda n: (0, 0, 0))
_QSCALE = HEAD_DIM ** -0.5


def _kv_specs(tr):
    qb = tr // BLOCK
    prev = lambda col: pl.BlockSpec((BLOCK, KV_WIDTH), lambda t: (jnp.maximum(t * qb - 1, 0), col))
    cur = lambda col: pl.BlockSpec((tr, KV_WIDTH), lambda t: (t, col))
    return [prev(4), cur(4), prev(5), cur(5)]


def _block_bias(b_ref, t, qb, i):
    return b_ref[2] if i >= 2 else b_ref[jnp.minimum(t * qb + i, 2)]


N_KV = ATTN_HEADS // GQA_GROUP


def _prob_specs(qb):
    return [pl.BlockSpec((qb, N_KV, 2 * BLOCK, GROUP_ROWS), lambda t: (t, 0, 0, 0)),
            pl.BlockSpec((qb, SUBLANES, GROUP_ROWS), lambda t: (t, 0, 0))]


def _attn_fwd(qkv, sinks):
    tp = qkv.shape[0]
    tr = _row_tile(tp)
    qb, nb = tr // BLOCK, tp // BLOCK
    sink_rows, bias = _attn_consts(sinks)

    def body(s_ref, b_ref, q_ref, kp_ref, kc_ref, vp_ref, vc_ref, o_ref, p_ref, ps_ref):
        t = pl.program_id(0)
        k_all = jnp.concatenate([kp_ref[...], kc_ref[...]], axis=0)
        v_all = jnp.concatenate([vp_ref[...], vc_ref[...]], axis=0)
        for i in range(qb):
            rows = slice(i * BLOCK, (i + 1) * BLOCK)
            q = q_ref[rows]
            k2, v2 = k_all[i * BLOCK:(i + 2) * BLOCK], v_all[i * BLOCK:(i + 2) * BLOCK]
            bias_n = _block_bias(b_ref, t, qb, i)
            outs, sink_probs = [], []
            for g in range(N_KV):
                cols = slice(g * HEAD_DIM, (g + 1) * HEAD_DIM)
                qg = _stack_heads(q, g) * jnp.asarray(_QSCALE, BF16)
                p, ps = _attn_probs_t(k2[:, cols], qg, bias_n, s_ref[g:g + 1])
                pb = p.astype(BF16)
                p_ref[i, g] = pb
                sink_probs.append(ps)
                outs.append(_dot_tn(pb, v2[:, cols]))
            o_ref[rows] = _unstack_heads(outs).astype(BF16)
            ps_ref[i] = jnp.concatenate(sink_probs + [jnp.zeros((SUBLANES - N_KV, GROUP_ROWS), F32)], axis=0)

    return pl.pallas_call(
        body, name="attn_fwd", grid=(tp // tr,),
        in_specs=[_SINK_SPEC, _BIAS_SPEC, pl.BlockSpec((tr, ATTN_WIDTH), lambda t: (t, 0))] + _kv_specs(tr),
        out_specs=[pl.BlockSpec((tr, ATTN_WIDTH), lambda t: (t, 0))] + _prob_specs(qb),
        out_shape=[jax.ShapeDtypeStruct((tp, ATTN_WIDTH), BF16),
                   jax.ShapeDtypeStruct((nb, N_KV, 2 * BLOCK, GROUP_ROWS), BF16),
                   jax.ShapeDtypeStruct((nb, SUBLANES, GROUP_ROWS), F32)],
        compiler_params=_params("parallel"),
    )(sink_rows, bias, qkv, qkv, qkv, qkv, qkv)


def _conv_taps(x, halo):
    ext = jnp.concatenate([halo, x], axis=0)
    return [ext[8:] if k == 3 else pltpu.roll(ext, 3 - k, 0)[8:] for k in range(4)]


def _lru_gates(xc, wa, ba, wx, bx, sp):
    xb = xc.astype(BF16)
    r = _sigmoid(_dot(xb, wa) + ba)
    ig = _sigmoid(_dot(xb, wx) + bx)
    log_a = (-LRU_C * sp) * r
    a = jnp.exp(log_a)
    mult = jnp.sqrt(_one_minus_exp2(log_a))
    return xb, r, ig, a, mult


SUBLANES = 8


def _scan_fwd(a, b, h_in):
    n, width = a.shape
    a, b = (v.reshape(n // SUBLANES, SUBLANES, width) for v in (a, b))
    in_group = lax.broadcasted_iota(jnp.int32, a.shape, 1)
    for d in (1, 2, 4):
        keep = in_group >= d
        b = jnp.where(keep, a * pltpu.roll(b, d, 1) + b, b)
        a = jnp.where(keep, a * pltpu.roll(a, d, 1), a)
    a, b = a.reshape(n, width), b.reshape(n, width)
    out, carry = [], h_in
    for g in range(0, n, SUBLANES):
        h = a[g:g + SUBLANES] * carry + b[g:g + SUBLANES]
        out.append(h)
        carry = h[SUBLANES - 1:]
    return jnp.concatenate(out, axis=0)


def _scan_rev(c, b, g_in):
    n, width = c.shape
    c, b = (v.reshape(n // SUBLANES, SUBLANES, width) for v in (c, b))
    in_group = lax.broadcasted_iota(jnp.int32, c.shape, 1)
    for d in (1, 2, 4):
        keep = in_group < SUBLANES - d
        b = jnp.where(keep, b + c * pltpu.roll(b, SUBLANES - d, 1), b)
        c = jnp.where(keep, c * pltpu.roll(c, SUBLANES - d, 1), c)
    c, b = c.reshape(n, width), b.reshape(n, width)
    out, carry = [], g_in
    for g in range(n - SUBLANES, -1, -SUBLANES):
        r = b[g:g + SUBLANES] + c[g:g + SUBLANES] * carry
        out.append(r)
        carry = r[:1]
    return jnp.concatenate(out[::-1], axis=0)


def _inproj_lru_fwd(head, x, g, w_in, conv_w, conv_b, wa, ba, wx, bx, lam, token):
    tp = BLOCK + x.shape[0]
    tr = _row_tile(tp)
    qb, nt = tr // BLOCK, tp // tr
    small = [conv_w, conv_b, wa, ba, wx, bx, lam]

    def body(*refs):
        head_ref, pieces = refs[0], refs[1:1 + qb]
        g_ref, w_ref, _, cw_ref, cb_ref, wa_ref, ba_ref, wx_ref, bx_ref, lam_ref = refs[1 + qb:11 + qb]
        u_ref, qkv_ref, xr_ref, yr_ref, hr_ref, rec_ref, zbuf, halo, hprev = refs[11 + qb:]
        i = pl.program_id(0)
        cur = i % 2

        @pl.when(i == 0)
        def _():
            halo[...] = jnp.zeros_like(halo)
            hprev[...] = jnp.zeros_like(hprev)
            zbuf[1] = jnp.zeros((tr, 2 * LRU_WIDTH), F32)

        def recurrent_branch(valid):
            cw, cb = cw_ref[...], cb_ref[...]
            wa_m, ba_v, wx_m, bx_v = wa_ref[...], ba_ref[...], wx_ref[...], bx_ref[...]
            sp = _softplus(-lam_ref[...])
            before, h_last = halo[...], hprev[0:1]
            for b in range(qb):
                rows = slice(b * BLOCK, (b + 1) * BLOCK)
                xy = zbuf[1 - cur, rows]
                xin = xy[:, :LRU_WIDTH]
                taps = _conv_taps(xin, before)
                before = xin[BLOCK - 8:]
                xc = cb + sum(cw[k:k + 1] * taps[k] for k in range(4))
                _, _, ig, a, mult = _lru_gates(xc, wa_m, ba_v, wx_m, bx_v, sp)
                u = mult * (ig * xc)
                if b == 0:
                    pos = (i - 1) * tr + lax.broadcasted_iota(jnp.int32, xc.shape, 0)
                    u = jnp.where(pos >= PAD_ROWS, u, 0.0)
                h = _scan_fwd(a, u, h_last)
                h_last = h[BLOCK - 1:]
                hr_ref[rows] = h
                gl, _ = _gelu(xy[:, LRU_WIDTH:])
                rec_ref[rows] = (gl * h).astype(BF16)
            halo[...] = jnp.where(valid, before, 0.0)
            hprev[0:1] = jnp.where(valid, h_last, 0.0)

        def projection():
            xhat, _ = _rms(_seq_tile(head_ref[...], pieces, i))
            u = (xhat * g_ref[...]).astype(BF16)
            u_ref[...] = u
            z = _dot_nt(u, w_ref[...])
            qkv_ref[...] = z[:, :QKV_WIDTH].astype(BF16)
            xr_ref[...] = z[:, QKV_WIDTH:QKV_WIDTH + LRU_WIDTH]
            yr_ref[...] = z[:, QKV_WIDTH + LRU_WIDTH:]
            zbuf[cur] = z[:, QKV_WIDTH:]

        @pl.when(i < nt)
        def _():
            recurrent_branch(i >= 1)
            projection()

        @pl.when(i == nt)
        def _():
            recurrent_branch(True)

    last = nt - 1
    this_row = lambda w: pl.BlockSpec((tr, w), lambda i: (jnp.minimum(i, last), 0))
    prev_row = lambda w: pl.BlockSpec((tr, w), lambda i: (jnp.maximum(i - 1, 0), 0))
    full = lambda a: pl.BlockSpec(a.shape, lambda i: (0,) * a.ndim)
    piece_specs = [pl.BlockSpec((BLOCK, D_MODEL), lambda i, s=s: (jnp.maximum(jnp.minimum(i, last) * qb + s - 1, 0), 0))
                   for s in range(qb)]
    return pl.pallas_call(
        body, name="inproj_lru_fwd", grid=(nt + 1,),
        in_specs=[full(head)] + piece_specs + [full(g), full(w_in), full(token)] + [full(a) for a in small],
        out_specs=[this_row(D_MODEL), this_row(QKV_WIDTH), this_row(LRU_WIDTH), this_row(LRU_WIDTH),
                   prev_row(LRU_WIDTH), prev_row(LRU_WIDTH)],
        out_shape=[jax.ShapeDtypeStruct((tp, D_MODEL), BF16), jax.ShapeDtypeStruct((tp, QKV_WIDTH), BF16),
                   jax.ShapeDtypeStruct((tp, LRU_WIDTH), F32), jax.ShapeDtypeStruct((tp, LRU_WIDTH), F32),
                   jax.ShapeDtypeStruct((tp, LRU_WIDTH), F32), jax.ShapeDtypeStruct((tp, LRU_WIDTH), BF16)],
        scratch_shapes=[pltpu.VMEM((2, tr, 2 * LRU_WIDTH), F32), pltpu.VMEM((8, LRU_WIDTH), F32),
                        pltpu.VMEM((8, LRU_WIDTH), F32)],
        compiler_params=_params("arbitrary"),
    )(head, *([x] * qb), g, w_in, token, *small)


def _outproj_fwd(attn, rec, w_out, head, x, g_post_mix, g_pre_ffn):
    tp = attn.shape[0]
    tr = _row_tile(tp)
    qb = tr // BLOCK

    def body(*refs):
        a_ref, r_ref, w_ref, head_ref = refs[:4]
        pieces = refs[4:4 + qb]
        gm_ref, gf_ref, mix_ref, h1_ref, u1_ref = refs[4 + qb:]
        mix = _dot(a_ref[...], w_ref[:ATTN_WIDTH]) + _dot(r_ref[...], w_ref[ATTN_WIDTH:])
        mix_ref[...] = mix
        mhat, _ = _rms(mix)
        h1 = _seq_tile(head_ref[...], pieces, pl.program_id(0)) + mhat * gm_ref[...]
        h1_ref[...] = h1
        hhat, _ = _rms(h1)
        u1_ref[...] = (hhat * gf_ref[...]).astype(BF16)

    row = lambda w: pl.BlockSpec((tr, w), lambda i: (i, 0))
    full = lambda a: pl.BlockSpec(a.shape, lambda i: (0,) * a.ndim)
    return pl.pallas_call(
        body, name="outproj_fwd", grid=(tp // tr,),
        in_specs=[row(ATTN_WIDTH), row(LRU_WIDTH), full(w_out), full(head)] + _seq_specs(tr)
        + [full(g_post_mix), full(g_pre_ffn)],
        out_specs=[row(D_MODEL), row(D_MODEL), row(D_MODEL)],
        out_shape=[jax.ShapeDtypeStruct((tp, D_MODEL), F32), jax.ShapeDtypeStruct((tp, D_MODEL), F32),
                   jax.ShapeDtypeStruct((tp, D_MODEL), BF16)],
        compiler_params=_params("parallel"),
    )(attn, rec, w_out, head, *([x] * qb), g_post_mix, g_pre_ffn)


FFN_STEPS = N_CHIPS


def _resident(a):
    return pl.BlockSpec(a.shape, lambda *_: (0,) * a.ndim, pipeline_mode=pl.Buffered(1))


def _ffn_fwd(u1, w1, w2, h1, tgt, g_post_ffn):
    tp = h1.shape[0]
    tr = _row_tile(tp)
    qb, nt = tr // BLOCK, tp // tr
    sr = tr // FFN_STEPS

    def body(*refs):
        u_ref, w1_ref, w2_ref, h1_ref = refs[:4]
        t_pieces = refs[4:4 + qb]
        g_ref, r1_ref, dy_ref, df2_ref, loss_ref, dg_ref, acc = refs[4 + qb:]
        i, c = pl.program_id(0), pl.program_id(1)
        cur = i % 2

        @pl.when((i == 0) & (c == 0))
        def _():
            loss_ref[...] = jnp.zeros_like(loss_ref)
            dg_ref[...] = jnp.zeros_like(dg_ref)
            acc[1] = jnp.zeros((tr, D_MODEL), F32)

        def matmuls():
            r = jnp.maximum(_dot(u_ref[...], w1_ref[c]), 0.0)
            r1_ref[...] = r.astype(BF16)
            return _dot((r * r).astype(BF16), w2_ref[c])

        def finish_previous_tile(k, valid):
            lo, hi = k * sr, (k + 1) * sr
            g = g_ref[...]
            fhat, rs = _rms(acc[1 - cur, lo:hi])
            h2 = h1_ref[...] + fhat * g
            rows = (i - 1) * tr + lo + lax.broadcasted_iota(jnp.int32, h2.shape, 0)
            tgt = jnp.concatenate([p[max(lo - s * BLOCK, 0):min(hi - s * BLOCK, BLOCK)] for s, p in enumerate(t_pieces)
                                   if lo < (s + 1) * BLOCK and hi > s * BLOCK], axis=0)
            err = jnp.where((rows >= BLOCK) & valid, h2 - tgt, 0.0)
            dy = err * (1.0 / D_MODEL)
            dy_ref[...] = dy
            loss_ref[...] += (0.5 / D_MODEL) * jnp.sum(err * err)
            df2, dg = _rms_bwd(fhat, rs, g, dy)
            df2_ref[...] = df2.astype(BF16)
            dg_ref[...] += dg

        for k in range(FFN_STEPS):
            @pl.when((c == k) & (i < nt))
            def _(k=k):
                finish_previous_tile(k, i >= 1)
                if k == 0:
                    acc[cur] = matmuls()
                else:
                    acc[cur] += matmuls()

            @pl.when((c == k) & (i == nt))
            def _(k=k):
                finish_previous_tile(k, True)

    last = nt - 1
    this_row = pl.BlockSpec((tr, D_MODEL), lambda i, c: (jnp.minimum(i, last), 0))
    prev_quarter = pl.BlockSpec((sr, D_MODEL), lambda i, c: (jnp.maximum(i - 1, 0) * FFN_STEPS + c, 0))
    prev_quarter_out = pl.BlockSpec(
        (sr, D_MODEL), lambda i, c: (jnp.where(i == 0, nt * FFN_STEPS, (i - 1) * FFN_STEPS + c), 0))
    full = lambda a: pl.BlockSpec(a.shape, lambda i, c: (0,) * a.ndim)
    return pl.pallas_call(
        body, name="ffn_fwd", grid=(nt + 1, FFN_STEPS),
        in_specs=[this_row, _resident(w1), _resident(w2), prev_quarter] + _seq_specs(tr, delay=1) + [full(g_post_ffn)],
        out_specs=[pl.BlockSpec((tr, FF_CHUNK), lambda i, c: (jnp.minimum(i, last), jnp.where(i < nt, c, FFN_STEPS - 1))),
                   prev_quarter_out, prev_quarter_out,
                   pl.BlockSpec((1, 1), lambda i, c: (0, 0)), pl.BlockSpec((1, D_MODEL), lambda i, c: (0, 0))],
        out_shape=[jax.ShapeDtypeStruct((tp, D_FF), BF16), jax.ShapeDtypeStruct((tp + sr, D_MODEL), F32),
                   jax.ShapeDtypeStruct((tp + sr, D_MODEL), BF16), jax.ShapeDtypeStruct((1, 1), F32),
                   jax.ShapeDtypeStruct((1, D_MODEL), F32)],
        scratch_shapes=[pltpu.VMEM((2, tr, D_MODEL), F32)],
        compiler_params=_params("arbitrary", "arbitrary"),
    )(u1, w1, w2, h1, *([tgt] * qb), g_post_ffn)


def _ffn_bwd_data(df2, r1, w1, w2, dy, h1, mix, g_pre_ffn, g_post_mix):
    tp = h1.shape[0]
    tr = _row_tile(tp)
    nt = tp // tr
    sr = tr // FFN_STEPS

    def body(df2_ref, r1_ref, w1_ref, w2_ref, dy_ref, h1_ref, mix_ref, gf_ref, gm_ref,
             da_ref, dh1_ref, dmix_ref, dgf_ref, dgm_ref, acc):
        i, c = pl.program_id(0), pl.program_id(1)
        cur = i % 2

        @pl.when((i == 0) & (c == 0))
        def _():
            dgf_ref[...] = jnp.zeros_like(dgf_ref)
            dgm_ref[...] = jnp.zeros_like(dgm_ref)
            acc[1] = jnp.zeros((tr, D_MODEL), F32)

        def matmuls():
            df = _dot_nt(df2_ref[...], w2_ref[c])
            da = (df * (2.0 * r1_ref[...].astype(F32))).astype(BF16)
            da_ref[...] = da
            return _dot_nt(da, w1_ref[c])

        def finish_previous_tile(k, valid):
            lo, hi = k * sr, (k + 1) * sr
            hhat, rs = _rms(h1_ref[...])
            dx, dgf = _rms_bwd(hhat, rs, gf_ref[...], acc[1 - cur, lo:hi])
            dh1 = dy_ref[...] + dx
            dh1_ref[...] = dh1
            mhat, rsm = _rms(mix_ref[...])
            dmix, dgm = _rms_bwd(mhat, rsm, gm_ref[...], dh1)
            dmix_ref[...] = dmix.astype(BF16)
            dgf_ref[...] += jnp.where(valid, dgf, 0.0)
            dgm_ref[...] += jnp.where(valid, dgm, 0.0)

        for k in range(FFN_STEPS):
            @pl.when((c == k) & (i < nt))
            def _(k=k):
                finish_previous_tile(k, i >= 1)
                if k == 0:
                    acc[cur] = matmuls()
                else:
                    acc[cur] += matmuls()

            @pl.when((c == k) & (i == nt))
            def _(k=k):
                finish_previous_tile(k, True)

    last = nt - 1
    this_row = pl.BlockSpec((tr, D_MODEL), lambda i, c: (jnp.minimum(i, last), 0))
    prev_quarter = pl.BlockSpec((sr, D_MODEL), lambda i, c: (jnp.maximum(i - 1, 0) * FFN_STEPS + c, 0))
    prev_quarter_out = pl.BlockSpec(
        (sr, D_MODEL), lambda i, c: (jnp.where(i == 0, nt * FFN_STEPS, (i - 1) * FFN_STEPS + c), 0))
    chunk = pl.BlockSpec((tr, FF_CHUNK), lambda i, c: (jnp.minimum(i, last), jnp.where(i < nt, c, FFN_STEPS - 1)))
    gain = pl.BlockSpec((1, D_MODEL), lambda i, c: (0, 0))
    return pl.pallas_call(
        body, name="ffn_bwd_data", grid=(nt + 1, FFN_STEPS),
        in_specs=[this_row, chunk, _resident(w1), _resident(w2), prev_quarter, prev_quarter, prev_quarter, gain, gain],
        out_specs=[chunk, prev_quarter_out, prev_quarter_out, gain, gain],
        out_shape=[jax.ShapeDtypeStruct((tp, D_FF), BF16), jax.ShapeDtypeStruct((tp + sr, D_MODEL), F32),
                   jax.ShapeDtypeStruct((tp + sr, D_MODEL), BF16), jax.ShapeDtypeStruct((1, D_MODEL), F32),
                   jax.ShapeDtypeStruct((1, D_MODEL), F32)],
        scratch_shapes=[pltpu.VMEM((2, tr, D_MODEL), F32)],
        compiler_params=_params("arbitrary", "arbitrary"),
    )(df2, r1, w1, w2, dy, h1, mix, g_pre_ffn, g_post_mix)


def _ffn_bwd_weights(u1, da1, r1, df2):
    tp = u1.shape[0]
    tr = _wgrad_row_tile(tp)

    def body(u_ref, da_ref, r1_ref, df2_ref, dw1_ref, dw2_ref):
        i = pl.program_id(1)

        def products():
            r = r1_ref[...].astype(F32)
            return _dot_tn(u_ref[...], da_ref[...]), _dot_tn((r * r).astype(BF16), df2_ref[...])

        @pl.when(i == 0)
        def _():
            dw1_ref[0], dw2_ref[0] = products()

        @pl.when(i > 0)
        def _():
            p1, p2 = products()
            dw1_ref[0] += p1
            dw2_ref[0] += p2

    row = pl.BlockSpec((tr, D_MODEL), lambda c, i: (i, 0))
    chunk = pl.BlockSpec((tr, FF_CHUNK), lambda c, i: (i, c))
    return pl.pallas_call(
        body, name="ffn_bwd_weights", grid=(N_CHIPS, tp // tr),
        in_specs=[row, chunk, chunk, row],
        out_specs=[pl.BlockSpec((1, D_MODEL, FF_CHUNK), lambda c, i: (c, 0, 0)),
                   pl.BlockSpec((1, FF_CHUNK, D_MODEL), lambda c, i: (c, 0, 0))],
        out_shape=[jax.ShapeDtypeStruct((N_CHIPS, D_MODEL, FF_CHUNK), F32),
                   jax.ShapeDtypeStruct((N_CHIPS, FF_CHUNK, D_MODEL), F32)],
        compiler_params=_params("parallel", "arbitrary"),
    )(u1, da1, r1, df2)


N_VEC_ROWS = 8


def _outproj_lru_bwd(dmix, w_out, attn, rec, xr, yr, hr, conv_w, conv_b, wa, ba, wx, bx, lam, token):
    tp = xr.shape[0]
    tr = _row_tile(tp)
    qb, nt = tr // BLOCK, tp // tr

    def body(dm_ref, w_ref, at_ref, rc_ref, xr_ref, xh_ref, yr_ref, hr_ref, hp_ref,
             cw_ref, cb_ref, wa_ref, ba_ref, wx_ref, bx_ref, lam_ref, _,
             dxr_ref, dyr_ref, dat_ref, dwo_ref, dwa_ref, dwx_ref, vec_ref, g_next, a_next, dxc_next, dsp):
        s = pl.program_id(0)
        t = nt - 1 - s

        @pl.when(s == 0)
        def _():
            g_next[...] = jnp.zeros_like(g_next)
            a_next[...] = jnp.zeros_like(a_next)
            dxc_next[...] = jnp.zeros_like(dxc_next)
            dsp[...] = jnp.zeros_like(dsp)
            dwo_ref[...] = jnp.zeros_like(dwo_ref)
            dwa_ref[...] = jnp.zeros_like(dwa_ref)
            dwx_ref[...] = jnp.zeros_like(dwx_ref)
            vec_ref[...] = jnp.zeros_like(vec_ref)

        dm = dm_ref[...]
        dcat = _dot_nt(dm, w_ref[...])
        dat_ref[...] = dcat[:, :ATTN_WIDTH].astype(BF16)
        drec_tile = dcat[:, ATTN_WIDTH:]
        dwo_ref[:ATTN_WIDTH] += _dot_tn(at_ref[...], dm)
        dwo_ref[ATTN_WIDTH:] += _dot_tn(rc_ref[...], dm)

        first_tile = t == 0
        cw, cb = cw_ref[...], cb_ref[...]
        lam_v = lam_ref[...]
        sp = _softplus(-lam_v)
        wa_m, ba_v, wx_m, bx_v = wa_ref[...], ba_ref[...], wx_ref[...], bx_ref[...]
        rows = lax.broadcasted_iota(jnp.int32, (BLOCK, LRU_WIDTH), 0)
        col = lambda v: jnp.sum(v, axis=0, keepdims=True)

        g_after, a_after, dxc_after = g_next[0:1], a_next[0:1], dxc_next[...]
        xbs, dgrs, dgis = [], [], []
        vec = [jnp.zeros((1, LRU_WIDTH), F32) for _ in range(N_VEC_ROWS)]
        for i in reversed(range(qb)):
            blk = slice(i * BLOCK, (i + 1) * BLOCK)
            if i == 0:
                x_before = jnp.where(first_tile, 0.0, xh_ref[...])
                h_before = jnp.where(first_tile, 0.0, hp_ref[7:8])
            else:
                x_before = xr_ref[i * BLOCK - 8:i * BLOCK]
                h_before = hr_ref[i * BLOCK - 1:i * BLOCK]
            taps = _conv_taps(xr_ref[blk], x_before)
            xc = cb + sum(cw[k:k + 1] * taps[k] for k in range(4))
            xb, r, ig, a, mult = _lru_gates(xc, wa_m, ba_v, wx_m, bx_v, sp)

            yr_v = yr_ref[blk]
            gl, th = _gelu(yr_v)
            h = hr_ref[blk]
            drec = drec_tile[blk]
            dyr_ref[blk] = (drec * h * _gelu_grad(yr_v, th)).astype(BF16)

            a_up = jnp.where(rows == BLOCK - 1, a_after, pltpu.roll(a, BLOCK - 1, 0))
            g = _scan_rev(a_up, drec * gl, g_after)
            g_after, a_after = g[0:1], a[0:1]

            h_prev = jnp.where(rows == 0, h_before, pltpu.roll(h, 1, 0))
            du, da = g, g * h_prev
            if i == 0:
                real = (t * tr + rows) >= PAD_ROWS
                du, da = jnp.where(real, du, 0.0), jnp.where(real, da, 0.0)
            dmult = du * (ig * xc)
            dig = du * (mult * xc)
            dxc = du * (mult * ig)
            dlog_a = da * a - dmult * (a * a / mult)
            if i == 0:
                dlog_a = jnp.where(real, dlog_a, 0.0)
            dgr = (dlog_a * (-LRU_C * sp)) * (r * (1.0 - r))
            dgi = dig * (ig * (1.0 - ig))
            dgr_b, dgi_b = dgr.astype(BF16), dgi.astype(BF16)
            dxc = dxc + _dot_nt(dgr_b, wa_m) + _dot_nt(dgi_b, wx_m)
            xbs.append(xb)
            dgrs.append(dgr_b)
            dgis.append(dgi_b)

            ext = jnp.concatenate([dxc, dxc_after], axis=0)
            up = [ext[:BLOCK] if j == 0 else pltpu.roll(ext, BLOCK + 8 - j, 0)[:BLOCK] for j in range(4)]
            dxr_ref[blk] = sum(cw[k:k + 1] * up[3 - k] for k in range(4)).astype(BF16)
            dxc_after = dxc[:8]

            for k in range(4):
                vec[k] = vec[k] + col(dxc * taps[k])
            vec[4] = vec[4] + col(dxc)
            vec[5] = vec[5] + col(dgr)
            vec[6] = vec[6] + col(dgi)
            vec[7] = vec[7] + col(dlog_a * (-LRU_C * r))

        g_next[0:1], a_next[0:1], dxc_next[...] = g_after, a_after, dxc_after
        xb_all = jnp.concatenate(xbs, axis=0)
        dwa_ref[...] += _dot_tn(xb_all, jnp.concatenate(dgrs, axis=0))
        dwx_ref[...] += _dot_tn(xb_all, jnp.concatenate(dgis, axis=0))
        for k in range(7):
            vec_ref[k:k + 1] += vec[k]
        dsp[0:1] += vec[7]

        @pl.when(s == nt - 1)
        def _():
            vec_ref[7:8] = dsp[0:1] * (-_sigmoid(-lam_v))

    blk_spec = pl.BlockSpec((tr, LRU_WIDTH), lambda s: (nt - 1 - s, 0))
    rows_before = pl.BlockSpec((8, LRU_WIDTH), lambda s: (jnp.maximum((nt - 1 - s) * (tr // 8) - 1, 0), 0))
    full = lambda a: pl.BlockSpec(a.shape, lambda s: (0,) * a.ndim)
    small = [conv_w, conv_b, wa, ba, wx, bx, lam, token]
    sq = pl.BlockSpec((LRU_WIDTH, LRU_WIDTH), lambda s: (0, 0))
    wide = pl.BlockSpec((tr, D_MODEL), lambda s: (nt - 1 - s, 0))
    whole = pl.BlockSpec((D_MODEL, D_MODEL), lambda s: (0, 0))
    return pl.pallas_call(
        body, name="outproj_lru_bwd", grid=(nt,),
        in_specs=[wide, whole, blk_spec, blk_spec, blk_spec, rows_before, blk_spec, blk_spec, rows_before]
        + [full(a) for a in small],
        out_specs=[blk_spec, blk_spec, blk_spec, whole, sq, sq, pl.BlockSpec((N_VEC_ROWS, LRU_WIDTH), lambda s: (0, 0))],
        out_shape=[jax.ShapeDtypeStruct((tp, LRU_WIDTH), BF16), jax.ShapeDtypeStruct((tp, LRU_WIDTH), BF16),
                   jax.ShapeDtypeStruct((tp, ATTN_WIDTH), BF16), jax.ShapeDtypeStruct((D_MODEL, D_MODEL), F32),
                   jax.ShapeDtypeStruct((LRU_WIDTH, LRU_WIDTH), F32), jax.ShapeDtypeStruct((LRU_WIDTH, LRU_WIDTH), F32),
                   jax.ShapeDtypeStruct((N_VEC_ROWS, LRU_WIDTH), F32)],
        scratch_shapes=[pltpu.VMEM((8, LRU_WIDTH), F32)] * 4,
        compiler_params=_params("arbitrary"),
    )(dmix, w_out, attn, rec, xr, xr, yr, hr, hr, *small)


def _attn_bwd_tile(tp):
    return _wgrad_row_tile(tp)


def _attn_bwd(qkv, dattn, probs, sink_probs, token):
    tp = qkv.shape[0]
    tr = _attn_bwd_tile(tp)
    qb, nt = tr // BLOCK, tp // tr
    n_groups = N_KV

    def body(p_ref, ps_ref, q_ref, kp_ref, kc_ref, vp_ref, vc_ref, do_ref, _, dq_ref, dkv_ref, ex_ref, ds_ref, dsink):
        t = pl.program_id(0)

        @pl.when(t == 0)
        def _():
            dsink[...] = jnp.zeros_like(dsink)

        k_all = jnp.concatenate([kp_ref[...], kc_ref[...]], axis=0)
        v_all = jnp.concatenate([vp_ref[...], vc_ref[...]], axis=0)
        tail = None
        for i in range(qb):
            rows = slice(i * BLOCK, (i + 1) * BLOCK)
            qt = (q_ref[rows].astype(F32) * _QSCALE).T
            dot = do_ref[rows].astype(F32).T
            k2, v2 = k_all[i * BLOCK:(i + 2) * BLOCK], v_all[i * BLOCK:(i + 2) * BLOCK]
            dqs, dks, dvs = [], [], []
            for g in range(n_groups):
                cols = slice(g * HEAD_DIM, (g + 1) * HEAD_DIM)
                k_g, v_g = k2[:, cols], v2[:, cols]
                qgt, dogt = _heads_t(qt, g), _heads_t(dot, g)
                pb = p_ref[i, g]
                p = pb.astype(F32)
                dpt = _dot(v_g, dogt)
                delta = jnp.sum(p * dpt, axis=0, keepdims=True)
                dst = (p * (dpt - delta)).astype(BF16)
                dqs.append(_dot_tn(k_g, dst) * _QSCALE)
                dks.append(_dot_nt(qgt, dst))
                dvs.append(_dot_nt(dogt, pb))
                dsink[g:g + 1] -= ps_ref[i, g:g + 1] * delta
            dq_ref[rows] = _from_heads_t(dqs).astype(BF16)
            dkv = jnp.concatenate([jnp.concatenate(dks, axis=0).T, jnp.concatenate(dvs, axis=0).T], axis=1)
            if i == 0:
                ex_ref[0] = dkv[:BLOCK]
            else:
                dkv_ref[(i - 1) * BLOCK:i * BLOCK] = (tail + dkv[:BLOCK]).astype(BF16)
            tail = dkv[BLOCK:]
        dkv_ref[(qb - 1) * BLOCK:] = tail.astype(BF16)

        @pl.when(t == nt - 1)
        def _():
            lane = lax.broadcasted_iota(jnp.int32, (1, ATTN_HEADS), 1)
            acc = jnp.zeros((1, ATTN_HEADS), F32)
            for h in range(ATTN_HEADS):
                g, hh = divmod(h, GQA_GROUP)
                acc = acc + jnp.where(lane == h, jnp.sum(dsink[g:g + 1, hh * BLOCK:(hh + 1) * BLOCK]), 0.0)
            ds_ref[...] = acc

    cur = lambda w: pl.BlockSpec((tr, w), lambda t: (t, 0))
    return pl.pallas_call(
        body, name="attn_bwd", grid=(nt,),
        in_specs=_prob_specs(qb) + [cur(ATTN_WIDTH)] + _kv_specs(tr)
        + [cur(ATTN_WIDTH), pl.BlockSpec(token.shape, lambda t: (0, 0))],
        out_specs=[cur(ATTN_WIDTH), cur(2 * KV_WIDTH), pl.BlockSpec((1, BLOCK, 2 * KV_WIDTH), lambda t: (t, 0, 0)),
                   pl.BlockSpec((1, ATTN_HEADS), lambda t: (0, 0))],
        out_shape=[jax.ShapeDtypeStruct((tp, ATTN_WIDTH), BF16), jax.ShapeDtypeStruct((tp, 2 * KV_WIDTH), BF16),
                   jax.ShapeDtypeStruct((nt, BLOCK, 2 * KV_WIDTH), F32), jax.ShapeDtypeStruct((1, ATTN_HEADS), F32)],
        scratch_shapes=[pltpu.VMEM((n_groups, GROUP_ROWS), F32)],
        compiler_params=_params("arbitrary"),
    )(probs, sink_probs, qkv, qkv, qkv, qkv, qkv, dattn, token)


def _fix_dkv(dkv, dkv_extra):
    tp = dkv.shape[0]
    tr = _attn_bwd_tile(tp)
    nt, qb = tp // tr, tr // BLOCK
    if nt == 1:
        return dkv

    def body(d_ref, ex_ref, o_ref):
        o_ref[...] = (d_ref[...].astype(F32) + ex_ref[0]).astype(BF16)

    last = pl.BlockSpec((BLOCK, 2 * KV_WIDTH), lambda t: (t * qb + qb - 1, 0))
    return pl.pallas_call(
        body, name="fix_dkv", grid=(nt - 1,),
        in_specs=[last, pl.BlockSpec((1, BLOCK, 2 * KV_WIDTH), lambda t: (t + 1, 0, 0))],
        out_specs=last, out_shape=jax.ShapeDtypeStruct(dkv.shape, dkv.dtype),
        input_output_aliases={0: 0}, compiler_params=_params("parallel"),
    )(dkv, dkv_extra)


def _inproj_wgrad(dq, dkv, dxr, dyr, u0):
    tp = dq.shape[0]
    tr = _wgrad_row_tile(tp)

    def body(dq_ref, dkv_ref, dxr_ref, dyr_ref, u_ref, dw_ref):
        i = pl.program_id(0)

        def product():
            dz = jnp.concatenate([dq_ref[...], dkv_ref[...], dxr_ref[...], dyr_ref[...]], axis=1)
            return _dot_tn(dz, u_ref[...])

        @pl.when(i == 0)
        def _():
            dw_ref[...] = product()

        @pl.when(i > 0)
        def _():
            dw_ref[...] += product()

    row = lambda w: pl.BlockSpec((tr, w), lambda i: (i, 0))
    return pl.pallas_call(
        body, name="inproj_wgrad", grid=(tp // tr,),
        in_specs=[row(ATTN_WIDTH), row(2 * KV_WIDTH), row(LRU_WIDTH), row(LRU_WIDTH), row(D_MODEL)],
        out_specs=pl.BlockSpec((IN_WIDTH, D_MODEL), lambda i: (0, 0)),
        out_shape=jax.ShapeDtypeStruct((IN_WIDTH, D_MODEL), F32),
        compiler_params=_params("arbitrary"),
    )(dq, dkv, dxr, dyr, u0)


def _inproj_dgrad(dq, dkv, dxr, dyr, w_in, head, x, dh1, g, token):
    tp = dq.shape[0]
    tr = _row_tile(tp)
    nt, qb = tp // tr, tr // BLOCK

    def body(*refs):
        dq_ref, dkv_ref, dxr_ref, dyr_ref, w_ref, head_ref = refs[:6]
        pieces = refs[6:6 + qb]
        dh1_ref, g_ref, _, gx_ref, dhead_ref, dg_ref, buf, sems = refs[6 + qb:]
        i = pl.program_id(0)
        slot = i % 2

        def out_copy(step, at):
            return pltpu.make_async_copy(buf.at[at], gx_ref.at[pl.ds(step * tr - BLOCK, tr)], sems.at[at])

        dz = jnp.concatenate([dq_ref[...], dkv_ref[...], dxr_ref[...], dyr_ref[...]], axis=1)
        du = _dot(dz, w_ref[...])
        hhat, rs = _rms(_seq_tile(head_ref[...], pieces, i))
        dx, dg = _rms_bwd(hhat, rs, g_ref[...], du)
        dh0 = dh1_ref[...] + dx

        @pl.when(i >= 3)
        def _():
            out_copy(i - 2, slot).wait()

        buf[slot] = dh0

        @pl.when(i == 0)
        def _():
            dg_ref[...] = dg
            dhead_ref[...] = dh0[:BLOCK]
            if tr > BLOCK:
                first = pltpu.make_async_copy(buf.at[0, pl.ds(BLOCK, tr - BLOCK)], gx_ref.at[pl.ds(0, tr - BLOCK)],
                                              sems.at[0])
                first.start()
                first.wait()

        @pl.when(i >= 1)
        def _():
            dg_ref[...] += dg
            out_copy(i, slot).start()

        @pl.when(i == nt - 1)
        def _():
            if nt >= 3:
                out_copy(nt - 2, (nt - 2) % 2).wait()
            if nt >= 2:
                out_copy(nt - 1, (nt - 1) % 2).wait()

    row = lambda w: pl.BlockSpec((tr, w), lambda i: (i, 0))
    full = lambda shape: pl.BlockSpec(shape, lambda i: (0,) * len(shape))
    return pl.pallas_call(
        body, name="inproj_dgrad", grid=(tp // tr,),
        in_specs=[row(ATTN_WIDTH), row(2 * KV_WIDTH), row(LRU_WIDTH), row(LRU_WIDTH), full(w_in.shape),
                  full(head.shape)] + _seq_specs(tr) + [row(D_MODEL), full(g.shape), full(token.shape)],
        out_specs=[pl.BlockSpec(memory_space=pl.ANY), full((BLOCK, D_MODEL)), full((1, D_MODEL))],
        out_shape=[jax.ShapeDtypeStruct(x.shape, F32), jax.ShapeDtypeStruct((BLOCK, D_MODEL), F32),
                   jax.ShapeDtypeStruct((1, D_MODEL), F32)],
        scratch_shapes=[pltpu.VMEM((2, tr, D_MODEL), F32), pltpu.SemaphoreType.DMA((2,))],
        compiler_params=_params("arbitrary"),
    )(dq, dkv, dxr, dyr, w_in, head, *([x] * qb), dh1, g, token)


def _dense_block_diag(w):
    eye = jnp.eye(LRU_BLOCKS, dtype=w.dtype)
    return (w[:, :, None, :] * eye[:, None, :, None]).reshape(LRU_WIDTH, LRU_WIDTH)


def _diag_blocks(dense):
    d4 = dense.reshape(LRU_BLOCKS, LRU_BLOCK, LRU_BLOCKS, LRU_BLOCK)
    return jnp.stack([d4[n, :, n, :] for n in range(LRU_BLOCKS)])


def _local_step(head, x, tgt, g_pre_mix, w_in, conv_w, conv_b, w_a, b_a, w_x, b_x, lam, sinks, g_post_mix,
                g_pre_ffn, g_post_ffn, late_weights, on_ffn_grads, on_outproj_bwd, on_mixer_grads, token):
    wa = _dense_block_diag(w_a).astype(BF16)
    wx = _dense_block_diag(w_x).astype(BF16)

    u0, qkv, xr, yr, hr, rec = _inproj_lru_fwd(head, x, g_pre_mix, w_in, conv_w, conv_b, wa, b_a, wx, b_x, lam, token)
    attn, probs, sink_probs = _attn_fwd(qkv, sinks)
    w_out, ffn_weights = late_weights([attn, rec])
    mix, h1, u1 = _outproj_fwd(attn, rec, w_out, head, x, g_post_mix, g_pre_ffn)
    w1, w2 = ffn_weights([u1])
    r1, dy, df2, loss, dg_post_ffn = _ffn_fwd(u1, w1, w2, h1, tgt, g_post_ffn)

    da1, dh1, dmix, dg_pre_ffn, dg_post_mix = _ffn_bwd_data(df2, r1, w1, w2, dy, h1, mix, g_pre_ffn, g_post_mix)
    dw1, dw2 = _ffn_bwd_weights(u1, da1, r1, df2)
    token2 = on_ffn_grads(dw1, dw2)
    dxr, dyr, dattn, dw_out, dwa, dwx, vec = _outproj_lru_bwd(dmix, w_out, attn, rec, xr, yr, hr, conv_w, conv_b,
                                                              wa, b_a, wx, b_x, lam, token2)
    token3 = on_outproj_bwd(dattn)
    dq, dkv, dkv_extra, dsinks = _attn_bwd(qkv, dattn, probs, sink_probs, token3)
    dkv = _fix_dkv(dkv, dkv_extra)
    dw_in = _inproj_wgrad(dq, dkv, dxr, dyr, u0)
    token4 = on_mixer_grads(dw_in, dw_out)
    dx, dhead, dg_pre_mix = _inproj_dgrad(dq, dkv, dxr, dyr, w_in, head, x, dh1, g_pre_mix, token4)

    grads = dict(
        g_pre_mix=dg_pre_mix, conv_w=vec[0:4], conv_b=vec[4:5], w_a=_diag_blocks(dwa), b_a=vec[5:6],
        w_x=_diag_blocks(dwx), b_x=vec[6:7], lru_lambda=vec[7:8], attn_sinks=dsinks,
        g_post_mix=dg_post_mix, g_pre_ffn=dg_pre_ffn, g_post_ffn=dg_post_ffn)
    return loss, dx, dhead, grads


HBM = pl.BlockSpec(memory_space=pltpu.HBM)


def _mesh_pos():
    return lax.axis_index("x"), lax.axis_index("y"), lax.axis_index("c")


def _other_chips(x, y):
    return [(1 - x, y), (x, 1 - y), (1 - x, 1 - y)]


def _remote(src, dst, send_sem, recv_sem, to):
    return pltpu.make_async_remote_copy(src_ref=src, dst_ref=dst, send_sem=send_sem, recv_sem=recv_sem,
                                        device_id=to, device_id_type=MESH)


def _gather_weights(shards, lands, tiny, tiny_land):
    nbig = len(shards)

    def body(*refs):
        srcs, tiny_src = refs[:nbig], refs[nbig]
        outs, tiny_out = refs[2 * nbig + 2:3 * nbig + 2], refs[3 * nbig + 2]
        ici_send, ici_recv, d2d_send, d2d_recv, tiny_send, tiny_recv = refs[3 * nbig + 3:]
        x, y, c = _mesh_pos()
        me = 2 * x + y
        chips = _other_chips(x, y)
        sibling = (x, y, 1 - c)
        sends = []
        for w, (src, out) in enumerate(zip(srcs, outs)):
            hr = src.shape[0] // 2
            for j, chip in enumerate(chips):
                k = 3 * w + j
                cp = _remote(src.at[pl.ds(c * hr, hr)], out.at[me, pl.ds(c * hr, hr)],
                             ici_send.at[k], ici_recv.at[k], (*chip, c))
                cp.start()
                sends.append(cp)
        for j, chip in enumerate(chips):
            cp = _remote(tiny_src, tiny_out.at[me], tiny_send.at[j], tiny_recv.at[j], (*chip, c))
            cp.start()
            sends.append(cp)
        for w, (src, out) in enumerate(zip(srcs, outs)):
            hr = src.shape[0] // 2
            for j, (px, py) in enumerate(chips):
                k = 3 * w + j
                landed = out.at[2 * px + py, pl.ds(c * hr, hr)]
                _remote(landed, landed, ici_send.at[k], ici_recv.at[k], sibling).wait_recv()
                cp = _remote(landed, landed, d2d_send.at[k], d2d_recv.at[k], sibling)
                cp.start()
                sends.append(cp)
        for w, (src, out) in enumerate(zip(srcs, outs)):
            hr = src.shape[0] // 2
            for j, (px, py) in enumerate(chips):
                k = 3 * w + j
                other = out.at[2 * px + py, pl.ds((1 - c) * hr, hr)]
                _remote(other, other, d2d_send.at[k], d2d_recv.at[k], sibling).wait_recv()
        for j, (px, py) in enumerate(chips):
            blk = tiny_out.at[2 * px + py]
            _remote(blk, blk, tiny_send.at[j], tiny_recv.at[j], sibling).wait_recv()
        for cp in sends:
            cp.wait_send()

    out_shape = [jax.ShapeDtypeStruct(l.shape, l.dtype) for l in list(lands) + [tiny_land]]
    n = 3 * nbig
    return pl.pallas_call(
        body, name="gather_weights", out_shape=out_shape,
        in_specs=[HBM] * (2 * nbig + 2), out_specs=[HBM] * (nbig + 1),
        input_output_aliases={nbig + 1 + i: i for i in range(nbig + 1)},
        scratch_shapes=[pltpu.SemaphoreType.DMA((n,)),
                        pltpu.SemaphoreType.DMA((n,)), pltpu.SemaphoreType.DMA((n,)), pltpu.SemaphoreType.DMA((n,)),
                        pltpu.SemaphoreType.DMA((3,)), pltpu.SemaphoreType.DMA((3,))],
    )(*shards, tiny, *lands, tiny_land)


def _prep_shard(w, me):
    rows, cols = w.shape
    tr = _elementwise_tile(rows)

    def body(me_ref, w_ref, s_ref, l_ref):
        b = w_ref[...].astype(BF16)
        s_ref[...] = b
        l_ref[0] = b

    return pl.pallas_call(
        body, name="prep_shard",
        grid_spec=pltpu.PrefetchScalarGridSpec(
            num_scalar_prefetch=1, grid=(rows // tr,),
            in_specs=[pl.BlockSpec((tr, cols), lambda i, me_ref: (i, 0))],
            out_specs=[pl.BlockSpec((tr, cols), lambda i, me_ref: (i, 0)),
                       pl.BlockSpec((1, tr, cols), lambda i, me_ref: (me_ref[0], i, 0))]),
        out_shape=[jax.ShapeDtypeStruct((rows, cols), BF16), jax.ShapeDtypeStruct((N_CHIPS, rows, cols), BF16)],
        compiler_params=_params("parallel"),
    )(me, w)


def _prep_tiny(tiny, me, slots=N_CHIPS):
    def body(me_ref, t_ref, l_ref):
        l_ref[0] = t_ref[...]

    return pl.pallas_call(
        body, name="prep_tiny",
        grid_spec=pltpu.PrefetchScalarGridSpec(
            num_scalar_prefetch=1, grid=(1,),
            in_specs=[pl.BlockSpec(tiny.shape, lambda i, me_ref: (0, 0))],
            out_specs=pl.BlockSpec((1,) + tiny.shape, lambda i, me_ref: (me_ref[0], 0, 0))),
        out_shape=jax.ShapeDtypeStruct((slots,) + tiny.shape, tiny.dtype),
    )(me, tiny)


N_DEV = 8


def _sibling_exchange(parts, token):
    def body(*refs):
        n = len(parts)
        srcs, outs, send_sems, recv_sems = refs[:n], refs[n + 1:2 * n + 1], refs[2 * n + 1], refs[2 * n + 2]
        x, y, c = _mesh_pos()
        sibling = (x, y, 1 - c)
        cps = []
        for w, (src, out) in enumerate(zip(srcs, outs)):
            hr = src.shape[1] // 2
            cp = _remote(src.at[:, pl.ds((1 - c) * hr, hr)], out, send_sems.at[w], recv_sems.at[w], sibling)
            cp.start()
            cps.append(cp)
        for cp in cps:
            cp.wait()

    n = len(parts)
    return pl.pallas_call(
        body, name="sibling_exchange",
        out_shape=[jax.ShapeDtypeStruct((p.shape[0], p.shape[1] // 2, p.shape[2]), p.dtype) for p in parts],
        in_specs=[HBM] * n + [pl.BlockSpec(memory_space=pl.ANY)], out_specs=[HBM] * n,
        scratch_shapes=[pltpu.SemaphoreType.DMA((n,)), pltpu.SemaphoreType.DMA((n,))],
    )(*parts, token)


def _chip_presum(part, from_sibling, pos):
    _, hr, cols = from_sibling.shape
    tr = _elementwise_tile(hr)
    steps = hr // tr

    def body(pos_ref, a_ref, b_ref, o_ref, land_ref):
        s = (a_ref[...] + b_ref[...]).astype(BF16)
        o_ref[...] = s

        @pl.when(pl.program_id(1) == pos_ref[1])
        def _():
            land_ref[...] = s

    return pl.pallas_call(
        body, name="chip_presum",
        grid_spec=pltpu.PrefetchScalarGridSpec(
            num_scalar_prefetch=1, grid=(steps, N_CHIPS),
            in_specs=[pl.BlockSpec((1, tr, cols), lambda i, j, p: (j, p[0] * steps + i, 0)),
                      pl.BlockSpec((1, tr, cols), lambda i, j, p: (j, i, 0))],
            out_specs=[pl.BlockSpec((1, tr, cols), lambda i, j, p: (j, i, 0)),
                       pl.BlockSpec((1, tr, cols), lambda i, j, p: (p[1], p[0] * steps + i, 0))]),
        out_shape=[jax.ShapeDtypeStruct(from_sibling.shape, BF16),
                   jax.ShapeDtypeStruct((N_CHIPS, 2 * hr, cols), BF16)],
        compiler_params=_params("arbitrary", "arbitrary"),
    )(pos, part, from_sibling)


def _scatter_partials(cparts, lands, done_cparts=(), done_lands=()):
    n_new = len(cparts)
    nw = n_new + len(done_cparts)

    def body(*refs):
        srcs = refs[:nw]
        outs = refs[2 * nw:3 * nw]
        own_send, own_recv, ici_send, ici_recv, d2d_send, d2d_recv = refs[3 * nw:]
        x, y, c = _mesh_pos()
        me = 2 * x + y
        chips = _other_chips(x, y)
        sibling = (x, y, 1 - c)
        sends = []
        for w in list(range(n_new, nw)) + list(range(n_new)):
            src, out = srcs[w], outs[w]
            hr = src.shape[1]
            mine = out.at[me, pl.ds(c * hr, hr)]
            cp = _remote(src.at[me], mine, own_send.at[w], own_recv.at[w], sibling)
            cp.start()
            sends.append(cp)
            for j, (px, py) in enumerate(chips):
                if w >= n_new:
                    break
                k = 3 * w + j
                cp = _remote(src.at[2 * px + py], mine, ici_send.at[k], ici_recv.at[k], (px, py, c))
                cp.start()
                sends.append(cp)
        for w in list(range(n_new, nw)) + list(range(n_new)):
            src, out = srcs[w], outs[w]
            hr = src.shape[1]
            for j, (px, py) in enumerate(chips):
                k = 3 * w + j
                landed = out.at[2 * px + py, pl.ds(c * hr, hr)]
                if w < n_new:
                    _remote(landed, landed, ici_send.at[k], ici_recv.at[k], sibling).wait_recv()
                cp = _remote(landed, landed, d2d_send.at[k], d2d_recv.at[k], sibling)
                cp.start()
                sends.append(cp)
        for w, (src, out) in enumerate(zip(srcs, outs)):
            hr = src.shape[1]
            other = out.at[me, pl.ds((1 - c) * hr, hr)]
            _remote(other, other, own_send.at[w], own_recv.at[w], sibling).wait_recv()
            for j, (px, py) in enumerate(chips):
                k = 3 * w + j
                other = out.at[2 * px + py, pl.ds((1 - c) * hr, hr)]
                _remote(other, other, d2d_send.at[k], d2d_recv.at[k], sibling).wait_recv()
        for cp in sends:
            cp.wait_send()

    n = 3 * nw
    dma = pltpu.SemaphoreType.DMA
    every = list(cparts) + list(done_cparts)
    every_lands = list(lands) + list(done_lands)
    return pl.pallas_call(
        body, name="scatter_partials",
        out_shape=[jax.ShapeDtypeStruct(l.shape, l.dtype) for l in every_lands],
        in_specs=[HBM] * (2 * nw), out_specs=[HBM] * nw,
        input_output_aliases={nw + i: i for i in range(nw)},
        scratch_shapes=[dma((nw,)), dma((nw,)), dma((n,)), dma((n,)), dma((n,)), dma((n,))],
    )(*every, *every_lands)


SEM = pl.BlockSpec(memory_space=pltpu.SEMAPHORE)
SPLIT_COPY = pltpu.CompilerParams(has_side_effects=pltpu.SideEffectType.DATAFLOW_SIDE_EFFECTING)


def _hbm(a):
    return pltpu.with_memory_space_constraint(a, pltpu.HBM)


def _gather_copies(srcs, lands, send_sems, recv_sems):
    x, y, c = _mesh_pos()
    me = 2 * x + y
    sends, recvs = [], []
    for w, (src, land) in enumerate(zip(srcs, lands)):
        hr = src.shape[0] // 2
        for j, (px, py) in enumerate(_other_chips(x, y)):
            k = 3 * w + j
            sends.append(_remote(src.at[pl.ds(c * hr, hr)], land.at[me, pl.ds(c * hr, hr)],
                                 send_sems.at[k], recv_sems.at[k], (px, py, c)))
            got = land.at[2 * px + py, pl.ds(c * hr, hr)]
            recvs.append(_remote(got, got, send_sems.at[k], recv_sems.at[k], (px, py, c)))
    return sends, recvs


def _scatter_copies(srcs, lands, send_sems, recv_sems):
    x, y, c = _mesh_pos()
    me = 2 * x + y
    sends, recvs = [], []
    for w, (src, land) in enumerate(zip(srcs, lands)):
        hr = src.shape[1]
        for j, (px, py) in enumerate(_other_chips(x, y)):
            k = 3 * w + j
            sends.append(_remote(src.at[2 * px + py], land.at[me, pl.ds(c * hr, hr)],
                                 send_sems.at[k], recv_sems.at[k], (px, py, c)))
            got = land.at[2 * px + py, pl.ds(c * hr, hr)]
            recvs.append(_remote(got, got, send_sems.at[k], recv_sems.at[k], (px, py, c)))
    return sends, recvs


def _sibling_copies(srcs, lands, send_sems, recv_sems):
    x, y, c = _mesh_pos()
    sibling = (x, y, 1 - c)
    sends, recvs = [], []
    for w, (src, land) in enumerate(zip(srcs, lands)):
        hr = src.shape[1] // 2
        sends.append(_remote(src.at[:, pl.ds((1 - c) * hr, hr)], land, send_sems.at[w], recv_sems.at[w], sibling))
        recvs.append(_remote(land, land, send_sems.at[w], recv_sems.at[w], sibling))
    return sends, recvs


def _inchip_copies(srcs, lands, send_sems, recv_sems):
    x, y, c = _mesh_pos()
    me = 2 * x + y
    sibling = (x, y, 1 - c)
    sends, recvs = [], []
    for w, (src, land) in enumerate(zip(srcs, lands)):
        hr = src.shape[1]
        mine, other = pl.ds(c * hr, hr), pl.ds((1 - c) * hr, hr)
        blocks = [(me, src.at[me])] + [(2 * px + py, None) for px, py in _other_chips(x, y)]
        for j, (blk, own_src) in enumerate(blocks):
            k = 4 * w + j
            landed = land.at[blk, mine]
            sends.append(_remote(landed if own_src is None else own_src, landed, send_sems.at[k], recv_sems.at[k], sibling))
            got = land.at[blk, other]
            recvs.append(_remote(got, got, send_sems.at[k], recv_sems.at[k], sibling))
    return sends, recvs


def _all_peers_copies(srcs, lands, send_sems, recv_sems):
    x, y, c = _mesh_pos()
    (src,), (land,) = srcs, lands
    flip = lambda v, bit: 1 - v if bit else v
    sends, recvs = [], []
    for k in range(N_DEV - 1):
        px, py, pc = flip(x, (k + 1) & 4), flip(y, (k + 1) & 2), flip(c, (k + 1) & 1)
        sends.append(_remote(src, land.at[4 * x + 2 * y + c], send_sems.at[k], recv_sems.at[k], (px, py, pc)))
        got = land.at[4 * px + 2 * py + pc]
        recvs.append(_remote(got, got, send_sems.at[k], recv_sems.at[k], (px, py, pc)))
    return sends, recvs


def _split_start(name, copies_of, srcs, land_shapes, n_copies=None):
    n = len(srcs)
    k = 3 * n if n_copies is None else n_copies

    def body(*refs):
        src_refs, land_refs = refs[:n], refs[n:2 * n]
        send_sems, recv_sems = refs[2 * n], refs[2 * n + 1]
        token = refs[-1]
        sends, _ = copies_of(src_refs, land_refs, send_sems, recv_sems)
        for cp in sends:
            cp.start()
        token[...] = jnp.zeros_like(token)

    lands = [_hbm(s) for s in land_shapes]
    dma = pltpu.SemaphoreType.DMA
    res = pl.pallas_call(
        body, name=name,
        out_shape=(dma((k,)), dma((k,)), *[pltpu.HBM(s.shape, s.dtype) for s in srcs],
                   *[pltpu.HBM(s.shape, s.dtype) for s in land_shapes], jax.ShapeDtypeStruct((8, 128), F32)),
        in_specs=[HBM] * (2 * n),
        out_specs=(SEM, SEM, *([HBM] * (2 * n)), pl.BlockSpec(memory_space=pltpu.VMEM)),
        input_output_aliases={i: 2 + i for i in range(2 * n)},
        compiler_params=SPLIT_COPY,
    )(*[_hbm(s) for s in srcs], *lands)
    return res[0], res[1], list(res[2:2 + n]), list(res[2 + n:2 + 2 * n]), res[-1]


def _split_wait(name, copies_of, send_sems, recv_sems, srcs, lands, after):
    n = len(srcs)

    def body(*refs):
        src_refs, land_refs = refs[:n], refs[n:2 * n]
        sends, recvs = copies_of(src_refs, land_refs, refs[2 * n], refs[2 * n + 1])
        for cp in sends:
            cp.wait_send()
        for cp in recvs:
            cp.wait_recv()

    res = pl.pallas_call(
        body, name=name,
        out_shape=tuple(pltpu.HBM(s.shape, s.dtype) for s in list(srcs) + list(lands)),
        in_specs=[HBM] * (2 * n) + [SEM, SEM] + [pl.BlockSpec(memory_space=pl.ANY)] * len(after),
        out_specs=tuple([HBM] * (2 * n)),
        input_output_aliases={i: i for i in range(2 * n)},
        compiler_params=SPLIT_COPY,
    )(*srcs, *lands, send_sems, recv_sems, *after)
    return list(res[:n]), list(res[n:])


def _forward_copies(srcs, lands, send_sems, recv_sems):
    x, y, c = _mesh_pos()
    sibling = (x, y, 1 - c)
    sends, recvs = [], []
    for w, land in enumerate(lands):
        hr = land.shape[1] // 2
        for j, (px, py) in enumerate(_other_chips(x, y)):
            k = 3 * w + j
            landed = land.at[2 * px + py, pl.ds(c * hr, hr)]
            sends.append(_remote(landed, landed, send_sems.at[k], recv_sems.at[k], sibling))
            other = land.at[2 * px + py, pl.ds((1 - c) * hr, hr)]
            recvs.append(_remote(other, other, send_sems.at[k], recv_sems.at[k], sibling))
    return sends, recvs


def _gather_finish(lands, n_forward):
    n = len(lands)

    def body(*refs):
        outs = refs[n:n + n_forward]
        d2d_send, d2d_recv = refs[2 * n:]
        x, y, c = _mesh_pos()
        chips = _other_chips(x, y)
        sibling = (x, y, 1 - c)
        sends = []
        for w, out in enumerate(outs):
            hr = out.shape[1] // 2
            for j, (px, py) in enumerate(chips):
                landed = out.at[2 * px + py, pl.ds(c * hr, hr)]
                cp = _remote(landed, landed, d2d_send.at[3 * w + j], d2d_recv.at[3 * w + j], sibling)
                cp.start()
                sends.append(cp)
        for w, out in enumerate(outs):
            hr = out.shape[1] // 2
            for j, (px, py) in enumerate(chips):
                other = out.at[2 * px + py, pl.ds((1 - c) * hr, hr)]
                _remote(other, other, d2d_send.at[3 * w + j], d2d_recv.at[3 * w + j], sibling).wait_recv()
        for cp in sends:
            cp.wait_send()

    dma = pltpu.SemaphoreType.DMA
    return pl.pallas_call(
        body, name="gather_finish",
        out_shape=[jax.ShapeDtypeStruct(l.shape, l.dtype) for l in lands],
        in_specs=[HBM] * n, out_specs=[HBM] * n,
        input_output_aliases={i: i for i in range(n)},
        scratch_shapes=[dma((3 * n,)), dma((3 * n,))],
    )(*lands)


def _adamw(w, g, m, v):
    m = ADAM_B1 * m + (1.0 - ADAM_B1) * g
    v = ADAM_B2 * v + (1.0 - ADAM_B2) * (g * g)
    m_hat = m / (1.0 - ADAM_B1 ** ADAM_STEP)
    v_hat = v / (1.0 - ADAM_B2 ** ADAM_STEP)
    delta = -ADAM_LR * (m_hat / (jnp.sqrt(v_hat) + ADAM_EPS) + ADAM_WD * w)
    return delta, m, v


def _adamw_big(partials, w, m, v):
    rows, cols = w.shape
    tr = _elementwise_tile(rows)

    def body(p_ref, w_ref, m_ref, v_ref, g_ref, d_ref, m2_ref, v2_ref):
        g = ((p_ref[0].astype(F32) + p_ref[1].astype(F32)) + p_ref[2].astype(F32)) + p_ref[3].astype(F32)
        g_ref[...] = g
        d_ref[...], m2_ref[...], v2_ref[...] = _adamw(w_ref[...], g, m_ref[...], v_ref[...])

    blk = pl.BlockSpec((tr, cols), lambda i: (i, 0))
    return pl.pallas_call(
        body, name="adamw_big", grid=(rows // tr,),
        in_specs=[pl.BlockSpec((N_CHIPS, tr, cols), lambda i: (0, i, 0)), blk, blk, blk],
        out_specs=[blk] * 4, out_shape=[jax.ShapeDtypeStruct((rows, cols), F32)] * 4,
        compiler_params=_params("parallel"),
    )(partials, w, m, v)


def _sum_devices(gathered, rows):
    cols = gathered.shape[1]

    def body(g_ref, o_ref):
        acc = g_ref[0:rows]
        for d in range(1, N_DEV):
            acc = acc + g_ref[d * rows:(d + 1) * rows]
        o_ref[...] = acc

    return pl.pallas_call(
        body, name="sum_devices", out_shape=jax.ShapeDtypeStruct((rows, cols), F32),
        in_specs=[pl.BlockSpec(memory_space=pltpu.VMEM)], out_specs=pl.BlockSpec(memory_space=pltpu.VMEM),
        compiler_params=pltpu.CompilerParams(vmem_limit_bytes=VMEM_LIMIT_V7X),
    )(gathered)


def _adamw_small(quads):
    n = len(quads)

    def body(*refs):
        ins, outs = refs[:4 * n], refs[4 * n:]
        for t in range(n):
            w, g, m, v = (r[...] for r in ins[4 * t:4 * t + 4])
            outs[3 * t][...], outs[3 * t + 1][...], outs[3 * t + 2][...] = _adamw(w, g, m, v)

    flat = [a for q in quads for a in q]
    vm = pl.BlockSpec(memory_space=pltpu.VMEM)
    res = pl.pallas_call(
        body, name="adamw_small",
        out_shape=[jax.ShapeDtypeStruct(q[0].shape, F32) for q in quads for _ in range(3)],
        in_specs=[vm] * (4 * n), out_specs=[vm] * (3 * n),
    )(*flat)
    return [tuple(res[3 * t:3 * t + 3]) for t in range(n)]


SMALL_PACK_ROWS = 96
META_COLS = D_MODEL // N_CHIPS
CONV_COLS = LRU_WIDTH // N_CHIPS
_WEIGHTS = ['meta_tokens', 'g_pre_mix', 'w_in', 'conv_w', 'conv_b', 'w_a', 'b_a', 'w_x', 'b_x', 'lru_lambda',
            'attn_sinks', 'w_out', 'g_post_mix', 'g_pre_ffn', 'w_ff1', 'w_ff2', 'g_post_ffn']
_BIG = ['w_in', 'w_out', 'w_ff1', 'w_ff2']


def _pack_small(dmeta, g, loss):
    z = lambda r, c: jnp.zeros((r, c), F32)
    rows = [
        dmeta,
        g['g_pre_mix'], g['g_post_mix'], g['g_pre_ffn'], g['g_post_ffn'],
        jnp.concatenate([g['conv_w'], z(4, 512)], axis=1),
        jnp.concatenate([g['conv_b'], g['b_a']], axis=1),
        jnp.concatenate([g['b_x'], g['lru_lambda']], axis=1),
        jnp.concatenate([g['attn_sinks'], z(1, D_MODEL - ATTN_HEADS)], axis=1),
        jnp.concatenate([loss, z(1, D_MODEL - 1)], axis=1),
        z(4, D_MODEL),
        g['w_a'].reshape(32, D_MODEL), g['w_x'].reshape(32, D_MODEL),
    ]
    return jnp.concatenate(rows, axis=0)


def _unpack_small(s, chip):
    return dict(
        meta_tokens=lax.dynamic_slice(s[0:N_META], (0, chip * META_COLS), (N_META, META_COLS)),
        g_pre_mix=s[16:17], g_post_mix=s[17:18], g_pre_ffn=s[18:19], g_post_ffn=s[19:20],
        conv_w=lax.dynamic_slice(s[20:24], (0, chip * CONV_COLS), (4, CONV_COLS)).reshape(1, 4, CONV_COLS),
        conv_b=s[24:25, :512], b_a=s[24:25, 512:], b_x=s[25:26, :512], lru_lambda=s[25:26, 512:],
        attn_sinks=s[26:27, :ATTN_HEADS], loss=s[27, 0],
        w_a=s[32:64].reshape(1, LRU_BLOCKS, LRU_BLOCK, LRU_BLOCK),
        w_x=s[64:96].reshape(1, LRU_BLOCKS, LRU_BLOCK, LRU_BLOCK))


def _as2d(a):
    if a.ndim == 2:
        return a
    return a.reshape(-1, a.shape[-1])


def kernel(x, meta_tokens, g_pre_mix, w_in, conv_w, conv_b, w_a, b_a, w_x, b_x, lru_lambda, attn_sinks, w_out, g_post_mix, g_pre_ffn, w_ff1, w_ff2, g_post_ffn, loss_target, m_meta_tokens, m_g_pre_mix, m_w_in, m_conv_w, m_conv_b, m_w_a, m_b_a, m_w_x, m_b_x, m_lru_lambda, m_attn_sinks, m_w_out, m_g_post_mix, m_g_pre_ffn, m_w_ff1, m_w_ff2, m_g_post_ffn, v_meta_tokens, v_g_pre_mix, v_w_in, v_conv_w, v_conv_b, v_w_a, v_b_a, v_w_x, v_b_x, v_lru_lambda, v_attn_sinks, v_w_out, v_g_post_mix, v_g_pre_ffn, v_w_ff1, v_w_ff2, v_g_post_ffn):
    weights = dict(meta_tokens=meta_tokens, g_pre_mix=g_pre_mix, w_in=w_in, conv_w=conv_w, conv_b=conv_b, w_a=w_a,
                   b_a=b_a, w_x=w_x, b_x=b_x, lru_lambda=lru_lambda, attn_sinks=attn_sinks, w_out=w_out,
                   g_post_mix=g_post_mix, g_pre_ffn=g_pre_ffn, w_ff1=w_ff1, w_ff2=w_ff2, g_post_ffn=g_post_ffn)
    mom1 = dict(zip(_WEIGHTS, [m_meta_tokens, m_g_pre_mix, m_w_in, m_conv_w, m_conv_b, m_w_a, m_b_a, m_w_x, m_b_x,
                               m_lru_lambda, m_attn_sinks, m_w_out, m_g_post_mix, m_g_pre_ffn, m_w_ff1, m_w_ff2,
                               m_g_post_ffn]))
    mom2 = dict(zip(_WEIGHTS, [v_meta_tokens, v_g_pre_mix, v_w_in, v_conv_w, v_conv_b, v_w_a, v_b_a, v_w_x, v_b_x,
                               v_lru_lambda, v_attn_sinks, v_w_out, v_g_post_mix, v_g_pre_ffn, v_w_ff1, v_w_ff2,
                               v_g_post_ffn]))
    xi, yi, ci = _mesh_pos()
    chip = 2 * xi + yi

    tiny = jnp.concatenate([meta_tokens, jnp.pad(conv_w[0], ((0, 4), (0, 128)))], axis=0)
    chip_arr = jnp.reshape(chip, (1,)).astype(jnp.int32)
    big2d = lambda a, name: a[0].T if name == 'w_in' else a[0]
    shards, lands = zip(*[_prep_shard(big2d(weights[n], n), chip_arr) for n in _BIG])
    g_in, g_tiny = _gather_weights(shards[:1], lands[:1], tiny, _prep_tiny(tiny, chip_arr))
    w_in_full = g_in.reshape(IN_WIDTH, D_MODEL)
    meta_full = jnp.concatenate([g_tiny[j, :N_META] for j in range(N_CHIPS)], axis=1)
    conv_w_full = jnp.concatenate([g_tiny[j, N_META:N_META + 4, :128] for j in range(N_CHIPS)], axis=1)
    g_send, g_recv, late_thru, late_lands, token = _split_start(
        "gather_late_start", _gather_copies, shards[1:], lands[1:])

    def late_weights(after):
        thru, landed = _split_wait("gather_late_wait", _gather_copies, g_send, g_recv, late_thru, late_lands, after)
        g_out, *ffn_landed = _gather_finish(landed, 1)
        f_send, f_recv, (g_out, spare), f_lands, _ = _split_start(
            "gather_forward_start", _forward_copies, [g_out, thru[1]], ffn_landed)

        def ffn_weights(after):
            _, (g_f1, g_f2) = _split_wait("gather_forward_wait", _forward_copies, f_send, f_recv, [spare, thru[2]],
                                          f_lands, after)
            return g_f1, g_f2

        return g_out.reshape(D_MODEL, D_MODEL), ffn_weights

    pos = jnp.stack([ci, chip]).astype(jnp.int32)
    ffn = {}


    def on_ffn_grads(dw1, dw2):
        parts = [dw1, dw2]
        lands = [lax.empty((p.shape[0], p.shape[1] // 2, p.shape[2]), p.dtype) for p in parts]
        ffn['sib'] = _split_start("sibling_ffn_start", _sibling_copies, parts, lands, len(parts))
        return ffn['sib'][4]

    def on_outproj_bwd(dattn):
        send, recv, thru, lands, _ = ffn['sib']
        parts, from_sibling = _split_wait("sibling_ffn_wait", _sibling_copies, send, recv, thru, lands, [dattn])
        cparts_ffn, lands_ffn = zip(*[_chip_presum(p, r, pos) for p, r in zip(parts, from_sibling)])
        ffn['send'], ffn['recv'], ffn['thru'], ffn['lands'], token3 = _split_start(
            "scatter_ffn_start", _scatter_copies, cparts_ffn, lands_ffn)
        return token3

    def on_mixer_grads(dw_in, dw_out):
        parts = [dw_in.reshape(N_CHIPS, IN_WIDTH // N_CHIPS, D_MODEL),
                 dw_out.reshape(N_CHIPS, D_MODEL // N_CHIPS, D_MODEL)]
        cparts, lands = zip(*[_chip_presum(p, r, pos) for p, r in zip(parts, _sibling_exchange(parts, pos))])
        ffn['mixer'] = _split_start("scatter_mixer_start", _scatter_copies, cparts, lands)
        ffn_cparts, ffn_lands = _split_wait("scatter_ffn_wait", _scatter_copies, ffn['send'], ffn['recv'],
                                            ffn['thru'], ffn['lands'], [ffn['mixer'][4]])
        ffn['inchip'] = _split_start("inchip_ffn_start", _inchip_copies, ffn_cparts, ffn_lands, 4 * len(ffn_cparts))
        return ffn['inchip'][4]

    head = jnp.concatenate([jnp.zeros((PAD_ROWS, D_MODEL), F32), meta_full], axis=0)
    loss, dx, dhead, grads = _local_step(head, x[0], loss_target[0], g_pre_mix, w_in_full, conv_w_full, conv_b, w_a[0],
                                         b_a, w_x[0], b_x, lru_lambda, attn_sinks, g_post_mix, g_pre_ffn, g_post_ffn,
                                         late_weights, on_ffn_grads, on_outproj_bwd, on_mixer_grads, token)
    grad_x = dx[None]

    pack = _pack_small(dhead[PAD_ROWS:], grads, loss)
    dev = jnp.reshape(4 * xi + 2 * yi + ci, (1,)).astype(jnp.int32)
    s_send, s_recv, s_thru, s_lands, token5 = _split_start(
        "gather_small_start", _all_peers_copies, [pack], [_prep_tiny(pack, dev, N_DEV)], N_DEV - 1)

    send, recv, thru, lands, _ = ffn['mixer']
    mixer_cparts, mixer_lands = _split_wait("scatter_mixer_wait", _scatter_copies, send, recv, thru, lands, [token5])
    m_send, m_recv, m_thru, m_lands, token6 = _split_start(
        "inchip_mixer_start", _inchip_copies, mixer_cparts, mixer_lands, 4 * len(mixer_cparts))
    send, recv, thru, lands, _ = ffn['inchip']
    _, ffn_partials = _split_wait("inchip_ffn_wait", _inchip_copies, send, recv, thru, lands, [token6])

    g_out_d, delta, new_m, new_v = {}, {}, {}, {}

    def adamw_big(names, partials):
        for name, part in zip(names, partials):
            shp = weights[name].shape
            res = _adamw_big(part, big2d(weights[name], name), big2d(mom1[name], name), big2d(mom2[name], name))
            g_out_d[name], delta[name], new_m[name], new_v[name] = (big2d(r[None], name).reshape(shp) for r in res)

    adamw_big(_BIG[2:], ffn_partials)
    _, mixer_partials = _split_wait("inchip_mixer_wait", _inchip_copies, m_send, m_recv, m_thru, m_lands,
                                    [g_out_d[n] for n in _BIG[2:]])
    adamw_big(_BIG[:2], mixer_partials)

    _, (gathered,) = _split_wait("gather_small_wait", _all_peers_copies, s_send, s_recv, s_thru, s_lands,
                                 [g_out_d[n] for n in _BIG])
    small = _unpack_small(_sum_devices(gathered.reshape(N_DEV * SMALL_PACK_ROWS, D_MODEL), SMALL_PACK_ROWS), chip)
    loss = small['loss']
    small_names = [n for n in _WEIGHTS if n not in _BIG]
    quads = [(_as2d(weights[n]), _as2d(small[n]), _as2d(mom1[n]), _as2d(mom2[n])) for n in small_names]
    for name, (d, m2, v2) in zip(small_names, _adamw_small(quads)):
        shp = weights[name].shape
        g_out_d[name] = small[name].reshape(shp)
        delta[name], new_m[name], new_v[name] = d.reshape(shp), m2.reshape(shp), v2.reshape(shp)

    return (loss, grad_x, *[g_out_d[n] for n in _WEIGHTS], *[delta[n] for n in _WEIGHTS],
            *[new_m[n] for n in _WEIGHTS], *[new_v[n] for n in _WEIGHTS])
```

```python
import numpy as np
import jax
import jax.numpy as jnp
from jax import lax
from jax.experimental import pallas as pl
from jax.experimental.pallas import tpu as pltpu

F32 = jnp.float32
BF16 = jnp.bfloat16

D_MODEL = 1024
N_META = 16
BLOCK = 128
PAD_ROWS = BLOCK - N_META
HEAD_DIM = 64
ATTN_HEADS = 8
GQA_GROUP = 4
ATTN_WIDTH = 512
KV_WIDTH = 128
QKV_WIDTH = ATTN_WIDTH + 2 * KV_WIDTH
LRU_WIDTH = 512
LRU_BLOCKS = 8
LRU_BLOCK = 64
LRU_C = 8.0
IN_WIDTH = 1792
D_FF = 4096
N_CHIPS = 4
FF_CHUNK = D_FF // N_CHIPS
EPS = 1e-6
NEG = -1e30

ADAM_LR = 0.001
ADAM_B1 = 0.9
ADAM_B2 = 0.999
ADAM_EPS = 1e-08
ADAM_WD = 0.01
ADAM_STEP = 10

VMEM_LIMIT_V7X = 62 * 1024 * 1024
MESH = pl.DeviceIdType.MESH

NT = (((1,), (1,)), ((), ()))
TN = (((0,), (0,)), ((), ()))


def _row_tile(tp):
    return 640 if tp % 640 == 0 else BLOCK


def _elementwise_tile(rows):
    return 512 if rows % 512 == 0 else rows


def _wgrad_row_tile(tp):
    return 1664 if tp % 1664 == 0 else _row_tile(tp)


def _params(*sem):
    return pltpu.CompilerParams(dimension_semantics=sem, vmem_limit_bytes=VMEM_LIMIT_V7X)


def _dot(a, b):
    return jnp.dot(a, b, preferred_element_type=F32)


def _dot_nt(a, b):
    return lax.dot_general(a, b, NT, preferred_element_type=F32)


def _dot_tn(a, b):
    return lax.dot_general(a, b, TN, preferred_element_type=F32)


def _rms(x):
    rs = lax.rsqrt(jnp.mean(x * x, axis=-1, keepdims=True) + EPS)
    return x * rs, rs


def _rms_bwd(xhat, rs, g, dy):
    dyg = dy * g
    dx = rs * (dyg - xhat * jnp.mean(dyg * xhat, axis=-1, keepdims=True))
    dg = jnp.sum(dy * xhat, axis=0, keepdims=True)
    return dx, dg


def _gelu(x):
    k = 0.7978845608028654
    t = jnp.tanh(x * (k + (k * 0.044715) * (x * x)))
    return (0.5 * x) * (1.0 + t), t


def _gelu_grad(x, t):
    k = 0.7978845608028654
    return 0.5 * (1.0 + t) + 0.5 * x * (1.0 - t * t) * k * (1.0 + 3 * 0.044715 * x * x)


def _sigmoid(x):
    return 0.5 * jnp.tanh(0.5 * x) + 0.5


def _one_minus_exp2(y):
    t = jnp.tanh(y)
    return (-2.0 * t) / (1.0 - t)


def _softplus(x):
    return jnp.maximum(x, 0.0) + jnp.log1p(jnp.exp(-jnp.abs(x)))


def _seq_specs(tr, delay=0):
    qb = tr // BLOCK
    tile = lambda i: jnp.maximum(i - delay, 0)
    return [pl.BlockSpec((BLOCK, D_MODEL), lambda i, *_, s=s: (jnp.maximum(tile(i) * qb + s - 1, 0), 0))
            for s in range(qb)]


def _seq_tile(head, pieces, i):
    first = jnp.where(i == 0, head, pieces[0][...])
    return jnp.concatenate([first] + [p[...] for p in pieces[1:]], axis=0)


GROUP_ROWS = GQA_GROUP * BLOCK


def _attn_bias():
    j = np.arange(2 * BLOCK)[:, None]
    i = np.arange(BLOCK)[None, :]
    band = (j - i >= 1) & (j - i <= BLOCK)
    out = []
    for n in range(3):
        ok = band & ((n - 1) * BLOCK + j >= PAD_ROWS) if n < 2 else band
        out.append(np.tile(np.where(ok, 0.0, NEG).astype(np.float32), (1, GQA_GROUP)))
    return jnp.asarray(np.stack(out))


def _heads_t(at, g):
    heads = range(GQA_GROUP * g, GQA_GROUP * (g + 1))
    return jnp.concatenate([at[h * HEAD_DIM:(h + 1) * HEAD_DIM] for h in heads], axis=1).astype(BF16)


def _from_heads_t(groups):
    pairs = []
    for p in groups:
        for h in range(0, GQA_GROUP, 2):
            two = jnp.concatenate([p[:, h * BLOCK:(h + 1) * BLOCK], p[:, (h + 1) * BLOCK:(h + 2) * BLOCK]], axis=0)
            pairs.append(two.T)
    return jnp.concatenate(pairs, axis=1)


def _stack_heads(a, g):
    heads = range(GQA_GROUP * g, GQA_GROUP * (g + 1))
    return jnp.concatenate([a[:, h * HEAD_DIM:(h + 1) * HEAD_DIM] for h in heads], axis=0)


def _unstack_heads(groups):
    return jnp.concatenate([p[h * BLOCK:(h + 1) * BLOCK] for p in groups for h in range(GQA_GROUP)], axis=1)


def _attn_probs_t(k_g, qg, bias, sink_row):
    st = _dot_nt(k_g, qg) + bias
    m = jnp.maximum(jnp.max(st, axis=0, keepdims=True), sink_row)
    p = jnp.exp(st - m)
    es = jnp.exp(sink_row - m)
    inv = 1.0 / (jnp.sum(p, axis=0, keepdims=True) + es)
    return p * inv, es * inv


def _attn_consts(sinks):
    return jnp.repeat(sinks.reshape(ATTN_HEADS), BLOCK).reshape(ATTN_HEADS // GQA_GROUP, GROUP_ROWS), _attn_bias()


_SINK_SPEC = pl.BlockSpec((ATTN_HEADS // GQA_GROUP, GROUP_ROWS), lambda n: (0, 0))
_BIAS_SPEC = pl.BlockSpec((3, 2 * BLOCK, GROUP_ROWS), lambda n: (0, 0, 0))
_QSCALE = HEAD_DIM ** -0.5


def _kv_specs(tr):
    qb = tr // BLOCK
    prev = lambda col: pl.BlockSpec((BLOCK, KV_WIDTH), lambda t: (jnp.maximum(t * qb - 1, 0), col))
    cur = lambda col: pl.BlockSpec((tr, KV_WIDTH), lambda t: (t, col))
    return [prev(4), cur(4), prev(5), cur(5)]


def _block_bias(b_ref, t, qb, i):
    return b_ref[2] if i >= 2 else b_ref[jnp.minimum(t * qb + i, 2)]


N_KV = ATTN_HEADS // GQA_GROUP


def _prob_specs(qb):
    return [pl.BlockSpec((qb, N_KV, 2 * BLOCK, GROUP_ROWS), lambda t: (t, 0, 0, 0)),
            pl.BlockSpec((qb, SUBLANES, GROUP_ROWS), lambda t: (t, 0, 0))]


def _attn_fwd(qkv, sinks):
    tp = qkv.shape[0]
    tr = _row_tile(tp)
    qb, nb = tr // BLOCK, tp // BLOCK
    sink_rows, bias = _attn_consts(sinks)

    def body(s_ref, b_ref, q_ref, kp_ref, kc_ref, vp_ref, vc_ref, o_ref, p_ref, ps_ref):
        t = pl.program_id(0)
        k_all = jnp.concatenate([kp_ref[...], kc_ref[...]], axis=0)
        v_all = jnp.concatenate([vp_ref[...], vc_ref[...]], axis=0)
        for i in range(qb):
            rows = slice(i * BLOCK, (i + 1) * BLOCK)
            q = q_ref[rows]
            k2, v2 = k_all[i * BLOCK:(i + 2) * BLOCK], v_all[i * BLOCK:(i + 2) * BLOCK]
            bias_n = _block_bias(b_ref, t, qb, i)
            outs, sink_probs = [], []
            for g in range(N_KV):
                cols = slice(g * HEAD_DIM, (g + 1) * HEAD_DIM)
                qg = _stack_heads(q, g) * jnp.asarray(_QSCALE, BF16)
                p, ps = _attn_probs_t(k2[:, cols], qg, bias_n, s_ref[g:g + 1])
                pb = p.astype(BF16)
                p_ref[i, g] = pb
                sink_probs.append(ps)
                outs.append(_dot_tn(pb, v2[:, cols]))
            o_ref[rows] = _unstack_heads(outs).astype(BF16)
            ps_ref[i] = jnp.concatenate(sink_probs + [jnp.zeros((SUBLANES - N_KV, GROUP_ROWS), F32)], axis=0)

    return pl.pallas_call(
        body, name="attn_fwd", grid=(tp // tr,),
        in_specs=[_SINK_SPEC, _BIAS_SPEC, pl.BlockSpec((tr, ATTN_WIDTH), lambda t: (t, 0))] + _kv_specs(tr),
        out_specs=[pl.BlockSpec((tr, ATTN_WIDTH), lambda t: (t, 0))] + _prob_specs(qb),
        out_shape=[jax.ShapeDtypeStruct((tp, ATTN_WIDTH), BF16),
                   jax.ShapeDtypeStruct((nb, N_KV, 2 * BLOCK, GROUP_ROWS), BF16),
                   jax.ShapeDtypeStruct((nb, SUBLANES, GROUP_ROWS), F32)],
        compiler_params=_params("parallel"),
    )(sink_rows, bias, qkv, qkv, qkv, qkv, qkv)


def _conv_taps(x, halo):
    ext = jnp.concatenate([halo, x], axis=0)
    return [ext[8:] if k == 3 else pltpu.roll(ext, 3 - k, 0)[8:] for k in range(4)]


def _lru_gates(xc, wa, ba, wx, bx, sp):
    xb = xc.astype(BF16)
    r = _sigmoid(_dot(xb, wa) + ba)
    ig = _sigmoid(_dot(xb, wx) + bx)
    log_a = (-LRU_C * sp) * r
    a = jnp.exp(log_a)
    mult = jnp.sqrt(_one_minus_exp2(log_a))
    return xb, r, ig, a, mult


SUBLANES = 8


def _scan_fwd(a, b, h_in):
    n, width = a.shape
    a, b = (v.reshape(n // SUBLANES, SUBLANES, width) for v in (a, b))
    in_group = lax.broadcasted_iota(jnp.int32, a.shape, 1)
    for d in (1, 2, 4):
        keep = in_group >= d
        b = jnp.where(keep, a * pltpu.roll(b, d, 1) + b, b)
        a = jnp.where(keep, a * pltpu.roll(a, d, 1), a)
    a, b = a.reshape(n, width), b.reshape(n, width)
    out, carry = [], h_in
    for g in range(0, n, SUBLANES):
        h = a[g:g + SUBLANES] * carry + b[g:g + SUBLANES]
        out.append(h)
        carry = h[SUBLANES - 1:]
    return jnp.concatenate(out, axis=0)


def _scan_rev(c, b, g_in):
    n, width = c.shape
    c, b = (v.reshape(n // SUBLANES, SUBLANES, width) for v in (c, b))
    in_group = lax.broadcasted_iota(jnp.int32, c.shape, 1)
    for d in (1, 2, 4):
        keep = in_group < SUBLANES - d
        b = jnp.where(keep, b + c * pltpu.roll(b, SUBLANES - d, 1), b)
        c = jnp.where(keep, c * pltpu.roll(c, SUBLANES - d, 1), c)
    c, b = c.reshape(n, width), b.reshape(n, width)
    out, carry = [], g_in
    for g in range(n - SUBLANES, -1, -SUBLANES):
        r = b[g:g + SUBLANES] + c[g:g + SUBLANES] * carry
        out.append(r)
        carry = r[:1]
    return jnp.concatenate(out[::-1], axis=0)


def _inproj_lru_fwd(head, x, g, w_in, conv_w, conv_b, wa, ba, wx, bx, lam, token):
    tp = BLOCK + x.shape[0]
    tr = _row_tile(tp)
    qb, nt = tr // BLOCK, tp // tr
    small = [conv_w, conv_b, wa, ba, wx, bx, lam]

    def body(*refs):
        head_ref, pieces = refs[0], refs[1:1 + qb]
        g_ref, w_ref, _, cw_ref, cb_ref, wa_ref, ba_ref, wx_ref, bx_ref, lam_ref = refs[1 + qb:11 + qb]
        u_ref, qkv_ref, xr_ref, yr_ref, hr_ref, rec_ref, zbuf, halo, hprev = refs[11 + qb:]
        i = pl.program_id(0)
        cur = i % 2

        @pl.when(i == 0)
        def _():
            halo[...] = jnp.zeros_like(halo)
            hprev[...] = jnp.zeros_like(hprev)
            zbuf[1] = jnp.zeros((tr, 2 * LRU_WIDTH), F32)

        def recurrent_branch(valid):
            cw, cb = cw_ref[...], cb_ref[...]
            wa_m, ba_v, wx_m, bx_v = wa_ref[...], ba_ref[...], wx_ref[...], bx_ref[...]
            sp = _softplus(-lam_ref[...])
            before, h_last = halo[...], hprev[0:1]
            for b in range(qb):
                rows = slice(b * BLOCK, (b + 1) * BLOCK)
                xy = zbuf[1 - cur, rows]
                xin = xy[:, :LRU_WIDTH]
                taps = _conv_taps(xin, before)
                before = xin[BLOCK - 8:]
                xc = cb + sum(cw[k:k + 1] * taps[k] for k in range(4))
                _, _, ig, a, mult = _lru_gates(xc, wa_m, ba_v, wx_m, bx_v, sp)
                u = mult * (ig * xc)
                if b == 0:
                    pos = (i - 1) * tr + lax.broadcasted_iota(jnp.int32, xc.shape, 0)
                    u = jnp.where(pos >= PAD_ROWS, u, 0.0)
                h = _scan_fwd(a, u, h_last)
                h_last = h[BLOCK - 1:]
                hr_ref[rows] = h
                gl, _ = _gelu(xy[:, LRU_WIDTH:])
                rec_ref[rows] = (gl * h).astype(BF16)
            halo[...] = jnp.where(valid, before, 0.0)
            hprev[0:1] = jnp.where(valid, h_last, 0.0)

        def projection():
            xhat, _ = _rms(_seq_tile(head_ref[...], pieces, i))
            u = (xhat * g_ref[...]).astype(BF16)
            u_ref[...] = u
            z = _dot_nt(u, w_ref[...])
            qkv_ref[...] = z[:, :QKV_WIDTH].astype(BF16)
            xr_ref[...] = z[:, QKV_WIDTH:QKV_WIDTH + LRU_WIDTH]
            yr_ref[...] = z[:, QKV_WIDTH + LRU_WIDTH:]
            zbuf[cur] = z[:, QKV_WIDTH:]

        @pl.when(i < nt)
        def _():
            recurrent_branch(i >= 1)
            projection()

        @pl.when(i == nt)
        def _():
            recurrent_branch(True)

    last = nt - 1
    this_row = lambda w: pl.BlockSpec((tr, w), lambda i: (jnp.minimum(i, last), 0))
    prev_row = lambda w: pl.BlockSpec((tr, w), lambda i: (jnp.maximum(i - 1, 0), 0))
    full = lambda a: pl.BlockSpec(a.shape, lambda i: (0,) * a.ndim)
    piece_specs = [pl.BlockSpec((BLOCK, D_MODEL), lambda i, s=s: (jnp.maximum(jnp.minimum(i, last) * qb + s - 1, 0), 0))
                   for s in range(qb)]
    return pl.pallas_call(
        body, name="inproj_lru_fwd", grid=(nt + 1,),
        in_specs=[full(head)] + piece_specs + [full(g), full(w_in), full(token)] + [full(a) for a in small],
        out_specs=[this_row(D_MODEL), this_row(QKV_WIDTH), this_row(LRU_WIDTH), this_row(LRU_WIDTH),
                   prev_row(LRU_WIDTH), prev_row(LRU_WIDTH)],
        out_shape=[jax.ShapeDtypeStruct((tp, D_MODEL), BF16), jax.ShapeDtypeStruct((tp, QKV_WIDTH), BF16),
                   jax.ShapeDtypeStruct((tp, LRU_WIDTH), F32), jax.ShapeDtypeStruct((tp, LRU_WIDTH), F32),
                   jax.ShapeDtypeStruct((tp, LRU_WIDTH), F32), jax.ShapeDtypeStruct((tp, LRU_WIDTH), BF16)],
        scratch_shapes=[pltpu.VMEM((2, tr, 2 * LRU_WIDTH), F32), pltpu.VMEM((8, LRU_WIDTH), F32),
                        pltpu.VMEM((8, LRU_WIDTH), F32)],
        compiler_params=_params("arbitrary"),
    )(head, *([x] * qb), g, w_in, token, *small)


def _outproj_fwd(attn, rec, w_out, head, x, g_post_mix, g_pre_ffn):
    tp = attn.shape[0]
    tr = _row_tile(tp)
    qb = tr // BLOCK

    def body(*refs):
        a_ref, r_ref, w_ref, head_ref = refs[:4]
        pieces = refs[4:4 + qb]
        gm_ref, gf_ref, mix_ref, h1_ref, u1_ref = refs[4 + qb:]
        mix = _dot(a_ref[...], w_ref[:ATTN_WIDTH]) + _dot(r_ref[...], w_ref[ATTN_WIDTH:])
        mix_ref[...] = mix
        mhat, _ = _rms(mix)
        h1 = _seq_tile(head_ref[...], pieces, pl.program_id(0)) + mhat * gm_ref[...]
        h1_ref[...] = h1
        hhat, _ = _rms(h1)
        u1_ref[...] = (hhat * gf_ref[...]).astype(BF16)

    row = lambda w: pl.BlockSpec((tr, w), lambda i: (i, 0))
    full = lambda a: pl.BlockSpec(a.shape, lambda i: (0,) * a.ndim)
    return pl.pallas_call(
        body, name="outproj_fwd", grid=(tp // tr,),
        in_specs=[row(ATTN_WIDTH), row(LRU_WIDTH), full(w_out), full(head)] + _seq_specs(tr)
        + [full(g_post_mix), full(g_pre_ffn)],
        out_specs=[row(D_MODEL), row(D_MODEL), row(D_MODEL)],
        out_shape=[jax.ShapeDtypeStruct((tp, D_MODEL), F32), jax.ShapeDtypeStruct((tp, D_MODEL), F32),
                   jax.ShapeDtypeStruct((tp, D_MODEL), BF16)],
        compiler_params=_params("parallel"),
    )(attn, rec, w_out, head, *([x] * qb), g_post_mix, g_pre_ffn)


FFN_STEPS = N_CHIPS


def _resident(a):
    return pl.BlockSpec(a.shape, lambda *_: (0,) * a.ndim, pipeline_mode=pl.Buffered(1))


def _ffn_fwd(u1, w1, w2, h1, tgt, g_post_ffn):
    tp = h1.shape[0]
    tr = _row_tile(tp)
    qb, nt = tr // BLOCK, tp // tr
    sr = tr // FFN_STEPS

    def body(*refs):
        u_ref, w1_ref, w2_ref, h1_ref = refs[:4]
        t_pieces = refs[4:4 + qb]
        g_ref, r1_ref, dy_ref, df2_ref, loss_ref, dg_ref, acc = refs[4 + qb:]
        i, c = pl.program_id(0), pl.program_id(1)
        cur = i % 2

        @pl.when((i == 0) & (c == 0))
        def _():
            loss_ref[...] = jnp.zeros_like(loss_ref)
            dg_ref[...] = jnp.zeros_like(dg_ref)
            acc[1] = jnp.zeros((tr, D_MODEL), F32)

        def matmuls():
            r = jnp.maximum(_dot(u_ref[...], w1_ref[c]), 0.0)
            r1_ref[...] = r.astype(BF16)
            return _dot((r * r).astype(BF16), w2_ref[c])

        def finish_previous_tile(k, valid):
            lo, hi = k * sr, (k + 1) * sr
            g = g_ref[...]
            fhat, rs = _rms(acc[1 - cur, lo:hi])
            h2 = h1_ref[...] + fhat * g
            rows = (i - 1) * tr + lo + lax.broadcasted_iota(jnp.int32, h2.shape, 0)
            tgt = jnp.concatenate([p[max(lo - s * BLOCK, 0):min(hi - s * BLOCK, BLOCK)] for s, p in enumerate(t_pieces)
                                   if lo < (s + 1) * BLOCK and hi > s * BLOCK], axis=0)
            err = jnp.where((rows >= BLOCK) & valid, h2 - tgt, 0.0)
            dy = err * (1.0 / D_MODEL)
            dy_ref[...] = dy
            loss_ref[...] += (0.5 / D_MODEL) * jnp.sum(err * err)
            df2, dg = _rms_bwd(fhat, rs, g, dy)
            df2_ref[...] = df2.astype(BF16)
            dg_ref[...] += dg

        for k in range(FFN_STEPS):
            @pl.when((c == k) & (i < nt))
            def _(k=k):
                finish_previous_tile(k, i >= 1)
                if k == 0:
                    acc[cur] = matmuls()
                else:
                    acc[cur] += matmuls()

            @pl.when((c == k) & (i == nt))
            def _(k=k):
                finish_previous_tile(k, True)

    last = nt - 1
    this_row = pl.BlockSpec((tr, D_MODEL), lambda i, c: (jnp.minimum(i, last), 0))
    prev_quarter = pl.BlockSpec((sr, D_MODEL), lambda i, c: (jnp.maximum(i - 1, 0) * FFN_STEPS + c, 0))
    prev_quarter_out = pl.BlockSpec(
        (sr, D_MODEL), lambda i, c: (jnp.where(i == 0, nt * FFN_STEPS, (i - 1) * FFN_STEPS + c), 0))
    full = lambda a: pl.BlockSpec(a.shape, lambda i, c: (0,) * a.ndim)
    return pl.pallas_call(
        body, name="ffn_fwd", grid=(nt + 1, FFN_STEPS),
        in_specs=[this_row, _resident(w1), _resident(w2), prev_quarter] + _seq_specs(tr, delay=1) + [full(g_post_ffn)],
        out_specs=[pl.BlockSpec((tr, FF_CHUNK), lambda i, c: (jnp.minimum(i, last), jnp.where(i < nt, c, FFN_STEPS - 1))),
                   prev_quarter_out, prev_quarter_out,
                   pl.BlockSpec((1, 1), lambda i, c: (0, 0)), pl.BlockSpec((1, D_MODEL), lambda i, c: (0, 0))],
        out_shape=[jax.ShapeDtypeStruct((tp, D_FF), BF16), jax.ShapeDtypeStruct((tp + sr, D_MODEL), F32),
                   jax.ShapeDtypeStruct((tp + sr, D_MODEL), BF16), jax.ShapeDtypeStruct((1, 1), F32),
                   jax.ShapeDtypeStruct((1, D_MODEL), F32)],
        scratch_shapes=[pltpu.VMEM((2, tr, D_MODEL), F32)],
        compiler_params=_params("arbitrary", "arbitrary"),
    )(u1, w1, w2, h1, *([tgt] * qb), g_post_ffn)


def _ffn_bwd_data(df2, r1, w1, w2, dy, h1, mix, g_pre_ffn, g_post_mix):
    tp = h1.shape[0]
    tr = _row_tile(tp)
    nt = tp // tr
    sr = tr // FFN_STEPS

    def body(df2_ref, r1_ref, w1_ref, w2_ref, dy_ref, h1_ref, mix_ref, gf_ref, gm_ref,
             da_ref, dh1_ref, dmix_ref, dgf_ref, dgm_ref, acc):
        i, c = pl.program_id(0), pl.program_id(1)
        cur = i % 2

        @pl.when((i == 0) & (c == 0))
        def _():
            dgf_ref[...] = jnp.zeros_like(dgf_ref)
            dgm_ref[...] = jnp.zeros_like(dgm_ref)
            acc[1] = jnp.zeros((tr, D_MODEL), F32)

        def matmuls():
            df = _dot_nt(df2_ref[...], w2_ref[c])
            da = (df * (2.0 * r1_ref[...].astype(F32))).astype(BF16)
            da_ref[...] = da
            return _dot_nt(da, w1_ref[c])

        def finish_previous_tile(k, valid):
            lo, hi = k * sr, (k + 1) * sr
            hhat, rs = _rms(h1_ref[...])
            dx, dgf = _rms_bwd(hhat, rs, gf_ref[...], acc[1 - cur, lo:hi])
            dh1 = dy_ref[...] + dx
            dh1_ref[...] = dh1
            mhat, rsm = _rms(mix_ref[...])
            dmix, dgm = _rms_bwd(mhat, rsm, gm_ref[...], dh1)
            dmix_ref[...] = dmix.astype(BF16)
            dgf_ref[...] += jnp.where(valid, dgf, 0.0)
            dgm_ref[...] += jnp.where(valid, dgm, 0.0)

        for k in range(FFN_STEPS):
            @pl.when((c == k) & (i < nt))
            def _(k=k):
                finish_previous_tile(k, i >= 1)
                if k == 0:
                    acc[cur] = matmuls()
                else:
                    acc[cur] += matmuls()

            @pl.when((c == k) & (i == nt))
            def _(k=k):
                finish_previous_tile(k, True)

    last = nt - 1
    this_row = pl.BlockSpec((tr, D_MODEL), lambda i, c: (jnp.minimum(i, last), 0))
    prev_quarter = pl.BlockSpec((sr, D_MODEL), lambda i, c: (jnp.maximum(i - 1, 0) * FFN_STEPS + c, 0))
    prev_quarter_out = pl.BlockSpec(
        (sr, D_MODEL), lambda i, c: (jnp.where(i == 0, nt * FFN_STEPS, (i - 1) * FFN_STEPS + c), 0))
    chunk = pl.BlockSpec((tr, FF_CHUNK), lambda i, c: (jnp.minimum(i, last), jnp.where(i < nt, c, FFN_STEPS - 1)))
    gain = pl.BlockSpec((1, D_MODEL), lambda i, c: (0, 0))
    return pl.pallas_call(
        body, name="ffn_bwd_data", grid=(nt + 1, FFN_STEPS),
        in_specs=[this_row, chunk, _resident(w1), _resident(w2), prev_quarter, prev_quarter, prev_quarter, gain, gain],
        out_specs=[chunk, prev_quarter_out, prev_quarter_out, gain, gain],
        out_shape=[jax.ShapeDtypeStruct((tp, D_FF), BF16), jax.ShapeDtypeStruct((tp + sr, D_MODEL), F32),
                   jax.ShapeDtypeStruct((tp + sr, D_MODEL), BF16), jax.ShapeDtypeStruct((1, D_MODEL), F32),
                   jax.ShapeDtypeStruct((1, D_MODEL), F32)],
        scratch_shapes=[pltpu.VMEM((2, tr, D_MODEL), F32)],
        compiler_params=_params("arbitrary", "arbitrary"),
    )(df2, r1, w1, w2, dy, h1, mix, g_pre_ffn, g_post_mix)


def _ffn_bwd_weights(u1, da1, r1, df2):
    tp = u1.shape[0]
    tr = _wgrad_row_tile(tp)

    def body(u_ref, da_ref, r1_ref, df2_ref, dw1_ref, dw2_ref):
        i = pl.program_id(1)

        def products():
            r = r1_ref[...].astype(F32)
            return _dot_tn(u_ref[...], da_ref[...]), _dot_tn((r * r).astype(BF16), df2_ref[...])

        @pl.when(i == 0)
        def _():
            dw1_ref[0], dw2_ref[0] = products()

        @pl.when(i > 0)
        def _():
            p1, p2 = products()
            dw1_ref[0] += p1
            dw2_ref[0] += p2

    row = pl.BlockSpec((tr, D_MODEL), lambda c, i: (i, 0))
    chunk = pl.BlockSpec((tr, FF_CHUNK), lambda c, i: (i, c))
    return pl.pallas_call(
        body, name="ffn_bwd_weights", grid=(N_CHIPS, tp // tr),
        in_specs=[row, chunk, chunk, row],
        out_specs=[pl.BlockSpec((1, D_MODEL, FF_CHUNK), lambda c, i: (c, 0, 0)),
                   pl.BlockSpec((1, FF_CHUNK, D_MODEL), lambda c, i: (c, 0, 0))],
        out_shape=[jax.ShapeDtypeStruct((N_CHIPS, D_MODEL, FF_CHUNK), F32),
                   jax.ShapeDtypeStruct((N_CHIPS, FF_CHUNK, D_MODEL), F32)],
        compiler_params=_params("parallel", "arbitrary"),
    )(u1, da1, r1, df2)


N_VEC_ROWS = 8


def _outproj_lru_bwd(dmix, w_out, attn, rec, xr, yr, hr, conv_w, conv_b, wa, ba, wx, bx, lam, token):
    tp = xr.shape[0]
    tr = _row_tile(tp)
    qb, nt = tr // BLOCK, tp // tr

    def body(dm_ref, w_ref, at_ref, rc_ref, xr_ref, xh_ref, yr_ref, hr_ref, hp_ref,
             cw_ref, cb_ref, wa_ref, ba_ref, wx_ref, bx_ref, lam_ref, _,
             dxr_ref, dyr_ref, dat_ref, dwo_ref, dwa_ref, dwx_ref, vec_ref, g_next, a_next, dxc_next, dsp):
        s = pl.program_id(0)
        t = nt - 1 - s

        @pl.when(s == 0)
        def _():
            g_next[...] = jnp.zeros_like(g_next)
            a_next[...] = jnp.zeros_like(a_next)
            dxc_next[...] = jnp.zeros_like(dxc_next)
            dsp[...] = jnp.zeros_like(dsp)
            dwo_ref[...] = jnp.zeros_like(dwo_ref)
            dwa_ref[...] = jnp.zeros_like(dwa_ref)
            dwx_ref[...] = jnp.zeros_like(dwx_ref)
            vec_ref[...] = jnp.zeros_like(vec_ref)

        dm = dm_ref[...]
        dcat = _dot_nt(dm, w_ref[...])
        dat_ref[...] = dcat[:, :ATTN_WIDTH].astype(BF16)
        drec_tile = dcat[:, ATTN_WIDTH:]
        dwo_ref[:ATTN_WIDTH] += _dot_tn(at_ref[...], dm)
        dwo_ref[ATTN_WIDTH:] += _dot_tn(rc_ref[...], dm)

        first_tile = t == 0
        cw, cb = cw_ref[...], cb_ref[...]
        lam_v = lam_ref[...]
        sp = _softplus(-lam_v)
        wa_m, ba_v, wx_m, bx_v = wa_ref[...], ba_ref[...], wx_ref[...], bx_ref[...]
        rows = lax.broadcasted_iota(jnp.int32, (BLOCK, LRU_WIDTH), 0)
        col = lambda v: jnp.sum(v, axis=0, keepdims=True)

        g_after, a_after, dxc_after = g_next[0:1], a_next[0:1], dxc_next[...]
        xbs, dgrs, dgis = [], [], []
        vec = [jnp.zeros((1, LRU_WIDTH), F32) for _ in range(N_VEC_ROWS)]
        for i in reversed(range(qb)):
            blk = slice(i * BLOCK, (i + 1) * BLOCK)
            if i == 0:
                x_before = jnp.where(first_tile, 0.0, xh_ref[...])
                h_before = jnp.where(first_tile, 0.0, hp_ref[7:8])
            else:
                x_before = xr_ref[i * BLOCK - 8:i * BLOCK]
                h_before = hr_ref[i * BLOCK - 1:i * BLOCK]
            taps = _conv_taps(xr_ref[blk], x_before)
            xc = cb + sum(cw[k:k + 1] * taps[k] for k in range(4))
            xb, r, ig, a, mult = _lru_gates(xc, wa_m, ba_v, wx_m, bx_v, sp)

            yr_v = yr_ref[blk]
            gl, th = _gelu(yr_v)
            h = hr_ref[blk]
            drec = drec_tile[blk]
            dyr_ref[blk] = (drec * h * _gelu_grad(yr_v, th)).astype(BF16)

            a_up = jnp.where(rows == BLOCK - 1, a_after, pltpu.roll(a, BLOCK - 1, 0))
            g = _scan_rev(a_up, drec * gl, g_after)
            g_after, a_after = g[0:1], a[0:1]

            h_prev = jnp.where(rows == 0, h_before, pltpu.roll(h, 1, 0))
            du, da = g, g * h_prev
            if i == 0:
                real = (t * tr + rows) >= PAD_ROWS
                du, da = jnp.where(real, du, 0.0), jnp.where(real, da, 0.0)
            dmult = du * (ig * xc)
            dig = du * (mult * xc)
            dxc = du * (mult * ig)
            dlog_a = da * a - dmult * (a * a / mult)
            if i == 0:
                dlog_a = jnp.where(real, dlog_a, 0.0)
            dgr = (dlog_a * (-LRU_C * sp)) * (r * (1.0 - r))
            dgi = dig * (ig * (1.0 - ig))
            dgr_b, dgi_b = dgr.astype(BF16), dgi.astype(BF16)
            dxc = dxc + _dot_nt(dgr_b, wa_m) + _dot_nt(dgi_b, wx_m)
            xbs.append(xb)
            dgrs.append(dgr_b)
            dgis.append(dgi_b)

            ext = jnp.concatenate([dxc, dxc_after], axis=0)
            up = [ext[:BLOCK] if j == 0 else pltpu.roll(ext, BLOCK + 8 - j, 0)[:BLOCK] for j in range(4)]
            dxr_ref[blk] = sum(cw[k:k + 1] * up[3 - k] for k in range(4)).astype(BF16)
            dxc_after = dxc[:8]

            for k in range(4):
                vec[k] = vec[k] + col(dxc * taps[k])
            vec[4] = vec[4] + col(dxc)
            vec[5] = vec[5] + col(dgr)
            vec[6] = vec[6] + col(dgi)
            vec[7] = vec[7] + col(dlog_a * (-LRU_C * r))

        g_next[0:1], a_next[0:1], dxc_next[...] = g_after, a_after, dxc_after
        xb_all = jnp.concatenate(xbs, axis=0)
        dwa_ref[...] += _dot_tn(xb_all, jnp.concatenate(dgrs, axis=0))
        dwx_ref[...] += _dot_tn(xb_all, jnp.concatenate(dgis, axis=0))
        for k in range(7):
            vec_ref[k:k + 1] += vec[k]
        dsp[0:1] += vec[7]

        @pl.when(s == nt - 1)
        def _():
            vec_ref[7:8] = dsp[0:1] * (-_sigmoid(-lam_v))

    blk_spec = pl.BlockSpec((tr, LRU_WIDTH), lambda s: (nt - 1 - s, 0))
    rows_before = pl.BlockSpec((8, LRU_WIDTH), lambda s: (jnp.maximum((nt - 1 - s) * (tr // 8) - 1, 0), 0))
    full = lambda a: pl.BlockSpec(a.shape, lambda s: (0,) * a.ndim)
    small = [conv_w, conv_b, wa, ba, wx, bx, lam, token]
    sq = pl.BlockSpec((LRU_WIDTH, LRU_WIDTH), lambda s: (0, 0))
    wide = pl.BlockSpec((tr, D_MODEL), lambda s: (nt - 1 - s, 0))
    whole = pl.BlockSpec((D_MODEL, D_MODEL), lambda s: (0, 0))
    return pl.pallas_call(
        body, name="outproj_lru_bwd", grid=(nt,),
        in_specs=[wide, whole, blk_spec, blk_spec, blk_spec, rows_before, blk_spec, blk_spec, rows_before]
        + [full(a) for a in small],
        out_specs=[blk_spec, blk_spec, blk_spec, whole, sq, sq, pl.BlockSpec((N_VEC_ROWS, LRU_WIDTH), lambda s: (0, 0))],
        out_shape=[jax.ShapeDtypeStruct((tp, LRU_WIDTH), BF16), jax.ShapeDtypeStruct((tp, LRU_WIDTH), BF16),
                   jax.ShapeDtypeStruct((tp, ATTN_WIDTH), BF16), jax.ShapeDtypeStruct((D_MODEL, D_MODEL), F32),
                   jax.ShapeDtypeStruct((LRU_WIDTH, LRU_WIDTH), F32), jax.ShapeDtypeStruct((LRU_WIDTH, LRU_WIDTH), F32),
                   jax.ShapeDtypeStruct((N_VEC_ROWS, LRU_WIDTH), F32)],
        scratch_shapes=[pltpu.VMEM((8, LRU_WIDTH), F32)] * 4,
        compiler_params=_params("arbitrary"),
    )(dmix, w_out, attn, rec, xr, xr, yr, hr, hr, *small)


def _attn_bwd_tile(tp):
    return _wgrad_row_tile(tp)


def _attn_bwd(qkv, dattn, probs, sink_probs, token):
    tp = qkv.shape[0]
    tr = _attn_bwd_tile(tp)
    qb, nt = tr // BLOCK, tp // tr
    n_groups = N_KV

    def body(p_ref, ps_ref, q_ref, kp_ref, kc_ref, vp_ref, vc_ref, do_ref, _, dq_ref, dkv_ref, ex_ref, ds_ref, dsink):
        t = pl.program_id(0)

        @pl.when(t == 0)
        def _():
            dsink[...] = jnp.zeros_like(dsink)

        k_all = jnp.concatenate([kp_ref[...], kc_ref[...]], axis=0)
        v_all = jnp.concatenate([vp_ref[...], vc_ref[...]], axis=0)
        tail = None
        for i in range(qb):
            rows = slice(i * BLOCK, (i + 1) * BLOCK)
            qt = (q_ref[rows].astype(F32) * _QSCALE).T
            dot = do_ref[rows].astype(F32).T
            k2, v2 = k_all[i * BLOCK:(i + 2) * BLOCK], v_all[i * BLOCK:(i + 2) * BLOCK]
            dqs, dks, dvs = [], [], []
            for g in range(n_groups):
                cols = slice(g * HEAD_DIM, (g + 1) * HEAD_DIM)
                k_g, v_g = k2[:, cols], v2[:, cols]
                qgt, dogt = _heads_t(qt, g), _heads_t(dot, g)
                pb = p_ref[i, g]
                p = pb.astype(F32)
                dpt = _dot(v_g, dogt)
                delta = jnp.sum(p * dpt, axis=0, keepdims=True)
                dst = (p * (dpt - delta)).astype(BF16)
                dqs.append(_dot_tn(k_g, dst) * _QSCALE)
                dks.append(_dot_nt(qgt, dst))
                dvs.append(_dot_nt(dogt, pb))
                dsink[g:g + 1] -= ps_ref[i, g:g + 1] * delta
            dq_ref[rows] = _from_heads_t(dqs).astype(BF16)
            dkv = jnp.concatenate([jnp.concatenate(dks, axis=0).T, jnp.concatenate(dvs, axis=0).T], axis=1)
            if i == 0:
                ex_ref[0] = dkv[:BLOCK]
            else:
                dkv_ref[(i - 1) * BLOCK:i * BLOCK] = (tail + dkv[:BLOCK]).astype(BF16)
            tail = dkv[BLOCK:]
        dkv_ref[(qb - 1) * BLOCK:] = tail.astype(BF16)

        @pl.when(t == nt - 1)
        def _():
            lane = lax.broadcasted_iota(jnp.int32, (1, ATTN_HEADS), 1)
            acc = jnp.zeros((1, ATTN_HEADS), F32)
            for h in range(ATTN_HEADS):
                g, hh = divmod(h, GQA_GROUP)
                acc = acc + jnp.where(lane == h, jnp.sum(dsink[g:g + 1, hh * BLOCK:(hh + 1) * BLOCK]), 0.0)
            ds_ref[...] = acc

    cur = lambda w: pl.BlockSpec((tr, w), lambda t: (t, 0))
    return pl.pallas_call(
        body, name="attn_bwd", grid=(nt,),
        in_specs=_prob_specs(qb) + [cur(ATTN_WIDTH)] + _kv_specs(tr)
        + [cur(ATTN_WIDTH), pl.BlockSpec(token.shape, lambda t: (0, 0))],
        out_specs=[cur(ATTN_WIDTH), cur(2 * KV_WIDTH), pl.BlockSpec((1, BLOCK, 2 * KV_WIDTH), lambda t: (t, 0, 0)),
                   pl.BlockSpec((1, ATTN_HEADS), lambda t: (0, 0))],
        out_shape=[jax.ShapeDtypeStruct((tp, ATTN_WIDTH), BF16), jax.ShapeDtypeStruct((tp, 2 * KV_WIDTH), BF16),
                   jax.ShapeDtypeStruct((nt, BLOCK, 2 * KV_WIDTH), F32), jax.ShapeDtypeStruct((1, ATTN_HEADS), F32)],
        scratch_shapes=[pltpu.VMEM((n_groups, GROUP_ROWS), F32)],
        compiler_params=_params("arbitrary"),
    )(probs, sink_probs, qkv, qkv, qkv, qkv, qkv, dattn, token)


def _fix_dkv(dkv, dkv_extra):
    tp = dkv.shape[0]
    tr = _attn_bwd_tile(tp)
    nt, qb = tp // tr, tr // BLOCK
    if nt == 1:
        return dkv

    def body(d_ref, ex_ref, o_ref):
        o_ref[...] = (d_ref[...].astype(F32) + ex_ref[0]).astype(BF16)

    last = pl.BlockSpec((BLOCK, 2 * KV_WIDTH), lambda t: (t * qb + qb - 1, 0))
    return pl.pallas_call(
        body, name="fix_dkv", grid=(nt - 1,),
        in_specs=[last, pl.BlockSpec((1, BLOCK, 2 * KV_WIDTH), lambda t: (t + 1, 0, 0))],
        out_specs=last, out_shape=jax.ShapeDtypeStruct(dkv.shape, dkv.dtype),
        input_output_aliases={0: 0}, compiler_params=_params("parallel"),
    )(dkv, dkv_extra)


def _inproj_wgrad(dq, dkv, dxr, dyr, u0):
    tp = dq.shape[0]
    tr = _wgrad_row_tile(tp)

    def body(dq_ref, dkv_ref, dxr_ref, dyr_ref, u_ref, dw_ref):
        i = pl.program_id(0)

        def product():
            dz = jnp.concatenate([dq_ref[...], dkv_ref[...], dxr_ref[...], dyr_ref[...]], axis=1)
            return _dot_tn(dz, u_ref[...])

        @pl.when(i == 0)
        def _():
            dw_ref[...] = product()

        @pl.when(i > 0)
        def _():
            dw_ref[...] += product()

    row = lambda w: pl.BlockSpec((tr, w), lambda i: (i, 0))
    return pl.pallas_call(
        body, name="inproj_wgrad", grid=(tp // tr,),
        in_specs=[row(ATTN_WIDTH), row(2 * KV_WIDTH), row(LRU_WIDTH), row(LRU_WIDTH), row(D_MODEL)],
        out_specs=pl.BlockSpec((IN_WIDTH, D_MODEL), lambda i: (0, 0)),
        out_shape=jax.ShapeDtypeStruct((IN_WIDTH, D_MODEL), F32),
        compiler_params=_params("arbitrary"),
    )(dq, dkv, dxr, dyr, u0)


def _inproj_dgrad(dq, dkv, dxr, dyr, w_in, head, x, dh1, g, token):
    tp = dq.shape[0]
    tr = _row_tile(tp)
    nt, qb = tp // tr, tr // BLOCK

    def body(*refs):
        dq_ref, dkv_ref, dxr_ref, dyr_ref, w_ref, head_ref = refs[:6]
        pieces = refs[6:6 + qb]
        dh1_ref, g_ref, _, gx_ref, dhead_ref, dg_ref, buf, sems = refs[6 + qb:]
        i = pl.program_id(0)
        slot = i % 2

        def out_copy(step, at):
            return pltpu.make_async_copy(buf.at[at], gx_ref.at[pl.ds(step * tr - BLOCK, tr)], sems.at[at])

        dz = jnp.concatenate([dq_ref[...], dkv_ref[...], dxr_ref[...], dyr_ref[...]], axis=1)
        du = _dot(dz, w_ref[...])
        hhat, rs = _rms(_seq_tile(head_ref[...], pieces, i))
        dx, dg = _rms_bwd(hhat, rs, g_ref[...], du)
        dh0 = dh1_ref[...] + dx

        @pl.when(i >= 3)
        def _():
            out_copy(i - 2, slot).wait()

        buf[slot] = dh0

        @pl.when(i == 0)
        def _():
            dg_ref[...] = dg
            dhead_ref[...] = dh0[:BLOCK]
            if tr > BLOCK:
                first = pltpu.make_async_copy(buf.at[0, pl.ds(BLOCK, tr - BLOCK)], gx_ref.at[pl.ds(0, tr - BLOCK)],
                                              sems.at[0])
                first.start()
                first.wait()

        @pl.when(i >= 1)
        def _():
            dg_ref[...] += dg
            out_copy(i, slot).start()

        @pl.when(i == nt - 1)
        def _():
            if nt >= 3:
                out_copy(nt - 2, (nt - 2) % 2).wait()
            if nt >= 2:
                out_copy(nt - 1, (nt - 1) % 2).wait()

    row = lambda w: pl.BlockSpec((tr, w), lambda i: (i, 0))
    full = lambda shape: pl.BlockSpec(shape, lambda i: (0,) * len(shape))
    return pl.pallas_call(
        body, name="inproj_dgrad", grid=(tp // tr,),
        in_specs=[row(ATTN_WIDTH), row(2 * KV_WIDTH), row(LRU_WIDTH), row(LRU_WIDTH), full(w_in.shape),
                  full(head.shape)] + _seq_specs(tr) + [row(D_MODEL), full(g.shape), full(token.shape)],
        out_specs=[pl.BlockSpec(memory_space=pl.ANY), full((BLOCK, D_MODEL)), full((1, D_MODEL))],
        out_shape=[jax.ShapeDtypeStruct(x.shape, F32), jax.ShapeDtypeStruct((BLOCK, D_MODEL), F32),
                   jax.ShapeDtypeStruct((1, D_MODEL), F32)],
        scratch_shapes=[pltpu.VMEM((2, tr, D_MODEL), F32), pltpu.SemaphoreType.DMA((2,))],
        compiler_params=_params("arbitrary"),
    )(dq, dkv, dxr, dyr, w_in, head, *([x] * qb), dh1, g, token)


def _dense_block_diag(w):
    eye = jnp.eye(LRU_BLOCKS, dtype=w.dtype)
    return (w[:, :, None, :] * eye[:, None, :, None]).reshape(LRU_WIDTH, LRU_WIDTH)


def _diag_blocks(dense):
    d4 = dense.reshape(LRU_BLOCKS, LRU_BLOCK, LRU_BLOCKS, LRU_BLOCK)
    return jnp.stack([d4[n, :, n, :] for n in range(LRU_BLOCKS)])


def _local_step(head, x, tgt, g_pre_mix, w_in, conv_w, conv_b, w_a, b_a, w_x, b_x, lam, sinks, g_post_mix,
                g_pre_ffn, g_post_ffn, late_weights, on_ffn_grads, on_outproj_bwd, on_mixer_grads, token):
    wa = _dense_block_diag(w_a).astype(BF16)
    wx = _dense_block_diag(w_x).astype(BF16)

    u0, qkv, xr, yr, hr, rec = _inproj_lru_fwd(head, x, g_pre_mix, w_in, conv_w, conv_b, wa, b_a, wx, b_x, lam, token)
    attn, probs, sink_probs = _attn_fwd(qkv, sinks)
    w_out, ffn_weights = late_weights([attn, rec])
    mix, h1, u1 = _outproj_fwd(attn, rec, w_out, head, x, g_post_mix, g_pre_ffn)
    w1, w2 = ffn_weights([u1])
    r1, dy, df2, loss, dg_post_ffn = _ffn_fwd(u1, w1, w2, h1, tgt, g_post_ffn)

    da1, dh1, dmix, dg_pre_ffn, dg_post_mix = _ffn_bwd_data(df2, r1, w1, w2, dy, h1, mix, g_pre_ffn, g_post_mix)
    dw1, dw2 = _ffn_bwd_weights(u1, da1, r1, df2)
    token2 = on_ffn_grads(dw1, dw2)
    dxr, dyr, dattn, dw_out, dwa, dwx, vec = _outproj_lru_bwd(dmix, w_out, attn, rec, xr, yr, hr, conv_w, conv_b,
                                                              wa, b_a, wx, b_x, lam, token2)
    token3 = on_outproj_bwd(dattn)
    dq, dkv, dkv_extra, dsinks = _attn_bwd(qkv, dattn, probs, sink_probs, token3)
    dkv = _fix_dkv(dkv, dkv_extra)
    dw_in = _inproj_wgrad(dq, dkv, dxr, dyr, u0)
    token4 = on_mixer_grads(dw_in, dw_out)
    dx, dhead, dg_pre_mix = _inproj_dgrad(dq, dkv, dxr, dyr, w_in, head, x, dh1, g_pre_mix, token4)

    grads = dict(
        g_pre_mix=dg_pre_mix, conv_w=vec[0:4], conv_b=vec[4:5], w_a=_diag_blocks(dwa), b_a=vec[5:6],
        w_x=_diag_blocks(dwx), b_x=vec[6:7], lru_lambda=vec[7:8], attn_sinks=dsinks,
        g_post_mix=dg_post_mix, g_pre_ffn=dg_pre_ffn, g_post_ffn=dg_post_ffn)
    return loss, dx, dhead, grads


HBM = pl.BlockSpec(memory_space=pltpu.HBM)


def _mesh_pos():
    return lax.axis_index("x"), lax.axis_index("y"), lax.axis_index("c")


def _other_chips(x, y):
    return [(1 - x, y), (x, 1 - y), (1 - x, 1 - y)]


def _remote(src, dst, send_sem, recv_sem, to):
    return pltpu.make_async_remote_copy(src_ref=src, dst_ref=dst, send_sem=send_sem, recv_sem=recv_sem,
                                        device_id=to, device_id_type=MESH)


def _gather_weights(shards, lands, tiny, tiny_land):
    nbig = len(shards)

    def body(*refs):
        srcs, tiny_src = refs[:nbig], refs[nbig]
        outs, tiny_out = refs[2 * nbig + 2:3 * nbig + 2], refs[3 * nbig + 2]
        ici_send, ici_recv, d2d_send, d2d_recv, tiny_send, tiny_recv = refs[3 * nbig + 3:]
        x, y, c = _mesh_pos()
        me = 2 * x + y
        chips = _other_chips(x, y)
        sibling = (x, y, 1 - c)
        sends = []
        for w, (src, out) in enumerate(zip(srcs, outs)):
            hr = src.shape[0] // 2
            for j, chip in enumerate(chips):
                k = 3 * w + j
                cp = _remote(src.at[pl.ds(c * hr, hr)], out.at[me, pl.ds(c * hr, hr)],
                             ici_send.at[k], ici_recv.at[k], (*chip, c))
                cp.start()
                sends.append(cp)
        for j, chip in enumerate(chips):
            cp = _remote(tiny_src, tiny_out.at[me], tiny_send.at[j], tiny_recv.at[j], (*chip, c))
            cp.start()
            sends.append(cp)
        for w, (src, out) in enumerate(zip(srcs, outs)):
            hr = src.shape[0] // 2
            for j, (px, py) in enumerate(chips):
                k = 3 * w + j
                landed = out.at[2 * px + py, pl.ds(c * hr, hr)]
                _remote(landed, landed, ici_send.at[k], ici_recv.at[k], sibling).wait_recv()
                cp = _remote(landed, landed, d2d_send.at[k], d2d_recv.at[k], sibling)
                cp.start()
                sends.append(cp)
        for w, (src, out) in enumerate(zip(srcs, outs)):
            hr = src.shape[0] // 2
            for j, (px, py) in enumerate(chips):
                k = 3 * w + j
                other = out.at[2 * px + py, pl.ds((1 - c) * hr, hr)]
                _remote(other, other, d2d_send.at[k], d2d_recv.at[k], sibling).wait_recv()
        for j, (px, py) in enumerate(chips):
            blk = tiny_out.at[2 * px + py]
            _remote(blk, blk, tiny_send.at[j], tiny_recv.at[j], sibling).wait_recv()
        for cp in sends:
            cp.wait_send()

    out_shape = [jax.ShapeDtypeStruct(l.shape, l.dtype) for l in list(lands) + [tiny_land]]
    n = 3 * nbig
    return pl.pallas_call(
        body, name="gather_weights", out_shape=out_shape,
        in_specs=[HBM] * (2 * nbig + 2), out_specs=[HBM] * (nbig + 1),
        input_output_aliases={nbig + 1 + i: i for i in range(nbig + 1)},
        scratch_shapes=[pltpu.SemaphoreType.DMA((n,)),
                        pltpu.SemaphoreType.DMA((n,)), pltpu.SemaphoreType.DMA((n,)), pltpu.SemaphoreType.DMA((n,)),
                        pltpu.SemaphoreType.DMA((3,)), pltpu.SemaphoreType.DMA((3,))],
    )(*shards, tiny, *lands, tiny_land)


def _prep_shard(w, me):
    rows, cols = w.shape
    tr = _elementwise_tile(rows)

    def body(me_ref, w_ref, s_ref, l_ref):
        b = w_ref[...].astype(BF16)
        s_ref[...] = b
        l_ref[0] = b

    return pl.pallas_call(
        body, name="prep_shard",
        grid_spec=pltpu.PrefetchScalarGridSpec(
            num_scalar_prefetch=1, grid=(rows // tr,),
            in_specs=[pl.BlockSpec((tr, cols), lambda i, me_ref: (i, 0))],
            out_specs=[pl.BlockSpec((tr, cols), lambda i, me_ref: (i, 0)),
                       pl.BlockSpec((1, tr, cols), lambda i, me_ref: (me_ref[0], i, 0))]),
        out_shape=[jax.ShapeDtypeStruct((rows, cols), BF16), jax.ShapeDtypeStruct((N_CHIPS, rows, cols), BF16)],
        compiler_params=_params("parallel"),
    )(me, w)


def _prep_tiny(tiny, me, slots=N_CHIPS):
    def body(me_ref, t_ref, l_ref):
        l_ref[0] = t_ref[...]

    return pl.pallas_call(
        body, name="prep_tiny",
        grid_spec=pltpu.PrefetchScalarGridSpec(
            num_scalar_prefetch=1, grid=(1,),
            in_specs=[pl.BlockSpec(tiny.shape, lambda i, me_ref: (0, 0))],
            out_specs=pl.BlockSpec((1,) + tiny.shape, lambda i, me_ref: (me_ref[0], 0, 0))),
        out_shape=jax.ShapeDtypeStruct((slots,) + tiny.shape, tiny.dtype),
    )(me, tiny)


N_DEV = 8


def _sibling_exchange(parts, token):
    def body(*refs):
        n = len(parts)
        srcs, outs, send_sems, recv_sems = refs[:n], refs[n + 1:2 * n + 1], refs[2 * n + 1], refs[2 * n + 2]
        x, y, c = _mesh_pos()
        sibling = (x, y, 1 - c)
        cps = []
        for w, (src, out) in enumerate(zip(srcs, outs)):
            hr = src.shape[1] // 2
            cp = _remote(src.at[:, pl.ds((1 - c) * hr, hr)], out, send_sems.at[w], recv_sems.at[w], sibling)
            cp.start()
            cps.append(cp)
        for cp in cps:
            cp.wait()

    n = len(parts)
    return pl.pallas_call(
        body, name="sibling_exchange",
        out_shape=[jax.ShapeDtypeStruct((p.shape[0], p.shape[1] // 2, p.shape[2]), p.dtype) for p in parts],
        in_specs=[HBM] * n + [pl.BlockSpec(memory_space=pl.ANY)], out_specs=[HBM] * n,
        scratch_shapes=[pltpu.SemaphoreType.DMA((n,)), pltpu.SemaphoreType.DMA((n,))],
    )(*parts, token)


def _chip_presum(part, from_sibling, pos):
    _, hr, cols = from_sibling.shape
    tr = _elementwise_tile(hr)
    steps = hr // tr

    def body(pos_ref, a_ref, b_ref, o_ref, land_ref):
        s = (a_ref[...] + b_ref[...]).astype(BF16)
        o_ref[...] = s

        @pl.when(pl.program_id(1) == pos_ref[1])
        def _():
            land_ref[...] = s

    return pl.pallas_call(
        body, name="chip_presum",
        grid_spec=pltpu.PrefetchScalarGridSpec(
            num_scalar_prefetch=1, grid=(steps, N_CHIPS),
            in_specs=[pl.BlockSpec((1, tr, cols), lambda i, j, p: (j, p[0] * steps + i, 0)),
                      pl.BlockSpec((1, tr, cols), lambda i, j, p: (j, i, 0))],
            out_specs=[pl.BlockSpec((1, tr, cols), lambda i, j, p: (j, i, 0)),
                       pl.BlockSpec((1, tr, cols), lambda i, j, p: (p[1], p[0] * steps + i, 0))]),
        out_shape=[jax.ShapeDtypeStruct(from_sibling.shape, BF16),
                   jax.ShapeDtypeStruct((N_CHIPS, 2 * hr, cols), BF16)],
        compiler_params=_params("arbitrary", "arbitrary"),
    )(pos, part, from_sibling)


def _scatter_partials(cparts, lands, done_cparts=(), done_lands=()):
    n_new = len(cparts)
    nw = n_new + len(done_cparts)

    def body(*refs):
        srcs = refs[:nw]
        outs = refs[2 * nw:3 * nw]
        own_send, own_recv, ici_send, ici_recv, d2d_send, d2d_recv = refs[3 * nw:]
        x, y, c = _mesh_pos()
        me = 2 * x + y
        chips = _other_chips(x, y)
        sibling = (x, y, 1 - c)
        sends = []
        for w in list(range(n_new, nw)) + list(range(n_new)):
            src, out = srcs[w], outs[w]
            hr = src.shape[1]
            mine = out.at[me, pl.ds(c * hr, hr)]
            cp = _remote(src.at[me], mine, own_send.at[w], own_recv.at[w], sibling)
            cp.start()
            sends.append(cp)
            for j, (px, py) in enumerate(chips):
                if w >= n_new:
                    break
                k = 3 * w + j
                cp = _remote(src.at[2 * px + py], mine, ici_send.at[k], ici_recv.at[k], (px, py, c))
                cp.start()
                sends.append(cp)
        for w in list(range(n_new, nw)) + list(range(n_new)):
            src, out = srcs[w], outs[w]
            hr = src.shape[1]
            for j, (px, py) in enumerate(chips):
                k = 3 * w + j
                landed = out.at[2 * px + py, pl.ds(c * hr, hr)]
                if w < n_new:
                    _remote(landed, landed, ici_send.at[k], ici_recv.at[k], sibling).wait_recv()
                cp = _remote(landed, landed, d2d_send.at[k], d2d_recv.at[k], sibling)
                cp.start()
                sends.append(cp)
        for w, (src, out) in enumerate(zip(srcs, outs)):
            hr = src.shape[1]
            other = out.at[me, pl.ds((1 - c) * hr, hr)]
            _remote(other, other, own_send.at[w], own_recv.at[w], sibling).wait_recv()
            for j, (px, py) in enumerate(chips):
                k = 3 * w + j
                other = out.at[2 * px + py, pl.ds((1 - c) * hr, hr)]
                _remote(other, other, d2d_send.at[k], d2d_recv.at[k], sibling).wait_recv()
        for cp in sends:
            cp.wait_send()

    n = 3 * nw
    dma = pltpu.SemaphoreType.DMA
    every = list(cparts) + list(done_cparts)
    every_lands = list(lands) + list(done_lands)
    return pl.pallas_call(
        body, name="scatter_partials",
        out_shape=[jax.ShapeDtypeStruct(l.shape, l.dtype) for l in every_lands],
        in_specs=[HBM] * (2 * nw), out_specs=[HBM] * nw,
        input_output_aliases={nw + i: i for i in range(nw)},
        scratch_shapes=[dma((nw,)), dma((nw,)), dma((n,)), dma((n,)), dma((n,)), dma((n,))],
    )(*every, *every_lands)


SEM = pl.BlockSpec(memory_space=pltpu.SEMAPHORE)
SPLIT_COPY = pltpu.CompilerParams(has_side_effects=pltpu.SideEffectType.DATAFLOW_SIDE_EFFECTING)


def _hbm(a):
    return pltpu.with_memory_space_constraint(a, pltpu.HBM)


def _gather_copies(srcs, lands, send_sems, recv_sems):
    x, y, c = _mesh_pos()
    me = 2 * x + y
    sends, recvs = [], []
    for w, (src, land) in enumerate(zip(srcs, lands)):
        hr = src.shape[0] // 2
        for j, (px, py) in enumerate(_other_chips(x, y)):
            k = 3 * w + j
            sends.append(_remote(src.at[pl.ds(c * hr, hr)], land.at[me, pl.ds(c * hr, hr)],
                                 send_sems.at[k], recv_sems.at[k], (px, py, c)))
            got = land.at[2 * px + py, pl.ds(c * hr, hr)]
            recvs.append(_remote(got, got, send_sems.at[k], recv_sems.at[k], (px, py, c)))
    return sends, recvs


def _scatter_copies(srcs, lands, send_sems, recv_sems):
    x, y, c = _mesh_pos()
    me = 2 * x + y
    sends, recvs = [], []
    for w, (src, land) in enumerate(zip(srcs, lands)):
        hr = src.shape[1]
        for j, (px, py) in enumerate(_other_chips(x, y)):
            k = 3 * w + j
            sends.append(_remote(src.at[2 * px + py], land.at[me, pl.ds(c * hr, hr)],
                                 send_sems.at[k], recv_sems.at[k], (px, py, c)))
            got = land.at[2 * px + py, pl.ds(c * hr, hr)]
            recvs.append(_remote(got, got, send_sems.at[k], recv_sems.at[k], (px, py, c)))
    return sends, recvs


def _sibling_copies(srcs, lands, send_sems, recv_sems):
    x, y, c = _mesh_pos()
    sibling = (x, y, 1 - c)
    sends, recvs = [], []
    for w, (src, land) in enumerate(zip(srcs, lands)):
        hr = src.shape[1] // 2
        sends.append(_remote(src.at[:, pl.ds((1 - c) * hr, hr)], land, send_sems.at[w], recv_sems.at[w], sibling))
        recvs.append(_remote(land, land, send_sems.at[w], recv_sems.at[w], sibling))
    return sends, recvs


def _inchip_copies(srcs, lands, send_sems, recv_sems):
    x, y, c = _mesh_pos()
    me = 2 * x + y
    sibling = (x, y, 1 - c)
    sends, recvs = [], []
    for w, (src, land) in enumerate(zip(srcs, lands)):
        hr = src.shape[1]
        mine, other = pl.ds(c * hr, hr), pl.ds((1 - c) * hr, hr)
        blocks = [(me, src.at[me])] + [(2 * px + py, None) for px, py in _other_chips(x, y)]
        for j, (blk, own_src) in enumerate(blocks):
            k = 4 * w + j
            landed = land.at[blk, mine]
            sends.append(_remote(landed if own_src is None else own_src, landed, send_sems.at[k], recv_sems.at[k], sibling))
            got = land.at[blk, other]
            recvs.append(_remote(got, got, send_sems.at[k], recv_sems.at[k], sibling))
    return sends, recvs


class _SemsFrom:
    def __init__(self, sems, first):
        self.sems, self.first = sems, first

    @property
    def at(self):
        return self

    def __getitem__(self, k):
        return self.sems.at[self.first + k]


def _shifted(copies_of, first):
    def copies(srcs, lands, send_sems, recv_sems):
        return copies_of(srcs, lands, _SemsFrom(send_sems, first), _SemsFrom(recv_sems, first))
    return copies


def _two_groups(copies_a, n_a, k_a, copies_b):
    shifted_b = _shifted(copies_b, k_a)

    def copies(srcs, lands, send_sems, recv_sems):
        sends_a, recvs_a = copies_a(srcs[:n_a], lands[:n_a], send_sems, recv_sems)
        sends_b, recvs_b = shifted_b(srcs[n_a:], lands[n_a:], send_sems, recv_sems)
        return sends_a + sends_b, recvs_a + recvs_b
    return copies


def _all_peers_copies(srcs, lands, send_sems, recv_sems):
    x, y, c = _mesh_pos()
    (src,), (land,) = srcs, lands
    flip = lambda v, bit: 1 - v if bit else v
    sends, recvs = [], []
    for k in range(N_DEV - 1):
        px, py, pc = flip(x, (k + 1) & 4), flip(y, (k + 1) & 2), flip(c, (k + 1) & 1)
        sends.append(_remote(src, land.at[4 * x + 2 * y + c], send_sems.at[k], recv_sems.at[k], (px, py, pc)))
        got = land.at[4 * px + 2 * py + pc]
        recvs.append(_remote(got, got, send_sems.at[k], recv_sems.at[k], (px, py, pc)))
    return sends, recvs


def _split_start(name, copies_of, srcs, land_shapes, n_copies=None):
    n = len(srcs)
    k = 3 * n if n_copies is None else n_copies

    def body(*refs):
        src_refs, land_refs = refs[:n], refs[n:2 * n]
        send_sems, recv_sems = refs[2 * n], refs[2 * n + 1]
        token = refs[-1]
        sends, _ = copies_of(src_refs, land_refs, send_sems, recv_sems)
        for cp in sends:
            cp.start()
        token[...] = jnp.zeros_like(token)

    lands = [_hbm(s) for s in land_shapes]
    dma = pltpu.SemaphoreType.DMA
    res = pl.pallas_call(
        body, name=name,
        out_shape=(dma((k,)), dma((k,)), *[pltpu.HBM(s.shape, s.dtype) for s in srcs],
                   *[pltpu.HBM(s.shape, s.dtype) for s in land_shapes], jax.ShapeDtypeStruct((8, 128), F32)),
        in_specs=[HBM] * (2 * n),
        out_specs=(SEM, SEM, *([HBM] * (2 * n)), pl.BlockSpec(memory_space=pltpu.VMEM)),
        input_output_aliases={i: 2 + i for i in range(2 * n)},
        compiler_params=SPLIT_COPY,
    )(*[_hbm(s) for s in srcs], *lands)
    return res[0], res[1], list(res[2:2 + n]), list(res[2 + n:2 + 2 * n]), res[-1]


def _split_wait(name, copies_of, send_sems, recv_sems, srcs, lands, after):
    n = len(srcs)

    def body(*refs):
        src_refs, land_refs = refs[:n], refs[n:2 * n]
        sends, recvs = copies_of(src_refs, land_refs, refs[2 * n], refs[2 * n + 1])
        for cp in sends:
            cp.wait_send()
        for cp in recvs:
            cp.wait_recv()

    res = pl.pallas_call(
        body, name=name,
        out_shape=tuple(pltpu.HBM(s.shape, s.dtype) for s in list(srcs) + list(lands)),
        in_specs=[HBM] * (2 * n) + [SEM, SEM] + [pl.BlockSpec(memory_space=pl.ANY)] * len(after),
        out_specs=tuple([HBM] * (2 * n)),
        input_output_aliases={i: i for i in range(2 * n)},
        compiler_params=SPLIT_COPY,
    )(*srcs, *lands, send_sems, recv_sems, *after)
    return list(res[:n]), list(res[n:])


def _forward_copies(srcs, lands, send_sems, recv_sems):
    x, y, c = _mesh_pos()
    sibling = (x, y, 1 - c)
    sends, recvs = [], []
    for w, land in enumerate(lands):
        hr = land.shape[1] // 2
        for j, (px, py) in enumerate(_other_chips(x, y)):
            k = 3 * w + j
            landed = land.at[2 * px + py, pl.ds(c * hr, hr)]
            sends.append(_remote(landed, landed, send_sems.at[k], recv_sems.at[k], sibling))
            other = land.at[2 * px + py, pl.ds((1 - c) * hr, hr)]
            recvs.append(_remote(other, other, send_sems.at[k], recv_sems.at[k], sibling))
    return sends, recvs


def _gather_finish(lands, n_forward):
    n = len(lands)

    def body(*refs):
        outs = refs[n:n + n_forward]
        d2d_send, d2d_recv = refs[2 * n:]
        sends, recvs = _forward_copies(None, outs, d2d_send, d2d_recv)
        for cp in sends:
            cp.start()
        for cp in recvs:
            cp.wait_recv()
        for cp in sends:
            cp.wait_send()

    dma = pltpu.SemaphoreType.DMA
    return pl.pallas_call(
        body, name="gather_finish",
        out_shape=[jax.ShapeDtypeStruct(l.shape, l.dtype) for l in lands],
        in_specs=[HBM] * n, out_specs=[HBM] * n,
        input_output_aliases={i: i for i in range(n)},
        scratch_shapes=[dma((3 * n,)), dma((3 * n,))],
    )(*lands)


def _adamw(w, g, m, v):
    m = ADAM_B1 * m + (1.0 - ADAM_B1) * g
    v = ADAM_B2 * v + (1.0 - ADAM_B2) * (g * g)
    m_hat = m / (1.0 - ADAM_B1 ** ADAM_STEP)
    v_hat = v / (1.0 - ADAM_B2 ** ADAM_STEP)
    delta = -ADAM_LR * (m_hat / (jnp.sqrt(v_hat) + ADAM_EPS) + ADAM_WD * w)
    return delta, m, v


def _adamw_big(partials, w, m, v):
    rows, cols = w.shape
    tr = _elementwise_tile(rows)

    def body(p_ref, w_ref, m_ref, v_ref, g_ref, d_ref, m2_ref, v2_ref):
        g = ((p_ref[0].astype(F32) + p_ref[1].astype(F32)) + p_ref[2].astype(F32)) + p_ref[3].astype(F32)
        g_ref[...] = g
        d_ref[...], m2_ref[...], v2_ref[...] = _adamw(w_ref[...], g, m_ref[...], v_ref[...])

    blk = pl.BlockSpec((tr, cols), lambda i: (i, 0))
    return pl.pallas_call(
        body, name="adamw_big", grid=(rows // tr,),
        in_specs=[pl.BlockSpec((N_CHIPS, tr, cols), lambda i: (0, i, 0)), blk, blk, blk],
        out_specs=[blk] * 4, out_shape=[jax.ShapeDtypeStruct((rows, cols), F32)] * 4,
        compiler_params=_params("parallel"),
    )(partials, w, m, v)


def _sum_devices(gathered, rows):
    cols = gathered.shape[1]

    def body(g_ref, o_ref):
        acc = g_ref[0:rows]
        for d in range(1, N_DEV):
            acc = acc + g_ref[d * rows:(d + 1) * rows]
        o_ref[...] = acc

    return pl.pallas_call(
        body, name="sum_devices", out_shape=jax.ShapeDtypeStruct((rows, cols), F32),
        in_specs=[pl.BlockSpec(memory_space=pltpu.VMEM)], out_specs=pl.BlockSpec(memory_space=pltpu.VMEM),
        compiler_params=pltpu.CompilerParams(vmem_limit_bytes=VMEM_LIMIT_V7X),
    )(gathered)


def _adamw_small(quads):
    n = len(quads)

    def body(*refs):
        ins, outs = refs[:4 * n], refs[4 * n:]
        for t in range(n):
            w, g, m, v = (r[...] for r in ins[4 * t:4 * t + 4])
            outs[3 * t][...], outs[3 * t + 1][...], outs[3 * t + 2][...] = _adamw(w, g, m, v)

    flat = [a for q in quads for a in q]
    vm = pl.BlockSpec(memory_space=pltpu.VMEM)
    res = pl.pallas_call(
        body, name="adamw_small",
        out_shape=[jax.ShapeDtypeStruct(q[0].shape, F32) for q in quads for _ in range(3)],
        in_specs=[vm] * (4 * n), out_specs=[vm] * (3 * n),
    )(*flat)
    return [tuple(res[3 * t:3 * t + 3]) for t in range(n)]


SMALL_PACK_ROWS = 96
META_COLS = D_MODEL // N_CHIPS
CONV_COLS = LRU_WIDTH // N_CHIPS
_WEIGHTS = ['meta_tokens', 'g_pre_mix', 'w_in', 'conv_w', 'conv_b', 'w_a', 'b_a', 'w_x', 'b_x', 'lru_lambda',
            'attn_sinks', 'w_out', 'g_post_mix', 'g_pre_ffn', 'w_ff1', 'w_ff2', 'g_post_ffn']
_BIG = ['w_in', 'w_out', 'w_ff1', 'w_ff2']


def _pack_small(dmeta, g, loss):
    z = lambda r, c: jnp.zeros((r, c), F32)
    rows = [
        dmeta,
        g['g_pre_mix'], g['g_post_mix'], g['g_pre_ffn'], g['g_post_ffn'],
        jnp.concatenate([g['conv_w'], z(4, 512)], axis=1),
        jnp.concatenate([g['conv_b'], g['b_a']], axis=1),
        jnp.concatenate([g['b_x'], g['lru_lambda']], axis=1),
        jnp.concatenate([g['attn_sinks'], z(1, D_MODEL - ATTN_HEADS)], axis=1),
        jnp.concatenate([loss, z(1, D_MODEL - 1)], axis=1),
        z(4, D_MODEL),
        g['w_a'].reshape(32, D_MODEL), g['w_x'].reshape(32, D_MODEL),
    ]
    return jnp.concatenate(rows, axis=0)


def _unpack_small(s, chip):
    return dict(
        meta_tokens=lax.dynamic_slice(s[0:N_META], (0, chip * META_COLS), (N_META, META_COLS)),
        g_pre_mix=s[16:17], g_post_mix=s[17:18], g_pre_ffn=s[18:19], g_post_ffn=s[19:20],
        conv_w=lax.dynamic_slice(s[20:24], (0, chip * CONV_COLS), (4, CONV_COLS)).reshape(1, 4, CONV_COLS),
        conv_b=s[24:25, :512], b_a=s[24:25, 512:], b_x=s[25:26, :512], lru_lambda=s[25:26, 512:],
        attn_sinks=s[26:27, :ATTN_HEADS], loss=s[27, 0],
        w_a=s[32:64].reshape(1, LRU_BLOCKS, LRU_BLOCK, LRU_BLOCK),
        w_x=s[64:96].reshape(1, LRU_BLOCKS, LRU_BLOCK, LRU_BLOCK))


def _as2d(a):
    if a.ndim == 2:
        return a
    return a.reshape(-1, a.shape[-1])


def kernel(x, meta_tokens, g_pre_mix, w_in, conv_w, conv_b, w_a, b_a, w_x, b_x, lru_lambda, attn_sinks, w_out, g_post_mix, g_pre_ffn, w_ff1, w_ff2, g_post_ffn, loss_target, m_meta_tokens, m_g_pre_mix, m_w_in, m_conv_w, m_conv_b, m_w_a, m_b_a, m_w_x, m_b_x, m_lru_lambda, m_attn_sinks, m_w_out, m_g_post_mix, m_g_pre_ffn, m_w_ff1, m_w_ff2, m_g_post_ffn, v_meta_tokens, v_g_pre_mix, v_w_in, v_conv_w, v_conv_b, v_w_a, v_b_a, v_w_x, v_b_x, v_lru_lambda, v_attn_sinks, v_w_out, v_g_post_mix, v_g_pre_ffn, v_w_ff1, v_w_ff2, v_g_post_ffn):
    weights = dict(meta_tokens=meta_tokens, g_pre_mix=g_pre_mix, w_in=w_in, conv_w=conv_w, conv_b=conv_b, w_a=w_a,
                   b_a=b_a, w_x=w_x, b_x=b_x, lru_lambda=lru_lambda, attn_sinks=attn_sinks, w_out=w_out,
                   g_post_mix=g_post_mix, g_pre_ffn=g_pre_ffn, w_ff1=w_ff1, w_ff2=w_ff2, g_post_ffn=g_post_ffn)
    mom1 = dict(zip(_WEIGHTS, [m_meta_tokens, m_g_pre_mix, m_w_in, m_conv_w, m_conv_b, m_w_a, m_b_a, m_w_x, m_b_x,
                               m_lru_lambda, m_attn_sinks, m_w_out, m_g_post_mix, m_g_pre_ffn, m_w_ff1, m_w_ff2,
                               m_g_post_ffn]))
    mom2 = dict(zip(_WEIGHTS, [v_meta_tokens, v_g_pre_mix, v_w_in, v_conv_w, v_conv_b, v_w_a, v_b_a, v_w_x, v_b_x,
                               v_lru_lambda, v_attn_sinks, v_w_out, v_g_post_mix, v_g_pre_ffn, v_w_ff1, v_w_ff2,
                               v_g_post_ffn]))
    xi, yi, ci = _mesh_pos()
    chip = 2 * xi + yi

    tiny = jnp.concatenate([meta_tokens, jnp.pad(conv_w[0], ((0, 4), (0, 128)))], axis=0)
    chip_arr = jnp.reshape(chip, (1,)).astype(jnp.int32)
    big2d = lambda a, name: a[0].T if name == 'w_in' else a[0]
    shards, lands = zip(*[_prep_shard(big2d(weights[n], n), chip_arr) for n in _BIG])
    g_in, g_tiny = _gather_weights(shards[:1], lands[:1], tiny, _prep_tiny(tiny, chip_arr))
    w_in_full = g_in.reshape(IN_WIDTH, D_MODEL)
    meta_full = jnp.concatenate([g_tiny[j, :N_META] for j in range(N_CHIPS)], axis=1)
    conv_w_full = jnp.concatenate([g_tiny[j, N_META:N_META + 4, :128] for j in range(N_CHIPS)], axis=1)
    g_send, g_recv, late_thru, late_lands, token = _split_start(
        "gather_late_start", _gather_copies, shards[1:], lands[1:])

    def late_weights(after):
        thru, landed = _split_wait("gather_late_wait", _gather_copies, g_send, g_recv, late_thru, late_lands, after)
        f_send, f_recv, f_thru, f_lands, _ = _split_start("gather_forward_start", _forward_copies, thru, landed)
        _, (g_out,) = _split_wait("gather_out_wait", _forward_copies, f_send, f_recv, f_thru[:1], f_lands[:1], [])

        def ffn_weights(after):
            _, (g_f1, g_f2) = _split_wait("gather_forward_wait", _shifted(_forward_copies, 3), f_send, f_recv,
                                          f_thru[1:], f_lands[1:], after)
            return g_f1, g_f2

        return g_out.reshape(D_MODEL, D_MODEL), ffn_weights

    pos = jnp.stack([ci, chip]).astype(jnp.int32)
    ffn = {}


    def on_ffn_grads(dw1, dw2):
        parts = [dw1, dw2]
        lands = [lax.empty((p.shape[0], p.shape[1] // 2, p.shape[2]), p.dtype) for p in parts]
        ffn['sib'] = _split_start("sibling_ffn_start", _sibling_copies, parts, lands, len(parts))
        return ffn['sib'][4]

    def on_outproj_bwd(dattn):
        send, recv, thru, lands, _ = ffn['sib']
        parts, from_sibling = _split_wait("sibling_ffn_wait", _sibling_copies, send, recv, thru, lands, [dattn])
        cparts_ffn, lands_ffn = zip(*[_chip_presum(p, r, pos) for p, r in zip(parts, from_sibling)])
        ffn['send'], ffn['recv'], ffn['thru'], ffn['lands'], token3 = _split_start(
            "scatter_ffn_start", _scatter_copies, cparts_ffn, lands_ffn)
        return token3

    def on_mixer_grads(dw_in, dw_out):
        parts = [dw_in.reshape(N_CHIPS, IN_WIDTH // N_CHIPS, D_MODEL),
                 dw_out.reshape(N_CHIPS, D_MODEL // N_CHIPS, D_MODEL)]
        cparts, lands = zip(*[_chip_presum(p, r, pos) for p, r in zip(parts, _sibling_exchange(parts, pos))])
        ffn_cparts, ffn_lands = _split_wait("scatter_ffn_wait", _scatter_copies, ffn['send'], ffn['recv'],
                                            ffn['thru'], ffn['lands'], list(cparts))
        n_ici = 3 * len(cparts)
        ffn['n_ici'] = n_ici
        ffn['mixer'] = _split_start("scatter_mixer_start", _two_groups(_scatter_copies, len(cparts), n_ici, _inchip_copies),
                                    list(cparts) + ffn_cparts, list(lands) + ffn_lands, n_ici + 4 * len(ffn_cparts))
        return ffn['mixer'][4]

    head = jnp.concatenate([jnp.zeros((PAD_ROWS, D_MODEL), F32), meta_full], axis=0)
    loss, dx, dhead, grads = _local_step(head, x[0], loss_target[0], g_pre_mix, w_in_full, conv_w_full, conv_b, w_a[0],
                                         b_a, w_x[0], b_x, lru_lambda, attn_sinks, g_post_mix, g_pre_ffn, g_post_ffn,
                                         late_weights, on_ffn_grads, on_outproj_bwd, on_mixer_grads, token)
    grad_x = dx[None]

    pack = _pack_small(dhead[PAD_ROWS:], grads, loss)
    dev = jnp.reshape(4 * xi + 2 * yi + ci, (1,)).astype(jnp.int32)
    s_send, s_recv, s_thru, s_lands, token5 = _split_start(
        "gather_small_start", _all_peers_copies, [pack], [_prep_tiny(pack, dev, N_DEV)], N_DEV - 1)

    send, recv, thru, lands, _ = ffn['mixer']
    nm = len(thru) // 2
    mixer_cparts, mixer_lands = _split_wait("scatter_mixer_wait", _scatter_copies, send, recv, thru[:nm], lands[:nm],
                                            [token5])
    m_send, m_recv, m_thru, m_lands, token6 = _split_start(
        "inchip_mixer_start", _inchip_copies, mixer_cparts, mixer_lands, 4 * len(mixer_cparts))
    _, ffn_partials = _split_wait("inchip_ffn_wait", _shifted(_inchip_copies, ffn['n_ici']), send, recv, thru[nm:],
                                  lands[nm:], [token6])

    g_out_d, delta, new_m, new_v = {}, {}, {}, {}

    def adamw_big(names, partials):
        for name, part in zip(names, partials):
            shp = weights[name].shape
            res = _adamw_big(part, big2d(weights[name], name), big2d(mom1[name], name), big2d(mom2[name], name))
            g_out_d[name], delta[name], new_m[name], new_v[name] = (big2d(r[None], name).reshape(shp) for r in res)

    adamw_big(_BIG[2:], ffn_partials)
    _, mixer_partials = _split_wait("inchip_mixer_wait", _inchip_copies, m_send, m_recv, m_thru, m_lands,
                                    [g_out_d[n] for n in _BIG[2:]])
    adamw_big(_BIG[:2], mixer_partials)

    _, (gathered,) = _split_wait("gather_small_wait", _all_peers_copies, s_send, s_recv, s_thru, s_lands,
                                 [g_out_d[n] for n in _BIG])
    small = _unpack_small(_sum_devices(gathered.reshape(N_DEV * SMALL_PACK_ROWS, D_MODEL), SMALL_PACK_ROWS), chip)
    loss = small['loss']
    small_names = [n for n in _WEIGHTS if n not in _BIG]
    quads = [(_as2d(weights[n]), _as2d(small[n]), _as2d(mom1[n]), _as2d(mom2[n])) for n in small_names]
    for name, (d, m2, v2) in zip(small_names, _adamw_small(quads)):
        shp = weights[name].shape
        g_out_d[name] = small[name].reshape(shp)
        delta[name], new_m[name], new_v[name] = d.reshape(shp), m2.reshape(shp), v2.reshape(shp)

    return (loss, grad_x, *[g_out_d[n] for n in _WEIGHTS], *[delta[n] for n in _WEIGHTS],
            *[new_m[n] for n in _WEIGHTS], *[new_v[n] for n in _WEIGHTS])
```

```python
import numpy as np
import jax
import jax.numpy as jnp
from jax import lax
from jax.experimental import pallas as pl
from jax.experimental.pallas import tpu as pltpu

F32 = jnp.float32
BF16 = jnp.bfloat16

D_MODEL = 1024
N_META = 16
BLOCK = 128
PAD_ROWS = BLOCK - N_META
HEAD_DIM = 64
ATTN_HEADS = 8
GQA_GROUP = 4
ATTN_WIDTH = 512
KV_WIDTH = 128
QKV_WIDTH = ATTN_WIDTH + 2 * KV_WIDTH
LRU_WIDTH = 512
LRU_BLOCKS = 8
LRU_BLOCK = 64
LRU_C = 8.0
IN_WIDTH = 1792
D_FF = 4096
N_CHIPS = 4
FF_CHUNK = D_FF // N_CHIPS
EPS = 1e-6
NEG = -1e30

ADAM_LR = 0.001
ADAM_B1 = 0.9
ADAM_B2 = 0.999
ADAM_EPS = 1e-08
ADAM_WD = 0.01
ADAM_STEP = 10

VMEM_LIMIT_V7X = 62 * 1024 * 1024
MESH = pl.DeviceIdType.MESH

NT = (((1,), (1,)), ((), ()))
TN = (((0,), (0,)), ((), ()))


def _row_tile(tp):
    return 640 if tp % 640 == 0 else BLOCK


def _elementwise_tile(rows):
    return 512 if rows % 512 == 0 else rows


def _wgrad_row_tile(tp):
    return 1664 if tp % 1664 == 0 else _row_tile(tp)


def _params(*sem):
    return pltpu.CompilerParams(dimension_semantics=sem, vmem_limit_bytes=VMEM_LIMIT_V7X)


def _dot(a, b):
    return jnp.dot(a, b, preferred_element_type=F32)


def _dot_nt(a, b):
    return lax.dot_general(a, b, NT, preferred_element_type=F32)


def _dot_tn(a, b):
    return lax.dot_general(a, b, TN, preferred_element_type=F32)


def _rms(x):
    rs = lax.rsqrt(jnp.mean(x * x, axis=-1, keepdims=True) + EPS)
    return x * rs, rs


def _rms_bwd(xhat, rs, g, dy):
    dyg = dy * g
    dx = rs * (dyg - xhat * jnp.mean(dyg * xhat, axis=-1, keepdims=True))
    dg = jnp.sum(dy * xhat, axis=0, keepdims=True)
    return dx, dg


def _gelu(x):
    k = 0.7978845608028654
    t = jnp.tanh(x * (k + (k * 0.044715) * (x * x)))
    return (0.5 * x) * (1.0 + t), t


def _gelu_grad(x, t):
    k = 0.7978845608028654
    return 0.5 * (1.0 + t) + 0.5 * x * (1.0 - t * t) * k * (1.0 + 3 * 0.044715 * x * x)


def _sigmoid(x):
    return 0.5 * jnp.tanh(0.5 * x) + 0.5


def _one_minus_exp2(y):
    t = jnp.tanh(y)
    return (-2.0 * t) / (1.0 - t)


def _softplus(x):
    return jnp.maximum(x, 0.0) + jnp.log1p(jnp.exp(-jnp.abs(x)))


def _seq_specs(tr, delay=0):
    qb = tr // BLOCK
    tile = lambda i: jnp.maximum(i - delay, 0)
    return [pl.BlockSpec((BLOCK, D_MODEL), lambda i, *_, s=s: (jnp.maximum(tile(i) * qb + s - 1, 0), 0))
            for s in range(qb)]


def _seq_tile(head, pieces, i):
    first = jnp.where(i == 0, head, pieces[0][...])
    return jnp.concatenate([first] + [p[...] for p in pieces[1:]], axis=0)


GROUP_ROWS = GQA_GROUP * BLOCK


def _attn_bias():
    j = np.arange(2 * BLOCK)[:, None]
    i = np.arange(BLOCK)[None, :]
    band = (j - i >= 1) & (j - i <= BLOCK)
    out = []
    for n in range(3):
        ok = band & ((n - 1) * BLOCK + j >= PAD_ROWS) if n < 2 else band
        out.append(np.tile(np.where(ok, 0.0, NEG).astype(np.float32), (1, GQA_GROUP)))
    return jnp.asarray(np.stack(out))


def _heads_t(at, g):
    heads = range(GQA_GROUP * g, GQA_GROUP * (g + 1))
    return jnp.concatenate([at[h * HEAD_DIM:(h + 1) * HEAD_DIM] for h in heads], axis=1).astype(BF16)


def _from_heads_t(groups):
    pairs = []
    for p in groups:
        for h in range(0, GQA_GROUP, 2):
            two = jnp.concatenate([p[:, h * BLOCK:(h + 1) * BLOCK], p[:, (h + 1) * BLOCK:(h + 2) * BLOCK]], axis=0)
            pairs.append(two.T)
    return jnp.concatenate(pairs, axis=1)


def _stack_heads(a, g):
    heads = range(GQA_GROUP * g, GQA_GROUP * (g + 1))
    return jnp.concatenate([a[:, h * HEAD_DIM:(h + 1) * HEAD_DIM] for h in heads], axis=0)


def _unstack_heads(groups):
    return jnp.concatenate([p[h * BLOCK:(h + 1) * BLOCK] for p in groups for h in range(GQA_GROUP)], axis=1)


def _attn_probs_t(k_g, qg, bias, sink_row):
    st = _dot_nt(k_g, qg) + bias
    m = jnp.maximum(jnp.max(st, axis=0, keepdims=True), sink_row)
    p = jnp.exp(st - m)
    es = jnp.exp(sink_row - m)
    inv = 1.0 / (jnp.sum(p, axis=0, keepdims=True) + es)
    return p * inv, es * inv


def _attn_consts(sinks):
    return jnp.repeat(sinks.reshape(ATTN_HEADS), BLOCK).reshape(ATTN_HEADS // GQA_GROUP, GROUP_ROWS), _attn_bias()


_SINK_SPEC = pl.BlockSpec((ATTN_HEADS // GQA_GROUP, GROUP_ROWS), lambda n: (0, 0))
_BIAS_SPEC = pl.BlockSpec((3, 2 * BLOCK, GROUP_ROWS), lambda n: (0, 0, 0))
_QSCALE = HEAD_DIM ** -0.5


def _kv_specs(tr):
    qb = tr // BLOCK
    prev = lambda col: pl.BlockSpec((BLOCK, KV_WIDTH), lambda t: (jnp.maximum(t * qb - 1, 0), col))
    cur = lambda col: pl.BlockSpec((tr, KV_WIDTH), lambda t: (t, col))
    return [prev(4), cur(4), prev(5), cur(5)]


def _block_bias(b_ref, t, qb, i):
    return b_ref[2] if i >= 2 else b_ref[jnp.minimum(t * qb + i, 2)]


N_KV = ATTN_HEADS // GQA_GROUP


def _prob_specs(qb):
    return [pl.BlockSpec((qb, N_KV, 2 * BLOCK, GROUP_ROWS), lambda t: (t, 0, 0, 0)),
            pl.BlockSpec((qb, SUBLANES, GROUP_ROWS), lambda t: (t, 0, 0))]


def _attn_fwd(qkv, sinks):
    tp = qkv.shape[0]
    tr = _row_tile(tp)
    qb, nb = tr // BLOCK, tp // BLOCK
    sink_rows, bias = _attn_consts(sinks)

    def body(s_ref, b_ref, q_ref, kp_ref, kc_ref, vp_ref, vc_ref, o_ref, p_ref, ps_ref):
        t = pl.program_id(0)
        k_all = jnp.concatenate([kp_ref[...], kc_ref[...]], axis=0)
        v_all = jnp.concatenate([vp_ref[...], vc_ref[...]], axis=0)
        for i in range(qb):
            rows = slice(i * BLOCK, (i + 1) * BLOCK)
            q = q_ref[rows]
            k2, v2 = k_all[i * BLOCK:(i + 2) * BLOCK], v_all[i * BLOCK:(i + 2) * BLOCK]
            bias_n = _block_bias(b_ref, t, qb, i)
            outs, sink_probs = [], []
            for g in range(N_KV):
                cols = slice(g * HEAD_DIM, (g + 1) * HEAD_DIM)
                qg = _stack_heads(q, g) * jnp.asarray(_QSCALE, BF16)
                p, ps = _attn_probs_t(k2[:, cols], qg, bias_n, s_ref[g:g + 1])
                pb = p.astype(BF16)
                p_ref[i, g] = pb
                sink_probs.append(ps)
                outs.append(_dot_tn(pb, v2[:, cols]))
            o_ref[rows] = _unstack_heads(outs).astype(BF16)
            ps_ref[i] = jnp.concatenate(sink_probs + [jnp.zeros((SUBLANES - N_KV, GROUP_ROWS), F32)], axis=0)

    return pl.pallas_call(
        body, name="attn_fwd", grid=(tp // tr,),
        in_specs=[_SINK_SPEC, _BIAS_SPEC, pl.BlockSpec((tr, ATTN_WIDTH), lambda t: (t, 0))] + _kv_specs(tr),
        out_specs=[pl.BlockSpec((tr, ATTN_WIDTH), lambda t: (t, 0))] + _prob_specs(qb),
        out_shape=[jax.ShapeDtypeStruct((tp, ATTN_WIDTH), BF16),
                   jax.ShapeDtypeStruct((nb, N_KV, 2 * BLOCK, GROUP_ROWS), BF16),
                   jax.ShapeDtypeStruct((nb, SUBLANES, GROUP_ROWS), F32)],
        compiler_params=_params("parallel"),
    )(sink_rows, bias, qkv, qkv, qkv, qkv, qkv)


def _conv_taps(x, halo):
    ext = jnp.concatenate([halo, x], axis=0)
    return [ext[8:] if k == 3 else pltpu.roll(ext, 3 - k, 0)[8:] for k in range(4)]


def _lru_gates(xc, wa, ba, wx, bx, sp):
    xb = xc.astype(BF16)
    r = _sigmoid(_dot(xb, wa) + ba)
    ig = _sigmoid(_dot(xb, wx) + bx)
    log_a = (-LRU_C * sp) * r
    a = jnp.exp(log_a)
    mult = jnp.sqrt(_one_minus_exp2(log_a))
    return xb, r, ig, a, mult


SUBLANES = 8


def _scan_fwd(a, b, h_in):
    n, width = a.shape
    a, b = (v.reshape(n // SUBLANES, SUBLANES, width) for v in (a, b))
    in_group = lax.broadcasted_iota(jnp.int32, a.shape, 1)
    for d in (1, 2, 4):
        keep = in_group >= d
        b = jnp.where(keep, a * pltpu.roll(b, d, 1) + b, b)
        a = jnp.where(keep, a * pltpu.roll(a, d, 1), a)
    a, b = a.reshape(n, width), b.reshape(n, width)
    out, carry = [], h_in
    for g in range(0, n, SUBLANES):
        h = a[g:g + SUBLANES] * carry + b[g:g + SUBLANES]
        out.append(h)
        carry = h[SUBLANES - 1:]
    return jnp.concatenate(out, axis=0)


def _scan_rev(c, b, g_in):
    n, width = c.shape
    c, b = (v.reshape(n // SUBLANES, SUBLANES, width) for v in (c, b))
    in_group = lax.broadcasted_iota(jnp.int32, c.shape, 1)
    for d in (1, 2, 4):
        keep = in_group < SUBLANES - d
        b = jnp.where(keep, b + c * pltpu.roll(b, SUBLANES - d, 1), b)
        c = jnp.where(keep, c * pltpu.roll(c, SUBLANES - d, 1), c)
    c, b = c.reshape(n, width), b.reshape(n, width)
    out, carry = [], g_in
    for g in range(n - SUBLANES, -1, -SUBLANES):
        r = b[g:g + SUBLANES] + c[g:g + SUBLANES] * carry
        out.append(r)
        carry = r[:1]
    return jnp.concatenate(out[::-1], axis=0)


def _inproj_lru_fwd(head, x, g, w_in, conv_w, conv_b, wa, ba, wx, bx, lam, token):
    tp = BLOCK + x.shape[0]
    tr = _row_tile(tp)
    qb, nt = tr // BLOCK, tp // tr
    small = [conv_w, conv_b, wa, ba, wx, bx, lam]

    def body(*refs):
        head_ref, pieces = refs[0], refs[1:1 + qb]
        g_ref, w_ref, _, cw_ref, cb_ref, wa_ref, ba_ref, wx_ref, bx_ref, lam_ref = refs[1 + qb:11 + qb]
        u_ref, qkv_ref, xr_ref, yr_ref, hr_ref, rec_ref, zbuf, halo, hprev = refs[11 + qb:]
        i = pl.program_id(0)
        cur = i % 2

        @pl.when(i == 0)
        def _():
            halo[...] = jnp.zeros_like(halo)
            hprev[...] = jnp.zeros_like(hprev)
            zbuf[1] = jnp.zeros((tr, 2 * LRU_WIDTH), F32)

        def recurrent_branch(valid):
            cw, cb = cw_ref[...], cb_ref[...]
            wa_m, ba_v, wx_m, bx_v = wa_ref[...], ba_ref[...], wx_ref[...], bx_ref[...]
            sp = _softplus(-lam_ref[...])
            before, h_last = halo[...], hprev[0:1]
            for b in range(qb):
                rows = slice(b * BLOCK, (b + 1) * BLOCK)
                xy = zbuf[1 - cur, rows]
                xin = xy[:, :LRU_WIDTH]
                taps = _conv_taps(xin, before)
                before = xin[BLOCK - 8:]
                xc = cb + sum(cw[k:k + 1] * taps[k] for k in range(4))
                _, _, ig, a, mult = _lru_gates(xc, wa_m, ba_v, wx_m, bx_v, sp)
                u = mult * (ig * xc)
                if b == 0:
                    pos = (i - 1) * tr + lax.broadcasted_iota(jnp.int32, xc.shape, 0)
                    u = jnp.where(pos >= PAD_ROWS, u, 0.0)
                h = _scan_fwd(a, u, h_last)
                h_last = h[BLOCK - 1:]
                hr_ref[rows] = h
                gl, _ = _gelu(xy[:, LRU_WIDTH:])
                rec_ref[rows] = (gl * h).astype(BF16)
            halo[...] = jnp.where(valid, before, 0.0)
            hprev[0:1] = jnp.where(valid, h_last, 0.0)

        def projection():
            xhat, _ = _rms(_seq_tile(head_ref[...], pieces, i))
            u = (xhat * g_ref[...]).astype(BF16)
            u_ref[...] = u
            z = _dot_nt(u, w_ref[...])
            qkv_ref[...] = z[:, :QKV_WIDTH].astype(BF16)
            xr_ref[...] = z[:, QKV_WIDTH:QKV_WIDTH + LRU_WIDTH]
            yr_ref[...] = z[:, QKV_WIDTH + LRU_WIDTH:]
            zbuf[cur] = z[:, QKV_WIDTH:]

        @pl.when(i < nt)
        def _():
            recurrent_branch(i >= 1)
            projection()

        @pl.when(i == nt)
        def _():
            recurrent_branch(True)

    last = nt - 1
    this_row = lambda w: pl.BlockSpec((tr, w), lambda i: (jnp.minimum(i, last), 0))
    prev_row = lambda w: pl.BlockSpec((tr, w), lambda i: (jnp.maximum(i - 1, 0), 0))
    full = lambda a: pl.BlockSpec(a.shape, lambda i: (0,) * a.ndim)
    piece_specs = [pl.BlockSpec((BLOCK, D_MODEL), lambda i, s=s: (jnp.maximum(jnp.minimum(i, last) * qb + s - 1, 0), 0))
                   for s in range(qb)]
    return pl.pallas_call(
        body, name="inproj_lru_fwd", grid=(nt + 1,),
        in_specs=[full(head)] + piece_specs + [full(g), full(w_in), full(token)] + [full(a) for a in small],
        out_specs=[this_row(D_MODEL), this_row(QKV_WIDTH), this_row(LRU_WIDTH), this_row(LRU_WIDTH),
                   prev_row(LRU_WIDTH), prev_row(LRU_WIDTH)],
        out_shape=[jax.ShapeDtypeStruct((tp, D_MODEL), BF16), jax.ShapeDtypeStruct((tp, QKV_WIDTH), BF16),
                   jax.ShapeDtypeStruct((tp, LRU_WIDTH), F32), jax.ShapeDtypeStruct((tp, LRU_WIDTH), F32),
                   jax.ShapeDtypeStruct((tp, LRU_WIDTH), F32), jax.ShapeDtypeStruct((tp, LRU_WIDTH), BF16)],
        scratch_shapes=[pltpu.VMEM((2, tr, 2 * LRU_WIDTH), F32), pltpu.VMEM((8, LRU_WIDTH), F32),
                        pltpu.VMEM((8, LRU_WIDTH), F32)],
        compiler_params=_params("arbitrary"),
    )(head, *([x] * qb), g, w_in, token, *small)


def _outproj_fwd(attn, rec, w_out, head, x, g_post_mix, g_pre_ffn):
    tp = attn.shape[0]
    tr = _row_tile(tp)
    qb = tr // BLOCK

    def body(*refs):
        a_ref, r_ref, w_ref, head_ref = refs[:4]
        pieces = refs[4:4 + qb]
        gm_ref, gf_ref, mix_ref, h1_ref, u1_ref = refs[4 + qb:]
        mix = _dot(a_ref[...], w_ref[:ATTN_WIDTH]) + _dot(r_ref[...], w_ref[ATTN_WIDTH:])
        mix_ref[...] = mix
        mhat, _ = _rms(mix)
        h1 = _seq_tile(head_ref[...], pieces, pl.program_id(0)) + mhat * gm_ref[...]
        h1_ref[...] = h1
        hhat, _ = _rms(h1)
        u1_ref[...] = (hhat * gf_ref[...]).astype(BF16)

    row = lambda w: pl.BlockSpec((tr, w), lambda i: (i, 0))
    full = lambda a: pl.BlockSpec(a.shape, lambda i: (0,) * a.ndim)
    return pl.pallas_call(
        body, name="outproj_fwd", grid=(tp // tr,),
        in_specs=[row(ATTN_WIDTH), row(LRU_WIDTH), full(w_out), full(head)] + _seq_specs(tr)
        + [full(g_post_mix), full(g_pre_ffn)],
        out_specs=[row(D_MODEL), row(D_MODEL), row(D_MODEL)],
        out_shape=[jax.ShapeDtypeStruct((tp, D_MODEL), F32), jax.ShapeDtypeStruct((tp, D_MODEL), F32),
                   jax.ShapeDtypeStruct((tp, D_MODEL), BF16)],
        compiler_params=_params("parallel"),
    )(attn, rec, w_out, head, *([x] * qb), g_post_mix, g_pre_ffn)


FFN_STEPS = N_CHIPS


def _resident(a):
    return pl.BlockSpec(a.shape, lambda *_: (0,) * a.ndim, pipeline_mode=pl.Buffered(1))


def _ffn_fwd(u1, w1, w2, h1, tgt, g_post_ffn):
    tp = h1.shape[0]
    tr = _row_tile(tp)
    qb, nt = tr // BLOCK, tp // tr
    sr = tr // FFN_STEPS

    def body(*refs):
        u_ref, w1_ref, w2_ref, h1_ref = refs[:4]
        t_pieces = refs[4:4 + qb]
        g_ref, r1_ref, dy_ref, df2_ref, loss_ref, dg_ref, acc = refs[4 + qb:]
        i, c = pl.program_id(0), pl.program_id(1)
        cur = i % 2

        @pl.when((i == 0) & (c == 0))
        def _():
            loss_ref[...] = jnp.zeros_like(loss_ref)
            dg_ref[...] = jnp.zeros_like(dg_ref)
            acc[1] = jnp.zeros((tr, D_MODEL), F32)

        def matmuls():
            r = jnp.maximum(_dot(u_ref[...], w1_ref[c]), 0.0)
            r1_ref[...] = r.astype(BF16)
            return _dot((r * r).astype(BF16), w2_ref[c])

        def finish_previous_tile(k, valid):
            lo, hi = k * sr, (k + 1) * sr
            g = g_ref[...]
            fhat, rs = _rms(acc[1 - cur, lo:hi])
            h2 = h1_ref[...] + fhat * g
            rows = (i - 1) * tr + lo + lax.broadcasted_iota(jnp.int32, h2.shape, 0)
            tgt = jnp.concatenate([p[max(lo - s * BLOCK, 0):min(hi - s * BLOCK, BLOCK)] for s, p in enumerate(t_pieces)
                                   if lo < (s + 1) * BLOCK and hi > s * BLOCK], axis=0)
            err = jnp.where((rows >= BLOCK) & valid, h2 - tgt, 0.0)
            dy = err * (1.0 / D_MODEL)
            dy_ref[...] = dy
            loss_ref[...] += (0.5 / D_MODEL) * jnp.sum(err * err)
            df2, dg = _rms_bwd(fhat, rs, g, dy)
            df2_ref[...] = df2.astype(BF16)
            dg_ref[...] += dg

        for k in range(FFN_STEPS):
            @pl.when((c == k) & (i < nt))
            def _(k=k):
                finish_previous_tile(k, i >= 1)
                if k == 0:
                    acc[cur] = matmuls()
                else:
                    acc[cur] += matmuls()

            @pl.when((c == k) & (i == nt))
            def _(k=k):
                finish_previous_tile(k, True)

    last = nt - 1
    this_row = pl.BlockSpec((tr, D_MODEL), lambda i, c: (jnp.minimum(i, last), 0))
    prev_quarter = pl.BlockSpec((sr, D_MODEL), lambda i, c: (jnp.maximum(i - 1, 0) * FFN_STEPS + c, 0))
    prev_quarter_out = pl.BlockSpec(
        (sr, D_MODEL), lambda i, c: (jnp.where(i == 0, nt * FFN_STEPS, (i - 1) * FFN_STEPS + c), 0))
    full = lambda a: pl.BlockSpec(a.shape, lambda i, c: (0,) * a.ndim)
    return pl.pallas_call(
        body, name="ffn_fwd", grid=(nt + 1, FFN_STEPS),
        in_specs=[this_row, _resident(w1), _resident(w2), prev_quarter] + _seq_specs(tr, delay=1) + [full(g_post_ffn)],
        out_specs=[pl.BlockSpec((tr, FF_CHUNK), lambda i, c: (jnp.minimum(i, last), jnp.where(i < nt, c, FFN_STEPS - 1))),
                   prev_quarter_out, prev_quarter_out,
                   pl.BlockSpec((1, 1), lambda i, c: (0, 0)), pl.BlockSpec((1, D_MODEL), lambda i, c: (0, 0))],
        out_shape=[jax.ShapeDtypeStruct((tp, D_FF), BF16), jax.ShapeDtypeStruct((tp + sr, D_MODEL), F32),
                   jax.ShapeDtypeStruct((tp + sr, D_MODEL), BF16), jax.ShapeDtypeStruct((1, 1), F32),
                   jax.ShapeDtypeStruct((1, D_MODEL), F32)],
        scratch_shapes=[pltpu.VMEM((2, tr, D_MODEL), F32)],
        compiler_params=_params("arbitrary", "arbitrary"),
    )(u1, w1, w2, h1, *([tgt] * qb), g_post_ffn)


def _ffn_bwd_data(df2, r1, w1, w2, dy, h1, mix, g_pre_ffn, g_post_mix):
    tp = h1.shape[0]
    tr = _row_tile(tp)
    nt = tp // tr
    sr = tr // FFN_STEPS

    def body(df2_ref, r1_ref, w1_ref, w2_ref, dy_ref, h1_ref, mix_ref, gf_ref, gm_ref,
             da_ref, dh1_ref, dmix_ref, dgf_ref, dgm_ref, acc):
        i, c = pl.program_id(0), pl.program_id(1)
        cur = i % 2

        @pl.when((i == 0) & (c == 0))
        def _():
            dgf_ref[...] = jnp.zeros_like(dgf_ref)
            dgm_ref[...] = jnp.zeros_like(dgm_ref)
            acc[1] = jnp.zeros((tr, D_MODEL), F32)

        def matmuls():
            df = _dot_nt(df2_ref[...], w2_ref[c])
            da = (df * (2.0 * r1_ref[...].astype(F32))).astype(BF16)
            da_ref[...] = da
            return _dot_nt(da, w1_ref[c])

        def finish_previous_tile(k, valid):
            lo, hi = k * sr, (k + 1) * sr
            hhat, rs = _rms(h1_ref[...])
            dx, dgf = _rms_bwd(hhat, rs, gf_ref[...], acc[1 - cur, lo:hi])
            dh1 = dy_ref[...] + dx
            dh1_ref[...] = dh1
            mhat, rsm = _rms(mix_ref[...])
            dmix, dgm = _rms_bwd(mhat, rsm, gm_ref[...], dh1)
            dmix_ref[...] = dmix.astype(BF16)
            dgf_ref[...] += jnp.where(valid, dgf, 0.0)
            dgm_ref[...] += jnp.where(valid, dgm, 0.0)

        for k in range(FFN_STEPS):
            @pl.when((c == k) & (i < nt))
            def _(k=k):
                finish_previous_tile(k, i >= 1)
                if k == 0:
                    acc[cur] = matmuls()
                else:
                    acc[cur] += matmuls()

            @pl.when((c == k) & (i == nt))
            def _(k=k):
                finish_previous_tile(k, True)

    last = nt - 1
    this_row = pl.BlockSpec((tr, D_MODEL), lambda i, c: (jnp.minimum(i, last), 0))
    prev_quarter = pl.BlockSpec((sr, D_MODEL), lambda i, c: (jnp.maximum(i - 1, 0) * FFN_STEPS + c, 0))
    prev_quarter_out = pl.BlockSpec(
        (sr, D_MODEL), lambda i, c: (jnp.where(i == 0, nt * FFN_STEPS, (i - 1) * FFN_STEPS + c), 0))
    chunk = pl.BlockSpec((tr, FF_CHUNK), lambda i, c: (jnp.minimum(i, last), jnp.where(i < nt, c, FFN_STEPS - 1)))
    gain = pl.BlockSpec((1, D_MODEL), lambda i, c: (0, 0))
    return pl.pallas_call(
        body, name="ffn_bwd_data", grid=(nt + 1, FFN_STEPS),
        in_specs=[this_row, chunk, _resident(w1), _resident(w2), prev_quarter, prev_quarter, prev_quarter, gain, gain],
        out_specs=[chunk, prev_quarter_out, prev_quarter_out, gain, gain],
        out_shape=[jax.ShapeDtypeStruct((tp, D_FF), BF16), jax.ShapeDtypeStruct((tp + sr, D_MODEL), F32),
                   jax.ShapeDtypeStruct((tp + sr, D_MODEL), BF16), jax.ShapeDtypeStruct((1, D_MODEL), F32),
                   jax.ShapeDtypeStruct((1, D_MODEL), F32)],
        scratch_shapes=[pltpu.VMEM((2, tr, D_MODEL), F32)],
        compiler_params=_params("arbitrary", "arbitrary"),
    )(df2, r1, w1, w2, dy, h1, mix, g_pre_ffn, g_post_mix)


def _ffn_bwd_weights(u1, da1, r1, df2):
    tp = u1.shape[0]
    tr = _wgrad_row_tile(tp)

    def body(u_ref, da_ref, r1_ref, df2_ref, dw1_ref, dw2_ref):
        i = pl.program_id(1)

        def products():
            r = r1_ref[...].astype(F32)
            return _dot_tn(u_ref[...], da_ref[...]), _dot_tn((r * r).astype(BF16), df2_ref[...])

        @pl.when(i == 0)
        def _():
            dw1_ref[0], dw2_ref[0] = products()

        @pl.when(i > 0)
        def _():
            p1, p2 = products()
            dw1_ref[0] += p1
            dw2_ref[0] += p2

    row = pl.BlockSpec((tr, D_MODEL), lambda c, i: (i, 0))
    chunk = pl.BlockSpec((tr, FF_CHUNK), lambda c, i: (i, c))
    return pl.pallas_call(
        body, name="ffn_bwd_weights", grid=(N_CHIPS, tp // tr),
        in_specs=[row, chunk, chunk, row],
        out_specs=[pl.BlockSpec((1, D_MODEL, FF_CHUNK), lambda c, i: (c, 0, 0)),
                   pl.BlockSpec((1, FF_CHUNK, D_MODEL), lambda c, i: (c, 0, 0))],
        out_shape=[jax.ShapeDtypeStruct((N_CHIPS, D_MODEL, FF_CHUNK), F32),
                   jax.ShapeDtypeStruct((N_CHIPS, FF_CHUNK, D_MODEL), F32)],
        compiler_params=_params("parallel", "arbitrary"),
    )(u1, da1, r1, df2)


N_VEC_ROWS = 8


def _outproj_lru_bwd(dmix, w_out, attn, rec, xr, yr, hr, conv_w, conv_b, wa, ba, wx, bx, lam, token):
    tp = xr.shape[0]
    tr = _row_tile(tp)
    qb, nt = tr // BLOCK, tp // tr

    def body(dm_ref, w_ref, at_ref, rc_ref, xr_ref, xh_ref, yr_ref, hr_ref, hp_ref,
             cw_ref, cb_ref, wa_ref, ba_ref, wx_ref, bx_ref, lam_ref, _,
             dxr_ref, dyr_ref, dat_ref, dwo_ref, dwa_ref, dwx_ref, vec_ref, g_next, a_next, dxc_next, dsp):
        s = pl.program_id(0)
        t = nt - 1 - s

        @pl.when(s == 0)
        def _():
            g_next[...] = jnp.zeros_like(g_next)
            a_next[...] = jnp.zeros_like(a_next)
            dxc_next[...] = jnp.zeros_like(dxc_next)
            dsp[...] = jnp.zeros_like(dsp)
            dwo_ref[...] = jnp.zeros_like(dwo_ref)
            dwa_ref[...] = jnp.zeros_like(dwa_ref)
            dwx_ref[...] = jnp.zeros_like(dwx_ref)
            vec_ref[...] = jnp.zeros_like(vec_ref)

        dm = dm_ref[...]
        dcat = _dot_nt(dm, w_ref[...])
        dat_ref[...] = dcat[:, :ATTN_WIDTH].astype(BF16)
        drec_tile = dcat[:, ATTN_WIDTH:]
        dwo_ref[:ATTN_WIDTH] += _dot_tn(at_ref[...], dm)
        dwo_ref[ATTN_WIDTH:] += _dot_tn(rc_ref[...], dm)

        first_tile = t == 0
        cw, cb = cw_ref[...], cb_ref[...]
        lam_v = lam_ref[...]
        sp = _softplus(-lam_v)
        wa_m, ba_v, wx_m, bx_v = wa_ref[...], ba_ref[...], wx_ref[...], bx_ref[...]
        rows = lax.broadcasted_iota(jnp.int32, (BLOCK, LRU_WIDTH), 0)
        col = lambda v: jnp.sum(v, axis=0, keepdims=True)

        g_after, a_after, dxc_after = g_next[0:1], a_next[0:1], dxc_next[...]
        xbs, dgrs, dgis = [], [], []
        vec = [jnp.zeros((1, LRU_WIDTH), F32) for _ in range(N_VEC_ROWS)]
        for i in reversed(range(qb)):
            blk = slice(i * BLOCK, (i + 1) * BLOCK)
            if i == 0:
                x_before = jnp.where(first_tile, 0.0, xh_ref[...])
                h_before = jnp.where(first_tile, 0.0, hp_ref[7:8])
            else:
                x_before = xr_ref[i * BLOCK - 8:i * BLOCK]
                h_before = hr_ref[i * BLOCK - 1:i * BLOCK]
            taps = _conv_taps(xr_ref[blk], x_before)
            xc = cb + sum(cw[k:k + 1] * taps[k] for k in range(4))
            xb, r, ig, a, mult = _lru_gates(xc, wa_m, ba_v, wx_m, bx_v, sp)

            yr_v = yr_ref[blk]
            gl, th = _gelu(yr_v)
            h = hr_ref[blk]
            drec = drec_tile[blk]
            dyr_ref[blk] = (drec * h * _gelu_grad(yr_v, th)).astype(BF16)

            a_up = jnp.where(rows == BLOCK - 1, a_after, pltpu.roll(a, BLOCK - 1, 0))
            g = _scan_rev(a_up, drec * gl, g_after)
            g_after, a_after = g[0:1], a[0:1]

            h_prev = jnp.where(rows == 0, h_before, pltpu.roll(h, 1, 0))
            du, da = g, g * h_prev
            if i == 0:
                real = (t * tr + rows) >= PAD_ROWS
                du, da = jnp.where(real, du, 0.0), jnp.where(real, da, 0.0)
            dmult = du * (ig * xc)
            dig = du * (mult * xc)
            dxc = du * (mult * ig)
            dlog_a = da * a - dmult * (a * a / mult)
            if i == 0:
                dlog_a = jnp.where(real, dlog_a, 0.0)
            dgr = (dlog_a * (-LRU_C * sp)) * (r * (1.0 - r))
            dgi = dig * (ig * (1.0 - ig))
            dgr_b, dgi_b = dgr.astype(BF16), dgi.astype(BF16)
            dxc = dxc + _dot_nt(dgr_b, wa_m) + _dot_nt(dgi_b, wx_m)
            xbs.append(xb)
            dgrs.append(dgr_b)
            dgis.append(dgi_b)

            ext = jnp.concatenate([dxc, dxc_after], axis=0)
            up = [ext[:BLOCK] if j == 0 else pltpu.roll(ext, BLOCK + 8 - j, 0)[:BLOCK] for j in range(4)]
            dxr_ref[blk] = sum(cw[k:k + 1] * up[3 - k] for k in range(4)).astype(BF16)
            dxc_after = dxc[:8]

            for k in range(4):
                vec[k] = vec[k] + col(dxc * taps[k])
            vec[4] = vec[4] + col(dxc)
            vec[5] = vec[5] + col(dgr)
            vec[6] = vec[6] + col(dgi)
            vec[7] = vec[7] + col(dlog_a * (-LRU_C * r))

        g_next[0:1], a_next[0:1], dxc_next[...] = g_after, a_after, dxc_after
        xb_all = jnp.concatenate(xbs, axis=0)
        dwa_ref[...] += _dot_tn(xb_all, jnp.concatenate(dgrs, axis=0))
        dwx_ref[...] += _dot_tn(xb_all, jnp.concatenate(dgis, axis=0))
        for k in range(7):
            vec_ref[k:k + 1] += vec[k]
        dsp[0:1] += vec[7]

        @pl.when(s == nt - 1)
        def _():
            vec_ref[7:8] = dsp[0:1] * (-_sigmoid(-lam_v))

    blk_spec = pl.BlockSpec((tr, LRU_WIDTH), lambda s: (nt - 1 - s, 0))
    rows_before = pl.BlockSpec((8, LRU_WIDTH), lambda s: (jnp.maximum((nt - 1 - s) * (tr // 8) - 1, 0), 0))
    full = lambda a: pl.BlockSpec(a.shape, lambda s: (0,) * a.ndim)
    small = [conv_w, conv_b, wa, ba, wx, bx, lam, token]
    sq = pl.BlockSpec((LRU_WIDTH, LRU_WIDTH), lambda s: (0, 0))
    wide = pl.BlockSpec((tr, D_MODEL), lambda s: (nt - 1 - s, 0))
    whole = pl.BlockSpec((D_MODEL, D_MODEL), lambda s: (0, 0))
    return pl.pallas_call(
        body, name="outproj_lru_bwd", grid=(nt,),
        in_specs=[wide, whole, blk_spec, blk_spec, blk_spec, rows_before, blk_spec, blk_spec, rows_before]
        + [full(a) for a in small],
        out_specs=[blk_spec, blk_spec, blk_spec, whole, sq, sq, pl.BlockSpec((N_VEC_ROWS, LRU_WIDTH), lambda s: (0, 0))],
        out_shape=[jax.ShapeDtypeStruct((tp, LRU_WIDTH), BF16), jax.ShapeDtypeStruct((tp, LRU_WIDTH), BF16),
                   jax.ShapeDtypeStruct((tp, ATTN_WIDTH), BF16), jax.ShapeDtypeStruct((D_MODEL, D_MODEL), F32),
                   jax.ShapeDtypeStruct((LRU_WIDTH, LRU_WIDTH), F32), jax.ShapeDtypeStruct((LRU_WIDTH, LRU_WIDTH), F32),
                   jax.ShapeDtypeStruct((N_VEC_ROWS, LRU_WIDTH), F32)],
        scratch_shapes=[pltpu.VMEM((8, LRU_WIDTH), F32)] * 4,
        compiler_params=_params("arbitrary"),
    )(dmix, w_out, attn, rec, xr, xr, yr, hr, hr, *small)


def _attn_bwd_tile(tp):
    return _wgrad_row_tile(tp)


def _attn_bwd(qkv, dattn, probs, sink_probs, token):
    tp = qkv.shape[0]
    tr = _attn_bwd_tile(tp)
    qb, nt = tr // BLOCK, tp // tr
    n_groups = N_KV

    def body(p_ref, ps_ref, q_ref, kp_ref, kc_ref, vp_ref, vc_ref, do_ref, _, dq_ref, dkv_ref, ex_ref, ds_ref, dsink):
        t = pl.program_id(0)

        @pl.when(t == 0)
        def _():
            dsink[...] = jnp.zeros_like(dsink)

        k_all = jnp.concatenate([kp_ref[...], kc_ref[...]], axis=0)
        v_all = jnp.concatenate([vp_ref[...], vc_ref[...]], axis=0)
        tail = None
        for i in range(qb):
            rows = slice(i * BLOCK, (i + 1) * BLOCK)
            qt = (q_ref[rows].astype(F32) * _QSCALE).T
            dot = do_ref[rows].astype(F32).T
            k2, v2 = k_all[i * BLOCK:(i + 2) * BLOCK], v_all[i * BLOCK:(i + 2) * BLOCK]
            dqs, dks, dvs = [], [], []
            for g in range(n_groups):
                cols = slice(g * HEAD_DIM, (g + 1) * HEAD_DIM)
                k_g, v_g = k2[:, cols], v2[:, cols]
                qgt, dogt = _heads_t(qt, g), _heads_t(dot, g)
                pb = p_ref[i, g]
                p = pb.astype(F32)
                dpt = _dot(v_g, dogt)
                delta = jnp.sum(p * dpt, axis=0, keepdims=True)
                dst = (p * (dpt - delta)).astype(BF16)
                dqs.append(_dot_tn(k_g, dst) * _QSCALE)
                dks.append(_dot_nt(qgt, dst))
                dvs.append(_dot_nt(dogt, pb))
                dsink[g:g + 1] -= ps_ref[i, g:g + 1] * delta
            dq_ref[rows] = _from_heads_t(dqs).astype(BF16)
            dkv = jnp.concatenate([jnp.concatenate(dks, axis=0).T, jnp.concatenate(dvs, axis=0).T], axis=1)
            if i == 0:
                ex_ref[0] = dkv[:BLOCK]
            else:
                dkv_ref[(i - 1) * BLOCK:i * BLOCK] = (tail + dkv[:BLOCK]).astype(BF16)
            tail = dkv[BLOCK:]
        dkv_ref[(qb - 1) * BLOCK:] = tail.astype(BF16)

        @pl.when(t == nt - 1)
        def _():
            lane = lax.broadcasted_iota(jnp.int32, (1, ATTN_HEADS), 1)
            acc = jnp.zeros((1, ATTN_HEADS), F32)
            for h in range(ATTN_HEADS):
                g, hh = divmod(h, GQA_GROUP)
                acc = acc + jnp.where(lane == h, jnp.sum(dsink[g:g + 1, hh * BLOCK:(hh + 1) * BLOCK]), 0.0)
            ds_ref[...] = acc

    cur = lambda w: pl.BlockSpec((tr, w), lambda t: (t, 0))
    return pl.pallas_call(
        body, name="attn_bwd", grid=(nt,),
        in_specs=_prob_specs(qb) + [cur(ATTN_WIDTH)] + _kv_specs(tr)
        + [cur(ATTN_WIDTH), pl.BlockSpec(token.shape, lambda t: (0, 0))],
        out_specs=[cur(ATTN_WIDTH), cur(2 * KV_WIDTH), pl.BlockSpec((1, BLOCK, 2 * KV_WIDTH), lambda t: (t, 0, 0)),
                   pl.BlockSpec((1, ATTN_HEADS), lambda t: (0, 0))],
        out_shape=[jax.ShapeDtypeStruct((tp, ATTN_WIDTH), BF16), jax.ShapeDtypeStruct((tp, 2 * KV_WIDTH), BF16),
                   jax.ShapeDtypeStruct((nt, BLOCK, 2 * KV_WIDTH), F32), jax.ShapeDtypeStruct((1, ATTN_HEADS), F32)],
        scratch_shapes=[pltpu.VMEM((n_groups, GROUP_ROWS), F32)],
        compiler_params=_params("arbitrary"),
    )(probs, sink_probs, qkv, qkv, qkv, qkv, qkv, dattn, token)


def _fix_dkv(dkv, dkv_extra):
    tp = dkv.shape[0]
    tr = _attn_bwd_tile(tp)
    nt, qb = tp // tr, tr // BLOCK
    if nt == 1:
        return dkv

    def body(d_ref, ex_ref, o_ref):
        o_ref[...] = (d_ref[...].astype(F32) + ex_ref[0]).astype(BF16)

    last = pl.BlockSpec((BLOCK, 2 * KV_WIDTH), lambda t: (t * qb + qb - 1, 0))
    return pl.pallas_call(
        body, name="fix_dkv", grid=(nt - 1,),
        in_specs=[last, pl.BlockSpec((1, BLOCK, 2 * KV_WIDTH), lambda t: (t + 1, 0, 0))],
        out_specs=last, out_shape=jax.ShapeDtypeStruct(dkv.shape, dkv.dtype),
        input_output_aliases={0: 0}, compiler_params=_params("parallel"),
    )(dkv, dkv_extra)


def _inproj_wgrad(dq, dkv, dxr, dyr, u0):
    tp = dq.shape[0]
    tr = _wgrad_row_tile(tp)

    def body(dq_ref, dkv_ref, dxr_ref, dyr_ref, u_ref, dw_ref):
        i = pl.program_id(0)

        def product():
            dz = jnp.concatenate([dq_ref[...], dkv_ref[...], dxr_ref[...], dyr_ref[...]], axis=1)
            return _dot_tn(dz, u_ref[...])

        @pl.when(i == 0)
        def _():
            dw_ref[...] = product()

        @pl.when(i > 0)
        def _():
            dw_ref[...] += product()

    row = lambda w: pl.BlockSpec((tr, w), lambda i: (i, 0))
    return pl.pallas_call(
        body, name="inproj_wgrad", grid=(tp // tr,),
        in_specs=[row(ATTN_WIDTH), row(2 * KV_WIDTH), row(LRU_WIDTH), row(LRU_WIDTH), row(D_MODEL)],
        out_specs=pl.BlockSpec((IN_WIDTH, D_MODEL), lambda i: (0, 0)),
        out_shape=jax.ShapeDtypeStruct((IN_WIDTH, D_MODEL), F32),
        compiler_params=_params("arbitrary"),
    )(dq, dkv, dxr, dyr, u0)


def _inproj_dgrad(dq, dkv, dxr, dyr, w_in, head, x, dh1, g, token):
    tp = dq.shape[0]
    tr = _row_tile(tp)
    nt, qb = tp // tr, tr // BLOCK

    def body(*refs):
        dq_ref, dkv_ref, dxr_ref, dyr_ref, w_ref, head_ref = refs[:6]
        pieces = refs[6:6 + qb]
        dh1_ref, g_ref, _, gx_ref, dhead_ref, dg_ref, buf, sems = refs[6 + qb:]
        i = pl.program_id(0)
        slot = i % 2

        def out_copy(step, at):
            return pltpu.make_async_copy(buf.at[at], gx_ref.at[pl.ds(step * tr - BLOCK, tr)], sems.at[at])

        dz = jnp.concatenate([dq_ref[...], dkv_ref[...], dxr_ref[...], dyr_ref[...]], axis=1)
        du = _dot(dz, w_ref[...])
        hhat, rs = _rms(_seq_tile(head_ref[...], pieces, i))
        dx, dg = _rms_bwd(hhat, rs, g_ref[...], du)
        dh0 = dh1_ref[...] + dx

        @pl.when(i >= 3)
        def _():
            out_copy(i - 2, slot).wait()

        buf[slot] = dh0

        @pl.when(i == 0)
        def _():
            dg_ref[...] = dg
            dhead_ref[...] = dh0[:BLOCK]
            if tr > BLOCK:
                first = pltpu.make_async_copy(buf.at[0, pl.ds(BLOCK, tr - BLOCK)], gx_ref.at[pl.ds(0, tr - BLOCK)],
                                              sems.at[0])
                first.start()
                first.wait()

        @pl.when(i >= 1)
        def _():
            dg_ref[...] += dg
            out_copy(i, slot).start()

        @pl.when(i == nt - 1)
        def _():
            if nt >= 3:
                out_copy(nt - 2, (nt - 2) % 2).wait()
            if nt >= 2:
                out_copy(nt - 1, (nt - 1) % 2).wait()

    row = lambda w: pl.BlockSpec((tr, w), lambda i: (i, 0))
    full = lambda shape: pl.BlockSpec(shape, lambda i: (0,) * len(shape))
    return pl.pallas_call(
        body, name="inproj_dgrad", grid=(tp // tr,),
        in_specs=[row(ATTN_WIDTH), row(2 * KV_WIDTH), row(LRU_WIDTH), row(LRU_WIDTH), full(w_in.shape),
                  full(head.shape)] + _seq_specs(tr) + [row(D_MODEL), full(g.shape), full(token.shape)],
        out_specs=[pl.BlockSpec(memory_space=pl.ANY), full((BLOCK, D_MODEL)), full((1, D_MODEL))],
        out_shape=[jax.ShapeDtypeStruct(x.shape, F32), jax.ShapeDtypeStruct((BLOCK, D_MODEL), F32),
                   jax.ShapeDtypeStruct((1, D_MODEL), F32)],
        scratch_shapes=[pltpu.VMEM((2, tr, D_MODEL), F32), pltpu.SemaphoreType.DMA((2,))],
        compiler_params=_params("arbitrary"),
    )(dq, dkv, dxr, dyr, w_in, head, *([x] * qb), dh1, g, token)


def _dense_block_diag(w):
    eye = jnp.eye(LRU_BLOCKS, dtype=w.dtype)
    return (w[:, :, None, :] * eye[:, None, :, None]).reshape(LRU_WIDTH, LRU_WIDTH)


def _diag_blocks(dense):
    d4 = dense.reshape(LRU_BLOCKS, LRU_BLOCK, LRU_BLOCKS, LRU_BLOCK)
    return jnp.stack([d4[n, :, n, :] for n in range(LRU_BLOCKS)])


def _local_step(head, x, tgt, g_pre_mix, w_in, conv_w, conv_b, w_a, b_a, w_x, b_x, lam, sinks, g_post_mix,
                g_pre_ffn, g_post_ffn, late_weights, on_ffn_grads, on_outproj_bwd, on_mixer_grads, token):
    wa = _dense_block_diag(w_a).astype(BF16)
    wx = _dense_block_diag(w_x).astype(BF16)

    u0, qkv, xr, yr, hr, rec = _inproj_lru_fwd(head, x, g_pre_mix, w_in, conv_w, conv_b, wa, b_a, wx, b_x, lam, token)
    attn, probs, sink_probs = _attn_fwd(qkv, sinks)
    w_out, ffn_weights = late_weights([attn, rec])
    mix, h1, u1 = _outproj_fwd(attn, rec, w_out, head, x, g_post_mix, g_pre_ffn)
    w1, w2 = ffn_weights([u1])
    r1, dy, df2, loss, dg_post_ffn = _ffn_fwd(u1, w1, w2, h1, tgt, g_post_ffn)

    da1, dh1, dmix, dg_pre_ffn, dg_post_mix = _ffn_bwd_data(df2, r1, w1, w2, dy, h1, mix, g_pre_ffn, g_post_mix)
    dw1, dw2 = _ffn_bwd_weights(u1, da1, r1, df2)
    token2 = on_ffn_grads(dw1, dw2)
    dxr, dyr, dattn, dw_out, dwa, dwx, vec = _outproj_lru_bwd(dmix, w_out, attn, rec, xr, yr, hr, conv_w, conv_b,
                                                              wa, b_a, wx, b_x, lam, token2)
    token3 = on_outproj_bwd(dattn)
    dq, dkv, dkv_extra, dsinks = _attn_bwd(qkv, dattn, probs, sink_probs, token3)
    dkv = _fix_dkv(dkv, dkv_extra)
    dw_in = _inproj_wgrad(dq, dkv, dxr, dyr, u0)
    token4 = on_mixer_grads(dw_in, dw_out)
    dx, dhead, dg_pre_mix = _inproj_dgrad(dq, dkv, dxr, dyr, w_in, head, x, dh1, g_pre_mix, token4)

    grads = dict(
        g_pre_mix=dg_pre_mix, conv_w=vec[0:4], conv_b=vec[4:5], w_a=_diag_blocks(dwa), b_a=vec[5:6],
        w_x=_diag_blocks(dwx), b_x=vec[6:7], lru_lambda=vec[7:8], attn_sinks=dsinks,
        g_post_mix=dg_post_mix, g_pre_ffn=dg_pre_ffn, g_post_ffn=dg_post_ffn)
    return loss, dx, dhead, grads


HBM = pl.BlockSpec(memory_space=pltpu.HBM)


def _mesh_pos():
    return lax.axis_index("x"), lax.axis_index("y"), lax.axis_index("c")


def _other_chips(x, y):
    return [(1 - x, y), (x, 1 - y), (1 - x, 1 - y)]


def _remote(src, dst, send_sem, recv_sem, to):
    return pltpu.make_async_remote_copy(src_ref=src, dst_ref=dst, send_sem=send_sem, recv_sem=recv_sem,
                                        device_id=to, device_id_type=MESH)


def _gather_weights(shards, lands, tiny, tiny_land):
    nbig = len(shards)

    def body(*refs):
        srcs, tiny_src = refs[:nbig], refs[nbig]
        outs, tiny_out = refs[2 * nbig + 2:3 * nbig + 2], refs[3 * nbig + 2]
        ici_send, ici_recv, d2d_send, d2d_recv, tiny_send, tiny_recv = refs[3 * nbig + 3:]
        x, y, c = _mesh_pos()
        me = 2 * x + y
        chips = _other_chips(x, y)
        sibling = (x, y, 1 - c)
        sends = []
        for w, (src, out) in enumerate(zip(srcs, outs)):
            hr = src.shape[0] // 2
            for j, chip in enumerate(chips):
                k = 3 * w + j
                cp = _remote(src.at[pl.ds(c * hr, hr)], out.at[me, pl.ds(c * hr, hr)],
                             ici_send.at[k], ici_recv.at[k], (*chip, c))
                cp.start()
                sends.append(cp)
        for j, chip in enumerate(chips):
            cp = _remote(tiny_src, tiny_out.at[me], tiny_send.at[j], tiny_recv.at[j], (*chip, c))
            cp.start()
            sends.append(cp)
        for w, (src, out) in enumerate(zip(srcs, outs)):
            hr = src.shape[0] // 2
            for j, (px, py) in enumerate(chips):
                k = 3 * w + j
                landed = out.at[2 * px + py, pl.ds(c * hr, hr)]
                _remote(landed, landed, ici_send.at[k], ici_recv.at[k], sibling).wait_recv()
                cp = _remote(landed, landed, d2d_send.at[k], d2d_recv.at[k], sibling)
                cp.start()
                sends.append(cp)
        for w, (src, out) in enumerate(zip(srcs, outs)):
            hr = src.shape[0] // 2
            for j, (px, py) in enumerate(chips):
                k = 3 * w + j
                other = out.at[2 * px + py, pl.ds((1 - c) * hr, hr)]
                _remote(other, other, d2d_send.at[k], d2d_recv.at[k], sibling).wait_recv()
        for j, (px, py) in enumerate(chips):
            blk = tiny_out.at[2 * px + py]
            _remote(blk, blk, tiny_send.at[j], tiny_recv.at[j], sibling).wait_recv()
        for cp in sends:
            cp.wait_send()

    out_shape = [jax.ShapeDtypeStruct(l.shape, l.dtype) for l in list(lands) + [tiny_land]]
    n = 3 * nbig
    return pl.pallas_call(
        body, name="gather_weights", out_shape=out_shape,
        in_specs=[HBM] * (2 * nbig + 2), out_specs=[HBM] * (nbig + 1),
        input_output_aliases={nbig + 1 + i: i for i in range(nbig + 1)},
        scratch_shapes=[pltpu.SemaphoreType.DMA((n,)),
                        pltpu.SemaphoreType.DMA((n,)), pltpu.SemaphoreType.DMA((n,)), pltpu.SemaphoreType.DMA((n,)),
                        pltpu.SemaphoreType.DMA((3,)), pltpu.SemaphoreType.DMA((3,))],
    )(*shards, tiny, *lands, tiny_land)


def _prep_shard(w, me):
    rows, cols = w.shape
    tr = _elementwise_tile(rows)

    def body(me_ref, w_ref, s_ref, l_ref):
        b = w_ref[...].astype(BF16)
        s_ref[...] = b
        l_ref[0] = b

    return pl.pallas_call(
        body, name="prep_shard",
        grid_spec=pltpu.PrefetchScalarGridSpec(
            num_scalar_prefetch=1, grid=(rows // tr,),
            in_specs=[pl.BlockSpec((tr, cols), lambda i, me_ref: (i, 0))],
            out_specs=[pl.BlockSpec((tr, cols), lambda i, me_ref: (i, 0)),
                       pl.BlockSpec((1, tr, cols), lambda i, me_ref: (me_ref[0], i, 0))]),
        out_shape=[jax.ShapeDtypeStruct((rows, cols), BF16), jax.ShapeDtypeStruct((N_CHIPS, rows, cols), BF16)],
        compiler_params=_params("parallel"),
    )(me, w)


def _prep_tiny(tiny, me, slots=N_CHIPS):
    def body(me_ref, t_ref, l_ref):
        l_ref[0] = t_ref[...]

    return pl.pallas_call(
        body, name="prep_tiny",
        grid_spec=pltpu.PrefetchScalarGridSpec(
            num_scalar_prefetch=1, grid=(1,),
            in_specs=[pl.BlockSpec(tiny.shape, lambda i, me_ref: (0, 0))],
            out_specs=pl.BlockSpec((1,) + tiny.shape, lambda i, me_ref: (me_ref[0], 0, 0))),
        out_shape=jax.ShapeDtypeStruct((slots,) + tiny.shape, tiny.dtype),
    )(me, tiny)


N_DEV = 8


def _sibling_exchange(parts, token):
    def body(*refs):
        n = len(parts)
        srcs, outs, send_sems, recv_sems = refs[:n], refs[n + 1:2 * n + 1], refs[2 * n + 1], refs[2 * n + 2]
        x, y, c = _mesh_pos()
        sibling = (x, y, 1 - c)
        cps = []
        for w, (src, out) in enumerate(zip(srcs, outs)):
            hr = src.shape[1] // 2
            cp = _remote(src.at[:, pl.ds((1 - c) * hr, hr)], out, send_sems.at[w], recv_sems.at[w], sibling)
            cp.start()
            cps.append(cp)
        for cp in cps:
            cp.wait()

    n = len(parts)
    return pl.pallas_call(
        body, name="sibling_exchange",
        out_shape=[jax.ShapeDtypeStruct((p.shape[0], p.shape[1] // 2, p.shape[2]), p.dtype) for p in parts],
        in_specs=[HBM] * n + [pl.BlockSpec(memory_space=pl.ANY)], out_specs=[HBM] * n,
        scratch_shapes=[pltpu.SemaphoreType.DMA((n,)), pltpu.SemaphoreType.DMA((n,))],
    )(*parts, token)


def _chip_presum(part, from_sibling, pos):
    _, hr, cols = from_sibling.shape
    tr = _elementwise_tile(hr)
    steps = hr // tr

    def body(pos_ref, a_ref, b_ref, o_ref, land_ref):
        s = (a_ref[...] + b_ref[...]).astype(BF16)
        o_ref[...] = s

        @pl.when(pl.program_id(1) == pos_ref[1])
        def _():
            land_ref[...] = s

    return pl.pallas_call(
        body, name="chip_presum",
        grid_spec=pltpu.PrefetchScalarGridSpec(
            num_scalar_prefetch=1, grid=(steps, N_CHIPS),
            in_specs=[pl.BlockSpec((1, tr, cols), lambda i, j, p: (j, p[0] * steps + i, 0)),
                      pl.BlockSpec((1, tr, cols), lambda i, j, p: (j, i, 0))],
            out_specs=[pl.BlockSpec((1, tr, cols), lambda i, j, p: (j, i, 0)),
                       pl.BlockSpec((1, tr, cols), lambda i, j, p: (p[1], p[0] * steps + i, 0))]),
        out_shape=[jax.ShapeDtypeStruct(from_sibling.shape, BF16),
                   jax.ShapeDtypeStruct((N_CHIPS, 2 * hr, cols), BF16)],
        compiler_params=_params("arbitrary", "arbitrary"),
    )(pos, part, from_sibling)


def _scatter_partials(cparts, lands, done_cparts=(), done_lands=()):
    n_new = len(cparts)
    nw = n_new + len(done_cparts)

    def body(*refs):
        srcs = refs[:nw]
        outs = refs[2 * nw:3 * nw]
        own_send, own_recv, ici_send, ici_recv, d2d_send, d2d_recv = refs[3 * nw:]
        x, y, c = _mesh_pos()
        me = 2 * x + y
        chips = _other_chips(x, y)
        sibling = (x, y, 1 - c)
        sends = []
        for w in list(range(n_new, nw)) + list(range(n_new)):
            src, out = srcs[w], outs[w]
            hr = src.shape[1]
            mine = out.at[me, pl.ds(c * hr, hr)]
            cp = _remote(src.at[me], mine, own_send.at[w], own_recv.at[w], sibling)
            cp.start()
            sends.append(cp)
            for j, (px, py) in enumerate(chips):
                if w >= n_new:
                    break
                k = 3 * w + j
                cp = _remote(src.at[2 * px + py], mine, ici_send.at[k], ici_recv.at[k], (px, py, c))
                cp.start()
                sends.append(cp)
        for w in list(range(n_new, nw)) + list(range(n_new)):
            src, out = srcs[w], outs[w]
            hr = src.shape[1]
            for j, (px, py) in enumerate(chips):
                k = 3 * w + j
                landed = out.at[2 * px + py, pl.ds(c * hr, hr)]
                if w < n_new:
                    _remote(landed, landed, ici_send.at[k], ici_recv.at[k], sibling).wait_recv()
                cp = _remote(landed, landed, d2d_send.at[k], d2d_recv.at[k], sibling)
                cp.start()
                sends.append(cp)
        for w, (src, out) in enumerate(zip(srcs, outs)):
            hr = src.shape[1]
            other = out.at[me, pl.ds((1 - c) * hr, hr)]
            _remote(other, other, own_send.at[w], own_recv.at[w], sibling).wait_recv()
            for j, (px, py) in enumerate(chips):
                k = 3 * w + j
                other = out.at[2 * px + py, pl.ds((1 - c) * hr, hr)]
                _remote(other, other, d2d_send.at[k], d2d_recv.at[k], sibling).wait_recv()
        for cp in sends:
            cp.wait_send()

    n = 3 * nw
    dma = pltpu.SemaphoreType.DMA
    every = list(cparts) + list(done_cparts)
    every_lands = list(lands) + list(done_lands)
    return pl.pallas_call(
        body, name="scatter_partials",
        out_shape=[jax.ShapeDtypeStruct(l.shape, l.dtype) for l in every_lands],
        in_specs=[HBM] * (2 * nw), out_specs=[HBM] * nw,
        input_output_aliases={nw + i: i for i in range(nw)},
        scratch_shapes=[dma((nw,)), dma((nw,)), dma((n,)), dma((n,)), dma((n,)), dma((n,))],
    )(*every, *every_lands)


SEM = pl.BlockSpec(memory_space=pltpu.SEMAPHORE)
SPLIT_COPY = pltpu.CompilerParams(has_side_effects=pltpu.SideEffectType.DATAFLOW_SIDE_EFFECTING)


def _hbm(a):
    return pltpu.with_memory_space_constraint(a, pltpu.HBM)


def _gather_copies(srcs, lands, send_sems, recv_sems):
    x, y, c = _mesh_pos()
    me = 2 * x + y
    sends, recvs = [], []
    for w, (src, land) in enumerate(zip(srcs, lands)):
        hr = src.shape[0] // 2
        for j, (px, py) in enumerate(_other_chips(x, y)):
            k = 3 * w + j
            sends.append(_remote(src.at[pl.ds(c * hr, hr)], land.at[me, pl.ds(c * hr, hr)],
                                 send_sems.at[k], recv_sems.at[k], (px, py, c)))
            got = land.at[2 * px + py, pl.ds(c * hr, hr)]
            recvs.append(_remote(got, got, send_sems.at[k], recv_sems.at[k], (px, py, c)))
    return sends, recvs


def _scatter_copies(srcs, lands, send_sems, recv_sems):
    x, y, c = _mesh_pos()
    me = 2 * x + y
    sends, recvs = [], []
    for w, (src, land) in enumerate(zip(srcs, lands)):
        hr = src.shape[1]
        for j, (px, py) in enumerate(_other_chips(x, y)):
            k = 3 * w + j
            sends.append(_remote(src.at[2 * px + py], land.at[me, pl.ds(c * hr, hr)],
                                 send_sems.at[k], recv_sems.at[k], (px, py, c)))
            got = land.at[2 * px + py, pl.ds(c * hr, hr)]
            recvs.append(_remote(got, got, send_sems.at[k], recv_sems.at[k], (px, py, c)))
    return sends, recvs


def _sibling_copies(srcs, lands, send_sems, recv_sems):
    x, y, c = _mesh_pos()
    sibling = (x, y, 1 - c)
    sends, recvs = [], []
    for w, (src, land) in enumerate(zip(srcs, lands)):
        hr = src.shape[1] // 2
        sends.append(_remote(src.at[:, pl.ds((1 - c) * hr, hr)], land, send_sems.at[w], recv_sems.at[w], sibling))
        recvs.append(_remote(land, land, send_sems.at[w], recv_sems.at[w], sibling))
    return sends, recvs


def _inchip_copies(srcs, lands, send_sems, recv_sems):
    x, y, c = _mesh_pos()
    me = 2 * x + y
    sibling = (x, y, 1 - c)
    sends, recvs = [], []
    for w, (src, land) in enumerate(zip(srcs, lands)):
        hr = src.shape[1]
        mine, other = pl.ds(c * hr, hr), pl.ds((1 - c) * hr, hr)
        blocks = [(me, src.at[me])] + [(2 * px + py, None) for px, py in _other_chips(x, y)]
        for j, (blk, own_src) in enumerate(blocks):
            k = 4 * w + j
            landed = land.at[blk, mine]
            sends.append(_remote(landed if own_src is None else own_src, landed, send_sems.at[k], recv_sems.at[k], sibling))
            got = land.at[blk, other]
            recvs.append(_remote(got, got, send_sems.at[k], recv_sems.at[k], sibling))
    return sends, recvs


class _SemsFrom:
    def __init__(self, sems, first):
        self.sems, self.first = sems, first

    @property
    def at(self):
        return self

    def __getitem__(self, k):
        return self.sems.at[self.first + k]


def _shifted(copies_of, first):
    def copies(srcs, lands, send_sems, recv_sems):
        return copies_of(srcs, lands, _SemsFrom(send_sems, first), _SemsFrom(recv_sems, first))
    return copies


def _two_groups(copies_a, n_a, k_a, copies_b):
    shifted_b = _shifted(copies_b, k_a)

    def copies(srcs, lands, send_sems, recv_sems):
        sends_a, recvs_a = copies_a(srcs[:n_a], lands[:n_a], send_sems, recv_sems)
        sends_b, recvs_b = shifted_b(srcs[n_a:], lands[n_a:], send_sems, recv_sems)
        return sends_a + sends_b, recvs_a + recvs_b
    return copies


def _all_peers_copies(srcs, lands, send_sems, recv_sems):
    x, y, c = _mesh_pos()
    (src,), (land,) = srcs, lands
    flip = lambda v, bit: 1 - v if bit else v
    sends, recvs = [], []
    for k in range(N_DEV - 1):
        px, py, pc = flip(x, (k + 1) & 4), flip(y, (k + 1) & 2), flip(c, (k + 1) & 1)
        sends.append(_remote(src, land.at[4 * x + 2 * y + c], send_sems.at[k], recv_sems.at[k], (px, py, pc)))
        got = land.at[4 * px + 2 * py + pc]
        recvs.append(_remote(got, got, send_sems.at[k], recv_sems.at[k], (px, py, pc)))
    return sends, recvs


def _split_start(name, copies_of, srcs, land_shapes, n_copies=None):
    n = len(srcs)
    k = 3 * n if n_copies is None else n_copies

    def body(*refs):
        src_refs, land_refs = refs[:n], refs[n:2 * n]
        send_sems, recv_sems = refs[2 * n], refs[2 * n + 1]
        token = refs[-1]
        sends, _ = copies_of(src_refs, land_refs, send_sems, recv_sems)
        for cp in sends:
            cp.start()
        token[...] = jnp.zeros_like(token)

    lands = [_hbm(s) for s in land_shapes]
    dma = pltpu.SemaphoreType.DMA
    res = pl.pallas_call(
        body, name=name,
        out_shape=(dma((k,)), dma((k,)), *[pltpu.HBM(s.shape, s.dtype) for s in srcs],
                   *[pltpu.HBM(s.shape, s.dtype) for s in land_shapes], jax.ShapeDtypeStruct((8, 128), F32)),
        in_specs=[HBM] * (2 * n),
        out_specs=(SEM, SEM, *([HBM] * (2 * n)), pl.BlockSpec(memory_space=pltpu.VMEM)),
        input_output_aliases={i: 2 + i for i in range(2 * n)},
        compiler_params=SPLIT_COPY,
    )(*[_hbm(s) for s in srcs], *lands)
    return res[0], res[1], list(res[2:2 + n]), list(res[2 + n:2 + 2 * n]), res[-1]


def _split_wait(name, copies_of, send_sems, recv_sems, srcs, lands, after):
    n = len(srcs)

    def body(*refs):
        src_refs, land_refs = refs[:n], refs[n:2 * n]
        sends, recvs = copies_of(src_refs, land_refs, refs[2 * n], refs[2 * n + 1])
        for cp in sends:
            cp.wait_send()
        for cp in recvs:
            cp.wait_recv()

    res = pl.pallas_call(
        body, name=name,
        out_shape=tuple(pltpu.HBM(s.shape, s.dtype) for s in list(srcs) + list(lands)),
        in_specs=[HBM] * (2 * n) + [SEM, SEM] + [pl.BlockSpec(memory_space=pl.ANY)] * len(after),
        out_specs=tuple([HBM] * (2 * n)),
        input_output_aliases={i: i for i in range(2 * n)},
        compiler_params=SPLIT_COPY,
    )(*srcs, *lands, send_sems, recv_sems, *after)
    return list(res[:n]), list(res[n:])


def _forward_copies(srcs, lands, send_sems, recv_sems):
    x, y, c = _mesh_pos()
    sibling = (x, y, 1 - c)
    sends, recvs = [], []
    for w, land in enumerate(lands):
        hr = land.shape[1] // 2
        for j, (px, py) in enumerate(_other_chips(x, y)):
            k = 3 * w + j
            landed = land.at[2 * px + py, pl.ds(c * hr, hr)]
            sends.append(_remote(landed, landed, send_sems.at[k], recv_sems.at[k], sibling))
            other = land.at[2 * px + py, pl.ds((1 - c) * hr, hr)]
            recvs.append(_remote(other, other, send_sems.at[k], recv_sems.at[k], sibling))
    return sends, recvs


def _gather_finish(lands, n_forward):
    n = len(lands)

    def body(*refs):
        outs = refs[n:n + n_forward]
        d2d_send, d2d_recv = refs[2 * n:]
        sends, recvs = _forward_copies(None, outs, d2d_send, d2d_recv)
        for cp in sends:
            cp.start()
        for cp in recvs:
            cp.wait_recv()
        for cp in sends:
            cp.wait_send()

    dma = pltpu.SemaphoreType.DMA
    return pl.pallas_call(
        body, name="gather_finish",
        out_shape=[jax.ShapeDtypeStruct(l.shape, l.dtype) for l in lands],
        in_specs=[HBM] * n, out_specs=[HBM] * n,
        input_output_aliases={i: i for i in range(n)},
        scratch_shapes=[dma((3 * n,)), dma((3 * n,))],
    )(*lands)


def _adamw(w, g, m, v):
    m = ADAM_B1 * m + (1.0 - ADAM_B1) * g
    v = ADAM_B2 * v + (1.0 - ADAM_B2) * (g * g)
    m_hat = m / (1.0 - ADAM_B1 ** ADAM_STEP)
    v_hat = v / (1.0 - ADAM_B2 ** ADAM_STEP)
    delta = -ADAM_LR * (m_hat / (jnp.sqrt(v_hat) + ADAM_EPS) + ADAM_WD * w)
    return delta, m, v


def _adamw_big(partials, w, m, v):
    rows, cols = w.shape
    tr = _elementwise_tile(rows)

    def body(p_ref, w_ref, m_ref, v_ref, g_ref, d_ref, m2_ref, v2_ref):
        g = ((p_ref[0].astype(F32) + p_ref[1].astype(F32)) + p_ref[2].astype(F32)) + p_ref[3].astype(F32)
        g_ref[...] = g
        d_ref[...], m2_ref[...], v2_ref[...] = _adamw(w_ref[...], g, m_ref[...], v_ref[...])

    blk = pl.BlockSpec((tr, cols), lambda i: (i, 0))
    return pl.pallas_call(
        body, name="adamw_big", grid=(rows // tr,),
        in_specs=[pl.BlockSpec((N_CHIPS, tr, cols), lambda i: (0, i, 0)), blk, blk, blk],
        out_specs=[blk] * 4, out_shape=[jax.ShapeDtypeStruct((rows, cols), F32)] * 4,
        compiler_params=_params("parallel"),
    )(partials, w, m, v)


def _sum_devices(gathered, rows):
    cols = gathered.shape[1]

    def body(g_ref, o_ref):
        acc = g_ref[0:rows]
        for d in range(1, N_DEV):
            acc = acc + g_ref[d * rows:(d + 1) * rows]
        o_ref[...] = acc

    return pl.pallas_call(
        body, name="sum_devices", out_shape=jax.ShapeDtypeStruct((rows, cols), F32),
        in_specs=[pl.BlockSpec(memory_space=pltpu.VMEM)], out_specs=pl.BlockSpec(memory_space=pltpu.VMEM),
        compiler_params=pltpu.CompilerParams(vmem_limit_bytes=VMEM_LIMIT_V7X),
    )(gathered)


def _adamw_small(quads):
    n = len(quads)

    def body(*refs):
        ins, outs = refs[:4 * n], refs[4 * n:]
        for t in range(n):
            w, g, m, v = (r[...] for r in ins[4 * t:4 * t + 4])
            outs[3 * t][...], outs[3 * t + 1][...], outs[3 * t + 2][...] = _adamw(w, g, m, v)

    flat = [a for q in quads for a in q]
    vm = pl.BlockSpec(memory_space=pltpu.VMEM)
    res = pl.pallas_call(
        body, name="adamw_small",
        out_shape=[jax.ShapeDtypeStruct(q[0].shape, F32) for q in quads for _ in range(3)],
        in_specs=[vm] * (4 * n), out_specs=[vm] * (3 * n),
    )(*flat)
    return [tuple(res[3 * t:3 * t + 3]) for t in range(n)]


SMALL_PACK_ROWS = 96
META_COLS = D_MODEL // N_CHIPS
CONV_COLS = LRU_WIDTH // N_CHIPS
_WEIGHTS = ['meta_tokens', 'g_pre_mix', 'w_in', 'conv_w', 'conv_b', 'w_a', 'b_a', 'w_x', 'b_x', 'lru_lambda',
            'attn_sinks', 'w_out', 'g_post_mix', 'g_pre_ffn', 'w_ff1', 'w_ff2', 'g_post_ffn']
_BIG = ['w_in', 'w_out', 'w_ff1', 'w_ff2']


def _pack_small(dmeta, g, loss):
    z = lambda r, c: jnp.zeros((r, c), F32)
    rows = [
        dmeta,
        g['g_pre_mix'], g['g_post_mix'], g['g_pre_ffn'], g['g_post_ffn'],
        jnp.concatenate([g['conv_w'], z(4, 512)], axis=1),
        jnp.concatenate([g['conv_b'], g['b_a']], axis=1),
        jnp.concatenate([g['b_x'], g['lru_lambda']], axis=1),
        jnp.concatenate([g['attn_sinks'], z(1, D_MODEL - ATTN_HEADS)], axis=1),
        jnp.concatenate([loss, z(1, D_MODEL - 1)], axis=1),
        z(4, D_MODEL),
        g['w_a'].reshape(32, D_MODEL), g['w_x'].reshape(32, D_MODEL),
    ]
    return jnp.concatenate(rows, axis=0)


def _unpack_small(s, chip):
    return dict(
        meta_tokens=lax.dynamic_slice(s[0:N_META], (0, chip * META_COLS), (N_META, META_COLS)),
        g_pre_mix=s[16:17], g_post_mix=s[17:18], g_pre_ffn=s[18:19], g_post_ffn=s[19:20],
        conv_w=lax.dynamic_slice(s[20:24], (0, chip * CONV_COLS), (4, CONV_COLS)).reshape(1, 4, CONV_COLS),
        conv_b=s[24:25, :512], b_a=s[24:25, 512:], b_x=s[25:26, :512], lru_lambda=s[25:26, 512:],
        attn_sinks=s[26:27, :ATTN_HEADS], loss=s[27, 0],
        w_a=s[32:64].reshape(1, LRU_BLOCKS, LRU_BLOCK, LRU_BLOCK),
        w_x=s[64:96].reshape(1, LRU_BLOCKS, LRU_BLOCK, LRU_BLOCK))


def _as2d(a):
    if a.ndim == 2:
        return a
    return a.reshape(-1, a.shape[-1])


def kernel(x, meta_tokens, g_pre_mix, w_in, conv_w, conv_b, w_a, b_a, w_x, b_x, lru_lambda, attn_sinks, w_out, g_post_mix, g_pre_ffn, w_ff1, w_ff2, g_post_ffn, loss_target, m_meta_tokens, m_g_pre_mix, m_w_in, m_conv_w, m_conv_b, m_w_a, m_b_a, m_w_x, m_b_x, m_lru_lambda, m_attn_sinks, m_w_out, m_g_post_mix, m_g_pre_ffn, m_w_ff1, m_w_ff2, m_g_post_ffn, v_meta_tokens, v_g_pre_mix, v_w_in, v_conv_w, v_conv_b, v_w_a, v_b_a, v_w_x, v_b_x, v_lru_lambda, v_attn_sinks, v_w_out, v_g_post_mix, v_g_pre_ffn, v_w_ff1, v_w_ff2, v_g_post_ffn):
    weights = dict(meta_tokens=meta_tokens, g_pre_mix=g_pre_mix, w_in=w_in, conv_w=conv_w, conv_b=conv_b, w_a=w_a,
                   b_a=b_a, w_x=w_x, b_x=b_x, lru_lambda=lru_lambda, attn_sinks=attn_sinks, w_out=w_out,
                   g_post_mix=g_post_mix, g_pre_ffn=g_pre_ffn, w_ff1=w_ff1, w_ff2=w_ff2, g_post_ffn=g_post_ffn)
    mom1 = dict(zip(_WEIGHTS, [m_meta_tokens, m_g_pre_mix, m_w_in, m_conv_w, m_conv_b, m_w_a, m_b_a, m_w_x, m_b_x,
                               m_lru_lambda, m_attn_sinks, m_w_out, m_g_post_mix, m_g_pre_ffn, m_w_ff1, m_w_ff2,
                               m_g_post_ffn]))
    mom2 = dict(zip(_WEIGHTS, [v_meta_tokens, v_g_pre_mix, v_w_in, v_conv_w, v_conv_b, v_w_a, v_b_a, v_w_x, v_b_x,
                               v_lru_lambda, v_attn_sinks, v_w_out, v_g_post_mix, v_g_pre_ffn, v_w_ff1, v_w_ff2,
                               v_g_post_ffn]))
    xi, yi, ci = _mesh_pos()
    chip = 2 * xi + yi

    tiny = jnp.concatenate([meta_tokens, jnp.pad(conv_w[0], ((0, 4), (0, 128)))], axis=0)
    chip_arr = jnp.reshape(chip, (1,)).astype(jnp.int32)
    big2d = lambda a, name: a[0].T if name == 'w_in' else a[0]
    shards, lands = zip(*[_prep_shard(big2d(weights[n], n), chip_arr) for n in _BIG])
    g_in, g_tiny = _gather_weights(shards[:1], lands[:1], tiny, _prep_tiny(tiny, chip_arr))
    w_in_full = g_in.reshape(IN_WIDTH, D_MODEL)
    meta_full = jnp.concatenate([g_tiny[j, :N_META] for j in range(N_CHIPS)], axis=1)
    conv_w_full = jnp.concatenate([g_tiny[j, N_META:N_META + 4, :128] for j in range(N_CHIPS)], axis=1)
    g_send, g_recv, late_thru, late_lands, token = _split_start(
        "gather_late_start", _gather_copies, shards[1:], lands[1:])

    def late_weights(after):
        thru, landed = _split_wait("gather_late_wait", _gather_copies, g_send, g_recv, late_thru, late_lands, after)
        f_send, f_recv, f_thru, f_lands, _ = _split_start("gather_forward_start", _forward_copies, thru, landed)
        _, (g_out,) = _split_wait("gather_out_wait", _forward_copies, f_send, f_recv, f_thru[:1], f_lands[:1], [])

        def ffn_weights(after):
            _, (g_f1, g_f2) = _split_wait("gather_forward_wait", _shifted(_forward_copies, 3), f_send, f_recv,
                                          f_thru[1:], f_lands[1:], after)
            return g_f1, g_f2

        return g_out.reshape(D_MODEL, D_MODEL), ffn_weights

    pos = jnp.stack([ci, chip]).astype(jnp.int32)
    ffn = {}


    def on_ffn_grads(dw1, dw2):
        parts = [dw1, dw2]
        lands = [lax.empty((p.shape[0], p.shape[1] // 2, p.shape[2]), p.dtype) for p in parts]
        ffn['sib'] = _split_start("sibling_ffn_start", _sibling_copies, parts, lands, len(parts))
        return ffn['sib'][4]

    def on_outproj_bwd(dattn):
        send, recv, thru, lands, _ = ffn['sib']
        parts, from_sibling = _split_wait("sibling_ffn_wait", _sibling_copies, send, recv, thru, lands, [dattn])
        cparts_ffn, lands_ffn = zip(*[_chip_presum(p, r, pos) for p, r in zip(parts, from_sibling)])
        ffn['send'], ffn['recv'], ffn['thru'], ffn['lands'], token3 = _split_start(
            "scatter_ffn_start", _scatter_copies, cparts_ffn, lands_ffn)
        return token3

    def on_mixer_grads(dw_in, dw_out):
        parts = [dw_in.reshape(N_CHIPS, IN_WIDTH // N_CHIPS, D_MODEL),
                 dw_out.reshape(N_CHIPS, D_MODEL // N_CHIPS, D_MODEL)]
        cparts, lands = zip(*[_chip_presum(p, r, pos) for p, r in zip(parts, _sibling_exchange(parts, pos))])
        ffn_cparts, ffn_lands = _split_wait("scatter_ffn_wait", _scatter_copies, ffn['send'], ffn['recv'],
                                            ffn['thru'], ffn['lands'], list(cparts))
        n_ici = 3 * len(cparts)
        ffn['n_ici'] = n_ici
        ffn['mixer'] = _split_start("scatter_mixer_start", _two_groups(_scatter_copies, len(cparts), n_ici, _inchip_copies),
                                    list(cparts) + ffn_cparts, list(lands) + ffn_lands, n_ici + 4 * len(ffn_cparts))
        return ffn['mixer'][4]

    head = jnp.concatenate([jnp.zeros((PAD_ROWS, D_MODEL), F32), meta_full], axis=0)
    loss, dx, dhead, grads = _local_step(head, x[0], loss_target[0], g_pre_mix, w_in_full, conv_w_full, conv_b, w_a[0],
                                         b_a, w_x[0], b_x, lru_lambda, attn_sinks, g_post_mix, g_pre_ffn, g_post_ffn,
                                         late_weights, on_ffn_grads, on_outproj_bwd, on_mixer_grads, token)
    grad_x = dx[None]

    pack = _pack_small(dhead[PAD_ROWS:], grads, loss)
    dev = jnp.reshape(4 * xi + 2 * yi + ci, (1,)).astype(jnp.int32)
    send, recv, thru, lands, _ = ffn['mixer']
    nm = len(thru) // 2
    mixer_cparts, mixer_lands = _split_wait("scatter_mixer_wait", _scatter_copies, send, recv, thru[:nm], lands[:nm],
                                            [pack])
    n_small = N_DEV - 1
    t_send, t_recv, t_thru, t_lands, token6 = _split_start(
        "gather_small_start", _two_groups(_all_peers_copies, 1, n_small, _inchip_copies), [pack] + mixer_cparts,
        [_prep_tiny(pack, dev, N_DEV)] + mixer_lands, n_small + 4 * len(mixer_cparts))
    _, ffn_partials = _split_wait("inchip_ffn_wait", _shifted(_inchip_copies, ffn['n_ici']), send, recv, thru[nm:],
                                  lands[nm:], [token6])

    g_out_d, delta, new_m, new_v = {}, {}, {}, {}

    def adamw_big(names, partials):
        for name, part in zip(names, partials):
            shp = weights[name].shape
            res = _adamw_big(part, big2d(weights[name], name), big2d(mom1[name], name), big2d(mom2[name], name))
            g_out_d[name], delta[name], new_m[name], new_v[name] = (big2d(r[None], name).reshape(shp) for r in res)

    adamw_big(_BIG[2:], ffn_partials)
    _, mixer_partials = _split_wait("inchip_mixer_wait", _shifted(_inchip_copies, n_small), t_send, t_recv, t_thru[1:],
                                    t_lands[1:], [g_out_d[n] for n in _BIG[2:]])
    adamw_big(_BIG[:2], mixer_partials)

    _, (gathered,) = _split_wait("gather_small_wait", _all_peers_copies, t_send, t_recv, t_thru[:1], t_lands[:1],
                                 [g_out_d[n] for n in _BIG])
    small = _unpack_small(_sum_devices(gathered.reshape(N_DEV * SMALL_PACK_ROWS, D_MODEL), SMALL_PACK_ROWS), chip)
    loss = small['loss']
    small_names = [n for n in _WEIGHTS if n not in _BIG]
    quads = [(_as2d(weights[n]), _as2d(small[n]), _as2d(mom1[n]), _as2d(mom2[n])) for n in small_names]
    for name, (d, m2, v2) in zip(small_names, _adamw_small(quads)):
        shp = weights[name].shape
        g_out_d[name] = small[name].reshape(shp)
        delta[name], new_m[name], new_v[name] = d.reshape(shp), m2.reshape(shp), v2.reshape(shp)

    return (loss, grad_x, *[g_out_d[n] for n in _WEIGHTS], *[delta[n] for n in _WEIGHTS],
            *[new_m[n] for n in _WEIGHTS], *[new_v[n] for n in _WEIGHTS])
```

```python
import numpy as np
import jax
import jax.numpy as jnp
from jax import lax
from jax.experimental import pallas as pl
from jax.experimental.pallas import tpu as pltpu

F32 = jnp.float32
BF16 = jnp.bfloat16

D_MODEL = 1024
N_META = 16
BLOCK = 128
PAD_ROWS = BLOCK - N_META
HEAD_DIM = 64
ATTN_HEADS = 8
GQA_GROUP = 4
ATTN_WIDTH = 512
KV_WIDTH = 128
QKV_WIDTH = ATTN_WIDTH + 2 * KV_WIDTH
LRU_WIDTH = 512
LRU_BLOCKS = 8
LRU_BLOCK = 64
LRU_C = 8.0
IN_WIDTH = 1792
D_FF = 4096
N_CHIPS = 4
FF_CHUNK = D_FF // N_CHIPS
EPS = 1e-6
NEG = -1e30

ADAM_LR = 0.001
ADAM_B1 = 0.9
ADAM_B2 = 0.999
ADAM_EPS = 1e-08
ADAM_WD = 0.01
ADAM_STEP = 10

VMEM_LIMIT_V7X = 62 * 1024 * 1024
MESH = pl.DeviceIdType.MESH

NT = (((1,), (1,)), ((), ()))
TN = (((0,), (0,)), ((), ()))


def _row_tile(tp):
    return 640 if tp % 640 == 0 else BLOCK


def _elementwise_tile(rows):
    return 512 if rows % 512 == 0 else rows


def _wgrad_row_tile(tp):
    return 1664 if tp % 1664 == 0 else _row_tile(tp)


def _params(*sem):
    return pltpu.CompilerParams(dimension_semantics=sem, vmem_limit_bytes=VMEM_LIMIT_V7X)


def _dot(a, b):
    return jnp.dot(a, b, preferred_element_type=F32)


def _dot_nt(a, b):
    return lax.dot_general(a, b, NT, preferred_element_type=F32)


def _dot_tn(a, b):
    return lax.dot_general(a, b, TN, preferred_element_type=F32)


def _rms(x):
    rs = lax.rsqrt(jnp.mean(x * x, axis=-1, keepdims=True) + EPS)
    return x * rs, rs


def _rms_bwd(xhat, rs, g, dy):
    dyg = dy * g
    dx = rs * (dyg - xhat * jnp.mean(dyg * xhat, axis=-1, keepdims=True))
    dg = jnp.sum(dy * xhat, axis=0, keepdims=True)
    return dx, dg


def _gelu(x):
    k = 0.7978845608028654
    t = jnp.tanh(x * (k + (k * 0.044715) * (x * x)))
    return (0.5 * x) * (1.0 + t), t


def _gelu_grad(x, t):
    k = 0.7978845608028654
    return 0.5 * (1.0 + t) + 0.5 * x * (1.0 - t * t) * k * (1.0 + 3 * 0.044715 * x * x)


def _sigmoid(x):
    return 0.5 * jnp.tanh(0.5 * x) + 0.5


def _one_minus_exp2(y):
    t = jnp.tanh(y)
    return (-2.0 * t) / (1.0 - t)


def _softplus(x):
    return jnp.maximum(x, 0.0) + jnp.log1p(jnp.exp(-jnp.abs(x)))


def _seq_specs(tr, delay=0):
    qb = tr // BLOCK
    tile = lambda i: jnp.maximum(i - delay, 0)
    return [pl.BlockSpec((BLOCK, D_MODEL), lambda i, *_, s=s: (jnp.maximum(tile(i) * qb + s - 1, 0), 0))
            for s in range(qb)]


def _seq_tile(head, pieces, i):
    first = jnp.where(i == 0, head, pieces[0][...])
    return jnp.concatenate([first] + [p[...] for p in pieces[1:]], axis=0)


GROUP_ROWS = GQA_GROUP * BLOCK


def _attn_bias():
    j = np.arange(2 * BLOCK)[:, None]
    i = np.arange(BLOCK)[None, :]
    band = (j - i >= 1) & (j - i <= BLOCK)
    out = []
    for n in range(3):
        ok = band & ((n - 1) * BLOCK + j >= PAD_ROWS) if n < 2 else band
        out.append(np.tile(np.where(ok, 0.0, NEG).astype(np.float32), (1, GQA_GROUP)))
    return jnp.asarray(np.stack(out))


def _heads_t(at, g):
    heads = range(GQA_GROUP * g, GQA_GROUP * (g + 1))
    return jnp.concatenate([at[h * HEAD_DIM:(h + 1) * HEAD_DIM] for h in heads], axis=1).astype(BF16)


def _from_heads_t(groups):
    pairs = []
    for p in groups:
        for h in range(0, GQA_GROUP, 2):
            two = jnp.concatenate([p[:, h * BLOCK:(h + 1) * BLOCK], p[:, (h + 1) * BLOCK:(h + 2) * BLOCK]], axis=0)
            pairs.append(two.T)
    return jnp.concatenate(pairs, axis=1)


def _stack_heads(a, g):
    heads = range(GQA_GROUP * g, GQA_GROUP * (g + 1))
    return jnp.concatenate([a[:, h * HEAD_DIM:(h + 1) * HEAD_DIM] for h in heads], axis=0)


def _unstack_heads(groups):
    return jnp.concatenate([p[h * BLOCK:(h + 1) * BLOCK] for p in groups for h in range(GQA_GROUP)], axis=1)


def _attn_probs_t(k_g, qg, bias, sink_row):
    st = _dot_nt(k_g, qg) + bias
    m = jnp.maximum(jnp.max(st, axis=0, keepdims=True), sink_row)
    p = jnp.exp(st - m)
    es = jnp.exp(sink_row - m)
    inv = 1.0 / (jnp.sum(p, axis=0, keepdims=True) + es)
    return p * inv, es * inv


def _attn_consts(sinks):
    return jnp.repeat(sinks.reshape(ATTN_HEADS), BLOCK).reshape(ATTN_HEADS // GQA_GROUP, GROUP_ROWS), _attn_bias()


_SINK_SPEC = pl.BlockSpec((ATTN_HEADS // GQA_GROUP, GROUP_ROWS), lambda n: (0, 0))
_BIAS_SPEC = pl.BlockSpec((3, 2 * BLOCK, GROUP_ROWS), lambda n: (0, 0, 0))
_QSCALE = HEAD_DIM ** -0.5


def _kv_specs(tr):
    qb = tr // BLOCK
    prev = lambda col: pl.BlockSpec((BLOCK, KV_WIDTH), lambda t: (jnp.maximum(t * qb - 1, 0), col))
    cur = lambda col: pl.BlockSpec((tr, KV_WIDTH), lambda t: (t, col))
    return [prev(4), cur(4), prev(5), cur(5)]


def _block_bias(b_ref, t, qb, i):
    return b_ref[2] if i >= 2 else b_ref[jnp.minimum(t * qb + i, 2)]


N_KV = ATTN_HEADS // GQA_GROUP


def _prob_specs(qb):
    return [pl.BlockSpec((qb, N_KV, 2 * BLOCK, GROUP_ROWS), lambda t: (t, 0, 0, 0)),
            pl.BlockSpec((qb, SUBLANES, GROUP_ROWS), lambda t: (t, 0, 0))]


def _attn_fwd(qkv, sinks):
    tp = qkv.shape[0]
    tr = _row_tile(tp)
    qb, nb = tr // BLOCK, tp // BLOCK
    sink_rows, bias = _attn_consts(sinks)

    def body(s_ref, b_ref, q_ref, kp_ref, kc_ref, vp_ref, vc_ref, o_ref, p_ref, ps_ref):
        t = pl.program_id(0)
        k_all = jnp.concatenate([kp_ref[...], kc_ref[...]], axis=0)
        v_all = jnp.concatenate([vp_ref[...], vc_ref[...]], axis=0)
        for i in range(qb):
            rows = slice(i * BLOCK, (i + 1) * BLOCK)
            q = q_ref[rows]
            k2, v2 = k_all[i * BLOCK:(i + 2) * BLOCK], v_all[i * BLOCK:(i + 2) * BLOCK]
            bias_n = _block_bias(b_ref, t, qb, i)
            outs, sink_probs = [], []
            for g in range(N_KV):
                cols = slice(g * HEAD_DIM, (g + 1) * HEAD_DIM)
                qg = _stack_heads(q, g) * jnp.asarray(_QSCALE, BF16)
                p, ps = _attn_probs_t(k2[:, cols], qg, bias_n, s_ref[g:g + 1])
                pb = p.astype(BF16)
                p_ref[i, g] = pb
                sink_probs.append(ps)
                outs.append(_dot_tn(pb, v2[:, cols]))
            o_ref[rows] = _unstack_heads(outs).astype(BF16)
            ps_ref[i] = jnp.concatenate(sink_probs + [jnp.zeros((SUBLANES - N_KV, GROUP_ROWS), F32)], axis=0)

    return pl.pallas_call(
        body, name="attn_fwd", grid=(tp // tr,),
        in_specs=[_SINK_SPEC, _BIAS_SPEC, pl.BlockSpec((tr, ATTN_WIDTH), lambda t: (t, 0))] + _kv_specs(tr),
        out_specs=[pl.BlockSpec((tr, ATTN_WIDTH), lambda t: (t, 0))] + _prob_specs(qb),
        out_shape=[jax.ShapeDtypeStruct((tp, ATTN_WIDTH), BF16),
                   jax.ShapeDtypeStruct((nb, N_KV, 2 * BLOCK, GROUP_ROWS), BF16),
                   jax.ShapeDtypeStruct((nb, SUBLANES, GROUP_ROWS), F32)],
        compiler_params=_params("parallel"),
    )(sink_rows, bias, qkv, qkv, qkv, qkv, qkv)


def _conv_taps(x, halo):
    ext = jnp.concatenate([halo, x], axis=0)
    return [ext[8:] if k == 3 else pltpu.roll(ext, 3 - k, 0)[8:] for k in range(4)]


def _lru_gates(xc, wa, ba, wx, bx, sp):
    xb = xc.astype(BF16)
    r = _sigmoid(_dot(xb, wa) + ba)
    ig = _sigmoid(_dot(xb, wx) + bx)
    log_a = (-LRU_C * sp) * r
    a = jnp.exp(log_a)
    mult = jnp.sqrt(_one_minus_exp2(log_a))
    return xb, r, ig, a, mult


SUBLANES = 8


def _scan_fwd(a, b, h_in):
    n, width = a.shape
    a, b = (v.reshape(n // SUBLANES, SUBLANES, width) for v in (a, b))
    in_group = lax.broadcasted_iota(jnp.int32, a.shape, 1)
    for d in (1, 2, 4):
        keep = in_group >= d
        b = jnp.where(keep, a * pltpu.roll(b, d, 1) + b, b)
        a = jnp.where(keep, a * pltpu.roll(a, d, 1), a)
    a, b = a.reshape(n, width), b.reshape(n, width)
    out, carry = [], h_in
    for g in range(0, n, SUBLANES):
        h = a[g:g + SUBLANES] * carry + b[g:g + SUBLANES]
        out.append(h)
        carry = h[SUBLANES - 1:]
    return jnp.concatenate(out, axis=0)


def _scan_rev(c, b, g_in):
    n, width = c.shape
    c, b = (v.reshape(n // SUBLANES, SUBLANES, width) for v in (c, b))
    in_group = lax.broadcasted_iota(jnp.int32, c.shape, 1)
    for d in (1, 2, 4):
        keep = in_group < SUBLANES - d
        b = jnp.where(keep, b + c * pltpu.roll(b, SUBLANES - d, 1), b)
        c = jnp.where(keep, c * pltpu.roll(c, SUBLANES - d, 1), c)
    c, b = c.reshape(n, width), b.reshape(n, width)
    out, carry = [], g_in
    for g in range(n - SUBLANES, -1, -SUBLANES):
        r = b[g:g + SUBLANES] + c[g:g + SUBLANES] * carry
        out.append(r)
        carry = r[:1]
    return jnp.concatenate(out[::-1], axis=0)


def _inproj_lru_fwd(head, x, g, w_in, conv_w, conv_b, wa, ba, wx, bx, lam, token):
    tp = BLOCK + x.shape[0]
    tr = _row_tile(tp)
    qb, nt = tr // BLOCK, tp // tr
    small = [conv_w, conv_b, wa, ba, wx, bx, lam]

    def body(*refs):
        head_ref, pieces = refs[0], refs[1:1 + qb]
        g_ref, w_ref, _, cw_ref, cb_ref, wa_ref, ba_ref, wx_ref, bx_ref, lam_ref = refs[1 + qb:11 + qb]
        u_ref, qkv_ref, xr_ref, yr_ref, hr_ref, rec_ref, zbuf, halo, hprev = refs[11 + qb:]
        i = pl.program_id(0)
        cur = i % 2

        @pl.when(i == 0)
        def _():
            halo[...] = jnp.zeros_like(halo)
            hprev[...] = jnp.zeros_like(hprev)
            zbuf[1] = jnp.zeros((tr, 2 * LRU_WIDTH), F32)

        def recurrent_branch(valid):
            cw, cb = cw_ref[...], cb_ref[...]
            wa_m, ba_v, wx_m, bx_v = wa_ref[...], ba_ref[...], wx_ref[...], bx_ref[...]
            sp = _softplus(-lam_ref[...])
            before, h_last = halo[...], hprev[0:1]
            for b in range(qb):
                rows = slice(b * BLOCK, (b + 1) * BLOCK)
                xy = zbuf[1 - cur, rows]
                xin = xy[:, :LRU_WIDTH]
                taps = _conv_taps(xin, before)
                before = xin[BLOCK - 8:]
                xc = cb + sum(cw[k:k + 1] * taps[k] for k in range(4))
                _, _, ig, a, mult = _lru_gates(xc, wa_m, ba_v, wx_m, bx_v, sp)
                u = mult * (ig * xc)
                if b == 0:
                    pos = (i - 1) * tr + lax.broadcasted_iota(jnp.int32, xc.shape, 0)
                    u = jnp.where(pos >= PAD_ROWS, u, 0.0)
                h = _scan_fwd(a, u, h_last)
                h_last = h[BLOCK - 1:]
                hr_ref[rows] = h
                gl, _ = _gelu(xy[:, LRU_WIDTH:])
                rec_ref[rows] = (gl * h).astype(BF16)
            halo[...] = jnp.where(valid, before, 0.0)
            hprev[0:1] = jnp.where(valid, h_last, 0.0)

        def projection():
            xhat, _ = _rms(_seq_tile(head_ref[...], pieces, i))
            u = (xhat * g_ref[...]).astype(BF16)
            u_ref[...] = u
            z = _dot_nt(u, w_ref[...])
            qkv_ref[...] = z[:, :QKV_WIDTH].astype(BF16)
            xr_ref[...] = z[:, QKV_WIDTH:QKV_WIDTH + LRU_WIDTH]
            yr_ref[...] = z[:, QKV_WIDTH + LRU_WIDTH:]
            zbuf[cur] = z[:, QKV_WIDTH:]

        @pl.when(i < nt)
        def _():
            recurrent_branch(i >= 1)
            projection()

        @pl.when(i == nt)
        def _():
            recurrent_branch(True)

    last = nt - 1
    this_row = lambda w: pl.BlockSpec((tr, w), lambda i: (jnp.minimum(i, last), 0))
    prev_row = lambda w: pl.BlockSpec((tr, w), lambda i: (jnp.maximum(i - 1, 0), 0))
    full = lambda a: pl.BlockSpec(a.shape, lambda i: (0,) * a.ndim)
    piece_specs = [pl.BlockSpec((BLOCK, D_MODEL), lambda i, s=s: (jnp.maximum(jnp.minimum(i, last) * qb + s - 1, 0), 0))
                   for s in range(qb)]
    return pl.pallas_call(
        body, name="inproj_lru_fwd", grid=(nt + 1,),
        in_specs=[full(head)] + piece_specs + [full(g), full(w_in), full(token)] + [full(a) for a in small],
        out_specs=[this_row(D_MODEL), this_row(QKV_WIDTH), this_row(LRU_WIDTH), this_row(LRU_WIDTH),
                   prev_row(LRU_WIDTH), prev_row(LRU_WIDTH)],
        out_shape=[jax.ShapeDtypeStruct((tp, D_MODEL), BF16), jax.ShapeDtypeStruct((tp, QKV_WIDTH), BF16),
                   jax.ShapeDtypeStruct((tp, LRU_WIDTH), F32), jax.ShapeDtypeStruct((tp, LRU_WIDTH), F32),
                   jax.ShapeDtypeStruct((tp, LRU_WIDTH), F32), jax.ShapeDtypeStruct((tp, LRU_WIDTH), BF16)],
        scratch_shapes=[pltpu.VMEM((2, tr, 2 * LRU_WIDTH), F32), pltpu.VMEM((8, LRU_WIDTH), F32),
                        pltpu.VMEM((8, LRU_WIDTH), F32)],
        compiler_params=_params("arbitrary"),
    )(head, *([x] * qb), g, w_in, token, *small)


def _outproj_fwd(attn, rec, w_out, head, x, g_post_mix, g_pre_ffn):
    tp = attn.shape[0]
    tr = _row_tile(tp)
    qb = tr // BLOCK

    def body(*refs):
        a_ref, r_ref, w_ref, head_ref = refs[:4]
        pieces = refs[4:4 + qb]
        gm_ref, gf_ref, mix_ref, h1_ref, u1_ref = refs[4 + qb:]
        mix = _dot(a_ref[...], w_ref[:ATTN_WIDTH]) + _dot(r_ref[...], w_ref[ATTN_WIDTH:])
        mix_ref[...] = mix
        mhat, _ = _rms(mix)
        h1 = _seq_tile(head_ref[...], pieces, pl.program_id(0)) + mhat * gm_ref[...]
        h1_ref[...] = h1
        hhat, _ = _rms(h1)
        u1_ref[...] = (hhat * gf_ref[...]).astype(BF16)

    row = lambda w: pl.BlockSpec((tr, w), lambda i: (i, 0))
    full = lambda a: pl.BlockSpec(a.shape, lambda i: (0,) * a.ndim)
    return pl.pallas_call(
        body, name="outproj_fwd", grid=(tp // tr,),
        in_specs=[row(ATTN_WIDTH), row(LRU_WIDTH), full(w_out), full(head)] + _seq_specs(tr)
        + [full(g_post_mix), full(g_pre_ffn)],
        out_specs=[row(D_MODEL), row(D_MODEL), row(D_MODEL)],
        out_shape=[jax.ShapeDtypeStruct((tp, D_MODEL), F32), jax.ShapeDtypeStruct((tp, D_MODEL), F32),
                   jax.ShapeDtypeStruct((tp, D_MODEL), BF16)],
        compiler_params=_params("parallel"),
    )(attn, rec, w_out, head, *([x] * qb), g_post_mix, g_pre_ffn)


FFN_STEPS = N_CHIPS


def _resident(a):
    return pl.BlockSpec(a.shape, lambda *_: (0,) * a.ndim, pipeline_mode=pl.Buffered(1))


def _ffn_fwd(u1, w1, w2, h1, tgt, g_post_ffn):
    tp = h1.shape[0]
    tr = _row_tile(tp)
    qb, nt = tr // BLOCK, tp // tr
    sr = tr // FFN_STEPS

    def body(*refs):
        u_ref, w1_ref, w2_ref, h1_ref = refs[:4]
        t_pieces = refs[4:4 + qb]
        g_ref, r1_ref, dy_ref, df2_ref, loss_ref, dg_ref, acc = refs[4 + qb:]
        i, c = pl.program_id(0), pl.program_id(1)
        cur = i % 2

        @pl.when((i == 0) & (c == 0))
        def _():
            loss_ref[...] = jnp.zeros_like(loss_ref)
            dg_ref[...] = jnp.zeros_like(dg_ref)
            acc[1] = jnp.zeros((tr, D_MODEL), F32)

        def matmuls():
            r = jnp.maximum(_dot(u_ref[...], w1_ref[c]), 0.0)
            r1_ref[...] = r.astype(BF16)
            return _dot((r * r).astype(BF16), w2_ref[c])

        def finish_previous_tile(k, valid):
            lo, hi = k * sr, (k + 1) * sr
            g = g_ref[...]
            fhat, rs = _rms(acc[1 - cur, lo:hi])
            h2 = h1_ref[...] + fhat * g
            rows = (i - 1) * tr + lo + lax.broadcasted_iota(jnp.int32, h2.shape, 0)
            tgt = jnp.concatenate([p[max(lo - s * BLOCK, 0):min(hi - s * BLOCK, BLOCK)] for s, p in enumerate(t_pieces)
                                   if lo < (s + 1) * BLOCK and hi > s * BLOCK], axis=0)
            err = jnp.where((rows >= BLOCK) & valid, h2 - tgt, 0.0)
            dy = err * (1.0 / D_MODEL)
            dy_ref[...] = dy
            loss_ref[...] += (0.5 / D_MODEL) * jnp.sum(err * err)
            df2, dg = _rms_bwd(fhat, rs, g, dy)
            df2_ref[...] = df2.astype(BF16)
            dg_ref[...] += dg

        for k in range(FFN_STEPS):
            @pl.when((c == k) & (i < nt))
            def _(k=k):
                finish_previous_tile(k, i >= 1)
                if k == 0:
                    acc[cur] = matmuls()
                else:
                    acc[cur] += matmuls()

            @pl.when((c == k) & (i == nt))
            def _(k=k):
                finish_previous_tile(k, True)

    last = nt - 1
    this_row = pl.BlockSpec((tr, D_MODEL), lambda i, c: (jnp.minimum(i, last), 0))
    prev_quarter = pl.BlockSpec((sr, D_MODEL), lambda i, c: (jnp.maximum(i - 1, 0) * FFN_STEPS + c, 0))
    prev_quarter_out = pl.BlockSpec(
        (sr, D_MODEL), lambda i, c: (jnp.where(i == 0, nt * FFN_STEPS, (i - 1) * FFN_STEPS + c), 0))
    full = lambda a: pl.BlockSpec(a.shape, lambda i, c: (0,) * a.ndim)
    return pl.pallas_call(
        body, name="ffn_fwd", grid=(nt + 1, FFN_STEPS),
        in_specs=[this_row, _resident(w1), _resident(w2), prev_quarter] + _seq_specs(tr, delay=1) + [full(g_post_ffn)],
        out_specs=[pl.BlockSpec((tr, FF_CHUNK), lambda i, c: (jnp.minimum(i, last), jnp.where(i < nt, c, FFN_STEPS - 1))),
                   prev_quarter_out, prev_quarter_out,
                   pl.BlockSpec((1, 1), lambda i, c: (0, 0)), pl.BlockSpec((1, D_MODEL), lambda i, c: (0, 0))],
        out_shape=[jax.ShapeDtypeStruct((tp, D_FF), BF16), jax.ShapeDtypeStruct((tp + sr, D_MODEL), F32),
                   jax.ShapeDtypeStruct((tp + sr, D_MODEL), BF16), jax.ShapeDtypeStruct((1, 1), F32),
                   jax.ShapeDtypeStruct((1, D_MODEL), F32)],
        scratch_shapes=[pltpu.VMEM((2, tr, D_MODEL), F32)],
        compiler_params=_params("arbitrary", "arbitrary"),
    )(u1, w1, w2, h1, *([tgt] * qb), g_post_ffn)


def _ffn_bwd_data(df2, r1, w1, w2, dy, h1, mix, g_pre_ffn, g_post_mix):
    tp = h1.shape[0]
    tr = _row_tile(tp)
    nt = tp // tr
    sr = tr // FFN_STEPS

    def body(df2_ref, r1_ref, w1_ref, w2_ref, dy_ref, h1_ref, mix_ref, gf_ref, gm_ref,
             da_ref, dh1_ref, dmix_ref, dgf_ref, dgm_ref, acc):
        i, c = pl.program_id(0), pl.program_id(1)
        cur = i % 2

        @pl.when((i == 0) & (c == 0))
        def _():
            dgf_ref[...] = jnp.zeros_like(dgf_ref)
            dgm_ref[...] = jnp.zeros_like(dgm_ref)
            acc[1] = jnp.zeros((tr, D_MODEL), F32)

        def matmuls():
            df = _dot_nt(df2_ref[...], w2_ref[c])
            da = (df * (2.0 * r1_ref[...].astype(F32))).astype(BF16)
            da_ref[...] = da
            return _dot_nt(da, w1_ref[c])

        def finish_previous_tile(k, valid):
            lo, hi = k * sr, (k + 1) * sr
            hhat, rs = _rms(h1_ref[...])
            dx, dgf = _rms_bwd(hhat, rs, gf_ref[...], acc[1 - cur, lo:hi])
            dh1 = dy_ref[...] + dx
            dh1_ref[...] = dh1
            mhat, rsm = _rms(mix_ref[...])
            dmix, dgm = _rms_bwd(mhat, rsm, gm_ref[...], dh1)
            dmix_ref[...] = dmix.astype(BF16)
            dgf_ref[...] += jnp.where(valid, dgf, 0.0)
            dgm_ref[...] += jnp.where(valid, dgm, 0.0)

        for k in range(FFN_STEPS):
            @pl.when((c == k) & (i < nt))
            def _(k=k):
                finish_previous_tile(k, i >= 1)
                if k == 0:
                    acc[cur] = matmuls()
                else:
                    acc[cur] += matmuls()

            @pl.when((c == k) & (i == nt))
            def _(k=k):
                finish_previous_tile(k, True)

    last = nt - 1
    this_row = pl.BlockSpec((tr, D_MODEL), lambda i, c: (jnp.minimum(i, last), 0))
    prev_quarter = pl.BlockSpec((sr, D_MODEL), lambda i, c: (jnp.maximum(i - 1, 0) * FFN_STEPS + c, 0))
    prev_quarter_out = pl.BlockSpec(
        (sr, D_MODEL), lambda i, c: (jnp.where(i == 0, nt * FFN_STEPS, (i - 1) * FFN_STEPS + c), 0))
    chunk = pl.BlockSpec((tr, FF_CHUNK), lambda i, c: (jnp.minimum(i, last), jnp.where(i < nt, c, FFN_STEPS - 1)))
    gain = pl.BlockSpec((1, D_MODEL), lambda i, c: (0, 0))
    return pl.pallas_call(
        body, name="ffn_bwd_data", grid=(nt + 1, FFN_STEPS),
        in_specs=[this_row, chunk, _resident(w1), _resident(w2), prev_quarter, prev_quarter, prev_quarter, gain, gain],
        out_specs=[chunk, prev_quarter_out, prev_quarter_out, gain, gain],
        out_shape=[jax.ShapeDtypeStruct((tp, D_FF), BF16), jax.ShapeDtypeStruct((tp + sr, D_MODEL), F32),
                   jax.ShapeDtypeStruct((tp + sr, D_MODEL), BF16), jax.ShapeDtypeStruct((1, D_MODEL), F32),
                   jax.ShapeDtypeStruct((1, D_MODEL), F32)],
        scratch_shapes=[pltpu.VMEM((2, tr, D_MODEL), F32)],
        compiler_params=_params("arbitrary", "arbitrary"),
    )(df2, r1, w1, w2, dy, h1, mix, g_pre_ffn, g_post_mix)


def _ffn_bwd_weights(u1, da1, r1, df2):
    tp = u1.shape[0]
    tr = _wgrad_row_tile(tp)

    def body(u_ref, da_ref, r1_ref, df2_ref, dw1_ref, dw2_ref):
        i = pl.program_id(1)

        def products():
            r = r1_ref[...].astype(F32)
            return _dot_tn(u_ref[...], da_ref[...]), _dot_tn((r * r).astype(BF16), df2_ref[...])

        @pl.when(i == 0)
        def _():
            dw1_ref[0], dw2_ref[0] = products()

        @pl.when(i > 0)
        def _():
            p1, p2 = products()
            dw1_ref[0] += p1
            dw2_ref[0] += p2

    row = pl.BlockSpec((tr, D_MODEL), lambda c, i: (i, 0))
    chunk = pl.BlockSpec((tr, FF_CHUNK), lambda c, i: (i, c))
    return pl.pallas_call(
        body, name="ffn_bwd_weights", grid=(N_CHIPS, tp // tr),
        in_specs=[row, chunk, chunk, row],
        out_specs=[pl.BlockSpec((1, D_MODEL, FF_CHUNK), lambda c, i: (c, 0, 0)),
                   pl.BlockSpec((1, FF_CHUNK, D_MODEL), lambda c, i: (c, 0, 0))],
        out_shape=[jax.ShapeDtypeStruct((N_CHIPS, D_MODEL, FF_CHUNK), F32),
                   jax.ShapeDtypeStruct((N_CHIPS, FF_CHUNK, D_MODEL), F32)],
        compiler_params=_params("parallel", "arbitrary"),
    )(u1, da1, r1, df2)


N_VEC_ROWS = 8


def _outproj_lru_bwd(dmix, w_out, attn, rec, xr, yr, hr, conv_w, conv_b, wa, ba, wx, bx, lam, token):
    tp = xr.shape[0]
    tr = _row_tile(tp)
    qb, nt = tr // BLOCK, tp // tr

    def body(dm_ref, w_ref, at_ref, rc_ref, xr_ref, xh_ref, yr_ref, hr_ref, hp_ref,
             cw_ref, cb_ref, wa_ref, ba_ref, wx_ref, bx_ref, lam_ref, _,
             dxr_ref, dyr_ref, dat_ref, dwo_ref, dwa_ref, dwx_ref, vec_ref, g_next, a_next, dxc_next, dsp):
        s = pl.program_id(0)
        t = nt - 1 - s

        @pl.when(s == 0)
        def _():
            g_next[...] = jnp.zeros_like(g_next)
            a_next[...] = jnp.zeros_like(a_next)
            dxc_next[...] = jnp.zeros_like(dxc_next)
            dsp[...] = jnp.zeros_like(dsp)
            dwo_ref[...] = jnp.zeros_like(dwo_ref)
            dwa_ref[...] = jnp.zeros_like(dwa_ref)
            dwx_ref[...] = jnp.zeros_like(dwx_ref)
            vec_ref[...] = jnp.zeros_like(vec_ref)

        dm = dm_ref[...]
        dcat = _dot_nt(dm, w_ref[...])
        dat_ref[...] = dcat[:, :ATTN_WIDTH].astype(BF16)
        drec_tile = dcat[:, ATTN_WIDTH:]
        dwo_ref[:ATTN_WIDTH] += _dot_tn(at_ref[...], dm)
        dwo_ref[ATTN_WIDTH:] += _dot_tn(rc_ref[...], dm)

        first_tile = t == 0
        cw, cb = cw_ref[...], cb_ref[...]
        lam_v = lam_ref[...]
        sp = _softplus(-lam_v)
        wa_m, ba_v, wx_m, bx_v = wa_ref[...], ba_ref[...], wx_ref[...], bx_ref[...]
        rows = lax.broadcasted_iota(jnp.int32, (BLOCK, LRU_WIDTH), 0)
        col = lambda v: jnp.sum(v, axis=0, keepdims=True)

        g_after, a_after, dxc_after = g_next[0:1], a_next[0:1], dxc_next[...]
        xbs, dgrs, dgis = [], [], []
        vec = [jnp.zeros((1, LRU_WIDTH), F32) for _ in range(N_VEC_ROWS)]
        for i in reversed(range(qb)):
            blk = slice(i * BLOCK, (i + 1) * BLOCK)
            if i == 0:
                x_before = jnp.where(first_tile, 0.0, xh_ref[...])
                h_before = jnp.where(first_tile, 0.0, hp_ref[7:8])
            else:
                x_before = xr_ref[i * BLOCK - 8:i * BLOCK]
                h_before = hr_ref[i * BLOCK - 1:i * BLOCK]
            taps = _conv_taps(xr_ref[blk], x_before)
            xc = cb + sum(cw[k:k + 1] * taps[k] for k in range(4))
            xb, r, ig, a, mult = _lru_gates(xc, wa_m, ba_v, wx_m, bx_v, sp)

            yr_v = yr_ref[blk]
            gl, th = _gelu(yr_v)
            h = hr_ref[blk]
            drec = drec_tile[blk]
            dyr_ref[blk] = (drec * h * _gelu_grad(yr_v, th)).astype(BF16)

            a_up = jnp.where(rows == BLOCK - 1, a_after, pltpu.roll(a, BLOCK - 1, 0))
            g = _scan_rev(a_up, drec * gl, g_after)
            g_after, a_after = g[0:1], a[0:1]

            h_prev = jnp.where(rows == 0, h_before, pltpu.roll(h, 1, 0))
            du, da = g, g * h_prev
            if i == 0:
                real = (t * tr + rows) >= PAD_ROWS
                du, da = jnp.where(real, du, 0.0), jnp.where(real, da, 0.0)
            dmult = du * (ig * xc)
            dig = du * (mult * xc)
            dxc = du * (mult * ig)
            dlog_a = da * a - dmult * (a * a / mult)
            if i == 0:
                dlog_a = jnp.where(real, dlog_a, 0.0)
            dgr = (dlog_a * (-LRU_C * sp)) * (r * (1.0 - r))
            dgi = dig * (ig * (1.0 - ig))
            dgr_b, dgi_b = dgr.astype(BF16), dgi.astype(BF16)
            dxc = dxc + _dot_nt(dgr_b, wa_m) + _dot_nt(dgi_b, wx_m)
            xbs.append(xb)
            dgrs.append(dgr_b)
            dgis.append(dgi_b)

            ext = jnp.concatenate([dxc, dxc_after], axis=0)
            up = [ext[:BLOCK] if j == 0 else pltpu.roll(ext, BLOCK + 8 - j, 0)[:BLOCK] for j in range(4)]
            dxr_ref[blk] = sum(cw[k:k + 1] * up[3 - k] for k in range(4)).astype(BF16)
            dxc_after = dxc[:8]

            for k in range(4):
                vec[k] = vec[k] + col(dxc * taps[k])
            vec[4] = vec[4] + col(dxc)
            vec[5] = vec[5] + col(dgr)
            vec[6] = vec[6] + col(dgi)
            vec[7] = vec[7] + col(dlog_a * (-LRU_C * r))

        g_next[0:1], a_next[0:1], dxc_next[...] = g_after, a_after, dxc_after
        xb_all = jnp.concatenate(xbs, axis=0)
        dwa_ref[...] += _dot_tn(xb_all, jnp.concatenate(dgrs, axis=0))
        dwx_ref[...] += _dot_tn(xb_all, jnp.concatenate(dgis, axis=0))
        for k in range(7):
            vec_ref[k:k + 1] += vec[k]
        dsp[0:1] += vec[7]

        @pl.when(s == nt - 1)
        def _():
            vec_ref[7:8] = dsp[0:1] * (-_sigmoid(-lam_v))

    blk_spec = pl.BlockSpec((tr, LRU_WIDTH), lambda s: (nt - 1 - s, 0))
    rows_before = pl.BlockSpec((8, LRU_WIDTH), lambda s: (jnp.maximum((nt - 1 - s) * (tr // 8) - 1, 0), 0))
    full = lambda a: pl.BlockSpec(a.shape, lambda s: (0,) * a.ndim)
    small = [conv_w, conv_b, wa, ba, wx, bx, lam, token]
    sq = pl.BlockSpec((LRU_WIDTH, LRU_WIDTH), lambda s: (0, 0))
    wide = pl.BlockSpec((tr, D_MODEL), lambda s: (nt - 1 - s, 0))
    whole = pl.BlockSpec((D_MODEL, D_MODEL), lambda s: (0, 0))
    return pl.pallas_call(
        body, name="outproj_lru_bwd", grid=(nt,),
        in_specs=[wide, whole, blk_spec, blk_spec, blk_spec, rows_before, blk_spec, blk_spec, rows_before]
        + [full(a) for a in small],
        out_specs=[blk_spec, blk_spec, blk_spec, whole, sq, sq, pl.BlockSpec((N_VEC_ROWS, LRU_WIDTH), lambda s: (0, 0))],
        out_shape=[jax.ShapeDtypeStruct((tp, LRU_WIDTH), BF16), jax.ShapeDtypeStruct((tp, LRU_WIDTH), BF16),
                   jax.ShapeDtypeStruct((tp, ATTN_WIDTH), BF16), jax.ShapeDtypeStruct((D_MODEL, D_MODEL), F32),
                   jax.ShapeDtypeStruct((LRU_WIDTH, LRU_WIDTH), F32), jax.ShapeDtypeStruct((LRU_WIDTH, LRU_WIDTH), F32),
                   jax.ShapeDtypeStruct((N_VEC_ROWS, LRU_WIDTH), F32)],
        scratch_shapes=[pltpu.VMEM((8, LRU_WIDTH), F32)] * 4,
        compiler_params=_params("arbitrary"),
    )(dmix, w_out, attn, rec, xr, xr, yr, hr, hr, *small)


def _attn_bwd_tile(tp):
    return _wgrad_row_tile(tp)


def _attn_bwd(qkv, dattn, probs, sink_probs, token):
    tp = qkv.shape[0]
    tr = _attn_bwd_tile(tp)
    qb, nt = tr // BLOCK, tp // tr
    n_groups = N_KV

    def body(p_ref, ps_ref, q_ref, kp_ref, kc_ref, vp_ref, vc_ref, do_ref, _, dq_ref, dkv_ref, ex_ref, ds_ref, dsink):
        t = pl.program_id(0)

        @pl.when(t == 0)
        def _():
            dsink[...] = jnp.zeros_like(dsink)

        k_all = jnp.concatenate([kp_ref[...], kc_ref[...]], axis=0)
        v_all = jnp.concatenate([vp_ref[...], vc_ref[...]], axis=0)
        tail = None
        for i in range(qb):
            rows = slice(i * BLOCK, (i + 1) * BLOCK)
            qt = (q_ref[rows].astype(F32) * _QSCALE).T
            dot = do_ref[rows].astype(F32).T
            k2, v2 = k_all[i * BLOCK:(i + 2) * BLOCK], v_all[i * BLOCK:(i + 2) * BLOCK]
            dqs, dks, dvs = [], [], []
            for g in range(n_groups):
                cols = slice(g * HEAD_DIM, (g + 1) * HEAD_DIM)
                k_g, v_g = k2[:, cols], v2[:, cols]
                qgt, dogt = _heads_t(qt, g), _heads_t(dot, g)
                pb = p_ref[i, g]
                p = pb.astype(F32)
                dpt = _dot(v_g, dogt)
                delta = jnp.sum(p * dpt, axis=0, keepdims=True)
                dst = (p * (dpt - delta)).astype(BF16)
                dqs.append(_dot_tn(k_g, dst) * _QSCALE)
                dks.append(_dot_nt(qgt, dst))
                dvs.append(_dot_nt(dogt, pb))
                dsink[g:g + 1] -= ps_ref[i, g:g + 1] * delta
            dq_ref[rows] = _from_heads_t(dqs).astype(BF16)
            dkv = jnp.concatenate([jnp.concatenate(dks, axis=0).T, jnp.concatenate(dvs, axis=0).T], axis=1)
            if i == 0:
                ex_ref[0] = dkv[:BLOCK]
            else:
                dkv_ref[(i - 1) * BLOCK:i * BLOCK] = (tail + dkv[:BLOCK]).astype(BF16)
            tail = dkv[BLOCK:]
        dkv_ref[(qb - 1) * BLOCK:] = tail.astype(BF16)

        @pl.when(t == nt - 1)
        def _():
            lane = lax.broadcasted_iota(jnp.int32, (1, ATTN_HEADS), 1)
            acc = jnp.zeros((1, ATTN_HEADS), F32)
            for h in range(ATTN_HEADS):
                g, hh = divmod(h, GQA_GROUP)
                acc = acc + jnp.where(lane == h, jnp.sum(dsink[g:g + 1, hh * BLOCK:(hh + 1) * BLOCK]), 0.0)
            ds_ref[...] = acc

    cur = lambda w: pl.BlockSpec((tr, w), lambda t: (t, 0))
    return pl.pallas_call(
        body, name="attn_bwd", grid=(nt,),
        in_specs=_prob_specs(qb) + [cur(ATTN_WIDTH)] + _kv_specs(tr)
        + [cur(ATTN_WIDTH), pl.BlockSpec(token.shape, lambda t: (0, 0))],
        out_specs=[cur(ATTN_WIDTH), cur(2 * KV_WIDTH), pl.BlockSpec((1, BLOCK, 2 * KV_WIDTH), lambda t: (t, 0, 0)),
                   pl.BlockSpec((1, ATTN_HEADS), lambda t: (0, 0))],
        out_shape=[jax.ShapeDtypeStruct((tp, ATTN_WIDTH), BF16), jax.ShapeDtypeStruct((tp, 2 * KV_WIDTH), BF16),
                   jax.ShapeDtypeStruct((nt, BLOCK, 2 * KV_WIDTH), F32), jax.ShapeDtypeStruct((1, ATTN_HEADS), F32)],
        scratch_shapes=[pltpu.VMEM((n_groups, GROUP_ROWS), F32)],
        compiler_params=_params("arbitrary"),
    )(probs, sink_probs, qkv, qkv, qkv, qkv, qkv, dattn, token)


def _fix_dkv(dkv, dkv_extra):
    tp = dkv.shape[0]
    tr = _attn_bwd_tile(tp)
    nt, qb = tp // tr, tr // BLOCK
    if nt == 1:
        return dkv

    def body(d_ref, ex_ref, o_ref):
        o_ref[...] = (d_ref[...].astype(F32) + ex_ref[0]).astype(BF16)

    last = pl.BlockSpec((BLOCK, 2 * KV_WIDTH), lambda t: (t * qb + qb - 1, 0))
    return pl.pallas_call(
        body, name="fix_dkv", grid=(nt - 1,),
        in_specs=[last, pl.BlockSpec((1, BLOCK, 2 * KV_WIDTH), lambda t: (t + 1, 0, 0))],
        out_specs=last, out_shape=jax.ShapeDtypeStruct(dkv.shape, dkv.dtype),
        input_output_aliases={0: 0}, compiler_params=_params("parallel"),
    )(dkv, dkv_extra)


def _inproj_wgrad(dq, dkv, dxr, dyr, u0):
    tp = dq.shape[0]
    tr = _wgrad_row_tile(tp)

    def body(dq_ref, dkv_ref, dxr_ref, dyr_ref, u_ref, dw_ref):
        i = pl.program_id(0)

        def product():
            dz = jnp.concatenate([dq_ref[...], dkv_ref[...], dxr_ref[...], dyr_ref[...]], axis=1)
            return _dot_tn(dz, u_ref[...])

        @pl.when(i == 0)
        def _():
            dw_ref[...] = product()

        @pl.when(i > 0)
        def _():
            dw_ref[...] += product()

    row = lambda w: pl.BlockSpec((tr, w), lambda i: (i, 0))
    return pl.pallas_call(
        body, name="inproj_wgrad", grid=(tp // tr,),
        in_specs=[row(ATTN_WIDTH), row(2 * KV_WIDTH), row(LRU_WIDTH), row(LRU_WIDTH), row(D_MODEL)],
        out_specs=pl.BlockSpec((IN_WIDTH, D_MODEL), lambda i: (0, 0)),
        out_shape=jax.ShapeDtypeStruct((IN_WIDTH, D_MODEL), F32),
        compiler_params=_params("arbitrary"),
    )(dq, dkv, dxr, dyr, u0)


def _inproj_dgrad(dq, dkv, dxr, dyr, w_in, head, x, dh1, g, token):
    tp = dq.shape[0]
    tr = _row_tile(tp)
    nt, qb = tp // tr, tr // BLOCK

    def body(*refs):
        dq_ref, dkv_ref, dxr_ref, dyr_ref, w_ref, head_ref = refs[:6]
        pieces = refs[6:6 + qb]
        dh1_ref, g_ref, _, gx_ref, dhead_ref, dg_ref, buf, sems = refs[6 + qb:]
        i = pl.program_id(0)
        slot = i % 2

        def out_copy(step, at):
            return pltpu.make_async_copy(buf.at[at], gx_ref.at[pl.ds(step * tr - BLOCK, tr)], sems.at[at])

        dz = jnp.concatenate([dq_ref[...], dkv_ref[...], dxr_ref[...], dyr_ref[...]], axis=1)
        du = _dot(dz, w_ref[...])
        hhat, rs = _rms(_seq_tile(head_ref[...], pieces, i))
        dx, dg = _rms_bwd(hhat, rs, g_ref[...], du)
        dh0 = dh1_ref[...] + dx

        @pl.when(i >= 3)
        def _():
            out_copy(i - 2, slot).wait()

        buf[slot] = dh0

        @pl.when(i == 0)
        def _():
            dg_ref[...] = dg
            dhead_ref[...] = dh0[:BLOCK]
            if tr > BLOCK:
                first = pltpu.make_async_copy(buf.at[0, pl.ds(BLOCK, tr - BLOCK)], gx_ref.at[pl.ds(0, tr - BLOCK)],
                                              sems.at[0])
                first.start()
                first.wait()

        @pl.when(i >= 1)
        def _():
            dg_ref[...] += dg
            out_copy(i, slot).start()

        @pl.when(i == nt - 1)
        def _():
            if nt >= 3:
                out_copy(nt - 2, (nt - 2) % 2).wait()
            if nt >= 2:
                out_copy(nt - 1, (nt - 1) % 2).wait()

    row = lambda w: pl.BlockSpec((tr, w), lambda i: (i, 0))
    full = lambda shape: pl.BlockSpec(shape, lambda i: (0,) * len(shape))
    return pl.pallas_call(
        body, name="inproj_dgrad", grid=(tp // tr,),
        in_specs=[row(ATTN_WIDTH), row(2 * KV_WIDTH), row(LRU_WIDTH), row(LRU_WIDTH), full(w_in.shape),
                  full(head.shape)] + _seq_specs(tr) + [row(D_MODEL), full(g.shape), full(token.shape)],
        out_specs=[pl.BlockSpec(memory_space=pl.ANY), full((BLOCK, D_MODEL)), full((1, D_MODEL))],
        out_shape=[jax.ShapeDtypeStruct(x.shape, F32), jax.ShapeDtypeStruct((BLOCK, D_MODEL), F32),
                   jax.ShapeDtypeStruct((1, D_MODEL), F32)],
        scratch_shapes=[pltpu.VMEM((2, tr, D_MODEL), F32), pltpu.SemaphoreType.DMA((2,))],
        compiler_params=_params("arbitrary"),
    )(dq, dkv, dxr, dyr, w_in, head, *([x] * qb), dh1, g, token)


def _dense_block_diag(w):
    eye = jnp.eye(LRU_BLOCKS, dtype=w.dtype)
    return (w[:, :, None, :] * eye[:, None, :, None]).reshape(LRU_WIDTH, LRU_WIDTH)


def _diag_blocks(dense):
    d4 = dense.reshape(LRU_BLOCKS, LRU_BLOCK, LRU_BLOCKS, LRU_BLOCK)
    return jnp.stack([d4[n, :, n, :] for n in range(LRU_BLOCKS)])


def _local_step(head, x, tgt, g_pre_mix, w_in, conv_w, conv_b, w_a, b_a, w_x, b_x, lam, sinks, g_post_mix,
                g_pre_ffn, g_post_ffn, late_weights, on_ffn_grads, on_outproj_bwd, on_mixer_grads, token):
    wa = _dense_block_diag(w_a).astype(BF16)
    wx = _dense_block_diag(w_x).astype(BF16)

    u0, qkv, xr, yr, hr, rec = _inproj_lru_fwd(head, x, g_pre_mix, w_in, conv_w, conv_b, wa, b_a, wx, b_x, lam, token)
    attn, probs, sink_probs = _attn_fwd(qkv, sinks)
    w_out, ffn_weights = late_weights([attn, rec])
    mix, h1, u1 = _outproj_fwd(attn, rec, w_out, head, x, g_post_mix, g_pre_ffn)
    w1, w2 = ffn_weights([u1])
    r1, dy, df2, loss, dg_post_ffn = _ffn_fwd(u1, w1, w2, h1, tgt, g_post_ffn)

    da1, dh1, dmix, dg_pre_ffn, dg_post_mix = _ffn_bwd_data(df2, r1, w1, w2, dy, h1, mix, g_pre_ffn, g_post_mix)
    dw1, dw2 = _ffn_bwd_weights(u1, da1, r1, df2)
    token2 = on_ffn_grads(dw1, dw2)
    dxr, dyr, dattn, dw_out, dwa, dwx, vec = _outproj_lru_bwd(dmix, w_out, attn, rec, xr, yr, hr, conv_w, conv_b,
                                                              wa, b_a, wx, b_x, lam, token2)
    token3 = on_outproj_bwd(dattn, dw_out)
    dq, dkv, dkv_extra, dsinks = _attn_bwd(qkv, dattn, probs, sink_probs, token3)
    dkv = _fix_dkv(dkv, dkv_extra)
    dw_in = _inproj_wgrad(dq, dkv, dxr, dyr, u0)
    token4 = on_mixer_grads(dw_in, dw_out)
    dx, dhead, dg_pre_mix = _inproj_dgrad(dq, dkv, dxr, dyr, w_in, head, x, dh1, g_pre_mix, token4)

    grads = dict(
        g_pre_mix=dg_pre_mix, conv_w=vec[0:4], conv_b=vec[4:5], w_a=_diag_blocks(dwa), b_a=vec[5:6],
        w_x=_diag_blocks(dwx), b_x=vec[6:7], lru_lambda=vec[7:8], attn_sinks=dsinks,
        g_post_mix=dg_post_mix, g_pre_ffn=dg_pre_ffn, g_post_ffn=dg_post_ffn)
    return loss, dx, dhead, grads


HBM = pl.BlockSpec(memory_space=pltpu.HBM)


def _mesh_pos():
    return lax.axis_index("x"), lax.axis_index("y"), lax.axis_index("c")


def _other_chips(x, y):
    return [(1 - x, y), (x, 1 - y), (1 - x, 1 - y)]


def _remote(src, dst, send_sem, recv_sem, to):
    return pltpu.make_async_remote_copy(src_ref=src, dst_ref=dst, send_sem=send_sem, recv_sem=recv_sem,
                                        device_id=to, device_id_type=MESH)


def _gather_weights(shards, lands, tiny, tiny_land):
    nbig = len(shards)

    def body(*refs):
        srcs, tiny_src = refs[:nbig], refs[nbig]
        outs, tiny_out = refs[2 * nbig + 2:3 * nbig + 2], refs[3 * nbig + 2]
        ici_send, ici_recv, d2d_send, d2d_recv, tiny_send, tiny_recv = refs[3 * nbig + 3:]
        x, y, c = _mesh_pos()
        me = 2 * x + y
        chips = _other_chips(x, y)
        sibling = (x, y, 1 - c)
        sends = []
        for w, (src, out) in enumerate(zip(srcs, outs)):
            hr = src.shape[0] // 2
            for j, chip in enumerate(chips):
                k = 3 * w + j
                cp = _remote(src.at[pl.ds(c * hr, hr)], out.at[me, pl.ds(c * hr, hr)],
                             ici_send.at[k], ici_recv.at[k], (*chip, c))
                cp.start()
                sends.append(cp)
        for j, chip in enumerate(chips):
            cp = _remote(tiny_src, tiny_out.at[me], tiny_send.at[j], tiny_recv.at[j], (*chip, c))
            cp.start()
            sends.append(cp)
        for w, (src, out) in enumerate(zip(srcs, outs)):
            hr = src.shape[0] // 2
            for j, (px, py) in enumerate(chips):
                k = 3 * w + j
                landed = out.at[2 * px + py, pl.ds(c * hr, hr)]
                _remote(landed, landed, ici_send.at[k], ici_recv.at[k], sibling).wait_recv()
                cp = _remote(landed, landed, d2d_send.at[k], d2d_recv.at[k], sibling)
                cp.start()
                sends.append(cp)
        for w, (src, out) in enumerate(zip(srcs, outs)):
            hr = src.shape[0] // 2
            for j, (px, py) in enumerate(chips):
                k = 3 * w + j
                other = out.at[2 * px + py, pl.ds((1 - c) * hr, hr)]
                _remote(other, other, d2d_send.at[k], d2d_recv.at[k], sibling).wait_recv()
        for j, (px, py) in enumerate(chips):
            blk = tiny_out.at[2 * px + py]
            _remote(blk, blk, tiny_send.at[j], tiny_recv.at[j], sibling).wait_recv()
        for cp in sends:
            cp.wait_send()

    out_shape = [jax.ShapeDtypeStruct(l.shape, l.dtype) for l in list(lands) + [tiny_land]]
    n = 3 * nbig
    return pl.pallas_call(
        body, name="gather_weights", out_shape=out_shape,
        in_specs=[HBM] * (2 * nbig + 2), out_specs=[HBM] * (nbig + 1),
        input_output_aliases={nbig + 1 + i: i for i in range(nbig + 1)},
        scratch_shapes=[pltpu.SemaphoreType.DMA((n,)),
                        pltpu.SemaphoreType.DMA((n,)), pltpu.SemaphoreType.DMA((n,)), pltpu.SemaphoreType.DMA((n,)),
                        pltpu.SemaphoreType.DMA((3,)), pltpu.SemaphoreType.DMA((3,))],
    )(*shards, tiny, *lands, tiny_land)


def _prep_shard(w, me):
    rows, cols = w.shape
    tr = _elementwise_tile(rows)

    def body(me_ref, w_ref, s_ref, l_ref):
        b = w_ref[...].astype(BF16)
        s_ref[...] = b
        l_ref[0] = b

    return pl.pallas_call(
        body, name="prep_shard",
        grid_spec=pltpu.PrefetchScalarGridSpec(
            num_scalar_prefetch=1, grid=(rows // tr,),
            in_specs=[pl.BlockSpec((tr, cols), lambda i, me_ref: (i, 0))],
            out_specs=[pl.BlockSpec((tr, cols), lambda i, me_ref: (i, 0)),
                       pl.BlockSpec((1, tr, cols), lambda i, me_ref: (me_ref[0], i, 0))]),
        out_shape=[jax.ShapeDtypeStruct((rows, cols), BF16), jax.ShapeDtypeStruct((N_CHIPS, rows, cols), BF16)],
        compiler_params=_params("parallel"),
    )(me, w)


def _prep_tiny(tiny, me, slots=N_CHIPS):
    def body(me_ref, t_ref, l_ref):
        l_ref[0] = t_ref[...]

    return pl.pallas_call(
        body, name="prep_tiny",
        grid_spec=pltpu.PrefetchScalarGridSpec(
            num_scalar_prefetch=1, grid=(1,),
            in_specs=[pl.BlockSpec(tiny.shape, lambda i, me_ref: (0, 0))],
            out_specs=pl.BlockSpec((1,) + tiny.shape, lambda i, me_ref: (me_ref[0], 0, 0))),
        out_shape=jax.ShapeDtypeStruct((slots,) + tiny.shape, tiny.dtype),
    )(me, tiny)


N_DEV = 8


def _sibling_exchange(parts, token):
    def body(*refs):
        n = len(parts)
        srcs, outs, send_sems, recv_sems = refs[:n], refs[n + 1:2 * n + 1], refs[2 * n + 1], refs[2 * n + 2]
        x, y, c = _mesh_pos()
        sibling = (x, y, 1 - c)
        cps = []
        for w, (src, out) in enumerate(zip(srcs, outs)):
            hr = src.shape[1] // 2
            cp = _remote(src.at[:, pl.ds((1 - c) * hr, hr)], out, send_sems.at[w], recv_sems.at[w], sibling)
            cp.start()
            cps.append(cp)
        for cp in cps:
            cp.wait()

    n = len(parts)
    return pl.pallas_call(
        body, name="sibling_exchange",
        out_shape=[jax.ShapeDtypeStruct((p.shape[0], p.shape[1] // 2, p.shape[2]), p.dtype) for p in parts],
        in_specs=[HBM] * n + [pl.BlockSpec(memory_space=pl.ANY)], out_specs=[HBM] * n,
        scratch_shapes=[pltpu.SemaphoreType.DMA((n,)), pltpu.SemaphoreType.DMA((n,))],
    )(*parts, token)


def _chip_presum(part, from_sibling, pos):
    _, hr, cols = from_sibling.shape
    tr = _elementwise_tile(hr)
    steps = hr // tr

    def body(pos_ref, a_ref, b_ref, o_ref, land_ref):
        s = (a_ref[...] + b_ref[...]).astype(BF16)
        o_ref[...] = s

        @pl.when(pl.program_id(1) == pos_ref[1])
        def _():
            land_ref[...] = s

    return pl.pallas_call(
        body, name="chip_presum",
        grid_spec=pltpu.PrefetchScalarGridSpec(
            num_scalar_prefetch=1, grid=(steps, N_CHIPS),
            in_specs=[pl.BlockSpec((1, tr, cols), lambda i, j, p: (j, p[0] * steps + i, 0)),
                      pl.BlockSpec((1, tr, cols), lambda i, j, p: (j, i, 0))],
            out_specs=[pl.BlockSpec((1, tr, cols), lambda i, j, p: (j, i, 0)),
                       pl.BlockSpec((1, tr, cols), lambda i, j, p: (p[1], p[0] * steps + i, 0))]),
        out_shape=[jax.ShapeDtypeStruct(from_sibling.shape, BF16),
                   jax.ShapeDtypeStruct((N_CHIPS, 2 * hr, cols), BF16)],
        compiler_params=_params("arbitrary", "arbitrary"),
    )(pos, part, from_sibling)


def _scatter_partials(cparts, lands, done_cparts=(), done_lands=()):
    n_new = len(cparts)
    nw = n_new + len(done_cparts)

    def body(*refs):
        srcs = refs[:nw]
        outs = refs[2 * nw:3 * nw]
        own_send, own_recv, ici_send, ici_recv, d2d_send, d2d_recv = refs[3 * nw:]
        x, y, c = _mesh_pos()
        me = 2 * x + y
        chips = _other_chips(x, y)
        sibling = (x, y, 1 - c)
        sends = []
        for w in list(range(n_new, nw)) + list(range(n_new)):
            src, out = srcs[w], outs[w]
            hr = src.shape[1]
            mine = out.at[me, pl.ds(c * hr, hr)]
            cp = _remote(src.at[me], mine, own_send.at[w], own_recv.at[w], sibling)
            cp.start()
            sends.append(cp)
            for j, (px, py) in enumerate(chips):
                if w >= n_new:
                    break
                k = 3 * w + j
                cp = _remote(src.at[2 * px + py], mine, ici_send.at[k], ici_recv.at[k], (px, py, c))
                cp.start()
                sends.append(cp)
        for w in list(range(n_new, nw)) + list(range(n_new)):
            src, out = srcs[w], outs[w]
            hr = src.shape[1]
            for j, (px, py) in enumerate(chips):
                k = 3 * w + j
                landed = out.at[2 * px + py, pl.ds(c * hr, hr)]
                if w < n_new:
                    _remote(landed, landed, ici_send.at[k], ici_recv.at[k], sibling).wait_recv()
                cp = _remote(landed, landed, d2d_send.at[k], d2d_recv.at[k], sibling)
                cp.start()
                sends.append(cp)
        for w, (src, out) in enumerate(zip(srcs, outs)):
            hr = src.shape[1]
            other = out.at[me, pl.ds((1 - c) * hr, hr)]
            _remote(other, other, own_send.at[w], own_recv.at[w], sibling).wait_recv()
            for j, (px, py) in enumerate(chips):
                k = 3 * w + j
                other = out.at[2 * px + py, pl.ds((1 - c) * hr, hr)]
                _remote(other, other, d2d_send.at[k], d2d_recv.at[k], sibling).wait_recv()
        for cp in sends:
            cp.wait_send()

    n = 3 * nw
    dma = pltpu.SemaphoreType.DMA
    every = list(cparts) + list(done_cparts)
    every_lands = list(lands) + list(done_lands)
    return pl.pallas_call(
        body, name="scatter_partials",
        out_shape=[jax.ShapeDtypeStruct(l.shape, l.dtype) for l in every_lands],
        in_specs=[HBM] * (2 * nw), out_specs=[HBM] * nw,
        input_output_aliases={nw + i: i for i in range(nw)},
        scratch_shapes=[dma((nw,)), dma((nw,)), dma((n,)), dma((n,)), dma((n,)), dma((n,))],
    )(*every, *every_lands)


SEM = pl.BlockSpec(memory_space=pltpu.SEMAPHORE)
SPLIT_COPY = pltpu.CompilerParams(has_side_effects=pltpu.SideEffectType.DATAFLOW_SIDE_EFFECTING)


def _hbm(a):
    return pltpu.with_memory_space_constraint(a, pltpu.HBM)


def _gather_copies(srcs, lands, send_sems, recv_sems):
    x, y, c = _mesh_pos()
    me = 2 * x + y
    sends, recvs = [], []
    for w, (src, land) in enumerate(zip(srcs, lands)):
        hr = src.shape[0] // 2
        for j, (px, py) in enumerate(_other_chips(x, y)):
            k = 3 * w + j
            sends.append(_remote(src.at[pl.ds(c * hr, hr)], land.at[me, pl.ds(c * hr, hr)],
                                 send_sems.at[k], recv_sems.at[k], (px, py, c)))
            got = land.at[2 * px + py, pl.ds(c * hr, hr)]
            recvs.append(_remote(got, got, send_sems.at[k], recv_sems.at[k], (px, py, c)))
    return sends, recvs


def _scatter_copies(srcs, lands, send_sems, recv_sems):
    x, y, c = _mesh_pos()
    me = 2 * x + y
    sends, recvs = [], []
    for w, (src, land) in enumerate(zip(srcs, lands)):
        hr = src.shape[1]
        for j, (px, py) in enumerate(_other_chips(x, y)):
            k = 3 * w + j
            sends.append(_remote(src.at[2 * px + py], land.at[me, pl.ds(c * hr, hr)],
                                 send_sems.at[k], recv_sems.at[k], (px, py, c)))
            got = land.at[2 * px + py, pl.ds(c * hr, hr)]
            recvs.append(_remote(got, got, send_sems.at[k], recv_sems.at[k], (px, py, c)))
    return sends, recvs


def _sibling_copies(srcs, lands, send_sems, recv_sems):
    x, y, c = _mesh_pos()
    sibling = (x, y, 1 - c)
    sends, recvs = [], []
    for w, (src, land) in enumerate(zip(srcs, lands)):
        hr = src.shape[1] // 2
        sends.append(_remote(src.at[:, pl.ds((1 - c) * hr, hr)], land, send_sems.at[w], recv_sems.at[w], sibling))
        recvs.append(_remote(land, land, send_sems.at[w], recv_sems.at[w], sibling))
    return sends, recvs


def _inchip_copies(srcs, lands, send_sems, recv_sems):
    x, y, c = _mesh_pos()
    me = 2 * x + y
    sibling = (x, y, 1 - c)
    sends, recvs = [], []
    for w, (src, land) in enumerate(zip(srcs, lands)):
        hr = src.shape[1]
        mine, other = pl.ds(c * hr, hr), pl.ds((1 - c) * hr, hr)
        blocks = [(me, src.at[me])] + [(2 * px + py, None) for px, py in _other_chips(x, y)]
        for j, (blk, own_src) in enumerate(blocks):
            k = 4 * w + j
            landed = land.at[blk, mine]
            sends.append(_remote(landed if own_src is None else own_src, landed, send_sems.at[k], recv_sems.at[k], sibling))
            got = land.at[blk, other]
            recvs.append(_remote(got, got, send_sems.at[k], recv_sems.at[k], sibling))
    return sends, recvs


class _SemsFrom:
    def __init__(self, sems, first):
        self.sems, self.first = sems, first

    @property
    def at(self):
        return self

    def __getitem__(self, k):
        return self.sems.at[self.first + k]


def _shifted(copies_of, first):
    def copies(srcs, lands, send_sems, recv_sems):
        return copies_of(srcs, lands, _SemsFrom(send_sems, first), _SemsFrom(recv_sems, first))
    return copies


def _two_groups(copies_a, n_a, k_a, copies_b):
    shifted_b = _shifted(copies_b, k_a)

    def copies(srcs, lands, send_sems, recv_sems):
        sends_a, recvs_a = copies_a(srcs[:n_a], lands[:n_a], send_sems, recv_sems)
        sends_b, recvs_b = shifted_b(srcs[n_a:], lands[n_a:], send_sems, recv_sems)
        return sends_a + sends_b, recvs_a + recvs_b
    return copies


def _all_peers_copies(srcs, lands, send_sems, recv_sems):
    x, y, c = _mesh_pos()
    (src,), (land,) = srcs, lands
    flip = lambda v, bit: 1 - v if bit else v
    sends, recvs = [], []
    for k in range(N_DEV - 1):
        px, py, pc = flip(x, (k + 1) & 4), flip(y, (k + 1) & 2), flip(c, (k + 1) & 1)
        sends.append(_remote(src, land.at[4 * x + 2 * y + c], send_sems.at[k], recv_sems.at[k], (px, py, pc)))
        got = land.at[4 * px + 2 * py + pc]
        recvs.append(_remote(got, got, send_sems.at[k], recv_sems.at[k], (px, py, pc)))
    return sends, recvs


def _split_start(name, copies_of, srcs, land_shapes, n_copies=None):
    n = len(srcs)
    k = 3 * n if n_copies is None else n_copies

    def body(*refs):
        src_refs, land_refs = refs[:n], refs[n:2 * n]
        send_sems, recv_sems = refs[2 * n], refs[2 * n + 1]
        token = refs[-1]
        sends, _ = copies_of(src_refs, land_refs, send_sems, recv_sems)
        for cp in sends:
            cp.start()
        token[...] = jnp.zeros_like(token)

    lands = [_hbm(s) for s in land_shapes]
    dma = pltpu.SemaphoreType.DMA
    res = pl.pallas_call(
        body, name=name,
        out_shape=(dma((k,)), dma((k,)), *[pltpu.HBM(s.shape, s.dtype) for s in srcs],
                   *[pltpu.HBM(s.shape, s.dtype) for s in land_shapes], jax.ShapeDtypeStruct((8, 128), F32)),
        in_specs=[HBM] * (2 * n),
        out_specs=(SEM, SEM, *([HBM] * (2 * n)), pl.BlockSpec(memory_space=pltpu.VMEM)),
        input_output_aliases={i: 2 + i for i in range(2 * n)},
        compiler_params=SPLIT_COPY,
    )(*[_hbm(s) for s in srcs], *lands)
    return res[0], res[1], list(res[2:2 + n]), list(res[2 + n:2 + 2 * n]), res[-1]


def _split_wait(name, copies_of, send_sems, recv_sems, srcs, lands, after):
    n = len(srcs)

    def body(*refs):
        src_refs, land_refs = refs[:n], refs[n:2 * n]
        sends, recvs = copies_of(src_refs, land_refs, refs[2 * n], refs[2 * n + 1])
        for cp in sends:
            cp.wait_send()
        for cp in recvs:
            cp.wait_recv()

    res = pl.pallas_call(
        body, name=name,
        out_shape=tuple(pltpu.HBM(s.shape, s.dtype) for s in list(srcs) + list(lands)),
        in_specs=[HBM] * (2 * n) + [SEM, SEM] + [pl.BlockSpec(memory_space=pl.ANY)] * len(after),
        out_specs=tuple([HBM] * (2 * n)),
        input_output_aliases={i: i for i in range(2 * n)},
        compiler_params=SPLIT_COPY,
    )(*srcs, *lands, send_sems, recv_sems, *after)
    return list(res[:n]), list(res[n:])


def _forward_copies(srcs, lands, send_sems, recv_sems):
    x, y, c = _mesh_pos()
    sibling = (x, y, 1 - c)
    sends, recvs = [], []
    for w, land in enumerate(lands):
        hr = land.shape[1] // 2
        for j, (px, py) in enumerate(_other_chips(x, y)):
            k = 3 * w + j
            landed = land.at[2 * px + py, pl.ds(c * hr, hr)]
            sends.append(_remote(landed, landed, send_sems.at[k], recv_sems.at[k], sibling))
            other = land.at[2 * px + py, pl.ds((1 - c) * hr, hr)]
            recvs.append(_remote(other, other, send_sems.at[k], recv_sems.at[k], sibling))
    return sends, recvs


def _gather_finish(lands, n_forward):
    n = len(lands)

    def body(*refs):
        outs = refs[n:n + n_forward]
        d2d_send, d2d_recv = refs[2 * n:]
        sends, recvs = _forward_copies(None, outs, d2d_send, d2d_recv)
        for cp in sends:
            cp.start()
        for cp in recvs:
            cp.wait_recv()
        for cp in sends:
            cp.wait_send()

    dma = pltpu.SemaphoreType.DMA
    return pl.pallas_call(
        body, name="gather_finish",
        out_shape=[jax.ShapeDtypeStruct(l.shape, l.dtype) for l in lands],
        in_specs=[HBM] * n, out_specs=[HBM] * n,
        input_output_aliases={i: i for i in range(n)},
        scratch_shapes=[dma((3 * n,)), dma((3 * n,))],
    )(*lands)


def _adamw(w, g, m, v):
    m = ADAM_B1 * m + (1.0 - ADAM_B1) * g
    v = ADAM_B2 * v + (1.0 - ADAM_B2) * (g * g)
    m_hat = m / (1.0 - ADAM_B1 ** ADAM_STEP)
    v_hat = v / (1.0 - ADAM_B2 ** ADAM_STEP)
    delta = -ADAM_LR * (m_hat / (jnp.sqrt(v_hat) + ADAM_EPS) + ADAM_WD * w)
    return delta, m, v


def _adamw_big(partials, w, m, v):
    rows, cols = w.shape
    tr = _elementwise_tile(rows)

    def body(p_ref, w_ref, m_ref, v_ref, g_ref, d_ref, m2_ref, v2_ref):
        g = ((p_ref[0].astype(F32) + p_ref[1].astype(F32)) + p_ref[2].astype(F32)) + p_ref[3].astype(F32)
        g_ref[...] = g
        d_ref[...], m2_ref[...], v2_ref[...] = _adamw(w_ref[...], g, m_ref[...], v_ref[...])

    blk = pl.BlockSpec((tr, cols), lambda i: (i, 0))
    return pl.pallas_call(
        body, name="adamw_big", grid=(rows // tr,),
        in_specs=[pl.BlockSpec((N_CHIPS, tr, cols), lambda i: (0, i, 0)), blk, blk, blk],
        out_specs=[blk] * 4, out_shape=[jax.ShapeDtypeStruct((rows, cols), F32)] * 4,
        compiler_params=_params("parallel"),
    )(partials, w, m, v)


def _sum_devices(gathered, rows):
    cols = gathered.shape[1]

    def body(g_ref, o_ref):
        acc = g_ref[0:rows]
        for d in range(1, N_DEV):
            acc = acc + g_ref[d * rows:(d + 1) * rows]
        o_ref[...] = acc

    return pl.pallas_call(
        body, name="sum_devices", out_shape=jax.ShapeDtypeStruct((rows, cols), F32),
        in_specs=[pl.BlockSpec(memory_space=pltpu.VMEM)], out_specs=pl.BlockSpec(memory_space=pltpu.VMEM),
        compiler_params=pltpu.CompilerParams(vmem_limit_bytes=VMEM_LIMIT_V7X),
    )(gathered)


def _adamw_small(quads):
    n = len(quads)

    def body(*refs):
        ins, outs = refs[:4 * n], refs[4 * n:]
        for t in range(n):
            w, g, m, v = (r[...] for r in ins[4 * t:4 * t + 4])
            outs[3 * t][...], outs[3 * t + 1][...], outs[3 * t + 2][...] = _adamw(w, g, m, v)

    flat = [a for q in quads for a in q]
    vm = pl.BlockSpec(memory_space=pltpu.VMEM)
    res = pl.pallas_call(
        body, name="adamw_small",
        out_shape=[jax.ShapeDtypeStruct(q[0].shape, F32) for q in quads for _ in range(3)],
        in_specs=[vm] * (4 * n), out_specs=[vm] * (3 * n),
    )(*flat)
    return [tuple(res[3 * t:3 * t + 3]) for t in range(n)]


SMALL_PACK_ROWS = 96
META_COLS = D_MODEL // N_CHIPS
CONV_COLS = LRU_WIDTH // N_CHIPS
_WEIGHTS = ['meta_tokens', 'g_pre_mix', 'w_in', 'conv_w', 'conv_b', 'w_a', 'b_a', 'w_x', 'b_x', 'lru_lambda',
            'attn_sinks', 'w_out', 'g_post_mix', 'g_pre_ffn', 'w_ff1', 'w_ff2', 'g_post_ffn']
_BIG = ['w_in', 'w_out', 'w_ff1', 'w_ff2']


def _pack_small(dmeta, g, loss):
    z = lambda r, c: jnp.zeros((r, c), F32)
    rows = [
        dmeta,
        g['g_pre_mix'], g['g_post_mix'], g['g_pre_ffn'], g['g_post_ffn'],
        jnp.concatenate([g['conv_w'], z(4, 512)], axis=1),
        jnp.concatenate([g['conv_b'], g['b_a']], axis=1),
        jnp.concatenate([g['b_x'], g['lru_lambda']], axis=1),
        jnp.concatenate([g['attn_sinks'], z(1, D_MODEL - ATTN_HEADS)], axis=1),
        jnp.concatenate([loss, z(1, D_MODEL - 1)], axis=1),
        z(4, D_MODEL),
        g['w_a'].reshape(32, D_MODEL), g['w_x'].reshape(32, D_MODEL),
    ]
    return jnp.concatenate(rows, axis=0)


def _unpack_small(s, chip):
    return dict(
        meta_tokens=lax.dynamic_slice(s[0:N_META], (0, chip * META_COLS), (N_META, META_COLS)),
        g_pre_mix=s[16:17], g_post_mix=s[17:18], g_pre_ffn=s[18:19], g_post_ffn=s[19:20],
        conv_w=lax.dynamic_slice(s[20:24], (0, chip * CONV_COLS), (4, CONV_COLS)).reshape(1, 4, CONV_COLS),
        conv_b=s[24:25, :512], b_a=s[24:25, 512:], b_x=s[25:26, :512], lru_lambda=s[25:26, 512:],
        attn_sinks=s[26:27, :ATTN_HEADS], loss=s[27, 0],
        w_a=s[32:64].reshape(1, LRU_BLOCKS, LRU_BLOCK, LRU_BLOCK),
        w_x=s[64:96].reshape(1, LRU_BLOCKS, LRU_BLOCK, LRU_BLOCK))


def _as2d(a):
    if a.ndim == 2:
        return a
    return a.reshape(-1, a.shape[-1])


def kernel(x, meta_tokens, g_pre_mix, w_in, conv_w, conv_b, w_a, b_a, w_x, b_x, lru_lambda, attn_sinks, w_out, g_post_mix, g_pre_ffn, w_ff1, w_ff2, g_post_ffn, loss_target, m_meta_tokens, m_g_pre_mix, m_w_in, m_conv_w, m_conv_b, m_w_a, m_b_a, m_w_x, m_b_x, m_lru_lambda, m_attn_sinks, m_w_out, m_g_post_mix, m_g_pre_ffn, m_w_ff1, m_w_ff2, m_g_post_ffn, v_meta_tokens, v_g_pre_mix, v_w_in, v_conv_w, v_conv_b, v_w_a, v_b_a, v_w_x, v_b_x, v_lru_lambda, v_attn_sinks, v_w_out, v_g_post_mix, v_g_pre_ffn, v_w_ff1, v_w_ff2, v_g_post_ffn):
    weights = dict(meta_tokens=meta_tokens, g_pre_mix=g_pre_mix, w_in=w_in, conv_w=conv_w, conv_b=conv_b, w_a=w_a,
                   b_a=b_a, w_x=w_x, b_x=b_x, lru_lambda=lru_lambda, attn_sinks=attn_sinks, w_out=w_out,
                   g_post_mix=g_post_mix, g_pre_ffn=g_pre_ffn, w_ff1=w_ff1, w_ff2=w_ff2, g_post_ffn=g_post_ffn)
    mom1 = dict(zip(_WEIGHTS, [m_meta_tokens, m_g_pre_mix, m_w_in, m_conv_w, m_conv_b, m_w_a, m_b_a, m_w_x, m_b_x,
                               m_lru_lambda, m_attn_sinks, m_w_out, m_g_post_mix, m_g_pre_ffn, m_w_ff1, m_w_ff2,
                               m_g_post_ffn]))
    mom2 = dict(zip(_WEIGHTS, [v_meta_tokens, v_g_pre_mix, v_w_in, v_conv_w, v_conv_b, v_w_a, v_b_a, v_w_x, v_b_x,
                               v_lru_lambda, v_attn_sinks, v_w_out, v_g_post_mix, v_g_pre_ffn, v_w_ff1, v_w_ff2,
                               v_g_post_ffn]))
    xi, yi, ci = _mesh_pos()
    chip = 2 * xi + yi

    tiny = jnp.concatenate([meta_tokens, jnp.pad(conv_w[0], ((0, 4), (0, 128)))], axis=0)
    chip_arr = jnp.reshape(chip, (1,)).astype(jnp.int32)
    big2d = lambda a, name: a[0].T if name == 'w_in' else a[0]
    shards, lands = zip(*[_prep_shard(big2d(weights[n], n), chip_arr) for n in _BIG])
    g_in, g_tiny = _gather_weights(shards[:1], lands[:1], tiny, _prep_tiny(tiny, chip_arr))
    w_in_full = g_in.reshape(IN_WIDTH, D_MODEL)
    meta_full = jnp.concatenate([g_tiny[j, :N_META] for j in range(N_CHIPS)], axis=1)
    conv_w_full = jnp.concatenate([g_tiny[j, N_META:N_META + 4, :128] for j in range(N_CHIPS)], axis=1)
    g_send, g_recv, late_thru, late_lands, token = _split_start(
        "gather_late_start", _gather_copies, shards[1:], lands[1:])

    def late_weights(after):
        thru, landed = _split_wait("gather_late_wait", _gather_copies, g_send, g_recv, late_thru, late_lands, after)
        f_send, f_recv, f_thru, f_lands, _ = _split_start("gather_forward_start", _forward_copies, thru, landed)
        _, (g_out,) = _split_wait("gather_out_wait", _forward_copies, f_send, f_recv, f_thru[:1], f_lands[:1], [])

        def ffn_weights(after):
            _, (g_f1, g_f2) = _split_wait("gather_forward_wait", _shifted(_forward_copies, 3), f_send, f_recv,
                                          f_thru[1:], f_lands[1:], after)
            return g_f1, g_f2

        return g_out.reshape(D_MODEL, D_MODEL), ffn_weights

    pos = jnp.stack([ci, chip]).astype(jnp.int32)
    ffn = {}


    def on_ffn_grads(dw1, dw2):
        parts = [dw1, dw2]
        lands = [lax.empty((p.shape[0], p.shape[1] // 2, p.shape[2]), p.dtype) for p in parts]
        ffn['sib'] = _split_start("sibling_ffn_start", _sibling_copies, parts, lands, len(parts))
        return ffn['sib'][4]

    def on_outproj_bwd(dattn, dw_out):
        send, recv, thru, lands, _ = ffn['sib']
        parts, from_sibling = _split_wait("sibling_ffn_wait", _sibling_copies, send, recv, thru, lands, [dattn])
        cparts_ffn, lands_ffn = zip(*[_chip_presum(p, r, pos) for p, r in zip(parts, from_sibling)])
        part_out = dw_out.reshape(N_CHIPS, D_MODEL // N_CHIPS, D_MODEL)
        land_out = lax.empty((N_CHIPS, D_MODEL // N_CHIPS // 2, D_MODEL), part_out.dtype)
        n_ici = 3 * len(cparts_ffn)
        ffn['send'], ffn['recv'], thru, lands, token3 = _split_start(
            "scatter_ffn_start", _two_groups(_scatter_copies, len(cparts_ffn), n_ici, _sibling_copies),
            list(cparts_ffn) + [part_out], list(lands_ffn) + [land_out], n_ici + 1)
        ffn['thru'], ffn['lands'], ffn['out'] = thru[:-1], lands[:-1], (thru[-1], lands[-1], n_ici)
        return token3

    def on_mixer_grads(dw_in, dw_out):
        part_in = dw_in.reshape(N_CHIPS, IN_WIDTH // N_CHIPS, D_MODEL)
        (sib_in,) = _sibling_exchange([part_in], pos)
        part_out, land_out, first = ffn['out']
        (part_out,), (sib_out,) = _split_wait("sibling_out_wait", _shifted(_sibling_copies, first), ffn['send'],
                                              ffn['recv'], [part_out], [land_out], [sib_in])
        parts, from_sibling = [part_in, part_out], [sib_in, sib_out]
        cparts, lands = zip(*[_chip_presum(p, r, pos) for p, r in zip(parts, from_sibling)])
        ffn_cparts, ffn_lands = _split_wait("scatter_ffn_wait", _scatter_copies, ffn['send'], ffn['recv'],
                                            ffn['thru'], ffn['lands'], list(cparts))
        n_ici = 3 * len(cparts)
        ffn['n_ici'] = n_ici
        ffn['mixer'] = _split_start("scatter_mixer_start", _two_groups(_scatter_copies, len(cparts), n_ici, _inchip_copies),
                                    list(cparts) + ffn_cparts, list(lands) + ffn_lands, n_ici + 4 * len(ffn_cparts))
        return ffn['mixer'][4]

    head = jnp.concatenate([jnp.zeros((PAD_ROWS, D_MODEL), F32), meta_full], axis=0)
    loss, dx, dhead, grads = _local_step(head, x[0], loss_target[0], g_pre_mix, w_in_full, conv_w_full, conv_b, w_a[0],
                                         b_a, w_x[0], b_x, lru_lambda, attn_sinks, g_post_mix, g_pre_ffn, g_post_ffn,
                                         late_weights, on_ffn_grads, on_outproj_bwd, on_mixer_grads, token)
    grad_x = dx[None]

    pack = _pack_small(dhead[PAD_ROWS:], grads, loss)
    dev = jnp.reshape(4 * xi + 2 * yi + ci, (1,)).astype(jnp.int32)
    send, recv, thru, lands, _ = ffn['mixer']
    nm = len(thru) // 2
    mixer_cparts, mixer_lands = _split_wait("scatter_mixer_wait", _scatter_copies, send, recv, thru[:nm], lands[:nm],
                                            [pack])
    n_small = N_DEV - 1
    t_send, t_recv, t_thru, t_lands, token6 = _split_start(
        "gather_small_start", _two_groups(_all_peers_copies, 1, n_small, _inchip_copies), [pack] + mixer_cparts,
        [_prep_tiny(pack, dev, N_DEV)] + mixer_lands, n_small + 4 * len(mixer_cparts))
    _, ffn_partials = _split_wait("inchip_ffn_wait", _shifted(_inchip_copies, ffn['n_ici']), send, recv, thru[nm:],
                                  lands[nm:], [token6])

    g_out_d, delta, new_m, new_v = {}, {}, {}, {}

    def adamw_big(names, partials):
        for name, part in zip(names, partials):
            shp = weights[name].shape
            res = _adamw_big(part, big2d(weights[name], name), big2d(mom1[name], name), big2d(mom2[name], name))
            g_out_d[name], delta[name], new_m[name], new_v[name] = (big2d(r[None], name).reshape(shp) for r in res)

    adamw_big(_BIG[2:], ffn_partials)
    _, mixer_partials = _split_wait("inchip_mixer_wait", _shifted(_inchip_copies, n_small), t_send, t_recv, t_thru[1:],
                                    t_lands[1:], [g_out_d[n] for n in _BIG[2:]])
    adamw_big(_BIG[:2], mixer_partials)

    _, (gathered,) = _split_wait("gather_small_wait", _all_peers_copies, t_send, t_recv, t_thru[:1], t_lands[:1],
                                 [g_out_d[n] for n in _BIG])
    small = _unpack_small(_sum_devices(gathered.reshape(N_DEV * SMALL_PACK_ROWS, D_MODEL), SMALL_PACK_ROWS), chip)
    loss = small['loss']
    small_names = [n for n in _WEIGHTS if n not in _BIG]
    quads = [(_as2d(weights[n]), _as2d(small[n]), _as2d(mom1[n]), _as2d(mom2[n])) for n in small_names]
    for name, (d, m2, v2) in zip(small_names, _adamw_small(quads)):
        shp = weights[name].shape
        g_out_d[name] = small[name].reshape(shp)
        delta[name], new_m[name], new_v[name] = d.reshape(shp), m2.reshape(shp), v2.reshape(shp)

    return (loss, grad_x, *[g_out_d[n] for n in _WEIGHTS], *[delta[n] for n in _WEIGHTS],
            *[new_m[n] for n in _WEIGHTS], *[new_v[n] for n in _WEIGHTS])
```

```python
import numpy as np
import jax
import jax.numpy as jnp
from jax import lax
from jax.experimental import pallas as pl
from jax.experimental.pallas import tpu as pltpu

F32 = jnp.float32
BF16 = jnp.bfloat16

D_MODEL = 1024
N_META = 16
BLOCK = 128
PAD_ROWS = BLOCK - N_META
HEAD_DIM = 64
ATTN_HEADS = 8
GQA_GROUP = 4
ATTN_WIDTH = 512
KV_WIDTH = 128
QKV_WIDTH = ATTN_WIDTH + 2 * KV_WIDTH
LRU_WIDTH = 512
LRU_BLOCKS = 8
LRU_BLOCK = 64
LRU_C = 8.0
IN_WIDTH = 1792
D_FF = 4096
N_CHIPS = 4
FF_CHUNK = D_FF // N_CHIPS
EPS = 1e-6
NEG = -1e30

ADAM_LR = 0.001
ADAM_B1 = 0.9
ADAM_B2 = 0.999
ADAM_EPS = 1e-08
ADAM_WD = 0.01
ADAM_STEP = 10

VMEM_LIMIT_V7X = 62 * 1024 * 1024
MESH = pl.DeviceIdType.MESH

NT = (((1,), (1,)), ((), ()))
TN = (((0,), (0,)), ((), ()))


def _row_tile(tp):
    return 640 if tp % 640 == 0 else BLOCK


def _elementwise_tile(rows):
    return 512 if rows % 512 == 0 else rows


def _wgrad_row_tile(tp):
    return 1664 if tp % 1664 == 0 else _row_tile(tp)


def _params(*sem):
    return pltpu.CompilerParams(dimension_semantics=sem, vmem_limit_bytes=VMEM_LIMIT_V7X)


def _dot(a, b):
    return jnp.dot(a, b, preferred_element_type=F32)


def _dot_nt(a, b):
    return lax.dot_general(a, b, NT, preferred_element_type=F32)


def _dot_tn(a, b):
    return lax.dot_general(a, b, TN, preferred_element_type=F32)


def _rms(x):
    rs = lax.rsqrt(jnp.mean(x * x, axis=-1, keepdims=True) + EPS)
    return x * rs, rs


def _rms_bwd(xhat, rs, g, dy):
    dyg = dy * g
    dx = rs * (dyg - xhat * jnp.mean(dyg * xhat, axis=-1, keepdims=True))
    dg = jnp.sum(dy * xhat, axis=0, keepdims=True)
    return dx, dg


def _gelu(x):
    k = 0.7978845608028654
    t = jnp.tanh(x * (k + (k * 0.044715) * (x * x)))
    return (0.5 * x) * (1.0 + t), t


def _gelu_grad(x, t):
    k = 0.7978845608028654
    return 0.5 * (1.0 + t) + 0.5 * x * (1.0 - t * t) * k * (1.0 + 3 * 0.044715 * x * x)


def _sigmoid(x):
    return 0.5 * jnp.tanh(0.5 * x) + 0.5


def _one_minus_exp2(y):
    t = jnp.tanh(y)
    return (-2.0 * t) / (1.0 - t)


def _softplus(x):
    return jnp.maximum(x, 0.0) + jnp.log1p(jnp.exp(-jnp.abs(x)))


def _seq_specs(tr, delay=0):
    qb = tr // BLOCK
    tile = lambda i: jnp.maximum(i - delay, 0)
    return [pl.BlockSpec((BLOCK, D_MODEL), lambda i, *_, s=s: (jnp.maximum(tile(i) * qb + s - 1, 0), 0))
            for s in range(qb)]


def _seq_tile(head, pieces, i):
    first = jnp.where(i == 0, head, pieces[0][...])
    return jnp.concatenate([first] + [p[...] for p in pieces[1:]], axis=0)


GROUP_ROWS = GQA_GROUP * BLOCK


def _attn_bias():
    j = np.arange(2 * BLOCK)[:, None]
    i = np.arange(BLOCK)[None, :]
    band = (j - i >= 1) & (j - i <= BLOCK)
    out = []
    for n in range(3):
        ok = band & ((n - 1) * BLOCK + j >= PAD_ROWS) if n < 2 else band
        out.append(np.tile(np.where(ok, 0.0, NEG).astype(np.float32), (1, GQA_GROUP)))
    return jnp.asarray(np.stack(out))


def _heads_t(at, g):
    heads = range(GQA_GROUP * g, GQA_GROUP * (g + 1))
    return jnp.concatenate([at[h * HEAD_DIM:(h + 1) * HEAD_DIM] for h in heads], axis=1).astype(BF16)


def _from_heads_t(groups):
    pairs = []
    for p in groups:
        for h in range(0, GQA_GROUP, 2):
            two = jnp.concatenate([p[:, h * BLOCK:(h + 1) * BLOCK], p[:, (h + 1) * BLOCK:(h + 2) * BLOCK]], axis=0)
            pairs.append(two.T)
    return jnp.concatenate(pairs, axis=1)


def _stack_heads(a, g):
    heads = range(GQA_GROUP * g, GQA_GROUP * (g + 1))
    return jnp.concatenate([a[:, h * HEAD_DIM:(h + 1) * HEAD_DIM] for h in heads], axis=0)


def _unstack_heads(groups):
    return jnp.concatenate([p[h * BLOCK:(h + 1) * BLOCK] for p in groups for h in range(GQA_GROUP)], axis=1)


def _attn_probs_t(k_g, qg, bias, sink_row):
    st = _dot_nt(k_g, qg) + bias
    m = jnp.maximum(jnp.max(st, axis=0, keepdims=True), sink_row)
    p = jnp.exp(st - m)
    es = jnp.exp(sink_row - m)
    inv = 1.0 / (jnp.sum(p, axis=0, keepdims=True) + es)
    return p * inv, es * inv


def _attn_consts(sinks):
    return jnp.repeat(sinks.reshape(ATTN_HEADS), BLOCK).reshape(ATTN_HEADS // GQA_GROUP, GROUP_ROWS), _attn_bias()


_SINK_SPEC = pl.BlockSpec((ATTN_HEADS // GQA_GROUP, GROUP_ROWS), lambda n: (0, 0))
_BIAS_SPEC = pl.BlockSpec((3, 2 * BLOCK, GROUP_ROWS), lambda n: (0, 0, 0))
_QSCALE = HEAD_DIM ** -0.5


def _kv_specs(tr):
    qb = tr // BLOCK
    prev = lambda col: pl.BlockSpec((BLOCK, KV_WIDTH), lambda t: (jnp.maximum(t * qb - 1, 0), col))
    cur = lambda col: pl.BlockSpec((tr, KV_WIDTH), lambda t: (t, col))
    return [prev(4), cur(4), prev(5), cur(5)]


def _block_bias(b_ref, t, qb, i):
    return b_ref[2] if i >= 2 else b_ref[jnp.minimum(t * qb + i, 2)]


N_KV = ATTN_HEADS // GQA_GROUP


def _prob_specs(qb):
    return [pl.BlockSpec((qb, N_KV, 2 * BLOCK, GROUP_ROWS), lambda t: (t, 0, 0, 0)),
            pl.BlockSpec((qb, SUBLANES, GROUP_ROWS), lambda t: (t, 0, 0))]


def _attn_fwd(qkv, sinks):
    tp = qkv.shape[0]
    tr = _row_tile(tp)
    qb, nb = tr // BLOCK, tp // BLOCK
    sink_rows, bias = _attn_consts(sinks)

    def body(s_ref, b_ref, q_ref, kp_ref, kc_ref, vp_ref, vc_ref, o_ref, p_ref, ps_ref):
        t = pl.program_id(0)
        k_all = jnp.concatenate([kp_ref[...], kc_ref[...]], axis=0)
        v_all = jnp.concatenate([vp_ref[...], vc_ref[...]], axis=0)
        for i in range(qb):
            rows = slice(i * BLOCK, (i + 1) * BLOCK)
            q = q_ref[rows]
            k2, v2 = k_all[i * BLOCK:(i + 2) * BLOCK], v_all[i * BLOCK:(i + 2) * BLOCK]
            bias_n = _block_bias(b_ref, t, qb, i)
            outs, sink_probs = [], []
            for g in range(N_KV):
                cols = slice(g * HEAD_DIM, (g + 1) * HEAD_DIM)
                qg = _stack_heads(q, g) * jnp.asarray(_QSCALE, BF16)
                p, ps = _attn_probs_t(k2[:, cols], qg, bias_n, s_ref[g:g + 1])
                pb = p.astype(BF16)
                p_ref[i, g] = pb
                sink_probs.append(ps)
                outs.append(_dot_tn(pb, v2[:, cols]))
            o_ref[rows] = _unstack_heads(outs).astype(BF16)
            ps_ref[i] = jnp.concatenate(sink_probs + [jnp.zeros((SUBLANES - N_KV, GROUP_ROWS), F32)], axis=0)

    return pl.pallas_call(
        body, name="attn_fwd", grid=(tp // tr,),
        in_specs=[_SINK_SPEC, _BIAS_SPEC, pl.BlockSpec((tr, ATTN_WIDTH), lambda t: (t, 0))] + _kv_specs(tr),
        out_specs=[pl.BlockSpec((tr, ATTN_WIDTH), lambda t: (t, 0))] + _prob_specs(qb),
        out_shape=[jax.ShapeDtypeStruct((tp, ATTN_WIDTH), BF16),
                   jax.ShapeDtypeStruct((nb, N_KV, 2 * BLOCK, GROUP_ROWS), BF16),
                   jax.ShapeDtypeStruct((nb, SUBLANES, GROUP_ROWS), F32)],
        compiler_params=_params("parallel"),
    )(sink_rows, bias, qkv, qkv, qkv, qkv, qkv)


def _conv_taps(x, halo):
    ext = jnp.concatenate([halo, x], axis=0)
    return [ext[8:] if k == 3 else pltpu.roll(ext, 3 - k, 0)[8:] for k in range(4)]


def _lru_gates(xc, wa, ba, wx, bx, sp):
    xb = xc.astype(BF16)
    r = _sigmoid(_dot(xb, wa) + ba)
    ig = _sigmoid(_dot(xb, wx) + bx)
    log_a = (-LRU_C * sp) * r
    a = jnp.exp(log_a)
    mult = jnp.sqrt(_one_minus_exp2(log_a))
    return xb, r, ig, a, mult


SUBLANES = 8


def _scan_fwd(a, b, h_in):
    n, width = a.shape
    a, b = (v.reshape(n // SUBLANES, SUBLANES, width) for v in (a, b))
    in_group = lax.broadcasted_iota(jnp.int32, a.shape, 1)
    for d in (1, 2, 4):
        keep = in_group >= d
        b = jnp.where(keep, a * pltpu.roll(b, d, 1) + b, b)
        a = jnp.where(keep, a * pltpu.roll(a, d, 1), a)
    a, b = a.reshape(n, width), b.reshape(n, width)
    out, carry = [], h_in
    for g in range(0, n, SUBLANES):
        h = a[g:g + SUBLANES] * carry + b[g:g + SUBLANES]
        out.append(h)
        carry = h[SUBLANES - 1:]
    return jnp.concatenate(out, axis=0)


def _scan_rev(c, b, g_in):
    n, width = c.shape
    c, b = (v.reshape(n // SUBLANES, SUBLANES, width) for v in (c, b))
    in_group = lax.broadcasted_iota(jnp.int32, c.shape, 1)
    for d in (1, 2, 4):
        keep = in_group < SUBLANES - d
        b = jnp.where(keep, b + c * pltpu.roll(b, SUBLANES - d, 1), b)
        c = jnp.where(keep, c * pltpu.roll(c, SUBLANES - d, 1), c)
    c, b = c.reshape(n, width), b.reshape(n, width)
    out, carry = [], g_in
    for g in range(n - SUBLANES, -1, -SUBLANES):
        r = b[g:g + SUBLANES] + c[g:g + SUBLANES] * carry
        out.append(r)
        carry = r[:1]
    return jnp.concatenate(out[::-1], axis=0)


def _inproj_lru_fwd(head, x, g, w_in, conv_w, conv_b, wa, ba, wx, bx, lam, token):
    tp = BLOCK + x.shape[0]
    tr = _row_tile(tp)
    qb, nt = tr // BLOCK, tp // tr
    small = [conv_w, conv_b, wa, ba, wx, bx, lam]

    def body(*refs):
        head_ref, pieces = refs[0], refs[1:1 + qb]
        g_ref, w_ref, _, cw_ref, cb_ref, wa_ref, ba_ref, wx_ref, bx_ref, lam_ref = refs[1 + qb:11 + qb]
        u_ref, qkv_ref, xr_ref, yr_ref, hr_ref, rec_ref, zbuf, halo, hprev = refs[11 + qb:]
        i = pl.program_id(0)
        cur = i % 2

        @pl.when(i == 0)
        def _():
            halo[...] = jnp.zeros_like(halo)
            hprev[...] = jnp.zeros_like(hprev)
            zbuf[1] = jnp.zeros((tr, 2 * LRU_WIDTH), F32)

        def recurrent_branch(valid):
            cw, cb = cw_ref[...], cb_ref[...]
            wa_m, ba_v, wx_m, bx_v = wa_ref[...], ba_ref[...], wx_ref[...], bx_ref[...]
            sp = _softplus(-lam_ref[...])
            before, h_last = halo[...], hprev[0:1]
            for b in range(qb):
                rows = slice(b * BLOCK, (b + 1) * BLOCK)
                xy = zbuf[1 - cur, rows]
                xin = xy[:, :LRU_WIDTH]
                taps = _conv_taps(xin, before)
                before = xin[BLOCK - 8:]
                xc = cb + sum(cw[k:k + 1] * taps[k] for k in range(4))
                _, _, ig, a, mult = _lru_gates(xc, wa_m, ba_v, wx_m, bx_v, sp)
                u = mult * (ig * xc)
                if b == 0:
                    pos = (i - 1) * tr + lax.broadcasted_iota(jnp.int32, xc.shape, 0)
                    u = jnp.where(pos >= PAD_ROWS, u, 0.0)
                h = _scan_fwd(a, u, h_last)
                h_last = h[BLOCK - 1:]
                hr_ref[rows] = h
                gl, _ = _gelu(xy[:, LRU_WIDTH:])
                rec_ref[rows] = (gl * h).astype(BF16)
            halo[...] = jnp.where(valid, before, 0.0)
            hprev[0:1] = jnp.where(valid, h_last, 0.0)

        def projection():
            xhat, _ = _rms(_seq_tile(head_ref[...], pieces, i))
            u = (xhat * g_ref[...]).astype(BF16)
            u_ref[...] = u
            z = _dot_nt(u, w_ref[...])
            qkv_ref[...] = z[:, :QKV_WIDTH].astype(BF16)
            xr_ref[...] = z[:, QKV_WIDTH:QKV_WIDTH + LRU_WIDTH]
            yr_ref[...] = z[:, QKV_WIDTH + LRU_WIDTH:]
            zbuf[cur] = z[:, QKV_WIDTH:]

        @pl.when(i < nt)
        def _():
            recurrent_branch(i >= 1)
            projection()

        @pl.when(i == nt)
        def _():
            recurrent_branch(True)

    last = nt - 1
    this_row = lambda w: pl.BlockSpec((tr, w), lambda i: (jnp.minimum(i, last), 0))
    prev_row = lambda w: pl.BlockSpec((tr, w), lambda i: (jnp.maximum(i - 1, 0), 0))
    full = lambda a: pl.BlockSpec(a.shape, lambda i: (0,) * a.ndim)
    piece_specs = [pl.BlockSpec((BLOCK, D_MODEL), lambda i, s=s: (jnp.maximum(jnp.minimum(i, last) * qb + s - 1, 0), 0))
                   for s in range(qb)]
    return pl.pallas_call(
        body, name="inproj_lru_fwd", grid=(nt + 1,),
        in_specs=[full(head)] + piece_specs + [full(g), full(w_in), full(token)] + [full(a) for a in small],
        out_specs=[this_row(D_MODEL), this_row(QKV_WIDTH), this_row(LRU_WIDTH), this_row(LRU_WIDTH),
                   prev_row(LRU_WIDTH), prev_row(LRU_WIDTH)],
        out_shape=[jax.ShapeDtypeStruct((tp, D_MODEL), BF16), jax.ShapeDtypeStruct((tp, QKV_WIDTH), BF16),
                   jax.ShapeDtypeStruct((tp, LRU_WIDTH), F32), jax.ShapeDtypeStruct((tp, LRU_WIDTH), F32),
                   jax.ShapeDtypeStruct((tp, LRU_WIDTH), F32), jax.ShapeDtypeStruct((tp, LRU_WIDTH), BF16)],
        scratch_shapes=[pltpu.VMEM((2, tr, 2 * LRU_WIDTH), F32), pltpu.VMEM((8, LRU_WIDTH), F32),
                        pltpu.VMEM((8, LRU_WIDTH), F32)],
        compiler_params=_params("arbitrary"),
    )(head, *([x] * qb), g, w_in, token, *small)


def _outproj_fwd(attn, rec, w_out, head, x, g_post_mix, g_pre_ffn):
    tp = attn.shape[0]
    tr = _row_tile(tp)
    qb = tr // BLOCK

    def body(*refs):
        a_ref, r_ref, w_ref, head_ref = refs[:4]
        pieces = refs[4:4 + qb]
        gm_ref, gf_ref, mix_ref, h1_ref, u1_ref = refs[4 + qb:]
        mix = _dot(a_ref[...], w_ref[:ATTN_WIDTH]) + _dot(r_ref[...], w_ref[ATTN_WIDTH:])
        mix_ref[...] = mix
        mhat, _ = _rms(mix)
        h1 = _seq_tile(head_ref[...], pieces, pl.program_id(0)) + mhat * gm_ref[...]
        h1_ref[...] = h1
        hhat, _ = _rms(h1)
        u1_ref[...] = (hhat * gf_ref[...]).astype(BF16)

    row = lambda w: pl.BlockSpec((tr, w), lambda i: (i, 0))
    full = lambda a: pl.BlockSpec(a.shape, lambda i: (0,) * a.ndim)
    return pl.pallas_call(
        body, name="outproj_fwd", grid=(tp // tr,),
        in_specs=[row(ATTN_WIDTH), row(LRU_WIDTH), full(w_out), full(head)] + _seq_specs(tr)
        + [full(g_post_mix), full(g_pre_ffn)],
        out_specs=[row(D_MODEL), row(D_MODEL), row(D_MODEL)],
        out_shape=[jax.ShapeDtypeStruct((tp, D_MODEL), F32), jax.ShapeDtypeStruct((tp, D_MODEL), F32),
                   jax.ShapeDtypeStruct((tp, D_MODEL), BF16)],
        compiler_params=_params("parallel"),
    )(attn, rec, w_out, head, *([x] * qb), g_post_mix, g_pre_ffn)


FFN_STEPS = N_CHIPS


def _resident(a):
    return pl.BlockSpec(a.shape, lambda *_: (0,) * a.ndim, pipeline_mode=pl.Buffered(1))


def _ffn_fwd(u1, w1, w2, h1, tgt, g_post_ffn):
    tp = h1.shape[0]
    tr = _row_tile(tp)
    qb, nt = tr // BLOCK, tp // tr
    sr = tr // FFN_STEPS

    def body(*refs):
        u_ref, w1_ref, w2_ref, h1_ref = refs[:4]
        t_pieces = refs[4:4 + qb]
        g_ref, r1_ref, dy_ref, df2_ref, loss_ref, dg_ref, acc = refs[4 + qb:]
        i, c = pl.program_id(0), pl.program_id(1)
        cur = i % 2

        @pl.when((i == 0) & (c == 0))
        def _():
            loss_ref[...] = jnp.zeros_like(loss_ref)
            dg_ref[...] = jnp.zeros_like(dg_ref)
            acc[1] = jnp.zeros((tr, D_MODEL), F32)

        def matmuls():
            r = jnp.maximum(_dot(u_ref[...], w1_ref[c]), 0.0)
            r1_ref[...] = r.astype(BF16)
            return _dot((r * r).astype(BF16), w2_ref[c])

        def finish_previous_tile(k, valid):
            lo, hi = k * sr, (k + 1) * sr
            g = g_ref[...]
            fhat, rs = _rms(acc[1 - cur, lo:hi])
            h2 = h1_ref[...] + fhat * g
            rows = (i - 1) * tr + lo + lax.broadcasted_iota(jnp.int32, h2.shape, 0)
            tgt = jnp.concatenate([p[max(lo - s * BLOCK, 0):min(hi - s * BLOCK, BLOCK)] for s, p in enumerate(t_pieces)
                                   if lo < (s + 1) * BLOCK and hi > s * BLOCK], axis=0)
            err = jnp.where((rows >= BLOCK) & valid, h2 - tgt, 0.0)
            dy = err * (1.0 / D_MODEL)
            dy_ref[...] = dy
            loss_ref[...] += (0.5 / D_MODEL) * jnp.sum(err * err)
            df2, dg = _rms_bwd(fhat, rs, g, dy)
            df2_ref[...] = df2.astype(BF16)
            dg_ref[...] += dg

        for k in range(FFN_STEPS):
            @pl.when((c == k) & (i < nt))
            def _(k=k):
                finish_previous_tile(k, i >= 1)
                if k == 0:
                    acc[cur] = matmuls()
                else:
                    acc[cur] += matmuls()

            @pl.when((c == k) & (i == nt))
            def _(k=k):
                finish_previous_tile(k, True)

    last = nt - 1
    this_row = pl.BlockSpec((tr, D_MODEL), lambda i, c: (jnp.minimum(i, last), 0))
    prev_quarter = pl.BlockSpec((sr, D_MODEL), lambda i, c: (jnp.maximum(i - 1, 0) * FFN_STEPS + c, 0))
    prev_quarter_out = pl.BlockSpec(
        (sr, D_MODEL), lambda i, c: (jnp.where(i == 0, nt * FFN_STEPS, (i - 1) * FFN_STEPS + c), 0))
    full = lambda a: pl.BlockSpec(a.shape, lambda i, c: (0,) * a.ndim)
    return pl.pallas_call(
        body, name="ffn_fwd", grid=(nt + 1, FFN_STEPS),
        in_specs=[this_row, _resident(w1), _resident(w2), prev_quarter] + _seq_specs(tr, delay=1) + [full(g_post_ffn)],
        out_specs=[pl.BlockSpec((tr, FF_CHUNK), lambda i, c: (jnp.minimum(i, last), jnp.where(i < nt, c, FFN_STEPS - 1))),
                   prev_quarter_out, prev_quarter_out,
                   pl.BlockSpec((1, 1), lambda i, c: (0, 0)), pl.BlockSpec((1, D_MODEL), lambda i, c: (0, 0))],
        out_shape=[jax.ShapeDtypeStruct((tp, D_FF), BF16), jax.ShapeDtypeStruct((tp + sr, D_MODEL), F32),
                   jax.ShapeDtypeStruct((tp + sr, D_MODEL), BF16), jax.ShapeDtypeStruct((1, 1), F32),
                   jax.ShapeDtypeStruct((1, D_MODEL), F32)],
        scratch_shapes=[pltpu.VMEM((2, tr, D_MODEL), F32)],
        compiler_params=_params("arbitrary", "arbitrary"),
    )(u1, w1, w2, h1, *([tgt] * qb), g_post_ffn)


def _ffn_bwd_data(df2, r1, w1, w2, dy, h1, mix, g_pre_ffn, g_post_mix):
    tp = h1.shape[0]
    tr = _row_tile(tp)
    nt = tp // tr
    sr = tr // FFN_STEPS

    def body(df2_ref, r1_ref, w1_ref, w2_ref, dy_ref, h1_ref, mix_ref, gf_ref, gm_ref,
             da_ref, dh1_ref, dmix_ref, dgf_ref, dgm_ref, acc):
        i, c = pl.program_id(0), pl.program_id(1)
        cur = i % 2

        @pl.when((i == 0) & (c == 0))
        def _():
            dgf_ref[...] = jnp.zeros_like(dgf_ref)
            dgm_ref[...] = jnp.zeros_like(dgm_ref)
            acc[1] = jnp.zeros((tr, D_MODEL), F32)

        def matmuls():
            df = _dot_nt(df2_ref[...], w2_ref[c])
            da = (df * (2.0 * r1_ref[...].astype(F32))).astype(BF16)
            da_ref[...] = da
            return _dot_nt(da, w1_ref[c])

        def finish_previous_tile(k, valid):
            lo, hi = k * sr, (k + 1) * sr
            hhat, rs = _rms(h1_ref[...])
            dx, dgf = _rms_bwd(hhat, rs, gf_ref[...], acc[1 - cur, lo:hi])
            dh1 = dy_ref[...] + dx
            dh1_ref[...] = dh1
            mhat, rsm = _rms(mix_ref[...])
            dmix, dgm = _rms_bwd(mhat, rsm, gm_ref[...], dh1)
            dmix_ref[...] = dmix.astype(BF16)
            dgf_ref[...] += jnp.where(valid, dgf, 0.0)
            dgm_ref[...] += jnp.where(valid, dgm, 0.0)

        for k in range(FFN_STEPS):
            @pl.when((c == k) & (i < nt))
            def _(k=k):
                finish_previous_tile(k, i >= 1)
                if k == 0:
                    acc[cur] = matmuls()
                else:
                    acc[cur] += matmuls()

            @pl.when((c == k) & (i == nt))
            def _(k=k):
                finish_previous_tile(k, True)

    last = nt - 1
    this_row = pl.BlockSpec((tr, D_MODEL), lambda i, c: (jnp.minimum(i, last), 0))
    prev_quarter = pl.BlockSpec((sr, D_MODEL), lambda i, c: (jnp.maximum(i - 1, 0) * FFN_STEPS + c, 0))
    prev_quarter_out = pl.BlockSpec(
        (sr, D_MODEL), lambda i, c: (jnp.where(i == 0, nt * FFN_STEPS, (i - 1) * FFN_STEPS + c), 0))
    chunk = pl.BlockSpec((tr, FF_CHUNK), lambda i, c: (jnp.minimum(i, last), jnp.where(i < nt, c, FFN_STEPS - 1)))
    gain = pl.BlockSpec((1, D_MODEL), lambda i, c: (0, 0))
    return pl.pallas_call(
        body, name="ffn_bwd_data", grid=(nt + 1, FFN_STEPS),
        in_specs=[this_row, chunk, _resident(w1), _resident(w2), prev_quarter, prev_quarter, prev_quarter, gain, gain],
        out_specs=[chunk, prev_quarter_out, prev_quarter_out, gain, gain],
        out_shape=[jax.ShapeDtypeStruct((tp, D_FF), BF16), jax.ShapeDtypeStruct((tp + sr, D_MODEL), F32),
                   jax.ShapeDtypeStruct((tp + sr, D_MODEL), BF16), jax.ShapeDtypeStruct((1, D_MODEL), F32),
                   jax.ShapeDtypeStruct((1, D_MODEL), F32)],
        scratch_shapes=[pltpu.VMEM((2, tr, D_MODEL), F32)],
        compiler_params=_params("arbitrary", "arbitrary"),
    )(df2, r1, w1, w2, dy, h1, mix, g_pre_ffn, g_post_mix)


def _ffn_bwd_weights(u1, da1, r1, df2):
    tp = u1.shape[0]
    tr = _wgrad_row_tile(tp)

    def body(u_ref, da_ref, r1_ref, df2_ref, dw1_ref, dw2_ref):
        i = pl.program_id(1)

        def products():
            r = r1_ref[...].astype(F32)
            return _dot_tn(u_ref[...], da_ref[...]), _dot_tn((r * r).astype(BF16), df2_ref[...])

        @pl.when(i == 0)
        def _():
            dw1_ref[0], dw2_ref[0] = products()

        @pl.when(i > 0)
        def _():
            p1, p2 = products()
            dw1_ref[0] += p1
            dw2_ref[0] += p2

    row = pl.BlockSpec((tr, D_MODEL), lambda c, i: (i, 0))
    chunk = pl.BlockSpec((tr, FF_CHUNK), lambda c, i: (i, c))
    return pl.pallas_call(
        body, name="ffn_bwd_weights", grid=(N_CHIPS, tp // tr),
        in_specs=[row, chunk, chunk, row],
        out_specs=[pl.BlockSpec((1, D_MODEL, FF_CHUNK), lambda c, i: (c, 0, 0)),
                   pl.BlockSpec((1, FF_CHUNK, D_MODEL), lambda c, i: (c, 0, 0))],
        out_shape=[jax.ShapeDtypeStruct((N_CHIPS, D_MODEL, FF_CHUNK), F32),
                   jax.ShapeDtypeStruct((N_CHIPS, FF_CHUNK, D_MODEL), F32)],
        compiler_params=_params("parallel", "arbitrary"),
    )(u1, da1, r1, df2)


N_VEC_ROWS = 8


def _outproj_lru_bwd(dmix, w_out, attn, rec, xr, yr, hr, conv_w, conv_b, wa, ba, wx, bx, lam, token):
    tp = xr.shape[0]
    tr = _row_tile(tp)
    qb, nt = tr // BLOCK, tp // tr

    def body(dm_ref, w_ref, at_ref, rc_ref, xr_ref, xh_ref, yr_ref, hr_ref, hp_ref,
             cw_ref, cb_ref, wa_ref, ba_ref, wx_ref, bx_ref, lam_ref, _,
             dxr_ref, dyr_ref, dat_ref, dwo_ref, dwa_ref, dwx_ref, vec_ref, g_next, a_next, dxc_next, dsp):
        s = pl.program_id(0)
        t = nt - 1 - s

        @pl.when(s == 0)
        def _():
            g_next[...] = jnp.zeros_like(g_next)
            a_next[...] = jnp.zeros_like(a_next)
            dxc_next[...] = jnp.zeros_like(dxc_next)
            dsp[...] = jnp.zeros_like(dsp)
            dwo_ref[...] = jnp.zeros_like(dwo_ref)
            dwa_ref[...] = jnp.zeros_like(dwa_ref)
            dwx_ref[...] = jnp.zeros_like(dwx_ref)
            vec_ref[...] = jnp.zeros_like(vec_ref)

        dm = dm_ref[...]
        dcat = _dot_nt(dm, w_ref[...])
        dat_ref[...] = dcat[:, :ATTN_WIDTH].astype(BF16)
        drec_tile = dcat[:, ATTN_WIDTH:]
        dwo_ref[:ATTN_WIDTH] += _dot_tn(at_ref[...], dm)
        dwo_ref[ATTN_WIDTH:] += _dot_tn(rc_ref[...], dm)

        first_tile = t == 0
        cw, cb = cw_ref[...], cb_ref[...]
        lam_v = lam_ref[...]
        sp = _softplus(-lam_v)
        wa_m, ba_v, wx_m, bx_v = wa_ref[...], ba_ref[...], wx_ref[...], bx_ref[...]
        rows = lax.broadcasted_iota(jnp.int32, (BLOCK, LRU_WIDTH), 0)
        col = lambda v: jnp.sum(v, axis=0, keepdims=True)

        g_after, a_after, dxc_after = g_next[0:1], a_next[0:1], dxc_next[...]
        xbs, dgrs, dgis = [], [], []
        vec = [jnp.zeros((1, LRU_WIDTH), F32) for _ in range(N_VEC_ROWS)]
        for i in reversed(range(qb)):
            blk = slice(i * BLOCK, (i + 1) * BLOCK)
            if i == 0:
                x_before = jnp.where(first_tile, 0.0, xh_ref[...])
                h_before = jnp.where(first_tile, 0.0, hp_ref[7:8])
            else:
                x_before = xr_ref[i * BLOCK - 8:i * BLOCK]
                h_before = hr_ref[i * BLOCK - 1:i * BLOCK]
            taps = _conv_taps(xr_ref[blk], x_before)
            xc = cb + sum(cw[k:k + 1] * taps[k] for k in range(4))
            xb, r, ig, a, mult = _lru_gates(xc, wa_m, ba_v, wx_m, bx_v, sp)

            yr_v = yr_ref[blk]
            gl, th = _gelu(yr_v)
            h = hr_ref[blk]
            drec = drec_tile[blk]
            dyr_ref[blk] = (drec * h * _gelu_grad(yr_v, th)).astype(BF16)

            a_up = jnp.where(rows == BLOCK - 1, a_after, pltpu.roll(a, BLOCK - 1, 0))
            g = _scan_rev(a_up, drec * gl, g_after)
            g_after, a_after = g[0:1], a[0:1]

            h_prev = jnp.where(rows == 0, h_before, pltpu.roll(h, 1, 0))
            du, da = g, g * h_prev
            if i == 0:
                real = (t * tr + rows) >= PAD_ROWS
                du, da = jnp.where(real, du, 0.0), jnp.where(real, da, 0.0)
            dmult = du * (ig * xc)
            dig = du * (mult * xc)
            dxc = du * (mult * ig)
            dlog_a = da * a - dmult * (a * a / mult)
            if i == 0:
                dlog_a = jnp.where(real, dlog_a, 0.0)
            dgr = (dlog_a * (-LRU_C * sp)) * (r * (1.0 - r))
            dgi = dig * (ig * (1.0 - ig))
            dgr_b, dgi_b = dgr.astype(BF16), dgi.astype(BF16)
            dxc = dxc + _dot_nt(dgr_b, wa_m) + _dot_nt(dgi_b, wx_m)
            xbs.append(xb)
            dgrs.append(dgr_b)
            dgis.append(dgi_b)

            ext = jnp.concatenate([dxc, dxc_after], axis=0)
            up = [ext[:BLOCK] if j == 0 else pltpu.roll(ext, BLOCK + 8 - j, 0)[:BLOCK] for j in range(4)]
            dxr_ref[blk] = sum(cw[k:k + 1] * up[3 - k] for k in range(4)).astype(BF16)
            dxc_after = dxc[:8]

            for k in range(4):
                vec[k] = vec[k] + col(dxc * taps[k])
            vec[4] = vec[4] + col(dxc)
            vec[5] = vec[5] + col(dgr)
            vec[6] = vec[6] + col(dgi)
            vec[7] = vec[7] + col(dlog_a * (-LRU_C * r))

        g_next[0:1], a_next[0:1], dxc_next[...] = g_after, a_after, dxc_after
        xb_all = jnp.concatenate(xbs, axis=0)
        dwa_ref[...] += _dot_tn(xb_all, jnp.concatenate(dgrs, axis=0))
        dwx_ref[...] += _dot_tn(xb_all, jnp.concatenate(dgis, axis=0))
        for k in range(7):
            vec_ref[k:k + 1] += vec[k]
        dsp[0:1] += vec[7]

        @pl.when(s == nt - 1)
        def _():
            vec_ref[7:8] = dsp[0:1] * (-_sigmoid(-lam_v))

    blk_spec = pl.BlockSpec((tr, LRU_WIDTH), lambda s: (nt - 1 - s, 0))
    rows_before = pl.BlockSpec((8, LRU_WIDTH), lambda s: (jnp.maximum((nt - 1 - s) * (tr // 8) - 1, 0), 0))
    full = lambda a: pl.BlockSpec(a.shape, lambda s: (0,) * a.ndim)
    small = [conv_w, conv_b, wa, ba, wx, bx, lam, token]
    sq = pl.BlockSpec((LRU_WIDTH, LRU_WIDTH), lambda s: (0, 0))
    wide = pl.BlockSpec((tr, D_MODEL), lambda s: (nt - 1 - s, 0))
    whole = pl.BlockSpec((D_MODEL, D_MODEL), lambda s: (0, 0))
    return pl.pallas_call(
        body, name="outproj_lru_bwd", grid=(nt,),
        in_specs=[wide, whole, blk_spec, blk_spec, blk_spec, rows_before, blk_spec, blk_spec, rows_before]
        + [full(a) for a in small],
        out_specs=[blk_spec, blk_spec, blk_spec, whole, sq, sq, pl.BlockSpec((N_VEC_ROWS, LRU_WIDTH), lambda s: (0, 0))],
        out_shape=[jax.ShapeDtypeStruct((tp, LRU_WIDTH), BF16), jax.ShapeDtypeStruct((tp, LRU_WIDTH), BF16),
                   jax.ShapeDtypeStruct((tp, ATTN_WIDTH), BF16), jax.ShapeDtypeStruct((D_MODEL, D_MODEL), F32),
                   jax.ShapeDtypeStruct((LRU_WIDTH, LRU_WIDTH), F32), jax.ShapeDtypeStruct((LRU_WIDTH, LRU_WIDTH), F32),
                   jax.ShapeDtypeStruct((N_VEC_ROWS, LRU_WIDTH), F32)],
        scratch_shapes=[pltpu.VMEM((8, LRU_WIDTH), F32)] * 4,
        compiler_params=_params("arbitrary"),
    )(dmix, w_out, attn, rec, xr, xr, yr, hr, hr, *small)


def _attn_bwd_tile(tp):
    return _wgrad_row_tile(tp)


def _attn_bwd(qkv, dattn, probs, sink_probs, token):
    tp = qkv.shape[0]
    tr = _attn_bwd_tile(tp)
    qb, nt = tr // BLOCK, tp // tr
    n_groups = N_KV

    def body(p_ref, ps_ref, q_ref, kp_ref, kc_ref, vp_ref, vc_ref, do_ref, _, dq_ref, dkv_ref, ex_ref, ds_ref, dsink):
        t = pl.program_id(0)

        @pl.when(t == 0)
        def _():
            dsink[...] = jnp.zeros_like(dsink)

        k_all = jnp.concatenate([kp_ref[...], kc_ref[...]], axis=0)
        v_all = jnp.concatenate([vp_ref[...], vc_ref[...]], axis=0)
        tail = None
        for i in range(qb):
            rows = slice(i * BLOCK, (i + 1) * BLOCK)
            qt = (q_ref[rows].astype(F32) * _QSCALE).T
            dot = do_ref[rows].astype(F32).T
            k2, v2 = k_all[i * BLOCK:(i + 2) * BLOCK], v_all[i * BLOCK:(i + 2) * BLOCK]
            dqs, dks, dvs = [], [], []
            for g in range(n_groups):
                cols = slice(g * HEAD_DIM, (g + 1) * HEAD_DIM)
                k_g, v_g = k2[:, cols], v2[:, cols]
                qgt, dogt = _heads_t(qt, g), _heads_t(dot, g)
                pb = p_ref[i, g]
                p = pb.astype(F32)
                dpt = _dot(v_g, dogt)
                delta = jnp.sum(p * dpt, axis=0, keepdims=True)
                dst = (p * (dpt - delta)).astype(BF16)
                dqs.append(_dot_tn(k_g, dst) * _QSCALE)
                dks.append(_dot_nt(qgt, dst))
                dvs.append(_dot_nt(dogt, pb))
                dsink[g:g + 1] -= ps_ref[i, g:g + 1] * delta
            dq_ref[rows] = _from_heads_t(dqs).astype(BF16)
            dkv = jnp.concatenate([jnp.concatenate(dks, axis=0).T, jnp.concatenate(dvs, axis=0).T], axis=1)
            if i == 0:
                ex_ref[0] = dkv[:BLOCK]
            else:
                dkv_ref[(i - 1) * BLOCK:i * BLOCK] = (tail + dkv[:BLOCK]).astype(BF16)
            tail = dkv[BLOCK:]
        dkv_ref[(qb - 1) * BLOCK:] = tail.astype(BF16)

        @pl.when(t == nt - 1)
        def _():
            lane = lax.broadcasted_iota(jnp.int32, (1, ATTN_HEADS), 1)
            acc = jnp.zeros((1, ATTN_HEADS), F32)
            for h in range(ATTN_HEADS):
                g, hh = divmod(h, GQA_GROUP)
                acc = acc + jnp.where(lane == h, jnp.sum(dsink[g:g + 1, hh * BLOCK:(hh + 1) * BLOCK]), 0.0)
            ds_ref[...] = acc

    cur = lambda w: pl.BlockSpec((tr, w), lambda t: (t, 0))
    return pl.pallas_call(
        body, name="attn_bwd", grid=(nt,),
        in_specs=_prob_specs(qb) + [cur(ATTN_WIDTH)] + _kv_specs(tr)
        + [cur(ATTN_WIDTH), pl.BlockSpec(token.shape, lambda t: (0, 0))],
        out_specs=[cur(ATTN_WIDTH), cur(2 * KV_WIDTH), pl.BlockSpec((1, BLOCK, 2 * KV_WIDTH), lambda t: (t, 0, 0)),
                   pl.BlockSpec((1, ATTN_HEADS), lambda t: (0, 0))],
        out_shape=[jax.ShapeDtypeStruct((tp, ATTN_WIDTH), BF16), jax.ShapeDtypeStruct((tp, 2 * KV_WIDTH), BF16),
                   jax.ShapeDtypeStruct((nt, BLOCK, 2 * KV_WIDTH), F32), jax.ShapeDtypeStruct((1, ATTN_HEADS), F32)],
        scratch_shapes=[pltpu.VMEM((n_groups, GROUP_ROWS), F32)],
        compiler_params=_params("arbitrary"),
    )(probs, sink_probs, qkv, qkv, qkv, qkv, qkv, dattn, token)


def _fix_dkv(dkv, dkv_extra):
    tp = dkv.shape[0]
    tr = _attn_bwd_tile(tp)
    nt, qb = tp // tr, tr // BLOCK
    if nt == 1:
        return dkv

    def body(d_ref, ex_ref, o_ref):
        o_ref[...] = (d_ref[...].astype(F32) + ex_ref[0]).astype(BF16)

    last = pl.BlockSpec((BLOCK, 2 * KV_WIDTH), lambda t: (t * qb + qb - 1, 0))
    return pl.pallas_call(
        body, name="fix_dkv", grid=(nt - 1,),
        in_specs=[last, pl.BlockSpec((1, BLOCK, 2 * KV_WIDTH), lambda t: (t + 1, 0, 0))],
        out_specs=last, out_shape=jax.ShapeDtypeStruct(dkv.shape, dkv.dtype),
        input_output_aliases={0: 0}, compiler_params=_params("parallel"),
    )(dkv, dkv_extra)


def _inproj_wgrad(dq, dkv, dxr, dyr, u0):
    tp = dq.shape[0]
    tr = _wgrad_row_tile(tp)

    def body(dq_ref, dkv_ref, dxr_ref, dyr_ref, u_ref, dw_ref):
        i = pl.program_id(0)

        def product():
            dz = jnp.concatenate([dq_ref[...], dkv_ref[...], dxr_ref[...], dyr_ref[...]], axis=1)
            return _dot_tn(dz, u_ref[...])

        @pl.when(i == 0)
        def _():
            dw_ref[...] = product()

        @pl.when(i > 0)
        def _():
            dw_ref[...] += product()

    row = lambda w: pl.BlockSpec((tr, w), lambda i: (i, 0))
    return pl.pallas_call(
        body, name="inproj_wgrad", grid=(tp // tr,),
        in_specs=[row(ATTN_WIDTH), row(2 * KV_WIDTH), row(LRU_WIDTH), row(LRU_WIDTH), row(D_MODEL)],
        out_specs=pl.BlockSpec((IN_WIDTH, D_MODEL), lambda i: (0, 0)),
        out_shape=jax.ShapeDtypeStruct((IN_WIDTH, D_MODEL), F32),
        compiler_params=_params("arbitrary"),
    )(dq, dkv, dxr, dyr, u0)


def _inproj_dgrad(dq, dkv, dxr, dyr, w_in, head, x, dh1, g, token):
    tp = dq.shape[0]
    tr = _row_tile(tp)
    nt, qb = tp // tr, tr // BLOCK

    def body(*refs):
        dq_ref, dkv_ref, dxr_ref, dyr_ref, w_ref, head_ref = refs[:6]
        pieces = refs[6:6 + qb]
        dh1_ref, g_ref, _, gx_ref, dhead_ref, dg_ref, buf, sems = refs[6 + qb:]
        i = pl.program_id(0)
        slot = i % 2

        def out_copy(step, at):
            return pltpu.make_async_copy(buf.at[at], gx_ref.at[pl.ds(step * tr - BLOCK, tr)], sems.at[at])

        dz = jnp.concatenate([dq_ref[...], dkv_ref[...], dxr_ref[...], dyr_ref[...]], axis=1)
        du = _dot(dz, w_ref[...])
        hhat, rs = _rms(_seq_tile(head_ref[...], pieces, i))
        dx, dg = _rms_bwd(hhat, rs, g_ref[...], du)
        dh0 = dh1_ref[...] + dx

        @pl.when(i >= 3)
        def _():
            out_copy(i - 2, slot).wait()

        buf[slot] = dh0

        @pl.when(i == 0)
        def _():
            dg_ref[...] = dg
            dhead_ref[...] = dh0[:BLOCK]
            if tr > BLOCK:
                first = pltpu.make_async_copy(buf.at[0, pl.ds(BLOCK, tr - BLOCK)], gx_ref.at[pl.ds(0, tr - BLOCK)],
                                              sems.at[0])
                first.start()
                first.wait()

        @pl.when(i >= 1)
        def _():
            dg_ref[...] += dg
            out_copy(i, slot).start()

        @pl.when(i == nt - 1)
        def _():
            if nt >= 3:
                out_copy(nt - 2, (nt - 2) % 2).wait()
            if nt >= 2:
                out_copy(nt - 1, (nt - 1) % 2).wait()

    row = lambda w: pl.BlockSpec((tr, w), lambda i: (i, 0))
    full = lambda shape: pl.BlockSpec(shape, lambda i: (0,) * len(shape))
    return pl.pallas_call(
        body, name="inproj_dgrad", grid=(tp // tr,),
        in_specs=[row(ATTN_WIDTH), row(2 * KV_WIDTH), row(LRU_WIDTH), row(LRU_WIDTH), full(w_in.shape),
                  full(head.shape)] + _seq_specs(tr) + [row(D_MODEL), full(g.shape), full(token.shape)],
        out_specs=[pl.BlockSpec(memory_space=pl.ANY), full((BLOCK, D_MODEL)), full((1, D_MODEL))],
        out_shape=[jax.ShapeDtypeStruct(x.shape, F32), jax.ShapeDtypeStruct((BLOCK, D_MODEL), F32),
                   jax.ShapeDtypeStruct((1, D_MODEL), F32)],
        scratch_shapes=[pltpu.VMEM((2, tr, D_MODEL), F32), pltpu.SemaphoreType.DMA((2,))],
        compiler_params=_params("arbitrary"),
    )(dq, dkv, dxr, dyr, w_in, head, *([x] * qb), dh1, g, token)


def _dense_block_diag(w):
    eye = jnp.eye(LRU_BLOCKS, dtype=w.dtype)
    return (w[:, :, None, :] * eye[:, None, :, None]).reshape(LRU_WIDTH, LRU_WIDTH)


def _diag_blocks(dense):
    d4 = dense.reshape(LRU_BLOCKS, LRU_BLOCK, LRU_BLOCKS, LRU_BLOCK)
    return jnp.stack([d4[n, :, n, :] for n in range(LRU_BLOCKS)])


def _local_step(head, x, tgt, g_pre_mix, w_in, conv_w, conv_b, w_a, b_a, w_x, b_x, lam, sinks, g_post_mix,
                g_pre_ffn, g_post_ffn, late_weights, on_ffn_grads, on_outproj_bwd, on_mixer_grads, token):
    wa = _dense_block_diag(w_a).astype(BF16)
    wx = _dense_block_diag(w_x).astype(BF16)

    u0, qkv, xr, yr, hr, rec = _inproj_lru_fwd(head, x, g_pre_mix, w_in, conv_w, conv_b, wa, b_a, wx, b_x, lam, token)
    attn, probs, sink_probs = _attn_fwd(qkv, sinks)
    w_out, ffn_weights = late_weights([attn, rec])
    mix, h1, u1 = _outproj_fwd(attn, rec, w_out, head, x, g_post_mix, g_pre_ffn)
    w1, w2 = ffn_weights([u1])
    r1, dy, df2, loss, dg_post_ffn = _ffn_fwd(u1, w1, w2, h1, tgt, g_post_ffn)

    da1, dh1, dmix, dg_pre_ffn, dg_post_mix = _ffn_bwd_data(df2, r1, w1, w2, dy, h1, mix, g_pre_ffn, g_post_mix)
    dw1, dw2 = _ffn_bwd_weights(u1, da1, r1, df2)
    token2 = on_ffn_grads(dw1, dw2)
    dxr, dyr, dattn, dw_out, dwa, dwx, vec = _outproj_lru_bwd(dmix, w_out, attn, rec, xr, yr, hr, conv_w, conv_b,
                                                              wa, b_a, wx, b_x, lam, token2)
    token3 = on_outproj_bwd(dattn)
    dq, dkv, dkv_extra, dsinks = _attn_bwd(qkv, dattn, probs, sink_probs, token3)
    dkv = _fix_dkv(dkv, dkv_extra)
    dw_in = _inproj_wgrad(dq, dkv, dxr, dyr, u0)
    token4 = on_mixer_grads(dw_in, dw_out)
    dx, dhead, dg_pre_mix = _inproj_dgrad(dq, dkv, dxr, dyr, w_in, head, x, dh1, g_pre_mix, token4)

    grads = dict(
        g_pre_mix=dg_pre_mix, conv_w=vec[0:4], conv_b=vec[4:5], w_a=_diag_blocks(dwa), b_a=vec[5:6],
        w_x=_diag_blocks(dwx), b_x=vec[6:7], lru_lambda=vec[7:8], attn_sinks=dsinks,
        g_post_mix=dg_post_mix, g_pre_ffn=dg_pre_ffn, g_post_ffn=dg_post_ffn)
    return loss, dx, dhead, grads


HBM = pl.BlockSpec(memory_space=pltpu.HBM)


def _mesh_pos():
    return lax.axis_index("x"), lax.axis_index("y"), lax.axis_index("c")


def _other_chips(x, y):
    return [(1 - x, y), (x, 1 - y), (1 - x, 1 - y)]


def _remote(src, dst, send_sem, recv_sem, to):
    return pltpu.make_async_remote_copy(src_ref=src, dst_ref=dst, send_sem=send_sem, recv_sem=recv_sem,
                                        device_id=to, device_id_type=MESH)


def _gather_weights(shards, lands, tiny, tiny_land):
    nbig = len(shards)

    def body(*refs):
        srcs, tiny_src = refs[:nbig], refs[nbig]
        outs, tiny_out = refs[2 * nbig + 2:3 * nbig + 2], refs[3 * nbig + 2]
        ici_send, ici_recv, d2d_send, d2d_recv, tiny_send, tiny_recv = refs[3 * nbig + 3:]
        x, y, c = _mesh_pos()
        me = 2 * x + y
        chips = _other_chips(x, y)
        sibling = (x, y, 1 - c)
        sends = []
        for w, (src, out) in enumerate(zip(srcs, outs)):
            hr = src.shape[0] // 2
            for j, chip in enumerate(chips):
                k = 3 * w + j
                cp = _remote(src.at[pl.ds(c * hr, hr)], out.at[me, pl.ds(c * hr, hr)],
                             ici_send.at[k], ici_recv.at[k], (*chip, c))
                cp.start()
                sends.append(cp)
        for j, chip in enumerate(chips):
            cp = _remote(tiny_src, tiny_out.at[me], tiny_send.at[j], tiny_recv.at[j], (*chip, c))
            cp.start()
            sends.append(cp)
        for w, (src, out) in enumerate(zip(srcs, outs)):
            hr = src.shape[0] // 2
            for j, (px, py) in enumerate(chips):
                k = 3 * w + j
                landed = out.at[2 * px + py, pl.ds(c * hr, hr)]
                _remote(landed, landed, ici_send.at[k], ici_recv.at[k], sibling).wait_recv()
                cp = _remote(landed, landed, d2d_send.at[k], d2d_recv.at[k], sibling)
                cp.start()
                sends.append(cp)
        for w, (src, out) in enumerate(zip(srcs, outs)):
            hr = src.shape[0] // 2
            for j, (px, py) in enumerate(chips):
                k = 3 * w + j
                other = out.at[2 * px + py, pl.ds((1 - c) * hr, hr)]
                _remote(other, other, d2d_send.at[k], d2d_recv.at[k], sibling).wait_recv()
        for j, (px, py) in enumerate(chips):
            blk = tiny_out.at[2 * px + py]
            _remote(blk, blk, tiny_send.at[j], tiny_recv.at[j], sibling).wait_recv()
        for cp in sends:
            cp.wait_send()

    out_shape = [jax.ShapeDtypeStruct(l.shape, l.dtype) for l in list(lands) + [tiny_land]]
    n = 3 * nbig
    return pl.pallas_call(
        body, name="gather_weights", out_shape=out_shape,
        in_specs=[HBM] * (2 * nbig + 2), out_specs=[HBM] * (nbig + 1),
        input_output_aliases={nbig + 1 + i: i for i in range(nbig + 1)},
        scratch_shapes=[pltpu.SemaphoreType.DMA((n,)),
                        pltpu.SemaphoreType.DMA((n,)), pltpu.SemaphoreType.DMA((n,)), pltpu.SemaphoreType.DMA((n,)),
                        pltpu.SemaphoreType.DMA((3,)), pltpu.SemaphoreType.DMA((3,))],
    )(*shards, tiny, *lands, tiny_land)


def _prep_shard(w, me):
    rows, cols = w.shape
    tr = _elementwise_tile(rows)

    def body(me_ref, w_ref, s_ref, l_ref):
        b = w_ref[...].astype(BF16)
        s_ref[...] = b
        l_ref[0] = b

    return pl.pallas_call(
        body, name="prep_shard",
        grid_spec=pltpu.PrefetchScalarGridSpec(
            num_scalar_prefetch=1, grid=(rows // tr,),
            in_specs=[pl.BlockSpec((tr, cols), lambda i, me_ref: (i, 0))],
            out_specs=[pl.BlockSpec((tr, cols), lambda i, me_ref: (i, 0)),
                       pl.BlockSpec((1, tr, cols), lambda i, me_ref: (me_ref[0], i, 0))]),
        out_shape=[jax.ShapeDtypeStruct((rows, cols), BF16), jax.ShapeDtypeStruct((N_CHIPS, rows, cols), BF16)],
        compiler_params=_params("parallel"),
    )(me, w)


def _prep_tiny(tiny, me, slots=N_CHIPS):
    def body(me_ref, t_ref, l_ref):
        l_ref[0] = t_ref[...]

    return pl.pallas_call(
        body, name="prep_tiny",
        grid_spec=pltpu.PrefetchScalarGridSpec(
            num_scalar_prefetch=1, grid=(1,),
            in_specs=[pl.BlockSpec(tiny.shape, lambda i, me_ref: (0, 0))],
            out_specs=pl.BlockSpec((1,) + tiny.shape, lambda i, me_ref: (me_ref[0], 0, 0))),
        out_shape=jax.ShapeDtypeStruct((slots,) + tiny.shape, tiny.dtype),
    )(me, tiny)


N_DEV = 8


def _sibling_exchange(parts, token):
    def body(*refs):
        n = len(parts)
        srcs, outs, send_sems, recv_sems = refs[:n], refs[n + 1:2 * n + 1], refs[2 * n + 1], refs[2 * n + 2]
        x, y, c = _mesh_pos()
        sibling = (x, y, 1 - c)
        cps = []
        for w, (src, out) in enumerate(zip(srcs, outs)):
            hr = src.shape[1] // 2
            cp = _remote(src.at[:, pl.ds((1 - c) * hr, hr)], out, send_sems.at[w], recv_sems.at[w], sibling)
            cp.start()
            cps.append(cp)
        for cp in cps:
            cp.wait()

    n = len(parts)
    return pl.pallas_call(
        body, name="sibling_exchange",
        out_shape=[jax.ShapeDtypeStruct((p.shape[0], p.shape[1] // 2, p.shape[2]), p.dtype) for p in parts],
        in_specs=[HBM] * n + [pl.BlockSpec(memory_space=pl.ANY)], out_specs=[HBM] * n,
        scratch_shapes=[pltpu.SemaphoreType.DMA((n,)), pltpu.SemaphoreType.DMA((n,))],
    )(*parts, token)


def _chip_presum(part, from_sibling, pos):
    _, hr, cols = from_sibling.shape
    tr = _elementwise_tile(hr)
    steps = hr // tr

    def body(pos_ref, a_ref, b_ref, o_ref, land_ref):
        s = (a_ref[...] + b_ref[...]).astype(BF16)
        o_ref[...] = s

        @pl.when(pl.program_id(1) == pos_ref[1])
        def _():
            land_ref[...] = s

    return pl.pallas_call(
        body, name="chip_presum",
        grid_spec=pltpu.PrefetchScalarGridSpec(
            num_scalar_prefetch=1, grid=(steps, N_CHIPS),
            in_specs=[pl.BlockSpec((1, tr, cols), lambda i, j, p: (j, p[0] * steps + i, 0)),
                      pl.BlockSpec((1, tr, cols), lambda i, j, p: (j, i, 0))],
            out_specs=[pl.BlockSpec((1, tr, cols), lambda i, j, p: (j, i, 0)),
                       pl.BlockSpec((1, tr, cols), lambda i, j, p: (p[1], p[0] * steps + i, 0))]),
        out_shape=[jax.ShapeDtypeStruct(from_sibling.shape, BF16),
                   jax.ShapeDtypeStruct((N_CHIPS, 2 * hr, cols), BF16)],
        compiler_params=_params("arbitrary", "arbitrary"),
    )(pos, part, from_sibling)


def _scatter_partials(cparts, lands, done_cparts=(), done_lands=()):
    n_new = len(cparts)
    nw = n_new + len(done_cparts)

    def body(*refs):
        srcs = refs[:nw]
        outs = refs[2 * nw:3 * nw]
        own_send, own_recv, ici_send, ici_recv, d2d_send, d2d_recv = refs[3 * nw:]
        x, y, c = _mesh_pos()
        me = 2 * x + y
        chips = _other_chips(x, y)
        sibling = (x, y, 1 - c)
        sends = []
        for w in list(range(n_new, nw)) + list(range(n_new)):
            src, out = srcs[w], outs[w]
            hr = src.shape[1]
            mine = out.at[me, pl.ds(c * hr, hr)]
            cp = _remote(src.at[me], mine, own_send.at[w], own_recv.at[w], sibling)
            cp.start()
            sends.append(cp)
            for j, (px, py) in enumerate(chips):
                if w >= n_new:
                    break
                k = 3 * w + j
                cp = _remote(src.at[2 * px + py], mine, ici_send.at[k], ici_recv.at[k], (px, py, c))
                cp.start()
                sends.append(cp)
        for w in list(range(n_new, nw)) + list(range(n_new)):
            src, out = srcs[w], outs[w]
            hr = src.shape[1]
            for j, (px, py) in enumerate(chips):
                k = 3 * w + j
                landed = out.at[2 * px + py, pl.ds(c * hr, hr)]
                if w < n_new:
                    _remote(landed, landed, ici_send.at[k], ici_recv.at[k], sibling).wait_recv()
                cp = _remote(landed, landed, d2d_send.at[k], d2d_recv.at[k], sibling)
                cp.start()
                sends.append(cp)
        for w, (src, out) in enumerate(zip(srcs, outs)):
            hr = src.shape[1]
            other = out.at[me, pl.ds((1 - c) * hr, hr)]
            _remote(other, other, own_send.at[w], own_recv.at[w], sibling).wait_recv()
            for j, (px, py) in enumerate(chips):
                k = 3 * w + j
                other = out.at[2 * px + py, pl.ds((1 - c) * hr, hr)]
                _remote(other, other, d2d_send.at[k], d2d_recv.at[k], sibling).wait_recv()
        for cp in sends:
            cp.wait_send()

    n = 3 * nw
    dma = pltpu.SemaphoreType.DMA
    every = list(cparts) + list(done_cparts)
    every_lands = list(lands) + list(done_lands)
    return pl.pallas_call(
        body, name="scatter_partials",
        out_shape=[jax.ShapeDtypeStruct(l.shape, l.dtype) for l in every_lands],
        in_specs=[HBM] * (2 * nw), out_specs=[HBM] * nw,
        input_output_aliases={nw + i: i for i in range(nw)},
        scratch_shapes=[dma((nw,)), dma((nw,)), dma((n,)), dma((n,)), dma((n,)), dma((n,))],
    )(*every, *every_lands)


SEM = pl.BlockSpec(memory_space=pltpu.SEMAPHORE)
SPLIT_COPY = pltpu.CompilerParams(has_side_effects=pltpu.SideEffectType.DATAFLOW_SIDE_EFFECTING)


def _hbm(a):
    return pltpu.with_memory_space_constraint(a, pltpu.HBM)


def _gather_copies(srcs, lands, send_sems, recv_sems):
    x, y, c = _mesh_pos()
    me = 2 * x + y
    sends, recvs = [], []
    for w, (src, land) in enumerate(zip(srcs, lands)):
        hr = src.shape[0] // 2
        for j, (px, py) in enumerate(_other_chips(x, y)):
            k = 3 * w + j
            sends.append(_remote(src.at[pl.ds(c * hr, hr)], land.at[me, pl.ds(c * hr, hr)],
                                 send_sems.at[k], recv_sems.at[k], (px, py, c)))
            got = land.at[2 * px + py, pl.ds(c * hr, hr)]
            recvs.append(_remote(got, got, send_sems.at[k], recv_sems.at[k], (px, py, c)))
    return sends, recvs


def _scatter_copies(srcs, lands, send_sems, recv_sems):
    x, y, c = _mesh_pos()
    me = 2 * x + y
    sends, recvs = [], []
    for w, (src, land) in enumerate(zip(srcs, lands)):
        hr = src.shape[1]
        for j, (px, py) in enumerate(_other_chips(x, y)):
            k = 3 * w + j
            sends.append(_remote(src.at[2 * px + py], land.at[me, pl.ds(c * hr, hr)],
                                 send_sems.at[k], recv_sems.at[k], (px, py, c)))
            got = land.at[2 * px + py, pl.ds(c * hr, hr)]
            recvs.append(_remote(got, got, send_sems.at[k], recv_sems.at[k], (px, py, c)))
    return sends, recvs


def _sibling_copies(srcs, lands, send_sems, recv_sems):
    x, y, c = _mesh_pos()
    sibling = (x, y, 1 - c)
    sends, recvs = [], []
    for w, (src, land) in enumerate(zip(srcs, lands)):
        hr = src.shape[1] // 2
        sends.append(_remote(src.at[:, pl.ds((1 - c) * hr, hr)], land, send_sems.at[w], recv_sems.at[w], sibling))
        recvs.append(_remote(land, land, send_sems.at[w], recv_sems.at[w], sibling))
    return sends, recvs


def _inchip_copies(srcs, lands, send_sems, recv_sems):
    x, y, c = _mesh_pos()
    me = 2 * x + y
    sibling = (x, y, 1 - c)
    sends, recvs = [], []
    for w, (src, land) in enumerate(zip(srcs, lands)):
        hr = src.shape[1]
        mine, other = pl.ds(c * hr, hr), pl.ds((1 - c) * hr, hr)
        blocks = [(me, src.at[me])] + [(2 * px + py, None) for px, py in _other_chips(x, y)]
        for j, (blk, own_src) in enumerate(blocks):
            k = 4 * w + j
            landed = land.at[blk, mine]
            sends.append(_remote(landed if own_src is None else own_src, landed, send_sems.at[k], recv_sems.at[k], sibling))
            got = land.at[blk, other]
            recvs.append(_remote(got, got, send_sems.at[k], recv_sems.at[k], sibling))
    return sends, recvs


class _SemsFrom:
    def __init__(self, sems, first):
        self.sems, self.first = sems, first

    @property
    def at(self):
        return self

    def __getitem__(self, k):
        return self.sems.at[self.first + k]


def _shifted(copies_of, first):
    def copies(srcs, lands, send_sems, recv_sems):
        return copies_of(srcs, lands, _SemsFrom(send_sems, first), _SemsFrom(recv_sems, first))
    return copies


def _two_groups(copies_a, n_a, k_a, copies_b):
    shifted_b = _shifted(copies_b, k_a)

    def copies(srcs, lands, send_sems, recv_sems):
        sends_a, recvs_a = copies_a(srcs[:n_a], lands[:n_a], send_sems, recv_sems)
        sends_b, recvs_b = shifted_b(srcs[n_a:], lands[n_a:], send_sems, recv_sems)
        return sends_a + sends_b, recvs_a + recvs_b
    return copies


def _all_peers_copies(srcs, lands, send_sems, recv_sems):
    x, y, c = _mesh_pos()
    (src,), (land,) = srcs, lands
    flip = lambda v, bit: 1 - v if bit else v
    sends, recvs = [], []
    for k in range(N_DEV - 1):
        px, py, pc = flip(x, (k + 1) & 4), flip(y, (k + 1) & 2), flip(c, (k + 1) & 1)
        sends.append(_remote(src, land.at[4 * x + 2 * y + c], send_sems.at[k], recv_sems.at[k], (px, py, pc)))
        got = land.at[4 * px + 2 * py + pc]
        recvs.append(_remote(got, got, send_sems.at[k], recv_sems.at[k], (px, py, pc)))
    return sends, recvs


def _split_start(name, copies_of, srcs, land_shapes, n_copies=None):
    n = len(srcs)
    k = 3 * n if n_copies is None else n_copies

    def body(*refs):
        src_refs, land_refs = refs[:n], refs[n:2 * n]
        send_sems, recv_sems = refs[2 * n], refs[2 * n + 1]
        token = refs[-1]
        sends, _ = copies_of(src_refs, land_refs, send_sems, recv_sems)
        for cp in sends:
            cp.start()
        token[...] = jnp.zeros_like(token)

    lands = [_hbm(s) for s in land_shapes]
    dma = pltpu.SemaphoreType.DMA
    res = pl.pallas_call(
        body, name=name,
        out_shape=(dma((k,)), dma((k,)), *[pltpu.HBM(s.shape, s.dtype) for s in srcs],
                   *[pltpu.HBM(s.shape, s.dtype) for s in land_shapes], jax.ShapeDtypeStruct((8, 128), F32)),
        in_specs=[HBM] * (2 * n),
        out_specs=(SEM, SEM, *([HBM] * (2 * n)), pl.BlockSpec(memory_space=pltpu.VMEM)),
        input_output_aliases={i: 2 + i for i in range(2 * n)},
        compiler_params=SPLIT_COPY,
    )(*[_hbm(s) for s in srcs], *lands)
    return res[0], res[1], list(res[2:2 + n]), list(res[2 + n:2 + 2 * n]), res[-1]


def _split_wait(name, copies_of, send_sems, recv_sems, srcs, lands, after):
    n = len(srcs)

    def body(*refs):
        src_refs, land_refs = refs[:n], refs[n:2 * n]
        sends, recvs = copies_of(src_refs, land_refs, refs[2 * n], refs[2 * n + 1])
        for cp in sends:
            cp.wait_send()
        for cp in recvs:
            cp.wait_recv()

    res = pl.pallas_call(
        body, name=name,
        out_shape=tuple(pltpu.HBM(s.shape, s.dtype) for s in list(srcs) + list(lands)),
        in_specs=[HBM] * (2 * n) + [SEM, SEM] + [pl.BlockSpec(memory_space=pl.ANY)] * len(after),
        out_specs=tuple([HBM] * (2 * n)),
        input_output_aliases={i: i for i in range(2 * n)},
        compiler_params=SPLIT_COPY,
    )(*srcs, *lands, send_sems, recv_sems, *after)
    return list(res[:n]), list(res[n:])


def _forward_copies(srcs, lands, send_sems, recv_sems):
    x, y, c = _mesh_pos()
    sibling = (x, y, 1 - c)
    sends, recvs = [], []
    for w, land in enumerate(lands):
        hr = land.shape[1] // 2
        for j, (px, py) in enumerate(_other_chips(x, y)):
            k = 3 * w + j
            landed = land.at[2 * px + py, pl.ds(c * hr, hr)]
            sends.append(_remote(landed, landed, send_sems.at[k], recv_sems.at[k], sibling))
            other = land.at[2 * px + py, pl.ds((1 - c) * hr, hr)]
            recvs.append(_remote(other, other, send_sems.at[k], recv_sems.at[k], sibling))
    return sends, recvs


def _gather_finish(lands, n_forward):
    n = len(lands)

    def body(*refs):
        outs = refs[n:n + n_forward]
        d2d_send, d2d_recv = refs[2 * n:]
        sends, recvs = _forward_copies(None, outs, d2d_send, d2d_recv)
        for cp in sends:
            cp.start()
        for cp in recvs:
            cp.wait_recv()
        for cp in sends:
            cp.wait_send()

    dma = pltpu.SemaphoreType.DMA
    return pl.pallas_call(
        body, name="gather_finish",
        out_shape=[jax.ShapeDtypeStruct(l.shape, l.dtype) for l in lands],
        in_specs=[HBM] * n, out_specs=[HBM] * n,
        input_output_aliases={i: i for i in range(n)},
        scratch_shapes=[dma((3 * n,)), dma((3 * n,))],
    )(*lands)


def _adamw(w, g, m, v):
    m = ADAM_B1 * m + (1.0 - ADAM_B1) * g
    v = ADAM_B2 * v + (1.0 - ADAM_B2) * (g * g)
    m_hat = m / (1.0 - ADAM_B1 ** ADAM_STEP)
    v_hat = v / (1.0 - ADAM_B2 ** ADAM_STEP)
    delta = -ADAM_LR * (m_hat / (jnp.sqrt(v_hat) + ADAM_EPS) + ADAM_WD * w)
    return delta, m, v


def _adamw_big(partials, w, m, v):
    rows, cols = w.shape
    tr = _elementwise_tile(rows)

    def body(p_ref, w_ref, m_ref, v_ref, g_ref, d_ref, m2_ref, v2_ref):
        g = ((p_ref[0].astype(F32) + p_ref[1].astype(F32)) + p_ref[2].astype(F32)) + p_ref[3].astype(F32)
        g_ref[...] = g
        d_ref[...], m2_ref[...], v2_ref[...] = _adamw(w_ref[...], g, m_ref[...], v_ref[...])

    blk = pl.BlockSpec((tr, cols), lambda i: (i, 0))
    return pl.pallas_call(
        body, name="adamw_big", grid=(rows // tr,),
        in_specs=[pl.BlockSpec((N_CHIPS, tr, cols), lambda i: (0, i, 0)), blk, blk, blk],
        out_specs=[blk] * 4, out_shape=[jax.ShapeDtypeStruct((rows, cols), F32)] * 4,
        compiler_params=_params("parallel"),
    )(partials, w, m, v)


def _sum_devices(gathered, rows):
    cols = gathered.shape[1]

    def body(g_ref, o_ref):
        acc = g_ref[0:rows]
        for d in range(1, N_DEV):
            acc = acc + g_ref[d * rows:(d + 1) * rows]
        o_ref[...] = acc

    return pl.pallas_call(
        body, name="sum_devices", out_shape=jax.ShapeDtypeStruct((rows, cols), F32),
        in_specs=[pl.BlockSpec(memory_space=pltpu.VMEM)], out_specs=pl.BlockSpec(memory_space=pltpu.VMEM),
        compiler_params=pltpu.CompilerParams(vmem_limit_bytes=VMEM_LIMIT_V7X),
    )(gathered)


def _adamw_small(quads):
    n = len(quads)

    def body(*refs):
        ins, outs = refs[:4 * n], refs[4 * n:]
        for t in range(n):
            w, g, m, v = (r[...] for r in ins[4 * t:4 * t + 4])
            outs[3 * t][...], outs[3 * t + 1][...], outs[3 * t + 2][...] = _adamw(w, g, m, v)

    flat = [a for q in quads for a in q]
    vm = pl.BlockSpec(memory_space=pltpu.VMEM)
    res = pl.pallas_call(
        body, name="adamw_small",
        out_shape=[jax.ShapeDtypeStruct(q[0].shape, F32) for q in quads for _ in range(3)],
        in_specs=[vm] * (4 * n), out_specs=[vm] * (3 * n),
    )(*flat)
    return [tuple(res[3 * t:3 * t + 3]) for t in range(n)]


SMALL_PACK_ROWS = 96
META_COLS = D_MODEL // N_CHIPS
CONV_COLS = LRU_WIDTH // N_CHIPS
_WEIGHTS = ['meta_tokens', 'g_pre_mix', 'w_in', 'conv_w', 'conv_b', 'w_a', 'b_a', 'w_x', 'b_x', 'lru_lambda',
            'attn_sinks', 'w_out', 'g_post_mix', 'g_pre_ffn', 'w_ff1', 'w_ff2', 'g_post_ffn']
_BIG = ['w_in', 'w_out', 'w_ff1', 'w_ff2']


def _pack_small(dmeta, g, loss):
    z = lambda r, c: jnp.zeros((r, c), F32)
    rows = [
        dmeta,
        g['g_pre_mix'], g['g_post_mix'], g['g_pre_ffn'], g['g_post_ffn'],
        jnp.concatenate([g['conv_w'], z(4, 512)], axis=1),
        jnp.concatenate([g['conv_b'], g['b_a']], axis=1),
        jnp.concatenate([g['b_x'], g['lru_lambda']], axis=1),
        jnp.concatenate([g['attn_sinks'], z(1, D_MODEL - ATTN_HEADS)], axis=1),
        jnp.concatenate([loss, z(1, D_MODEL - 1)], axis=1),
        z(4, D_MODEL),
        g['w_a'].reshape(32, D_MODEL), g['w_x'].reshape(32, D_MODEL),
    ]
    return jnp.concatenate(rows, axis=0)


def _unpack_small(s, chip):
    return dict(
        meta_tokens=lax.dynamic_slice(s[0:N_META], (0, chip * META_COLS), (N_META, META_COLS)),
        g_pre_mix=s[16:17], g_post_mix=s[17:18], g_pre_ffn=s[18:19], g_post_ffn=s[19:20],
        conv_w=lax.dynamic_slice(s[20:24], (0, chip * CONV_COLS), (4, CONV_COLS)).reshape(1, 4, CONV_COLS),
        conv_b=s[24:25, :512], b_a=s[24:25, 512:], b_x=s[25:26, :512], lru_lambda=s[25:26, 512:],
        attn_sinks=s[26:27, :ATTN_HEADS], loss=s[27, 0],
        w_a=s[32:64].reshape(1, LRU_BLOCKS, LRU_BLOCK, LRU_BLOCK),
        w_x=s[64:96].reshape(1, LRU_BLOCKS, LRU_BLOCK, LRU_BLOCK))


def _as2d(a):
    if a.ndim == 2:
        return a
    return a.reshape(-1, a.shape[-1])


def kernel(x, meta_tokens, g_pre_mix, w_in, conv_w, conv_b, w_a, b_a, w_x, b_x, lru_lambda, attn_sinks, w_out, g_post_mix, g_pre_ffn, w_ff1, w_ff2, g_post_ffn, loss_target, m_meta_tokens, m_g_pre_mix, m_w_in, m_conv_w, m_conv_b, m_w_a, m_b_a, m_w_x, m_b_x, m_lru_lambda, m_attn_sinks, m_w_out, m_g_post_mix, m_g_pre_ffn, m_w_ff1, m_w_ff2, m_g_post_ffn, v_meta_tokens, v_g_pre_mix, v_w_in, v_conv_w, v_conv_b, v_w_a, v_b_a, v_w_x, v_b_x, v_lru_lambda, v_attn_sinks, v_w_out, v_g_post_mix, v_g_pre_ffn, v_w_ff1, v_w_ff2, v_g_post_ffn):
    weights = dict(meta_tokens=meta_tokens, g_pre_mix=g_pre_mix, w_in=w_in, conv_w=conv_w, conv_b=conv_b, w_a=w_a,
                   b_a=b_a, w_x=w_x, b_x=b_x, lru_lambda=lru_lambda, attn_sinks=attn_sinks, w_out=w_out,
                   g_post_mix=g_post_mix, g_pre_ffn=g_pre_ffn, w_ff1=w_ff1, w_ff2=w_ff2, g_post_ffn=g_post_ffn)
    mom1 = dict(zip(_WEIGHTS, [m_meta_tokens, m_g_pre_mix, m_w_in, m_conv_w, m_conv_b, m_w_a, m_b_a, m_w_x, m_b_x,
                               m_lru_lambda, m_attn_sinks, m_w_out, m_g_post_mix, m_g_pre_ffn, m_w_ff1, m_w_ff2,
                               m_g_post_ffn]))
    mom2 = dict(zip(_WEIGHTS, [v_meta_tokens, v_g_pre_mix, v_w_in, v_conv_w, v_conv_b, v_w_a, v_b_a, v_w_x, v_b_x,
                               v_lru_lambda, v_attn_sinks, v_w_out, v_g_post_mix, v_g_pre_ffn, v_w_ff1, v_w_ff2,
                               v_g_post_ffn]))
    xi, yi, ci = _mesh_pos()
    chip = 2 * xi + yi

    tiny = jnp.concatenate([meta_tokens, jnp.pad(conv_w[0], ((0, 4), (0, 128)))], axis=0)
    chip_arr = jnp.reshape(chip, (1,)).astype(jnp.int32)
    big2d = lambda a, name: a[0].T if name == 'w_in' else a[0]
    shards, lands = zip(*[_prep_shard(big2d(weights[n], n), chip_arr) for n in _BIG])
    g_in, g_tiny = _gather_weights(shards[:1], lands[:1], tiny, _prep_tiny(tiny, chip_arr))
    w_in_full = g_in.reshape(IN_WIDTH, D_MODEL)
    meta_full = jnp.concatenate([g_tiny[j, :N_META] for j in range(N_CHIPS)], axis=1)
    conv_w_full = jnp.concatenate([g_tiny[j, N_META:N_META + 4, :128] for j in range(N_CHIPS)], axis=1)
    g_send, g_recv, late_thru, late_lands, token = _split_start(
        "gather_late_start", _gather_copies, shards[1:], lands[1:])

    def late_weights(after):
        thru, landed = _split_wait("gather_late_wait", _gather_copies, g_send, g_recv, late_thru, late_lands, after)
        f_send, f_recv, f_thru, f_lands, _ = _split_start("gather_forward_start", _forward_copies, thru, landed)
        _, (g_out,) = _split_wait("gather_out_wait", _forward_copies, f_send, f_recv, f_thru[:1], f_lands[:1], [])

        def ffn_weights(after):
            _, (g_f1, g_f2) = _split_wait("gather_forward_wait", _shifted(_forward_copies, 3), f_send, f_recv,
                                          f_thru[1:], f_lands[1:], after)
            return g_f1, g_f2

        return g_out.reshape(D_MODEL, D_MODEL), ffn_weights

    pos = jnp.stack([ci, chip]).astype(jnp.int32)
    ffn = {}


    def on_ffn_grads(dw1, dw2):
        parts = [dw1, dw2]
        lands = [lax.empty((p.shape[0], p.shape[1] // 2, p.shape[2]), p.dtype) for p in parts]
        ffn['sib'] = _split_start("sibling_ffn_start", _sibling_copies, parts, lands, len(parts))
        return ffn['sib'][4]

    def on_outproj_bwd(dattn):
        send, recv, thru, lands, _ = ffn['sib']
        parts, from_sibling = _split_wait("sibling_ffn_wait", _sibling_copies, send, recv, thru, lands, [dattn])
        cparts_ffn, lands_ffn = zip(*[_chip_presum(p, r, pos) for p, r in zip(parts, from_sibling)])
        ffn['send'], ffn['recv'], ffn['thru'], ffn['lands'], token3 = _split_start(
            "scatter_ffn_start", _scatter_copies, cparts_ffn, lands_ffn)
        return token3

    def on_mixer_grads(dw_in, dw_out):
        parts = [dw_in.reshape(N_CHIPS, IN_WIDTH // N_CHIPS, D_MODEL),
                 dw_out.reshape(N_CHIPS, D_MODEL // N_CHIPS, D_MODEL)]
        cparts, lands = zip(*[_chip_presum(p, r, pos) for p, r in zip(parts, _sibling_exchange(parts, pos))])
        ffn_cparts, ffn_lands = _split_wait("scatter_ffn_wait", _scatter_copies, ffn['send'], ffn['recv'],
                                            ffn['thru'], ffn['lands'], list(cparts))
        n_ici = 3 * len(cparts)
        ffn['n_ici'] = n_ici
        ffn['mixer'] = _split_start("scatter_mixer_start", _two_groups(_scatter_copies, len(cparts), n_ici, _inchip_copies),
                                    list(cparts) + ffn_cparts, list(lands) + ffn_lands, n_ici + 4 * len(ffn_cparts))
        return ffn['mixer'][4]

    head = jnp.concatenate([jnp.zeros((PAD_ROWS, D_MODEL), F32), meta_full], axis=0)
    loss, dx, dhead, grads = _local_step(head, x[0], loss_target[0], g_pre_mix, w_in_full, conv_w_full, conv_b, w_a[0],
                                         b_a, w_x[0], b_x, lru_lambda, attn_sinks, g_post_mix, g_pre_ffn, g_post_ffn,
                                         late_weights, on_ffn_grads, on_outproj_bwd, on_mixer_grads, token)
    grad_x = dx[None]

    pack = _pack_small(dhead[PAD_ROWS:], grads, loss)
    dev = jnp.reshape(4 * xi + 2 * yi + ci, (1,)).astype(jnp.int32)
    send, recv, thru, lands, _ = ffn['mixer']
    nm = len(thru) // 2
    mixer_cparts, mixer_lands = _split_wait("scatter_mixer_wait", _scatter_copies, send, recv, thru[:nm], lands[:nm],
                                            [pack])
    n_small = N_DEV - 1
    t_send, t_recv, t_thru, t_lands, token6 = _split_start(
        "gather_small_start", _two_groups(_all_peers_copies, 1, n_small, _inchip_copies), [pack] + mixer_cparts,
        [_prep_tiny(pack, dev, N_DEV)] + mixer_lands, n_small + 4 * len(mixer_cparts))
    _, ffn_partials = _split_wait("inchip_ffn_wait", _shifted(_inchip_copies, ffn['n_ici']), send, recv, thru[nm:],
                                  lands[nm:], [token6])

    g_out_d, delta, new_m, new_v = {}, {}, {}, {}
    done = {}

    def adamw_big(names, partials):
        for name, part in zip(names, partials):
            shp = weights[name].shape
            res = _adamw_big(part, big2d(weights[name], name), big2d(mom1[name], name), big2d(mom2[name], name))
            done[name] = res[0]
            g_out_d[name], delta[name], new_m[name], new_v[name] = (big2d(r[None], name).reshape(shp) for r in res)

    adamw_big(_BIG[2:], ffn_partials)
    _, mixer_partials = _split_wait("inchip_mixer_wait", _shifted(_inchip_copies, n_small), t_send, t_recv, t_thru[1:],
                                    t_lands[1:], [done[n] for n in _BIG[2:]])
    adamw_big(_BIG[:2], mixer_partials)

    _, (gathered,) = _split_wait("gather_small_wait", _all_peers_copies, t_send, t_recv, t_thru[:1], t_lands[:1],
                                 [done[n] for n in _BIG])
    small = _unpack_small(_sum_devices(gathered.reshape(N_DEV * SMALL_PACK_ROWS, D_MODEL), SMALL_PACK_ROWS), chip)
    loss = small['loss']
    small_names = [n for n in _WEIGHTS if n not in _BIG]
    quads = [(_as2d(weights[n]), _as2d(small[n]), _as2d(mom1[n]), _as2d(mom2[n])) for n in small_names]
    for name, (d, m2, v2) in zip(small_names, _adamw_small(quads)):
        shp = weights[name].shape
        g_out_d[name] = small[name].reshape(shp)
        delta[name], new_m[name], new_v[name] = d.reshape(shp), m2.reshape(shp), v2.reshape(shp)

    return (loss, grad_x, *[g_out_d[n] for n in _WEIGHTS], *[delta[n] for n in _WEIGHTS],
            *[new_m[n] for n in _WEIGHTS], *[new_v[n] for n in _WEIGHTS])
```

```python
import numpy as np
import jax
import jax.numpy as jnp
from jax import lax
from jax.experimental import pallas as pl
from jax.experimental.pallas import tpu as pltpu

F32 = jnp.float32
BF16 = jnp.bfloat16

D_MODEL = 1024
N_META = 16
BLOCK = 128
PAD_ROWS = BLOCK - N_META
HEAD_DIM = 64
ATTN_HEADS = 8
GQA_GROUP = 4
ATTN_WIDTH = 512
KV_WIDTH = 128
QKV_WIDTH = ATTN_WIDTH + 2 * KV_WIDTH
LRU_WIDTH = 512
LRU_BLOCKS = 8
LRU_BLOCK = 64
LRU_C = 8.0
IN_WIDTH = 1792
D_FF = 4096
N_CHIPS = 4
FF_CHUNK = D_FF // N_CHIPS
EPS = 1e-6
NEG = -1e30

ADAM_LR = 0.001
ADAM_B1 = 0.9
ADAM_B2 = 0.999
ADAM_EPS = 1e-08
ADAM_WD = 0.01
ADAM_STEP = 10

VMEM_LIMIT_V7X = 62 * 1024 * 1024
MESH = pl.DeviceIdType.MESH

NT = (((1,), (1,)), ((), ()))
TN = (((0,), (0,)), ((), ()))


def _row_tile(tp):
    return 640 if tp % 640 == 0 else BLOCK


def _elementwise_tile(rows):
    return 512 if rows % 512 == 0 else rows


def _wgrad_row_tile(tp):
    return 1664 if tp % 1664 == 0 else _row_tile(tp)


def _params(*sem):
    return pltpu.CompilerParams(dimension_semantics=sem, vmem_limit_bytes=VMEM_LIMIT_V7X)


def _dot(a, b):
    return jnp.dot(a, b, preferred_element_type=F32)


def _dot_nt(a, b):
    return lax.dot_general(a, b, NT, preferred_element_type=F32)


def _dot_tn(a, b):
    return lax.dot_general(a, b, TN, preferred_element_type=F32)


def _rms(x):
    rs = lax.rsqrt(jnp.mean(x * x, axis=-1, keepdims=True) + EPS)
    return x * rs, rs


def _rms_bwd(xhat, rs, g, dy):
    dyg = dy * g
    dx = rs * (dyg - xhat * jnp.mean(dyg * xhat, axis=-1, keepdims=True))
    dg = jnp.sum(dy * xhat, axis=0, keepdims=True)
    return dx, dg


def _gelu(x):
    k = 0.7978845608028654
    t = jnp.tanh(x * (k + (k * 0.044715) * (x * x)))
    return (0.5 * x) * (1.0 + t), t


def _gelu_grad(x, t):
    k = 0.7978845608028654
    return 0.5 * (1.0 + t) + 0.5 * x * (1.0 - t * t) * k * (1.0 + 3 * 0.044715 * x * x)


def _sigmoid(x):
    return 0.5 * jnp.tanh(0.5 * x) + 0.5


def _one_minus_exp2(y):
    t = jnp.tanh(y)
    return (-2.0 * t) / (1.0 - t)


def _softplus(x):
    return jnp.maximum(x, 0.0) + jnp.log1p(jnp.exp(-jnp.abs(x)))


def _seq_specs(tr, delay=0):
    qb = tr // BLOCK
    tile = lambda i: jnp.maximum(i - delay, 0)
    return [pl.BlockSpec((BLOCK, D_MODEL), lambda i, *_, s=s: (jnp.maximum(tile(i) * qb + s - 1, 0), 0))
            for s in range(qb)]


def _seq_tile(head, pieces, i):
    first = jnp.where(i == 0, head, pieces[0][...])
    return jnp.concatenate([first] + [p[...] for p in pieces[1:]], axis=0)


GROUP_ROWS = GQA_GROUP * BLOCK


def _attn_bias():
    j = np.arange(2 * BLOCK)[:, None]
    i = np.arange(BLOCK)[None, :]
    band = (j - i >= 1) & (j - i <= BLOCK)
    out = []
    for n in range(3):
        ok = band & ((n - 1) * BLOCK + j >= PAD_ROWS) if n < 2 else band
        out.append(np.tile(np.where(ok, 0.0, NEG).astype(np.float32), (1, GQA_GROUP)))
    return jnp.asarray(np.stack(out))


def _heads_t(at, g):
    heads = range(GQA_GROUP * g, GQA_GROUP * (g + 1))
    return jnp.concatenate([at[h * HEAD_DIM:(h + 1) * HEAD_DIM] for h in heads], axis=1).astype(BF16)


def _from_heads_t(groups):
    pairs = []
    for p in groups:
        for h in range(0, GQA_GROUP, 2):
            two = jnp.concatenate([p[:, h * BLOCK:(h + 1) * BLOCK], p[:, (h + 1) * BLOCK:(h + 2) * BLOCK]], axis=0)
            pairs.append(two.T)
    return jnp.concatenate(pairs, axis=1)


def _stack_heads(a, g):
    heads = range(GQA_GROUP * g, GQA_GROUP * (g + 1))
    return jnp.concatenate([a[:, h * HEAD_DIM:(h + 1) * HEAD_DIM] for h in heads], axis=0)


def _unstack_heads(groups):
    return jnp.concatenate([p[h * BLOCK:(h + 1) * BLOCK] for p in groups for h in range(GQA_GROUP)], axis=1)


def _attn_probs_t(k_g, qg, bias, sink_row):
    st = _dot_nt(k_g, qg) + bias
    m = jnp.maximum(jnp.max(st, axis=0, keepdims=True), sink_row)
    p = jnp.exp(st - m)
    es = jnp.exp(sink_row - m)
    inv = 1.0 / (jnp.sum(p, axis=0, keepdims=True) + es)
    return p * inv, es * inv


def _attn_consts(sinks):
    return jnp.repeat(sinks.reshape(ATTN_HEADS), BLOCK).reshape(ATTN_HEADS // GQA_GROUP, GROUP_ROWS), _attn_bias()


_SINK_SPEC = pl.BlockSpec((ATTN_HEADS // GQA_GROUP, GROUP_ROWS), lambda n: (0, 0))
_BIAS_SPEC = pl.BlockSpec((3, 2 * BLOCK, GROUP_ROWS), lambda n: (0, 0, 0))
_QSCALE = HEAD_DIM ** -0.5


def _kv_specs(tr):
    qb = tr // BLOCK
    prev = lambda col: pl.BlockSpec((BLOCK, KV_WIDTH), lambda t: (jnp.maximum(t * qb - 1, 0), col))
    cur = lambda col: pl.BlockSpec((tr, KV_WIDTH), lambda t: (t, col))
    return [prev(4), cur(4), prev(5), cur(5)]


def _block_bias(b_ref, t, qb, i):
    return b_ref[2] if i >= 2 else b_ref[jnp.minimum(t * qb + i, 2)]


N_KV = ATTN_HEADS // GQA_GROUP


def _prob_specs(qb):
    return [pl.BlockSpec((qb, N_KV, 2 * BLOCK, GROUP_ROWS), lambda t: (t, 0, 0, 0)),
            pl.BlockSpec((qb, SUBLANES, GROUP_ROWS), lambda t: (t, 0, 0))]


def _attn_fwd(qkv, sinks):
    tp = qkv.shape[0]
    tr = _row_tile(tp)
    qb, nb = tr // BLOCK, tp // BLOCK
    sink_rows, bias = _attn_consts(sinks)

    def body(s_ref, b_ref, q_ref, kp_ref, kc_ref, vp_ref, vc_ref, o_ref, p_ref, ps_ref):
        t = pl.program_id(0)
        k_all = jnp.concatenate([kp_ref[...], kc_ref[...]], axis=0)
        v_all = jnp.concatenate([vp_ref[...], vc_ref[...]], axis=0)
        for i in range(qb):
            rows = slice(i * BLOCK, (i + 1) * BLOCK)
            q = q_ref[rows]
            k2, v2 = k_all[i * BLOCK:(i + 2) * BLOCK], v_all[i * BLOCK:(i + 2) * BLOCK]
            bias_n = _block_bias(b_ref, t, qb, i)
            outs, sink_probs = [], []
            for g in range(N_KV):
                cols = slice(g * HEAD_DIM, (g + 1) * HEAD_DIM)
                qg = _stack_heads(q, g) * jnp.asarray(_QSCALE, BF16)
                p, ps = _attn_probs_t(k2[:, cols], qg, bias_n, s_ref[g:g + 1])
                pb = p.astype(BF16)
                p_ref[i, g] = pb
                sink_probs.append(ps)
                outs.append(_dot_tn(pb, v2[:, cols]))
            o_ref[rows] = _unstack_heads(outs).astype(BF16)
            ps_ref[i] = jnp.concatenate(sink_probs + [jnp.zeros((SUBLANES - N_KV, GROUP_ROWS), F32)], axis=0)

    return pl.pallas_call(
        body, name="attn_fwd", grid=(tp // tr,),
        in_specs=[_SINK_SPEC, _BIAS_SPEC, pl.BlockSpec((tr, ATTN_WIDTH), lambda t: (t, 0))] + _kv_specs(tr),
        out_specs=[pl.BlockSpec((tr, ATTN_WIDTH), lambda t: (t, 0))] + _prob_specs(qb),
        out_shape=[jax.ShapeDtypeStruct((tp, ATTN_WIDTH), BF16),
                   jax.ShapeDtypeStruct((nb, N_KV, 2 * BLOCK, GROUP_ROWS), BF16),
                   jax.ShapeDtypeStruct((nb, SUBLANES, GROUP_ROWS), F32)],
        compiler_params=_params("parallel"),
    )(sink_rows, bias, qkv, qkv, qkv, qkv, qkv)


def _conv_taps(x, halo):
    ext = jnp.concatenate([halo, x], axis=0)
    return [ext[8:] if k == 3 else pltpu.roll(ext, 3 - k, 0)[8:] for k in range(4)]


def _lru_gates(xc, wa, ba, wx, bx, sp):
    xb = xc.astype(BF16)
    r = _sigmoid(_dot(xb, wa) + ba)
    ig = _sigmoid(_dot(xb, wx) + bx)
    log_a = (-LRU_C * sp) * r
    a = jnp.exp(log_a)
    mult = jnp.sqrt(_one_minus_exp2(log_a))
    return xb, r, ig, a, mult


SUBLANES = 8


def _scan_fwd(a, b, h_in):
    n, width = a.shape
    a, b = (v.reshape(n // SUBLANES, SUBLANES, width) for v in (a, b))
    in_group = lax.broadcasted_iota(jnp.int32, a.shape, 1)
    for d in (1, 2, 4):
        keep = in_group >= d
        b = jnp.where(keep, a * pltpu.roll(b, d, 1) + b, b)
        a = jnp.where(keep, a * pltpu.roll(a, d, 1), a)
    a, b = a.reshape(n, width), b.reshape(n, width)
    out, carry = [], h_in
    for g in range(0, n, SUBLANES):
        h = a[g:g + SUBLANES] * carry + b[g:g + SUBLANES]
        out.append(h)
        carry = h[SUBLANES - 1:]
    return jnp.concatenate(out, axis=0)


def _scan_rev(c, b, g_in):
    n, width = c.shape
    c, b = (v.reshape(n // SUBLANES, SUBLANES, width) for v in (c, b))
    in_group = lax.broadcasted_iota(jnp.int32, c.shape, 1)
    for d in (1, 2, 4):
        keep = in_group < SUBLANES - d
        b = jnp.where(keep, b + c * pltpu.roll(b, SUBLANES - d, 1), b)
        c = jnp.where(keep, c * pltpu.roll(c, SUBLANES - d, 1), c)
    c, b = c.reshape(n, width), b.reshape(n, width)
    out, carry = [], g_in
    for g in range(n - SUBLANES, -1, -SUBLANES):
        r = b[g:g + SUBLANES] + c[g:g + SUBLANES] * carry
        out.append(r)
        carry = r[:1]
    return jnp.concatenate(out[::-1], axis=0)


def _inproj_lru_fwd(head, x, g, w_in, conv_w, conv_b, wa, ba, wx, bx, lam, token):
    tp = BLOCK + x.shape[0]
    tr = _row_tile(tp)
    qb, nt = tr // BLOCK, tp // tr
    small = [conv_w, conv_b, wa, ba, wx, bx, lam]

    def body(*refs):
        head_ref, pieces = refs[0], refs[1:1 + qb]
        g_ref, w_ref, _, cw_ref, cb_ref, wa_ref, ba_ref, wx_ref, bx_ref, lam_ref = refs[1 + qb:11 + qb]
        u_ref, qkv_ref, xr_ref, yr_ref, hr_ref, rec_ref, zbuf, halo, hprev = refs[11 + qb:]
        i = pl.program_id(0)
        cur = i % 2

        @pl.when(i == 0)
        def _():
            halo[...] = jnp.zeros_like(halo)
            hprev[...] = jnp.zeros_like(hprev)
            zbuf[1] = jnp.zeros((tr, 2 * LRU_WIDTH), F32)

        def recurrent_branch(valid):
            cw, cb = cw_ref[...], cb_ref[...]
            wa_m, ba_v, wx_m, bx_v = wa_ref[...], ba_ref[...], wx_ref[...], bx_ref[...]
            sp = _softplus(-lam_ref[...])
            before, h_last = halo[...], hprev[0:1]
            for b in range(qb):
                rows = slice(b * BLOCK, (b + 1) * BLOCK)
                xy = zbuf[1 - cur, rows]
                xin = xy[:, :LRU_WIDTH]
                taps = _conv_taps(xin, before)
                before = xin[BLOCK - 8:]
                xc = cb + sum(cw[k:k + 1] * taps[k] for k in range(4))
                _, _, ig, a, mult = _lru_gates(xc, wa_m, ba_v, wx_m, bx_v, sp)
                u = mult * (ig * xc)
                if b == 0:
                    pos = (i - 1) * tr + lax.broadcasted_iota(jnp.int32, xc.shape, 0)
                    u = jnp.where(pos >= PAD_ROWS, u, 0.0)
                h = _scan_fwd(a, u, h_last)
                h_last = h[BLOCK - 1:]
                hr_ref[rows] = h
                gl, _ = _gelu(xy[:, LRU_WIDTH:])
                rec_ref[rows] = (gl * h).astype(BF16)
            halo[...] = jnp.where(valid, before, 0.0)
            hprev[0:1] = jnp.where(valid, h_last, 0.0)

        def projection():
            xhat, _ = _rms(_seq_tile(head_ref[...], pieces, i))
            u = (xhat * g_ref[...]).astype(BF16)
            u_ref[...] = u
            z = _dot_nt(u, w_ref[...])
            qkv_ref[...] = z[:, :QKV_WIDTH].astype(BF16)
            xr_ref[...] = z[:, QKV_WIDTH:QKV_WIDTH + LRU_WIDTH]
            yr_ref[...] = z[:, QKV_WIDTH + LRU_WIDTH:]
            zbuf[cur] = z[:, QKV_WIDTH:]

        @pl.when(i < nt)
        def _():
            recurrent_branch(i >= 1)
            projection()

        @pl.when(i == nt)
        def _():
            recurrent_branch(True)

    last = nt - 1
    this_row = lambda w: pl.BlockSpec((tr, w), lambda i: (jnp.minimum(i, last), 0))
    prev_row = lambda w: pl.BlockSpec((tr, w), lambda i: (jnp.maximum(i - 1, 0), 0))
    full = lambda a: pl.BlockSpec(a.shape, lambda i: (0,) * a.ndim)
    piece_specs = [pl.BlockSpec((BLOCK, D_MODEL), lambda i, s=s: (jnp.maximum(jnp.minimum(i, last) * qb + s - 1, 0), 0))
                   for s in range(qb)]
    return pl.pallas_call(
        body, name="inproj_lru_fwd", grid=(nt + 1,),
        in_specs=[full(head)] + piece_specs + [full(g), full(w_in), full(token)] + [full(a) for a in small],
        out_specs=[this_row(D_MODEL), this_row(QKV_WIDTH), this_row(LRU_WIDTH), this_row(LRU_WIDTH),
                   prev_row(LRU_WIDTH), prev_row(LRU_WIDTH)],
        out_shape=[jax.ShapeDtypeStruct((tp, D_MODEL), BF16), jax.ShapeDtypeStruct((tp, QKV_WIDTH), BF16),
                   jax.ShapeDtypeStruct((tp, LRU_WIDTH), F32), jax.ShapeDtypeStruct((tp, LRU_WIDTH), F32),
                   jax.ShapeDtypeStruct((tp, LRU_WIDTH), F32), jax.ShapeDtypeStruct((tp, LRU_WIDTH), BF16)],
        scratch_shapes=[pltpu.VMEM((2, tr, 2 * LRU_WIDTH), F32), pltpu.VMEM((8, LRU_WIDTH), F32),
                        pltpu.VMEM((8, LRU_WIDTH), F32)],
        compiler_params=_params("arbitrary"),
    )(head, *([x] * qb), g, w_in, token, *small)


def _outproj_fwd(attn, rec, w_out, head, x, g_post_mix, g_pre_ffn):
    tp = attn.shape[0]
    tr = _row_tile(tp)
    qb = tr // BLOCK

    def body(*refs):
        a_ref, r_ref, w_ref, head_ref = refs[:4]
        pieces = refs[4:4 + qb]
        gm_ref, gf_ref, mix_ref, h1_ref, u1_ref = refs[4 + qb:]
        mix = _dot(a_ref[...], w_ref[:ATTN_WIDTH]) + _dot(r_ref[...], w_ref[ATTN_WIDTH:])
        mix_ref[...] = mix
        mhat, _ = _rms(mix)
        h1 = _seq_tile(head_ref[...], pieces, pl.program_id(0)) + mhat * gm_ref[...]
        h1_ref[...] = h1
        hhat, _ = _rms(h1)
        u1_ref[...] = (hhat * gf_ref[...]).astype(BF16)

    row = lambda w: pl.BlockSpec((tr, w), lambda i: (i, 0))
    full = lambda a: pl.BlockSpec(a.shape, lambda i: (0,) * a.ndim)
    return pl.pallas_call(
        body, name="outproj_fwd", grid=(tp // tr,),
        in_specs=[row(ATTN_WIDTH), row(LRU_WIDTH), full(w_out), full(head)] + _seq_specs(tr)
        + [full(g_post_mix), full(g_pre_ffn)],
        out_specs=[row(D_MODEL), row(D_MODEL), row(D_MODEL)],
        out_shape=[jax.ShapeDtypeStruct((tp, D_MODEL), F32), jax.ShapeDtypeStruct((tp, D_MODEL), F32),
                   jax.ShapeDtypeStruct((tp, D_MODEL), BF16)],
        compiler_params=_params("parallel"),
    )(attn, rec, w_out, head, *([x] * qb), g_post_mix, g_pre_ffn)


FFN_STEPS = N_CHIPS


def _resident(a):
    return pl.BlockSpec(a.shape, lambda *_: (0,) * a.ndim, pipeline_mode=pl.Buffered(1))


def _ffn_fwd(u1, w1, w2, h1, tgt, g_post_ffn):
    tp = h1.shape[0]
    tr = _row_tile(tp)
    qb, nt = tr // BLOCK, tp // tr
    sr = tr // FFN_STEPS

    def body(*refs):
        u_ref, w1_ref, w2_ref, h1_ref = refs[:4]
        t_pieces = refs[4:4 + qb]
        g_ref, r1_ref, dy_ref, df2_ref, loss_ref, dg_ref, acc = refs[4 + qb:]
        i, c = pl.program_id(0), pl.program_id(1)
        cur = i % 2

        @pl.when((i == 0) & (c == 0))
        def _():
            loss_ref[...] = jnp.zeros_like(loss_ref)
            dg_ref[...] = jnp.zeros_like(dg_ref)
            acc[1] = jnp.zeros((tr, D_MODEL), F32)

        def matmuls():
            r = jnp.maximum(_dot(u_ref[...], w1_ref[c]), 0.0)
            r1_ref[...] = r.astype(BF16)
            return _dot((r * r).astype(BF16), w2_ref[c])

        def finish_previous_tile(k, valid):
            lo, hi = k * sr, (k + 1) * sr
            g = g_ref[...]
            fhat, rs = _rms(acc[1 - cur, lo:hi])
            h2 = h1_ref[...] + fhat * g
            rows = (i - 1) * tr + lo + lax.broadcasted_iota(jnp.int32, h2.shape, 0)
            tgt = jnp.concatenate([p[max(lo - s * BLOCK, 0):min(hi - s * BLOCK, BLOCK)] for s, p in enumerate(t_pieces)
                                   if lo < (s + 1) * BLOCK and hi > s * BLOCK], axis=0)
            err = jnp.where((rows >= BLOCK) & valid, h2 - tgt, 0.0)
            dy = err * (1.0 / D_MODEL)
            dy_ref[...] = dy
            loss_ref[...] += (0.5 / D_MODEL) * jnp.sum(err * err)
            df2, dg = _rms_bwd(fhat, rs, g, dy)
            df2_ref[...] = df2.astype(BF16)
            dg_ref[...] += dg

        for k in range(FFN_STEPS):
            @pl.when((c == k) & (i < nt))
            def _(k=k):
                finish_previous_tile(k, i >= 1)
                if k == 0:
                    acc[cur] = matmuls()
                else:
                    acc[cur] += matmuls()

            @pl.when((c == k) & (i == nt))
            def _(k=k):
                finish_previous_tile(k, True)

    last = nt - 1
    this_row = pl.BlockSpec((tr, D_MODEL), lambda i, c: (jnp.minimum(i, last), 0))
    prev_quarter = pl.BlockSpec((sr, D_MODEL), lambda i, c: (jnp.maximum(i - 1, 0) * FFN_STEPS + c, 0))
    prev_quarter_out = pl.BlockSpec(
        (sr, D_MODEL), lambda i, c: (jnp.where(i == 0, nt * FFN_STEPS, (i - 1) * FFN_STEPS + c), 0))
    full = lambda a: pl.BlockSpec(a.shape, lambda i, c: (0,) * a.ndim)
    return pl.pallas_call(
        body, name="ffn_fwd", grid=(nt + 1, FFN_STEPS),
        in_specs=[this_row, _resident(w1), _resident(w2), prev_quarter] + _seq_specs(tr, delay=1) + [full(g_post_ffn)],
        out_specs=[pl.BlockSpec((tr, FF_CHUNK), lambda i, c: (jnp.minimum(i, last), jnp.where(i < nt, c, FFN_STEPS - 1))),
                   prev_quarter_out, prev_quarter_out,
                   pl.BlockSpec((1, 1), lambda i, c: (0, 0)), pl.BlockSpec((1, D_MODEL), lambda i, c: (0, 0))],
        out_shape=[jax.ShapeDtypeStruct((tp, D_FF), BF16), jax.ShapeDtypeStruct((tp + sr, D_MODEL), F32),
                   jax.ShapeDtypeStruct((tp + sr, D_MODEL), BF16), jax.ShapeDtypeStruct((1, 1), F32),
                   jax.ShapeDtypeStruct((1, D_MODEL), F32)],
        scratch_shapes=[pltpu.VMEM((2, tr, D_MODEL), F32)],
        compiler_params=_params("arbitrary", "arbitrary"),
    )(u1, w1, w2, h1, *([tgt] * qb), g_post_ffn)


def _ffn_bwd_data(df2, r1, w1, w2, dy, h1, mix, g_pre_ffn, g_post_mix):
    tp = h1.shape[0]
    tr = _row_tile(tp)
    nt = tp // tr
    sr = tr // FFN_STEPS

    def body(df2_ref, r1_ref, w1_ref, w2_ref, dy_ref, h1_ref, mix_ref, gf_ref, gm_ref,
             da_ref, dh1_ref, dmix_ref, dgf_ref, dgm_ref, acc):
        i, c = pl.program_id(0), pl.program_id(1)
        cur = i % 2

        @pl.when((i == 0) & (c == 0))
        def _():
            dgf_ref[...] = jnp.zeros_like(dgf_ref)
            dgm_ref[...] = jnp.zeros_like(dgm_ref)
            acc[1] = jnp.zeros((tr, D_MODEL), F32)

        def matmuls():
            df = _dot_nt(df2_ref[...], w2_ref[c])
            da = (df * (2.0 * r1_ref[...].astype(F32))).astype(BF16)
            da_ref[...] = da
            return _dot_nt(da, w1_ref[c])

        def finish_previous_tile(k, valid):
            lo, hi = k * sr, (k + 1) * sr
            hhat, rs = _rms(h1_ref[...])
            dx, dgf = _rms_bwd(hhat, rs, gf_ref[...], acc[1 - cur, lo:hi])
            dh1 = dy_ref[...] + dx
            dh1_ref[...] = dh1
            mhat, rsm = _rms(mix_ref[...])
            dmix, dgm = _rms_bwd(mhat, rsm, gm_ref[...], dh1)
            dmix_ref[...] = dmix.astype(BF16)
            dgf_ref[...] += jnp.where(valid, dgf, 0.0)
            dgm_ref[...] += jnp.where(valid, dgm, 0.0)

        for k in range(FFN_STEPS):
            @pl.when((c == k) & (i < nt))
            def _(k=k):
                finish_previous_tile(k, i >= 1)
                if k == 0:
                    acc[cur] = matmuls()
                else:
                    acc[cur] += matmuls()

            @pl.when((c == k) & (i == nt))
            def _(k=k):
                finish_previous_tile(k, True)

    last = nt - 1
    this_row = pl.BlockSpec((tr, D_MODEL), lambda i, c: (jnp.minimum(i, last), 0))
    prev_quarter = pl.BlockSpec((sr, D_MODEL), lambda i, c: (jnp.maximum(i - 1, 0) * FFN_STEPS + c, 0))
    prev_quarter_out = pl.BlockSpec(
        (sr, D_MODEL), lambda i, c: (jnp.where(i == 0, nt * FFN_STEPS, (i - 1) * FFN_STEPS + c), 0))
    chunk = pl.BlockSpec((tr, FF_CHUNK), lambda i, c: (jnp.minimum(i, last), jnp.where(i < nt, c, FFN_STEPS - 1)))
    gain = pl.BlockSpec((1, D_MODEL), lambda i, c: (0, 0))
    return pl.pallas_call(
        body, name="ffn_bwd_data", grid=(nt + 1, FFN_STEPS),
        in_specs=[this_row, chunk, _resident(w1), _resident(w2), prev_quarter, prev_quarter, prev_quarter, gain, gain],
        out_specs=[chunk, prev_quarter_out, prev_quarter_out, gain, gain],
        out_shape=[jax.ShapeDtypeStruct((tp, D_FF), BF16), jax.ShapeDtypeStruct((tp + sr, D_MODEL), F32),
                   jax.ShapeDtypeStruct((tp + sr, D_MODEL), BF16), jax.ShapeDtypeStruct((1, D_MODEL), F32),
                   jax.ShapeDtypeStruct((1, D_MODEL), F32)],
        scratch_shapes=[pltpu.VMEM((2, tr, D_MODEL), F32)],
        compiler_params=_params("arbitrary", "arbitrary"),
    )(df2, r1, w1, w2, dy, h1, mix, g_pre_ffn, g_post_mix)


def _ffn_bwd_weights(u1, da1, r1, df2):
    tp = u1.shape[0]
    tr = _wgrad_row_tile(tp)

    def body(u_ref, da_ref, r1_ref, df2_ref, dw1_ref, dw2_ref):
        i = pl.program_id(1)

        def products():
            r = r1_ref[...].astype(F32)
            return _dot_tn(u_ref[...], da_ref[...]), _dot_tn((r * r).astype(BF16), df2_ref[...])

        @pl.when(i == 0)
        def _():
            dw1_ref[0], dw2_ref[0] = products()

        @pl.when(i > 0)
        def _():
            p1, p2 = products()
            dw1_ref[0] += p1
            dw2_ref[0] += p2

    row = pl.BlockSpec((tr, D_MODEL), lambda c, i: (i, 0))
    chunk = pl.BlockSpec((tr, FF_CHUNK), lambda c, i: (i, c))
    return pl.pallas_call(
        body, name="ffn_bwd_weights", grid=(N_CHIPS, tp // tr),
        in_specs=[row, chunk, chunk, row],
        out_specs=[pl.BlockSpec((1, D_MODEL, FF_CHUNK), lambda c, i: (c, 0, 0)),
                   pl.BlockSpec((1, FF_CHUNK, D_MODEL), lambda c, i: (c, 0, 0))],
        out_shape=[jax.ShapeDtypeStruct((N_CHIPS, D_MODEL, FF_CHUNK), F32),
                   jax.ShapeDtypeStruct((N_CHIPS, FF_CHUNK, D_MODEL), F32)],
        compiler_params=_params("parallel", "arbitrary"),
    )(u1, da1, r1, df2)


N_VEC_ROWS = 8


def _outproj_lru_bwd(dmix, w_out, attn, rec, xr, yr, hr, conv_w, conv_b, wa, ba, wx, bx, lam, token):
    tp = xr.shape[0]
    tr = _row_tile(tp)
    qb, nt = tr // BLOCK, tp // tr

    def body(dm_ref, w_ref, at_ref, rc_ref, xr_ref, xh_ref, yr_ref, hr_ref, hp_ref,
             cw_ref, cb_ref, wa_ref, ba_ref, wx_ref, bx_ref, lam_ref, _,
             dxr_ref, dyr_ref, dat_ref, dwo_ref, dwa_ref, dwx_ref, vec_ref, g_next, a_next, dxc_next, dsp):
        s = pl.program_id(0)
        t = nt - 1 - s

        @pl.when(s == 0)
        def _():
            g_next[...] = jnp.zeros_like(g_next)
            a_next[...] = jnp.zeros_like(a_next)
            dxc_next[...] = jnp.zeros_like(dxc_next)
            dsp[...] = jnp.zeros_like(dsp)
            dwo_ref[...] = jnp.zeros_like(dwo_ref)
            dwa_ref[...] = jnp.zeros_like(dwa_ref)
            dwx_ref[...] = jnp.zeros_like(dwx_ref)
            vec_ref[...] = jnp.zeros_like(vec_ref)

        dm = dm_ref[...]
        dcat = _dot_nt(dm, w_ref[...])
        dat_ref[...] = dcat[:, :ATTN_WIDTH].astype(BF16)
        drec_tile = dcat[:, ATTN_WIDTH:]
        dwo_ref[:ATTN_WIDTH] += _dot_tn(at_ref[...], dm)
        dwo_ref[ATTN_WIDTH:] += _dot_tn(rc_ref[...], dm)

        first_tile = t == 0
        cw, cb = cw_ref[...], cb_ref[...]
        lam_v = lam_ref[...]
        sp = _softplus(-lam_v)
        wa_m, ba_v, wx_m, bx_v = wa_ref[...], ba_ref[...], wx_ref[...], bx_ref[...]
        rows = lax.broadcasted_iota(jnp.int32, (BLOCK, LRU_WIDTH), 0)
        col = lambda v: jnp.sum(v, axis=0, keepdims=True)

        g_after, a_after, dxc_after = g_next[0:1], a_next[0:1], dxc_next[...]
        xbs, dgrs, dgis = [], [], []
        vec = [jnp.zeros((1, LRU_WIDTH), F32) for _ in range(N_VEC_ROWS)]
        for i in reversed(range(qb)):
            blk = slice(i * BLOCK, (i + 1) * BLOCK)
            if i == 0:
                x_before = jnp.where(first_tile, 0.0, xh_ref[...])
                h_before = jnp.where(first_tile, 0.0, hp_ref[7:8])
            else:
                x_before = xr_ref[i * BLOCK - 8:i * BLOCK]
                h_before = hr_ref[i * BLOCK - 1:i * BLOCK]
            taps = _conv_taps(xr_ref[blk], x_before)
            xc = cb + sum(cw[k:k + 1] * taps[k] for k in range(4))
            xb, r, ig, a, mult = _lru_gates(xc, wa_m, ba_v, wx_m, bx_v, sp)

            yr_v = yr_ref[blk]
            gl, th = _gelu(yr_v)
            h = hr_ref[blk]
            drec = drec_tile[blk]
            dyr_ref[blk] = (drec * h * _gelu_grad(yr_v, th)).astype(BF16)

            a_up = jnp.where(rows == BLOCK - 1, a_after, pltpu.roll(a, BLOCK - 1, 0))
            g = _scan_rev(a_up, drec * gl, g_after)
            g_after, a_after = g[0:1], a[0:1]

            h_prev = jnp.where(rows == 0, h_before, pltpu.roll(h, 1, 0))
            du, da = g, g * h_prev
            if i == 0:
                real = (t * tr + rows) >= PAD_ROWS
                du, da = jnp.where(real, du, 0.0), jnp.where(real, da, 0.0)
            dmult = du * (ig * xc)
            dig = du * (mult * xc)
            dxc = du * (mult * ig)
            dlog_a = da * a - dmult * (a * a / mult)
            if i == 0:
                dlog_a = jnp.where(real, dlog_a, 0.0)
            dgr = (dlog_a * (-LRU_C * sp)) * (r * (1.0 - r))
            dgi = dig * (ig * (1.0 - ig))
            dgr_b, dgi_b = dgr.astype(BF16), dgi.astype(BF16)
            dxc = dxc + _dot_nt(dgr_b, wa_m) + _dot_nt(dgi_b, wx_m)
            xbs.append(xb)
            dgrs.append(dgr_b)
            dgis.append(dgi_b)

            ext = jnp.concatenate([dxc, dxc_after], axis=0)
            up = [ext[:BLOCK] if j == 0 else pltpu.roll(ext, BLOCK + 8 - j, 0)[:BLOCK] for j in range(4)]
            dxr_ref[blk] = sum(cw[k:k + 1] * up[3 - k] for k in range(4)).astype(BF16)
            dxc_after = dxc[:8]

            for k in range(4):
                vec[k] = vec[k] + col(dxc * taps[k])
            vec[4] = vec[4] + col(dxc)
            vec[5] = vec[5] + col(dgr)
            vec[6] = vec[6] + col(dgi)
            vec[7] = vec[7] + col(dlog_a * (-LRU_C * r))

        g_next[0:1], a_next[0:1], dxc_next[...] = g_after, a_after, dxc_after
        xb_all = jnp.concatenate(xbs, axis=0)
        dwa_ref[...] += _dot_tn(xb_all, jnp.concatenate(dgrs, axis=0))
        dwx_ref[...] += _dot_tn(xb_all, jnp.concatenate(dgis, axis=0))
        for k in range(7):
            vec_ref[k:k + 1] += vec[k]
        dsp[0:1] += vec[7]

        @pl.when(s == nt - 1)
        def _():
            vec_ref[7:8] = dsp[0:1] * (-_sigmoid(-lam_v))

    blk_spec = pl.BlockSpec((tr, LRU_WIDTH), lambda s: (nt - 1 - s, 0))
    rows_before = pl.BlockSpec((8, LRU_WIDTH), lambda s: (jnp.maximum((nt - 1 - s) * (tr // 8) - 1, 0), 0))
    full = lambda a: pl.BlockSpec(a.shape, lambda s: (0,) * a.ndim)
    small = [conv_w, conv_b, wa, ba, wx, bx, lam, token]
    sq = pl.BlockSpec((LRU_WIDTH, LRU_WIDTH), lambda s: (0, 0))
    wide = pl.BlockSpec((tr, D_MODEL), lambda s: (nt - 1 - s, 0))
    whole = pl.BlockSpec((D_MODEL, D_MODEL), lambda s: (0, 0))
    return pl.pallas_call(
        body, name="outproj_lru_bwd", grid=(nt,),
        in_specs=[wide, whole, blk_spec, blk_spec, blk_spec, rows_before, blk_spec, blk_spec, rows_before]
        + [full(a) for a in small],
        out_specs=[blk_spec, blk_spec, blk_spec, whole, sq, sq, pl.BlockSpec((N_VEC_ROWS, LRU_WIDTH), lambda s: (0, 0))],
        out_shape=[jax.ShapeDtypeStruct((tp, LRU_WIDTH), BF16), jax.ShapeDtypeStruct((tp, LRU_WIDTH), BF16),
                   jax.ShapeDtypeStruct((tp, ATTN_WIDTH), BF16), jax.ShapeDtypeStruct((D_MODEL, D_MODEL), F32),
                   jax.ShapeDtypeStruct((LRU_WIDTH, LRU_WIDTH), F32), jax.ShapeDtypeStruct((LRU_WIDTH, LRU_WIDTH), F32),
                   jax.ShapeDtypeStruct((N_VEC_ROWS, LRU_WIDTH), F32)],
        scratch_shapes=[pltpu.VMEM((8, LRU_WIDTH), F32)] * 4,
        compiler_params=_params("arbitrary"),
    )(dmix, w_out, attn, rec, xr, xr, yr, hr, hr, *small)


def _attn_bwd_tile(tp):
    return _wgrad_row_tile(tp)


def _attn_bwd(qkv, dattn, probs, sink_probs, token):
    tp = qkv.shape[0]
    tr = _attn_bwd_tile(tp)
    qb, nt = tr // BLOCK, tp // tr
    n_groups = N_KV

    def body(p_ref, ps_ref, q_ref, kp_ref, kc_ref, vp_ref, vc_ref, do_ref, _, dq_ref, dkv_ref, ex_ref, ds_ref, dsink):
        t = pl.program_id(0)

        @pl.when(t == 0)
        def _():
            dsink[...] = jnp.zeros_like(dsink)

        k_all = jnp.concatenate([kp_ref[...], kc_ref[...]], axis=0)
        v_all = jnp.concatenate([vp_ref[...], vc_ref[...]], axis=0)
        tail = None
        for i in range(qb):
            rows = slice(i * BLOCK, (i + 1) * BLOCK)
            qt = (q_ref[rows].astype(F32) * _QSCALE).T
            dot = do_ref[rows].astype(F32).T
            k2, v2 = k_all[i * BLOCK:(i + 2) * BLOCK], v_all[i * BLOCK:(i + 2) * BLOCK]
            dqs, dks, dvs = [], [], []
            for g in range(n_groups):
                cols = slice(g * HEAD_DIM, (g + 1) * HEAD_DIM)
                k_g, v_g = k2[:, cols], v2[:, cols]
                qgt, dogt = _heads_t(qt, g), _heads_t(dot, g)
                pb = p_ref[i, g]
                p = pb.astype(F32)
                dpt = _dot(v_g, dogt)
                delta = jnp.sum(p * dpt, axis=0, keepdims=True)
                dst = (p * (dpt - delta)).astype(BF16)
                dqs.append(_dot_tn(k_g, dst) * _QSCALE)
                dks.append(_dot_nt(qgt, dst))
                dvs.append(_dot_nt(dogt, pb))
                dsink[g:g + 1] -= ps_ref[i, g:g + 1] * delta
            dq_ref[rows] = _from_heads_t(dqs).astype(BF16)
            dkv = jnp.concatenate([jnp.concatenate(dks, axis=0).T, jnp.concatenate(dvs, axis=0).T], axis=1)
            if i == 0:
                ex_ref[0] = dkv[:BLOCK]
            else:
                dkv_ref[(i - 1) * BLOCK:i * BLOCK] = (tail + dkv[:BLOCK]).astype(BF16)
            tail = dkv[BLOCK:]
        dkv_ref[(qb - 1) * BLOCK:] = tail.astype(BF16)

        @pl.when(t == nt - 1)
        def _():
            lane = lax.broadcasted_iota(jnp.int32, (1, ATTN_HEADS), 1)
            acc = jnp.zeros((1, ATTN_HEADS), F32)
            for h in range(ATTN_HEADS):
                g, hh = divmod(h, GQA_GROUP)
                acc = acc + jnp.where(lane == h, jnp.sum(dsink[g:g + 1, hh * BLOCK:(hh + 1) * BLOCK]), 0.0)
            ds_ref[...] = acc

    cur = lambda w: pl.BlockSpec((tr, w), lambda t: (t, 0))
    return pl.pallas_call(
        body, name="attn_bwd", grid=(nt,),
        in_specs=_prob_specs(qb) + [cur(ATTN_WIDTH)] + _kv_specs(tr)
        + [cur(ATTN_WIDTH), pl.BlockSpec(token.shape, lambda t: (0, 0))],
        out_specs=[cur(ATTN_WIDTH), cur(2 * KV_WIDTH), pl.BlockSpec((1, BLOCK, 2 * KV_WIDTH), lambda t: (t, 0, 0)),
                   pl.BlockSpec((1, ATTN_HEADS), lambda t: (0, 0))],
        out_shape=[jax.ShapeDtypeStruct((tp, ATTN_WIDTH), BF16), jax.ShapeDtypeStruct((tp, 2 * KV_WIDTH), BF16),
                   jax.ShapeDtypeStruct((nt, BLOCK, 2 * KV_WIDTH), F32), jax.ShapeDtypeStruct((1, ATTN_HEADS), F32)],
        scratch_shapes=[pltpu.VMEM((n_groups, GROUP_ROWS), F32)],
        compiler_params=_params("arbitrary"),
    )(probs, sink_probs, qkv, qkv, qkv, qkv, qkv, dattn, token)


def _fix_dkv(dkv, dkv_extra):
    tp = dkv.shape[0]
    tr = _attn_bwd_tile(tp)
    nt, qb = tp // tr, tr // BLOCK
    if nt == 1:
        return dkv

    def body(d_ref, ex_ref, o_ref):
        o_ref[...] = (d_ref[...].astype(F32) + ex_ref[0]).astype(BF16)

    last = pl.BlockSpec((BLOCK, 2 * KV_WIDTH), lambda t: (t * qb + qb - 1, 0))
    return pl.pallas_call(
        body, name="fix_dkv", grid=(nt - 1,),
        in_specs=[last, pl.BlockSpec((1, BLOCK, 2 * KV_WIDTH), lambda t: (t + 1, 0, 0))],
        out_specs=last, out_shape=jax.ShapeDtypeStruct(dkv.shape, dkv.dtype),
        input_output_aliases={0: 0}, compiler_params=_params("parallel"),
    )(dkv, dkv_extra)


def _inproj_wgrad(dq, dkv, dxr, dyr, u0):
    tp = dq.shape[0]
    tr = _wgrad_row_tile(tp)

    def body(dq_ref, dkv_ref, dxr_ref, dyr_ref, u_ref, dw_ref):
        i = pl.program_id(0)

        def product():
            dz = jnp.concatenate([dq_ref[...], dkv_ref[...], dxr_ref[...], dyr_ref[...]], axis=1)
            return _dot_tn(dz, u_ref[...])

        @pl.when(i == 0)
        def _():
            dw_ref[...] = product()

        @pl.when(i > 0)
        def _():
            dw_ref[...] += product()

    row = lambda w: pl.BlockSpec((tr, w), lambda i: (i, 0))
    return pl.pallas_call(
        body, name="inproj_wgrad", grid=(tp // tr,),
        in_specs=[row(ATTN_WIDTH), row(2 * KV_WIDTH), row(LRU_WIDTH), row(LRU_WIDTH), row(D_MODEL)],
        out_specs=pl.BlockSpec((IN_WIDTH, D_MODEL), lambda i: (0, 0)),
        out_shape=jax.ShapeDtypeStruct((IN_WIDTH, D_MODEL), F32),
        compiler_params=_params("arbitrary"),
    )(dq, dkv, dxr, dyr, u0)


def _inproj_dgrad(dq, dkv, dxr, dyr, w_in, head, x, dh1, g, token):
    tp = dq.shape[0]
    tr = _row_tile(tp)
    nt, qb = tp // tr, tr // BLOCK

    def body(*refs):
        dq_ref, dkv_ref, dxr_ref, dyr_ref, w_ref, head_ref = refs[:6]
        pieces = refs[6:6 + qb]
        dh1_ref, g_ref, _, gx_ref, dhead_ref, dg_ref, buf, sems = refs[6 + qb:]
        i = pl.program_id(0)
        slot = i % 2

        def out_copy(step, at):
            return pltpu.make_async_copy(buf.at[at], gx_ref.at[pl.ds(step * tr - BLOCK, tr)], sems.at[at])

        dz = jnp.concatenate([dq_ref[...], dkv_ref[...], dxr_ref[...], dyr_ref[...]], axis=1)
        du = _dot(dz, w_ref[...])
        hhat, rs = _rms(_seq_tile(head_ref[...], pieces, i))
        dx, dg = _rms_bwd(hhat, rs, g_ref[...], du)
        dh0 = dh1_ref[...] + dx

        @pl.when(i >= 3)
        def _():
            out_copy(i - 2, slot).wait()

        buf[slot] = dh0

        @pl.when(i == 0)
        def _():
            dg_ref[...] = dg
            dhead_ref[...] = dh0[:BLOCK]
            if tr > BLOCK:
                first = pltpu.make_async_copy(buf.at[0, pl.ds(BLOCK, tr - BLOCK)], gx_ref.at[pl.ds(0, tr - BLOCK)],
                                              sems.at[0])
                first.start()
                first.wait()

        @pl.when(i >= 1)
        def _():
            dg_ref[...] += dg
            out_copy(i, slot).start()

        @pl.when(i == nt - 1)
        def _():
            if nt >= 3:
                out_copy(nt - 2, (nt - 2) % 2).wait()
            if nt >= 2:
                out_copy(nt - 1, (nt - 1) % 2).wait()

    row = lambda w: pl.BlockSpec((tr, w), lambda i: (i, 0))
    full = lambda shape: pl.BlockSpec(shape, lambda i: (0,) * len(shape))
    return pl.pallas_call(
        body, name="inproj_dgrad", grid=(tp // tr,),
        in_specs=[row(ATTN_WIDTH), row(2 * KV_WIDTH), row(LRU_WIDTH), row(LRU_WIDTH), full(w_in.shape),
                  full(head.shape)] + _seq_specs(tr) + [row(D_MODEL), full(g.shape), full(token.shape)],
        out_specs=[pl.BlockSpec(memory_space=pl.ANY), full((BLOCK, D_MODEL)), full((1, D_MODEL))],
        out_shape=[jax.ShapeDtypeStruct(x.shape, F32), jax.ShapeDtypeStruct((BLOCK, D_MODEL), F32),
                   jax.ShapeDtypeStruct((1, D_MODEL), F32)],
        scratch_shapes=[pltpu.VMEM((2, tr, D_MODEL), F32), pltpu.SemaphoreType.DMA((2,))],
        compiler_params=_params("arbitrary"),
    )(dq, dkv, dxr, dyr, w_in, head, *([x] * qb), dh1, g, token)


def _dense_block_diag(w):
    eye = jnp.eye(LRU_BLOCKS, dtype=w.dtype)
    return (w[:, :, None, :] * eye[:, None, :, None]).reshape(LRU_WIDTH, LRU_WIDTH)


def _diag_blocks(dense):
    d4 = dense.reshape(LRU_BLOCKS, LRU_BLOCK, LRU_BLOCKS, LRU_BLOCK)
    return jnp.stack([d4[n, :, n, :] for n in range(LRU_BLOCKS)])


def _local_step(head, x, tgt, g_pre_mix, w_in, conv_w, conv_b, w_a, b_a, w_x, b_x, lam, sinks, g_post_mix,
                g_pre_ffn, g_post_ffn, late_weights, on_ffn_grads, on_outproj_bwd, on_mixer_grads, token):
    wa = _dense_block_diag(w_a).astype(BF16)
    wx = _dense_block_diag(w_x).astype(BF16)

    u0, qkv, xr, yr, hr, rec = _inproj_lru_fwd(head, x, g_pre_mix, w_in, conv_w, conv_b, wa, b_a, wx, b_x, lam, token)
    attn, probs, sink_probs = _attn_fwd(qkv, sinks)
    w_out, ffn_weights = late_weights([attn, rec])
    mix, h1, u1 = _outproj_fwd(attn, rec, w_out, head, x, g_post_mix, g_pre_ffn)
    w1, w2 = ffn_weights([u1])
    r1, dy, df2, loss, dg_post_ffn = _ffn_fwd(u1, w1, w2, h1, tgt, g_post_ffn)

    da1, dh1, dmix, dg_pre_ffn, dg_post_mix = _ffn_bwd_data(df2, r1, w1, w2, dy, h1, mix, g_pre_ffn, g_post_mix)
    dw1, dw2 = _ffn_bwd_weights(u1, da1, r1, df2)
    token2 = on_ffn_grads(dw1, dw2)
    dxr, dyr, dattn, dw_out, dwa, dwx, vec = _outproj_lru_bwd(dmix, w_out, attn, rec, xr, yr, hr, conv_w, conv_b,
                                                              wa, b_a, wx, b_x, lam, token2)
    token3 = on_outproj_bwd(dattn)
    dq, dkv, dkv_extra, dsinks = _attn_bwd(qkv, dattn, probs, sink_probs, token3)
    dkv = _fix_dkv(dkv, dkv_extra)
    dw_in = _inproj_wgrad(dq, dkv, dxr, dyr, u0)
    token4 = on_mixer_grads(dw_in, dw_out)
    dx, dhead, dg_pre_mix = _inproj_dgrad(dq, dkv, dxr, dyr, w_in, head, x, dh1, g_pre_mix, token4)

    grads = dict(
        g_pre_mix=dg_pre_mix, conv_w=vec[0:4], conv_b=vec[4:5], w_a=_diag_blocks(dwa), b_a=vec[5:6],
        w_x=_diag_blocks(dwx), b_x=vec[6:7], lru_lambda=vec[7:8], attn_sinks=dsinks,
        g_post_mix=dg_post_mix, g_pre_ffn=dg_pre_ffn, g_post_ffn=dg_post_ffn)
    return loss, dx, dhead, grads


HBM = pl.BlockSpec(memory_space=pltpu.HBM)


def _mesh_pos():
    return lax.axis_index("x"), lax.axis_index("y"), lax.axis_index("c")


def _other_chips(x, y):
    return [(1 - x, y), (x, 1 - y), (1 - x, 1 - y)]


def _remote(src, dst, send_sem, recv_sem, to):
    return pltpu.make_async_remote_copy(src_ref=src, dst_ref=dst, send_sem=send_sem, recv_sem=recv_sem,
                                        device_id=to, device_id_type=MESH)


def _gather_weights(shards, lands, tiny, tiny_land):
    nbig = len(shards)

    def body(*refs):
        srcs, tiny_src = refs[:nbig], refs[nbig]
        outs, tiny_out = refs[2 * nbig + 2:3 * nbig + 2], refs[3 * nbig + 2]
        ici_send, ici_recv, d2d_send, d2d_recv, tiny_send, tiny_recv = refs[3 * nbig + 3:]
        x, y, c = _mesh_pos()
        me = 2 * x + y
        chips = _other_chips(x, y)
        sibling = (x, y, 1 - c)
        sends = []
        for w, (src, out) in enumerate(zip(srcs, outs)):
            hr = src.shape[0] // 2
            for j, chip in enumerate(chips):
                k = 3 * w + j
                cp = _remote(src.at[pl.ds(c * hr, hr)], out.at[me, pl.ds(c * hr, hr)],
                             ici_send.at[k], ici_recv.at[k], (*chip, c))
                cp.start()
                sends.append(cp)
        for j, chip in enumerate(chips):
            cp = _remote(tiny_src, tiny_out.at[me], tiny_send.at[j], tiny_recv.at[j], (*chip, c))
            cp.start()
            sends.append(cp)
        for w, (src, out) in enumerate(zip(srcs, outs)):
            hr = src.shape[0] // 2
            for j, (px, py) in enumerate(chips):
                k = 3 * w + j
                landed = out.at[2 * px + py, pl.ds(c * hr, hr)]
                _remote(landed, landed, ici_send.at[k], ici_recv.at[k], sibling).wait_recv()
                cp = _remote(landed, landed, d2d_send.at[k], d2d_recv.at[k], sibling)
                cp.start()
                sends.append(cp)
        for w, (src, out) in enumerate(zip(srcs, outs)):
            hr = src.shape[0] // 2
            for j, (px, py) in enumerate(chips):
                k = 3 * w + j
                other = out.at[2 * px + py, pl.ds((1 - c) * hr, hr)]
                _remote(other, other, d2d_send.at[k], d2d_recv.at[k], sibling).wait_recv()
        for j, (px, py) in enumerate(chips):
            blk = tiny_out.at[2 * px + py]
            _remote(blk, blk, tiny_send.at[j], tiny_recv.at[j], sibling).wait_recv()
        for cp in sends:
            cp.wait_send()

    out_shape = [jax.ShapeDtypeStruct(l.shape, l.dtype) for l in list(lands) + [tiny_land]]
    n = 3 * nbig
    return pl.pallas_call(
        body, name="gather_weights", out_shape=out_shape,
        in_specs=[HBM] * (2 * nbig + 2), out_specs=[HBM] * (nbig + 1),
        input_output_aliases={nbig + 1 + i: i for i in range(nbig + 1)},
        scratch_shapes=[pltpu.SemaphoreType.DMA((n,)),
                        pltpu.SemaphoreType.DMA((n,)), pltpu.SemaphoreType.DMA((n,)), pltpu.SemaphoreType.DMA((n,)),
                        pltpu.SemaphoreType.DMA((3,)), pltpu.SemaphoreType.DMA((3,))],
    )(*shards, tiny, *lands, tiny_land)


def _prep_shard(w, me):
    rows, cols = w.shape
    tr = _elementwise_tile(rows)

    def body(me_ref, w_ref, s_ref, l_ref):
        b = w_ref[...].astype(BF16)
        s_ref[...] = b
        l_ref[0] = b

    return pl.pallas_call(
        body, name="prep_shard",
        grid_spec=pltpu.PrefetchScalarGridSpec(
            num_scalar_prefetch=1, grid=(rows // tr,),
            in_specs=[pl.BlockSpec((tr, cols), lambda i, me_ref: (i, 0))],
            out_specs=[pl.BlockSpec((tr, cols), lambda i, me_ref: (i, 0)),
                       pl.BlockSpec((1, tr, cols), lambda i, me_ref: (me_ref[0], i, 0))]),
        out_shape=[jax.ShapeDtypeStruct((rows, cols), BF16), jax.ShapeDtypeStruct((N_CHIPS, rows, cols), BF16)],
        compiler_params=_params("parallel"),
    )(me, w)


def _prep_tiny(tiny, me, slots=N_CHIPS):
    def body(me_ref, t_ref, l_ref):
        l_ref[0] = t_ref[...]

    return pl.pallas_call(
        body, name="prep_tiny",
        grid_spec=pltpu.PrefetchScalarGridSpec(
            num_scalar_prefetch=1, grid=(1,),
            in_specs=[pl.BlockSpec(tiny.shape, lambda i, me_ref: (0, 0))],
            out_specs=pl.BlockSpec((1,) + tiny.shape, lambda i, me_ref: (me_ref[0], 0, 0))),
        out_shape=jax.ShapeDtypeStruct((slots,) + tiny.shape, tiny.dtype),
    )(me, tiny)


N_DEV = 8


def _sibling_exchange(parts, token):
    def body(*refs):
        n = len(parts)
        srcs, outs, send_sems, recv_sems = refs[:n], refs[n + 1:2 * n + 1], refs[2 * n + 1], refs[2 * n + 2]
        x, y, c = _mesh_pos()
        sibling = (x, y, 1 - c)
        cps = []
        for w, (src, out) in enumerate(zip(srcs, outs)):
            hr = src.shape[1] // 2
            cp = _remote(src.at[:, pl.ds((1 - c) * hr, hr)], out, send_sems.at[w], recv_sems.at[w], sibling)
            cp.start()
            cps.append(cp)
        for cp in cps:
            cp.wait()

    n = len(parts)
    return pl.pallas_call(
        body, name="sibling_exchange",
        out_shape=[jax.ShapeDtypeStruct((p.shape[0], p.shape[1] // 2, p.shape[2]), p.dtype) for p in parts],
        in_specs=[HBM] * n + [pl.BlockSpec(memory_space=pl.ANY)], out_specs=[HBM] * n,
        scratch_shapes=[pltpu.SemaphoreType.DMA((n,)), pltpu.SemaphoreType.DMA((n,))],
    )(*parts, token)


def _chip_presum(part, from_sibling, pos):
    _, hr, cols = from_sibling.shape
    tr = _elementwise_tile(hr)
    steps = hr // tr

    def body(pos_ref, a_ref, b_ref, o_ref, land_ref):
        s = (a_ref[...] + b_ref[...]).astype(BF16)
        o_ref[...] = s

        @pl.when(pl.program_id(1) == pos_ref[1])
        def _():
            land_ref[...] = s

    return pl.pallas_call(
        body, name="chip_presum",
        grid_spec=pltpu.PrefetchScalarGridSpec(
            num_scalar_prefetch=1, grid=(steps, N_CHIPS),
            in_specs=[pl.BlockSpec((1, tr, cols), lambda i, j, p: (j, p[0] * steps + i, 0)),
                      pl.BlockSpec((1, tr, cols), lambda i, j, p: (j, i, 0))],
            out_specs=[pl.BlockSpec((1, tr, cols), lambda i, j, p: (j, i, 0)),
                       pl.BlockSpec((1, tr, cols), lambda i, j, p: (p[1], p[0] * steps + i, 0))]),
        out_shape=[jax.ShapeDtypeStruct(from_sibling.shape, BF16),
                   jax.ShapeDtypeStruct((N_CHIPS, 2 * hr, cols), BF16)],
        compiler_params=_params("arbitrary", "arbitrary"),
    )(pos, part, from_sibling)


def _scatter_partials(cparts, lands, done_cparts=(), done_lands=()):
    n_new = len(cparts)
    nw = n_new + len(done_cparts)

    def body(*refs):
        srcs = refs[:nw]
        outs = refs[2 * nw:3 * nw]
        own_send, own_recv, ici_send, ici_recv, d2d_send, d2d_recv = refs[3 * nw:]
        x, y, c = _mesh_pos()
        me = 2 * x + y
        chips = _other_chips(x, y)
        sibling = (x, y, 1 - c)
        sends = []
        for w in list(range(n_new, nw)) + list(range(n_new)):
            src, out = srcs[w], outs[w]
            hr = src.shape[1]
            mine = out.at[me, pl.ds(c * hr, hr)]
            cp = _remote(src.at[me], mine, own_send.at[w], own_recv.at[w], sibling)
            cp.start()
            sends.append(cp)
            for j, (px, py) in enumerate(chips):
                if w >= n_new:
                    break
                k = 3 * w + j
                cp = _remote(src.at[2 * px + py], mine, ici_send.at[k], ici_recv.at[k], (px, py, c))
                cp.start()
                sends.append(cp)
        for w in list(range(n_new, nw)) + list(range(n_new)):
            src, out = srcs[w], outs[w]
            hr = src.shape[1]
            for j, (px, py) in enumerate(chips):
                k = 3 * w + j
                landed = out.at[2 * px + py, pl.ds(c * hr, hr)]
                if w < n_new:
                    _remote(landed, landed, ici_send.at[k], ici_recv.at[k], sibling).wait_recv()
                cp = _remote(landed, landed, d2d_send.at[k], d2d_recv.at[k], sibling)
                cp.start()
                sends.append(cp)
        for w, (src, out) in enumerate(zip(srcs, outs)):
            hr = src.shape[1]
            other = out.at[me, pl.ds((1 - c) * hr, hr)]
            _remote(other, other, own_send.at[w], own_recv.at[w], sibling).wait_recv()
            for j, (px, py) in enumerate(chips):
                k = 3 * w + j
                other = out.at[2 * px + py, pl.ds((1 - c) * hr, hr)]
                _remote(other, other, d2d_send.at[k], d2d_recv.at[k], sibling).wait_recv()
        for cp in sends:
            cp.wait_send()

    n = 3 * nw
    dma = pltpu.SemaphoreType.DMA
    every = list(cparts) + list(done_cparts)
    every_lands = list(lands) + list(done_lands)
    return pl.pallas_call(
        body, name="scatter_partials",
        out_shape=[jax.ShapeDtypeStruct(l.shape, l.dtype) for l in every_lands],
        in_specs=[HBM] * (2 * nw), out_specs=[HBM] * nw,
        input_output_aliases={nw + i: i for i in range(nw)},
        scratch_shapes=[dma((nw,)), dma((nw,)), dma((n,)), dma((n,)), dma((n,)), dma((n,))],
    )(*every, *every_lands)


SEM = pl.BlockSpec(memory_space=pltpu.SEMAPHORE)
SPLIT_COPY = pltpu.CompilerParams(has_side_effects=pltpu.SideEffectType.DATAFLOW_SIDE_EFFECTING)


def _hbm(a):
    return pltpu.with_memory_space_constraint(a, pltpu.HBM)


def _gather_copies(srcs, lands, send_sems, recv_sems):
    x, y, c = _mesh_pos()
    me = 2 * x + y
    sends, recvs = [], []
    for w, (src, land) in enumerate(zip(srcs, lands)):
        hr = src.shape[0] // 2
        for j, (px, py) in enumerate(_other_chips(x, y)):
            k = 3 * w + j
            sends.append(_remote(src.at[pl.ds(c * hr, hr)], land.at[me, pl.ds(c * hr, hr)],
                                 send_sems.at[k], recv_sems.at[k], (px, py, c)))
            got = land.at[2 * px + py, pl.ds(c * hr, hr)]
            recvs.append(_remote(got, got, send_sems.at[k], recv_sems.at[k], (px, py, c)))
    return sends, recvs


def _scatter_copies(srcs, lands, send_sems, recv_sems):
    x, y, c = _mesh_pos()
    me = 2 * x + y
    sends, recvs = [], []
    for w, (src, land) in enumerate(zip(srcs, lands)):
        hr = src.shape[1]
        for j, (px, py) in enumerate(_other_chips(x, y)):
            k = 3 * w + j
            sends.append(_remote(src.at[2 * px + py], land.at[me, pl.ds(c * hr, hr)],
                                 send_sems.at[k], recv_sems.at[k], (px, py, c)))
            got = land.at[2 * px + py, pl.ds(c * hr, hr)]
            recvs.append(_remote(got, got, send_sems.at[k], recv_sems.at[k], (px, py, c)))
    return sends, recvs


def _sibling_copies(srcs, lands, send_sems, recv_sems):
    x, y, c = _mesh_pos()
    sibling = (x, y, 1 - c)
    sends, recvs = [], []
    for w, (src, land) in enumerate(zip(srcs, lands)):
        hr = src.shape[1] // 2
        sends.append(_remote(src.at[:, pl.ds((1 - c) * hr, hr)], land, send_sems.at[w], recv_sems.at[w], sibling))
        recvs.append(_remote(land, land, send_sems.at[w], recv_sems.at[w], sibling))
    return sends, recvs


def _inchip_copies(srcs, lands, send_sems, recv_sems):
    x, y, c = _mesh_pos()
    me = 2 * x + y
    sibling = (x, y, 1 - c)
    sends, recvs = [], []
    for w, (src, land) in enumerate(zip(srcs, lands)):
        hr = src.shape[1]
        mine, other = pl.ds(c * hr, hr), pl.ds((1 - c) * hr, hr)
        blocks = [(me, src.at[me])] + [(2 * px + py, None) for px, py in _other_chips(x, y)]
        for j, (blk, own_src) in enumerate(blocks):
            k = 4 * w + j
            landed = land.at[blk, mine]
            sends.append(_remote(landed if own_src is None else own_src, landed, send_sems.at[k], recv_sems.at[k], sibling))
            got = land.at[blk, other]
            recvs.append(_remote(got, got, send_sems.at[k], recv_sems.at[k], sibling))
    return sends, recvs


class _SemsFrom:
    def __init__(self, sems, first):
        self.sems, self.first = sems, first

    @property
    def at(self):
        return self

    def __getitem__(self, k):
        return self.sems.at[self.first + k]


def _shifted(copies_of, first):
    def copies(srcs, lands, send_sems, recv_sems):
        return copies_of(srcs, lands, _SemsFrom(send_sems, first), _SemsFrom(recv_sems, first))
    return copies


def _two_groups(copies_a, n_a, k_a, copies_b):
    shifted_b = _shifted(copies_b, k_a)

    def copies(srcs, lands, send_sems, recv_sems):
        sends_a, recvs_a = copies_a(srcs[:n_a], lands[:n_a], send_sems, recv_sems)
        sends_b, recvs_b = shifted_b(srcs[n_a:], lands[n_a:], send_sems, recv_sems)
        return sends_a + sends_b, recvs_a + recvs_b
    return copies


def _all_peers_copies(srcs, lands, send_sems, recv_sems):
    x, y, c = _mesh_pos()
    (src,), (land,) = srcs, lands
    flip = lambda v, bit: 1 - v if bit else v
    sends, recvs = [], []
    for k in range(N_DEV - 1):
        px, py, pc = flip(x, (k + 1) & 4), flip(y, (k + 1) & 2), flip(c, (k + 1) & 1)
        sends.append(_remote(src, land.at[4 * x + 2 * y + c], send_sems.at[k], recv_sems.at[k], (px, py, pc)))
        got = land.at[4 * px + 2 * py + pc]
        recvs.append(_remote(got, got, send_sems.at[k], recv_sems.at[k], (px, py, pc)))
    return sends, recvs


def _split_start(name, copies_of, srcs, land_shapes, n_copies=None):
    n = len(srcs)
    k = 3 * n if n_copies is None else n_copies

    def body(*refs):
        src_refs, land_refs = refs[:n], refs[n:2 * n]
        send_sems, recv_sems = refs[2 * n], refs[2 * n + 1]
        token = refs[-1]
        sends, _ = copies_of(src_refs, land_refs, send_sems, recv_sems)
        for cp in sends:
            cp.start()
        token[...] = jnp.zeros_like(token)

    lands = [_hbm(s) for s in land_shapes]
    dma = pltpu.SemaphoreType.DMA
    res = pl.pallas_call(
        body, name=name,
        out_shape=(dma((k,)), dma((k,)), *[pltpu.HBM(s.shape, s.dtype) for s in srcs],
                   *[pltpu.HBM(s.shape, s.dtype) for s in land_shapes], jax.ShapeDtypeStruct((8, 128), F32)),
        in_specs=[HBM] * (2 * n),
        out_specs=(SEM, SEM, *([HBM] * (2 * n)), pl.BlockSpec(memory_space=pltpu.VMEM)),
        input_output_aliases={i: 2 + i for i in range(2 * n)},
        compiler_params=SPLIT_COPY,
    )(*[_hbm(s) for s in srcs], *lands)
    return res[0], res[1], list(res[2:2 + n]), list(res[2 + n:2 + 2 * n]), res[-1]


def _split_wait(name, copies_of, send_sems, recv_sems, srcs, lands, after):
    n = len(srcs)

    def body(*refs):
        src_refs, land_refs = refs[:n], refs[n:2 * n]
        sends, recvs = copies_of(src_refs, land_refs, refs[2 * n], refs[2 * n + 1])
        for cp in sends:
            cp.wait_send()
        for cp in recvs:
            cp.wait_recv()

    res = pl.pallas_call(
        body, name=name,
        out_shape=tuple(pltpu.HBM(s.shape, s.dtype) for s in list(srcs) + list(lands)),
        in_specs=[HBM] * (2 * n) + [SEM, SEM] + [pl.BlockSpec(memory_space=pl.ANY)] * len(after),
        out_specs=tuple([HBM] * (2 * n)),
        input_output_aliases={i: i for i in range(2 * n)},
        compiler_params=SPLIT_COPY,
    )(*srcs, *lands, send_sems, recv_sems, *after)
    return list(res[:n]), list(res[n:])


def _forward_copies(srcs, lands, send_sems, recv_sems):
    x, y, c = _mesh_pos()
    sibling = (x, y, 1 - c)
    sends, recvs = [], []
    for w, land in enumerate(lands):
        hr = land.shape[1] // 2
        for j, (px, py) in enumerate(_other_chips(x, y)):
            k = 3 * w + j
            landed = land.at[2 * px + py, pl.ds(c * hr, hr)]
            sends.append(_remote(landed, landed, send_sems.at[k], recv_sems.at[k], sibling))
            other = land.at[2 * px + py, pl.ds((1 - c) * hr, hr)]
            recvs.append(_remote(other, other, send_sems.at[k], recv_sems.at[k], sibling))
    return sends, recvs


def _gather_finish(lands, n_forward):
    n = len(lands)

    def body(*refs):
        outs = refs[n:n + n_forward]
        d2d_send, d2d_recv = refs[2 * n:]
        sends, recvs = _forward_copies(None, outs, d2d_send, d2d_recv)
        for cp in sends:
            cp.start()
        for cp in recvs:
            cp.wait_recv()
        for cp in sends:
            cp.wait_send()

    dma = pltpu.SemaphoreType.DMA
    return pl.pallas_call(
        body, name="gather_finish",
        out_shape=[jax.ShapeDtypeStruct(l.shape, l.dtype) for l in lands],
        in_specs=[HBM] * n, out_specs=[HBM] * n,
        input_output_aliases={i: i for i in range(n)},
        scratch_shapes=[dma((3 * n,)), dma((3 * n,))],
    )(*lands)


def _adamw(w, g, m, v):
    m = ADAM_B1 * m + (1.0 - ADAM_B1) * g
    v = ADAM_B2 * v + (1.0 - ADAM_B2) * (g * g)
    m_hat = m / (1.0 - ADAM_B1 ** ADAM_STEP)
    v_hat = v / (1.0 - ADAM_B2 ** ADAM_STEP)
    delta = -ADAM_LR * (m_hat / (jnp.sqrt(v_hat) + ADAM_EPS) + ADAM_WD * w)
    return delta, m, v


def _adamw_big(partials, w, m, v):
    rows, cols = w.shape
    tr = _elementwise_tile(rows)

    def body(p_ref, w_ref, m_ref, v_ref, g_ref, d_ref, m2_ref, v2_ref):
        g = ((p_ref[0].astype(F32) + p_ref[1].astype(F32)) + p_ref[2].astype(F32)) + p_ref[3].astype(F32)
        g_ref[...] = g
        d_ref[...], m2_ref[...], v2_ref[...] = _adamw(w_ref[...], g, m_ref[...], v_ref[...])

    blk = pl.BlockSpec((tr, cols), lambda i: (i, 0))
    return pl.pallas_call(
        body, name="adamw_big", grid=(rows // tr,),
        in_specs=[pl.BlockSpec((N_CHIPS, tr, cols), lambda i: (0, i, 0)), blk, blk, blk],
        out_specs=[blk] * 4, out_shape=[jax.ShapeDtypeStruct((rows, cols), F32)] * 4,
        compiler_params=_params("parallel"),
    )(partials, w, m, v)


def _sum_devices(gathered, rows):
    cols = gathered.shape[1]

    def body(g_ref, o_ref):
        acc = g_ref[0:rows]
        for d in range(1, N_DEV):
            acc = acc + g_ref[d * rows:(d + 1) * rows]
        o_ref[...] = acc

    return pl.pallas_call(
        body, name="sum_devices", out_shape=jax.ShapeDtypeStruct((rows, cols), F32),
        in_specs=[pl.BlockSpec(memory_space=pltpu.VMEM)], out_specs=pl.BlockSpec(memory_space=pltpu.VMEM),
        compiler_params=pltpu.CompilerParams(vmem_limit_bytes=VMEM_LIMIT_V7X),
    )(gathered)


def _adamw_small(quads):
    n = len(quads)

    def body(*refs):
        ins, outs = refs[:4 * n], refs[4 * n:]
        for t in range(n):
            w, g, m, v = (r[...] for r in ins[4 * t:4 * t + 4])
            outs[3 * t][...], outs[3 * t + 1][...], outs[3 * t + 2][...] = _adamw(w, g, m, v)

    flat = [a for q in quads for a in q]
    vm = pl.BlockSpec(memory_space=pltpu.VMEM)
    res = pl.pallas_call(
        body, name="adamw_small",
        out_shape=[jax.ShapeDtypeStruct(q[0].shape, F32) for q in quads for _ in range(3)],
        in_specs=[vm] * (4 * n), out_specs=[vm] * (3 * n),
    )(*flat)
    return [tuple(res[3 * t:3 * t + 3]) for t in range(n)]


SMALL_PACK_ROWS = 96
META_COLS = D_MODEL // N_CHIPS
CONV_COLS = LRU_WIDTH // N_CHIPS
_WEIGHTS = ['meta_tokens', 'g_pre_mix', 'w_in', 'conv_w', 'conv_b', 'w_a', 'b_a', 'w_x', 'b_x', 'lru_lambda',
            'attn_sinks', 'w_out', 'g_post_mix', 'g_pre_ffn', 'w_ff1', 'w_ff2', 'g_post_ffn']
_BIG = ['w_in', 'w_out', 'w_ff1', 'w_ff2']


def _pack_small(dmeta, g, loss):
    def body(dm, g0, g1, g2, g3, cw, cb, ba, bx, lam, sinks, loss_ref, wa, wx, out):
        out[...] = jnp.zeros_like(out)
        out[0:N_META] = dm[...]
        for r, ref in enumerate([g0, g1, g2, g3]):
            out[16 + r:17 + r] = ref[...]
        out[20:24, 0:LRU_WIDTH] = cw[...]
        out[24:25, 0:LRU_WIDTH] = cb[...]
        out[24:25, LRU_WIDTH:] = ba[...]
        out[25:26, 0:LRU_WIDTH] = bx[...]
        out[25:26, LRU_WIDTH:] = lam[...]
        out[26:27, 0:ATTN_HEADS] = sinks[...]
        out[27:28, 0:1] = loss_ref[...]
        out[32:64] = wa[...]
        out[64:96] = wx[...]

    return pl.pallas_call(
        body, name="pack_small", out_shape=jax.ShapeDtypeStruct((SMALL_PACK_ROWS, D_MODEL), F32),
    )(dmeta, g['g_pre_mix'], g['g_post_mix'], g['g_pre_ffn'], g['g_post_ffn'], g['conv_w'], g['conv_b'], g['b_a'],
      g['b_x'], g['lru_lambda'], g['attn_sinks'], loss, g['w_a'].reshape(32, D_MODEL), g['w_x'].reshape(32, D_MODEL))


def _unpack_small(s, chip):
    return dict(
        meta_tokens=lax.dynamic_slice(s[0:N_META], (0, chip * META_COLS), (N_META, META_COLS)),
        g_pre_mix=s[16:17], g_post_mix=s[17:18], g_pre_ffn=s[18:19], g_post_ffn=s[19:20],
        conv_w=lax.dynamic_slice(s[20:24], (0, chip * CONV_COLS), (4, CONV_COLS)).reshape(1, 4, CONV_COLS),
        conv_b=s[24:25, :512], b_a=s[24:25, 512:], b_x=s[25:26, :512], lru_lambda=s[25:26, 512:],
        attn_sinks=s[26:27, :ATTN_HEADS], loss=s[27, 0],
        w_a=s[32:64].reshape(1, LRU_BLOCKS, LRU_BLOCK, LRU_BLOCK),
        w_x=s[64:96].reshape(1, LRU_BLOCKS, LRU_BLOCK, LRU_BLOCK))


def _as2d(a):
    if a.ndim == 2:
        return a
    return a.reshape(-1, a.shape[-1])


def kernel(x, meta_tokens, g_pre_mix, w_in, conv_w, conv_b, w_a, b_a, w_x, b_x, lru_lambda, attn_sinks, w_out, g_post_mix, g_pre_ffn, w_ff1, w_ff2, g_post_ffn, loss_target, m_meta_tokens, m_g_pre_mix, m_w_in, m_conv_w, m_conv_b, m_w_a, m_b_a, m_w_x, m_b_x, m_lru_lambda, m_attn_sinks, m_w_out, m_g_post_mix, m_g_pre_ffn, m_w_ff1, m_w_ff2, m_g_post_ffn, v_meta_tokens, v_g_pre_mix, v_w_in, v_conv_w, v_conv_b, v_w_a, v_b_a, v_w_x, v_b_x, v_lru_lambda, v_attn_sinks, v_w_out, v_g_post_mix, v_g_pre_ffn, v_w_ff1, v_w_ff2, v_g_post_ffn):
    weights = dict(meta_tokens=meta_tokens, g_pre_mix=g_pre_mix, w_in=w_in, conv_w=conv_w, conv_b=conv_b, w_a=w_a,
                   b_a=b_a, w_x=w_x, b_x=b_x, lru_lambda=lru_lambda, attn_sinks=attn_sinks, w_out=w_out,
                   g_post_mix=g_post_mix, g_pre_ffn=g_pre_ffn, w_ff1=w_ff1, w_ff2=w_ff2, g_post_ffn=g_post_ffn)
    mom1 = dict(zip(_WEIGHTS, [m_meta_tokens, m_g_pre_mix, m_w_in, m_conv_w, m_conv_b, m_w_a, m_b_a, m_w_x, m_b_x,
                               m_lru_lambda, m_attn_sinks, m_w_out, m_g_post_mix, m_g_pre_ffn, m_w_ff1, m_w_ff2,
                               m_g_post_ffn]))
    mom2 = dict(zip(_WEIGHTS, [v_meta_tokens, v_g_pre_mix, v_w_in, v_conv_w, v_conv_b, v_w_a, v_b_a, v_w_x, v_b_x,
                               v_lru_lambda, v_attn_sinks, v_w_out, v_g_post_mix, v_g_pre_ffn, v_w_ff1, v_w_ff2,
                               v_g_post_ffn]))
    xi, yi, ci = _mesh_pos()
    chip = 2 * xi + yi

    tiny = jnp.concatenate([meta_tokens, jnp.pad(conv_w[0], ((0, 4), (0, 128)))], axis=0)
    chip_arr = jnp.reshape(chip, (1,)).astype(jnp.int32)
    big2d = lambda a, name: a[0].T if name == 'w_in' else a[0]
    shards, lands = zip(*[_prep_shard(big2d(weights[n], n), chip_arr) for n in _BIG])
    g_in, g_tiny = _gather_weights(shards[:1], lands[:1], tiny, _prep_tiny(tiny, chip_arr))
    w_in_full = g_in.reshape(IN_WIDTH, D_MODEL)
    meta_full = jnp.concatenate([g_tiny[j, :N_META] for j in range(N_CHIPS)], axis=1)
    conv_w_full = jnp.concatenate([g_tiny[j, N_META:N_META + 4, :128] for j in range(N_CHIPS)], axis=1)
    g_send, g_recv, late_thru, late_lands, token = _split_start(
        "gather_late_start", _gather_copies, shards[1:], lands[1:])

    def late_weights(after):
        thru, landed = _split_wait("gather_late_wait", _gather_copies, g_send, g_recv, late_thru, late_lands, after)
        f_send, f_recv, f_thru, f_lands, _ = _split_start("gather_forward_start", _forward_copies, thru, landed)
        _, (g_out,) = _split_wait("gather_out_wait", _forward_copies, f_send, f_recv, f_thru[:1], f_lands[:1], [])

        def ffn_weights(after):
            _, (g_f1, g_f2) = _split_wait("gather_forward_wait", _shifted(_forward_copies, 3), f_send, f_recv,
                                          f_thru[1:], f_lands[1:], after)
            return g_f1, g_f2

        return g_out.reshape(D_MODEL, D_MODEL), ffn_weights

    pos = jnp.stack([ci, chip]).astype(jnp.int32)
    ffn = {}


    def on_ffn_grads(dw1, dw2):
        parts = [dw1, dw2]
        lands = [lax.empty((p.shape[0], p.shape[1] // 2, p.shape[2]), p.dtype) for p in parts]
        ffn['sib'] = _split_start("sibling_ffn_start", _sibling_copies, parts, lands, len(parts))
        return ffn['sib'][4]

    def on_outproj_bwd(dattn):
        send, recv, thru, lands, _ = ffn['sib']
        parts, from_sibling = _split_wait("sibling_ffn_wait", _sibling_copies, send, recv, thru, lands, [dattn])
        cparts_ffn, lands_ffn = zip(*[_chip_presum(p, r, pos) for p, r in zip(parts, from_sibling)])
        ffn['send'], ffn['recv'], ffn['thru'], ffn['lands'], token3 = _split_start(
            "scatter_ffn_start", _scatter_copies, cparts_ffn, lands_ffn)
        return token3

    def on_mixer_grads(dw_in, dw_out):
        parts = [dw_in.reshape(N_CHIPS, IN_WIDTH // N_CHIPS, D_MODEL),
                 dw_out.reshape(N_CHIPS, D_MODEL // N_CHIPS, D_MODEL)]
        cparts, lands = zip(*[_chip_presum(p, r, pos) for p, r in zip(parts, _sibling_exchange(parts, pos))])
        ffn_cparts, ffn_lands = _split_wait("scatter_ffn_wait", _scatter_copies, ffn['send'], ffn['recv'],
                                            ffn['thru'], ffn['lands'], list(cparts))
        n_ici = 3 * len(cparts)
        ffn['n_ici'] = n_ici
        ffn['mixer'] = _split_start("scatter_mixer_start", _two_groups(_scatter_copies, len(cparts), n_ici, _inchip_copies),
                                    list(cparts) + ffn_cparts, list(lands) + ffn_lands, n_ici + 4 * len(ffn_cparts))
        return ffn['mixer'][4]

    head = jnp.concatenate([jnp.zeros((PAD_ROWS, D_MODEL), F32), meta_full], axis=0)
    loss, dx, dhead, grads = _local_step(head, x[0], loss_target[0], g_pre_mix, w_in_full, conv_w_full, conv_b, w_a[0],
                                         b_a, w_x[0], b_x, lru_lambda, attn_sinks, g_post_mix, g_pre_ffn, g_post_ffn,
                                         late_weights, on_ffn_grads, on_outproj_bwd, on_mixer_grads, token)
    grad_x = dx[None]

    pack = _pack_small(dhead[PAD_ROWS:], grads, loss)
    dev = jnp.reshape(4 * xi + 2 * yi + ci, (1,)).astype(jnp.int32)
    send, recv, thru, lands, _ = ffn['mixer']
    nm = len(thru) // 2
    mixer_cparts, mixer_lands = _split_wait("scatter_mixer_wait", _scatter_copies, send, recv, thru[:nm], lands[:nm],
                                            [pack])
    n_small = N_DEV - 1
    t_send, t_recv, t_thru, t_lands, token6 = _split_start(
        "gather_small_start", _two_groups(_all_peers_copies, 1, n_small, _inchip_copies), [pack] + mixer_cparts,
        [_prep_tiny(pack, dev, N_DEV)] + mixer_lands, n_small + 4 * len(mixer_cparts))
    _, ffn_partials = _split_wait("inchip_ffn_wait", _shifted(_inchip_copies, ffn['n_ici']), send, recv, thru[nm:],
                                  lands[nm:], [token6])

    g_out_d, delta, new_m, new_v = {}, {}, {}, {}
    done = {}

    def adamw_big(names, partials):
        for name, part in zip(names, partials):
            shp = weights[name].shape
            res = _adamw_big(part, big2d(weights[name], name), big2d(mom1[name], name), big2d(mom2[name], name))
            done[name] = res[0]
            g_out_d[name], delta[name], new_m[name], new_v[name] = (big2d(r[None], name).reshape(shp) for r in res)

    adamw_big(_BIG[2:], ffn_partials)
    _, mixer_partials = _split_wait("inchip_mixer_wait", _shifted(_inchip_copies, n_small), t_send, t_recv, t_thru[1:],
                                    t_lands[1:], [done[n] for n in _BIG[2:]])
    adamw_big(_BIG[:2], mixer_partials)

    _, (gathered,) = _split_wait("gather_small_wait", _all_peers_copies, t_send, t_recv, t_thru[:1], t_lands[:1],
                                 [done[n] for n in _BIG])
    small = _unpack_small(_sum_devices(gathered.reshape(N_DEV * SMALL_PACK_ROWS, D_MODEL), SMALL_PACK_ROWS), chip)
    loss = small['loss']
    small_names = [n for n in _WEIGHTS if n not in _BIG]
    quads = [(_as2d(weights[n]), _as2d(small[n]), _as2d(mom1[n]), _as2d(mom2[n])) for n in small_names]
    for name, (d, m2, v2) in zip(small_names, _adamw_small(quads)):
        shp = weights[name].shape
        g_out_d[name] = small[name].reshape(shp)
        delta[name], new_m[name], new_v[name] = d.reshape(shp), m2.reshape(shp), v2.reshape(shp)

    return (loss, grad_x, *[g_out_d[n] for n in _WEIGHTS], *[delta[n] for n in _WEIGHTS],
            *[new_m[n] for n in _WEIGHTS], *[new_v[n] for n in _WEIGHTS])
```
